```python
import jax, jax.numpy as jnp
from jax import lax
import numpy as np

D_MODEL = 2048
BATCH = 8
SEQ = 8192
DEPTH = 1

D_MIX = D_MODEL
D_HGRN = D_MIX // 2
HGRN_HEAD = 128
HGRN_HEADS = D_HGRN // HGRN_HEAD
HGRN_CHUNK = 64
D_ATTN = D_MIX - D_HGRN
ATTN_HEAD = 64
ATTN_HEADS = D_ATTN // ATTN_HEAD
DILATED_PATTERNS = ((128, 1), (512, 4), (2048, 16))
NORM_EPS = 1e-6
SPLIT_WIDTHS = (D_HGRN, D_HGRN, D_HGRN, D_HGRN, D_ATTN, D_ATTN, D_ATTN, D_ATTN)
D_IN = sum(SPLIT_WIDTHS)

kernel_name = "hymba_hgrn2_dilated_alibi_block"


def rms_norm(x, gain):
    xf = x.astype(jnp.float32)
    y = xf * lax.rsqrt(jnp.mean(xf * xf, axis=-1, keepdims=True) + NORM_EPS)
    return (y * gain.astype(jnp.float32)).astype(x.dtype)


def alibi_slopes(n_heads):
    return jnp.exp2(-8.0 * jnp.arange(1, n_heads + 1, dtype=jnp.float32) / n_heads)


def chunked_gated_recurrence(q, k, log_f, v):
    B, H, S, Dk = q.shape
    Dv = v.shape[-1]
    C = HGRN_CHUNK
    N = S // C
    q, k, log_f = (a.reshape(B, H, N, C, Dk) for a in (q, k, log_f))
    v = v.reshape(B, H, N, C, Dv)
    b = jnp.cumsum(log_f, axis=3)
    b_last = b[:, :, :, -1:, :]
    q_dec = q * jnp.exp(b)
    k_dec = k * jnp.exp(-b)
    k_end = k * jnp.exp(b_last - b)
    causal = jnp.tril(jnp.ones((C, C), dtype=bool))
    a_intra = jnp.where(causal, jnp.einsum('bhncd,bhnsd->bhncs', q_dec, k_dec), 0.0)
    o_intra = jnp.einsum('bhncs,bhnsv->bhncv', a_intra, v)
    kv_chunk = jnp.einsum('bhncd,bhncv->nbhdv', k_end, v)
    chunk_decay = jnp.exp(b_last[:, :, :, 0, :]).transpose(2, 0, 1, 3)

    def step(state, inp):
        dec, kv_n = inp
        return dec[..., None] * state + kv_n, state

    _, prev_states = lax.scan(step, jnp.zeros((B, H, Dk, Dv), jnp.float32),
                              (chunk_decay, kv_chunk))
    o_inter = jnp.einsum('bhncd,nbhdv->bhncv', q_dec, prev_states)
    return (o_intra + o_inter).reshape(B, H, S, Dv)


def hgrn2_mixer(q_pre, f_pre, i_pre, lb, g_norm):
    B, S, _ = q_pre.shape

    def heads(a):
        return a.astype(jnp.float32).reshape(B, S, HGRN_HEADS, HGRN_HEAD).transpose(0, 2, 1, 3)

    q = jax.nn.silu(heads(q_pre))
    lbh = lb.astype(jnp.float32).reshape(HGRN_HEADS, 1, HGRN_HEAD)
    f = lbh + (1.0 - lbh) * jax.nn.sigmoid(heads(f_pre))
    k = 1.0 - f
    o = chunked_gated_recurrence(q, k, jnp.log(f), heads(i_pre))
    o = o * lax.rsqrt(jnp.mean(o * o, axis=-1, keepdims=True) + NORM_EPS) * g_norm.astype(jnp.float32)
    return o.transpose(0, 2, 1, 3).reshape(B, S, D_HGRN)


def dilated_pattern_attention(q, k, v, slopes, window, dilation):
    B, H, S, Dh = q.shape
    d = dilation
    W = window // d
    L = S // d
    nb = -(-L // W)
    Lp = nb * W

    def regroup(a):
        a = a.reshape(B, H, L, d, Dh).transpose(0, 1, 3, 2, 4)
        a = jnp.pad(a, ((0, 0), (0, 0), (0, 0), (0, Lp - L), (0, 0)))
        return a.reshape(B, H, d, nb, W, Dh)

    def with_prev_block(a):
        prev = jnp.pad(a, ((0, 0), (0, 0), (0, 0), (1, 0), (0, 0), (0, 0)))[:, :, :, :-1]
        return jnp.concatenate([prev, a], axis=4)

    qb = regroup(q)
    kc = with_prev_block(regroup(k))
    vc = with_prev_block(regroup(v))
    s = jnp.einsum('bhrnqd,bhrnkd->bhrnqk', qb, kc) * (Dh ** -0.5)
    i_idx = jnp.arange(W)[:, None]
    j_idx = jnp.arange(2 * W)[None, :]
    delta = W + i_idx - j_idx
    blk = jnp.arange(nb)[:, None, None]
    valid = (delta >= 0) & (delta <= W) & ((blk - 1) * W + j_idx >= 0)
    bias = -slopes[:, None, None] * (d * delta).astype(jnp.float32)
    s = jnp.where(valid, s + bias[None, :, None, None], -jnp.inf)
    m = jnp.max(s, axis=-1, keepdims=True)
    p = jnp.exp(s - m)
    den = jnp.sum(p, axis=-1, keepdims=True)
    o = jnp.einsum('bhrnqk,bhrnkd->bhrnqd', p, vc) / den
    lse = (m + jnp.log(den))[..., 0]
    o = o.reshape(B, H, d, Lp, Dh)[:, :, :, :L].transpose(0, 1, 3, 2, 4).reshape(B, H, S, Dh)
    lse = lse.reshape(B, H, d, Lp)[:, :, :, :L].transpose(0, 1, 3, 2).reshape(B, H, S)
    return o, lse


def dilated_attention_mixer(q_pre, k_pre, v_pre):
    B, S, _ = q_pre.shape

    def heads(a):
        return a.astype(jnp.float32).reshape(B, S, ATTN_HEADS, ATTN_HEAD).transpose(0, 2, 1, 3)

    q, k, v = heads(q_pre), heads(k_pre), heads(v_pre)
    slopes = alibi_slopes(ATTN_HEADS)
    outs, lses = [], []
    for window, dilation in DILATED_PATTERNS:
        o, lse = dilated_pattern_attention(q, k, v, slopes, window, dilation)
        outs.append(o)
        lses.append(lse)
    weights = jax.nn.softmax(jnp.stack(lses, axis=0), axis=0)
    o = jnp.einsum('pbhs,pbhsd->bhsd', weights, jnp.stack(outs, axis=0))
    return o.transpose(0, 2, 1, 3).reshape(B, S, D_ATTN)


def _fwd_setup_inputs(seed: int = 0) -> dict:
    key = jax.random.key(seed)
    ks = jax.random.split(key, 8)
    x = jax.random.normal(ks[0], (BATCH, SEQ, D_MODEL), jnp.float32)
    norm_gain = 1.0 + 0.01 * jax.random.normal(ks[1], (DEPTH, D_MODEL), jnp.float32)
    w_in = jax.random.normal(ks[2], (DEPTH, D_MODEL, D_IN), jnp.float32) * D_MODEL ** -0.5
    lb_logits = 0.1 * jax.random.normal(ks[3], (DEPTH + 1, D_HGRN), jnp.float32)
    hgrn_gnorm = 1.0 + 0.01 * jax.random.normal(ks[4], (DEPTH, HGRN_HEAD), jnp.float32)
    w_out = jax.random.normal(ks[5], (DEPTH, D_MIX, D_MODEL), jnp.float32) * D_MIX ** -0.5
    final_gain = 1.0 + 0.01 * jax.random.normal(ks[6], (D_MODEL,), jnp.float32)
    return {"x": x, "norm_gain": norm_gain, "w_in": w_in, "lb_logits": lb_logits,
            "hgrn_gnorm": hgrn_gnorm, "w_out": w_out, "final_gain": final_gain}


def _fwd_reference(x, norm_gain, w_in, lb_logits, hgrn_gnorm, w_out, final_gain):
    lb_all = jnp.cumsum(jax.nn.softmax(lb_logits.astype(jnp.float32), axis=0), axis=0)
    split_points = [int(v) for v in np.cumsum(SPLIT_WIDTHS)[:-1]]
    for layer in range(DEPTH):
        h = rms_norm(x, norm_gain[layer])
        z = jnp.einsum('bsd,de->bse', h, w_in[layer])
        q_h, f_h, i_h, g_h, q_a, k_a, v_a, g_a = jnp.split(z, split_points, axis=-1)
        y_h = hgrn2_mixer(q_h, f_h, i_h, lb_all[layer], hgrn_gnorm[layer]) * jax.nn.silu(g_h.astype(jnp.float32))
        y_a = dilated_attention_mixer(q_a, k_a, v_a) * jax.nn.silu(g_a.astype(jnp.float32))
        y = jnp.concatenate([y_h, y_a], axis=-1).astype(x.dtype)
        x = x + jnp.einsum('bse,ed->bsd', y, w_out[layer])
    return rms_norm(x, final_gain)


import jax as _jax
import jax.numpy as _jnp

TWIN_FORMAT = 'train_step'
FWD_PARAMS = ['x', 'norm_gain', 'w_in', 'lb_logits', 'hgrn_gnorm', 'w_out', 'final_gain']
TWIN_WEIGHTS = ['norm_gain', 'w_in', 'lb_logits', 'hgrn_gnorm', 'w_out', 'final_gain']
TWIN_DIFF_INPUT = 'x'
TWIN_INPUTS = ['x', 'norm_gain', 'w_in', 'lb_logits', 'hgrn_gnorm', 'w_out', 'final_gain', 'loss_target', 'm_norm_gain', 'm_w_in', 'm_lb_logits', 'm_hgrn_gnorm', 'm_w_out', 'm_final_gain', 'v_norm_gain', 'v_w_in', 'v_lb_logits', 'v_hgrn_gnorm', 'v_w_out', 'v_final_gain']
TWIN_OUTPUTS = ['loss', 'grad_x', 'grad_norm_gain', 'grad_w_in', 'grad_lb_logits', 'grad_hgrn_gnorm', 'grad_w_out', 'grad_final_gain', 'delta_norm_gain', 'delta_w_in', 'delta_lb_logits', 'delta_hgrn_gnorm', 'delta_w_out', 'delta_final_gain', 'new_m_norm_gain', 'new_m_w_in', 'new_m_lb_logits', 'new_m_hgrn_gnorm', 'new_m_w_out', 'new_m_final_gain', 'new_v_norm_gain', 'new_v_w_in', 'new_v_lb_logits', 'new_v_hgrn_gnorm', 'new_v_w_out', 'new_v_final_gain']
TWIN_LEAF_KINDS = {'loss': 'loss', 'grad_x': 'grad_x', 'grad_norm_gain': 'grad_w', 'grad_w_in': 'grad_w', 'grad_lb_logits': 'grad_w', 'grad_hgrn_gnorm': 'grad_w', 'grad_w_out': 'grad_w', 'grad_final_gain': 'grad_w', 'delta_norm_gain': 'delta_w', 'delta_w_in': 'delta_w', 'delta_lb_logits': 'delta_w', 'delta_hgrn_gnorm': 'delta_w', 'delta_w_out': 'delta_w', 'delta_final_gain': 'delta_w', 'new_m_norm_gain': 'new_m', 'new_m_w_in': 'new_m', 'new_m_lb_logits': 'new_m', 'new_m_hgrn_gnorm': 'new_m', 'new_m_w_out': 'new_m', 'new_m_final_gain': 'new_m', 'new_v_norm_gain': 'new_v', 'new_v_w_in': 'new_v', 'new_v_lb_logits': 'new_v', 'new_v_hgrn_gnorm': 'new_v', 'new_v_w_out': 'new_v', 'new_v_final_gain': 'new_v'}


def _forward(args):
    return _fwd_reference(*[args[k] for k in FWD_PARAMS])


def _output_shape():
    def fwd():
        inp = _fwd_setup_inputs(0)
        return _fwd_reference(*[inp[k] for k in FWD_PARAMS])
    out = _jax.eval_shape(fwd)
    return out.shape, out.dtype

N_MICROBATCH = 1
ADAM_LR = 0.001
ADAM_B1 = 0.9
ADAM_B2 = 0.999
ADAM_EPS = 1e-08
ADAM_WD = 0.01
ADAM_STEP = 10
PER_EXAMPLE_BATCH_AXIS = {'x': 0, 'loss_target': 0}
SHARED_INPUTS = []
_WEIGHT_DTYPES = {'norm_gain': _jnp.float32, 'w_in': _jnp.float32, 'lb_logits': _jnp.float32, 'hgrn_gnorm': _jnp.float32, 'w_out': _jnp.float32, 'final_gain': _jnp.float32}
MOMENT_SCALE = {'norm_gain': 7.737896e-02, 'w_in': 3.855911e-02, 'lb_logits': 6.441831e-03, 'hgrn_gnorm': 2.307248e-01, 'w_out': 5.174815e-02, 'final_gain': 3.196581e+01}


def _to_microbatches(a, axis):
    t = _jnp.moveaxis(a, axis, 0)
    t = t.reshape((N_MICROBATCH, t.shape[0] // N_MICROBATCH) + t.shape[1:])
    return _jnp.moveaxis(t, 1, axis + 1)


def setup_inputs(seed: int = 0) -> dict:
    inp = _fwd_setup_inputs(seed)
    key = _jax.random.fold_in(_jax.random.key(seed), 7919)
    shape, _ = _output_shape()
    out = dict(inp)
    out["loss_target"] = _jax.random.normal(_jax.random.fold_in(key, 0), shape, _jnp.float32)
    for i, name in enumerate(TWIN_WEIGHTS):
        w = inp[name].astype(_jnp.float32)
        if MOMENT_SCALE is None:
            s = _jnp.sqrt(_jnp.mean(_jnp.square(w)) + 1e-30)
        else:
            s = MOMENT_SCALE[name]
        km, kv = _jax.random.split(_jax.random.fold_in(key, i + 1))
        out[name] = w
        out["m_" + name] = s * _jax.random.normal(km, w.shape, _jnp.float32)
        out["v_" + name] = (s * s) * _jax.random.uniform(kv, w.shape, _jnp.float32, 0.5, 1.5)
    if N_MICROBATCH > 1:
        for name, axis in PER_EXAMPLE_BATCH_AXIS.items():
            out[name] = _to_microbatches(out[name], axis)
    return {'x': out['x'], 'norm_gain': out['norm_gain'], 'w_in': out['w_in'], 'lb_logits': out['lb_logits'], 'hgrn_gnorm': out['hgrn_gnorm'], 'w_out': out['w_out'], 'final_gain': out['final_gain'], 'loss_target': out['loss_target'], 'm_norm_gain': out['m_norm_gain'], 'm_w_in': out['m_w_in'], 'm_lb_logits': out['m_lb_logits'], 'm_hgrn_gnorm': out['m_hgrn_gnorm'], 'm_w_out': out['m_w_out'], 'm_final_gain': out['m_final_gain'], 'v_norm_gain': out['v_norm_gain'], 'v_w_in': out['v_w_in'], 'v_lb_logits': out['v_lb_logits'], 'v_hgrn_gnorm': out['v_hgrn_gnorm'], 'v_w_out': out['v_w_out'], 'v_final_gain': out['v_final_gain']}


def _loss(weights, diff, rest, loss_target):
    with _jax.named_scope("forward"):
        args = {**rest, TWIN_DIFF_INPUT: diff, **{k: w.astype(_WEIGHT_DTYPES[k]) for k, w in weights.items()}}
        y = _forward(args)
    with _jax.named_scope("loss_head"):
        err = _jnp.square(y.astype(_jnp.float32) - loss_target)
        return 0.5 * _jnp.sum(_jnp.mean(err, axis=-1)) if err.ndim else 0.5 * err


def _adamw(w, g, m, v):
    m = ADAM_B1 * m + (1.0 - ADAM_B1) * g
    v = ADAM_B2 * v + (1.0 - ADAM_B2) * _jnp.square(g)
    m_hat = m / (1.0 - ADAM_B1 ** ADAM_STEP)
    v_hat = v / (1.0 - ADAM_B2 ** ADAM_STEP)
    delta = -ADAM_LR * (m_hat / (_jnp.sqrt(v_hat) + ADAM_EPS) + ADAM_WD * w)
    return delta, m, v


def reference(x, norm_gain, w_in, lb_logits, hgrn_gnorm, w_out, final_gain, loss_target, m_norm_gain, m_w_in, m_lb_logits, m_hgrn_gnorm, m_w_out, m_final_gain, v_norm_gain, v_w_in, v_lb_logits, v_hgrn_gnorm, v_w_out, v_final_gain):
    given = dict(x=x, norm_gain=norm_gain, w_in=w_in, lb_logits=lb_logits, hgrn_gnorm=hgrn_gnorm, w_out=w_out, final_gain=final_gain, loss_target=loss_target, m_norm_gain=m_norm_gain, m_w_in=m_w_in, m_lb_logits=m_lb_logits, m_hgrn_gnorm=m_hgrn_gnorm, m_w_out=m_w_out, m_final_gain=m_final_gain, v_norm_gain=v_norm_gain, v_w_in=v_w_in, v_lb_logits=v_lb_logits, v_hgrn_gnorm=v_hgrn_gnorm, v_w_out=v_w_out, v_final_gain=v_final_gain)
    weights = {n: given[n] for n in TWIN_WEIGHTS}
    shared = {n: given[n] for n in SHARED_INPUTS}
    per_example = {n: given[n] for n in ['x']}
    grad_fn = _jax.value_and_grad(_loss, argnums=(0, 1))

    def one_microbatch(ex, loss_target):
        ex = dict(ex)
        diff = ex.pop(TWIN_DIFF_INPUT)
        return grad_fn(weights, diff, {**shared, **ex}, loss_target)

    if N_MICROBATCH == 1:
        loss, (grad_w, grad_x) = one_microbatch(per_example, given["loss_target"])
    else:
        def body(carry, xs):
            loss_sum, grad_sum = carry
            l_k, (gw_k, gx_k) = one_microbatch(xs[0], xs[1])
            with _jax.named_scope("update"):
                return (loss_sum + l_k, _jax.tree.map(_jnp.add, grad_sum, gw_k)), gx_k

        init = (_jnp.zeros((), _jnp.float32), _jax.tree.map(_jnp.zeros_like, weights))
        (loss, grad_w), grad_x = _jax.lax.scan(body, init, (per_example, given["loss_target"]))
    with _jax.named_scope("update"):
        delta_w, new_m, new_v = {}, {}, {}
        for n in TWIN_WEIGHTS:
            delta_w[n], new_m[n], new_v[n] = _adamw(weights[n], grad_w[n], given["m_" + n], given["v_" + n])
    return (loss, grad_x, *[grad_w[n] for n in TWIN_WEIGHTS], *[delta_w[n] for n in TWIN_WEIGHTS],
            *[new_m[n] for n in TWIN_WEIGHTS], *[new_v[n] for n in TWIN_WEIGHTS])
```

```python
import functools
import math

import jax
import jax.numpy as jnp
from jax import lax
from jax.experimental import pallas as pl
from jax.experimental.pallas import tpu as pltpu

NORM_EPS = 1e-6
HGRN_HEAD = 128
HGRN_CHUNK = 64
ATTN_HEAD = 64
ATTN_BAND = 128
DILATIONS = (1, 4, 16)
N_SPLITS = 8
N_DEV = 8
ADAM_LR = 0.001
ADAM_B1 = 0.9
ADAM_B2 = 0.999
ADAM_EPS = 1e-08
ADAM_WD = 0.01
ADAM_STEP = 10
LANES = 128
MESH = pl.DeviceIdType.MESH
F32 = jnp.float32
BF16 = jnp.bfloat16
NEG_BIG = -1e30
VMEM_LIMIT = 56 * 1024 * 1024

ANY = pl.BlockSpec(memory_space=pl.ANY)


def _params(*sem):
    return pltpu.CompilerParams(dimension_semantics=sem, vmem_limit_bytes=VMEM_LIMIT)


def _tile(n, pref):
    t = min(n, pref)
    assert n % t == 0, (n, pref)
    return t


def _dot(a, b, precision=None):
    return jnp.dot(a, b, preferred_element_type=F32, precision=precision)


def _dot_nt(a, b):
    return lax.dot_general(a, b, (((1,), (1,)), ((), ())), preferred_element_type=F32)


def _dot_tn(a, b):
    return lax.dot_general(a, b, (((0,), (0,)), ((), ())), preferred_element_type=F32)


def _sigmoid(x):
    return 1.0 / (1.0 + jnp.exp(-x))


def _dsilu(x, s):
    return s * (1.0 + x * (1.0 - s))


def _adamw(w, g, m, v):
    m = ADAM_B1 * m + (1.0 - ADAM_B1) * g
    v = ADAM_B2 * v + (1.0 - ADAM_B2) * (g * g)
    m_hat = m / (1.0 - ADAM_B1 ** ADAM_STEP)
    v_hat = v / (1.0 - ADAM_B2 ** ADAM_STEP)
    delta = -ADAM_LR * (m_hat / (jnp.sqrt(v_hat) + ADAM_EPS) + ADAM_WD * w)
    return delta, m, v


def _cast_bf16(a):
    r, c = a.shape
    tr = _tile(r, 256)

    def body(a_ref, o_ref):
        o_ref[...] = a_ref[...].astype(BF16)

    return pl.pallas_call(
        body, name="cast_bf16", grid=(r // tr,), out_shape=jax.ShapeDtypeStruct((r, c), BF16),
        in_specs=[pl.BlockSpec((tr, c), lambda i: (i, 0))], out_specs=pl.BlockSpec((tr, c), lambda i: (i, 0)),
        compiler_params=_params("parallel"))(a)


def _rmsnorm_fwd(x, gain):
    s, d = x.shape
    tm = _tile(s, 512)

    def body(x_ref, g_ref, h_ref):
        xv = x_ref[...]
        r = lax.rsqrt(jnp.mean(xv * xv, axis=-1, keepdims=True) + NORM_EPS)
        h_ref[...] = (xv * r * g_ref[...]).astype(BF16)

    return pl.pallas_call(
        body, name="rmsnorm_fwd", grid=(s // tm,), out_shape=jax.ShapeDtypeStruct((s, d), BF16),
        in_specs=[pl.BlockSpec((tm, d), lambda i: (i, 0)), pl.BlockSpec((1, d), lambda i: (0, 0))],
        out_specs=pl.BlockSpec((tm, d), lambda i: (i, 0)), compiler_params=_params("parallel"))(x, gain)


def _inproj(h, w_full):
    s, d = h.shape
    e = w_full.shape[2]
    tm = _tile(s, 512)

    def body(h_ref, w_ref, z_ref, zb_ref):
        acc = _dot(h_ref[...], w_ref[...])
        z_ref[...] = acc
        zb_ref[...] = acc.astype(BF16)

    return pl.pallas_call(
        body, name="inproj", grid=(s // tm, N_SPLITS),
        out_shape=(jax.ShapeDtypeStruct((s, N_SPLITS * e), F32), jax.ShapeDtypeStruct((s, N_SPLITS * e), BF16)),
        in_specs=[pl.BlockSpec((tm, d), lambda i, j: (i, 0)), pl.BlockSpec((None, d, e), lambda i, j: (j, 0, 0))],
        out_specs=(pl.BlockSpec((tm, e), lambda i, j: (i, j)), pl.BlockSpec((tm, e), lambda i, j: (i, j))),
        compiler_params=_params("parallel", "arbitrary"))(h, w_full)


def _chunk_masks(tb):
    row = lax.broadcasted_iota(jnp.int32, (tb, tb), 0)
    col = lax.broadcasted_iota(jnp.int32, (tb, tb), 1)
    same = (row // HGRN_CHUNK) == (col // HGRN_CHUNK)
    lower = jnp.where(same & (col <= row), 1.0, 0.0).astype(F32)
    upper = jnp.where(same & (col >= row), 1.0, 0.0).astype(F32)
    return lower, upper


def _hgrn_gates(qp, fp, lbv):
    lb = _sigmoid(lbv[0:1] - lbv[1:2])
    sq = _sigmoid(qp)
    q = qp * sq
    sg = _sigmoid(fp)
    f = lb + (1.0 - lb) * sg
    k = 1.0 - f
    return lb, sq, q, sg, f, k


def _hgrn_fwd(z, lb_logits, gnorm):
    s = z.shape[0]
    e = z.shape[1] // N_SPLITS
    nh = e // HGRN_HEAD
    tb = _tile(s, 256)
    nc = tb // HGRN_CHUNK
    nb = s // tb
    C = HGRN_CHUNK

    def body(q_ref, f_ref, i_ref, g_ref, lb_ref, gn_ref, y_ref, st_ref, state, o_scr):
        @pl.when(pl.program_id(1) == 0)
        def _():
            state[...] = jnp.zeros_like(state)

        lb, sq, q, sg, f, k = _hgrn_gates(q_ref[...], f_ref[...], lb_ref[...])
        lower, _ = _chunk_masks(tb)
        b = _dot(lower, jnp.log(f), precision=lax.Precision.HIGHEST)
        b3 = b.reshape(nc, C, HGRN_HEAD)
        bc = b3[:, C - 1:C, :]
        qt = (q * jnp.exp(b)).astype(BF16)
        kt = (k * jnp.exp(-b)).astype(BF16)
        ke = (k.reshape(nc, C, HGRN_HEAD) * jnp.exp(bc - b3)).reshape(tb, HGRN_HEAD).astype(BF16)
        v = i_ref[...].astype(BF16)
        tri = lax.broadcasted_iota(jnp.int32, (C, C), 1) <= lax.broadcasted_iota(jnp.int32, (C, C), 0)
        for c in range(nc):
            sl = slice(c * C, (c + 1) * C)
            st = state[...]
            st_ref[c] = st
            a = jnp.where(tri, _dot_nt(qt[sl], kt[sl]), 0.0)
            o_scr[sl, :] = _dot(a.astype(BF16), v[sl]) + _dot_nt(qt[sl], st.astype(BF16))
            state[...] = st * jnp.exp(bc[c]) + _dot_tn(v[sl], ke[sl])
        o = o_scr[...]
        rms = lax.rsqrt(jnp.mean(o * o, axis=-1, keepdims=True) + NORM_EPS)
        gp = g_ref[...]
        y_ref[...] = (o * rms * gn_ref[...] * (gp * _sigmoid(gp))).astype(BF16)

    col = lambda kk: (lambda h, n: (n, kk * nh + h))
    return pl.pallas_call(
        body, name="hgrn_fwd", grid=(nh, nb),
        out_shape=(jax.ShapeDtypeStruct((s, e), BF16),
                   jax.ShapeDtypeStruct((nh, s // C, HGRN_HEAD, HGRN_HEAD), F32)),
        in_specs=[pl.BlockSpec((tb, HGRN_HEAD), col(0)), pl.BlockSpec((tb, HGRN_HEAD), col(1)),
                  pl.BlockSpec((tb, HGRN_HEAD), col(2)), pl.BlockSpec((tb, HGRN_HEAD), col(3)),
                  pl.BlockSpec((2, HGRN_HEAD), lambda h, n: (0, h)), pl.BlockSpec((1, HGRN_HEAD), lambda h, n: (0, 0))],
        out_specs=(pl.BlockSpec((tb, HGRN_HEAD), lambda h, n: (n, h)),
                   pl.BlockSpec((None, nc, HGRN_HEAD, HGRN_HEAD), lambda h, n: (h, n, 0, 0))),
        scratch_shapes=[pltpu.VMEM((HGRN_HEAD, HGRN_HEAD), F32), pltpu.VMEM((tb, HGRN_HEAD), F32)],
        compiler_params=_params("parallel", "arbitrary"))(z, z, z, z, lb_logits, gnorm)


def _hgrn_bwd(z, dy, states, lb_logits, gnorm):
    s = z.shape[0]
    e = z.shape[1] // N_SPLITS
    nh = e // HGRN_HEAD
    tb = _tile(s, 256)
    nc = tb // HGRN_CHUNK
    nb = s // tb
    C = HGRN_CHUNK
    H = HGRN_HEAD

    def body(q_ref, f_ref, i_ref, g_ref, dy_ref, st_ref, lb_ref, gn_ref, dz_ref, dlb_ref, dgn_ref,
             gstate, o_scr, a_scr, dq_scr, dk_scr, dv_scr, e_scr):
        first = (pl.program_id(0) == 0) & (pl.program_id(1) == 0)

        @pl.when(first)
        def _():
            dgn_ref[...] = jnp.zeros_like(dgn_ref)

        @pl.when(pl.program_id(1) == 0)
        def _():
            gstate[...] = jnp.zeros_like(gstate)
            dlb_ref[...] = jnp.zeros_like(dlb_ref)

        qp = q_ref[...]
        lb, sq, q, sg, f, k = _hgrn_gates(qp, f_ref[...], lb_ref[...])
        lower, upper = _chunk_masks(tb)
        b = _dot(lower, jnp.log(f), precision=lax.Precision.HIGHEST)
        b3 = b.reshape(nc, C, H)
        bc = b3[:, C - 1:C, :]
        eb = jnp.exp(b)
        enb = jnp.exp(-b)
        eend = jnp.exp(bc - b3).reshape(tb, H)
        qt = (q * eb).astype(BF16)
        kt = (k * enb).astype(BF16)
        ke = (k * eend).astype(BF16)
        v = i_ref[...].astype(BF16)
        tri = lax.broadcasted_iota(jnp.int32, (C, C), 1) <= lax.broadcasted_iota(jnp.int32, (C, C), 0)
        for c in range(nc):
            sl = slice(c * C, (c + 1) * C)
            a = jnp.where(tri, _dot_nt(qt[sl], kt[sl]), 0.0)
            a_scr[sl, :] = a
            o_scr[sl, :] = _dot(a.astype(BF16), v[sl]) + _dot_nt(qt[sl], st_ref[c].astype(BF16))
        o = o_scr[...]
        rms = lax.rsqrt(jnp.mean(o * o, axis=-1, keepdims=True) + NORM_EPS)
        on = o * rms
        gn = gn_ref[...]
        gp = g_ref[...]
        sgg = _sigmoid(gp)
        dyv = dy_ref[...]
        d_on = dyv * (gp * sgg)
        dz_ref[3] = (dyv * on * gn * _dsilu(gp, sgg)).astype(BF16)
        dgn_ref[...] += jnp.sum(d_on * on, axis=0, keepdims=True)
        u = d_on * gn
        do = (rms * (u - on * jnp.mean(u * on, axis=-1, keepdims=True))).astype(BF16)
        for c in reversed(range(nc)):
            sl = slice(c * C, (c + 1) * C)
            stp = st_ref[c]
            gt = gstate[...]
            gtb = gt.astype(BF16)
            dec = jnp.exp(bc[c])
            ab = a_scr[sl, :].astype(BF16)
            da = jnp.where(tri, _dot_nt(do[sl], v[sl]), 0.0).astype(BF16)
            dqt = _dot(da, kt[sl]) + _dot(do[sl], stp.astype(BF16))
            dkt = _dot_tn(da, qt[sl])
            dks = _dot(v[sl], gtb) * eend[sl]
            dv_scr[sl, :] = _dot_tn(ab, do[sl]) + _dot_nt(ke[sl], gtb)
            dq_scr[sl, :] = dqt * eb[sl]
            dk_scr[sl, :] = dkt * enb[sl] + dks
            ech = jnp.sum(k[sl] * dks, axis=0, keepdims=True) + jnp.sum(gt * dec * stp, axis=0, keepdims=True)
            e_scr[sl, :] = jnp.broadcast_to(ech, (C, H))
            gstate[...] = gt * dec + _dot_tn(do[sl], qt[sl])
        dq = dq_scr[...]
        dk = dk_scr[...]
        dlf = _dot(upper, q * dq - k * dk, precision=lax.Precision.HIGHEST) + e_scr[...]
        dft = dlf / f - dk
        dz_ref[0] = (dq * _dsilu(qp, sq)).astype(BF16)
        dz_ref[1] = (dft * (1.0 - lb) * sg * (1.0 - sg)).astype(BF16)
        dz_ref[2] = dv_scr[...].astype(BF16)
        dlb_ref[...] += jnp.sum(dft * (1.0 - sg), axis=0, keepdims=True)

    col = lambda kk: (lambda h, n: (nb - 1 - n, kk * nh + h))
    return pl.pallas_call(
        body, name="hgrn_bwd", grid=(nh, nb),
        out_shape=(jax.ShapeDtypeStruct((4, s, e), BF16), jax.ShapeDtypeStruct((1, e), F32),
                   jax.ShapeDtypeStruct((1, H), F32)),
        in_specs=[pl.BlockSpec((tb, H), col(0)), pl.BlockSpec((tb, H), col(1)),
                  pl.BlockSpec((tb, H), col(2)), pl.BlockSpec((tb, H), col(3)),
                  pl.BlockSpec((tb, H), lambda h, n: (nb - 1 - n, h)),
                  pl.BlockSpec((None, nc, H, H), lambda h, n: (h, nb - 1 - n, 0, 0)),
                  pl.BlockSpec((2, H), lambda h, n: (0, h)), pl.BlockSpec((1, H), lambda h, n: (0, 0))],
        out_specs=(pl.BlockSpec((4, tb, H), lambda h, n: (0, nb - 1 - n, h)),
                   pl.BlockSpec((1, H), lambda h, n: (0, h)), pl.BlockSpec((1, H), lambda h, n: (0, 0))),
        scratch_shapes=[pltpu.VMEM((H, H), F32)] + [pltpu.VMEM((tb, H), F32)] + [pltpu.VMEM((tb, C), F32)]
        + [pltpu.VMEM((tb, H), F32)] * 4,
        compiler_params=_params("arbitrary", "arbitrary"))(z, z, z, z, dy, states, lb_logits, gnorm)


def _slope_times(hh, dil, nheads):
    head = (2 * pl.program_id(0) + hh + 1).astype(F32)
    return jnp.exp(jnp.full((1, 1), -8.0 / nheads * math.log(2.0), F32) * head) * float(dil)


def _attn_fwd(zb, dil):
    s = zb.shape[0]
    e = zb.shape[1] // N_SPLITS
    npair = e // LANES
    L = s // dil
    tq = _tile(L, 512)
    ns = tq // ATTN_BAND
    nub = L // tq
    W = ATTN_BAND
    zv = zb.reshape(L, dil * N_SPLITS * e)
    ncol = N_SPLITS * npair

    def body(q_ref, kp_ref, kc_ref, vp_ref, vc_ref, o_ref, l_ref, kbuf, vbuf):
        ub = pl.program_id(2)
        kbuf[0:W, :] = kp_ref[...]
        kbuf[W:, :] = kc_ref[...]
        vbuf[0:W, :] = vp_ref[...]
        vbuf[W:, :] = vc_ref[...]
        qi = lax.broadcasted_iota(jnp.int32, (W, 2 * W), 0)
        kj = lax.broadcasted_iota(jnp.int32, (W, 2 * W), 1)
        delta = W + qi - kj
        band = (delta >= 0) & (delta <= W)
        dist = delta.astype(F32)
        for a in range(ns):
            first_key = jnp.where(ub * ns + a > 0, 0, W)
            valid = band & (kj >= first_key)
            for hh in range(2):
                cs = slice(hh * ATTN_HEAD, (hh + 1) * ATTN_HEAD)
                q = q_ref[a * W:(a + 1) * W, cs]
                k = kbuf[a * W:(a + 2) * W, cs]
                vv = vbuf[a * W:(a + 2) * W, cs]
                sc = _dot_nt(q, k) * (ATTN_HEAD ** -0.5) - _slope_times(hh, dil, 2 * npair) * dist
                sc = jnp.where(valid, sc, NEG_BIG)
                m = jnp.max(sc, axis=-1, keepdims=True)
                p = jnp.exp(sc - m)
                den = jnp.sum(p, axis=-1, keepdims=True)
                o_ref[a * W:(a + 1) * W, cs] = _dot(p.astype(BF16), vv) / den
                l_ref[a * W:(a + 1) * W, cs] = jnp.broadcast_to(m + jnp.log(den), (W, ATTN_HEAD))

    def cur(split):
        return lambda hp, r, ub: (ub, r * ncol + split * npair + hp)

    def prev(split):
        return lambda hp, r, ub: (jnp.maximum(ub * ns - 1, 0), r * ncol + split * npair + hp)

    o, lse = pl.pallas_call(
        body, name=f"attn_fwd_d{dil}", grid=(npair, dil, nub),
        out_shape=(jax.ShapeDtypeStruct((L, dil * e), F32), jax.ShapeDtypeStruct((L, dil * e), F32)),
        in_specs=[pl.BlockSpec((tq, LANES), cur(4)),
                  pl.BlockSpec((W, LANES), prev(5)), pl.BlockSpec((tq, LANES), cur(5)),
                  pl.BlockSpec((W, LANES), prev(6)), pl.BlockSpec((tq, LANES), cur(6))],
        out_specs=(pl.BlockSpec((tq, LANES), lambda hp, r, ub: (ub, r * npair + hp)),
                   pl.BlockSpec((tq, LANES), lambda hp, r, ub: (ub, r * npair + hp))),
        scratch_shapes=[pltpu.VMEM((tq + W, LANES), BF16), pltpu.VMEM((tq + W, LANES), BF16)],
        compiler_params=_params("parallel", "parallel", "arbitrary"))(zv, zv, zv, zv, zv)
    return o.reshape(s, e), lse.reshape(s, e)


def _attn_merge(outs, lses, z):
    s, e = outs[0].shape
    tm = _tile(s, 512)
    gcol = 7

    def body(o1, o2, o3, l1, l2, l3, g_ref, o_ref, l_ref, y_ref):
        a1, a2, a3 = l1[...], l2[...], l3[...]
        m = jnp.maximum(jnp.maximum(a1, a2), a3)
        w1, w2, w3 = jnp.exp(a1 - m), jnp.exp(a2 - m), jnp.exp(a3 - m)
        den = w1 + w2 + w3
        o = (w1 * o1[...] + w2 * o2[...] + w3 * o3[...]) / den
        o_ref[...] = o
        l_ref[...] = m + jnp.log(den)
        gp = g_ref[...]
        y_ref[...] = (o * (gp * _sigmoid(gp))).astype(BF16)

    blk = pl.BlockSpec((tm, e), lambda i: (i, 0))
    return pl.pallas_call(
        body, name="attn_merge", grid=(s // tm,),
        out_shape=(jax.ShapeDtypeStruct((s, e), F32), jax.ShapeDtypeStruct((s, e), F32),
                   jax.ShapeDtypeStruct((s, e), BF16)),
        in_specs=[blk] * 6 + [pl.BlockSpec((tm, e), lambda i: (i, gcol))], out_specs=(blk, blk, blk),
        compiler_params=_params("parallel"))(*outs, *lses, z)


def _outproj_loss(x, y_h, y_a, w_out_full, final_gain, target):
    s, d = x.shape
    e = y_h.shape[1]
    tm = _tile(s, 256)

    def body(x_ref, yh_ref, ya_ref, w_ref, g_ref, t_ref, dx_ref, dxb_ref, dy_ref, loss_ref, dg_ref):
        @pl.when(pl.program_id(0) == 0)
        def _():
            loss_ref[...] = jnp.zeros_like(loss_ref)
            dg_ref[...] = jnp.zeros_like(dg_ref)

        w = w_ref[...]
        x2 = x_ref[...] + _dot(yh_ref[...], w[0:e]) + _dot(ya_ref[...], w[e:2 * e])
        r = lax.rsqrt(jnp.mean(x2 * x2, axis=-1, keepdims=True) + NORM_EPS)
        xn = x2 * r
        g = g_ref[...]
        err = xn * g - t_ref[...]
        loss_ref[...] += jnp.sum(err * err, axis=0, keepdims=True) * (0.5 / d)
        dyo = err * (1.0 / d)
        dg_ref[...] += jnp.sum(dyo * xn, axis=0, keepdims=True)
        u = dyo * g
        dx2 = r * (u - xn * jnp.mean(u * xn, axis=-1, keepdims=True))
        dx_ref[...] = dx2
        dxb = dx2.astype(BF16)
        dxb_ref[...] = dxb
        dy_ref[...] = _dot_nt(dxb, w)

    row = pl.BlockSpec((tm, d), lambda i: (i, 0))
    half = pl.BlockSpec((tm, e), lambda i: (i, 0))
    vec = pl.BlockSpec((1, d), lambda i: (0, 0))
    return pl.pallas_call(
        body, name="outproj_loss", grid=(s // tm,),
        out_shape=(jax.ShapeDtypeStruct((s, d), F32), jax.ShapeDtypeStruct((s, d), BF16),
                   jax.ShapeDtypeStruct((s, 2 * e), F32), jax.ShapeDtypeStruct((1, d), F32),
                   jax.ShapeDtypeStruct((1, d), F32)),
        in_specs=[row, half, half, pl.BlockSpec((2 * e, d), lambda i: (0, 0)), vec, row],
        out_specs=(row, row, pl.BlockSpec((tm, 2 * e), lambda i: (i, 0)), vec, vec),
        compiler_params=_params("arbitrary"))(x, y_h, y_a, w_out_full, final_gain, target)


def _dwout(y_h, y_a, dxb):
    s, e = y_h.shape
    d = dxb.shape[1]
    ts = _tile(s, 512)
    ns = s // ts

    def body(yh_ref, ya_ref, dx_ref, o_ref, acc):
        half = pl.program_id(0)
        step = pl.program_id(1)

        @pl.when(step == 0)
        def _():
            acc[...] = jnp.zeros_like(acc)

        @pl.when(half == 0)
        def _():
            acc[...] += _dot_tn(yh_ref[...], dx_ref[...])

        @pl.when(half == 1)
        def _():
            acc[...] += _dot_tn(ya_ref[...], dx_ref[...])

        @pl.when(step == ns - 1)
        def _():
            o_ref[...] = acc[...].astype(BF16)

    return pl.pallas_call(
        body, name="dwout", grid=(2, ns), out_shape=jax.ShapeDtypeStruct((2 * e, d), BF16),
        in_specs=[pl.BlockSpec((ts, e), lambda hf, k: (k * (1 - hf), 0)), pl.BlockSpec((ts, e), lambda hf, k: (k * hf, 0)),
                  pl.BlockSpec((ts, d), lambda hf, k: (k, 0))],
        out_specs=pl.BlockSpec((e, d), lambda hf, k: (hf, 0)),
        scratch_shapes=[pltpu.VMEM((e, d), F32)],
        compiler_params=_params("parallel", "arbitrary"))(y_h, y_a, dxb)


def _attn_gate_bwd(dy, o, z):
    s, e = o.shape
    tm = _tile(s, 512)
    gcol = 7

    def body(dy_ref, o_ref, g_ref, do_ref, dg_ref):
        gp = g_ref[...]
        sg = _sigmoid(gp)
        dyv = dy_ref[...]
        do_ref[...] = dyv * (gp * sg)
        dg_ref[0] = (dyv * o_ref[...] * _dsilu(gp, sg)).astype(BF16)

    blk = pl.BlockSpec((tm, e), lambda i: (i, 0))
    return pl.pallas_call(
        body, name="attn_gate_bwd", grid=(s // tm,),
        out_shape=(jax.ShapeDtypeStruct((s, e), F32), jax.ShapeDtypeStruct((1, s, e), BF16)),
        in_specs=[pl.BlockSpec((tm, e), lambda i: (i, 1)), blk, pl.BlockSpec((tm, e), lambda i: (i, gcol))],
        out_specs=(blk, pl.BlockSpec((1, tm, e), lambda i: (0, i, 0))),
        compiler_params=_params("parallel"))(dy, o, z)


def _attn_bwd(zb, do, o, lse, dil, acc, out_dtype):
    s = zb.shape[0]
    e = zb.shape[1] // N_SPLITS
    npair = e // LANES
    L = s // dil
    tq = _tile(L, 512)
    ns = tq // ATTN_BAND
    nub = L // tq
    W = ATTN_BAND
    zv = zb.reshape(L, dil * N_SPLITS * e)
    ncol = N_SPLITS * npair
    view = lambda a: a.reshape(L, dil * e)
    has_acc = acc is not None

    def body(*refs):
        (qc_ref, qn_ref, k_ref, v_ref, doc_ref, don_ref, oc_ref, on_ref, lc_ref, ln_ref) = refs[:10]
        refs = refs[10:]
        if has_acc:
            acc_ref, refs = refs[0], refs[1:]
        out_ref, qbuf, dobuf, obuf, lbuf, dqacc, carry = refs
        ub = pl.program_id(2)
        qbuf[0:tq, :] = qc_ref[...]
        qbuf[tq:, :] = qn_ref[...]
        dobuf[0:tq, :] = doc_ref[...]
        dobuf[tq:, :] = don_ref[...]
        obuf[0:tq, :] = oc_ref[...]
        obuf[tq:, :] = on_ref[...]
        lbuf[0:tq, :] = lc_ref[...]
        lbuf[tq:, :] = ln_ref[...]

        @pl.when(ub == 0)
        def _():
            carry[...] = jnp.zeros_like(carry)

        dqacc[0:W, :] = carry[...]
        dqacc[W:, :] = jnp.zeros((tq, LANES), F32)
        qi = lax.broadcasted_iota(jnp.int32, (2 * W, W), 0)
        kj = lax.broadcasted_iota(jnp.int32, (2 * W, W), 1)
        delta = qi - kj
        band = (delta >= 0) & (delta <= W)
        dist = delta.astype(F32)
        scale = ATTN_HEAD ** -0.5
        for b in range(ns):
            q_limit = jnp.where(ub * ns + b + 1 < nub * ns, 2 * W, W)
            valid = band & (qi < q_limit)
            rows = slice(b * W, (b + 2) * W)
            for hh in range(2):
                cs = slice(hh * ATTN_HEAD, (hh + 1) * ATTN_HEAD)
                k = k_ref[b * W:(b + 1) * W, cs]
                vv = v_ref[b * W:(b + 1) * W, cs]
                q = qbuf[rows, cs]
                dof = dobuf[rows, cs]
                dob = dof.astype(BF16)
                lcol = lbuf[rows, hh * ATTN_HEAD:hh * ATTN_HEAD + 1]
                dcol = jnp.sum(dof * obuf[rows, cs], axis=-1, keepdims=True)
                sc = _dot_nt(q, k) * scale - _slope_times(hh, dil, 2 * npair) * dist - lcol
                p = jnp.exp(jnp.where(valid, sc, NEG_BIG))
                dp = _dot_nt(dob, vv)
                ds = (p * (dp - dcol)).astype(BF16)
                pb = p.astype(BF16)
                dv = _dot_tn(pb, dob)
                dk = _dot_tn(ds, q) * scale
                dqacc[rows, cs] += _dot(ds, k) * scale
                if has_acc:
                    dk = dk + acc_ref[1, b * W:(b + 1) * W, cs]
                    dv = dv + acc_ref[2, b * W:(b + 1) * W, cs]
                out_ref[1, b * W:(b + 1) * W, cs] = dk.astype(out_dtype)
                out_ref[2, b * W:(b + 1) * W, cs] = dv.astype(out_dtype)
        dq = dqacc[0:tq, :]
        if has_acc:
            dq = dq + acc_ref[0]
        out_ref[0] = dq.astype(out_dtype)
        carry[...] = dqacc[tq:, :]

    def zc(split):
        return lambda hp, r, ub: (ub, r * ncol + split * npair + hp)

    def zn(split):
        return lambda hp, r, ub: (jnp.minimum((ub + 1) * ns, nub * ns - 1), r * ncol + split * npair + hp)

    ec = lambda hp, r, ub: (ub, r * npair + hp)
    en = lambda hp, r, ub: (jnp.minimum((ub + 1) * ns, nub * ns - 1), r * npair + hp)
    cur_e = pl.BlockSpec((tq, LANES), ec)
    nxt_e = pl.BlockSpec((W, LANES), en)
    in_specs = [pl.BlockSpec((tq, LANES), zc(4)), pl.BlockSpec((W, LANES), zn(4)),
                pl.BlockSpec((tq, LANES), zc(5)), pl.BlockSpec((tq, LANES), zc(6)),
                cur_e, nxt_e, cur_e, nxt_e, cur_e, nxt_e]
    args = [zv, zv, zv, zv, view(do), view(do), view(o), view(o), view(lse), view(lse)]
    trio = pl.BlockSpec((3, tq, LANES), lambda hp, r, ub: (0, ub, r * npair + hp))
    if has_acc:
        in_specs.append(trio)
        args.append(acc.reshape(3, L, dil * e))
    out = pl.pallas_call(
        body, name=f"attn_bwd_d{dil}", grid=(npair, dil, nub),
        out_shape=jax.ShapeDtypeStruct((3, L, dil * e), out_dtype), in_specs=in_specs, out_specs=trio,
        scratch_shapes=[pltpu.VMEM((tq + W, LANES), BF16), pltpu.VMEM((tq + W, LANES), F32),
                        pltpu.VMEM((tq + W, LANES), F32), pltpu.VMEM((tq + W, LANES), F32),
                        pltpu.VMEM((tq + W, LANES), F32), pltpu.VMEM((W, LANES), F32)],
        compiler_params=_params("parallel", "parallel", "arbitrary"))(*args)
    return out.reshape(3, s, e)


def _dz_specs(tm, e, axis):
    def mk(lo, hi):
        def index(i, k):
            row, grp = (i, k) if axis == 1 else (k, i)
            return (jnp.clip(grp - lo, 0, hi - lo - 1), row, 0)
        return pl.BlockSpec((None, tm, e), index)
    return [mk(0, 4), mk(4, 7), mk(7, 8)]


def _dz_pick(grp, dzh_ref, dza_ref, dzg_ref, fn):
    @pl.when(grp < 4)
    def _():
        fn(dzh_ref[...])

    @pl.when((grp >= 4) & (grp < 7))
    def _():
        fn(dza_ref[...])

    @pl.when(grp == 7)
    def _():
        fn(dzg_ref[...])


def _dh_dx(dzh, dza, dzg, w_full, x, gain, dx2):
    s, d = x.shape
    e = dzh.shape[2]
    tm = _tile(s, 512)

    def body(dzh_ref, dza_ref, dzg_ref, w_ref, x_ref, g_ref, dx2_ref, gx_ref, dg_ref, acc):
        i, k = pl.program_id(0), pl.program_id(1)

        @pl.when((i == 0) & (k == 0))
        def _():
            dg_ref[...] = jnp.zeros_like(dg_ref)

        @pl.when(k == 0)
        def _():
            acc[...] = jnp.zeros_like(acc)

        def add(dz):
            acc[...] += _dot_nt(dz, w_ref[...])

        _dz_pick(k, dzh_ref, dza_ref, dzg_ref, add)

        @pl.when(k == N_SPLITS - 1)
        def _():
            dh = acc[...]
            xv = x_ref[...]
            r = lax.rsqrt(jnp.mean(xv * xv, axis=-1, keepdims=True) + NORM_EPS)
            xn = xv * r
            dg_ref[...] += jnp.sum(dh * xn, axis=0, keepdims=True)
            u = dh * g_ref[...]
            gx_ref[...] = dx2_ref[...] + r * (u - xn * jnp.mean(u * xn, axis=-1, keepdims=True))

    row = pl.BlockSpec((tm, d), lambda i, k: (i, 0))
    vec = pl.BlockSpec((1, d), lambda i, k: (0, 0))
    return pl.pallas_call(
        body, name="dh_dx", grid=(s // tm, N_SPLITS),
        out_shape=(jax.ShapeDtypeStruct((s, d), F32), jax.ShapeDtypeStruct((1, d), F32)),
        in_specs=_dz_specs(tm, e, 1) + [pl.BlockSpec((None, d, e), lambda i, k: (k, 0, 0)), row, vec, row],
        out_specs=(row, vec), scratch_shapes=[pltpu.VMEM((tm, d), F32)],
        compiler_params=_params("arbitrary", "arbitrary"))(dzh, dza, dzg, w_full, x, gain, dx2)


def _dwin(h, dzh, dza, dzg):
    s, d = h.shape
    e = dzh.shape[2]
    ts = _tile(s, 512)
    ns = s // ts

    def body(dzh_ref, dza_ref, dzg_ref, h_ref, o_ref, acc):
        j, k = pl.program_id(0), pl.program_id(1)

        @pl.when(k == 0)
        def _():
            acc[...] = jnp.zeros_like(acc)

        def add(dz):
            acc[...] += _dot_tn(h_ref[...], dz)

        _dz_pick(j, dzh_ref, dza_ref, dzg_ref, add)

        @pl.when(k == ns - 1)
        def _():
            o_ref[...] = acc[...].astype(BF16)

    return pl.pallas_call(
        body, name="dwin", grid=(N_SPLITS, ns), out_shape=jax.ShapeDtypeStruct((N_SPLITS, d, e), BF16),
        in_specs=_dz_specs(ts, e, 0) + [pl.BlockSpec((ts, d), lambda j, k: (k, 0))],
        out_specs=pl.BlockSpec((None, d, e), lambda j, k: (j, 0, 0)),
        scratch_shapes=[pltpu.VMEM((d, e), F32)],
        compiler_params=_params("parallel", "arbitrary"))(dzh, dza, dzg, h)


def _pair_add(p, ra, c_idx):
    _, r, c = p.shape
    tr = _tile(r, 256)
    p4 = p.reshape(4, 2, r, c)

    def body(c_ref, p_ref, ra_ref, o_ref):
        o_ref[...] = (p_ref[...].astype(F32) + ra_ref[...].astype(F32)).astype(BF16)

    grid_spec = pltpu.PrefetchScalarGridSpec(
        num_scalar_prefetch=1, grid=(4, r // tr),
        in_specs=[pl.BlockSpec((None, None, tr, c), lambda j, i, cref: (j, cref[0], i, 0)),
                  pl.BlockSpec((None, tr, c), lambda j, i, cref: (j, i, 0))],
        out_specs=pl.BlockSpec((None, tr, c), lambda j, i, cref: (j, i, 0)))
    return pl.pallas_call(
        body, name="pair_add", grid_spec=grid_spec, out_shape=jax.ShapeDtypeStruct((4, r, c), BF16),
        compiler_params=_params("parallel", "parallel"))(c_idx, p4, ra)


def _sum_adamw(rb, w, m, v):
    r, c = w.shape
    tr = _tile(r, 128)

    def body(rb_ref, w_ref, m_ref, v_ref, g_ref, d_ref, mo_ref, vo_ref):
        g = rb_ref[0].astype(F32)
        for j in range(1, 4):
            g = g + rb_ref[j].astype(F32)
        g_ref[...] = g
        d_ref[...], mo_ref[...], vo_ref[...] = _adamw(w_ref[...], g, m_ref[...], v_ref[...])

    blk = pl.BlockSpec((tr, c), lambda i: (i, 0))
    shp = jax.ShapeDtypeStruct((r, c), F32)
    return pl.pallas_call(
        body, name="sum_adamw", grid=(r // tr,), out_shape=(shp, shp, shp, shp),
        in_specs=[pl.BlockSpec((4, tr, c), lambda i: (0, i, 0)), blk, blk, blk], out_specs=(blk, blk, blk, blk),
        compiler_params=_params("parallel"))(rb, w, m, v)


def _position():
    x, y, c = lax.axis_index("x"), lax.axis_index("y"), lax.axis_index("c")
    return x, y, c


def _all_gather_weights(a, b):
    def body(a_ref, b_ref, ao_ref, bo_ref, send_sems, recv_sems, local_sems):
        x, y, c = _position()
        me, sibling = (x, y, c), (x, y, 1 - c)
        chips = [(1 - x, y), (x, 1 - y), (1 - x, 1 - y)]
        srcs, outs = (a_ref, b_ref), (ao_ref, bo_ref)

        def slot(t, px, py, pc):
            return outs[t].at[4 * px + 2 * py + pc]

        def copy(t, k, block, to, src=None):
            dst = slot(t, *block)
            return pltpu.make_async_remote_copy(
                src_ref=dst if src is None else src, dst_ref=dst, send_sem=send_sems.at[t, k],
                recv_sem=recv_sems.at[t, k], device_id=to, device_id_type=MESH)

        mine = [pltpu.make_async_copy(srcs[t], slot(t, *me), local_sems.at[t]) for t in range(2)]
        for cp in mine:
            cp.start()
        first = []
        for t in range(2):
            first.append(copy(t, 0, me, sibling, src=srcs[t]))
            first += [copy(t, 1 + j, me, (*chip, c), src=srcs[t]) for j, chip in enumerate(chips)]
        for cp in first:
            cp.start()
        passed = []
        for t in range(2):
            for j, chip in enumerate(chips):
                copy(t, 1 + j, (*chip, c), me).wait_recv()
                fwd = copy(t, 4 + j, (*chip, c), sibling)
                fwd.start()
                passed.append(fwd)
        for t in range(2):
            copy(t, 0, sibling, me).wait_recv()
            for j, chip in enumerate(chips):
                copy(t, 4 + j, (*chip, 1 - c), me).wait_recv()
        for cp in first + passed:
            cp.wait_send()
        for cp in mine:
            cp.wait()

    return pl.pallas_call(
        body, name="all_gather_weights",
        out_shape=(jax.ShapeDtypeStruct((N_DEV,) + a.shape, a.dtype), jax.ShapeDtypeStruct((N_DEV,) + b.shape, b.dtype)),
        in_specs=[ANY, ANY], out_specs=(ANY, ANY),
        scratch_shapes=[pltpu.SemaphoreType.DMA((2, 7)), pltpu.SemaphoreType.DMA((2, 7)), pltpu.SemaphoreType.DMA((2,))],
    )(a, b)


def _exchange_sibling(pa, pb):
    def body(pa_ref, pb_ref, ra_ref, rb_ref, send_sems, recv_sems):
        x, y, c = _position()
        sibling = (x, y, 1 - c)
        copies = []
        for t, (p_ref, r_ref) in enumerate(((pa_ref, ra_ref), (pb_ref, rb_ref))):
            for j in range(4):
                copies.append(pltpu.make_async_remote_copy(
                    src_ref=p_ref.at[2 * j + 1 - c], dst_ref=r_ref.at[j], send_sem=send_sems.at[t, j],
                    recv_sem=recv_sems.at[t, j], device_id=sibling, device_id_type=MESH))
        for cp in copies:
            cp.start()
        for cp in copies:
            cp.wait_recv()
        for cp in copies:
            cp.wait_send()

    return pl.pallas_call(
        body, name="exchange_sibling",
        out_shape=(jax.ShapeDtypeStruct((4,) + pa.shape[1:], pa.dtype), jax.ShapeDtypeStruct((4,) + pb.shape[1:], pb.dtype)),
        in_specs=[ANY, ANY], out_specs=(ANY, ANY),
        scratch_shapes=[pltpu.SemaphoreType.DMA((2, 4)), pltpu.SemaphoreType.DMA((2, 4))],
    )(pa, pb)


def _exchange_chips(ta, tb):
    def body(ta_ref, tb_ref, ra_ref, rb_ref, send_sems, recv_sems, local_sems):
        x, y, c = _position()
        chips = [(1 - x, y), (x, 1 - y), (1 - x, 1 - y)]
        my_chip = 2 * x + y
        copies, own = [], []
        for t, (t_ref, r_ref) in enumerate(((ta_ref, ra_ref), (tb_ref, rb_ref))):
            own.append(pltpu.make_async_copy(t_ref.at[my_chip], r_ref.at[my_chip], local_sems.at[t]))
            for j, (px, py) in enumerate(chips):
                copies.append(pltpu.make_async_remote_copy(
                    src_ref=t_ref.at[2 * px + py], dst_ref=r_ref.at[my_chip], send_sem=send_sems.at[t, j],
                    recv_sem=recv_sems.at[t, j], device_id=(px, py, c), device_id_type=MESH))
        for cp in own + copies:
            cp.start()
        for cp in copies:
            cp.wait_recv()
        for cp in copies:
            cp.wait_send()
        for cp in own:
            cp.wait()

    return pl.pallas_call(
        body, name="exchange_chips",
        out_shape=(jax.ShapeDtypeStruct(ta.shape, ta.dtype), jax.ShapeDtypeStruct(tb.shape, tb.dtype)),
        in_specs=[ANY, ANY], out_specs=(ANY, ANY),
        scratch_shapes=[pltpu.SemaphoreType.DMA((2, 3)), pltpu.SemaphoreType.DMA((2, 3)), pltpu.SemaphoreType.DMA((2,))],
    )(ta, tb)


SMALL_ROWS = 8
ROW_LB = 4
ROW_GN = 6
ROW_LOSS = 7


def _small_allreduce_adamw(part, w, m, v, lb_logits):
    width = part.shape[1]

    def body(p_ref, w_ref, m_ref, v_ref, lb_ref, g_ref, d_ref, mo_ref, vo_ref, buf, send_sems, recv_sems):
        x, y, c = _position()
        me = 4 * x + 2 * y + c
        buf[me] = p_ref[...]
        copies = []
        for k in range(N_DEV - 1):
            bx, by, bc = ((k + 1) >> 2) & 1, ((k + 1) >> 1) & 1, (k + 1) & 1
            peer = (x ^ bx, y ^ by, c ^ bc)
            copies.append(pltpu.make_async_remote_copy(
                src_ref=p_ref, dst_ref=buf.at[me], send_sem=send_sems.at[k], recv_sem=recv_sems.at[k],
                device_id=peer, device_id_type=MESH))
        for cp in copies:
            cp.start()
        for cp in copies:
            cp.wait_recv()
        for cp in copies:
            cp.wait_send()
        tot = buf[0]
        for dev in range(1, N_DEV):
            tot = tot + buf[dev]
        lbv = lb_ref[...]
        lb = _sigmoid(lbv[0:1] - lbv[1:2])
        glb = tot[ROW_LB:ROW_LB + 1] * lb * (1.0 - lb)
        loss = jnp.sum(tot[ROW_LOSS:ROW_LOSS + 1], axis=-1, keepdims=True)
        row = lax.broadcasted_iota(jnp.int32, (SMALL_ROWS, width), 0)
        g = jnp.where(row == ROW_LB, glb, jnp.where(row == ROW_LB + 1, -glb, tot))
        g = jnp.where(row == ROW_LOSS, loss, g)
        g_ref[...] = g
        d_ref[...], mo_ref[...], vo_ref[...] = _adamw(w_ref[...], g, m_ref[...], v_ref[...])

    vm = pl.BlockSpec(memory_space=pltpu.VMEM)
    shp = jax.ShapeDtypeStruct((SMALL_ROWS, width), F32)
    return pl.pallas_call(
        body, name="small_allreduce_adamw", out_shape=(shp, shp, shp, shp),
        in_specs=[vm] * 5, out_specs=(vm, vm, vm, vm),
        scratch_shapes=[pltpu.VMEM((N_DEV, SMALL_ROWS, width), F32), pltpu.SemaphoreType.DMA((N_DEV - 1,)),
                        pltpu.SemaphoreType.DMA((N_DEV - 1,))],
    )(part, w, m, v, lb_logits)


def _pack_small(norm_gain, final_gain, lb2, gnorm, last_row, width):
    pad = lambda a: jnp.pad(a.reshape(1, -1), ((0, 0), (0, width - a.size)))
    return jnp.concatenate([norm_gain.reshape(2, width), final_gain.reshape(2, width), lb2.reshape(2, width),
                            pad(gnorm), last_row.reshape(1, width)], axis=0)


def _unpack_small(p, d, e, hd):
    return (p[0:2].reshape(1, d), p[2:4].reshape(d), p[4:6].reshape(2, e), p[6:7, :hd].reshape(1, hd))


def kernel(x, norm_gain, w_in, lb_logits, hgrn_gnorm, w_out, final_gain, loss_target, m_norm_gain, m_w_in, m_lb_logits, m_hgrn_gnorm, m_w_out, m_final_gain, v_norm_gain, v_w_in, v_lb_logits, v_hgrn_gnorm, v_w_out, v_final_gain):
    s, d = x.shape[1], x.shape[2]
    e = w_in.shape[2]
    assert d == 2 * e and lb_logits.shape == (2, e) and w_out.shape[1] * N_DEV == 2 * e
    x2d = x.reshape(s, d)
    tgt = loss_target.reshape(s, d)

    w_in_full, w_out_full = _all_gather_weights(_cast_bf16(w_in[0]), _cast_bf16(w_out[0]))
    w_out_full = w_out_full.reshape(2 * e, d)

    h = _rmsnorm_fwd(x2d, norm_gain)
    z, zb = _inproj(h, w_in_full)
    y_h, states = _hgrn_fwd(z, lb_logits, hgrn_gnorm)
    pats = [_attn_fwd(zb, dil) for dil in DILATIONS]
    o_attn, lse, y_a = _attn_merge([p[0] for p in pats], [p[1] for p in pats], z)
    dx2, dx2b, dy, loss_vec, dfg = _outproj_loss(x2d, y_h, y_a, w_out_full, final_gain.reshape(1, d), tgt)

    pwo = _dwout(y_h, y_a, dx2b)
    do_attn, dzg = _attn_gate_bwd(dy, o_attn, z)
    acc = None
    for n, dil in enumerate(DILATIONS):
        last = n == len(DILATIONS) - 1
        acc = _attn_bwd(zb, do_attn, o_attn, lse, dil, acc, BF16 if last else F32)
    dzh, dlb, dgn = _hgrn_bwd(z, dy, states, lb_logits, hgrn_gnorm)
    grad_x, dng = _dh_dx(dzh, acc, dzg, w_in_full, x2d, norm_gain, dx2)
    pwi = _dwin(h, dzh, acc, dzg)

    c_idx = lax.axis_index("c").astype(jnp.int32).reshape(1)
    pwo8 = pwo.reshape(N_DEV, w_out.shape[1], d)
    ra_i, ra_o = _exchange_sibling(pwi, pwo8)
    rb_i, rb_o = _exchange_chips(_pair_add(pwi, ra_i, c_idx), _pair_add(pwo8, ra_o, c_idx))
    g_wi, d_wi, nm_wi, nv_wi = _sum_adamw(rb_i, w_in[0], m_w_in[0], v_w_in[0])
    g_wo, d_wo, nm_wo, nv_wo = _sum_adamw(rb_o, w_out[0], m_w_out[0], v_w_out[0])

    width = d // 2
    zero_row = jnp.zeros((1, width), F32)
    loss_row = loss_vec[:, :width] + loss_vec[:, width:]
    part = _pack_small(dng, dfg, jnp.concatenate([dlb, zero_row], axis=0), dgn, loss_row, width)
    pw = _pack_small(norm_gain, final_gain, lb_logits, hgrn_gnorm, zero_row, width)
    pm = _pack_small(m_norm_gain, m_final_gain, m_lb_logits, m_hgrn_gnorm, zero_row, width)
    pv = _pack_small(v_norm_gain, v_final_gain, v_lb_logits, v_hgrn_gnorm, zero_row, width)
    sg, sd, sm, sv = _small_allreduce_adamw(part, pw, pm, pv, lb_logits)
    hd = hgrn_gnorm.shape[1]
    g_ng, g_fg, g_lb, g_gn = _unpack_small(sg, d, e, hd)
    d_ng, d_fg, d_lb, d_gn = _unpack_small(sd, d, e, hd)
    m_ng, m_fg, m_lb, m_gn = _unpack_small(sm, d, e, hd)
    v_ng, v_fg, v_lb, v_gn = _unpack_small(sv, d, e, hd)
    loss = sg[ROW_LOSS, 0]

    one = lambda a: a[None]
    return (loss, grad_x.reshape(1, s, d), g_ng, one(g_wi), g_lb, g_gn, one(g_wo), g_fg,
            d_ng, one(d_wi), d_lb, d_gn, one(d_wo), d_fg,
            m_ng, one(nm_wi), m_lb, m_gn, one(nm_wo), m_fg,
            v_ng, one(nv_wi), v_lb, v_gn, one(nv_wo), v_fg)
```

```python
import functools
import math

import jax
import jax.numpy as jnp
from jax import lax
from jax.experimental import pallas as pl
from jax.experimental.pallas import tpu as pltpu

NORM_EPS = 1e-6
HGRN_HEAD = 128
HGRN_CHUNK = 64
ATTN_HEAD = 64
ATTN_BAND = 128
DILATIONS = (1, 4, 16)
N_SPLITS = 8
N_DEV = 8
ADAM_LR = 0.001
ADAM_B1 = 0.9
ADAM_B2 = 0.999
ADAM_EPS = 1e-08
ADAM_WD = 0.01
ADAM_STEP = 10
LANES = 128
MESH = pl.DeviceIdType.MESH
F32 = jnp.float32
BF16 = jnp.bfloat16
NEG_BIG = -1e30
VMEM_LIMIT = 56 * 1024 * 1024

ANY = pl.BlockSpec(memory_space=pl.ANY)


def _params(*sem):
    return pltpu.CompilerParams(dimension_semantics=sem, vmem_limit_bytes=VMEM_LIMIT)


def _tile(n, pref):
    t = min(n, pref)
    assert n % t == 0, (n, pref)
    return t


def _dot(a, b, precision=None):
    return jnp.dot(a, b, preferred_element_type=F32, precision=precision)


def _dot_nt(a, b):
    return lax.dot_general(a, b, (((1,), (1,)), ((), ())), preferred_element_type=F32)


def _dot_tn(a, b):
    return lax.dot_general(a, b, (((0,), (0,)), ((), ())), preferred_element_type=F32)


def _sigmoid(x):
    return 1.0 / (1.0 + jnp.exp(-x))


def _dsilu(x, s):
    return s * (1.0 + x * (1.0 - s))


def _adamw(w, g, m, v):
    m = ADAM_B1 * m + (1.0 - ADAM_B1) * g
    v = ADAM_B2 * v + (1.0 - ADAM_B2) * (g * g)
    m_hat = m / (1.0 - ADAM_B1 ** ADAM_STEP)
    v_hat = v / (1.0 - ADAM_B2 ** ADAM_STEP)
    delta = -ADAM_LR * (m_hat / (jnp.sqrt(v_hat) + ADAM_EPS) + ADAM_WD * w)
    return delta, m, v


def _cast_bf16(a):
    r, c = a.shape
    tr = _tile(r, 256)

    def body(a_ref, o_ref):
        o_ref[...] = a_ref[...].astype(BF16)

    return pl.pallas_call(
        body, name="cast_bf16", grid=(r // tr,), out_shape=jax.ShapeDtypeStruct((r, c), BF16),
        in_specs=[pl.BlockSpec((tr, c), lambda i: (i, 0))], out_specs=pl.BlockSpec((tr, c), lambda i: (i, 0)),
        compiler_params=_params("parallel"))(a)


def _rmsnorm_fwd(x, gain):
    s, d = x.shape
    tm = _tile(s, 512)

    def body(x_ref, g_ref, h_ref):
        xv = x_ref[...]
        r = lax.rsqrt(jnp.mean(xv * xv, axis=-1, keepdims=True) + NORM_EPS)
        h_ref[...] = (xv * r * g_ref[...]).astype(BF16)

    return pl.pallas_call(
        body, name="rmsnorm_fwd", grid=(s // tm,), out_shape=jax.ShapeDtypeStruct((s, d), BF16),
        in_specs=[pl.BlockSpec((tm, d), lambda i: (i, 0)), pl.BlockSpec((1, d), lambda i: (0, 0))],
        out_specs=pl.BlockSpec((tm, d), lambda i: (i, 0)), compiler_params=_params("parallel"))(x, gain)


def _inproj(h, w_full):
    s, d = h.shape
    e = w_full.shape[2]
    tm = _tile(s, 512)

    def body(h_ref, w_ref, z_ref):
        z_ref[...] = _dot(h_ref[...], w_ref[...])

    return pl.pallas_call(
        body, name="inproj", grid=(s // tm, N_SPLITS),
        out_shape=jax.ShapeDtypeStruct((s, N_SPLITS * e), F32),
        in_specs=[pl.BlockSpec((tm, d), lambda i, j: (i, 0)), pl.BlockSpec((None, d, e), lambda i, j: (j, 0, 0))],
        out_specs=pl.BlockSpec((tm, e), lambda i, j: (i, j)),
        compiler_params=_params("parallel", "arbitrary"))(h, w_full)


def _chunk_masks(tb):
    row = lax.broadcasted_iota(jnp.int32, (tb, tb), 0)
    col = lax.broadcasted_iota(jnp.int32, (tb, tb), 1)
    same = (row // HGRN_CHUNK) == (col // HGRN_CHUNK)
    lower = jnp.where(same & (col <= row), 1.0, 0.0).astype(F32)
    upper = jnp.where(same & (col >= row), 1.0, 0.0).astype(F32)
    return lower, upper


def _hgrn_gates(qp, fp, lbv):
    lb = _sigmoid(lbv[0:1] - lbv[1:2])
    sq = _sigmoid(qp)
    q = qp * sq
    sg = _sigmoid(fp)
    f = lb + (1.0 - lb) * sg
    k = 1.0 - f
    return lb, sq, q, sg, f, k


def _hgrn_fwd(z, lb_logits, gnorm):
    s = z.shape[0]
    e = z.shape[1] // N_SPLITS
    nh = e // HGRN_HEAD
    tb = _tile(s, 256)
    nc = tb // HGRN_CHUNK
    nb = s // tb
    C = HGRN_CHUNK

    def body(q_ref, f_ref, i_ref, g_ref, lb_ref, gn_ref, y_ref, st_ref, state, o_scr):
        @pl.when(pl.program_id(1) == 0)
        def _():
            state[...] = jnp.zeros_like(state)

        lb, sq, q, sg, f, k = _hgrn_gates(q_ref[...], f_ref[...], lb_ref[...])
        lower, _ = _chunk_masks(tb)
        b = _dot(lower, jnp.log(f), precision=lax.Precision.HIGHEST)
        b3 = b.reshape(nc, C, HGRN_HEAD)
        bc = b3[:, C - 1:C, :]
        qt = (q * jnp.exp(b)).astype(BF16)
        kt = (k * jnp.exp(-b)).astype(BF16)
        ke = (k.reshape(nc, C, HGRN_HEAD) * jnp.exp(bc - b3)).reshape(tb, HGRN_HEAD).astype(BF16)
        v = i_ref[...].astype(BF16)
        tri = lax.broadcasted_iota(jnp.int32, (C, C), 1) <= lax.broadcasted_iota(jnp.int32, (C, C), 0)
        for c in range(nc):
            sl = slice(c * C, (c + 1) * C)
            st = state[...]
            st_ref[c] = st
            a = jnp.where(tri, _dot_nt(qt[sl], kt[sl]), 0.0)
            o_scr[sl, :] = _dot(a.astype(BF16), v[sl]) + _dot_nt(qt[sl], st.astype(BF16))
            state[...] = st * jnp.exp(bc[c]) + _dot_tn(v[sl], ke[sl])
        o = o_scr[...]
        rms = lax.rsqrt(jnp.mean(o * o, axis=-1, keepdims=True) + NORM_EPS)
        gp = g_ref[...]
        y_ref[...] = (o * rms * gn_ref[...] * (gp * _sigmoid(gp))).astype(BF16)

    col = lambda kk: (lambda h, n: (n, kk * nh + h))
    return pl.pallas_call(
        body, name="hgrn_fwd", grid=(nh, nb),
        out_shape=(jax.ShapeDtypeStruct((s, e), BF16),
                   jax.ShapeDtypeStruct((nh, s // C, HGRN_HEAD, HGRN_HEAD), F32)),
        in_specs=[pl.BlockSpec((tb, HGRN_HEAD), col(0)), pl.BlockSpec((tb, HGRN_HEAD), col(1)),
                  pl.BlockSpec((tb, HGRN_HEAD), col(2)), pl.BlockSpec((tb, HGRN_HEAD), col(3)),
                  pl.BlockSpec((2, HGRN_HEAD), lambda h, n: (0, h)), pl.BlockSpec((1, HGRN_HEAD), lambda h, n: (0, 0))],
        out_specs=(pl.BlockSpec((tb, HGRN_HEAD), lambda h, n: (n, h)),
                   pl.BlockSpec((None, nc, HGRN_HEAD, HGRN_HEAD), lambda h, n: (h, n, 0, 0))),
        scratch_shapes=[pltpu.VMEM((HGRN_HEAD, HGRN_HEAD), F32), pltpu.VMEM((tb, HGRN_HEAD), F32)],
        compiler_params=_params("parallel", "arbitrary"))(z, z, z, z, lb_logits, gnorm)


def _hgrn_bwd(z, dy, states, lb_logits, gnorm):
    s = z.shape[0]
    e = z.shape[1] // N_SPLITS
    nh = e // HGRN_HEAD
    tb = _tile(s, 256)
    nc = tb // HGRN_CHUNK
    nb = s // tb
    C = HGRN_CHUNK
    H = HGRN_HEAD

    def body(q_ref, f_ref, i_ref, g_ref, dy_ref, st_ref, lb_ref, gn_ref, dz_ref, dlb_ref, dgn_ref,
             gstate, o_scr, a_scr, dq_scr, dk_scr, dv_scr, e_scr):
        first = (pl.program_id(0) == 0) & (pl.program_id(1) == 0)

        @pl.when(first)
        def _():
            dgn_ref[...] = jnp.zeros_like(dgn_ref)

        @pl.when(pl.program_id(1) == 0)
        def _():
            gstate[...] = jnp.zeros_like(gstate)
            dlb_ref[...] = jnp.zeros_like(dlb_ref)

        qp = q_ref[...]
        lb, sq, q, sg, f, k = _hgrn_gates(qp, f_ref[...], lb_ref[...])
        lower, upper = _chunk_masks(tb)
        b = _dot(lower, jnp.log(f), precision=lax.Precision.HIGHEST)
        b3 = b.reshape(nc, C, H)
        bc = b3[:, C - 1:C, :]
        eb = jnp.exp(b)
        enb = jnp.exp(-b)
        eend = jnp.exp(bc - b3).reshape(tb, H)
        qt = (q * eb).astype(BF16)
        kt = (k * enb).astype(BF16)
        ke = (k * eend).astype(BF16)
        v = i_ref[...].astype(BF16)
        tri = lax.broadcasted_iota(jnp.int32, (C, C), 1) <= lax.broadcasted_iota(jnp.int32, (C, C), 0)
        for c in range(nc):
            sl = slice(c * C, (c + 1) * C)
            a = jnp.where(tri, _dot_nt(qt[sl], kt[sl]), 0.0)
            a_scr[sl, :] = a
            o_scr[sl, :] = _dot(a.astype(BF16), v[sl]) + _dot_nt(qt[sl], st_ref[c].astype(BF16))
        o = o_scr[...]
        rms = lax.rsqrt(jnp.mean(o * o, axis=-1, keepdims=True) + NORM_EPS)
        on = o * rms
        gn = gn_ref[...]
        gp = g_ref[...]
        sgg = _sigmoid(gp)
        dyv = dy_ref[...]
        d_on = dyv * (gp * sgg)
        dz_ref[3] = (dyv * on * gn * _dsilu(gp, sgg)).astype(BF16)
        dgn_ref[...] += jnp.sum(d_on * on, axis=0, keepdims=True)
        u = d_on * gn
        do = (rms * (u - on * jnp.mean(u * on, axis=-1, keepdims=True))).astype(BF16)
        for c in reversed(range(nc)):
            sl = slice(c * C, (c + 1) * C)
            stp = st_ref[c]
            gt = gstate[...]
            gtb = gt.astype(BF16)
            dec = jnp.exp(bc[c])
            ab = a_scr[sl, :].astype(BF16)
            da = jnp.where(tri, _dot_nt(do[sl], v[sl]), 0.0).astype(BF16)
            dqt = _dot(da, kt[sl]) + _dot(do[sl], stp.astype(BF16))
            dkt = _dot_tn(da, qt[sl])
            dks = _dot(v[sl], gtb) * eend[sl]
            dv_scr[sl, :] = _dot_tn(ab, do[sl]) + _dot_nt(ke[sl], gtb)
            dq_scr[sl, :] = dqt * eb[sl]
            dk_scr[sl, :] = dkt * enb[sl] + dks
            ech = jnp.sum(k[sl] * dks, axis=0, keepdims=True) + jnp.sum(gt * dec * stp, axis=0, keepdims=True)
            e_scr[sl, :] = jnp.broadcast_to(ech, (C, H))
            gstate[...] = gt * dec + _dot_tn(do[sl], qt[sl])
        dq = dq_scr[...]
        dk = dk_scr[...]
        dlf = _dot(upper, q * dq - k * dk, precision=lax.Precision.HIGHEST) + e_scr[...]
        dft = dlf / f - dk
        dz_ref[0] = (dq * _dsilu(qp, sq)).astype(BF16)
        dz_ref[1] = (dft * (1.0 - lb) * sg * (1.0 - sg)).astype(BF16)
        dz_ref[2] = dv_scr[...].astype(BF16)
        dlb_ref[...] += jnp.sum(dft * (1.0 - sg), axis=0, keepdims=True)

    col = lambda kk: (lambda h, n: (nb - 1 - n, kk * nh + h))
    return pl.pallas_call(
        body, name="hgrn_bwd", grid=(nh, nb),
        out_shape=(jax.ShapeDtypeStruct((4, s, e), BF16), jax.ShapeDtypeStruct((1, e), F32),
                   jax.ShapeDtypeStruct((1, H), F32)),
        in_specs=[pl.BlockSpec((tb, H), col(0)), pl.BlockSpec((tb, H), col(1)),
                  pl.BlockSpec((tb, H), col(2)), pl.BlockSpec((tb, H), col(3)),
                  pl.BlockSpec((tb, H), lambda h, n: (nb - 1 - n, h)),
                  pl.BlockSpec((None, nc, H, H), lambda h, n: (h, nb - 1 - n, 0, 0)),
                  pl.BlockSpec((2, H), lambda h, n: (0, h)), pl.BlockSpec((1, H), lambda h, n: (0, 0))],
        out_specs=(pl.BlockSpec((4, tb, H), lambda h, n: (0, nb - 1 - n, h)),
                   pl.BlockSpec((1, H), lambda h, n: (0, h)), pl.BlockSpec((1, H), lambda h, n: (0, 0))),
        scratch_shapes=[pltpu.VMEM((H, H), F32)] + [pltpu.VMEM((tb, H), F32)] + [pltpu.VMEM((tb, C), F32)]
        + [pltpu.VMEM((tb, H), F32)] * 4,
        compiler_params=_params("arbitrary", "arbitrary"))(z, z, z, z, dy, states, lb_logits, gnorm)


ATTN_T = 16 * ATTN_BAND
SCALE = ATTN_HEAD ** -0.5


def _slope(hh, nheads):
    head = (2 * pl.program_id(0) + hh + 1).astype(F32)
    return jnp.exp(jnp.full((1, 1), -8.0 / nheads * math.log(2.0), F32) * head)


def _fill_bias(bias, nheads, delta, edge_ok):
    band = (delta >= 0) & (delta <= ATTN_BAND)
    dist = delta.astype(F32)
    for pi, dil in enumerate(DILATIONS):
        for hh in range(2):
            full = jnp.where(band, -(_slope(hh, nheads) * float(dil)) * dist, NEG_BIG)
            bias[(pi * 2 + hh) * 2] = full
            bias[(pi * 2 + hh) * 2 + 1] = jnp.where(edge_ok, full, NEG_BIG)


def _rows(start, size, stride):
    if stride == 1:
        return pl.ds(pl.multiple_of(start, ATTN_BAND), size)
    return pl.ds(start, size, stride=stride)


def _pair(a, b):
    return jnp.concatenate([a, b], axis=1)


def _attn_fwd(z):
    s = z.shape[0]
    e = z.shape[1] // N_SPLITS
    npair = e // LANES
    T = ATTN_T
    assert s % T == 0
    nsb = s // T
    W = ATTN_BAND
    nt = T // W
    HD = ATTN_HEAD
    chunk = 256

    def body(q_ref, kp_ref, kc_ref, vp_ref, vc_ref, g_ref, o_ref, l_ref, y_ref, qs, kbuf, vbuf, bias, accs, ms, ls):
        sb = pl.program_id(1)
        qs[...] = q_ref[...] * SCALE
        kbuf[0:T, :] = kp_ref[...]
        kbuf[T:, :] = kc_ref[...]
        vbuf[0:T, :] = vp_ref[...]
        vbuf[T:, :] = vc_ref[...]
        qi = lax.broadcasted_iota(jnp.int32, (W, 2 * W), 0)
        kj = lax.broadcasted_iota(jnp.int32, (W, 2 * W), 1)
        _fill_bias(bias, 2 * npair, W + qi - kj, kj >= W)

        def tile(tau, carry):
            for pi, dil in enumerate(DILATIONS):
                r = tau % dil
                ub = tau // dil
                qrows = _rows(r + dil * W * ub, W, dil)
                krows = _rows(T + dil * W * (ub - 1) + r, 2 * W, dil)
                var = jnp.where((sb == 0) & (ub == 0), 1, 0)
                qt = qs[qrows, :].astype(BF16)
                kt = kbuf[krows, :].astype(BF16)
                vt = vbuf[krows, :].astype(BF16)
                res = []
                for hh in range(2):
                    cs = slice(hh * HD, (hh + 1) * HD)
                    sc = _dot_nt(qt[:, cs], kt[:, cs]) + bias[(pi * 2 + hh) * 2 + var]
                    m = jnp.max(sc, axis=-1, keepdims=True)
                    p = jnp.exp(sc - m)
                    l = jnp.sum(p, axis=-1, keepdims=True)
                    res.append((_dot(p.astype(BF16), vt[:, cs]), jnp.broadcast_to(m, (W, HD)), jnp.broadcast_to(l, (W, HD))))
                accs[pi, qrows, :] = _pair(res[0][0], res[1][0])
                ms[pi, qrows, :] = _pair(res[0][1], res[1][1])
                ls[pi, qrows, :] = _pair(res[0][2], res[1][2])
            return carry

        lax.fori_loop(0, nt, tile, 0)

        def merge(i, carry):
            rows = pl.ds(pl.multiple_of(i * chunk, chunk), chunk)
            m1, m2, m3 = ms[0, rows, :], ms[1, rows, :], ms[2, rows, :]
            mx = jnp.maximum(jnp.maximum(m1, m2), m3)
            w1, w2, w3 = jnp.exp(m1 - mx), jnp.exp(m2 - mx), jnp.exp(m3 - mx)
            den = w1 * ls[0, rows, :] + w2 * ls[1, rows, :] + w3 * ls[2, rows, :]
            o = (w1 * accs[0, rows, :] + w2 * accs[1, rows, :] + w3 * accs[2, rows, :]) / den
            o_ref[rows, :] = o
            l_ref[rows, :] = mx + jnp.log(den)
            gp = g_ref[rows, :]
            y_ref[rows, :] = (o * (gp * _sigmoid(gp))).astype(BF16)
            return carry

        lax.fori_loop(0, T // chunk, merge, 0)

    cur = lambda split: (lambda hp, sb: (sb, split * npair + hp))
    prev = lambda split: (lambda hp, sb: (jnp.maximum(sb - 1, 0), split * npair + hp))
    blk = lambda index: pl.BlockSpec((T, LANES), index)
    out = blk(lambda hp, sb: (sb, hp))
    buf = lambda rows: pltpu.VMEM((rows, LANES), F32)
    return pl.pallas_call(
        body, name="attn_fwd", grid=(npair, nsb),
        out_shape=(jax.ShapeDtypeStruct((s, e), F32), jax.ShapeDtypeStruct((s, e), F32), jax.ShapeDtypeStruct((s, e), BF16)),
        in_specs=[blk(cur(4)), blk(prev(5)), blk(cur(5)), blk(prev(6)), blk(cur(6)), blk(cur(7))],
        out_specs=(out, out, out),
        scratch_shapes=[buf(T), buf(2 * T), buf(2 * T), pltpu.VMEM((12, W, 2 * W), F32),
                        pltpu.VMEM((3, T, LANES), F32), pltpu.VMEM((3, T, LANES), F32), pltpu.VMEM((3, T, LANES), F32)],
        compiler_params=_params("parallel", "arbitrary"))(z, z, z, z, z, z)


def _outproj_loss(x, y_h, y_a, w_out_full, final_gain, target):
    s, d = x.shape
    e = y_h.shape[1]
    tm = _tile(s, 256)

    def body(x_ref, yh_ref, ya_ref, w_ref, g_ref, t_ref, dx_ref, dxb_ref, dy_ref, loss_ref, dg_ref):
        @pl.when(pl.program_id(0) == 0)
        def _():
            loss_ref[...] = jnp.zeros_like(loss_ref)
            dg_ref[...] = jnp.zeros_like(dg_ref)

        w = w_ref[...]
        x2 = x_ref[...] + _dot(yh_ref[...], w[0:e]) + _dot(ya_ref[...], w[e:2 * e])
        r = lax.rsqrt(jnp.mean(x2 * x2, axis=-1, keepdims=True) + NORM_EPS)
        xn = x2 * r
        g = g_ref[...]
        err = xn * g - t_ref[...]
        loss_ref[...] += jnp.sum(err * err, axis=0, keepdims=True) * (0.5 / d)
        dyo = err * (1.0 / d)
        dg_ref[...] += jnp.sum(dyo * xn, axis=0, keepdims=True)
        u = dyo * g
        dx2 = r * (u - xn * jnp.mean(u * xn, axis=-1, keepdims=True))
        dx_ref[...] = dx2
        dxb = dx2.astype(BF16)
        dxb_ref[...] = dxb
        dy_ref[...] = _dot_nt(dxb, w)

    row = pl.BlockSpec((tm, d), lambda i: (i, 0))
    half = pl.BlockSpec((tm, e), lambda i: (i, 0))
    vec = pl.BlockSpec((1, d), lambda i: (0, 0))
    return pl.pallas_call(
        body, name="outproj_loss", grid=(s // tm,),
        out_shape=(jax.ShapeDtypeStruct((s, d), F32), jax.ShapeDtypeStruct((s, d), BF16),
                   jax.ShapeDtypeStruct((s, 2 * e), F32), jax.ShapeDtypeStruct((1, d), F32),
                   jax.ShapeDtypeStruct((1, d), F32)),
        in_specs=[row, half, half, pl.BlockSpec((2 * e, d), lambda i: (0, 0)), vec, row],
        out_specs=(row, row, pl.BlockSpec((tm, 2 * e), lambda i: (i, 0)), vec, vec),
        compiler_params=_params("arbitrary"))(x, y_h, y_a, w_out_full, final_gain, target)


def _dwout(y_h, y_a, dxb):
    s, e = y_h.shape
    d = dxb.shape[1]
    ts = _tile(s, 512)
    ns = s // ts

    def body(yh_ref, ya_ref, dx_ref, o_ref, acc):
        half = pl.program_id(0)
        step = pl.program_id(1)

        @pl.when(step == 0)
        def _():
            acc[...] = jnp.zeros_like(acc)

        @pl.when(half == 0)
        def _():
            acc[...] += _dot_tn(yh_ref[...], dx_ref[...])

        @pl.when(half == 1)
        def _():
            acc[...] += _dot_tn(ya_ref[...], dx_ref[...])

        @pl.when(step == ns - 1)
        def _():
            o_ref[...] = acc[...].astype(BF16)

    return pl.pallas_call(
        body, name="dwout", grid=(2, ns), out_shape=jax.ShapeDtypeStruct((2 * e, d), BF16),
        in_specs=[pl.BlockSpec((ts, e), lambda hf, k: (k * (1 - hf), 0)), pl.BlockSpec((ts, e), lambda hf, k: (k * hf, 0)),
                  pl.BlockSpec((ts, d), lambda hf, k: (k, 0))],
        out_specs=pl.BlockSpec((e, d), lambda hf, k: (hf, 0)),
        scratch_shapes=[pltpu.VMEM((e, d), F32)],
        compiler_params=_params("parallel", "arbitrary"))(y_h, y_a, dxb)


def _attn_bwd(z, dy, o, lse):
    s, e = o.shape
    npair = e // LANES
    T = ATTN_T
    assert s % T == 0
    nsb = s // T
    W = ATTN_BAND
    nt = T // W
    HD = ATTN_HEAD
    chunk = 256

    def body(k_ref, v_ref, qc_ref, qn_ref, dyc_ref, dyn_ref, gc_ref, gn_ref, oc_ref, on_ref, lc_ref, ln_ref,
             dz_ref, qbuf, dobuf, dlbuf, lbuf, dqacc, dkacc, dvacc, bias):
        sb = pl.program_id(1)
        halves = ((qc_ref, dyc_ref, gc_ref, oc_ref, lc_ref), (qn_ref, dyn_ref, gn_ref, on_ref, ln_ref))
        for half, (q_r, dy_r, g_r, o_r, l_r) in enumerate(halves):
            def stage(i, carry):
                rows = pl.ds(pl.multiple_of(i * chunk, chunk), chunk)
                dst = pl.ds(pl.multiple_of(half * T + i * chunk, chunk), chunk)
                gp = g_r[rows, :]
                sg = _sigmoid(gp)
                dyv = dy_r[rows, :]
                ov = o_r[rows, :]
                dov = dyv * (gp * sg)
                prod = dov * ov
                tot = [jnp.broadcast_to(jnp.sum(prod[:, hh * HD:(hh + 1) * HD], axis=-1, keepdims=True), (chunk, HD))
                       for hh in range(2)]
                qbuf[dst, :] = q_r[rows, :] * SCALE
                dobuf[dst, :] = dov
                dlbuf[dst, :] = _pair(tot[0], tot[1])
                lbuf[dst, :] = l_r[rows, :]
                if half == 0:
                    dz_ref[3, rows, :] = (dyv * ov * _dsilu(gp, sg)).astype(BF16)
                return carry

            lax.fori_loop(0, T // chunk, stage, 0)

        @pl.when(sb == 0)
        def _():
            dqacc[0:T, :] = jnp.zeros((T, LANES), F32)

        dqacc[T:, :] = jnp.zeros((T, LANES), F32)
        dkacc[...] = jnp.zeros_like(dkacc)
        dvacc[...] = jnp.zeros_like(dvacc)
        qi = lax.broadcasted_iota(jnp.int32, (2 * W, W), 0)
        kj = lax.broadcasted_iota(jnp.int32, (2 * W, W), 1)
        _fill_bias(bias, 2 * npair, qi - kj, qi < W)

        def tile(tau, carry):
            for pi, dil in enumerate(DILATIONS):
                r = tau % dil
                ub = tau // dil
                start = r + dil * W * ub
                krows = _rows(start, W, dil)
                qrows = _rows(start, 2 * W, dil)
                var = jnp.where((sb == nsb - 1) & (ub == nt // dil - 1), 1, 0)
                kt = k_ref[krows, :].astype(BF16)
                vt = v_ref[krows, :].astype(BF16)
                qt = qbuf[qrows, :].astype(BF16)
                dot_ = dobuf[qrows, :].astype(BF16)
                lt = lbuf[qrows, :]
                dlt = dlbuf[qrows, :]
                dks, dvs, dqs = [], [], []
                for hh in range(2):
                    cs = slice(hh * HD, (hh + 1) * HD)
                    sc = _dot_nt(qt[:, cs], kt[:, cs]) + bias[(pi * 2 + hh) * 2 + var] - lt[:, hh * HD:hh * HD + 1]
                    p = jnp.exp(sc)
                    dp = _dot_nt(dot_[:, cs], vt[:, cs])
                    ds = (p * (dp - dlt[:, hh * HD:hh * HD + 1])).astype(BF16)
                    dvs.append(_dot_tn(p.astype(BF16), dot_[:, cs]))
                    dks.append(_dot_tn(ds, qt[:, cs]))
                    dqs.append(_dot(ds, kt[:, cs]) * SCALE)
                dkacc[krows, :] += _pair(dks[0], dks[1])
                dvacc[krows, :] += _pair(dvs[0], dvs[1])
                dqacc[qrows, :] += _pair(dqs[0], dqs[1])
            return carry

        lax.fori_loop(0, nt, tile, 0)

        def flush(i, carry):
            rows = pl.ds(pl.multiple_of(i * chunk, chunk), chunk)
            nxt = pl.ds(pl.multiple_of(T + i * chunk, chunk), chunk)
            dz_ref[0, rows, :] = dqacc[rows, :].astype(BF16)
            dz_ref[1, rows, :] = dkacc[rows, :].astype(BF16)
            dz_ref[2, rows, :] = dvacc[rows, :].astype(BF16)
            dqacc[rows, :] = dqacc[nxt, :]
            return carry

        lax.fori_loop(0, T // chunk, flush, 0)

    zc = lambda split: (lambda hp, sb: (sb, split * npair + hp))
    zn = lambda split: (lambda hp, sb: (jnp.minimum(sb + 1, nsb - 1), split * npair + hp))
    ec = lambda off: (lambda hp, sb: (sb, off + hp))
    en = lambda off: (lambda hp, sb: (jnp.minimum(sb + 1, nsb - 1), off + hp))
    blk = lambda index: pl.BlockSpec((T, LANES), index)
    buf = lambda rows: pltpu.VMEM((rows, LANES), F32)
    return pl.pallas_call(
        body, name="attn_bwd", grid=(npair, nsb), out_shape=jax.ShapeDtypeStruct((4, s, e), BF16),
        in_specs=[blk(zc(5)), blk(zc(6)), blk(zc(4)), blk(zn(4)), blk(ec(npair)), blk(en(npair)),
                  blk(zc(7)), blk(zn(7)), blk(ec(0)), blk(en(0)), blk(ec(0)), blk(en(0))],
        out_specs=pl.BlockSpec((4, T, LANES), lambda hp, sb: (0, sb, hp)),
        scratch_shapes=[buf(2 * T), buf(2 * T), buf(2 * T), buf(2 * T), buf(2 * T), buf(T), buf(T),
                        pltpu.VMEM((12, 2 * W, W), F32)],
        compiler_params=_params("parallel", "arbitrary"))(z, z, z, z, dy, dy, z, z, o, o, lse, lse)


def _dz_specs(tm, e, axis):
    def mk(lo, hi):
        def index(i, k):
            row, grp = (i, k) if axis == 1 else (k, i)
            return (jnp.clip(grp - lo, 0, hi - lo - 1), row, 0)
        return pl.BlockSpec((None, tm, e), index)
    return [mk(0, 4), mk(4, 8)]


def _dz_pick(grp, dzh_ref, dza_ref, fn):
    @pl.when(grp < 4)
    def _():
        fn(dzh_ref[...])

    @pl.when(grp >= 4)
    def _():
        fn(dza_ref[...])


def _dh_dx(dzh, dza, w_full, x, gain, dx2):
    s, d = x.shape
    e = dzh.shape[2]
    tm = _tile(s, 512)

    def body(dzh_ref, dza_ref, w_ref, x_ref, g_ref, dx2_ref, gx_ref, dg_ref, acc):
        i, k = pl.program_id(0), pl.program_id(1)

        @pl.when((i == 0) & (k == 0))
        def _():
            dg_ref[...] = jnp.zeros_like(dg_ref)

        @pl.when(k == 0)
        def _():
            acc[...] = jnp.zeros_like(acc)

        def add(dz):
            acc[...] += _dot_nt(dz, w_ref[...])

        _dz_pick(k, dzh_ref, dza_ref, add)

        @pl.when(k == N_SPLITS - 1)
        def _():
            dh = acc[...]
            xv = x_ref[...]
            r = lax.rsqrt(jnp.mean(xv * xv, axis=-1, keepdims=True) + NORM_EPS)
            xn = xv * r
            dg_ref[...] += jnp.sum(dh * xn, axis=0, keepdims=True)
            u = dh * g_ref[...]
            gx_ref[...] = dx2_ref[...] + r * (u - xn * jnp.mean(u * xn, axis=-1, keepdims=True))

    row = pl.BlockSpec((tm, d), lambda i, k: (i, 0))
    vec = pl.BlockSpec((1, d), lambda i, k: (0, 0))
    return pl.pallas_call(
        body, name="dh_dx", grid=(s // tm, N_SPLITS),
        out_shape=(jax.ShapeDtypeStruct((s, d), F32), jax.ShapeDtypeStruct((1, d), F32)),
        in_specs=_dz_specs(tm, e, 1) + [pl.BlockSpec((None, d, e), lambda i, k: (k, 0, 0)), row, vec, row],
        out_specs=(row, vec), scratch_shapes=[pltpu.VMEM((tm, d), F32)],
        compiler_params=_params("arbitrary", "arbitrary"))(dzh, dza, w_full, x, gain, dx2)


def _dwin(h, dzh, dza):
    s, d = h.shape
    e = dzh.shape[2]
    ts = _tile(s, 512)
    ns = s // ts

    def body(dzh_ref, dza_ref, h_ref, o_ref, acc):
        j, k = pl.program_id(0), pl.program_id(1)

        @pl.when(k == 0)
        def _():
            acc[...] = jnp.zeros_like(acc)

        def add(dz):
            acc[...] += _dot_tn(h_ref[...], dz)

        _dz_pick(j, dzh_ref, dza_ref, add)

        @pl.when(k == ns - 1)
        def _():
            o_ref[...] = acc[...].astype(BF16)

    return pl.pallas_call(
        body, name="dwin", grid=(N_SPLITS, ns), out_shape=jax.ShapeDtypeStruct((N_SPLITS, d, e), BF16),
        in_specs=_dz_specs(ts, e, 0) + [pl.BlockSpec((ts, d), lambda j, k: (k, 0))],
        out_specs=pl.BlockSpec((None, d, e), lambda j, k: (j, 0, 0)),
        scratch_shapes=[pltpu.VMEM((d, e), F32)],
        compiler_params=_params("parallel", "arbitrary"))(dzh, dza, h)


def _pair_add(p, ra, c_idx):
    _, r, c = p.shape
    tr = _tile(r, 256)
    p4 = p.reshape(4, 2, r, c)

    def body(c_ref, p_ref, ra_ref, o_ref):
        o_ref[...] = (p_ref[...].astype(F32) + ra_ref[...].astype(F32)).astype(BF16)

    grid_spec = pltpu.PrefetchScalarGridSpec(
        num_scalar_prefetch=1, grid=(4, r // tr),
        in_specs=[pl.BlockSpec((None, None, tr, c), lambda j, i, cref: (j, cref[0], i, 0)),
                  pl.BlockSpec((None, tr, c), lambda j, i, cref: (j, i, 0))],
        out_specs=pl.BlockSpec((None, tr, c), lambda j, i, cref: (j, i, 0)))
    return pl.pallas_call(
        body, name="pair_add", grid_spec=grid_spec, out_shape=jax.ShapeDtypeStruct((4, r, c), BF16),
        compiler_params=_params("parallel", "parallel"))(c_idx, p4, ra)


def _sum_adamw(rb, w, m, v):
    r, c = w.shape
    tr = _tile(r, 128)

    def body(rb_ref, w_ref, m_ref, v_ref, g_ref, d_ref, mo_ref, vo_ref):
        g = rb_ref[0].astype(F32)
        for j in range(1, 4):
            g = g + rb_ref[j].astype(F32)
        g_ref[...] = g
        d_ref[...], mo_ref[...], vo_ref[...] = _adamw(w_ref[...], g, m_ref[...], v_ref[...])

    blk = pl.BlockSpec((tr, c), lambda i: (i, 0))
    shp = jax.ShapeDtypeStruct((r, c), F32)
    return pl.pallas_call(
        body, name="sum_adamw", grid=(r // tr,), out_shape=(shp, shp, shp, shp),
        in_specs=[pl.BlockSpec((4, tr, c), lambda i: (0, i, 0)), blk, blk, blk], out_specs=(blk, blk, blk, blk),
        compiler_params=_params("parallel"))(rb, w, m, v)


def _position():
    x, y, c = lax.axis_index("x"), lax.axis_index("y"), lax.axis_index("c")
    return x, y, c


def _all_gather_weights(a, b):
    def body(a_ref, b_ref, ao_ref, bo_ref, send_sems, recv_sems, local_sems):
        x, y, c = _position()
        me, sibling = (x, y, c), (x, y, 1 - c)
        chips = [(1 - x, y), (x, 1 - y), (1 - x, 1 - y)]
        srcs, outs = (a_ref, b_ref), (ao_ref, bo_ref)

        def slot(t, px, py, pc):
            return outs[t].at[4 * px + 2 * py + pc]

        def copy(t, k, block, to, src=None):
            dst = slot(t, *block)
            return pltpu.make_async_remote_copy(
                src_ref=dst if src is None else src, dst_ref=dst, send_sem=send_sems.at[t, k],
                recv_sem=recv_sems.at[t, k], device_id=to, device_id_type=MESH)

        mine = [pltpu.make_async_copy(srcs[t], slot(t, *me), local_sems.at[t]) for t in range(2)]
        for cp in mine:
            cp.start()
        first = []
        for t in range(2):
            first.append(copy(t, 0, me, sibling, src=srcs[t]))
            first += [copy(t, 1 + j, me, (*chip, c), src=srcs[t]) for j, chip in enumerate(chips)]
        for cp in first:
            cp.start()
        passed = []
        for t in range(2):
            for j, chip in enumerate(chips):
                copy(t, 1 + j, (*chip, c), me).wait_recv()
                fwd = copy(t, 4 + j, (*chip, c), sibling)
                fwd.start()
                passed.append(fwd)
        for t in range(2):
            copy(t, 0, sibling, me).wait_recv()
            for j, chip in enumerate(chips):
                copy(t, 4 + j, (*chip, 1 - c), me).wait_recv()
        for cp in first + passed:
            cp.wait_send()
        for cp in mine:
            cp.wait()

    return pl.pallas_call(
        body, name="all_gather_weights",
        out_shape=(jax.ShapeDtypeStruct((N_DEV,) + a.shape, a.dtype), jax.ShapeDtypeStruct((N_DEV,) + b.shape, b.dtype)),
        in_specs=[ANY, ANY], out_specs=(ANY, ANY),
        scratch_shapes=[pltpu.SemaphoreType.DMA((2, 7)), pltpu.SemaphoreType.DMA((2, 7)), pltpu.SemaphoreType.DMA((2,))],
    )(a, b)


def _exchange_sibling(pa, pb):
    def body(pa_ref, pb_ref, ra_ref, rb_ref, send_sems, recv_sems):
        x, y, c = _position()
        sibling = (x, y, 1 - c)
        copies = []
        for t, (p_ref, r_ref) in enumerate(((pa_ref, ra_ref), (pb_ref, rb_ref))):
            for j in range(4):
                copies.append(pltpu.make_async_remote_copy(
                    src_ref=p_ref.at[2 * j + 1 - c], dst_ref=r_ref.at[j], send_sem=send_sems.at[t, j],
                    recv_sem=recv_sems.at[t, j], device_id=sibling, device_id_type=MESH))
        for cp in copies:
            cp.start()
        for cp in copies:
            cp.wait_recv()
        for cp in copies:
            cp.wait_send()

    return pl.pallas_call(
        body, name="exchange_sibling",
        out_shape=(jax.ShapeDtypeStruct((4,) + pa.shape[1:], pa.dtype), jax.ShapeDtypeStruct((4,) + pb.shape[1:], pb.dtype)),
        in_specs=[ANY, ANY], out_specs=(ANY, ANY),
        scratch_shapes=[pltpu.SemaphoreType.DMA((2, 4)), pltpu.SemaphoreType.DMA((2, 4))],
    )(pa, pb)


def _exchange_chips(ta, tb):
    def body(ta_ref, tb_ref, ra_ref, rb_ref, send_sems, recv_sems, local_sems):
        x, y, c = _position()
        chips = [(1 - x, y), (x, 1 - y), (1 - x, 1 - y)]
        my_chip = 2 * x + y
        copies, own = [], []
        for t, (t_ref, r_ref) in enumerate(((ta_ref, ra_ref), (tb_ref, rb_ref))):
            own.append(pltpu.make_async_copy(t_ref.at[my_chip], r_ref.at[my_chip], local_sems.at[t]))
            for j, (px, py) in enumerate(chips):
                copies.append(pltpu.make_async_remote_copy(
                    src_ref=t_ref.at[2 * px + py], dst_ref=r_ref.at[my_chip], send_sem=send_sems.at[t, j],
                    recv_sem=recv_sems.at[t, j], device_id=(px, py, c), device_id_type=MESH))
        for cp in own + copies:
            cp.start()
        for cp in copies:
            cp.wait_recv()
        for cp in copies:
            cp.wait_send()
        for cp in own:
            cp.wait()

    return pl.pallas_call(
        body, name="exchange_chips",
        out_shape=(jax.ShapeDtypeStruct(ta.shape, ta.dtype), jax.ShapeDtypeStruct(tb.shape, tb.dtype)),
        in_specs=[ANY, ANY], out_specs=(ANY, ANY),
        scratch_shapes=[pltpu.SemaphoreType.DMA((2, 3)), pltpu.SemaphoreType.DMA((2, 3)), pltpu.SemaphoreType.DMA((2,))],
    )(ta, tb)


SMALL_ROWS = 8
ROW_LB = 4
ROW_GN = 6
ROW_LOSS = 7


def _small_allreduce_adamw(part, w, m, v, lb_logits):
    width = part.shape[1]

    def body(p_ref, w_ref, m_ref, v_ref, lb_ref, g_ref, d_ref, mo_ref, vo_ref, buf, send_sems, recv_sems):
        x, y, c = _position()
        me = 4 * x + 2 * y + c
        buf[me] = p_ref[...]
        copies = []
        for k in range(N_DEV - 1):
            bx, by, bc = ((k + 1) >> 2) & 1, ((k + 1) >> 1) & 1, (k + 1) & 1
            peer = (x ^ bx, y ^ by, c ^ bc)
            copies.append(pltpu.make_async_remote_copy(
                src_ref=p_ref, dst_ref=buf.at[me], send_sem=send_sems.at[k], recv_sem=recv_sems.at[k],
                device_id=peer, device_id_type=MESH))
        for cp in copies:
            cp.start()
        for cp in copies:
            cp.wait_recv()
        for cp in copies:
            cp.wait_send()
        tot = buf[0]
        for dev in range(1, N_DEV):
            tot = tot + buf[dev]
        lbv = lb_ref[...]
        lb = _sigmoid(lbv[0:1] - lbv[1:2])
        glb = tot[ROW_LB:ROW_LB + 1] * lb * (1.0 - lb)
        loss = jnp.sum(tot[ROW_LOSS:ROW_LOSS + 1], axis=-1, keepdims=True)
        row = lax.broadcasted_iota(jnp.int32, (SMALL_ROWS, width), 0)
        g = jnp.where(row == ROW_LB, glb, jnp.where(row == ROW_LB + 1, -glb, tot))
        g = jnp.where(row == ROW_LOSS, loss, g)
        g_ref[...] = g
        d_ref[...], mo_ref[...], vo_ref[...] = _adamw(w_ref[...], g, m_ref[...], v_ref[...])

    vm = pl.BlockSpec(memory_space=pltpu.VMEM)
    shp = jax.ShapeDtypeStruct((SMALL_ROWS, width), F32)
    return pl.pallas_call(
        body, name="small_allreduce_adamw", out_shape=(shp, shp, shp, shp),
        in_specs=[vm] * 5, out_specs=(vm, vm, vm, vm),
        scratch_shapes=[pltpu.VMEM((N_DEV, SMALL_ROWS, width), F32), pltpu.SemaphoreType.DMA((N_DEV - 1,)),
                        pltpu.SemaphoreType.DMA((N_DEV - 1,))],
    )(part, w, m, v, lb_logits)


def _pack_small(norm_gain, final_gain, lb2, gnorm, last_row, width):
    pad = lambda a: jnp.pad(a.reshape(1, -1), ((0, 0), (0, width - a.size)))
    return jnp.concatenate([norm_gain.reshape(2, width), final_gain.reshape(2, width), lb2.reshape(2, width),
                            pad(gnorm), last_row.reshape(1, width)], axis=0)


def _unpack_small(p, d, e, hd):
    return (p[0:2].reshape(1, d), p[2:4].reshape(d), p[4:6].reshape(2, e), p[6:7, :hd].reshape(1, hd))


def kernel(x, norm_gain, w_in, lb_logits, hgrn_gnorm, w_out, final_gain, loss_target, m_norm_gain, m_w_in, m_lb_logits, m_hgrn_gnorm, m_w_out, m_final_gain, v_norm_gain, v_w_in, v_lb_logits, v_hgrn_gnorm, v_w_out, v_final_gain):
    s, d = x.shape[1], x.shape[2]
    e = w_in.shape[2]
    assert d == 2 * e and lb_logits.shape == (2, e) and w_out.shape[1] * N_DEV == 2 * e
    x2d = x.reshape(s, d)
    tgt = loss_target.reshape(s, d)

    w_in_full, w_out_full = _all_gather_weights(_cast_bf16(w_in[0]), _cast_bf16(w_out[0]))
    w_out_full = w_out_full.reshape(2 * e, d)

    h = _rmsnorm_fwd(x2d, norm_gain)
    z = _inproj(h, w_in_full)
    y_h, states = _hgrn_fwd(z, lb_logits, hgrn_gnorm)
    o_attn, lse, y_a = _attn_fwd(z)
    dx2, dx2b, dy, loss_vec, dfg = _outproj_loss(x2d, y_h, y_a, w_out_full, final_gain.reshape(1, d), tgt)

    pwo = _dwout(y_h, y_a, dx2b)
    dza = _attn_bwd(z, dy, o_attn, lse)
    dzh, dlb, dgn = _hgrn_bwd(z, dy, states, lb_logits, hgrn_gnorm)
    grad_x, dng = _dh_dx(dzh, dza, w_in_full, x2d, norm_gain, dx2)
    pwi = _dwin(h, dzh, dza)

    c_idx = lax.axis_index("c").astype(jnp.int32).reshape(1)
    pwo8 = pwo.reshape(N_DEV, w_out.shape[1], d)
    ra_i, ra_o = _exchange_sibling(pwi, pwo8)
    rb_i, rb_o = _exchange_chips(_pair_add(pwi, ra_i, c_idx), _pair_add(pwo8, ra_o, c_idx))
    g_wi, d_wi, nm_wi, nv_wi = _sum_adamw(rb_i, w_in[0], m_w_in[0], v_w_in[0])
    g_wo, d_wo, nm_wo, nv_wo = _sum_adamw(rb_o, w_out[0], m_w_out[0], v_w_out[0])

    width = d // 2
    zero_row = jnp.zeros((1, width), F32)
    loss_row = loss_vec[:, :width] + loss_vec[:, width:]
    part = _pack_small(dng, dfg, jnp.concatenate([dlb, zero_row], axis=0), dgn, loss_row, width)
    pw = _pack_small(norm_gain, final_gain, lb_logits, hgrn_gnorm, zero_row, width)
    pm = _pack_small(m_norm_gain, m_final_gain, m_lb_logits, m_hgrn_gnorm, zero_row, width)
    pv = _pack_small(v_norm_gain, v_final_gain, v_lb_logits, v_hgrn_gnorm, zero_row, width)
    sg, sd, sm, sv = _small_allreduce_adamw(part, pw, pm, pv, lb_logits)
    hd = hgrn_gnorm.shape[1]
    g_ng, g_fg, g_lb, g_gn = _unpack_small(sg, d, e, hd)
    d_ng, d_fg, d_lb, d_gn = _unpack_small(sd, d, e, hd)
    m_ng, m_fg, m_lb, m_gn = _unpack_small(sm, d, e, hd)
    v_ng, v_fg, v_lb, v_gn = _unpack_small(sv, d, e, hd)
    loss = sg[ROW_LOSS, 0]

    one = lambda a: a[None]
    return (loss, grad_x.reshape(1, s, d), g_ng, one(g_wi), g_lb, g_gn, one(g_wo), g_fg,
            d_ng, one(d_wi), d_lb, d_gn, one(d_wo), d_fg,
            m_ng, one(nm_wi), m_lb, m_gn, one(nm_wo), m_fg,
            v_ng, one(nv_wi), v_lb, v_gn, one(nv_wo), v_fg)
```

```python
import functools
import math

import jax
import jax.numpy as jnp
from jax import lax
from jax.experimental import pallas as pl
from jax.experimental.pallas import tpu as pltpu

NORM_EPS = 1e-6
HGRN_HEAD = 128
HGRN_CHUNK = 64
ATTN_HEAD = 64
ATTN_BAND = 128
DILATIONS = (1, 4, 16)
N_SPLITS = 8
N_DEV = 8
ADAM_LR = 0.001
ADAM_B1 = 0.9
ADAM_B2 = 0.999
ADAM_EPS = 1e-08
ADAM_WD = 0.01
ADAM_STEP = 10
LANES = 128
MESH = pl.DeviceIdType.MESH
F32 = jnp.float32
BF16 = jnp.bfloat16
NEG_BIG = -1e30
VMEM_LIMIT = 56 * 1024 * 1024

ANY = pl.BlockSpec(memory_space=pl.ANY)


def _params(*sem):
    return pltpu.CompilerParams(dimension_semantics=sem, vmem_limit_bytes=VMEM_LIMIT)


def _tile(n, pref):
    t = min(n, pref)
    assert n % t == 0, (n, pref)
    return t


def _dot(a, b, precision=None):
    return jnp.dot(a, b, preferred_element_type=F32, precision=precision)


def _dot_nt(a, b):
    return lax.dot_general(a, b, (((1,), (1,)), ((), ())), preferred_element_type=F32)


def _dot_tn(a, b):
    return lax.dot_general(a, b, (((0,), (0,)), ((), ())), preferred_element_type=F32)


def _sigmoid(x):
    return 1.0 / (1.0 + jnp.exp(-x))


def _dsilu(x, s):
    return s * (1.0 + x * (1.0 - s))


def _adamw(w, g, m, v):
    m = ADAM_B1 * m + (1.0 - ADAM_B1) * g
    v = ADAM_B2 * v + (1.0 - ADAM_B2) * (g * g)
    m_hat = m / (1.0 - ADAM_B1 ** ADAM_STEP)
    v_hat = v / (1.0 - ADAM_B2 ** ADAM_STEP)
    delta = -ADAM_LR * (m_hat / (jnp.sqrt(v_hat) + ADAM_EPS) + ADAM_WD * w)
    return delta, m, v


def _cast_bf16(a):
    r, c = a.shape
    tr = _tile(r, 256)

    def body(a_ref, o_ref):
        o_ref[...] = a_ref[...].astype(BF16)

    return pl.pallas_call(
        body, name="cast_bf16", grid=(r // tr,), out_shape=jax.ShapeDtypeStruct((r, c), BF16),
        in_specs=[pl.BlockSpec((tr, c), lambda i: (i, 0))], out_specs=pl.BlockSpec((tr, c), lambda i: (i, 0)),
        compiler_params=_params("parallel"))(a)


def _rmsnorm_fwd(x, gain):
    s, d = x.shape
    tm = _tile(s, 512)

    def body(x_ref, g_ref, h_ref):
        xv = x_ref[...]
        r = lax.rsqrt(jnp.mean(xv * xv, axis=-1, keepdims=True) + NORM_EPS)
        h_ref[...] = (xv * r * g_ref[...]).astype(BF16)

    return pl.pallas_call(
        body, name="rmsnorm_fwd", grid=(s // tm,), out_shape=jax.ShapeDtypeStruct((s, d), BF16),
        in_specs=[pl.BlockSpec((tm, d), lambda i: (i, 0)), pl.BlockSpec((1, d), lambda i: (0, 0))],
        out_specs=pl.BlockSpec((tm, d), lambda i: (i, 0)), compiler_params=_params("parallel"))(x, gain)


def _inproj(h, w_full):
    s, d = h.shape
    e = w_full.shape[2]
    tm = _tile(s, 512)

    def body(h_ref, w_ref, z_ref):
        z_ref[...] = _dot(h_ref[...], w_ref[...])

    return pl.pallas_call(
        body, name="inproj", grid=(s // tm, N_SPLITS),
        out_shape=jax.ShapeDtypeStruct((s, N_SPLITS * e), F32),
        in_specs=[pl.BlockSpec((tm, d), lambda i, j: (i, 0)), pl.BlockSpec((None, d, e), lambda i, j: (j, 0, 0))],
        out_specs=pl.BlockSpec((tm, e), lambda i, j: (i, j)),
        compiler_params=_params("parallel", "arbitrary"))(h, w_full)


def _chunk_masks(tb):
    row = lax.broadcasted_iota(jnp.int32, (tb, tb), 0)
    col = lax.broadcasted_iota(jnp.int32, (tb, tb), 1)
    same = (row // HGRN_CHUNK) == (col // HGRN_CHUNK)
    lower = jnp.where(same & (col <= row), 1.0, 0.0).astype(F32)
    upper = jnp.where(same & (col >= row), 1.0, 0.0).astype(F32)
    return lower, upper


def _hgrn_gates(qp, fp, lbv):
    lb = _sigmoid(lbv[0:1] - lbv[1:2])
    sq = _sigmoid(qp)
    q = qp * sq
    sg = _sigmoid(fp)
    f = lb + (1.0 - lb) * sg
    k = 1.0 - f
    return lb, sq, q, sg, f, k


def _hgrn_fwd(z, lb_logits, gnorm):
    s = z.shape[0]
    e = z.shape[1] // N_SPLITS
    nh = e // HGRN_HEAD
    tb = _tile(s, 256)
    nc = tb // HGRN_CHUNK
    nb = s // tb
    C = HGRN_CHUNK

    def body(q_ref, f_ref, i_ref, g_ref, lb_ref, gn_ref, y_ref, st_ref, state, o_scr):
        @pl.when(pl.program_id(1) == 0)
        def _():
            state[...] = jnp.zeros_like(state)

        lb, sq, q, sg, f, k = _hgrn_gates(q_ref[...], f_ref[...], lb_ref[...])
        lower, _ = _chunk_masks(tb)
        b = _dot(lower, jnp.log(f), precision=lax.Precision.HIGHEST)
        b3 = b.reshape(nc, C, HGRN_HEAD)
        bc = b3[:, C - 1:C, :]
        qt = (q * jnp.exp(b)).astype(BF16)
        kt = (k * jnp.exp(-b)).astype(BF16)
        ke = (k.reshape(nc, C, HGRN_HEAD) * jnp.exp(bc - b3)).reshape(tb, HGRN_HEAD).astype(BF16)
        v = i_ref[...].astype(BF16)
        tri = lax.broadcasted_iota(jnp.int32, (C, C), 1) <= lax.broadcasted_iota(jnp.int32, (C, C), 0)
        for c in range(nc):
            sl = slice(c * C, (c + 1) * C)
            st = state[...]
            st_ref[c] = st
            a = jnp.where(tri, _dot_nt(qt[sl], kt[sl]), 0.0)
            o_scr[sl, :] = _dot(a.astype(BF16), v[sl]) + _dot_nt(qt[sl], st.astype(BF16))
            state[...] = st * jnp.exp(bc[c]) + _dot_tn(v[sl], ke[sl])
        o = o_scr[...]
        rms = lax.rsqrt(jnp.mean(o * o, axis=-1, keepdims=True) + NORM_EPS)
        gp = g_ref[...]
        y_ref[...] = (o * rms * gn_ref[...] * (gp * _sigmoid(gp))).astype(BF16)

    col = lambda kk: (lambda h, n: (n, kk * nh + h))
    return pl.pallas_call(
        body, name="hgrn_fwd", grid=(nh, nb),
        out_shape=(jax.ShapeDtypeStruct((s, e), BF16),
                   jax.ShapeDtypeStruct((nh, s // C, HGRN_HEAD, HGRN_HEAD), F32)),
        in_specs=[pl.BlockSpec((tb, HGRN_HEAD), col(0)), pl.BlockSpec((tb, HGRN_HEAD), col(1)),
                  pl.BlockSpec((tb, HGRN_HEAD), col(2)), pl.BlockSpec((tb, HGRN_HEAD), col(3)),
                  pl.BlockSpec((2, HGRN_HEAD), lambda h, n: (0, h)), pl.BlockSpec((1, HGRN_HEAD), lambda h, n: (0, 0))],
        out_specs=(pl.BlockSpec((tb, HGRN_HEAD), lambda h, n: (n, h)),
                   pl.BlockSpec((None, nc, HGRN_HEAD, HGRN_HEAD), lambda h, n: (h, n, 0, 0))),
        scratch_shapes=[pltpu.VMEM((HGRN_HEAD, HGRN_HEAD), F32), pltpu.VMEM((tb, HGRN_HEAD), F32)],
        compiler_params=_params("parallel", "arbitrary"))(z, z, z, z, lb_logits, gnorm)


def _hgrn_bwd(z, dy, states, lb_logits, gnorm):
    s = z.shape[0]
    e = z.shape[1] // N_SPLITS
    nh = e // HGRN_HEAD
    tb = _tile(s, 256)
    nc = tb // HGRN_CHUNK
    nb = s // tb
    C = HGRN_CHUNK
    H = HGRN_HEAD

    def body(q_ref, f_ref, i_ref, g_ref, dy_ref, st_ref, lb_ref, gn_ref, dz_ref, dlb_ref, dgn_ref,
             gstate, o_scr, a_scr, dq_scr, dk_scr, dv_scr, e_scr):
        first = (pl.program_id(0) == 0) & (pl.program_id(1) == 0)

        @pl.when(first)
        def _():
            dgn_ref[...] = jnp.zeros_like(dgn_ref)

        @pl.when(pl.program_id(1) == 0)
        def _():
            gstate[...] = jnp.zeros_like(gstate)
            dlb_ref[...] = jnp.zeros_like(dlb_ref)

        qp = q_ref[...]
        lb, sq, q, sg, f, k = _hgrn_gates(qp, f_ref[...], lb_ref[...])
        lower, upper = _chunk_masks(tb)
        b = _dot(lower, jnp.log(f), precision=lax.Precision.HIGHEST)
        b3 = b.reshape(nc, C, H)
        bc = b3[:, C - 1:C, :]
        eb = jnp.exp(b)
        enb = jnp.exp(-b)
        eend = jnp.exp(bc - b3).reshape(tb, H)
        qt = (q * eb).astype(BF16)
        kt = (k * enb).astype(BF16)
        ke = (k * eend).astype(BF16)
        v = i_ref[...].astype(BF16)
        tri = lax.broadcasted_iota(jnp.int32, (C, C), 1) <= lax.broadcasted_iota(jnp.int32, (C, C), 0)
        for c in range(nc):
            sl = slice(c * C, (c + 1) * C)
            a = jnp.where(tri, _dot_nt(qt[sl], kt[sl]), 0.0)
            a_scr[sl, :] = a
            o_scr[sl, :] = _dot(a.astype(BF16), v[sl]) + _dot_nt(qt[sl], st_ref[c].astype(BF16))
        o = o_scr[...]
        rms = lax.rsqrt(jnp.mean(o * o, axis=-1, keepdims=True) + NORM_EPS)
        on = o * rms
        gn = gn_ref[...]
        gp = g_ref[...]
        sgg = _sigmoid(gp)
        dyv = dy_ref[...]
        d_on = dyv * (gp * sgg)
        dz_ref[3] = (dyv * on * gn * _dsilu(gp, sgg)).astype(BF16)
        dgn_ref[...] += jnp.sum(d_on * on, axis=0, keepdims=True)
        u = d_on * gn
        do = (rms * (u - on * jnp.mean(u * on, axis=-1, keepdims=True))).astype(BF16)
        for c in reversed(range(nc)):
            sl = slice(c * C, (c + 1) * C)
            stp = st_ref[c]
            gt = gstate[...]
            gtb = gt.astype(BF16)
            dec = jnp.exp(bc[c])
            ab = a_scr[sl, :].astype(BF16)
            da = jnp.where(tri, _dot_nt(do[sl], v[sl]), 0.0).astype(BF16)
            dqt = _dot(da, kt[sl]) + _dot(do[sl], stp.astype(BF16))
            dkt = _dot_tn(da, qt[sl])
            dks = _dot(v[sl], gtb) * eend[sl]
            dv_scr[sl, :] = _dot_tn(ab, do[sl]) + _dot_nt(ke[sl], gtb)
            dq_scr[sl, :] = dqt * eb[sl]
            dk_scr[sl, :] = dkt * enb[sl] + dks
            ech = jnp.sum(k[sl] * dks, axis=0, keepdims=True) + jnp.sum(gt * dec * stp, axis=0, keepdims=True)
            e_scr[sl, :] = jnp.broadcast_to(ech, (C, H))
            gstate[...] = gt * dec + _dot_tn(do[sl], qt[sl])
        dq = dq_scr[...]
        dk = dk_scr[...]
        dlf = _dot(upper, q * dq - k * dk, precision=lax.Precision.HIGHEST) + e_scr[...]
        dft = dlf / f - dk
        dz_ref[0] = (dq * _dsilu(qp, sq)).astype(BF16)
        dz_ref[1] = (dft * (1.0 - lb) * sg * (1.0 - sg)).astype(BF16)
        dz_ref[2] = dv_scr[...].astype(BF16)
        dlb_ref[...] += jnp.sum(dft * (1.0 - sg), axis=0, keepdims=True)

    col = lambda kk: (lambda h, n: (nb - 1 - n, kk * nh + h))
    return pl.pallas_call(
        body, name="hgrn_bwd", grid=(nh, nb),
        out_shape=(jax.ShapeDtypeStruct((4, s, e), BF16), jax.ShapeDtypeStruct((1, e), F32),
                   jax.ShapeDtypeStruct((1, H), F32)),
        in_specs=[pl.BlockSpec((tb, H), col(0)), pl.BlockSpec((tb, H), col(1)),
                  pl.BlockSpec((tb, H), col(2)), pl.BlockSpec((tb, H), col(3)),
                  pl.BlockSpec((tb, H), lambda h, n: (nb - 1 - n, h)),
                  pl.BlockSpec((None, nc, H, H), lambda h, n: (h, nb - 1 - n, 0, 0)),
                  pl.BlockSpec((2, H), lambda h, n: (0, h)), pl.BlockSpec((1, H), lambda h, n: (0, 0))],
        out_specs=(pl.BlockSpec((4, tb, H), lambda h, n: (0, nb - 1 - n, h)),
                   pl.BlockSpec((1, H), lambda h, n: (0, h)), pl.BlockSpec((1, H), lambda h, n: (0, 0))),
        scratch_shapes=[pltpu.VMEM((H, H), F32)] + [pltpu.VMEM((tb, H), F32)] + [pltpu.VMEM((tb, C), F32)]
        + [pltpu.VMEM((tb, H), F32)] * 4,
        compiler_params=_params("arbitrary", "arbitrary"))(z, z, z, z, dy, states, lb_logits, gnorm)


ATTN_T = 16 * ATTN_BAND
SCALE = ATTN_HEAD ** -0.5
TILE_UNROLL = 2


def _slope(hh, nheads):
    head = (2 * pl.program_id(0) + hh + 1).astype(F32)
    return jnp.exp(jnp.full((1, 1), -8.0 / nheads * math.log(2.0), F32) * head)


def _fill_bias(bias, nheads, delta, edge_ok):
    band = (delta >= 0) & (delta <= ATTN_BAND)
    dist = delta.astype(F32)
    for pi, dil in enumerate(DILATIONS):
        for hh in range(2):
            full = jnp.where(band, -(_slope(hh, nheads) * float(dil)) * dist, NEG_BIG)
            bias[(pi * 2 + hh) * 2] = full
            bias[(pi * 2 + hh) * 2 + 1] = jnp.where(edge_ok, full, NEG_BIG)


def _rows(start, size, stride):
    if stride == 1:
        return pl.ds(pl.multiple_of(start, ATTN_BAND), size)
    return pl.ds(start, size, stride=stride)


def _head_lanes(rows, hh):
    return (lax.broadcasted_iota(jnp.int32, (rows, LANES), 1) // ATTN_HEAD) == hh


def _split3(a):
    hi = a.astype(BF16).astype(F32)
    mid = (a - hi).astype(BF16).astype(F32)
    lo = (a - hi - mid).astype(BF16).astype(F32)
    return hi, mid, lo


def _attn_fwd(z):
    s = z.shape[0]
    e = z.shape[1] // N_SPLITS
    npair = e // LANES
    T = ATTN_T
    assert s % T == 0
    nsb = s // T
    W = ATTN_BAND
    nt = T // W
    HD = ATTN_HEAD
    chunk = 256

    def body(q_ref, kp_ref, kc_ref, vp_ref, vc_ref, g_ref, o_ref, l_ref, y_ref, qa, kbuf, va, bias, accs, ms, msw, lsw):
        sb = pl.program_id(1)
        for hh in range(2):
            def stage(i, carry):
                rows = pl.ds(pl.multiple_of(i * chunk, chunk), chunk)
                mine = _head_lanes(chunk, hh)
                qa[hh, rows, :] = jnp.where(mine, q_ref[rows, :] * SCALE, 0.0)
                va[hh, rows, :] = jnp.where(mine, vp_ref[rows, :], 1.0)
                va[hh, pl.ds(pl.multiple_of(T + i * chunk, chunk), chunk), :] = jnp.where(mine, vc_ref[rows, :], 1.0)
                return carry

            lax.fori_loop(0, T // chunk, stage, 0)
        kbuf[0:T, :] = kp_ref[...]
        kbuf[T:, :] = kc_ref[...]
        qi = lax.broadcasted_iota(jnp.int32, (W, 2 * W), 0)
        kj = lax.broadcasted_iota(jnp.int32, (W, 2 * W), 1)
        _fill_bias(bias, 2 * npair, W + qi - kj, kj >= W)

        def tile(tau, carry):
            first = _head_lanes(W, 0)
            rows, scores = [], []
            for pi, dil in enumerate(DILATIONS):
                r = tau % dil
                ub = tau // dil
                qrows = _rows(r + dil * W * ub, W, dil)
                krows = _rows(T + dil * W * (ub - 1) + r, 2 * W, dil)
                var = jnp.where((sb == 0) & (ub == 0), 1, 0)
                kt = kbuf[krows, :].astype(BF16)
                rows.append((qrows, krows))
                scores.append([_dot_nt(qa[hh, qrows, :].astype(BF16), kt) + bias[(pi * 2 + hh) * 2 + var]
                               for hh in range(2)])
            maxes = [[jnp.max(sc, axis=-1, keepdims=True) for sc in pair] for pair in scores]
            probs = [[jnp.exp(sc - m).astype(BF16) for sc, m in zip(ps, pm)] for ps, pm in zip(scores, maxes)]
            for pi, (qrows, krows) in enumerate(rows):
                outs = [_dot(probs[pi][hh], va[hh, krows, :].astype(BF16)) for hh in range(2)]
                accs[pi, qrows, :] = jnp.where(first, outs[0], outs[1])
                lsw[pi, qrows, :] = jnp.where(first, outs[1], outs[0])
                ms[pi, qrows, :] = jnp.where(first, maxes[pi][0], maxes[pi][1])
                msw[pi, qrows, :] = jnp.where(first, maxes[pi][1], maxes[pi][0])
            return carry

        lax.fori_loop(0, nt, tile, 0, unroll=TILE_UNROLL)

        def merge(i, carry):
            rows = pl.ds(pl.multiple_of(i * chunk, chunk), chunk)
            m1, m2, m3 = ms[0, rows, :], ms[1, rows, :], ms[2, rows, :]
            mx = jnp.maximum(jnp.maximum(m1, m2), m3)
            s1, s2, s3 = msw[0, rows, :], msw[1, rows, :], msw[2, rows, :]
            sx = jnp.maximum(jnp.maximum(s1, s2), s3)
            den_sw = (jnp.exp(s1 - sx) * lsw[0, rows, :] + jnp.exp(s2 - sx) * lsw[1, rows, :]
                      + jnp.exp(s3 - sx) * lsw[2, rows, :])
            den = pltpu.roll(den_sw, ATTN_HEAD, 1)
            o = (jnp.exp(m1 - mx) * accs[0, rows, :] + jnp.exp(m2 - mx) * accs[1, rows, :]
                 + jnp.exp(m3 - mx) * accs[2, rows, :]) / den
            o_ref[rows, :] = o
            l_ref[rows, :] = mx + jnp.log(den)
            gp = g_ref[rows, :]
            y_ref[rows, :] = (o * (gp * _sigmoid(gp))).astype(BF16)
            return carry

        lax.fori_loop(0, T // chunk, merge, 0)

    cur = lambda split: (lambda hp, sb: (sb, split * npair + hp))
    prev = lambda split: (lambda hp, sb: (jnp.maximum(sb - 1, 0), split * npair + hp))
    blk = lambda index: pl.BlockSpec((T, LANES), index)
    out = blk(lambda hp, sb: (sb, hp))
    buf = lambda rows: pltpu.VMEM((rows, LANES), F32)
    return pl.pallas_call(
        body, name="attn_fwd", grid=(npair, nsb),
        out_shape=(jax.ShapeDtypeStruct((s, e), F32), jax.ShapeDtypeStruct((s, e), F32), jax.ShapeDtypeStruct((s, e), BF16)),
        in_specs=[blk(cur(4)), blk(prev(5)), blk(cur(5)), blk(prev(6)), blk(cur(6)), blk(cur(7))],
        out_specs=(out, out, out),
        scratch_shapes=[pltpu.VMEM((2, T, LANES), F32), buf(2 * T), pltpu.VMEM((2, 2 * T, LANES), F32),
                        pltpu.VMEM((12, W, 2 * W), F32)] + [pltpu.VMEM((3, T, LANES), F32)] * 4,
        compiler_params=_params("parallel", "arbitrary"))(z, z, z, z, z, z)


def _outproj_loss(x, y_h, y_a, w_out_full, final_gain, target):
    s, d = x.shape
    e = y_h.shape[1]
    tm = _tile(s, 256)

    def body(x_ref, yh_ref, ya_ref, w_ref, g_ref, t_ref, dx_ref, dxb_ref, dy_ref, loss_ref, dg_ref):
        @pl.when(pl.program_id(0) == 0)
        def _():
            loss_ref[...] = jnp.zeros_like(loss_ref)
            dg_ref[...] = jnp.zeros_like(dg_ref)

        w = w_ref[...]
        x2 = x_ref[...] + _dot(yh_ref[...], w[0:e]) + _dot(ya_ref[...], w[e:2 * e])
        r = lax.rsqrt(jnp.mean(x2 * x2, axis=-1, keepdims=True) + NORM_EPS)
        xn = x2 * r
        g = g_ref[...]
        err = xn * g - t_ref[...]
        loss_ref[...] += jnp.sum(err * err, axis=0, keepdims=True) * (0.5 / d)
        dyo = err * (1.0 / d)
        dg_ref[...] += jnp.sum(dyo * xn, axis=0, keepdims=True)
        u = dyo * g
        dx2 = r * (u - xn * jnp.mean(u * xn, axis=-1, keepdims=True))
        dx_ref[...] = dx2
        dxb = dx2.astype(BF16)
        dxb_ref[...] = dxb
        dy_ref[...] = _dot_nt(dxb, w)

    row = pl.BlockSpec((tm, d), lambda i: (i, 0))
    half = pl.BlockSpec((tm, e), lambda i: (i, 0))
    vec = pl.BlockSpec((1, d), lambda i: (0, 0))
    return pl.pallas_call(
        body, name="outproj_loss", grid=(s // tm,),
        out_shape=(jax.ShapeDtypeStruct((s, d), F32), jax.ShapeDtypeStruct((s, d), BF16),
                   jax.ShapeDtypeStruct((s, 2 * e), F32), jax.ShapeDtypeStruct((1, d), F32),
                   jax.ShapeDtypeStruct((1, d), F32)),
        in_specs=[row, half, half, pl.BlockSpec((2 * e, d), lambda i: (0, 0)), vec, row],
        out_specs=(row, row, pl.BlockSpec((tm, 2 * e), lambda i: (i, 0)), vec, vec),
        compiler_params=_params("arbitrary"))(x, y_h, y_a, w_out_full, final_gain, target)


def _dwout(y_h, y_a, dxb):
    s, e = y_h.shape
    d = dxb.shape[1]
    ts = _tile(s, 512)
    ns = s // ts

    def body(yh_ref, ya_ref, dx_ref, o_ref, acc):
        half = pl.program_id(0)
        step = pl.program_id(1)

        @pl.when(step == 0)
        def _():
            acc[...] = jnp.zeros_like(acc)

        @pl.when(half == 0)
        def _():
            acc[...] += _dot_tn(yh_ref[...], dx_ref[...])

        @pl.when(half == 1)
        def _():
            acc[...] += _dot_tn(ya_ref[...], dx_ref[...])

        @pl.when(step == ns - 1)
        def _():
            o_ref[...] = acc[...].astype(BF16)

    return pl.pallas_call(
        body, name="dwout", grid=(2, ns), out_shape=jax.ShapeDtypeStruct((2 * e, d), BF16),
        in_specs=[pl.BlockSpec((ts, e), lambda hf, k: (k * (1 - hf), 0)), pl.BlockSpec((ts, e), lambda hf, k: (k * hf, 0)),
                  pl.BlockSpec((ts, d), lambda hf, k: (k, 0))],
        out_specs=pl.BlockSpec((e, d), lambda hf, k: (hf, 0)),
        scratch_shapes=[pltpu.VMEM((e, d), F32)],
        compiler_params=_params("parallel", "arbitrary"))(y_h, y_a, dxb)


def _attn_bwd(z, dy, o, lse):
    s, e = o.shape
    npair = e // LANES
    T = ATTN_T
    assert s % T == 0
    nsb = s // T
    W = ATTN_BAND
    nt = T // W
    HD = ATTN_HEAD
    chunk = 256

    def body(k_ref, v_ref, qc_ref, qn_ref, dyc_ref, dyn_ref, gc_ref, gn_ref, oc_ref, on_ref, lc_ref, ln_ref,
             dz_ref, qa, doa, ka, va, dqacc, dkacc, dvacc, bias):
        sb = pl.program_id(1)
        halves = ((qc_ref, dyc_ref, gc_ref, oc_ref, lc_ref), (qn_ref, dyn_ref, gn_ref, on_ref, ln_ref))
        for half, (q_r, dy_r, g_r, o_r, l_r) in enumerate(halves):
            def stage(i, carry):
                rows = pl.ds(pl.multiple_of(i * chunk, chunk), chunk)
                dst = pl.ds(pl.multiple_of(half * T + i * chunk, chunk), chunk)
                lane = lax.broadcasted_iota(jnp.int32, (chunk, LANES), 1)
                gp = g_r[rows, :]
                sg = _sigmoid(gp)
                dyv = dy_r[rows, :]
                ov = o_r[rows, :]
                dov = dyv * (gp * sg)
                prod = dov * ov
                qv = q_r[rows, :] * SCALE
                lv = l_r[rows, :]
                for hh in range(2):
                    mine = _head_lanes(chunk, hh)
                    spare = (1 - hh) * HD
                    lse_parts = _split3(lv[:, hh * HD:hh * HD + 1])
                    dl_parts = _split3(jnp.sum(prod[:, hh * HD:(hh + 1) * HD], axis=-1, keepdims=True))
                    qh = jnp.where(mine, qv, 0.0)
                    dh = jnp.where(mine, dov, 0.0)
                    for j in range(3):
                        qh = jnp.where(lane == spare + j, lse_parts[j], qh)
                        dh = jnp.where(lane == spare + j, dl_parts[j], dh)
                    qa[hh, dst, :] = qh
                    doa[hh, dst, :] = dh
                    if half == 0:
                        minus = (lane >= spare) & (lane < spare + 3)
                        ka[hh, rows, :] = jnp.where(minus, -1.0, k_ref[rows, :])
                        va[hh, rows, :] = jnp.where(minus, -1.0, v_ref[rows, :])
                if half == 0:
                    dz_ref[3, rows, :] = (dyv * ov * _dsilu(gp, sg)).astype(BF16)
                return carry

            lax.fori_loop(0, T // chunk, stage, 0)

        @pl.when(sb == 0)
        def _():
            dqacc[0:T, :] = jnp.zeros((T, LANES), F32)

        dqacc[T:, :] = jnp.zeros((T, LANES), F32)
        dkacc[...] = jnp.zeros_like(dkacc)
        dvacc[...] = jnp.zeros_like(dvacc)
        qi = lax.broadcasted_iota(jnp.int32, (2 * W, W), 0)
        kj = lax.broadcasted_iota(jnp.int32, (2 * W, W), 1)
        _fill_bias(bias, 2 * npair, qi - kj, qi < W)

        def tile(tau, carry):
            rows, ops, sc, dpd = [], [], [], []
            for pi, dil in enumerate(DILATIONS):
                r = tau % dil
                ub = tau // dil
                start = r + dil * W * ub
                krows = _rows(start, W, dil)
                qrows = _rows(start, 2 * W, dil)
                var = jnp.where((sb == nsb - 1) & (ub == nt // dil - 1), 1, 0)
                rows.append((krows, qrows))
                for hh in range(2):
                    kt = ka[hh, krows, :].astype(BF16)
                    vt = va[hh, krows, :].astype(BF16)
                    qt = qa[hh, qrows, :].astype(BF16)
                    dt = doa[hh, qrows, :].astype(BF16)
                    ops.append((kt, qt, dt))
                    sc.append(_dot_nt(qt, kt) + bias[(pi * 2 + hh) * 2 + var])
                    dpd.append(_dot_nt(dt, vt))
            ps = [jnp.exp(s) for s in sc]
            dss = [(p * d).astype(BF16) for p, d in zip(ps, dpd)]
            pbs = [p.astype(BF16) for p in ps]
            dvs = [_dot_tn(pb, dt) for pb, (kt, qt, dt) in zip(pbs, ops)]
            dks = [_dot_tn(ds, qt) for ds, (kt, qt, dt) in zip(dss, ops)]
            dqs = [_dot(ds, kt) for ds, (kt, qt, dt) in zip(dss, ops)]
            for pi, (krows, qrows) in enumerate(rows):
                dkacc[krows, :] += jnp.where(_head_lanes(W, 0), dks[2 * pi], dks[2 * pi + 1])
                dvacc[krows, :] += jnp.where(_head_lanes(W, 0), dvs[2 * pi], dvs[2 * pi + 1])
                dqacc[qrows, :] += jnp.where(_head_lanes(2 * W, 0), dqs[2 * pi], dqs[2 * pi + 1]) * SCALE
            return carry

        lax.fori_loop(0, nt, tile, 0, unroll=TILE_UNROLL)

        def flush(i, carry):
            rows = pl.ds(pl.multiple_of(i * chunk, chunk), chunk)
            nxt = pl.ds(pl.multiple_of(T + i * chunk, chunk), chunk)
            dz_ref[0, rows, :] = dqacc[rows, :].astype(BF16)
            dz_ref[1, rows, :] = dkacc[rows, :].astype(BF16)
            dz_ref[2, rows, :] = dvacc[rows, :].astype(BF16)
            dqacc[rows, :] = dqacc[nxt, :]
            return carry

        lax.fori_loop(0, T // chunk, flush, 0)

    zc = lambda split: (lambda hp, sb: (sb, split * npair + hp))
    zn = lambda split: (lambda hp, sb: (jnp.minimum(sb + 1, nsb - 1), split * npair + hp))
    ec = lambda off: (lambda hp, sb: (sb, off + hp))
    en = lambda off: (lambda hp, sb: (jnp.minimum(sb + 1, nsb - 1), off + hp))
    blk = lambda index: pl.BlockSpec((T, LANES), index)
    buf = lambda rows: pltpu.VMEM((rows, LANES), F32)
    return pl.pallas_call(
        body, name="attn_bwd", grid=(npair, nsb), out_shape=jax.ShapeDtypeStruct((4, s, e), BF16),
        in_specs=[blk(zc(5)), blk(zc(6)), blk(zc(4)), blk(zn(4)), blk(ec(npair)), blk(en(npair)),
                  blk(zc(7)), blk(zn(7)), blk(ec(0)), blk(en(0)), blk(ec(0)), blk(en(0))],
        out_specs=pl.BlockSpec((4, T, LANES), lambda hp, sb: (0, sb, hp)),
        scratch_shapes=[pltpu.VMEM((2, 2 * T, LANES), F32), pltpu.VMEM((2, 2 * T, LANES), F32),
                        pltpu.VMEM((2, T, LANES), F32), pltpu.VMEM((2, T, LANES), F32),
                        buf(2 * T), buf(T), buf(T), pltpu.VMEM((12, 2 * W, W), F32)],
        compiler_params=_params("parallel", "arbitrary"))(z, z, z, z, dy, dy, z, z, o, o, lse, lse)


def _dz_specs(tm, e, axis):
    def mk(lo, hi):
        def index(i, k):
            row, grp = (i, k) if axis == 1 else (k, i)
            return (jnp.clip(grp - lo, 0, hi - lo - 1), row, 0)
        return pl.BlockSpec((None, tm, e), index)
    return [mk(0, 4), mk(4, 8)]


def _dz_pick(grp, dzh_ref, dza_ref, fn):
    @pl.when(grp < 4)
    def _():
        fn(dzh_ref[...])

    @pl.when(grp >= 4)
    def _():
        fn(dza_ref[...])


def _dh_dx(dzh, dza, w_full, x, gain, dx2):
    s, d = x.shape
    e = dzh.shape[2]
    tm = _tile(s, 512)

    def body(dzh_ref, dza_ref, w_ref, x_ref, g_ref, dx2_ref, gx_ref, dg_ref, acc):
        i, k = pl.program_id(0), pl.program_id(1)

        @pl.when((i == 0) & (k == 0))
        def _():
            dg_ref[...] = jnp.zeros_like(dg_ref)

        @pl.when(k == 0)
        def _():
            acc[...] = jnp.zeros_like(acc)

        def add(dz):
            acc[...] += _dot_nt(dz, w_ref[...])

        _dz_pick(k, dzh_ref, dza_ref, add)

        @pl.when(k == N_SPLITS - 1)
        def _():
            dh = acc[...]
            xv = x_ref[...]
            r = lax.rsqrt(jnp.mean(xv * xv, axis=-1, keepdims=True) + NORM_EPS)
            xn = xv * r
            dg_ref[...] += jnp.sum(dh * xn, axis=0, keepdims=True)
            u = dh * g_ref[...]
            gx_ref[...] = dx2_ref[...] + r * (u - xn * jnp.mean(u * xn, axis=-1, keepdims=True))

    row = pl.BlockSpec((tm, d), lambda i, k: (i, 0))
    vec = pl.BlockSpec((1, d), lambda i, k: (0, 0))
    return pl.pallas_call(
        body, name="dh_dx", grid=(s // tm, N_SPLITS),
        out_shape=(jax.ShapeDtypeStruct((s, d), F32), jax.ShapeDtypeStruct((1, d), F32)),
        in_specs=_dz_specs(tm, e, 1) + [pl.BlockSpec((None, d, e), lambda i, k: (k, 0, 0)), row, vec, row],
        out_specs=(row, vec), scratch_shapes=[pltpu.VMEM((tm, d), F32)],
        compiler_params=_params("arbitrary", "arbitrary"))(dzh, dza, w_full, x, gain, dx2)


def _dwin(h, dzh, dza):
    s, d = h.shape
    e = dzh.shape[2]
    ts = _tile(s, 512)
    ns = s // ts

    def body(dzh_ref, dza_ref, h_ref, o_ref, acc):
        j, k = pl.program_id(0), pl.program_id(1)

        @pl.when(k == 0)
        def _():
            acc[...] = jnp.zeros_like(acc)

        def add(dz):
            acc[...] += _dot_tn(h_ref[...], dz)

        _dz_pick(j, dzh_ref, dza_ref, add)

        @pl.when(k == ns - 1)
        def _():
            o_ref[...] = acc[...].astype(BF16)

    return pl.pallas_call(
        body, name="dwin", grid=(N_SPLITS, ns), out_shape=jax.ShapeDtypeStruct((N_SPLITS, d, e), BF16),
        in_specs=_dz_specs(ts, e, 0) + [pl.BlockSpec((ts, d), lambda j, k: (k, 0))],
        out_specs=pl.BlockSpec((None, d, e), lambda j, k: (j, 0, 0)),
        scratch_shapes=[pltpu.VMEM((d, e), F32)],
        compiler_params=_params("parallel", "arbitrary"))(dzh, dza, h)


def _pair_add(p, ra, c_idx):
    _, r, c = p.shape
    tr = _tile(r, 256)
    p4 = p.reshape(4, 2, r, c)

    def body(c_ref, p_ref, ra_ref, o_ref):
        o_ref[...] = (p_ref[...].astype(F32) + ra_ref[...].astype(F32)).astype(BF16)

    grid_spec = pltpu.PrefetchScalarGridSpec(
        num_scalar_prefetch=1, grid=(4, r // tr),
        in_specs=[pl.BlockSpec((None, None, tr, c), lambda j, i, cref: (j, cref[0], i, 0)),
                  pl.BlockSpec((None, tr, c), lambda j, i, cref: (j, i, 0))],
        out_specs=pl.BlockSpec((None, tr, c), lambda j, i, cref: (j, i, 0)))
    return pl.pallas_call(
        body, name="pair_add", grid_spec=grid_spec, out_shape=jax.ShapeDtypeStruct((4, r, c), BF16),
        compiler_params=_params("parallel", "parallel"))(c_idx, p4, ra)


def _sum_adamw(rb, w, m, v):
    r, c = w.shape
    tr = _tile(r, 128)

    def body(rb_ref, w_ref, m_ref, v_ref, g_ref, d_ref, mo_ref, vo_ref):
        g = rb_ref[0].astype(F32)
        for j in range(1, 4):
            g = g + rb_ref[j].astype(F32)
        g_ref[...] = g
        d_ref[...], mo_ref[...], vo_ref[...] = _adamw(w_ref[...], g, m_ref[...], v_ref[...])

    blk = pl.BlockSpec((tr, c), lambda i: (i, 0))
    shp = jax.ShapeDtypeStruct((r, c), F32)
    return pl.pallas_call(
        body, name="sum_adamw", grid=(r // tr,), out_shape=(shp, shp, shp, shp),
        in_specs=[pl.BlockSpec((4, tr, c), lambda i: (0, i, 0)), blk, blk, blk], out_specs=(blk, blk, blk, blk),
        compiler_params=_params("parallel"))(rb, w, m, v)


def _position():
    x, y, c = lax.axis_index("x"), lax.axis_index("y"), lax.axis_index("c")
    return x, y, c


def _all_gather_weights(a, b):
    def body(a_ref, b_ref, ao_ref, bo_ref, send_sems, recv_sems, local_sems):
        x, y, c = _position()
        me, sibling = (x, y, c), (x, y, 1 - c)
        chips = [(1 - x, y), (x, 1 - y), (1 - x, 1 - y)]
        srcs, outs = (a_ref, b_ref), (ao_ref, bo_ref)

        def slot(t, px, py, pc):
            return outs[t].at[4 * px + 2 * py + pc]

        def copy(t, k, block, to, src=None):
            dst = slot(t, *block)
            return pltpu.make_async_remote_copy(
                src_ref=dst if src is None else src, dst_ref=dst, send_sem=send_sems.at[t, k],
                recv_sem=recv_sems.at[t, k], device_id=to, device_id_type=MESH)

        mine = [pltpu.make_async_copy(srcs[t], slot(t, *me), local_sems.at[t]) for t in range(2)]
        for cp in mine:
            cp.start()
        first = []
        for t in range(2):
            first.append(copy(t, 0, me, sibling, src=srcs[t]))
            first += [copy(t, 1 + j, me, (*chip, c), src=srcs[t]) for j, chip in enumerate(chips)]
        for cp in first:
            cp.start()
        passed = []
        for t in range(2):
            for j, chip in enumerate(chips):
                copy(t, 1 + j, (*chip, c), me).wait_recv()
                fwd = copy(t, 4 + j, (*chip, c), sibling)
                fwd.start()
                passed.append(fwd)
        for t in range(2):
            copy(t, 0, sibling, me).wait_recv()
            for j, chip in enumerate(chips):
                copy(t, 4 + j, (*chip, 1 - c), me).wait_recv()
        for cp in first + passed:
            cp.wait_send()
        for cp in mine:
            cp.wait()

    return pl.pallas_call(
        body, name="all_gather_weights",
        out_shape=(jax.ShapeDtypeStruct((N_DEV,) + a.shape, a.dtype), jax.ShapeDtypeStruct((N_DEV,) + b.shape, b.dtype)),
        in_specs=[ANY, ANY], out_specs=(ANY, ANY),
        scratch_shapes=[pltpu.SemaphoreType.DMA((2, 7)), pltpu.SemaphoreType.DMA((2, 7)), pltpu.SemaphoreType.DMA((2,))],
    )(a, b)


def _exchange_sibling(pa, pb):
    def body(pa_ref, pb_ref, ra_ref, rb_ref, send_sems, recv_sems):
        x, y, c = _position()
        sibling = (x, y, 1 - c)
        copies = []
        for t, (p_ref, r_ref) in enumerate(((pa_ref, ra_ref), (pb_ref, rb_ref))):
            for j in range(4):
                copies.append(pltpu.make_async_remote_copy(
                    src_ref=p_ref.at[2 * j + 1 - c], dst_ref=r_ref.at[j], send_sem=send_sems.at[t, j],
                    recv_sem=recv_sems.at[t, j], device_id=sibling, device_id_type=MESH))
        for cp in copies:
            cp.start()
        for cp in copies:
            cp.wait_recv()
        for cp in copies:
            cp.wait_send()

    return pl.pallas_call(
        body, name="exchange_sibling",
        out_shape=(jax.ShapeDtypeStruct((4,) + pa.shape[1:], pa.dtype), jax.ShapeDtypeStruct((4,) + pb.shape[1:], pb.dtype)),
        in_specs=[ANY, ANY], out_specs=(ANY, ANY),
        scratch_shapes=[pltpu.SemaphoreType.DMA((2, 4)), pltpu.SemaphoreType.DMA((2, 4))],
    )(pa, pb)


def _exchange_chips(ta, tb):
    def body(ta_ref, tb_ref, ra_ref, rb_ref, send_sems, recv_sems, local_sems):
        x, y, c = _position()
        chips = [(1 - x, y), (x, 1 - y), (1 - x, 1 - y)]
        my_chip = 2 * x + y
        copies, own = [], []
        for t, (t_ref, r_ref) in enumerate(((ta_ref, ra_ref), (tb_ref, rb_ref))):
            own.append(pltpu.make_async_copy(t_ref.at[my_chip], r_ref.at[my_chip], local_sems.at[t]))
            for j, (px, py) in enumerate(chips):
                copies.append(pltpu.make_async_remote_copy(
                    src_ref=t_ref.at[2 * px + py], dst_ref=r_ref.at[my_chip], send_sem=send_sems.at[t, j],
                    recv_sem=recv_sems.at[t, j], device_id=(px, py, c), device_id_type=MESH))
        for cp in own + copies:
            cp.start()
        for cp in copies:
            cp.wait_recv()
        for cp in copies:
            cp.wait_send()
        for cp in own:
            cp.wait()

    return pl.pallas_call(
        body, name="exchange_chips",
        out_shape=(jax.ShapeDtypeStruct(ta.shape, ta.dtype), jax.ShapeDtypeStruct(tb.shape, tb.dtype)),
        in_specs=[ANY, ANY], out_specs=(ANY, ANY),
        scratch_shapes=[pltpu.SemaphoreType.DMA((2, 3)), pltpu.SemaphoreType.DMA((2, 3)), pltpu.SemaphoreType.DMA((2,))],
    )(ta, tb)


SMALL_ROWS = 8
ROW_LB = 4
ROW_GN = 6
ROW_LOSS = 7


def _small_allreduce_adamw(part, w, m, v, lb_logits):
    width = part.shape[1]

    def body(p_ref, w_ref, m_ref, v_ref, lb_ref, g_ref, d_ref, mo_ref, vo_ref, buf, send_sems, recv_sems):
        x, y, c = _position()
        me = 4 * x + 2 * y + c
        buf[me] = p_ref[...]
        copies = []
        for k in range(N_DEV - 1):
            bx, by, bc = ((k + 1) >> 2) & 1, ((k + 1) >> 1) & 1, (k + 1) & 1
            peer = (x ^ bx, y ^ by, c ^ bc)
            copies.append(pltpu.make_async_remote_copy(
                src_ref=p_ref, dst_ref=buf.at[me], send_sem=send_sems.at[k], recv_sem=recv_sems.at[k],
                device_id=peer, device_id_type=MESH))
        for cp in copies:
            cp.start()
        for cp in copies:
            cp.wait_recv()
        for cp in copies:
            cp.wait_send()
        tot = buf[0]
        for dev in range(1, N_DEV):
            tot = tot + buf[dev]
        lbv = lb_ref[...]
        lb = _sigmoid(lbv[0:1] - lbv[1:2])
        glb = tot[ROW_LB:ROW_LB + 1] * lb * (1.0 - lb)
        loss = jnp.sum(tot[ROW_LOSS:ROW_LOSS + 1], axis=-1, keepdims=True)
        row = lax.broadcasted_iota(jnp.int32, (SMALL_ROWS, width), 0)
        g = jnp.where(row == ROW_LB, glb, jnp.where(row == ROW_LB + 1, -glb, tot))
        g = jnp.where(row == ROW_LOSS, loss, g)
        g_ref[...] = g
        d_ref[...], mo_ref[...], vo_ref[...] = _adamw(w_ref[...], g, m_ref[...], v_ref[...])

    vm = pl.BlockSpec(memory_space=pltpu.VMEM)
    shp = jax.ShapeDtypeStruct((SMALL_ROWS, width), F32)
    return pl.pallas_call(
        body, name="small_allreduce_adamw", out_shape=(shp, shp, shp, shp),
        in_specs=[vm] * 5, out_specs=(vm, vm, vm, vm),
        scratch_shapes=[pltpu.VMEM((N_DEV, SMALL_ROWS, width), F32), pltpu.SemaphoreType.DMA((N_DEV - 1,)),
                        pltpu.SemaphoreType.DMA((N_DEV - 1,))],
    )(part, w, m, v, lb_logits)


def _pack_small(norm_gain, final_gain, lb2, gnorm, last_row, width):
    pad = lambda a: jnp.pad(a.reshape(1, -1), ((0, 0), (0, width - a.size)))
    return jnp.concatenate([norm_gain.reshape(2, width), final_gain.reshape(2, width), lb2.reshape(2, width),
                            pad(gnorm), last_row.reshape(1, width)], axis=0)


def _unpack_small(p, d, e, hd):
    return (p[0:2].reshape(1, d), p[2:4].reshape(d), p[4:6].reshape(2, e), p[6:7, :hd].reshape(1, hd))


def kernel(x, norm_gain, w_in, lb_logits, hgrn_gnorm, w_out, final_gain, loss_target, m_norm_gain, m_w_in, m_lb_logits, m_hgrn_gnorm, m_w_out, m_final_gain, v_norm_gain, v_w_in, v_lb_logits, v_hgrn_gnorm, v_w_out, v_final_gain):
    s, d = x.shape[1], x.shape[2]
    e = w_in.shape[2]
    assert d == 2 * e and lb_logits.shape == (2, e) and w_out.shape[1] * N_DEV == 2 * e
    x2d = x.reshape(s, d)
    tgt = loss_target.reshape(s, d)

    w_in_full, w_out_full = _all_gather_weights(_cast_bf16(w_in[0]), _cast_bf16(w_out[0]))
    w_out_full = w_out_full.reshape(2 * e, d)

    h = _rmsnorm_fwd(x2d, norm_gain)
    z = _inproj(h, w_in_full)
    y_h, states = _hgrn_fwd(z, lb_logits, hgrn_gnorm)
    o_attn, lse, y_a = _attn_fwd(z)
    dx2, dx2b, dy, loss_vec, dfg = _outproj_loss(x2d, y_h, y_a, w_out_full, final_gain.reshape(1, d), tgt)

    pwo = _dwout(y_h, y_a, dx2b)
    dza = _attn_bwd(z, dy, o_attn, lse)
    dzh, dlb, dgn = _hgrn_bwd(z, dy, states, lb_logits, hgrn_gnorm)
    grad_x, dng = _dh_dx(dzh, dza, w_in_full, x2d, norm_gain, dx2)
    pwi = _dwin(h, dzh, dza)

    c_idx = lax.axis_index("c").astype(jnp.int32).reshape(1)
    pwo8 = pwo.reshape(N_DEV, w_out.shape[1], d)
    ra_i, ra_o = _exchange_sibling(pwi, pwo8)
    rb_i, rb_o = _exchange_chips(_pair_add(pwi, ra_i, c_idx), _pair_add(pwo8, ra_o, c_idx))
    g_wi, d_wi, nm_wi, nv_wi = _sum_adamw(rb_i, w_in[0], m_w_in[0], v_w_in[0])
    g_wo, d_wo, nm_wo, nv_wo = _sum_adamw(rb_o, w_out[0], m_w_out[0], v_w_out[0])

    width = d // 2
    zero_row = jnp.zeros((1, width), F32)
    loss_row = loss_vec[:, :width] + loss_vec[:, width:]
    part = _pack_small(dng, dfg, jnp.concatenate([dlb, zero_row], axis=0), dgn, loss_row, width)
    pw = _pack_small(norm_gain, final_gain, lb_logits, hgrn_gnorm, zero_row, width)
    pm = _pack_small(m_norm_gain, m_final_gain, m_lb_logits, m_hgrn_gnorm, zero_row, width)
    pv = _pack_small(v_norm_gain, v_final_gain, v_lb_logits, v_hgrn_gnorm, zero_row, width)
    sg, sd, sm, sv = _small_allreduce_adamw(part, pw, pm, pv, lb_logits)
    hd = hgrn_gnorm.shape[1]
    g_ng, g_fg, g_lb, g_gn = _unpack_small(sg, d, e, hd)
    d_ng, d_fg, d_lb, d_gn = _unpack_small(sd, d, e, hd)
    m_ng, m_fg, m_lb, m_gn = _unpack_small(sm, d, e, hd)
    v_ng, v_fg, v_lb, v_gn = _unpack_small(sv, d, e, hd)
    loss = sg[ROW_LOSS, 0]

    one = lambda a: a[None]
    return (loss, grad_x.reshape(1, s, d), g_ng, one(g_wi), g_lb, g_gn, one(g_wo), g_fg,
            d_ng, one(d_wi), d_lb, d_gn, one(d_wo), d_fg,
            m_ng, one(nm_wi), m_lb, m_gn, one(nm_wo), m_fg,
            v_ng, one(nv_wi), v_lb, v_gn, one(nv_wo), v_fg)
```

```python
import functools
import math

import jax
import jax.numpy as jnp
from jax import lax
from jax.experimental import pallas as pl
from jax.experimental.pallas import tpu as pltpu

NORM_EPS = 1e-6
HGRN_HEAD = 128
HGRN_CHUNK = 64
ATTN_HEAD = 64
ATTN_BAND = 128
DILATIONS = (1, 4, 16)
N_SPLITS = 8
N_DEV = 8
ADAM_LR = 0.001
ADAM_B1 = 0.9
ADAM_B2 = 0.999
ADAM_EPS = 1e-08
ADAM_WD = 0.01
ADAM_STEP = 10
LANES = 128
MESH = pl.DeviceIdType.MESH
F32 = jnp.float32
BF16 = jnp.bfloat16
NEG_BIG = -1e30
VMEM_LIMIT = 56 * 1024 * 1024

ANY = pl.BlockSpec(memory_space=pl.ANY)


def _params(*sem):
    return pltpu.CompilerParams(dimension_semantics=sem, vmem_limit_bytes=VMEM_LIMIT)


def _tile(n, pref):
    t = min(n, pref)
    assert n % t == 0, (n, pref)
    return t


def _dot(a, b, precision=None):
    return jnp.dot(a, b, preferred_element_type=F32, precision=precision)


def _dot_nt(a, b):
    return lax.dot_general(a, b, (((1,), (1,)), ((), ())), preferred_element_type=F32)


def _dot_tn(a, b):
    return lax.dot_general(a, b, (((0,), (0,)), ((), ())), preferred_element_type=F32)


def _sigmoid(x):
    return 1.0 / (1.0 + jnp.exp(-x))


def _dsilu(x, s):
    return s * (1.0 + x * (1.0 - s))


def _adamw(w, g, m, v):
    m = ADAM_B1 * m + (1.0 - ADAM_B1) * g
    v = ADAM_B2 * v + (1.0 - ADAM_B2) * (g * g)
    m_hat = m / (1.0 - ADAM_B1 ** ADAM_STEP)
    v_hat = v / (1.0 - ADAM_B2 ** ADAM_STEP)
    delta = -ADAM_LR * (m_hat / (jnp.sqrt(v_hat) + ADAM_EPS) + ADAM_WD * w)
    return delta, m, v


def _cast_bf16(a):
    r, c = a.shape
    tr = _tile(r, 256)

    def body(a_ref, o_ref):
        o_ref[...] = a_ref[...].astype(BF16)

    return pl.pallas_call(
        body, name="cast_bf16", grid=(r // tr,), out_shape=jax.ShapeDtypeStruct((r, c), BF16),
        in_specs=[pl.BlockSpec((tr, c), lambda i: (i, 0))], out_specs=pl.BlockSpec((tr, c), lambda i: (i, 0)),
        compiler_params=_params("parallel"))(a)


def _rmsnorm_fwd(x, gain):
    s, d = x.shape
    tm = _tile(s, 512)

    def body(x_ref, g_ref, h_ref):
        xv = x_ref[...]
        r = lax.rsqrt(jnp.mean(xv * xv, axis=-1, keepdims=True) + NORM_EPS)
        h_ref[...] = (xv * r * g_ref[...]).astype(BF16)

    return pl.pallas_call(
        body, name="rmsnorm_fwd", grid=(s // tm,), out_shape=jax.ShapeDtypeStruct((s, d), BF16),
        in_specs=[pl.BlockSpec((tm, d), lambda i: (i, 0)), pl.BlockSpec((1, d), lambda i: (0, 0))],
        out_specs=pl.BlockSpec((tm, d), lambda i: (i, 0)), compiler_params=_params("parallel"))(x, gain)


def _inproj(h, w_full):
    s, d = h.shape
    e = w_full.shape[2]
    tm = _tile(s, 512)

    def body(h_ref, w_ref, z_ref):
        z_ref[...] = _dot(h_ref[...], w_ref[...])

    return pl.pallas_call(
        body, name="inproj", grid=(s // tm, N_SPLITS),
        out_shape=jax.ShapeDtypeStruct((s, N_SPLITS * e), F32),
        in_specs=[pl.BlockSpec((tm, d), lambda i, j: (i, 0)), pl.BlockSpec((None, d, e), lambda i, j: (j, 0, 0))],
        out_specs=pl.BlockSpec((tm, e), lambda i, j: (i, j)),
        compiler_params=_params("parallel", "arbitrary"))(h, w_full)


HGRN_BLOCK = 1024
TRI_ROWS = 256


def _chunk_masks():
    tb = TRI_ROWS
    row = lax.broadcasted_iota(jnp.int32, (tb, tb), 0)
    col = lax.broadcasted_iota(jnp.int32, (tb, tb), 1)
    same = (row // HGRN_CHUNK) == (col // HGRN_CHUNK)
    lower = jnp.where(same & (col <= row), 1.0, 0.0).astype(BF16)
    upper = jnp.where(same & (col >= row), 1.0, 0.0).astype(BF16)
    return lower, upper


def _split3(a):
    hi = a.astype(BF16).astype(F32)
    mid = (a - hi).astype(BF16).astype(F32)
    lo = (a - hi - mid).astype(BF16).astype(F32)
    return hi, mid, lo


def _tri_dot(tri, x):
    hi, mid, lo = (p.astype(BF16) for p in _split3(x))
    outs = []
    for r in range(0, x.shape[0], TRI_ROWS):
        sl = slice(r, r + TRI_ROWS)
        outs.append(_dot(tri, hi[sl]) + _dot(tri, mid[sl]) + _dot(tri, lo[sl]))
    return outs[0] if len(outs) == 1 else jnp.concatenate(outs, axis=0)


def _hgrn_gates(qp, fp, lbv):
    lb = _sigmoid(lbv[0:1] - lbv[1:2])
    sq = _sigmoid(qp)
    q = qp * sq
    sg = _sigmoid(fp)
    f = lb + (1.0 - lb) * sg
    k = 1.0 - f
    return lb, sq, q, sg, f, k


def _hgrn_fwd(z, lb_logits, gnorm):
    s = z.shape[0]
    e = z.shape[1] // N_SPLITS
    nh = e // HGRN_HEAD
    tb = _tile(s, HGRN_BLOCK)
    nc = tb // HGRN_CHUNK
    nb = s // tb
    C = HGRN_CHUNK

    def body(q_ref, f_ref, i_ref, g_ref, lb_ref, gn_ref, y_ref, st_ref, state, o_scr):
        @pl.when(pl.program_id(1) == 0)
        def _():
            state[...] = jnp.zeros_like(state)

        lb, sq, q, sg, f, k = _hgrn_gates(q_ref[...], f_ref[...], lb_ref[...])
        lower, _ = _chunk_masks()
        b = _tri_dot(lower, jnp.log(f))
        b3 = b.reshape(nc, C, HGRN_HEAD)
        bc = b3[:, C - 1:C, :]
        qt = (q * jnp.exp(b)).astype(BF16)
        kt = (k * jnp.exp(-b)).astype(BF16)
        ke = (k.reshape(nc, C, HGRN_HEAD) * jnp.exp(bc - b3)).reshape(tb, HGRN_HEAD).astype(BF16)
        v = i_ref[...].astype(BF16)
        tri = lax.broadcasted_iota(jnp.int32, (C, C), 1) <= lax.broadcasted_iota(jnp.int32, (C, C), 0)
        sls = [slice(c * C, (c + 1) * C) for c in range(nc)]
        kv = [_dot_tn(v[sl], ke[sl]) for sl in sls]
        a = [jnp.where(tri, _dot_nt(qt[sl], kt[sl]), 0.0).astype(BF16) for sl in sls]
        st = state[...]
        sts = []
        for c in range(nc):
            sts.append(st)
            st_ref[c] = st
            st = st * jnp.exp(bc[c]) + kv[c]
        state[...] = st
        for c, sl in enumerate(sls):
            o_scr[sl, :] = _dot(a[c], v[sl]) + _dot_nt(qt[sl], sts[c].astype(BF16))
        o = o_scr[...]
        rms = lax.rsqrt(jnp.mean(o * o, axis=-1, keepdims=True) + NORM_EPS)
        gp = g_ref[...]
        y_ref[...] = (o * rms * gn_ref[...] * (gp * _sigmoid(gp))).astype(BF16)

    col = lambda kk: (lambda h, n: (n, kk * nh + h))
    return pl.pallas_call(
        body, name="hgrn_fwd", grid=(nh, nb),
        out_shape=(jax.ShapeDtypeStruct((s, e), BF16),
                   jax.ShapeDtypeStruct((nh, s // C, HGRN_HEAD, HGRN_HEAD), F32)),
        in_specs=[pl.BlockSpec((tb, HGRN_HEAD), col(0)), pl.BlockSpec((tb, HGRN_HEAD), col(1)),
                  pl.BlockSpec((tb, HGRN_HEAD), col(2)), pl.BlockSpec((tb, HGRN_HEAD), col(3)),
                  pl.BlockSpec((2, HGRN_HEAD), lambda h, n: (0, h)), pl.BlockSpec((1, HGRN_HEAD), lambda h, n: (0, 0))],
        out_specs=(pl.BlockSpec((tb, HGRN_HEAD), lambda h, n: (n, h)),
                   pl.BlockSpec((None, nc, HGRN_HEAD, HGRN_HEAD), lambda h, n: (h, n, 0, 0))),
        scratch_shapes=[pltpu.VMEM((HGRN_HEAD, HGRN_HEAD), F32), pltpu.VMEM((tb, HGRN_HEAD), F32)],
        compiler_params=_params("parallel", "arbitrary"))(z, z, z, z, lb_logits, gnorm)


def _hgrn_bwd(z, dy, states, lb_logits, gnorm):
    s = z.shape[0]
    e = z.shape[1] // N_SPLITS
    nh = e // HGRN_HEAD
    tb = _tile(s, HGRN_BLOCK)
    nc = tb // HGRN_CHUNK
    nb = s // tb
    C = HGRN_CHUNK
    H = HGRN_HEAD

    def body(q_ref, f_ref, i_ref, g_ref, dy_ref, st_ref, lb_ref, gn_ref, dz_ref, dlb_ref, dgn_ref,
             gstate, o_scr, dq_scr, dk_scr, dv_scr, e_scr):
        first = (pl.program_id(0) == 0) & (pl.program_id(1) == 0)

        @pl.when(first)
        def _():
            dgn_ref[...] = jnp.zeros_like(dgn_ref)

        @pl.when(pl.program_id(1) == 0)
        def _():
            gstate[...] = jnp.zeros_like(gstate)
            dlb_ref[...] = jnp.zeros_like(dlb_ref)

        qp = q_ref[...]
        lb, sq, q, sg, f, k = _hgrn_gates(qp, f_ref[...], lb_ref[...])
        lower, upper = _chunk_masks()
        b = _tri_dot(lower, jnp.log(f))
        b3 = b.reshape(nc, C, H)
        bc = b3[:, C - 1:C, :]
        eb = jnp.exp(b)
        enb = jnp.exp(-b)
        eend = jnp.exp(bc - b3).reshape(tb, H)
        qt = (q * eb).astype(BF16)
        kt = (k * enb).astype(BF16)
        ke = (k * eend).astype(BF16)
        v = i_ref[...].astype(BF16)
        tri = lax.broadcasted_iota(jnp.int32, (C, C), 1) <= lax.broadcasted_iota(jnp.int32, (C, C), 0)
        sls = [slice(c * C, (c + 1) * C) for c in range(nc)]
        a = [jnp.where(tri, _dot_nt(qt[sl], kt[sl]), 0.0).astype(BF16) for sl in sls]
        for c, sl in enumerate(sls):
            o_scr[sl, :] = _dot(a[c], v[sl]) + _dot_nt(qt[sl], st_ref[c].astype(BF16))
        o = o_scr[...]
        rms = lax.rsqrt(jnp.mean(o * o, axis=-1, keepdims=True) + NORM_EPS)
        on = o * rms
        gn = gn_ref[...]
        gp = g_ref[...]
        sgg = _sigmoid(gp)
        dyv = dy_ref[...]
        d_on = dyv * (gp * sgg)
        dz_ref[3] = (dyv * on * gn * _dsilu(gp, sgg)).astype(BF16)
        dgn_ref[...] += jnp.sum(d_on * on, axis=0, keepdims=True)
        u = d_on * gn
        do = (rms * (u - on * jnp.mean(u * on, axis=-1, keepdims=True))).astype(BF16)
        gup = [_dot_tn(do[sl], qt[sl]) for sl in sls]
        da = [jnp.where(tri, _dot_nt(do[sl], v[sl]), 0.0).astype(BF16) for sl in sls]
        gt = gstate[...]
        gts = [None] * nc
        for c in reversed(range(nc)):
            gts[c] = gt
            gt = gt * jnp.exp(bc[c]) + gup[c]
        gstate[...] = gt
        for c, sl in enumerate(sls):
            stp = st_ref[c]
            gtb = gts[c].astype(BF16)
            dqt = _dot(da[c], kt[sl]) + _dot(do[sl], stp.astype(BF16))
            dkt = _dot_tn(da[c], qt[sl])
            dks = _dot(v[sl], gtb) * eend[sl]
            dv_scr[sl, :] = _dot_tn(a[c], do[sl]) + _dot_nt(ke[sl], gtb)
            dq_scr[sl, :] = dqt * eb[sl]
            dk_scr[sl, :] = dkt * enb[sl] + dks
            ech = (jnp.sum(k[sl] * dks, axis=0, keepdims=True)
                   + jnp.sum(gts[c] * jnp.exp(bc[c]) * stp, axis=0, keepdims=True))
            e_scr[sl, :] = jnp.broadcast_to(ech, (C, H))
        dq = dq_scr[...]
        dk = dk_scr[...]
        dlf = _tri_dot(upper, q * dq - k * dk) + e_scr[...]
        dft = dlf / f - dk
        dz_ref[0] = (dq * _dsilu(qp, sq)).astype(BF16)
        dz_ref[1] = (dft * (1.0 - lb) * sg * (1.0 - sg)).astype(BF16)
        dz_ref[2] = dv_scr[...].astype(BF16)
        dlb_ref[...] += jnp.sum(dft * (1.0 - sg), axis=0, keepdims=True)

    col = lambda kk: (lambda h, n: (nb - 1 - n, kk * nh + h))
    return pl.pallas_call(
        body, name="hgrn_bwd", grid=(nh, nb),
        out_shape=(jax.ShapeDtypeStruct((4, s, e), BF16), jax.ShapeDtypeStruct((1, e), F32),
                   jax.ShapeDtypeStruct((1, H), F32)),
        in_specs=[pl.BlockSpec((tb, H), col(0)), pl.BlockSpec((tb, H), col(1)),
                  pl.BlockSpec((tb, H), col(2)), pl.BlockSpec((tb, H), col(3)),
                  pl.BlockSpec((tb, H), lambda h, n: (nb - 1 - n, h)),
                  pl.BlockSpec((None, nc, H, H), lambda h, n: (h, nb - 1 - n, 0, 0)),
                  pl.BlockSpec((2, H), lambda h, n: (0, h)), pl.BlockSpec((1, H), lambda h, n: (0, 0))],
        out_specs=(pl.BlockSpec((4, tb, H), lambda h, n: (0, nb - 1 - n, h)),
                   pl.BlockSpec((1, H), lambda h, n: (0, h)), pl.BlockSpec((1, H), lambda h, n: (0, 0))),
        scratch_shapes=[pltpu.VMEM((H, H), F32)] + [pltpu.VMEM((tb, H), F32)] * 5,
        compiler_params=_params("arbitrary", "arbitrary"))(z, z, z, z, dy, states, lb_logits, gnorm)


ATTN_T = 16 * ATTN_BAND
SCALE = ATTN_HEAD ** -0.5
TILE_UNROLL = 2


def _slope(hh, nheads):
    head = (2 * pl.program_id(0) + hh + 1).astype(F32)
    return jnp.exp(jnp.full((1, 1), -8.0 / nheads * math.log(2.0), F32) * head)


def _fill_bias(bias, nheads, delta, edge_ok):
    band = (delta >= 0) & (delta <= ATTN_BAND)
    dist = delta.astype(F32)
    for pi, dil in enumerate(DILATIONS):
        for hh in range(2):
            full = jnp.where(band, -(_slope(hh, nheads) * float(dil)) * dist, NEG_BIG)
            bias[(pi * 2 + hh) * 2] = full
            bias[(pi * 2 + hh) * 2 + 1] = jnp.where(edge_ok, full, NEG_BIG)


def _rows(start, size, stride):
    if stride == 1:
        return pl.ds(pl.multiple_of(start, ATTN_BAND), size)
    return pl.ds(start, size, stride=stride)


def _head_lanes(rows, hh):
    return (lax.broadcasted_iota(jnp.int32, (rows, LANES), 1) // ATTN_HEAD) == hh


def _attn_fwd(z):
    s = z.shape[0]
    e = z.shape[1] // N_SPLITS
    npair = e // LANES
    T = ATTN_T
    assert s % T == 0
    nsb = s // T
    W = ATTN_BAND
    nt = T // W
    HD = ATTN_HEAD
    chunk = 256

    def body(q_ref, kp_ref, kc_ref, vp_ref, vc_ref, g_ref, o_ref, l_ref, y_ref, qa, kbuf, va, bias, accs, ms, msw, lsw):
        sb = pl.program_id(1)
        for hh in range(2):
            def stage(i, carry):
                rows = pl.ds(pl.multiple_of(i * chunk, chunk), chunk)
                mine = _head_lanes(chunk, hh)
                qa[hh, rows, :] = jnp.where(mine, q_ref[rows, :] * SCALE, 0.0)
                va[hh, rows, :] = jnp.where(mine, vp_ref[rows, :], 1.0)
                va[hh, pl.ds(pl.multiple_of(T + i * chunk, chunk), chunk), :] = jnp.where(mine, vc_ref[rows, :], 1.0)
                return carry

            lax.fori_loop(0, T // chunk, stage, 0)
        kbuf[0:T, :] = kp_ref[...]
        kbuf[T:, :] = kc_ref[...]
        qi = lax.broadcasted_iota(jnp.int32, (W, 2 * W), 0)
        kj = lax.broadcasted_iota(jnp.int32, (W, 2 * W), 1)
        _fill_bias(bias, 2 * npair, W + qi - kj, kj >= W)

        def tile(tau, carry):
            first = _head_lanes(W, 0)
            rows, scores = [], []
            for pi, dil in enumerate(DILATIONS):
                r = tau % dil
                ub = tau // dil
                qrows = _rows(r + dil * W * ub, W, dil)
                krows = _rows(T + dil * W * (ub - 1) + r, 2 * W, dil)
                var = jnp.where((sb == 0) & (ub == 0), 1, 0)
                kt = kbuf[krows, :].astype(BF16)
                rows.append((qrows, krows))
                scores.append([_dot_nt(qa[hh, qrows, :].astype(BF16), kt) + bias[(pi * 2 + hh) * 2 + var]
                               for hh in range(2)])
            maxes = [[jnp.max(sc, axis=-1, keepdims=True) for sc in pair] for pair in scores]
            probs = [[jnp.exp(sc - m).astype(BF16) for sc, m in zip(ps, pm)] for ps, pm in zip(scores, maxes)]
            for pi, (qrows, krows) in enumerate(rows):
                outs = [_dot(probs[pi][hh], va[hh, krows, :].astype(BF16)) for hh in range(2)]
                accs[pi, qrows, :] = jnp.where(first, outs[0], outs[1])
                lsw[pi, qrows, :] = jnp.where(first, outs[1], outs[0])
                ms[pi, qrows, :] = jnp.where(first, maxes[pi][0], maxes[pi][1])
                msw[pi, qrows, :] = jnp.where(first, maxes[pi][1], maxes[pi][0])
            return carry

        lax.fori_loop(0, nt, tile, 0, unroll=TILE_UNROLL)

        def merge(i, carry):
            rows = pl.ds(pl.multiple_of(i * chunk, chunk), chunk)
            m1, m2, m3 = ms[0, rows, :], ms[1, rows, :], ms[2, rows, :]
            mx = jnp.maximum(jnp.maximum(m1, m2), m3)
            s1, s2, s3 = msw[0, rows, :], msw[1, rows, :], msw[2, rows, :]
            sx = jnp.maximum(jnp.maximum(s1, s2), s3)
            den_sw = (jnp.exp(s1 - sx) * lsw[0, rows, :] + jnp.exp(s2 - sx) * lsw[1, rows, :]
                      + jnp.exp(s3 - sx) * lsw[2, rows, :])
            den = pltpu.roll(den_sw, ATTN_HEAD, 1)
            o = (jnp.exp(m1 - mx) * accs[0, rows, :] + jnp.exp(m2 - mx) * accs[1, rows, :]
                 + jnp.exp(m3 - mx) * accs[2, rows, :]) / den
            o_ref[rows, :] = o
            l_ref[rows, :] = mx + jnp.log(den)
            gp = g_ref[rows, :]
            y_ref[rows, :] = (o * (gp * _sigmoid(gp))).astype(BF16)
            return carry

        lax.fori_loop(0, T // chunk, merge, 0)

    cur = lambda split: (lambda hp, sb: (sb, split * npair + hp))
    prev = lambda split: (lambda hp, sb: (jnp.maximum(sb - 1, 0), split * npair + hp))
    blk = lambda index: pl.BlockSpec((T, LANES), index)
    out = blk(lambda hp, sb: (sb, hp))
    buf = lambda rows: pltpu.VMEM((rows, LANES), F32)
    return pl.pallas_call(
        body, name="attn_fwd", grid=(npair, nsb),
        out_shape=(jax.ShapeDtypeStruct((s, e), F32), jax.ShapeDtypeStruct((s, e), F32), jax.ShapeDtypeStruct((s, e), BF16)),
        in_specs=[blk(cur(4)), blk(prev(5)), blk(cur(5)), blk(prev(6)), blk(cur(6)), blk(cur(7))],
        out_specs=(out, out, out),
        scratch_shapes=[pltpu.VMEM((2, T, LANES), F32), buf(2 * T), pltpu.VMEM((2, 2 * T, LANES), F32),
                        pltpu.VMEM((12, W, 2 * W), F32)] + [pltpu.VMEM((3, T, LANES), F32)] * 4,
        compiler_params=_params("parallel", "arbitrary"))(z, z, z, z, z, z)


def _outproj_loss(x, y_h, y_a, w_out_full, final_gain, target):
    s, d = x.shape
    e = y_h.shape[1]
    tm = _tile(s, 256)

    def body(x_ref, yh_ref, ya_ref, w_ref, g_ref, t_ref, dx_ref, dxb_ref, dy_ref, loss_ref, dg_ref):
        @pl.when(pl.program_id(0) == 0)
        def _():
            loss_ref[...] = jnp.zeros_like(loss_ref)
            dg_ref[...] = jnp.zeros_like(dg_ref)

        w = w_ref[...]
        x2 = x_ref[...] + _dot(yh_ref[...], w[0:e]) + _dot(ya_ref[...], w[e:2 * e])
        r = lax.rsqrt(jnp.mean(x2 * x2, axis=-1, keepdims=True) + NORM_EPS)
        xn = x2 * r
        g = g_ref[...]
        err = xn * g - t_ref[...]
        loss_ref[...] += jnp.sum(err * err, axis=0, keepdims=True) * (0.5 / d)
        dyo = err * (1.0 / d)
        dg_ref[...] += jnp.sum(dyo * xn, axis=0, keepdims=True)
        u = dyo * g
        dx2 = r * (u - xn * jnp.mean(u * xn, axis=-1, keepdims=True))
        dx_ref[...] = dx2
        dxb = dx2.astype(BF16)
        dxb_ref[...] = dxb
        dy_ref[...] = _dot_nt(dxb, w)

    row = pl.BlockSpec((tm, d), lambda i: (i, 0))
    half = pl.BlockSpec((tm, e), lambda i: (i, 0))
    vec = pl.BlockSpec((1, d), lambda i: (0, 0))
    return pl.pallas_call(
        body, name="outproj_loss", grid=(s // tm,),
        out_shape=(jax.ShapeDtypeStruct((s, d), F32), jax.ShapeDtypeStruct((s, d), BF16),
                   jax.ShapeDtypeStruct((s, 2 * e), F32), jax.ShapeDtypeStruct((1, d), F32),
                   jax.ShapeDtypeStruct((1, d), F32)),
        in_specs=[row, half, half, pl.BlockSpec((2 * e, d), lambda i: (0, 0)), vec, row],
        out_specs=(row, row, pl.BlockSpec((tm, 2 * e), lambda i: (i, 0)), vec, vec),
        compiler_params=_params("arbitrary"))(x, y_h, y_a, w_out_full, final_gain, target)


def _dwout(y_h, y_a, dxb):
    s, e = y_h.shape
    d = dxb.shape[1]
    ts = _tile(s, 512)
    ns = s // ts

    def body(yh_ref, ya_ref, dx_ref, o_ref, acc):
        half = pl.program_id(0)
        step = pl.program_id(1)

        @pl.when(step == 0)
        def _():
            acc[...] = jnp.zeros_like(acc)

        @pl.when(half == 0)
        def _():
            acc[...] += _dot_tn(yh_ref[...], dx_ref[...])

        @pl.when(half == 1)
        def _():
            acc[...] += _dot_tn(ya_ref[...], dx_ref[...])

        @pl.when(step == ns - 1)
        def _():
            o_ref[...] = acc[...].astype(BF16)

    return pl.pallas_call(
        body, name="dwout", grid=(2, ns), out_shape=jax.ShapeDtypeStruct((2 * e, d), BF16),
        in_specs=[pl.BlockSpec((ts, e), lambda hf, k: (k * (1 - hf), 0)), pl.BlockSpec((ts, e), lambda hf, k: (k * hf, 0)),
                  pl.BlockSpec((ts, d), lambda hf, k: (k, 0))],
        out_specs=pl.BlockSpec((e, d), lambda hf, k: (hf, 0)),
        scratch_shapes=[pltpu.VMEM((e, d), F32)],
        compiler_params=_params("parallel", "arbitrary"))(y_h, y_a, dxb)


def _attn_bwd(z, dy, o, lse):
    s, e = o.shape
    npair = e // LANES
    T = ATTN_T
    assert s % T == 0
    nsb = s // T
    W = ATTN_BAND
    nt = T // W
    HD = ATTN_HEAD
    chunk = 256

    def body(k_ref, v_ref, qc_ref, qn_ref, dyc_ref, dyn_ref, gc_ref, gn_ref, oc_ref, on_ref, lc_ref, ln_ref,
             dz_ref, qa, doa, ka, va, dqacc, dkacc, dvacc, bias):
        sb = pl.program_id(1)
        halves = ((qc_ref, dyc_ref, gc_ref, oc_ref, lc_ref), (qn_ref, dyn_ref, gn_ref, on_ref, ln_ref))
        for half, (q_r, dy_r, g_r, o_r, l_r) in enumerate(halves):
            def stage(i, carry):
                rows = pl.ds(pl.multiple_of(i * chunk, chunk), chunk)
                dst = pl.ds(pl.multiple_of(half * T + i * chunk, chunk), chunk)
                lane = lax.broadcasted_iota(jnp.int32, (chunk, LANES), 1)
                gp = g_r[rows, :]
                sg = _sigmoid(gp)
                dyv = dy_r[rows, :]
                ov = o_r[rows, :]
                dov = dyv * (gp * sg)
                prod = dov * ov
                qv = q_r[rows, :] * SCALE
                lv = l_r[rows, :]
                for hh in range(2):
                    mine = _head_lanes(chunk, hh)
                    spare = (1 - hh) * HD
                    lse_parts = _split3(lv[:, hh * HD:hh * HD + 1])
                    dl_parts = _split3(jnp.sum(prod[:, hh * HD:(hh + 1) * HD], axis=-1, keepdims=True))
                    qh = jnp.where(mine, qv, 0.0)
                    dh = jnp.where(mine, dov, 0.0)
                    for j in range(3):
                        qh = jnp.where(lane == spare + j, lse_parts[j], qh)
                        dh = jnp.where(lane == spare + j, dl_parts[j], dh)
                    qa[hh, dst, :] = qh
                    doa[hh, dst, :] = dh
                    if half == 0:
                        minus = (lane >= spare) & (lane < spare + 3)
                        ka[hh, rows, :] = jnp.where(minus, -1.0, k_ref[rows, :])
                        va[hh, rows, :] = jnp.where(minus, -1.0, v_ref[rows, :])
                if half == 0:
                    dz_ref[3, rows, :] = (dyv * ov * _dsilu(gp, sg)).astype(BF16)
                return carry

            lax.fori_loop(0, T // chunk, stage, 0)

        @pl.when(sb == 0)
        def _():
            dqacc[0:T, :] = jnp.zeros((T, LANES), F32)

        dqacc[T:, :] = jnp.zeros((T, LANES), F32)
        dkacc[...] = jnp.zeros_like(dkacc)
        dvacc[...] = jnp.zeros_like(dvacc)
        qi = lax.broadcasted_iota(jnp.int32, (2 * W, W), 0)
        kj = lax.broadcasted_iota(jnp.int32, (2 * W, W), 1)
        _fill_bias(bias, 2 * npair, qi - kj, qi < W)

        def tile(tau, carry):
            rows, ops, sc, dpd = [], [], [], []
            for pi, dil in enumerate(DILATIONS):
                r = tau % dil
                ub = tau // dil
                start = r + dil * W * ub
                krows = _rows(start, W, dil)
                qrows = _rows(start, 2 * W, dil)
                var = jnp.where((sb == nsb - 1) & (ub == nt // dil - 1), 1, 0)
                rows.append((krows, qrows))
                for hh in range(2):
                    kt = ka[hh, krows, :].astype(BF16)
                    vt = va[hh, krows, :].astype(BF16)
                    qt = qa[hh, qrows, :].astype(BF16)
                    dt = doa[hh, qrows, :].astype(BF16)
                    ops.append((kt, qt, dt))
                    sc.append(_dot_nt(qt, kt) + bias[(pi * 2 + hh) * 2 + var])
                    dpd.append(_dot_nt(dt, vt))
            ps = [jnp.exp(s) for s in sc]
            dss = [(p * d).astype(BF16) for p, d in zip(ps, dpd)]
            pbs = [p.astype(BF16) for p in ps]
            dvs = [_dot_tn(pb, dt) for pb, (kt, qt, dt) in zip(pbs, ops)]
            dks = [_dot_tn(ds, qt) for ds, (kt, qt, dt) in zip(dss, ops)]
            dqs = [_dot(ds, kt) for ds, (kt, qt, dt) in zip(dss, ops)]
            for pi, (krows, qrows) in enumerate(rows):
                dkacc[krows, :] += jnp.where(_head_lanes(W, 0), dks[2 * pi], dks[2 * pi + 1])
                dvacc[krows, :] += jnp.where(_head_lanes(W, 0), dvs[2 * pi], dvs[2 * pi + 1])
                dqacc[qrows, :] += jnp.where(_head_lanes(2 * W, 0), dqs[2 * pi], dqs[2 * pi + 1]) * SCALE
            return carry

        lax.fori_loop(0, nt, tile, 0, unroll=TILE_UNROLL)

        def flush(i, carry):
            rows = pl.ds(pl.multiple_of(i * chunk, chunk), chunk)
            nxt = pl.ds(pl.multiple_of(T + i * chunk, chunk), chunk)
            dz_ref[0, rows, :] = dqacc[rows, :].astype(BF16)
            dz_ref[1, rows, :] = dkacc[rows, :].astype(BF16)
            dz_ref[2, rows, :] = dvacc[rows, :].astype(BF16)
            dqacc[rows, :] = dqacc[nxt, :]
            return carry

        lax.fori_loop(0, T // chunk, flush, 0)

    zc = lambda split: (lambda hp, sb: (sb, split * npair + hp))
    zn = lambda split: (lambda hp, sb: (jnp.minimum(sb + 1, nsb - 1), split * npair + hp))
    ec = lambda off: (lambda hp, sb: (sb, off + hp))
    en = lambda off: (lambda hp, sb: (jnp.minimum(sb + 1, nsb - 1), off + hp))
    blk = lambda index: pl.BlockSpec((T, LANES), index)
    buf = lambda rows: pltpu.VMEM((rows, LANES), F32)
    return pl.pallas_call(
        body, name="attn_bwd", grid=(npair, nsb), out_shape=jax.ShapeDtypeStruct((4, s, e), BF16),
        in_specs=[blk(zc(5)), blk(zc(6)), blk(zc(4)), blk(zn(4)), blk(ec(npair)), blk(en(npair)),
                  blk(zc(7)), blk(zn(7)), blk(ec(0)), blk(en(0)), blk(ec(0)), blk(en(0))],
        out_specs=pl.BlockSpec((4, T, LANES), lambda hp, sb: (0, sb, hp)),
        scratch_shapes=[pltpu.VMEM((2, 2 * T, LANES), F32), pltpu.VMEM((2, 2 * T, LANES), F32),
                        pltpu.VMEM((2, T, LANES), F32), pltpu.VMEM((2, T, LANES), F32),
                        buf(2 * T), buf(T), buf(T), pltpu.VMEM((12, 2 * W, W), F32)],
        compiler_params=_params("parallel", "arbitrary"))(z, z, z, z, dy, dy, z, z, o, o, lse, lse)


def _dz_specs(tm, e, axis):
    def mk(lo, hi):
        def index(i, k):
            row, grp = (i, k) if axis == 1 else (k, i)
            return (jnp.clip(grp - lo, 0, hi - lo - 1), row, 0)
        return pl.BlockSpec((None, tm, e), index)
    return [mk(0, 4), mk(4, 8)]


def _dz_pick(grp, dzh_ref, dza_ref, fn):
    @pl.when(grp < 4)
    def _():
        fn(dzh_ref[...])

    @pl.when(grp >= 4)
    def _():
        fn(dza_ref[...])


def _dh_dx(dzh, dza, w_full, x, gain, dx2):
    s, d = x.shape
    e = dzh.shape[2]
    tm = _tile(s, 512)

    def body(dzh_ref, dza_ref, w_ref, x_ref, g_ref, dx2_ref, gx_ref, dg_ref, acc):
        i, k = pl.program_id(0), pl.program_id(1)

        @pl.when((i == 0) & (k == 0))
        def _():
            dg_ref[...] = jnp.zeros_like(dg_ref)

        @pl.when(k == 0)
        def _():
            acc[...] = jnp.zeros_like(acc)

        def add(dz):
            acc[...] += _dot_nt(dz, w_ref[...])

        _dz_pick(k, dzh_ref, dza_ref, add)

        @pl.when(k == N_SPLITS - 1)
        def _():
            dh = acc[...]
            xv = x_ref[...]
            r = lax.rsqrt(jnp.mean(xv * xv, axis=-1, keepdims=True) + NORM_EPS)
            xn = xv * r
            dg_ref[...] += jnp.sum(dh * xn, axis=0, keepdims=True)
            u = dh * g_ref[...]
            gx_ref[...] = dx2_ref[...] + r * (u - xn * jnp.mean(u * xn, axis=-1, keepdims=True))

    row = pl.BlockSpec((tm, d), lambda i, k: (i, 0))
    vec = pl.BlockSpec((1, d), lambda i, k: (0, 0))
    return pl.pallas_call(
        body, name="dh_dx", grid=(s // tm, N_SPLITS),
        out_shape=(jax.ShapeDtypeStruct((s, d), F32), jax.ShapeDtypeStruct((1, d), F32)),
        in_specs=_dz_specs(tm, e, 1) + [pl.BlockSpec((None, d, e), lambda i, k: (k, 0, 0)), row, vec, row],
        out_specs=(row, vec), scratch_shapes=[pltpu.VMEM((tm, d), F32)],
        compiler_params=_params("arbitrary", "arbitrary"))(dzh, dza, w_full, x, gain, dx2)


def _dwin(h, dzh, dza):
    s, d = h.shape
    e = dzh.shape[2]
    ts = _tile(s, 512)
    ns = s // ts

    def body(dzh_ref, dza_ref, h_ref, o_ref, acc):
        j, k = pl.program_id(0), pl.program_id(1)

        @pl.when(k == 0)
        def _():
            acc[...] = jnp.zeros_like(acc)

        def add(dz):
            acc[...] += _dot_tn(h_ref[...], dz)

        _dz_pick(j, dzh_ref, dza_ref, add)

        @pl.when(k == ns - 1)
        def _():
            o_ref[...] = acc[...].astype(BF16)

    return pl.pallas_call(
        body, name="dwin", grid=(N_SPLITS, ns), out_shape=jax.ShapeDtypeStruct((N_SPLITS, d, e), BF16),
        in_specs=_dz_specs(ts, e, 0) + [pl.BlockSpec((ts, d), lambda j, k: (k, 0))],
        out_specs=pl.BlockSpec((None, d, e), lambda j, k: (j, 0, 0)),
        scratch_shapes=[pltpu.VMEM((d, e), F32)],
        compiler_params=_params("parallel", "arbitrary"))(dzh, dza, h)


def _pair_add(p, ra, c_idx):
    _, r, c = p.shape
    tr = _tile(r, 256)
    p4 = p.reshape(4, 2, r, c)

    def body(c_ref, p_ref, ra_ref, o_ref):
        o_ref[...] = (p_ref[...].astype(F32) + ra_ref[...].astype(F32)).astype(BF16)

    grid_spec = pltpu.PrefetchScalarGridSpec(
        num_scalar_prefetch=1, grid=(4, r // tr),
        in_specs=[pl.BlockSpec((None, None, tr, c), lambda j, i, cref: (j, cref[0], i, 0)),
                  pl.BlockSpec((None, tr, c), lambda j, i, cref: (j, i, 0))],
        out_specs=pl.BlockSpec((None, tr, c), lambda j, i, cref: (j, i, 0)))
    return pl.pallas_call(
        body, name="pair_add", grid_spec=grid_spec, out_shape=jax.ShapeDtypeStruct((4, r, c), BF16),
        compiler_params=_params("parallel", "parallel"))(c_idx, p4, ra)


def _sum_adamw(rb, w, m, v):
    r, c = w.shape
    tr = _tile(r, 128)

    def body(rb_ref, w_ref, m_ref, v_ref, g_ref, d_ref, mo_ref, vo_ref):
        g = rb_ref[0].astype(F32)
        for j in range(1, 4):
            g = g + rb_ref[j].astype(F32)
        g_ref[...] = g
        d_ref[...], mo_ref[...], vo_ref[...] = _adamw(w_ref[...], g, m_ref[...], v_ref[...])

    blk = pl.BlockSpec((tr, c), lambda i: (i, 0))
    shp = jax.ShapeDtypeStruct((r, c), F32)
    return pl.pallas_call(
        body, name="sum_adamw", grid=(r // tr,), out_shape=(shp, shp, shp, shp),
        in_specs=[pl.BlockSpec((4, tr, c), lambda i: (0, i, 0)), blk, blk, blk], out_specs=(blk, blk, blk, blk),
        compiler_params=_params("parallel"))(rb, w, m, v)


def _position():
    x, y, c = lax.axis_index("x"), lax.axis_index("y"), lax.axis_index("c")
    return x, y, c


def _all_gather_weights(a, b):
    def body(a_ref, b_ref, ao_ref, bo_ref, send_sems, recv_sems, local_sems):
        x, y, c = _position()
        me, sibling = (x, y, c), (x, y, 1 - c)
        chips = [(1 - x, y), (x, 1 - y), (1 - x, 1 - y)]
        srcs, outs = (a_ref, b_ref), (ao_ref, bo_ref)

        def slot(t, px, py, pc):
            return outs[t].at[4 * px + 2 * py + pc]

        def copy(t, k, block, to, src=None):
            dst = slot(t, *block)
            return pltpu.make_async_remote_copy(
                src_ref=dst if src is None else src, dst_ref=dst, send_sem=send_sems.at[t, k],
                recv_sem=recv_sems.at[t, k], device_id=to, device_id_type=MESH)

        mine = [pltpu.make_async_copy(srcs[t], slot(t, *me), local_sems.at[t]) for t in range(2)]
        for cp in mine:
            cp.start()
        first = []
        for t in range(2):
            first.append(copy(t, 0, me, sibling, src=srcs[t]))
            first += [copy(t, 1 + j, me, (*chip, c), src=srcs[t]) for j, chip in enumerate(chips)]
        for cp in first:
            cp.start()
        passed = []
        for t in range(2):
            for j, chip in enumerate(chips):
                copy(t, 1 + j, (*chip, c), me).wait_recv()
                fwd = copy(t, 4 + j, (*chip, c), sibling)
                fwd.start()
                passed.append(fwd)
        for t in range(2):
            copy(t, 0, sibling, me).wait_recv()
            for j, chip in enumerate(chips):
                copy(t, 4 + j, (*chip, 1 - c), me).wait_recv()
        for cp in first + passed:
            cp.wait_send()
        for cp in mine:
            cp.wait()

    return pl.pallas_call(
        body, name="all_gather_weights",
        out_shape=(jax.ShapeDtypeStruct((N_DEV,) + a.shape, a.dtype), jax.ShapeDtypeStruct((N_DEV,) + b.shape, b.dtype)),
        in_specs=[ANY, ANY], out_specs=(ANY, ANY),
        scratch_shapes=[pltpu.SemaphoreType.DMA((2, 7)), pltpu.SemaphoreType.DMA((2, 7)), pltpu.SemaphoreType.DMA((2,))],
    )(a, b)


def _exchange_sibling(pa, pb):
    def body(pa_ref, pb_ref, ra_ref, rb_ref, send_sems, recv_sems):
        x, y, c = _position()
        sibling = (x, y, 1 - c)
        copies = []
        for t, (p_ref, r_ref) in enumerate(((pa_ref, ra_ref), (pb_ref, rb_ref))):
            for j in range(4):
                copies.append(pltpu.make_async_remote_copy(
                    src_ref=p_ref.at[2 * j + 1 - c], dst_ref=r_ref.at[j], send_sem=send_sems.at[t, j],
                    recv_sem=recv_sems.at[t, j], device_id=sibling, device_id_type=MESH))
        for cp in copies:
            cp.start()
        for cp in copies:
            cp.wait_recv()
        for cp in copies:
            cp.wait_send()

    return pl.pallas_call(
        body, name="exchange_sibling",
        out_shape=(jax.ShapeDtypeStruct((4,) + pa.shape[1:], pa.dtype), jax.ShapeDtypeStruct((4,) + pb.shape[1:], pb.dtype)),
        in_specs=[ANY, ANY], out_specs=(ANY, ANY),
        scratch_shapes=[pltpu.SemaphoreType.DMA((2, 4)), pltpu.SemaphoreType.DMA((2, 4))],
    )(pa, pb)


def _exchange_chips(ta, tb):
    def body(ta_ref, tb_ref, ra_ref, rb_ref, send_sems, recv_sems, local_sems):
        x, y, c = _position()
        chips = [(1 - x, y), (x, 1 - y), (1 - x, 1 - y)]
        my_chip = 2 * x + y
        copies, own = [], []
        for t, (t_ref, r_ref) in enumerate(((ta_ref, ra_ref), (tb_ref, rb_ref))):
            own.append(pltpu.make_async_copy(t_ref.at[my_chip], r_ref.at[my_chip], local_sems.at[t]))
            for j, (px, py) in enumerate(chips):
                copies.append(pltpu.make_async_remote_copy(
                    src_ref=t_ref.at[2 * px + py], dst_ref=r_ref.at[my_chip], send_sem=send_sems.at[t, j],
                    recv_sem=recv_sems.at[t, j], device_id=(px, py, c), device_id_type=MESH))
        for cp in own + copies:
            cp.start()
        for cp in copies:
            cp.wait_recv()
        for cp in copies:
            cp.wait_send()
        for cp in own:
            cp.wait()

    return pl.pallas_call(
        body, name="exchange_chips",
        out_shape=(jax.ShapeDtypeStruct(ta.shape, ta.dtype), jax.ShapeDtypeStruct(tb.shape, tb.dtype)),
        in_specs=[ANY, ANY], out_specs=(ANY, ANY),
        scratch_shapes=[pltpu.SemaphoreType.DMA((2, 3)), pltpu.SemaphoreType.DMA((2, 3)), pltpu.SemaphoreType.DMA((2,))],
    )(ta, tb)


SMALL_ROWS = 8
ROW_LB = 4
ROW_GN = 6
ROW_LOSS = 7


def _small_allreduce_adamw(part, w, m, v, lb_logits):
    width = part.shape[1]

    def body(p_ref, w_ref, m_ref, v_ref, lb_ref, g_ref, d_ref, mo_ref, vo_ref, buf, send_sems, recv_sems):
        x, y, c = _position()
        me = 4 * x + 2 * y + c
        buf[me] = p_ref[...]
        copies = []
        for k in range(N_DEV - 1):
            bx, by, bc = ((k + 1) >> 2) & 1, ((k + 1) >> 1) & 1, (k + 1) & 1
            peer = (x ^ bx, y ^ by, c ^ bc)
            copies.append(pltpu.make_async_remote_copy(
                src_ref=p_ref, dst_ref=buf.at[me], send_sem=send_sems.at[k], recv_sem=recv_sems.at[k],
                device_id=peer, device_id_type=MESH))
        for cp in copies:
            cp.start()
        for cp in copies:
            cp.wait_recv()
        for cp in copies:
            cp.wait_send()
        tot = buf[0]
        for dev in range(1, N_DEV):
            tot = tot + buf[dev]
        lbv = lb_ref[...]
        lb = _sigmoid(lbv[0:1] - lbv[1:2])
        glb = tot[ROW_LB:ROW_LB + 1] * lb * (1.0 - lb)
        loss = jnp.sum(tot[ROW_LOSS:ROW_LOSS + 1], axis=-1, keepdims=True)
        row = lax.broadcasted_iota(jnp.int32, (SMALL_ROWS, width), 0)
        g = jnp.where(row == ROW_LB, glb, jnp.where(row == ROW_LB + 1, -glb, tot))
        g = jnp.where(row == ROW_LOSS, loss, g)
        g_ref[...] = g
        d_ref[...], mo_ref[...], vo_ref[...] = _adamw(w_ref[...], g, m_ref[...], v_ref[...])

    vm = pl.BlockSpec(memory_space=pltpu.VMEM)
    shp = jax.ShapeDtypeStruct((SMALL_ROWS, width), F32)
    return pl.pallas_call(
        body, name="small_allreduce_adamw", out_shape=(shp, shp, shp, shp),
        in_specs=[vm] * 5, out_specs=(vm, vm, vm, vm),
        scratch_shapes=[pltpu.VMEM((N_DEV, SMALL_ROWS, width), F32), pltpu.SemaphoreType.DMA((N_DEV - 1,)),
                        pltpu.SemaphoreType.DMA((N_DEV - 1,))],
    )(part, w, m, v, lb_logits)


def _pack_small(norm_gain, final_gain, lb2, gnorm, last_row, width):
    pad = lambda a: jnp.pad(a.reshape(1, -1), ((0, 0), (0, width - a.size)))
    return jnp.concatenate([norm_gain.reshape(2, width), final_gain.reshape(2, width), lb2.reshape(2, width),
                            pad(gnorm), last_row.reshape(1, width)], axis=0)


def _unpack_small(p, d, e, hd):
    return (p[0:2].reshape(1, d), p[2:4].reshape(d), p[4:6].reshape(2, e), p[6:7, :hd].reshape(1, hd))


def kernel(x, norm_gain, w_in, lb_logits, hgrn_gnorm, w_out, final_gain, loss_target, m_norm_gain, m_w_in, m_lb_logits, m_hgrn_gnorm, m_w_out, m_final_gain, v_norm_gain, v_w_in, v_lb_logits, v_hgrn_gnorm, v_w_out, v_final_gain):
    s, d = x.shape[1], x.shape[2]
    e = w_in.shape[2]
    assert d == 2 * e and lb_logits.shape == (2, e) and w_out.shape[1] * N_DEV == 2 * e
    x2d = x.reshape(s, d)
    tgt = loss_target.reshape(s, d)

    w_in_full, w_out_full = _all_gather_weights(_cast_bf16(w_in[0]), _cast_bf16(w_out[0]))
    w_out_full = w_out_full.reshape(2 * e, d)

    h = _rmsnorm_fwd(x2d, norm_gain)
    z = _inproj(h, w_in_full)
    y_h, states = _hgrn_fwd(z, lb_logits, hgrn_gnorm)
    o_attn, lse, y_a = _attn_fwd(z)
    dx2, dx2b, dy, loss_vec, dfg = _outproj_loss(x2d, y_h, y_a, w_out_full, final_gain.reshape(1, d), tgt)

    pwo = _dwout(y_h, y_a, dx2b)
    dza = _attn_bwd(z, dy, o_attn, lse)
    dzh, dlb, dgn = _hgrn_bwd(z, dy, states, lb_logits, hgrn_gnorm)
    grad_x, dng = _dh_dx(dzh, dza, w_in_full, x2d, norm_gain, dx2)
    pwi = _dwin(h, dzh, dza)

    c_idx = lax.axis_index("c").astype(jnp.int32).reshape(1)
    pwo8 = pwo.reshape(N_DEV, w_out.shape[1], d)
    ra_i, ra_o = _exchange_sibling(pwi, pwo8)
    rb_i, rb_o = _exchange_chips(_pair_add(pwi, ra_i, c_idx), _pair_add(pwo8, ra_o, c_idx))
    g_wi, d_wi, nm_wi, nv_wi = _sum_adamw(rb_i, w_in[0], m_w_in[0], v_w_in[0])
    g_wo, d_wo, nm_wo, nv_wo = _sum_adamw(rb_o, w_out[0], m_w_out[0], v_w_out[0])

    width = d // 2
    zero_row = jnp.zeros((1, width), F32)
    loss_row = loss_vec[:, :width] + loss_vec[:, width:]
    part = _pack_small(dng, dfg, jnp.concatenate([dlb, zero_row], axis=0), dgn, loss_row, width)
    pw = _pack_small(norm_gain, final_gain, lb_logits, hgrn_gnorm, zero_row, width)
    pm = _pack_small(m_norm_gain, m_final_gain, m_lb_logits, m_hgrn_gnorm, zero_row, width)
    pv = _pack_small(v_norm_gain, v_final_gain, v_lb_logits, v_hgrn_gnorm, zero_row, width)
    sg, sd, sm, sv = _small_allreduce_adamw(part, pw, pm, pv, lb_logits)
    hd = hgrn_gnorm.shape[1]
    g_ng, g_fg, g_lb, g_gn = _unpack_small(sg, d, e, hd)
    d_ng, d_fg, d_lb, d_gn = _unpack_small(sd, d, e, hd)
    m_ng, m_fg, m_lb, m_gn = _unpack_small(sm, d, e, hd)
    v_ng, v_fg, v_lb, v_gn = _unpack_small(sv, d, e, hd)
    loss = sg[ROW_LOSS, 0]

    one = lambda a: a[None]
    return (loss, grad_x.reshape(1, s, d), g_ng, one(g_wi), g_lb, g_gn, one(g_wo), g_fg,
            d_ng, one(d_wi), d_lb, d_gn, one(d_wo), d_fg,
            m_ng, one(nm_wi), m_lb, m_gn, one(nm_wo), m_fg,
            v_ng, one(nv_wi), v_lb, v_gn, one(nv_wo), v_fg)
```

```python
import functools
import math

import jax
import jax.numpy as jnp
from jax import lax
from jax.experimental import pallas as pl
from jax.experimental.pallas import tpu as pltpu

NORM_EPS = 1e-6
HGRN_HEAD = 128
HGRN_CHUNK = 64
ATTN_HEAD = 64
ATTN_BAND = 128
DILATIONS = (1, 4, 16)
N_SPLITS = 8
N_DEV = 8
ADAM_LR = 0.001
ADAM_B1 = 0.9
ADAM_B2 = 0.999
ADAM_EPS = 1e-08
ADAM_WD = 0.01
ADAM_STEP = 10
LANES = 128
MESH = pl.DeviceIdType.MESH
F32 = jnp.float32
BF16 = jnp.bfloat16
NEG_BIG = -1e30
VMEM_LIMIT = 56 * 1024 * 1024

ANY = pl.BlockSpec(memory_space=pl.ANY)


def _params(*sem):
    return pltpu.CompilerParams(dimension_semantics=sem, vmem_limit_bytes=VMEM_LIMIT)


def _tile(n, pref):
    t = min(n, pref)
    assert n % t == 0, (n, pref)
    return t


def _dot(a, b, precision=None):
    return jnp.dot(a, b, preferred_element_type=F32, precision=precision)


def _dot_nt(a, b):
    return lax.dot_general(a, b, (((1,), (1,)), ((), ())), preferred_element_type=F32)


def _dot_tn(a, b):
    return lax.dot_general(a, b, (((0,), (0,)), ((), ())), preferred_element_type=F32)


def _sigmoid(x):
    return 1.0 / (1.0 + jnp.exp(-x))


def _dsilu(x, s):
    return s * (1.0 + x * (1.0 - s))


def _adamw(w, g, m, v):
    m = ADAM_B1 * m + (1.0 - ADAM_B1) * g
    v = ADAM_B2 * v + (1.0 - ADAM_B2) * (g * g)
    m_hat = m / (1.0 - ADAM_B1 ** ADAM_STEP)
    v_hat = v / (1.0 - ADAM_B2 ** ADAM_STEP)
    delta = -ADAM_LR * (m_hat / (jnp.sqrt(v_hat) + ADAM_EPS) + ADAM_WD * w)
    return delta, m, v


def _cast_bf16(a):
    r, c = a.shape
    tr = _tile(r, 256)

    def body(a_ref, o_ref):
        o_ref[...] = a_ref[...].astype(BF16)

    return pl.pallas_call(
        body, name="cast_bf16", grid=(r // tr,), out_shape=jax.ShapeDtypeStruct((r, c), BF16),
        in_specs=[pl.BlockSpec((tr, c), lambda i: (i, 0))], out_specs=pl.BlockSpec((tr, c), lambda i: (i, 0)),
        compiler_params=_params("parallel"))(a)


def _rmsnorm_fwd(x, gain):
    s, d = x.shape
    tm = _tile(s, 512)

    def body(x_ref, g_ref, h_ref):
        xv = x_ref[...]
        r = lax.rsqrt(jnp.mean(xv * xv, axis=-1, keepdims=True) + NORM_EPS)
        h_ref[...] = (xv * r * g_ref[...]).astype(BF16)

    return pl.pallas_call(
        body, name="rmsnorm_fwd", grid=(s // tm,), out_shape=jax.ShapeDtypeStruct((s, d), BF16),
        in_specs=[pl.BlockSpec((tm, d), lambda i: (i, 0)), pl.BlockSpec((1, d), lambda i: (0, 0))],
        out_specs=pl.BlockSpec((tm, d), lambda i: (i, 0)), compiler_params=_params("parallel"))(x, gain)


def _inproj(h, w_full):
    s, d = h.shape
    e = w_full.shape[2]
    tm = _tile(s, 512)

    def body(h_ref, w_ref, z_ref):
        z_ref[...] = _dot(h_ref[...], w_ref[...])

    return pl.pallas_call(
        body, name="inproj", grid=(s // tm, N_SPLITS),
        out_shape=jax.ShapeDtypeStruct((s, N_SPLITS * e), F32),
        in_specs=[pl.BlockSpec((tm, d), lambda i, j: (i, 0)), pl.BlockSpec((None, d, e), lambda i, j: (j, 0, 0))],
        out_specs=pl.BlockSpec((tm, e), lambda i, j: (i, j)),
        compiler_params=_params("parallel", "arbitrary"))(h, w_full)


HGRN_BLOCK = 1024
TRI_ROWS = 256


def _chunk_masks():
    tb = TRI_ROWS
    row = lax.broadcasted_iota(jnp.int32, (tb, tb), 0)
    col = lax.broadcasted_iota(jnp.int32, (tb, tb), 1)
    same = (row // HGRN_CHUNK) == (col // HGRN_CHUNK)
    lower = jnp.where(same & (col <= row), 1.0, 0.0).astype(BF16)
    upper = jnp.where(same & (col >= row), 1.0, 0.0).astype(BF16)
    return lower, upper


def _split3(a):
    hi = a.astype(BF16).astype(F32)
    mid = (a - hi).astype(BF16).astype(F32)
    lo = (a - hi - mid).astype(BF16).astype(F32)
    return hi, mid, lo


def _tri_dot(tri, x):
    hi, mid, lo = (p.astype(BF16) for p in _split3(x))
    outs = []
    for r in range(0, x.shape[0], TRI_ROWS):
        sl = slice(r, r + TRI_ROWS)
        outs.append(_dot(tri, hi[sl]) + _dot(tri, mid[sl]) + _dot(tri, lo[sl]))
    return outs[0] if len(outs) == 1 else jnp.concatenate(outs, axis=0)


def _hgrn_gates(qp, fp, lbv):
    lb = _sigmoid(lbv[0:1] - lbv[1:2])
    sq = _sigmoid(qp)
    q = qp * sq
    sg = _sigmoid(fp)
    f = lb + (1.0 - lb) * sg
    k = 1.0 - f
    return lb, sq, q, sg, f, k


def _hgrn_fwd(z, lb_logits, gnorm):
    s = z.shape[0]
    e = z.shape[1] // N_SPLITS
    nh = e // HGRN_HEAD
    tb = _tile(s, HGRN_BLOCK)
    nc = tb // HGRN_CHUNK
    nb = s // tb
    C = HGRN_CHUNK

    def body(q_ref, f_ref, i_ref, g_ref, lb_ref, gn_ref, y_ref, st_ref, state, o_scr):
        @pl.when(pl.program_id(1) == 0)
        def _():
            state[...] = jnp.zeros_like(state)

        lb, sq, q, sg, f, k = _hgrn_gates(q_ref[...], f_ref[...], lb_ref[...])
        lower, _ = _chunk_masks()
        b = _tri_dot(lower, jnp.log(f))
        b3 = b.reshape(nc, C, HGRN_HEAD)
        bc = b3[:, C - 1:C, :]
        qt = (q * jnp.exp(b)).astype(BF16)
        kt = (k * jnp.exp(-b)).astype(BF16)
        ke = (k.reshape(nc, C, HGRN_HEAD) * jnp.exp(bc - b3)).reshape(tb, HGRN_HEAD).astype(BF16)
        v = i_ref[...].astype(BF16)
        tri = lax.broadcasted_iota(jnp.int32, (C, C), 1) <= lax.broadcasted_iota(jnp.int32, (C, C), 0)
        sls = [slice(c * C, (c + 1) * C) for c in range(nc)]
        kv = [_dot_tn(v[sl], ke[sl]) for sl in sls]
        a = [jnp.where(tri, _dot_nt(qt[sl], kt[sl]), 0.0).astype(BF16) for sl in sls]
        st = state[...]
        sts = []
        for c in range(nc):
            sts.append(st)
            st_ref[c] = st
            st = st * jnp.exp(bc[c]) + kv[c]
        state[...] = st
        for c, sl in enumerate(sls):
            o_scr[sl, :] = _dot(a[c], v[sl]) + _dot_nt(qt[sl], sts[c].astype(BF16))
        o = o_scr[...]
        rms = lax.rsqrt(jnp.mean(o * o, axis=-1, keepdims=True) + NORM_EPS)
        gp = g_ref[...]
        y_ref[...] = (o * rms * gn_ref[...] * (gp * _sigmoid(gp))).astype(BF16)

    col = lambda kk: (lambda h, n: (n, kk * nh + h))
    return pl.pallas_call(
        body, name="hgrn_fwd", grid=(nh, nb),
        out_shape=(jax.ShapeDtypeStruct((s, e), BF16),
                   jax.ShapeDtypeStruct((nh, s // C, HGRN_HEAD, HGRN_HEAD), F32)),
        in_specs=[pl.BlockSpec((tb, HGRN_HEAD), col(0)), pl.BlockSpec((tb, HGRN_HEAD), col(1)),
                  pl.BlockSpec((tb, HGRN_HEAD), col(2)), pl.BlockSpec((tb, HGRN_HEAD), col(3)),
                  pl.BlockSpec((2, HGRN_HEAD), lambda h, n: (0, h)), pl.BlockSpec((1, HGRN_HEAD), lambda h, n: (0, 0))],
        out_specs=(pl.BlockSpec((tb, HGRN_HEAD), lambda h, n: (n, h)),
                   pl.BlockSpec((None, nc, HGRN_HEAD, HGRN_HEAD), lambda h, n: (h, n, 0, 0))),
        scratch_shapes=[pltpu.VMEM((HGRN_HEAD, HGRN_HEAD), F32), pltpu.VMEM((tb, HGRN_HEAD), F32)],
        compiler_params=_params("parallel", "arbitrary"))(z, z, z, z, lb_logits, gnorm)


def _hgrn_bwd(z, dy, states, lb_logits, gnorm):
    s = z.shape[0]
    e = z.shape[1] // N_SPLITS
    nh = e // HGRN_HEAD
    tb = _tile(s, HGRN_BLOCK)
    nc = tb // HGRN_CHUNK
    nb = s // tb
    C = HGRN_CHUNK
    H = HGRN_HEAD

    def body(q_ref, f_ref, i_ref, g_ref, dy_ref, st_ref, lb_ref, gn_ref, dz_ref, dlb_ref, dgn_ref,
             gstate, o_scr, dq_scr, dk_scr, dv_scr, e_scr):
        first = (pl.program_id(0) == 0) & (pl.program_id(1) == 0)

        @pl.when(first)
        def _():
            dgn_ref[...] = jnp.zeros_like(dgn_ref)

        @pl.when(pl.program_id(1) == 0)
        def _():
            gstate[...] = jnp.zeros_like(gstate)
            dlb_ref[...] = jnp.zeros_like(dlb_ref)

        qp = q_ref[...]
        lb, sq, q, sg, f, k = _hgrn_gates(qp, f_ref[...], lb_ref[...])
        lower, upper = _chunk_masks()
        b = _tri_dot(lower, jnp.log(f))
        b3 = b.reshape(nc, C, H)
        bc = b3[:, C - 1:C, :]
        eb = jnp.exp(b)
        enb = jnp.exp(-b)
        eend = jnp.exp(bc - b3).reshape(tb, H)
        qt = (q * eb).astype(BF16)
        kt = (k * enb).astype(BF16)
        ke = (k * eend).astype(BF16)
        v = i_ref[...].astype(BF16)
        tri = lax.broadcasted_iota(jnp.int32, (C, C), 1) <= lax.broadcasted_iota(jnp.int32, (C, C), 0)
        sls = [slice(c * C, (c + 1) * C) for c in range(nc)]
        a = [jnp.where(tri, _dot_nt(qt[sl], kt[sl]), 0.0).astype(BF16) for sl in sls]
        for c, sl in enumerate(sls):
            o_scr[sl, :] = _dot(a[c], v[sl]) + _dot_nt(qt[sl], st_ref[c].astype(BF16))
        o = o_scr[...]
        rms = lax.rsqrt(jnp.mean(o * o, axis=-1, keepdims=True) + NORM_EPS)
        on = o * rms
        gn = gn_ref[...]
        gp = g_ref[...]
        sgg = _sigmoid(gp)
        dyv = dy_ref[...]
        d_on = dyv * (gp * sgg)
        dz_ref[3] = (dyv * on * gn * _dsilu(gp, sgg)).astype(BF16)
        dgn_ref[...] += jnp.sum(d_on * on, axis=0, keepdims=True)
        u = d_on * gn
        do = (rms * (u - on * jnp.mean(u * on, axis=-1, keepdims=True))).astype(BF16)
        gup = [_dot_tn(do[sl], qt[sl]) for sl in sls]
        da = [jnp.where(tri, _dot_nt(do[sl], v[sl]), 0.0).astype(BF16) for sl in sls]
        gt = gstate[...]
        gts = [None] * nc
        for c in reversed(range(nc)):
            gts[c] = gt
            gt = gt * jnp.exp(bc[c]) + gup[c]
        gstate[...] = gt
        for c, sl in enumerate(sls):
            stp = st_ref[c]
            gtb = gts[c].astype(BF16)
            dqt = _dot(da[c], kt[sl]) + _dot(do[sl], stp.astype(BF16))
            dkt = _dot_tn(da[c], qt[sl])
            dks = _dot(v[sl], gtb) * eend[sl]
            dv_scr[sl, :] = _dot_tn(a[c], do[sl]) + _dot_nt(ke[sl], gtb)
            dq_scr[sl, :] = dqt * eb[sl]
            dk_scr[sl, :] = dkt * enb[sl] + dks
            ech = (jnp.sum(k[sl] * dks, axis=0, keepdims=True)
                   + jnp.sum(gts[c] * jnp.exp(bc[c]) * stp, axis=0, keepdims=True))
            e_scr[sl, :] = jnp.broadcast_to(ech, (C, H))
        dq = dq_scr[...]
        dk = dk_scr[...]
        dlf = _tri_dot(upper, q * dq - k * dk) + e_scr[...]
        dft = dlf / f - dk
        dz_ref[0] = (dq * _dsilu(qp, sq)).astype(BF16)
        dz_ref[1] = (dft * (1.0 - lb) * sg * (1.0 - sg)).astype(BF16)
        dz_ref[2] = dv_scr[...].astype(BF16)
        dlb_ref[...] += jnp.sum(dft * (1.0 - sg), axis=0, keepdims=True)

    col = lambda kk: (lambda h, n: (nb - 1 - n, kk * nh + h))
    return pl.pallas_call(
        body, name="hgrn_bwd", grid=(nh, nb),
        out_shape=(jax.ShapeDtypeStruct((4, s, e), BF16), jax.ShapeDtypeStruct((1, e), F32),
                   jax.ShapeDtypeStruct((1, H), F32)),
        in_specs=[pl.BlockSpec((tb, H), col(0)), pl.BlockSpec((tb, H), col(1)),
                  pl.BlockSpec((tb, H), col(2)), pl.BlockSpec((tb, H), col(3)),
                  pl.BlockSpec((tb, H), lambda h, n: (nb - 1 - n, h)),
                  pl.BlockSpec((None, nc, H, H), lambda h, n: (h, nb - 1 - n, 0, 0)),
                  pl.BlockSpec((2, H), lambda h, n: (0, h)), pl.BlockSpec((1, H), lambda h, n: (0, 0))],
        out_specs=(pl.BlockSpec((4, tb, H), lambda h, n: (0, nb - 1 - n, h)),
                   pl.BlockSpec((1, H), lambda h, n: (0, h)), pl.BlockSpec((1, H), lambda h, n: (0, 0))),
        scratch_shapes=[pltpu.VMEM((H, H), F32)] + [pltpu.VMEM((tb, H), F32)] * 5,
        compiler_params=_params("arbitrary", "arbitrary"))(z, z, z, z, dy, states, lb_logits, gnorm)


ATTN_T = 16 * ATTN_BAND
SCALE = ATTN_HEAD ** -0.5
TILE_UNROLL = 2


def _slope(hh, nheads):
    head = (2 * pl.program_id(0) + hh + 1).astype(F32)
    return jnp.exp(jnp.full((1, 1), -8.0 / nheads * math.log(2.0), F32) * head)


def _fill_bias(bias, nheads, delta, edge_ok):
    band = (delta >= 0) & (delta <= ATTN_BAND)
    dist = delta.astype(F32)
    for pi, dil in enumerate(DILATIONS):
        for hh in range(2):
            full = jnp.where(band, -(_slope(hh, nheads) * float(dil)) * dist, NEG_BIG)
            bias[(pi * 2 + hh) * 2] = full
            bias[(pi * 2 + hh) * 2 + 1] = jnp.where(edge_ok, full, NEG_BIG)


def _rows(start, size, stride):
    if stride == 1:
        return pl.ds(pl.multiple_of(start, ATTN_BAND), size)
    return pl.ds(start, size, stride=stride)


def _head_lanes(rows, hh):
    return (lax.broadcasted_iota(jnp.int32, (rows, LANES), 1) // ATTN_HEAD) == hh


def _attn_fwd(z):
    s = z.shape[0]
    e = z.shape[1] // N_SPLITS
    npair = e // LANES
    T = ATTN_T
    assert s % T == 0
    nsb = s // T
    W = ATTN_BAND
    nt = T // W
    HD = ATTN_HEAD
    chunk = 256

    def body(q_ref, kp_ref, kc_ref, vp_ref, vc_ref, g_ref, o_ref, l_ref, y_ref, qa, kbuf, va, bias, accs, ms, msw, lsw):
        sb = pl.program_id(1)
        for hh in range(2):
            def stage(i, carry):
                rows = pl.ds(pl.multiple_of(i * chunk, chunk), chunk)
                mine = _head_lanes(chunk, hh)
                qa[hh, rows, :] = jnp.where(mine, q_ref[rows, :] * SCALE, 0.0)
                va[hh, rows, :] = jnp.where(mine, vp_ref[rows, :], 1.0)
                va[hh, pl.ds(pl.multiple_of(T + i * chunk, chunk), chunk), :] = jnp.where(mine, vc_ref[rows, :], 1.0)
                return carry

            lax.fori_loop(0, T // chunk, stage, 0)
        kbuf[0:T, :] = kp_ref[...]
        kbuf[T:, :] = kc_ref[...]
        qi = lax.broadcasted_iota(jnp.int32, (W, 2 * W), 0)
        kj = lax.broadcasted_iota(jnp.int32, (W, 2 * W), 1)
        _fill_bias(bias, 2 * npair, W + qi - kj, kj >= W)

        def tile(tau, carry):
            first = _head_lanes(W, 0)
            rows, scores = [], []
            for pi, dil in enumerate(DILATIONS):
                r = tau % dil
                ub = tau // dil
                qrows = _rows(r + dil * W * ub, W, dil)
                krows = _rows(T + dil * W * (ub - 1) + r, 2 * W, dil)
                var = jnp.where((sb == 0) & (ub == 0), 1, 0)
                kt = kbuf[krows, :].astype(BF16)
                rows.append((qrows, krows))
                scores.append([_dot_nt(qa[hh, qrows, :].astype(BF16), kt) + bias[(pi * 2 + hh) * 2 + var]
                               for hh in range(2)])
            maxes = [[jnp.max(sc, axis=-1, keepdims=True) for sc in pair] for pair in scores]
            probs = [[jnp.exp(sc - m).astype(BF16) for sc, m in zip(ps, pm)] for ps, pm in zip(scores, maxes)]
            for pi, (qrows, krows) in enumerate(rows):
                outs = [_dot(probs[pi][hh], va[hh, krows, :].astype(BF16)) for hh in range(2)]
                accs[pi, qrows, :] = jnp.where(first, outs[0], outs[1])
                lsw[pi, qrows, :] = jnp.where(first, outs[1], outs[0])
                ms[pi, qrows, :] = jnp.where(first, maxes[pi][0], maxes[pi][1])
                msw[pi, qrows, :] = jnp.where(first, maxes[pi][1], maxes[pi][0])
            return carry

        lax.fori_loop(0, nt, tile, 0, unroll=TILE_UNROLL)

        def merge(i, carry):
            rows = pl.ds(pl.multiple_of(i * chunk, chunk), chunk)
            m1, m2, m3 = ms[0, rows, :], ms[1, rows, :], ms[2, rows, :]
            mx = jnp.maximum(jnp.maximum(m1, m2), m3)
            s1, s2, s3 = msw[0, rows, :], msw[1, rows, :], msw[2, rows, :]
            sx = jnp.maximum(jnp.maximum(s1, s2), s3)
            den_sw = (jnp.exp(s1 - sx) * lsw[0, rows, :] + jnp.exp(s2 - sx) * lsw[1, rows, :]
                      + jnp.exp(s3 - sx) * lsw[2, rows, :])
            den = pltpu.roll(den_sw, ATTN_HEAD, 1)
            o = (jnp.exp(m1 - mx) * accs[0, rows, :] + jnp.exp(m2 - mx) * accs[1, rows, :]
                 + jnp.exp(m3 - mx) * accs[2, rows, :]) / den
            o_ref[rows, :] = o
            l_ref[rows, :] = mx + jnp.log(den)
            gp = g_ref[rows, :]
            y_ref[rows, :] = (o * (gp * _sigmoid(gp))).astype(BF16)
            return carry

        lax.fori_loop(0, T // chunk, merge, 0)

    cur = lambda split: (lambda hp, sb: (sb, split * npair + hp))
    prev = lambda split: (lambda hp, sb: (jnp.maximum(sb - 1, 0), split * npair + hp))
    blk = lambda index: pl.BlockSpec((T, LANES), index)
    out = blk(lambda hp, sb: (sb, hp))
    buf = lambda rows: pltpu.VMEM((rows, LANES), F32)
    return pl.pallas_call(
        body, name="attn_fwd", grid=(npair, nsb),
        out_shape=(jax.ShapeDtypeStruct((s, e), F32), jax.ShapeDtypeStruct((s, e), F32), jax.ShapeDtypeStruct((s, e), BF16)),
        in_specs=[blk(cur(4)), blk(prev(5)), blk(cur(5)), blk(prev(6)), blk(cur(6)), blk(cur(7))],
        out_specs=(out, out, out),
        scratch_shapes=[pltpu.VMEM((2, T, LANES), F32), buf(2 * T), pltpu.VMEM((2, 2 * T, LANES), F32),
                        pltpu.VMEM((12, W, 2 * W), F32)] + [pltpu.VMEM((3, T, LANES), F32)] * 4,
        compiler_params=_params("parallel", "arbitrary"))(z, z, z, z, z, z)


def _outproj_loss(x, y_h, y_a, w_out_full, final_gain, target):
    s, d = x.shape
    e = y_h.shape[1]
    tm = _tile(s, 256)

    def body(x_ref, yh_ref, ya_ref, w_ref, g_ref, t_ref, dx_ref, dxb_ref, dy_ref, loss_ref, dg_ref):
        @pl.when(pl.program_id(0) == 0)
        def _():
            loss_ref[...] = jnp.zeros_like(loss_ref)
            dg_ref[...] = jnp.zeros_like(dg_ref)

        w = w_ref[...]
        x2 = x_ref[...] + _dot(yh_ref[...], w[0:e]) + _dot(ya_ref[...], w[e:2 * e])
        r = lax.rsqrt(jnp.mean(x2 * x2, axis=-1, keepdims=True) + NORM_EPS)
        xn = x2 * r
        g = g_ref[...]
        err = xn * g - t_ref[...]
        loss_ref[...] += jnp.sum(err * err, axis=0, keepdims=True) * (0.5 / d)
        dyo = err * (1.0 / d)
        dg_ref[...] += jnp.sum(dyo * xn, axis=0, keepdims=True)
        u = dyo * g
        dx2 = r * (u - xn * jnp.mean(u * xn, axis=-1, keepdims=True))
        dx_ref[...] = dx2
        dxb = dx2.astype(BF16)
        dxb_ref[...] = dxb
        dy_ref[...] = _dot_nt(dxb, w)

    row = pl.BlockSpec((tm, d), lambda i: (i, 0))
    half = pl.BlockSpec((tm, e), lambda i: (i, 0))
    vec = pl.BlockSpec((1, d), lambda i: (0, 0))
    return pl.pallas_call(
        body, name="outproj_loss", grid=(s // tm,),
        out_shape=(jax.ShapeDtypeStruct((s, d), F32), jax.ShapeDtypeStruct((s, d), BF16),
                   jax.ShapeDtypeStruct((s, 2 * e), F32), jax.ShapeDtypeStruct((1, d), F32),
                   jax.ShapeDtypeStruct((1, d), F32)),
        in_specs=[row, half, half, pl.BlockSpec((2 * e, d), lambda i: (0, 0)), vec, row],
        out_specs=(row, row, pl.BlockSpec((tm, 2 * e), lambda i: (i, 0)), vec, vec),
        compiler_params=_params("arbitrary"))(x, y_h, y_a, w_out_full, final_gain, target)


def _dwout(y_h, y_a, dxb):
    s, e = y_h.shape
    d = dxb.shape[1]
    ts = _tile(s, 512)
    ns = s // ts

    def body(yh_ref, ya_ref, dx_ref, o_ref, acc):
        half = pl.program_id(0)
        step = pl.program_id(1)

        @pl.when(step == 0)
        def _():
            acc[...] = jnp.zeros_like(acc)

        @pl.when(half == 0)
        def _():
            acc[...] += _dot_tn(yh_ref[...], dx_ref[...])

        @pl.when(half == 1)
        def _():
            acc[...] += _dot_tn(ya_ref[...], dx_ref[...])

        @pl.when(step == ns - 1)
        def _():
            o_ref[...] = acc[...].astype(BF16)

    return pl.pallas_call(
        body, name="dwout", grid=(2, ns), out_shape=jax.ShapeDtypeStruct((2 * e, d), BF16),
        in_specs=[pl.BlockSpec((ts, e), lambda hf, k: (k * (1 - hf), 0)), pl.BlockSpec((ts, e), lambda hf, k: (k * hf, 0)),
                  pl.BlockSpec((ts, d), lambda hf, k: (k, 0))],
        out_specs=pl.BlockSpec((e, d), lambda hf, k: (hf, 0)),
        scratch_shapes=[pltpu.VMEM((e, d), F32)],
        compiler_params=_params("parallel", "arbitrary"))(y_h, y_a, dxb)


def _attn_bwd(z, dy, o, lse):
    s, e = o.shape
    npair = e // LANES
    T = ATTN_T
    assert s % T == 0
    nsb = s // T
    W = ATTN_BAND
    nt = T // W
    HD = ATTN_HEAD
    chunk = 256

    def body(k_ref, v_ref, qc_ref, qn_ref, dyc_ref, dyn_ref, gc_ref, gn_ref, oc_ref, on_ref, lc_ref, ln_ref,
             dz_ref, qa, doa, ka, va, dqacc, dkacc, dvacc, bias):
        sb = pl.program_id(1)
        halves = ((qc_ref, dyc_ref, gc_ref, oc_ref, lc_ref), (qn_ref, dyn_ref, gn_ref, on_ref, ln_ref))
        for half, (q_r, dy_r, g_r, o_r, l_r) in enumerate(halves):
            def stage(i, carry):
                rows = pl.ds(pl.multiple_of(i * chunk, chunk), chunk)
                dst = pl.ds(pl.multiple_of(half * T + i * chunk, chunk), chunk)
                lane = lax.broadcasted_iota(jnp.int32, (chunk, LANES), 1)
                gp = g_r[rows, :]
                sg = _sigmoid(gp)
                dyv = dy_r[rows, :]
                ov = o_r[rows, :]
                dov = dyv * (gp * sg)
                prod = dov * ov
                qv = q_r[rows, :] * SCALE
                lv = l_r[rows, :]
                for hh in range(2):
                    mine = _head_lanes(chunk, hh)
                    spare = (1 - hh) * HD
                    lse_parts = _split3(lv[:, hh * HD:hh * HD + 1])
                    dl_parts = _split3(jnp.sum(prod[:, hh * HD:(hh + 1) * HD], axis=-1, keepdims=True))
                    qh = jnp.where(mine, qv, 0.0)
                    dh = jnp.where(mine, dov, 0.0)
                    for j in range(3):
                        qh = jnp.where(lane == spare + j, lse_parts[j], qh)
                        dh = jnp.where(lane == spare + j, dl_parts[j], dh)
                    qa[hh, dst, :] = qh
                    doa[hh, dst, :] = dh
                    if half == 0:
                        minus = (lane >= spare) & (lane < spare + 3)
                        ka[hh, rows, :] = jnp.where(minus, -1.0, k_ref[rows, :])
                        va[hh, rows, :] = jnp.where(minus, -1.0, v_ref[rows, :])
                if half == 0:
                    dz_ref[3, rows, :] = (dyv * ov * _dsilu(gp, sg)).astype(BF16)
                return carry

            lax.fori_loop(0, T // chunk, stage, 0)

        @pl.when(sb == 0)
        def _():
            dqacc[0:T, :] = jnp.zeros((T, LANES), F32)

        dqacc[T:, :] = jnp.zeros((T, LANES), F32)
        dkacc[...] = jnp.zeros_like(dkacc)
        dvacc[...] = jnp.zeros_like(dvacc)
        qi = lax.broadcasted_iota(jnp.int32, (2 * W, W), 0)
        kj = lax.broadcasted_iota(jnp.int32, (2 * W, W), 1)
        _fill_bias(bias, 2 * npair, qi - kj, qi < W)

        def tile(tau, carry):
            rows, ops, sc, dpd = [], [], [], []
            for pi, dil in enumerate(DILATIONS):
                r = tau % dil
                ub = tau // dil
                start = r + dil * W * ub
                krows = _rows(start, W, dil)
                qrows = _rows(start, 2 * W, dil)
                var = jnp.where((sb == nsb - 1) & (ub == nt // dil - 1), 1, 0)
                rows.append((krows, qrows))
                for hh in range(2):
                    kt = ka[hh, krows, :].astype(BF16)
                    vt = va[hh, krows, :].astype(BF16)
                    qt = qa[hh, qrows, :].astype(BF16)
                    dt = doa[hh, qrows, :].astype(BF16)
                    ops.append((kt, qt, dt))
                    sc.append(_dot_nt(qt, kt) + bias[(pi * 2 + hh) * 2 + var])
                    dpd.append(_dot_nt(dt, vt))
            ps = [jnp.exp(s) for s in sc]
            dss = [(p * d).astype(BF16) for p, d in zip(ps, dpd)]
            pbs = [p.astype(BF16) for p in ps]
            dvs = [_dot_tn(pb, dt) for pb, (kt, qt, dt) in zip(pbs, ops)]
            dks = [_dot_tn(ds, qt) for ds, (kt, qt, dt) in zip(dss, ops)]
            dqs = [_dot(ds, kt) for ds, (kt, qt, dt) in zip(dss, ops)]
            for pi, (krows, qrows) in enumerate(rows):
                dkacc[krows, :] += jnp.where(_head_lanes(W, 0), dks[2 * pi], dks[2 * pi + 1])
                dvacc[krows, :] += jnp.where(_head_lanes(W, 0), dvs[2 * pi], dvs[2 * pi + 1])
                dqacc[qrows, :] += jnp.where(_head_lanes(2 * W, 0), dqs[2 * pi], dqs[2 * pi + 1]) * SCALE
            return carry

        lax.fori_loop(0, nt, tile, 0, unroll=TILE_UNROLL)

        def flush(i, carry):
            rows = pl.ds(pl.multiple_of(i * chunk, chunk), chunk)
            nxt = pl.ds(pl.multiple_of(T + i * chunk, chunk), chunk)
            dz_ref[0, rows, :] = dqacc[rows, :].astype(BF16)
            dz_ref[1, rows, :] = dkacc[rows, :].astype(BF16)
            dz_ref[2, rows, :] = dvacc[rows, :].astype(BF16)
            dqacc[rows, :] = dqacc[nxt, :]
            return carry

        lax.fori_loop(0, T // chunk, flush, 0)

    zc = lambda split: (lambda hp, sb: (sb, split * npair + hp))
    zn = lambda split: (lambda hp, sb: (jnp.minimum(sb + 1, nsb - 1), split * npair + hp))
    ec = lambda off: (lambda hp, sb: (sb, off + hp))
    en = lambda off: (lambda hp, sb: (jnp.minimum(sb + 1, nsb - 1), off + hp))
    blk = lambda index: pl.BlockSpec((T, LANES), index)
    buf = lambda rows: pltpu.VMEM((rows, LANES), F32)
    return pl.pallas_call(
        body, name="attn_bwd", grid=(npair, nsb), out_shape=jax.ShapeDtypeStruct((4, s, e), BF16),
        in_specs=[blk(zc(5)), blk(zc(6)), blk(zc(4)), blk(zn(4)), blk(ec(npair)), blk(en(npair)),
                  blk(zc(7)), blk(zn(7)), blk(ec(0)), blk(en(0)), blk(ec(0)), blk(en(0))],
        out_specs=pl.BlockSpec((4, T, LANES), lambda hp, sb: (0, sb, hp)),
        scratch_shapes=[pltpu.VMEM((2, 2 * T, LANES), F32), pltpu.VMEM((2, 2 * T, LANES), F32),
                        pltpu.VMEM((2, T, LANES), F32), pltpu.VMEM((2, T, LANES), F32),
                        buf(2 * T), buf(T), buf(T), pltpu.VMEM((12, 2 * W, W), F32)],
        compiler_params=_params("parallel", "arbitrary"))(z, z, z, z, dy, dy, z, z, o, o, lse, lse)


def _dz_specs(tm, e, axis):
    def mk(lo, hi):
        def index(i, k):
            row, grp = (i, k) if axis == 1 else (k, i)
            return (jnp.clip(grp - lo, 0, hi - lo - 1), row, 0)
        return pl.BlockSpec((None, tm, e), index)
    return [mk(0, 4), mk(4, 8)]


def _dz_pick(grp, dzh_ref, dza_ref, fn):
    @pl.when(grp < 4)
    def _():
        fn(dzh_ref[...])

    @pl.when(grp >= 4)
    def _():
        fn(dza_ref[...])


def _dh_dx(dzh, dza, w_full, x, gain, dx2):
    s, d = x.shape
    e = dzh.shape[2]
    tm = _tile(s, 512)

    def body(dzh_ref, dza_ref, w_ref, x_ref, g_ref, dx2_ref, gx_ref, dg_ref, acc):
        i, k = pl.program_id(0), pl.program_id(1)

        @pl.when((i == 0) & (k == 0))
        def _():
            dg_ref[...] = jnp.zeros_like(dg_ref)

        @pl.when(k == 0)
        def _():
            acc[...] = jnp.zeros_like(acc)

        def add(dz):
            acc[...] += _dot_nt(dz, w_ref[...])

        _dz_pick(k, dzh_ref, dza_ref, add)

        @pl.when(k == N_SPLITS - 1)
        def _():
            dh = acc[...]
            xv = x_ref[...]
            r = lax.rsqrt(jnp.mean(xv * xv, axis=-1, keepdims=True) + NORM_EPS)
            xn = xv * r
            dg_ref[...] += jnp.sum(dh * xn, axis=0, keepdims=True)
            u = dh * g_ref[...]
            gx_ref[...] = dx2_ref[...] + r * (u - xn * jnp.mean(u * xn, axis=-1, keepdims=True))

    row = pl.BlockSpec((tm, d), lambda i, k: (i, 0))
    vec = pl.BlockSpec((1, d), lambda i, k: (0, 0))
    return pl.pallas_call(
        body, name="dh_dx", grid=(s // tm, N_SPLITS),
        out_shape=(jax.ShapeDtypeStruct((s, d), F32), jax.ShapeDtypeStruct((1, d), F32)),
        in_specs=_dz_specs(tm, e, 1) + [pl.BlockSpec((None, d, e), lambda i, k: (k, 0, 0)), row, vec, row],
        out_specs=(row, vec), scratch_shapes=[pltpu.VMEM((tm, d), F32)],
        compiler_params=_params("arbitrary", "arbitrary"))(dzh, dza, w_full, x, gain, dx2)


def _dwin(h, dzh, dza):
    s, d = h.shape
    e = dzh.shape[2]
    ts = _tile(s, 512)
    ns = s // ts

    def body(dzh_ref, dza_ref, h_ref, o_ref, acc):
        j, k = pl.program_id(0), pl.program_id(1)

        @pl.when(k == 0)
        def _():
            acc[...] = jnp.zeros_like(acc)

        def add(dz):
            acc[...] += _dot_tn(h_ref[...], dz)

        _dz_pick(j, dzh_ref, dza_ref, add)

        @pl.when(k == ns - 1)
        def _():
            o_ref[...] = acc[...].astype(BF16)

    return pl.pallas_call(
        body, name="dwin", grid=(N_SPLITS, ns), out_shape=jax.ShapeDtypeStruct((N_SPLITS, d, e), BF16),
        in_specs=_dz_specs(ts, e, 0) + [pl.BlockSpec((ts, d), lambda j, k: (k, 0))],
        out_specs=pl.BlockSpec((None, d, e), lambda j, k: (j, 0, 0)),
        scratch_shapes=[pltpu.VMEM((d, e), F32)],
        compiler_params=_params("parallel", "arbitrary"))(dzh, dza, h)


def _pair_add(p, ra, c_idx):
    _, r, c = p.shape
    tr = _tile(r, 256)
    p4 = p.reshape(4, 2, r, c)

    def body(c_ref, p_ref, ra_ref, o_ref):
        o_ref[...] = (p_ref[...].astype(F32) + ra_ref[...].astype(F32)).astype(BF16)

    grid_spec = pltpu.PrefetchScalarGridSpec(
        num_scalar_prefetch=1, grid=(4, r // tr),
        in_specs=[pl.BlockSpec((None, None, tr, c), lambda j, i, cref: (j, cref[0], i, 0)),
                  pl.BlockSpec((None, tr, c), lambda j, i, cref: (j, i, 0))],
        out_specs=pl.BlockSpec((None, tr, c), lambda j, i, cref: (j, i, 0)))
    return pl.pallas_call(
        body, name="pair_add", grid_spec=grid_spec, out_shape=jax.ShapeDtypeStruct((4, r, c), BF16),
        compiler_params=_params("parallel", "parallel"))(c_idx, p4, ra)


def _sum_adamw(rb, w, m, v):
    r, c = w.shape
    tr = _tile(r, 128)

    def body(rb_ref, w_ref, m_ref, v_ref, g_ref, d_ref, mo_ref, vo_ref):
        g = rb_ref[0].astype(F32)
        for j in range(1, 4):
            g = g + rb_ref[j].astype(F32)
        g_ref[...] = g
        d_ref[...], mo_ref[...], vo_ref[...] = _adamw(w_ref[...], g, m_ref[...], v_ref[...])

    blk = pl.BlockSpec((tr, c), lambda i: (i, 0))
    shp = jax.ShapeDtypeStruct((r, c), F32)
    return pl.pallas_call(
        body, name="sum_adamw", grid=(r // tr,), out_shape=(shp, shp, shp, shp),
        in_specs=[pl.BlockSpec((4, tr, c), lambda i: (0, i, 0)), blk, blk, blk], out_specs=(blk, blk, blk, blk),
        compiler_params=_params("parallel"))(rb, w, m, v)


def _position():
    x, y, c = lax.axis_index("x"), lax.axis_index("y"), lax.axis_index("c")
    return x, y, c


def _all_gather_weights(a, b):
    def body(a_ref, b_ref, ao_ref, bo_ref, send_sems, recv_sems, local_sems):
        x, y, c = _position()
        me, sibling = (x, y, c), (x, y, 1 - c)
        chips = [(1 - x, y), (x, 1 - y), (1 - x, 1 - y)]
        srcs, outs = (a_ref, b_ref), (ao_ref, bo_ref)

        def slot(t, px, py, pc):
            return outs[t].at[4 * px + 2 * py + pc]

        def copy(t, k, block, to, src=None):
            dst = slot(t, *block)
            return pltpu.make_async_remote_copy(
                src_ref=dst if src is None else src, dst_ref=dst, send_sem=send_sems.at[t, k],
                recv_sem=recv_sems.at[t, k], device_id=to, device_id_type=MESH)

        mine = [pltpu.make_async_copy(srcs[t], slot(t, *me), local_sems.at[t]) for t in range(2)]
        for cp in mine:
            cp.start()
        first = []
        for t in range(2):
            first.append(copy(t, 0, me, sibling, src=srcs[t]))
            first += [copy(t, 1 + j, me, (*chip, c), src=srcs[t]) for j, chip in enumerate(chips)]
        for cp in first:
            cp.start()
        passed = []
        for t in range(2):
            for j, chip in enumerate(chips):
                copy(t, 1 + j, (*chip, c), me).wait_recv()
                fwd = copy(t, 4 + j, (*chip, c), sibling)
                fwd.start()
                passed.append(fwd)
        for t in range(2):
            copy(t, 0, sibling, me).wait_recv()
            for j, chip in enumerate(chips):
                copy(t, 4 + j, (*chip, 1 - c), me).wait_recv()
        for cp in first + passed:
            cp.wait_send()
        for cp in mine:
            cp.wait()

    return pl.pallas_call(
        body, name="all_gather_weights",
        out_shape=(jax.ShapeDtypeStruct((N_DEV,) + a.shape, a.dtype), jax.ShapeDtypeStruct((N_DEV,) + b.shape, b.dtype)),
        in_specs=[ANY, ANY], out_specs=(ANY, ANY),
        scratch_shapes=[pltpu.SemaphoreType.DMA((2, 7)), pltpu.SemaphoreType.DMA((2, 7)), pltpu.SemaphoreType.DMA((2,))],
    )(a, b)


def _exchange_sibling(pa, pb):
    def body(pa_ref, pb_ref, ra_ref, rb_ref, send_sems, recv_sems):
        x, y, c = _position()
        sibling = (x, y, 1 - c)
        copies = []
        for t, (p_ref, r_ref) in enumerate(((pa_ref, ra_ref), (pb_ref, rb_ref))):
            for j in range(4):
                copies.append(pltpu.make_async_remote_copy(
                    src_ref=p_ref.at[2 * j + 1 - c], dst_ref=r_ref.at[j], send_sem=send_sems.at[t, j],
                    recv_sem=recv_sems.at[t, j], device_id=sibling, device_id_type=MESH))
        for cp in copies:
            cp.start()
        for cp in copies:
            cp.wait_recv()
        for cp in copies:
            cp.wait_send()

    return pl.pallas_call(
        body, name="exchange_sibling",
        out_shape=(jax.ShapeDtypeStruct((4,) + pa.shape[1:], pa.dtype), jax.ShapeDtypeStruct((4,) + pb.shape[1:], pb.dtype)),
        in_specs=[ANY, ANY], out_specs=(ANY, ANY),
        scratch_shapes=[pltpu.SemaphoreType.DMA((2, 4)), pltpu.SemaphoreType.DMA((2, 4))],
    )(pa, pb)


def _exchange_chips(ta, tb):
    def body(ta_ref, tb_ref, ra_ref, rb_ref, send_sems, recv_sems, local_sems):
        x, y, c = _position()
        chips = [(1 - x, y), (x, 1 - y), (1 - x, 1 - y)]
        my_chip = 2 * x + y
        copies, own = [], []
        for t, (t_ref, r_ref) in enumerate(((ta_ref, ra_ref), (tb_ref, rb_ref))):
            own.append(pltpu.make_async_copy(t_ref.at[my_chip], r_ref.at[my_chip], local_sems.at[t]))
            for j, (px, py) in enumerate(chips):
                copies.append(pltpu.make_async_remote_copy(
                    src_ref=t_ref.at[2 * px + py], dst_ref=r_ref.at[my_chip], send_sem=send_sems.at[t, j],
                    recv_sem=recv_sems.at[t, j], device_id=(px, py, c), device_id_type=MESH))
        for cp in own + copies:
            cp.start()
        for cp in copies:
            cp.wait_recv()
        for cp in copies:
            cp.wait_send()
        for cp in own:
            cp.wait()

    return pl.pallas_call(
        body, name="exchange_chips",
        out_shape=(jax.ShapeDtypeStruct(ta.shape, ta.dtype), jax.ShapeDtypeStruct(tb.shape, tb.dtype)),
        in_specs=[ANY, ANY], out_specs=(ANY, ANY),
        scratch_shapes=[pltpu.SemaphoreType.DMA((2, 3)), pltpu.SemaphoreType.DMA((2, 3)), pltpu.SemaphoreType.DMA((2,))],
    )(ta, tb)


def _xor_peer(x, y, c, mask):
    return (x ^ ((mask >> 2) & 1), y ^ ((mask >> 1) & 1), c ^ (mask & 1))


def _block_order(masks):
    me = 4 * lax.axis_index("x") + 2 * lax.axis_index("y") + lax.axis_index("c")
    return jnp.stack([me ^ m for m in masks]).astype(jnp.int32)


GATHER_MASKS = (0, 1, 4, 2, 6, 5, 3, 7)


def _inproj_gather(h, w_loc, wo_loc):
    s, d = h.shape
    e = w_loc.shape[1]
    tm = _tile(s, 512)
    ni = s // tm
    pre = max(ni - 3, 0)

    def body(order_ref, h_ref, w_ref, wo_ref, z_ref, wf_ref, wof_ref, wbuf, send_sems, recv_sems, osend, orecv,
             local_sems, wsems):
        j, i = pl.program_id(0), pl.program_id(1)
        x, y, c = _position()
        me, sibling = (x, y, c), (x, y, 1 - c)
        chips = [(1 - x, y), (x, 1 - y), (1 - x, 1 - y)]
        blk = lambda p: 4 * p[0] + 2 * p[1] + p[2]

        def copy(k, block, to, src=None):
            dst = wf_ref.at[blk(block)]
            return pltpu.make_async_remote_copy(
                src_ref=dst if src is None else src, dst_ref=dst, send_sem=send_sems.at[k], recv_sem=recv_sems.at[k],
                device_id=to, device_id_type=MESH)

        first = [copy(0, me, sibling, src=w_ref)] + [copy(1 + q, me, (*chip, c), src=w_ref) for q, chip in enumerate(chips)]
        passed = [copy(4 + q, (*chip, c), sibling) for q, chip in enumerate(chips)]
        mine = pltpu.make_async_copy(w_ref, wf_ref.at[blk(me)], local_sems.at[0])
        ocopies = [pltpu.make_async_remote_copy(
            src_ref=wo_ref, dst_ref=wof_ref.at[blk(me)], send_sem=osend.at[k], recv_sem=orecv.at[k],
            device_id=_xor_peer(x, y, c, k + 1), device_id_type=MESH) for k in range(N_DEV - 1)]
        omine = pltpu.make_async_copy(wo_ref, wof_ref.at[blk(me)], local_sems.at[1])
        blocks = [me, sibling] + [(*chip, c) for chip in chips] + [(*chip, 1 - c) for chip in chips]
        arrive = [None, copy(0, sibling, me)] + [copy(1 + q, (*chip, c), me) for q, chip in enumerate(chips)] \
            + [copy(4 + q, (*chip, 1 - c), me) for q, chip in enumerate(chips)]
        forward = [None, None] + passed + [None, None, None]

        def load(slot, src):
            return pltpu.make_async_copy(src, wbuf.at[slot], wsems.at[slot])

        @pl.when((j == 0) & (i == 0))
        def _():
            for cp in [mine, omine] + first + ocopies:
                cp.start()
            load(0, w_ref).start()

        for jj in range(N_DEV):
            @pl.when((j == jj) & (i == 0))
            def _():
                load(jj % 2, w_ref).wait()

            if jj + 1 < N_DEV:
                @pl.when((j == jj) & (i == pre))
                def _():
                    arrive[jj + 1].wait_recv()
                    if forward[jj + 1] is not None:
                        forward[jj + 1].start()
                    load((jj + 1) % 2, wf_ref.at[blk(blocks[jj + 1])]).start()

        z_ref[...] = _dot(h_ref[...], wbuf[j % 2])

        @pl.when((j == N_DEV - 1) & (i == ni - 1))
        def _():
            for cp in first + passed:
                cp.wait_send()
            for cp in ocopies:
                cp.wait_send()
                cp.wait_recv()
            mine.wait()
            omine.wait()

    grid_spec = pltpu.PrefetchScalarGridSpec(
        num_scalar_prefetch=1, grid=(N_DEV, ni),
        in_specs=[pl.BlockSpec((tm, d), lambda j, i, o: (i, 0)), ANY, ANY],
        out_specs=(pl.BlockSpec((tm, e), lambda j, i, o: (i, o[j])), ANY, ANY),
        scratch_shapes=[pltpu.VMEM((2, d, e), BF16), pltpu.SemaphoreType.DMA((7,)), pltpu.SemaphoreType.DMA((7,)),
                        pltpu.SemaphoreType.DMA((7,)), pltpu.SemaphoreType.DMA((7,)), pltpu.SemaphoreType.DMA((2,)),
                        pltpu.SemaphoreType.DMA((2,))])
    return pl.pallas_call(
        body, name="inproj_gather", grid_spec=grid_spec,
        out_shape=(jax.ShapeDtypeStruct((s, N_SPLITS * e), F32), jax.ShapeDtypeStruct((N_DEV, d, e), BF16),
                   jax.ShapeDtypeStruct((N_DEV,) + wo_loc.shape, BF16)),
        compiler_params=_params("arbitrary", "arbitrary"))(_block_order(GATHER_MASKS), h, w_loc, wo_loc)


SCATTER_MASKS = (6, 7, 4, 5, 2, 3, 1, 0)


def _scatter_block(k, acc, stage, own_ref, recv_ref, send_sems, recv_sems, own_sem, last):
    x, y, c = _position()

    def copy(kk):
        m = SCATTER_MASKS[kk]
        return pltpu.make_async_remote_copy(
            src_ref=stage.at[kk % 2], dst_ref=recv_ref.at[m - 1], send_sem=send_sems.at[kk], recv_sem=recv_sems.at[kk],
            device_id=_xor_peer(x, y, c, m), device_id_type=MESH)

    for kk in range(N_DEV):
        @pl.when(last & (k == kk))
        def _():
            if SCATTER_MASKS[kk] == 0:
                keep = pltpu.make_async_copy(acc, own_ref, own_sem)
                keep.start()
                keep.wait()
            else:
                if kk >= 2:
                    copy(kk - 2).wait_send()
                stage[kk % 2] = acc[...].astype(BF16)
                copy(kk).start()
            if kk == N_DEV - 1:
                for q in range(N_DEV - 1):
                    if q >= N_DEV - 3:
                        copy(q).wait_send()
                    copy(q).wait_recv()


def _dwin_scatter(h, dzh, dza):
    s, d = h.shape
    e = dzh.shape[2]
    ts = _tile(s, 512)
    ns = s // ts

    def body(order_ref, dzh_ref, dza_ref, h_ref, own_ref, recv_ref, acc, stage, send_sems, recv_sems, own_sem):
        k, step = pl.program_id(0), pl.program_id(1)

        @pl.when(step == 0)
        def _():
            acc[...] = jnp.zeros_like(acc)

        def add(dz):
            acc[...] += _dot_tn(h_ref[...], dz)

        _dz_pick(order_ref[k], dzh_ref, dza_ref, add)
        _scatter_block(k, acc, stage, own_ref, recv_ref, send_sems, recv_sems, own_sem, step == ns - 1)

    def dz_spec(lo):
        return pl.BlockSpec((None, ts, e), lambda k, st, o: (jnp.clip(o[k] - lo, 0, 3), st, 0))

    grid_spec = pltpu.PrefetchScalarGridSpec(
        num_scalar_prefetch=1, grid=(N_DEV, ns),
        in_specs=[dz_spec(0), dz_spec(4), pl.BlockSpec((ts, d), lambda k, st, o: (st, 0))],
        out_specs=(ANY, ANY),
        scratch_shapes=[pltpu.VMEM((d, e), F32), pltpu.VMEM((2, d, e), BF16), pltpu.SemaphoreType.DMA((7,)),
                        pltpu.SemaphoreType.DMA((7,)), pltpu.SemaphoreType.DMA(())])
    return pl.pallas_call(
        body, name="dwin_scatter", grid_spec=grid_spec,
        out_shape=(jax.ShapeDtypeStruct((d, e), F32), jax.ShapeDtypeStruct((N_DEV - 1, d, e), BF16)),
        compiler_params=_params("arbitrary", "arbitrary"))(_block_order(SCATTER_MASKS), dzh, dza, h)


def _dwout_scatter(y_h, y_a, dxb):
    s, e = y_h.shape
    d = dxb.shape[1]
    r = 2 * e // N_DEV
    per = e // r
    ts = _tile(s, 512)
    ns = s // ts

    def body(order_ref, yh_ref, ya_ref, dx_ref, own_ref, recv_ref, acc, stage, send_sems, recv_sems, own_sem):
        k, step = pl.program_id(0), pl.program_id(1)

        @pl.when(step == 0)
        def _():
            acc[...] = jnp.zeros_like(acc)

        @pl.when(order_ref[k] < per)
        def _():
            acc[...] += _dot_tn(yh_ref[...], dx_ref[...])

        @pl.when(order_ref[k] >= per)
        def _():
            acc[...] += _dot_tn(ya_ref[...], dx_ref[...])

        _scatter_block(k, acc, stage, own_ref, recv_ref, send_sems, recv_sems, own_sem, step == ns - 1)

    def y_spec(lo):
        return pl.BlockSpec((ts, r), lambda k, st, o: (st, jnp.clip(o[k] - lo, 0, per - 1)))

    grid_spec = pltpu.PrefetchScalarGridSpec(
        num_scalar_prefetch=1, grid=(N_DEV, ns),
        in_specs=[y_spec(0), y_spec(per), pl.BlockSpec((ts, d), lambda k, st, o: (st, 0))],
        out_specs=(ANY, ANY),
        scratch_shapes=[pltpu.VMEM((r, d), F32), pltpu.VMEM((2, r, d), BF16), pltpu.SemaphoreType.DMA((7,)),
                        pltpu.SemaphoreType.DMA((7,)), pltpu.SemaphoreType.DMA(())])
    return pl.pallas_call(
        body, name="dwout_scatter", grid_spec=grid_spec,
        out_shape=(jax.ShapeDtypeStruct((r, d), F32), jax.ShapeDtypeStruct((N_DEV - 1, r, d), BF16)),
        compiler_params=_params("arbitrary", "arbitrary"))(_block_order(SCATTER_MASKS), y_h, y_a, dxb)


def _sum7_adamw(own, recv, w, m, v):
    r, c = w.shape
    tr = _tile(r, 128)

    def body(own_ref, rc_ref, w_ref, m_ref, v_ref, g_ref, d_ref, mo_ref, vo_ref):
        g = own_ref[...]
        for q in range(N_DEV - 1):
            g = g + rc_ref[q].astype(F32)
        g_ref[...] = g
        d_ref[...], mo_ref[...], vo_ref[...] = _adamw(w_ref[...], g, m_ref[...], v_ref[...])

    blk = pl.BlockSpec((tr, c), lambda i: (i, 0))
    shp = jax.ShapeDtypeStruct((r, c), F32)
    return pl.pallas_call(
        body, name="sum7_adamw", grid=(r // tr,), out_shape=(shp, shp, shp, shp),
        in_specs=[blk, pl.BlockSpec((N_DEV - 1, tr, c), lambda i: (0, i, 0)), blk, blk, blk],
        out_specs=(blk, blk, blk, blk), compiler_params=_params("parallel"))(own, recv, w, m, v)


SMALL_ROWS = 8
ROW_LB = 4
ROW_GN = 6
ROW_LOSS = 7


def _small_allreduce_adamw(part, w, m, v, lb_logits):
    width = part.shape[1]

    def body(p_ref, w_ref, m_ref, v_ref, lb_ref, g_ref, d_ref, mo_ref, vo_ref, buf, send_sems, recv_sems):
        x, y, c = _position()
        me = 4 * x + 2 * y + c
        buf[me] = p_ref[...]
        copies = []
        for k in range(N_DEV - 1):
            bx, by, bc = ((k + 1) >> 2) & 1, ((k + 1) >> 1) & 1, (k + 1) & 1
            peer = (x ^ bx, y ^ by, c ^ bc)
            copies.append(pltpu.make_async_remote_copy(
                src_ref=p_ref, dst_ref=buf.at[me], send_sem=send_sems.at[k], recv_sem=recv_sems.at[k],
                device_id=peer, device_id_type=MESH))
        for cp in copies:
            cp.start()
        for cp in copies:
            cp.wait_recv()
        for cp in copies:
            cp.wait_send()
        tot = buf[0]
        for dev in range(1, N_DEV):
            tot = tot + buf[dev]
        lbv = lb_ref[...]
        lb = _sigmoid(lbv[0:1] - lbv[1:2])
        glb = tot[ROW_LB:ROW_LB + 1] * lb * (1.0 - lb)
        loss = jnp.sum(tot[ROW_LOSS:ROW_LOSS + 1], axis=-1, keepdims=True)
        row = lax.broadcasted_iota(jnp.int32, (SMALL_ROWS, width), 0)
        g = jnp.where(row == ROW_LB, glb, jnp.where(row == ROW_LB + 1, -glb, tot))
        g = jnp.where(row == ROW_LOSS, loss, g)
        g_ref[...] = g
        d_ref[...], mo_ref[...], vo_ref[...] = _adamw(w_ref[...], g, m_ref[...], v_ref[...])

    vm = pl.BlockSpec(memory_space=pltpu.VMEM)
    shp = jax.ShapeDtypeStruct((SMALL_ROWS, width), F32)
    return pl.pallas_call(
        body, name="small_allreduce_adamw", out_shape=(shp, shp, shp, shp),
        in_specs=[vm] * 5, out_specs=(vm, vm, vm, vm),
        scratch_shapes=[pltpu.VMEM((N_DEV, SMALL_ROWS, width), F32), pltpu.SemaphoreType.DMA((N_DEV - 1,)),
                        pltpu.SemaphoreType.DMA((N_DEV - 1,))],
    )(part, w, m, v, lb_logits)


def _pack_small(norm_gain, final_gain, lb2, gnorm, last_row, width):
    pad = lambda a: jnp.pad(a.reshape(1, -1), ((0, 0), (0, width - a.size)))
    return jnp.concatenate([norm_gain.reshape(2, width), final_gain.reshape(2, width), lb2.reshape(2, width),
                            pad(gnorm), last_row.reshape(1, width)], axis=0)


def _unpack_small(p, d, e, hd):
    return (p[0:2].reshape(1, d), p[2:4].reshape(d), p[4:6].reshape(2, e), p[6:7, :hd].reshape(1, hd))


def kernel(x, norm_gain, w_in, lb_logits, hgrn_gnorm, w_out, final_gain, loss_target, m_norm_gain, m_w_in, m_lb_logits, m_hgrn_gnorm, m_w_out, m_final_gain, v_norm_gain, v_w_in, v_lb_logits, v_hgrn_gnorm, v_w_out, v_final_gain):
    s, d = x.shape[1], x.shape[2]
    e = w_in.shape[2]
    assert d == 2 * e and lb_logits.shape == (2, e) and w_out.shape[1] * N_DEV == 2 * e
    x2d = x.reshape(s, d)
    tgt = loss_target.reshape(s, d)

    h = _rmsnorm_fwd(x2d, norm_gain)
    z, w_in_full, w_out_full = _inproj_gather(h, _cast_bf16(w_in[0]), _cast_bf16(w_out[0]))
    w_out_full = w_out_full.reshape(2 * e, d)
    y_h, states = _hgrn_fwd(z, lb_logits, hgrn_gnorm)
    o_attn, lse, y_a = _attn_fwd(z)
    dx2, dx2b, dy, loss_vec, dfg = _outproj_loss(x2d, y_h, y_a, w_out_full, final_gain.reshape(1, d), tgt)

    own_o, recv_o = _dwout_scatter(y_h, y_a, dx2b)
    dza = _attn_bwd(z, dy, o_attn, lse)
    dzh, dlb, dgn = _hgrn_bwd(z, dy, states, lb_logits, hgrn_gnorm)
    grad_x, dng = _dh_dx(dzh, dza, w_in_full, x2d, norm_gain, dx2)
    own_i, recv_i = _dwin_scatter(h, dzh, dza)
    g_wi, d_wi, nm_wi, nv_wi = _sum7_adamw(own_i, recv_i, w_in[0], m_w_in[0], v_w_in[0])
    g_wo, d_wo, nm_wo, nv_wo = _sum7_adamw(own_o, recv_o, w_out[0], m_w_out[0], v_w_out[0])

    width = d // 2
    zero_row = jnp.zeros((1, width), F32)
    loss_row = loss_vec[:, :width] + loss_vec[:, width:]
    part = _pack_small(dng, dfg, jnp.concatenate([dlb, zero_row], axis=0), dgn, loss_row, width)
    pw = _pack_small(norm_gain, final_gain, lb_logits, hgrn_gnorm, zero_row, width)
    pm = _pack_small(m_norm_gain, m_final_gain, m_lb_logits, m_hgrn_gnorm, zero_row, width)
    pv = _pack_small(v_norm_gain, v_final_gain, v_lb_logits, v_hgrn_gnorm, zero_row, width)
    sg, sd, sm, sv = _small_allreduce_adamw(part, pw, pm, pv, lb_logits)
    hd = hgrn_gnorm.shape[1]
    g_ng, g_fg, g_lb, g_gn = _unpack_small(sg, d, e, hd)
    d_ng, d_fg, d_lb, d_gn = _unpack_small(sd, d, e, hd)
    m_ng, m_fg, m_lb, m_gn = _unpack_small(sm, d, e, hd)
    v_ng, v_fg, v_lb, v_gn = _unpack_small(sv, d, e, hd)
    loss = sg[ROW_LOSS, 0]

    one = lambda a: a[None]
    return (loss, grad_x.reshape(1, s, d), g_ng, one(g_wi), g_lb, g_gn, one(g_wo), g_fg,
            d_ng, one(d_wi), d_lb, d_gn, one(d_wo), d_fg,
            m_ng, one(nm_wi), m_lb, m_gn, one(nm_wo), m_fg,
            v_ng, one(nv_wi), v_lb, v_gn, one(nv_wo), v_fg)
```

```python
import functools
import math

import jax
import jax.numpy as jnp
from jax import lax
from jax.experimental import pallas as pl
from jax.experimental.pallas import tpu as pltpu

NORM_EPS = 1e-6
HGRN_HEAD = 128
HGRN_CHUNK = 64
ATTN_HEAD = 64
ATTN_BAND = 128
DILATIONS = (1, 4, 16)
N_SPLITS = 8
N_DEV = 8
ADAM_LR = 0.001
ADAM_B1 = 0.9
ADAM_B2 = 0.999
ADAM_EPS = 1e-08
ADAM_WD = 0.01
ADAM_STEP = 10
LANES = 128
MESH = pl.DeviceIdType.MESH
F32 = jnp.float32
BF16 = jnp.bfloat16
NEG_BIG = -1e30
VMEM_LIMIT = 56 * 1024 * 1024

ANY = pl.BlockSpec(memory_space=pl.ANY)


def _params(*sem):
    return pltpu.CompilerParams(dimension_semantics=sem, vmem_limit_bytes=VMEM_LIMIT)


def _tile(n, pref):
    t = min(n, pref)
    assert n % t == 0, (n, pref)
    return t


def _dot(a, b, precision=None):
    return jnp.dot(a, b, preferred_element_type=F32, precision=precision)


def _dot_nt(a, b):
    return lax.dot_general(a, b, (((1,), (1,)), ((), ())), preferred_element_type=F32)


def _dot_tn(a, b):
    return lax.dot_general(a, b, (((0,), (0,)), ((), ())), preferred_element_type=F32)


def _sigmoid(x):
    return 1.0 / (1.0 + jnp.exp(-x))


def _dsilu(x, s):
    return s * (1.0 + x * (1.0 - s))


def _adamw(w, g, m, v):
    m = ADAM_B1 * m + (1.0 - ADAM_B1) * g
    v = ADAM_B2 * v + (1.0 - ADAM_B2) * (g * g)
    m_hat = m / (1.0 - ADAM_B1 ** ADAM_STEP)
    v_hat = v / (1.0 - ADAM_B2 ** ADAM_STEP)
    delta = -ADAM_LR * (m_hat / (jnp.sqrt(v_hat) + ADAM_EPS) + ADAM_WD * w)
    return delta, m, v


def _cast_bf16(a):
    r, c = a.shape
    tr = _tile(r, 256)

    def body(a_ref, o_ref):
        o_ref[...] = a_ref[...].astype(BF16)

    return pl.pallas_call(
        body, name="cast_bf16", grid=(r // tr,), out_shape=jax.ShapeDtypeStruct((r, c), BF16),
        in_specs=[pl.BlockSpec((tr, c), lambda i: (i, 0))], out_specs=pl.BlockSpec((tr, c), lambda i: (i, 0)),
        compiler_params=_params("parallel"))(a)


def _rmsnorm_fwd(x, gain):
    s, d = x.shape
    tm = _tile(s, 512)

    def body(x_ref, g_ref, h_ref):
        xv = x_ref[...]
        r = lax.rsqrt(jnp.mean(xv * xv, axis=-1, keepdims=True) + NORM_EPS)
        h_ref[...] = (xv * r * g_ref[...]).astype(BF16)

    return pl.pallas_call(
        body, name="rmsnorm_fwd", grid=(s // tm,), out_shape=jax.ShapeDtypeStruct((s, d), BF16),
        in_specs=[pl.BlockSpec((tm, d), lambda i: (i, 0)), pl.BlockSpec((1, d), lambda i: (0, 0))],
        out_specs=pl.BlockSpec((tm, d), lambda i: (i, 0)), compiler_params=_params("parallel"))(x, gain)


HGRN_BLOCK = 1024
TRI_ROWS = 256


def _chunk_masks():
    tb = TRI_ROWS
    row = lax.broadcasted_iota(jnp.int32, (tb, tb), 0)
    col = lax.broadcasted_iota(jnp.int32, (tb, tb), 1)
    same = (row // HGRN_CHUNK) == (col // HGRN_CHUNK)
    lower = jnp.where(same & (col <= row), 1.0, 0.0).astype(BF16)
    upper = jnp.where(same & (col >= row), 1.0, 0.0).astype(BF16)
    return lower, upper


def _split3(a):
    hi = a.astype(BF16).astype(F32)
    mid = (a - hi).astype(BF16).astype(F32)
    lo = (a - hi - mid).astype(BF16).astype(F32)
    return hi, mid, lo


def _tri_dot(tri, x):
    hi, mid, lo = (p.astype(BF16) for p in _split3(x))
    outs = []
    for r in range(0, x.shape[0], TRI_ROWS):
        sl = slice(r, r + TRI_ROWS)
        outs.append(_dot(tri, hi[sl]) + _dot(tri, mid[sl]) + _dot(tri, lo[sl]))
    return outs[0] if len(outs) == 1 else jnp.concatenate(outs, axis=0)


def _hgrn_gates(qp, fp, lbv):
    lb = _sigmoid(lbv[0:1] - lbv[1:2])
    sq = _sigmoid(qp)
    q = qp * sq
    sg = _sigmoid(fp)
    f = lb + (1.0 - lb) * sg
    k = 1.0 - f
    return lb, sq, q, sg, f, k


def _hgrn_fwd(z, lb_logits, gnorm):
    s = z.shape[0]
    e = z.shape[1] // N_SPLITS
    nh = e // HGRN_HEAD
    tb = _tile(s, HGRN_BLOCK)
    nc = tb // HGRN_CHUNK
    nb = s // tb
    C = HGRN_CHUNK

    def body(q_ref, f_ref, i_ref, g_ref, lb_ref, gn_ref, y_ref, st_ref, state, o_scr):
        @pl.when(pl.program_id(1) == 0)
        def _():
            state[...] = jnp.zeros_like(state)

        lb, sq, q, sg, f, k = _hgrn_gates(q_ref[...], f_ref[...], lb_ref[...])
        lower, _ = _chunk_masks()
        b = _tri_dot(lower, jnp.log(f))
        b3 = b.reshape(nc, C, HGRN_HEAD)
        bc = b3[:, C - 1:C, :]
        qt = (q * jnp.exp(b)).astype(BF16)
        kt = (k * jnp.exp(-b)).astype(BF16)
        ke = (k.reshape(nc, C, HGRN_HEAD) * jnp.exp(bc - b3)).reshape(tb, HGRN_HEAD).astype(BF16)
        v = i_ref[...].astype(BF16)
        tri = lax.broadcasted_iota(jnp.int32, (C, C), 1) <= lax.broadcasted_iota(jnp.int32, (C, C), 0)
        sls = [slice(c * C, (c + 1) * C) for c in range(nc)]
        kv = [_dot_tn(v[sl], ke[sl]) for sl in sls]
        a = [jnp.where(tri, _dot_nt(qt[sl], kt[sl]), 0.0).astype(BF16) for sl in sls]
        st = state[...]
        sts = []
        for c in range(nc):
            sts.append(st)
            st_ref[c] = st
            st = st * jnp.exp(bc[c]) + kv[c]
        state[...] = st
        for c, sl in enumerate(sls):
            o_scr[sl, :] = _dot(a[c], v[sl]) + _dot_nt(qt[sl], sts[c].astype(BF16))
        o = o_scr[...]
        rms = lax.rsqrt(jnp.mean(o * o, axis=-1, keepdims=True) + NORM_EPS)
        gp = g_ref[...]
        y_ref[...] = (o * rms * gn_ref[...] * (gp * _sigmoid(gp))).astype(BF16)

    col = lambda kk: (lambda h, n: (n, kk * nh + h))
    return pl.pallas_call(
        body, name="hgrn_fwd", grid=(nh, nb),
        out_shape=(jax.ShapeDtypeStruct((s, e), BF16),
                   jax.ShapeDtypeStruct((nh, s // C, HGRN_HEAD, HGRN_HEAD), F32)),
        in_specs=[pl.BlockSpec((tb, HGRN_HEAD), col(0)), pl.BlockSpec((tb, HGRN_HEAD), col(1)),
                  pl.BlockSpec((tb, HGRN_HEAD), col(2)), pl.BlockSpec((tb, HGRN_HEAD), col(3)),
                  pl.BlockSpec((2, HGRN_HEAD), lambda h, n: (0, h)), pl.BlockSpec((1, HGRN_HEAD), lambda h, n: (0, 0))],
        out_specs=(pl.BlockSpec((tb, HGRN_HEAD), lambda h, n: (n, h)),
                   pl.BlockSpec((None, nc, HGRN_HEAD, HGRN_HEAD), lambda h, n: (h, n, 0, 0))),
        scratch_shapes=[pltpu.VMEM((HGRN_HEAD, HGRN_HEAD), F32), pltpu.VMEM((tb, HGRN_HEAD), F32)],
        compiler_params=_params("parallel", "arbitrary"))(z, z, z, z, lb_logits, gnorm)


def _hgrn_bwd(z, dy, states, lb_logits, gnorm):
    s = z.shape[0]
    e = z.shape[1] // N_SPLITS
    nh = e // HGRN_HEAD
    tb = _tile(s, HGRN_BLOCK)
    nc = tb // HGRN_CHUNK
    nb = s // tb
    C = HGRN_CHUNK
    H = HGRN_HEAD

    def body(q_ref, f_ref, i_ref, g_ref, dy_ref, st_ref, lb_ref, gn_ref, dz_ref, dlb_ref, dgn_ref,
             gstate, o_scr, dq_scr, dk_scr, dv_scr, e_scr):
        first = (pl.program_id(0) == 0) & (pl.program_id(1) == 0)

        @pl.when(first)
        def _():
            dgn_ref[...] = jnp.zeros_like(dgn_ref)

        @pl.when(pl.program_id(1) == 0)
        def _():
            gstate[...] = jnp.zeros_like(gstate)
            dlb_ref[...] = jnp.zeros_like(dlb_ref)

        qp = q_ref[...]
        lb, sq, q, sg, f, k = _hgrn_gates(qp, f_ref[...], lb_ref[...])
        lower, upper = _chunk_masks()
        b = _tri_dot(lower, jnp.log(f))
        b3 = b.reshape(nc, C, H)
        bc = b3[:, C - 1:C, :]
        eb = jnp.exp(b)
        enb = jnp.exp(-b)
        eend = jnp.exp(bc - b3).reshape(tb, H)
        qt = (q * eb).astype(BF16)
        kt = (k * enb).astype(BF16)
        ke = (k * eend).astype(BF16)
        v = i_ref[...].astype(BF16)
        tri = lax.broadcasted_iota(jnp.int32, (C, C), 1) <= lax.broadcasted_iota(jnp.int32, (C, C), 0)
        sls = [slice(c * C, (c + 1) * C) for c in range(nc)]
        a = [jnp.where(tri, _dot_nt(qt[sl], kt[sl]), 0.0).astype(BF16) for sl in sls]
        for c, sl in enumerate(sls):
            o_scr[sl, :] = _dot(a[c], v[sl]) + _dot_nt(qt[sl], st_ref[c].astype(BF16))
        o = o_scr[...]
        rms = lax.rsqrt(jnp.mean(o * o, axis=-1, keepdims=True) + NORM_EPS)
        on = o * rms
        gn = gn_ref[...]
        gp = g_ref[...]
        sgg = _sigmoid(gp)
        dyv = dy_ref[...]
        d_on = dyv * (gp * sgg)
        dz_ref[3] = (dyv * on * gn * _dsilu(gp, sgg)).astype(BF16)
        dgn_ref[...] += jnp.sum(d_on * on, axis=0, keepdims=True)
        u = d_on * gn
        do = (rms * (u - on * jnp.mean(u * on, axis=-1, keepdims=True))).astype(BF16)
        gup = [_dot_tn(do[sl], qt[sl]) for sl in sls]
        da = [jnp.where(tri, _dot_nt(do[sl], v[sl]), 0.0).astype(BF16) for sl in sls]
        gt = gstate[...]
        gts = [None] * nc
        for c in reversed(range(nc)):
            gts[c] = gt
            gt = gt * jnp.exp(bc[c]) + gup[c]
        gstate[...] = gt
        for c, sl in enumerate(sls):
            stp = st_ref[c]
            gtb = gts[c].astype(BF16)
            dqt = _dot(da[c], kt[sl]) + _dot(do[sl], stp.astype(BF16))
            dkt = _dot_tn(da[c], qt[sl])
            dks = _dot(v[sl], gtb) * eend[sl]
            dv_scr[sl, :] = _dot_tn(a[c], do[sl]) + _dot_nt(ke[sl], gtb)
            dq_scr[sl, :] = dqt * eb[sl]
            dk_scr[sl, :] = dkt * enb[sl] + dks
            ech = (jnp.sum(k[sl] * dks, axis=0, keepdims=True)
                   + jnp.sum(gts[c] * jnp.exp(bc[c]) * stp, axis=0, keepdims=True))
            e_scr[sl, :] = jnp.broadcast_to(ech, (C, H))
        dq = dq_scr[...]
        dk = dk_scr[...]
        dlf = _tri_dot(upper, q * dq - k * dk) + e_scr[...]
        dft = dlf / f - dk
        dz_ref[0] = (dq * _dsilu(qp, sq)).astype(BF16)
        dz_ref[1] = (dft * (1.0 - lb) * sg * (1.0 - sg)).astype(BF16)
        dz_ref[2] = dv_scr[...].astype(BF16)
        dlb_ref[...] += jnp.sum(dft * (1.0 - sg), axis=0, keepdims=True)

    col = lambda kk: (lambda h, n: (nb - 1 - n, kk * nh + h))
    return pl.pallas_call(
        body, name="hgrn_bwd", grid=(nh, nb),
        out_shape=(jax.ShapeDtypeStruct((4, s, e), BF16), jax.ShapeDtypeStruct((1, e), F32),
                   jax.ShapeDtypeStruct((1, H), F32)),
        in_specs=[pl.BlockSpec((tb, H), col(0)), pl.BlockSpec((tb, H), col(1)),
                  pl.BlockSpec((tb, H), col(2)), pl.BlockSpec((tb, H), col(3)),
                  pl.BlockSpec((tb, H), lambda h, n: (nb - 1 - n, h)),
                  pl.BlockSpec((None, nc, H, H), lambda h, n: (h, nb - 1 - n, 0, 0)),
                  pl.BlockSpec((2, H), lambda h, n: (0, h)), pl.BlockSpec((1, H), lambda h, n: (0, 0))],
        out_specs=(pl.BlockSpec((4, tb, H), lambda h, n: (0, nb - 1 - n, h)),
                   pl.BlockSpec((1, H), lambda h, n: (0, h)), pl.BlockSpec((1, H), lambda h, n: (0, 0))),
        scratch_shapes=[pltpu.VMEM((H, H), F32)] + [pltpu.VMEM((tb, H), F32)] * 5,
        compiler_params=_params("arbitrary", "arbitrary"))(z, z, z, z, dy, states, lb_logits, gnorm)


ATTN_T = 16 * ATTN_BAND
SCALE = ATTN_HEAD ** -0.5
TILE_UNROLL = 2


def _slope(hh, nheads):
    head = (2 * pl.program_id(0) + hh + 1).astype(F32)
    return jnp.exp(jnp.full((1, 1), -8.0 / nheads * math.log(2.0), F32) * head)


def _fill_bias(bias, nheads, delta, edge_ok):
    band = (delta >= 0) & (delta <= ATTN_BAND)
    dist = delta.astype(F32)
    for pi, dil in enumerate(DILATIONS):
        for hh in range(2):
            full = jnp.where(band, -(_slope(hh, nheads) * float(dil)) * dist, NEG_BIG)
            bias[(pi * 2 + hh) * 2] = full
            bias[(pi * 2 + hh) * 2 + 1] = jnp.where(edge_ok, full, NEG_BIG)


def _rows(start, size, stride):
    if stride == 1:
        return pl.ds(pl.multiple_of(start, ATTN_BAND), size)
    return pl.ds(start, size, stride=stride)


def _head_lanes(rows, hh):
    return (lax.broadcasted_iota(jnp.int32, (rows, LANES), 1) // ATTN_HEAD) == hh


def _attn_fwd(z):
    s = z.shape[0]
    e = z.shape[1] // N_SPLITS
    npair = e // LANES
    T = ATTN_T
    assert s % T == 0
    nsb = s // T
    W = ATTN_BAND
    nt = T // W
    HD = ATTN_HEAD
    chunk = 256

    def body(q_ref, kp_ref, kc_ref, vp_ref, vc_ref, g_ref, o_ref, l_ref, y_ref, qa, kbuf, va, bias, accs, ms, msw, lsw):
        sb = pl.program_id(1)
        for hh in range(2):
            def stage(i, carry):
                rows = pl.ds(pl.multiple_of(i * chunk, chunk), chunk)
                mine = _head_lanes(chunk, hh)
                qa[hh, rows, :] = jnp.where(mine, q_ref[rows, :] * SCALE, 0.0)
                va[hh, rows, :] = jnp.where(mine, vp_ref[rows, :], 1.0)
                va[hh, pl.ds(pl.multiple_of(T + i * chunk, chunk), chunk), :] = jnp.where(mine, vc_ref[rows, :], 1.0)
                return carry

            lax.fori_loop(0, T // chunk, stage, 0)
        kbuf[0:T, :] = kp_ref[...]
        kbuf[T:, :] = kc_ref[...]
        qi = lax.broadcasted_iota(jnp.int32, (W, 2 * W), 0)
        kj = lax.broadcasted_iota(jnp.int32, (W, 2 * W), 1)
        _fill_bias(bias, 2 * npair, W + qi - kj, kj >= W)

        def tile(tau, carry):
            first = _head_lanes(W, 0)
            rows, scores = [], []
            for pi, dil in enumerate(DILATIONS):
                r = tau % dil
                ub = tau // dil
                qrows = _rows(r + dil * W * ub, W, dil)
                krows = _rows(T + dil * W * (ub - 1) + r, 2 * W, dil)
                var = jnp.where((sb == 0) & (ub == 0), 1, 0)
                kt = kbuf[krows, :].astype(BF16)
                rows.append((qrows, krows))
                scores.append([_dot_nt(qa[hh, qrows, :].astype(BF16), kt) + bias[(pi * 2 + hh) * 2 + var]
                               for hh in range(2)])
            maxes = [[jnp.max(sc, axis=-1, keepdims=True) for sc in pair] for pair in scores]
            probs = [[jnp.exp(sc - m).astype(BF16) for sc, m in zip(ps, pm)] for ps, pm in zip(scores, maxes)]
            for pi, (qrows, krows) in enumerate(rows):
                outs = [_dot(probs[pi][hh], va[hh, krows, :].astype(BF16)) for hh in range(2)]
                accs[pi, qrows, :] = jnp.where(first, outs[0], outs[1])
                lsw[pi, qrows, :] = jnp.where(first, outs[1], outs[0])
                ms[pi, qrows, :] = jnp.where(first, maxes[pi][0], maxes[pi][1])
                msw[pi, qrows, :] = jnp.where(first, maxes[pi][1], maxes[pi][0])
            return carry

        lax.fori_loop(0, nt, tile, 0, unroll=TILE_UNROLL)

        def merge(i, carry):
            rows = pl.ds(pl.multiple_of(i * chunk, chunk), chunk)
            m1, m2, m3 = ms[0, rows, :], ms[1, rows, :], ms[2, rows, :]
            mx = jnp.maximum(jnp.maximum(m1, m2), m3)
            s1, s2, s3 = msw[0, rows, :], msw[1, rows, :], msw[2, rows, :]
            sx = jnp.maximum(jnp.maximum(s1, s2), s3)
            den_sw = (jnp.exp(s1 - sx) * lsw[0, rows, :] + jnp.exp(s2 - sx) * lsw[1, rows, :]
                      + jnp.exp(s3 - sx) * lsw[2, rows, :])
            den = pltpu.roll(den_sw, ATTN_HEAD, 1)
            o = (jnp.exp(m1 - mx) * accs[0, rows, :] + jnp.exp(m2 - mx) * accs[1, rows, :]
                 + jnp.exp(m3 - mx) * accs[2, rows, :]) / den
            o_ref[rows, :] = o
            l_ref[rows, :] = mx + jnp.log(den)
            gp = g_ref[rows, :]
            y_ref[rows, :] = (o * (gp * _sigmoid(gp))).astype(BF16)
            return carry

        lax.fori_loop(0, T // chunk, merge, 0)

    cur = lambda split: (lambda hp, sb: (sb, split * npair + hp))
    prev = lambda split: (lambda hp, sb: (jnp.maximum(sb - 1, 0), split * npair + hp))
    blk = lambda index: pl.BlockSpec((T, LANES), index)
    out = blk(lambda hp, sb: (sb, hp))
    buf = lambda rows: pltpu.VMEM((rows, LANES), F32)
    return pl.pallas_call(
        body, name="attn_fwd", grid=(npair, nsb),
        out_shape=(jax.ShapeDtypeStruct((s, e), F32), jax.ShapeDtypeStruct((s, e), F32), jax.ShapeDtypeStruct((s, e), BF16)),
        in_specs=[blk(cur(4)), blk(prev(5)), blk(cur(5)), blk(prev(6)), blk(cur(6)), blk(cur(7))],
        out_specs=(out, out, out),
        scratch_shapes=[pltpu.VMEM((2, T, LANES), F32), buf(2 * T), pltpu.VMEM((2, 2 * T, LANES), F32),
                        pltpu.VMEM((12, W, 2 * W), F32)] + [pltpu.VMEM((3, T, LANES), F32)] * 4,
        compiler_params=_params("parallel", "arbitrary"))(z, z, z, z, z, z)


def _outproj_loss(x, y_h, y_a, w_out_full, final_gain, target):
    s, d = x.shape
    e = y_h.shape[1]
    tm = _tile(s, 256)

    def body(x_ref, yh_ref, ya_ref, w_ref, g_ref, t_ref, dx_ref, dxb_ref, dy_ref, loss_ref, dg_ref):
        @pl.when(pl.program_id(0) == 0)
        def _():
            loss_ref[...] = jnp.zeros_like(loss_ref)
            dg_ref[...] = jnp.zeros_like(dg_ref)

        w = w_ref[...]
        x2 = x_ref[...] + _dot(yh_ref[...], w[0:e]) + _dot(ya_ref[...], w[e:2 * e])
        r = lax.rsqrt(jnp.mean(x2 * x2, axis=-1, keepdims=True) + NORM_EPS)
        xn = x2 * r
        g = g_ref[...]
        err = xn * g - t_ref[...]
        loss_ref[...] += jnp.sum(err * err, axis=0, keepdims=True) * (0.5 / d)
        dyo = err * (1.0 / d)
        dg_ref[...] += jnp.sum(dyo * xn, axis=0, keepdims=True)
        u = dyo * g
        dx2 = r * (u - xn * jnp.mean(u * xn, axis=-1, keepdims=True))
        dx_ref[...] = dx2
        dxb = dx2.astype(BF16)
        dxb_ref[...] = dxb
        dy_ref[...] = _dot_nt(dxb, w)

    row = pl.BlockSpec((tm, d), lambda i: (i, 0))
    half = pl.BlockSpec((tm, e), lambda i: (i, 0))
    vec = pl.BlockSpec((1, d), lambda i: (0, 0))
    return pl.pallas_call(
        body, name="outproj_loss", grid=(s // tm,),
        out_shape=(jax.ShapeDtypeStruct((s, d), F32), jax.ShapeDtypeStruct((s, d), BF16),
                   jax.ShapeDtypeStruct((s, 2 * e), F32), jax.ShapeDtypeStruct((1, d), F32),
                   jax.ShapeDtypeStruct((1, d), F32)),
        in_specs=[row, half, half, pl.BlockSpec((2 * e, d), lambda i: (0, 0)), vec, row],
        out_specs=(row, row, pl.BlockSpec((tm, 2 * e), lambda i: (i, 0)), vec, vec),
        compiler_params=_params("arbitrary"))(x, y_h, y_a, w_out_full, final_gain, target)


def _attn_bwd(z, dy, o, lse):
    s, e = o.shape
    npair = e // LANES
    T = ATTN_T
    assert s % T == 0
    nsb = s // T
    W = ATTN_BAND
    nt = T // W
    HD = ATTN_HEAD
    chunk = 256

    def body(k_ref, v_ref, qc_ref, qn_ref, dyc_ref, dyn_ref, gc_ref, gn_ref, oc_ref, on_ref, lc_ref, ln_ref,
             dz_ref, qa, doa, ka, va, dqacc, dkacc, dvacc, bias):
        sb = pl.program_id(1)
        def stage_queries(half, q_r, dy_r, g_r, o_r, l_r):
            def stage(i, carry):
                rows = pl.ds(pl.multiple_of(i * chunk, chunk), chunk)
                dst = pl.ds(pl.multiple_of(half * T + i * chunk, chunk), chunk)
                lane = lax.broadcasted_iota(jnp.int32, (chunk, LANES), 1)
                gp = g_r[rows, :]
                dov = dy_r[rows, :] * (gp * _sigmoid(gp))
                prod = dov * o_r[rows, :]
                qv = q_r[rows, :] * SCALE
                lv = l_r[rows, :]
                for hh in range(2):
                    mine = _head_lanes(chunk, hh)
                    spare = (1 - hh) * HD
                    lse_parts = _split3(lv[:, hh * HD:hh * HD + 1])
                    dl_parts = _split3(jnp.sum(prod[:, hh * HD:(hh + 1) * HD], axis=-1, keepdims=True))
                    qh = jnp.where(mine, qv, 0.0)
                    dh = jnp.where(mine, dov, 0.0)
                    for j in range(3):
                        qh = jnp.where(lane == spare + j, lse_parts[j], qh)
                        dh = jnp.where(lane == spare + j, dl_parts[j], dh)
                    qa[hh, dst, :] = qh
                    doa[hh, dst, :] = dh
                return carry

            lax.fori_loop(0, T // chunk, stage, 0)

        @pl.when(sb == 0)
        def _():
            stage_queries(0, qc_ref, dyc_ref, gc_ref, oc_ref, lc_ref)

        stage_queries(1, qn_ref, dyn_ref, gn_ref, on_ref, ln_ref)

        def stage_keys(i, carry):
            rows = pl.ds(pl.multiple_of(i * chunk, chunk), chunk)
            lane = lax.broadcasted_iota(jnp.int32, (chunk, LANES), 1)
            for hh in range(2):
                spare = (1 - hh) * HD
                minus = (lane >= spare) & (lane < spare + 3)
                ka[hh, rows, :] = jnp.where(minus, -1.0, k_ref[rows, :])
                va[hh, rows, :] = jnp.where(minus, -1.0, v_ref[rows, :])
            gp = gc_ref[rows, :]
            dz_ref[3, rows, :] = (dyc_ref[rows, :] * oc_ref[rows, :] * _dsilu(gp, _sigmoid(gp))).astype(BF16)
            return carry

        lax.fori_loop(0, T // chunk, stage_keys, 0)

        @pl.when(sb == 0)
        def _():
            dqacc[0:T, :] = jnp.zeros((T, LANES), F32)

        dqacc[T:, :] = jnp.zeros((T, LANES), F32)
        dkacc[...] = jnp.zeros_like(dkacc)
        dvacc[...] = jnp.zeros_like(dvacc)
        qi = lax.broadcasted_iota(jnp.int32, (2 * W, W), 0)
        kj = lax.broadcasted_iota(jnp.int32, (2 * W, W), 1)
        _fill_bias(bias, 2 * npair, qi - kj, qi < W)

        def tile(tau, carry):
            rows, ops, sc, dpd = [], [], [], []
            for pi, dil in enumerate(DILATIONS):
                r = tau % dil
                ub = tau // dil
                start = r + dil * W * ub
                krows = _rows(start, W, dil)
                qrows = _rows(start, 2 * W, dil)
                var = jnp.where((sb == nsb - 1) & (ub == nt // dil - 1), 1, 0)
                rows.append((krows, qrows))
                for hh in range(2):
                    kt = ka[hh, krows, :].astype(BF16)
                    vt = va[hh, krows, :].astype(BF16)
                    qt = qa[hh, qrows, :].astype(BF16)
                    dt = doa[hh, qrows, :].astype(BF16)
                    ops.append((kt, qt, dt))
                    sc.append(_dot_nt(qt, kt) + bias[(pi * 2 + hh) * 2 + var])
                    dpd.append(_dot_nt(dt, vt))
            ps = [jnp.exp(s) for s in sc]
            dss = [(p * d).astype(BF16) for p, d in zip(ps, dpd)]
            pbs = [p.astype(BF16) for p in ps]
            dvs = [_dot_tn(pb, dt) for pb, (kt, qt, dt) in zip(pbs, ops)]
            dks = [_dot_tn(ds, qt) for ds, (kt, qt, dt) in zip(dss, ops)]
            dqs = [_dot(ds, kt) for ds, (kt, qt, dt) in zip(dss, ops)]
            for pi, (krows, qrows) in enumerate(rows):
                dkacc[krows, :] += jnp.where(_head_lanes(W, 0), dks[2 * pi], dks[2 * pi + 1])
                dvacc[krows, :] += jnp.where(_head_lanes(W, 0), dvs[2 * pi], dvs[2 * pi + 1])
                dqacc[qrows, :] += jnp.where(_head_lanes(2 * W, 0), dqs[2 * pi], dqs[2 * pi + 1]) * SCALE
            return carry

        lax.fori_loop(0, nt, tile, 0, unroll=TILE_UNROLL)

        def flush(i, carry):
            rows = pl.ds(pl.multiple_of(i * chunk, chunk), chunk)
            nxt = pl.ds(pl.multiple_of(T + i * chunk, chunk), chunk)
            dz_ref[0, rows, :] = dqacc[rows, :].astype(BF16)
            dz_ref[1, rows, :] = dkacc[rows, :].astype(BF16)
            dz_ref[2, rows, :] = dvacc[rows, :].astype(BF16)
            dqacc[rows, :] = dqacc[nxt, :]
            for hh in range(2):
                qa[hh, rows, :] = qa[hh, nxt, :]
                doa[hh, rows, :] = doa[hh, nxt, :]
            return carry

        lax.fori_loop(0, T // chunk, flush, 0)

    zc = lambda split: (lambda hp, sb: (sb, split * npair + hp))
    zn = lambda split: (lambda hp, sb: (jnp.minimum(sb + 1, nsb - 1), split * npair + hp))
    ec = lambda off: (lambda hp, sb: (sb, off + hp))
    en = lambda off: (lambda hp, sb: (jnp.minimum(sb + 1, nsb - 1), off + hp))
    blk = lambda index: pl.BlockSpec((T, LANES), index)
    buf = lambda rows: pltpu.VMEM((rows, LANES), F32)
    return pl.pallas_call(
        body, name="attn_bwd", grid=(npair, nsb), out_shape=jax.ShapeDtypeStruct((4, s, e), BF16),
        in_specs=[blk(zc(5)), blk(zc(6)), blk(zc(4)), blk(zn(4)), blk(ec(npair)), blk(en(npair)),
                  blk(zc(7)), blk(zn(7)), blk(ec(0)), blk(en(0)), blk(ec(0)), blk(en(0))],
        out_specs=pl.BlockSpec((4, T, LANES), lambda hp, sb: (0, sb, hp)),
        scratch_shapes=[pltpu.VMEM((2, 2 * T, LANES), F32), pltpu.VMEM((2, 2 * T, LANES), F32),
                        pltpu.VMEM((2, T, LANES), F32), pltpu.VMEM((2, T, LANES), F32),
                        buf(2 * T), buf(T), buf(T), pltpu.VMEM((12, 2 * W, W), F32)],
        compiler_params=_params("parallel", "arbitrary"))(z, z, z, z, dy, dy, z, z, o, o, lse, lse)


def _dz_specs(tm, e, axis):
    def mk(lo, hi):
        def index(i, k):
            row, grp = (i, k) if axis == 1 else (k, i)
            return (jnp.clip(grp - lo, 0, hi - lo - 1), row, 0)
        return pl.BlockSpec((None, tm, e), index)
    return [mk(0, 4), mk(4, 8)]


def _dz_pick(grp, dzh_ref, dza_ref, fn):
    @pl.when(grp < 4)
    def _():
        fn(dzh_ref[...])

    @pl.when(grp >= 4)
    def _():
        fn(dza_ref[...])


def _dh_dx(dzh, dza, w_full, x, gain, dx2):
    s, d = x.shape
    e = dzh.shape[2]
    tm = _tile(s, 512)

    def body(dzh_ref, dza_ref, w_ref, x_ref, g_ref, dx2_ref, gx_ref, dg_ref, acc):
        i, k = pl.program_id(0), pl.program_id(1)

        @pl.when((i == 0) & (k == 0))
        def _():
            dg_ref[...] = jnp.zeros_like(dg_ref)

        @pl.when(k == 0)
        def _():
            acc[...] = jnp.zeros_like(acc)

        def add(dz):
            acc[...] += _dot_nt(dz, w_ref[...])

        _dz_pick(k, dzh_ref, dza_ref, add)

        @pl.when(k == N_SPLITS - 1)
        def _():
            dh = acc[...]
            xv = x_ref[...]
            r = lax.rsqrt(jnp.mean(xv * xv, axis=-1, keepdims=True) + NORM_EPS)
            xn = xv * r
            dg_ref[...] += jnp.sum(dh * xn, axis=0, keepdims=True)
            u = dh * g_ref[...]
            gx_ref[...] = dx2_ref[...] + r * (u - xn * jnp.mean(u * xn, axis=-1, keepdims=True))

    row = pl.BlockSpec((tm, d), lambda i, k: (i, 0))
    vec = pl.BlockSpec((1, d), lambda i, k: (0, 0))
    return pl.pallas_call(
        body, name="dh_dx", grid=(s // tm, N_SPLITS),
        out_shape=(jax.ShapeDtypeStruct((s, d), F32), jax.ShapeDtypeStruct((1, d), F32)),
        in_specs=_dz_specs(tm, e, 1) + [pl.BlockSpec((None, d, e), lambda i, k: (k, 0, 0)), row, vec, row],
        out_specs=(row, vec), scratch_shapes=[pltpu.VMEM((tm, d), F32)],
        compiler_params=_params("arbitrary", "arbitrary"))(dzh, dza, w_full, x, gain, dx2)


def _position():
    x, y, c = lax.axis_index("x"), lax.axis_index("y"), lax.axis_index("c")
    return x, y, c


def _xor_peer(x, y, c, mask):
    return (x ^ ((mask >> 2) & 1), y ^ ((mask >> 1) & 1), c ^ (mask & 1))


def _block_order(masks):
    me = 4 * lax.axis_index("x") + 2 * lax.axis_index("y") + lax.axis_index("c")
    return jnp.stack([me ^ m for m in masks]).astype(jnp.int32)


GATHER_MASKS = (0, 1, 4, 2, 6, 5, 3, 7)


def _inproj_gather(h, w_loc, wo_loc):
    s, d = h.shape
    e = w_loc.shape[1]
    tm = _tile(s, 512)
    ni = s // tm
    pre = max(ni - 3, 0)

    def body(order_ref, h_ref, w_ref, wo_ref, z_ref, wf_ref, wof_ref, wbuf, send_sems, recv_sems, osend, orecv,
             local_sems, wsems):
        j, i = pl.program_id(0), pl.program_id(1)
        x, y, c = _position()
        me, sibling = (x, y, c), (x, y, 1 - c)
        chips = [(1 - x, y), (x, 1 - y), (1 - x, 1 - y)]
        blk = lambda p: 4 * p[0] + 2 * p[1] + p[2]

        def copy(k, block, to, src=None):
            dst = wf_ref.at[blk(block)]
            return pltpu.make_async_remote_copy(
                src_ref=dst if src is None else src, dst_ref=dst, send_sem=send_sems.at[k], recv_sem=recv_sems.at[k],
                device_id=to, device_id_type=MESH)

        first = [copy(0, me, sibling, src=w_ref)] + [copy(1 + q, me, (*chip, c), src=w_ref) for q, chip in enumerate(chips)]
        passed = [copy(4 + q, (*chip, c), sibling) for q, chip in enumerate(chips)]
        mine = pltpu.make_async_copy(w_ref, wf_ref.at[blk(me)], local_sems.at[0])
        ocopies = [pltpu.make_async_remote_copy(
            src_ref=wo_ref, dst_ref=wof_ref.at[blk(me)], send_sem=osend.at[k], recv_sem=orecv.at[k],
            device_id=_xor_peer(x, y, c, k + 1), device_id_type=MESH) for k in range(N_DEV - 1)]
        omine = pltpu.make_async_copy(wo_ref, wof_ref.at[blk(me)], local_sems.at[1])
        blocks = [me, sibling] + [(*chip, c) for chip in chips] + [(*chip, 1 - c) for chip in chips]
        arrive = [None, copy(0, sibling, me)] + [copy(1 + q, (*chip, c), me) for q, chip in enumerate(chips)] \
            + [copy(4 + q, (*chip, 1 - c), me) for q, chip in enumerate(chips)]
        forward = [None, None] + passed + [None, None, None]

        def load(slot, src):
            return pltpu.make_async_copy(src, wbuf.at[slot], wsems.at[slot])

        @pl.when((j == 0) & (i == 0))
        def _():
            for cp in [mine, omine] + first + ocopies:
                cp.start()
            load(0, w_ref).start()

        for jj in range(N_DEV):
            @pl.when((j == jj) & (i == 0))
            def _():
                load(jj % 2, w_ref).wait()

            if jj + 1 < N_DEV:
                @pl.when((j == jj) & (i == pre))
                def _():
                    arrive[jj + 1].wait_recv()
                    if forward[jj + 1] is not None:
                        forward[jj + 1].start()
                    load((jj + 1) % 2, wf_ref.at[blk(blocks[jj + 1])]).start()

        z_ref[...] = _dot(h_ref[...], wbuf[j % 2])

        @pl.when((j == N_DEV - 1) & (i == ni - 1))
        def _():
            for cp in first + passed:
                cp.wait_send()
            for cp in ocopies:
                cp.wait_send()
                cp.wait_recv()
            mine.wait()
            omine.wait()

    grid_spec = pltpu.PrefetchScalarGridSpec(
        num_scalar_prefetch=1, grid=(N_DEV, ni),
        in_specs=[pl.BlockSpec((tm, d), lambda j, i, o: (i, 0)), ANY, ANY],
        out_specs=(pl.BlockSpec((tm, e), lambda j, i, o: (i, o[j])), ANY, ANY),
        scratch_shapes=[pltpu.VMEM((2, d, e), BF16), pltpu.SemaphoreType.DMA((7,)), pltpu.SemaphoreType.DMA((7,)),
                        pltpu.SemaphoreType.DMA((7,)), pltpu.SemaphoreType.DMA((7,)), pltpu.SemaphoreType.DMA((2,)),
                        pltpu.SemaphoreType.DMA((2,))])
    return pl.pallas_call(
        body, name="inproj_gather", grid_spec=grid_spec,
        out_shape=(jax.ShapeDtypeStruct((s, N_SPLITS * e), F32), jax.ShapeDtypeStruct((N_DEV, d, e), BF16),
                   jax.ShapeDtypeStruct((N_DEV,) + wo_loc.shape, BF16)),
        compiler_params=_params("arbitrary", "arbitrary"))(_block_order(GATHER_MASKS), h, w_loc, wo_loc)


SCATTER_MASKS = (7, 6, 5, 4, 3, 2, 1, 0)
N_CHIPS = 4


def _scatter_block(k, acc, stage, tmp, own_ref, ra_ref, rb_ref, sa_send, sa_recv, sb_send, sb_recv, loc_sem, last):
    x, y, c = _position()
    chip_of = lambda t: _xor_peer(x, y, c, SCATTER_MASKS[2 * t + 1])

    def ship(t):
        return pltpu.make_async_remote_copy(
            src_ref=stage.at[0], dst_ref=ra_ref.at[t], send_sem=sa_send.at[t], recv_sem=sa_recv.at[t],
            device_id=(x, y, 1 - c), device_id_type=MESH)

    def send(t):
        return pltpu.make_async_remote_copy(
            src_ref=stage.at[1], dst_ref=rb_ref.at[t], send_sem=sb_send.at[t], recv_sem=sb_recv.at[t],
            device_id=chip_of(t), device_id_type=MESH)

    for kk in range(N_DEV):
        t = kk // 2

        @pl.when(last & (k == kk))
        def _():
            if kk % 2 == 0:
                if t >= 1:
                    ship(t - 1).wait_send()
                stage[0] = acc[...].astype(BF16)
                ship(t).start()
            else:
                ship(t).wait_recv()
                fetch = pltpu.make_async_copy(ra_ref.at[t], tmp, loc_sem)
                fetch.start()
                fetch.wait()
                acc[...] += tmp[...].astype(F32)
                if t < N_CHIPS - 1:
                    if t >= 1:
                        send(t - 1).wait_send()
                    stage[1] = acc[...].astype(BF16)
                    send(t).start()
                else:
                    keep = pltpu.make_async_copy(acc, own_ref, loc_sem)
                    keep.start()
                    keep.wait()
                    ship(t).wait_send()
                    send(t - 1).wait_send()
                    for q in range(N_CHIPS - 1):
                        send(q).wait_recv()


def _scatter_scratch(rows, cols):
    return [pltpu.VMEM((rows, cols), F32), pltpu.VMEM((2, rows, cols), BF16), pltpu.VMEM((rows, cols), BF16),
            pltpu.SemaphoreType.DMA((N_CHIPS,)), pltpu.SemaphoreType.DMA((N_CHIPS,)),
            pltpu.SemaphoreType.DMA((N_CHIPS - 1,)), pltpu.SemaphoreType.DMA((N_CHIPS - 1,)), pltpu.SemaphoreType.DMA(())]


def _scatter_out(rows, cols):
    return (jax.ShapeDtypeStruct((rows, cols), F32), jax.ShapeDtypeStruct((N_CHIPS, rows, cols), BF16),
            jax.ShapeDtypeStruct((N_CHIPS - 1, rows, cols), BF16))


def _dwin_scatter(h, dzh, dza):
    s, d = h.shape
    e = dzh.shape[2]
    ts = _tile(s, 512)
    ns = s // ts

    def body(order_ref, dzh_ref, dza_ref, h_ref, own_ref, ra_ref, rb_ref, acc, stage, tmp, *sems):
        k, step = pl.program_id(0), pl.program_id(1)

        @pl.when(step == 0)
        def _():
            acc[...] = jnp.zeros_like(acc)

        def add(dz):
            acc[...] += _dot_tn(h_ref[...], dz)

        _dz_pick(order_ref[k], dzh_ref, dza_ref, add)
        _scatter_block(k, acc, stage, tmp, own_ref, ra_ref, rb_ref, *sems, step == ns - 1)

    def dz_spec(lo):
        return pl.BlockSpec((None, ts, e), lambda k, st, o: (jnp.clip(o[k] - lo, 0, 3), st, 0))

    grid_spec = pltpu.PrefetchScalarGridSpec(
        num_scalar_prefetch=1, grid=(N_DEV, ns),
        in_specs=[dz_spec(0), dz_spec(4), pl.BlockSpec((ts, d), lambda k, st, o: (st, 0))],
        out_specs=(ANY, ANY, ANY), scratch_shapes=_scatter_scratch(d, e))
    own, _, rb = pl.pallas_call(
        body, name="dwin_scatter", grid_spec=grid_spec, out_shape=_scatter_out(d, e),
        compiler_params=_params("arbitrary", "arbitrary"))(_block_order(SCATTER_MASKS), dzh, dza, h)
    return own, rb


def _dwout_scatter(y_h, y_a, dxb):
    s, e = y_h.shape
    d = dxb.shape[1]
    r = 2 * e // N_DEV
    per = e // r
    ts = _tile(s, 512)
    ns = s // ts

    def body(order_ref, yh_ref, ya_ref, dx_ref, own_ref, ra_ref, rb_ref, acc, stage, tmp, *sems):
        k, step = pl.program_id(0), pl.program_id(1)

        @pl.when(step == 0)
        def _():
            acc[...] = jnp.zeros_like(acc)

        @pl.when(order_ref[k] < per)
        def _():
            acc[...] += _dot_tn(yh_ref[...], dx_ref[...])

        @pl.when(order_ref[k] >= per)
        def _():
            acc[...] += _dot_tn(ya_ref[...], dx_ref[...])

        _scatter_block(k, acc, stage, tmp, own_ref, ra_ref, rb_ref, *sems, step == ns - 1)

    def y_spec(lo):
        return pl.BlockSpec((ts, r), lambda k, st, o: (st, jnp.clip(o[k] - lo, 0, per - 1)))

    grid_spec = pltpu.PrefetchScalarGridSpec(
        num_scalar_prefetch=1, grid=(N_DEV, ns),
        in_specs=[y_spec(0), y_spec(per), pl.BlockSpec((ts, d), lambda k, st, o: (st, 0))],
        out_specs=(ANY, ANY, ANY), scratch_shapes=_scatter_scratch(r, d))
    own, _, rb = pl.pallas_call(
        body, name="dwout_scatter", grid_spec=grid_spec, out_shape=_scatter_out(r, d),
        compiler_params=_params("arbitrary", "arbitrary"))(_block_order(SCATTER_MASKS), y_h, y_a, dxb)
    return own, rb


def _sum_chips_adamw(own, recv, w, m, v):
    r, c = w.shape
    tr = _tile(r, 128)

    def body(own_ref, rc_ref, w_ref, m_ref, v_ref, g_ref, d_ref, mo_ref, vo_ref):
        g = own_ref[...]
        for q in range(N_CHIPS - 1):
            g = g + rc_ref[q].astype(F32)
        g_ref[...] = g
        d_ref[...], mo_ref[...], vo_ref[...] = _adamw(w_ref[...], g, m_ref[...], v_ref[...])

    blk = pl.BlockSpec((tr, c), lambda i: (i, 0))
    shp = jax.ShapeDtypeStruct((r, c), F32)
    return pl.pallas_call(
        body, name="sum_chips_adamw", grid=(r // tr,), out_shape=(shp, shp, shp, shp),
        in_specs=[blk, pl.BlockSpec((N_CHIPS - 1, tr, c), lambda i: (0, i, 0)), blk, blk, blk],
        out_specs=(blk, blk, blk, blk), compiler_params=_params("parallel"))(own, recv, w, m, v)


SMALL_ROWS = 8
ROW_LB = 4
ROW_GN = 6
ROW_LOSS = 7


def _small_allreduce_adamw(part, w, m, v, lb_logits):
    width = part.shape[1]

    def body(p_ref, w_ref, m_ref, v_ref, lb_ref, g_ref, d_ref, mo_ref, vo_ref, buf, send_sems, recv_sems):
        x, y, c = _position()
        me = 4 * x + 2 * y + c
        buf[me] = p_ref[...]
        copies = []
        for k in range(N_DEV - 1):
            bx, by, bc = ((k + 1) >> 2) & 1, ((k + 1) >> 1) & 1, (k + 1) & 1
            peer = (x ^ bx, y ^ by, c ^ bc)
            copies.append(pltpu.make_async_remote_copy(
                src_ref=p_ref, dst_ref=buf.at[me], send_sem=send_sems.at[k], recv_sem=recv_sems.at[k],
                device_id=peer, device_id_type=MESH))
        for cp in copies:
            cp.start()
        for cp in copies:
            cp.wait_recv()
        for cp in copies:
            cp.wait_send()
        tot = buf[0]
        for dev in range(1, N_DEV):
            tot = tot + buf[dev]
        lbv = lb_ref[...]
        lb = _sigmoid(lbv[0:1] - lbv[1:2])
        glb = tot[ROW_LB:ROW_LB + 1] * lb * (1.0 - lb)
        loss = jnp.sum(tot[ROW_LOSS:ROW_LOSS + 1], axis=-1, keepdims=True)
        row = lax.broadcasted_iota(jnp.int32, (SMALL_ROWS, width), 0)
        g = jnp.where(row == ROW_LB, glb, jnp.where(row == ROW_LB + 1, -glb, tot))
        g = jnp.where(row == ROW_LOSS, loss, g)
        g_ref[...] = g
        d_ref[...], mo_ref[...], vo_ref[...] = _adamw(w_ref[...], g, m_ref[...], v_ref[...])

    vm = pl.BlockSpec(memory_space=pltpu.VMEM)
    shp = jax.ShapeDtypeStruct((SMALL_ROWS, width), F32)
    return pl.pallas_call(
        body, name="small_allreduce_adamw", out_shape=(shp, shp, shp, shp),
        in_specs=[vm] * 5, out_specs=(vm, vm, vm, vm),
        scratch_shapes=[pltpu.VMEM((N_DEV, SMALL_ROWS, width), F32), pltpu.SemaphoreType.DMA((N_DEV - 1,)),
                        pltpu.SemaphoreType.DMA((N_DEV - 1,))],
    )(part, w, m, v, lb_logits)


def _pack_small(norm_gain, final_gain, lb2, gnorm, last_row, width):
    pad = lambda a: jnp.pad(a.reshape(1, -1), ((0, 0), (0, width - a.size)))
    return jnp.concatenate([norm_gain.reshape(2, width), final_gain.reshape(2, width), lb2.reshape(2, width),
                            pad(gnorm), last_row.reshape(1, width)], axis=0)


def _unpack_small(p, d, e, hd):
    return (p[0:2].reshape(1, d), p[2:4].reshape(d), p[4:6].reshape(2, e), p[6:7, :hd].reshape(1, hd))


def kernel(x, norm_gain, w_in, lb_logits, hgrn_gnorm, w_out, final_gain, loss_target, m_norm_gain, m_w_in, m_lb_logits, m_hgrn_gnorm, m_w_out, m_final_gain, v_norm_gain, v_w_in, v_lb_logits, v_hgrn_gnorm, v_w_out, v_final_gain):
    s, d = x.shape[1], x.shape[2]
    e = w_in.shape[2]
    assert d == 2 * e and lb_logits.shape == (2, e) and w_out.shape[1] * N_DEV == 2 * e
    x2d = x.reshape(s, d)
    tgt = loss_target.reshape(s, d)

    h = _rmsnorm_fwd(x2d, norm_gain)
    z, w_in_full, w_out_full = _inproj_gather(h, _cast_bf16(w_in[0]), _cast_bf16(w_out[0]))
    w_out_full = w_out_full.reshape(2 * e, d)
    y_h, states = _hgrn_fwd(z, lb_logits, hgrn_gnorm)
    o_attn, lse, y_a = _attn_fwd(z)
    dx2, dx2b, dy, loss_vec, dfg = _outproj_loss(x2d, y_h, y_a, w_out_full, final_gain.reshape(1, d), tgt)

    own_o, recv_o = _dwout_scatter(y_h, y_a, dx2b)
    dza = _attn_bwd(z, dy, o_attn, lse)
    dzh, dlb, dgn = _hgrn_bwd(z, dy, states, lb_logits, hgrn_gnorm)
    grad_x, dng = _dh_dx(dzh, dza, w_in_full, x2d, norm_gain, dx2)
    own_i, recv_i = _dwin_scatter(h, dzh, dza)
    g_wi, d_wi, nm_wi, nv_wi = _sum_chips_adamw(own_i, recv_i, w_in[0], m_w_in[0], v_w_in[0])
    g_wo, d_wo, nm_wo, nv_wo = _sum_chips_adamw(own_o, recv_o, w_out[0], m_w_out[0], v_w_out[0])

    width = d // 2
    zero_row = jnp.zeros((1, width), F32)
    loss_row = loss_vec[:, :width] + loss_vec[:, width:]
    part = _pack_small(dng, dfg, jnp.concatenate([dlb, zero_row], axis=0), dgn, loss_row, width)
    pw = _pack_small(norm_gain, final_gain, lb_logits, hgrn_gnorm, zero_row, width)
    pm = _pack_small(m_norm_gain, m_final_gain, m_lb_logits, m_hgrn_gnorm, zero_row, width)
    pv = _pack_small(v_norm_gain, v_final_gain, v_lb_logits, v_hgrn_gnorm, zero_row, width)
    sg, sd, sm, sv = _small_allreduce_adamw(part, pw, pm, pv, lb_logits)
    hd = hgrn_gnorm.shape[1]
    g_ng, g_fg, g_lb, g_gn = _unpack_small(sg, d, e, hd)
    d_ng, d_fg, d_lb, d_gn = _unpack_small(sd, d, e, hd)
    m_ng, m_fg, m_lb, m_gn = _unpack_small(sm, d, e, hd)
    v_ng, v_fg, v_lb, v_gn = _unpack_small(sv, d, e, hd)
    loss = sg[ROW_LOSS, 0]

    one = lambda a: a[None]
    return (loss, grad_x.reshape(1, s, d), g_ng, one(g_wi), g_lb, g_gn, one(g_wo), g_fg,
            d_ng, one(d_wi), d_lb, d_gn, one(d_wo), d_fg,
            m_ng, one(nm_wi), m_lb, m_gn, one(nm_wo), m_fg,
            v_ng, one(nv_wi), v_lb, v_gn, one(nv_wo), v_fg)
```

```python
import functools
import math

import jax
import jax.numpy as jnp
from jax import lax
from jax.experimental import pallas as pl
from jax.experimental.pallas import tpu as pltpu

NORM_EPS = 1e-6
HGRN_HEAD = 128
HGRN_CHUNK = 64
ATTN_HEAD = 64
ATTN_BAND = 128
DILATIONS = (1, 4, 16)
N_SPLITS = 8
N_DEV = 8
ADAM_LR = 0.001
ADAM_B1 = 0.9
ADAM_B2 = 0.999
ADAM_EPS = 1e-08
ADAM_WD = 0.01
ADAM_STEP = 10
LANES = 128
MESH = pl.DeviceIdType.MESH
F32 = jnp.float32
BF16 = jnp.bfloat16
NEG_BIG = -1e30
VMEM_LIMIT = 56 * 1024 * 1024

ANY = pl.BlockSpec(memory_space=pl.ANY)


def _params(*sem):
    return pltpu.CompilerParams(dimension_semantics=sem, vmem_limit_bytes=VMEM_LIMIT)


def _tile(n, pref):
    t = min(n, pref)
    assert n % t == 0, (n, pref)
    return t


def _dot(a, b, precision=None):
    return jnp.dot(a, b, preferred_element_type=F32, precision=precision)


def _dot_nt(a, b):
    return lax.dot_general(a, b, (((1,), (1,)), ((), ())), preferred_element_type=F32)


def _dot_tn(a, b):
    return lax.dot_general(a, b, (((0,), (0,)), ((), ())), preferred_element_type=F32)


def _sigmoid(x):
    return 1.0 / (1.0 + jnp.exp(-x))


def _dsilu(x, s):
    return s * (1.0 + x * (1.0 - s))


def _adamw(w, g, m, v):
    m = ADAM_B1 * m + (1.0 - ADAM_B1) * g
    v = ADAM_B2 * v + (1.0 - ADAM_B2) * (g * g)
    m_hat = m / (1.0 - ADAM_B1 ** ADAM_STEP)
    v_hat = v / (1.0 - ADAM_B2 ** ADAM_STEP)
    delta = -ADAM_LR * (m_hat / (jnp.sqrt(v_hat) + ADAM_EPS) + ADAM_WD * w)
    return delta, m, v


def _cast_bf16(a):
    r, c = a.shape
    tr = _tile(r, 256)

    def body(a_ref, o_ref):
        o_ref[...] = a_ref[...].astype(BF16)

    return pl.pallas_call(
        body, name="cast_bf16", grid=(r // tr,), out_shape=jax.ShapeDtypeStruct((r, c), BF16),
        in_specs=[pl.BlockSpec((tr, c), lambda i: (i, 0))], out_specs=pl.BlockSpec((tr, c), lambda i: (i, 0)),
        compiler_params=_params("parallel"))(a)


def _rmsnorm_fwd(x, gain):
    s, d = x.shape
    tm = _tile(s, 512)

    def body(x_ref, g_ref, h_ref):
        xv = x_ref[...]
        r = lax.rsqrt(jnp.mean(xv * xv, axis=-1, keepdims=True) + NORM_EPS)
        h_ref[...] = (xv * r * g_ref[...]).astype(BF16)

    return pl.pallas_call(
        body, name="rmsnorm_fwd", grid=(s // tm,), out_shape=jax.ShapeDtypeStruct((s, d), BF16),
        in_specs=[pl.BlockSpec((tm, d), lambda i: (i, 0)), pl.BlockSpec((1, d), lambda i: (0, 0))],
        out_specs=pl.BlockSpec((tm, d), lambda i: (i, 0)), compiler_params=_params("parallel"))(x, gain)


HGRN_BLOCK = 1024
TRI_ROWS = 256


def _chunk_masks():
    tb = TRI_ROWS
    row = lax.broadcasted_iota(jnp.int32, (tb, tb), 0)
    col = lax.broadcasted_iota(jnp.int32, (tb, tb), 1)
    same = (row // HGRN_CHUNK) == (col // HGRN_CHUNK)
    lower = jnp.where(same & (col <= row), 1.0, 0.0).astype(BF16)
    upper = jnp.where(same & (col >= row), 1.0, 0.0).astype(BF16)
    return lower, upper


def _split3(a):
    hi = a.astype(BF16).astype(F32)
    mid = (a - hi).astype(BF16).astype(F32)
    lo = (a - hi - mid).astype(BF16).astype(F32)
    return hi, mid, lo


def _tri_dot(tri, x):
    hi, mid, lo = (p.astype(BF16) for p in _split3(x))
    outs = []
    for r in range(0, x.shape[0], TRI_ROWS):
        sl = slice(r, r + TRI_ROWS)
        outs.append(_dot(tri, hi[sl]) + _dot(tri, mid[sl]) + _dot(tri, lo[sl]))
    return outs[0] if len(outs) == 1 else jnp.concatenate(outs, axis=0)


def _hgrn_gates(qp, fp, lbv):
    lb = _sigmoid(lbv[0:1] - lbv[1:2])
    sq = _sigmoid(qp)
    q = qp * sq
    sg = _sigmoid(fp)
    f = lb + (1.0 - lb) * sg
    k = 1.0 - f
    return lb, sq, q, sg, f, k


def _hgrn_fwd(z, lb_logits, gnorm):
    s = z.shape[0]
    e = z.shape[1] // N_SPLITS
    nh = e // HGRN_HEAD
    tb = _tile(s, HGRN_BLOCK)
    nc = tb // HGRN_CHUNK
    nb = s // tb
    C = HGRN_CHUNK

    def body(q_ref, f_ref, i_ref, g_ref, lb_ref, gn_ref, y_ref, st_ref, state, o_scr):
        @pl.when(pl.program_id(1) == 0)
        def _():
            state[...] = jnp.zeros_like(state)

        lb, sq, q, sg, f, k = _hgrn_gates(q_ref[...], f_ref[...], lb_ref[...])
        lower, _ = _chunk_masks()
        b = _tri_dot(lower, jnp.log(f))
        b3 = b.reshape(nc, C, HGRN_HEAD)
        bc = b3[:, C - 1:C, :]
        qt = (q * jnp.exp(b)).astype(BF16)
        kt = (k * jnp.exp(-b)).astype(BF16)
        ke = (k.reshape(nc, C, HGRN_HEAD) * jnp.exp(bc - b3)).reshape(tb, HGRN_HEAD).astype(BF16)
        v = i_ref[...].astype(BF16)
        tri = lax.broadcasted_iota(jnp.int32, (C, C), 1) <= lax.broadcasted_iota(jnp.int32, (C, C), 0)
        sls = [slice(c * C, (c + 1) * C) for c in range(nc)]
        kv = [_dot_tn(v[sl], ke[sl]) for sl in sls]
        a = [jnp.where(tri, _dot_nt(qt[sl], kt[sl]), 0.0).astype(BF16) for sl in sls]
        st = state[...]
        sts = []
        for c in range(nc):
            sts.append(st)
            st_ref[c] = st
            st = st * jnp.exp(bc[c]) + kv[c]
        state[...] = st
        for c, sl in enumerate(sls):
            o_scr[sl, :] = _dot(a[c], v[sl]) + _dot_nt(qt[sl], sts[c].astype(BF16))
        o = o_scr[...]
        rms = lax.rsqrt(jnp.mean(o * o, axis=-1, keepdims=True) + NORM_EPS)
        gp = g_ref[...]
        y_ref[...] = (o * rms * gn_ref[...] * (gp * _sigmoid(gp))).astype(BF16)

    col = lambda kk: (lambda h, n: (n, kk * nh + h))
    return pl.pallas_call(
        body, name="hgrn_fwd", grid=(nh, nb),
        out_shape=(jax.ShapeDtypeStruct((s, e), BF16),
                   jax.ShapeDtypeStruct((nh, s // C, HGRN_HEAD, HGRN_HEAD), F32)),
        in_specs=[pl.BlockSpec((tb, HGRN_HEAD), col(0)), pl.BlockSpec((tb, HGRN_HEAD), col(1)),
                  pl.BlockSpec((tb, HGRN_HEAD), col(2)), pl.BlockSpec((tb, HGRN_HEAD), col(3)),
                  pl.BlockSpec((2, HGRN_HEAD), lambda h, n: (0, h)), pl.BlockSpec((1, HGRN_HEAD), lambda h, n: (0, 0))],
        out_specs=(pl.BlockSpec((tb, HGRN_HEAD), lambda h, n: (n, h)),
                   pl.BlockSpec((None, nc, HGRN_HEAD, HGRN_HEAD), lambda h, n: (h, n, 0, 0))),
        scratch_shapes=[pltpu.VMEM((HGRN_HEAD, HGRN_HEAD), F32), pltpu.VMEM((tb, HGRN_HEAD), F32)],
        compiler_params=_params("parallel", "arbitrary"))(z, z, z, z, lb_logits, gnorm)


def _hgrn_bwd(z, dy, states, lb_logits, gnorm):
    s = z.shape[0]
    e = z.shape[1] // N_SPLITS
    nh = e // HGRN_HEAD
    tb = _tile(s, HGRN_BLOCK)
    nc = tb // HGRN_CHUNK
    nb = s // tb
    C = HGRN_CHUNK
    H = HGRN_HEAD

    def body(q_ref, f_ref, i_ref, g_ref, dy_ref, st_ref, lb_ref, gn_ref, dz_ref, dlb_ref, dgn_ref,
             gstate, o_scr, dq_scr, dk_scr, dv_scr, e_scr):
        first = (pl.program_id(0) == 0) & (pl.program_id(1) == 0)

        @pl.when(first)
        def _():
            dgn_ref[...] = jnp.zeros_like(dgn_ref)

        @pl.when(pl.program_id(1) == 0)
        def _():
            gstate[...] = jnp.zeros_like(gstate)
            dlb_ref[...] = jnp.zeros_like(dlb_ref)

        qp = q_ref[...]
        lb, sq, q, sg, f, k = _hgrn_gates(qp, f_ref[...], lb_ref[...])
        lower, upper = _chunk_masks()
        b = _tri_dot(lower, jnp.log(f))
        b3 = b.reshape(nc, C, H)
        bc = b3[:, C - 1:C, :]
        eb = jnp.exp(b)
        enb = jnp.exp(-b)
        eend = jnp.exp(bc - b3).reshape(tb, H)
        qt = (q * eb).astype(BF16)
        kt = (k * enb).astype(BF16)
        ke = (k * eend).astype(BF16)
        v = i_ref[...].astype(BF16)
        tri = lax.broadcasted_iota(jnp.int32, (C, C), 1) <= lax.broadcasted_iota(jnp.int32, (C, C), 0)
        sls = [slice(c * C, (c + 1) * C) for c in range(nc)]
        a = [jnp.where(tri, _dot_nt(qt[sl], kt[sl]), 0.0).astype(BF16) for sl in sls]
        for c, sl in enumerate(sls):
            o_scr[sl, :] = _dot(a[c], v[sl]) + _dot_nt(qt[sl], st_ref[c].astype(BF16))
        o = o_scr[...]
        rms = lax.rsqrt(jnp.mean(o * o, axis=-1, keepdims=True) + NORM_EPS)
        on = o * rms
        gn = gn_ref[...]
        gp = g_ref[...]
        sgg = _sigmoid(gp)
        dyv = dy_ref[...]
        d_on = dyv * (gp * sgg)
        dz_ref[3] = (dyv * on * gn * _dsilu(gp, sgg)).astype(BF16)
        dgn_ref[...] += jnp.sum(d_on * on, axis=0, keepdims=True)
        u = d_on * gn
        do = (rms * (u - on * jnp.mean(u * on, axis=-1, keepdims=True))).astype(BF16)
        gup = [_dot_tn(do[sl], qt[sl]) for sl in sls]
        da = [jnp.where(tri, _dot_nt(do[sl], v[sl]), 0.0).astype(BF16) for sl in sls]
        gt = gstate[...]
        gts = [None] * nc
        for c in reversed(range(nc)):
            gts[c] = gt
            gt = gt * jnp.exp(bc[c]) + gup[c]
        gstate[...] = gt
        for c, sl in enumerate(sls):
            stp = st_ref[c]
            gtb = gts[c].astype(BF16)
            dqt = _dot(da[c], kt[sl]) + _dot(do[sl], stp.astype(BF16))
            dkt = _dot_tn(da[c], qt[sl])
            dks = _dot(v[sl], gtb) * eend[sl]
            dv_scr[sl, :] = _dot_tn(a[c], do[sl]) + _dot_nt(ke[sl], gtb)
            dq_scr[sl, :] = dqt * eb[sl]
            dk_scr[sl, :] = dkt * enb[sl] + dks
            ech = (jnp.sum(k[sl] * dks, axis=0, keepdims=True)
                   + jnp.sum(gts[c] * jnp.exp(bc[c]) * stp, axis=0, keepdims=True))
            e_scr[sl, :] = jnp.broadcast_to(ech, (C, H))
        dq = dq_scr[...]
        dk = dk_scr[...]
        dlf = _tri_dot(upper, q * dq - k * dk) + e_scr[...]
        dft = dlf / f - dk
        dz_ref[0] = (dq * _dsilu(qp, sq)).astype(BF16)
        dz_ref[1] = (dft * (1.0 - lb) * sg * (1.0 - sg)).astype(BF16)
        dz_ref[2] = dv_scr[...].astype(BF16)
        dlb_ref[...] += jnp.sum(dft * (1.0 - sg), axis=0, keepdims=True)

    col = lambda kk: (lambda h, n: (nb - 1 - n, kk * nh + h))
    return pl.pallas_call(
        body, name="hgrn_bwd", grid=(nh, nb),
        out_shape=(jax.ShapeDtypeStruct((4, s, e), BF16), jax.ShapeDtypeStruct((1, e), F32),
                   jax.ShapeDtypeStruct((1, H), F32)),
        in_specs=[pl.BlockSpec((tb, H), col(0)), pl.BlockSpec((tb, H), col(1)),
                  pl.BlockSpec((tb, H), col(2)), pl.BlockSpec((tb, H), col(3)),
                  pl.BlockSpec((tb, H), lambda h, n: (nb - 1 - n, h)),
                  pl.BlockSpec((None, nc, H, H), lambda h, n: (h, nb - 1 - n, 0, 0)),
                  pl.BlockSpec((2, H), lambda h, n: (0, h)), pl.BlockSpec((1, H), lambda h, n: (0, 0))],
        out_specs=(pl.BlockSpec((4, tb, H), lambda h, n: (0, nb - 1 - n, h)),
                   pl.BlockSpec((1, H), lambda h, n: (0, h)), pl.BlockSpec((1, H), lambda h, n: (0, 0))),
        scratch_shapes=[pltpu.VMEM((H, H), F32)] + [pltpu.VMEM((tb, H), F32)] * 5,
        compiler_params=_params("arbitrary", "arbitrary"))(z, z, z, z, dy, states, lb_logits, gnorm)


ATTN_T = 16 * ATTN_BAND
SCALE = ATTN_HEAD ** -0.5
TILE_UNROLL = 2


def _slope(hh, nheads):
    head = (2 * pl.program_id(0) + hh + 1).astype(F32)
    return jnp.exp(jnp.full((1, 1), -8.0 / nheads * math.log(2.0), F32) * head)


def _fill_bias(bias, nheads, delta, edge_ok):
    band = (delta >= 0) & (delta <= ATTN_BAND)
    dist = delta.astype(F32)
    for pi, dil in enumerate(DILATIONS):
        for hh in range(2):
            full = jnp.where(band, -(_slope(hh, nheads) * float(dil)) * dist, NEG_BIG)
            bias[(pi * 2 + hh) * 2] = full
            bias[(pi * 2 + hh) * 2 + 1] = jnp.where(edge_ok, full, NEG_BIG)


def _rows(start, size, stride):
    if stride == 1:
        return pl.ds(pl.multiple_of(start, ATTN_BAND), size)
    return pl.ds(start, size, stride=stride)


def _head_lanes(rows, hh):
    return (lax.broadcasted_iota(jnp.int32, (rows, LANES), 1) // ATTN_HEAD) == hh


def _attn_fwd(z):
    s = z.shape[0]
    e = z.shape[1] // N_SPLITS
    npair = e // LANES
    T = ATTN_T
    assert s % T == 0
    nsb = s // T
    W = ATTN_BAND
    nt = T // W
    HD = ATTN_HEAD
    chunk = 256

    def body(q_ref, kp_ref, kc_ref, vp_ref, vc_ref, g_ref, o_ref, l_ref, y_ref, qa, kbuf, va, bias, accs, ms, msw, lsw):
        sb = pl.program_id(1)
        for hh in range(2):
            def stage(i, carry):
                rows = pl.ds(pl.multiple_of(i * chunk, chunk), chunk)
                mine = _head_lanes(chunk, hh)
                qa[hh, rows, :] = jnp.where(mine, q_ref[rows, :] * SCALE, 0.0)
                va[hh, rows, :] = jnp.where(mine, vp_ref[rows, :], 1.0)
                va[hh, pl.ds(pl.multiple_of(T + i * chunk, chunk), chunk), :] = jnp.where(mine, vc_ref[rows, :], 1.0)
                return carry

            lax.fori_loop(0, T // chunk, stage, 0)
        kbuf[0:T, :] = kp_ref[...]
        kbuf[T:, :] = kc_ref[...]
        qi = lax.broadcasted_iota(jnp.int32, (W, 2 * W), 0)
        kj = lax.broadcasted_iota(jnp.int32, (W, 2 * W), 1)
        _fill_bias(bias, 2 * npair, W + qi - kj, kj >= W)

        def tile(tau, carry):
            first = _head_lanes(W, 0)
            rows, scores = [], []
            for pi, dil in enumerate(DILATIONS):
                r = tau % dil
                ub = tau // dil
                qrows = _rows(r + dil * W * ub, W, dil)
                krows = _rows(T + dil * W * (ub - 1) + r, 2 * W, dil)
                var = jnp.where((sb == 0) & (ub == 0), 1, 0)
                kt = kbuf[krows, :].astype(BF16)
                rows.append((qrows, krows))
                scores.append([_dot_nt(qa[hh, qrows, :].astype(BF16), kt) + bias[(pi * 2 + hh) * 2 + var]
                               for hh in range(2)])
            maxes = [[jnp.max(sc, axis=-1, keepdims=True) for sc in pair] for pair in scores]
            probs = [[jnp.exp(sc - m).astype(BF16) for sc, m in zip(ps, pm)] for ps, pm in zip(scores, maxes)]
            for pi, (qrows, krows) in enumerate(rows):
                outs = [_dot(probs[pi][hh], va[hh, krows, :].astype(BF16)) for hh in range(2)]
                accs[pi, qrows, :] = jnp.where(first, outs[0], outs[1])
                lsw[pi, qrows, :] = jnp.where(first, outs[1], outs[0])
                ms[pi, qrows, :] = jnp.where(first, maxes[pi][0], maxes[pi][1])
                msw[pi, qrows, :] = jnp.where(first, maxes[pi][1], maxes[pi][0])
            return carry

        lax.fori_loop(0, nt, tile, 0, unroll=TILE_UNROLL)

        def merge(i, carry):
            rows = pl.ds(pl.multiple_of(i * chunk, chunk), chunk)
            m1, m2, m3 = ms[0, rows, :], ms[1, rows, :], ms[2, rows, :]
            mx = jnp.maximum(jnp.maximum(m1, m2), m3)
            s1, s2, s3 = msw[0, rows, :], msw[1, rows, :], msw[2, rows, :]
            sx = jnp.maximum(jnp.maximum(s1, s2), s3)
            den_sw = (jnp.exp(s1 - sx) * lsw[0, rows, :] + jnp.exp(s2 - sx) * lsw[1, rows, :]
                      + jnp.exp(s3 - sx) * lsw[2, rows, :])
            den = pltpu.roll(den_sw, ATTN_HEAD, 1)
            o = (jnp.exp(m1 - mx) * accs[0, rows, :] + jnp.exp(m2 - mx) * accs[1, rows, :]
                 + jnp.exp(m3 - mx) * accs[2, rows, :]) / den
            o_ref[rows, :] = o
            l_ref[rows, :] = mx + jnp.log(den)
            gp = g_ref[rows, :]
            y_ref[rows, :] = (o * (gp * _sigmoid(gp))).astype(BF16)
            return carry

        lax.fori_loop(0, T // chunk, merge, 0)

    cur = lambda split: (lambda hp, sb: (sb, split * npair + hp))
    prev = lambda split: (lambda hp, sb: (jnp.maximum(sb - 1, 0), split * npair + hp))
    blk = lambda index: pl.BlockSpec((T, LANES), index)
    out = blk(lambda hp, sb: (sb, hp))
    buf = lambda rows: pltpu.VMEM((rows, LANES), F32)
    return pl.pallas_call(
        body, name="attn_fwd", grid=(npair, nsb),
        out_shape=(jax.ShapeDtypeStruct((s, e), F32), jax.ShapeDtypeStruct((s, e), F32), jax.ShapeDtypeStruct((s, e), BF16)),
        in_specs=[blk(cur(4)), blk(prev(5)), blk(cur(5)), blk(prev(6)), blk(cur(6)), blk(cur(7))],
        out_specs=(out, out, out),
        scratch_shapes=[pltpu.VMEM((2, T, LANES), F32), buf(2 * T), pltpu.VMEM((2, 2 * T, LANES), F32),
                        pltpu.VMEM((12, W, 2 * W), F32)] + [pltpu.VMEM((3, T, LANES), F32)] * 4,
        compiler_params=_params("parallel", "arbitrary"))(z, z, z, z, z, z)


def _outproj_loss(x, y_h, y_a, w_out_full, final_gain, target):
    s, d = x.shape
    e = y_h.shape[1]
    tm = _tile(s, 256)

    def body(x_ref, yh_ref, ya_ref, w_ref, g_ref, t_ref, dx_ref, dxb_ref, dy_ref, loss_ref, dg_ref):
        @pl.when(pl.program_id(0) == 0)
        def _():
            loss_ref[...] = jnp.zeros_like(loss_ref)
            dg_ref[...] = jnp.zeros_like(dg_ref)

        w = w_ref[...]
        x2 = x_ref[...] + _dot(yh_ref[...], w[0:e]) + _dot(ya_ref[...], w[e:2 * e])
        r = lax.rsqrt(jnp.mean(x2 * x2, axis=-1, keepdims=True) + NORM_EPS)
        xn = x2 * r
        g = g_ref[...]
        err = xn * g - t_ref[...]
        loss_ref[...] += jnp.sum(err * err, axis=0, keepdims=True) * (0.5 / d)
        dyo = err * (1.0 / d)
        dg_ref[...] += jnp.sum(dyo * xn, axis=0, keepdims=True)
        u = dyo * g
        dx2 = r * (u - xn * jnp.mean(u * xn, axis=-1, keepdims=True))
        dx_ref[...] = dx2
        dxb = dx2.astype(BF16)
        dxb_ref[...] = dxb
        dy_ref[...] = _dot_nt(dxb, w)

    row = pl.BlockSpec((tm, d), lambda i: (i, 0))
    half = pl.BlockSpec((tm, e), lambda i: (i, 0))
    vec = pl.BlockSpec((1, d), lambda i: (0, 0))
    return pl.pallas_call(
        body, name="outproj_loss", grid=(s // tm,),
        out_shape=(jax.ShapeDtypeStruct((s, d), F32), jax.ShapeDtypeStruct((s, d), BF16),
                   jax.ShapeDtypeStruct((s, 2 * e), F32), jax.ShapeDtypeStruct((1, d), F32),
                   jax.ShapeDtypeStruct((1, d), F32)),
        in_specs=[row, half, half, pl.BlockSpec((2 * e, d), lambda i: (0, 0)), vec, row],
        out_specs=(row, row, pl.BlockSpec((tm, 2 * e), lambda i: (i, 0)), vec, vec),
        compiler_params=_params("arbitrary"))(x, y_h, y_a, w_out_full, final_gain, target)


def _attn_bwd(z, dy, o, lse):
    s, e = o.shape
    npair = e // LANES
    T = ATTN_T
    assert s % T == 0
    nsb = s // T
    W = ATTN_BAND
    nt = T // W
    HD = ATTN_HEAD
    chunk = 256

    def body(k_ref, v_ref, qc_ref, qn_ref, dyc_ref, dyn_ref, gc_ref, gn_ref, oc_ref, on_ref, lc_ref, ln_ref,
             dz_ref, qa, doa, ka, va, dqacc, dkacc, dvacc, bias):
        sb = pl.program_id(1)
        def stage_queries(half, q_r, dy_r, g_r, o_r, l_r):
            def stage(i, carry):
                rows = pl.ds(pl.multiple_of(i * chunk, chunk), chunk)
                dst = pl.ds(pl.multiple_of(half * T + i * chunk, chunk), chunk)
                lane = lax.broadcasted_iota(jnp.int32, (chunk, LANES), 1)
                gp = g_r[rows, :]
                dov = dy_r[rows, :] * (gp * _sigmoid(gp))
                prod = dov * o_r[rows, :]
                qv = q_r[rows, :] * SCALE
                lv = l_r[rows, :]
                for hh in range(2):
                    mine = _head_lanes(chunk, hh)
                    spare = (1 - hh) * HD
                    lse_parts = _split3(lv[:, hh * HD:hh * HD + 1])
                    dl_parts = _split3(jnp.sum(prod[:, hh * HD:(hh + 1) * HD], axis=-1, keepdims=True))
                    qh = jnp.where(mine, qv, 0.0)
                    dh = jnp.where(mine, dov, 0.0)
                    for j in range(3):
                        qh = jnp.where(lane == spare + j, lse_parts[j], qh)
                        dh = jnp.where(lane == spare + j, dl_parts[j], dh)
                    qa[hh, dst, :] = qh
                    doa[hh, dst, :] = dh
                return carry

            lax.fori_loop(0, T // chunk, stage, 0)

        @pl.when(sb == 0)
        def _():
            stage_queries(0, qc_ref, dyc_ref, gc_ref, oc_ref, lc_ref)

        stage_queries(1, qn_ref, dyn_ref, gn_ref, on_ref, ln_ref)

        def stage_keys(i, carry):
            rows = pl.ds(pl.multiple_of(i * chunk, chunk), chunk)
            lane = lax.broadcasted_iota(jnp.int32, (chunk, LANES), 1)
            for hh in range(2):
                spare = (1 - hh) * HD
                minus = (lane >= spare) & (lane < spare + 3)
                ka[hh, rows, :] = jnp.where(minus, -1.0, k_ref[rows, :])
                va[hh, rows, :] = jnp.where(minus, -1.0, v_ref[rows, :])
            gp = gc_ref[rows, :]
            dz_ref[3, rows, :] = (dyc_ref[rows, :] * oc_ref[rows, :] * _dsilu(gp, _sigmoid(gp))).astype(BF16)
            return carry

        lax.fori_loop(0, T // chunk, stage_keys, 0)

        @pl.when(sb == 0)
        def _():
            dqacc[0:T, :] = jnp.zeros((T, LANES), F32)

        dqacc[T:, :] = jnp.zeros((T, LANES), F32)
        dkacc[...] = jnp.zeros_like(dkacc)
        dvacc[...] = jnp.zeros_like(dvacc)
        qi = lax.broadcasted_iota(jnp.int32, (2 * W, W), 0)
        kj = lax.broadcasted_iota(jnp.int32, (2 * W, W), 1)
        _fill_bias(bias, 2 * npair, qi - kj, qi < W)

        def tile(tau, carry):
            rows, ops, sc, dpd = [], [], [], []
            for pi, dil in enumerate(DILATIONS):
                r = tau % dil
                ub = tau // dil
                start = r + dil * W * ub
                krows = _rows(start, W, dil)
                qrows = _rows(start, 2 * W, dil)
                var = jnp.where((sb == nsb - 1) & (ub == nt // dil - 1), 1, 0)
                rows.append((krows, qrows))
                for hh in range(2):
                    kt = ka[hh, krows, :].astype(BF16)
                    vt = va[hh, krows, :].astype(BF16)
                    qt = qa[hh, qrows, :].astype(BF16)
                    dt = doa[hh, qrows, :].astype(BF16)
                    ops.append((kt, qt, dt))
                    sc.append(_dot_nt(qt, kt) + bias[(pi * 2 + hh) * 2 + var])
                    dpd.append(_dot_nt(dt, vt))
            ps = [jnp.exp(s) for s in sc]
            dss = [(p * d).astype(BF16) for p, d in zip(ps, dpd)]
            pbs = [p.astype(BF16) for p in ps]
            dvs = [_dot_tn(pb, dt) for pb, (kt, qt, dt) in zip(pbs, ops)]
            dks = [_dot_tn(ds, qt) for ds, (kt, qt, dt) in zip(dss, ops)]
            dqs = [_dot(ds, kt) for ds, (kt, qt, dt) in zip(dss, ops)]
            for pi, (krows, qrows) in enumerate(rows):
                dkacc[krows, :] += jnp.where(_head_lanes(W, 0), dks[2 * pi], dks[2 * pi + 1])
                dvacc[krows, :] += jnp.where(_head_lanes(W, 0), dvs[2 * pi], dvs[2 * pi + 1])
                dqacc[qrows, :] += jnp.where(_head_lanes(2 * W, 0), dqs[2 * pi], dqs[2 * pi + 1]) * SCALE
            return carry

        lax.fori_loop(0, nt, tile, 0, unroll=TILE_UNROLL)

        def flush(i, carry):
            rows = pl.ds(pl.multiple_of(i * chunk, chunk), chunk)
            nxt = pl.ds(pl.multiple_of(T + i * chunk, chunk), chunk)
            dz_ref[0, rows, :] = dqacc[rows, :].astype(BF16)
            dz_ref[1, rows, :] = dkacc[rows, :].astype(BF16)
            dz_ref[2, rows, :] = dvacc[rows, :].astype(BF16)
            dqacc[rows, :] = dqacc[nxt, :]
            for hh in range(2):
                qa[hh, rows, :] = qa[hh, nxt, :]
                doa[hh, rows, :] = doa[hh, nxt, :]
            return carry

        lax.fori_loop(0, T // chunk, flush, 0)

    zc = lambda split: (lambda hp, sb: (sb, split * npair + hp))
    zn = lambda split: (lambda hp, sb: (jnp.minimum(sb + 1, nsb - 1), split * npair + hp))
    ec = lambda off: (lambda hp, sb: (sb, off + hp))
    en = lambda off: (lambda hp, sb: (jnp.minimum(sb + 1, nsb - 1), off + hp))
    blk = lambda index: pl.BlockSpec((T, LANES), index)
    buf = lambda rows: pltpu.VMEM((rows, LANES), F32)
    return pl.pallas_call(
        body, name="attn_bwd", grid=(npair, nsb), out_shape=jax.ShapeDtypeStruct((4, s, e), BF16),
        in_specs=[blk(zc(5)), blk(zc(6)), blk(zc(4)), blk(zn(4)), blk(ec(npair)), blk(en(npair)),
                  blk(zc(7)), blk(zn(7)), blk(ec(0)), blk(en(0)), blk(ec(0)), blk(en(0))],
        out_specs=pl.BlockSpec((4, T, LANES), lambda hp, sb: (0, sb, hp)),
        scratch_shapes=[pltpu.VMEM((2, 2 * T, LANES), F32), pltpu.VMEM((2, 2 * T, LANES), F32),
                        pltpu.VMEM((2, T, LANES), F32), pltpu.VMEM((2, T, LANES), F32),
                        buf(2 * T), buf(T), buf(T), pltpu.VMEM((12, 2 * W, W), F32)],
        compiler_params=_params("parallel", "arbitrary"))(z, z, z, z, dy, dy, z, z, o, o, lse, lse)


def _dz_specs(tm, e, axis):
    def mk(lo, hi):
        def index(i, k):
            row, grp = (i, k) if axis == 1 else (k, i)
            return (jnp.clip(grp - lo, 0, hi - lo - 1), row, 0)
        return pl.BlockSpec((None, tm, e), index)
    return [mk(0, 4), mk(4, 8)]


def _dz_pick(grp, dzh_ref, dza_ref, fn):
    @pl.when(grp < 4)
    def _():
        fn(dzh_ref[...])

    @pl.when(grp >= 4)
    def _():
        fn(dza_ref[...])


def _dh_dx(dzh, dza, w_full, x, gain, dx2):
    s, d = x.shape
    e = dzh.shape[2]
    tm = _tile(s, 512)

    def body(dzh_ref, dza_ref, w_ref, x_ref, g_ref, dx2_ref, gx_ref, dg_ref, acc):
        i, k = pl.program_id(0), pl.program_id(1)

        @pl.when((i == 0) & (k == 0))
        def _():
            dg_ref[...] = jnp.zeros_like(dg_ref)

        @pl.when(k == 0)
        def _():
            acc[...] = jnp.zeros_like(acc)

        def add(dz):
            acc[...] += _dot_nt(dz, w_ref[...])

        _dz_pick(k, dzh_ref, dza_ref, add)

        @pl.when(k == N_SPLITS - 1)
        def _():
            dh = acc[...]
            xv = x_ref[...]
            r = lax.rsqrt(jnp.mean(xv * xv, axis=-1, keepdims=True) + NORM_EPS)
            xn = xv * r
            dg_ref[...] += jnp.sum(dh * xn, axis=0, keepdims=True)
            u = dh * g_ref[...]
            gx_ref[...] = dx2_ref[...] + r * (u - xn * jnp.mean(u * xn, axis=-1, keepdims=True))

    row = pl.BlockSpec((tm, d), lambda i, k: (i, 0))
    vec = pl.BlockSpec((1, d), lambda i, k: (0, 0))
    return pl.pallas_call(
        body, name="dh_dx", grid=(s // tm, N_SPLITS),
        out_shape=(jax.ShapeDtypeStruct((s, d), F32), jax.ShapeDtypeStruct((1, d), F32)),
        in_specs=_dz_specs(tm, e, 1) + [pl.BlockSpec((None, d, e), lambda i, k: (k, 0, 0)), row, vec, row],
        out_specs=(row, vec), scratch_shapes=[pltpu.VMEM((tm, d), F32)],
        compiler_params=_params("arbitrary", "arbitrary"))(dzh, dza, w_full, x, gain, dx2)


def _position():
    x, y, c = lax.axis_index("x"), lax.axis_index("y"), lax.axis_index("c")
    return x, y, c


def _xor_peer(x, y, c, mask):
    return (x ^ ((mask >> 2) & 1), y ^ ((mask >> 1) & 1), c ^ (mask & 1))


def _block_order(masks):
    me = 4 * lax.axis_index("x") + 2 * lax.axis_index("y") + lax.axis_index("c")
    return jnp.stack([me ^ m for m in masks]).astype(jnp.int32)


GATHER_MASKS = (0, 1, 4, 2, 6, 5, 3, 7)


def _inproj_gather(h, w_loc, wo_loc):
    s, d = h.shape
    e = w_loc.shape[1]
    tm = _tile(s, 512)
    ni = s // tm
    pre = max(ni - 3, 0)

    def body(order_ref, h_ref, w_ref, wo_ref, z_ref, wf_ref, wof_ref, wbuf, send_sems, recv_sems, osend, orecv,
             local_sems, wsems):
        j, i = pl.program_id(0), pl.program_id(1)
        x, y, c = _position()
        me, sibling = (x, y, c), (x, y, 1 - c)
        chips = [(1 - x, y), (x, 1 - y), (1 - x, 1 - y)]
        blk = lambda p: 4 * p[0] + 2 * p[1] + p[2]

        def copy(k, block, to, src=None):
            dst = wf_ref.at[blk(block)]
            return pltpu.make_async_remote_copy(
                src_ref=dst if src is None else src, dst_ref=dst, send_sem=send_sems.at[k], recv_sem=recv_sems.at[k],
                device_id=to, device_id_type=MESH)

        first = [copy(0, me, sibling, src=w_ref)] + [copy(1 + q, me, (*chip, c), src=w_ref) for q, chip in enumerate(chips)]
        passed = [copy(4 + q, (*chip, c), sibling) for q, chip in enumerate(chips)]
        mine = pltpu.make_async_copy(w_ref, wf_ref.at[blk(me)], local_sems.at[0])
        ocopies = [pltpu.make_async_remote_copy(
            src_ref=wo_ref, dst_ref=wof_ref.at[blk(me)], send_sem=osend.at[k], recv_sem=orecv.at[k],
            device_id=_xor_peer(x, y, c, k + 1), device_id_type=MESH) for k in range(N_DEV - 1)]
        omine = pltpu.make_async_copy(wo_ref, wof_ref.at[blk(me)], local_sems.at[1])
        blocks = [me, sibling] + [(*chip, c) for chip in chips] + [(*chip, 1 - c) for chip in chips]
        arrive = [None, copy(0, sibling, me)] + [copy(1 + q, (*chip, c), me) for q, chip in enumerate(chips)] \
            + [copy(4 + q, (*chip, 1 - c), me) for q, chip in enumerate(chips)]
        forward = [None, None] + passed + [None, None, None]

        def load(slot, src):
            return pltpu.make_async_copy(src, wbuf.at[slot], wsems.at[slot])

        @pl.when((j == 0) & (i == 0))
        def _():
            for cp in [mine, omine] + first + ocopies:
                cp.start()
            load(0, w_ref).start()

        for jj in range(N_DEV):
            @pl.when((j == jj) & (i == 0))
            def _():
                load(jj % 2, w_ref).wait()

            if jj + 1 < N_DEV:
                @pl.when((j == jj) & (i == pre))
                def _():
                    arrive[jj + 1].wait_recv()
                    if forward[jj + 1] is not None:
                        forward[jj + 1].start()
                    load((jj + 1) % 2, wf_ref.at[blk(blocks[jj + 1])]).start()

        z_ref[...] = _dot(h_ref[...], wbuf[j % 2])

        @pl.when((j == N_DEV - 1) & (i == ni - 1))
        def _():
            for cp in first + passed:
                cp.wait_send()
            for cp in ocopies:
                cp.wait_send()
                cp.wait_recv()
            mine.wait()
            omine.wait()

    grid_spec = pltpu.PrefetchScalarGridSpec(
        num_scalar_prefetch=1, grid=(N_DEV, ni),
        in_specs=[pl.BlockSpec((tm, d), lambda j, i, o: (i, 0)), ANY, ANY],
        out_specs=(pl.BlockSpec((tm, e), lambda j, i, o: (i, o[j])), ANY, ANY),
        scratch_shapes=[pltpu.VMEM((2, d, e), BF16), pltpu.SemaphoreType.DMA((7,)), pltpu.SemaphoreType.DMA((7,)),
                        pltpu.SemaphoreType.DMA((7,)), pltpu.SemaphoreType.DMA((7,)), pltpu.SemaphoreType.DMA((2,)),
                        pltpu.SemaphoreType.DMA((2,))])
    return pl.pallas_call(
        body, name="inproj_gather", grid_spec=grid_spec,
        out_shape=(jax.ShapeDtypeStruct((s, N_SPLITS * e), F32), jax.ShapeDtypeStruct((N_DEV, d, e), BF16),
                   jax.ShapeDtypeStruct((N_DEV,) + wo_loc.shape, BF16)),
        compiler_params=_params("arbitrary", "arbitrary"))(_block_order(GATHER_MASKS), h, w_loc, wo_loc)


SCATTER_MASKS = (7, 6, 5, 4, 3, 2, 1, 0)
N_CHIPS = 4


def _scatter_block(k, acc, stage, tmp, own_ref, ra_ref, rb_ref, sa_send, sa_recv, sb_send, sb_recv, loc_sem, last):
    x, y, c = _position()
    chip_of = lambda t: _xor_peer(x, y, c, SCATTER_MASKS[2 * t + 1])

    def ship(t):
        return pltpu.make_async_remote_copy(
            src_ref=stage.at[0], dst_ref=ra_ref.at[t], send_sem=sa_send.at[t], recv_sem=sa_recv.at[t],
            device_id=(x, y, 1 - c), device_id_type=MESH)

    def send(t):
        return pltpu.make_async_remote_copy(
            src_ref=stage.at[1], dst_ref=rb_ref.at[t], send_sem=sb_send.at[t], recv_sem=sb_recv.at[t],
            device_id=chip_of(t), device_id_type=MESH)

    for kk in range(N_DEV):
        t = kk // 2

        @pl.when(last & (k == kk))
        def _():
            if kk % 2 == 0:
                if t >= 1:
                    ship(t - 1).wait_send()
                stage[0] = acc[...].astype(BF16)
                ship(t).start()
            else:
                ship(t).wait_recv()
                fetch = pltpu.make_async_copy(ra_ref.at[t], tmp, loc_sem)
                fetch.start()
                fetch.wait()
                acc[...] += tmp[...].astype(F32)
                if t < N_CHIPS - 1:
                    if t >= 1:
                        send(t - 1).wait_send()
                    stage[1] = acc[...].astype(BF16)
                    send(t).start()
                else:
                    keep = pltpu.make_async_copy(acc, own_ref, loc_sem)
                    keep.start()
                    keep.wait()
                    ship(t).wait_send()
                    send(t - 1).wait_send()
                    for q in range(N_CHIPS - 1):
                        send(q).wait_recv()


def _scatter_scratch(rows, cols):
    return [pltpu.VMEM((rows, cols), F32), pltpu.VMEM((2, rows, cols), BF16), pltpu.VMEM((rows, cols), BF16),
            pltpu.SemaphoreType.DMA((N_CHIPS,)), pltpu.SemaphoreType.DMA((N_CHIPS,)),
            pltpu.SemaphoreType.DMA((N_CHIPS - 1,)), pltpu.SemaphoreType.DMA((N_CHIPS - 1,)), pltpu.SemaphoreType.DMA(())]


def _scatter_out(rows, cols):
    return (jax.ShapeDtypeStruct((rows, cols), F32), jax.ShapeDtypeStruct((N_CHIPS, rows, cols), BF16),
            jax.ShapeDtypeStruct((N_CHIPS - 1, rows, cols), BF16))


def _dwin_scatter(h, dzh, dza):
    s, d = h.shape
    e = dzh.shape[2]
    ts = _tile(s, 512)
    ns = s // ts

    def body(order_ref, dzh_ref, dza_ref, h_ref, own_ref, ra_ref, rb_ref, acc, stage, tmp, *sems):
        k, step = pl.program_id(0), pl.program_id(1)

        @pl.when(step == 0)
        def _():
            acc[...] = jnp.zeros_like(acc)

        def add(dz):
            acc[...] += _dot_tn(h_ref[...], dz)

        _dz_pick(order_ref[k], dzh_ref, dza_ref, add)
        _scatter_block(k, acc, stage, tmp, own_ref, ra_ref, rb_ref, *sems, step == ns - 1)

    def dz_spec(lo):
        return pl.BlockSpec((None, ts, e), lambda k, st, o: (jnp.clip(o[k] - lo, 0, 3), st, 0))

    grid_spec = pltpu.PrefetchScalarGridSpec(
        num_scalar_prefetch=1, grid=(N_DEV, ns),
        in_specs=[dz_spec(0), dz_spec(4), pl.BlockSpec((ts, d), lambda k, st, o: (st, 0))],
        out_specs=(ANY, ANY, ANY), scratch_shapes=_scatter_scratch(d, e))
    own, _, rb = pl.pallas_call(
        body, name="dwin_scatter", grid_spec=grid_spec, out_shape=_scatter_out(d, e),
        compiler_params=_params("arbitrary", "arbitrary"))(_block_order(SCATTER_MASKS), dzh, dza, h)
    return own, rb


def _dwout_scatter(y_h, y_a, dxb):
    s, e = y_h.shape
    d = dxb.shape[1]
    r = 2 * e // N_DEV
    pairs = e // (2 * r)
    ts = _tile(s, 512)
    ns = s // ts
    chip_masks = SCATTER_MASKS[1::2]

    def body(pair_ref, yh_ref, ya_ref, dx_ref, own_ref, ra_ref, rb_ref, acc, pend, stage, tmp,
             sa_send, sa_recv, sb_send, sb_recv, loc_sem):
        t, step = pl.program_id(0), pl.program_id(1)
        x, y, c = _position()

        @pl.when(step == 0)
        def _():
            acc[...] = jnp.zeros_like(acc)

        @pl.when(pair_ref[t] < pairs)
        def _():
            acc[...] += _dot_tn(yh_ref[...], dx_ref[...])

        @pl.when(pair_ref[t] >= pairs)
        def _():
            acc[...] += _dot_tn(ya_ref[...], dx_ref[...])

        mine = pl.ds(pl.multiple_of(c * r, r), r)
        other = pl.ds(pl.multiple_of((1 - c) * r, r), r)

        def ship(q):
            return pltpu.make_async_remote_copy(
                src_ref=stage.at[0], dst_ref=ra_ref.at[q], send_sem=sa_send.at[q], recv_sem=sa_recv.at[q],
                device_id=(x, y, 1 - c), device_id_type=MESH)

        def send(q):
            return pltpu.make_async_remote_copy(
                src_ref=stage.at[1], dst_ref=rb_ref.at[q], send_sem=sb_send.at[q], recv_sem=sb_recv.at[q],
                device_id=_xor_peer(x, y, c, chip_masks[q]), device_id_type=MESH)

        def sibling_share(q):
            ship(q).wait_recv()
            fetch = pltpu.make_async_copy(ra_ref.at[q], tmp, loc_sem)
            fetch.start()
            fetch.wait()
            return tmp[...].astype(F32)

        for tt in range(N_CHIPS):
            @pl.when((step == ns - 1) & (t == tt))
            def _():
                if tt >= 1:
                    ship(tt - 1).wait_send()
                stage[0] = acc[other, :].astype(BF16)
                ship(tt).start()
                if tt >= 1:
                    if tt >= 2:
                        send(tt - 2).wait_send()
                    stage[1] = (pend[...] + sibling_share(tt - 1)).astype(BF16)
                    send(tt - 1).start()
                if tt < N_CHIPS - 1:
                    pend[...] = acc[mine, :]
                else:
                    pend[...] = acc[mine, :] + sibling_share(tt)
                    keep = pltpu.make_async_copy(pend, own_ref, loc_sem)
                    keep.start()
                    keep.wait()
                    ship(tt).wait_send()
                    send(tt - 1).wait_send()
                    for q in range(N_CHIPS - 1):
                        send(q).wait_recv()

    def y_spec(lo):
        return pl.BlockSpec((ts, 2 * r), lambda t, st, o: (st, jnp.clip(o[t] - lo, 0, pairs - 1)))

    grid_spec = pltpu.PrefetchScalarGridSpec(
        num_scalar_prefetch=1, grid=(N_CHIPS, ns),
        in_specs=[y_spec(0), y_spec(pairs), pl.BlockSpec((ts, d), lambda t, st, o: (st, 0))],
        out_specs=(ANY, ANY, ANY),
        scratch_shapes=[pltpu.VMEM((2 * r, d), F32), pltpu.VMEM((r, d), F32)] + _scatter_scratch(r, d)[1:])
    own, _, rb = pl.pallas_call(
        body, name="dwout_scatter", grid_spec=grid_spec, out_shape=_scatter_out(r, d),
        compiler_params=_params("arbitrary", "arbitrary"))(_block_order(chip_masks) // 2, y_h, y_a, dxb)
    return own, rb


def _sum_chips_adamw(own, recv, w, m, v):
    r, c = w.shape
    tr = _tile(r, 128)

    def body(own_ref, rc_ref, w_ref, m_ref, v_ref, g_ref, d_ref, mo_ref, vo_ref):
        g = own_ref[...]
        for q in range(N_CHIPS - 1):
            g = g + rc_ref[q].astype(F32)
        g_ref[...] = g
        d_ref[...], mo_ref[...], vo_ref[...] = _adamw(w_ref[...], g, m_ref[...], v_ref[...])

    blk = pl.BlockSpec((tr, c), lambda i: (i, 0))
    shp = jax.ShapeDtypeStruct((r, c), F32)
    return pl.pallas_call(
        body, name="sum_chips_adamw", grid=(r // tr,), out_shape=(shp, shp, shp, shp),
        in_specs=[blk, pl.BlockSpec((N_CHIPS - 1, tr, c), lambda i: (0, i, 0)), blk, blk, blk],
        out_specs=(blk, blk, blk, blk), compiler_params=_params("parallel"))(own, recv, w, m, v)


SMALL_ROWS = 8
ROW_LB = 4
ROW_GN = 6
ROW_LOSS = 7


def _small_allreduce_adamw(part, w, m, v, lb_logits):
    width = part.shape[1]

    def body(p_ref, w_ref, m_ref, v_ref, lb_ref, g_ref, d_ref, mo_ref, vo_ref, buf, send_sems, recv_sems):
        x, y, c = _position()
        me = 4 * x + 2 * y + c
        buf[me] = p_ref[...]
        copies = []
        for k in range(N_DEV - 1):
            bx, by, bc = ((k + 1) >> 2) & 1, ((k + 1) >> 1) & 1, (k + 1) & 1
            peer = (x ^ bx, y ^ by, c ^ bc)
            copies.append(pltpu.make_async_remote_copy(
                src_ref=p_ref, dst_ref=buf.at[me], send_sem=send_sems.at[k], recv_sem=recv_sems.at[k],
                device_id=peer, device_id_type=MESH))
        for cp in copies:
            cp.start()
        for cp in copies:
            cp.wait_recv()
        for cp in copies:
            cp.wait_send()
        tot = buf[0]
        for dev in range(1, N_DEV):
            tot = tot + buf[dev]
        lbv = lb_ref[...]
        lb = _sigmoid(lbv[0:1] - lbv[1:2])
        glb = tot[ROW_LB:ROW_LB + 1] * lb * (1.0 - lb)
        loss = jnp.sum(tot[ROW_LOSS:ROW_LOSS + 1], axis=-1, keepdims=True)
        row = lax.broadcasted_iota(jnp.int32, (SMALL_ROWS, width), 0)
        g = jnp.where(row == ROW_LB, glb, jnp.where(row == ROW_LB + 1, -glb, tot))
        g = jnp.where(row == ROW_LOSS, loss, g)
        g_ref[...] = g
        d_ref[...], mo_ref[...], vo_ref[...] = _adamw(w_ref[...], g, m_ref[...], v_ref[...])

    vm = pl.BlockSpec(memory_space=pltpu.VMEM)
    shp = jax.ShapeDtypeStruct((SMALL_ROWS, width), F32)
    return pl.pallas_call(
        body, name="small_allreduce_adamw", out_shape=(shp, shp, shp, shp),
        in_specs=[vm] * 5, out_specs=(vm, vm, vm, vm),
        scratch_shapes=[pltpu.VMEM((N_DEV, SMALL_ROWS, width), F32), pltpu.SemaphoreType.DMA((N_DEV - 1,)),
                        pltpu.SemaphoreType.DMA((N_DEV - 1,))],
    )(part, w, m, v, lb_logits)


def _pack_small(norm_gain, final_gain, lb2, gnorm, last_row, width):
    pad = lambda a: jnp.pad(a.reshape(1, -1), ((0, 0), (0, width - a.size)))
    return jnp.concatenate([norm_gain.reshape(2, width), final_gain.reshape(2, width), lb2.reshape(2, width),
                            pad(gnorm), last_row.reshape(1, width)], axis=0)


def _unpack_small(p, d, e, hd):
    return (p[0:2].reshape(1, d), p[2:4].reshape(d), p[4:6].reshape(2, e), p[6:7, :hd].reshape(1, hd))


def kernel(x, norm_gain, w_in, lb_logits, hgrn_gnorm, w_out, final_gain, loss_target, m_norm_gain, m_w_in, m_lb_logits, m_hgrn_gnorm, m_w_out, m_final_gain, v_norm_gain, v_w_in, v_lb_logits, v_hgrn_gnorm, v_w_out, v_final_gain):
    s, d = x.shape[1], x.shape[2]
    e = w_in.shape[2]
    assert d == 2 * e and lb_logits.shape == (2, e) and w_out.shape[1] * N_DEV == 2 * e
    x2d = x.reshape(s, d)
    tgt = loss_target.reshape(s, d)

    h = _rmsnorm_fwd(x2d, norm_gain)
    z, w_in_full, w_out_full = _inproj_gather(h, _cast_bf16(w_in[0]), _cast_bf16(w_out[0]))
    w_out_full = w_out_full.reshape(2 * e, d)
    y_h, states = _hgrn_fwd(z, lb_logits, hgrn_gnorm)
    o_attn, lse, y_a = _attn_fwd(z)
    dx2, dx2b, dy, loss_vec, dfg = _outproj_loss(x2d, y_h, y_a, w_out_full, final_gain.reshape(1, d), tgt)

    own_o, recv_o = _dwout_scatter(y_h, y_a, dx2b)
    dza = _attn_bwd(z, dy, o_attn, lse)
    dzh, dlb, dgn = _hgrn_bwd(z, dy, states, lb_logits, hgrn_gnorm)
    grad_x, dng = _dh_dx(dzh, dza, w_in_full, x2d, norm_gain, dx2)
    own_i, recv_i = _dwin_scatter(h, dzh, dza)
    g_wi, d_wi, nm_wi, nv_wi = _sum_chips_adamw(own_i, recv_i, w_in[0], m_w_in[0], v_w_in[0])
    g_wo, d_wo, nm_wo, nv_wo = _sum_chips_adamw(own_o, recv_o, w_out[0], m_w_out[0], v_w_out[0])

    width = d // 2
    zero_row = jnp.zeros((1, width), F32)
    loss_row = loss_vec[:, :width] + loss_vec[:, width:]
    part = _pack_small(dng, dfg, jnp.concatenate([dlb, zero_row], axis=0), dgn, loss_row, width)
    pw = _pack_small(norm_gain, final_gain, lb_logits, hgrn_gnorm, zero_row, width)
    pm = _pack_small(m_norm_gain, m_final_gain, m_lb_logits, m_hgrn_gnorm, zero_row, width)
    pv = _pack_small(v_norm_gain, v_final_gain, v_lb_logits, v_hgrn_gnorm, zero_row, width)
    sg, sd, sm, sv = _small_allreduce_adamw(part, pw, pm, pv, lb_logits)
    hd = hgrn_gnorm.shape[1]
    g_ng, g_fg, g_lb, g_gn = _unpack_small(sg, d, e, hd)
    d_ng, d_fg, d_lb, d_gn = _unpack_small(sd, d, e, hd)
    m_ng, m_fg, m_lb, m_gn = _unpack_small(sm, d, e, hd)
    v_ng, v_fg, v_lb, v_gn = _unpack_small(sv, d, e, hd)
    loss = sg[ROW_LOSS, 0]

    one = lambda a: a[None]
    return (loss, grad_x.reshape(1, s, d), g_ng, one(g_wi), g_lb, g_gn, one(g_wo), g_fg,
            d_ng, one(d_wi), d_lb, d_gn, one(d_wo), d_fg,
            m_ng, one(nm_wi), m_lb, m_gn, one(nm_wo), m_fg,
            v_ng, one(nv_wi), v_lb, v_gn, one(nv_wo), v_fg)
```

```python
import functools
import math

import jax
import jax.numpy as jnp
from jax import lax
from jax.experimental import pallas as pl
from jax.experimental.pallas import tpu as pltpu

NORM_EPS = 1e-6
HGRN_HEAD = 128
HGRN_CHUNK = 64
ATTN_HEAD = 64
ATTN_BAND = 128
DILATIONS = (1, 4, 16)
N_SPLITS = 8
N_DEV = 8
ADAM_LR = 0.001
ADAM_B1 = 0.9
ADAM_B2 = 0.999
ADAM_EPS = 1e-08
ADAM_WD = 0.01
ADAM_STEP = 10
LANES = 128
MESH = pl.DeviceIdType.MESH
F32 = jnp.float32
BF16 = jnp.bfloat16
NEG_BIG = -1e30
VMEM_LIMIT = 56 * 1024 * 1024

ANY = pl.BlockSpec(memory_space=pl.ANY)


def _params(*sem):
    return pltpu.CompilerParams(dimension_semantics=sem, vmem_limit_bytes=VMEM_LIMIT)


def _tile(n, pref):
    t = min(n, pref)
    assert n % t == 0, (n, pref)
    return t


def _dot(a, b, precision=None):
    return jnp.dot(a, b, preferred_element_type=F32, precision=precision)


def _dot_nt(a, b):
    return lax.dot_general(a, b, (((1,), (1,)), ((), ())), preferred_element_type=F32)


def _dot_tn(a, b):
    return lax.dot_general(a, b, (((0,), (0,)), ((), ())), preferred_element_type=F32)


def _sigmoid(x):
    return 1.0 / (1.0 + jnp.exp(-x))


def _dsilu(x, s):
    return s * (1.0 + x * (1.0 - s))


def _adamw(w, g, m, v):
    m = ADAM_B1 * m + (1.0 - ADAM_B1) * g
    v = ADAM_B2 * v + (1.0 - ADAM_B2) * (g * g)
    m_hat = m / (1.0 - ADAM_B1 ** ADAM_STEP)
    v_hat = v / (1.0 - ADAM_B2 ** ADAM_STEP)
    delta = -ADAM_LR * (m_hat / (jnp.sqrt(v_hat) + ADAM_EPS) + ADAM_WD * w)
    return delta, m, v


def _cast_bf16(a):
    r, c = a.shape
    tr = _tile(r, 256)

    def body(a_ref, o_ref):
        o_ref[...] = a_ref[...].astype(BF16)

    return pl.pallas_call(
        body, name="cast_bf16", grid=(r // tr,), out_shape=jax.ShapeDtypeStruct((r, c), BF16),
        in_specs=[pl.BlockSpec((tr, c), lambda i: (i, 0))], out_specs=pl.BlockSpec((tr, c), lambda i: (i, 0)),
        compiler_params=_params("parallel"))(a)


def _rmsnorm_fwd(x, gain):
    s, d = x.shape
    tm = _tile(s, 512)

    def body(x_ref, g_ref, h_ref):
        xv = x_ref[...]
        r = lax.rsqrt(jnp.mean(xv * xv, axis=-1, keepdims=True) + NORM_EPS)
        h_ref[...] = (xv * r * g_ref[...]).astype(BF16)

    return pl.pallas_call(
        body, name="rmsnorm_fwd", grid=(s // tm,), out_shape=jax.ShapeDtypeStruct((s, d), BF16),
        in_specs=[pl.BlockSpec((tm, d), lambda i: (i, 0)), pl.BlockSpec((1, d), lambda i: (0, 0))],
        out_specs=pl.BlockSpec((tm, d), lambda i: (i, 0)), compiler_params=_params("parallel"))(x, gain)


HGRN_BLOCK = 1024
TRI_ROWS = 256


def _chunk_masks():
    tb = TRI_ROWS
    row = lax.broadcasted_iota(jnp.int32, (tb, tb), 0)
    col = lax.broadcasted_iota(jnp.int32, (tb, tb), 1)
    same = (row // HGRN_CHUNK) == (col // HGRN_CHUNK)
    lower = jnp.where(same & (col <= row), 1.0, 0.0).astype(BF16)
    upper = jnp.where(same & (col >= row), 1.0, 0.0).astype(BF16)
    return lower, upper


def _split3(a):
    hi = a.astype(BF16).astype(F32)
    mid = (a - hi).astype(BF16).astype(F32)
    lo = (a - hi - mid).astype(BF16).astype(F32)
    return hi, mid, lo


def _tri_dot(tri, x):
    hi, mid, lo = (p.astype(BF16) for p in _split3(x))
    outs = []
    for r in range(0, x.shape[0], TRI_ROWS):
        sl = slice(r, r + TRI_ROWS)
        outs.append(_dot(tri, hi[sl]) + _dot(tri, mid[sl]) + _dot(tri, lo[sl]))
    return outs[0] if len(outs) == 1 else jnp.concatenate(outs, axis=0)


def _hgrn_gates(qp, fp, lbv):
    lb = _sigmoid(lbv[0:1] - lbv[1:2])
    sq = _sigmoid(qp)
    q = qp * sq
    sg = _sigmoid(fp)
    f = lb + (1.0 - lb) * sg
    k = 1.0 - f
    return lb, sq, q, sg, f, k


def _hgrn_fwd(z, lb_logits, gnorm):
    s = z.shape[0]
    e = z.shape[1] // N_SPLITS
    nh = e // HGRN_HEAD
    tb = _tile(s, HGRN_BLOCK)
    nc = tb // HGRN_CHUNK
    nb = s // tb
    C = HGRN_CHUNK

    def body(q_ref, f_ref, i_ref, g_ref, lb_ref, gn_ref, y_ref, st_ref, state, o_scr):
        @pl.when(pl.program_id(1) == 0)
        def _():
            state[...] = jnp.zeros_like(state)

        lb, sq, q, sg, f, k = _hgrn_gates(q_ref[...], f_ref[...], lb_ref[...])
        lower, _ = _chunk_masks()
        b = _tri_dot(lower, jnp.log(f))
        b3 = b.reshape(nc, C, HGRN_HEAD)
        bc = b3[:, C - 1:C, :]
        qt = (q * jnp.exp(b)).astype(BF16)
        kt = (k * jnp.exp(-b)).astype(BF16)
        ke = (k.reshape(nc, C, HGRN_HEAD) * jnp.exp(bc - b3)).reshape(tb, HGRN_HEAD).astype(BF16)
        v = i_ref[...].astype(BF16)
        tri = lax.broadcasted_iota(jnp.int32, (C, C), 1) <= lax.broadcasted_iota(jnp.int32, (C, C), 0)
        sls = [slice(c * C, (c + 1) * C) for c in range(nc)]
        kv = [_dot_tn(v[sl], ke[sl]) for sl in sls]
        a = [jnp.where(tri, _dot_nt(qt[sl], kt[sl]), 0.0).astype(BF16) for sl in sls]
        st = state[...]
        sts = []
        for c in range(nc):
            sts.append(st)
            st_ref[c] = st
            st = st * jnp.exp(bc[c]) + kv[c]
        state[...] = st
        for c, sl in enumerate(sls):
            o_scr[sl, :] = _dot(a[c], v[sl]) + _dot_nt(qt[sl], sts[c].astype(BF16))
        o = o_scr[...]
        rms = lax.rsqrt(jnp.mean(o * o, axis=-1, keepdims=True) + NORM_EPS)
        gp = g_ref[...]
        y_ref[...] = (o * rms * gn_ref[...] * (gp * _sigmoid(gp))).astype(BF16)

    col = lambda kk: (lambda h, n: (n, kk * nh + h))
    return pl.pallas_call(
        body, name="hgrn_fwd", grid=(nh, nb),
        out_shape=(jax.ShapeDtypeStruct((s, e), BF16),
                   jax.ShapeDtypeStruct((nh, s // C, HGRN_HEAD, HGRN_HEAD), F32)),
        in_specs=[pl.BlockSpec((tb, HGRN_HEAD), col(0)), pl.BlockSpec((tb, HGRN_HEAD), col(1)),
                  pl.BlockSpec((tb, HGRN_HEAD), col(2)), pl.BlockSpec((tb, HGRN_HEAD), col(3)),
                  pl.BlockSpec((2, HGRN_HEAD), lambda h, n: (0, h)), pl.BlockSpec((1, HGRN_HEAD), lambda h, n: (0, 0))],
        out_specs=(pl.BlockSpec((tb, HGRN_HEAD), lambda h, n: (n, h)),
                   pl.BlockSpec((None, nc, HGRN_HEAD, HGRN_HEAD), lambda h, n: (h, n, 0, 0))),
        scratch_shapes=[pltpu.VMEM((HGRN_HEAD, HGRN_HEAD), F32), pltpu.VMEM((tb, HGRN_HEAD), F32)],
        compiler_params=_params("parallel", "arbitrary"))(z, z, z, z, lb_logits, gnorm)


def _hgrn_bwd(z, dy, states, lb_logits, gnorm):
    s = z.shape[0]
    e = z.shape[1] // N_SPLITS
    nh = e // HGRN_HEAD
    tb = _tile(s, HGRN_BLOCK)
    nc = tb // HGRN_CHUNK
    nb = s // tb
    C = HGRN_CHUNK
    H = HGRN_HEAD

    def body(q_ref, f_ref, i_ref, g_ref, dy_ref, st_ref, lb_ref, gn_ref, dz_ref, dlb_ref, dgn_ref,
             gstate, o_scr, dq_scr, dk_scr, dv_scr, e_scr):
        first = (pl.program_id(0) == 0) & (pl.program_id(1) == 0)

        @pl.when(first)
        def _():
            dgn_ref[...] = jnp.zeros_like(dgn_ref)

        @pl.when(pl.program_id(1) == 0)
        def _():
            gstate[...] = jnp.zeros_like(gstate)
            dlb_ref[...] = jnp.zeros_like(dlb_ref)

        qp = q_ref[...]
        lb, sq, q, sg, f, k = _hgrn_gates(qp, f_ref[...], lb_ref[...])
        lower, upper = _chunk_masks()
        b = _tri_dot(lower, jnp.log(f))
        b3 = b.reshape(nc, C, H)
        bc = b3[:, C - 1:C, :]
        eb = jnp.exp(b)
        enb = jnp.exp(-b)
        eend = jnp.exp(bc - b3).reshape(tb, H)
        qt = (q * eb).astype(BF16)
        kt = (k * enb).astype(BF16)
        ke = (k * eend).astype(BF16)
        v = i_ref[...].astype(BF16)
        tri = lax.broadcasted_iota(jnp.int32, (C, C), 1) <= lax.broadcasted_iota(jnp.int32, (C, C), 0)
        sls = [slice(c * C, (c + 1) * C) for c in range(nc)]
        a = [jnp.where(tri, _dot_nt(qt[sl], kt[sl]), 0.0).astype(BF16) for sl in sls]
        for c, sl in enumerate(sls):
            o_scr[sl, :] = _dot(a[c], v[sl]) + _dot_nt(qt[sl], st_ref[c].astype(BF16))
        o = o_scr[...]
        rms = lax.rsqrt(jnp.mean(o * o, axis=-1, keepdims=True) + NORM_EPS)
        on = o * rms
        gn = gn_ref[...]
        gp = g_ref[...]
        sgg = _sigmoid(gp)
        dyv = dy_ref[...]
        d_on = dyv * (gp * sgg)
        dz_ref[3] = (dyv * on * gn * _dsilu(gp, sgg)).astype(BF16)
        dgn_ref[...] += jnp.sum(d_on * on, axis=0, keepdims=True)
        u = d_on * gn
        do = (rms * (u - on * jnp.mean(u * on, axis=-1, keepdims=True))).astype(BF16)
        gup = [_dot_tn(do[sl], qt[sl]) for sl in sls]
        da = [jnp.where(tri, _dot_nt(do[sl], v[sl]), 0.0).astype(BF16) for sl in sls]
        gt = gstate[...]
        gts = [None] * nc
        for c in reversed(range(nc)):
            gts[c] = gt
            gt = gt * jnp.exp(bc[c]) + gup[c]
        gstate[...] = gt
        for c, sl in enumerate(sls):
            stp = st_ref[c]
            gtb = gts[c].astype(BF16)
            dqt = _dot(da[c], kt[sl]) + _dot(do[sl], stp.astype(BF16))
            dkt = _dot_tn(da[c], qt[sl])
            dks = _dot(v[sl], gtb) * eend[sl]
            dv_scr[sl, :] = _dot_tn(a[c], do[sl]) + _dot_nt(ke[sl], gtb)
            dq_scr[sl, :] = dqt * eb[sl]
            dk_scr[sl, :] = dkt * enb[sl] + dks
            ech = (jnp.sum(k[sl] * dks, axis=0, keepdims=True)
                   + jnp.sum(gts[c] * jnp.exp(bc[c]) * stp, axis=0, keepdims=True))
            e_scr[sl, :] = jnp.broadcast_to(ech, (C, H))
        dq = dq_scr[...]
        dk = dk_scr[...]
        dlf = _tri_dot(upper, q * dq - k * dk) + e_scr[...]
        dft = dlf / f - dk
        dz_ref[0] = (dq * _dsilu(qp, sq)).astype(BF16)
        dz_ref[1] = (dft * (1.0 - lb) * sg * (1.0 - sg)).astype(BF16)
        dz_ref[2] = dv_scr[...].astype(BF16)
        dlb_ref[...] += jnp.sum(dft * (1.0 - sg), axis=0, keepdims=True)

    col = lambda kk: (lambda h, n: (nb - 1 - n, kk * nh + h))
    return pl.pallas_call(
        body, name="hgrn_bwd", grid=(nh, nb),
        out_shape=(jax.ShapeDtypeStruct((4, s, e), BF16), jax.ShapeDtypeStruct((1, e), F32),
                   jax.ShapeDtypeStruct((1, H), F32)),
        in_specs=[pl.BlockSpec((tb, H), col(0)), pl.BlockSpec((tb, H), col(1)),
                  pl.BlockSpec((tb, H), col(2)), pl.BlockSpec((tb, H), col(3)),
                  pl.BlockSpec((tb, H), lambda h, n: (nb - 1 - n, h)),
                  pl.BlockSpec((None, nc, H, H), lambda h, n: (h, nb - 1 - n, 0, 0)),
                  pl.BlockSpec((2, H), lambda h, n: (0, h)), pl.BlockSpec((1, H), lambda h, n: (0, 0))],
        out_specs=(pl.BlockSpec((4, tb, H), lambda h, n: (0, nb - 1 - n, h)),
                   pl.BlockSpec((1, H), lambda h, n: (0, h)), pl.BlockSpec((1, H), lambda h, n: (0, 0))),
        scratch_shapes=[pltpu.VMEM((H, H), F32)] + [pltpu.VMEM((tb, H), F32)] * 5,
        compiler_params=_params("arbitrary", "arbitrary"))(z, z, z, z, dy, states, lb_logits, gnorm)


ATTN_T = 16 * ATTN_BAND
SCALE = ATTN_HEAD ** -0.5
TILE_UNROLL = 2


def _slope(hh, nheads):
    head = (2 * pl.program_id(0) + hh + 1).astype(F32)
    return jnp.exp(jnp.full((1, 1), -8.0 / nheads * math.log(2.0), F32) * head)


def _fill_bias(bias, nheads, delta, edge_ok):
    band = (delta >= 0) & (delta <= ATTN_BAND)
    dist = delta.astype(F32)
    for pi, dil in enumerate(DILATIONS):
        for hh in range(2):
            full = jnp.where(band, -(_slope(hh, nheads) * float(dil)) * dist, NEG_BIG)
            bias[(pi * 2 + hh) * 2] = full
            bias[(pi * 2 + hh) * 2 + 1] = jnp.where(edge_ok, full, NEG_BIG)


def _rows(start, size, stride):
    if stride == 1:
        return pl.ds(pl.multiple_of(start, ATTN_BAND), size)
    return pl.ds(start, size, stride=stride)


def _head_lanes(rows, hh):
    return (lax.broadcasted_iota(jnp.int32, (rows, LANES), 1) // ATTN_HEAD) == hh


def _attn_fwd(z):
    s = z.shape[0]
    e = z.shape[1] // N_SPLITS
    npair = e // LANES
    T = ATTN_T
    assert s % T == 0
    nsb = s // T
    W = ATTN_BAND
    nt = T // W
    HD = ATTN_HEAD
    chunk = 256

    def body(q_ref, kp_ref, kc_ref, vp_ref, vc_ref, g_ref, o_ref, l_ref, y_ref, qa, kbuf, va, bias, accs, ms, lsw):
        sb = pl.program_id(1)
        def stage(i, carry):
            rows = pl.ds(pl.multiple_of(i * chunk, chunk), chunk)
            upper = pl.ds(pl.multiple_of(T + i * chunk, chunk), chunk)
            kbuf[upper, :] = kc_ref[rows, :]
            for hh in range(2):
                mine = _head_lanes(chunk, hh)
                qa[hh, rows, :] = jnp.where(mine, q_ref[rows, :] * SCALE, 0.0)
                va[hh, upper, :] = jnp.where(mine, vc_ref[rows, :], 1.0)
            return carry

        lax.fori_loop(0, T // chunk, stage, 0)

        @pl.when(sb == 0)
        def _():
            def stage_prev(i, carry):
                rows = pl.ds(pl.multiple_of(i * chunk, chunk), chunk)
                kbuf[rows, :] = kp_ref[rows, :]
                for hh in range(2):
                    va[hh, rows, :] = jnp.where(_head_lanes(chunk, hh), vp_ref[rows, :], 1.0)
                return carry

            lax.fori_loop(0, T // chunk, stage_prev, 0)
        qi = lax.broadcasted_iota(jnp.int32, (W, 2 * W), 0)
        kj = lax.broadcasted_iota(jnp.int32, (W, 2 * W), 1)
        _fill_bias(bias, 2 * npair, W + qi - kj, kj >= W)

        def tile(tau, carry):
            first = _head_lanes(W, 0)
            rows, scores = [], []
            for pi, dil in enumerate(DILATIONS):
                r = tau % dil
                ub = tau // dil
                qrows = _rows(r + dil * W * ub, W, dil)
                krows = _rows(T + dil * W * (ub - 1) + r, 2 * W, dil)
                var = jnp.where((sb == 0) & (ub == 0), 1, 0)
                kt = kbuf[krows, :].astype(BF16)
                rows.append((qrows, krows))
                scores.append([_dot_nt(qa[hh, qrows, :].astype(BF16), kt) + bias[(pi * 2 + hh) * 2 + var]
                               for hh in range(2)])
            maxes = [[jnp.max(sc, axis=-1, keepdims=True) for sc in pair] for pair in scores]
            probs = [[jnp.exp(sc - m).astype(BF16) for sc, m in zip(ps, pm)] for ps, pm in zip(scores, maxes)]
            for pi, (qrows, krows) in enumerate(rows):
                outs = [_dot(probs[pi][hh], va[hh, krows, :].astype(BF16)) for hh in range(2)]
                accs[pi, qrows, :] = jnp.where(first, outs[0], outs[1])
                lsw[pi, qrows, :] = jnp.where(first, outs[1], outs[0])
                ms[pi, qrows, :] = jnp.where(first, maxes[pi][0], maxes[pi][1])
            return carry

        lax.fori_loop(0, nt, tile, 0, unroll=TILE_UNROLL)

        def merge(i, carry):
            rows = pl.ds(pl.multiple_of(i * chunk, chunk), chunk)
            m1, m2, m3 = ms[0, rows, :], ms[1, rows, :], ms[2, rows, :]
            mx = jnp.maximum(jnp.maximum(m1, m2), m3)
            w1, w2, w3 = jnp.exp(m1 - mx), jnp.exp(m2 - mx), jnp.exp(m3 - mx)
            unswap = lambda a: pltpu.roll(a, ATTN_HEAD, 1)
            den = w1 * unswap(lsw[0, rows, :]) + w2 * unswap(lsw[1, rows, :]) + w3 * unswap(lsw[2, rows, :])
            o = (w1 * accs[0, rows, :] + w2 * accs[1, rows, :] + w3 * accs[2, rows, :]) / den
            o_ref[rows, :] = o
            l_ref[rows, :] = mx + jnp.log(den)
            gp = g_ref[rows, :]
            y_ref[rows, :] = (o * (gp * _sigmoid(gp))).astype(BF16)
            upper = pl.ds(pl.multiple_of(T + i * chunk, chunk), chunk)
            kbuf[rows, :] = kbuf[upper, :]
            for hh in range(2):
                va[hh, rows, :] = va[hh, upper, :]
            return carry

        lax.fori_loop(0, T // chunk, merge, 0)

    cur = lambda split: (lambda hp, sb: (sb, split * npair + hp))
    prev = lambda split: (lambda hp, sb: (jnp.maximum(sb - 1, 0), split * npair + hp))
    blk = lambda index: pl.BlockSpec((T, LANES), index)
    out = blk(lambda hp, sb: (sb, hp))
    buf = lambda rows: pltpu.VMEM((rows, LANES), F32)
    return pl.pallas_call(
        body, name="attn_fwd", grid=(npair, nsb),
        out_shape=(jax.ShapeDtypeStruct((s, e), F32), jax.ShapeDtypeStruct((s, e), F32), jax.ShapeDtypeStruct((s, e), BF16)),
        in_specs=[blk(cur(4)), blk(prev(5)), blk(cur(5)), blk(prev(6)), blk(cur(6)), blk(cur(7))],
        out_specs=(out, out, out),
        scratch_shapes=[pltpu.VMEM((2, T, LANES), F32), buf(2 * T), pltpu.VMEM((2, 2 * T, LANES), F32),
                        pltpu.VMEM((12, W, 2 * W), F32)] + [pltpu.VMEM((3, T, LANES), F32)] * 3,
        compiler_params=_params("parallel", "arbitrary"))(z, z, z, z, z, z)


def _outproj_loss(x, y_h, y_a, w_out_full, final_gain, target):
    s, d = x.shape
    e = y_h.shape[1]
    tm = _tile(s, 256)

    def body(x_ref, yh_ref, ya_ref, w_ref, g_ref, t_ref, dx_ref, dxb_ref, dy_ref, loss_ref, dg_ref):
        @pl.when(pl.program_id(0) == 0)
        def _():
            loss_ref[...] = jnp.zeros_like(loss_ref)
            dg_ref[...] = jnp.zeros_like(dg_ref)

        w = w_ref[...]
        x2 = x_ref[...] + _dot(yh_ref[...], w[0:e]) + _dot(ya_ref[...], w[e:2 * e])
        r = lax.rsqrt(jnp.mean(x2 * x2, axis=-1, keepdims=True) + NORM_EPS)
        xn = x2 * r
        g = g_ref[...]
        err = xn * g - t_ref[...]
        loss_ref[...] += jnp.sum(err * err, axis=0, keepdims=True) * (0.5 / d)
        dyo = err * (1.0 / d)
        dg_ref[...] += jnp.sum(dyo * xn, axis=0, keepdims=True)
        u = dyo * g
        dx2 = r * (u - xn * jnp.mean(u * xn, axis=-1, keepdims=True))
        dx_ref[...] = dx2
        dxb = dx2.astype(BF16)
        dxb_ref[...] = dxb
        dy_ref[...] = _dot_nt(dxb, w)

    row = pl.BlockSpec((tm, d), lambda i: (i, 0))
    half = pl.BlockSpec((tm, e), lambda i: (i, 0))
    vec = pl.BlockSpec((1, d), lambda i: (0, 0))
    return pl.pallas_call(
        body, name="outproj_loss", grid=(s // tm,),
        out_shape=(jax.ShapeDtypeStruct((s, d), F32), jax.ShapeDtypeStruct((s, d), BF16),
                   jax.ShapeDtypeStruct((s, 2 * e), F32), jax.ShapeDtypeStruct((1, d), F32),
                   jax.ShapeDtypeStruct((1, d), F32)),
        in_specs=[row, half, half, pl.BlockSpec((2 * e, d), lambda i: (0, 0)), vec, row],
        out_specs=(row, row, pl.BlockSpec((tm, 2 * e), lambda i: (i, 0)), vec, vec),
        compiler_params=_params("arbitrary"))(x, y_h, y_a, w_out_full, final_gain, target)


def _attn_bwd(z, dy, o, lse):
    s, e = o.shape
    npair = e // LANES
    T = ATTN_T
    assert s % T == 0
    nsb = s // T
    W = ATTN_BAND
    nt = T // W
    HD = ATTN_HEAD
    chunk = 256

    def body(k_ref, v_ref, qc_ref, qn_ref, dyc_ref, dyn_ref, gc_ref, gn_ref, oc_ref, on_ref, lc_ref, ln_ref,
             dz_ref, qa, doa, ka, va, dqacc, dkacc, dvacc, bias):
        sb = pl.program_id(1)
        def stage_queries(half, q_r, dy_r, g_r, o_r, l_r):
            def stage(i, carry):
                rows = pl.ds(pl.multiple_of(i * chunk, chunk), chunk)
                dst = pl.ds(pl.multiple_of(half * T + i * chunk, chunk), chunk)
                lane = lax.broadcasted_iota(jnp.int32, (chunk, LANES), 1)
                gp = g_r[rows, :]
                dov = dy_r[rows, :] * (gp * _sigmoid(gp))
                qv = q_r[rows, :] * SCALE
                same_head = (lax.broadcasted_iota(jnp.int32, (LANES, LANES), 0) // HD
                             == lax.broadcasted_iota(jnp.int32, (LANES, LANES), 1) // HD)
                ones = jnp.where(same_head, 1.0, 0.0).astype(BF16)
                hi, mid, lo = (p.astype(BF16) for p in _split3(dov * o_r[rows, :]))
                delta = _dot(hi, ones) + _dot(mid, ones) + _dot(lo, ones)
                swap = lambda a: pltpu.roll(a, HD, 1)
                lse_parts = [swap(p) for p in _split3(l_r[rows, :])]
                dl_parts = [swap(p) for p in _split3(delta)]
                for hh in range(2):
                    mine = _head_lanes(chunk, hh)
                    spare = (1 - hh) * HD
                    qh = jnp.where(mine, qv, 0.0)
                    dh = jnp.where(mine, dov, 0.0)
                    for j in range(3):
                        qh = jnp.where(lane == spare + j, lse_parts[j], qh)
                        dh = jnp.where(lane == spare + j, dl_parts[j], dh)
                    qa[hh, dst, :] = qh
                    doa[hh, dst, :] = dh
                return carry

            lax.fori_loop(0, T // chunk, stage, 0)

        @pl.when(sb == 0)
        def _():
            stage_queries(0, qc_ref, dyc_ref, gc_ref, oc_ref, lc_ref)

        stage_queries(1, qn_ref, dyn_ref, gn_ref, on_ref, ln_ref)

        def stage_keys(i, carry):
            rows = pl.ds(pl.multiple_of(i * chunk, chunk), chunk)
            lane = lax.broadcasted_iota(jnp.int32, (chunk, LANES), 1)
            for hh in range(2):
                spare = (1 - hh) * HD
                minus = (lane >= spare) & (lane < spare + 3)
                ka[hh, rows, :] = jnp.where(minus, -1.0, k_ref[rows, :])
                va[hh, rows, :] = jnp.where(minus, -1.0, v_ref[rows, :])
            gp = gc_ref[rows, :]
            dz_ref[3, rows, :] = (dyc_ref[rows, :] * oc_ref[rows, :] * _dsilu(gp, _sigmoid(gp))).astype(BF16)
            return carry

        lax.fori_loop(0, T // chunk, stage_keys, 0)

        @pl.when(sb == 0)
        def _():
            dqacc[0:T, :] = jnp.zeros((T, LANES), F32)

        dqacc[T:, :] = jnp.zeros((T, LANES), F32)
        dkacc[...] = jnp.zeros_like(dkacc)
        dvacc[...] = jnp.zeros_like(dvacc)
        qi = lax.broadcasted_iota(jnp.int32, (2 * W, W), 0)
        kj = lax.broadcasted_iota(jnp.int32, (2 * W, W), 1)
        _fill_bias(bias, 2 * npair, qi - kj, qi < W)

        def tile(tau, carry):
            rows, ops, sc, dpd = [], [], [], []
            for pi, dil in enumerate(DILATIONS):
                r = tau % dil
                ub = tau // dil
                start = r + dil * W * ub
                krows = _rows(start, W, dil)
                qrows = _rows(start, 2 * W, dil)
                var = jnp.where((sb == nsb - 1) & (ub == nt // dil - 1), 1, 0)
                rows.append((krows, qrows))
                for hh in range(2):
                    kt = ka[hh, krows, :].astype(BF16)
                    vt = va[hh, krows, :].astype(BF16)
                    qt = qa[hh, qrows, :].astype(BF16)
                    dt = doa[hh, qrows, :].astype(BF16)
                    ops.append((kt, qt, dt))
                    sc.append(_dot_nt(qt, kt) + bias[(pi * 2 + hh) * 2 + var])
                    dpd.append(_dot_nt(dt, vt))
            ps = [jnp.exp(s) for s in sc]
            dss = [(p * d).astype(BF16) for p, d in zip(ps, dpd)]
            pbs = [p.astype(BF16) for p in ps]
            dvs = [_dot_tn(pb, dt) for pb, (kt, qt, dt) in zip(pbs, ops)]
            dks = [_dot_tn(ds, qt) for ds, (kt, qt, dt) in zip(dss, ops)]
            dqs = [_dot(ds, kt) for ds, (kt, qt, dt) in zip(dss, ops)]
            for pi, (krows, qrows) in enumerate(rows):
                dkacc[krows, :] += jnp.where(_head_lanes(W, 0), dks[2 * pi], dks[2 * pi + 1])
                dvacc[krows, :] += jnp.where(_head_lanes(W, 0), dvs[2 * pi], dvs[2 * pi + 1])
                dqacc[qrows, :] += jnp.where(_head_lanes(2 * W, 0), dqs[2 * pi], dqs[2 * pi + 1]) * SCALE
            return carry

        lax.fori_loop(0, nt, tile, 0, unroll=TILE_UNROLL)

        def flush(i, carry):
            rows = pl.ds(pl.multiple_of(i * chunk, chunk), chunk)
            nxt = pl.ds(pl.multiple_of(T + i * chunk, chunk), chunk)
            dz_ref[0, rows, :] = dqacc[rows, :].astype(BF16)
            dz_ref[1, rows, :] = dkacc[rows, :].astype(BF16)
            dz_ref[2, rows, :] = dvacc[rows, :].astype(BF16)
            dqacc[rows, :] = dqacc[nxt, :]
            for hh in range(2):
                qa[hh, rows, :] = qa[hh, nxt, :]
                doa[hh, rows, :] = doa[hh, nxt, :]
            return carry

        lax.fori_loop(0, T // chunk, flush, 0)

    zc = lambda split: (lambda hp, sb: (sb, split * npair + hp))
    zn = lambda split: (lambda hp, sb: (jnp.minimum(sb + 1, nsb - 1), split * npair + hp))
    ec = lambda off: (lambda hp, sb: (sb, off + hp))
    en = lambda off: (lambda hp, sb: (jnp.minimum(sb + 1, nsb - 1), off + hp))
    blk = lambda index: pl.BlockSpec((T, LANES), index)
    buf = lambda rows: pltpu.VMEM((rows, LANES), F32)
    return pl.pallas_call(
        body, name="attn_bwd", grid=(npair, nsb), out_shape=jax.ShapeDtypeStruct((4, s, e), BF16),
        in_specs=[blk(zc(5)), blk(zc(6)), blk(zc(4)), blk(zn(4)), blk(ec(npair)), blk(en(npair)),
                  blk(zc(7)), blk(zn(7)), blk(ec(0)), blk(en(0)), blk(ec(0)), blk(en(0))],
        out_specs=pl.BlockSpec((4, T, LANES), lambda hp, sb: (0, sb, hp)),
        scratch_shapes=[pltpu.VMEM((2, 2 * T, LANES), F32), pltpu.VMEM((2, 2 * T, LANES), F32),
                        pltpu.VMEM((2, T, LANES), F32), pltpu.VMEM((2, T, LANES), F32),
                        buf(2 * T), buf(T), buf(T), pltpu.VMEM((12, 2 * W, W), F32)],
        compiler_params=_params("parallel", "arbitrary"))(z, z, z, z, dy, dy, z, z, o, o, lse, lse)


def _dz_specs(tm, e, axis):
    def mk(lo, hi):
        def index(i, k):
            row, grp = (i, k) if axis == 1 else (k, i)
            return (jnp.clip(grp - lo, 0, hi - lo - 1), row, 0)
        return pl.BlockSpec((None, tm, e), index)
    return [mk(0, 4), mk(4, 8)]


def _dz_pick(grp, dzh_ref, dza_ref, fn):
    @pl.when(grp < 4)
    def _():
        fn(dzh_ref[...])

    @pl.when(grp >= 4)
    def _():
        fn(dza_ref[...])


def _dh_dx(dzh, dza, w_full, x, gain, dx2):
    s, d = x.shape
    e = dzh.shape[2]
    tm = _tile(s, 512)

    def body(dzh_ref, dza_ref, w_ref, x_ref, g_ref, dx2_ref, gx_ref, dg_ref, acc):
        i, k = pl.program_id(0), pl.program_id(1)

        @pl.when((i == 0) & (k == 0))
        def _():
            dg_ref[...] = jnp.zeros_like(dg_ref)

        @pl.when(k == 0)
        def _():
            acc[...] = jnp.zeros_like(acc)

        def add(dz):
            acc[...] += _dot_nt(dz, w_ref[...])

        _dz_pick(k, dzh_ref, dza_ref, add)

        @pl.when(k == N_SPLITS - 1)
        def _():
            dh = acc[...]
            xv = x_ref[...]
            r = lax.rsqrt(jnp.mean(xv * xv, axis=-1, keepdims=True) + NORM_EPS)
            xn = xv * r
            dg_ref[...] += jnp.sum(dh * xn, axis=0, keepdims=True)
            u = dh * g_ref[...]
            gx_ref[...] = dx2_ref[...] + r * (u - xn * jnp.mean(u * xn, axis=-1, keepdims=True))

    row = pl.BlockSpec((tm, d), lambda i, k: (i, 0))
    vec = pl.BlockSpec((1, d), lambda i, k: (0, 0))
    return pl.pallas_call(
        body, name="dh_dx", grid=(s // tm, N_SPLITS),
        out_shape=(jax.ShapeDtypeStruct((s, d), F32), jax.ShapeDtypeStruct((1, d), F32)),
        in_specs=_dz_specs(tm, e, 1) + [pl.BlockSpec((None, d, e), lambda i, k: (k, 0, 0)), row, vec, row],
        out_specs=(row, vec), scratch_shapes=[pltpu.VMEM((tm, d), F32)],
        compiler_params=_params("arbitrary", "arbitrary"))(dzh, dza, w_full, x, gain, dx2)


def _position():
    x, y, c = lax.axis_index("x"), lax.axis_index("y"), lax.axis_index("c")
    return x, y, c


def _xor_peer(x, y, c, mask):
    return (x ^ ((mask >> 2) & 1), y ^ ((mask >> 1) & 1), c ^ (mask & 1))


def _block_order(masks):
    me = 4 * lax.axis_index("x") + 2 * lax.axis_index("y") + lax.axis_index("c")
    return jnp.stack([me ^ m for m in masks]).astype(jnp.int32)


GATHER_MASKS = (0, 1, 4, 2, 6, 5, 3, 7)


def _inproj_gather(h, w_loc, wo_loc):
    s, d = h.shape
    e = w_loc.shape[1]
    tm = _tile(s, 1024)
    ni = s // tm
    pre = max(ni - 2, 0)

    def body(order_ref, h_ref, w_ref, wo_ref, z_ref, wf_ref, wof_ref, wbuf, send_sems, recv_sems, osend, orecv,
             local_sems, wsems):
        j, i = pl.program_id(0), pl.program_id(1)
        x, y, c = _position()
        me, sibling = (x, y, c), (x, y, 1 - c)
        chips = [(1 - x, y), (x, 1 - y), (1 - x, 1 - y)]
        blk = lambda p: 4 * p[0] + 2 * p[1] + p[2]

        def copy(k, block, to, src=None):
            dst = wf_ref.at[blk(block)]
            return pltpu.make_async_remote_copy(
                src_ref=dst if src is None else src, dst_ref=dst, send_sem=send_sems.at[k], recv_sem=recv_sems.at[k],
                device_id=to, device_id_type=MESH)

        first = [copy(0, me, sibling, src=w_ref)] + [copy(1 + q, me, (*chip, c), src=w_ref) for q, chip in enumerate(chips)]
        passed = [copy(4 + q, (*chip, c), sibling) for q, chip in enumerate(chips)]
        mine = pltpu.make_async_copy(w_ref, wf_ref.at[blk(me)], local_sems.at[0])
        ocopies = [pltpu.make_async_remote_copy(
            src_ref=wo_ref, dst_ref=wof_ref.at[blk(me)], send_sem=osend.at[k], recv_sem=orecv.at[k],
            device_id=_xor_peer(x, y, c, k + 1), device_id_type=MESH) for k in range(N_DEV - 1)]
        omine = pltpu.make_async_copy(wo_ref, wof_ref.at[blk(me)], local_sems.at[1])
        blocks = [me, sibling] + [(*chip, c) for chip in chips] + [(*chip, 1 - c) for chip in chips]
        arrive = [None, copy(0, sibling, me)] + [copy(1 + q, (*chip, c), me) for q, chip in enumerate(chips)] \
            + [copy(4 + q, (*chip, 1 - c), me) for q, chip in enumerate(chips)]
        forward = [None, None] + passed + [None, None, None]

        def load(slot, src):
            return pltpu.make_async_copy(src, wbuf.at[slot], wsems.at[slot])

        @pl.when((j == 0) & (i == 0))
        def _():
            for cp in [mine, omine] + first + ocopies:
                cp.start()
            load(0, w_ref).start()

        for jj in range(N_DEV):
            @pl.when((j == jj) & (i == 0))
            def _():
                load(jj % 2, w_ref).wait()

            if jj + 1 < N_DEV:
                @pl.when((j == jj) & (i == pre))
                def _():
                    arrive[jj + 1].wait_recv()
                    if forward[jj + 1] is not None:
                        forward[jj + 1].start()
                    load((jj + 1) % 2, wf_ref.at[blk(blocks[jj + 1])]).start()

        z_ref[...] = _dot(h_ref[...], wbuf[j % 2])

        @pl.when((j == N_DEV - 1) & (i == ni - 1))
        def _():
            for cp in first + passed:
                cp.wait_send()
            for cp in ocopies:
                cp.wait_send()
                cp.wait_recv()
            mine.wait()
            omine.wait()

    grid_spec = pltpu.PrefetchScalarGridSpec(
        num_scalar_prefetch=1, grid=(N_DEV, ni),
        in_specs=[pl.BlockSpec((tm, d), lambda j, i, o: (i, 0)), ANY, ANY],
        out_specs=(pl.BlockSpec((tm, e), lambda j, i, o: (i, o[j])), ANY, ANY),
        scratch_shapes=[pltpu.VMEM((2, d, e), BF16), pltpu.SemaphoreType.DMA((7,)), pltpu.SemaphoreType.DMA((7,)),
                        pltpu.SemaphoreType.DMA((7,)), pltpu.SemaphoreType.DMA((7,)), pltpu.SemaphoreType.DMA((2,)),
                        pltpu.SemaphoreType.DMA((2,))])
    return pl.pallas_call(
        body, name="inproj_gather", grid_spec=grid_spec,
        out_shape=(jax.ShapeDtypeStruct((s, N_SPLITS * e), F32), jax.ShapeDtypeStruct((N_DEV, d, e), BF16),
                   jax.ShapeDtypeStruct((N_DEV,) + wo_loc.shape, BF16)),
        compiler_params=_params("arbitrary", "arbitrary"))(_block_order(GATHER_MASKS), h, w_loc, wo_loc)


SCATTER_MASKS = (7, 6, 5, 4, 3, 2, 1, 0)
N_CHIPS = 4


def _scatter_block(k, acc, stage, tmp, own_ref, ra_ref, rb_ref, sa_send, sa_recv, sb_send, sb_recv, loc_sem, last):
    x, y, c = _position()
    chip_of = lambda t: _xor_peer(x, y, c, SCATTER_MASKS[2 * t + 1])

    def ship(t):
        return pltpu.make_async_remote_copy(
            src_ref=stage.at[0], dst_ref=ra_ref.at[t], send_sem=sa_send.at[t], recv_sem=sa_recv.at[t],
            device_id=(x, y, 1 - c), device_id_type=MESH)

    def send(t):
        return pltpu.make_async_remote_copy(
            src_ref=stage.at[1], dst_ref=rb_ref.at[t], send_sem=sb_send.at[t], recv_sem=sb_recv.at[t],
            device_id=chip_of(t), device_id_type=MESH)

    for kk in range(N_DEV):
        t = kk // 2

        @pl.when(last & (k == kk))
        def _():
            if kk % 2 == 0:
                if t >= 1:
                    ship(t - 1).wait_send()
                stage[0] = acc[...].astype(BF16)
                ship(t).start()
            else:
                ship(t).wait_recv()
                fetch = pltpu.make_async_copy(ra_ref.at[t], tmp, loc_sem)
                fetch.start()
                fetch.wait()
                acc[...] += tmp[...].astype(F32)
                if t < N_CHIPS - 1:
                    if t >= 1:
                        send(t - 1).wait_send()
                    stage[1] = acc[...].astype(BF16)
                    send(t).start()
                else:
                    keep = pltpu.make_async_copy(acc, own_ref, loc_sem)
                    keep.start()
                    keep.wait()
                    ship(t).wait_send()
                    send(t - 1).wait_send()
                    for q in range(N_CHIPS - 1):
                        send(q).wait_recv()


def _scatter_scratch(rows, cols):
    return [pltpu.VMEM((rows, cols), F32), pltpu.VMEM((2, rows, cols), BF16), pltpu.VMEM((rows, cols), BF16),
            pltpu.SemaphoreType.DMA((N_CHIPS,)), pltpu.SemaphoreType.DMA((N_CHIPS,)),
            pltpu.SemaphoreType.DMA((N_CHIPS - 1,)), pltpu.SemaphoreType.DMA((N_CHIPS - 1,)), pltpu.SemaphoreType.DMA(())]


def _scatter_out(rows, cols):
    return (jax.ShapeDtypeStruct((rows, cols), F32), jax.ShapeDtypeStruct((N_CHIPS, rows, cols), BF16),
            jax.ShapeDtypeStruct((N_CHIPS - 1, rows, cols), BF16))


def _dwin_scatter(h, dzh, dza):
    s, d = h.shape
    e = dzh.shape[2]
    ts = _tile(s, 1024)
    ns = s // ts

    def body(order_ref, dzh_ref, dza_ref, h_ref, own_ref, ra_ref, rb_ref, acc, stage, tmp, *sems):
        k, step = pl.program_id(0), pl.program_id(1)

        @pl.when(step == 0)
        def _():
            acc[...] = jnp.zeros_like(acc)

        def add(dz):
            acc[...] += _dot_tn(h_ref[...], dz)

        _dz_pick(order_ref[k], dzh_ref, dza_ref, add)
        _scatter_block(k, acc, stage, tmp, own_ref, ra_ref, rb_ref, *sems, step == ns - 1)

    def dz_spec(lo):
        return pl.BlockSpec((None, ts, e), lambda k, st, o: (jnp.clip(o[k] - lo, 0, 3), st, 0))

    grid_spec = pltpu.PrefetchScalarGridSpec(
        num_scalar_prefetch=1, grid=(N_DEV, ns),
        in_specs=[dz_spec(0), dz_spec(4), pl.BlockSpec((ts, d), lambda k, st, o: (st, 0))],
        out_specs=(ANY, ANY, ANY), scratch_shapes=_scatter_scratch(d, e))
    own, _, rb = pl.pallas_call(
        body, name="dwin_scatter", grid_spec=grid_spec, out_shape=_scatter_out(d, e),
        compiler_params=_params("arbitrary", "arbitrary"))(_block_order(SCATTER_MASKS), dzh, dza, h)
    return own, rb


def _dwout_scatter(y_h, y_a, dxb):
    s, e = y_h.shape
    d = dxb.shape[1]
    r = 2 * e // N_DEV
    pairs = e // (2 * r)
    ts = _tile(s, 1024)
    ns = s // ts
    chip_masks = SCATTER_MASKS[1::2]

    def body(pair_ref, yh_ref, ya_ref, dx_ref, own_ref, ra_ref, rb_ref, acc, pend, stage, tmp,
             sa_send, sa_recv, sb_send, sb_recv, loc_sem):
        t, step = pl.program_id(0), pl.program_id(1)
        x, y, c = _position()

        @pl.when(step == 0)
        def _():
            acc[...] = jnp.zeros_like(acc)

        @pl.when(pair_ref[t] < pairs)
        def _():
            acc[...] += _dot_tn(yh_ref[...], dx_ref[...])

        @pl.when(pair_ref[t] >= pairs)
        def _():
            acc[...] += _dot_tn(ya_ref[...], dx_ref[...])

        mine = pl.ds(pl.multiple_of(c * r, r), r)
        other = pl.ds(pl.multiple_of((1 - c) * r, r), r)

        def ship(q):
            return pltpu.make_async_remote_copy(
                src_ref=stage.at[0], dst_ref=ra_ref.at[q], send_sem=sa_send.at[q], recv_sem=sa_recv.at[q],
                device_id=(x, y, 1 - c), device_id_type=MESH)

        def send(q):
            return pltpu.make_async_remote_copy(
                src_ref=stage.at[1], dst_ref=rb_ref.at[q], send_sem=sb_send.at[q], recv_sem=sb_recv.at[q],
                device_id=_xor_peer(x, y, c, chip_masks[q]), device_id_type=MESH)

        def sibling_share(q):
            ship(q).wait_recv()
            fetch = pltpu.make_async_copy(ra_ref.at[q], tmp, loc_sem)
            fetch.start()
            fetch.wait()
            return tmp[...].astype(F32)

        for tt in range(N_CHIPS):
            @pl.when((step == ns - 1) & (t == tt))
            def _():
                if tt >= 1:
                    ship(tt - 1).wait_send()
                stage[0] = acc[other, :].astype(BF16)
                ship(tt).start()
                if tt >= 1:
                    if tt >= 2:
                        send(tt - 2).wait_send()
                    stage[1] = (pend[...] + sibling_share(tt - 1)).astype(BF16)
                    send(tt - 1).start()
                if tt < N_CHIPS - 1:
                    pend[...] = acc[mine, :]
                else:
                    pend[...] = acc[mine, :] + sibling_share(tt)
                    keep = pltpu.make_async_copy(pend, own_ref, loc_sem)
                    keep.start()
                    keep.wait()
                    ship(tt).wait_send()
                    send(tt - 1).wait_send()
                    for q in range(N_CHIPS - 1):
                        send(q).wait_recv()

    def y_spec(lo):
        return pl.BlockSpec((ts, 2 * r), lambda t, st, o: (st, jnp.clip(o[t] - lo, 0, pairs - 1)))

    grid_spec = pltpu.PrefetchScalarGridSpec(
        num_scalar_prefetch=1, grid=(N_CHIPS, ns),
        in_specs=[y_spec(0), y_spec(pairs), pl.BlockSpec((ts, d), lambda t, st, o: (st, 0))],
        out_specs=(ANY, ANY, ANY),
        scratch_shapes=[pltpu.VMEM((2 * r, d), F32), pltpu.VMEM((r, d), F32)] + _scatter_scratch(r, d)[1:])
    own, _, rb = pl.pallas_call(
        body, name="dwout_scatter", grid_spec=grid_spec, out_shape=_scatter_out(r, d),
        compiler_params=_params("arbitrary", "arbitrary"))(_block_order(chip_masks) // 2, y_h, y_a, dxb)
    return own, rb


def _sum_chips_adamw(own, recv, w, m, v):
    r, c = w.shape
    tr = _tile(r, 128)

    def body(own_ref, rc_ref, w_ref, m_ref, v_ref, g_ref, d_ref, mo_ref, vo_ref):
        g = own_ref[...]
        for q in range(N_CHIPS - 1):
            g = g + rc_ref[q].astype(F32)
        g_ref[...] = g
        d_ref[...], mo_ref[...], vo_ref[...] = _adamw(w_ref[...], g, m_ref[...], v_ref[...])

    blk = pl.BlockSpec((tr, c), lambda i: (i, 0))
    shp = jax.ShapeDtypeStruct((r, c), F32)
    return pl.pallas_call(
        body, name="sum_chips_adamw", grid=(r // tr,), out_shape=(shp, shp, shp, shp),
        in_specs=[blk, pl.BlockSpec((N_CHIPS - 1, tr, c), lambda i: (0, i, 0)), blk, blk, blk],
        out_specs=(blk, blk, blk, blk), compiler_params=_params("parallel"))(own, recv, w, m, v)


SMALL_ROWS = 8
ROW_LB = 4
ROW_GN = 6
ROW_LOSS = 7


def _small_allreduce_adamw(part, w, m, v, lb_logits):
    width = part.shape[1]

    def body(p_ref, w_ref, m_ref, v_ref, lb_ref, g_ref, d_ref, mo_ref, vo_ref, buf, send_sems, recv_sems):
        x, y, c = _position()
        me = 4 * x + 2 * y + c
        buf[me] = p_ref[...]
        copies = []
        for k in range(N_DEV - 1):
            bx, by, bc = ((k + 1) >> 2) & 1, ((k + 1) >> 1) & 1, (k + 1) & 1
            peer = (x ^ bx, y ^ by, c ^ bc)
            copies.append(pltpu.make_async_remote_copy(
                src_ref=p_ref, dst_ref=buf.at[me], send_sem=send_sems.at[k], recv_sem=recv_sems.at[k],
                device_id=peer, device_id_type=MESH))
        for cp in copies:
            cp.start()
        for cp in copies:
            cp.wait_recv()
        for cp in copies:
            cp.wait_send()
        tot = buf[0]
        for dev in range(1, N_DEV):
            tot = tot + buf[dev]
        lbv = lb_ref[...]
        lb = _sigmoid(lbv[0:1] - lbv[1:2])
        glb = tot[ROW_LB:ROW_LB + 1] * lb * (1.0 - lb)
        loss = jnp.sum(tot[ROW_LOSS:ROW_LOSS + 1], axis=-1, keepdims=True)
        row = lax.broadcasted_iota(jnp.int32, (SMALL_ROWS, width), 0)
        g = jnp.where(row == ROW_LB, glb, jnp.where(row == ROW_LB + 1, -glb, tot))
        g = jnp.where(row == ROW_LOSS, loss, g)
        g_ref[...] = g
        d_ref[...], mo_ref[...], vo_ref[...] = _adamw(w_ref[...], g, m_ref[...], v_ref[...])

    vm = pl.BlockSpec(memory_space=pltpu.VMEM)
    shp = jax.ShapeDtypeStruct((SMALL_ROWS, width), F32)
    return pl.pallas_call(
        body, name="small_allreduce_adamw", out_shape=(shp, shp, shp, shp),
        in_specs=[vm] * 5, out_specs=(vm, vm, vm, vm),
        scratch_shapes=[pltpu.VMEM((N_DEV, SMALL_ROWS, width), F32), pltpu.SemaphoreType.DMA((N_DEV - 1,)),
                        pltpu.SemaphoreType.DMA((N_DEV - 1,))],
    )(part, w, m, v, lb_logits)


def _pack_small(norm_gain, final_gain, lb2, gnorm, last_row, width):
    pad = lambda a: jnp.pad(a.reshape(1, -1), ((0, 0), (0, width - a.size)))
    return jnp.concatenate([norm_gain.reshape(2, width), final_gain.reshape(2, width), lb2.reshape(2, width),
                            pad(gnorm), last_row.reshape(1, width)], axis=0)


def _unpack_small(p, d, e, hd):
    return (p[0:2].reshape(1, d), p[2:4].reshape(d), p[4:6].reshape(2, e), p[6:7, :hd].reshape(1, hd))


def kernel(x, norm_gain, w_in, lb_logits, hgrn_gnorm, w_out, final_gain, loss_target, m_norm_gain, m_w_in, m_lb_logits, m_hgrn_gnorm, m_w_out, m_final_gain, v_norm_gain, v_w_in, v_lb_logits, v_hgrn_gnorm, v_w_out, v_final_gain):
    s, d = x.shape[1], x.shape[2]
    e = w_in.shape[2]
    assert d == 2 * e and lb_logits.shape == (2, e) and w_out.shape[1] * N_DEV == 2 * e
    x2d = x.reshape(s, d)
    tgt = loss_target.reshape(s, d)

    h = _rmsnorm_fwd(x2d, norm_gain)
    z, w_in_full, w_out_full = _inproj_gather(h, _cast_bf16(w_in[0]), _cast_bf16(w_out[0]))
    w_out_full = w_out_full.reshape(2 * e, d)
    y_h, states = _hgrn_fwd(z, lb_logits, hgrn_gnorm)
    o_attn, lse, y_a = _attn_fwd(z)
    dx2, dx2b, dy, loss_vec, dfg = _outproj_loss(x2d, y_h, y_a, w_out_full, final_gain.reshape(1, d), tgt)

    own_o, recv_o = _dwout_scatter(y_h, y_a, dx2b)
    dza = _attn_bwd(z, dy, o_attn, lse)
    dzh, dlb, dgn = _hgrn_bwd(z, dy, states, lb_logits, hgrn_gnorm)
    grad_x, dng = _dh_dx(dzh, dza, w_in_full, x2d, norm_gain, dx2)
    own_i, recv_i = _dwin_scatter(h, dzh, dza)
    g_wi, d_wi, nm_wi, nv_wi = _sum_chips_adamw(own_i, recv_i, w_in[0], m_w_in[0], v_w_in[0])
    g_wo, d_wo, nm_wo, nv_wo = _sum_chips_adamw(own_o, recv_o, w_out[0], m_w_out[0], v_w_out[0])

    width = d // 2
    zero_row = jnp.zeros((1, width), F32)
    loss_row = loss_vec[:, :width] + loss_vec[:, width:]
    part = _pack_small(dng, dfg, jnp.concatenate([dlb, zero_row], axis=0), dgn, loss_row, width)
    pw = _pack_small(norm_gain, final_gain, lb_logits, hgrn_gnorm, zero_row, width)
    pm = _pack_small(m_norm_gain, m_final_gain, m_lb_logits, m_hgrn_gnorm, zero_row, width)
    pv = _pack_small(v_norm_gain, v_final_gain, v_lb_logits, v_hgrn_gnorm, zero_row, width)
    sg, sd, sm, sv = _small_allreduce_adamw(part, pw, pm, pv, lb_logits)
    hd = hgrn_gnorm.shape[1]
    g_ng, g_fg, g_lb, g_gn = _unpack_small(sg, d, e, hd)
    d_ng, d_fg, d_lb, d_gn = _unpack_small(sd, d, e, hd)
    m_ng, m_fg, m_lb, m_gn = _unpack_small(sm, d, e, hd)
    v_ng, v_fg, v_lb, v_gn = _unpack_small(sv, d, e, hd)
    loss = sg[ROW_LOSS, 0]

    one = lambda a: a[None]
    return (loss, grad_x.reshape(1, s, d), g_ng, one(g_wi), g_lb, g_gn, one(g_wo), g_fg,
            d_ng, one(d_wi), d_lb, d_gn, one(d_wo), d_fg,
            m_ng, one(nm_wi), m_lb, m_gn, one(nm_wo), m_fg,
            v_ng, one(nv_wi), v_lb, v_gn, one(nv_wo), v_fg)
```

```python
import functools
import math

import jax
import jax.numpy as jnp
from jax import lax
from jax.experimental import pallas as pl
from jax.experimental.pallas import tpu as pltpu

NORM_EPS = 1e-6
HGRN_HEAD = 128
HGRN_CHUNK = 64
ATTN_HEAD = 64
ATTN_BAND = 128
DILATIONS = (1, 4, 16)
N_SPLITS = 8
N_DEV = 8
ADAM_LR = 0.001
ADAM_B1 = 0.9
ADAM_B2 = 0.999
ADAM_EPS = 1e-08
ADAM_WD = 0.01
ADAM_STEP = 10
LANES = 128
MESH = pl.DeviceIdType.MESH
F32 = jnp.float32
BF16 = jnp.bfloat16
NEG_BIG = -1e30
VMEM_LIMIT = 56 * 1024 * 1024

ANY = pl.BlockSpec(memory_space=pl.ANY)


def _params(*sem):
    return pltpu.CompilerParams(dimension_semantics=sem, vmem_limit_bytes=VMEM_LIMIT)


def _tile(n, pref):
    t = min(n, pref)
    assert n % t == 0, (n, pref)
    return t


def _dot(a, b, precision=None):
    return jnp.dot(a, b, preferred_element_type=F32, precision=precision)


def _dot_nt(a, b):
    return lax.dot_general(a, b, (((1,), (1,)), ((), ())), preferred_element_type=F32)


def _dot_tn(a, b):
    return lax.dot_general(a, b, (((0,), (0,)), ((), ())), preferred_element_type=F32)


def _sigmoid(x):
    return 1.0 / (1.0 + jnp.exp(-x))


def _dsilu(x, s):
    return s * (1.0 + x * (1.0 - s))


def _adamw(w, g, m, v):
    m = ADAM_B1 * m + (1.0 - ADAM_B1) * g
    v = ADAM_B2 * v + (1.0 - ADAM_B2) * (g * g)
    m_hat = m / (1.0 - ADAM_B1 ** ADAM_STEP)
    v_hat = v / (1.0 - ADAM_B2 ** ADAM_STEP)
    delta = -ADAM_LR * (m_hat / (jnp.sqrt(v_hat) + ADAM_EPS) + ADAM_WD * w)
    return delta, m, v


def _cast_bf16(a):
    r, c = a.shape
    tr = _tile(r, 256)

    def body(a_ref, o_ref):
        o_ref[...] = a_ref[...].astype(BF16)

    return pl.pallas_call(
        body, name="cast_bf16", grid=(r // tr,), out_shape=jax.ShapeDtypeStruct((r, c), BF16),
        in_specs=[pl.BlockSpec((tr, c), lambda i: (i, 0))], out_specs=pl.BlockSpec((tr, c), lambda i: (i, 0)),
        compiler_params=_params("parallel"))(a)


def _rmsnorm_fwd(x, gain):
    s, d = x.shape
    tm = _tile(s, 512)

    def body(x_ref, g_ref, h_ref):
        xv = x_ref[...]
        r = lax.rsqrt(jnp.mean(xv * xv, axis=-1, keepdims=True) + NORM_EPS)
        h_ref[...] = (xv * r * g_ref[...]).astype(BF16)

    return pl.pallas_call(
        body, name="rmsnorm_fwd", grid=(s // tm,), out_shape=jax.ShapeDtypeStruct((s, d), BF16),
        in_specs=[pl.BlockSpec((tm, d), lambda i: (i, 0)), pl.BlockSpec((1, d), lambda i: (0, 0))],
        out_specs=pl.BlockSpec((tm, d), lambda i: (i, 0)), compiler_params=_params("parallel"))(x, gain)


HGRN_BLOCK = 1024
TRI_ROWS = 256


def _chunk_masks():
    tb = TRI_ROWS
    row = lax.broadcasted_iota(jnp.int32, (tb, tb), 0)
    col = lax.broadcasted_iota(jnp.int32, (tb, tb), 1)
    same = (row // HGRN_CHUNK) == (col // HGRN_CHUNK)
    lower = jnp.where(same & (col <= row), 1.0, 0.0).astype(BF16)
    upper = jnp.where(same & (col >= row), 1.0, 0.0).astype(BF16)
    return lower, upper


def _split3(a):
    hi = a.astype(BF16).astype(F32)
    mid = (a - hi).astype(BF16).astype(F32)
    lo = (a - hi - mid).astype(BF16).astype(F32)
    return hi, mid, lo


def _tri_dot(tri, x):
    hi, mid, lo = (p.astype(BF16) for p in _split3(x))
    outs = []
    for r in range(0, x.shape[0], TRI_ROWS):
        sl = slice(r, r + TRI_ROWS)
        outs.append(_dot(tri, hi[sl]) + _dot(tri, mid[sl]) + _dot(tri, lo[sl]))
    return outs[0] if len(outs) == 1 else jnp.concatenate(outs, axis=0)


def _hgrn_gates(qp, fp, lbv):
    lb = _sigmoid(lbv[0:1] - lbv[1:2])
    sq = _sigmoid(qp)
    q = qp * sq
    sg = _sigmoid(fp)
    f = lb + (1.0 - lb) * sg
    k = 1.0 - f
    return lb, sq, q, sg, f, k


def _hgrn_fwd(z, lb_logits, gnorm):
    s = z.shape[0]
    e = z.shape[1] // N_SPLITS
    nh = e // HGRN_HEAD
    tb = _tile(s, HGRN_BLOCK)
    nc = tb // HGRN_CHUNK
    nb = s // tb
    C = HGRN_CHUNK

    def body(q_ref, f_ref, i_ref, g_ref, lb_ref, gn_ref, y_ref, st_ref, state, o_scr):
        @pl.when(pl.program_id(1) == 0)
        def _():
            state[...] = jnp.zeros_like(state)

        lb, sq, q, sg, f, k = _hgrn_gates(q_ref[...], f_ref[...], lb_ref[...])
        lower, _ = _chunk_masks()
        b = _tri_dot(lower, jnp.log(f))
        b3 = b.reshape(nc, C, HGRN_HEAD)
        bc = b3[:, C - 1:C, :]
        qt = (q * jnp.exp(b)).astype(BF16)
        kt = (k * jnp.exp(-b)).astype(BF16)
        ke = (k.reshape(nc, C, HGRN_HEAD) * jnp.exp(bc - b3)).reshape(tb, HGRN_HEAD).astype(BF16)
        v = i_ref[...].astype(BF16)
        tri = lax.broadcasted_iota(jnp.int32, (C, C), 1) <= lax.broadcasted_iota(jnp.int32, (C, C), 0)
        sls = [slice(c * C, (c + 1) * C) for c in range(nc)]
        kv = [_dot_tn(v[sl], ke[sl]) for sl in sls]
        a = [jnp.where(tri, _dot_nt(qt[sl], kt[sl]), 0.0).astype(BF16) for sl in sls]
        st = state[...]
        sts = []
        for c in range(nc):
            sts.append(st)
            st_ref[c] = st
            st = st * jnp.exp(bc[c]) + kv[c]
        state[...] = st
        for c, sl in enumerate(sls):
            o_scr[sl, :] = _dot(a[c], v[sl]) + _dot_nt(qt[sl], sts[c].astype(BF16))
        o = o_scr[...]
        rms = lax.rsqrt(jnp.mean(o * o, axis=-1, keepdims=True) + NORM_EPS)
        gp = g_ref[...]
        y_ref[...] = (o * rms * gn_ref[...] * (gp * _sigmoid(gp))).astype(BF16)

    col = lambda kk: (lambda h, n: (n, kk * nh + h))
    return pl.pallas_call(
        body, name="hgrn_fwd", grid=(nh, nb),
        out_shape=(jax.ShapeDtypeStruct((s, e), BF16),
                   jax.ShapeDtypeStruct((nh, s // C, HGRN_HEAD, HGRN_HEAD), F32)),
        in_specs=[pl.BlockSpec((tb, HGRN_HEAD), col(0)), pl.BlockSpec((tb, HGRN_HEAD), col(1)),
                  pl.BlockSpec((tb, HGRN_HEAD), col(2)), pl.BlockSpec((tb, HGRN_HEAD), col(3)),
                  pl.BlockSpec((2, HGRN_HEAD), lambda h, n: (0, h)), pl.BlockSpec((1, HGRN_HEAD), lambda h, n: (0, 0))],
        out_specs=(pl.BlockSpec((tb, HGRN_HEAD), lambda h, n: (n, h)),
                   pl.BlockSpec((None, nc, HGRN_HEAD, HGRN_HEAD), lambda h, n: (h, n, 0, 0))),
        scratch_shapes=[pltpu.VMEM((HGRN_HEAD, HGRN_HEAD), F32), pltpu.VMEM((tb, HGRN_HEAD), F32)],
        compiler_params=_params("parallel", "arbitrary"))(z, z, z, z, lb_logits, gnorm)


def _hgrn_bwd(z, dy, states, lb_logits, gnorm):
    s = z.shape[0]
    e = z.shape[1] // N_SPLITS
    nh = e // HGRN_HEAD
    tb = _tile(s, HGRN_BLOCK)
    nc = tb // HGRN_CHUNK
    nb = s // tb
    C = HGRN_CHUNK
    H = HGRN_HEAD

    def body(q_ref, f_ref, i_ref, g_ref, dy_ref, st_ref, lb_ref, gn_ref, dz_ref, dlb_ref, dgn_ref,
             gstate, o_scr, dq_scr, dk_scr, dv_scr, e_scr):
        first = (pl.program_id(0) == 0) & (pl.program_id(1) == 0)

        @pl.when(first)
        def _():
            dgn_ref[...] = jnp.zeros_like(dgn_ref)

        @pl.when(pl.program_id(1) == 0)
        def _():
            gstate[...] = jnp.zeros_like(gstate)
            dlb_ref[...] = jnp.zeros_like(dlb_ref)

        qp = q_ref[...]
        lb, sq, q, sg, f, k = _hgrn_gates(qp, f_ref[...], lb_ref[...])
        lower, upper = _chunk_masks()
        b = _tri_dot(lower, jnp.log(f))
        b3 = b.reshape(nc, C, H)
        bc = b3[:, C - 1:C, :]
        eb = jnp.exp(b)
        enb = jnp.exp(-b)
        eend = jnp.exp(bc - b3).reshape(tb, H)
        qt = (q * eb).astype(BF16)
        kt = (k * enb).astype(BF16)
        ke = (k * eend).astype(BF16)
        v = i_ref[...].astype(BF16)
        tri = lax.broadcasted_iota(jnp.int32, (C, C), 1) <= lax.broadcasted_iota(jnp.int32, (C, C), 0)
        sls = [slice(c * C, (c + 1) * C) for c in range(nc)]
        a = [jnp.where(tri, _dot_nt(qt[sl], kt[sl]), 0.0).astype(BF16) for sl in sls]
        for c, sl in enumerate(sls):
            o_scr[sl, :] = _dot(a[c], v[sl]) + _dot_nt(qt[sl], st_ref[c].astype(BF16))
        o = o_scr[...]
        rms = lax.rsqrt(jnp.mean(o * o, axis=-1, keepdims=True) + NORM_EPS)
        on = o * rms
        gn = gn_ref[...]
        gp = g_ref[...]
        sgg = _sigmoid(gp)
        dyv = dy_ref[...]
        d_on = dyv * (gp * sgg)
        dz_ref[3] = (dyv * on * gn * _dsilu(gp, sgg)).astype(BF16)
        dgn_ref[...] += jnp.sum(d_on * on, axis=0, keepdims=True)
        u = d_on * gn
        do = (rms * (u - on * jnp.mean(u * on, axis=-1, keepdims=True))).astype(BF16)
        gup = [_dot_tn(do[sl], qt[sl]) for sl in sls]
        da = [jnp.where(tri, _dot_nt(do[sl], v[sl]), 0.0).astype(BF16) for sl in sls]
        gt = gstate[...]
        gts = [None] * nc
        for c in reversed(range(nc)):
            gts[c] = gt
            gt = gt * jnp.exp(bc[c]) + gup[c]
        gstate[...] = gt
        for c, sl in enumerate(sls):
            stp = st_ref[c]
            gtb = gts[c].astype(BF16)
            dqt = _dot(da[c], kt[sl]) + _dot(do[sl], stp.astype(BF16))
            dkt = _dot_tn(da[c], qt[sl])
            dks = _dot(v[sl], gtb) * eend[sl]
            dv_scr[sl, :] = _dot_tn(a[c], do[sl]) + _dot_nt(ke[sl], gtb)
            dq_scr[sl, :] = dqt * eb[sl]
            dk_scr[sl, :] = dkt * enb[sl] + dks
            ech = (jnp.sum(k[sl] * dks, axis=0, keepdims=True)
                   + jnp.sum(gts[c] * jnp.exp(bc[c]) * stp, axis=0, keepdims=True))
            e_scr[sl, :] = jnp.broadcast_to(ech, (C, H))
        dq = dq_scr[...]
        dk = dk_scr[...]
        dlf = _tri_dot(upper, q * dq - k * dk) + e_scr[...]
        dft = dlf / f - dk
        dz_ref[0] = (dq * _dsilu(qp, sq)).astype(BF16)
        dz_ref[1] = (dft * (1.0 - lb) * sg * (1.0 - sg)).astype(BF16)
        dz_ref[2] = dv_scr[...].astype(BF16)
        dlb_ref[...] += jnp.sum(dft * (1.0 - sg), axis=0, keepdims=True)

    col = lambda kk: (lambda h, n: (nb - 1 - n, kk * nh + h))
    return pl.pallas_call(
        body, name="hgrn_bwd", grid=(nh, nb),
        out_shape=(jax.ShapeDtypeStruct((4, s, e), BF16), jax.ShapeDtypeStruct((1, e), F32),
                   jax.ShapeDtypeStruct((1, H), F32)),
        in_specs=[pl.BlockSpec((tb, H), col(0)), pl.BlockSpec((tb, H), col(1)),
                  pl.BlockSpec((tb, H), col(2)), pl.BlockSpec((tb, H), col(3)),
                  pl.BlockSpec((tb, H), lambda h, n: (nb - 1 - n, h)),
                  pl.BlockSpec((None, nc, H, H), lambda h, n: (h, nb - 1 - n, 0, 0)),
                  pl.BlockSpec((2, H), lambda h, n: (0, h)), pl.BlockSpec((1, H), lambda h, n: (0, 0))],
        out_specs=(pl.BlockSpec((4, tb, H), lambda h, n: (0, nb - 1 - n, h)),
                   pl.BlockSpec((1, H), lambda h, n: (0, h)), pl.BlockSpec((1, H), lambda h, n: (0, 0))),
        scratch_shapes=[pltpu.VMEM((H, H), F32)] + [pltpu.VMEM((tb, H), F32)] * 5,
        compiler_params=_params("arbitrary", "arbitrary"))(z, z, z, z, dy, states, lb_logits, gnorm)


ATTN_T = 16 * ATTN_BAND
SCALE = ATTN_HEAD ** -0.5
TILE_UNROLL = 2


def _slope(hh, nheads):
    head = (2 * pl.program_id(0) + hh + 1).astype(F32)
    return jnp.exp(jnp.full((1, 1), -8.0 / nheads * math.log(2.0), F32) * head)


def _fill_bias(bias, nheads, delta, edge_ok):
    band = (delta >= 0) & (delta <= ATTN_BAND)
    dist = delta.astype(F32)
    for pi, dil in enumerate(DILATIONS):
        for hh in range(2):
            full = jnp.where(band, -(_slope(hh, nheads) * float(dil)) * dist, NEG_BIG)
            bias[(pi * 2 + hh) * 2] = full
            bias[(pi * 2 + hh) * 2 + 1] = jnp.where(edge_ok, full, NEG_BIG)


def _rows(start, size, stride):
    if stride == 1:
        return pl.ds(pl.multiple_of(start, ATTN_BAND), size)
    return pl.ds(start, size, stride=stride)


def _head_lanes(rows, hh):
    return (lax.broadcasted_iota(jnp.int32, (rows, LANES), 1) // ATTN_HEAD) == hh


def _attn_fwd(z):
    s = z.shape[0]
    e = z.shape[1] // N_SPLITS
    npair = e // LANES
    T = ATTN_T
    assert s % T == 0
    nsb = s // T
    W = ATTN_BAND
    nt = T // W
    HD = ATTN_HEAD
    chunk = 256

    def body(q_ref, kp_ref, kc_ref, vp_ref, vc_ref, g_ref, o_ref, l_ref, y_ref, qa, kbuf, va, bias, accs, ms, lsw):
        sb = pl.program_id(1)
        def stage(i, carry):
            rows = pl.ds(pl.multiple_of(i * chunk, chunk), chunk)
            upper = pl.ds(pl.multiple_of(T + i * chunk, chunk), chunk)
            kbuf[upper, :] = kc_ref[rows, :]
            for hh in range(2):
                mine = _head_lanes(chunk, hh)
                qa[hh, rows, :] = jnp.where(mine, q_ref[rows, :] * SCALE, 0.0)
                va[hh, upper, :] = jnp.where(mine, vc_ref[rows, :], 1.0)
            return carry

        lax.fori_loop(0, T // chunk, stage, 0)

        @pl.when(sb == 0)
        def _():
            def stage_prev(i, carry):
                rows = pl.ds(pl.multiple_of(i * chunk, chunk), chunk)
                kbuf[rows, :] = kp_ref[rows, :]
                for hh in range(2):
                    va[hh, rows, :] = jnp.where(_head_lanes(chunk, hh), vp_ref[rows, :], 1.0)
                return carry

            lax.fori_loop(0, T // chunk, stage_prev, 0)
        qi = lax.broadcasted_iota(jnp.int32, (W, 2 * W), 0)
        kj = lax.broadcasted_iota(jnp.int32, (W, 2 * W), 1)
        _fill_bias(bias, 2 * npair, W + qi - kj, kj >= W)

        def tile(tau, carry):
            first = _head_lanes(W, 0)
            rows, scores = [], []
            for pi, dil in enumerate(DILATIONS):
                r = tau % dil
                ub = tau // dil
                qrows = _rows(r + dil * W * ub, W, dil)
                krows = _rows(T + dil * W * (ub - 1) + r, 2 * W, dil)
                var = jnp.where((sb == 0) & (ub == 0), 1, 0)
                kt = kbuf[krows, :].astype(BF16)
                rows.append((qrows, krows))
                scores.append([_dot_nt(qa[hh, qrows, :].astype(BF16), kt) + bias[(pi * 2 + hh) * 2 + var]
                               for hh in range(2)])
            maxes = [[jnp.max(sc, axis=-1, keepdims=True) for sc in pair] for pair in scores]
            probs = [[jnp.exp(sc - m).astype(BF16) for sc, m in zip(ps, pm)] for ps, pm in zip(scores, maxes)]
            for pi, (qrows, krows) in enumerate(rows):
                outs = [_dot(probs[pi][hh], va[hh, krows, :].astype(BF16)) for hh in range(2)]
                accs[pi, qrows, :] = jnp.where(first, outs[0], outs[1])
                lsw[pi, qrows, :] = jnp.where(first, outs[1], outs[0])
                ms[pi, qrows, :] = jnp.where(first, maxes[pi][0], maxes[pi][1])
            return carry

        lax.fori_loop(0, nt, tile, 0, unroll=TILE_UNROLL)

        def merge(i, carry):
            rows = pl.ds(pl.multiple_of(i * chunk, chunk), chunk)
            m1, m2, m3 = ms[0, rows, :], ms[1, rows, :], ms[2, rows, :]
            mx = jnp.maximum(jnp.maximum(m1, m2), m3)
            w1, w2, w3 = jnp.exp(m1 - mx), jnp.exp(m2 - mx), jnp.exp(m3 - mx)
            unswap = lambda a: pltpu.roll(a, ATTN_HEAD, 1)
            den = w1 * unswap(lsw[0, rows, :]) + w2 * unswap(lsw[1, rows, :]) + w3 * unswap(lsw[2, rows, :])
            o = (w1 * accs[0, rows, :] + w2 * accs[1, rows, :] + w3 * accs[2, rows, :]) / den
            o_ref[rows, :] = o
            l_ref[rows, :] = mx + jnp.log(den)
            gp = g_ref[rows, :]
            y_ref[rows, :] = (o * (gp * _sigmoid(gp))).astype(BF16)
            upper = pl.ds(pl.multiple_of(T + i * chunk, chunk), chunk)
            kbuf[rows, :] = kbuf[upper, :]
            for hh in range(2):
                va[hh, rows, :] = va[hh, upper, :]
            return carry

        lax.fori_loop(0, T // chunk, merge, 0)

    cur = lambda split: (lambda hp, sb: (sb, split * npair + hp))
    prev = lambda split: (lambda hp, sb: (jnp.maximum(sb - 1, 0), split * npair + hp))
    blk = lambda index: pl.BlockSpec((T, LANES), index)
    out = blk(lambda hp, sb: (sb, hp))
    buf = lambda rows: pltpu.VMEM((rows, LANES), F32)
    return pl.pallas_call(
        body, name="attn_fwd", grid=(npair, nsb),
        out_shape=(jax.ShapeDtypeStruct((s, e), F32), jax.ShapeDtypeStruct((s, e), F32), jax.ShapeDtypeStruct((s, e), BF16)),
        in_specs=[blk(cur(4)), blk(prev(5)), blk(cur(5)), blk(prev(6)), blk(cur(6)), blk(cur(7))],
        out_specs=(out, out, out),
        scratch_shapes=[pltpu.VMEM((2, T, LANES), F32), buf(2 * T), pltpu.VMEM((2, 2 * T, LANES), F32),
                        pltpu.VMEM((12, W, 2 * W), F32)] + [pltpu.VMEM((3, T, LANES), F32)] * 3,
        compiler_params=_params("parallel", "arbitrary"))(z, z, z, z, z, z)


def _outproj_loss(x, y_h, y_a, w_out_full, final_gain, target):
    s, d = x.shape
    e = y_h.shape[1]
    tm = _tile(s, 256)

    def body(x_ref, yh_ref, ya_ref, w_ref, g_ref, t_ref, dx_ref, dxb_ref, dy_ref, loss_ref, dg_ref):
        @pl.when(pl.program_id(0) == 0)
        def _():
            loss_ref[...] = jnp.zeros_like(loss_ref)
            dg_ref[...] = jnp.zeros_like(dg_ref)

        w = w_ref[...]
        x2 = x_ref[...] + _dot(yh_ref[...], w[0:e]) + _dot(ya_ref[...], w[e:2 * e])
        r = lax.rsqrt(jnp.mean(x2 * x2, axis=-1, keepdims=True) + NORM_EPS)
        xn = x2 * r
        g = g_ref[...]
        err = xn * g - t_ref[...]
        loss_ref[...] += jnp.sum(err * err, axis=0, keepdims=True) * (0.5 / d)
        dyo = err * (1.0 / d)
        dg_ref[...] += jnp.sum(dyo * xn, axis=0, keepdims=True)
        u = dyo * g
        dx2 = r * (u - xn * jnp.mean(u * xn, axis=-1, keepdims=True))
        dx_ref[...] = dx2
        dxb = dx2.astype(BF16)
        dxb_ref[...] = dxb
        dy_ref[...] = _dot_nt(dxb, w)

    row = pl.BlockSpec((tm, d), lambda i: (i, 0))
    half = pl.BlockSpec((tm, e), lambda i: (i, 0))
    vec = pl.BlockSpec((1, d), lambda i: (0, 0))
    return pl.pallas_call(
        body, name="outproj_loss", grid=(s // tm,),
        out_shape=(jax.ShapeDtypeStruct((s, d), F32), jax.ShapeDtypeStruct((s, d), BF16),
                   jax.ShapeDtypeStruct((s, 2 * e), F32), jax.ShapeDtypeStruct((1, d), F32),
                   jax.ShapeDtypeStruct((1, d), F32)),
        in_specs=[row, half, half, pl.BlockSpec((2 * e, d), lambda i: (0, 0)), vec, row],
        out_specs=(row, row, pl.BlockSpec((tm, 2 * e), lambda i: (i, 0)), vec, vec),
        compiler_params=_params("arbitrary"))(x, y_h, y_a, w_out_full, final_gain, target)


def _attn_bwd(z, dy, o, lse):
    s, e = o.shape
    npair = e // LANES
    T = ATTN_T
    assert s % T == 0
    nsb = s // T
    W = ATTN_BAND
    nt = T // W
    HD = ATTN_HEAD
    chunk = 256

    def body(k_ref, v_ref, qc_ref, qn_ref, dyc_ref, dyn_ref, gc_ref, gn_ref, oc_ref, on_ref, lc_ref, ln_ref,
             dz_ref, qa, doa, ka, va, dqacc, dkacc, dvacc, bias):
        sb = pl.program_id(1)
        def stage_queries(half, q_r, dy_r, g_r, o_r, l_r):
            def stage(i, carry):
                rows = pl.ds(pl.multiple_of(i * chunk, chunk), chunk)
                dst = pl.ds(pl.multiple_of(half * T + i * chunk, chunk), chunk)
                lane = lax.broadcasted_iota(jnp.int32, (chunk, LANES), 1)
                gp = g_r[rows, :]
                dov = dy_r[rows, :] * (gp * _sigmoid(gp))
                qv = q_r[rows, :] * SCALE
                same_head = (lax.broadcasted_iota(jnp.int32, (LANES, LANES), 0) // HD
                             == lax.broadcasted_iota(jnp.int32, (LANES, LANES), 1) // HD)
                ones = jnp.where(same_head, 1.0, 0.0).astype(BF16)
                hi, mid, lo = (p.astype(BF16) for p in _split3(dov * o_r[rows, :]))
                delta = _dot(hi, ones) + _dot(mid, ones) + _dot(lo, ones)
                swap = lambda a: pltpu.roll(a, HD, 1)
                lse_parts = [swap(p) for p in _split3(l_r[rows, :])]
                dl_parts = [swap(p) for p in _split3(delta)]
                for hh in range(2):
                    mine = _head_lanes(chunk, hh)
                    spare = (1 - hh) * HD
                    qh = jnp.where(mine, qv, 0.0)
                    dh = jnp.where(mine, dov, 0.0)
                    for j in range(3):
                        qh = jnp.where(lane == spare + j, lse_parts[j], qh)
                        dh = jnp.where(lane == spare + j, dl_parts[j], dh)
                    qa[hh, dst, :] = qh
                    doa[hh, dst, :] = dh
                return carry

            lax.fori_loop(0, T // chunk, stage, 0)

        @pl.when(sb == 0)
        def _():
            stage_queries(0, qc_ref, dyc_ref, gc_ref, oc_ref, lc_ref)

        stage_queries(1, qn_ref, dyn_ref, gn_ref, on_ref, ln_ref)

        def stage_keys(i, carry):
            rows = pl.ds(pl.multiple_of(i * chunk, chunk), chunk)
            lane = lax.broadcasted_iota(jnp.int32, (chunk, LANES), 1)
            for hh in range(2):
                spare = (1 - hh) * HD
                minus = (lane >= spare) & (lane < spare + 3)
                ka[hh, rows, :] = jnp.where(minus, -1.0, k_ref[rows, :])
                va[hh, rows, :] = jnp.where(minus, -1.0, v_ref[rows, :])
            gp = gc_ref[rows, :]
            dz_ref[3, rows, :] = (dyc_ref[rows, :] * oc_ref[rows, :] * _dsilu(gp, _sigmoid(gp))).astype(BF16)
            return carry

        lax.fori_loop(0, T // chunk, stage_keys, 0)

        @pl.when(sb == 0)
        def _():
            dqacc[0:T, :] = jnp.zeros((T, LANES), F32)

        dqacc[T:, :] = jnp.zeros((T, LANES), F32)
        dkacc[...] = jnp.zeros_like(dkacc)
        dvacc[...] = jnp.zeros_like(dvacc)
        qi = lax.broadcasted_iota(jnp.int32, (2 * W, W), 0)
        kj = lax.broadcasted_iota(jnp.int32, (2 * W, W), 1)
        _fill_bias(bias, 2 * npair, qi - kj, qi < W)

        def tile(tau, carry):
            def scores(step, pi):
                dil = DILATIONS[pi]
                r = step % dil
                ub = step // dil
                start = r + dil * W * ub
                krows = _rows(start, W, dil)
                qrows = _rows(start, 2 * W, dil)
                var = jnp.where((sb == nsb - 1) & (ub == nt // dil - 1), 1, 0)
                unit = dict(krows=krows, qrows=qrows, ops=[], sc=[], dpd=[])
                for hh in range(2):
                    kt = ka[hh, krows, :].astype(BF16)
                    vt = va[hh, krows, :].astype(BF16)
                    qt = qa[hh, qrows, :].astype(BF16)
                    dt = doa[hh, qrows, :].astype(BF16)
                    unit["ops"].append((kt, qt, dt))
                    unit["sc"].append(_dot_nt(qt, kt) + bias[(pi * 2 + hh) * 2 + var])
                    unit["dpd"].append(_dot_nt(dt, vt))
                return unit

            def elementwise(unit):
                ps = [jnp.exp(s_) for s_ in unit["sc"]]
                unit["ds"] = [(p * d).astype(BF16) for p, d in zip(ps, unit["dpd"])]
                unit["pb"] = [p.astype(BF16) for p in ps]

            def products(unit):
                dvs = [_dot_tn(pb, dt) for pb, (kt, qt, dt) in zip(unit["pb"], unit["ops"])]
                dks = [_dot_tn(ds, qt) for ds, (kt, qt, dt) in zip(unit["ds"], unit["ops"])]
                dqs = [_dot(ds, kt) for ds, (kt, qt, dt) in zip(unit["ds"], unit["ops"])]
                dkacc[unit["krows"], :] += jnp.where(_head_lanes(W, 0), dks[0], dks[1])
                dvacc[unit["krows"], :] += jnp.where(_head_lanes(W, 0), dvs[0], dvs[1])
                dqacc[unit["qrows"], :] += jnp.where(_head_lanes(2 * W, 0), dqs[0], dqs[1]) * SCALE

            order = [(2 * tau + half, pi) for half in range(2) for pi in range(len(DILATIONS))]
            units = [None] * len(order)
            for n in range(len(order) + 2):
                if n < len(order):
                    units[n] = scores(*order[n])
                if 1 <= n <= len(order):
                    elementwise(units[n - 1])
                if n >= 2:
                    products(units[n - 2])
            return carry

        lax.fori_loop(0, nt // 2, tile, 0)

        def flush(i, carry):
            rows = pl.ds(pl.multiple_of(i * chunk, chunk), chunk)
            nxt = pl.ds(pl.multiple_of(T + i * chunk, chunk), chunk)
            dz_ref[0, rows, :] = dqacc[rows, :].astype(BF16)
            dz_ref[1, rows, :] = dkacc[rows, :].astype(BF16)
            dz_ref[2, rows, :] = dvacc[rows, :].astype(BF16)
            dqacc[rows, :] = dqacc[nxt, :]
            for hh in range(2):
                qa[hh, rows, :] = qa[hh, nxt, :]
                doa[hh, rows, :] = doa[hh, nxt, :]
            return carry

        lax.fori_loop(0, T // chunk, flush, 0)

    zc = lambda split: (lambda hp, sb: (sb, split * npair + hp))
    zn = lambda split: (lambda hp, sb: (jnp.minimum(sb + 1, nsb - 1), split * npair + hp))
    ec = lambda off: (lambda hp, sb: (sb, off + hp))
    en = lambda off: (lambda hp, sb: (jnp.minimum(sb + 1, nsb - 1), off + hp))
    blk = lambda index: pl.BlockSpec((T, LANES), index)
    buf = lambda rows: pltpu.VMEM((rows, LANES), F32)
    return pl.pallas_call(
        body, name="attn_bwd", grid=(npair, nsb), out_shape=jax.ShapeDtypeStruct((4, s, e), BF16),
        in_specs=[blk(zc(5)), blk(zc(6)), blk(zc(4)), blk(zn(4)), blk(ec(npair)), blk(en(npair)),
                  blk(zc(7)), blk(zn(7)), blk(ec(0)), blk(en(0)), blk(ec(0)), blk(en(0))],
        out_specs=pl.BlockSpec((4, T, LANES), lambda hp, sb: (0, sb, hp)),
        scratch_shapes=[pltpu.VMEM((2, 2 * T, LANES), F32), pltpu.VMEM((2, 2 * T, LANES), F32),
                        pltpu.VMEM((2, T, LANES), F32), pltpu.VMEM((2, T, LANES), F32),
                        buf(2 * T), buf(T), buf(T), pltpu.VMEM((12, 2 * W, W), F32)],
        compiler_params=_params("parallel", "arbitrary"))(z, z, z, z, dy, dy, z, z, o, o, lse, lse)


def _dz_specs(tm, e, axis):
    def mk(lo, hi):
        def index(i, k):
            row, grp = (i, k) if axis == 1 else (k, i)
            return (jnp.clip(grp - lo, 0, hi - lo - 1), row, 0)
        return pl.BlockSpec((None, tm, e), index)
    return [mk(0, 4), mk(4, 8)]


def _dz_pick(grp, dzh_ref, dza_ref, fn):
    @pl.when(grp < 4)
    def _():
        fn(dzh_ref[...])

    @pl.when(grp >= 4)
    def _():
        fn(dza_ref[...])


def _dh_dx(dzh, dza, w_full, x, gain, dx2):
    s, d = x.shape
    e = dzh.shape[2]
    tm = _tile(s, 512)

    def body(dzh_ref, dza_ref, w_ref, x_ref, g_ref, dx2_ref, gx_ref, dg_ref, acc):
        i, k = pl.program_id(0), pl.program_id(1)

        @pl.when((i == 0) & (k == 0))
        def _():
            dg_ref[...] = jnp.zeros_like(dg_ref)

        @pl.when(k == 0)
        def _():
            acc[...] = jnp.zeros_like(acc)

        def add(dz):
            acc[...] += _dot_nt(dz, w_ref[...])

        _dz_pick(k, dzh_ref, dza_ref, add)

        @pl.when(k == N_SPLITS - 1)
        def _():
            dh = acc[...]
            xv = x_ref[...]
            r = lax.rsqrt(jnp.mean(xv * xv, axis=-1, keepdims=True) + NORM_EPS)
            xn = xv * r
            dg_ref[...] += jnp.sum(dh * xn, axis=0, keepdims=True)
            u = dh * g_ref[...]
            gx_ref[...] = dx2_ref[...] + r * (u - xn * jnp.mean(u * xn, axis=-1, keepdims=True))

    row = pl.BlockSpec((tm, d), lambda i, k: (i, 0))
    vec = pl.BlockSpec((1, d), lambda i, k: (0, 0))
    return pl.pallas_call(
        body, name="dh_dx", grid=(s // tm, N_SPLITS),
        out_shape=(jax.ShapeDtypeStruct((s, d), F32), jax.ShapeDtypeStruct((1, d), F32)),
        in_specs=_dz_specs(tm, e, 1) + [pl.BlockSpec((None, d, e), lambda i, k: (k, 0, 0)), row, vec, row],
        out_specs=(row, vec), scratch_shapes=[pltpu.VMEM((tm, d), F32)],
        compiler_params=_params("arbitrary", "arbitrary"))(dzh, dza, w_full, x, gain, dx2)


def _position():
    x, y, c = lax.axis_index("x"), lax.axis_index("y"), lax.axis_index("c")
    return x, y, c


def _xor_peer(x, y, c, mask):
    return (x ^ ((mask >> 2) & 1), y ^ ((mask >> 1) & 1), c ^ (mask & 1))


def _block_order(masks):
    me = 4 * lax.axis_index("x") + 2 * lax.axis_index("y") + lax.axis_index("c")
    return jnp.stack([me ^ m for m in masks]).astype(jnp.int32)


GATHER_MASKS = (0, 1, 4, 2, 6, 5, 3, 7)


def _inproj_gather(h, w_loc, wo_loc):
    s, d = h.shape
    e = w_loc.shape[1]
    tm = _tile(s, 1024)
    ni = s // tm
    pre = max(ni - 2, 0)

    def body(order_ref, h_ref, w_ref, wo_ref, z_ref, wf_ref, wof_ref, wbuf, send_sems, recv_sems, osend, orecv,
             local_sems, wsems):
        j, i = pl.program_id(0), pl.program_id(1)
        x, y, c = _position()
        me, sibling = (x, y, c), (x, y, 1 - c)
        chips = [(1 - x, y), (x, 1 - y), (1 - x, 1 - y)]
        blk = lambda p: 4 * p[0] + 2 * p[1] + p[2]

        def copy(k, block, to, src=None):
            dst = wf_ref.at[blk(block)]
            return pltpu.make_async_remote_copy(
                src_ref=dst if src is None else src, dst_ref=dst, send_sem=send_sems.at[k], recv_sem=recv_sems.at[k],
                device_id=to, device_id_type=MESH)

        first = [copy(0, me, sibling, src=w_ref)] + [copy(1 + q, me, (*chip, c), src=w_ref) for q, chip in enumerate(chips)]
        passed = [copy(4 + q, (*chip, c), sibling) for q, chip in enumerate(chips)]
        mine = pltpu.make_async_copy(w_ref, wf_ref.at[blk(me)], local_sems.at[0])
        ocopies = [pltpu.make_async_remote_copy(
            src_ref=wo_ref, dst_ref=wof_ref.at[blk(me)], send_sem=osend.at[k], recv_sem=orecv.at[k],
            device_id=_xor_peer(x, y, c, k + 1), device_id_type=MESH) for k in range(N_DEV - 1)]
        omine = pltpu.make_async_copy(wo_ref, wof_ref.at[blk(me)], local_sems.at[1])
        blocks = [me, sibling] + [(*chip, c) for chip in chips] + [(*chip, 1 - c) for chip in chips]
        arrive = [None, copy(0, sibling, me)] + [copy(1 + q, (*chip, c), me) for q, chip in enumerate(chips)] \
            + [copy(4 + q, (*chip, 1 - c), me) for q, chip in enumerate(chips)]
        forward = [None, None] + passed + [None, None, None]

        def load(slot, src):
            return pltpu.make_async_copy(src, wbuf.at[slot], wsems.at[slot])

        @pl.when((j == 0) & (i == 0))
        def _():
            for cp in [mine, omine] + first + ocopies:
                cp.start()
            load(0, w_ref).start()

        for jj in range(N_DEV):
            @pl.when((j == jj) & (i == 0))
            def _():
                load(jj % 2, w_ref).wait()

            if jj + 1 < N_DEV:
                @pl.when((j == jj) & (i == pre))
                def _():
                    arrive[jj + 1].wait_recv()
                    if forward[jj + 1] is not None:
                        forward[jj + 1].start()
                    load((jj + 1) % 2, wf_ref.at[blk(blocks[jj + 1])]).start()

        z_ref[...] = _dot(h_ref[...], wbuf[j % 2])

        @pl.when((j == N_DEV - 1) & (i == ni - 1))
        def _():
            for cp in first + passed:
                cp.wait_send()
            for cp in ocopies:
                cp.wait_send()
                cp.wait_recv()
            mine.wait()
            omine.wait()

    grid_spec = pltpu.PrefetchScalarGridSpec(
        num_scalar_prefetch=1, grid=(N_DEV, ni),
        in_specs=[pl.BlockSpec((tm, d), lambda j, i, o: (i, 0)), ANY, ANY],
        out_specs=(pl.BlockSpec((tm, e), lambda j, i, o: (i, o[j])), ANY, ANY),
        scratch_shapes=[pltpu.VMEM((2, d, e), BF16), pltpu.SemaphoreType.DMA((7,)), pltpu.SemaphoreType.DMA((7,)),
                        pltpu.SemaphoreType.DMA((7,)), pltpu.SemaphoreType.DMA((7,)), pltpu.SemaphoreType.DMA((2,)),
                        pltpu.SemaphoreType.DMA((2,))])
    return pl.pallas_call(
        body, name="inproj_gather", grid_spec=grid_spec,
        out_shape=(jax.ShapeDtypeStruct((s, N_SPLITS * e), F32), jax.ShapeDtypeStruct((N_DEV, d, e), BF16),
                   jax.ShapeDtypeStruct((N_DEV,) + wo_loc.shape, BF16)),
        compiler_params=_params("arbitrary", "arbitrary"))(_block_order(GATHER_MASKS), h, w_loc, wo_loc)


SCATTER_MASKS = (7, 6, 5, 4, 3, 2, 1, 0)
N_CHIPS = 4


def _scatter_block(k, acc, stage, tmp, own_ref, ra_ref, rb_ref, sa_send, sa_recv, sb_send, sb_recv, loc_sem, last):
    x, y, c = _position()
    chip_of = lambda t: _xor_peer(x, y, c, SCATTER_MASKS[2 * t + 1])

    def ship(t):
        return pltpu.make_async_remote_copy(
            src_ref=stage.at[0], dst_ref=ra_ref.at[t], send_sem=sa_send.at[t], recv_sem=sa_recv.at[t],
            device_id=(x, y, 1 - c), device_id_type=MESH)

    def send(t):
        return pltpu.make_async_remote_copy(
            src_ref=stage.at[1], dst_ref=rb_ref.at[t], send_sem=sb_send.at[t], recv_sem=sb_recv.at[t],
            device_id=chip_of(t), device_id_type=MESH)

    for kk in range(N_DEV):
        t = kk // 2

        @pl.when(last & (k == kk))
        def _():
            if kk % 2 == 0:
                if t >= 1:
                    ship(t - 1).wait_send()
                stage[0] = acc[...].astype(BF16)
                ship(t).start()
            else:
                ship(t).wait_recv()
                fetch = pltpu.make_async_copy(ra_ref.at[t], tmp, loc_sem)
                fetch.start()
                fetch.wait()
                acc[...] += tmp[...].astype(F32)
                if t < N_CHIPS - 1:
                    if t >= 1:
                        send(t - 1).wait_send()
                    stage[1] = acc[...].astype(BF16)
                    send(t).start()
                else:
                    keep = pltpu.make_async_copy(acc, own_ref, loc_sem)
                    keep.start()
                    keep.wait()
                    ship(t).wait_send()
                    send(t - 1).wait_send()
                    for q in range(N_CHIPS - 1):
                        send(q).wait_recv()


def _scatter_scratch(rows, cols):
    return [pltpu.VMEM((rows, cols), F32), pltpu.VMEM((2, rows, cols), BF16), pltpu.VMEM((rows, cols), BF16),
            pltpu.SemaphoreType.DMA((N_CHIPS,)), pltpu.SemaphoreType.DMA((N_CHIPS,)),
            pltpu.SemaphoreType.DMA((N_CHIPS - 1,)), pltpu.SemaphoreType.DMA((N_CHIPS - 1,)), pltpu.SemaphoreType.DMA(())]


def _scatter_out(rows, cols):
    return (jax.ShapeDtypeStruct((rows, cols), F32), jax.ShapeDtypeStruct((N_CHIPS, rows, cols), BF16),
            jax.ShapeDtypeStruct((N_CHIPS - 1, rows, cols), BF16))


def _dwin_scatter(h, dzh, dza):
    s, d = h.shape
    e = dzh.shape[2]
    ts = _tile(s, 1024)
    ns = s // ts

    def body(order_ref, dzh_ref, dza_ref, h_ref, own_ref, ra_ref, rb_ref, acc, stage, tmp, *sems):
        k, step = pl.program_id(0), pl.program_id(1)

        @pl.when(step == 0)
        def _():
            acc[...] = jnp.zeros_like(acc)

        def add(dz):
            acc[...] += _dot_tn(h_ref[...], dz)

        _dz_pick(order_ref[k], dzh_ref, dza_ref, add)
        _scatter_block(k, acc, stage, tmp, own_ref, ra_ref, rb_ref, *sems, step == ns - 1)

    def dz_spec(lo):
        return pl.BlockSpec((None, ts, e), lambda k, st, o: (jnp.clip(o[k] - lo, 0, 3), st, 0))

    grid_spec = pltpu.PrefetchScalarGridSpec(
        num_scalar_prefetch=1, grid=(N_DEV, ns),
        in_specs=[dz_spec(0), dz_spec(4), pl.BlockSpec((ts, d), lambda k, st, o: (st, 0))],
        out_specs=(ANY, ANY, ANY), scratch_shapes=_scatter_scratch(d, e))
    own, _, rb = pl.pallas_call(
        body, name="dwin_scatter", grid_spec=grid_spec, out_shape=_scatter_out(d, e),
        compiler_params=_params("arbitrary", "arbitrary"))(_block_order(SCATTER_MASKS), dzh, dza, h)
    return own, rb


def _dwout_scatter(y_h, y_a, dxb):
    s, e = y_h.shape
    d = dxb.shape[1]
    r = 2 * e // N_DEV
    pairs = e // (2 * r)
    ts = _tile(s, 1024)
    ns = s // ts
    chip_masks = SCATTER_MASKS[1::2]

    def body(pair_ref, yh_ref, ya_ref, dx_ref, own_ref, ra_ref, rb_ref, acc, pend, stage, tmp,
             sa_send, sa_recv, sb_send, sb_recv, loc_sem):
        t, step = pl.program_id(0), pl.program_id(1)
        x, y, c = _position()

        @pl.when(step == 0)
        def _():
            acc[...] = jnp.zeros_like(acc)

        @pl.when(pair_ref[t] < pairs)
        def _():
            acc[...] += _dot_tn(yh_ref[...], dx_ref[...])

        @pl.when(pair_ref[t] >= pairs)
        def _():
            acc[...] += _dot_tn(ya_ref[...], dx_ref[...])

        mine = pl.ds(pl.multiple_of(c * r, r), r)
        other = pl.ds(pl.multiple_of((1 - c) * r, r), r)

        def ship(q):
            return pltpu.make_async_remote_copy(
                src_ref=stage.at[0], dst_ref=ra_ref.at[q], send_sem=sa_send.at[q], recv_sem=sa_recv.at[q],
                device_id=(x, y, 1 - c), device_id_type=MESH)

        def send(q):
            return pltpu.make_async_remote_copy(
                src_ref=stage.at[1], dst_ref=rb_ref.at[q], send_sem=sb_send.at[q], recv_sem=sb_recv.at[q],
                device_id=_xor_peer(x, y, c, chip_masks[q]), device_id_type=MESH)

        def sibling_share(q):
            ship(q).wait_recv()
            fetch = pltpu.make_async_copy(ra_ref.at[q], tmp, loc_sem)
            fetch.start()
            fetch.wait()
            return tmp[...].astype(F32)

        for tt in range(N_CHIPS):
            @pl.when((step == ns - 1) & (t == tt))
            def _():
                if tt >= 1:
                    ship(tt - 1).wait_send()
                stage[0] = acc[other, :].astype(BF16)
                ship(tt).start()
                if tt >= 1:
                    if tt >= 2:
                        send(tt - 2).wait_send()
                    stage[1] = (pend[...] + sibling_share(tt - 1)).astype(BF16)
                    send(tt - 1).start()
                if tt < N_CHIPS - 1:
                    pend[...] = acc[mine, :]
                else:
                    pend[...] = acc[mine, :] + sibling_share(tt)
                    keep = pltpu.make_async_copy(pend, own_ref, loc_sem)
                    keep.start()
                    keep.wait()
                    ship(tt).wait_send()
                    send(tt - 1).wait_send()
                    for q in range(N_CHIPS - 1):
                        send(q).wait_recv()

    def y_spec(lo):
        return pl.BlockSpec((ts, 2 * r), lambda t, st, o: (st, jnp.clip(o[t] - lo, 0, pairs - 1)))

    grid_spec = pltpu.PrefetchScalarGridSpec(
        num_scalar_prefetch=1, grid=(N_CHIPS, ns),
        in_specs=[y_spec(0), y_spec(pairs), pl.BlockSpec((ts, d), lambda t, st, o: (st, 0))],
        out_specs=(ANY, ANY, ANY),
        scratch_shapes=[pltpu.VMEM((2 * r, d), F32), pltpu.VMEM((r, d), F32)] + _scatter_scratch(r, d)[1:])
    own, _, rb = pl.pallas_call(
        body, name="dwout_scatter", grid_spec=grid_spec, out_shape=_scatter_out(r, d),
        compiler_params=_params("arbitrary", "arbitrary"))(_block_order(chip_masks) // 2, y_h, y_a, dxb)
    return own, rb


def _sum_chips_adamw(own, recv, w, m, v):
    r, c = w.shape
    tr = _tile(r, 128)

    def body(own_ref, rc_ref, w_ref, m_ref, v_ref, g_ref, d_ref, mo_ref, vo_ref):
        g = own_ref[...]
        for q in range(N_CHIPS - 1):
            g = g + rc_ref[q].astype(F32)
        g_ref[...] = g
        d_ref[...], mo_ref[...], vo_ref[...] = _adamw(w_ref[...], g, m_ref[...], v_ref[...])

    blk = pl.BlockSpec((tr, c), lambda i: (i, 0))
    shp = jax.ShapeDtypeStruct((r, c), F32)
    return pl.pallas_call(
        body, name="sum_chips_adamw", grid=(r // tr,), out_shape=(shp, shp, shp, shp),
        in_specs=[blk, pl.BlockSpec((N_CHIPS - 1, tr, c), lambda i: (0, i, 0)), blk, blk, blk],
        out_specs=(blk, blk, blk, blk), compiler_params=_params("parallel"))(own, recv, w, m, v)


SMALL_ROWS = 8
ROW_LB = 4
ROW_GN = 6
ROW_LOSS = 7


def _small_allreduce_adamw(part, w, m, v, lb_logits):
    width = part.shape[1]

    def body(p_ref, w_ref, m_ref, v_ref, lb_ref, g_ref, d_ref, mo_ref, vo_ref, buf, send_sems, recv_sems):
        x, y, c = _position()
        me = 4 * x + 2 * y + c
        buf[me] = p_ref[...]
        copies = []
        for k in range(N_DEV - 1):
            bx, by, bc = ((k + 1) >> 2) & 1, ((k + 1) >> 1) & 1, (k + 1) & 1
            peer = (x ^ bx, y ^ by, c ^ bc)
            copies.append(pltpu.make_async_remote_copy(
                src_ref=p_ref, dst_ref=buf.at[me], send_sem=send_sems.at[k], recv_sem=recv_sems.at[k],
                device_id=peer, device_id_type=MESH))
        for cp in copies:
            cp.start()
        for cp in copies:
            cp.wait_recv()
        for cp in copies:
            cp.wait_send()
        tot = buf[0]
        for dev in range(1, N_DEV):
            tot = tot + buf[dev]
        lbv = lb_ref[...]
        lb = _sigmoid(lbv[0:1] - lbv[1:2])
        glb = tot[ROW_LB:ROW_LB + 1] * lb * (1.0 - lb)
        loss = jnp.sum(tot[ROW_LOSS:ROW_LOSS + 1], axis=-1, keepdims=True)
        row = lax.broadcasted_iota(jnp.int32, (SMALL_ROWS, width), 0)
        g = jnp.where(row == ROW_LB, glb, jnp.where(row == ROW_LB + 1, -glb, tot))
        g = jnp.where(row == ROW_LOSS, loss, g)
        g_ref[...] = g
        d_ref[...], mo_ref[...], vo_ref[...] = _adamw(w_ref[...], g, m_ref[...], v_ref[...])

    vm = pl.BlockSpec(memory_space=pltpu.VMEM)
    shp = jax.ShapeDtypeStruct((SMALL_ROWS, width), F32)
    return pl.pallas_call(
        body, name="small_allreduce_adamw", out_shape=(shp, shp, shp, shp),
        in_specs=[vm] * 5, out_specs=(vm, vm, vm, vm),
        scratch_shapes=[pltpu.VMEM((N_DEV, SMALL_ROWS, width), F32), pltpu.SemaphoreType.DMA((N_DEV - 1,)),
                        pltpu.SemaphoreType.DMA((N_DEV - 1,))],
    )(part, w, m, v, lb_logits)


def _pack_small(norm_gain, final_gain, lb2, gnorm, last_row, width):
    pad = lambda a: jnp.pad(a.reshape(1, -1), ((0, 0), (0, width - a.size)))
    return jnp.concatenate([norm_gain.reshape(2, width), final_gain.reshape(2, width), lb2.reshape(2, width),
                            pad(gnorm), last_row.reshape(1, width)], axis=0)


def _unpack_small(p, d, e, hd):
    return (p[0:2].reshape(1, d), p[2:4].reshape(d), p[4:6].reshape(2, e), p[6:7, :hd].reshape(1, hd))


def kernel(x, norm_gain, w_in, lb_logits, hgrn_gnorm, w_out, final_gain, loss_target, m_norm_gain, m_w_in, m_lb_logits, m_hgrn_gnorm, m_w_out, m_final_gain, v_norm_gain, v_w_in, v_lb_logits, v_hgrn_gnorm, v_w_out, v_final_gain):
    s, d = x.shape[1], x.shape[2]
    e = w_in.shape[2]
    assert d == 2 * e and lb_logits.shape == (2, e) and w_out.shape[1] * N_DEV == 2 * e
    x2d = x.reshape(s, d)
    tgt = loss_target.reshape(s, d)

    h = _rmsnorm_fwd(x2d, norm_gain)
    z, w_in_full, w_out_full = _inproj_gather(h, _cast_bf16(w_in[0]), _cast_bf16(w_out[0]))
    w_out_full = w_out_full.reshape(2 * e, d)
    y_h, states = _hgrn_fwd(z, lb_logits, hgrn_gnorm)
    o_attn, lse, y_a = _attn_fwd(z)
    dx2, dx2b, dy, loss_vec, dfg = _outproj_loss(x2d, y_h, y_a, w_out_full, final_gain.reshape(1, d), tgt)

    own_o, recv_o = _dwout_scatter(y_h, y_a, dx2b)
    dza = _attn_bwd(z, dy, o_attn, lse)
    dzh, dlb, dgn = _hgrn_bwd(z, dy, states, lb_logits, hgrn_gnorm)
    grad_x, dng = _dh_dx(dzh, dza, w_in_full, x2d, norm_gain, dx2)
    own_i, recv_i = _dwin_scatter(h, dzh, dza)
    g_wi, d_wi, nm_wi, nv_wi = _sum_chips_adamw(own_i, recv_i, w_in[0], m_w_in[0], v_w_in[0])
    g_wo, d_wo, nm_wo, nv_wo = _sum_chips_adamw(own_o, recv_o, w_out[0], m_w_out[0], v_w_out[0])

    width = d // 2
    zero_row = jnp.zeros((1, width), F32)
    loss_row = loss_vec[:, :width] + loss_vec[:, width:]
    part = _pack_small(dng, dfg, jnp.concatenate([dlb, zero_row], axis=0), dgn, loss_row, width)
    pw = _pack_small(norm_gain, final_gain, lb_logits, hgrn_gnorm, zero_row, width)
    pm = _pack_small(m_norm_gain, m_final_gain, m_lb_logits, m_hgrn_gnorm, zero_row, width)
    pv = _pack_small(v_norm_gain, v_final_gain, v_lb_logits, v_hgrn_gnorm, zero_row, width)
    sg, sd, sm, sv = _small_allreduce_adamw(part, pw, pm, pv, lb_logits)
    hd = hgrn_gnorm.shape[1]
    g_ng, g_fg, g_lb, g_gn = _unpack_small(sg, d, e, hd)
    d_ng, d_fg, d_lb, d_gn = _unpack_small(sd, d, e, hd)
    m_ng, m_fg, m_lb, m_gn = _unpack_small(sm, d, e, hd)
    v_ng, v_fg, v_lb, v_gn = _unpack_small(sv, d, e, hd)
    loss = sg[ROW_LOSS, 0]

    one = lambda a: a[None]
    return (loss, grad_x.reshape(1, s, d), g_ng, one(g_wi), g_lb, g_gn, one(g_wo), g_fg,
            d_ng, one(d_wi), d_lb, d_gn, one(d_wo), d_fg,
            m_ng, one(nm_wi), m_lb, m_gn, one(nm_wo), m_fg,
            v_ng, one(nv_wi), v_lb, v_gn, one(nv_wo), v_fg)
```

```python
import functools
import math

import jax
import jax.numpy as jnp
from jax import lax
from jax.experimental import pallas as pl
from jax.experimental.pallas import tpu as pltpu

NORM_EPS = 1e-6
HGRN_HEAD = 128
HGRN_CHUNK = 64
ATTN_HEAD = 64
ATTN_BAND = 128
DILATIONS = (1, 4, 16)
N_SPLITS = 8
N_DEV = 8
ADAM_LR = 0.001
ADAM_B1 = 0.9
ADAM_B2 = 0.999
ADAM_EPS = 1e-08
ADAM_WD = 0.01
ADAM_STEP = 10
LANES = 128
MESH = pl.DeviceIdType.MESH
F32 = jnp.float32
BF16 = jnp.bfloat16
NEG_BIG = -1e30
VMEM_LIMIT = 56 * 1024 * 1024

ANY = pl.BlockSpec(memory_space=pl.ANY)


def _params(*sem):
    return pltpu.CompilerParams(dimension_semantics=sem, vmem_limit_bytes=VMEM_LIMIT)


def _tile(n, pref):
    t = min(n, pref)
    assert n % t == 0, (n, pref)
    return t


def _dot(a, b, precision=None):
    return jnp.dot(a, b, preferred_element_type=F32, precision=precision)


def _dot_nt(a, b):
    return lax.dot_general(a, b, (((1,), (1,)), ((), ())), preferred_element_type=F32)


def _dot_tn(a, b):
    return lax.dot_general(a, b, (((0,), (0,)), ((), ())), preferred_element_type=F32)


def _sigmoid(x):
    return 1.0 / (1.0 + jnp.exp(-x))


def _dsilu(x, s):
    return s * (1.0 + x * (1.0 - s))


def _adamw(w, g, m, v):
    m = ADAM_B1 * m + (1.0 - ADAM_B1) * g
    v = ADAM_B2 * v + (1.0 - ADAM_B2) * (g * g)
    m_hat = m / (1.0 - ADAM_B1 ** ADAM_STEP)
    v_hat = v / (1.0 - ADAM_B2 ** ADAM_STEP)
    delta = -ADAM_LR * (m_hat / (jnp.sqrt(v_hat) + ADAM_EPS) + ADAM_WD * w)
    return delta, m, v


def _cast_bf16(a):
    r, c = a.shape
    tr = _tile(r, 256)

    def body(a_ref, o_ref):
        o_ref[...] = a_ref[...].astype(BF16)

    return pl.pallas_call(
        body, name="cast_bf16", grid=(r // tr,), out_shape=jax.ShapeDtypeStruct((r, c), BF16),
        in_specs=[pl.BlockSpec((tr, c), lambda i: (i, 0))], out_specs=pl.BlockSpec((tr, c), lambda i: (i, 0)),
        compiler_params=_params("parallel"))(a)


def _rmsnorm_fwd(x, gain):
    s, d = x.shape
    tm = _tile(s, 512)

    def body(x_ref, g_ref, h_ref):
        xv = x_ref[...]
        r = lax.rsqrt(jnp.mean(xv * xv, axis=-1, keepdims=True) + NORM_EPS)
        h_ref[...] = (xv * r * g_ref[...]).astype(BF16)

    return pl.pallas_call(
        body, name="rmsnorm_fwd", grid=(s // tm,), out_shape=jax.ShapeDtypeStruct((s, d), BF16),
        in_specs=[pl.BlockSpec((tm, d), lambda i: (i, 0)), pl.BlockSpec((1, d), lambda i: (0, 0))],
        out_specs=pl.BlockSpec((tm, d), lambda i: (i, 0)), compiler_params=_params("parallel"))(x, gain)


HGRN_BLOCK = 1024
TRI_ROWS = 256


def _chunk_masks():
    tb = TRI_ROWS
    row = lax.broadcasted_iota(jnp.int32, (tb, tb), 0)
    col = lax.broadcasted_iota(jnp.int32, (tb, tb), 1)
    same = (row // HGRN_CHUNK) == (col // HGRN_CHUNK)
    lower = jnp.where(same & (col <= row), 1.0, 0.0).astype(BF16)
    upper = jnp.where(same & (col >= row), 1.0, 0.0).astype(BF16)
    return lower, upper


def _split3(a):
    hi = a.astype(BF16).astype(F32)
    mid = (a - hi).astype(BF16).astype(F32)
    lo = (a - hi - mid).astype(BF16).astype(F32)
    return hi, mid, lo


def _tri_dot(tri, x):
    hi, mid, lo = (p.astype(BF16) for p in _split3(x))
    outs = []
    for r in range(0, x.shape[0], TRI_ROWS):
        sl = slice(r, r + TRI_ROWS)
        outs.append(_dot(tri, hi[sl]) + _dot(tri, mid[sl]) + _dot(tri, lo[sl]))
    return outs[0] if len(outs) == 1 else jnp.concatenate(outs, axis=0)


def _hgrn_gates(qp, fp, lbv):
    lb = _sigmoid(lbv[0:1] - lbv[1:2])
    sq = _sigmoid(qp)
    q = qp * sq
    sg = _sigmoid(fp)
    f = lb + (1.0 - lb) * sg
    k = 1.0 - f
    return lb, sq, q, sg, f, k


def _hgrn_fwd(z, lb_logits, gnorm):
    s = z.shape[0]
    e = z.shape[1] // N_SPLITS
    nh = e // HGRN_HEAD
    tb = _tile(s, HGRN_BLOCK)
    nc = tb // HGRN_CHUNK
    nb = s // tb
    C = HGRN_CHUNK

    def body(q_ref, f_ref, i_ref, g_ref, lb_ref, gn_ref, y_ref, st_ref, state, o_scr):
        @pl.when(pl.program_id(1) == 0)
        def _():
            state[...] = jnp.zeros_like(state)

        lb, sq, q, sg, f, k = _hgrn_gates(q_ref[...], f_ref[...], lb_ref[...])
        lower, _ = _chunk_masks()
        b = _tri_dot(lower, jnp.log(f))
        b3 = b.reshape(nc, C, HGRN_HEAD)
        bc = b3[:, C - 1:C, :]
        qt = (q * jnp.exp(b)).astype(BF16)
        kt = (k * jnp.exp(-b)).astype(BF16)
        ke = (k.reshape(nc, C, HGRN_HEAD) * jnp.exp(bc - b3)).reshape(tb, HGRN_HEAD).astype(BF16)
        v = i_ref[...].astype(BF16)
        tri = lax.broadcasted_iota(jnp.int32, (C, C), 1) <= lax.broadcasted_iota(jnp.int32, (C, C), 0)
        sls = [slice(c * C, (c + 1) * C) for c in range(nc)]
        kv = [_dot_tn(v[sl], ke[sl]) for sl in sls]
        a = [jnp.where(tri, _dot_nt(qt[sl], kt[sl]), 0.0).astype(BF16) for sl in sls]
        st = state[...]
        sts = []
        for c in range(nc):
            sts.append(st)
            st_ref[c] = st
            st = st * jnp.exp(bc[c]) + kv[c]
        state[...] = st
        for c, sl in enumerate(sls):
            o_scr[sl, :] = _dot(a[c], v[sl]) + _dot_nt(qt[sl], sts[c].astype(BF16))
        o = o_scr[...]
        rms = lax.rsqrt(jnp.mean(o * o, axis=-1, keepdims=True) + NORM_EPS)
        gp = g_ref[...]
        y_ref[...] = (o * rms * gn_ref[...] * (gp * _sigmoid(gp))).astype(BF16)

    col = lambda kk: (lambda h, n: (n, kk * nh + h))
    return pl.pallas_call(
        body, name="hgrn_fwd", grid=(nh, nb),
        out_shape=(jax.ShapeDtypeStruct((s, e), BF16),
                   jax.ShapeDtypeStruct((nh, s // C, HGRN_HEAD, HGRN_HEAD), F32)),
        in_specs=[pl.BlockSpec((tb, HGRN_HEAD), col(0)), pl.BlockSpec((tb, HGRN_HEAD), col(1)),
                  pl.BlockSpec((tb, HGRN_HEAD), col(2)), pl.BlockSpec((tb, HGRN_HEAD), col(3)),
                  pl.BlockSpec((2, HGRN_HEAD), lambda h, n: (0, h)), pl.BlockSpec((1, HGRN_HEAD), lambda h, n: (0, 0))],
        out_specs=(pl.BlockSpec((tb, HGRN_HEAD), lambda h, n: (n, h)),
                   pl.BlockSpec((None, nc, HGRN_HEAD, HGRN_HEAD), lambda h, n: (h, n, 0, 0))),
        scratch_shapes=[pltpu.VMEM((HGRN_HEAD, HGRN_HEAD), F32), pltpu.VMEM((tb, HGRN_HEAD), F32)],
        compiler_params=_params("parallel", "arbitrary"))(z, z, z, z, lb_logits, gnorm)


def _hgrn_bwd(z, dy, states, lb_logits, gnorm):
    s = z.shape[0]
    e = z.shape[1] // N_SPLITS
    nh = e // HGRN_HEAD
    tb = _tile(s, HGRN_BLOCK)
    nc = tb // HGRN_CHUNK
    nb = s // tb
    C = HGRN_CHUNK
    H = HGRN_HEAD

    def body(q_ref, f_ref, i_ref, g_ref, dy_ref, st_ref, lb_ref, gn_ref, dz_ref, dlb_ref, dgn_ref,
             gstate, o_scr, dq_scr, dk_scr, dv_scr, e_scr):
        first = (pl.program_id(0) == 0) & (pl.program_id(1) == 0)

        @pl.when(first)
        def _():
            dgn_ref[...] = jnp.zeros_like(dgn_ref)

        @pl.when(pl.program_id(1) == 0)
        def _():
            gstate[...] = jnp.zeros_like(gstate)
            dlb_ref[...] = jnp.zeros_like(dlb_ref)

        qp = q_ref[...]
        lb, sq, q, sg, f, k = _hgrn_gates(qp, f_ref[...], lb_ref[...])
        lower, upper = _chunk_masks()
        b = _tri_dot(lower, jnp.log(f))
        b3 = b.reshape(nc, C, H)
        bc = b3[:, C - 1:C, :]
        eb = jnp.exp(b)
        enb = jnp.exp(-b)
        eend = jnp.exp(bc - b3).reshape(tb, H)
        qt = (q * eb).astype(BF16)
        kt = (k * enb).astype(BF16)
        ke = (k * eend).astype(BF16)
        v = i_ref[...].astype(BF16)
        tri = lax.broadcasted_iota(jnp.int32, (C, C), 1) <= lax.broadcasted_iota(jnp.int32, (C, C), 0)
        sls = [slice(c * C, (c + 1) * C) for c in range(nc)]
        a = [jnp.where(tri, _dot_nt(qt[sl], kt[sl]), 0.0).astype(BF16) for sl in sls]
        for c, sl in enumerate(sls):
            o_scr[sl, :] = _dot(a[c], v[sl]) + _dot_nt(qt[sl], st_ref[c].astype(BF16))
        o = o_scr[...]
        rms = lax.rsqrt(jnp.mean(o * o, axis=-1, keepdims=True) + NORM_EPS)
        on = o * rms
        gn = gn_ref[...]
        gp = g_ref[...]
        sgg = _sigmoid(gp)
        dyv = dy_ref[...]
        d_on = dyv * (gp * sgg)
        dz_ref[3] = (dyv * on * gn * _dsilu(gp, sgg)).astype(BF16)
        dgn_ref[...] += jnp.sum(d_on * on, axis=0, keepdims=True)
        u = d_on * gn
        do = (rms * (u - on * jnp.mean(u * on, axis=-1, keepdims=True))).astype(BF16)
        gup = [_dot_tn(do[sl], qt[sl]) for sl in sls]
        da = [jnp.where(tri, _dot_nt(do[sl], v[sl]), 0.0).astype(BF16) for sl in sls]
        gt = gstate[...]
        gts = [None] * nc
        for c in reversed(range(nc)):
            gts[c] = gt
            gt = gt * jnp.exp(bc[c]) + gup[c]
        gstate[...] = gt
        for c, sl in enumerate(sls):
            stp = st_ref[c]
            gtb = gts[c].astype(BF16)
            dqt = _dot(da[c], kt[sl]) + _dot(do[sl], stp.astype(BF16))
            dkt = _dot_tn(da[c], qt[sl])
            dks = _dot(v[sl], gtb) * eend[sl]
            dv_scr[sl, :] = _dot_tn(a[c], do[sl]) + _dot_nt(ke[sl], gtb)
            dq_scr[sl, :] = dqt * eb[sl]
            dk_scr[sl, :] = dkt * enb[sl] + dks
            ech = (jnp.sum(k[sl] * dks, axis=0, keepdims=True)
                   + jnp.sum(gts[c] * jnp.exp(bc[c]) * stp, axis=0, keepdims=True))
            e_scr[sl, :] = jnp.broadcast_to(ech, (C, H))
        dq = dq_scr[...]
        dk = dk_scr[...]
        dlf = _tri_dot(upper, q * dq - k * dk) + e_scr[...]
        dft = dlf / f - dk
        dz_ref[0] = (dq * _dsilu(qp, sq)).astype(BF16)
        dz_ref[1] = (dft * (1.0 - lb) * sg * (1.0 - sg)).astype(BF16)
        dz_ref[2] = dv_scr[...].astype(BF16)
        dlb_ref[...] += jnp.sum(dft * (1.0 - sg), axis=0, keepdims=True)

    col = lambda kk: (lambda h, n: (nb - 1 - n, kk * nh + h))
    return pl.pallas_call(
        body, name="hgrn_bwd", grid=(nh, nb),
        out_shape=(jax.ShapeDtypeStruct((4, s, e), BF16), jax.ShapeDtypeStruct((1, e), F32),
                   jax.ShapeDtypeStruct((1, H), F32)),
        in_specs=[pl.BlockSpec((tb, H), col(0)), pl.BlockSpec((tb, H), col(1)),
                  pl.BlockSpec((tb, H), col(2)), pl.BlockSpec((tb, H), col(3)),
                  pl.BlockSpec((tb, H), lambda h, n: (nb - 1 - n, h)),
                  pl.BlockSpec((None, nc, H, H), lambda h, n: (h, nb - 1 - n, 0, 0)),
                  pl.BlockSpec((2, H), lambda h, n: (0, h)), pl.BlockSpec((1, H), lambda h, n: (0, 0))],
        out_specs=(pl.BlockSpec((4, tb, H), lambda h, n: (0, nb - 1 - n, h)),
                   pl.BlockSpec((1, H), lambda h, n: (0, h)), pl.BlockSpec((1, H), lambda h, n: (0, 0))),
        scratch_shapes=[pltpu.VMEM((H, H), F32)] + [pltpu.VMEM((tb, H), F32)] * 5,
        compiler_params=_params("arbitrary", "arbitrary"))(z, z, z, z, dy, states, lb_logits, gnorm)


ATTN_T = 16 * ATTN_BAND
SCALE = ATTN_HEAD ** -0.5
TILE_UNROLL = 2


def _slope(hh, nheads):
    head = (2 * pl.program_id(0) + hh + 1).astype(F32)
    return jnp.exp(jnp.full((1, 1), -8.0 / nheads * math.log(2.0), F32) * head)


def _fill_bias(bias, nheads, delta, edge_ok):
    band = (delta >= 0) & (delta <= ATTN_BAND)
    dist = delta.astype(F32)
    for pi, dil in enumerate(DILATIONS):
        for hh in range(2):
            full = jnp.where(band, -(_slope(hh, nheads) * float(dil)) * dist, NEG_BIG)
            bias[(pi * 2 + hh) * 2] = full
            bias[(pi * 2 + hh) * 2 + 1] = jnp.where(edge_ok, full, NEG_BIG)


def _rows(start, size, stride):
    if stride == 1:
        return pl.ds(pl.multiple_of(start, ATTN_BAND), size)
    return pl.ds(start, size, stride=stride)


def _head_lanes(rows, hh):
    return (lax.broadcasted_iota(jnp.int32, (rows, LANES), 1) // ATTN_HEAD) == hh


def _attn_fwd(z):
    s = z.shape[0]
    e = z.shape[1] // N_SPLITS
    npair = e // LANES
    T = ATTN_T
    assert s % T == 0
    nsb = s // T
    W = ATTN_BAND
    nt = T // W
    HD = ATTN_HEAD
    chunk = 256

    def body(q_ref, kp_ref, kc_ref, vp_ref, vc_ref, g_ref, o_ref, l_ref, y_ref, qa, kbuf, va, bias, accs, ms, lsw):
        sb = pl.program_id(1)
        def stage(i, carry):
            rows = pl.ds(pl.multiple_of(i * chunk, chunk), chunk)
            upper = pl.ds(pl.multiple_of(T + i * chunk, chunk), chunk)
            kbuf[upper, :] = kc_ref[rows, :]
            for hh in range(2):
                mine = _head_lanes(chunk, hh)
                qa[hh, rows, :] = jnp.where(mine, q_ref[rows, :] * SCALE, 0.0)
                va[hh, upper, :] = jnp.where(mine, vc_ref[rows, :], 1.0)
            return carry

        lax.fori_loop(0, T // chunk, stage, 0)

        @pl.when(sb == 0)
        def _():
            def stage_prev(i, carry):
                rows = pl.ds(pl.multiple_of(i * chunk, chunk), chunk)
                kbuf[rows, :] = kp_ref[rows, :]
                for hh in range(2):
                    va[hh, rows, :] = jnp.where(_head_lanes(chunk, hh), vp_ref[rows, :], 1.0)
                return carry

            lax.fori_loop(0, T // chunk, stage_prev, 0)
        qi = lax.broadcasted_iota(jnp.int32, (W, 2 * W), 0)
        kj = lax.broadcasted_iota(jnp.int32, (W, 2 * W), 1)
        _fill_bias(bias, 2 * npair, W + qi - kj, kj >= W)

        def tile(tau, carry):
            first = _head_lanes(W, 0)
            rows, scores = [], []
            for pi, dil in enumerate(DILATIONS):
                r = tau % dil
                ub = tau // dil
                qrows = _rows(r + dil * W * ub, W, dil)
                krows = _rows(T + dil * W * (ub - 1) + r, 2 * W, dil)
                var = jnp.where((sb == 0) & (ub == 0), 1, 0)
                kt = kbuf[krows, :].astype(BF16)
                rows.append((qrows, krows))
                scores.append([_dot_nt(qa[hh, qrows, :].astype(BF16), kt) + bias[(pi * 2 + hh) * 2 + var]
                               for hh in range(2)])
            maxes = [[jnp.max(sc, axis=-1, keepdims=True) for sc in pair] for pair in scores]
            probs = [[jnp.exp(sc - m).astype(BF16) for sc, m in zip(ps, pm)] for ps, pm in zip(scores, maxes)]
            for pi, (qrows, krows) in enumerate(rows):
                outs = [_dot(probs[pi][hh], va[hh, krows, :].astype(BF16)) for hh in range(2)]
                accs[pi, qrows, :] = jnp.where(first, outs[0], outs[1])
                lsw[pi, qrows, :] = jnp.where(first, outs[1], outs[0])
                ms[pi, qrows, :] = jnp.where(first, maxes[pi][0], maxes[pi][1])
            return carry

        lax.fori_loop(0, nt, tile, 0, unroll=TILE_UNROLL)

        def merge(i, carry):
            rows = pl.ds(pl.multiple_of(i * chunk, chunk), chunk)
            m1, m2, m3 = ms[0, rows, :], ms[1, rows, :], ms[2, rows, :]
            mx = jnp.maximum(jnp.maximum(m1, m2), m3)
            w1, w2, w3 = jnp.exp(m1 - mx), jnp.exp(m2 - mx), jnp.exp(m3 - mx)
            unswap = lambda a: pltpu.roll(a, ATTN_HEAD, 1)
            den = w1 * unswap(lsw[0, rows, :]) + w2 * unswap(lsw[1, rows, :]) + w3 * unswap(lsw[2, rows, :])
            o = (w1 * accs[0, rows, :] + w2 * accs[1, rows, :] + w3 * accs[2, rows, :]) / den
            o_ref[rows, :] = o
            l_ref[rows, :] = mx + jnp.log(den)
            gp = g_ref[rows, :]
            y_ref[rows, :] = (o * (gp * _sigmoid(gp))).astype(BF16)
            upper = pl.ds(pl.multiple_of(T + i * chunk, chunk), chunk)
            kbuf[rows, :] = kbuf[upper, :]
            for hh in range(2):
                va[hh, rows, :] = va[hh, upper, :]
            return carry

        lax.fori_loop(0, T // chunk, merge, 0)

    cur = lambda split: (lambda hp, sb: (sb, split * npair + hp))
    prev = lambda split: (lambda hp, sb: (jnp.maximum(sb - 1, 0), split * npair + hp))
    blk = lambda index: pl.BlockSpec((T, LANES), index)
    out = blk(lambda hp, sb: (sb, hp))
    buf = lambda rows: pltpu.VMEM((rows, LANES), F32)
    return pl.pallas_call(
        body, name="attn_fwd", grid=(npair, nsb),
        out_shape=(jax.ShapeDtypeStruct((s, e), F32), jax.ShapeDtypeStruct((s, e), F32), jax.ShapeDtypeStruct((s, e), BF16)),
        in_specs=[blk(cur(4)), blk(prev(5)), blk(cur(5)), blk(prev(6)), blk(cur(6)), blk(cur(7))],
        out_specs=(out, out, out),
        scratch_shapes=[pltpu.VMEM((2, T, LANES), F32), buf(2 * T), pltpu.VMEM((2, 2 * T, LANES), F32),
                        pltpu.VMEM((12, W, 2 * W), F32)] + [pltpu.VMEM((3, T, LANES), F32)] * 3,
        compiler_params=_params("parallel", "arbitrary"))(z, z, z, z, z, z)


def _outproj_loss(x, y_h, y_a, w_out_full, final_gain, target):
    s, d = x.shape
    e = y_h.shape[1]
    tm = _tile(s, 256)

    def body(x_ref, yh_ref, ya_ref, w_ref, g_ref, t_ref, dx_ref, dxb_ref, dy_ref, loss_ref, dg_ref):
        @pl.when(pl.program_id(0) == 0)
        def _():
            loss_ref[...] = jnp.zeros_like(loss_ref)
            dg_ref[...] = jnp.zeros_like(dg_ref)

        w = w_ref[...]
        x2 = x_ref[...] + _dot(yh_ref[...], w[0:e]) + _dot(ya_ref[...], w[e:2 * e])
        r = lax.rsqrt(jnp.mean(x2 * x2, axis=-1, keepdims=True) + NORM_EPS)
        xn = x2 * r
        g = g_ref[...]
        err = xn * g - t_ref[...]
        loss_ref[...] += jnp.sum(err * err, axis=0, keepdims=True) * (0.5 / d)
        dyo = err * (1.0 / d)
        dg_ref[...] += jnp.sum(dyo * xn, axis=0, keepdims=True)
        u = dyo * g
        dx2 = r * (u - xn * jnp.mean(u * xn, axis=-1, keepdims=True))
        dx_ref[...] = dx2
        dxb = dx2.astype(BF16)
        dxb_ref[...] = dxb
        dy_ref[...] = _dot_nt(dxb, w)

    row = pl.BlockSpec((tm, d), lambda i: (i, 0))
    half = pl.BlockSpec((tm, e), lambda i: (i, 0))
    vec = pl.BlockSpec((1, d), lambda i: (0, 0))
    return pl.pallas_call(
        body, name="outproj_loss", grid=(s // tm,),
        out_shape=(jax.ShapeDtypeStruct((s, d), F32), jax.ShapeDtypeStruct((s, d), BF16),
                   jax.ShapeDtypeStruct((s, 2 * e), F32), jax.ShapeDtypeStruct((1, d), F32),
                   jax.ShapeDtypeStruct((1, d), F32)),
        in_specs=[row, half, half, pl.BlockSpec((2 * e, d), lambda i: (0, 0)), vec, row],
        out_specs=(row, row, pl.BlockSpec((tm, 2 * e), lambda i: (i, 0)), vec, vec),
        compiler_params=_params("arbitrary"))(x, y_h, y_a, w_out_full, final_gain, target)


def _attn_bwd(z, dy, o, lse):
    s, e = o.shape
    npair = e // LANES
    T = ATTN_T
    assert s % T == 0
    nsb = s // T
    W = ATTN_BAND
    nt = T // W
    HD = ATTN_HEAD
    chunk = 256

    def body(k_ref, v_ref, qc_ref, qn_ref, dyc_ref, dyn_ref, gc_ref, gn_ref, oc_ref, on_ref, lc_ref, ln_ref,
             dz_ref, qa, doa, ka, va, dqacc, dkacc, dvacc, bias):
        sb = pl.program_id(1)
        def stage_queries(half, q_r, dy_r, g_r, o_r, l_r):
            def stage(i, carry):
                rows = pl.ds(pl.multiple_of(i * chunk, chunk), chunk)
                dst = pl.ds(pl.multiple_of(half * T + i * chunk, chunk), chunk)
                lane = lax.broadcasted_iota(jnp.int32, (chunk, LANES), 1)
                gp = g_r[rows, :]
                dov = dy_r[rows, :] * (gp * _sigmoid(gp))
                qv = q_r[rows, :] * SCALE
                same_head = (lax.broadcasted_iota(jnp.int32, (LANES, LANES), 0) // HD
                             == lax.broadcasted_iota(jnp.int32, (LANES, LANES), 1) // HD)
                ones = jnp.where(same_head, 1.0, 0.0).astype(BF16)
                hi, mid, lo = (p.astype(BF16) for p in _split3(dov * o_r[rows, :]))
                delta = _dot(hi, ones) + _dot(mid, ones) + _dot(lo, ones)
                swap = lambda a: pltpu.roll(a, HD, 1)
                lse_parts = [swap(p) for p in _split3(l_r[rows, :])]
                dl_parts = [swap(p) for p in _split3(delta)]
                for hh in range(2):
                    mine = _head_lanes(chunk, hh)
                    spare = (1 - hh) * HD
                    qh = jnp.where(mine, qv, 0.0)
                    dh = jnp.where(mine, dov, 0.0)
                    for j in range(3):
                        qh = jnp.where(lane == spare + j, lse_parts[j], qh)
                        dh = jnp.where(lane == spare + j, dl_parts[j], dh)
                    qa[hh, dst, :] = qh
                    doa[hh, dst, :] = dh
                return carry

            lax.fori_loop(0, T // chunk, stage, 0)

        @pl.when(sb == 0)
        def _():
            stage_queries(0, qc_ref, dyc_ref, gc_ref, oc_ref, lc_ref)

        stage_queries(1, qn_ref, dyn_ref, gn_ref, on_ref, ln_ref)

        def stage_keys(i, carry):
            rows = pl.ds(pl.multiple_of(i * chunk, chunk), chunk)
            lane = lax.broadcasted_iota(jnp.int32, (chunk, LANES), 1)
            for hh in range(2):
                spare = (1 - hh) * HD
                minus = (lane >= spare) & (lane < spare + 3)
                ka[hh, rows, :] = jnp.where(minus, -1.0, k_ref[rows, :])
                va[hh, rows, :] = jnp.where(minus, -1.0, v_ref[rows, :])
            gp = gc_ref[rows, :]
            dz_ref[3, rows, :] = (dyc_ref[rows, :] * oc_ref[rows, :] * _dsilu(gp, _sigmoid(gp))).astype(BF16)
            return carry

        lax.fori_loop(0, T // chunk, stage_keys, 0)

        @pl.when(sb == 0)
        def _():
            dqacc[0:T, :] = jnp.zeros((T, LANES), F32)

        dqacc[T:, :] = jnp.zeros((T, LANES), F32)
        dkacc[...] = jnp.zeros_like(dkacc)
        dvacc[...] = jnp.zeros_like(dvacc)
        qi = lax.broadcasted_iota(jnp.int32, (2 * W, W), 0)
        kj = lax.broadcasted_iota(jnp.int32, (2 * W, W), 1)
        _fill_bias(bias, 2 * npair, qi - kj, qi < W)

        def tile(tau, carry):
            def scores(step, pi):
                dil = DILATIONS[pi]
                r = step % dil
                ub = step // dil
                start = r + dil * W * ub
                krows = _rows(start, W, dil)
                qrows = _rows(start, 2 * W, dil)
                var = jnp.where((sb == nsb - 1) & (ub == nt // dil - 1), 1, 0)
                unit = dict(krows=krows, qrows=qrows, ops=[], sc=[], dpd=[])
                for hh in range(2):
                    kt = ka[hh, krows, :].astype(BF16)
                    vt = va[hh, krows, :].astype(BF16)
                    qt = qa[hh, qrows, :].astype(BF16)
                    dt = doa[hh, qrows, :].astype(BF16)
                    unit["ops"].append((kt, qt, dt))
                    unit["sc"].append(_dot_nt(qt, kt) + bias[(pi * 2 + hh) * 2 + var])
                    unit["dpd"].append(_dot_nt(dt, vt))
                return unit

            def elementwise(unit):
                ps = [jnp.exp(s_) for s_ in unit["sc"]]
                unit["ds"] = [(p * d).astype(BF16) for p, d in zip(ps, unit["dpd"])]
                unit["pb"] = [p.astype(BF16) for p in ps]

            def products(unit):
                dvs = [_dot_tn(pb, dt) for pb, (kt, qt, dt) in zip(unit["pb"], unit["ops"])]
                dks = [_dot_tn(ds, qt) for ds, (kt, qt, dt) in zip(unit["ds"], unit["ops"])]
                dqs = [_dot(ds, kt) for ds, (kt, qt, dt) in zip(unit["ds"], unit["ops"])]
                dkacc[unit["krows"], :] += jnp.where(_head_lanes(W, 0), dks[0], dks[1])
                dvacc[unit["krows"], :] += jnp.where(_head_lanes(W, 0), dvs[0], dvs[1])
                dqacc[unit["qrows"], :] += jnp.where(_head_lanes(2 * W, 0), dqs[0], dqs[1]) * SCALE

            order = [(2 * tau + half, pi) for half in range(2) for pi in range(len(DILATIONS))]
            units = [None] * len(order)
            for n in range(len(order) + 2):
                if n < len(order):
                    units[n] = scores(*order[n])
                if 1 <= n <= len(order):
                    elementwise(units[n - 1])
                if n >= 2:
                    products(units[n - 2])
            return carry

        lax.fori_loop(0, nt // 2, tile, 0)

        def flush(i, carry):
            rows = pl.ds(pl.multiple_of(i * chunk, chunk), chunk)
            nxt = pl.ds(pl.multiple_of(T + i * chunk, chunk), chunk)
            dz_ref[0, rows, :] = dqacc[rows, :].astype(BF16)
            dz_ref[1, rows, :] = dkacc[rows, :].astype(BF16)
            dz_ref[2, rows, :] = dvacc[rows, :].astype(BF16)
            dqacc[rows, :] = dqacc[nxt, :]
            for hh in range(2):
                qa[hh, rows, :] = qa[hh, nxt, :]
                doa[hh, rows, :] = doa[hh, nxt, :]
            return carry

        lax.fori_loop(0, T // chunk, flush, 0)

    zc = lambda split: (lambda hp, sb: (sb, split * npair + hp))
    zn = lambda split: (lambda hp, sb: (jnp.minimum(sb + 1, nsb - 1), split * npair + hp))
    ec = lambda off: (lambda hp, sb: (sb, off + hp))
    en = lambda off: (lambda hp, sb: (jnp.minimum(sb + 1, nsb - 1), off + hp))
    blk = lambda index: pl.BlockSpec((T, LANES), index)
    buf = lambda rows: pltpu.VMEM((rows, LANES), F32)
    return pl.pallas_call(
        body, name="attn_bwd", grid=(npair, nsb), out_shape=jax.ShapeDtypeStruct((4, s, e), BF16),
        in_specs=[blk(zc(5)), blk(zc(6)), blk(zc(4)), blk(zn(4)), blk(ec(npair)), blk(en(npair)),
                  blk(zc(7)), blk(zn(7)), blk(ec(0)), blk(en(0)), blk(ec(0)), blk(en(0))],
        out_specs=pl.BlockSpec((4, T, LANES), lambda hp, sb: (0, sb, hp)),
        scratch_shapes=[pltpu.VMEM((2, 2 * T, LANES), F32), pltpu.VMEM((2, 2 * T, LANES), F32),
                        pltpu.VMEM((2, T, LANES), F32), pltpu.VMEM((2, T, LANES), F32),
                        buf(2 * T), buf(T), buf(T), pltpu.VMEM((12, 2 * W, W), F32)],
        compiler_params=_params("parallel", "arbitrary"))(z, z, z, z, dy, dy, z, z, o, o, lse, lse)


def _dz_specs(tm, e, axis):
    def mk(lo, hi):
        def index(i, k):
            row, grp = (i, k) if axis == 1 else (k, i)
            return (jnp.clip(grp - lo, 0, hi - lo - 1), row, 0)
        return pl.BlockSpec((None, tm, e), index)
    return [mk(0, 4), mk(4, 8)]


def _dz_pick(grp, dzh_ref, dza_ref, fn):
    @pl.when(grp < 4)
    def _():
        fn(dzh_ref[...])

    @pl.when(grp >= 4)
    def _():
        fn(dza_ref[...])


def _dh_dx(dzh, dza, w_full, x, gain, dx2):
    s, d = x.shape
    e = dzh.shape[2]
    tm = _tile(s, 1024)
    ni = s // tm
    chunk = _tile(tm, 256)
    fetch_at = 2

    def body(dzh_ref, dza_ref, w_ref, x_hbm, g_ref, dx2_hbm, gx_hbm, dg_ref, acc, xbuf, dbuf, sems):
        i, k = pl.program_id(0), pl.program_id(1)
        tile_rows = pl.ds(pl.multiple_of(i * tm, tm), tm)
        fetch_x = pltpu.make_async_copy(x_hbm.at[tile_rows, :], xbuf, sems.at[0])
        fetch_d = pltpu.make_async_copy(dx2_hbm.at[tile_rows, :], dbuf, sems.at[1])
        store = pltpu.make_async_copy(xbuf, gx_hbm.at[tile_rows, :], sems.at[2])

        @pl.when((i == 0) & (k == 0))
        def _():
            dg_ref[...] = jnp.zeros_like(dg_ref)

        @pl.when(k == 0)
        def _():
            acc[...] = jnp.zeros_like(acc)

        @pl.when((k == fetch_at) & (i > 0))
        def _():
            store.wait()

        @pl.when(k == fetch_at)
        def _():
            fetch_x.start()
            fetch_d.start()

        def add(dz):
            acc[...] += _dot_nt(dz, w_ref[...])

        _dz_pick(k, dzh_ref, dza_ref, add)

        @pl.when(k == N_SPLITS - 1)
        def _():
            fetch_x.wait()
            fetch_d.wait()
            gain_row = g_ref[...]

            def finish(c, dg):
                rows = pl.ds(pl.multiple_of(c * chunk, chunk), chunk)
                dh = acc[rows, :]
                xv = xbuf[rows, :]
                r = lax.rsqrt(jnp.mean(xv * xv, axis=-1, keepdims=True) + NORM_EPS)
                xn = xv * r
                u = dh * gain_row
                xbuf[rows, :] = dbuf[rows, :] + r * (u - xn * jnp.mean(u * xn, axis=-1, keepdims=True))
                return dg + jnp.sum(dh * xn, axis=0, keepdims=True)

            dg_ref[...] += lax.fori_loop(0, tm // chunk, finish, jnp.zeros((1, d), F32))
            store.start()

        @pl.when((k == N_SPLITS - 1) & (i == ni - 1))
        def _():
            store.wait()

    vec = pl.BlockSpec((1, d), lambda i, k: (0, 0))
    return pl.pallas_call(
        body, name="dh_dx", grid=(ni, N_SPLITS),
        out_shape=(jax.ShapeDtypeStruct((s, d), F32), jax.ShapeDtypeStruct((1, d), F32)),
        in_specs=_dz_specs(tm, e, 1) + [pl.BlockSpec((None, d, e), lambda i, k: (k, 0, 0)), ANY, vec, ANY],
        out_specs=(ANY, vec),
        scratch_shapes=[pltpu.VMEM((tm, d), F32), pltpu.VMEM((tm, d), F32), pltpu.VMEM((tm, d), F32),
                        pltpu.SemaphoreType.DMA((3,))],
        compiler_params=_params("arbitrary", "arbitrary"))(dzh, dza, w_full, x, gain, dx2)


def _position():
    x, y, c = lax.axis_index("x"), lax.axis_index("y"), lax.axis_index("c")
    return x, y, c


def _xor_peer(x, y, c, mask):
    return (x ^ ((mask >> 2) & 1), y ^ ((mask >> 1) & 1), c ^ (mask & 1))


def _block_order(masks):
    me = 4 * lax.axis_index("x") + 2 * lax.axis_index("y") + lax.axis_index("c")
    return jnp.stack([me ^ m for m in masks]).astype(jnp.int32)


GATHER_MASKS = (0, 1, 4, 2, 6, 5, 3, 7)


def _inproj_gather(h, w_loc, wo_loc):
    s, d = h.shape
    e = w_loc.shape[1]
    tm = _tile(s, 1024)
    ni = s // tm
    pre = max(ni - 2, 0)

    def body(order_ref, h_ref, w_ref, wo_ref, z_ref, wf_ref, wof_ref, wbuf, send_sems, recv_sems, osend, orecv,
             local_sems, wsems):
        j, i = pl.program_id(0), pl.program_id(1)
        x, y, c = _position()
        me, sibling = (x, y, c), (x, y, 1 - c)
        chips = [(1 - x, y), (x, 1 - y), (1 - x, 1 - y)]
        blk = lambda p: 4 * p[0] + 2 * p[1] + p[2]

        def copy(k, block, to, src=None):
            dst = wf_ref.at[blk(block)]
            return pltpu.make_async_remote_copy(
                src_ref=dst if src is None else src, dst_ref=dst, send_sem=send_sems.at[k], recv_sem=recv_sems.at[k],
                device_id=to, device_id_type=MESH)

        first = [copy(0, me, sibling, src=w_ref)] + [copy(1 + q, me, (*chip, c), src=w_ref) for q, chip in enumerate(chips)]
        passed = [copy(4 + q, (*chip, c), sibling) for q, chip in enumerate(chips)]
        mine = pltpu.make_async_copy(w_ref, wf_ref.at[blk(me)], local_sems.at[0])
        ocopies = [pltpu.make_async_remote_copy(
            src_ref=wo_ref, dst_ref=wof_ref.at[blk(me)], send_sem=osend.at[k], recv_sem=orecv.at[k],
            device_id=_xor_peer(x, y, c, k + 1), device_id_type=MESH) for k in range(N_DEV - 1)]
        omine = pltpu.make_async_copy(wo_ref, wof_ref.at[blk(me)], local_sems.at[1])
        blocks = [me, sibling] + [(*chip, c) for chip in chips] + [(*chip, 1 - c) for chip in chips]
        arrive = [None, copy(0, sibling, me)] + [copy(1 + q, (*chip, c), me) for q, chip in enumerate(chips)] \
            + [copy(4 + q, (*chip, 1 - c), me) for q, chip in enumerate(chips)]
        forward = [None, None] + passed + [None, None, None]

        def load(slot, src):
            return pltpu.make_async_copy(src, wbuf.at[slot], wsems.at[slot])

        @pl.when((j == 0) & (i == 0))
        def _():
            for cp in [mine, omine] + first + ocopies:
                cp.start()
            load(0, w_ref).start()

        for jj in range(N_DEV):
            @pl.when((j == jj) & (i == 0))
            def _():
                load(jj % 2, w_ref).wait()

            if jj + 1 < N_DEV:
                @pl.when((j == jj) & (i == pre))
                def _():
                    arrive[jj + 1].wait_recv()
                    if forward[jj + 1] is not None:
                        forward[jj + 1].start()
                    load((jj + 1) % 2, wf_ref.at[blk(blocks[jj + 1])]).start()

        z_ref[...] = _dot(h_ref[...], wbuf[j % 2])

        @pl.when((j == N_DEV - 1) & (i == ni - 1))
        def _():
            for cp in first + passed:
                cp.wait_send()
            for cp in ocopies:
                cp.wait_send()
                cp.wait_recv()
            mine.wait()
            omine.wait()

    grid_spec = pltpu.PrefetchScalarGridSpec(
        num_scalar_prefetch=1, grid=(N_DEV, ni),
        in_specs=[pl.BlockSpec((tm, d), lambda j, i, o: (i, 0)), ANY, ANY],
        out_specs=(pl.BlockSpec((tm, e), lambda j, i, o: (i, o[j])), ANY, ANY),
        scratch_shapes=[pltpu.VMEM((2, d, e), BF16), pltpu.SemaphoreType.DMA((7,)), pltpu.SemaphoreType.DMA((7,)),
                        pltpu.SemaphoreType.DMA((7,)), pltpu.SemaphoreType.DMA((7,)), pltpu.SemaphoreType.DMA((2,)),
                        pltpu.SemaphoreType.DMA((2,))])
    return pl.pallas_call(
        body, name="inproj_gather", grid_spec=grid_spec,
        out_shape=(jax.ShapeDtypeStruct((s, N_SPLITS * e), F32), jax.ShapeDtypeStruct((N_DEV, d, e), BF16),
                   jax.ShapeDtypeStruct((N_DEV,) + wo_loc.shape, BF16)),
        compiler_params=_params("arbitrary", "arbitrary"))(_block_order(GATHER_MASKS), h, w_loc, wo_loc)


SCATTER_MASKS = (7, 6, 5, 4, 3, 2, 1, 0)
N_CHIPS = 4


def _scatter_block(k, acc, stage, tmp, own_ref, ra_ref, rb_ref, sa_send, sa_recv, sb_send, sb_recv, loc_sem, last):
    x, y, c = _position()
    chip_of = lambda t: _xor_peer(x, y, c, SCATTER_MASKS[2 * t + 1])

    def ship(t):
        return pltpu.make_async_remote_copy(
            src_ref=stage.at[0], dst_ref=ra_ref.at[t], send_sem=sa_send.at[t], recv_sem=sa_recv.at[t],
            device_id=(x, y, 1 - c), device_id_type=MESH)

    def send(t):
        return pltpu.make_async_remote_copy(
            src_ref=stage.at[1], dst_ref=rb_ref.at[t], send_sem=sb_send.at[t], recv_sem=sb_recv.at[t],
            device_id=chip_of(t), device_id_type=MESH)

    for kk in range(N_DEV):
        t = kk // 2

        @pl.when(last & (k == kk))
        def _():
            if kk % 2 == 0:
                if t >= 1:
                    ship(t - 1).wait_send()
                stage[0] = acc[...].astype(BF16)
                ship(t).start()
            else:
                ship(t).wait_recv()
                fetch = pltpu.make_async_copy(ra_ref.at[t], tmp, loc_sem)
                fetch.start()
                fetch.wait()
                acc[...] += tmp[...].astype(F32)
                if t < N_CHIPS - 1:
                    if t >= 1:
                        send(t - 1).wait_send()
                    stage[1] = acc[...].astype(BF16)
                    send(t).start()
                else:
                    keep = pltpu.make_async_copy(acc, own_ref, loc_sem)
                    keep.start()
                    keep.wait()
                    ship(t).wait_send()
                    send(t - 1).wait_send()
                    for q in range(N_CHIPS - 1):
                        send(q).wait_recv()


def _scatter_scratch(rows, cols):
    return [pltpu.VMEM((rows, cols), F32), pltpu.VMEM((2, rows, cols), BF16), pltpu.VMEM((rows, cols), BF16),
            pltpu.SemaphoreType.DMA((N_CHIPS,)), pltpu.SemaphoreType.DMA((N_CHIPS,)),
            pltpu.SemaphoreType.DMA((N_CHIPS - 1,)), pltpu.SemaphoreType.DMA((N_CHIPS - 1,)), pltpu.SemaphoreType.DMA(())]


def _scatter_out(rows, cols):
    return (jax.ShapeDtypeStruct((rows, cols), F32), jax.ShapeDtypeStruct((N_CHIPS, rows, cols), BF16),
            jax.ShapeDtypeStruct((N_CHIPS - 1, rows, cols), BF16))


def _dwin_scatter(h, dzh, dza):
    s, d = h.shape
    e = dzh.shape[2]
    ts = _tile(s, 1024)
    ns = s // ts

    def body(order_ref, dzh_ref, dza_ref, h_ref, own_ref, ra_ref, rb_ref, acc, stage, tmp, *sems):
        k, step = pl.program_id(0), pl.program_id(1)

        @pl.when(step == 0)
        def _():
            acc[...] = jnp.zeros_like(acc)

        def add(dz):
            acc[...] += _dot_tn(h_ref[...], dz)

        _dz_pick(order_ref[k], dzh_ref, dza_ref, add)
        _scatter_block(k, acc, stage, tmp, own_ref, ra_ref, rb_ref, *sems, step == ns - 1)

    def dz_spec(lo):
        return pl.BlockSpec((None, ts, e), lambda k, st, o: (jnp.clip(o[k] - lo, 0, 3), st, 0))

    grid_spec = pltpu.PrefetchScalarGridSpec(
        num_scalar_prefetch=1, grid=(N_DEV, ns),
        in_specs=[dz_spec(0), dz_spec(4), pl.BlockSpec((ts, d), lambda k, st, o: (st, 0))],
        out_specs=(ANY, ANY, ANY), scratch_shapes=_scatter_scratch(d, e))
    own, _, rb = pl.pallas_call(
        body, name="dwin_scatter", grid_spec=grid_spec, out_shape=_scatter_out(d, e),
        compiler_params=_params("arbitrary", "arbitrary"))(_block_order(SCATTER_MASKS), dzh, dza, h)
    return own, rb


def _dwout_scatter(y_h, y_a, dxb):
    s, e = y_h.shape
    d = dxb.shape[1]
    r = 2 * e // N_DEV
    pairs = e // (2 * r)
    ts = _tile(s, 1024)
    ns = s // ts
    chip_masks = SCATTER_MASKS[1::2]

    def body(pair_ref, yh_ref, ya_ref, dx_ref, own_ref, ra_ref, rb_ref, acc, pend, stage, tmp,
             sa_send, sa_recv, sb_send, sb_recv, loc_sem):
        t, step = pl.program_id(0), pl.program_id(1)
        x, y, c = _position()

        @pl.when(step == 0)
        def _():
            acc[...] = jnp.zeros_like(acc)

        @pl.when(pair_ref[t] < pairs)
        def _():
            acc[...] += _dot_tn(yh_ref[...], dx_ref[...])

        @pl.when(pair_ref[t] >= pairs)
        def _():
            acc[...] += _dot_tn(ya_ref[...], dx_ref[...])

        mine = pl.ds(pl.multiple_of(c * r, r), r)
        other = pl.ds(pl.multiple_of((1 - c) * r, r), r)

        def ship(q):
            return pltpu.make_async_remote_copy(
                src_ref=stage.at[0], dst_ref=ra_ref.at[q], send_sem=sa_send.at[q], recv_sem=sa_recv.at[q],
                device_id=(x, y, 1 - c), device_id_type=MESH)

        def send(q):
            return pltpu.make_async_remote_copy(
                src_ref=stage.at[1], dst_ref=rb_ref.at[q], send_sem=sb_send.at[q], recv_sem=sb_recv.at[q],
                device_id=_xor_peer(x, y, c, chip_masks[q]), device_id_type=MESH)

        def sibling_share(q):
            ship(q).wait_recv()
            fetch = pltpu.make_async_copy(ra_ref.at[q], tmp, loc_sem)
            fetch.start()
            fetch.wait()
            return tmp[...].astype(F32)

        for tt in range(N_CHIPS):
            @pl.when((step == ns - 1) & (t == tt))
            def _():
                if tt >= 1:
                    ship(tt - 1).wait_send()
                stage[0] = acc[other, :].astype(BF16)
                ship(tt).start()
                if tt >= 1:
                    if tt >= 2:
                        send(tt - 2).wait_send()
                    stage[1] = (pend[...] + sibling_share(tt - 1)).astype(BF16)
                    send(tt - 1).start()
                if tt < N_CHIPS - 1:
                    pend[...] = acc[mine, :]
                else:
                    pend[...] = acc[mine, :] + sibling_share(tt)
                    keep = pltpu.make_async_copy(pend, own_ref, loc_sem)
                    keep.start()
                    keep.wait()
                    ship(tt).wait_send()
                    send(tt - 1).wait_send()
                    for q in range(N_CHIPS - 1):
                        send(q).wait_recv()

    def y_spec(lo):
        return pl.BlockSpec((ts, 2 * r), lambda t, st, o: (st, jnp.clip(o[t] - lo, 0, pairs - 1)))

    grid_spec = pltpu.PrefetchScalarGridSpec(
        num_scalar_prefetch=1, grid=(N_CHIPS, ns),
        in_specs=[y_spec(0), y_spec(pairs), pl.BlockSpec((ts, d), lambda t, st, o: (st, 0))],
        out_specs=(ANY, ANY, ANY),
        scratch_shapes=[pltpu.VMEM((2 * r, d), F32), pltpu.VMEM((r, d), F32)] + _scatter_scratch(r, d)[1:])
    own, _, rb = pl.pallas_call(
        body, name="dwout_scatter", grid_spec=grid_spec, out_shape=_scatter_out(r, d),
        compiler_params=_params("arbitrary", "arbitrary"))(_block_order(chip_masks) // 2, y_h, y_a, dxb)
    return own, rb


def _sum_chips_adamw(own, recv, w, m, v):
    r, c = w.shape
    tr = _tile(r, 128)

    def body(own_ref, rc_ref, w_ref, m_ref, v_ref, g_ref, d_ref, mo_ref, vo_ref):
        g = own_ref[...]
        for q in range(N_CHIPS - 1):
            g = g + rc_ref[q].astype(F32)
        g_ref[...] = g
        d_ref[...], mo_ref[...], vo_ref[...] = _adamw(w_ref[...], g, m_ref[...], v_ref[...])

    blk = pl.BlockSpec((tr, c), lambda i: (i, 0))
    shp = jax.ShapeDtypeStruct((r, c), F32)
    return pl.pallas_call(
        body, name="sum_chips_adamw", grid=(r // tr,), out_shape=(shp, shp, shp, shp),
        in_specs=[blk, pl.BlockSpec((N_CHIPS - 1, tr, c), lambda i: (0, i, 0)), blk, blk, blk],
        out_specs=(blk, blk, blk, blk), compiler_params=_params("parallel"))(own, recv, w, m, v)


SMALL_ROWS = 8
ROW_LB = 4
ROW_GN = 6
ROW_LOSS = 7


def _small_allreduce_adamw(part, w, m, v, lb_logits):
    width = part.shape[1]

    def body(p_ref, w_ref, m_ref, v_ref, lb_ref, g_ref, d_ref, mo_ref, vo_ref, buf, send_sems, recv_sems):
        x, y, c = _position()
        me = 4 * x + 2 * y + c
        buf[me] = p_ref[...]
        copies = []
        for k in range(N_DEV - 1):
            bx, by, bc = ((k + 1) >> 2) & 1, ((k + 1) >> 1) & 1, (k + 1) & 1
            peer = (x ^ bx, y ^ by, c ^ bc)
            copies.append(pltpu.make_async_remote_copy(
                src_ref=p_ref, dst_ref=buf.at[me], send_sem=send_sems.at[k], recv_sem=recv_sems.at[k],
                device_id=peer, device_id_type=MESH))
        for cp in copies:
            cp.start()
        for cp in copies:
            cp.wait_recv()
        for cp in copies:
            cp.wait_send()
        tot = buf[0]
        for dev in range(1, N_DEV):
            tot = tot + buf[dev]
        lbv = lb_ref[...]
        lb = _sigmoid(lbv[0:1] - lbv[1:2])
        glb = tot[ROW_LB:ROW_LB + 1] * lb * (1.0 - lb)
        loss = jnp.sum(tot[ROW_LOSS:ROW_LOSS + 1], axis=-1, keepdims=True)
        row = lax.broadcasted_iota(jnp.int32, (SMALL_ROWS, width), 0)
        g = jnp.where(row == ROW_LB, glb, jnp.where(row == ROW_LB + 1, -glb, tot))
        g = jnp.where(row == ROW_LOSS, loss, g)
        g_ref[...] = g
        d_ref[...], mo_ref[...], vo_ref[...] = _adamw(w_ref[...], g, m_ref[...], v_ref[...])

    vm = pl.BlockSpec(memory_space=pltpu.VMEM)
    shp = jax.ShapeDtypeStruct((SMALL_ROWS, width), F32)
    return pl.pallas_call(
        body, name="small_allreduce_adamw", out_shape=(shp, shp, shp, shp),
        in_specs=[vm] * 5, out_specs=(vm, vm, vm, vm),
        scratch_shapes=[pltpu.VMEM((N_DEV, SMALL_ROWS, width), F32), pltpu.SemaphoreType.DMA((N_DEV - 1,)),
                        pltpu.SemaphoreType.DMA((N_DEV - 1,))],
    )(part, w, m, v, lb_logits)


def _pack_small(norm_gain, final_gain, lb2, gnorm, last_row, width):
    pad = lambda a: jnp.pad(a.reshape(1, -1), ((0, 0), (0, width - a.size)))
    return jnp.concatenate([norm_gain.reshape(2, width), final_gain.reshape(2, width), lb2.reshape(2, width),
                            pad(gnorm), last_row.reshape(1, width)], axis=0)


def _unpack_small(p, d, e, hd):
    return (p[0:2].reshape(1, d), p[2:4].reshape(d), p[4:6].reshape(2, e), p[6:7, :hd].reshape(1, hd))


def kernel(x, norm_gain, w_in, lb_logits, hgrn_gnorm, w_out, final_gain, loss_target, m_norm_gain, m_w_in, m_lb_logits, m_hgrn_gnorm, m_w_out, m_final_gain, v_norm_gain, v_w_in, v_lb_logits, v_hgrn_gnorm, v_w_out, v_final_gain):
    s, d = x.shape[1], x.shape[2]
    e = w_in.shape[2]
    assert d == 2 * e and lb_logits.shape == (2, e) and w_out.shape[1] * N_DEV == 2 * e
    x2d = x.reshape(s, d)
    tgt = loss_target.reshape(s, d)

    h = _rmsnorm_fwd(x2d, norm_gain)
    z, w_in_full, w_out_full = _inproj_gather(h, _cast_bf16(w_in[0]), _cast_bf16(w_out[0]))
    w_out_full = w_out_full.reshape(2 * e, d)
    y_h, states = _hgrn_fwd(z, lb_logits, hgrn_gnorm)
    o_attn, lse, y_a = _attn_fwd(z)
    dx2, dx2b, dy, loss_vec, dfg = _outproj_loss(x2d, y_h, y_a, w_out_full, final_gain.reshape(1, d), tgt)

    own_o, recv_o = _dwout_scatter(y_h, y_a, dx2b)
    dza = _attn_bwd(z, dy, o_attn, lse)
    dzh, dlb, dgn = _hgrn_bwd(z, dy, states, lb_logits, hgrn_gnorm)
    grad_x, dng = _dh_dx(dzh, dza, w_in_full, x2d, norm_gain, dx2)
    own_i, recv_i = _dwin_scatter(h, dzh, dza)
    g_wi, d_wi, nm_wi, nv_wi = _sum_chips_adamw(own_i, recv_i, w_in[0], m_w_in[0], v_w_in[0])
    g_wo, d_wo, nm_wo, nv_wo = _sum_chips_adamw(own_o, recv_o, w_out[0], m_w_out[0], v_w_out[0])

    width = d // 2
    zero_row = jnp.zeros((1, width), F32)
    loss_row = loss_vec[:, :width] + loss_vec[:, width:]
    part = _pack_small(dng, dfg, jnp.concatenate([dlb, zero_row], axis=0), dgn, loss_row, width)
    pw = _pack_small(norm_gain, final_gain, lb_logits, hgrn_gnorm, zero_row, width)
    pm = _pack_small(m_norm_gain, m_final_gain, m_lb_logits, m_hgrn_gnorm, zero_row, width)
    pv = _pack_small(v_norm_gain, v_final_gain, v_lb_logits, v_hgrn_gnorm, zero_row, width)
    sg, sd, sm, sv = _small_allreduce_adamw(part, pw, pm, pv, lb_logits)
    hd = hgrn_gnorm.shape[1]
    g_ng, g_fg, g_lb, g_gn = _unpack_small(sg, d, e, hd)
    d_ng, d_fg, d_lb, d_gn = _unpack_small(sd, d, e, hd)
    m_ng, m_fg, m_lb, m_gn = _unpack_small(sm, d, e, hd)
    v_ng, v_fg, v_lb, v_gn = _unpack_small(sv, d, e, hd)
    loss = sg[ROW_LOSS, 0]

    one = lambda a: a[None]
    return (loss, grad_x.reshape(1, s, d), g_ng, one(g_wi), g_lb, g_gn, one(g_wo), g_fg,
            d_ng, one(d_wi), d_lb, d_gn, one(d_wo), d_fg,
            m_ng, one(nm_wi), m_lb, m_gn, one(nm_wo), m_fg,
            v_ng, one(nv_wi), v_lb, v_gn, one(nv_wo), v_fg)
```

```python
import functools
import math

import jax
import jax.numpy as jnp
from jax import lax
from jax.experimental import pallas as pl
from jax.experimental.pallas import tpu as pltpu

NORM_EPS = 1e-6
HGRN_HEAD = 128
HGRN_CHUNK = 64
ATTN_HEAD = 64
ATTN_BAND = 128
DILATIONS = (1, 4, 16)
N_SPLITS = 8
N_DEV = 8
ADAM_LR = 0.001
ADAM_B1 = 0.9
ADAM_B2 = 0.999
ADAM_EPS = 1e-08
ADAM_WD = 0.01
ADAM_STEP = 10
LANES = 128
MESH = pl.DeviceIdType.MESH
F32 = jnp.float32
BF16 = jnp.bfloat16
NEG_BIG = -1e30
VMEM_LIMIT = 56 * 1024 * 1024

ANY = pl.BlockSpec(memory_space=pl.ANY)


def _params(*sem):
    return pltpu.CompilerParams(dimension_semantics=sem, vmem_limit_bytes=VMEM_LIMIT)


def _tile(n, pref):
    t = min(n, pref)
    assert n % t == 0, (n, pref)
    return t


def _dot(a, b, precision=None):
    return jnp.dot(a, b, preferred_element_type=F32, precision=precision)


def _dot_nt(a, b):
    return lax.dot_general(a, b, (((1,), (1,)), ((), ())), preferred_element_type=F32)


def _dot_tn(a, b):
    return lax.dot_general(a, b, (((0,), (0,)), ((), ())), preferred_element_type=F32)


def _sigmoid(x):
    return 0.5 * jnp.tanh(0.5 * x) + 0.5


def _dsilu(x, s):
    return s * (1.0 + x * (1.0 - s))


def _adamw(w, g, m, v):
    m = ADAM_B1 * m + (1.0 - ADAM_B1) * g
    v = ADAM_B2 * v + (1.0 - ADAM_B2) * (g * g)
    m_hat = m / (1.0 - ADAM_B1 ** ADAM_STEP)
    v_hat = v / (1.0 - ADAM_B2 ** ADAM_STEP)
    delta = -ADAM_LR * (m_hat / (jnp.sqrt(v_hat) + ADAM_EPS) + ADAM_WD * w)
    return delta, m, v


def _cast_bf16(a):
    r, c = a.shape
    tr = _tile(r, 256)

    def body(a_ref, o_ref):
        o_ref[...] = a_ref[...].astype(BF16)

    return pl.pallas_call(
        body, name="cast_bf16", grid=(r // tr,), out_shape=jax.ShapeDtypeStruct((r, c), BF16),
        in_specs=[pl.BlockSpec((tr, c), lambda i: (i, 0))], out_specs=pl.BlockSpec((tr, c), lambda i: (i, 0)),
        compiler_params=_params("parallel"))(a)


def _rmsnorm_fwd(x, gain):
    s, d = x.shape
    tm = _tile(s, 512)

    def body(x_ref, g_ref, h_ref):
        xv = x_ref[...]
        r = lax.rsqrt(jnp.mean(xv * xv, axis=-1, keepdims=True) + NORM_EPS)
        h_ref[...] = (xv * r * g_ref[...]).astype(BF16)

    return pl.pallas_call(
        body, name="rmsnorm_fwd", grid=(s // tm,), out_shape=jax.ShapeDtypeStruct((s, d), BF16),
        in_specs=[pl.BlockSpec((tm, d), lambda i: (i, 0)), pl.BlockSpec((1, d), lambda i: (0, 0))],
        out_specs=pl.BlockSpec((tm, d), lambda i: (i, 0)), compiler_params=_params("parallel"))(x, gain)


HGRN_BLOCK = 2048
TRI_ROWS = 256


def _chunk_masks():
    tb = TRI_ROWS
    row = lax.broadcasted_iota(jnp.int32, (tb, tb), 0)
    col = lax.broadcasted_iota(jnp.int32, (tb, tb), 1)
    same = (row // HGRN_CHUNK) == (col // HGRN_CHUNK)
    lower = jnp.where(same & (col <= row), 1.0, 0.0).astype(BF16)
    upper = jnp.where(same & (col >= row), 1.0, 0.0).astype(BF16)
    return lower, upper


def _split3(a):
    hi = a.astype(BF16).astype(F32)
    mid = (a - hi).astype(BF16).astype(F32)
    lo = (a - hi - mid).astype(BF16).astype(F32)
    return hi, mid, lo


def _tri_dot(tri, x):
    hi, mid, lo = (p.astype(BF16) for p in _split3(x))
    outs = []
    for r in range(0, x.shape[0], TRI_ROWS):
        sl = slice(r, r + TRI_ROWS)
        outs.append(_dot(tri, hi[sl]) + _dot(tri, mid[sl]) + _dot(tri, lo[sl]))
    return outs[0] if len(outs) == 1 else jnp.concatenate(outs, axis=0)


def _hgrn_gates(qp, fp, lbv):
    lb = _sigmoid(lbv[0:1] - lbv[1:2])
    sq = _sigmoid(qp)
    q = qp * sq
    sg = _sigmoid(fp)
    f = lb + (1.0 - lb) * sg
    k = 1.0 - f
    return lb, sq, q, sg, f, k


def _hgrn_fwd(z, lb_logits, gnorm):
    s = z.shape[0]
    e = z.shape[1] // N_SPLITS
    nh = e // HGRN_HEAD
    tb = _tile(s, HGRN_BLOCK)
    nc = tb // HGRN_CHUNK
    nb = s // tb
    C = HGRN_CHUNK

    def body(q_ref, f_ref, i_ref, g_ref, lb_ref, gn_ref, y_ref, st_ref, state, o_scr):
        @pl.when(pl.program_id(1) == 0)
        def _():
            state[...] = jnp.zeros_like(state)

        lb, sq, q, sg, f, k = _hgrn_gates(q_ref[...], f_ref[...], lb_ref[...])
        lower, _ = _chunk_masks()
        b = _tri_dot(lower, jnp.log(f))
        b3 = b.reshape(nc, C, HGRN_HEAD)
        bc = b3[:, C - 1:C, :]
        qt = (q * jnp.exp(b)).astype(BF16)
        kt = (k * jnp.exp(-b)).astype(BF16)
        ke = (k.reshape(nc, C, HGRN_HEAD) * jnp.exp(bc - b3)).reshape(tb, HGRN_HEAD).astype(BF16)
        v = i_ref[...].astype(BF16)
        tri = lax.broadcasted_iota(jnp.int32, (C, C), 1) <= lax.broadcasted_iota(jnp.int32, (C, C), 0)
        sls = [slice(c * C, (c + 1) * C) for c in range(nc)]
        kv = [_dot_tn(v[sl], ke[sl]) for sl in sls]
        a = [jnp.where(tri, _dot_nt(qt[sl], kt[sl]), 0.0).astype(BF16) for sl in sls]
        st = state[...]
        sts = []
        for c in range(nc):
            sts.append(st)
            st_ref[c] = st
            st = st * jnp.exp(bc[c]) + kv[c]
        state[...] = st
        for c, sl in enumerate(sls):
            o_scr[sl, :] = _dot(a[c], v[sl]) + _dot_nt(qt[sl], sts[c].astype(BF16))
        o = o_scr[...]
        rms = lax.rsqrt(jnp.mean(o * o, axis=-1, keepdims=True) + NORM_EPS)
        gp = g_ref[...]
        y_ref[...] = (o * rms * gn_ref[...] * (gp * _sigmoid(gp))).astype(BF16)

    col = lambda kk: (lambda h, n: (n, kk * nh + h))
    return pl.pallas_call(
        body, name="hgrn_fwd", grid=(nh, nb),
        out_shape=(jax.ShapeDtypeStruct((s, e), BF16),
                   jax.ShapeDtypeStruct((nh, s // C, HGRN_HEAD, HGRN_HEAD), F32)),
        in_specs=[pl.BlockSpec((tb, HGRN_HEAD), col(0)), pl.BlockSpec((tb, HGRN_HEAD), col(1)),
                  pl.BlockSpec((tb, HGRN_HEAD), col(2)), pl.BlockSpec((tb, HGRN_HEAD), col(3)),
                  pl.BlockSpec((2, HGRN_HEAD), lambda h, n: (0, h)), pl.BlockSpec((1, HGRN_HEAD), lambda h, n: (0, 0))],
        out_specs=(pl.BlockSpec((tb, HGRN_HEAD), lambda h, n: (n, h)),
                   pl.BlockSpec((None, nc, HGRN_HEAD, HGRN_HEAD), lambda h, n: (h, n, 0, 0))),
        scratch_shapes=[pltpu.VMEM((HGRN_HEAD, HGRN_HEAD), F32), pltpu.VMEM((tb, HGRN_HEAD), F32)],
        compiler_params=_params("parallel", "arbitrary"))(z, z, z, z, lb_logits, gnorm)


def _hgrn_bwd(z, dy, states, lb_logits, gnorm):
    s = z.shape[0]
    e = z.shape[1] // N_SPLITS
    nh = e // HGRN_HEAD
    tb = _tile(s, HGRN_BLOCK)
    nc = tb // HGRN_CHUNK
    nb = s // tb
    C = HGRN_CHUNK
    H = HGRN_HEAD

    def body(q_ref, f_ref, i_ref, g_ref, dy_ref, st_ref, lb_ref, gn_ref, dz_ref, dlb_ref, dgn_ref,
             gstate, o_scr, dq_scr, dk_scr, dv_scr, e_scr):
        first = (pl.program_id(0) == 0) & (pl.program_id(1) == 0)

        @pl.when(first)
        def _():
            dgn_ref[...] = jnp.zeros_like(dgn_ref)

        @pl.when(pl.program_id(1) == 0)
        def _():
            gstate[...] = jnp.zeros_like(gstate)
            dlb_ref[...] = jnp.zeros_like(dlb_ref)

        qp = q_ref[...]
        lb, sq, q, sg, f, k = _hgrn_gates(qp, f_ref[...], lb_ref[...])
        lower, upper = _chunk_masks()
        b = _tri_dot(lower, jnp.log(f))
        b3 = b.reshape(nc, C, H)
        bc = b3[:, C - 1:C, :]
        eb = jnp.exp(b)
        enb = jnp.exp(-b)
        eend = jnp.exp(bc - b3).reshape(tb, H)
        qt = (q * eb).astype(BF16)
        kt = (k * enb).astype(BF16)
        ke = (k * eend).astype(BF16)
        v = i_ref[...].astype(BF16)
        tri = lax.broadcasted_iota(jnp.int32, (C, C), 1) <= lax.broadcasted_iota(jnp.int32, (C, C), 0)
        sls = [slice(c * C, (c + 1) * C) for c in range(nc)]
        a = [jnp.where(tri, _dot_nt(qt[sl], kt[sl]), 0.0).astype(BF16) for sl in sls]
        for c, sl in enumerate(sls):
            o_scr[sl, :] = _dot(a[c], v[sl]) + _dot_nt(qt[sl], st_ref[c].astype(BF16))
        o = o_scr[...]
        rms = lax.rsqrt(jnp.mean(o * o, axis=-1, keepdims=True) + NORM_EPS)
        on = o * rms
        gn = gn_ref[...]
        gp = g_ref[...]
        sgg = _sigmoid(gp)
        dyv = dy_ref[...]
        d_on = dyv * (gp * sgg)
        dz_ref[3] = (dyv * on * gn * _dsilu(gp, sgg)).astype(BF16)
        dgn_ref[...] += jnp.sum(d_on * on, axis=0, keepdims=True)
        u = d_on * gn
        do = (rms * (u - on * jnp.mean(u * on, axis=-1, keepdims=True))).astype(BF16)
        gup = [_dot_tn(do[sl], qt[sl]) for sl in sls]
        da = [jnp.where(tri, _dot_nt(do[sl], v[sl]), 0.0).astype(BF16) for sl in sls]
        gt = gstate[...]
        gts = [None] * nc
        for c in reversed(range(nc)):
            gts[c] = gt
            gt = gt * jnp.exp(bc[c]) + gup[c]
        gstate[...] = gt
        for c, sl in enumerate(sls):
            stp = st_ref[c]
            gtb = gts[c].astype(BF16)
            dqt = _dot(da[c], kt[sl]) + _dot(do[sl], stp.astype(BF16))
            dkt = _dot_tn(da[c], qt[sl])
            dks = _dot(v[sl], gtb) * eend[sl]
            dv_scr[sl, :] = _dot_tn(a[c], do[sl]) + _dot_nt(ke[sl], gtb)
            dq_scr[sl, :] = dqt * eb[sl]
            dk_scr[sl, :] = dkt * enb[sl] + dks
            ech = (jnp.sum(k[sl] * dks, axis=0, keepdims=True)
                   + jnp.sum(gts[c] * jnp.exp(bc[c]) * stp, axis=0, keepdims=True))
            e_scr[sl, :] = jnp.broadcast_to(ech, (C, H))
        dq = dq_scr[...]
        dk = dk_scr[...]
        dlf = _tri_dot(upper, q * dq - k * dk) + e_scr[...]
        dft = dlf / f - dk
        dz_ref[0] = (dq * _dsilu(qp, sq)).astype(BF16)
        dz_ref[1] = (dft * (1.0 - lb) * sg * (1.0 - sg)).astype(BF16)
        dz_ref[2] = dv_scr[...].astype(BF16)
        dlb_ref[...] += jnp.sum(dft * (1.0 - sg), axis=0, keepdims=True)

    col = lambda kk: (lambda h, n: (nb - 1 - n, kk * nh + h))
    return pl.pallas_call(
        body, name="hgrn_bwd", grid=(nh, nb),
        out_shape=(jax.ShapeDtypeStruct((4, s, e), BF16), jax.ShapeDtypeStruct((1, e), F32),
                   jax.ShapeDtypeStruct((1, H), F32)),
        in_specs=[pl.BlockSpec((tb, H), col(0)), pl.BlockSpec((tb, H), col(1)),
                  pl.BlockSpec((tb, H), col(2)), pl.BlockSpec((tb, H), col(3)),
                  pl.BlockSpec((tb, H), lambda h, n: (nb - 1 - n, h)),
                  pl.BlockSpec((None, nc, H, H), lambda h, n: (h, nb - 1 - n, 0, 0)),
                  pl.BlockSpec((2, H), lambda h, n: (0, h)), pl.BlockSpec((1, H), lambda h, n: (0, 0))],
        out_specs=(pl.BlockSpec((4, tb, H), lambda h, n: (0, nb - 1 - n, h)),
                   pl.BlockSpec((1, H), lambda h, n: (0, h)), pl.BlockSpec((1, H), lambda h, n: (0, 0))),
        scratch_shapes=[pltpu.VMEM((H, H), F32)] + [pltpu.VMEM((tb, H), F32)] * 5,
        compiler_params=_params("arbitrary", "arbitrary"))(z, z, z, z, dy, states, lb_logits, gnorm)


ATTN_T = 16 * ATTN_BAND
SCALE = ATTN_HEAD ** -0.5
TILE_UNROLL = 2


def _slope(hh, nheads):
    head = (2 * pl.program_id(0) + hh + 1).astype(F32)
    return jnp.exp(jnp.full((1, 1), -8.0 / nheads * math.log(2.0), F32) * head)


def _fill_bias(bias, nheads, delta, edge_ok):
    band = (delta >= 0) & (delta <= ATTN_BAND)
    dist = delta.astype(F32)
    for pi, dil in enumerate(DILATIONS):
        for hh in range(2):
            full = jnp.where(band, -(_slope(hh, nheads) * float(dil)) * dist, NEG_BIG)
            bias[(pi * 2 + hh) * 2] = full
            bias[(pi * 2 + hh) * 2 + 1] = jnp.where(edge_ok, full, NEG_BIG)


def _rows(start, size, stride):
    if stride == 1:
        return pl.ds(pl.multiple_of(start, ATTN_BAND), size)
    return pl.ds(start, size, stride=stride)


def _head_lanes(rows, hh):
    return (lax.broadcasted_iota(jnp.int32, (rows, LANES), 1) // ATTN_HEAD) == hh


def _attn_fwd(z):
    s = z.shape[0]
    e = z.shape[1] // N_SPLITS
    npair = e // LANES
    T = ATTN_T
    assert s % T == 0
    nsb = s // T
    W = ATTN_BAND
    nt = T // W
    HD = ATTN_HEAD
    chunk = 256

    def body(q_ref, kp_ref, kc_ref, vp_ref, vc_ref, g_ref, o_ref, l_ref, y_ref, qa, kbuf, va, bias, accs, ms, lsw):
        sb = pl.program_id(1)
        def stage(i, carry):
            rows = pl.ds(pl.multiple_of(i * chunk, chunk), chunk)
            upper = pl.ds(pl.multiple_of(T + i * chunk, chunk), chunk)
            kbuf[upper, :] = kc_ref[rows, :]
            for hh in range(2):
                mine = _head_lanes(chunk, hh)
                qa[hh, rows, :] = jnp.where(mine, q_ref[rows, :] * SCALE, 0.0)
                va[hh, upper, :] = jnp.where(mine, vc_ref[rows, :], 1.0)
            return carry

        lax.fori_loop(0, T // chunk, stage, 0)

        @pl.when(sb == 0)
        def _():
            def stage_prev(i, carry):
                rows = pl.ds(pl.multiple_of(i * chunk, chunk), chunk)
                kbuf[rows, :] = kp_ref[rows, :]
                for hh in range(2):
                    va[hh, rows, :] = jnp.where(_head_lanes(chunk, hh), vp_ref[rows, :], 1.0)
                return carry

            lax.fori_loop(0, T // chunk, stage_prev, 0)
        qi = lax.broadcasted_iota(jnp.int32, (W, 2 * W), 0)
        kj = lax.broadcasted_iota(jnp.int32, (W, 2 * W), 1)
        _fill_bias(bias, 2 * npair, W + qi - kj, kj >= W)

        def tile(tau, carry):
            first = _head_lanes(W, 0)
            rows, scores = [], []
            for pi, dil in enumerate(DILATIONS):
                r = tau % dil
                ub = tau // dil
                qrows = _rows(r + dil * W * ub, W, dil)
                krows = _rows(T + dil * W * (ub - 1) + r, 2 * W, dil)
                var = jnp.where((sb == 0) & (ub == 0), 1, 0)
                kt = kbuf[krows, :].astype(BF16)
                rows.append((qrows, krows))
                scores.append([_dot_nt(qa[hh, qrows, :].astype(BF16), kt) + bias[(pi * 2 + hh) * 2 + var]
                               for hh in range(2)])
            maxes = [[jnp.max(sc, axis=-1, keepdims=True) for sc in pair] for pair in scores]
            probs = [[jnp.exp(sc - m).astype(BF16) for sc, m in zip(ps, pm)] for ps, pm in zip(scores, maxes)]
            for pi, (qrows, krows) in enumerate(rows):
                outs = [_dot(probs[pi][hh], va[hh, krows, :].astype(BF16)) for hh in range(2)]
                accs[pi, qrows, :] = jnp.where(first, outs[0], outs[1])
                lsw[pi, qrows, :] = jnp.where(first, outs[1], outs[0])
                ms[pi, qrows, :] = jnp.where(first, maxes[pi][0], maxes[pi][1])
            return carry

        lax.fori_loop(0, nt, tile, 0, unroll=TILE_UNROLL)

        def merge(i, carry):
            rows = pl.ds(pl.multiple_of(i * chunk, chunk), chunk)
            m1, m2, m3 = ms[0, rows, :], ms[1, rows, :], ms[2, rows, :]
            mx = jnp.maximum(jnp.maximum(m1, m2), m3)
            w1, w2, w3 = jnp.exp(m1 - mx), jnp.exp(m2 - mx), jnp.exp(m3 - mx)
            unswap = lambda a: pltpu.roll(a, ATTN_HEAD, 1)
            den = w1 * unswap(lsw[0, rows, :]) + w2 * unswap(lsw[1, rows, :]) + w3 * unswap(lsw[2, rows, :])
            o = (w1 * accs[0, rows, :] + w2 * accs[1, rows, :] + w3 * accs[2, rows, :]) / den
            o_ref[rows, :] = o
            l_ref[rows, :] = mx + jnp.log(den)
            gp = g_ref[rows, :]
            y_ref[rows, :] = (o * (gp * _sigmoid(gp))).astype(BF16)
            upper = pl.ds(pl.multiple_of(T + i * chunk, chunk), chunk)
            kbuf[rows, :] = kbuf[upper, :]
            for hh in range(2):
                va[hh, rows, :] = va[hh, upper, :]
            return carry

        lax.fori_loop(0, T // chunk, merge, 0)

    cur = lambda split: (lambda hp, sb: (sb, split * npair + hp))
    prev = lambda split: (lambda hp, sb: (jnp.maximum(sb - 1, 0), split * npair + hp))
    blk = lambda index: pl.BlockSpec((T, LANES), index)
    out = blk(lambda hp, sb: (sb, hp))
    buf = lambda rows: pltpu.VMEM((rows, LANES), F32)
    return pl.pallas_call(
        body, name="attn_fwd", grid=(npair, nsb),
        out_shape=(jax.ShapeDtypeStruct((s, e), F32), jax.ShapeDtypeStruct((s, e), F32), jax.ShapeDtypeStruct((s, e), BF16)),
        in_specs=[blk(cur(4)), blk(prev(5)), blk(cur(5)), blk(prev(6)), blk(cur(6)), blk(cur(7))],
        out_specs=(out, out, out),
        scratch_shapes=[pltpu.VMEM((2, T, LANES), F32), buf(2 * T), pltpu.VMEM((2, 2 * T, LANES), F32),
                        pltpu.VMEM((12, W, 2 * W), F32)] + [pltpu.VMEM((3, T, LANES), F32)] * 3,
        compiler_params=_params("parallel", "arbitrary"))(z, z, z, z, z, z)


def _outproj_loss(x, y_h, y_a, w_out_full, final_gain, target):
    s, d = x.shape
    e = y_h.shape[1]
    tm = _tile(s, 256)

    def body(x_ref, yh_ref, ya_ref, w_ref, g_ref, t_ref, dx_ref, dxb_ref, dy_ref, loss_ref, dg_ref):
        @pl.when(pl.program_id(0) == 0)
        def _():
            loss_ref[...] = jnp.zeros_like(loss_ref)
            dg_ref[...] = jnp.zeros_like(dg_ref)

        w = w_ref[...]
        x2 = x_ref[...] + _dot(yh_ref[...], w[0:e]) + _dot(ya_ref[...], w[e:2 * e])
        r = lax.rsqrt(jnp.mean(x2 * x2, axis=-1, keepdims=True) + NORM_EPS)
        xn = x2 * r
        g = g_ref[...]
        err = xn * g - t_ref[...]
        loss_ref[...] += jnp.sum(err * err, axis=0, keepdims=True) * (0.5 / d)
        dyo = err * (1.0 / d)
        dg_ref[...] += jnp.sum(dyo * xn, axis=0, keepdims=True)
        u = dyo * g
        dx2 = r * (u - xn * jnp.mean(u * xn, axis=-1, keepdims=True))
        dx_ref[...] = dx2
        dxb = dx2.astype(BF16)
        dxb_ref[...] = dxb
        dy_ref[...] = _dot_nt(dxb, w)

    row = pl.BlockSpec((tm, d), lambda i: (i, 0))
    half = pl.BlockSpec((tm, e), lambda i: (i, 0))
    vec = pl.BlockSpec((1, d), lambda i: (0, 0))
    return pl.pallas_call(
        body, name="outproj_loss", grid=(s // tm,),
        out_shape=(jax.ShapeDtypeStruct((s, d), F32), jax.ShapeDtypeStruct((s, d), BF16),
                   jax.ShapeDtypeStruct((s, 2 * e), F32), jax.ShapeDtypeStruct((1, d), F32),
                   jax.ShapeDtypeStruct((1, d), F32)),
        in_specs=[row, half, half, pl.BlockSpec((2 * e, d), lambda i: (0, 0)), vec, row],
        out_specs=(row, row, pl.BlockSpec((tm, 2 * e), lambda i: (i, 0)), vec, vec),
        compiler_params=_params("arbitrary"))(x, y_h, y_a, w_out_full, final_gain, target)


def _attn_bwd(z, dy, o, lse):
    s, e = o.shape
    npair = e // LANES
    T = ATTN_T
    assert s % T == 0
    nsb = s // T
    W = ATTN_BAND
    nt = T // W
    HD = ATTN_HEAD
    chunk = 256

    def body(k_ref, v_ref, qc_ref, qn_ref, dyc_ref, dyn_ref, gc_ref, gn_ref, oc_ref, on_ref, lc_ref, ln_ref,
             dz_ref, qa, doa, ka, va, dqacc, dkacc, dvacc, bias):
        sb = pl.program_id(1)
        def stage_queries(half, q_r, dy_r, g_r, o_r, l_r):
            def stage(i, carry):
                rows = pl.ds(pl.multiple_of(i * chunk, chunk), chunk)
                dst = pl.ds(pl.multiple_of(half * T + i * chunk, chunk), chunk)
                lane = lax.broadcasted_iota(jnp.int32, (chunk, LANES), 1)
                gp = g_r[rows, :]
                dov = dy_r[rows, :] * (gp * _sigmoid(gp))
                qv = q_r[rows, :] * SCALE
                same_head = (lax.broadcasted_iota(jnp.int32, (LANES, LANES), 0) // HD
                             == lax.broadcasted_iota(jnp.int32, (LANES, LANES), 1) // HD)
                ones = jnp.where(same_head, 1.0, 0.0).astype(BF16)
                hi, mid, lo = (p.astype(BF16) for p in _split3(dov * o_r[rows, :]))
                delta = _dot(hi, ones) + _dot(mid, ones) + _dot(lo, ones)
                swap = lambda a: pltpu.roll(a, HD, 1)
                lse_parts = [swap(p) for p in _split3(l_r[rows, :])]
                dl_parts = [swap(p) for p in _split3(delta)]
                for hh in range(2):
                    mine = _head_lanes(chunk, hh)
                    spare = (1 - hh) * HD
                    qh = jnp.where(mine, qv, 0.0)
                    dh = jnp.where(mine, dov, 0.0)
                    for j in range(3):
                        qh = jnp.where(lane == spare + j, lse_parts[j], qh)
                        dh = jnp.where(lane == spare + j, dl_parts[j], dh)
                    qa[hh, dst, :] = qh
                    doa[hh, dst, :] = dh
                return carry

            lax.fori_loop(0, T // chunk, stage, 0)

        @pl.when(sb == 0)
        def _():
            stage_queries(0, qc_ref, dyc_ref, gc_ref, oc_ref, lc_ref)

        stage_queries(1, qn_ref, dyn_ref, gn_ref, on_ref, ln_ref)

        def stage_keys(i, carry):
            rows = pl.ds(pl.multiple_of(i * chunk, chunk), chunk)
            lane = lax.broadcasted_iota(jnp.int32, (chunk, LANES), 1)
            for hh in range(2):
                spare = (1 - hh) * HD
                minus = (lane >= spare) & (lane < spare + 3)
                ka[hh, rows, :] = jnp.where(minus, -1.0, k_ref[rows, :])
                va[hh, rows, :] = jnp.where(minus, -1.0, v_ref[rows, :])
            gp = gc_ref[rows, :]
            dz_ref[3, rows, :] = (dyc_ref[rows, :] * oc_ref[rows, :] * _dsilu(gp, _sigmoid(gp))).astype(BF16)
            return carry

        lax.fori_loop(0, T // chunk, stage_keys, 0)

        @pl.when(sb == 0)
        def _():
            dqacc[0:T, :] = jnp.zeros((T, LANES), F32)

        dqacc[T:, :] = jnp.zeros((T, LANES), F32)
        dkacc[...] = jnp.zeros_like(dkacc)
        dvacc[...] = jnp.zeros_like(dvacc)
        qi = lax.broadcasted_iota(jnp.int32, (2 * W, W), 0)
        kj = lax.broadcasted_iota(jnp.int32, (2 * W, W), 1)
        _fill_bias(bias, 2 * npair, qi - kj, qi < W)

        def tile(tau, carry):
            def scores(step, pi):
                dil = DILATIONS[pi]
                r = step % dil
                ub = step // dil
                start = r + dil * W * ub
                krows = _rows(start, W, dil)
                qrows = _rows(start, 2 * W, dil)
                var = jnp.where((sb == nsb - 1) & (ub == nt // dil - 1), 1, 0)
                unit = dict(krows=krows, qrows=qrows, ops=[], sc=[], dpd=[])
                for hh in range(2):
                    kt = ka[hh, krows, :].astype(BF16)
                    vt = va[hh, krows, :].astype(BF16)
                    qt = qa[hh, qrows, :].astype(BF16)
                    dt = doa[hh, qrows, :].astype(BF16)
                    unit["ops"].append((kt, qt, dt))
                    unit["sc"].append(_dot_nt(qt, kt) + bias[(pi * 2 + hh) * 2 + var])
                    unit["dpd"].append(_dot_nt(dt, vt))
                return unit

            def elementwise(unit):
                ps = [jnp.exp(s_) for s_ in unit["sc"]]
                unit["ds"] = [(p * d).astype(BF16) for p, d in zip(ps, unit["dpd"])]
                unit["pb"] = [p.astype(BF16) for p in ps]

            def products(unit):
                dvs = [_dot_tn(pb, dt) for pb, (kt, qt, dt) in zip(unit["pb"], unit["ops"])]
                dks = [_dot_tn(ds, qt) for ds, (kt, qt, dt) in zip(unit["ds"], unit["ops"])]
                dqs = [_dot(ds, kt) for ds, (kt, qt, dt) in zip(unit["ds"], unit["ops"])]
                dkacc[unit["krows"], :] += jnp.where(_head_lanes(W, 0), dks[0], dks[1])
                dvacc[unit["krows"], :] += jnp.where(_head_lanes(W, 0), dvs[0], dvs[1])
                dqacc[unit["qrows"], :] += jnp.where(_head_lanes(2 * W, 0), dqs[0], dqs[1]) * SCALE

            order = [(2 * tau + half, pi) for half in range(2) for pi in range(len(DILATIONS))]
            units = [None] * len(order)
            for n in range(len(order) + 2):
                if n < len(order):
                    units[n] = scores(*order[n])
                if 1 <= n <= len(order):
                    elementwise(units[n - 1])
                if n >= 2:
                    products(units[n - 2])
            return carry

        lax.fori_loop(0, nt // 2, tile, 0)

        def flush(i, carry):
            rows = pl.ds(pl.multiple_of(i * chunk, chunk), chunk)
            nxt = pl.ds(pl.multiple_of(T + i * chunk, chunk), chunk)
            dz_ref[0, rows, :] = dqacc[rows, :].astype(BF16)
            dz_ref[1, rows, :] = dkacc[rows, :].astype(BF16)
            dz_ref[2, rows, :] = dvacc[rows, :].astype(BF16)
            dqacc[rows, :] = dqacc[nxt, :]
            for hh in range(2):
                qa[hh, rows, :] = qa[hh, nxt, :]
                doa[hh, rows, :] = doa[hh, nxt, :]
            return carry

        lax.fori_loop(0, T // chunk, flush, 0)

    zc = lambda split: (lambda hp, sb: (sb, split * npair + hp))
    zn = lambda split: (lambda hp, sb: (jnp.minimum(sb + 1, nsb - 1), split * npair + hp))
    ec = lambda off: (lambda hp, sb: (sb, off + hp))
    en = lambda off: (lambda hp, sb: (jnp.minimum(sb + 1, nsb - 1), off + hp))
    blk = lambda index: pl.BlockSpec((T, LANES), index)
    buf = lambda rows: pltpu.VMEM((rows, LANES), F32)
    return pl.pallas_call(
        body, name="attn_bwd", grid=(npair, nsb), out_shape=jax.ShapeDtypeStruct((4, s, e), BF16),
        in_specs=[blk(zc(5)), blk(zc(6)), blk(zc(4)), blk(zn(4)), blk(ec(npair)), blk(en(npair)),
                  blk(zc(7)), blk(zn(7)), blk(ec(0)), blk(en(0)), blk(ec(0)), blk(en(0))],
        out_specs=pl.BlockSpec((4, T, LANES), lambda hp, sb: (0, sb, hp)),
        scratch_shapes=[pltpu.VMEM((2, 2 * T, LANES), F32), pltpu.VMEM((2, 2 * T, LANES), F32),
                        pltpu.VMEM((2, T, LANES), F32), pltpu.VMEM((2, T, LANES), F32),
                        buf(2 * T), buf(T), buf(T), pltpu.VMEM((12, 2 * W, W), F32)],
        compiler_params=_params("parallel", "arbitrary"))(z, z, z, z, dy, dy, z, z, o, o, lse, lse)


def _dz_specs(tm, e, axis):
    def mk(lo, hi):
        def index(i, k):
            row, grp = (i, k) if axis == 1 else (k, i)
            return (jnp.clip(grp - lo, 0, hi - lo - 1), row, 0)
        return pl.BlockSpec((None, tm, e), index)
    return [mk(0, 4), mk(4, 8)]


def _dz_pick(grp, dzh_ref, dza_ref, fn):
    @pl.when(grp < 4)
    def _():
        fn(dzh_ref[...])

    @pl.when(grp >= 4)
    def _():
        fn(dza_ref[...])


def _dh_dx(dzh, dza, w_full, x, gain, dx2):
    s, d = x.shape
    e = dzh.shape[2]
    tm = _tile(s, 1024)
    ni = s // tm
    chunk = _tile(tm, 256)
    fetch_at = 2

    def body(dzh_ref, dza_ref, w_ref, x_hbm, g_ref, dx2_hbm, gx_hbm, dg_ref, acc, xbuf, dbuf, sems):
        i, k = pl.program_id(0), pl.program_id(1)
        tile_rows = pl.ds(pl.multiple_of(i * tm, tm), tm)
        fetch_x = pltpu.make_async_copy(x_hbm.at[tile_rows, :], xbuf, sems.at[0])
        fetch_d = pltpu.make_async_copy(dx2_hbm.at[tile_rows, :], dbuf, sems.at[1])
        store = pltpu.make_async_copy(xbuf, gx_hbm.at[tile_rows, :], sems.at[2])

        @pl.when((i == 0) & (k == 0))
        def _():
            dg_ref[...] = jnp.zeros_like(dg_ref)

        @pl.when(k == 0)
        def _():
            acc[...] = jnp.zeros_like(acc)

        @pl.when((k == fetch_at) & (i > 0))
        def _():
            store.wait()

        @pl.when(k == fetch_at)
        def _():
            fetch_x.start()
            fetch_d.start()

        def add(dz):
            acc[...] += _dot_nt(dz, w_ref[...])

        _dz_pick(k, dzh_ref, dza_ref, add)

        @pl.when(k == N_SPLITS - 1)
        def _():
            fetch_x.wait()
            fetch_d.wait()
            gain_row = g_ref[...]

            def finish(c, dg):
                rows = pl.ds(pl.multiple_of(c * chunk, chunk), chunk)
                dh = acc[rows, :]
                xv = xbuf[rows, :]
                r = lax.rsqrt(jnp.mean(xv * xv, axis=-1, keepdims=True) + NORM_EPS)
                xn = xv * r
                u = dh * gain_row
                xbuf[rows, :] = dbuf[rows, :] + r * (u - xn * jnp.mean(u * xn, axis=-1, keepdims=True))
                return dg + jnp.sum(dh * xn, axis=0, keepdims=True)

            dg_ref[...] += lax.fori_loop(0, tm // chunk, finish, jnp.zeros((1, d), F32))
            store.start()

        @pl.when((k == N_SPLITS - 1) & (i == ni - 1))
        def _():
            store.wait()

    vec = pl.BlockSpec((1, d), lambda i, k: (0, 0))
    return pl.pallas_call(
        body, name="dh_dx", grid=(ni, N_SPLITS),
        out_shape=(jax.ShapeDtypeStruct((s, d), F32), jax.ShapeDtypeStruct((1, d), F32)),
        in_specs=_dz_specs(tm, e, 1) + [pl.BlockSpec((None, d, e), lambda i, k: (k, 0, 0)), ANY, vec, ANY],
        out_specs=(ANY, vec),
        scratch_shapes=[pltpu.VMEM((tm, d), F32), pltpu.VMEM((tm, d), F32), pltpu.VMEM((tm, d), F32),
                        pltpu.SemaphoreType.DMA((3,))],
        compiler_params=_params("arbitrary", "arbitrary"))(dzh, dza, w_full, x, gain, dx2)


def _position():
    x, y, c = lax.axis_index("x"), lax.axis_index("y"), lax.axis_index("c")
    return x, y, c


def _xor_peer(x, y, c, mask):
    return (x ^ ((mask >> 2) & 1), y ^ ((mask >> 1) & 1), c ^ (mask & 1))


def _block_order(masks):
    me = 4 * lax.axis_index("x") + 2 * lax.axis_index("y") + lax.axis_index("c")
    return jnp.stack([me ^ m for m in masks]).astype(jnp.int32)


GATHER_MASKS = (0, 1, 4, 2, 6, 5, 3, 7)


def _inproj_gather(h, w_loc, wo_loc):
    s, d = h.shape
    e = w_loc.shape[1]
    tm = _tile(s, 1024)
    ni = s // tm
    pre = max(ni - 2, 0)

    def body(order_ref, h_ref, w_ref, wo_ref, z_ref, wf_ref, wof_ref, wbuf, send_sems, recv_sems, osend, orecv,
             local_sems, wsems):
        j, i = pl.program_id(0), pl.program_id(1)
        x, y, c = _position()
        me, sibling = (x, y, c), (x, y, 1 - c)
        chips = [(1 - x, y), (x, 1 - y), (1 - x, 1 - y)]
        blk = lambda p: 4 * p[0] + 2 * p[1] + p[2]

        def copy(k, block, to, src=None):
            dst = wf_ref.at[blk(block)]
            return pltpu.make_async_remote_copy(
                src_ref=dst if src is None else src, dst_ref=dst, send_sem=send_sems.at[k], recv_sem=recv_sems.at[k],
                device_id=to, device_id_type=MESH)

        first = [copy(0, me, sibling, src=w_ref)] + [copy(1 + q, me, (*chip, c), src=w_ref) for q, chip in enumerate(chips)]
        passed = [copy(4 + q, (*chip, c), sibling) for q, chip in enumerate(chips)]
        mine = pltpu.make_async_copy(w_ref, wf_ref.at[blk(me)], local_sems.at[0])
        ocopies = [pltpu.make_async_remote_copy(
            src_ref=wo_ref, dst_ref=wof_ref.at[blk(me)], send_sem=osend.at[k], recv_sem=orecv.at[k],
            device_id=_xor_peer(x, y, c, k + 1), device_id_type=MESH) for k in range(N_DEV - 1)]
        omine = pltpu.make_async_copy(wo_ref, wof_ref.at[blk(me)], local_sems.at[1])
        blocks = [me, sibling] + [(*chip, c) for chip in chips] + [(*chip, 1 - c) for chip in chips]
        arrive = [None, copy(0, sibling, me)] + [copy(1 + q, (*chip, c), me) for q, chip in enumerate(chips)] \
            + [copy(4 + q, (*chip, 1 - c), me) for q, chip in enumerate(chips)]
        forward = [None, None] + passed + [None, None, None]

        def load(slot, src):
            return pltpu.make_async_copy(src, wbuf.at[slot], wsems.at[slot])

        @pl.when((j == 0) & (i == 0))
        def _():
            for cp in [mine, omine] + first + ocopies:
                cp.start()
            load(0, w_ref).start()

        for jj in range(N_DEV):
            @pl.when((j == jj) & (i == 0))
            def _():
                load(jj % 2, w_ref).wait()

            if jj + 1 < N_DEV:
                @pl.when((j == jj) & (i == pre))
                def _():
                    arrive[jj + 1].wait_recv()
                    if forward[jj + 1] is not None:
                        forward[jj + 1].start()
                    load((jj + 1) % 2, wf_ref.at[blk(blocks[jj + 1])]).start()

        z_ref[...] = _dot(h_ref[...], wbuf[j % 2])

        @pl.when((j == N_DEV - 1) & (i == ni - 1))
        def _():
            for cp in first + passed:
                cp.wait_send()
            for cp in ocopies:
                cp.wait_send()
                cp.wait_recv()
            mine.wait()
            omine.wait()

    grid_spec = pltpu.PrefetchScalarGridSpec(
        num_scalar_prefetch=1, grid=(N_DEV, ni),
        in_specs=[pl.BlockSpec((tm, d), lambda j, i, o: (i, 0)), ANY, ANY],
        out_specs=(pl.BlockSpec((tm, e), lambda j, i, o: (i, o[j])), ANY, ANY),
        scratch_shapes=[pltpu.VMEM((2, d, e), BF16), pltpu.SemaphoreType.DMA((7,)), pltpu.SemaphoreType.DMA((7,)),
                        pltpu.SemaphoreType.DMA((7,)), pltpu.SemaphoreType.DMA((7,)), pltpu.SemaphoreType.DMA((2,)),
                        pltpu.SemaphoreType.DMA((2,))])
    return pl.pallas_call(
        body, name="inproj_gather", grid_spec=grid_spec,
        out_shape=(jax.ShapeDtypeStruct((s, N_SPLITS * e), F32), jax.ShapeDtypeStruct((N_DEV, d, e), BF16),
                   jax.ShapeDtypeStruct((N_DEV,) + wo_loc.shape, BF16)),
        compiler_params=_params("arbitrary", "arbitrary"))(_block_order(GATHER_MASKS), h, w_loc, wo_loc)


SCATTER_MASKS = (7, 6, 5, 4, 3, 2, 1, 0)
N_CHIPS = 4


def _scatter_block(k, acc, stage, tmp, own_ref, ra_ref, rb_ref, sa_send, sa_recv, sb_send, sb_recv, loc_sem, last):
    x, y, c = _position()
    chip_of = lambda t: _xor_peer(x, y, c, SCATTER_MASKS[2 * t + 1])

    def ship(t):
        return pltpu.make_async_remote_copy(
            src_ref=stage.at[0], dst_ref=ra_ref.at[t], send_sem=sa_send.at[t], recv_sem=sa_recv.at[t],
            device_id=(x, y, 1 - c), device_id_type=MESH)

    def send(t):
        return pltpu.make_async_remote_copy(
            src_ref=stage.at[1], dst_ref=rb_ref.at[t], send_sem=sb_send.at[t], recv_sem=sb_recv.at[t],
            device_id=chip_of(t), device_id_type=MESH)

    for kk in range(N_DEV):
        t = kk // 2

        @pl.when(last & (k == kk))
        def _():
            if kk % 2 == 0:
                if t >= 1:
                    ship(t - 1).wait_send()
                stage[0] = acc[...].astype(BF16)
                ship(t).start()
            else:
                ship(t).wait_recv()
                fetch = pltpu.make_async_copy(ra_ref.at[t], tmp, loc_sem)
                fetch.start()
                fetch.wait()
                acc[...] += tmp[...].astype(F32)
                if t < N_CHIPS - 1:
                    if t >= 1:
                        send(t - 1).wait_send()
                    stage[1] = acc[...].astype(BF16)
                    send(t).start()
                else:
                    keep = pltpu.make_async_copy(acc, own_ref, loc_sem)
                    keep.start()
                    keep.wait()
                    ship(t).wait_send()
                    send(t - 1).wait_send()
                    for q in range(N_CHIPS - 1):
                        send(q).wait_recv()


def _scatter_scratch(rows, cols):
    return [pltpu.VMEM((rows, cols), F32), pltpu.VMEM((2, rows, cols), BF16), pltpu.VMEM((rows, cols), BF16),
            pltpu.SemaphoreType.DMA((N_CHIPS,)), pltpu.SemaphoreType.DMA((N_CHIPS,)),
            pltpu.SemaphoreType.DMA((N_CHIPS - 1,)), pltpu.SemaphoreType.DMA((N_CHIPS - 1,)), pltpu.SemaphoreType.DMA(())]


def _scatter_out(rows, cols):
    return (jax.ShapeDtypeStruct((rows, cols), F32), jax.ShapeDtypeStruct((N_CHIPS, rows, cols), BF16),
            jax.ShapeDtypeStruct((N_CHIPS - 1, rows, cols), BF16))


def _dwin_scatter(h, dzh, dza):
    s, d = h.shape
    e = dzh.shape[2]
    ts = _tile(s, 1024)
    ns = s // ts

    def body(order_ref, dzh_ref, dza_ref, h_ref, own_ref, ra_ref, rb_ref, acc, stage, tmp, *sems):
        k, step = pl.program_id(0), pl.program_id(1)

        @pl.when(step == 0)
        def _():
            acc[...] = jnp.zeros_like(acc)

        def add(dz):
            acc[...] += _dot_tn(h_ref[...], dz)

        _dz_pick(order_ref[k], dzh_ref, dza_ref, add)
        _scatter_block(k, acc, stage, tmp, own_ref, ra_ref, rb_ref, *sems, step == ns - 1)

    def dz_spec(lo):
        return pl.BlockSpec((None, ts, e), lambda k, st, o: (jnp.clip(o[k] - lo, 0, 3), st, 0))

    grid_spec = pltpu.PrefetchScalarGridSpec(
        num_scalar_prefetch=1, grid=(N_DEV, ns),
        in_specs=[dz_spec(0), dz_spec(4), pl.BlockSpec((ts, d), lambda k, st, o: (st, 0))],
        out_specs=(ANY, ANY, ANY), scratch_shapes=_scatter_scratch(d, e))
    own, _, rb = pl.pallas_call(
        body, name="dwin_scatter", grid_spec=grid_spec, out_shape=_scatter_out(d, e),
        compiler_params=_params("arbitrary", "arbitrary"))(_block_order(SCATTER_MASKS), dzh, dza, h)
    return own, rb


def _dwout_scatter(y_h, y_a, dxb):
    s, e = y_h.shape
    d = dxb.shape[1]
    r = 2 * e // N_DEV
    pairs = e // (2 * r)
    ts = _tile(s, 1024)
    ns = s // ts
    chip_masks = SCATTER_MASKS[1::2]

    def body(pair_ref, yh_ref, ya_ref, dx_ref, own_ref, ra_ref, rb_ref, acc, pend, stage, tmp,
             sa_send, sa_recv, sb_send, sb_recv, loc_sem):
        t, step = pl.program_id(0), pl.program_id(1)
        x, y, c = _position()

        @pl.when(step == 0)
        def _():
            acc[...] = jnp.zeros_like(acc)

        @pl.when(pair_ref[t] < pairs)
        def _():
            acc[...] += _dot_tn(yh_ref[...], dx_ref[...])

        @pl.when(pair_ref[t] >= pairs)
        def _():
            acc[...] += _dot_tn(ya_ref[...], dx_ref[...])

        mine = pl.ds(pl.multiple_of(c * r, r), r)
        other = pl.ds(pl.multiple_of((1 - c) * r, r), r)

        def ship(q):
            return pltpu.make_async_remote_copy(
                src_ref=stage.at[0], dst_ref=ra_ref.at[q], send_sem=sa_send.at[q], recv_sem=sa_recv.at[q],
                device_id=(x, y, 1 - c), device_id_type=MESH)

        def send(q):
            return pltpu.make_async_remote_copy(
                src_ref=stage.at[1], dst_ref=rb_ref.at[q], send_sem=sb_send.at[q], recv_sem=sb_recv.at[q],
                device_id=_xor_peer(x, y, c, chip_masks[q]), device_id_type=MESH)

        def sibling_share(q):
            ship(q).wait_recv()
            fetch = pltpu.make_async_copy(ra_ref.at[q], tmp, loc_sem)
            fetch.start()
            fetch.wait()
            return tmp[...].astype(F32)

        for tt in range(N_CHIPS):
            @pl.when((step == ns - 1) & (t == tt))
            def _():
                if tt >= 1:
                    ship(tt - 1).wait_send()
                stage[0] = acc[other, :].astype(BF16)
                ship(tt).start()
                if tt >= 1:
                    if tt >= 2:
                        send(tt - 2).wait_send()
                    stage[1] = (pend[...] + sibling_share(tt - 1)).astype(BF16)
                    send(tt - 1).start()
                if tt < N_CHIPS - 1:
                    pend[...] = acc[mine, :]
                else:
                    pend[...] = acc[mine, :] + sibling_share(tt)
                    keep = pltpu.make_async_copy(pend, own_ref, loc_sem)
                    keep.start()
                    keep.wait()
                    ship(tt).wait_send()
                    send(tt - 1).wait_send()
                    for q in range(N_CHIPS - 1):
                        send(q).wait_recv()

    def y_spec(lo):
        return pl.BlockSpec((ts, 2 * r), lambda t, st, o: (st, jnp.clip(o[t] - lo, 0, pairs - 1)))

    grid_spec = pltpu.PrefetchScalarGridSpec(
        num_scalar_prefetch=1, grid=(N_CHIPS, ns),
        in_specs=[y_spec(0), y_spec(pairs), pl.BlockSpec((ts, d), lambda t, st, o: (st, 0))],
        out_specs=(ANY, ANY, ANY),
        scratch_shapes=[pltpu.VMEM((2 * r, d), F32), pltpu.VMEM((r, d), F32)] + _scatter_scratch(r, d)[1:])
    own, _, rb = pl.pallas_call(
        body, name="dwout_scatter", grid_spec=grid_spec, out_shape=_scatter_out(r, d),
        compiler_params=_params("arbitrary", "arbitrary"))(_block_order(chip_masks) // 2, y_h, y_a, dxb)
    return own, rb


def _sum_chips_adamw(own, recv, w, m, v):
    r, c = w.shape
    tr = _tile(r, 128)

    def body(own_ref, rc_ref, w_ref, m_ref, v_ref, g_ref, d_ref, mo_ref, vo_ref):
        g = own_ref[...]
        for q in range(N_CHIPS - 1):
            g = g + rc_ref[q].astype(F32)
        g_ref[...] = g
        d_ref[...], mo_ref[...], vo_ref[...] = _adamw(w_ref[...], g, m_ref[...], v_ref[...])

    blk = pl.BlockSpec((tr, c), lambda i: (i, 0))
    shp = jax.ShapeDtypeStruct((r, c), F32)
    return pl.pallas_call(
        body, name="sum_chips_adamw", grid=(r // tr,), out_shape=(shp, shp, shp, shp),
        in_specs=[blk, pl.BlockSpec((N_CHIPS - 1, tr, c), lambda i: (0, i, 0)), blk, blk, blk],
        out_specs=(blk, blk, blk, blk), compiler_params=_params("parallel"))(own, recv, w, m, v)


SMALL_ROWS = 8
ROW_LB = 4
ROW_GN = 6
ROW_LOSS = 7


def _small_allreduce_adamw(part, w, m, v, lb_logits):
    width = part.shape[1]

    def body(p_ref, w_ref, m_ref, v_ref, lb_ref, g_ref, d_ref, mo_ref, vo_ref, buf, send_sems, recv_sems):
        x, y, c = _position()
        me = 4 * x + 2 * y + c
        buf[me] = p_ref[...]
        copies = []
        for k in range(N_DEV - 1):
            bx, by, bc = ((k + 1) >> 2) & 1, ((k + 1) >> 1) & 1, (k + 1) & 1
            peer = (x ^ bx, y ^ by, c ^ bc)
            copies.append(pltpu.make_async_remote_copy(
                src_ref=p_ref, dst_ref=buf.at[me], send_sem=send_sems.at[k], recv_sem=recv_sems.at[k],
                device_id=peer, device_id_type=MESH))
        for cp in copies:
            cp.start()
        for cp in copies:
            cp.wait_recv()
        for cp in copies:
            cp.wait_send()
        tot = buf[0]
        for dev in range(1, N_DEV):
            tot = tot + buf[dev]
        lbv = lb_ref[...]
        lb = _sigmoid(lbv[0:1] - lbv[1:2])
        glb = tot[ROW_LB:ROW_LB + 1] * lb * (1.0 - lb)
        loss = jnp.sum(tot[ROW_LOSS:ROW_LOSS + 1], axis=-1, keepdims=True)
        row = lax.broadcasted_iota(jnp.int32, (SMALL_ROWS, width), 0)
        g = jnp.where(row == ROW_LB, glb, jnp.where(row == ROW_LB + 1, -glb, tot))
        g = jnp.where(row == ROW_LOSS, loss, g)
        g_ref[...] = g
        d_ref[...], mo_ref[...], vo_ref[...] = _adamw(w_ref[...], g, m_ref[...], v_ref[...])

    vm = pl.BlockSpec(memory_space=pltpu.VMEM)
    shp = jax.ShapeDtypeStruct((SMALL_ROWS, width), F32)
    return pl.pallas_call(
        body, name="small_allreduce_adamw", out_shape=(shp, shp, shp, shp),
        in_specs=[vm] * 5, out_specs=(vm, vm, vm, vm),
        scratch_shapes=[pltpu.VMEM((N_DEV, SMALL_ROWS, width), F32), pltpu.SemaphoreType.DMA((N_DEV - 1,)),
                        pltpu.SemaphoreType.DMA((N_DEV - 1,))],
    )(part, w, m, v, lb_logits)


def _pack_small(norm_gain, final_gain, lb2, gnorm, last_row, width):
    pad = lambda a: jnp.pad(a.reshape(1, -1), ((0, 0), (0, width - a.size)))
    return jnp.concatenate([norm_gain.reshape(2, width), final_gain.reshape(2, width), lb2.reshape(2, width),
                            pad(gnorm), last_row.reshape(1, width)], axis=0)


def _unpack_small(p, d, e, hd):
    return (p[0:2].reshape(1, d), p[2:4].reshape(d), p[4:6].reshape(2, e), p[6:7, :hd].reshape(1, hd))


def kernel(x, norm_gain, w_in, lb_logits, hgrn_gnorm, w_out, final_gain, loss_target, m_norm_gain, m_w_in, m_lb_logits, m_hgrn_gnorm, m_w_out, m_final_gain, v_norm_gain, v_w_in, v_lb_logits, v_hgrn_gnorm, v_w_out, v_final_gain):
    s, d = x.shape[1], x.shape[2]
    e = w_in.shape[2]
    assert d == 2 * e and lb_logits.shape == (2, e) and w_out.shape[1] * N_DEV == 2 * e
    x2d = x.reshape(s, d)
    tgt = loss_target.reshape(s, d)

    h = _rmsnorm_fwd(x2d, norm_gain)
    z, w_in_full, w_out_full = _inproj_gather(h, _cast_bf16(w_in[0]), _cast_bf16(w_out[0]))
    w_out_full = w_out_full.reshape(2 * e, d)
    y_h, states = _hgrn_fwd(z, lb_logits, hgrn_gnorm)
    o_attn, lse, y_a = _attn_fwd(z)
    dx2, dx2b, dy, loss_vec, dfg = _outproj_loss(x2d, y_h, y_a, w_out_full, final_gain.reshape(1, d), tgt)

    own_o, recv_o = _dwout_scatter(y_h, y_a, dx2b)
    dza = _attn_bwd(z, dy, o_attn, lse)
    dzh, dlb, dgn = _hgrn_bwd(z, dy, states, lb_logits, hgrn_gnorm)
    grad_x, dng = _dh_dx(dzh, dza, w_in_full, x2d, norm_gain, dx2)
    own_i, recv_i = _dwin_scatter(h, dzh, dza)
    g_wi, d_wi, nm_wi, nv_wi = _sum_chips_adamw(own_i, recv_i, w_in[0], m_w_in[0], v_w_in[0])
    g_wo, d_wo, nm_wo, nv_wo = _sum_chips_adamw(own_o, recv_o, w_out[0], m_w_out[0], v_w_out[0])

    width = d // 2
    zero_row = jnp.zeros((1, width), F32)
    loss_row = loss_vec[:, :width] + loss_vec[:, width:]
    part = _pack_small(dng, dfg, jnp.concatenate([dlb, zero_row], axis=0), dgn, loss_row, width)
    pw = _pack_small(norm_gain, final_gain, lb_logits, hgrn_gnorm, zero_row, width)
    pm = _pack_small(m_norm_gain, m_final_gain, m_lb_logits, m_hgrn_gnorm, zero_row, width)
    pv = _pack_small(v_norm_gain, v_final_gain, v_lb_logits, v_hgrn_gnorm, zero_row, width)
    sg, sd, sm, sv = _small_allreduce_adamw(part, pw, pm, pv, lb_logits)
    hd = hgrn_gnorm.shape[1]
    g_ng, g_fg, g_lb, g_gn = _unpack_small(sg, d, e, hd)
    d_ng, d_fg, d_lb, d_gn = _unpack_small(sd, d, e, hd)
    m_ng, m_fg, m_lb, m_gn = _unpack_small(sm, d, e, hd)
    v_ng, v_fg, v_lb, v_gn = _unpack_small(sv, d, e, hd)
    loss = sg[ROW_LOSS, 0]

    one = lambda a: a[None]
    return (loss, grad_x.reshape(1, s, d), g_ng, one(g_wi), g_lb, g_gn, one(g_wo), g_fg,
            d_ng, one(d_wi), d_lb, d_gn, one(d_wo), d_fg,
            m_ng, one(nm_wi), m_lb, m_gn, one(nm_wo), m_fg,
            v_ng, one(nv_wi), v_lb, v_gn, one(nv_wo), v_fg)
```

```python
import functools
import math

import jax
import jax.numpy as jnp
from jax import lax
from jax.experimental import pallas as pl
from jax.experimental.pallas import tpu as pltpu

NORM_EPS = 1e-6
HGRN_HEAD = 128
HGRN_CHUNK = 64
ATTN_HEAD = 64
ATTN_BAND = 128
DILATIONS = (1, 4, 16)
N_SPLITS = 8
N_DEV = 8
ADAM_LR = 0.001
ADAM_B1 = 0.9
ADAM_B2 = 0.999
ADAM_EPS = 1e-08
ADAM_WD = 0.01
ADAM_STEP = 10
LANES = 128
MESH = pl.DeviceIdType.MESH
F32 = jnp.float32
BF16 = jnp.bfloat16
NEG_BIG = -1e30
VMEM_LIMIT = 56 * 1024 * 1024

ANY = pl.BlockSpec(memory_space=pl.ANY)


def _params(*sem):
    return pltpu.CompilerParams(dimension_semantics=sem, vmem_limit_bytes=VMEM_LIMIT)


def _tile(n, pref):
    t = min(n, pref)
    assert n % t == 0, (n, pref)
    return t


def _dot(a, b, precision=None):
    return jnp.dot(a, b, preferred_element_type=F32, precision=precision)


def _dot_nt(a, b):
    return lax.dot_general(a, b, (((1,), (1,)), ((), ())), preferred_element_type=F32)


def _dot_tn(a, b):
    return lax.dot_general(a, b, (((0,), (0,)), ((), ())), preferred_element_type=F32)


def _sigmoid(x):
    return 0.5 * jnp.tanh(0.5 * x) + 0.5


def _dsilu(x, s):
    return s * (1.0 + x * (1.0 - s))


def _adamw(w, g, m, v):
    m = ADAM_B1 * m + (1.0 - ADAM_B1) * g
    v = ADAM_B2 * v + (1.0 - ADAM_B2) * (g * g)
    m_hat = m / (1.0 - ADAM_B1 ** ADAM_STEP)
    v_hat = v / (1.0 - ADAM_B2 ** ADAM_STEP)
    delta = -ADAM_LR * (m_hat / (jnp.sqrt(v_hat) + ADAM_EPS) + ADAM_WD * w)
    return delta, m, v


def _cast_bf16(a):
    r, c = a.shape
    tr = _tile(r, 256)

    def body(a_ref, o_ref):
        o_ref[...] = a_ref[...].astype(BF16)

    return pl.pallas_call(
        body, name="cast_bf16", grid=(r // tr,), out_shape=jax.ShapeDtypeStruct((r, c), BF16),
        in_specs=[pl.BlockSpec((tr, c), lambda i: (i, 0))], out_specs=pl.BlockSpec((tr, c), lambda i: (i, 0)),
        compiler_params=_params("parallel"))(a)


def _rmsnorm_fwd(x, gain):
    s, d = x.shape
    tm = _tile(s, 512)

    def body(x_ref, g_ref, h_ref):
        xv = x_ref[...]
        r = lax.rsqrt(jnp.mean(xv * xv, axis=-1, keepdims=True) + NORM_EPS)
        h_ref[...] = (xv * r * g_ref[...]).astype(BF16)

    return pl.pallas_call(
        body, name="rmsnorm_fwd", grid=(s // tm,), out_shape=jax.ShapeDtypeStruct((s, d), BF16),
        in_specs=[pl.BlockSpec((tm, d), lambda i: (i, 0)), pl.BlockSpec((1, d), lambda i: (0, 0))],
        out_specs=pl.BlockSpec((tm, d), lambda i: (i, 0)), compiler_params=_params("parallel"))(x, gain)


HGRN_BLOCK = 2048
TRI_ROWS = 256


def _chunk_masks():
    tb = TRI_ROWS
    row = lax.broadcasted_iota(jnp.int32, (tb, tb), 0)
    col = lax.broadcasted_iota(jnp.int32, (tb, tb), 1)
    same = (row // HGRN_CHUNK) == (col // HGRN_CHUNK)
    lower = jnp.where(same & (col <= row), 1.0, 0.0).astype(BF16)
    upper = jnp.where(same & (col >= row), 1.0, 0.0).astype(BF16)
    return lower, upper


def _split3(a):
    hi = a.astype(BF16).astype(F32)
    mid = (a - hi).astype(BF16).astype(F32)
    lo = (a - hi - mid).astype(BF16).astype(F32)
    return hi, mid, lo


def _tri_dot(tri, x):
    hi, mid, lo = (p.astype(BF16) for p in _split3(x))
    outs = []
    for r in range(0, x.shape[0], TRI_ROWS):
        sl = slice(r, r + TRI_ROWS)
        outs.append(_dot(tri, hi[sl]) + _dot(tri, mid[sl]) + _dot(tri, lo[sl]))
    return outs[0] if len(outs) == 1 else jnp.concatenate(outs, axis=0)


def _hgrn_gates(qp, fp, lbv):
    lb = _sigmoid(lbv[0:1] - lbv[1:2])
    sq = _sigmoid(qp)
    q = qp * sq
    sg = _sigmoid(fp)
    f = lb + (1.0 - lb) * sg
    k = 1.0 - f
    return lb, sq, q, sg, f, k


def _hgrn_fwd(z, lb_logits, gnorm):
    s = z.shape[0]
    e = z.shape[1] // N_SPLITS
    nh = e // HGRN_HEAD
    tb = _tile(s, HGRN_BLOCK)
    nc = tb // HGRN_CHUNK
    nb = s // tb
    C = HGRN_CHUNK

    def body(q_ref, f_ref, i_ref, g_ref, lb_ref, gn_ref, y_ref, st_ref, state, o_scr):
        @pl.when(pl.program_id(1) == 0)
        def _():
            state[...] = jnp.zeros_like(state)

        lb, sq, q, sg, f, k = _hgrn_gates(q_ref[...], f_ref[...], lb_ref[...])
        lower, _ = _chunk_masks()
        b = _tri_dot(lower, jnp.log(f))
        b3 = b.reshape(nc, C, HGRN_HEAD)
        bc = b3[:, C - 1:C, :]
        qt = (q * jnp.exp(b)).astype(BF16)
        kt = (k * jnp.exp(-b)).astype(BF16)
        ke = (k.reshape(nc, C, HGRN_HEAD) * jnp.exp(bc - b3)).reshape(tb, HGRN_HEAD).astype(BF16)
        v = i_ref[...].astype(BF16)
        tri = lax.broadcasted_iota(jnp.int32, (C, C), 1) <= lax.broadcasted_iota(jnp.int32, (C, C), 0)
        sls = [slice(c * C, (c + 1) * C) for c in range(nc)]
        kv = [_dot_tn(v[sl], ke[sl]) for sl in sls]
        a = [jnp.where(tri, _dot_nt(qt[sl], kt[sl]), 0.0).astype(BF16) for sl in sls]
        st = state[...]
        sts = []
        for c in range(nc):
            sts.append(st)
            st_ref[c] = st
            st = st * jnp.exp(bc[c]) + kv[c]
        state[...] = st
        for c, sl in enumerate(sls):
            o_scr[sl, :] = _dot(a[c], v[sl]) + _dot_nt(qt[sl], sts[c].astype(BF16))
        o = o_scr[...]
        rms = lax.rsqrt(jnp.mean(o * o, axis=-1, keepdims=True) + NORM_EPS)
        gp = g_ref[...]
        y_ref[...] = (o * rms * gn_ref[...] * (gp * _sigmoid(gp))).astype(BF16)

    col = lambda kk: (lambda h, n: (n, kk * nh + h))
    return pl.pallas_call(
        body, name="hgrn_fwd", grid=(nh, nb),
        out_shape=(jax.ShapeDtypeStruct((s, e), BF16),
                   jax.ShapeDtypeStruct((nh, s // C, HGRN_HEAD, HGRN_HEAD), F32)),
        in_specs=[pl.BlockSpec((tb, HGRN_HEAD), col(0)), pl.BlockSpec((tb, HGRN_HEAD), col(1)),
                  pl.BlockSpec((tb, HGRN_HEAD), col(2)), pl.BlockSpec((tb, HGRN_HEAD), col(3)),
                  pl.BlockSpec((2, HGRN_HEAD), lambda h, n: (0, h)), pl.BlockSpec((1, HGRN_HEAD), lambda h, n: (0, 0))],
        out_specs=(pl.BlockSpec((tb, HGRN_HEAD), lambda h, n: (n, h)),
                   pl.BlockSpec((None, nc, HGRN_HEAD, HGRN_HEAD), lambda h, n: (h, n, 0, 0))),
        scratch_shapes=[pltpu.VMEM((HGRN_HEAD, HGRN_HEAD), F32), pltpu.VMEM((tb, HGRN_HEAD), F32)],
        compiler_params=_params("parallel", "arbitrary"))(z, z, z, z, lb_logits, gnorm)


def _hgrn_bwd(z, dy, states, lb_logits, gnorm):
    s = z.shape[0]
    e = z.shape[1] // N_SPLITS
    nh = e // HGRN_HEAD
    tb = _tile(s, HGRN_BLOCK)
    nc = tb // HGRN_CHUNK
    nb = s // tb
    C = HGRN_CHUNK
    H = HGRN_HEAD

    def body(q_ref, f_ref, i_ref, g_ref, dy_ref, st_ref, lb_ref, gn_ref, dz_ref, dlb_ref, dgn_ref,
             gstate, o_scr, dq_scr, dk_scr, dv_scr, e_scr):
        first = (pl.program_id(0) == 0) & (pl.program_id(1) == 0)

        @pl.when(first)
        def _():
            dgn_ref[...] = jnp.zeros_like(dgn_ref)

        @pl.when(pl.program_id(1) == 0)
        def _():
            gstate[...] = jnp.zeros_like(gstate)
            dlb_ref[...] = jnp.zeros_like(dlb_ref)

        qp = q_ref[...]
        lb, sq, q, sg, f, k = _hgrn_gates(qp, f_ref[...], lb_ref[...])
        lower, upper = _chunk_masks()
        b = _tri_dot(lower, jnp.log(f))
        b3 = b.reshape(nc, C, H)
        bc = b3[:, C - 1:C, :]
        eb = jnp.exp(b)
        enb = jnp.exp(-b)
        eend = jnp.exp(bc - b3).reshape(tb, H)
        qt = (q * eb).astype(BF16)
        kt = (k * enb).astype(BF16)
        ke = (k * eend).astype(BF16)
        v = i_ref[...].astype(BF16)
        tri = lax.broadcasted_iota(jnp.int32, (C, C), 1) <= lax.broadcasted_iota(jnp.int32, (C, C), 0)
        sls = [slice(c * C, (c + 1) * C) for c in range(nc)]
        a = [jnp.where(tri, _dot_nt(qt[sl], kt[sl]), 0.0).astype(BF16) for sl in sls]
        for c, sl in enumerate(sls):
            o_scr[sl, :] = _dot(a[c], v[sl]) + _dot_nt(qt[sl], st_ref[c].astype(BF16))
        o = o_scr[...]
        rms = lax.rsqrt(jnp.mean(o * o, axis=-1, keepdims=True) + NORM_EPS)
        on = o * rms
        gn = gn_ref[...]
        gp = g_ref[...]
        sgg = _sigmoid(gp)
        dyv = dy_ref[...]
        d_on = dyv * (gp * sgg)
        dz_ref[3] = (dyv * on * gn * _dsilu(gp, sgg)).astype(BF16)
        dgn_ref[...] += jnp.sum(d_on * on, axis=0, keepdims=True)
        u = d_on * gn
        do = (rms * (u - on * jnp.mean(u * on, axis=-1, keepdims=True))).astype(BF16)
        gup = [_dot_tn(do[sl], qt[sl]) for sl in sls]
        da = [jnp.where(tri, _dot_nt(do[sl], v[sl]), 0.0).astype(BF16) for sl in sls]
        gt = gstate[...]
        gts = [None] * nc
        for c in reversed(range(nc)):
            gts[c] = gt
            gt = gt * jnp.exp(bc[c]) + gup[c]
        gstate[...] = gt
        for c, sl in enumerate(sls):
            stp = st_ref[c]
            gtb = gts[c].astype(BF16)
            dqt = _dot(da[c], kt[sl]) + _dot(do[sl], stp.astype(BF16))
            dkt = _dot_tn(da[c], qt[sl])
            dks = _dot(v[sl], gtb) * eend[sl]
            dv_scr[sl, :] = _dot_tn(a[c], do[sl]) + _dot_nt(ke[sl], gtb)
            dq_scr[sl, :] = dqt * eb[sl]
            dk_scr[sl, :] = dkt * enb[sl] + dks
            ech = (jnp.sum(k[sl] * dks, axis=0, keepdims=True)
                   + jnp.sum(gts[c] * jnp.exp(bc[c]) * stp, axis=0, keepdims=True))
            e_scr[sl, :] = jnp.broadcast_to(ech, (C, H))
        dq = dq_scr[...]
        dk = dk_scr[...]
        dlf = _tri_dot(upper, q * dq - k * dk) + e_scr[...]
        dft = dlf / f - dk
        dz_ref[0] = (dq * _dsilu(qp, sq)).astype(BF16)
        dz_ref[1] = (dft * (1.0 - lb) * sg * (1.0 - sg)).astype(BF16)
        dz_ref[2] = dv_scr[...].astype(BF16)
        dlb_ref[...] += jnp.sum(dft * (1.0 - sg), axis=0, keepdims=True)

    col = lambda kk: (lambda h, n: (nb - 1 - n, kk * nh + h))
    return pl.pallas_call(
        body, name="hgrn_bwd", grid=(nh, nb),
        out_shape=(jax.ShapeDtypeStruct((4, s, e), BF16), jax.ShapeDtypeStruct((1, e), F32),
                   jax.ShapeDtypeStruct((1, H), F32)),
        in_specs=[pl.BlockSpec((tb, H), col(0)), pl.BlockSpec((tb, H), col(1)),
                  pl.BlockSpec((tb, H), col(2)), pl.BlockSpec((tb, H), col(3)),
                  pl.BlockSpec((tb, H), lambda h, n: (nb - 1 - n, h)),
                  pl.BlockSpec((None, nc, H, H), lambda h, n: (h, nb - 1 - n, 0, 0)),
                  pl.BlockSpec((2, H), lambda h, n: (0, h)), pl.BlockSpec((1, H), lambda h, n: (0, 0))],
        out_specs=(pl.BlockSpec((4, tb, H), lambda h, n: (0, nb - 1 - n, h)),
                   pl.BlockSpec((1, H), lambda h, n: (0, h)), pl.BlockSpec((1, H), lambda h, n: (0, 0))),
        scratch_shapes=[pltpu.VMEM((H, H), F32)] + [pltpu.VMEM((tb, H), F32)] * 5,
        compiler_params=_params("arbitrary", "arbitrary"))(z, z, z, z, dy, states, lb_logits, gnorm)


ATTN_T = 16 * ATTN_BAND
SCALE = ATTN_HEAD ** -0.5
TILE_UNROLL = 2


def _slope(hh, nheads):
    head = (2 * pl.program_id(0) + hh + 1).astype(F32)
    return jnp.exp(jnp.full((1, 1), -8.0 / nheads * math.log(2.0), F32) * head)


def _fill_bias(bias, nheads, delta, edge_ok):
    band = (delta >= 0) & (delta <= ATTN_BAND)
    dist = delta.astype(F32)
    for pi, dil in enumerate(DILATIONS):
        for hh in range(2):
            full = jnp.where(band, -(_slope(hh, nheads) * float(dil)) * dist, NEG_BIG)
            bias[(pi * 2 + hh) * 2] = full
            bias[(pi * 2 + hh) * 2 + 1] = jnp.where(edge_ok, full, NEG_BIG)


def _rows(start, size, stride):
    if stride == 1:
        return pl.ds(pl.multiple_of(start, ATTN_BAND), size)
    return pl.ds(start, size, stride=stride)


def _head_lanes(rows, hh):
    return (lax.broadcasted_iota(jnp.int32, (rows, LANES), 1) // ATTN_HEAD) == hh


def _attn_fwd(z):
    s = z.shape[0]
    e = z.shape[1] // N_SPLITS
    npair = e // LANES
    T = ATTN_T
    assert s % T == 0
    nsb = s // T
    W = ATTN_BAND
    nt = T // W
    HD = ATTN_HEAD
    chunk = 256

    def body(q_ref, kp_ref, kc_ref, vp_ref, vc_ref, g_ref, o_ref, l_ref, y_ref, qa, kbuf, va, bias, accs, ms, lsw):
        sb = pl.program_id(1)
        def stage(i, carry):
            rows = pl.ds(pl.multiple_of(i * chunk, chunk), chunk)
            upper = pl.ds(pl.multiple_of(T + i * chunk, chunk), chunk)
            kbuf[upper, :] = kc_ref[rows, :]
            for hh in range(2):
                mine = _head_lanes(chunk, hh)
                qa[hh, rows, :] = jnp.where(mine, q_ref[rows, :] * SCALE, 0.0)
                va[hh, upper, :] = jnp.where(mine, vc_ref[rows, :], 1.0)
            return carry

        lax.fori_loop(0, T // chunk, stage, 0)

        @pl.when(sb == 0)
        def _():
            def stage_prev(i, carry):
                rows = pl.ds(pl.multiple_of(i * chunk, chunk), chunk)
                kbuf[rows, :] = kp_ref[rows, :]
                for hh in range(2):
                    va[hh, rows, :] = jnp.where(_head_lanes(chunk, hh), vp_ref[rows, :], 1.0)
                return carry

            lax.fori_loop(0, T // chunk, stage_prev, 0)
        qi = lax.broadcasted_iota(jnp.int32, (W, 2 * W), 0)
        kj = lax.broadcasted_iota(jnp.int32, (W, 2 * W), 1)
        _fill_bias(bias, 2 * npair, W + qi - kj, kj >= W)

        def tile(tau, carry):
            first = _head_lanes(W, 0)
            rows, scores = [], []
            for pi, dil in enumerate(DILATIONS):
                r = tau % dil
                ub = tau // dil
                qrows = _rows(r + dil * W * ub, W, dil)
                krows = _rows(T + dil * W * (ub - 1) + r, 2 * W, dil)
                var = jnp.where((sb == 0) & (ub == 0), 1, 0)
                kt = kbuf[krows, :].astype(BF16)
                rows.append((qrows, krows))
                scores.append([_dot_nt(qa[hh, qrows, :].astype(BF16), kt) + bias[(pi * 2 + hh) * 2 + var]
                               for hh in range(2)])
            maxes = [[jnp.max(sc, axis=-1, keepdims=True) for sc in pair] for pair in scores]
            probs = [[jnp.exp(sc - m).astype(BF16) for sc, m in zip(ps, pm)] for ps, pm in zip(scores, maxes)]
            for pi, (qrows, krows) in enumerate(rows):
                outs = [_dot(probs[pi][hh], va[hh, krows, :].astype(BF16)) for hh in range(2)]
                accs[pi, qrows, :] = jnp.where(first, outs[0], outs[1])
                lsw[pi, qrows, :] = jnp.where(first, outs[1], outs[0])
                ms[pi, qrows, :] = jnp.where(first, maxes[pi][0], maxes[pi][1])
            return carry

        lax.fori_loop(0, nt, tile, 0, unroll=TILE_UNROLL)

        def merge(i, carry):
            rows = pl.ds(pl.multiple_of(i * chunk, chunk), chunk)
            m1, m2, m3 = ms[0, rows, :], ms[1, rows, :], ms[2, rows, :]
            mx = jnp.maximum(jnp.maximum(m1, m2), m3)
            w1, w2, w3 = jnp.exp(m1 - mx), jnp.exp(m2 - mx), jnp.exp(m3 - mx)
            unswap = lambda a: pltpu.roll(a, ATTN_HEAD, 1)
            den = w1 * unswap(lsw[0, rows, :]) + w2 * unswap(lsw[1, rows, :]) + w3 * unswap(lsw[2, rows, :])
            o = (w1 * accs[0, rows, :] + w2 * accs[1, rows, :] + w3 * accs[2, rows, :]) / den
            o_ref[rows, :] = o
            l_ref[rows, :] = mx + jnp.log(den)
            gp = g_ref[rows, :]
            y_ref[rows, :] = (o * (gp * _sigmoid(gp))).astype(BF16)
            upper = pl.ds(pl.multiple_of(T + i * chunk, chunk), chunk)
            kbuf[rows, :] = kbuf[upper, :]
            for hh in range(2):
                va[hh, rows, :] = va[hh, upper, :]
            return carry

        lax.fori_loop(0, T // chunk, merge, 0)

    cur = lambda split: (lambda hp, sb: (sb, split * npair + hp))
    prev = lambda split: (lambda hp, sb: (jnp.maximum(sb - 1, 0), split * npair + hp))
    blk = lambda index: pl.BlockSpec((T, LANES), index)
    out = blk(lambda hp, sb: (sb, hp))
    buf = lambda rows: pltpu.VMEM((rows, LANES), F32)
    return pl.pallas_call(
        body, name="attn_fwd", grid=(npair, nsb),
        out_shape=(jax.ShapeDtypeStruct((s, e), F32), jax.ShapeDtypeStruct((s, e), F32), jax.ShapeDtypeStruct((s, e), BF16)),
        in_specs=[blk(cur(4)), blk(prev(5)), blk(cur(5)), blk(prev(6)), blk(cur(6)), blk(cur(7))],
        out_specs=(out, out, out),
        scratch_shapes=[pltpu.VMEM((2, T, LANES), F32), buf(2 * T), pltpu.VMEM((2, 2 * T, LANES), F32),
                        pltpu.VMEM((12, W, 2 * W), F32)] + [pltpu.VMEM((3, T, LANES), F32)] * 3,
        compiler_params=_params("parallel", "arbitrary"))(z, z, z, z, z, z)


def _outproj_loss(x, y_h, y_a, w_out_full, final_gain, target):
    s, d = x.shape
    e = y_h.shape[1]
    tm = _tile(s, 256)

    def body(x_ref, yh_ref, ya_ref, w_ref, g_ref, t_ref, dx_ref, dxb_ref, dy_ref, loss_ref, dg_ref):
        @pl.when(pl.program_id(0) == 0)
        def _():
            loss_ref[...] = jnp.zeros_like(loss_ref)
            dg_ref[...] = jnp.zeros_like(dg_ref)

        w = w_ref[...]
        x2 = x_ref[...] + _dot(yh_ref[...], w[0:e]) + _dot(ya_ref[...], w[e:2 * e])
        r = lax.rsqrt(jnp.mean(x2 * x2, axis=-1, keepdims=True) + NORM_EPS)
        xn = x2 * r
        g = g_ref[...]
        err = xn * g - t_ref[...]
        loss_ref[...] += jnp.sum(err * err, axis=0, keepdims=True) * (0.5 / d)
        dyo = err * (1.0 / d)
        dg_ref[...] += jnp.sum(dyo * xn, axis=0, keepdims=True)
        u = dyo * g
        dx2 = r * (u - xn * jnp.mean(u * xn, axis=-1, keepdims=True))
        dx_ref[...] = dx2
        dxb = dx2.astype(BF16)
        dxb_ref[...] = dxb
        dy_ref[...] = _dot_nt(dxb, w)

    row = pl.BlockSpec((tm, d), lambda i: (i, 0))
    half = pl.BlockSpec((tm, e), lambda i: (i, 0))
    vec = pl.BlockSpec((1, d), lambda i: (0, 0))
    return pl.pallas_call(
        body, name="outproj_loss", grid=(s // tm,),
        out_shape=(jax.ShapeDtypeStruct((s, d), F32), jax.ShapeDtypeStruct((s, d), BF16),
                   jax.ShapeDtypeStruct((s, 2 * e), F32), jax.ShapeDtypeStruct((1, d), F32),
                   jax.ShapeDtypeStruct((1, d), F32)),
        in_specs=[row, half, half, pl.BlockSpec((2 * e, d), lambda i: (0, 0)), vec, row],
        out_specs=(row, row, pl.BlockSpec((tm, 2 * e), lambda i: (i, 0)), vec, vec),
        compiler_params=_params("arbitrary"))(x, y_h, y_a, w_out_full, final_gain, target)


def _attn_bwd(z, dy, o, lse):
    s, e = o.shape
    npair = e // LANES
    T = ATTN_T
    assert s % T == 0
    nsb = s // T
    W = ATTN_BAND
    nt = T // W
    HD = ATTN_HEAD
    chunk = 256

    def body(k_ref, v_ref, qc_ref, qn_ref, dyc_ref, dyn_ref, gc_ref, gn_ref, oc_ref, on_ref, lc_ref, ln_ref,
             dz_ref, qa, doa, ka, va, dqacc, dkacc, dvacc, bias):
        sb = pl.program_id(1)
        def stage_queries(half, q_r, dy_r, g_r, o_r, l_r):
            def stage(i, carry):
                rows = pl.ds(pl.multiple_of(i * chunk, chunk), chunk)
                dst = pl.ds(pl.multiple_of(half * T + i * chunk, chunk), chunk)
                lane = lax.broadcasted_iota(jnp.int32, (chunk, LANES), 1)
                gp = g_r[rows, :]
                dov = dy_r[rows, :] * (gp * _sigmoid(gp))
                qv = q_r[rows, :] * SCALE
                same_head = (lax.broadcasted_iota(jnp.int32, (LANES, LANES), 0) // HD
                             == lax.broadcasted_iota(jnp.int32, (LANES, LANES), 1) // HD)
                ones = jnp.where(same_head, 1.0, 0.0).astype(BF16)
                hi, mid, lo = (p.astype(BF16) for p in _split3(dov * o_r[rows, :]))
                delta = _dot(hi, ones) + _dot(mid, ones) + _dot(lo, ones)
                swap = lambda a: pltpu.roll(a, HD, 1)
                lse_parts = [swap(p) for p in _split3(l_r[rows, :])]
                dl_parts = [swap(p) for p in _split3(delta)]
                for hh in range(2):
                    mine = _head_lanes(chunk, hh)
                    spare = (1 - hh) * HD
                    qh = jnp.where(mine, qv, 0.0)
                    dh = jnp.where(mine, dov, 0.0)
                    for j in range(3):
                        qh = jnp.where(lane == spare + j, lse_parts[j], qh)
                        dh = jnp.where(lane == spare + j, dl_parts[j], dh)
                    qa[hh, dst, :] = qh
                    doa[hh, dst, :] = dh
                return carry

            lax.fori_loop(0, T // chunk, stage, 0)

        @pl.when(sb == 0)
        def _():
            stage_queries(0, qc_ref, dyc_ref, gc_ref, oc_ref, lc_ref)

        stage_queries(1, qn_ref, dyn_ref, gn_ref, on_ref, ln_ref)

        def stage_keys(i, carry):
            rows = pl.ds(pl.multiple_of(i * chunk, chunk), chunk)
            lane = lax.broadcasted_iota(jnp.int32, (chunk, LANES), 1)
            for hh in range(2):
                spare = (1 - hh) * HD
                minus = (lane >= spare) & (lane < spare + 3)
                ka[hh, rows, :] = jnp.where(minus, -1.0, k_ref[rows, :])
                va[hh, rows, :] = jnp.where(minus, -1.0, v_ref[rows, :])
            gp = gc_ref[rows, :]
            dz_ref[3, rows, :] = (dyc_ref[rows, :] * oc_ref[rows, :] * _dsilu(gp, _sigmoid(gp))).astype(BF16)
            return carry

        lax.fori_loop(0, T // chunk, stage_keys, 0)

        @pl.when(sb == 0)
        def _():
            dqacc[0:T, :] = jnp.zeros((T, LANES), F32)

        dqacc[T:, :] = jnp.zeros((T, LANES), F32)
        dkacc[...] = jnp.zeros_like(dkacc)
        dvacc[...] = jnp.zeros_like(dvacc)
        qi = lax.broadcasted_iota(jnp.int32, (2 * W, W), 0)
        kj = lax.broadcasted_iota(jnp.int32, (2 * W, W), 1)
        _fill_bias(bias, 2 * npair, qi - kj, qi < W)

        def tile(tau, carry):
            def scores(step, pi):
                dil = DILATIONS[pi]
                r = step % dil
                ub = step // dil
                start = r + dil * W * ub
                krows = _rows(start, W, dil)
                qrows = _rows(start, 2 * W, dil)
                var = jnp.where((sb == nsb - 1) & (ub == nt // dil - 1), 1, 0)
                unit = dict(krows=krows, qrows=qrows, ops=[], sc=[], dpd=[])
                for hh in range(2):
                    kt = ka[hh, krows, :].astype(BF16)
                    vt = va[hh, krows, :].astype(BF16)
                    qt = qa[hh, qrows, :].astype(BF16)
                    dt = doa[hh, qrows, :].astype(BF16)
                    unit["ops"].append((kt, qt, dt))
                    unit["sc"].append(_dot_nt(qt, kt) + bias[(pi * 2 + hh) * 2 + var])
                    unit["dpd"].append(_dot_nt(dt, vt))
                return unit

            def elementwise(unit):
                ps = [jnp.exp(s_) for s_ in unit["sc"]]
                unit["ds"] = [(p * d).astype(BF16) for p, d in zip(ps, unit["dpd"])]
                unit["pb"] = [p.astype(BF16) for p in ps]

            def products(unit):
                dvs = [_dot_tn(pb, dt) for pb, (kt, qt, dt) in zip(unit["pb"], unit["ops"])]
                dks = [_dot_tn(ds, qt) for ds, (kt, qt, dt) in zip(unit["ds"], unit["ops"])]
                dqs = [_dot(ds, kt) for ds, (kt, qt, dt) in zip(unit["ds"], unit["ops"])]
                dkacc[unit["krows"], :] += jnp.where(_head_lanes(W, 0), dks[0], dks[1])
                dvacc[unit["krows"], :] += jnp.where(_head_lanes(W, 0), dvs[0], dvs[1])
                dqacc[unit["qrows"], :] += jnp.where(_head_lanes(2 * W, 0), dqs[0], dqs[1]) * SCALE

            order = [(2 * tau + half, pi) for half in range(2) for pi in range(len(DILATIONS))]
            units = [None] * len(order)
            for n in range(len(order) + 2):
                if n < len(order):
                    units[n] = scores(*order[n])
                if 1 <= n <= len(order):
                    elementwise(units[n - 1])
                if n >= 2:
                    products(units[n - 2])
            return carry

        lax.fori_loop(0, nt // 2, tile, 0)

        def flush(i, carry):
            rows = pl.ds(pl.multiple_of(i * chunk, chunk), chunk)
            nxt = pl.ds(pl.multiple_of(T + i * chunk, chunk), chunk)
            dz_ref[0, rows, :] = dqacc[rows, :].astype(BF16)
            dz_ref[1, rows, :] = dkacc[rows, :].astype(BF16)
            dz_ref[2, rows, :] = dvacc[rows, :].astype(BF16)
            dqacc[rows, :] = dqacc[nxt, :]
            for hh in range(2):
                qa[hh, rows, :] = qa[hh, nxt, :]
                doa[hh, rows, :] = doa[hh, nxt, :]
            return carry

        lax.fori_loop(0, T // chunk, flush, 0)

    zc = lambda split: (lambda hp, sb: (sb, split * npair + hp))
    zn = lambda split: (lambda hp, sb: (jnp.minimum(sb + 1, nsb - 1), split * npair + hp))
    ec = lambda off: (lambda hp, sb: (sb, off + hp))
    en = lambda off: (lambda hp, sb: (jnp.minimum(sb + 1, nsb - 1), off + hp))
    blk = lambda index: pl.BlockSpec((T, LANES), index)
    buf = lambda rows: pltpu.VMEM((rows, LANES), F32)
    return pl.pallas_call(
        body, name="attn_bwd", grid=(npair, nsb), out_shape=jax.ShapeDtypeStruct((4, s, e), BF16),
        in_specs=[blk(zc(5)), blk(zc(6)), blk(zc(4)), blk(zn(4)), blk(ec(npair)), blk(en(npair)),
                  blk(zc(7)), blk(zn(7)), blk(ec(0)), blk(en(0)), blk(ec(0)), blk(en(0))],
        out_specs=pl.BlockSpec((4, T, LANES), lambda hp, sb: (0, sb, hp)),
        scratch_shapes=[pltpu.VMEM((2, 2 * T, LANES), F32), pltpu.VMEM((2, 2 * T, LANES), F32),
                        pltpu.VMEM((2, T, LANES), F32), pltpu.VMEM((2, T, LANES), F32),
                        buf(2 * T), buf(T), buf(T), pltpu.VMEM((12, 2 * W, W), F32)],
        compiler_params=_params("parallel", "arbitrary"))(z, z, z, z, dy, dy, z, z, o, o, lse, lse)


def _dz_specs(tm, e, axis):
    def mk(lo, hi):
        def index(i, k):
            row, grp = (i, k) if axis == 1 else (k, i)
            return (jnp.clip(grp - lo, 0, hi - lo - 1), row, 0)
        return pl.BlockSpec((None, tm, e), index)
    return [mk(0, 4), mk(4, 8)]


def _dz_pick(grp, dzh_ref, dza_ref, fn):
    @pl.when(grp < 4)
    def _():
        fn(dzh_ref[...])

    @pl.when(grp >= 4)
    def _():
        fn(dza_ref[...])


def _dh_dx(dzh, dza, w_full, x, gain, dx2):
    s, d = x.shape
    e = dzh.shape[2]
    tm = _tile(s, 1024)
    ni = s // tm
    chunk = _tile(tm, 256)
    fetch_at = 2

    def body(dzh_ref, dza_ref, w_ref, x_hbm, g_ref, dx2_hbm, gx_hbm, dg_ref, acc, xbuf, dbuf, sems):
        i, k = pl.program_id(0), pl.program_id(1)
        tile_rows = pl.ds(pl.multiple_of(i * tm, tm), tm)
        fetch_x = pltpu.make_async_copy(x_hbm.at[tile_rows, :], xbuf, sems.at[0])
        fetch_d = pltpu.make_async_copy(dx2_hbm.at[tile_rows, :], dbuf, sems.at[1])
        store = pltpu.make_async_copy(xbuf, gx_hbm.at[tile_rows, :], sems.at[2])

        @pl.when((i == 0) & (k == 0))
        def _():
            dg_ref[...] = jnp.zeros_like(dg_ref)

        @pl.when(k == 0)
        def _():
            acc[...] = jnp.zeros_like(acc)

        @pl.when((k == fetch_at) & (i > 0))
        def _():
            store.wait()

        @pl.when(k == fetch_at)
        def _():
            fetch_x.start()
            fetch_d.start()

        def add(dz):
            acc[...] += _dot_nt(dz, w_ref[...])

        _dz_pick(k, dzh_ref, dza_ref, add)

        @pl.when(k == N_SPLITS - 1)
        def _():
            fetch_x.wait()
            fetch_d.wait()
            gain_row = g_ref[...]

            def finish(c, dg):
                rows = pl.ds(pl.multiple_of(c * chunk, chunk), chunk)
                dh = acc[rows, :]
                xv = xbuf[rows, :]
                r = lax.rsqrt(jnp.mean(xv * xv, axis=-1, keepdims=True) + NORM_EPS)
                xn = xv * r
                u = dh * gain_row
                xbuf[rows, :] = dbuf[rows, :] + r * (u - xn * jnp.mean(u * xn, axis=-1, keepdims=True))
                return dg + jnp.sum(dh * xn, axis=0, keepdims=True)

            dg_ref[...] += lax.fori_loop(0, tm // chunk, finish, jnp.zeros((1, d), F32))
            store.start()

        @pl.when((k == N_SPLITS - 1) & (i == ni - 1))
        def _():
            store.wait()

    vec = pl.BlockSpec((1, d), lambda i, k: (0, 0))
    return pl.pallas_call(
        body, name="dh_dx", grid=(ni, N_SPLITS),
        out_shape=(jax.ShapeDtypeStruct((s, d), F32), jax.ShapeDtypeStruct((1, d), F32)),
        in_specs=_dz_specs(tm, e, 1) + [pl.BlockSpec((None, d, e), lambda i, k: (k, 0, 0)), ANY, vec, ANY],
        out_specs=(ANY, vec),
        scratch_shapes=[pltpu.VMEM((tm, d), F32), pltpu.VMEM((tm, d), F32), pltpu.VMEM((tm, d), F32),
                        pltpu.SemaphoreType.DMA((3,))],
        compiler_params=_params("arbitrary", "arbitrary"))(dzh, dza, w_full, x, gain, dx2)


def _position():
    x, y, c = lax.axis_index("x"), lax.axis_index("y"), lax.axis_index("c")
    return x, y, c


def _xor_peer(x, y, c, mask):
    return (x ^ ((mask >> 2) & 1), y ^ ((mask >> 1) & 1), c ^ (mask & 1))


def _block_order(masks):
    me = 4 * lax.axis_index("x") + 2 * lax.axis_index("y") + lax.axis_index("c")
    return jnp.stack([me ^ m for m in masks]).astype(jnp.int32)


GATHER_MASKS = (0, 1, 4, 2, 6, 5, 3, 7)


def _inproj_gather(h, w_loc, wo_loc):
    s, d = h.shape
    e = w_loc.shape[1]
    tm = _tile(s, 1024)
    ni = s // tm
    pre = max(ni - 2, 0)

    def body(order_ref, h_ref, w_ref, wo_ref, z_ref, wf_ref, wof_ref, wbuf, send_sems, recv_sems, osend, orecv,
             local_sems, wsems):
        j, i = pl.program_id(0), pl.program_id(1)
        x, y, c = _position()
        me, sibling = (x, y, c), (x, y, 1 - c)
        chips = [(1 - x, y), (x, 1 - y), (1 - x, 1 - y)]
        blk = lambda p: 4 * p[0] + 2 * p[1] + p[2]

        def copy(k, block, to, src=None):
            dst = wf_ref.at[blk(block)]
            return pltpu.make_async_remote_copy(
                src_ref=dst if src is None else src, dst_ref=dst, send_sem=send_sems.at[k], recv_sem=recv_sems.at[k],
                device_id=to, device_id_type=MESH)

        first = [copy(0, me, sibling, src=w_ref)] + [copy(1 + q, me, (*chip, c), src=w_ref) for q, chip in enumerate(chips)]
        passed = [copy(4 + q, (*chip, c), sibling) for q, chip in enumerate(chips)]
        mine = pltpu.make_async_copy(w_ref, wf_ref.at[blk(me)], local_sems.at[0])
        ocopies = [pltpu.make_async_remote_copy(
            src_ref=wo_ref, dst_ref=wof_ref.at[blk(me)], send_sem=osend.at[k], recv_sem=orecv.at[k],
            device_id=_xor_peer(x, y, c, k + 1), device_id_type=MESH) for k in range(N_DEV - 1)]
        omine = pltpu.make_async_copy(wo_ref, wof_ref.at[blk(me)], local_sems.at[1])
        blocks = [me, sibling] + [(*chip, c) for chip in chips] + [(*chip, 1 - c) for chip in chips]
        arrive = [None, copy(0, sibling, me)] + [copy(1 + q, (*chip, c), me) for q, chip in enumerate(chips)] \
            + [copy(4 + q, (*chip, 1 - c), me) for q, chip in enumerate(chips)]
        forward = [None, None] + passed + [None, None, None]

        def load(slot, src):
            return pltpu.make_async_copy(src, wbuf.at[slot], wsems.at[slot])

        @pl.when((j == 0) & (i == 0))
        def _():
            for cp in [mine, omine] + first + ocopies:
                cp.start()
            load(0, w_ref).start()

        for jj in range(N_DEV):
            @pl.when((j == jj) & (i == 0))
            def _():
                load(jj % 2, w_ref).wait()

            if jj + 1 < N_DEV:
                @pl.when((j == jj) & (i == pre))
                def _():
                    arrive[jj + 1].wait_recv()
                    if forward[jj + 1] is not None:
                        forward[jj + 1].start()
                    load((jj + 1) % 2, wf_ref.at[blk(blocks[jj + 1])]).start()

        z_ref[...] = _dot(h_ref[...], wbuf[j % 2])

        @pl.when((j == N_DEV - 1) & (i == ni - 1))
        def _():
            for cp in first + passed:
                cp.wait_send()
            for cp in ocopies:
                cp.wait_send()
                cp.wait_recv()
            mine.wait()
            omine.wait()

    grid_spec = pltpu.PrefetchScalarGridSpec(
        num_scalar_prefetch=1, grid=(N_DEV, ni),
        in_specs=[pl.BlockSpec((tm, d), lambda j, i, o: (i, 0)), ANY, ANY],
        out_specs=(pl.BlockSpec((tm, e), lambda j, i, o: (i, o[j])), ANY, ANY),
        scratch_shapes=[pltpu.VMEM((2, d, e), BF16), pltpu.SemaphoreType.DMA((7,)), pltpu.SemaphoreType.DMA((7,)),
                        pltpu.SemaphoreType.DMA((7,)), pltpu.SemaphoreType.DMA((7,)), pltpu.SemaphoreType.DMA((2,)),
                        pltpu.SemaphoreType.DMA((2,))])
    return pl.pallas_call(
        body, name="inproj_gather", grid_spec=grid_spec,
        out_shape=(jax.ShapeDtypeStruct((s, N_SPLITS * e), F32), jax.ShapeDtypeStruct((N_DEV, d, e), BF16),
                   jax.ShapeDtypeStruct((N_DEV,) + wo_loc.shape, BF16)),
        compiler_params=_params("arbitrary", "arbitrary"))(_block_order(GATHER_MASKS), h, w_loc, wo_loc)


SCATTER_MASKS = (7, 6, 5, 4, 3, 2, 1, 0)
N_CHIPS = 4


def _scatter_block(k, acc, stage, tmp, own_ref, ra_ref, rb_ref, sa_send, sa_recv, sb_send, sb_recv, loc_sem, last):
    x, y, c = _position()
    chip_of = lambda t: _xor_peer(x, y, c, SCATTER_MASKS[2 * t + 1])

    def ship(t):
        return pltpu.make_async_remote_copy(
            src_ref=stage.at[0], dst_ref=ra_ref.at[t], send_sem=sa_send.at[t], recv_sem=sa_recv.at[t],
            device_id=(x, y, 1 - c), device_id_type=MESH)

    def send(t):
        return pltpu.make_async_remote_copy(
            src_ref=stage.at[1], dst_ref=rb_ref.at[t], send_sem=sb_send.at[t], recv_sem=sb_recv.at[t],
            device_id=chip_of(t), device_id_type=MESH)

    for kk in range(N_DEV):
        t = kk // 2

        @pl.when(last & (k == kk))
        def _():
            if kk % 2 == 0:
                if t >= 1:
                    ship(t - 1).wait_send()
                stage[0] = acc[...].astype(BF16)
                ship(t).start()
            else:
                ship(t).wait_recv()
                fetch = pltpu.make_async_copy(ra_ref.at[t], tmp, loc_sem)
                fetch.start()
                fetch.wait()
                acc[...] += tmp[...].astype(F32)
                if t < N_CHIPS - 1:
                    if t >= 1:
                        send(t - 1).wait_send()
                    stage[1] = acc[...].astype(BF16)
                    send(t).start()
                else:
                    keep = pltpu.make_async_copy(acc, own_ref, loc_sem)
                    keep.start()
                    keep.wait()
                    ship(t).wait_send()
                    send(t - 1).wait_send()
                    for q in range(N_CHIPS - 1):
                        send(q).wait_recv()


def _scatter_scratch(rows, cols):
    return [pltpu.VMEM((rows, cols), F32), pltpu.VMEM((2, rows, cols), BF16), pltpu.VMEM((rows, cols), BF16),
            pltpu.SemaphoreType.DMA((N_CHIPS,)), pltpu.SemaphoreType.DMA((N_CHIPS,)),
            pltpu.SemaphoreType.DMA((N_CHIPS - 1,)), pltpu.SemaphoreType.DMA((N_CHIPS - 1,)), pltpu.SemaphoreType.DMA(())]


def _scatter_out(rows, cols):
    return (jax.ShapeDtypeStruct((rows, cols), F32), jax.ShapeDtypeStruct((N_CHIPS, rows, cols), BF16),
            jax.ShapeDtypeStruct((N_CHIPS - 1, rows, cols), BF16))


def _dwin_scatter(h, dzh, dza):
    s, d = h.shape
    e = dzh.shape[2]
    ts = _tile(s, 1024)
    ns = s // ts

    def body(order_ref, dzh_ref, dza_ref, h_ref, own_ref, ra_ref, rb_ref, acc, stage, tmp, *sems):
        k, step = pl.program_id(0), pl.program_id(1)

        @pl.when(step == 0)
        def _():
            acc[...] = jnp.zeros_like(acc)

        def add(dz):
            acc[...] += _dot_tn(h_ref[...], dz)

        _dz_pick(order_ref[k], dzh_ref, dza_ref, add)
        _scatter_block(k, acc, stage, tmp, own_ref, ra_ref, rb_ref, *sems, step == ns - 1)

    def dz_spec(lo):
        return pl.BlockSpec((None, ts, e), lambda k, st, o: (jnp.clip(o[k] - lo, 0, 3), st, 0))

    grid_spec = pltpu.PrefetchScalarGridSpec(
        num_scalar_prefetch=1, grid=(N_DEV, ns),
        in_specs=[dz_spec(0), dz_spec(4), pl.BlockSpec((ts, d), lambda k, st, o: (st, 0))],
        out_specs=(ANY, ANY, ANY), scratch_shapes=_scatter_scratch(d, e))
    own, _, rb = pl.pallas_call(
        body, name="dwin_scatter", grid_spec=grid_spec, out_shape=_scatter_out(d, e),
        compiler_params=_params("arbitrary", "arbitrary"))(_block_order(SCATTER_MASKS), dzh, dza, h)
    return own, rb


def _dwout_scatter(y_h, y_a, dxb):
    s, e = y_h.shape
    d = dxb.shape[1]
    r = 2 * e // N_DEV
    pairs = e // (2 * r)
    ts = _tile(s, 1024)
    ns = s // ts
    chip_masks = SCATTER_MASKS[1::2]

    def body(pair_ref, yh_ref, ya_ref, dx_ref, own_ref, ra_ref, rb_ref, acc, pend, stage, tmp,
             sa_send, sa_recv, sb_send, sb_recv, loc_sem):
        t, step = pl.program_id(0), pl.program_id(1)
        x, y, c = _position()

        @pl.when(step == 0)
        def _():
            acc[...] = jnp.zeros_like(acc)

        @pl.when(pair_ref[t] < pairs)
        def _():
            acc[...] += _dot_tn(yh_ref[...], dx_ref[...])

        @pl.when(pair_ref[t] >= pairs)
        def _():
            acc[...] += _dot_tn(ya_ref[...], dx_ref[...])

        mine = pl.ds(pl.multiple_of(c * r, r), r)
        other = pl.ds(pl.multiple_of((1 - c) * r, r), r)

        def ship(q):
            return pltpu.make_async_remote_copy(
                src_ref=stage.at[0], dst_ref=ra_ref.at[q], send_sem=sa_send.at[q], recv_sem=sa_recv.at[q],
                device_id=(x, y, 1 - c), device_id_type=MESH)

        def send(q):
            return pltpu.make_async_remote_copy(
                src_ref=stage.at[1], dst_ref=rb_ref.at[q], send_sem=sb_send.at[q], recv_sem=sb_recv.at[q],
                device_id=_xor_peer(x, y, c, chip_masks[q]), device_id_type=MESH)

        def sibling_share(q):
            ship(q).wait_recv()
            fetch = pltpu.make_async_copy(ra_ref.at[q], tmp, loc_sem)
            fetch.start()
            fetch.wait()
            return tmp[...].astype(F32)

        for tt in range(N_CHIPS):
            @pl.when((step == ns - 1) & (t == tt))
            def _():
                if tt >= 1:
                    ship(tt - 1).wait_send()
                stage[0] = acc[other, :].astype(BF16)
                ship(tt).start()
                if tt >= 1:
                    if tt >= 2:
                        send(tt - 2).wait_send()
                    stage[1] = (pend[...] + sibling_share(tt - 1)).astype(BF16)
                    send(tt - 1).start()
                if tt < N_CHIPS - 1:
                    pend[...] = acc[mine, :]
                else:
                    pend[...] = acc[mine, :] + sibling_share(tt)
                    keep = pltpu.make_async_copy(pend, own_ref, loc_sem)
                    keep.start()
                    keep.wait()
                    ship(tt).wait_send()
                    send(tt - 1).wait_send()
                    for q in range(N_CHIPS - 1):
                        send(q).wait_recv()

    def y_spec(lo):
        return pl.BlockSpec((ts, 2 * r), lambda t, st, o: (st, jnp.clip(o[t] - lo, 0, pairs - 1)))

    grid_spec = pltpu.PrefetchScalarGridSpec(
        num_scalar_prefetch=1, grid=(N_CHIPS, ns),
        in_specs=[y_spec(0), y_spec(pairs), pl.BlockSpec((ts, d), lambda t, st, o: (st, 0))],
        out_specs=(ANY, ANY, ANY),
        scratch_shapes=[pltpu.VMEM((2 * r, d), F32), pltpu.VMEM((r, d), F32)] + _scatter_scratch(r, d)[1:])
    own, _, rb = pl.pallas_call(
        body, name="dwout_scatter", grid_spec=grid_spec, out_shape=_scatter_out(r, d),
        compiler_params=_params("arbitrary", "arbitrary"))(_block_order(chip_masks) // 2, y_h, y_a, dxb)
    return own, rb


def _sum_chips_adamw(own, recv, w, m, v):
    r, c = w.shape
    tr = _tile(r, 128)

    def body(own_ref, rc_ref, w_ref, m_ref, v_ref, g_ref, d_ref, mo_ref, vo_ref):
        g = own_ref[...]
        for q in range(N_CHIPS - 1):
            g = g + rc_ref[q].astype(F32)
        g_ref[...] = g
        d_ref[...], mo_ref[...], vo_ref[...] = _adamw(w_ref[...], g, m_ref[...], v_ref[...])

    blk = pl.BlockSpec((tr, c), lambda i: (i, 0))
    shp = jax.ShapeDtypeStruct((r, c), F32)
    return pl.pallas_call(
        body, name="sum_chips_adamw", grid=(r // tr,), out_shape=(shp, shp, shp, shp),
        in_specs=[blk, pl.BlockSpec((N_CHIPS - 1, tr, c), lambda i: (0, i, 0)), blk, blk, blk],
        out_specs=(blk, blk, blk, blk), compiler_params=_params("parallel"))(own, recv, w, m, v)


SMALL_ROWS = 8
ROW_LB = 4
ROW_GN = 6
ROW_LOSS = 7


def _small_allreduce_adamw(part, w, m, v, lb_logits):
    width = part.shape[1]

    def body(p_ref, w_ref, m_ref, v_ref, lb_ref, g_ref, d_ref, mo_ref, vo_ref, buf, send_sems, recv_sems):
        x, y, c = _position()
        me = 4 * x + 2 * y + c
        buf[me] = p_ref[...]
        copies = []
        for k in range(N_DEV - 1):
            bx, by, bc = ((k + 1) >> 2) & 1, ((k + 1) >> 1) & 1, (k + 1) & 1
            peer = (x ^ bx, y ^ by, c ^ bc)
            copies.append(pltpu.make_async_remote_copy(
                src_ref=p_ref, dst_ref=buf.at[me], send_sem=send_sems.at[k], recv_sem=recv_sems.at[k],
                device_id=peer, device_id_type=MESH))
        for cp in copies:
            cp.start()
        for cp in copies:
            cp.wait_recv()
        for cp in copies:
            cp.wait_send()
        tot = buf[0]
        for dev in range(1, N_DEV):
            tot = tot + buf[dev]
        lbv = lb_ref[...]
        lb = _sigmoid(lbv[0:1] - lbv[1:2])
        glb = tot[ROW_LB:ROW_LB + 1] * lb * (1.0 - lb)
        loss = jnp.sum(tot[ROW_LOSS:ROW_LOSS + 1], axis=-1, keepdims=True)
        row = lax.broadcasted_iota(jnp.int32, (SMALL_ROWS, width), 0)
        g = jnp.where(row == ROW_LB, glb, jnp.where(row == ROW_LB + 1, -glb, tot))
        g = jnp.where(row == ROW_LOSS, loss, g)
        g_ref[...] = g
        d_ref[...], mo_ref[...], vo_ref[...] = _adamw(w_ref[...], g, m_ref[...], v_ref[...])

    vm = pl.BlockSpec(memory_space=pltpu.VMEM)
    shp = jax.ShapeDtypeStruct((SMALL_ROWS, width), F32)
    return pl.pallas_call(
        body, name="small_allreduce_adamw", out_shape=(shp, shp, shp, shp),
        in_specs=[vm] * 5, out_specs=(vm, vm, vm, vm),
        scratch_shapes=[pltpu.VMEM((N_DEV, SMALL_ROWS, width), F32), pltpu.SemaphoreType.DMA((N_DEV - 1,)),
                        pltpu.SemaphoreType.DMA((N_DEV - 1,))],
    )(part, w, m, v, lb_logits)


def _pack_small(norm_gain, final_gain, lb2, gnorm, last_row, width):
    pad = lambda a: jnp.pad(a.reshape(1, -1), ((0, 0), (0, width - a.size)))
    return jnp.concatenate([norm_gain.reshape(2, width), final_gain.reshape(2, width), lb2.reshape(2, width),
                            pad(gnorm), last_row.reshape(1, width)], axis=0)


def _unpack_small(p, d, e, hd):
    return (p[0:2].reshape(1, d), p[2:4].reshape(d), p[4:6].reshape(2, e), p[6:7, :hd].reshape(1, hd))


def kernel(x, norm_gain, w_in, lb_logits, hgrn_gnorm, w_out, final_gain, loss_target, m_norm_gain, m_w_in, m_lb_logits, m_hgrn_gnorm, m_w_out, m_final_gain, v_norm_gain, v_w_in, v_lb_logits, v_hgrn_gnorm, v_w_out, v_final_gain):
    s, d = x.shape[1], x.shape[2]
    e = w_in.shape[2]
    assert d == 2 * e and lb_logits.shape == (2, e) and w_out.shape[1] * N_DEV == 2 * e
    x2d = x.reshape(s, d)
    tgt = loss_target.reshape(s, d)

    h = _rmsnorm_fwd(x2d, norm_gain)
    z, w_in_full, w_out_full = _inproj_gather(h, _cast_bf16(w_in[0]), _cast_bf16(w_out[0]))
    w_out_full = w_out_full.reshape(2 * e, d)
    y_h, states = _hgrn_fwd(z, lb_logits, hgrn_gnorm)
    o_attn, lse, y_a = _attn_fwd(z)
    dx2, dx2b, dy, loss_vec, dfg = _outproj_loss(x2d, y_h, y_a, w_out_full, final_gain.reshape(1, d), tgt)

    own_o, recv_o = _dwout_scatter(y_h, y_a, dx2b)
    dza = _attn_bwd(z, dy, o_attn, lse)
    dzh, dlb, dgn = _hgrn_bwd(z, dy, states, lb_logits, hgrn_gnorm)
    grad_x, dng = _dh_dx(dzh, dza, w_in_full, x2d, norm_gain, dx2)
    g_wo, d_wo, nm_wo, nv_wo = _sum_chips_adamw(own_o, recv_o, w_out[0], m_w_out[0], v_w_out[0])

    width = d // 2
    zero_row = jnp.zeros((1, width), F32)
    loss_row = loss_vec[:, :width] + loss_vec[:, width:]
    part = _pack_small(dng, dfg, jnp.concatenate([dlb, zero_row], axis=0), dgn, loss_row, width)
    pw = _pack_small(norm_gain, final_gain, lb_logits, hgrn_gnorm, zero_row, width)
    pm = _pack_small(m_norm_gain, m_final_gain, m_lb_logits, m_hgrn_gnorm, zero_row, width)
    pv = _pack_small(v_norm_gain, v_final_gain, v_lb_logits, v_hgrn_gnorm, zero_row, width)
    sg, sd, sm, sv = _small_allreduce_adamw(part, pw, pm, pv, lb_logits)
    own_i, recv_i = _dwin_scatter(h, dzh, dza)
    g_wi, d_wi, nm_wi, nv_wi = _sum_chips_adamw(own_i, recv_i, w_in[0], m_w_in[0], v_w_in[0])
    hd = hgrn_gnorm.shape[1]
    g_ng, g_fg, g_lb, g_gn = _unpack_small(sg, d, e, hd)
    d_ng, d_fg, d_lb, d_gn = _unpack_small(sd, d, e, hd)
    m_ng, m_fg, m_lb, m_gn = _unpack_small(sm, d, e, hd)
    v_ng, v_fg, v_lb, v_gn = _unpack_small(sv, d, e, hd)
    loss = sg[ROW_LOSS, 0]

    one = lambda a: a[None]
    return (loss, grad_x.reshape(1, s, d), g_ng, one(g_wi), g_lb, g_gn, one(g_wo), g_fg,
            d_ng, one(d_wi), d_lb, d_gn, one(d_wo), d_fg,
            m_ng, one(nm_wi), m_lb, m_gn, one(nm_wo), m_fg,
            v_ng, one(nv_wi), v_lb, v_gn, one(nv_wo), v_fg)
```

```python
import functools
import math

import jax
import jax.numpy as jnp
from jax import lax
from jax.experimental import pallas as pl
from jax.experimental.pallas import tpu as pltpu

NORM_EPS = 1e-6
HGRN_HEAD = 128
HGRN_CHUNK = 64
ATTN_HEAD = 64
ATTN_BAND = 128
DILATIONS = (1, 4, 16)
N_SPLITS = 8
N_DEV = 8
ADAM_LR = 0.001
ADAM_B1 = 0.9
ADAM_B2 = 0.999
ADAM_EPS = 1e-08
ADAM_WD = 0.01
ADAM_STEP = 10
LANES = 128
MESH = pl.DeviceIdType.MESH
F32 = jnp.float32
BF16 = jnp.bfloat16
NEG_BIG = -1e30
VMEM_LIMIT = 56 * 1024 * 1024

ANY = pl.BlockSpec(memory_space=pl.ANY)


def _params(*sem):
    return pltpu.CompilerParams(dimension_semantics=sem, vmem_limit_bytes=VMEM_LIMIT)


def _tile(n, pref):
    t = min(n, pref)
    assert n % t == 0, (n, pref)
    return t


def _dot(a, b, precision=None):
    return jnp.dot(a, b, preferred_element_type=F32, precision=precision)


def _dot_nt(a, b):
    return lax.dot_general(a, b, (((1,), (1,)), ((), ())), preferred_element_type=F32)


def _dot_tn(a, b):
    return lax.dot_general(a, b, (((0,), (0,)), ((), ())), preferred_element_type=F32)


def _sigmoid(x):
    return 0.5 * jnp.tanh(0.5 * x) + 0.5


def _dsilu(x, s):
    return s * (1.0 + x * (1.0 - s))


def _adamw(w, g, m, v):
    m = ADAM_B1 * m + (1.0 - ADAM_B1) * g
    v = ADAM_B2 * v + (1.0 - ADAM_B2) * (g * g)
    m_hat = m / (1.0 - ADAM_B1 ** ADAM_STEP)
    v_hat = v / (1.0 - ADAM_B2 ** ADAM_STEP)
    delta = -ADAM_LR * (m_hat / (jnp.sqrt(v_hat) + ADAM_EPS) + ADAM_WD * w)
    return delta, m, v


def _cast_bf16(a):
    r, c = a.shape
    tr = _tile(r, 256)

    def body(a_ref, o_ref):
        o_ref[...] = a_ref[...].astype(BF16)

    return pl.pallas_call(
        body, name="cast_bf16", grid=(r // tr,), out_shape=jax.ShapeDtypeStruct((r, c), BF16),
        in_specs=[pl.BlockSpec((tr, c), lambda i: (i, 0))], out_specs=pl.BlockSpec((tr, c), lambda i: (i, 0)),
        compiler_params=_params("parallel"))(a)


def _rmsnorm_fwd(x, gain):
    s, d = x.shape
    tm = _tile(s, 512)

    def body(x_ref, g_ref, h_ref):
        xv = x_ref[...]
        r = lax.rsqrt(jnp.mean(xv * xv, axis=-1, keepdims=True) + NORM_EPS)
        h_ref[...] = (xv * r * g_ref[...]).astype(BF16)

    return pl.pallas_call(
        body, name="rmsnorm_fwd", grid=(s // tm,), out_shape=jax.ShapeDtypeStruct((s, d), BF16),
        in_specs=[pl.BlockSpec((tm, d), lambda i: (i, 0)), pl.BlockSpec((1, d), lambda i: (0, 0))],
        out_specs=pl.BlockSpec((tm, d), lambda i: (i, 0)), compiler_params=_params("parallel"))(x, gain)


HGRN_BLOCK = 2048
TRI_ROWS = 256


def _chunk_masks():
    tb = TRI_ROWS
    row = lax.broadcasted_iota(jnp.int32, (tb, tb), 0)
    col = lax.broadcasted_iota(jnp.int32, (tb, tb), 1)
    same = (row // HGRN_CHUNK) == (col // HGRN_CHUNK)
    lower = jnp.where(same & (col <= row), 1.0, 0.0).astype(BF16)
    upper = jnp.where(same & (col >= row), 1.0, 0.0).astype(BF16)
    return lower, upper


def _split3(a):
    hi = a.astype(BF16).astype(F32)
    mid = (a - hi).astype(BF16).astype(F32)
    lo = (a - hi - mid).astype(BF16).astype(F32)
    return hi, mid, lo


def _tri_dot(tri, x):
    hi, mid, lo = (p.astype(BF16) for p in _split3(x))
    outs = []
    for r in range(0, x.shape[0], TRI_ROWS):
        sl = slice(r, r + TRI_ROWS)
        outs.append(_dot(tri, hi[sl]) + _dot(tri, mid[sl]) + _dot(tri, lo[sl]))
    return outs[0] if len(outs) == 1 else jnp.concatenate(outs, axis=0)


def _hgrn_gates(qp, fp, lbv):
    lb = _sigmoid(lbv[0:1] - lbv[1:2])
    sq = _sigmoid(qp)
    q = qp * sq
    sg = _sigmoid(fp)
    f = lb + (1.0 - lb) * sg
    k = 1.0 - f
    return lb, sq, q, sg, f, k


def _hgrn_fwd(z, lb_logits, gnorm):
    s = z.shape[0]
    e = z.shape[1] // N_SPLITS
    nh = e // HGRN_HEAD
    tb = _tile(s, HGRN_BLOCK)
    nc = tb // HGRN_CHUNK
    nb = s // tb
    C = HGRN_CHUNK

    def body(q_ref, f_ref, i_ref, g_ref, lb_ref, gn_ref, y_ref, st_ref, state, o_scr):
        @pl.when(pl.program_id(1) == 0)
        def _():
            state[...] = jnp.zeros_like(state)

        lb, sq, q, sg, f, k = _hgrn_gates(q_ref[...], f_ref[...], lb_ref[...])
        lower, _ = _chunk_masks()
        b = _tri_dot(lower, jnp.log(f))
        b3 = b.reshape(nc, C, HGRN_HEAD)
        bc = b3[:, C - 1:C, :]
        qt = (q * jnp.exp(b)).astype(BF16)
        kt = (k * jnp.exp(-b)).astype(BF16)
        ke = (k.reshape(nc, C, HGRN_HEAD) * jnp.exp(bc - b3)).reshape(tb, HGRN_HEAD).astype(BF16)
        v = i_ref[...].astype(BF16)
        tri = lax.broadcasted_iota(jnp.int32, (C, C), 1) <= lax.broadcasted_iota(jnp.int32, (C, C), 0)
        sls = [slice(c * C, (c + 1) * C) for c in range(nc)]
        kv = [_dot_tn(v[sl], ke[sl]) for sl in sls]
        a = [jnp.where(tri, _dot_nt(qt[sl], kt[sl]), 0.0).astype(BF16) for sl in sls]
        st = state[...]
        sts = []
        for c in range(nc):
            sts.append(st)
            st_ref[c] = st
            st = st * jnp.exp(bc[c]) + kv[c]
        state[...] = st
        for c, sl in enumerate(sls):
            o_scr[sl, :] = _dot(a[c], v[sl]) + _dot_nt(qt[sl], sts[c].astype(BF16))
        o = o_scr[...]
        rms = lax.rsqrt(jnp.mean(o * o, axis=-1, keepdims=True) + NORM_EPS)
        gp = g_ref[...]
        y_ref[...] = (o * rms * gn_ref[...] * (gp * _sigmoid(gp))).astype(BF16)

    col = lambda kk: (lambda h, n: (n, kk * nh + h))
    return pl.pallas_call(
        body, name="hgrn_fwd", grid=(nh, nb),
        out_shape=(jax.ShapeDtypeStruct((s, e), BF16),
                   jax.ShapeDtypeStruct((nh, s // C, HGRN_HEAD, HGRN_HEAD), F32)),
        in_specs=[pl.BlockSpec((tb, HGRN_HEAD), col(0)), pl.BlockSpec((tb, HGRN_HEAD), col(1)),
                  pl.BlockSpec((tb, HGRN_HEAD), col(2)), pl.BlockSpec((tb, HGRN_HEAD), col(3)),
                  pl.BlockSpec((2, HGRN_HEAD), lambda h, n: (0, h)), pl.BlockSpec((1, HGRN_HEAD), lambda h, n: (0, 0))],
        out_specs=(pl.BlockSpec((tb, HGRN_HEAD), lambda h, n: (n, h)),
                   pl.BlockSpec((None, nc, HGRN_HEAD, HGRN_HEAD), lambda h, n: (h, n, 0, 0))),
        scratch_shapes=[pltpu.VMEM((HGRN_HEAD, HGRN_HEAD), F32), pltpu.VMEM((tb, HGRN_HEAD), F32)],
        compiler_params=_params("parallel", "arbitrary"))(z, z, z, z, lb_logits, gnorm)


def _hgrn_bwd(z, dy, states, lb_logits, gnorm):
    s = z.shape[0]
    e = z.shape[1] // N_SPLITS
    nh = e // HGRN_HEAD
    tb = _tile(s, HGRN_BLOCK)
    nc = tb // HGRN_CHUNK
    nb = s // tb
    C = HGRN_CHUNK
    H = HGRN_HEAD

    def body(q_ref, f_ref, i_ref, g_ref, dy_ref, st_ref, lb_ref, gn_ref, dz_ref, dlb_ref, dgn_ref,
             gstate, o_scr, dq_scr, dk_scr, dv_scr, e_scr):
        first = (pl.program_id(0) == 0) & (pl.program_id(1) == 0)

        @pl.when(first)
        def _():
            dgn_ref[...] = jnp.zeros_like(dgn_ref)

        @pl.when(pl.program_id(1) == 0)
        def _():
            gstate[...] = jnp.zeros_like(gstate)
            dlb_ref[...] = jnp.zeros_like(dlb_ref)

        qp = q_ref[...]
        lb, sq, q, sg, f, k = _hgrn_gates(qp, f_ref[...], lb_ref[...])
        lower, upper = _chunk_masks()
        b = _tri_dot(lower, jnp.log(f))
        b3 = b.reshape(nc, C, H)
        bc = b3[:, C - 1:C, :]
        eb = jnp.exp(b)
        enb = jnp.exp(-b)
        eend = jnp.exp(bc - b3).reshape(tb, H)
        qt = (q * eb).astype(BF16)
        kt = (k * enb).astype(BF16)
        ke = (k * eend).astype(BF16)
        v = i_ref[...].astype(BF16)
        tri = lax.broadcasted_iota(jnp.int32, (C, C), 1) <= lax.broadcasted_iota(jnp.int32, (C, C), 0)
        sls = [slice(c * C, (c + 1) * C) for c in range(nc)]
        a = [jnp.where(tri, _dot_nt(qt[sl], kt[sl]), 0.0).astype(BF16) for sl in sls]
        for c, sl in enumerate(sls):
            o_scr[sl, :] = _dot(a[c], v[sl]) + _dot_nt(qt[sl], st_ref[c].astype(BF16))
        o = o_scr[...]
        rms = lax.rsqrt(jnp.mean(o * o, axis=-1, keepdims=True) + NORM_EPS)
        on = o * rms
        gn = gn_ref[...]
        gp = g_ref[...]
        sgg = _sigmoid(gp)
        dyv = dy_ref[...]
        d_on = dyv * (gp * sgg)
        dz_ref[3] = (dyv * on * gn * _dsilu(gp, sgg)).astype(BF16)
        dgn_ref[...] += jnp.sum(d_on * on, axis=0, keepdims=True)
        u = d_on * gn
        do = (rms * (u - on * jnp.mean(u * on, axis=-1, keepdims=True))).astype(BF16)
        gup = [_dot_tn(do[sl], qt[sl]) for sl in sls]
        da = [jnp.where(tri, _dot_nt(do[sl], v[sl]), 0.0).astype(BF16) for sl in sls]
        gt = gstate[...]
        gts = [None] * nc
        for c in reversed(range(nc)):
            gts[c] = gt
            gt = gt * jnp.exp(bc[c]) + gup[c]
        gstate[...] = gt
        for c, sl in enumerate(sls):
            stp = st_ref[c]
            gtb = gts[c].astype(BF16)
            dqt = _dot(da[c], kt[sl]) + _dot(do[sl], stp.astype(BF16))
            dkt = _dot_tn(da[c], qt[sl])
            dks = _dot(v[sl], gtb) * eend[sl]
            dv_scr[sl, :] = _dot_tn(a[c], do[sl]) + _dot_nt(ke[sl], gtb)
            dq_scr[sl, :] = dqt * eb[sl]
            dk_scr[sl, :] = dkt * enb[sl] + dks
            ech = (jnp.sum(k[sl] * dks, axis=0, keepdims=True)
                   + jnp.sum(gts[c] * jnp.exp(bc[c]) * stp, axis=0, keepdims=True))
            e_scr[sl, :] = jnp.broadcast_to(ech, (C, H))
        dq = dq_scr[...]
        dk = dk_scr[...]
        dlf = _tri_dot(upper, q * dq - k * dk) + e_scr[...]
        dft = dlf / f - dk
        dz_ref[0] = (dq * _dsilu(qp, sq)).astype(BF16)
        dz_ref[1] = (dft * (1.0 - lb) * sg * (1.0 - sg)).astype(BF16)
        dz_ref[2] = dv_scr[...].astype(BF16)
        dlb_ref[...] += jnp.sum(dft * (1.0 - sg), axis=0, keepdims=True)

    col = lambda kk: (lambda h, n: (nb - 1 - n, kk * nh + h))
    return pl.pallas_call(
        body, name="hgrn_bwd", grid=(nh, nb),
        out_shape=(jax.ShapeDtypeStruct((4, s, e), BF16), jax.ShapeDtypeStruct((1, e), F32),
                   jax.ShapeDtypeStruct((1, H), F32)),
        in_specs=[pl.BlockSpec((tb, H), col(0)), pl.BlockSpec((tb, H), col(1)),
                  pl.BlockSpec((tb, H), col(2)), pl.BlockSpec((tb, H), col(3)),
                  pl.BlockSpec((tb, H), lambda h, n: (nb - 1 - n, h)),
                  pl.BlockSpec((None, nc, H, H), lambda h, n: (h, nb - 1 - n, 0, 0)),
                  pl.BlockSpec((2, H), lambda h, n: (0, h)), pl.BlockSpec((1, H), lambda h, n: (0, 0))],
        out_specs=(pl.BlockSpec((4, tb, H), lambda h, n: (0, nb - 1 - n, h)),
                   pl.BlockSpec((1, H), lambda h, n: (0, h)), pl.BlockSpec((1, H), lambda h, n: (0, 0))),
        scratch_shapes=[pltpu.VMEM((H, H), F32)] + [pltpu.VMEM((tb, H), F32)] * 5,
        compiler_params=_params("arbitrary", "arbitrary"))(z, z, z, z, dy, states, lb_logits, gnorm)


ATTN_T = 16 * ATTN_BAND
SCALE = ATTN_HEAD ** -0.5
TILE_UNROLL = 2


def _slope(hh, nheads):
    head = (2 * pl.program_id(0) + hh + 1).astype(F32)
    return jnp.exp(jnp.full((1, 1), -8.0 / nheads * math.log(2.0), F32) * head)


def _fill_bias(bias, nheads, delta, edge_ok):
    band = (delta >= 0) & (delta <= ATTN_BAND)
    dist = delta.astype(F32)
    for pi, dil in enumerate(DILATIONS):
        for hh in range(2):
            full = jnp.where(band, -(_slope(hh, nheads) * float(dil)) * dist, NEG_BIG)
            bias[(pi * 2 + hh) * 2] = full
            bias[(pi * 2 + hh) * 2 + 1] = jnp.where(edge_ok, full, NEG_BIG)


def _rows(start, size, stride):
    if stride == 1:
        return pl.ds(pl.multiple_of(start, ATTN_BAND), size)
    return pl.ds(start, size, stride=stride)


def _head_lanes(rows, hh):
    return (lax.broadcasted_iota(jnp.int32, (rows, LANES), 1) // ATTN_HEAD) == hh


def _attn_fwd(z):
    s = z.shape[0]
    e = z.shape[1] // N_SPLITS
    npair = e // LANES
    T = ATTN_T
    assert s % T == 0
    nsb = s // T
    W = ATTN_BAND
    nt = T // W
    HD = ATTN_HEAD
    chunk = 256

    def body(q_ref, kp_ref, kc_ref, vp_ref, vc_ref, g_ref, o_ref, l_ref, y_ref, qa, kbuf, va, bias, accs, ms, lsw):
        sb = pl.program_id(1)
        def stage(i, carry):
            rows = pl.ds(pl.multiple_of(i * chunk, chunk), chunk)
            upper = pl.ds(pl.multiple_of(T + i * chunk, chunk), chunk)
            kbuf[upper, :] = kc_ref[rows, :]
            for hh in range(2):
                mine = _head_lanes(chunk, hh)
                qa[hh, rows, :] = jnp.where(mine, q_ref[rows, :] * SCALE, 0.0)
                va[hh, upper, :] = jnp.where(mine, vc_ref[rows, :], 1.0)
            return carry

        lax.fori_loop(0, T // chunk, stage, 0)

        @pl.when(sb == 0)
        def _():
            def stage_prev(i, carry):
                rows = pl.ds(pl.multiple_of(i * chunk, chunk), chunk)
                kbuf[rows, :] = kp_ref[rows, :]
                for hh in range(2):
                    va[hh, rows, :] = jnp.where(_head_lanes(chunk, hh), vp_ref[rows, :], 1.0)
                return carry

            lax.fori_loop(0, T // chunk, stage_prev, 0)
        qi = lax.broadcasted_iota(jnp.int32, (W, 2 * W), 0)
        kj = lax.broadcasted_iota(jnp.int32, (W, 2 * W), 1)
        _fill_bias(bias, 2 * npair, W + qi - kj, kj >= W)

        def tile(tau, carry):
            first = _head_lanes(W, 0)
            rows, scores = [], []
            for pi, dil in enumerate(DILATIONS):
                r = tau % dil
                ub = tau // dil
                qrows = _rows(r + dil * W * ub, W, dil)
                krows = _rows(T + dil * W * (ub - 1) + r, 2 * W, dil)
                var = jnp.where((sb == 0) & (ub == 0), 1, 0)
                kt = kbuf[krows, :].astype(BF16)
                rows.append((qrows, krows))
                scores.append([_dot_nt(qa[hh, qrows, :].astype(BF16), kt) + bias[(pi * 2 + hh) * 2 + var]
                               for hh in range(2)])
            maxes = [[jnp.max(sc, axis=-1, keepdims=True) for sc in pair] for pair in scores]
            probs = [[jnp.exp(sc - m).astype(BF16) for sc, m in zip(ps, pm)] for ps, pm in zip(scores, maxes)]
            for pi, (qrows, krows) in enumerate(rows):
                outs = [_dot(probs[pi][hh], va[hh, krows, :].astype(BF16)) for hh in range(2)]
                accs[pi, qrows, :] = jnp.where(first, outs[0], outs[1])
                lsw[pi, qrows, :] = jnp.where(first, outs[1], outs[0])
                ms[pi, qrows, :] = jnp.where(first, maxes[pi][0], maxes[pi][1])
            return carry

        lax.fori_loop(0, nt, tile, 0, unroll=TILE_UNROLL)

        def merge(i, carry):
            rows = pl.ds(pl.multiple_of(i * chunk, chunk), chunk)
            m1, m2, m3 = ms[0, rows, :], ms[1, rows, :], ms[2, rows, :]
            mx = jnp.maximum(jnp.maximum(m1, m2), m3)
            w1, w2, w3 = jnp.exp(m1 - mx), jnp.exp(m2 - mx), jnp.exp(m3 - mx)
            unswap = lambda a: pltpu.roll(a, ATTN_HEAD, 1)
            den = w1 * unswap(lsw[0, rows, :]) + w2 * unswap(lsw[1, rows, :]) + w3 * unswap(lsw[2, rows, :])
            o = (w1 * accs[0, rows, :] + w2 * accs[1, rows, :] + w3 * accs[2, rows, :]) / den
            o_ref[rows, :] = o
            l_ref[rows, :] = mx + jnp.log(den)
            gp = g_ref[rows, :]
            y_ref[rows, :] = (o * (gp * _sigmoid(gp))).astype(BF16)
            upper = pl.ds(pl.multiple_of(T + i * chunk, chunk), chunk)
            kbuf[rows, :] = kbuf[upper, :]
            for hh in range(2):
                va[hh, rows, :] = va[hh, upper, :]
            return carry

        lax.fori_loop(0, T // chunk, merge, 0)

    cur = lambda split: (lambda hp, sb: (sb, split * npair + hp))
    prev = lambda split: (lambda hp, sb: (jnp.maximum(sb - 1, 0), split * npair + hp))
    blk = lambda index: pl.BlockSpec((T, LANES), index)
    out = blk(lambda hp, sb: (sb, hp))
    buf = lambda rows: pltpu.VMEM((rows, LANES), F32)
    return pl.pallas_call(
        body, name="attn_fwd", grid=(npair, nsb),
        out_shape=(jax.ShapeDtypeStruct((s, e), F32), jax.ShapeDtypeStruct((s, e), F32), jax.ShapeDtypeStruct((s, e), BF16)),
        in_specs=[blk(cur(4)), blk(prev(5)), blk(cur(5)), blk(prev(6)), blk(cur(6)), blk(cur(7))],
        out_specs=(out, out, out),
        scratch_shapes=[pltpu.VMEM((2, T, LANES), F32), buf(2 * T), pltpu.VMEM((2, 2 * T, LANES), F32),
                        pltpu.VMEM((12, W, 2 * W), F32)] + [pltpu.VMEM((3, T, LANES), F32)] * 3,
        compiler_params=_params("parallel", "arbitrary"))(z, z, z, z, z, z)


def _outproj_loss(x, y_h, y_a, w_out_full, final_gain, target):
    s, d = x.shape
    e = y_h.shape[1]
    tm = _tile(s, 256)

    def body(x_ref, yh_ref, ya_ref, w_ref, g_ref, t_ref, dx_ref, dxb_ref, dy_ref, loss_ref, dg_ref):
        @pl.when(pl.program_id(0) == 0)
        def _():
            loss_ref[...] = jnp.zeros_like(loss_ref)
            dg_ref[...] = jnp.zeros_like(dg_ref)

        w = w_ref[...]
        x2 = x_ref[...] + _dot(yh_ref[...], w[0:e]) + _dot(ya_ref[...], w[e:2 * e])
        r = lax.rsqrt(jnp.mean(x2 * x2, axis=-1, keepdims=True) + NORM_EPS)
        xn = x2 * r
        g = g_ref[...]
        err = xn * g - t_ref[...]
        loss_ref[...] += jnp.sum(err * err, axis=0, keepdims=True) * (0.5 / d)
        dyo = err * (1.0 / d)
        dg_ref[...] += jnp.sum(dyo * xn, axis=0, keepdims=True)
        u = dyo * g
        dx2 = r * (u - xn * jnp.mean(u * xn, axis=-1, keepdims=True))
        dx_ref[...] = dx2
        dxb = dx2.astype(BF16)
        dxb_ref[...] = dxb
        dy_ref[...] = _dot_nt(dxb, w)

    row = pl.BlockSpec((tm, d), lambda i: (i, 0))
    half = pl.BlockSpec((tm, e), lambda i: (i, 0))
    vec = pl.BlockSpec((1, d), lambda i: (0, 0))
    return pl.pallas_call(
        body, name="outproj_loss", grid=(s // tm,),
        out_shape=(jax.ShapeDtypeStruct((s, d), F32), jax.ShapeDtypeStruct((s, d), BF16),
                   jax.ShapeDtypeStruct((s, 2 * e), F32), jax.ShapeDtypeStruct((1, d), F32),
                   jax.ShapeDtypeStruct((1, d), F32)),
        in_specs=[row, half, half, pl.BlockSpec((2 * e, d), lambda i: (0, 0)), vec, row],
        out_specs=(row, row, pl.BlockSpec((tm, 2 * e), lambda i: (i, 0)), vec, vec),
        compiler_params=_params("arbitrary"))(x, y_h, y_a, w_out_full, final_gain, target)


def _attn_bwd(z, dy, o, lse):
    s, e = o.shape
    npair = e // LANES
    T = ATTN_T
    assert s % T == 0
    nsb = s // T
    W = ATTN_BAND
    nt = T // W
    HD = ATTN_HEAD
    chunk = 256

    def body(k_ref, v_ref, qc_ref, qn_ref, dyc_ref, dyn_ref, gc_ref, gn_ref, oc_ref, on_ref, lc_ref, ln_ref,
             dz_ref, qa, doa, ka, va, dqacc, dkacc, dvacc, bias):
        sb = pl.program_id(1)
        def stage_queries(half, q_r, dy_r, g_r, o_r, l_r):
            def stage(i, carry):
                rows = pl.ds(pl.multiple_of(i * chunk, chunk), chunk)
                dst = pl.ds(pl.multiple_of(half * T + i * chunk, chunk), chunk)
                lane = lax.broadcasted_iota(jnp.int32, (chunk, LANES), 1)
                gp = g_r[rows, :]
                dov = dy_r[rows, :] * (gp * _sigmoid(gp))
                qv = q_r[rows, :] * SCALE
                same_head = (lax.broadcasted_iota(jnp.int32, (LANES, LANES), 0) // HD
                             == lax.broadcasted_iota(jnp.int32, (LANES, LANES), 1) // HD)
                ones = jnp.where(same_head, 1.0, 0.0).astype(BF16)
                hi, mid, lo = (p.astype(BF16) for p in _split3(dov * o_r[rows, :]))
                delta = _dot(hi, ones) + _dot(mid, ones) + _dot(lo, ones)
                swap = lambda a: pltpu.roll(a, HD, 1)
                lse_parts = [swap(p) for p in _split3(l_r[rows, :])]
                dl_parts = [swap(p) for p in _split3(delta)]
                for hh in range(2):
                    mine = _head_lanes(chunk, hh)
                    spare = (1 - hh) * HD
                    qh = jnp.where(mine, qv, 0.0)
                    dh = jnp.where(mine, dov, 0.0)
                    for j in range(3):
                        qh = jnp.where(lane == spare + j, lse_parts[j], qh)
                        dh = jnp.where(lane == spare + j, dl_parts[j], dh)
                    qa[hh, dst, :] = qh
                    doa[hh, dst, :] = dh
                return carry

            lax.fori_loop(0, T // chunk, stage, 0)

        @pl.when(sb == 0)
        def _():
            stage_queries(0, qc_ref, dyc_ref, gc_ref, oc_ref, lc_ref)

        stage_queries(1, qn_ref, dyn_ref, gn_ref, on_ref, ln_ref)

        def stage_keys(i, carry):
            rows = pl.ds(pl.multiple_of(i * chunk, chunk), chunk)
            lane = lax.broadcasted_iota(jnp.int32, (chunk, LANES), 1)
            for hh in range(2):
                spare = (1 - hh) * HD
                minus = (lane >= spare) & (lane < spare + 3)
                ka[hh, rows, :] = jnp.where(minus, -1.0, k_ref[rows, :])
                va[hh, rows, :] = jnp.where(minus, -1.0, v_ref[rows, :])
            gp = gc_ref[rows, :]
            dz_ref[3, rows, :] = (dyc_ref[rows, :] * oc_ref[rows, :] * _dsilu(gp, _sigmoid(gp))).astype(BF16)
            return carry

        lax.fori_loop(0, T // chunk, stage_keys, 0)

        @pl.when(sb == 0)
        def _():
            dqacc[0:T, :] = jnp.zeros((T, LANES), F32)

        dqacc[T:, :] = jnp.zeros((T, LANES), F32)
        dkacc[...] = jnp.zeros_like(dkacc)
        dvacc[...] = jnp.zeros_like(dvacc)
        qi = lax.broadcasted_iota(jnp.int32, (2 * W, W), 0)
        kj = lax.broadcasted_iota(jnp.int32, (2 * W, W), 1)
        _fill_bias(bias, 2 * npair, qi - kj, qi < W)

        def tile(tau, carry):
            def scores(step, pi):
                dil = DILATIONS[pi]
                r = step % dil
                ub = step // dil
                start = r + dil * W * ub
                krows = _rows(start, W, dil)
                qrows = _rows(start, 2 * W, dil)
                var = jnp.where((sb == nsb - 1) & (ub == nt // dil - 1), 1, 0)
                unit = dict(krows=krows, qrows=qrows, ops=[], sc=[], dpd=[])
                for hh in range(2):
                    kt = ka[hh, krows, :].astype(BF16)
                    vt = va[hh, krows, :].astype(BF16)
                    qt = qa[hh, qrows, :].astype(BF16)
                    dt = doa[hh, qrows, :].astype(BF16)
                    unit["ops"].append((kt, qt, dt))
                    unit["sc"].append(_dot_nt(qt, kt) + bias[(pi * 2 + hh) * 2 + var])
                    unit["dpd"].append(_dot_nt(dt, vt))
                return unit

            def elementwise(unit):
                ps = [jnp.exp(s_) for s_ in unit["sc"]]
                unit["ds"] = [(p * d).astype(BF16) for p, d in zip(ps, unit["dpd"])]
                unit["pb"] = [p.astype(BF16) for p in ps]

            def products(unit):
                dvs = [_dot_tn(pb, dt) for pb, (kt, qt, dt) in zip(unit["pb"], unit["ops"])]
                dks = [_dot_tn(ds, qt) for ds, (kt, qt, dt) in zip(unit["ds"], unit["ops"])]
                dqs = [_dot(ds, kt) for ds, (kt, qt, dt) in zip(unit["ds"], unit["ops"])]
                dkacc[unit["krows"], :] += jnp.where(_head_lanes(W, 0), dks[0], dks[1])
                dvacc[unit["krows"], :] += jnp.where(_head_lanes(W, 0), dvs[0], dvs[1])
                dqacc[unit["qrows"], :] += jnp.where(_head_lanes(2 * W, 0), dqs[0], dqs[1]) * SCALE

            order = [(2 * tau + half, pi) for half in range(2) for pi in range(len(DILATIONS))]
            units = [None] * len(order)
            for n in range(len(order) + 2):
                if n < len(order):
                    units[n] = scores(*order[n])
                if 1 <= n <= len(order):
                    elementwise(units[n - 1])
                if n >= 2:
                    products(units[n - 2])
            return carry

        lax.fori_loop(0, nt // 2, tile, 0)

        def flush(i, carry):
            rows = pl.ds(pl.multiple_of(i * chunk, chunk), chunk)
            nxt = pl.ds(pl.multiple_of(T + i * chunk, chunk), chunk)
            dz_ref[0, rows, :] = dqacc[rows, :].astype(BF16)
            dz_ref[1, rows, :] = dkacc[rows, :].astype(BF16)
            dz_ref[2, rows, :] = dvacc[rows, :].astype(BF16)
            dqacc[rows, :] = dqacc[nxt, :]
            for hh in range(2):
                qa[hh, rows, :] = qa[hh, nxt, :]
                doa[hh, rows, :] = doa[hh, nxt, :]
            return carry

        lax.fori_loop(0, T // chunk, flush, 0)

    zc = lambda split: (lambda hp, sb: (sb, split * npair + hp))
    zn = lambda split: (lambda hp, sb: (jnp.minimum(sb + 1, nsb - 1), split * npair + hp))
    ec = lambda off: (lambda hp, sb: (sb, off + hp))
    en = lambda off: (lambda hp, sb: (jnp.minimum(sb + 1, nsb - 1), off + hp))
    blk = lambda index: pl.BlockSpec((T, LANES), index)
    buf = lambda rows: pltpu.VMEM((rows, LANES), F32)
    return pl.pallas_call(
        body, name="attn_bwd", grid=(npair, nsb), out_shape=jax.ShapeDtypeStruct((4, s, e), BF16),
        in_specs=[blk(zc(5)), blk(zc(6)), blk(zc(4)), blk(zn(4)), blk(ec(npair)), blk(en(npair)),
                  blk(zc(7)), blk(zn(7)), blk(ec(0)), blk(en(0)), blk(ec(0)), blk(en(0))],
        out_specs=pl.BlockSpec((4, T, LANES), lambda hp, sb: (0, sb, hp)),
        scratch_shapes=[pltpu.VMEM((2, 2 * T, LANES), F32), pltpu.VMEM((2, 2 * T, LANES), F32),
                        pltpu.VMEM((2, T, LANES), F32), pltpu.VMEM((2, T, LANES), F32),
                        buf(2 * T), buf(T), buf(T), pltpu.VMEM((12, 2 * W, W), F32)],
        compiler_params=_params("parallel", "arbitrary"))(z, z, z, z, dy, dy, z, z, o, o, lse, lse)


def _dz_specs(tm, e, axis):
    def mk(lo, hi):
        def index(i, k):
            row, grp = (i, k) if axis == 1 else (k, i)
            return (jnp.clip(grp - lo, 0, hi - lo - 1), row, 0)
        return pl.BlockSpec((None, tm, e), index)
    return [mk(0, 4), mk(4, 8)]


def _dz_pick(grp, dzh_ref, dza_ref, fn):
    @pl.when(grp < 4)
    def _():
        fn(dzh_ref[...])

    @pl.when(grp >= 4)
    def _():
        fn(dza_ref[...])


def _dh_dx(dzh, dza, w_full, x, gain, dx2):
    s, d = x.shape
    e = dzh.shape[2]
    tm = _tile(s, 1024)
    ni = s // tm
    chunk = _tile(tm, 256)
    fetch_at = 2

    def body(dzh_ref, dza_ref, w_ref, x_hbm, g_ref, dx2_hbm, gx_hbm, dg_ref, acc, xbuf, dbuf, sems):
        i, k = pl.program_id(0), pl.program_id(1)
        tile_rows = pl.ds(pl.multiple_of(i * tm, tm), tm)
        fetch_x = pltpu.make_async_copy(x_hbm.at[tile_rows, :], xbuf, sems.at[0])
        fetch_d = pltpu.make_async_copy(dx2_hbm.at[tile_rows, :], dbuf, sems.at[1])
        store = pltpu.make_async_copy(xbuf, gx_hbm.at[tile_rows, :], sems.at[2])

        @pl.when((i == 0) & (k == 0))
        def _():
            dg_ref[...] = jnp.zeros_like(dg_ref)

        @pl.when(k == 0)
        def _():
            acc[...] = jnp.zeros_like(acc)

        @pl.when((k == fetch_at) & (i > 0))
        def _():
            store.wait()

        @pl.when(k == fetch_at)
        def _():
            fetch_x.start()
            fetch_d.start()

        def add(dz):
            acc[...] += _dot_nt(dz, w_ref[...])

        _dz_pick(k, dzh_ref, dza_ref, add)

        @pl.when(k == N_SPLITS - 1)
        def _():
            fetch_x.wait()
            fetch_d.wait()
            gain_row = g_ref[...]

            def finish(c, dg):
                rows = pl.ds(pl.multiple_of(c * chunk, chunk), chunk)
                dh = acc[rows, :]
                xv = xbuf[rows, :]
                r = lax.rsqrt(jnp.mean(xv * xv, axis=-1, keepdims=True) + NORM_EPS)
                xn = xv * r
                u = dh * gain_row
                xbuf[rows, :] = dbuf[rows, :] + r * (u - xn * jnp.mean(u * xn, axis=-1, keepdims=True))
                return dg + jnp.sum(dh * xn, axis=0, keepdims=True)

            dg_ref[...] += lax.fori_loop(0, tm // chunk, finish, jnp.zeros((1, d), F32))
            store.start()

        @pl.when((k == N_SPLITS - 1) & (i == ni - 1))
        def _():
            store.wait()

    vec = pl.BlockSpec((1, d), lambda i, k: (0, 0))
    return pl.pallas_call(
        body, name="dh_dx", grid=(ni, N_SPLITS),
        out_shape=(jax.ShapeDtypeStruct((s, d), F32), jax.ShapeDtypeStruct((1, d), F32)),
        in_specs=_dz_specs(tm, e, 1) + [pl.BlockSpec((None, d, e), lambda i, k: (k, 0, 0)), ANY, vec, ANY],
        out_specs=(ANY, vec),
        scratch_shapes=[pltpu.VMEM((tm, d), F32), pltpu.VMEM((tm, d), F32), pltpu.VMEM((tm, d), F32),
                        pltpu.SemaphoreType.DMA((3,))],
        compiler_params=_params("arbitrary", "arbitrary"))(dzh, dza, w_full, x, gain, dx2)


def _position():
    x, y, c = lax.axis_index("x"), lax.axis_index("y"), lax.axis_index("c")
    return x, y, c


def _xor_peer(x, y, c, mask):
    return (x ^ ((mask >> 2) & 1), y ^ ((mask >> 1) & 1), c ^ (mask & 1))


def _block_order(masks):
    me = 4 * lax.axis_index("x") + 2 * lax.axis_index("y") + lax.axis_index("c")
    return jnp.stack([me ^ m for m in masks]).astype(jnp.int32)


GATHER_MASKS = (0, 1, 4, 2, 6, 5, 3, 7)


def _inproj_gather(h, w_loc, wo_loc):
    s, d = h.shape
    e = w_loc.shape[1]
    tm = _tile(s, 1024)
    ni = s // tm
    pre = max(ni - 2, 0)

    def body(order_ref, h_ref, w_ref, wo_ref, z_ref, wf_ref, wof_ref, wbuf, send_sems, recv_sems, osend, orecv,
             local_sems, wsems):
        j, i = pl.program_id(0), pl.program_id(1)
        x, y, c = _position()
        me, sibling = (x, y, c), (x, y, 1 - c)
        chips = [(1 - x, y), (x, 1 - y), (1 - x, 1 - y)]
        blk = lambda p: 4 * p[0] + 2 * p[1] + p[2]

        def copy(k, block, to, src=None):
            dst = wf_ref.at[blk(block)]
            return pltpu.make_async_remote_copy(
                src_ref=dst if src is None else src, dst_ref=dst, send_sem=send_sems.at[k], recv_sem=recv_sems.at[k],
                device_id=to, device_id_type=MESH)

        first = [copy(0, me, sibling, src=w_ref)] + [copy(1 + q, me, (*chip, c), src=w_ref) for q, chip in enumerate(chips)]
        passed = [copy(4 + q, (*chip, c), sibling) for q, chip in enumerate(chips)]
        mine = pltpu.make_async_copy(w_ref, wf_ref.at[blk(me)], local_sems.at[0])
        ocopies = [pltpu.make_async_remote_copy(
            src_ref=wo_ref, dst_ref=wof_ref.at[blk(me)], send_sem=osend.at[k], recv_sem=orecv.at[k],
            device_id=_xor_peer(x, y, c, k + 1), device_id_type=MESH) for k in range(N_DEV - 1)]
        omine = pltpu.make_async_copy(wo_ref, wof_ref.at[blk(me)], local_sems.at[1])
        blocks = [me, sibling] + [(*chip, c) for chip in chips] + [(*chip, 1 - c) for chip in chips]
        arrive = [None, copy(0, sibling, me)] + [copy(1 + q, (*chip, c), me) for q, chip in enumerate(chips)] \
            + [copy(4 + q, (*chip, 1 - c), me) for q, chip in enumerate(chips)]
        forward = [None, None] + passed + [None, None, None]

        def load(slot, src):
            return pltpu.make_async_copy(src, wbuf.at[slot], wsems.at[slot])

        @pl.when((j == 0) & (i == 0))
        def _():
            for cp in [mine, omine] + first + ocopies:
                cp.start()
            load(0, w_ref).start()

        for jj in range(N_DEV):
            @pl.when((j == jj) & (i == 0))
            def _():
                load(jj % 2, w_ref).wait()

            if jj + 1 < N_DEV:
                @pl.when((j == jj) & (i == pre))
                def _():
                    arrive[jj + 1].wait_recv()
                    if forward[jj + 1] is not None:
                        forward[jj + 1].start()
                    load((jj + 1) % 2, wf_ref.at[blk(blocks[jj + 1])]).start()

        z_ref[...] = _dot(h_ref[...], wbuf[j % 2])

        @pl.when((j == N_DEV - 1) & (i == ni - 1))
        def _():
            for cp in first + passed:
                cp.wait_send()
            for cp in ocopies:
                cp.wait_send()
                cp.wait_recv()
            mine.wait()
            omine.wait()

    grid_spec = pltpu.PrefetchScalarGridSpec(
        num_scalar_prefetch=1, grid=(N_DEV, ni),
        in_specs=[pl.BlockSpec((tm, d), lambda j, i, o: (i, 0)), ANY, ANY],
        out_specs=(pl.BlockSpec((tm, e), lambda j, i, o: (i, o[j])), ANY, ANY),
        scratch_shapes=[pltpu.VMEM((2, d, e), BF16), pltpu.SemaphoreType.DMA((7,)), pltpu.SemaphoreType.DMA((7,)),
                        pltpu.SemaphoreType.DMA((7,)), pltpu.SemaphoreType.DMA((7,)), pltpu.SemaphoreType.DMA((2,)),
                        pltpu.SemaphoreType.DMA((2,))])
    return pl.pallas_call(
        body, name="inproj_gather", grid_spec=grid_spec,
        out_shape=(jax.ShapeDtypeStruct((s, N_SPLITS * e), F32), jax.ShapeDtypeStruct((N_DEV, d, e), BF16),
                   jax.ShapeDtypeStruct((N_DEV,) + wo_loc.shape, BF16)),
        compiler_params=_params("arbitrary", "arbitrary"))(_block_order(GATHER_MASKS), h, w_loc, wo_loc)


SCATTER_MASKS = (7, 6, 5, 4, 3, 2, 1, 0)
N_CHIPS = 4


def _scatter_block(k, acc, stage, tmp, own_ref, ra_ref, rb_ref, sa_send, sa_recv, sb_send, sb_recv, loc_sem, last):
    x, y, c = _position()
    chip_of = lambda t: _xor_peer(x, y, c, SCATTER_MASKS[2 * t + 1])

    def ship(t):
        return pltpu.make_async_remote_copy(
            src_ref=stage.at[0], dst_ref=ra_ref.at[t], send_sem=sa_send.at[t], recv_sem=sa_recv.at[t],
            device_id=(x, y, 1 - c), device_id_type=MESH)

    def send(t):
        return pltpu.make_async_remote_copy(
            src_ref=stage.at[1], dst_ref=rb_ref.at[t], send_sem=sb_send.at[t], recv_sem=sb_recv.at[t],
            device_id=chip_of(t), device_id_type=MESH)

    for kk in range(N_DEV):
        t = kk // 2

        @pl.when(last & (k == kk))
        def _():
            if kk % 2 == 0:
                if t >= 1:
                    ship(t - 1).wait_send()
                stage[0] = acc[...].astype(BF16)
                ship(t).start()
            else:
                ship(t).wait_recv()
                fetch = pltpu.make_async_copy(ra_ref.at[t], tmp, loc_sem)
                fetch.start()
                fetch.wait()
                acc[...] += tmp[...].astype(F32)
                if t < N_CHIPS - 1:
                    if t >= 1:
                        send(t - 1).wait_send()
                    stage[1] = acc[...].astype(BF16)
                    send(t).start()
                else:
                    keep = pltpu.make_async_copy(acc, own_ref, loc_sem)
                    keep.start()
                    keep.wait()
                    ship(t).wait_send()
                    send(t - 1).wait_send()
                    for q in range(N_CHIPS - 1):
                        send(q).wait_recv()


def _scatter_scratch(rows, cols):
    return [pltpu.VMEM((rows, cols), F32), pltpu.VMEM((2, rows, cols), BF16), pltpu.VMEM((rows, cols), BF16),
            pltpu.SemaphoreType.DMA((N_CHIPS,)), pltpu.SemaphoreType.DMA((N_CHIPS,)),
            pltpu.SemaphoreType.DMA((N_CHIPS - 1,)), pltpu.SemaphoreType.DMA((N_CHIPS - 1,)), pltpu.SemaphoreType.DMA(())]


def _scatter_out(rows, cols):
    return (jax.ShapeDtypeStruct((rows, cols), F32), jax.ShapeDtypeStruct((N_CHIPS, rows, cols), BF16),
            jax.ShapeDtypeStruct((N_CHIPS - 1, rows, cols), BF16))


def _dwin_scatter(h, dzh, dza):
    s, d = h.shape
    e = dzh.shape[2]
    ts = _tile(s, 1024)
    ns = s // ts

    def body(order_ref, dzh_ref, dza_ref, h_ref, own_ref, ra_ref, rb_ref, acc, stage, tmp, *sems):
        k, step = pl.program_id(0), pl.program_id(1)

        @pl.when(step == 0)
        def _():
            acc[...] = jnp.zeros_like(acc)

        def add(dz):
            acc[...] += _dot_tn(h_ref[...], dz)

        _dz_pick(order_ref[k], dzh_ref, dza_ref, add)
        _scatter_block(k, acc, stage, tmp, own_ref, ra_ref, rb_ref, *sems, step == ns - 1)

    def dz_spec(lo):
        return pl.BlockSpec((None, ts, e), lambda k, st, o: (jnp.clip(o[k] - lo, 0, 3), st, 0))

    grid_spec = pltpu.PrefetchScalarGridSpec(
        num_scalar_prefetch=1, grid=(N_DEV, ns),
        in_specs=[dz_spec(0), dz_spec(4), pl.BlockSpec((ts, d), lambda k, st, o: (st, 0))],
        out_specs=(ANY, ANY, ANY), scratch_shapes=_scatter_scratch(d, e))
    own, _, rb = pl.pallas_call(
        body, name="dwin_scatter", grid_spec=grid_spec, out_shape=_scatter_out(d, e),
        compiler_params=_params("arbitrary", "arbitrary"))(_block_order(SCATTER_MASKS), dzh, dza, h)
    return own, rb


def _dwout_scatter(y_h, y_a, dxb):
    s, e = y_h.shape
    d = dxb.shape[1]
    r = 2 * e // N_DEV
    pairs = e // (2 * r)
    ts = _tile(s, 1024)
    ns = s // ts
    chip_masks = SCATTER_MASKS[1::2]
    per_pass = 2

    def body(pair_ref, yh0_ref, ya0_ref, yh1_ref, ya1_ref, dx_ref, own_ref, ra_ref, rb_ref, acc, keep_buf, ship_buf,
             send_buf, tmp, sa_send, sa_recv, sb_send, sb_recv, loc_sem):
        p, step = pl.program_id(0), pl.program_id(1)
        x, y, c = _position()

        @pl.when(step == 0)
        def _():
            acc[...] = jnp.zeros_like(acc)

        for u, (yh_ref, ya_ref) in enumerate(((yh0_ref, ya0_ref), (yh1_ref, ya1_ref))):
            rows = slice(u * 2 * r, (u + 1) * 2 * r)

            @pl.when(pair_ref[per_pass * p + u] < pairs)
            def _():
                acc[rows, :] += _dot_tn(yh_ref[...], dx_ref[...])

            @pl.when(pair_ref[per_pass * p + u] >= pairs)
            def _():
                acc[rows, :] += _dot_tn(ya_ref[...], dx_ref[...])

        def block_rows(u, core):
            return pl.ds(pl.multiple_of(u * 2 * r + core * r, r), r)

        def ship(q):
            return pltpu.make_async_remote_copy(
                src_ref=ship_buf.at[q % per_pass], dst_ref=ra_ref.at[q], send_sem=sa_send.at[q], recv_sem=sa_recv.at[q],
                device_id=(x, y, 1 - c), device_id_type=MESH)

        def send(q):
            return pltpu.make_async_remote_copy(
                src_ref=send_buf.at[q % per_pass], dst_ref=rb_ref.at[q], send_sem=sb_send.at[q], recv_sem=sb_recv.at[q],
                device_id=_xor_peer(x, y, c, chip_masks[q]), device_id_type=MESH)

        def sibling_share(q):
            ship(q).wait_recv()
            fetch = pltpu.make_async_copy(ra_ref.at[q], tmp, loc_sem)
            fetch.start()
            fetch.wait()
            return tmp[...].astype(F32)

        for pp in range(N_CHIPS // per_pass):
            @pl.when((step == ns - 1) & (p == pp))
            def _():
                chips = [per_pass * pp + u for u in range(per_pass)]
                for u, q in enumerate(chips):
                    if q >= per_pass:
                        ship(q - per_pass).wait_send()
                    ship_buf[u] = acc[block_rows(u, 1 - c), :].astype(BF16)
                    ship(q).start()
                for u, q in enumerate(chips):
                    total = acc[block_rows(u, c), :] + sibling_share(q)
                    if q < N_CHIPS - 1:
                        if q >= per_pass:
                            send(q - per_pass).wait_send()
                        send_buf[u] = total.astype(BF16)
                        send(q).start()
                    else:
                        keep_buf[...] = total
                        keep = pltpu.make_async_copy(keep_buf, own_ref, loc_sem)
                        keep.start()
                        keep.wait()
                if pp == N_CHIPS // per_pass - 1:
                    for q in chips:
                        ship(q).wait_send()
                    for q in range(N_CHIPS - 1):
                        if q >= N_CHIPS - 1 - per_pass:
                            send(q).wait_send()
                        send(q).wait_recv()

    def y_spec(u, lo):
        return pl.BlockSpec((ts, 2 * r), lambda p, st, o: (st, jnp.clip(o[per_pass * p + u] - lo, 0, pairs - 1)))

    grid_spec = pltpu.PrefetchScalarGridSpec(
        num_scalar_prefetch=1, grid=(N_CHIPS // per_pass, ns),
        in_specs=[y_spec(0, 0), y_spec(0, pairs), y_spec(1, 0), y_spec(1, pairs),
                  pl.BlockSpec((ts, d), lambda p, st, o: (st, 0))],
        out_specs=(ANY, ANY, ANY),
        scratch_shapes=[pltpu.VMEM((per_pass * 2 * r, d), F32), pltpu.VMEM((r, d), F32),
                        pltpu.VMEM((per_pass, r, d), BF16)] + _scatter_scratch(r, d)[1:])
    own, _, rb = pl.pallas_call(
        body, name="dwout_scatter", grid_spec=grid_spec, out_shape=_scatter_out(r, d),
        compiler_params=_params("arbitrary", "arbitrary"))(_block_order(chip_masks) // 2, y_h, y_a, y_h, y_a, dxb)
    return own, rb


def _sum_chips_adamw(own, recv, w, m, v):
    r, c = w.shape
    tr = _tile(r, 128)

    def body(own_ref, rc_ref, w_ref, m_ref, v_ref, g_ref, d_ref, mo_ref, vo_ref):
        g = own_ref[...]
        for q in range(N_CHIPS - 1):
            g = g + rc_ref[q].astype(F32)
        g_ref[...] = g
        d_ref[...], mo_ref[...], vo_ref[...] = _adamw(w_ref[...], g, m_ref[...], v_ref[...])

    blk = pl.BlockSpec((tr, c), lambda i: (i, 0))
    shp = jax.ShapeDtypeStruct((r, c), F32)
    return pl.pallas_call(
        body, name="sum_chips_adamw", grid=(r // tr,), out_shape=(shp, shp, shp, shp),
        in_specs=[blk, pl.BlockSpec((N_CHIPS - 1, tr, c), lambda i: (0, i, 0)), blk, blk, blk],
        out_specs=(blk, blk, blk, blk), compiler_params=_params("parallel"))(own, recv, w, m, v)


SMALL_ROWS = 8
ROW_LB = 4
ROW_GN = 6
ROW_LOSS = 7


def _small_allreduce_adamw(part, w, m, v, lb_logits):
    width = part.shape[1]

    def body(p_ref, w_ref, m_ref, v_ref, lb_ref, g_ref, d_ref, mo_ref, vo_ref, buf, send_sems, recv_sems):
        x, y, c = _position()
        me = 4 * x + 2 * y + c
        buf[me] = p_ref[...]
        copies = []
        for k in range(N_DEV - 1):
            bx, by, bc = ((k + 1) >> 2) & 1, ((k + 1) >> 1) & 1, (k + 1) & 1
            peer = (x ^ bx, y ^ by, c ^ bc)
            copies.append(pltpu.make_async_remote_copy(
                src_ref=p_ref, dst_ref=buf.at[me], send_sem=send_sems.at[k], recv_sem=recv_sems.at[k],
                device_id=peer, device_id_type=MESH))
        for cp in copies:
            cp.start()
        for cp in copies:
            cp.wait_recv()
        for cp in copies:
            cp.wait_send()
        tot = buf[0]
        for dev in range(1, N_DEV):
            tot = tot + buf[dev]
        lbv = lb_ref[...]
        lb = _sigmoid(lbv[0:1] - lbv[1:2])
        glb = tot[ROW_LB:ROW_LB + 1] * lb * (1.0 - lb)
        loss = jnp.sum(tot[ROW_LOSS:ROW_LOSS + 1], axis=-1, keepdims=True)
        row = lax.broadcasted_iota(jnp.int32, (SMALL_ROWS, width), 0)
        g = jnp.where(row == ROW_LB, glb, jnp.where(row == ROW_LB + 1, -glb, tot))
        g = jnp.where(row == ROW_LOSS, loss, g)
        g_ref[...] = g
        d_ref[...], mo_ref[...], vo_ref[...] = _adamw(w_ref[...], g, m_ref[...], v_ref[...])

    vm = pl.BlockSpec(memory_space=pltpu.VMEM)
    shp = jax.ShapeDtypeStruct((SMALL_ROWS, width), F32)
    return pl.pallas_call(
        body, name="small_allreduce_adamw", out_shape=(shp, shp, shp, shp),
        in_specs=[vm] * 5, out_specs=(vm, vm, vm, vm),
        scratch_shapes=[pltpu.VMEM((N_DEV, SMALL_ROWS, width), F32), pltpu.SemaphoreType.DMA((N_DEV - 1,)),
                        pltpu.SemaphoreType.DMA((N_DEV - 1,))],
    )(part, w, m, v, lb_logits)


def _pack_small(norm_gain, final_gain, lb2, gnorm, last_row, width):
    pad = lambda a: jnp.pad(a.reshape(1, -1), ((0, 0), (0, width - a.size)))
    return jnp.concatenate([norm_gain.reshape(2, width), final_gain.reshape(2, width), lb2.reshape(2, width),
                            pad(gnorm), last_row.reshape(1, width)], axis=0)


def _unpack_small(p, d, e, hd):
    return (p[0:2].reshape(1, d), p[2:4].reshape(d), p[4:6].reshape(2, e), p[6:7, :hd].reshape(1, hd))


def kernel(x, norm_gain, w_in, lb_logits, hgrn_gnorm, w_out, final_gain, loss_target, m_norm_gain, m_w_in, m_lb_logits, m_hgrn_gnorm, m_w_out, m_final_gain, v_norm_gain, v_w_in, v_lb_logits, v_hgrn_gnorm, v_w_out, v_final_gain):
    s, d = x.shape[1], x.shape[2]
    e = w_in.shape[2]
    assert d == 2 * e and lb_logits.shape == (2, e) and w_out.shape[1] * N_DEV == 2 * e
    x2d = x.reshape(s, d)
    tgt = loss_target.reshape(s, d)

    h = _rmsnorm_fwd(x2d, norm_gain)
    z, w_in_full, w_out_full = _inproj_gather(h, _cast_bf16(w_in[0]), _cast_bf16(w_out[0]))
    w_out_full = w_out_full.reshape(2 * e, d)
    y_h, states = _hgrn_fwd(z, lb_logits, hgrn_gnorm)
    o_attn, lse, y_a = _attn_fwd(z)
    dx2, dx2b, dy, loss_vec, dfg = _outproj_loss(x2d, y_h, y_a, w_out_full, final_gain.reshape(1, d), tgt)

    own_o, recv_o = _dwout_scatter(y_h, y_a, dx2b)
    dza = _attn_bwd(z, dy, o_attn, lse)
    dzh, dlb, dgn = _hgrn_bwd(z, dy, states, lb_logits, hgrn_gnorm)
    grad_x, dng = _dh_dx(dzh, dza, w_in_full, x2d, norm_gain, dx2)
    g_wo, d_wo, nm_wo, nv_wo = _sum_chips_adamw(own_o, recv_o, w_out[0], m_w_out[0], v_w_out[0])

    width = d // 2
    zero_row = jnp.zeros((1, width), F32)
    loss_row = loss_vec[:, :width] + loss_vec[:, width:]
    part = _pack_small(dng, dfg, jnp.concatenate([dlb, zero_row], axis=0), dgn, loss_row, width)
    pw = _pack_small(norm_gain, final_gain, lb_logits, hgrn_gnorm, zero_row, width)
    pm = _pack_small(m_norm_gain, m_final_gain, m_lb_logits, m_hgrn_gnorm, zero_row, width)
    pv = _pack_small(v_norm_gain, v_final_gain, v_lb_logits, v_hgrn_gnorm, zero_row, width)
    sg, sd, sm, sv = _small_allreduce_adamw(part, pw, pm, pv, lb_logits)
    own_i, recv_i = _dwin_scatter(h, dzh, dza)
    g_wi, d_wi, nm_wi, nv_wi = _sum_chips_adamw(own_i, recv_i, w_in[0], m_w_in[0], v_w_in[0])
    hd = hgrn_gnorm.shape[1]
    g_ng, g_fg, g_lb, g_gn = _unpack_small(sg, d, e, hd)
    d_ng, d_fg, d_lb, d_gn = _unpack_small(sd, d, e, hd)
    m_ng, m_fg, m_lb, m_gn = _unpack_small(sm, d, e, hd)
    v_ng, v_fg, v_lb, v_gn = _unpack_small(sv, d, e, hd)
    loss = sg[ROW_LOSS, 0]

    one = lambda a: a[None]
    return (loss, grad_x.reshape(1, s, d), g_ng, one(g_wi), g_lb, g_gn, one(g_wo), g_fg,
            d_ng, one(d_wi), d_lb, d_gn, one(d_wo), d_fg,
            m_ng, one(nm_wi), m_lb, m_gn, one(nm_wo), m_fg,
            v_ng, one(nv_wi), v_lb, v_gn, one(nv_wo), v_fg)
```

```python
import functools
import math

import jax
import jax.numpy as jnp
from jax import lax
from jax.experimental import pallas as pl
from jax.experimental.pallas import tpu as pltpu

NORM_EPS = 1e-6
HGRN_HEAD = 128
HGRN_CHUNK = 64
ATTN_HEAD = 64
ATTN_BAND = 128
DILATIONS = (1, 4, 16)
N_SPLITS = 8
N_DEV = 8
ADAM_LR = 0.001
ADAM_B1 = 0.9
ADAM_B2 = 0.999
ADAM_EPS = 1e-08
ADAM_WD = 0.01
ADAM_STEP = 10
LANES = 128
MESH = pl.DeviceIdType.MESH
F32 = jnp.float32
BF16 = jnp.bfloat16
NEG_BIG = -1e30
VMEM_LIMIT = 56 * 1024 * 1024

ANY = pl.BlockSpec(memory_space=pl.ANY)


def _params(*sem):
    return pltpu.CompilerParams(dimension_semantics=sem, vmem_limit_bytes=VMEM_LIMIT)


def _tile(n, pref):
    t = min(n, pref)
    assert n % t == 0, (n, pref)
    return t


def _dot(a, b, precision=None):
    return jnp.dot(a, b, preferred_element_type=F32, precision=precision)


def _dot_nt(a, b):
    return lax.dot_general(a, b, (((1,), (1,)), ((), ())), preferred_element_type=F32)


def _dot_tn(a, b):
    return lax.dot_general(a, b, (((0,), (0,)), ((), ())), preferred_element_type=F32)


def _sigmoid(x):
    return 0.5 * jnp.tanh(0.5 * x) + 0.5


def _dsilu(x, s):
    return s * (1.0 + x * (1.0 - s))


def _adamw(w, g, m, v):
    m = ADAM_B1 * m + (1.0 - ADAM_B1) * g
    v = ADAM_B2 * v + (1.0 - ADAM_B2) * (g * g)
    m_hat = m / (1.0 - ADAM_B1 ** ADAM_STEP)
    v_hat = v / (1.0 - ADAM_B2 ** ADAM_STEP)
    delta = -ADAM_LR * (m_hat / (jnp.sqrt(v_hat) + ADAM_EPS) + ADAM_WD * w)
    return delta, m, v


def _cast_bf16(a):
    r, c = a.shape
    tr = _tile(r, 256)

    def body(a_ref, o_ref):
        o_ref[...] = a_ref[...].astype(BF16)

    return pl.pallas_call(
        body, name="cast_bf16", grid=(r // tr,), out_shape=jax.ShapeDtypeStruct((r, c), BF16),
        in_specs=[pl.BlockSpec((tr, c), lambda i: (i, 0))], out_specs=pl.BlockSpec((tr, c), lambda i: (i, 0)),
        compiler_params=_params("parallel"))(a)


def _rmsnorm_fwd(x, gain):
    s, d = x.shape
    tm = _tile(s, 512)

    def body(x_ref, g_ref, h_ref):
        xv = x_ref[...]
        r = lax.rsqrt(jnp.mean(xv * xv, axis=-1, keepdims=True) + NORM_EPS)
        h_ref[...] = (xv * r * g_ref[...]).astype(BF16)

    return pl.pallas_call(
        body, name="rmsnorm_fwd", grid=(s // tm,), out_shape=jax.ShapeDtypeStruct((s, d), BF16),
        in_specs=[pl.BlockSpec((tm, d), lambda i: (i, 0)), pl.BlockSpec((1, d), lambda i: (0, 0))],
        out_specs=pl.BlockSpec((tm, d), lambda i: (i, 0)), compiler_params=_params("parallel"))(x, gain)


HGRN_BLOCK = 2048
TRI_ROWS = 256


def _chunk_masks():
    tb = TRI_ROWS
    row = lax.broadcasted_iota(jnp.int32, (tb, tb), 0)
    col = lax.broadcasted_iota(jnp.int32, (tb, tb), 1)
    same = (row // HGRN_CHUNK) == (col // HGRN_CHUNK)
    lower = jnp.where(same & (col <= row), 1.0, 0.0).astype(BF16)
    upper = jnp.where(same & (col >= row), 1.0, 0.0).astype(BF16)
    return lower, upper


def _split3(a):
    hi = a.astype(BF16).astype(F32)
    mid = (a - hi).astype(BF16).astype(F32)
    lo = (a - hi - mid).astype(BF16).astype(F32)
    return hi, mid, lo


def _tri_dot(tri, x):
    hi, mid, lo = (p.astype(BF16) for p in _split3(x))
    outs = []
    for r in range(0, x.shape[0], TRI_ROWS):
        sl = slice(r, r + TRI_ROWS)
        outs.append(_dot(tri, hi[sl]) + _dot(tri, mid[sl]) + _dot(tri, lo[sl]))
    return outs[0] if len(outs) == 1 else jnp.concatenate(outs, axis=0)


def _hgrn_gates(qp, fp, lbv):
    lb = _sigmoid(lbv[0:1] - lbv[1:2])
    sq = _sigmoid(qp)
    q = qp * sq
    sg = _sigmoid(fp)
    f = lb + (1.0 - lb) * sg
    k = 1.0 - f
    return lb, sq, q, sg, f, k


def _hgrn_fwd(z, lb_logits, gnorm):
    s = z.shape[0]
    e = z.shape[1] // N_SPLITS
    nh = e // HGRN_HEAD
    tb = _tile(s, HGRN_BLOCK)
    nc = tb // HGRN_CHUNK
    nb = s // tb
    C = HGRN_CHUNK

    def body(q_ref, f_ref, i_ref, g_ref, lb_ref, gn_ref, y_ref, st_ref, state, o_scr):
        @pl.when(pl.program_id(1) == 0)
        def _():
            state[...] = jnp.zeros_like(state)

        lb, sq, q, sg, f, k = _hgrn_gates(q_ref[...], f_ref[...], lb_ref[...])
        lower, _ = _chunk_masks()
        b = _tri_dot(lower, jnp.log(f))
        b3 = b.reshape(nc, C, HGRN_HEAD)
        bc = b3[:, C - 1:C, :]
        qt = (q * jnp.exp(b)).astype(BF16)
        kt = (k * jnp.exp(-b)).astype(BF16)
        ke = (k.reshape(nc, C, HGRN_HEAD) * jnp.exp(bc - b3)).reshape(tb, HGRN_HEAD).astype(BF16)
        v = i_ref[...].astype(BF16)
        tri = lax.broadcasted_iota(jnp.int32, (C, C), 1) <= lax.broadcasted_iota(jnp.int32, (C, C), 0)
        sls = [slice(c * C, (c + 1) * C) for c in range(nc)]
        kv = [_dot_tn(v[sl], ke[sl]) for sl in sls]
        a = [jnp.where(tri, _dot_nt(qt[sl], kt[sl]), 0.0).astype(BF16) for sl in sls]
        st = state[...]
        sts = []
        for c in range(nc):
            sts.append(st)
            st_ref[c] = st
            st = st * jnp.exp(bc[c]) + kv[c]
        state[...] = st
        for c, sl in enumerate(sls):
            o_scr[sl, :] = _dot(a[c], v[sl]) + _dot_nt(qt[sl], sts[c].astype(BF16))
        o = o_scr[...]
        rms = lax.rsqrt(jnp.mean(o * o, axis=-1, keepdims=True) + NORM_EPS)
        gp = g_ref[...]
        y_ref[...] = (o * rms * gn_ref[...] * (gp * _sigmoid(gp))).astype(BF16)

    col = lambda kk: (lambda h, n: (n, kk * nh + h))
    return pl.pallas_call(
        body, name="hgrn_fwd", grid=(nh, nb),
        out_shape=(jax.ShapeDtypeStruct((s, e), BF16),
                   jax.ShapeDtypeStruct((nh, s // C, HGRN_HEAD, HGRN_HEAD), F32)),
        in_specs=[pl.BlockSpec((tb, HGRN_HEAD), col(0)), pl.BlockSpec((tb, HGRN_HEAD), col(1)),
                  pl.BlockSpec((tb, HGRN_HEAD), col(2)), pl.BlockSpec((tb, HGRN_HEAD), col(3)),
                  pl.BlockSpec((2, HGRN_HEAD), lambda h, n: (0, h)), pl.BlockSpec((1, HGRN_HEAD), lambda h, n: (0, 0))],
        out_specs=(pl.BlockSpec((tb, HGRN_HEAD), lambda h, n: (n, h)),
                   pl.BlockSpec((None, nc, HGRN_HEAD, HGRN_HEAD), lambda h, n: (h, n, 0, 0))),
        scratch_shapes=[pltpu.VMEM((HGRN_HEAD, HGRN_HEAD), F32), pltpu.VMEM((tb, HGRN_HEAD), F32)],
        compiler_params=_params("parallel", "arbitrary"))(z, z, z, z, lb_logits, gnorm)


def _hgrn_bwd(z, dy, states, lb_logits, gnorm):
    s = z.shape[0]
    e = z.shape[1] // N_SPLITS
    nh = e // HGRN_HEAD
    tb = _tile(s, HGRN_BLOCK)
    nc = tb // HGRN_CHUNK
    nb = s // tb
    C = HGRN_CHUNK
    H = HGRN_HEAD

    def body(q_ref, f_ref, i_ref, g_ref, dy_ref, st_ref, lb_ref, gn_ref, dz_ref, dlb_ref, dgn_ref,
             gstate, o_scr, dq_scr, dk_scr, dv_scr, e_scr):
        first = (pl.program_id(0) == 0) & (pl.program_id(1) == 0)

        @pl.when(first)
        def _():
            dgn_ref[...] = jnp.zeros_like(dgn_ref)

        @pl.when(pl.program_id(1) == 0)
        def _():
            gstate[...] = jnp.zeros_like(gstate)
            dlb_ref[...] = jnp.zeros_like(dlb_ref)

        qp = q_ref[...]
        lb, sq, q, sg, f, k = _hgrn_gates(qp, f_ref[...], lb_ref[...])
        lower, upper = _chunk_masks()
        b = _tri_dot(lower, jnp.log(f))
        b3 = b.reshape(nc, C, H)
        bc = b3[:, C - 1:C, :]
        eb = jnp.exp(b)
        enb = jnp.exp(-b)
        eend = jnp.exp(bc - b3).reshape(tb, H)
        qt = (q * eb).astype(BF16)
        kt = (k * enb).astype(BF16)
        ke = (k * eend).astype(BF16)
        v = i_ref[...].astype(BF16)
        tri = lax.broadcasted_iota(jnp.int32, (C, C), 1) <= lax.broadcasted_iota(jnp.int32, (C, C), 0)
        sls = [slice(c * C, (c + 1) * C) for c in range(nc)]
        a = [jnp.where(tri, _dot_nt(qt[sl], kt[sl]), 0.0).astype(BF16) for sl in sls]
        for c, sl in enumerate(sls):
            o_scr[sl, :] = _dot(a[c], v[sl]) + _dot_nt(qt[sl], st_ref[c].astype(BF16))
        o = o_scr[...]
        rms = lax.rsqrt(jnp.mean(o * o, axis=-1, keepdims=True) + NORM_EPS)
        on = o * rms
        gn = gn_ref[...]
        gp = g_ref[...]
        sgg = _sigmoid(gp)
        dyv = dy_ref[...]
        d_on = dyv * (gp * sgg)
        dz_ref[3] = (dyv * on * gn * _dsilu(gp, sgg)).astype(BF16)
        dgn_ref[...] += jnp.sum(d_on * on, axis=0, keepdims=True)
        u = d_on * gn
        do = (rms * (u - on * jnp.mean(u * on, axis=-1, keepdims=True))).astype(BF16)
        gup = [_dot_tn(do[sl], qt[sl]) for sl in sls]
        da = [jnp.where(tri, _dot_nt(do[sl], v[sl]), 0.0).astype(BF16) for sl in sls]
        gt = gstate[...]
        gts = [None] * nc
        for c in reversed(range(nc)):
            gts[c] = gt
            gt = gt * jnp.exp(bc[c]) + gup[c]
        gstate[...] = gt
        for c, sl in enumerate(sls):
            stp = st_ref[c]
            gtb = gts[c].astype(BF16)
            dqt = _dot(da[c], kt[sl]) + _dot(do[sl], stp.astype(BF16))
            dkt = _dot_tn(da[c], qt[sl])
            dks = _dot(v[sl], gtb) * eend[sl]
            dv_scr[sl, :] = _dot_tn(a[c], do[sl]) + _dot_nt(ke[sl], gtb)
            dq_scr[sl, :] = dqt * eb[sl]
            dk_scr[sl, :] = dkt * enb[sl] + dks
            ech = (jnp.sum(k[sl] * dks, axis=0, keepdims=True)
                   + jnp.sum(gts[c] * jnp.exp(bc[c]) * stp, axis=0, keepdims=True))
            e_scr[sl, :] = jnp.broadcast_to(ech, (C, H))
        dq = dq_scr[...]
        dk = dk_scr[...]
        dlf = _tri_dot(upper, q * dq - k * dk) + e_scr[...]
        dft = dlf / f - dk
        dz_ref[0] = (dq * _dsilu(qp, sq)).astype(BF16)
        dz_ref[1] = (dft * (1.0 - lb) * sg * (1.0 - sg)).astype(BF16)
        dz_ref[2] = dv_scr[...].astype(BF16)
        dlb_ref[...] += jnp.sum(dft * (1.0 - sg), axis=0, keepdims=True)

    col = lambda kk: (lambda h, n: (nb - 1 - n, kk * nh + h))
    return pl.pallas_call(
        body, name="hgrn_bwd", grid=(nh, nb),
        out_shape=(jax.ShapeDtypeStruct((4, s, e), BF16), jax.ShapeDtypeStruct((1, e), F32),
                   jax.ShapeDtypeStruct((1, H), F32)),
        in_specs=[pl.BlockSpec((tb, H), col(0)), pl.BlockSpec((tb, H), col(1)),
                  pl.BlockSpec((tb, H), col(2)), pl.BlockSpec((tb, H), col(3)),
                  pl.BlockSpec((tb, H), lambda h, n: (nb - 1 - n, h)),
                  pl.BlockSpec((None, nc, H, H), lambda h, n: (h, nb - 1 - n, 0, 0)),
                  pl.BlockSpec((2, H), lambda h, n: (0, h)), pl.BlockSpec((1, H), lambda h, n: (0, 0))],
        out_specs=(pl.BlockSpec((4, tb, H), lambda h, n: (0, nb - 1 - n, h)),
                   pl.BlockSpec((1, H), lambda h, n: (0, h)), pl.BlockSpec((1, H), lambda h, n: (0, 0))),
        scratch_shapes=[pltpu.VMEM((H, H), F32)] + [pltpu.VMEM((tb, H), F32)] * 5,
        compiler_params=_params("arbitrary", "arbitrary"))(z, z, z, z, dy, states, lb_logits, gnorm)


ATTN_T = 16 * ATTN_BAND
SCALE = ATTN_HEAD ** -0.5
TILE_UNROLL = 2


def _slope(hh, nheads):
    head = (2 * pl.program_id(0) + hh + 1).astype(F32)
    return jnp.exp(jnp.full((1, 1), -8.0 / nheads * math.log(2.0), F32) * head)


def _fill_bias(bias, nheads, delta, edge_ok):
    band = (delta >= 0) & (delta <= ATTN_BAND)
    dist = delta.astype(F32)
    for pi, dil in enumerate(DILATIONS):
        for hh in range(2):
            full = jnp.where(band, -(_slope(hh, nheads) * float(dil)) * dist, NEG_BIG)
            bias[(pi * 2 + hh) * 2] = full
            bias[(pi * 2 + hh) * 2 + 1] = jnp.where(edge_ok, full, NEG_BIG)


def _rows(start, size, stride):
    if stride == 1:
        return pl.ds(pl.multiple_of(start, ATTN_BAND), size)
    return pl.ds(start, size, stride=stride)


def _head_lanes(rows, hh):
    return (lax.broadcasted_iota(jnp.int32, (rows, LANES), 1) // ATTN_HEAD) == hh


def _attn_fwd(z):
    s = z.shape[0]
    e = z.shape[1] // N_SPLITS
    npair = e // LANES
    T = ATTN_T
    assert s % T == 0
    nsb = s // T
    W = ATTN_BAND
    nt = T // W
    HD = ATTN_HEAD
    chunk = 256

    def body(q_ref, kp_ref, kc_ref, vp_ref, vc_ref, g_ref, o_ref, l_ref, y_ref, qa, kbuf, va, bias, accs, ms, lsw):
        sb = pl.program_id(1)
        def stage(i, carry):
            rows = pl.ds(pl.multiple_of(i * chunk, chunk), chunk)
            upper = pl.ds(pl.multiple_of(T + i * chunk, chunk), chunk)
            kbuf[upper, :] = kc_ref[rows, :]
            for hh in range(2):
                mine = _head_lanes(chunk, hh)
                qa[hh, rows, :] = jnp.where(mine, q_ref[rows, :] * SCALE, 0.0)
                va[hh, upper, :] = jnp.where(mine, vc_ref[rows, :], 1.0)
            return carry

        lax.fori_loop(0, T // chunk, stage, 0)

        @pl.when(sb == 0)
        def _():
            def stage_prev(i, carry):
                rows = pl.ds(pl.multiple_of(i * chunk, chunk), chunk)
                kbuf[rows, :] = kp_ref[rows, :]
                for hh in range(2):
                    va[hh, rows, :] = jnp.where(_head_lanes(chunk, hh), vp_ref[rows, :], 1.0)
                return carry

            lax.fori_loop(0, T // chunk, stage_prev, 0)
        qi = lax.broadcasted_iota(jnp.int32, (W, 2 * W), 0)
        kj = lax.broadcasted_iota(jnp.int32, (W, 2 * W), 1)
        _fill_bias(bias, 2 * npair, W + qi - kj, kj >= W)

        def tile(tau, carry):
            first = _head_lanes(W, 0)
            rows, scores = [], []
            for pi, dil in enumerate(DILATIONS):
                r = tau % dil
                ub = tau // dil
                qrows = _rows(r + dil * W * ub, W, dil)
                krows = _rows(T + dil * W * (ub - 1) + r, 2 * W, dil)
                var = jnp.where((sb == 0) & (ub == 0), 1, 0)
                kt = kbuf[krows, :].astype(BF16)
                rows.append((qrows, krows))
                scores.append([_dot_nt(qa[hh, qrows, :].astype(BF16), kt) + bias[(pi * 2 + hh) * 2 + var]
                               for hh in range(2)])
            maxes = [[jnp.max(sc, axis=-1, keepdims=True) for sc in pair] for pair in scores]
            probs = [[jnp.exp(sc - m).astype(BF16) for sc, m in zip(ps, pm)] for ps, pm in zip(scores, maxes)]
            for pi, (qrows, krows) in enumerate(rows):
                outs = [_dot(probs[pi][hh], va[hh, krows, :].astype(BF16)) for hh in range(2)]
                accs[pi, qrows, :] = jnp.where(first, outs[0], outs[1])
                lsw[pi, qrows, :] = jnp.where(first, outs[1], outs[0])
                ms[pi, qrows, :] = jnp.where(first, maxes[pi][0], maxes[pi][1])
            return carry

        lax.fori_loop(0, nt, tile, 0, unroll=TILE_UNROLL)

        def merge(i, carry):
            rows = pl.ds(pl.multiple_of(i * chunk, chunk), chunk)
            m1, m2, m3 = ms[0, rows, :], ms[1, rows, :], ms[2, rows, :]
            mx = jnp.maximum(jnp.maximum(m1, m2), m3)
            w1, w2, w3 = jnp.exp(m1 - mx), jnp.exp(m2 - mx), jnp.exp(m3 - mx)
            unswap = lambda a: pltpu.roll(a, ATTN_HEAD, 1)
            den = w1 * unswap(lsw[0, rows, :]) + w2 * unswap(lsw[1, rows, :]) + w3 * unswap(lsw[2, rows, :])
            o = (w1 * accs[0, rows, :] + w2 * accs[1, rows, :] + w3 * accs[2, rows, :]) / den
            o_ref[rows, :] = o
            l_ref[rows, :] = mx + jnp.log(den)
            gp = g_ref[rows, :]
            y_ref[rows, :] = (o * (gp * _sigmoid(gp))).astype(BF16)
            upper = pl.ds(pl.multiple_of(T + i * chunk, chunk), chunk)
            kbuf[rows, :] = kbuf[upper, :]
            for hh in range(2):
                va[hh, rows, :] = va[hh, upper, :]
            return carry

        lax.fori_loop(0, T // chunk, merge, 0)

    cur = lambda split: (lambda hp, sb: (sb, split * npair + hp))
    prev = lambda split: (lambda hp, sb: (jnp.maximum(sb - 1, 0), split * npair + hp))
    blk = lambda index: pl.BlockSpec((T, LANES), index)
    out = blk(lambda hp, sb: (sb, hp))
    buf = lambda rows: pltpu.VMEM((rows, LANES), F32)
    return pl.pallas_call(
        body, name="attn_fwd", grid=(npair, nsb),
        out_shape=(jax.ShapeDtypeStruct((s, e), F32), jax.ShapeDtypeStruct((s, e), F32), jax.ShapeDtypeStruct((s, e), BF16)),
        in_specs=[blk(cur(4)), blk(prev(5)), blk(cur(5)), blk(prev(6)), blk(cur(6)), blk(cur(7))],
        out_specs=(out, out, out),
        scratch_shapes=[pltpu.VMEM((2, T, LANES), F32), buf(2 * T), pltpu.VMEM((2, 2 * T, LANES), F32),
                        pltpu.VMEM((12, W, 2 * W), F32)] + [pltpu.VMEM((3, T, LANES), F32)] * 3,
        compiler_params=_params("parallel", "arbitrary"))(z, z, z, z, z, z)


def _outproj_loss(x, y_h, y_a, w_out_full, final_gain, target):
    s, d = x.shape
    e = y_h.shape[1]
    tm = _tile(s, 256)

    def body(x_ref, yh_ref, ya_ref, w_ref, g_ref, t_ref, dx_ref, dxb_ref, dy_ref, loss_ref, dg_ref):
        @pl.when(pl.program_id(0) == 0)
        def _():
            loss_ref[...] = jnp.zeros_like(loss_ref)
            dg_ref[...] = jnp.zeros_like(dg_ref)

        w = w_ref[...]
        x2 = x_ref[...] + _dot(yh_ref[...], w[0:e]) + _dot(ya_ref[...], w[e:2 * e])
        r = lax.rsqrt(jnp.mean(x2 * x2, axis=-1, keepdims=True) + NORM_EPS)
        xn = x2 * r
        g = g_ref[...]
        err = xn * g - t_ref[...]
        loss_ref[...] += jnp.sum(err * err, axis=0, keepdims=True) * (0.5 / d)
        dyo = err * (1.0 / d)
        dg_ref[...] += jnp.sum(dyo * xn, axis=0, keepdims=True)
        u = dyo * g
        dx2 = r * (u - xn * jnp.mean(u * xn, axis=-1, keepdims=True))
        dx_ref[...] = dx2
        dxb = dx2.astype(BF16)
        dxb_ref[...] = dxb
        dy_ref[...] = _dot_nt(dxb, w)

    row = pl.BlockSpec((tm, d), lambda i: (i, 0))
    half = pl.BlockSpec((tm, e), lambda i: (i, 0))
    vec = pl.BlockSpec((1, d), lambda i: (0, 0))
    return pl.pallas_call(
        body, name="outproj_loss", grid=(s // tm,),
        out_shape=(jax.ShapeDtypeStruct((s, d), F32), jax.ShapeDtypeStruct((s, d), BF16),
                   jax.ShapeDtypeStruct((s, 2 * e), F32), jax.ShapeDtypeStruct((1, d), F32),
                   jax.ShapeDtypeStruct((1, d), F32)),
        in_specs=[row, half, half, pl.BlockSpec((2 * e, d), lambda i: (0, 0)), vec, row],
        out_specs=(row, row, pl.BlockSpec((tm, 2 * e), lambda i: (i, 0)), vec, vec),
        compiler_params=_params("arbitrary"))(x, y_h, y_a, w_out_full, final_gain, target)


def _attn_bwd(z, dy, o, lse):
    s, e = o.shape
    npair = e // LANES
    T = ATTN_T
    assert s % T == 0
    nsb = s // T
    W = ATTN_BAND
    nt = T // W
    HD = ATTN_HEAD
    chunk = 256

    def body(k_ref, v_ref, qc_ref, qn_ref, dyc_ref, dyn_ref, gc_ref, gn_ref, oc_ref, on_ref, lc_ref, ln_ref,
             dz_ref, qa, doa, ka, va, dqacc, dkacc, dvacc, bias):
        sb = pl.program_id(1)
        def stage_queries(half, q_r, dy_r, g_r, o_r, l_r):
            def stage(i, carry):
                rows = pl.ds(pl.multiple_of(i * chunk, chunk), chunk)
                dst = pl.ds(pl.multiple_of(half * T + i * chunk, chunk), chunk)
                lane = lax.broadcasted_iota(jnp.int32, (chunk, LANES), 1)
                gp = g_r[rows, :]
                dov = dy_r[rows, :] * (gp * _sigmoid(gp))
                qv = q_r[rows, :] * SCALE
                same_head = (lax.broadcasted_iota(jnp.int32, (LANES, LANES), 0) // HD
                             == lax.broadcasted_iota(jnp.int32, (LANES, LANES), 1) // HD)
                ones = jnp.where(same_head, 1.0, 0.0).astype(BF16)
                hi, mid, lo = (p.astype(BF16) for p in _split3(dov * o_r[rows, :]))
                delta = _dot(hi, ones) + _dot(mid, ones) + _dot(lo, ones)
                swap = lambda a: pltpu.roll(a, HD, 1)
                lse_parts = [swap(p) for p in _split3(l_r[rows, :])]
                dl_parts = [swap(p) for p in _split3(delta)]
                for hh in range(2):
                    mine = _head_lanes(chunk, hh)
                    spare = (1 - hh) * HD
                    qh = jnp.where(mine, qv, 0.0)
                    dh = jnp.where(mine, dov, 0.0)
                    for j in range(3):
                        qh = jnp.where(lane == spare + j, lse_parts[j], qh)
                        dh = jnp.where(lane == spare + j, dl_parts[j], dh)
                    qa[hh, dst, :] = qh
                    doa[hh, dst, :] = dh
                return carry

            lax.fori_loop(0, T // chunk, stage, 0)

        @pl.when(sb == 0)
        def _():
            stage_queries(0, qc_ref, dyc_ref, gc_ref, oc_ref, lc_ref)

        stage_queries(1, qn_ref, dyn_ref, gn_ref, on_ref, ln_ref)

        def stage_keys(i, carry):
            rows = pl.ds(pl.multiple_of(i * chunk, chunk), chunk)
            lane = lax.broadcasted_iota(jnp.int32, (chunk, LANES), 1)
            for hh in range(2):
                spare = (1 - hh) * HD
                minus = (lane >= spare) & (lane < spare + 3)
                ka[hh, rows, :] = jnp.where(minus, -1.0, k_ref[rows, :])
                va[hh, rows, :] = jnp.where(minus, -1.0, v_ref[rows, :])
            gp = gc_ref[rows, :]
            dz_ref[3, rows, :] = (dyc_ref[rows, :] * oc_ref[rows, :] * _dsilu(gp, _sigmoid(gp))).astype(BF16)
            return carry

        lax.fori_loop(0, T // chunk, stage_keys, 0)

        @pl.when(sb == 0)
        def _():
            dqacc[0:T, :] = jnp.zeros((T, LANES), F32)

        dqacc[T:, :] = jnp.zeros((T, LANES), F32)
        dkacc[...] = jnp.zeros_like(dkacc)
        dvacc[...] = jnp.zeros_like(dvacc)
        qi = lax.broadcasted_iota(jnp.int32, (2 * W, W), 0)
        kj = lax.broadcasted_iota(jnp.int32, (2 * W, W), 1)
        _fill_bias(bias, 2 * npair, qi - kj, qi < W)

        def tile(tau, carry):
            def scores(step, pi):
                dil = DILATIONS[pi]
                r = step % dil
                ub = step // dil
                start = r + dil * W * ub
                krows = _rows(start, W, dil)
                qrows = _rows(start, 2 * W, dil)
                var = jnp.where((sb == nsb - 1) & (ub == nt // dil - 1), 1, 0)
                unit = dict(krows=krows, qrows=qrows, ops=[], sc=[], dpd=[])
                for hh in range(2):
                    kt = ka[hh, krows, :].astype(BF16)
                    vt = va[hh, krows, :].astype(BF16)
                    qt = qa[hh, qrows, :].astype(BF16)
                    dt = doa[hh, qrows, :].astype(BF16)
                    unit["ops"].append((kt, qt, dt))
                    unit["sc"].append(_dot_nt(qt, kt) + bias[(pi * 2 + hh) * 2 + var])
                    unit["dpd"].append(_dot_nt(dt, vt))
                return unit

            def elementwise(unit):
                ps = [jnp.exp(s_) for s_ in unit["sc"]]
                unit["ds"] = [(p * d).astype(BF16) for p, d in zip(ps, unit["dpd"])]
                unit["pb"] = [p.astype(BF16) for p in ps]

            def products(unit):
                dvs = [_dot_tn(pb, dt) for pb, (kt, qt, dt) in zip(unit["pb"], unit["ops"])]
                dks = [_dot_tn(ds, qt) for ds, (kt, qt, dt) in zip(unit["ds"], unit["ops"])]
                dqs = [_dot(ds, kt) for ds, (kt, qt, dt) in zip(unit["ds"], unit["ops"])]
                dkacc[unit["krows"], :] += jnp.where(_head_lanes(W, 0), dks[0], dks[1])
                dvacc[unit["krows"], :] += jnp.where(_head_lanes(W, 0), dvs[0], dvs[1])
                dqacc[unit["qrows"], :] += jnp.where(_head_lanes(2 * W, 0), dqs[0], dqs[1]) * SCALE

            order = [(2 * tau + half, pi) for half in range(2) for pi in range(len(DILATIONS))]
            units = [None] * len(order)
            for n in range(len(order) + 2):
                if n < len(order):
                    units[n] = scores(*order[n])
                if 1 <= n <= len(order):
                    elementwise(units[n - 1])
                if n >= 2:
                    products(units[n - 2])
            return carry

        lax.fori_loop(0, nt // 2, tile, 0)

        def flush(i, carry):
            rows = pl.ds(pl.multiple_of(i * chunk, chunk), chunk)
            nxt = pl.ds(pl.multiple_of(T + i * chunk, chunk), chunk)
            dz_ref[0, rows, :] = dqacc[rows, :].astype(BF16)
            dz_ref[1, rows, :] = dkacc[rows, :].astype(BF16)
            dz_ref[2, rows, :] = dvacc[rows, :].astype(BF16)
            dqacc[rows, :] = dqacc[nxt, :]
            for hh in range(2):
                qa[hh, rows, :] = qa[hh, nxt, :]
                doa[hh, rows, :] = doa[hh, nxt, :]
            return carry

        lax.fori_loop(0, T // chunk, flush, 0)

    zc = lambda split: (lambda hp, sb: (sb, split * npair + hp))
    zn = lambda split: (lambda hp, sb: (jnp.minimum(sb + 1, nsb - 1), split * npair + hp))
    ec = lambda off: (lambda hp, sb: (sb, off + hp))
    en = lambda off: (lambda hp, sb: (jnp.minimum(sb + 1, nsb - 1), off + hp))
    blk = lambda index: pl.BlockSpec((T, LANES), index)
    buf = lambda rows: pltpu.VMEM((rows, LANES), F32)
    return pl.pallas_call(
        body, name="attn_bwd", grid=(npair, nsb), out_shape=jax.ShapeDtypeStruct((4, s, e), BF16),
        in_specs=[blk(zc(5)), blk(zc(6)), blk(zc(4)), blk(zn(4)), blk(ec(npair)), blk(en(npair)),
                  blk(zc(7)), blk(zn(7)), blk(ec(0)), blk(en(0)), blk(ec(0)), blk(en(0))],
        out_specs=pl.BlockSpec((4, T, LANES), lambda hp, sb: (0, sb, hp)),
        scratch_shapes=[pltpu.VMEM((2, 2 * T, LANES), F32), pltpu.VMEM((2, 2 * T, LANES), F32),
                        pltpu.VMEM((2, T, LANES), F32), pltpu.VMEM((2, T, LANES), F32),
                        buf(2 * T), buf(T), buf(T), pltpu.VMEM((12, 2 * W, W), F32)],
        compiler_params=_params("parallel", "arbitrary"))(z, z, z, z, dy, dy, z, z, o, o, lse, lse)


def _dz_specs(tm, e, axis):
    def mk(lo, hi):
        def index(i, k):
            row, grp = (i, k) if axis == 1 else (k, i)
            return (jnp.clip(grp - lo, 0, hi - lo - 1), row, 0)
        return pl.BlockSpec((None, tm, e), index)
    return [mk(0, 4), mk(4, 8)]


def _dz_pick(grp, dzh_ref, dza_ref, fn):
    @pl.when(grp < 4)
    def _():
        fn(dzh_ref[...])

    @pl.when(grp >= 4)
    def _():
        fn(dza_ref[...])


def _dh_dx(dzh, dza, w_full, x, gain, dx2):
    s, d = x.shape
    e = dzh.shape[2]
    tm = _tile(s, 1024)
    ni = s // tm
    chunk = _tile(tm, 256)
    fetch_at = 2

    def body(dzh_ref, dza_ref, w_ref, x_hbm, g_ref, dx2_hbm, gx_hbm, dg_ref, acc, xbuf, dbuf, sems):
        i, k = pl.program_id(0), pl.program_id(1)
        tile_rows = pl.ds(pl.multiple_of(i * tm, tm), tm)
        fetch_x = pltpu.make_async_copy(x_hbm.at[tile_rows, :], xbuf, sems.at[0])
        fetch_d = pltpu.make_async_copy(dx2_hbm.at[tile_rows, :], dbuf, sems.at[1])
        store = pltpu.make_async_copy(xbuf, gx_hbm.at[tile_rows, :], sems.at[2])

        @pl.when((i == 0) & (k == 0))
        def _():
            dg_ref[...] = jnp.zeros_like(dg_ref)

        @pl.when(k == 0)
        def _():
            acc[...] = jnp.zeros_like(acc)

        @pl.when((k == fetch_at) & (i > 0))
        def _():
            store.wait()

        @pl.when(k == fetch_at)
        def _():
            fetch_x.start()
            fetch_d.start()

        def add(dz):
            acc[...] += _dot_nt(dz, w_ref[...])

        _dz_pick(k, dzh_ref, dza_ref, add)

        @pl.when(k == N_SPLITS - 1)
        def _():
            fetch_x.wait()
            fetch_d.wait()
            gain_row = g_ref[...]

            def finish(c, dg):
                rows = pl.ds(pl.multiple_of(c * chunk, chunk), chunk)
                dh = acc[rows, :]
                xv = xbuf[rows, :]
                r = lax.rsqrt(jnp.mean(xv * xv, axis=-1, keepdims=True) + NORM_EPS)
                xn = xv * r
                u = dh * gain_row
                xbuf[rows, :] = dbuf[rows, :] + r * (u - xn * jnp.mean(u * xn, axis=-1, keepdims=True))
                return dg + jnp.sum(dh * xn, axis=0, keepdims=True)

            dg_ref[...] += lax.fori_loop(0, tm // chunk, finish, jnp.zeros((1, d), F32))
            store.start()

        @pl.when((k == N_SPLITS - 1) & (i == ni - 1))
        def _():
            store.wait()

    vec = pl.BlockSpec((1, d), lambda i, k: (0, 0))
    return pl.pallas_call(
        body, name="dh_dx", grid=(ni, N_SPLITS),
        out_shape=(jax.ShapeDtypeStruct((s, d), F32), jax.ShapeDtypeStruct((1, d), F32)),
        in_specs=_dz_specs(tm, e, 1) + [pl.BlockSpec((None, d, e), lambda i, k: (k, 0, 0)), ANY, vec, ANY],
        out_specs=(ANY, vec),
        scratch_shapes=[pltpu.VMEM((tm, d), F32), pltpu.VMEM((tm, d), F32), pltpu.VMEM((tm, d), F32),
                        pltpu.SemaphoreType.DMA((3,))],
        compiler_params=_params("arbitrary", "arbitrary"))(dzh, dza, w_full, x, gain, dx2)


def _position():
    x, y, c = lax.axis_index("x"), lax.axis_index("y"), lax.axis_index("c")
    return x, y, c


def _xor_peer(x, y, c, mask):
    return (x ^ ((mask >> 2) & 1), y ^ ((mask >> 1) & 1), c ^ (mask & 1))


def _block_order(masks):
    me = 4 * lax.axis_index("x") + 2 * lax.axis_index("y") + lax.axis_index("c")
    return jnp.stack([me ^ m for m in masks]).astype(jnp.int32)


GATHER_MASKS = (0, 1, 4, 2, 6, 5, 3, 7)


def _inproj_gather(h, w_loc, wo_loc):
    s, d = h.shape
    e = w_loc.shape[1]
    tm = _tile(s, 1024)
    ni = s // tm
    pre = max(ni - 2, 0)

    def body(order_ref, h_ref, w_ref, wo_ref, z_ref, wf_ref, wof_ref, wbuf, send_sems, recv_sems, osend, orecv,
             local_sems, wsems):
        j, i = pl.program_id(0), pl.program_id(1)
        x, y, c = _position()
        me, sibling = (x, y, c), (x, y, 1 - c)
        chips = [(1 - x, y), (x, 1 - y), (1 - x, 1 - y)]
        blk = lambda p: 4 * p[0] + 2 * p[1] + p[2]

        def copy(k, block, to, src=None):
            dst = wf_ref.at[blk(block)]
            return pltpu.make_async_remote_copy(
                src_ref=dst if src is None else src, dst_ref=dst, send_sem=send_sems.at[k], recv_sem=recv_sems.at[k],
                device_id=to, device_id_type=MESH)

        first = [copy(0, me, sibling, src=w_ref)] + [copy(1 + q, me, (*chip, c), src=w_ref) for q, chip in enumerate(chips)]
        passed = [copy(4 + q, (*chip, c), sibling) for q, chip in enumerate(chips)]
        mine = pltpu.make_async_copy(w_ref, wf_ref.at[blk(me)], local_sems.at[0])
        ocopies = [pltpu.make_async_remote_copy(
            src_ref=wo_ref, dst_ref=wof_ref.at[blk(me)], send_sem=osend.at[k], recv_sem=orecv.at[k],
            device_id=_xor_peer(x, y, c, k + 1), device_id_type=MESH) for k in range(N_DEV - 1)]
        omine = pltpu.make_async_copy(wo_ref, wof_ref.at[blk(me)], local_sems.at[1])
        blocks = [me, sibling] + [(*chip, c) for chip in chips] + [(*chip, 1 - c) for chip in chips]
        arrive = [None, copy(0, sibling, me)] + [copy(1 + q, (*chip, c), me) for q, chip in enumerate(chips)] \
            + [copy(4 + q, (*chip, 1 - c), me) for q, chip in enumerate(chips)]
        forward = [None, None] + passed + [None, None, None]

        def load(slot, src):
            return pltpu.make_async_copy(src, wbuf.at[slot], wsems.at[slot])

        @pl.when((j == 0) & (i == 0))
        def _():
            for cp in [mine, omine] + first + ocopies:
                cp.start()
            load(0, w_ref).start()

        for jj in range(N_DEV):
            @pl.when((j == jj) & (i == 0))
            def _():
                load(jj % 2, w_ref).wait()

            if jj + 1 < N_DEV:
                @pl.when((j == jj) & (i == pre))
                def _():
                    arrive[jj + 1].wait_recv()
                    if forward[jj + 1] is not None:
                        forward[jj + 1].start()
                    load((jj + 1) % 2, wf_ref.at[blk(blocks[jj + 1])]).start()

        z_ref[...] = _dot(h_ref[...], wbuf[j % 2])

        @pl.when((j == N_DEV - 1) & (i == ni - 1))
        def _():
            for cp in first + passed:
                cp.wait_send()
            for cp in ocopies:
                cp.wait_send()
                cp.wait_recv()
            mine.wait()
            omine.wait()

    grid_spec = pltpu.PrefetchScalarGridSpec(
        num_scalar_prefetch=1, grid=(N_DEV, ni),
        in_specs=[pl.BlockSpec((tm, d), lambda j, i, o: (i, 0)), ANY, ANY],
        out_specs=(pl.BlockSpec((tm, e), lambda j, i, o: (i, o[j])), ANY, ANY),
        scratch_shapes=[pltpu.VMEM((2, d, e), BF16), pltpu.SemaphoreType.DMA((7,)), pltpu.SemaphoreType.DMA((7,)),
                        pltpu.SemaphoreType.DMA((7,)), pltpu.SemaphoreType.DMA((7,)), pltpu.SemaphoreType.DMA((2,)),
                        pltpu.SemaphoreType.DMA((2,))])
    return pl.pallas_call(
        body, name="inproj_gather", grid_spec=grid_spec,
        out_shape=(jax.ShapeDtypeStruct((s, N_SPLITS * e), F32), jax.ShapeDtypeStruct((N_DEV, d, e), BF16),
                   jax.ShapeDtypeStruct((N_DEV,) + wo_loc.shape, BF16)),
        compiler_params=_params("arbitrary", "arbitrary"))(_block_order(GATHER_MASKS), h, w_loc, wo_loc)


SCATTER_MASKS = (7, 6, 5, 4, 3, 2, 1, 0)
N_CHIPS = 4


def _scatter_block(k, acc, stage, tmp, own_ref, ra_ref, rb_ref, sa_send, sa_recv, sb_send, sb_recv, loc_sem, last):
    x, y, c = _position()
    chip_of = lambda t: _xor_peer(x, y, c, SCATTER_MASKS[2 * t + 1])

    def ship(t):
        return pltpu.make_async_remote_copy(
            src_ref=stage.at[0], dst_ref=ra_ref.at[t], send_sem=sa_send.at[t], recv_sem=sa_recv.at[t],
            device_id=(x, y, 1 - c), device_id_type=MESH)

    def send(t):
        return pltpu.make_async_remote_copy(
            src_ref=stage.at[1], dst_ref=rb_ref.at[t], send_sem=sb_send.at[t], recv_sem=sb_recv.at[t],
            device_id=chip_of(t), device_id_type=MESH)

    for kk in range(N_DEV):
        t = kk // 2

        @pl.when(last & (k == kk))
        def _():
            if kk % 2 == 0:
                if t >= 1:
                    ship(t - 1).wait_send()
                stage[0] = acc[...].astype(BF16)
                ship(t).start()
            else:
                ship(t).wait_recv()
                fetch = pltpu.make_async_copy(ra_ref.at[t], tmp, loc_sem)
                fetch.start()
                fetch.wait()
                acc[...] += tmp[...].astype(F32)
                if t < N_CHIPS - 1:
                    if t >= 1:
                        send(t - 1).wait_send()
                    stage[1] = acc[...].astype(BF16)
                    send(t).start()
                else:
                    keep = pltpu.make_async_copy(acc, own_ref, loc_sem)
                    keep.start()
                    keep.wait()
                    ship(t).wait_send()
                    send(t - 1).wait_send()
                    for q in range(N_CHIPS - 1):
                        send(q).wait_recv()


def _scatter_scratch(rows, cols):
    return [pltpu.VMEM((rows, cols), F32), pltpu.VMEM((2, rows, cols), BF16), pltpu.VMEM((rows, cols), BF16),
            pltpu.SemaphoreType.DMA((N_CHIPS,)), pltpu.SemaphoreType.DMA((N_CHIPS,)),
            pltpu.SemaphoreType.DMA((N_CHIPS - 1,)), pltpu.SemaphoreType.DMA((N_CHIPS - 1,)), pltpu.SemaphoreType.DMA(())]


def _scatter_out(rows, cols):
    return (jax.ShapeDtypeStruct((rows, cols), F32), jax.ShapeDtypeStruct((N_CHIPS, rows, cols), BF16),
            jax.ShapeDtypeStruct((N_CHIPS - 1, rows, cols), BF16))


def _dwin_scatter(h, dzh, dza):
    s, d = h.shape
    e = dzh.shape[2]
    ts = _tile(s, 1024)
    ns = s // ts

    def body(order_ref, dzh_ref, dza_ref, h_ref, own_ref, ra_ref, rb_ref, acc, stage, tmp, *sems):
        k, step = pl.program_id(0), pl.program_id(1)

        @pl.when(step == 0)
        def _():
            acc[...] = jnp.zeros_like(acc)

        def add(dz):
            acc[...] += _dot_tn(h_ref[...], dz)

        _dz_pick(order_ref[k], dzh_ref, dza_ref, add)
        _scatter_block(k, acc, stage, tmp, own_ref, ra_ref, rb_ref, *sems, step == ns - 1)

    def dz_spec(lo):
        return pl.BlockSpec((None, ts, e), lambda k, st, o: (jnp.clip(o[k] - lo, 0, 3), st, 0))

    grid_spec = pltpu.PrefetchScalarGridSpec(
        num_scalar_prefetch=1, grid=(N_DEV, ns),
        in_specs=[dz_spec(0), dz_spec(4), pl.BlockSpec((ts, d), lambda k, st, o: (st, 0))],
        out_specs=(ANY, ANY, ANY), scratch_shapes=_scatter_scratch(d, e))
    own, _, rb = pl.pallas_call(
        body, name="dwin_scatter", grid_spec=grid_spec, out_shape=_scatter_out(d, e),
        compiler_params=_params("arbitrary", "arbitrary"))(_block_order(SCATTER_MASKS), dzh, dza, h)
    return own, rb


def _dwout_scatter(y_h, y_a, dxb):
    s, e = y_h.shape
    d = dxb.shape[1]
    r = 2 * e // N_DEV
    pairs = e // (2 * r)
    ts = _tile(s, 1024)
    ns = s // ts
    chip_masks = SCATTER_MASKS[1::2]
    passes = ((0, 1), (2,), (3,))
    slots = max(len(chips) for chips in passes)
    slot_chip = [chips[min(u, len(chips) - 1)] for chips in passes for u in range(slots)]

    def body(pair_ref, yh0_ref, ya0_ref, yh1_ref, ya1_ref, dx_ref, own_ref, ra_ref, rb_ref, acc, keep_buf, ship_buf,
             send_buf, tmp, sa_send, sa_recv, sb_send, sb_recv, loc_sem):
        p, step = pl.program_id(0), pl.program_id(1)
        x, y, c = _position()

        @pl.when(step == 0)
        def _():
            acc[...] = jnp.zeros_like(acc)

        for u, (yh_ref, ya_ref) in enumerate(((yh0_ref, ya0_ref), (yh1_ref, ya1_ref))):
            rows = slice(u * 2 * r, (u + 1) * 2 * r)
            used = functools.reduce(jnp.logical_or, [p == pp for pp, chips in enumerate(passes) if u < len(chips)])

            @pl.when(used & (pair_ref[slots * p + u] < pairs))
            def _():
                acc[rows, :] += _dot_tn(yh_ref[...], dx_ref[...])

            @pl.when(used & (pair_ref[slots * p + u] >= pairs))
            def _():
                acc[rows, :] += _dot_tn(ya_ref[...], dx_ref[...])

        def block_rows(u, core):
            return pl.ds(pl.multiple_of(u * 2 * r + core * r, r), r)

        slot_of = {q: u for chips in passes for u, q in enumerate(chips)}

        def ship(q):
            return pltpu.make_async_remote_copy(
                src_ref=ship_buf.at[slot_of[q]], dst_ref=ra_ref.at[q], send_sem=sa_send.at[q], recv_sem=sa_recv.at[q],
                device_id=(x, y, 1 - c), device_id_type=MESH)

        def send(q):
            return pltpu.make_async_remote_copy(
                src_ref=send_buf.at[slot_of[q]], dst_ref=rb_ref.at[q], send_sem=sb_send.at[q], recv_sem=sb_recv.at[q],
                device_id=_xor_peer(x, y, c, chip_masks[q]), device_id_type=MESH)

        def sibling_share(q):
            ship(q).wait_recv()
            fetch = pltpu.make_async_copy(ra_ref.at[q], tmp, loc_sem)
            fetch.start()
            fetch.wait()
            return tmp[...].astype(F32)

        shipped, sent = {}, {}
        for pp, chips in enumerate(passes):
            @pl.when((step == ns - 1) & (p == pp))
            def _():
                for u, q in enumerate(chips):
                    if u in shipped:
                        ship(shipped.pop(u)).wait_send()
                    ship_buf[u] = acc[block_rows(u, 1 - c), :].astype(BF16)
                    ship(q).start()
                    shipped[u] = q
                for u, q in enumerate(chips):
                    total = acc[block_rows(u, c), :] + sibling_share(q)
                    if q < N_CHIPS - 1:
                        if u in sent:
                            send(sent.pop(u)).wait_send()
                        send_buf[u] = total.astype(BF16)
                        send(q).start()
                        sent[u] = q
                    else:
                        keep_buf[...] = total
                        keep = pltpu.make_async_copy(keep_buf, own_ref, loc_sem)
                        keep.start()
                        keep.wait()
                if pp == len(passes) - 1:
                    for q in shipped.values():
                        ship(q).wait_send()
                    for q in sent.values():
                        send(q).wait_send()
                    for q in range(N_CHIPS - 1):
                        send(q).wait_recv()

    def y_spec(u, lo):
        return pl.BlockSpec((ts, 2 * r), lambda p, st, o: (st, jnp.clip(o[slots * p + u] - lo, 0, pairs - 1)))

    pair_of_chip = _block_order(chip_masks) // 2
    grid_spec = pltpu.PrefetchScalarGridSpec(
        num_scalar_prefetch=1, grid=(len(passes), ns),
        in_specs=[y_spec(0, 0), y_spec(0, pairs), y_spec(1, 0), y_spec(1, pairs),
                  pl.BlockSpec((ts, d), lambda p, st, o: (st, 0))],
        out_specs=(ANY, ANY, ANY),
        scratch_shapes=[pltpu.VMEM((slots * 2 * r, d), F32), pltpu.VMEM((r, d), F32),
                        pltpu.VMEM((slots, r, d), BF16)] + _scatter_scratch(r, d)[1:])
    own, _, rb = pl.pallas_call(
        body, name="dwout_scatter", grid_spec=grid_spec, out_shape=_scatter_out(r, d),
        compiler_params=_params("arbitrary", "arbitrary"))(
            jnp.stack([pair_of_chip[q] for q in slot_chip]), y_h, y_a, y_h, y_a, dxb)
    return own, rb


def _sum_chips_adamw(own, recv, w, m, v):
    r, c = w.shape
    tr = _tile(r, 128)

    def body(own_ref, rc_ref, w_ref, m_ref, v_ref, g_ref, d_ref, mo_ref, vo_ref):
        g = own_ref[...]
        for q in range(N_CHIPS - 1):
            g = g + rc_ref[q].astype(F32)
        g_ref[...] = g
        d_ref[...], mo_ref[...], vo_ref[...] = _adamw(w_ref[...], g, m_ref[...], v_ref[...])

    blk = pl.BlockSpec((tr, c), lambda i: (i, 0))
    shp = jax.ShapeDtypeStruct((r, c), F32)
    return pl.pallas_call(
        body, name="sum_chips_adamw", grid=(r // tr,), out_shape=(shp, shp, shp, shp),
        in_specs=[blk, pl.BlockSpec((N_CHIPS - 1, tr, c), lambda i: (0, i, 0)), blk, blk, blk],
        out_specs=(blk, blk, blk, blk), compiler_params=_params("parallel"))(own, recv, w, m, v)


SMALL_ROWS = 8
ROW_LB = 4
ROW_GN = 6
ROW_LOSS = 7


def _small_allreduce_adamw(part, w, m, v, lb_logits):
    width = part.shape[1]

    def body(p_ref, w_ref, m_ref, v_ref, lb_ref, g_ref, d_ref, mo_ref, vo_ref, buf, send_sems, recv_sems):
        x, y, c = _position()
        me = 4 * x + 2 * y + c
        buf[me] = p_ref[...]
        copies = []
        for k in range(N_DEV - 1):
            bx, by, bc = ((k + 1) >> 2) & 1, ((k + 1) >> 1) & 1, (k + 1) & 1
            peer = (x ^ bx, y ^ by, c ^ bc)
            copies.append(pltpu.make_async_remote_copy(
                src_ref=p_ref, dst_ref=buf.at[me], send_sem=send_sems.at[k], recv_sem=recv_sems.at[k],
                device_id=peer, device_id_type=MESH))
        for cp in copies:
            cp.start()
        for cp in copies:
            cp.wait_recv()
        for cp in copies:
            cp.wait_send()
        tot = buf[0]
        for dev in range(1, N_DEV):
            tot = tot + buf[dev]
        lbv = lb_ref[...]
        lb = _sigmoid(lbv[0:1] - lbv[1:2])
        glb = tot[ROW_LB:ROW_LB + 1] * lb * (1.0 - lb)
        loss = jnp.sum(tot[ROW_LOSS:ROW_LOSS + 1], axis=-1, keepdims=True)
        row = lax.broadcasted_iota(jnp.int32, (SMALL_ROWS, width), 0)
        g = jnp.where(row == ROW_LB, glb, jnp.where(row == ROW_LB + 1, -glb, tot))
        g = jnp.where(row == ROW_LOSS, loss, g)
        g_ref[...] = g
        d_ref[...], mo_ref[...], vo_ref[...] = _adamw(w_ref[...], g, m_ref[...], v_ref[...])

    vm = pl.BlockSpec(memory_space=pltpu.VMEM)
    shp = jax.ShapeDtypeStruct((SMALL_ROWS, width), F32)
    return pl.pallas_call(
        body, name="small_allreduce_adamw", out_shape=(shp, shp, shp, shp),
        in_specs=[vm] * 5, out_specs=(vm, vm, vm, vm),
        scratch_shapes=[pltpu.VMEM((N_DEV, SMALL_ROWS, width), F32), pltpu.SemaphoreType.DMA((N_DEV - 1,)),
                        pltpu.SemaphoreType.DMA((N_DEV - 1,))],
    )(part, w, m, v, lb_logits)


def _pack_small(norm_gain, final_gain, lb2, gnorm, last_row, width):
    pad = lambda a: jnp.pad(a.reshape(1, -1), ((0, 0), (0, width - a.size)))
    return jnp.concatenate([norm_gain.reshape(2, width), final_gain.reshape(2, width), lb2.reshape(2, width),
                            pad(gnorm), last_row.reshape(1, width)], axis=0)


def _unpack_small(p, d, e, hd):
    return (p[0:2].reshape(1, d), p[2:4].reshape(d), p[4:6].reshape(2, e), p[6:7, :hd].reshape(1, hd))


def kernel(x, norm_gain, w_in, lb_logits, hgrn_gnorm, w_out, final_gain, loss_target, m_norm_gain, m_w_in, m_lb_logits, m_hgrn_gnorm, m_w_out, m_final_gain, v_norm_gain, v_w_in, v_lb_logits, v_hgrn_gnorm, v_w_out, v_final_gain):
    s, d = x.shape[1], x.shape[2]
    e = w_in.shape[2]
    assert d == 2 * e and lb_logits.shape == (2, e) and w_out.shape[1] * N_DEV == 2 * e
    x2d = x.reshape(s, d)
    tgt = loss_target.reshape(s, d)

    h = _rmsnorm_fwd(x2d, norm_gain)
    z, w_in_full, w_out_full = _inproj_gather(h, _cast_bf16(w_in[0]), _cast_bf16(w_out[0]))
    w_out_full = w_out_full.reshape(2 * e, d)
    y_h, states = _hgrn_fwd(z, lb_logits, hgrn_gnorm)
    o_attn, lse, y_a = _attn_fwd(z)
    dx2, dx2b, dy, loss_vec, dfg = _outproj_loss(x2d, y_h, y_a, w_out_full, final_gain.reshape(1, d), tgt)

    own_o, recv_o = _dwout_scatter(y_h, y_a, dx2b)
    dza = _attn_bwd(z, dy, o_attn, lse)
    dzh, dlb, dgn = _hgrn_bwd(z, dy, states, lb_logits, hgrn_gnorm)
    grad_x, dng = _dh_dx(dzh, dza, w_in_full, x2d, norm_gain, dx2)
    g_wo, d_wo, nm_wo, nv_wo = _sum_chips_adamw(own_o, recv_o, w_out[0], m_w_out[0], v_w_out[0])

    width = d // 2
    zero_row = jnp.zeros((1, width), F32)
    loss_row = loss_vec[:, :width] + loss_vec[:, width:]
    part = _pack_small(dng, dfg, jnp.concatenate([dlb, zero_row], axis=0), dgn, loss_row, width)
    pw = _pack_small(norm_gain, final_gain, lb_logits, hgrn_gnorm, zero_row, width)
    pm = _pack_small(m_norm_gain, m_final_gain, m_lb_logits, m_hgrn_gnorm, zero_row, width)
    pv = _pack_small(v_norm_gain, v_final_gain, v_lb_logits, v_hgrn_gnorm, zero_row, width)
    sg, sd, sm, sv = _small_allreduce_adamw(part, pw, pm, pv, lb_logits)
    own_i, recv_i = _dwin_scatter(h, dzh, dza)
    g_wi, d_wi, nm_wi, nv_wi = _sum_chips_adamw(own_i, recv_i, w_in[0], m_w_in[0], v_w_in[0])
    hd = hgrn_gnorm.shape[1]
    g_ng, g_fg, g_lb, g_gn = _unpack_small(sg, d, e, hd)
    d_ng, d_fg, d_lb, d_gn = _unpack_small(sd, d, e, hd)
    m_ng, m_fg, m_lb, m_gn = _unpack_small(sm, d, e, hd)
    v_ng, v_fg, v_lb, v_gn = _unpack_small(sv, d, e, hd)
    loss = sg[ROW_LOSS, 0]

    one = lambda a: a[None]
    return (loss, grad_x.reshape(1, s, d), g_ng, one(g_wi), g_lb, g_gn, one(g_wo), g_fg,
            d_ng, one(d_wi), d_lb, d_gn, one(d_wo), d_fg,
            m_ng, one(nm_wi), m_lb, m_gn, one(nm_wo), m_fg,
            v_ng, one(nv_wi), v_lb, v_gn, one(nv_wo), v_fg)
```

```python
import functools
import math

import jax
import jax.numpy as jnp
from jax import lax
from jax.experimental import pallas as pl
from jax.experimental.pallas import tpu as pltpu

NORM_EPS = 1e-6
HGRN_HEAD = 128
HGRN_CHUNK = 64
ATTN_HEAD = 64
ATTN_BAND = 128
DILATIONS = (1, 4, 16)
N_SPLITS = 8
N_DEV = 8
ADAM_LR = 0.001
ADAM_B1 = 0.9
ADAM_B2 = 0.999
ADAM_EPS = 1e-08
ADAM_WD = 0.01
ADAM_STEP = 10
LANES = 128
MESH = pl.DeviceIdType.MESH
F32 = jnp.float32
BF16 = jnp.bfloat16
NEG_BIG = -1e30
VMEM_LIMIT = 56 * 1024 * 1024

ANY = pl.BlockSpec(memory_space=pl.ANY)


def _params(*sem):
    return pltpu.CompilerParams(dimension_semantics=sem, vmem_limit_bytes=VMEM_LIMIT)


def _tile(n, pref):
    t = min(n, pref)
    assert n % t == 0, (n, pref)
    return t


def _dot(a, b, precision=None):
    return jnp.dot(a, b, preferred_element_type=F32, precision=precision)


def _dot_nt(a, b):
    return lax.dot_general(a, b, (((1,), (1,)), ((), ())), preferred_element_type=F32)


def _dot_tn(a, b):
    return lax.dot_general(a, b, (((0,), (0,)), ((), ())), preferred_element_type=F32)


def _sigmoid(x):
    return 0.5 * jnp.tanh(0.5 * x) + 0.5


def _dsilu(x, s):
    return s * (1.0 + x * (1.0 - s))


def _adamw(w, g, m, v):
    m = ADAM_B1 * m + (1.0 - ADAM_B1) * g
    v = ADAM_B2 * v + (1.0 - ADAM_B2) * (g * g)
    m_hat = m / (1.0 - ADAM_B1 ** ADAM_STEP)
    v_hat = v / (1.0 - ADAM_B2 ** ADAM_STEP)
    delta = -ADAM_LR * (m_hat / (jnp.sqrt(v_hat) + ADAM_EPS) + ADAM_WD * w)
    return delta, m, v


def _cast_bf16(a):
    r, c = a.shape
    tr = _tile(r, 256)

    def body(a_ref, o_ref):
        o_ref[...] = a_ref[...].astype(BF16)

    return pl.pallas_call(
        body, name="cast_bf16", grid=(r // tr,), out_shape=jax.ShapeDtypeStruct((r, c), BF16),
        in_specs=[pl.BlockSpec((tr, c), lambda i: (i, 0))], out_specs=pl.BlockSpec((tr, c), lambda i: (i, 0)),
        compiler_params=_params("parallel"))(a)


def _rmsnorm_fwd(x, gain):
    s, d = x.shape
    tm = _tile(s, 512)

    def body(x_ref, g_ref, h_ref):
        xv = x_ref[...]
        r = lax.rsqrt(jnp.mean(xv * xv, axis=-1, keepdims=True) + NORM_EPS)
        h_ref[...] = (xv * r * g_ref[...]).astype(BF16)

    return pl.pallas_call(
        body, name="rmsnorm_fwd", grid=(s // tm,), out_shape=jax.ShapeDtypeStruct((s, d), BF16),
        in_specs=[pl.BlockSpec((tm, d), lambda i: (i, 0)), pl.BlockSpec((1, d), lambda i: (0, 0))],
        out_specs=pl.BlockSpec((tm, d), lambda i: (i, 0)), compiler_params=_params("parallel"))(x, gain)


HGRN_BLOCK = 2048
TRI_ROWS = 256


def _chunk_masks():
    tb = TRI_ROWS
    row = lax.broadcasted_iota(jnp.int32, (tb, tb), 0)
    col = lax.broadcasted_iota(jnp.int32, (tb, tb), 1)
    same = (row // HGRN_CHUNK) == (col // HGRN_CHUNK)
    lower = jnp.where(same & (col <= row), 1.0, 0.0).astype(BF16)
    upper = jnp.where(same & (col >= row), 1.0, 0.0).astype(BF16)
    return lower, upper


def _split3(a):
    hi = a.astype(BF16).astype(F32)
    mid = (a - hi).astype(BF16).astype(F32)
    lo = (a - hi - mid).astype(BF16).astype(F32)
    return hi, mid, lo


def _tri_dot(tri, x):
    hi, mid, lo = (p.astype(BF16) for p in _split3(x))
    outs = []
    for r in range(0, x.shape[0], TRI_ROWS):
        sl = slice(r, r + TRI_ROWS)
        outs.append(_dot(tri, hi[sl]) + _dot(tri, mid[sl]) + _dot(tri, lo[sl]))
    return outs[0] if len(outs) == 1 else jnp.concatenate(outs, axis=0)


def _hgrn_gates(qp, fp, lbv):
    lb = _sigmoid(lbv[0:1] - lbv[1:2])
    sq = _sigmoid(qp)
    q = qp * sq
    sg = _sigmoid(fp)
    f = lb + (1.0 - lb) * sg
    k = 1.0 - f
    return lb, sq, q, sg, f, k


def _hgrn_fwd(z, lb_logits, gnorm):
    s = z.shape[0]
    e = z.shape[1] // N_SPLITS
    nh = e // HGRN_HEAD
    tb = _tile(s, HGRN_BLOCK)
    nc = tb // HGRN_CHUNK
    nb = s // tb
    C = HGRN_CHUNK

    def body(q_ref, f_ref, i_ref, g_ref, lb_ref, gn_ref, y_ref, st_ref, state, o_scr):
        @pl.when(pl.program_id(1) == 0)
        def _():
            state[...] = jnp.zeros_like(state)

        lb, sq, q, sg, f, k = _hgrn_gates(q_ref[...], f_ref[...], lb_ref[...])
        lower, _ = _chunk_masks()
        b = _tri_dot(lower, jnp.log(f))
        b3 = b.reshape(nc, C, HGRN_HEAD)
        bc = b3[:, C - 1:C, :]
        qt = (q * jnp.exp(b)).astype(BF16)
        kt = (k * jnp.exp(-b)).astype(BF16)
        ke = (k.reshape(nc, C, HGRN_HEAD) * jnp.exp(bc - b3)).reshape(tb, HGRN_HEAD).astype(BF16)
        v = i_ref[...].astype(BF16)
        tri = lax.broadcasted_iota(jnp.int32, (C, C), 1) <= lax.broadcasted_iota(jnp.int32, (C, C), 0)
        sls = [slice(c * C, (c + 1) * C) for c in range(nc)]
        kv = [_dot_tn(v[sl], ke[sl]) for sl in sls]
        a = [jnp.where(tri, _dot_nt(qt[sl], kt[sl]), 0.0).astype(BF16) for sl in sls]
        st = state[...]
        sts = []
        for c in range(nc):
            sts.append(st)
            st_ref[c] = st
            st = st * jnp.exp(bc[c]) + kv[c]
        state[...] = st
        for c, sl in enumerate(sls):
            o_scr[sl, :] = _dot(a[c], v[sl]) + _dot_nt(qt[sl], sts[c].astype(BF16))
        o = o_scr[...]
        rms = lax.rsqrt(jnp.mean(o * o, axis=-1, keepdims=True) + NORM_EPS)
        gp = g_ref[...]
        y_ref[...] = (o * rms * gn_ref[...] * (gp * _sigmoid(gp))).astype(BF16)

    col = lambda kk: (lambda h, n: (n, kk * nh + h))
    return pl.pallas_call(
        body, name="hgrn_fwd", grid=(nh, nb),
        out_shape=(jax.ShapeDtypeStruct((s, e), BF16),
                   jax.ShapeDtypeStruct((nh, s // C, HGRN_HEAD, HGRN_HEAD), F32)),
        in_specs=[pl.BlockSpec((tb, HGRN_HEAD), col(0)), pl.BlockSpec((tb, HGRN_HEAD), col(1)),
                  pl.BlockSpec((tb, HGRN_HEAD), col(2)), pl.BlockSpec((tb, HGRN_HEAD), col(3)),
                  pl.BlockSpec((2, HGRN_HEAD), lambda h, n: (0, h)), pl.BlockSpec((1, HGRN_HEAD), lambda h, n: (0, 0))],
        out_specs=(pl.BlockSpec((tb, HGRN_HEAD), lambda h, n: (n, h)),
                   pl.BlockSpec((None, nc, HGRN_HEAD, HGRN_HEAD), lambda h, n: (h, n, 0, 0))),
        scratch_shapes=[pltpu.VMEM((HGRN_HEAD, HGRN_HEAD), F32), pltpu.VMEM((tb, HGRN_HEAD), F32)],
        compiler_params=_params("parallel", "arbitrary"))(z, z, z, z, lb_logits, gnorm)


def _hgrn_bwd(z, dy, states, lb_logits, gnorm):
    s = z.shape[0]
    e = z.shape[1] // N_SPLITS
    nh = e // HGRN_HEAD
    tb = _tile(s, HGRN_BLOCK)
    nc = tb // HGRN_CHUNK
    nb = s // tb
    C = HGRN_CHUNK
    H = HGRN_HEAD

    def body(q_ref, f_ref, i_ref, g_ref, dy_ref, st_ref, lb_ref, gn_ref, dz_ref, dlb_ref, dgn_ref,
             gstate, o_scr, dq_scr, dk_scr, dv_scr, e_scr):
        first = (pl.program_id(0) == 0) & (pl.program_id(1) == 0)

        @pl.when(first)
        def _():
            dgn_ref[...] = jnp.zeros_like(dgn_ref)

        @pl.when(pl.program_id(1) == 0)
        def _():
            gstate[...] = jnp.zeros_like(gstate)
            dlb_ref[...] = jnp.zeros_like(dlb_ref)

        qp = q_ref[...]
        lb, sq, q, sg, f, k = _hgrn_gates(qp, f_ref[...], lb_ref[...])
        lower, upper = _chunk_masks()
        b = _tri_dot(lower, jnp.log(f))
        b3 = b.reshape(nc, C, H)
        bc = b3[:, C - 1:C, :]
        eb = jnp.exp(b)
        enb = jnp.exp(-b)
        eend = jnp.exp(bc - b3).reshape(tb, H)
        qt = (q * eb).astype(BF16)
        kt = (k * enb).astype(BF16)
        ke = (k * eend).astype(BF16)
        v = i_ref[...].astype(BF16)
        tri = lax.broadcasted_iota(jnp.int32, (C, C), 1) <= lax.broadcasted_iota(jnp.int32, (C, C), 0)
        sls = [slice(c * C, (c + 1) * C) for c in range(nc)]
        a = [jnp.where(tri, _dot_nt(qt[sl], kt[sl]), 0.0).astype(BF16) for sl in sls]
        for c, sl in enumerate(sls):
            o_scr[sl, :] = _dot(a[c], v[sl]) + _dot_nt(qt[sl], st_ref[c].astype(BF16))
        o = o_scr[...]
        rms = lax.rsqrt(jnp.mean(o * o, axis=-1, keepdims=True) + NORM_EPS)
        on = o * rms
        gn = gn_ref[...]
        gp = g_ref[...]
        sgg = _sigmoid(gp)
        dyv = dy_ref[...]
        d_on = dyv * (gp * sgg)
        dz_ref[3] = (dyv * on * gn * _dsilu(gp, sgg)).astype(BF16)
        dgn_ref[...] += jnp.sum(d_on * on, axis=0, keepdims=True)
        u = d_on * gn
        do = (rms * (u - on * jnp.mean(u * on, axis=-1, keepdims=True))).astype(BF16)
        gup = [_dot_tn(do[sl], qt[sl]) for sl in sls]
        da = [jnp.where(tri, _dot_nt(do[sl], v[sl]), 0.0).astype(BF16) for sl in sls]
        gt = gstate[...]
        gts = [None] * nc
        for c in reversed(range(nc)):
            gts[c] = gt
            gt = gt * jnp.exp(bc[c]) + gup[c]
        gstate[...] = gt
        for c, sl in enumerate(sls):
            stp = st_ref[c]
            gtb = gts[c].astype(BF16)
            dqt = _dot(da[c], kt[sl]) + _dot(do[sl], stp.astype(BF16))
            dkt = _dot_tn(da[c], qt[sl])
            dks = _dot(v[sl], gtb) * eend[sl]
            dv_scr[sl, :] = _dot_tn(a[c], do[sl]) + _dot_nt(ke[sl], gtb)
            dq_scr[sl, :] = dqt * eb[sl]
            dk_scr[sl, :] = dkt * enb[sl] + dks
            ech = (jnp.sum(k[sl] * dks, axis=0, keepdims=True)
                   + jnp.sum(gts[c] * jnp.exp(bc[c]) * stp, axis=0, keepdims=True))
            e_scr[sl, :] = jnp.broadcast_to(ech, (C, H))
        dq = dq_scr[...]
        dk = dk_scr[...]
        dlf = _tri_dot(upper, q * dq - k * dk) + e_scr[...]
        dft = dlf / f - dk
        dz_ref[0] = (dq * _dsilu(qp, sq)).astype(BF16)
        dz_ref[1] = (dft * (1.0 - lb) * sg * (1.0 - sg)).astype(BF16)
        dz_ref[2] = dv_scr[...].astype(BF16)
        dlb_ref[...] += jnp.sum(dft * (1.0 - sg), axis=0, keepdims=True)

    col = lambda kk: (lambda h, n: (nb - 1 - n, kk * nh + h))
    return pl.pallas_call(
        body, name="hgrn_bwd", grid=(nh, nb),
        out_shape=(jax.ShapeDtypeStruct((4, s, e), BF16), jax.ShapeDtypeStruct((1, e), F32),
                   jax.ShapeDtypeStruct((1, H), F32)),
        in_specs=[pl.BlockSpec((tb, H), col(0)), pl.BlockSpec((tb, H), col(1)),
                  pl.BlockSpec((tb, H), col(2)), pl.BlockSpec((tb, H), col(3)),
                  pl.BlockSpec((tb, H), lambda h, n: (nb - 1 - n, h)),
                  pl.BlockSpec((None, nc, H, H), lambda h, n: (h, nb - 1 - n, 0, 0)),
                  pl.BlockSpec((2, H), lambda h, n: (0, h)), pl.BlockSpec((1, H), lambda h, n: (0, 0))],
        out_specs=(pl.BlockSpec((4, tb, H), lambda h, n: (0, nb - 1 - n, h)),
                   pl.BlockSpec((1, H), lambda h, n: (0, h)), pl.BlockSpec((1, H), lambda h, n: (0, 0))),
        scratch_shapes=[pltpu.VMEM((H, H), F32)] + [pltpu.VMEM((tb, H), F32)] * 5,
        compiler_params=_params("arbitrary", "arbitrary"))(z, z, z, z, dy, states, lb_logits, gnorm)


ATTN_T = 16 * ATTN_BAND
SCALE = ATTN_HEAD ** -0.5
TILE_UNROLL = 2


def _slope(hh, nheads):
    head = (2 * pl.program_id(0) + hh + 1).astype(F32)
    return jnp.exp(jnp.full((1, 1), -8.0 / nheads * math.log(2.0), F32) * head)


def _fill_bias(bias, nheads, delta, edge_ok):
    band = (delta >= 0) & (delta <= ATTN_BAND)
    dist = delta.astype(F32)
    for pi, dil in enumerate(DILATIONS):
        for hh in range(2):
            full = jnp.where(band, -(_slope(hh, nheads) * float(dil)) * dist, NEG_BIG)
            bias[(pi * 2 + hh) * 2] = full
            bias[(pi * 2 + hh) * 2 + 1] = jnp.where(edge_ok, full, NEG_BIG)


def _rows(start, size, stride):
    if stride == 1:
        return pl.ds(pl.multiple_of(start, ATTN_BAND), size)
    return pl.ds(start, size, stride=stride)


def _head_lanes(rows, hh):
    return (lax.broadcasted_iota(jnp.int32, (rows, LANES), 1) // ATTN_HEAD) == hh


def _attn_fwd(z):
    s = z.shape[0]
    e = z.shape[1] // N_SPLITS
    npair = e // LANES
    T = ATTN_T
    assert s % T == 0
    nsb = s // T
    W = ATTN_BAND
    nt = T // W
    HD = ATTN_HEAD
    chunk = 256

    def body(q_ref, kp_ref, kc_ref, vp_ref, vc_ref, g_ref, o_ref, l_ref, y_ref, qa, kbuf, va, bias, accs, ms, lsw):
        sb = pl.program_id(1)
        def stage(i, carry):
            rows = pl.ds(pl.multiple_of(i * chunk, chunk), chunk)
            upper = pl.ds(pl.multiple_of(T + i * chunk, chunk), chunk)
            kbuf[upper, :] = kc_ref[rows, :]
            for hh in range(2):
                mine = _head_lanes(chunk, hh)
                qa[hh, rows, :] = jnp.where(mine, q_ref[rows, :] * SCALE, 0.0)
                va[hh, upper, :] = jnp.where(mine, vc_ref[rows, :], 1.0)
            return carry

        lax.fori_loop(0, T // chunk, stage, 0)

        @pl.when(sb == 0)
        def _():
            def stage_prev(i, carry):
                rows = pl.ds(pl.multiple_of(i * chunk, chunk), chunk)
                kbuf[rows, :] = kp_ref[rows, :]
                for hh in range(2):
                    va[hh, rows, :] = jnp.where(_head_lanes(chunk, hh), vp_ref[rows, :], 1.0)
                return carry

            lax.fori_loop(0, T // chunk, stage_prev, 0)
        qi = lax.broadcasted_iota(jnp.int32, (W, 2 * W), 0)
        kj = lax.broadcasted_iota(jnp.int32, (W, 2 * W), 1)
        _fill_bias(bias, 2 * npair, W + qi - kj, kj >= W)

        def tile(tau, carry):
            first = _head_lanes(W, 0)
            rows, scores = [], []
            for pi, dil in enumerate(DILATIONS):
                r = tau % dil
                ub = tau // dil
                qrows = _rows(r + dil * W * ub, W, dil)
                krows = _rows(T + dil * W * (ub - 1) + r, 2 * W, dil)
                var = jnp.where((sb == 0) & (ub == 0), 1, 0)
                kt = kbuf[krows, :].astype(BF16)
                rows.append((qrows, krows))
                scores.append([_dot_nt(qa[hh, qrows, :].astype(BF16), kt) + bias[(pi * 2 + hh) * 2 + var]
                               for hh in range(2)])
            maxes = [[jnp.max(sc, axis=-1, keepdims=True) for sc in pair] for pair in scores]
            probs = [[jnp.exp(sc - m).astype(BF16) for sc, m in zip(ps, pm)] for ps, pm in zip(scores, maxes)]
            for pi, (qrows, krows) in enumerate(rows):
                outs = [_dot(probs[pi][hh], va[hh, krows, :].astype(BF16)) for hh in range(2)]
                accs[pi, qrows, :] = jnp.where(first, outs[0], outs[1])
                lsw[pi, qrows, :] = jnp.where(first, outs[1], outs[0])
                ms[pi, qrows, :] = jnp.where(first, maxes[pi][0], maxes[pi][1])
            return carry

        lax.fori_loop(0, nt, tile, 0, unroll=TILE_UNROLL)

        def merge(i, carry):
            rows = pl.ds(pl.multiple_of(i * chunk, chunk), chunk)
            m1, m2, m3 = ms[0, rows, :], ms[1, rows, :], ms[2, rows, :]
            mx = jnp.maximum(jnp.maximum(m1, m2), m3)
            w1, w2, w3 = jnp.exp(m1 - mx), jnp.exp(m2 - mx), jnp.exp(m3 - mx)
            unswap = lambda a: pltpu.roll(a, ATTN_HEAD, 1)
            den = w1 * unswap(lsw[0, rows, :]) + w2 * unswap(lsw[1, rows, :]) + w3 * unswap(lsw[2, rows, :])
            o = (w1 * accs[0, rows, :] + w2 * accs[1, rows, :] + w3 * accs[2, rows, :]) / den
            o_ref[rows, :] = o
            l_ref[rows, :] = mx + jnp.log(den)
            gp = g_ref[rows, :]
            y_ref[rows, :] = (o * (gp * _sigmoid(gp))).astype(BF16)
            upper = pl.ds(pl.multiple_of(T + i * chunk, chunk), chunk)
            kbuf[rows, :] = kbuf[upper, :]
            for hh in range(2):
                va[hh, rows, :] = va[hh, upper, :]
            return carry

        lax.fori_loop(0, T // chunk, merge, 0)

    cur = lambda split: (lambda hp, sb: (sb, split * npair + hp))
    prev = lambda split: (lambda hp, sb: (0, split * npair + hp))
    blk = lambda index: pl.BlockSpec((T, LANES), index)
    out = blk(lambda hp, sb: (sb, hp))
    buf = lambda rows: pltpu.VMEM((rows, LANES), F32)
    return pl.pallas_call(
        body, name="attn_fwd", grid=(npair, nsb),
        out_shape=(jax.ShapeDtypeStruct((s, e), F32), jax.ShapeDtypeStruct((s, e), F32), jax.ShapeDtypeStruct((s, e), BF16)),
        in_specs=[blk(cur(4)), blk(prev(5)), blk(cur(5)), blk(prev(6)), blk(cur(6)), blk(cur(7))],
        out_specs=(out, out, out),
        scratch_shapes=[pltpu.VMEM((2, T, LANES), F32), buf(2 * T), pltpu.VMEM((2, 2 * T, LANES), F32),
                        pltpu.VMEM((12, W, 2 * W), F32)] + [pltpu.VMEM((3, T, LANES), F32)] * 3,
        compiler_params=_params("parallel", "arbitrary"))(z, z, z, z, z, z)


def _outproj_loss(x, y_h, y_a, w_out_full, final_gain, target):
    s, d = x.shape
    e = y_h.shape[1]
    tm = _tile(s, 256)

    def body(x_ref, yh_ref, ya_ref, w_ref, g_ref, t_ref, dx_ref, dxb_ref, dy_ref, loss_ref, dg_ref):
        @pl.when(pl.program_id(0) == 0)
        def _():
            loss_ref[...] = jnp.zeros_like(loss_ref)
            dg_ref[...] = jnp.zeros_like(dg_ref)

        w = w_ref[...]
        x2 = x_ref[...] + _dot(yh_ref[...], w[0:e]) + _dot(ya_ref[...], w[e:2 * e])
        r = lax.rsqrt(jnp.mean(x2 * x2, axis=-1, keepdims=True) + NORM_EPS)
        xn = x2 * r
        g = g_ref[...]
        err = xn * g - t_ref[...]
        loss_ref[...] += jnp.sum(err * err, axis=0, keepdims=True) * (0.5 / d)
        dyo = err * (1.0 / d)
        dg_ref[...] += jnp.sum(dyo * xn, axis=0, keepdims=True)
        u = dyo * g
        dx2 = r * (u - xn * jnp.mean(u * xn, axis=-1, keepdims=True))
        dx_ref[...] = dx2
        dxb = dx2.astype(BF16)
        dxb_ref[...] = dxb
        dy_ref[...] = _dot_nt(dxb, w)

    row = pl.BlockSpec((tm, d), lambda i: (i, 0))
    half = pl.BlockSpec((tm, e), lambda i: (i, 0))
    vec = pl.BlockSpec((1, d), lambda i: (0, 0))
    return pl.pallas_call(
        body, name="outproj_loss", grid=(s // tm,),
        out_shape=(jax.ShapeDtypeStruct((s, d), F32), jax.ShapeDtypeStruct((s, d), BF16),
                   jax.ShapeDtypeStruct((s, 2 * e), F32), jax.ShapeDtypeStruct((1, d), F32),
                   jax.ShapeDtypeStruct((1, d), F32)),
        in_specs=[row, half, half, pl.BlockSpec((2 * e, d), lambda i: (0, 0)), vec, row],
        out_specs=(row, row, pl.BlockSpec((tm, 2 * e), lambda i: (i, 0)), vec, vec),
        compiler_params=_params("arbitrary"))(x, y_h, y_a, w_out_full, final_gain, target)


def _attn_bwd(z, dy, o, lse):
    s, e = o.shape
    npair = e // LANES
    T = ATTN_T
    assert s % T == 0
    nsb = s // T
    W = ATTN_BAND
    nt = T // W
    HD = ATTN_HEAD
    chunk = 256

    def body(k_ref, v_ref, qc_ref, qn_ref, dyc_ref, dyn_ref, gc_ref, gn_ref, oc_ref, on_ref, lc_ref, ln_ref,
             dz_ref, qa, doa, ka, va, dqacc, dkacc, dvacc, bias):
        sb = pl.program_id(1)
        def stage_queries(half, q_r, dy_r, g_r, o_r, l_r):
            def stage(i, carry):
                rows = pl.ds(pl.multiple_of(i * chunk, chunk), chunk)
                dst = pl.ds(pl.multiple_of(half * T + i * chunk, chunk), chunk)
                lane = lax.broadcasted_iota(jnp.int32, (chunk, LANES), 1)
                gp = g_r[rows, :]
                dov = dy_r[rows, :] * (gp * _sigmoid(gp))
                qv = q_r[rows, :] * SCALE
                same_head = (lax.broadcasted_iota(jnp.int32, (LANES, LANES), 0) // HD
                             == lax.broadcasted_iota(jnp.int32, (LANES, LANES), 1) // HD)
                ones = jnp.where(same_head, 1.0, 0.0).astype(BF16)
                hi, mid, lo = (p.astype(BF16) for p in _split3(dov * o_r[rows, :]))
                delta = _dot(hi, ones) + _dot(mid, ones) + _dot(lo, ones)
                swap = lambda a: pltpu.roll(a, HD, 1)
                lse_parts = [swap(p) for p in _split3(l_r[rows, :])]
                dl_parts = [swap(p) for p in _split3(delta)]
                for hh in range(2):
                    mine = _head_lanes(chunk, hh)
                    spare = (1 - hh) * HD
                    qh = jnp.where(mine, qv, 0.0)
                    dh = jnp.where(mine, dov, 0.0)
                    for j in range(3):
                        qh = jnp.where(lane == spare + j, lse_parts[j], qh)
                        dh = jnp.where(lane == spare + j, dl_parts[j], dh)
                    qa[hh, dst, :] = qh
                    doa[hh, dst, :] = dh
                return carry

            lax.fori_loop(0, T // chunk, stage, 0)

        @pl.when(sb == 0)
        def _():
            stage_queries(0, qc_ref, dyc_ref, gc_ref, oc_ref, lc_ref)

        stage_queries(1, qn_ref, dyn_ref, gn_ref, on_ref, ln_ref)

        def stage_keys(i, carry):
            rows = pl.ds(pl.multiple_of(i * chunk, chunk), chunk)
            lane = lax.broadcasted_iota(jnp.int32, (chunk, LANES), 1)
            for hh in range(2):
                spare = (1 - hh) * HD
                minus = (lane >= spare) & (lane < spare + 3)
                ka[hh, rows, :] = jnp.where(minus, -1.0, k_ref[rows, :])
                va[hh, rows, :] = jnp.where(minus, -1.0, v_ref[rows, :])
            gp = gc_ref[rows, :]
            dz_ref[3, rows, :] = (dyc_ref[rows, :] * oc_ref[rows, :] * _dsilu(gp, _sigmoid(gp))).astype(BF16)
            return carry

        lax.fori_loop(0, T // chunk, stage_keys, 0)

        @pl.when(sb == 0)
        def _():
            dqacc[0:T, :] = jnp.zeros((T, LANES), F32)

        dqacc[T:, :] = jnp.zeros((T, LANES), F32)
        dkacc[...] = jnp.zeros_like(dkacc)
        dvacc[...] = jnp.zeros_like(dvacc)
        qi = lax.broadcasted_iota(jnp.int32, (2 * W, W), 0)
        kj = lax.broadcasted_iota(jnp.int32, (2 * W, W), 1)
        _fill_bias(bias, 2 * npair, qi - kj, qi < W)

        def tile(tau, carry):
            def scores(step, pi):
                dil = DILATIONS[pi]
                r = step % dil
                ub = step // dil
                start = r + dil * W * ub
                krows = _rows(start, W, dil)
                qrows = _rows(start, 2 * W, dil)
                var = jnp.where((sb == nsb - 1) & (ub == nt // dil - 1), 1, 0)
                unit = dict(krows=krows, qrows=qrows, ops=[], sc=[], dpd=[])
                for hh in range(2):
                    kt = ka[hh, krows, :].astype(BF16)
                    vt = va[hh, krows, :].astype(BF16)
                    qt = qa[hh, qrows, :].astype(BF16)
                    dt = doa[hh, qrows, :].astype(BF16)
                    unit["ops"].append((kt, qt, dt))
                    unit["sc"].append(_dot_nt(qt, kt) + bias[(pi * 2 + hh) * 2 + var])
                    unit["dpd"].append(_dot_nt(dt, vt))
                return unit

            def elementwise(unit):
                ps = [jnp.exp(s_) for s_ in unit["sc"]]
                unit["ds"] = [(p * d).astype(BF16) for p, d in zip(ps, unit["dpd"])]
                unit["pb"] = [p.astype(BF16) for p in ps]

            def products(unit):
                dvs = [_dot_tn(pb, dt) for pb, (kt, qt, dt) in zip(unit["pb"], unit["ops"])]
                dks = [_dot_tn(ds, qt) for ds, (kt, qt, dt) in zip(unit["ds"], unit["ops"])]
                dqs = [_dot(ds, kt) for ds, (kt, qt, dt) in zip(unit["ds"], unit["ops"])]
                dkacc[unit["krows"], :] += jnp.where(_head_lanes(W, 0), dks[0], dks[1])
                dvacc[unit["krows"], :] += jnp.where(_head_lanes(W, 0), dvs[0], dvs[1])
                dqacc[unit["qrows"], :] += jnp.where(_head_lanes(2 * W, 0), dqs[0], dqs[1]) * SCALE

            order = [(2 * tau + half, pi) for half in range(2) for pi in range(len(DILATIONS))]
            units = [None] * len(order)
            for n in range(len(order) + 2):
                if n < len(order):
                    units[n] = scores(*order[n])
                if 1 <= n <= len(order):
                    elementwise(units[n - 1])
                if n >= 2:
                    products(units[n - 2])
            return carry

        lax.fori_loop(0, nt // 2, tile, 0)

        def flush(i, carry):
            rows = pl.ds(pl.multiple_of(i * chunk, chunk), chunk)
            nxt = pl.ds(pl.multiple_of(T + i * chunk, chunk), chunk)
            dz_ref[0, rows, :] = dqacc[rows, :].astype(BF16)
            dz_ref[1, rows, :] = dkacc[rows, :].astype(BF16)
            dz_ref[2, rows, :] = dvacc[rows, :].astype(BF16)
            dqacc[rows, :] = dqacc[nxt, :]
            for hh in range(2):
                qa[hh, rows, :] = qa[hh, nxt, :]
                doa[hh, rows, :] = doa[hh, nxt, :]
            return carry

        lax.fori_loop(0, T // chunk, flush, 0)

    zc = lambda split: (lambda hp, sb: (sb, split * npair + hp))
    zn = lambda split: (lambda hp, sb: (jnp.minimum(sb + 1, nsb - 1), split * npair + hp))
    ec = lambda off: (lambda hp, sb: (sb, off + hp))
    en = lambda off: (lambda hp, sb: (jnp.minimum(sb + 1, nsb - 1), off + hp))
    z0 = lambda split: (lambda hp, sb: (0, split * npair + hp))
    e0 = lambda off: (lambda hp, sb: (0, off + hp))
    blk = lambda index: pl.BlockSpec((T, LANES), index)
    buf = lambda rows: pltpu.VMEM((rows, LANES), F32)
    return pl.pallas_call(
        body, name="attn_bwd", grid=(npair, nsb), out_shape=jax.ShapeDtypeStruct((4, s, e), BF16),
        in_specs=[blk(zc(5)), blk(zc(6)), blk(z0(4)), blk(zn(4)), blk(ec(npair)), blk(en(npair)),
                  blk(zc(7)), blk(zn(7)), blk(ec(0)), blk(en(0)), blk(e0(0)), blk(en(0))],
        out_specs=pl.BlockSpec((4, T, LANES), lambda hp, sb: (0, sb, hp)),
        scratch_shapes=[pltpu.VMEM((2, 2 * T, LANES), F32), pltpu.VMEM((2, 2 * T, LANES), F32),
                        pltpu.VMEM((2, T, LANES), F32), pltpu.VMEM((2, T, LANES), F32),
                        buf(2 * T), buf(T), buf(T), pltpu.VMEM((12, 2 * W, W), F32)],
        compiler_params=_params("parallel", "arbitrary"))(z, z, z, z, dy, dy, z, z, o, o, lse, lse)


def _dz_specs(tm, e):
    def mk(lo, hi):
        return pl.BlockSpec((None, tm, e), lambda i, k: (jnp.clip(k - lo, 0, hi - lo - 1), i, 0))
    return [mk(0, 4), mk(4, 8)]


def _dz_pick(grp, dzh_ref, dza_ref, fn):
    @pl.when(grp < 4)
    def _():
        fn(dzh_ref[...])

    @pl.when(grp >= 4)
    def _():
        fn(dza_ref[...])


def _dh_dx(dzh, dza, w_full, x, gain, dx2):
    s, d = x.shape
    e = dzh.shape[2]
    tm = _tile(s, 1024)
    ni = s // tm
    chunk = _tile(tm, 256)
    fetch_at = 2

    def body(dzh_ref, dza_ref, w_ref, x_hbm, g_ref, dx2_hbm, gx_hbm, dg_ref, acc, xbuf, dbuf, sems):
        i, k = pl.program_id(0), pl.program_id(1)
        tile_rows = pl.ds(pl.multiple_of(i * tm, tm), tm)
        fetch_x = pltpu.make_async_copy(x_hbm.at[tile_rows, :], xbuf, sems.at[0])
        fetch_d = pltpu.make_async_copy(dx2_hbm.at[tile_rows, :], dbuf, sems.at[1])
        store = pltpu.make_async_copy(xbuf, gx_hbm.at[tile_rows, :], sems.at[2])

        @pl.when((i == 0) & (k == 0))
        def _():
            dg_ref[...] = jnp.zeros_like(dg_ref)

        @pl.when(k == 0)
        def _():
            acc[...] = jnp.zeros_like(acc)

        @pl.when((k == fetch_at) & (i > 0))
        def _():
            store.wait()

        @pl.when(k == fetch_at)
        def _():
            fetch_x.start()
            fetch_d.start()

        def add(dz):
            acc[...] += _dot_nt(dz, w_ref[...])

        _dz_pick(k, dzh_ref, dza_ref, add)

        @pl.when(k == N_SPLITS - 1)
        def _():
            fetch_x.wait()
            fetch_d.wait()
            gain_row = g_ref[...]

            def finish(c, dg):
                rows = pl.ds(pl.multiple_of(c * chunk, chunk), chunk)
                dh = acc[rows, :]
                xv = xbuf[rows, :]
                r = lax.rsqrt(jnp.mean(xv * xv, axis=-1, keepdims=True) + NORM_EPS)
                xn = xv * r
                u = dh * gain_row
                xbuf[rows, :] = dbuf[rows, :] + r * (u - xn * jnp.mean(u * xn, axis=-1, keepdims=True))
                return dg + jnp.sum(dh * xn, axis=0, keepdims=True)

            dg_ref[...] += lax.fori_loop(0, tm // chunk, finish, jnp.zeros((1, d), F32))
            store.start()

        @pl.when((k == N_SPLITS - 1) & (i == ni - 1))
        def _():
            store.wait()

    vec = pl.BlockSpec((1, d), lambda i, k: (0, 0))
    return pl.pallas_call(
        body, name="dh_dx", grid=(ni, N_SPLITS),
        out_shape=(jax.ShapeDtypeStruct((s, d), F32), jax.ShapeDtypeStruct((1, d), F32)),
        in_specs=_dz_specs(tm, e) + [pl.BlockSpec((None, d, e), lambda i, k: (k, 0, 0)), ANY, vec, ANY],
        out_specs=(ANY, vec),
        scratch_shapes=[pltpu.VMEM((tm, d), F32), pltpu.VMEM((tm, d), F32), pltpu.VMEM((tm, d), F32),
                        pltpu.SemaphoreType.DMA((3,))],
        compiler_params=_params("arbitrary", "arbitrary"))(dzh, dza, w_full, x, gain, dx2)


def _position():
    x, y, c = lax.axis_index("x"), lax.axis_index("y"), lax.axis_index("c")
    return x, y, c


def _xor_peer(x, y, c, mask):
    return (x ^ ((mask >> 2) & 1), y ^ ((mask >> 1) & 1), c ^ (mask & 1))


def _block_order(masks):
    me = 4 * lax.axis_index("x") + 2 * lax.axis_index("y") + lax.axis_index("c")
    return jnp.stack([me ^ m for m in masks]).astype(jnp.int32)


GATHER_MASKS = (0, 1, 4, 2, 6, 5, 3, 7)


def _inproj_gather(h, w_loc, wo_loc):
    s, d = h.shape
    e = w_loc.shape[1]
    tm = _tile(s, 1024)
    ni = s // tm
    pre = max(ni - 2, 0)

    def body(order_ref, h_ref, w_ref, wo_ref, z_ref, wf_ref, wof_ref, wbuf, send_sems, recv_sems, osend, orecv,
             local_sems, wsems):
        j, i = pl.program_id(0), pl.program_id(1)
        x, y, c = _position()
        me, sibling = (x, y, c), (x, y, 1 - c)
        chips = [(1 - x, y), (x, 1 - y), (1 - x, 1 - y)]
        blk = lambda p: 4 * p[0] + 2 * p[1] + p[2]

        def copy(k, block, to, src=None):
            dst = wf_ref.at[blk(block)]
            return pltpu.make_async_remote_copy(
                src_ref=dst if src is None else src, dst_ref=dst, send_sem=send_sems.at[k], recv_sem=recv_sems.at[k],
                device_id=to, device_id_type=MESH)

        first = [copy(0, me, sibling, src=w_ref)] + [copy(1 + q, me, (*chip, c), src=w_ref) for q, chip in enumerate(chips)]
        passed = [copy(4 + q, (*chip, c), sibling) for q, chip in enumerate(chips)]
        mine = pltpu.make_async_copy(w_ref, wf_ref.at[blk(me)], local_sems.at[0])
        ocopies = [pltpu.make_async_remote_copy(
            src_ref=wo_ref, dst_ref=wof_ref.at[blk(me)], send_sem=osend.at[k], recv_sem=orecv.at[k],
            device_id=_xor_peer(x, y, c, k + 1), device_id_type=MESH) for k in range(N_DEV - 1)]
        omine = pltpu.make_async_copy(wo_ref, wof_ref.at[blk(me)], local_sems.at[1])
        blocks = [me, sibling] + [(*chip, c) for chip in chips] + [(*chip, 1 - c) for chip in chips]
        arrive = [None, copy(0, sibling, me)] + [copy(1 + q, (*chip, c), me) for q, chip in enumerate(chips)] \
            + [copy(4 + q, (*chip, 1 - c), me) for q, chip in enumerate(chips)]
        forward = [None, None] + passed + [None, None, None]

        def load(slot, src):
            return pltpu.make_async_copy(src, wbuf.at[slot], wsems.at[slot])

        @pl.when((j == 0) & (i == 0))
        def _():
            for cp in [mine, omine] + first + ocopies:
                cp.start()
            load(0, w_ref).start()

        for jj in range(N_DEV):
            @pl.when((j == jj) & (i == 0))
            def _():
                load(jj % 2, w_ref).wait()

            if jj + 1 < N_DEV:
                @pl.when((j == jj) & (i == pre))
                def _():
                    arrive[jj + 1].wait_recv()
                    if forward[jj + 1] is not None:
                        forward[jj + 1].start()
                    load((jj + 1) % 2, wf_ref.at[blk(blocks[jj + 1])]).start()

        z_ref[...] = _dot(h_ref[...], wbuf[j % 2])

        @pl.when((j == N_DEV - 1) & (i == ni - 1))
        def _():
            for cp in first + passed:
                cp.wait_send()
            for cp in ocopies:
                cp.wait_send()
                cp.wait_recv()
            mine.wait()
            omine.wait()

    grid_spec = pltpu.PrefetchScalarGridSpec(
        num_scalar_prefetch=1, grid=(N_DEV, ni),
        in_specs=[pl.BlockSpec((tm, d), lambda j, i, o: (i, 0)), ANY, ANY],
        out_specs=(pl.BlockSpec((tm, e), lambda j, i, o: (i, o[j])), ANY, ANY),
        scratch_shapes=[pltpu.VMEM((2, d, e), BF16), pltpu.SemaphoreType.DMA((7,)), pltpu.SemaphoreType.DMA((7,)),
                        pltpu.SemaphoreType.DMA((7,)), pltpu.SemaphoreType.DMA((7,)), pltpu.SemaphoreType.DMA((2,)),
                        pltpu.SemaphoreType.DMA((2,))])
    return pl.pallas_call(
        body, name="inproj_gather", grid_spec=grid_spec,
        out_shape=(jax.ShapeDtypeStruct((s, N_SPLITS * e), F32), jax.ShapeDtypeStruct((N_DEV, d, e), BF16),
                   jax.ShapeDtypeStruct((N_DEV,) + wo_loc.shape, BF16)),
        compiler_params=_params("arbitrary", "arbitrary"))(_block_order(GATHER_MASKS), h, w_loc, wo_loc)


SCATTER_MASKS = (7, 6, 5, 4, 3, 2, 1, 0)
N_CHIPS = 4


def _scatter_block(k, acc, stage, tmp, own_ref, ra_ref, rb_ref, sa_send, sa_recv, sb_send, sb_recv, loc_sem, last):
    x, y, c = _position()
    chip_of = lambda t: _xor_peer(x, y, c, SCATTER_MASKS[2 * t + 1])

    def ship(t):
        return pltpu.make_async_remote_copy(
            src_ref=stage.at[0], dst_ref=ra_ref.at[t], send_sem=sa_send.at[t], recv_sem=sa_recv.at[t],
            device_id=(x, y, 1 - c), device_id_type=MESH)

    def send(t):
        return pltpu.make_async_remote_copy(
            src_ref=stage.at[1], dst_ref=rb_ref.at[t], send_sem=sb_send.at[t], recv_sem=sb_recv.at[t],
            device_id=chip_of(t), device_id_type=MESH)

    for kk in range(N_DEV):
        t = kk // 2

        @pl.when(last & (k == kk))
        def _():
            if kk % 2 == 0:
                if t >= 1:
                    ship(t - 1).wait_send()
                stage[0] = acc[...].astype(BF16)
                ship(t).start()
            else:
                ship(t).wait_recv()
                fetch = pltpu.make_async_copy(ra_ref.at[t], tmp, loc_sem)
                fetch.start()
                fetch.wait()
                acc[...] += tmp[...].astype(F32)
                if t < N_CHIPS - 1:
                    if t >= 1:
                        send(t - 1).wait_send()
                    stage[1] = acc[...].astype(BF16)
                    send(t).start()
                else:
                    keep = pltpu.make_async_copy(acc, own_ref, loc_sem)
                    keep.start()
                    keep.wait()
                    ship(t).wait_send()
                    send(t - 1).wait_send()
                    for q in range(N_CHIPS - 1):
                        send(q).wait_recv()


def _scatter_scratch(rows, cols):
    return [pltpu.VMEM((rows, cols), F32), pltpu.VMEM((2, rows, cols), BF16), pltpu.VMEM((rows, cols), BF16),
            pltpu.SemaphoreType.DMA((N_CHIPS,)), pltpu.SemaphoreType.DMA((N_CHIPS,)),
            pltpu.SemaphoreType.DMA((N_CHIPS - 1,)), pltpu.SemaphoreType.DMA((N_CHIPS - 1,)), pltpu.SemaphoreType.DMA(())]


def _scatter_out(rows, cols):
    return (jax.ShapeDtypeStruct((rows, cols), F32), jax.ShapeDtypeStruct((N_CHIPS, rows, cols), BF16),
            jax.ShapeDtypeStruct((N_CHIPS - 1, rows, cols), BF16))


def _dwin_scatter(h, dzh, dza):
    s, d = h.shape
    e = dzh.shape[2]
    ts = _tile(s, 1024)
    ns = s // ts

    def body(order_ref, dzh_ref, dza_ref, h_ref, own_ref, ra_ref, rb_ref, acc, stage, tmp, *sems):
        k, step = pl.program_id(0), pl.program_id(1)

        @pl.when(step == 0)
        def _():
            acc[...] = jnp.zeros_like(acc)

        def add(dz):
            acc[...] += _dot_tn(h_ref[...], dz)

        _dz_pick(order_ref[k], dzh_ref, dza_ref, add)
        _scatter_block(k, acc, stage, tmp, own_ref, ra_ref, rb_ref, *sems, step == ns - 1)

    def dz_spec(lo):
        return pl.BlockSpec((None, ts, e), lambda k, st, o: (jnp.clip(o[k] - lo, 0, 3), st, 0))

    grid_spec = pltpu.PrefetchScalarGridSpec(
        num_scalar_prefetch=1, grid=(N_DEV, ns),
        in_specs=[dz_spec(0), dz_spec(4), pl.BlockSpec((ts, d), lambda k, st, o: (st, 0))],
        out_specs=(ANY, ANY, ANY), scratch_shapes=_scatter_scratch(d, e))
    own, _, rb = pl.pallas_call(
        body, name="dwin_scatter", grid_spec=grid_spec, out_shape=_scatter_out(d, e),
        compiler_params=_params("arbitrary", "arbitrary"))(_block_order(SCATTER_MASKS), dzh, dza, h)
    return own, rb


def _dwout_scatter(y_h, y_a, dxb):
    s, e = y_h.shape
    d = dxb.shape[1]
    r = 2 * e // N_DEV
    pairs = e // (2 * r)
    ts = _tile(s, 1024)
    ns = s // ts
    chip_masks = SCATTER_MASKS[1::2]
    passes = ((0, 1), (2,), (3,))
    slots = max(len(chips) for chips in passes)
    slot_chip = [chips[min(u, len(chips) - 1)] for chips in passes for u in range(slots)]

    def body(pair_ref, yh0_ref, ya0_ref, yh1_ref, ya1_ref, dx_ref, own_ref, ra_ref, rb_ref, acc, keep_buf, ship_buf,
             send_buf, tmp, sa_send, sa_recv, sb_send, sb_recv, loc_sem):
        p, step = pl.program_id(0), pl.program_id(1)
        x, y, c = _position()

        @pl.when(step == 0)
        def _():
            acc[...] = jnp.zeros_like(acc)

        for u, (yh_ref, ya_ref) in enumerate(((yh0_ref, ya0_ref), (yh1_ref, ya1_ref))):
            rows = slice(u * 2 * r, (u + 1) * 2 * r)
            used = functools.reduce(jnp.logical_or, [p == pp for pp, chips in enumerate(passes) if u < len(chips)])

            @pl.when(used & (pair_ref[slots * p + u] < pairs))
            def _():
                acc[rows, :] += _dot_tn(yh_ref[...], dx_ref[...])

            @pl.when(used & (pair_ref[slots * p + u] >= pairs))
            def _():
                acc[rows, :] += _dot_tn(ya_ref[...], dx_ref[...])

        def block_rows(u, core):
            return pl.ds(pl.multiple_of(u * 2 * r + core * r, r), r)

        slot_of = {q: u for chips in passes for u, q in enumerate(chips)}

        def ship(q):
            return pltpu.make_async_remote_copy(
                src_ref=ship_buf.at[slot_of[q]], dst_ref=ra_ref.at[q], send_sem=sa_send.at[q], recv_sem=sa_recv.at[q],
                device_id=(x, y, 1 - c), device_id_type=MESH)

        def send(q):
            return pltpu.make_async_remote_copy(
                src_ref=send_buf.at[slot_of[q]], dst_ref=rb_ref.at[q], send_sem=sb_send.at[q], recv_sem=sb_recv.at[q],
                device_id=_xor_peer(x, y, c, chip_masks[q]), device_id_type=MESH)

        def sibling_share(q):
            ship(q).wait_recv()
            fetch = pltpu.make_async_copy(ra_ref.at[q], tmp, loc_sem)
            fetch.start()
            fetch.wait()
            return tmp[...].astype(F32)

        shipped, sent = {}, {}
        for pp, chips in enumerate(passes):
            @pl.when((step == ns - 1) & (p == pp))
            def _():
                for u, q in enumerate(chips):
                    if u in shipped:
                        ship(shipped.pop(u)).wait_send()
                    ship_buf[u] = acc[block_rows(u, 1 - c), :].astype(BF16)
                    ship(q).start()
                    shipped[u] = q
                for u, q in enumerate(chips):
                    total = acc[block_rows(u, c), :] + sibling_share(q)
                    if q < N_CHIPS - 1:
                        if u in sent:
                            send(sent.pop(u)).wait_send()
                        send_buf[u] = total.astype(BF16)
                        send(q).start()
                        sent[u] = q
                    else:
                        keep_buf[...] = total
                        keep = pltpu.make_async_copy(keep_buf, own_ref, loc_sem)
                        keep.start()
                        keep.wait()
                if pp == len(passes) - 1:
                    for q in shipped.values():
                        ship(q).wait_send()
                    for q in sent.values():
                        send(q).wait_send()
                    for q in range(N_CHIPS - 1):
                        send(q).wait_recv()

    def y_spec(u, lo):
        return pl.BlockSpec((ts, 2 * r), lambda p, st, o: (st, jnp.clip(o[slots * p + u] - lo, 0, pairs - 1)))

    pair_of_chip = _block_order(chip_masks) // 2
    grid_spec = pltpu.PrefetchScalarGridSpec(
        num_scalar_prefetch=1, grid=(len(passes), ns),
        in_specs=[y_spec(0, 0), y_spec(0, pairs), y_spec(1, 0), y_spec(1, pairs),
                  pl.BlockSpec((ts, d), lambda p, st, o: (st, 0))],
        out_specs=(ANY, ANY, ANY),
        scratch_shapes=[pltpu.VMEM((slots * 2 * r, d), F32), pltpu.VMEM((r, d), F32),
                        pltpu.VMEM((slots, r, d), BF16)] + _scatter_scratch(r, d)[1:])
    own, _, rb = pl.pallas_call(
        body, name="dwout_scatter", grid_spec=grid_spec, out_shape=_scatter_out(r, d),
        compiler_params=_params("arbitrary", "arbitrary"))(
            jnp.stack([pair_of_chip[q] for q in slot_chip]), y_h, y_a, y_h, y_a, dxb)
    return own, rb


def _sum_chips_adamw(own, recv, w, m, v):
    r, c = w.shape
    tr = _tile(r, 128)

    def body(own_ref, rc_ref, w_ref, m_ref, v_ref, g_ref, d_ref, mo_ref, vo_ref):
        g = own_ref[...]
        for q in range(N_CHIPS - 1):
            g = g + rc_ref[q].astype(F32)
        g_ref[...] = g
        d_ref[...], mo_ref[...], vo_ref[...] = _adamw(w_ref[...], g, m_ref[...], v_ref[...])

    blk = pl.BlockSpec((tr, c), lambda i: (i, 0))
    shp = jax.ShapeDtypeStruct((r, c), F32)
    return pl.pallas_call(
        body, name="sum_chips_adamw", grid=(r // tr,), out_shape=(shp, shp, shp, shp),
        in_specs=[blk, pl.BlockSpec((N_CHIPS - 1, tr, c), lambda i: (0, i, 0)), blk, blk, blk],
        out_specs=(blk, blk, blk, blk), compiler_params=_params("parallel"))(own, recv, w, m, v)


SMALL_ROWS = 8
ROW_LB = 4
ROW_GN = 6
ROW_LOSS = 7


def _small_allreduce_adamw(part, w, m, v, lb_logits):
    width = part.shape[1]

    def body(p_ref, w_ref, m_ref, v_ref, lb_ref, g_ref, d_ref, mo_ref, vo_ref, buf, send_sems, recv_sems):
        x, y, c = _position()
        me = 4 * x + 2 * y + c
        buf[me] = p_ref[...]
        copies = []
        for k in range(N_DEV - 1):
            bx, by, bc = ((k + 1) >> 2) & 1, ((k + 1) >> 1) & 1, (k + 1) & 1
            peer = (x ^ bx, y ^ by, c ^ bc)
            copies.append(pltpu.make_async_remote_copy(
                src_ref=p_ref, dst_ref=buf.at[me], send_sem=send_sems.at[k], recv_sem=recv_sems.at[k],
                device_id=peer, device_id_type=MESH))
        for cp in copies:
            cp.start()
        for cp in copies:
            cp.wait_recv()
        for cp in copies:
            cp.wait_send()
        tot = buf[0]
        for dev in range(1, N_DEV):
            tot = tot + buf[dev]
        lbv = lb_ref[...]
        lb = _sigmoid(lbv[0:1] - lbv[1:2])
        glb = tot[ROW_LB:ROW_LB + 1] * lb * (1.0 - lb)
        loss = jnp.sum(tot[ROW_LOSS:ROW_LOSS + 1], axis=-1, keepdims=True)
        row = lax.broadcasted_iota(jnp.int32, (SMALL_ROWS, width), 0)
        g = jnp.where(row == ROW_LB, glb, jnp.where(row == ROW_LB + 1, -glb, tot))
        g = jnp.where(row == ROW_LOSS, loss, g)
        g_ref[...] = g
        d_ref[...], mo_ref[...], vo_ref[...] = _adamw(w_ref[...], g, m_ref[...], v_ref[...])

    vm = pl.BlockSpec(memory_space=pltpu.VMEM)
    shp = jax.ShapeDtypeStruct((SMALL_ROWS, width), F32)
    return pl.pallas_call(
        body, name="small_allreduce_adamw", out_shape=(shp, shp, shp, shp),
        in_specs=[vm] * 5, out_specs=(vm, vm, vm, vm),
        scratch_shapes=[pltpu.VMEM((N_DEV, SMALL_ROWS, width), F32), pltpu.SemaphoreType.DMA((N_DEV - 1,)),
                        pltpu.SemaphoreType.DMA((N_DEV - 1,))],
    )(part, w, m, v, lb_logits)


def _pack_small(norm_gain, final_gain, lb2, gnorm, last_row, width):
    pad = lambda a: jnp.pad(a.reshape(1, -1), ((0, 0), (0, width - a.size)))
    return jnp.concatenate([norm_gain.reshape(2, width), final_gain.reshape(2, width), lb2.reshape(2, width),
                            pad(gnorm), last_row.reshape(1, width)], axis=0)


def _unpack_small(p, d, e, hd):
    return (p[0:2].reshape(1, d), p[2:4].reshape(d), p[4:6].reshape(2, e), p[6:7, :hd].reshape(1, hd))


def kernel(x, norm_gain, w_in, lb_logits, hgrn_gnorm, w_out, final_gain, loss_target, m_norm_gain, m_w_in, m_lb_logits, m_hgrn_gnorm, m_w_out, m_final_gain, v_norm_gain, v_w_in, v_lb_logits, v_hgrn_gnorm, v_w_out, v_final_gain):
    s, d = x.shape[1], x.shape[2]
    e = w_in.shape[2]
    assert d == 2 * e and lb_logits.shape == (2, e) and w_out.shape[1] * N_DEV == 2 * e
    x2d = x.reshape(s, d)
    tgt = loss_target.reshape(s, d)

    h = _rmsnorm_fwd(x2d, norm_gain)
    z, w_in_full, w_out_full = _inproj_gather(h, _cast_bf16(w_in[0]), _cast_bf16(w_out[0]))
    w_out_full = w_out_full.reshape(2 * e, d)
    y_h, states = _hgrn_fwd(z, lb_logits, hgrn_gnorm)
    o_attn, lse, y_a = _attn_fwd(z)
    dx2, dx2b, dy, loss_vec, dfg = _outproj_loss(x2d, y_h, y_a, w_out_full, final_gain.reshape(1, d), tgt)

    own_o, recv_o = _dwout_scatter(y_h, y_a, dx2b)
    dza = _attn_bwd(z, dy, o_attn, lse)
    dzh, dlb, dgn = _hgrn_bwd(z, dy, states, lb_logits, hgrn_gnorm)
    grad_x, dng = _dh_dx(dzh, dza, w_in_full, x2d, norm_gain, dx2)
    g_wo, d_wo, nm_wo, nv_wo = _sum_chips_adamw(own_o, recv_o, w_out[0], m_w_out[0], v_w_out[0])

    width = d // 2
    zero_row = jnp.zeros((1, width), F32)
    loss_row = loss_vec[:, :width] + loss_vec[:, width:]
    part = _pack_small(dng, dfg, jnp.concatenate([dlb, zero_row], axis=0), dgn, loss_row, width)
    pw = _pack_small(norm_gain, final_gain, lb_logits, hgrn_gnorm, zero_row, width)
    pm = _pack_small(m_norm_gain, m_final_gain, m_lb_logits, m_hgrn_gnorm, zero_row, width)
    pv = _pack_small(v_norm_gain, v_final_gain, v_lb_logits, v_hgrn_gnorm, zero_row, width)
    sg, sd, sm, sv = _small_allreduce_adamw(part, pw, pm, pv, lb_logits)
    own_i, recv_i = _dwin_scatter(h, dzh, dza)
    g_wi, d_wi, nm_wi, nv_wi = _sum_chips_adamw(own_i, recv_i, w_in[0], m_w_in[0], v_w_in[0])
    hd = hgrn_gnorm.shape[1]
    g_ng, g_fg, g_lb, g_gn = _unpack_small(sg, d, e, hd)
    d_ng, d_fg, d_lb, d_gn = _unpack_small(sd, d, e, hd)
    m_ng, m_fg, m_lb, m_gn = _unpack_small(sm, d, e, hd)
    v_ng, v_fg, v_lb, v_gn = _unpack_small(sv, d, e, hd)
    loss = sg[ROW_LOSS, 0]

    one = lambda a: a[None]
    return (loss, grad_x.reshape(1, s, d), g_ng, one(g_wi), g_lb, g_gn, one(g_wo), g_fg,
            d_ng, one(d_wi), d_lb, d_gn, one(d_wo), d_fg,
            m_ng, one(nm_wi), m_lb, m_gn, one(nm_wo), m_fg,
            v_ng, one(nv_wi), v_lb, v_gn, one(nv_wo), v_fg)
```

```python
import functools
import math

import jax
import jax.numpy as jnp
from jax import lax
from jax.experimental import pallas as pl
from jax.experimental.pallas import tpu as pltpu

NORM_EPS = 1e-6
HGRN_HEAD = 128
HGRN_CHUNK = 64
ATTN_HEAD = 64
ATTN_BAND = 128
DILATIONS = (1, 4, 16)
N_SPLITS = 8
N_DEV = 8
ADAM_LR = 0.001
ADAM_B1 = 0.9
ADAM_B2 = 0.999
ADAM_EPS = 1e-08
ADAM_WD = 0.01
ADAM_STEP = 10
LANES = 128
MESH = pl.DeviceIdType.MESH
F32 = jnp.float32
BF16 = jnp.bfloat16
NEG_BIG = -1e30
VMEM_LIMIT = 56 * 1024 * 1024

ANY = pl.BlockSpec(memory_space=pl.ANY)


def _params(*sem):
    return pltpu.CompilerParams(dimension_semantics=sem, vmem_limit_bytes=VMEM_LIMIT)


def _tile(n, pref):
    t = min(n, pref)
    assert n % t == 0, (n, pref)
    return t


def _dot(a, b, precision=None):
    return jnp.dot(a, b, preferred_element_type=F32, precision=precision)


def _dot_nt(a, b):
    return lax.dot_general(a, b, (((1,), (1,)), ((), ())), preferred_element_type=F32)


def _dot_tn(a, b):
    return lax.dot_general(a, b, (((0,), (0,)), ((), ())), preferred_element_type=F32)


def _sigmoid(x):
    return 0.5 * jnp.tanh(0.5 * x) + 0.5


def _dsilu(x, s):
    return s * (1.0 + x * (1.0 - s))


def _adamw(w, g, m, v):
    m = ADAM_B1 * m + (1.0 - ADAM_B1) * g
    v = ADAM_B2 * v + (1.0 - ADAM_B2) * (g * g)
    m_hat = m / (1.0 - ADAM_B1 ** ADAM_STEP)
    v_hat = v / (1.0 - ADAM_B2 ** ADAM_STEP)
    delta = -ADAM_LR * (m_hat / (jnp.sqrt(v_hat) + ADAM_EPS) + ADAM_WD * w)
    return delta, m, v


def _cast_bf16(a):
    r, c = a.shape
    tr = _tile(r, 256)

    def body(a_ref, o_ref):
        o_ref[...] = a_ref[...].astype(BF16)

    return pl.pallas_call(
        body, name="cast_bf16", grid=(r // tr,), out_shape=jax.ShapeDtypeStruct((r, c), BF16),
        in_specs=[pl.BlockSpec((tr, c), lambda i: (i, 0))], out_specs=pl.BlockSpec((tr, c), lambda i: (i, 0)),
        compiler_params=_params("parallel"))(a)


def _rmsnorm_fwd(x, gain):
    s, d = x.shape
    tm = _tile(s, 512)

    def body(x_ref, g_ref, h_ref):
        xv = x_ref[...]
        r = lax.rsqrt(jnp.mean(xv * xv, axis=-1, keepdims=True) + NORM_EPS)
        h_ref[...] = (xv * r * g_ref[...]).astype(BF16)

    return pl.pallas_call(
        body, name="rmsnorm_fwd", grid=(s // tm,), out_shape=jax.ShapeDtypeStruct((s, d), BF16),
        in_specs=[pl.BlockSpec((tm, d), lambda i: (i, 0)), pl.BlockSpec((1, d), lambda i: (0, 0))],
        out_specs=pl.BlockSpec((tm, d), lambda i: (i, 0)), compiler_params=_params("parallel"))(x, gain)


HGRN_BLOCK = 2048
TRI_ROWS = 256


def _chunk_masks():
    tb = TRI_ROWS
    row = lax.broadcasted_iota(jnp.int32, (tb, tb), 0)
    col = lax.broadcasted_iota(jnp.int32, (tb, tb), 1)
    same = (row // HGRN_CHUNK) == (col // HGRN_CHUNK)
    lower = jnp.where(same & (col <= row), 1.0, 0.0).astype(BF16)
    upper = jnp.where(same & (col >= row), 1.0, 0.0).astype(BF16)
    return lower, upper


def _split3(a):
    hi = a.astype(BF16).astype(F32)
    mid = (a - hi).astype(BF16).astype(F32)
    lo = (a - hi - mid).astype(BF16).astype(F32)
    return hi, mid, lo


def _tri_dot(tri, x):
    hi, mid, lo = (p.astype(BF16) for p in _split3(x))
    outs = []
    for r in range(0, x.shape[0], TRI_ROWS):
        sl = slice(r, r + TRI_ROWS)
        outs.append(_dot(tri, hi[sl]) + _dot(tri, mid[sl]) + _dot(tri, lo[sl]))
    return outs[0] if len(outs) == 1 else jnp.concatenate(outs, axis=0)


def _hgrn_gates(qp, fp, lbv):
    lb = _sigmoid(lbv[0:1] - lbv[1:2])
    sq = _sigmoid(qp)
    q = qp * sq
    sg = _sigmoid(fp)
    f = lb + (1.0 - lb) * sg
    k = 1.0 - f
    return lb, sq, q, sg, f, k


def _hgrn_fwd(z, lb_logits, gnorm):
    s = z.shape[0]
    e = z.shape[1] // N_SPLITS
    nh = e // HGRN_HEAD
    tb = _tile(s, HGRN_BLOCK)
    nc = tb // HGRN_CHUNK
    nb = s // tb
    C = HGRN_CHUNK

    def body(q_ref, f_ref, i_ref, g_ref, lb_ref, gn_ref, y_ref, st_ref, state, o_scr):
        @pl.when(pl.program_id(1) == 0)
        def _():
            state[...] = jnp.zeros_like(state)

        lb, sq, q, sg, f, k = _hgrn_gates(q_ref[...], f_ref[...], lb_ref[...])
        lower, _ = _chunk_masks()
        b = _tri_dot(lower, jnp.log(f))
        b3 = b.reshape(nc, C, HGRN_HEAD)
        bc = b3[:, C - 1:C, :]
        qt = (q * jnp.exp(b)).astype(BF16)
        kt = (k * jnp.exp(-b)).astype(BF16)
        ke = (k.reshape(nc, C, HGRN_HEAD) * jnp.exp(bc - b3)).reshape(tb, HGRN_HEAD).astype(BF16)
        v = i_ref[...].astype(BF16)
        tri = lax.broadcasted_iota(jnp.int32, (C, C), 1) <= lax.broadcasted_iota(jnp.int32, (C, C), 0)
        sls = [slice(c * C, (c + 1) * C) for c in range(nc)]
        kv = [_dot_tn(v[sl], ke[sl]) for sl in sls]
        a = [jnp.where(tri, _dot_nt(qt[sl], kt[sl]), 0.0).astype(BF16) for sl in sls]
        st = state[...]
        sts = []
        for c in range(nc):
            sts.append(st)
            st_ref[c] = st
            st = st * jnp.exp(bc[c]) + kv[c]
        state[...] = st
        for c, sl in enumerate(sls):
            o_scr[sl, :] = _dot(a[c], v[sl]) + _dot_nt(qt[sl], sts[c].astype(BF16))
        o = o_scr[...]
        rms = lax.rsqrt(jnp.mean(o * o, axis=-1, keepdims=True) + NORM_EPS)
        gp = g_ref[...]
        y_ref[...] = (o * rms * gn_ref[...] * (gp * _sigmoid(gp))).astype(BF16)

    col = lambda kk: (lambda h, n: (n, kk * nh + h))
    return pl.pallas_call(
        body, name="hgrn_fwd", grid=(nh, nb),
        out_shape=(jax.ShapeDtypeStruct((s, e), BF16),
                   jax.ShapeDtypeStruct((nh, s // C, HGRN_HEAD, HGRN_HEAD), F32)),
        in_specs=[pl.BlockSpec((tb, HGRN_HEAD), col(0)), pl.BlockSpec((tb, HGRN_HEAD), col(1)),
                  pl.BlockSpec((tb, HGRN_HEAD), col(2)), pl.BlockSpec((tb, HGRN_HEAD), col(3)),
                  pl.BlockSpec((2, HGRN_HEAD), lambda h, n: (0, h)), pl.BlockSpec((1, HGRN_HEAD), lambda h, n: (0, 0))],
        out_specs=(pl.BlockSpec((tb, HGRN_HEAD), lambda h, n: (n, h)),
                   pl.BlockSpec((None, nc, HGRN_HEAD, HGRN_HEAD), lambda h, n: (h, n, 0, 0))),
        scratch_shapes=[pltpu.VMEM((HGRN_HEAD, HGRN_HEAD), F32), pltpu.VMEM((tb, HGRN_HEAD), F32)],
        compiler_params=_params("parallel", "arbitrary"))(z, z, z, z, lb_logits, gnorm)


def _hgrn_bwd(z, dy, states, lb_logits, gnorm):
    s = z.shape[0]
    e = z.shape[1] // N_SPLITS
    nh = e // HGRN_HEAD
    tb = _tile(s, HGRN_BLOCK)
    nc = tb // HGRN_CHUNK
    nb = s // tb
    C = HGRN_CHUNK
    H = HGRN_HEAD

    def body(q_ref, f_ref, i_ref, g_ref, dy_ref, st_ref, lb_ref, gn_ref, dz_ref, dlb_ref, dgn_ref,
             gstate, o_scr, dq_scr, dk_scr, dv_scr, e_scr):
        first = (pl.program_id(0) == 0) & (pl.program_id(1) == 0)

        @pl.when(first)
        def _():
            dgn_ref[...] = jnp.zeros_like(dgn_ref)

        @pl.when(pl.program_id(1) == 0)
        def _():
            gstate[...] = jnp.zeros_like(gstate)
            dlb_ref[...] = jnp.zeros_like(dlb_ref)

        qp = q_ref[...]
        lb, sq, q, sg, f, k = _hgrn_gates(qp, f_ref[...], lb_ref[...])
        lower, upper = _chunk_masks()
        b = _tri_dot(lower, jnp.log(f))
        b3 = b.reshape(nc, C, H)
        bc = b3[:, C - 1:C, :]
        eb = jnp.exp(b)
        enb = jnp.exp(-b)
        eend = jnp.exp(bc - b3).reshape(tb, H)
        qt = (q * eb).astype(BF16)
        kt = (k * enb).astype(BF16)
        ke = (k * eend).astype(BF16)
        v = i_ref[...].astype(BF16)
        tri = lax.broadcasted_iota(jnp.int32, (C, C), 1) <= lax.broadcasted_iota(jnp.int32, (C, C), 0)
        sls = [slice(c * C, (c + 1) * C) for c in range(nc)]
        a = [jnp.where(tri, _dot_nt(qt[sl], kt[sl]), 0.0).astype(BF16) for sl in sls]
        for c, sl in enumerate(sls):
            o_scr[sl, :] = _dot(a[c], v[sl]) + _dot_nt(qt[sl], st_ref[c].astype(BF16))
        o = o_scr[...]
        rms = lax.rsqrt(jnp.mean(o * o, axis=-1, keepdims=True) + NORM_EPS)
        on = o * rms
        gn = gn_ref[...]
        gp = g_ref[...]
        sgg = _sigmoid(gp)
        dyv = dy_ref[...]
        d_on = dyv * (gp * sgg)
        dz_ref[3] = (dyv * on * gn * _dsilu(gp, sgg)).astype(BF16)
        dgn_ref[...] += jnp.sum(d_on * on, axis=0, keepdims=True)
        u = d_on * gn
        do = (rms * (u - on * jnp.mean(u * on, axis=-1, keepdims=True))).astype(BF16)
        gup = [_dot_tn(do[sl], qt[sl]) for sl in sls]
        da = [jnp.where(tri, _dot_nt(do[sl], v[sl]), 0.0).astype(BF16) for sl in sls]
        gt = gstate[...]
        gts = [None] * nc
        for c in reversed(range(nc)):
            gts[c] = gt
            gt = gt * jnp.exp(bc[c]) + gup[c]
        gstate[...] = gt
        for c, sl in enumerate(sls):
            stp = st_ref[c]
            gtb = gts[c].astype(BF16)
            dqt = _dot(da[c], kt[sl]) + _dot(do[sl], stp.astype(BF16))
            dkt = _dot_tn(da[c], qt[sl])
            dks = _dot(v[sl], gtb) * eend[sl]
            dv_scr[sl, :] = _dot_tn(a[c], do[sl]) + _dot_nt(ke[sl], gtb)
            dq_scr[sl, :] = dqt * eb[sl]
            dk_scr[sl, :] = dkt * enb[sl] + dks
            ech = (jnp.sum(k[sl] * dks, axis=0, keepdims=True)
                   + jnp.sum(gts[c] * jnp.exp(bc[c]) * stp, axis=0, keepdims=True))
            e_scr[sl, :] = jnp.broadcast_to(ech, (C, H))
        dq = dq_scr[...]
        dk = dk_scr[...]
        dlf = _tri_dot(upper, q * dq - k * dk) + e_scr[...]
        dft = dlf / f - dk
        dz_ref[0] = (dq * _dsilu(qp, sq)).astype(BF16)
        dz_ref[1] = (dft * (1.0 - lb) * sg * (1.0 - sg)).astype(BF16)
        dz_ref[2] = dv_scr[...].astype(BF16)
        dlb_ref[...] += jnp.sum(dft * (1.0 - sg), axis=0, keepdims=True)

    col = lambda kk: (lambda h, n: (nb - 1 - n, kk * nh + h))
    return pl.pallas_call(
        body, name="hgrn_bwd", grid=(nh, nb),
        out_shape=(jax.ShapeDtypeStruct((4, s, e), BF16), jax.ShapeDtypeStruct((1, e), F32),
                   jax.ShapeDtypeStruct((1, H), F32)),
        in_specs=[pl.BlockSpec((tb, H), col(0)), pl.BlockSpec((tb, H), col(1)),
                  pl.BlockSpec((tb, H), col(2)), pl.BlockSpec((tb, H), col(3)),
                  pl.BlockSpec((tb, H), lambda h, n: (nb - 1 - n, h)),
                  pl.BlockSpec((None, nc, H, H), lambda h, n: (h, nb - 1 - n, 0, 0)),
                  pl.BlockSpec((2, H), lambda h, n: (0, h)), pl.BlockSpec((1, H), lambda h, n: (0, 0))],
        out_specs=(pl.BlockSpec((4, tb, H), lambda h, n: (0, nb - 1 - n, h)),
                   pl.BlockSpec((1, H), lambda h, n: (0, h)), pl.BlockSpec((1, H), lambda h, n: (0, 0))),
        scratch_shapes=[pltpu.VMEM((H, H), F32)] + [pltpu.VMEM((tb, H), F32)] * 5,
        compiler_params=_params("arbitrary", "arbitrary"))(z, z, z, z, dy, states, lb_logits, gnorm)


ATTN_T = 16 * ATTN_BAND
SCALE = ATTN_HEAD ** -0.5
TILE_UNROLL = 2


def _slope(hh, nheads):
    head = (2 * pl.program_id(0) + hh + 1).astype(F32)
    return jnp.exp(jnp.full((1, 1), -8.0 / nheads * math.log(2.0), F32) * head)


def _fill_bias(bias, nheads, delta, edge_ok):
    band = (delta >= 0) & (delta <= ATTN_BAND)
    dist = delta.astype(F32)
    for pi, dil in enumerate(DILATIONS):
        for hh in range(2):
            full = jnp.where(band, -(_slope(hh, nheads) * float(dil)) * dist, NEG_BIG)
            bias[(pi * 2 + hh) * 2] = full
            bias[(pi * 2 + hh) * 2 + 1] = jnp.where(edge_ok, full, NEG_BIG)


def _rows(start, size, stride):
    if stride == 1:
        return pl.ds(pl.multiple_of(start, ATTN_BAND), size)
    return pl.ds(start, size, stride=stride)


def _head_lanes(rows, hh):
    return (lax.broadcasted_iota(jnp.int32, (rows, LANES), 1) // ATTN_HEAD) == hh


def _attn_fwd(z):
    s = z.shape[0]
    e = z.shape[1] // N_SPLITS
    npair = e // LANES
    T = ATTN_T
    assert s % T == 0
    nsb = s // T
    W = ATTN_BAND
    nt = T // W
    HD = ATTN_HEAD
    chunk = 256

    def body(q_ref, kp_ref, kc_ref, vp_ref, vc_ref, g_ref, o_ref, l_ref, y_ref, qa, kbuf, va, bias, accs, ms, lsw):
        sb = pl.program_id(1)
        def stage(i, carry):
            rows = pl.ds(pl.multiple_of(i * chunk, chunk), chunk)
            upper = pl.ds(pl.multiple_of(T + i * chunk, chunk), chunk)
            kbuf[upper, :] = kc_ref[rows, :]
            for hh in range(2):
                mine = _head_lanes(chunk, hh)
                qa[hh, rows, :] = jnp.where(mine, q_ref[rows, :] * SCALE, 0.0)
                va[hh, upper, :] = jnp.where(mine, vc_ref[rows, :], 1.0)
            return carry

        lax.fori_loop(0, T // chunk, stage, 0)

        @pl.when(sb == 0)
        def _():
            def stage_prev(i, carry):
                rows = pl.ds(pl.multiple_of(i * chunk, chunk), chunk)
                kbuf[rows, :] = kp_ref[rows, :]
                for hh in range(2):
                    va[hh, rows, :] = jnp.where(_head_lanes(chunk, hh), vp_ref[rows, :], 1.0)
                return carry

            lax.fori_loop(0, T // chunk, stage_prev, 0)
        qi = lax.broadcasted_iota(jnp.int32, (W, 2 * W), 0)
        kj = lax.broadcasted_iota(jnp.int32, (W, 2 * W), 1)
        _fill_bias(bias, 2 * npair, W + qi - kj, kj >= W)

        def tile(tau, carry):
            first = _head_lanes(W, 0)
            rows, scores = [], []
            for pi, dil in enumerate(DILATIONS):
                r = tau % dil
                ub = tau // dil
                qrows = _rows(r + dil * W * ub, W, dil)
                krows = _rows(T + dil * W * (ub - 1) + r, 2 * W, dil)
                var = jnp.where((sb == 0) & (ub == 0), 1, 0)
                kt = kbuf[krows, :].astype(BF16)
                rows.append((qrows, krows))
                scores.append([_dot_nt(qa[hh, qrows, :].astype(BF16), kt) + bias[(pi * 2 + hh) * 2 + var]
                               for hh in range(2)])
            maxes = [[jnp.max(sc, axis=-1, keepdims=True) for sc in pair] for pair in scores]
            probs = [[jnp.exp(sc - m).astype(BF16) for sc, m in zip(ps, pm)] for ps, pm in zip(scores, maxes)]
            for pi, (qrows, krows) in enumerate(rows):
                outs = [_dot(probs[pi][hh], va[hh, krows, :].astype(BF16)) for hh in range(2)]
                accs[pi, qrows, :] = jnp.where(first, outs[0], outs[1])
                lsw[pi, qrows, :] = jnp.where(first, outs[1], outs[0])
                ms[pi, qrows, :] = jnp.where(first, maxes[pi][0], maxes[pi][1])
            return carry

        lax.fori_loop(0, nt, tile, 0, unroll=TILE_UNROLL)

        def merge(i, carry):
            rows = pl.ds(pl.multiple_of(i * chunk, chunk), chunk)
            m1, m2, m3 = ms[0, rows, :], ms[1, rows, :], ms[2, rows, :]
            mx = jnp.maximum(jnp.maximum(m1, m2), m3)
            w1, w2, w3 = jnp.exp(m1 - mx), jnp.exp(m2 - mx), jnp.exp(m3 - mx)
            unswap = lambda a: pltpu.roll(a, ATTN_HEAD, 1)
            den = w1 * unswap(lsw[0, rows, :]) + w2 * unswap(lsw[1, rows, :]) + w3 * unswap(lsw[2, rows, :])
            o = (w1 * accs[0, rows, :] + w2 * accs[1, rows, :] + w3 * accs[2, rows, :]) / den
            o_ref[rows, :] = o
            l_ref[rows, :] = mx + jnp.log(den)
            gp = g_ref[rows, :]
            y_ref[rows, :] = (o * (gp * _sigmoid(gp))).astype(BF16)
            upper = pl.ds(pl.multiple_of(T + i * chunk, chunk), chunk)
            kbuf[rows, :] = kbuf[upper, :]
            for hh in range(2):
                va[hh, rows, :] = va[hh, upper, :]
            return carry

        lax.fori_loop(0, T // chunk, merge, 0)

    cur = lambda split: (lambda hp, sb: (sb, split * npair + hp))
    prev = lambda split: (lambda hp, sb: (0, split * npair + hp))
    blk = lambda index: pl.BlockSpec((T, LANES), index)
    out = blk(lambda hp, sb: (sb, hp))
    buf = lambda rows: pltpu.VMEM((rows, LANES), F32)
    return pl.pallas_call(
        body, name="attn_fwd", grid=(npair, nsb),
        out_shape=(jax.ShapeDtypeStruct((s, e), F32), jax.ShapeDtypeStruct((s, e), F32), jax.ShapeDtypeStruct((s, e), BF16)),
        in_specs=[blk(cur(4)), blk(prev(5)), blk(cur(5)), blk(prev(6)), blk(cur(6)), blk(cur(7))],
        out_specs=(out, out, out),
        scratch_shapes=[pltpu.VMEM((2, T, LANES), F32), buf(2 * T), pltpu.VMEM((2, 2 * T, LANES), F32),
                        pltpu.VMEM((12, W, 2 * W), F32)] + [pltpu.VMEM((3, T, LANES), F32)] * 3,
        compiler_params=_params("parallel", "arbitrary"))(z, z, z, z, z, z)


def _outproj_loss(x, y_h, y_a, w_out_full, final_gain, target):
    s, d = x.shape
    e = y_h.shape[1]
    tm = _tile(s, 256)

    def body(x_ref, yh_ref, ya_ref, w_ref, g_ref, t_ref, dx_ref, dxb_ref, dy_ref, loss_ref, dg_ref):
        @pl.when(pl.program_id(0) == 0)
        def _():
            loss_ref[...] = jnp.zeros_like(loss_ref)
            dg_ref[...] = jnp.zeros_like(dg_ref)

        w = w_ref[...]
        x2 = x_ref[...] + _dot(yh_ref[...], w[0:e]) + _dot(ya_ref[...], w[e:2 * e])
        r = lax.rsqrt(jnp.mean(x2 * x2, axis=-1, keepdims=True) + NORM_EPS)
        xn = x2 * r
        g = g_ref[...]
        err = xn * g - t_ref[...]
        loss_ref[...] += jnp.sum(err * err, axis=0, keepdims=True) * (0.5 / d)
        dyo = err * (1.0 / d)
        dg_ref[...] += jnp.sum(dyo * xn, axis=0, keepdims=True)
        u = dyo * g
        dx2 = r * (u - xn * jnp.mean(u * xn, axis=-1, keepdims=True))
        dx_ref[...] = dx2
        dxb = dx2.astype(BF16)
        dxb_ref[...] = dxb
        dy_ref[...] = _dot_nt(dxb, w)

    row = pl.BlockSpec((tm, d), lambda i: (i, 0))
    half = pl.BlockSpec((tm, e), lambda i: (i, 0))
    vec = pl.BlockSpec((1, d), lambda i: (0, 0))
    return pl.pallas_call(
        body, name="outproj_loss", grid=(s // tm,),
        out_shape=(jax.ShapeDtypeStruct((s, d), F32), jax.ShapeDtypeStruct((s, d), BF16),
                   jax.ShapeDtypeStruct((s, 2 * e), F32), jax.ShapeDtypeStruct((1, d), F32),
                   jax.ShapeDtypeStruct((1, d), F32)),
        in_specs=[row, half, half, pl.BlockSpec((2 * e, d), lambda i: (0, 0)), vec, row],
        out_specs=(row, row, pl.BlockSpec((tm, 2 * e), lambda i: (i, 0)), vec, vec),
        compiler_params=_params("arbitrary"))(x, y_h, y_a, w_out_full, final_gain, target)


def _attn_bwd(z, dy, o, lse):
    s, e = o.shape
    npair = e // LANES
    T = ATTN_T
    assert s % T == 0
    nsb = s // T
    W = ATTN_BAND
    nt = T // W
    HD = ATTN_HEAD
    chunk = 256

    def body(k_ref, v_ref, qc_ref, qn_ref, dyc_ref, dyn_ref, gc_ref, gn_ref, oc_ref, on_ref, lc_ref, ln_ref,
             dz_ref, qa, doa, ka, va, dqacc, dkacc, dvacc, bias):
        sb = pl.program_id(1)
        def stage_queries(half, q_r, dy_r, g_r, o_r, l_r):
            def stage(i, carry):
                rows = pl.ds(pl.multiple_of(i * chunk, chunk), chunk)
                dst = pl.ds(pl.multiple_of(half * T + i * chunk, chunk), chunk)
                lane = lax.broadcasted_iota(jnp.int32, (chunk, LANES), 1)
                gp = g_r[rows, :]
                dov = dy_r[rows, :] * (gp * _sigmoid(gp))
                qv = q_r[rows, :] * SCALE
                same_head = (lax.broadcasted_iota(jnp.int32, (LANES, LANES), 0) // HD
                             == lax.broadcasted_iota(jnp.int32, (LANES, LANES), 1) // HD)
                ones = jnp.where(same_head, 1.0, 0.0).astype(BF16)
                hi, mid, lo = (p.astype(BF16) for p in _split3(dov * o_r[rows, :]))
                delta = _dot(hi, ones) + _dot(mid, ones) + _dot(lo, ones)
                swap = lambda a: pltpu.roll(a, HD, 1)
                lse_parts = [swap(p) for p in _split3(l_r[rows, :])]
                dl_parts = [swap(p) for p in _split3(delta)]
                for hh in range(2):
                    mine = _head_lanes(chunk, hh)
                    spare = (1 - hh) * HD
                    qh = jnp.where(mine, qv, 0.0)
                    dh = jnp.where(mine, dov, 0.0)
                    for j in range(3):
                        qh = jnp.where(lane == spare + j, lse_parts[j], qh)
                        dh = jnp.where(lane == spare + j, dl_parts[j], dh)
                    qa[hh, dst, :] = qh
                    doa[hh, dst, :] = dh
                return carry

            lax.fori_loop(0, T // chunk, stage, 0)

        @pl.when(sb == 0)
        def _():
            stage_queries(0, qc_ref, dyc_ref, gc_ref, oc_ref, lc_ref)

        stage_queries(1, qn_ref, dyn_ref, gn_ref, on_ref, ln_ref)

        def stage_keys(i, carry):
            rows = pl.ds(pl.multiple_of(i * chunk, chunk), chunk)
            lane = lax.broadcasted_iota(jnp.int32, (chunk, LANES), 1)
            for hh in range(2):
                spare = (1 - hh) * HD
                minus = (lane >= spare) & (lane < spare + 3)
                ka[hh, rows, :] = jnp.where(minus, -1.0, k_ref[rows, :])
                va[hh, rows, :] = jnp.where(minus, -1.0, v_ref[rows, :])
            gp = gc_ref[rows, :]
            dz_ref[3, rows, :] = (dyc_ref[rows, :] * oc_ref[rows, :] * _dsilu(gp, _sigmoid(gp))).astype(BF16)
            return carry

        lax.fori_loop(0, T // chunk, stage_keys, 0)

        @pl.when(sb == 0)
        def _():
            dqacc[0:T, :] = jnp.zeros((T, LANES), F32)

        dqacc[T:, :] = jnp.zeros((T, LANES), F32)
        dkacc[...] = jnp.zeros_like(dkacc)
        dvacc[...] = jnp.zeros_like(dvacc)
        kj = lax.broadcasted_iota(jnp.int32, (W, 2 * W), 0)
        qi = lax.broadcasted_iota(jnp.int32, (W, 2 * W), 1)
        _fill_bias(bias, 2 * npair, qi - kj, qi < W)

        def tile(tau, carry):
            def scores(step, pi):
                dil = DILATIONS[pi]
                r = step % dil
                ub = step // dil
                start = r + dil * W * ub
                krows = _rows(start, W, dil)
                qrows = _rows(start, 2 * W, dil)
                var = jnp.where((sb == nsb - 1) & (ub == nt // dil - 1), 1, 0)
                unit = dict(krows=krows, qrows=qrows, ops=[], sc=[], dpd=[])
                for hh in range(2):
                    kt = ka[hh, krows, :].astype(BF16)
                    vt = va[hh, krows, :].astype(BF16)
                    qt = qa[hh, qrows, :].astype(BF16)
                    dt = doa[hh, qrows, :].astype(BF16)
                    unit["ops"].append((kt, qt, dt))
                    unit["sc"].append(_dot_nt(kt, qt) + bias[(pi * 2 + hh) * 2 + var])
                    unit["dpd"].append(_dot_nt(vt, dt))
                return unit

            def elementwise(unit):
                ps = [jnp.exp(s_) for s_ in unit["sc"]]
                unit["ds"] = [(p * d).astype(BF16) for p, d in zip(ps, unit["dpd"])]
                unit["pb"] = [p.astype(BF16) for p in ps]

            def products(unit):
                dvs = [_dot(pb, dt) for pb, (kt, qt, dt) in zip(unit["pb"], unit["ops"])]
                dks = [_dot(ds, qt) for ds, (kt, qt, dt) in zip(unit["ds"], unit["ops"])]
                dqs = [_dot_tn(ds, kt) for ds, (kt, qt, dt) in zip(unit["ds"], unit["ops"])]
                dkacc[unit["krows"], :] += jnp.where(_head_lanes(W, 0), dks[0], dks[1])
                dvacc[unit["krows"], :] += jnp.where(_head_lanes(W, 0), dvs[0], dvs[1])
                dqacc[unit["qrows"], :] += jnp.where(_head_lanes(2 * W, 0), dqs[0], dqs[1]) * SCALE

            order = [(2 * tau + half, pi) for half in range(2) for pi in range(len(DILATIONS))]
            units = [None] * len(order)
            for n in range(len(order) + 2):
                if n < len(order):
                    units[n] = scores(*order[n])
                if 1 <= n <= len(order):
                    elementwise(units[n - 1])
                if n >= 2:
                    products(units[n - 2])
            return carry

        lax.fori_loop(0, nt // 2, tile, 0)

        def flush(i, carry):
            rows = pl.ds(pl.multiple_of(i * chunk, chunk), chunk)
            nxt = pl.ds(pl.multiple_of(T + i * chunk, chunk), chunk)
            dz_ref[0, rows, :] = dqacc[rows, :].astype(BF16)
            dz_ref[1, rows, :] = dkacc[rows, :].astype(BF16)
            dz_ref[2, rows, :] = dvacc[rows, :].astype(BF16)
            dqacc[rows, :] = dqacc[nxt, :]
            for hh in range(2):
                qa[hh, rows, :] = qa[hh, nxt, :]
                doa[hh, rows, :] = doa[hh, nxt, :]
            return carry

        lax.fori_loop(0, T // chunk, flush, 0)

    zc = lambda split: (lambda hp, sb: (sb, split * npair + hp))
    zn = lambda split: (lambda hp, sb: (jnp.minimum(sb + 1, nsb - 1), split * npair + hp))
    ec = lambda off: (lambda hp, sb: (sb, off + hp))
    en = lambda off: (lambda hp, sb: (jnp.minimum(sb + 1, nsb - 1), off + hp))
    z0 = lambda split: (lambda hp, sb: (0, split * npair + hp))
    e0 = lambda off: (lambda hp, sb: (0, off + hp))
    blk = lambda index: pl.BlockSpec((T, LANES), index)
    buf = lambda rows: pltpu.VMEM((rows, LANES), F32)
    return pl.pallas_call(
        body, name="attn_bwd", grid=(npair, nsb), out_shape=jax.ShapeDtypeStruct((4, s, e), BF16),
        in_specs=[blk(zc(5)), blk(zc(6)), blk(z0(4)), blk(zn(4)), blk(ec(npair)), blk(en(npair)),
                  blk(zc(7)), blk(zn(7)), blk(ec(0)), blk(en(0)), blk(e0(0)), blk(en(0))],
        out_specs=pl.BlockSpec((4, T, LANES), lambda hp, sb: (0, sb, hp)),
        scratch_shapes=[pltpu.VMEM((2, 2 * T, LANES), F32), pltpu.VMEM((2, 2 * T, LANES), F32),
                        pltpu.VMEM((2, T, LANES), F32), pltpu.VMEM((2, T, LANES), F32),
                        buf(2 * T), buf(T), buf(T), pltpu.VMEM((12, W, 2 * W), F32)],
        compiler_params=_params("parallel", "arbitrary"))(z, z, z, z, dy, dy, z, z, o, o, lse, lse)


def _dz_specs(tm, e):
    def mk(lo, hi):
        return pl.BlockSpec((None, tm, e), lambda i, k: (jnp.clip(k - lo, 0, hi - lo - 1), i, 0))
    return [mk(0, 4), mk(4, 8)]


def _dz_pick(grp, dzh_ref, dza_ref, fn):
    @pl.when(grp < 4)
    def _():
        fn(dzh_ref[...])

    @pl.when(grp >= 4)
    def _():
        fn(dza_ref[...])


def _dh_dx(dzh, dza, w_full, x, gain, dx2):
    s, d = x.shape
    e = dzh.shape[2]
    tm = _tile(s, 1024)
    ni = s // tm
    chunk = _tile(tm, 256)
    fetch_at = 2

    def body(dzh_ref, dza_ref, w_ref, x_hbm, g_ref, dx2_hbm, gx_hbm, dg_ref, acc, xbuf, dbuf, sems):
        i, k = pl.program_id(0), pl.program_id(1)
        tile_rows = pl.ds(pl.multiple_of(i * tm, tm), tm)
        fetch_x = pltpu.make_async_copy(x_hbm.at[tile_rows, :], xbuf, sems.at[0])
        fetch_d = pltpu.make_async_copy(dx2_hbm.at[tile_rows, :], dbuf, sems.at[1])
        store = pltpu.make_async_copy(xbuf, gx_hbm.at[tile_rows, :], sems.at[2])

        @pl.when((i == 0) & (k == 0))
        def _():
            dg_ref[...] = jnp.zeros_like(dg_ref)

        @pl.when(k == 0)
        def _():
            acc[...] = jnp.zeros_like(acc)

        @pl.when((k == fetch_at) & (i > 0))
        def _():
            store.wait()

        @pl.when(k == fetch_at)
        def _():
            fetch_x.start()
            fetch_d.start()

        def add(dz):
            acc[...] += _dot_nt(dz, w_ref[...])

        _dz_pick(k, dzh_ref, dza_ref, add)

        @pl.when(k == N_SPLITS - 1)
        def _():
            fetch_x.wait()
            fetch_d.wait()
            gain_row = g_ref[...]

            def finish(c, dg):
                rows = pl.ds(pl.multiple_of(c * chunk, chunk), chunk)
                dh = acc[rows, :]
                xv = xbuf[rows, :]
                r = lax.rsqrt(jnp.mean(xv * xv, axis=-1, keepdims=True) + NORM_EPS)
                xn = xv * r
                u = dh * gain_row
                xbuf[rows, :] = dbuf[rows, :] + r * (u - xn * jnp.mean(u * xn, axis=-1, keepdims=True))
                return dg + jnp.sum(dh * xn, axis=0, keepdims=True)

            dg_ref[...] += lax.fori_loop(0, tm // chunk, finish, jnp.zeros((1, d), F32))
            store.start()

        @pl.when((k == N_SPLITS - 1) & (i == ni - 1))
        def _():
            store.wait()

    vec = pl.BlockSpec((1, d), lambda i, k: (0, 0))
    return pl.pallas_call(
        body, name="dh_dx", grid=(ni, N_SPLITS),
        out_shape=(jax.ShapeDtypeStruct((s, d), F32), jax.ShapeDtypeStruct((1, d), F32)),
        in_specs=_dz_specs(tm, e) + [pl.BlockSpec((None, d, e), lambda i, k: (k, 0, 0)), ANY, vec, ANY],
        out_specs=(ANY, vec),
        scratch_shapes=[pltpu.VMEM((tm, d), F32), pltpu.VMEM((tm, d), F32), pltpu.VMEM((tm, d), F32),
                        pltpu.SemaphoreType.DMA((3,))],
        compiler_params=_params("arbitrary", "arbitrary"))(dzh, dza, w_full, x, gain, dx2)


def _position():
    x, y, c = lax.axis_index("x"), lax.axis_index("y"), lax.axis_index("c")
    return x, y, c


def _xor_peer(x, y, c, mask):
    return (x ^ ((mask >> 2) & 1), y ^ ((mask >> 1) & 1), c ^ (mask & 1))


def _block_order(masks):
    me = 4 * lax.axis_index("x") + 2 * lax.axis_index("y") + lax.axis_index("c")
    return jnp.stack([me ^ m for m in masks]).astype(jnp.int32)


GATHER_MASKS = (0, 1, 4, 2, 6, 5, 3, 7)


def _inproj_gather(h, w_loc, wo_loc):
    s, d = h.shape
    e = w_loc.shape[1]
    tm = _tile(s, 1024)
    ni = s // tm
    pre = max(ni - 2, 0)

    def body(order_ref, h_ref, w_ref, wo_ref, z_ref, wf_ref, wof_ref, wbuf, send_sems, recv_sems, osend, orecv,
             local_sems, wsems):
        j, i = pl.program_id(0), pl.program_id(1)
        x, y, c = _position()
        me, sibling = (x, y, c), (x, y, 1 - c)
        chips = [(1 - x, y), (x, 1 - y), (1 - x, 1 - y)]
        blk = lambda p: 4 * p[0] + 2 * p[1] + p[2]

        def copy(k, block, to, src=None):
            dst = wf_ref.at[blk(block)]
            return pltpu.make_async_remote_copy(
                src_ref=dst if src is None else src, dst_ref=dst, send_sem=send_sems.at[k], recv_sem=recv_sems.at[k],
                device_id=to, device_id_type=MESH)

        first = [copy(0, me, sibling, src=w_ref)] + [copy(1 + q, me, (*chip, c), src=w_ref) for q, chip in enumerate(chips)]
        passed = [copy(4 + q, (*chip, c), sibling) for q, chip in enumerate(chips)]
        mine = pltpu.make_async_copy(w_ref, wf_ref.at[blk(me)], local_sems.at[0])
        ocopies = [pltpu.make_async_remote_copy(
            src_ref=wo_ref, dst_ref=wof_ref.at[blk(me)], send_sem=osend.at[k], recv_sem=orecv.at[k],
            device_id=_xor_peer(x, y, c, k + 1), device_id_type=MESH) for k in range(N_DEV - 1)]
        omine = pltpu.make_async_copy(wo_ref, wof_ref.at[blk(me)], local_sems.at[1])
        blocks = [me, sibling] + [(*chip, c) for chip in chips] + [(*chip, 1 - c) for chip in chips]
        arrive = [None, copy(0, sibling, me)] + [copy(1 + q, (*chip, c), me) for q, chip in enumerate(chips)] \
            + [copy(4 + q, (*chip, 1 - c), me) for q, chip in enumerate(chips)]
        forward = [None, None] + passed + [None, None, None]

        def load(slot, src):
            return pltpu.make_async_copy(src, wbuf.at[slot], wsems.at[slot])

        @pl.when((j == 0) & (i == 0))
        def _():
            for cp in [mine, omine] + first + ocopies:
                cp.start()
            load(0, w_ref).start()

        for jj in range(N_DEV):
            @pl.when((j == jj) & (i == 0))
            def _():
                load(jj % 2, w_ref).wait()

            if jj + 1 < N_DEV:
                @pl.when((j == jj) & (i == pre))
                def _():
                    arrive[jj + 1].wait_recv()
                    if forward[jj + 1] is not None:
                        forward[jj + 1].start()
                    load((jj + 1) % 2, wf_ref.at[blk(blocks[jj + 1])]).start()

        z_ref[...] = _dot(h_ref[...], wbuf[j % 2])

        @pl.when((j == N_DEV - 1) & (i == ni - 1))
        def _():
            for cp in first + passed:
                cp.wait_send()
            for cp in ocopies:
                cp.wait_send()
                cp.wait_recv()
            mine.wait()
            omine.wait()

    grid_spec = pltpu.PrefetchScalarGridSpec(
        num_scalar_prefetch=1, grid=(N_DEV, ni),
        in_specs=[pl.BlockSpec((tm, d), lambda j, i, o: (i, 0)), ANY, ANY],
        out_specs=(pl.BlockSpec((tm, e), lambda j, i, o: (i, o[j])), ANY, ANY),
        scratch_shapes=[pltpu.VMEM((2, d, e), BF16), pltpu.SemaphoreType.DMA((7,)), pltpu.SemaphoreType.DMA((7,)),
                        pltpu.SemaphoreType.DMA((7,)), pltpu.SemaphoreType.DMA((7,)), pltpu.SemaphoreType.DMA((2,)),
                        pltpu.SemaphoreType.DMA((2,))])
    return pl.pallas_call(
        body, name="inproj_gather", grid_spec=grid_spec,
        out_shape=(jax.ShapeDtypeStruct((s, N_SPLITS * e), F32), jax.ShapeDtypeStruct((N_DEV, d, e), BF16),
                   jax.ShapeDtypeStruct((N_DEV,) + wo_loc.shape, BF16)),
        compiler_params=_params("arbitrary", "arbitrary"))(_block_order(GATHER_MASKS), h, w_loc, wo_loc)


SCATTER_MASKS = (7, 6, 5, 4, 3, 2, 1, 0)
N_CHIPS = 4


def _scatter_block(k, acc, stage, tmp, own_ref, ra_ref, rb_ref, sa_send, sa_recv, sb_send, sb_recv, loc_sem, last):
    x, y, c = _position()
    chip_of = lambda t: _xor_peer(x, y, c, SCATTER_MASKS[2 * t + 1])

    def ship(t):
        return pltpu.make_async_remote_copy(
            src_ref=stage.at[0], dst_ref=ra_ref.at[t], send_sem=sa_send.at[t], recv_sem=sa_recv.at[t],
            device_id=(x, y, 1 - c), device_id_type=MESH)

    def send(t):
        return pltpu.make_async_remote_copy(
            src_ref=stage.at[1], dst_ref=rb_ref.at[t], send_sem=sb_send.at[t], recv_sem=sb_recv.at[t],
            device_id=chip_of(t), device_id_type=MESH)

    for kk in range(N_DEV):
        t = kk // 2

        @pl.when(last & (k == kk))
        def _():
            if kk % 2 == 0:
                if t >= 1:
                    ship(t - 1).wait_send()
                stage[0] = acc[...].astype(BF16)
                ship(t).start()
            else:
                ship(t).wait_recv()
                fetch = pltpu.make_async_copy(ra_ref.at[t], tmp, loc_sem)
                fetch.start()
                fetch.wait()
                acc[...] += tmp[...].astype(F32)
                if t < N_CHIPS - 1:
                    if t >= 1:
                        send(t - 1).wait_send()
                    stage[1] = acc[...].astype(BF16)
                    send(t).start()
                else:
                    keep = pltpu.make_async_copy(acc, own_ref, loc_sem)
                    keep.start()
                    keep.wait()
                    ship(t).wait_send()
                    send(t - 1).wait_send()
                    for q in range(N_CHIPS - 1):
                        send(q).wait_recv()


def _scatter_scratch(rows, cols):
    return [pltpu.VMEM((rows, cols), F32), pltpu.VMEM((2, rows, cols), BF16), pltpu.VMEM((rows, cols), BF16),
            pltpu.SemaphoreType.DMA((N_CHIPS,)), pltpu.SemaphoreType.DMA((N_CHIPS,)),
            pltpu.SemaphoreType.DMA((N_CHIPS - 1,)), pltpu.SemaphoreType.DMA((N_CHIPS - 1,)), pltpu.SemaphoreType.DMA(())]


def _scatter_out(rows, cols):
    return (jax.ShapeDtypeStruct((rows, cols), F32), jax.ShapeDtypeStruct((N_CHIPS, rows, cols), BF16),
            jax.ShapeDtypeStruct((N_CHIPS - 1, rows, cols), BF16))


def _dwin_scatter(h, dzh, dza):
    s, d = h.shape
    e = dzh.shape[2]
    ts = _tile(s, 1024)
    ns = s // ts

    def body(order_ref, dzh_ref, dza_ref, h_ref, own_ref, ra_ref, rb_ref, acc, stage, tmp, *sems):
        k, step = pl.program_id(0), pl.program_id(1)

        @pl.when(step == 0)
        def _():
            acc[...] = jnp.zeros_like(acc)

        def add(dz):
            acc[...] += _dot_tn(h_ref[...], dz)

        _dz_pick(order_ref[k], dzh_ref, dza_ref, add)
        _scatter_block(k, acc, stage, tmp, own_ref, ra_ref, rb_ref, *sems, step == ns - 1)

    def dz_spec(lo):
        return pl.BlockSpec((None, ts, e), lambda k, st, o: (jnp.clip(o[k] - lo, 0, 3), st, 0))

    grid_spec = pltpu.PrefetchScalarGridSpec(
        num_scalar_prefetch=1, grid=(N_DEV, ns),
        in_specs=[dz_spec(0), dz_spec(4), pl.BlockSpec((ts, d), lambda k, st, o: (st, 0))],
        out_specs=(ANY, ANY, ANY), scratch_shapes=_scatter_scratch(d, e))
    own, _, rb = pl.pallas_call(
        body, name="dwin_scatter", grid_spec=grid_spec, out_shape=_scatter_out(d, e),
        compiler_params=_params("arbitrary", "arbitrary"))(_block_order(SCATTER_MASKS), dzh, dza, h)
    return own, rb


def _dwout_scatter(y_h, y_a, dxb):
    s, e = y_h.shape
    d = dxb.shape[1]
    r = 2 * e // N_DEV
    pairs = e // (2 * r)
    ts = _tile(s, 1024)
    ns = s // ts
    chip_masks = SCATTER_MASKS[1::2]
    passes = ((0, 1), (2,), (3,))
    slots = max(len(chips) for chips in passes)
    slot_chip = [chips[min(u, len(chips) - 1)] for chips in passes for u in range(slots)]

    def body(pair_ref, yh0_ref, ya0_ref, yh1_ref, ya1_ref, dx_ref, own_ref, ra_ref, rb_ref, acc, keep_buf, ship_buf,
             send_buf, tmp, sa_send, sa_recv, sb_send, sb_recv, loc_sem):
        p, step = pl.program_id(0), pl.program_id(1)
        x, y, c = _position()

        @pl.when(step == 0)
        def _():
            acc[...] = jnp.zeros_like(acc)

        for u, (yh_ref, ya_ref) in enumerate(((yh0_ref, ya0_ref), (yh1_ref, ya1_ref))):
            rows = slice(u * 2 * r, (u + 1) * 2 * r)
            used = functools.reduce(jnp.logical_or, [p == pp for pp, chips in enumerate(passes) if u < len(chips)])

            @pl.when(used & (pair_ref[slots * p + u] < pairs))
            def _():
                acc[rows, :] += _dot_tn(yh_ref[...], dx_ref[...])

            @pl.when(used & (pair_ref[slots * p + u] >= pairs))
            def _():
                acc[rows, :] += _dot_tn(ya_ref[...], dx_ref[...])

        def block_rows(u, core):
            return pl.ds(pl.multiple_of(u * 2 * r + core * r, r), r)

        slot_of = {q: u for chips in passes for u, q in enumerate(chips)}

        def ship(q):
            return pltpu.make_async_remote_copy(
                src_ref=ship_buf.at[slot_of[q]], dst_ref=ra_ref.at[q], send_sem=sa_send.at[q], recv_sem=sa_recv.at[q],
                device_id=(x, y, 1 - c), device_id_type=MESH)

        def send(q):
            return pltpu.make_async_remote_copy(
                src_ref=send_buf.at[slot_of[q]], dst_ref=rb_ref.at[q], send_sem=sb_send.at[q], recv_sem=sb_recv.at[q],
                device_id=_xor_peer(x, y, c, chip_masks[q]), device_id_type=MESH)

        def sibling_share(q):
            ship(q).wait_recv()
            fetch = pltpu.make_async_copy(ra_ref.at[q], tmp, loc_sem)
            fetch.start()
            fetch.wait()
            return tmp[...].astype(F32)

        shipped, sent = {}, {}
        for pp, chips in enumerate(passes):
            @pl.when((step == ns - 1) & (p == pp))
            def _():
                for u, q in enumerate(chips):
                    if u in shipped:
                        ship(shipped.pop(u)).wait_send()
                    ship_buf[u] = acc[block_rows(u, 1 - c), :].astype(BF16)
                    ship(q).start()
                    shipped[u] = q
                for u, q in enumerate(chips):
                    total = acc[block_rows(u, c), :] + sibling_share(q)
                    if q < N_CHIPS - 1:
                        if u in sent:
                            send(sent.pop(u)).wait_send()
                        send_buf[u] = total.astype(BF16)
                        send(q).start()
                        sent[u] = q
                    else:
                        keep_buf[...] = total
                        keep = pltpu.make_async_copy(keep_buf, own_ref, loc_sem)
                        keep.start()
                        keep.wait()
                if pp == len(passes) - 1:
                    for q in shipped.values():
                        ship(q).wait_send()
                    for q in sent.values():
                        send(q).wait_send()
                    for q in range(N_CHIPS - 1):
                        send(q).wait_recv()

    def y_spec(u, lo):
        return pl.BlockSpec((ts, 2 * r), lambda p, st, o: (st, jnp.clip(o[slots * p + u] - lo, 0, pairs - 1)))

    pair_of_chip = _block_order(chip_masks) // 2
    grid_spec = pltpu.PrefetchScalarGridSpec(
        num_scalar_prefetch=1, grid=(len(passes), ns),
        in_specs=[y_spec(0, 0), y_spec(0, pairs), y_spec(1, 0), y_spec(1, pairs),
                  pl.BlockSpec((ts, d), lambda p, st, o: (st, 0))],
        out_specs=(ANY, ANY, ANY),
        scratch_shapes=[pltpu.VMEM((slots * 2 * r, d), F32), pltpu.VMEM((r, d), F32),
                        pltpu.VMEM((slots, r, d), BF16)] + _scatter_scratch(r, d)[1:])
    own, _, rb = pl.pallas_call(
        body, name="dwout_scatter", grid_spec=grid_spec, out_shape=_scatter_out(r, d),
        compiler_params=_params("arbitrary", "arbitrary"))(
            jnp.stack([pair_of_chip[q] for q in slot_chip]), y_h, y_a, y_h, y_a, dxb)
    return own, rb


def _sum_chips_adamw(own, recv, w, m, v):
    r, c = w.shape
    tr = _tile(r, 128)

    def body(own_ref, rc_ref, w_ref, m_ref, v_ref, g_ref, d_ref, mo_ref, vo_ref):
        g = own_ref[...]
        for q in range(N_CHIPS - 1):
            g = g + rc_ref[q].astype(F32)
        g_ref[...] = g
        d_ref[...], mo_ref[...], vo_ref[...] = _adamw(w_ref[...], g, m_ref[...], v_ref[...])

    blk = pl.BlockSpec((tr, c), lambda i: (i, 0))
    shp = jax.ShapeDtypeStruct((r, c), F32)
    return pl.pallas_call(
        body, name="sum_chips_adamw", grid=(r // tr,), out_shape=(shp, shp, shp, shp),
        in_specs=[blk, pl.BlockSpec((N_CHIPS - 1, tr, c), lambda i: (0, i, 0)), blk, blk, blk],
        out_specs=(blk, blk, blk, blk), compiler_params=_params("parallel"))(own, recv, w, m, v)


SMALL_ROWS = 8
ROW_LB = 4
ROW_GN = 6
ROW_LOSS = 7


def _small_allreduce_adamw(part, w, m, v, lb_logits):
    width = part.shape[1]

    def body(p_ref, w_ref, m_ref, v_ref, lb_ref, g_ref, d_ref, mo_ref, vo_ref, buf, send_sems, recv_sems):
        x, y, c = _position()
        me = 4 * x + 2 * y + c
        buf[me] = p_ref[...]
        copies = []
        for k in range(N_DEV - 1):
            bx, by, bc = ((k + 1) >> 2) & 1, ((k + 1) >> 1) & 1, (k + 1) & 1
            peer = (x ^ bx, y ^ by, c ^ bc)
            copies.append(pltpu.make_async_remote_copy(
                src_ref=p_ref, dst_ref=buf.at[me], send_sem=send_sems.at[k], recv_sem=recv_sems.at[k],
                device_id=peer, device_id_type=MESH))
        for cp in copies:
            cp.start()
        for cp in copies:
            cp.wait_recv()
        for cp in copies:
            cp.wait_send()
        tot = buf[0]
        for dev in range(1, N_DEV):
            tot = tot + buf[dev]
        lbv = lb_ref[...]
        lb = _sigmoid(lbv[0:1] - lbv[1:2])
        glb = tot[ROW_LB:ROW_LB + 1] * lb * (1.0 - lb)
        loss = jnp.sum(tot[ROW_LOSS:ROW_LOSS + 1], axis=-1, keepdims=True)
        row = lax.broadcasted_iota(jnp.int32, (SMALL_ROWS, width), 0)
        g = jnp.where(row == ROW_LB, glb, jnp.where(row == ROW_LB + 1, -glb, tot))
        g = jnp.where(row == ROW_LOSS, loss, g)
        g_ref[...] = g
        d_ref[...], mo_ref[...], vo_ref[...] = _adamw(w_ref[...], g, m_ref[...], v_ref[...])

    vm = pl.BlockSpec(memory_space=pltpu.VMEM)
    shp = jax.ShapeDtypeStruct((SMALL_ROWS, width), F32)
    return pl.pallas_call(
        body, name="small_allreduce_adamw", out_shape=(shp, shp, shp, shp),
        in_specs=[vm] * 5, out_specs=(vm, vm, vm, vm),
        scratch_shapes=[pltpu.VMEM((N_DEV, SMALL_ROWS, width), F32), pltpu.SemaphoreType.DMA((N_DEV - 1,)),
                        pltpu.SemaphoreType.DMA((N_DEV - 1,))],
    )(part, w, m, v, lb_logits)


def _pack_small(norm_gain, final_gain, lb2, gnorm, last_row, width):
    pad = lambda a: jnp.pad(a.reshape(1, -1), ((0, 0), (0, width - a.size)))
    return jnp.concatenate([norm_gain.reshape(2, width), final_gain.reshape(2, width), lb2.reshape(2, width),
                            pad(gnorm), last_row.reshape(1, width)], axis=0)


def _unpack_small(p, d, e, hd):
    return (p[0:2].reshape(1, d), p[2:4].reshape(d), p[4:6].reshape(2, e), p[6:7, :hd].reshape(1, hd))


def kernel(x, norm_gain, w_in, lb_logits, hgrn_gnorm, w_out, final_gain, loss_target, m_norm_gain, m_w_in, m_lb_logits, m_hgrn_gnorm, m_w_out, m_final_gain, v_norm_gain, v_w_in, v_lb_logits, v_hgrn_gnorm, v_w_out, v_final_gain):
    s, d = x.shape[1], x.shape[2]
    e = w_in.shape[2]
    assert d == 2 * e and lb_logits.shape == (2, e) and w_out.shape[1] * N_DEV == 2 * e
    x2d = x.reshape(s, d)
    tgt = loss_target.reshape(s, d)

    h = _rmsnorm_fwd(x2d, norm_gain)
    z, w_in_full, w_out_full = _inproj_gather(h, _cast_bf16(w_in[0]), _cast_bf16(w_out[0]))
    w_out_full = w_out_full.reshape(2 * e, d)
    y_h, states = _hgrn_fwd(z, lb_logits, hgrn_gnorm)
    o_attn, lse, y_a = _attn_fwd(z)
    dx2, dx2b, dy, loss_vec, dfg = _outproj_loss(x2d, y_h, y_a, w_out_full, final_gain.reshape(1, d), tgt)

    own_o, recv_o = _dwout_scatter(y_h, y_a, dx2b)
    dza = _attn_bwd(z, dy, o_attn, lse)
    dzh, dlb, dgn = _hgrn_bwd(z, dy, states, lb_logits, hgrn_gnorm)
    grad_x, dng = _dh_dx(dzh, dza, w_in_full, x2d, norm_gain, dx2)
    g_wo, d_wo, nm_wo, nv_wo = _sum_chips_adamw(own_o, recv_o, w_out[0], m_w_out[0], v_w_out[0])

    width = d // 2
    zero_row = jnp.zeros((1, width), F32)
    loss_row = loss_vec[:, :width] + loss_vec[:, width:]
    part = _pack_small(dng, dfg, jnp.concatenate([dlb, zero_row], axis=0), dgn, loss_row, width)
    pw = _pack_small(norm_gain, final_gain, lb_logits, hgrn_gnorm, zero_row, width)
    pm = _pack_small(m_norm_gain, m_final_gain, m_lb_logits, m_hgrn_gnorm, zero_row, width)
    pv = _pack_small(v_norm_gain, v_final_gain, v_lb_logits, v_hgrn_gnorm, zero_row, width)
    sg, sd, sm, sv = _small_allreduce_adamw(part, pw, pm, pv, lb_logits)
    own_i, recv_i = _dwin_scatter(h, dzh, dza)
    g_wi, d_wi, nm_wi, nv_wi = _sum_chips_adamw(own_i, recv_i, w_in[0], m_w_in[0], v_w_in[0])
    hd = hgrn_gnorm.shape[1]
    g_ng, g_fg, g_lb, g_gn = _unpack_small(sg, d, e, hd)
    d_ng, d_fg, d_lb, d_gn = _unpack_small(sd, d, e, hd)
    m_ng, m_fg, m_lb, m_gn = _unpack_small(sm, d, e, hd)
    v_ng, v_fg, v_lb, v_gn = _unpack_small(sv, d, e, hd)
    loss = sg[ROW_LOSS, 0]

    one = lambda a: a[None]
    return (loss, grad_x.reshape(1, s, d), g_ng, one(g_wi), g_lb, g_gn, one(g_wo), g_fg,
            d_ng, one(d_wi), d_lb, d_gn, one(d_wo), d_fg,
            m_ng, one(nm_wi), m_lb, m_gn, one(nm_wo), m_fg,
            v_ng, one(nv_wi), v_lb, v_gn, one(nv_wo), v_fg)
```

```python
import functools
import math

import jax
import jax.numpy as jnp
from jax import lax
from jax.experimental import pallas as pl
from jax.experimental.pallas import tpu as pltpu

NORM_EPS = 1e-6
HGRN_HEAD = 128
HGRN_CHUNK = 64
ATTN_HEAD = 64
ATTN_BAND = 128
DILATIONS = (1, 4, 16)
N_SPLITS = 8
N_DEV = 8
ADAM_LR = 0.001
ADAM_B1 = 0.9
ADAM_B2 = 0.999
ADAM_EPS = 1e-08
ADAM_WD = 0.01
ADAM_STEP = 10
LANES = 128
MESH = pl.DeviceIdType.MESH
F32 = jnp.float32
BF16 = jnp.bfloat16
NEG_BIG = -1e30
VMEM_LIMIT = 56 * 1024 * 1024

ANY = pl.BlockSpec(memory_space=pl.ANY)


def _params(*sem):
    return pltpu.CompilerParams(dimension_semantics=sem, vmem_limit_bytes=VMEM_LIMIT)


def _tile(n, pref):
    t = min(n, pref)
    assert n % t == 0, (n, pref)
    return t


def _dot(a, b, precision=None):
    return jnp.dot(a, b, preferred_element_type=F32, precision=precision)


def _dot_nt(a, b):
    return lax.dot_general(a, b, (((1,), (1,)), ((), ())), preferred_element_type=F32)


def _dot_tn(a, b):
    return lax.dot_general(a, b, (((0,), (0,)), ((), ())), preferred_element_type=F32)


def _sigmoid(x):
    return 0.5 * jnp.tanh(0.5 * x) + 0.5


def _dsilu(x, s):
    return s * (1.0 + x * (1.0 - s))


def _adamw(w, g, m, v):
    m = ADAM_B1 * m + (1.0 - ADAM_B1) * g
    v = ADAM_B2 * v + (1.0 - ADAM_B2) * (g * g)
    m_hat = m / (1.0 - ADAM_B1 ** ADAM_STEP)
    v_hat = v / (1.0 - ADAM_B2 ** ADAM_STEP)
    delta = -ADAM_LR * (m_hat / (jnp.sqrt(v_hat) + ADAM_EPS) + ADAM_WD * w)
    return delta, m, v


def _cast_bf16(a):
    r, c = a.shape
    tr = _tile(r, 256)

    def body(a_ref, o_ref):
        o_ref[...] = a_ref[...].astype(BF16)

    return pl.pallas_call(
        body, name="cast_bf16", grid=(r // tr,), out_shape=jax.ShapeDtypeStruct((r, c), BF16),
        in_specs=[pl.BlockSpec((tr, c), lambda i: (i, 0))], out_specs=pl.BlockSpec((tr, c), lambda i: (i, 0)),
        compiler_params=_params("parallel"))(a)


def _rmsnorm_fwd(x, gain):
    s, d = x.shape
    tm = _tile(s, 512)

    def body(x_ref, g_ref, h_ref):
        xv = x_ref[...]
        r = lax.rsqrt(jnp.mean(xv * xv, axis=-1, keepdims=True) + NORM_EPS)
        h_ref[...] = (xv * r * g_ref[...]).astype(BF16)

    return pl.pallas_call(
        body, name="rmsnorm_fwd", grid=(s // tm,), out_shape=jax.ShapeDtypeStruct((s, d), BF16),
        in_specs=[pl.BlockSpec((tm, d), lambda i: (i, 0)), pl.BlockSpec((1, d), lambda i: (0, 0))],
        out_specs=pl.BlockSpec((tm, d), lambda i: (i, 0)), compiler_params=_params("parallel"))(x, gain)


HGRN_BLOCK = 2048
TRI_ROWS = 256


def _chunk_masks():
    tb = TRI_ROWS
    row = lax.broadcasted_iota(jnp.int32, (tb, tb), 0)
    col = lax.broadcasted_iota(jnp.int32, (tb, tb), 1)
    same = (row // HGRN_CHUNK) == (col // HGRN_CHUNK)
    lower = jnp.where(same & (col <= row), 1.0, 0.0).astype(BF16)
    upper = jnp.where(same & (col >= row), 1.0, 0.0).astype(BF16)
    return lower, upper


def _split3(a):
    hi = a.astype(BF16).astype(F32)
    mid = (a - hi).astype(BF16).astype(F32)
    lo = (a - hi - mid).astype(BF16).astype(F32)
    return hi, mid, lo


def _tri_dot(tri, x):
    hi, mid, lo = (p.astype(BF16) for p in _split3(x))
    outs = []
    for r in range(0, x.shape[0], TRI_ROWS):
        sl = slice(r, r + TRI_ROWS)
        outs.append(_dot(tri, hi[sl]) + _dot(tri, mid[sl]) + _dot(tri, lo[sl]))
    return outs[0] if len(outs) == 1 else jnp.concatenate(outs, axis=0)


def _hgrn_gates(qp, fp, lbv):
    lb = _sigmoid(lbv[0:1] - lbv[1:2])
    sq = _sigmoid(qp)
    q = qp * sq
    sg = _sigmoid(fp)
    f = lb + (1.0 - lb) * sg
    k = 1.0 - f
    return lb, sq, q, sg, f, k


def _hgrn_fwd(z, lb_logits, gnorm):
    s = z.shape[0]
    e = z.shape[1] // N_SPLITS
    nh = e // HGRN_HEAD
    tb = _tile(s, HGRN_BLOCK)
    nc = tb // HGRN_CHUNK
    nb = s // tb
    C = HGRN_CHUNK

    def body(q_ref, f_ref, i_ref, g_ref, lb_ref, gn_ref, y_ref, st_ref, state, o_scr):
        @pl.when(pl.program_id(1) == 0)
        def _():
            state[...] = jnp.zeros_like(state)

        lb, sq, q, sg, f, k = _hgrn_gates(q_ref[...], f_ref[...], lb_ref[...])
        lower, _ = _chunk_masks()
        b = _tri_dot(lower, jnp.log(f))
        b3 = b.reshape(nc, C, HGRN_HEAD)
        bc = b3[:, C - 1:C, :]
        qt = (q * jnp.exp(b)).astype(BF16)
        kt = (k * jnp.exp(-b)).astype(BF16)
        ke = (k.reshape(nc, C, HGRN_HEAD) * jnp.exp(bc - b3)).reshape(tb, HGRN_HEAD).astype(BF16)
        v = i_ref[...].astype(BF16)
        tri = lax.broadcasted_iota(jnp.int32, (C, C), 1) <= lax.broadcasted_iota(jnp.int32, (C, C), 0)
        sls = [slice(c * C, (c + 1) * C) for c in range(nc)]
        kv = [_dot_tn(v[sl], ke[sl]) for sl in sls]
        a = [jnp.where(tri, _dot_nt(qt[sl], kt[sl]), 0.0).astype(BF16) for sl in sls]
        st = state[...]
        sts = []
        for c in range(nc):
            sts.append(st)
            st_ref[c] = st
            st = st * jnp.exp(bc[c]) + kv[c]
        state[...] = st
        for c, sl in enumerate(sls):
            o_scr[sl, :] = _dot(a[c], v[sl]) + _dot_nt(qt[sl], sts[c].astype(BF16))
        o = o_scr[...]
        rms = lax.rsqrt(jnp.mean(o * o, axis=-1, keepdims=True) + NORM_EPS)
        gp = g_ref[...]
        y_ref[...] = (o * rms * gn_ref[...] * (gp * _sigmoid(gp))).astype(BF16)

    col = lambda kk: (lambda h, n: (n, kk * nh + h))
    return pl.pallas_call(
        body, name="hgrn_fwd", grid=(nh, nb),
        out_shape=(jax.ShapeDtypeStruct((s, e), BF16),
                   jax.ShapeDtypeStruct((nh, s // C, HGRN_HEAD, HGRN_HEAD), F32)),
        in_specs=[pl.BlockSpec((tb, HGRN_HEAD), col(0)), pl.BlockSpec((tb, HGRN_HEAD), col(1)),
                  pl.BlockSpec((tb, HGRN_HEAD), col(2)), pl.BlockSpec((tb, HGRN_HEAD), col(3)),
                  pl.BlockSpec((2, HGRN_HEAD), lambda h, n: (0, h)), pl.BlockSpec((1, HGRN_HEAD), lambda h, n: (0, 0))],
        out_specs=(pl.BlockSpec((tb, HGRN_HEAD), lambda h, n: (n, h)),
                   pl.BlockSpec((None, nc, HGRN_HEAD, HGRN_HEAD), lambda h, n: (h, n, 0, 0))),
        scratch_shapes=[pltpu.VMEM((HGRN_HEAD, HGRN_HEAD), F32), pltpu.VMEM((tb, HGRN_HEAD), F32)],
        compiler_params=_params("parallel", "arbitrary"))(z, z, z, z, lb_logits, gnorm)


def _hgrn_bwd(z, dy, states, lb_logits, gnorm):
    s = z.shape[0]
    e = z.shape[1] // N_SPLITS
    nh = e // HGRN_HEAD
    tb = _tile(s, HGRN_BLOCK)
    nc = tb // HGRN_CHUNK
    nb = s // tb
    C = HGRN_CHUNK
    H = HGRN_HEAD

    def body(q_ref, f_ref, i_ref, g_ref, dy_ref, st_ref, lb_ref, gn_ref, dz_ref, dlb_ref, dgn_ref,
             gstate, o_scr, dq_scr, dk_scr, dv_scr, e_scr):
        first = (pl.program_id(0) == 0) & (pl.program_id(1) == 0)

        @pl.when(first)
        def _():
            dgn_ref[...] = jnp.zeros_like(dgn_ref)

        @pl.when(pl.program_id(1) == 0)
        def _():
            gstate[...] = jnp.zeros_like(gstate)
            dlb_ref[...] = jnp.zeros_like(dlb_ref)

        qp = q_ref[...]
        lb, sq, q, sg, f, k = _hgrn_gates(qp, f_ref[...], lb_ref[...])
        lower, upper = _chunk_masks()
        b = _tri_dot(lower, jnp.log(f))
        b3 = b.reshape(nc, C, H)
        bc = b3[:, C - 1:C, :]
        eb = jnp.exp(b)
        enb = jnp.exp(-b)
        eend = jnp.exp(bc - b3).reshape(tb, H)
        qt = (q * eb).astype(BF16)
        kt = (k * enb).astype(BF16)
        ke = (k * eend).astype(BF16)
        v = i_ref[...].astype(BF16)
        tri = lax.broadcasted_iota(jnp.int32, (C, C), 1) <= lax.broadcasted_iota(jnp.int32, (C, C), 0)
        sls = [slice(c * C, (c + 1) * C) for c in range(nc)]
        a = [jnp.where(tri, _dot_nt(qt[sl], kt[sl]), 0.0).astype(BF16) for sl in sls]
        for c, sl in enumerate(sls):
            o_scr[sl, :] = _dot(a[c], v[sl]) + _dot_nt(qt[sl], st_ref[c].astype(BF16))
        o = o_scr[...]
        rms = lax.rsqrt(jnp.mean(o * o, axis=-1, keepdims=True) + NORM_EPS)
        on = o * rms
        gn = gn_ref[...]
        gp = g_ref[...]
        sgg = _sigmoid(gp)
        dyv = dy_ref[...]
        d_on = dyv * (gp * sgg)
        dz_ref[3] = (dyv * on * gn * _dsilu(gp, sgg)).astype(BF16)
        dgn_ref[...] += jnp.sum(d_on * on, axis=0, keepdims=True)
        u = d_on * gn
        do = (rms * (u - on * jnp.mean(u * on, axis=-1, keepdims=True))).astype(BF16)
        gup = [_dot_tn(do[sl], qt[sl]) for sl in sls]
        da = [jnp.where(tri, _dot_nt(do[sl], v[sl]), 0.0).astype(BF16) for sl in sls]
        gt = gstate[...]
        gts = [None] * nc
        for c in reversed(range(nc)):
            gts[c] = gt
            gt = gt * jnp.exp(bc[c]) + gup[c]
        gstate[...] = gt
        for c, sl in enumerate(sls):
            stp = st_ref[c]
            gtb = gts[c].astype(BF16)
            dqt = _dot(da[c], kt[sl]) + _dot(do[sl], stp.astype(BF16))
            dkt = _dot_tn(da[c], qt[sl])
            dks = _dot(v[sl], gtb) * eend[sl]
            dv_scr[sl, :] = _dot_tn(a[c], do[sl]) + _dot_nt(ke[sl], gtb)
            dq_scr[sl, :] = dqt * eb[sl]
            dk_scr[sl, :] = dkt * enb[sl] + dks
            ech = (jnp.sum(k[sl] * dks, axis=0, keepdims=True)
                   + jnp.sum(gts[c] * jnp.exp(bc[c]) * stp, axis=0, keepdims=True))
            e_scr[sl, :] = jnp.broadcast_to(ech, (C, H))
        dq = dq_scr[...]
        dk = dk_scr[...]
        dlf = _tri_dot(upper, q * dq - k * dk) + e_scr[...]
        dft = dlf / f - dk
        dz_ref[0] = (dq * _dsilu(qp, sq)).astype(BF16)
        dz_ref[1] = (dft * (1.0 - lb) * sg * (1.0 - sg)).astype(BF16)
        dz_ref[2] = dv_scr[...].astype(BF16)
        dlb_ref[...] += jnp.sum(dft * (1.0 - sg), axis=0, keepdims=True)

    col = lambda kk: (lambda h, n: (nb - 1 - n, kk * nh + h))
    return pl.pallas_call(
        body, name="hgrn_bwd", grid=(nh, nb),
        out_shape=(jax.ShapeDtypeStruct((4, s, e), BF16), jax.ShapeDtypeStruct((1, e), F32),
                   jax.ShapeDtypeStruct((1, H), F32)),
        in_specs=[pl.BlockSpec((tb, H), col(0)), pl.BlockSpec((tb, H), col(1)),
                  pl.BlockSpec((tb, H), col(2)), pl.BlockSpec((tb, H), col(3)),
                  pl.BlockSpec((tb, H), lambda h, n: (nb - 1 - n, h)),
                  pl.BlockSpec((None, nc, H, H), lambda h, n: (h, nb - 1 - n, 0, 0)),
                  pl.BlockSpec((2, H), lambda h, n: (0, h)), pl.BlockSpec((1, H), lambda h, n: (0, 0))],
        out_specs=(pl.BlockSpec((4, tb, H), lambda h, n: (0, nb - 1 - n, h)),
                   pl.BlockSpec((1, H), lambda h, n: (0, h)), pl.BlockSpec((1, H), lambda h, n: (0, 0))),
        scratch_shapes=[pltpu.VMEM((H, H), F32)] + [pltpu.VMEM((tb, H), F32)] * 5,
        compiler_params=_params("arbitrary", "arbitrary"))(z, z, z, z, dy, states, lb_logits, gnorm)


ATTN_T = 16 * ATTN_BAND
SCALE = ATTN_HEAD ** -0.5
TILE_UNROLL = 2


def _slope(hh, nheads):
    head = (2 * pl.program_id(0) + hh + 1).astype(F32)
    return jnp.exp(jnp.full((1, 1), -8.0 / nheads * math.log(2.0), F32) * head)


def _fill_bias(bias, nheads, delta, edge_ok):
    band = (delta >= 0) & (delta <= ATTN_BAND)
    dist = delta.astype(F32)
    for pi, dil in enumerate(DILATIONS):
        for hh in range(2):
            full = jnp.where(band, -(_slope(hh, nheads) * float(dil)) * dist, NEG_BIG)
            bias[(pi * 2 + hh) * 2] = full
            bias[(pi * 2 + hh) * 2 + 1] = jnp.where(edge_ok, full, NEG_BIG)


def _rows(start, size, stride):
    if stride == 1:
        return pl.ds(pl.multiple_of(start, ATTN_BAND), size)
    return pl.ds(start, size, stride=stride)


def _head_lanes(rows, hh):
    return (lax.broadcasted_iota(jnp.int32, (rows, LANES), 1) // ATTN_HEAD) == hh


def _attn_fwd(z):
    s = z.shape[0]
    e = z.shape[1] // N_SPLITS
    npair = e // LANES
    T = ATTN_T
    assert s % T == 0
    nsb = s // T
    W = ATTN_BAND
    nt = T // W
    HD = ATTN_HEAD
    chunk = 256

    def body(q_ref, kp_ref, kc_ref, vp_ref, vc_ref, g_ref, o_ref, l_ref, y_ref, qa, kbuf, va, bias, accs, ms, lsw):
        sb = pl.program_id(1)
        def stage(i, carry):
            rows = pl.ds(pl.multiple_of(i * chunk, chunk), chunk)
            upper = pl.ds(pl.multiple_of(T + i * chunk, chunk), chunk)
            kbuf[upper, :] = kc_ref[rows, :]
            for hh in range(2):
                mine = _head_lanes(chunk, hh)
                qa[hh, rows, :] = jnp.where(mine, q_ref[rows, :] * SCALE, 0.0)
                va[hh, upper, :] = jnp.where(mine, vc_ref[rows, :], 1.0)
            return carry

        lax.fori_loop(0, T // chunk, stage, 0)

        @pl.when(sb == 0)
        def _():
            def stage_prev(i, carry):
                rows = pl.ds(pl.multiple_of(i * chunk, chunk), chunk)
                kbuf[rows, :] = kp_ref[rows, :]
                for hh in range(2):
                    va[hh, rows, :] = jnp.where(_head_lanes(chunk, hh), vp_ref[rows, :], 1.0)
                return carry

            lax.fori_loop(0, T // chunk, stage_prev, 0)
        qi = lax.broadcasted_iota(jnp.int32, (W, 2 * W), 0)
        kj = lax.broadcasted_iota(jnp.int32, (W, 2 * W), 1)
        _fill_bias(bias, 2 * npair, W + qi - kj, kj >= W)

        def tile(tau, carry):
            first = _head_lanes(W, 0)
            rows, scores = [], []
            for pi, dil in enumerate(DILATIONS):
                r = tau % dil
                ub = tau // dil
                qrows = _rows(r + dil * W * ub, W, dil)
                krows = _rows(T + dil * W * (ub - 1) + r, 2 * W, dil)
                var = jnp.where((sb == 0) & (ub == 0), 1, 0)
                kt = kbuf[krows, :].astype(BF16)
                rows.append((qrows, krows))
                scores.append([_dot_nt(qa[hh, qrows, :].astype(BF16), kt) + bias[(pi * 2 + hh) * 2 + var]
                               for hh in range(2)])
            maxes = [[jnp.max(sc, axis=-1, keepdims=True) for sc in pair] for pair in scores]
            probs = [[jnp.exp(sc - m).astype(BF16) for sc, m in zip(ps, pm)] for ps, pm in zip(scores, maxes)]
            for pi, (qrows, krows) in enumerate(rows):
                outs = [_dot(probs[pi][hh], va[hh, krows, :].astype(BF16)) for hh in range(2)]
                accs[pi, qrows, :] = jnp.where(first, outs[0], outs[1])
                lsw[pi, qrows, :] = jnp.where(first, outs[1], outs[0])
                ms[pi, qrows, :] = jnp.where(first, maxes[pi][0], maxes[pi][1])
            return carry

        lax.fori_loop(0, nt, tile, 0, unroll=TILE_UNROLL)

        def merge(i, carry):
            rows = pl.ds(pl.multiple_of(i * chunk, chunk), chunk)
            m1, m2, m3 = ms[0, rows, :], ms[1, rows, :], ms[2, rows, :]
            mx = jnp.maximum(jnp.maximum(m1, m2), m3)
            w1, w2, w3 = jnp.exp(m1 - mx), jnp.exp(m2 - mx), jnp.exp(m3 - mx)
            unswap = lambda a: pltpu.roll(a, ATTN_HEAD, 1)
            den = w1 * unswap(lsw[0, rows, :]) + w2 * unswap(lsw[1, rows, :]) + w3 * unswap(lsw[2, rows, :])
            o = (w1 * accs[0, rows, :] + w2 * accs[1, rows, :] + w3 * accs[2, rows, :]) / den
            o_ref[rows, :] = o
            l_ref[rows, :] = mx + jnp.log(den)
            gp = g_ref[rows, :]
            y_ref[rows, :] = (o * (gp * _sigmoid(gp))).astype(BF16)
            upper = pl.ds(pl.multiple_of(T + i * chunk, chunk), chunk)
            kbuf[rows, :] = kbuf[upper, :]
            for hh in range(2):
                va[hh, rows, :] = va[hh, upper, :]
            return carry

        lax.fori_loop(0, T // chunk, merge, 0)

    cur = lambda split: (lambda hp, sb: (sb, split * npair + hp))
    prev = lambda split: (lambda hp, sb: (0, split * npair + hp))
    blk = lambda index: pl.BlockSpec((T, LANES), index)
    out = blk(lambda hp, sb: (sb, hp))
    buf = lambda rows: pltpu.VMEM((rows, LANES), F32)
    return pl.pallas_call(
        body, name="attn_fwd", grid=(npair, nsb),
        out_shape=(jax.ShapeDtypeStruct((s, e), F32), jax.ShapeDtypeStruct((s, e), F32), jax.ShapeDtypeStruct((s, e), BF16)),
        in_specs=[blk(cur(4)), blk(prev(5)), blk(cur(5)), blk(prev(6)), blk(cur(6)), blk(cur(7))],
        out_specs=(out, out, out),
        scratch_shapes=[pltpu.VMEM((2, T, LANES), F32), buf(2 * T), pltpu.VMEM((2, 2 * T, LANES), F32),
                        pltpu.VMEM((12, W, 2 * W), F32)] + [pltpu.VMEM((3, T, LANES), F32)] * 3,
        compiler_params=_params("parallel", "arbitrary"))(z, z, z, z, z, z)


def _outproj_loss(x, y_h, y_a, w_out_full, final_gain, target):
    s, d = x.shape
    e = y_h.shape[1]
    tm = _tile(s, 256)

    def body(x_ref, yh_ref, ya_ref, w_ref, g_ref, t_ref, dx_ref, dxb_ref, dy_ref, loss_ref, dg_ref):
        @pl.when(pl.program_id(0) == 0)
        def _():
            loss_ref[...] = jnp.zeros_like(loss_ref)
            dg_ref[...] = jnp.zeros_like(dg_ref)

        w = w_ref[...]
        x2 = x_ref[...] + _dot(yh_ref[...], w[0:e]) + _dot(ya_ref[...], w[e:2 * e])
        r = lax.rsqrt(jnp.mean(x2 * x2, axis=-1, keepdims=True) + NORM_EPS)
        xn = x2 * r
        g = g_ref[...]
        err = xn * g - t_ref[...]
        loss_ref[...] += jnp.sum(err * err, axis=0, keepdims=True) * (0.5 / d)
        dyo = err * (1.0 / d)
        dg_ref[...] += jnp.sum(dyo * xn, axis=0, keepdims=True)
        u = dyo * g
        dx2 = r * (u - xn * jnp.mean(u * xn, axis=-1, keepdims=True))
        dx_ref[...] = dx2
        dxb = dx2.astype(BF16)
        dxb_ref[...] = dxb
        dy_ref[...] = _dot_nt(dxb, w)

    row = pl.BlockSpec((tm, d), lambda i: (i, 0))
    half = pl.BlockSpec((tm, e), lambda i: (i, 0))
    vec = pl.BlockSpec((1, d), lambda i: (0, 0))
    return pl.pallas_call(
        body, name="outproj_loss", grid=(s // tm,),
        out_shape=(jax.ShapeDtypeStruct((s, d), F32), jax.ShapeDtypeStruct((s, d), BF16),
                   jax.ShapeDtypeStruct((s, 2 * e), F32), jax.ShapeDtypeStruct((1, d), F32),
                   jax.ShapeDtypeStruct((1, d), F32)),
        in_specs=[row, half, half, pl.BlockSpec((2 * e, d), lambda i: (0, 0)), vec, row],
        out_specs=(row, row, pl.BlockSpec((tm, 2 * e), lambda i: (i, 0)), vec, vec),
        compiler_params=_params("arbitrary"))(x, y_h, y_a, w_out_full, final_gain, target)


def _attn_bwd(z, dy, o, lse):
    s, e = o.shape
    npair = e // LANES
    T = ATTN_T
    assert s % T == 0
    nsb = s // T
    W = ATTN_BAND
    nt = T // W
    HD = ATTN_HEAD
    chunk = 256

    def body(k_ref, v_ref, qc_ref, qn_ref, dyc_ref, dyn_ref, gc_ref, gn_ref, oc_ref, on_ref, lc_ref, ln_ref,
             dz_ref, qa, doa, ka, va, dqacc, dkacc, dvacc, bias):
        sb = pl.program_id(1)
        def stage_queries(half, q_r, dy_r, g_r, o_r, l_r):
            def stage(i, carry):
                rows = pl.ds(pl.multiple_of(i * chunk, chunk), chunk)
                dst = pl.ds(pl.multiple_of(half * T + i * chunk, chunk), chunk)
                lane = lax.broadcasted_iota(jnp.int32, (chunk, LANES), 1)
                gp = g_r[rows, :]
                dov = dy_r[rows, :] * (gp * _sigmoid(gp))
                qv = q_r[rows, :] * SCALE
                same_head = (lax.broadcasted_iota(jnp.int32, (LANES, LANES), 0) // HD
                             == lax.broadcasted_iota(jnp.int32, (LANES, LANES), 1) // HD)
                ones = jnp.where(same_head, 1.0, 0.0).astype(BF16)
                hi, mid, lo = (p.astype(BF16) for p in _split3(dov * o_r[rows, :]))
                delta = _dot(hi, ones) + _dot(mid, ones) + _dot(lo, ones)
                swap = lambda a: pltpu.roll(a, HD, 1)
                lse_parts = [swap(p) for p in _split3(l_r[rows, :])]
                dl_parts = [swap(p) for p in _split3(delta)]
                for hh in range(2):
                    mine = _head_lanes(chunk, hh)
                    spare = (1 - hh) * HD
                    qh = jnp.where(mine, qv, 0.0)
                    dh = jnp.where(mine, dov, 0.0)
                    for j in range(3):
                        qh = jnp.where(lane == spare + j, lse_parts[j], qh)
                        dh = jnp.where(lane == spare + j, dl_parts[j], dh)
                    qa[hh, dst, :] = qh
                    doa[hh, dst, :] = dh
                return carry

            lax.fori_loop(0, T // chunk, stage, 0)

        @pl.when(sb == 0)
        def _():
            stage_queries(0, qc_ref, dyc_ref, gc_ref, oc_ref, lc_ref)

        stage_queries(1, qn_ref, dyn_ref, gn_ref, on_ref, ln_ref)

        def stage_keys(i, carry):
            rows = pl.ds(pl.multiple_of(i * chunk, chunk), chunk)
            lane = lax.broadcasted_iota(jnp.int32, (chunk, LANES), 1)
            for hh in range(2):
                spare = (1 - hh) * HD
                minus = (lane >= spare) & (lane < spare + 3)
                ka[hh, rows, :] = jnp.where(minus, -1.0, k_ref[rows, :])
                va[hh, rows, :] = jnp.where(minus, -1.0, v_ref[rows, :])
            gp = gc_ref[rows, :]
            dz_ref[3, rows, :] = (dyc_ref[rows, :] * oc_ref[rows, :] * _dsilu(gp, _sigmoid(gp))).astype(BF16)
            return carry

        lax.fori_loop(0, T // chunk, stage_keys, 0)

        @pl.when(sb == 0)
        def _():
            dqacc[0:T, :] = jnp.zeros((T, LANES), F32)

        dqacc[T:, :] = jnp.zeros((T, LANES), F32)
        dkacc[...] = jnp.zeros_like(dkacc)
        dvacc[...] = jnp.zeros_like(dvacc)
        qi = lax.broadcasted_iota(jnp.int32, (2 * W, W), 0)
        kj = lax.broadcasted_iota(jnp.int32, (2 * W, W), 1)
        _fill_bias(bias, 2 * npair, qi - kj, qi < W)

        def tile(tau, carry):
            def scores(step, pi):
                dil = DILATIONS[pi]
                r = step % dil
                ub = step // dil
                start = r + dil * W * ub
                krows = _rows(start, W, dil)
                qrows = _rows(start, 2 * W, dil)
                var = jnp.where((sb == nsb - 1) & (ub == nt // dil - 1), 1, 0)
                unit = dict(krows=krows, qrows=qrows, ops=[], sc=[], dpd=[])
                for hh in range(2):
                    kt = ka[hh, krows, :].astype(BF16)
                    vt = va[hh, krows, :].astype(BF16)
                    qt = qa[hh, qrows, :].astype(BF16)
                    dt = doa[hh, qrows, :].astype(BF16)
                    unit["ops"].append((kt, qt, dt))
                    unit["sc"].append(_dot_nt(qt, kt) + bias[(pi * 2 + hh) * 2 + var])
                    unit["dpd"].append(_dot_nt(dt, vt))
                return unit

            def elementwise(unit):
                ps = [jnp.exp(s_) for s_ in unit["sc"]]
                unit["ds"] = [(p * d).astype(BF16) for p, d in zip(ps, unit["dpd"])]
                unit["pb"] = [p.astype(BF16) for p in ps]

            def products(unit):
                dvs = [_dot_tn(pb, dt) for pb, (kt, qt, dt) in zip(unit["pb"], unit["ops"])]
                dks = [_dot_tn(ds, qt) for ds, (kt, qt, dt) in zip(unit["ds"], unit["ops"])]
                dqs = [_dot(ds, kt) for ds, (kt, qt, dt) in zip(unit["ds"], unit["ops"])]
                dkacc[unit["krows"], :] += jnp.where(_head_lanes(W, 0), dks[0], dks[1])
                dvacc[unit["krows"], :] += jnp.where(_head_lanes(W, 0), dvs[0], dvs[1])
                dqacc[unit["qrows"], :] += jnp.where(_head_lanes(2 * W, 0), dqs[0], dqs[1]) * SCALE

            order = [(2 * tau + half, pi) for half in range(2) for pi in range(len(DILATIONS))]
            units = [None] * len(order)
            for n in range(len(order) + 2):
                if n < len(order):
                    units[n] = scores(*order[n])
                if 1 <= n <= len(order):
                    elementwise(units[n - 1])
                if n >= 2:
                    products(units[n - 2])
            return carry

        lax.fori_loop(0, nt // 2, tile, 0)

        def flush(i, carry):
            rows = pl.ds(pl.multiple_of(i * chunk, chunk), chunk)
            nxt = pl.ds(pl.multiple_of(T + i * chunk, chunk), chunk)
            dz_ref[0, rows, :] = dqacc[rows, :].astype(BF16)
            dz_ref[1, rows, :] = dkacc[rows, :].astype(BF16)
            dz_ref[2, rows, :] = dvacc[rows, :].astype(BF16)
            dqacc[rows, :] = dqacc[nxt, :]
            for hh in range(2):
                qa[hh, rows, :] = qa[hh, nxt, :]
                doa[hh, rows, :] = doa[hh, nxt, :]
            return carry

        lax.fori_loop(0, T // chunk, flush, 0)

    zc = lambda split: (lambda hp, sb: (sb, split * npair + hp))
    zn = lambda split: (lambda hp, sb: (jnp.minimum(sb + 1, nsb - 1), split * npair + hp))
    ec = lambda off: (lambda hp, sb: (sb, off + hp))
    en = lambda off: (lambda hp, sb: (jnp.minimum(sb + 1, nsb - 1), off + hp))
    z0 = lambda split: (lambda hp, sb: (0, split * npair + hp))
    e0 = lambda off: (lambda hp, sb: (0, off + hp))
    blk = lambda index: pl.BlockSpec((T, LANES), index)
    buf = lambda rows: pltpu.VMEM((rows, LANES), F32)
    return pl.pallas_call(
        body, name="attn_bwd", grid=(npair, nsb), out_shape=jax.ShapeDtypeStruct((4, s, e), BF16),
        in_specs=[blk(zc(5)), blk(zc(6)), blk(z0(4)), blk(zn(4)), blk(ec(npair)), blk(en(npair)),
                  blk(zc(7)), blk(zn(7)), blk(ec(0)), blk(en(0)), blk(e0(0)), blk(en(0))],
        out_specs=pl.BlockSpec((4, T, LANES), lambda hp, sb: (0, sb, hp)),
        scratch_shapes=[pltpu.VMEM((2, 2 * T, LANES), F32), pltpu.VMEM((2, 2 * T, LANES), F32),
                        pltpu.VMEM((2, T, LANES), F32), pltpu.VMEM((2, T, LANES), F32),
                        buf(2 * T), buf(T), buf(T), pltpu.VMEM((12, 2 * W, W), F32)],
        compiler_params=_params("parallel", "arbitrary"))(z, z, z, z, dy, dy, z, z, o, o, lse, lse)


def _dz_specs(tm, e):
    def mk(lo, hi):
        return pl.BlockSpec((None, tm, e), lambda i, k: (jnp.clip(k - lo, 0, hi - lo - 1), i, 0))
    return [mk(0, 4), mk(4, 8)]


def _dz_pick(grp, dzh_ref, dza_ref, fn):
    @pl.when(grp < 4)
    def _():
        fn(dzh_ref[...])

    @pl.when(grp >= 4)
    def _():
        fn(dza_ref[...])


def _dh_dx(dzh, dza, w_full, x, gain, dx2):
    s, d = x.shape
    e = dzh.shape[2]
    tm = _tile(s, 1024)
    ni = s // tm
    chunk = _tile(tm, 256)
    fetch_at = 2

    def body(dzh_ref, dza_ref, w_ref, x_hbm, g_ref, dx2_hbm, gx_hbm, dg_ref, acc, xbuf, dbuf, sems):
        i, k = pl.program_id(0), pl.program_id(1)
        tile_rows = pl.ds(pl.multiple_of(i * tm, tm), tm)
        fetch_x = pltpu.make_async_copy(x_hbm.at[tile_rows, :], xbuf, sems.at[0])
        fetch_d = pltpu.make_async_copy(dx2_hbm.at[tile_rows, :], dbuf, sems.at[1])
        store = pltpu.make_async_copy(xbuf, gx_hbm.at[tile_rows, :], sems.at[2])

        @pl.when((i == 0) & (k == 0))
        def _():
            dg_ref[...] = jnp.zeros_like(dg_ref)

        @pl.when(k == 0)
        def _():
            acc[...] = jnp.zeros_like(acc)

        @pl.when((k == fetch_at) & (i > 0))
        def _():
            store.wait()

        @pl.when(k == fetch_at)
        def _():
            fetch_x.start()
            fetch_d.start()

        def add(dz):
            acc[...] += _dot_nt(dz, w_ref[...])

        _dz_pick(k, dzh_ref, dza_ref, add)

        @pl.when(k == N_SPLITS - 1)
        def _():
            fetch_x.wait()
            fetch_d.wait()
            gain_row = g_ref[...]

            def finish(c, dg):
                rows = pl.ds(pl.multiple_of(c * chunk, chunk), chunk)
                dh = acc[rows, :]
                xv = xbuf[rows, :]
                r = lax.rsqrt(jnp.mean(xv * xv, axis=-1, keepdims=True) + NORM_EPS)
                xn = xv * r
                u = dh * gain_row
                xbuf[rows, :] = dbuf[rows, :] + r * (u - xn * jnp.mean(u * xn, axis=-1, keepdims=True))
                return dg + jnp.sum(dh * xn, axis=0, keepdims=True)

            dg_ref[...] += lax.fori_loop(0, tm // chunk, finish, jnp.zeros((1, d), F32))
            store.start()

        @pl.when((k == N_SPLITS - 1) & (i == ni - 1))
        def _():
            store.wait()

    vec = pl.BlockSpec((1, d), lambda i, k: (0, 0))
    return pl.pallas_call(
        body, name="dh_dx", grid=(ni, N_SPLITS),
        out_shape=(jax.ShapeDtypeStruct((s, d), F32), jax.ShapeDtypeStruct((1, d), F32)),
        in_specs=_dz_specs(tm, e) + [pl.BlockSpec((None, d, e), lambda i, k: (k, 0, 0)), ANY, vec, ANY],
        out_specs=(ANY, vec),
        scratch_shapes=[pltpu.VMEM((tm, d), F32), pltpu.VMEM((tm, d), F32), pltpu.VMEM((tm, d), F32),
                        pltpu.SemaphoreType.DMA((3,))],
        compiler_params=_params("arbitrary", "arbitrary"))(dzh, dza, w_full, x, gain, dx2)


def _position():
    x, y, c = lax.axis_index("x"), lax.axis_index("y"), lax.axis_index("c")
    return x, y, c


def _xor_peer(x, y, c, mask):
    return (x ^ ((mask >> 2) & 1), y ^ ((mask >> 1) & 1), c ^ (mask & 1))


def _block_order(masks):
    me = 4 * lax.axis_index("x") + 2 * lax.axis_index("y") + lax.axis_index("c")
    return jnp.stack([me ^ m for m in masks]).astype(jnp.int32)


GATHER_MASKS = (0, 1, 4, 2, 5, 3, 6, 7)


def _inproj_gather(h, w_loc, wo_loc):
    s, d = h.shape
    e = w_loc.shape[1]
    tm = _tile(s, 1024)
    ni = s // tm
    pre = max(ni - 2, 0)

    def body(order_ref, h_ref, w_ref, wo_ref, z_ref, wf_ref, wof_ref, wbuf, send_sems, recv_sems, osend, orecv,
             local_sems, wsems):
        j, i = pl.program_id(0), pl.program_id(1)
        x, y, c = _position()
        me, sibling = (x, y, c), (x, y, 1 - c)
        chips = [(1 - x, y), (x, 1 - y), (1 - x, 1 - y)]
        blk = lambda p: 4 * p[0] + 2 * p[1] + p[2]

        def copy(k, block, to, src=None):
            dst = wf_ref.at[blk(block)]
            return pltpu.make_async_remote_copy(
                src_ref=dst if src is None else src, dst_ref=dst, send_sem=send_sems.at[k], recv_sem=recv_sems.at[k],
                device_id=to, device_id_type=MESH)

        first = [copy(0, me, sibling, src=w_ref)] + [copy(1 + q, me, (*chip, c), src=w_ref) for q, chip in enumerate(chips)]
        passed = [copy(4 + q, (*chip, c), sibling) for q, chip in enumerate(chips)]
        mine = pltpu.make_async_copy(w_ref, wf_ref.at[blk(me)], local_sems.at[0])
        ocopies = [pltpu.make_async_remote_copy(
            src_ref=wo_ref, dst_ref=wof_ref.at[blk(me)], send_sem=osend.at[k], recv_sem=orecv.at[k],
            device_id=_xor_peer(x, y, c, k + 1), device_id_type=MESH) for k in range(N_DEV - 1)]
        omine = pltpu.make_async_copy(wo_ref, wof_ref.at[blk(me)], local_sems.at[1])
        blocks = [me, sibling] + [(*chip, c) for chip in chips] + [(*chip, 1 - c) for chip in chips]
        arrive = [None, copy(0, sibling, me)] + [copy(1 + q, (*chip, c), me) for q, chip in enumerate(chips)] \
            + [copy(4 + q, (*chip, 1 - c), me) for q, chip in enumerate(chips)]
        forward = [None, None] + passed + [None, None, None]
        use_order = (0, 1, 2, 3, 5, 6, 4, 7)
        blocks, arrive, forward = ([lst[n] for n in use_order] for lst in (blocks, arrive, forward))

        def load(slot, src):
            return pltpu.make_async_copy(src, wbuf.at[slot], wsems.at[slot])

        @pl.when((j == 0) & (i == 0))
        def _():
            for cp in [mine, omine] + first + ocopies:
                cp.start()
            load(0, w_ref).start()

        for jj in range(N_DEV):
            @pl.when((j == jj) & (i == 0))
            def _():
                load(jj % 2, w_ref).wait()

            if jj + 1 < N_DEV:
                @pl.when((j == jj) & (i == pre))
                def _():
                    arrive[jj + 1].wait_recv()
                    if forward[jj + 1] is not None:
                        forward[jj + 1].start()
                    load((jj + 1) % 2, wf_ref.at[blk(blocks[jj + 1])]).start()

        z_ref[...] = _dot(h_ref[...], wbuf[j % 2])

        @pl.when((j == N_DEV - 1) & (i == ni - 1))
        def _():
            for cp in first + passed:
                cp.wait_send()
            for cp in ocopies:
                cp.wait_send()
                cp.wait_recv()
            mine.wait()
            omine.wait()

    grid_spec = pltpu.PrefetchScalarGridSpec(
        num_scalar_prefetch=1, grid=(N_DEV, ni),
        in_specs=[pl.BlockSpec((tm, d), lambda j, i, o: (i, 0)), ANY, ANY],
        out_specs=(pl.BlockSpec((tm, e), lambda j, i, o: (i, o[j])), ANY, ANY),
        scratch_shapes=[pltpu.VMEM((2, d, e), BF16), pltpu.SemaphoreType.DMA((7,)), pltpu.SemaphoreType.DMA((7,)),
                        pltpu.SemaphoreType.DMA((7,)), pltpu.SemaphoreType.DMA((7,)), pltpu.SemaphoreType.DMA((2,)),
                        pltpu.SemaphoreType.DMA((2,))])
    return pl.pallas_call(
        body, name="inproj_gather", grid_spec=grid_spec,
        out_shape=(jax.ShapeDtypeStruct((s, N_SPLITS * e), F32), jax.ShapeDtypeStruct((N_DEV, d, e), BF16),
                   jax.ShapeDtypeStruct((N_DEV,) + wo_loc.shape, BF16)),
        compiler_params=_params("arbitrary", "arbitrary"))(_block_order(GATHER_MASKS), h, w_loc, wo_loc)


SCATTER_MASKS = (7, 6, 5, 4, 3, 2, 1, 0)
N_CHIPS = 4


def _scatter_block(k, acc, stage, tmp, own_ref, ra_ref, rb_ref, sa_send, sa_recv, sb_send, sb_recv, loc_sem, last):
    x, y, c = _position()
    chip_of = lambda t: _xor_peer(x, y, c, SCATTER_MASKS[2 * t + 1])

    def ship(t):
        return pltpu.make_async_remote_copy(
            src_ref=stage.at[0], dst_ref=ra_ref.at[t], send_sem=sa_send.at[t], recv_sem=sa_recv.at[t],
            device_id=(x, y, 1 - c), device_id_type=MESH)

    def send(t):
        return pltpu.make_async_remote_copy(
            src_ref=stage.at[1], dst_ref=rb_ref.at[t], send_sem=sb_send.at[t], recv_sem=sb_recv.at[t],
            device_id=chip_of(t), device_id_type=MESH)

    for kk in range(N_DEV):
        t = kk // 2

        @pl.when(last & (k == kk))
        def _():
            if kk % 2 == 0:
                if t >= 1:
                    ship(t - 1).wait_send()
                stage[0] = acc[...].astype(BF16)
                ship(t).start()
            else:
                ship(t).wait_recv()
                fetch = pltpu.make_async_copy(ra_ref.at[t], tmp, loc_sem)
                fetch.start()
                fetch.wait()
                acc[...] += tmp[...].astype(F32)
                if t < N_CHIPS - 1:
                    if t >= 1:
                        send(t - 1).wait_send()
                    stage[1] = acc[...].astype(BF16)
                    send(t).start()
                else:
                    keep = pltpu.make_async_copy(acc, own_ref, loc_sem)
                    keep.start()
                    keep.wait()
                    ship(t).wait_send()
                    send(t - 1).wait_send()
                    for q in range(N_CHIPS - 1):
                        send(q).wait_recv()


def _scatter_scratch(rows, cols):
    return [pltpu.VMEM((rows, cols), F32), pltpu.VMEM((2, rows, cols), BF16), pltpu.VMEM((rows, cols), BF16),
            pltpu.SemaphoreType.DMA((N_CHIPS,)), pltpu.SemaphoreType.DMA((N_CHIPS,)),
            pltpu.SemaphoreType.DMA((N_CHIPS - 1,)), pltpu.SemaphoreType.DMA((N_CHIPS - 1,)), pltpu.SemaphoreType.DMA(())]


def _scatter_out(rows, cols):
    return (jax.ShapeDtypeStruct((rows, cols), F32), jax.ShapeDtypeStruct((N_CHIPS, rows, cols), BF16),
            jax.ShapeDtypeStruct((N_CHIPS - 1, rows, cols), BF16))


def _dwin_scatter(h, dzh, dza):
    s, d = h.shape
    e = dzh.shape[2]
    ts = _tile(s, 1024)
    ns = s // ts

    def body(order_ref, dzh_ref, dza_ref, h_ref, own_ref, ra_ref, rb_ref, acc, stage, tmp, *sems):
        k, step = pl.program_id(0), pl.program_id(1)

        @pl.when(step == 0)
        def _():
            acc[...] = jnp.zeros_like(acc)

        def add(dz):
            acc[...] += _dot_tn(h_ref[...], dz)

        _dz_pick(order_ref[k], dzh_ref, dza_ref, add)
        _scatter_block(k, acc, stage, tmp, own_ref, ra_ref, rb_ref, *sems, step == ns - 1)

    def dz_spec(lo):
        return pl.BlockSpec((None, ts, e), lambda k, st, o: (jnp.clip(o[k] - lo, 0, 3), st, 0))

    grid_spec = pltpu.PrefetchScalarGridSpec(
        num_scalar_prefetch=1, grid=(N_DEV, ns),
        in_specs=[dz_spec(0), dz_spec(4), pl.BlockSpec((ts, d), lambda k, st, o: (st, 0))],
        out_specs=(ANY, ANY, ANY), scratch_shapes=_scatter_scratch(d, e))
    own, _, rb = pl.pallas_call(
        body, name="dwin_scatter", grid_spec=grid_spec, out_shape=_scatter_out(d, e),
        compiler_params=_params("arbitrary", "arbitrary"))(_block_order(SCATTER_MASKS), dzh, dza, h)
    return own, rb


def _dwout_scatter(y_h, y_a, dxb):
    s, e = y_h.shape
    d = dxb.shape[1]
    r = 2 * e // N_DEV
    pairs = e // (2 * r)
    ts = _tile(s, 1024)
    ns = s // ts
    chip_masks = SCATTER_MASKS[1::2]
    passes = ((0, 1), (2,), (3,))
    slots = max(len(chips) for chips in passes)
    slot_chip = [chips[min(u, len(chips) - 1)] for chips in passes for u in range(slots)]

    def body(pair_ref, yh0_ref, ya0_ref, yh1_ref, ya1_ref, dx_ref, own_ref, ra_ref, rb_ref, acc, keep_buf, ship_buf,
             send_buf, tmp, sa_send, sa_recv, sb_send, sb_recv, loc_sem):
        p, step = pl.program_id(0), pl.program_id(1)
        x, y, c = _position()

        @pl.when(step == 0)
        def _():
            acc[...] = jnp.zeros_like(acc)

        for u, (yh_ref, ya_ref) in enumerate(((yh0_ref, ya0_ref), (yh1_ref, ya1_ref))):
            rows = slice(u * 2 * r, (u + 1) * 2 * r)
            used = functools.reduce(jnp.logical_or, [p == pp for pp, chips in enumerate(passes) if u < len(chips)])

            @pl.when(used & (pair_ref[slots * p + u] < pairs))
            def _():
                acc[rows, :] += _dot_tn(yh_ref[...], dx_ref[...])

            @pl.when(used & (pair_ref[slots * p + u] >= pairs))
            def _():
                acc[rows, :] += _dot_tn(ya_ref[...], dx_ref[...])

        def block_rows(u, core):
            return pl.ds(pl.multiple_of(u * 2 * r + core * r, r), r)

        slot_of = {q: u for chips in passes for u, q in enumerate(chips)}

        def ship(q):
            return pltpu.make_async_remote_copy(
                src_ref=ship_buf.at[slot_of[q]], dst_ref=ra_ref.at[q], send_sem=sa_send.at[q], recv_sem=sa_recv.at[q],
                device_id=(x, y, 1 - c), device_id_type=MESH)

        def send(q):
            return pltpu.make_async_remote_copy(
                src_ref=send_buf.at[slot_of[q]], dst_ref=rb_ref.at[q], send_sem=sb_send.at[q], recv_sem=sb_recv.at[q],
                device_id=_xor_peer(x, y, c, chip_masks[q]), device_id_type=MESH)

        def sibling_share(q):
            ship(q).wait_recv()
            fetch = pltpu.make_async_copy(ra_ref.at[q], tmp, loc_sem)
            fetch.start()
            fetch.wait()
            return tmp[...].astype(F32)

        shipped, sent = {}, {}
        for pp, chips in enumerate(passes):
            @pl.when((step == ns - 1) & (p == pp))
            def _():
                for u, q in enumerate(chips):
                    if u in shipped:
                        ship(shipped.pop(u)).wait_send()
                    ship_buf[u] = acc[block_rows(u, 1 - c), :].astype(BF16)
                    ship(q).start()
                    shipped[u] = q
                for u, q in enumerate(chips):
                    total = acc[block_rows(u, c), :] + sibling_share(q)
                    if q < N_CHIPS - 1:
                        if u in sent:
                            send(sent.pop(u)).wait_send()
                        send_buf[u] = total.astype(BF16)
                        send(q).start()
                        sent[u] = q
                    else:
                        keep_buf[...] = total
                        keep = pltpu.make_async_copy(keep_buf, own_ref, loc_sem)
                        keep.start()
                        keep.wait()
                if pp == len(passes) - 1:
                    for q in shipped.values():
                        ship(q).wait_send()
                    for q in sent.values():
                        send(q).wait_send()
                    for q in range(N_CHIPS - 1):
                        send(q).wait_recv()

    def y_spec(u, lo):
        return pl.BlockSpec((ts, 2 * r), lambda p, st, o: (st, jnp.clip(o[slots * p + u] - lo, 0, pairs - 1)))

    pair_of_chip = _block_order(chip_masks) // 2
    grid_spec = pltpu.PrefetchScalarGridSpec(
        num_scalar_prefetch=1, grid=(len(passes), ns),
        in_specs=[y_spec(0, 0), y_spec(0, pairs), y_spec(1, 0), y_spec(1, pairs),
                  pl.BlockSpec((ts, d), lambda p, st, o: (st, 0))],
        out_specs=(ANY, ANY, ANY),
        scratch_shapes=[pltpu.VMEM((slots * 2 * r, d), F32), pltpu.VMEM((r, d), F32),
                        pltpu.VMEM((slots, r, d), BF16)] + _scatter_scratch(r, d)[1:])
    own, _, rb = pl.pallas_call(
        body, name="dwout_scatter", grid_spec=grid_spec, out_shape=_scatter_out(r, d),
        compiler_params=_params("arbitrary", "arbitrary"))(
            jnp.stack([pair_of_chip[q] for q in slot_chip]), y_h, y_a, y_h, y_a, dxb)
    return own, rb


def _sum_chips_adamw(own, recv, w, m, v):
    r, c = w.shape
    tr = _tile(r, 128)

    def body(own_ref, rc_ref, w_ref, m_ref, v_ref, g_ref, d_ref, mo_ref, vo_ref):
        g = own_ref[...]
        for q in range(N_CHIPS - 1):
            g = g + rc_ref[q].astype(F32)
        g_ref[...] = g
        d_ref[...], mo_ref[...], vo_ref[...] = _adamw(w_ref[...], g, m_ref[...], v_ref[...])

    blk = pl.BlockSpec((tr, c), lambda i: (i, 0))
    shp = jax.ShapeDtypeStruct((r, c), F32)
    return pl.pallas_call(
        body, name="sum_chips_adamw", grid=(r // tr,), out_shape=(shp, shp, shp, shp),
        in_specs=[blk, pl.BlockSpec((N_CHIPS - 1, tr, c), lambda i: (0, i, 0)), blk, blk, blk],
        out_specs=(blk, blk, blk, blk), compiler_params=_params("parallel"))(own, recv, w, m, v)


SMALL_ROWS = 8
ROW_LB = 4
ROW_GN = 6
ROW_LOSS = 7


def _small_allreduce_adamw(part, w, m, v, lb_logits):
    width = part.shape[1]

    def body(p_ref, w_ref, m_ref, v_ref, lb_ref, g_ref, d_ref, mo_ref, vo_ref, buf, send_sems, recv_sems):
        x, y, c = _position()
        me = 4 * x + 2 * y + c
        buf[me] = p_ref[...]
        copies = []
        for k in range(N_DEV - 1):
            bx, by, bc = ((k + 1) >> 2) & 1, ((k + 1) >> 1) & 1, (k + 1) & 1
            peer = (x ^ bx, y ^ by, c ^ bc)
            copies.append(pltpu.make_async_remote_copy(
                src_ref=p_ref, dst_ref=buf.at[me], send_sem=send_sems.at[k], recv_sem=recv_sems.at[k],
                device_id=peer, device_id_type=MESH))
        for cp in copies:
            cp.start()
        for cp in copies:
            cp.wait_recv()
        for cp in copies:
            cp.wait_send()
        tot = buf[0]
        for dev in range(1, N_DEV):
            tot = tot + buf[dev]
        lbv = lb_ref[...]
        lb = _sigmoid(lbv[0:1] - lbv[1:2])
        glb = tot[ROW_LB:ROW_LB + 1] * lb * (1.0 - lb)
        loss = jnp.sum(tot[ROW_LOSS:ROW_LOSS + 1], axis=-1, keepdims=True)
        row = lax.broadcasted_iota(jnp.int32, (SMALL_ROWS, width), 0)
        g = jnp.where(row == ROW_LB, glb, jnp.where(row == ROW_LB + 1, -glb, tot))
        g = jnp.where(row == ROW_LOSS, loss, g)
        g_ref[...] = g
        d_ref[...], mo_ref[...], vo_ref[...] = _adamw(w_ref[...], g, m_ref[...], v_ref[...])

    vm = pl.BlockSpec(memory_space=pltpu.VMEM)
    shp = jax.ShapeDtypeStruct((SMALL_ROWS, width), F32)
    return pl.pallas_call(
        body, name="small_allreduce_adamw", out_shape=(shp, shp, shp, shp),
        in_specs=[vm] * 5, out_specs=(vm, vm, vm, vm),
        scratch_shapes=[pltpu.VMEM((N_DEV, SMALL_ROWS, width), F32), pltpu.SemaphoreType.DMA((N_DEV - 1,)),
                        pltpu.SemaphoreType.DMA((N_DEV - 1,))],
    )(part, w, m, v, lb_logits)


def _pack_small(norm_gain, final_gain, lb2, gnorm, last_row, width):
    pad = lambda a: jnp.pad(a.reshape(1, -1), ((0, 0), (0, width - a.size)))
    return jnp.concatenate([norm_gain.reshape(2, width), final_gain.reshape(2, width), lb2.reshape(2, width),
                            pad(gnorm), last_row.reshape(1, width)], axis=0)


def _unpack_small(p, d, e, hd):
    return (p[0:2].reshape(1, d), p[2:4].reshape(d), p[4:6].reshape(2, e), p[6:7, :hd].reshape(1, hd))


def kernel(x, norm_gain, w_in, lb_logits, hgrn_gnorm, w_out, final_gain, loss_target, m_norm_gain, m_w_in, m_lb_logits, m_hgrn_gnorm, m_w_out, m_final_gain, v_norm_gain, v_w_in, v_lb_logits, v_hgrn_gnorm, v_w_out, v_final_gain):
    s, d = x.shape[1], x.shape[2]
    e = w_in.shape[2]
    assert d == 2 * e and lb_logits.shape == (2, e) and w_out.shape[1] * N_DEV == 2 * e
    x2d = x.reshape(s, d)
    tgt = loss_target.reshape(s, d)

    h = _rmsnorm_fwd(x2d, norm_gain)
    z, w_in_full, w_out_full = _inproj_gather(h, _cast_bf16(w_in[0]), _cast_bf16(w_out[0]))
    w_out_full = w_out_full.reshape(2 * e, d)
    y_h, states = _hgrn_fwd(z, lb_logits, hgrn_gnorm)
    o_attn, lse, y_a = _attn_fwd(z)
    dx2, dx2b, dy, loss_vec, dfg = _outproj_loss(x2d, y_h, y_a, w_out_full, final_gain.reshape(1, d), tgt)

    own_o, recv_o = _dwout_scatter(y_h, y_a, dx2b)
    dza = _attn_bwd(z, dy, o_attn, lse)
    dzh, dlb, dgn = _hgrn_bwd(z, dy, states, lb_logits, hgrn_gnorm)
    grad_x, dng = _dh_dx(dzh, dza, w_in_full, x2d, norm_gain, dx2)
    g_wo, d_wo, nm_wo, nv_wo = _sum_chips_adamw(own_o, recv_o, w_out[0], m_w_out[0], v_w_out[0])

    width = d // 2
    zero_row = jnp.zeros((1, width), F32)
    loss_row = loss_vec[:, :width] + loss_vec[:, width:]
    part = _pack_small(dng, dfg, jnp.concatenate([dlb, zero_row], axis=0), dgn, loss_row, width)
    pw = _pack_small(norm_gain, final_gain, lb_logits, hgrn_gnorm, zero_row, width)
    pm = _pack_small(m_norm_gain, m_final_gain, m_lb_logits, m_hgrn_gnorm, zero_row, width)
    pv = _pack_small(v_norm_gain, v_final_gain, v_lb_logits, v_hgrn_gnorm, zero_row, width)
    sg, sd, sm, sv = _small_allreduce_adamw(part, pw, pm, pv, lb_logits)
    own_i, recv_i = _dwin_scatter(h, dzh, dza)
    g_wi, d_wi, nm_wi, nv_wi = _sum_chips_adamw(own_i, recv_i, w_in[0], m_w_in[0], v_w_in[0])
    hd = hgrn_gnorm.shape[1]
    g_ng, g_fg, g_lb, g_gn = _unpack_small(sg, d, e, hd)
    d_ng, d_fg, d_lb, d_gn = _unpack_small(sd, d, e, hd)
    m_ng, m_fg, m_lb, m_gn = _unpack_small(sm, d, e, hd)
    v_ng, v_fg, v_lb, v_gn = _unpack_small(sv, d, e, hd)
    loss = sg[ROW_LOSS, 0]

    one = lambda a: a[None]
    return (loss, grad_x.reshape(1, s, d), g_ng, one(g_wi), g_lb, g_gn, one(g_wo), g_fg,
            d_ng, one(d_wi), d_lb, d_gn, one(d_wo), d_fg,
            m_ng, one(nm_wi), m_lb, m_gn, one(nm_wo), m_fg,
            v_ng, one(nv_wi), v_lb, v_gn, one(nv_wo), v_fg)
```

```python
import functools
import math

import jax
import jax.numpy as jnp
from jax import lax
from jax.experimental import pallas as pl
from jax.experimental.pallas import tpu as pltpu

NORM_EPS = 1e-6
HGRN_HEAD = 128
HGRN_CHUNK = 64
ATTN_HEAD = 64
ATTN_BAND = 128
DILATIONS = (1, 4, 16)
N_SPLITS = 8
N_DEV = 8
ADAM_LR = 0.001
ADAM_B1 = 0.9
ADAM_B2 = 0.999
ADAM_EPS = 1e-08
ADAM_WD = 0.01
ADAM_STEP = 10
LANES = 128
MESH = pl.DeviceIdType.MESH
F32 = jnp.float32
BF16 = jnp.bfloat16
NEG_BIG = -1e30
VMEM_LIMIT = 56 * 1024 * 1024

ANY = pl.BlockSpec(memory_space=pl.ANY)


def _params(*sem):
    return pltpu.CompilerParams(dimension_semantics=sem, vmem_limit_bytes=VMEM_LIMIT)


def _tile(n, pref):
    t = min(n, pref)
    assert n % t == 0, (n, pref)
    return t


def _dot(a, b, precision=None):
    return jnp.dot(a, b, preferred_element_type=F32, precision=precision)


def _dot_nt(a, b):
    return lax.dot_general(a, b, (((1,), (1,)), ((), ())), preferred_element_type=F32)


def _dot_tn(a, b):
    return lax.dot_general(a, b, (((0,), (0,)), ((), ())), preferred_element_type=F32)


def _sigmoid(x):
    return 0.5 * jnp.tanh(0.5 * x) + 0.5


def _dsilu(x, s):
    return s * (1.0 + x * (1.0 - s))


def _adamw(w, g, m, v):
    m = ADAM_B1 * m + (1.0 - ADAM_B1) * g
    v = ADAM_B2 * v + (1.0 - ADAM_B2) * (g * g)
    m_hat = m / (1.0 - ADAM_B1 ** ADAM_STEP)
    v_hat = v / (1.0 - ADAM_B2 ** ADAM_STEP)
    delta = -ADAM_LR * (m_hat / (jnp.sqrt(v_hat) + ADAM_EPS) + ADAM_WD * w)
    return delta, m, v


def _cast_bf16(a):
    r, c = a.shape
    tr = _tile(r, 256)

    def body(a_ref, o_ref):
        o_ref[...] = a_ref[...].astype(BF16)

    return pl.pallas_call(
        body, name="cast_bf16", grid=(r // tr,), out_shape=jax.ShapeDtypeStruct((r, c), BF16),
        in_specs=[pl.BlockSpec((tr, c), lambda i: (i, 0))], out_specs=pl.BlockSpec((tr, c), lambda i: (i, 0)),
        compiler_params=_params("parallel"))(a)


def _rmsnorm_fwd(x, gain):
    s, d = x.shape
    tm = _tile(s, 512)

    def body(x_ref, g_ref, h_ref):
        xv = x_ref[...]
        r = lax.rsqrt(jnp.mean(xv * xv, axis=-1, keepdims=True) + NORM_EPS)
        h_ref[...] = (xv * r * g_ref[...]).astype(BF16)

    return pl.pallas_call(
        body, name="rmsnorm_fwd", grid=(s // tm,), out_shape=jax.ShapeDtypeStruct((s, d), BF16),
        in_specs=[pl.BlockSpec((tm, d), lambda i: (i, 0)), pl.BlockSpec((1, d), lambda i: (0, 0))],
        out_specs=pl.BlockSpec((tm, d), lambda i: (i, 0)), compiler_params=_params("parallel"))(x, gain)


HGRN_BLOCK = 2048
TRI_ROWS = 256


def _chunk_masks():
    tb = TRI_ROWS
    row = lax.broadcasted_iota(jnp.int32, (tb, tb), 0)
    col = lax.broadcasted_iota(jnp.int32, (tb, tb), 1)
    same = (row // HGRN_CHUNK) == (col // HGRN_CHUNK)
    lower = jnp.where(same & (col <= row), 1.0, 0.0).astype(BF16)
    upper = jnp.where(same & (col >= row), 1.0, 0.0).astype(BF16)
    return lower, upper


def _split3(a):
    hi = a.astype(BF16).astype(F32)
    mid = (a - hi).astype(BF16).astype(F32)
    lo = (a - hi - mid).astype(BF16).astype(F32)
    return hi, mid, lo


def _tri_dot(tri, x):
    hi, mid, lo = (p.astype(BF16) for p in _split3(x))
    outs = []
    for r in range(0, x.shape[0], TRI_ROWS):
        sl = slice(r, r + TRI_ROWS)
        outs.append(_dot(tri, hi[sl]) + _dot(tri, mid[sl]) + _dot(tri, lo[sl]))
    return outs[0] if len(outs) == 1 else jnp.concatenate(outs, axis=0)


def _hgrn_gates(qp, fp, lbv):
    lb = _sigmoid(lbv[0:1] - lbv[1:2])
    sq = _sigmoid(qp)
    q = qp * sq
    sg = _sigmoid(fp)
    f = lb + (1.0 - lb) * sg
    k = 1.0 - f
    return lb, sq, q, sg, f, k


def _hgrn_fwd(z, lb_logits, gnorm):
    s = z.shape[0]
    e = z.shape[1] // N_SPLITS
    nh = e // HGRN_HEAD
    tb = _tile(s, HGRN_BLOCK)
    nc = tb // HGRN_CHUNK
    nb = s // tb
    C = HGRN_CHUNK

    def body(q_ref, f_ref, i_ref, g_ref, lb_ref, gn_ref, y_ref, st_ref, state, o_scr):
        @pl.when(pl.program_id(1) == 0)
        def _():
            state[...] = jnp.zeros_like(state)

        lb, sq, q, sg, f, k = _hgrn_gates(q_ref[...], f_ref[...], lb_ref[...])
        lower, _ = _chunk_masks()
        b = _tri_dot(lower, jnp.log(f))
        b3 = b.reshape(nc, C, HGRN_HEAD)
        bc = b3[:, C - 1:C, :]
        qt = (q * jnp.exp(b)).astype(BF16)
        kt = (k * jnp.exp(-b)).astype(BF16)
        ke = (k.reshape(nc, C, HGRN_HEAD) * jnp.exp(bc - b3)).reshape(tb, HGRN_HEAD).astype(BF16)
        v = i_ref[...].astype(BF16)
        tri = lax.broadcasted_iota(jnp.int32, (C, C), 1) <= lax.broadcasted_iota(jnp.int32, (C, C), 0)
        sls = [slice(c * C, (c + 1) * C) for c in range(nc)]
        kv = [_dot_tn(v[sl], ke[sl]) for sl in sls]
        a = [jnp.where(tri, _dot_nt(qt[sl], kt[sl]), 0.0).astype(BF16) for sl in sls]
        st = state[...]
        sts = []
        for c in range(nc):
            sts.append(st)
            st_ref[c] = st
            st = st * jnp.exp(bc[c]) + kv[c]
        state[...] = st
        for c, sl in enumerate(sls):
            o_scr[sl, :] = _dot(a[c], v[sl]) + _dot_nt(qt[sl], sts[c].astype(BF16))
        o = o_scr[...]
        rms = lax.rsqrt(jnp.mean(o * o, axis=-1, keepdims=True) + NORM_EPS)
        gp = g_ref[...]
        y_ref[...] = (o * rms * gn_ref[...] * (gp * _sigmoid(gp))).astype(BF16)

    col = lambda kk: (lambda h, n: (n, kk * nh + h))
    return pl.pallas_call(
        body, name="hgrn_fwd", grid=(nh, nb),
        out_shape=(jax.ShapeDtypeStruct((s, e), BF16),
                   jax.ShapeDtypeStruct((nh, s // C, HGRN_HEAD, HGRN_HEAD), F32)),
        in_specs=[pl.BlockSpec((tb, HGRN_HEAD), col(0)), pl.BlockSpec((tb, HGRN_HEAD), col(1)),
                  pl.BlockSpec((tb, HGRN_HEAD), col(2)), pl.BlockSpec((tb, HGRN_HEAD), col(3)),
                  pl.BlockSpec((2, HGRN_HEAD), lambda h, n: (0, h)), pl.BlockSpec((1, HGRN_HEAD), lambda h, n: (0, 0))],
        out_specs=(pl.BlockSpec((tb, HGRN_HEAD), lambda h, n: (n, h)),
                   pl.BlockSpec((None, nc, HGRN_HEAD, HGRN_HEAD), lambda h, n: (h, n, 0, 0))),
        scratch_shapes=[pltpu.VMEM((HGRN_HEAD, HGRN_HEAD), F32), pltpu.VMEM((tb, HGRN_HEAD), F32)],
        compiler_params=_params("parallel", "arbitrary"))(z, z, z, z, lb_logits, gnorm)


def _hgrn_bwd(z, dy, states, lb_logits, gnorm):
    s = z.shape[0]
    e = z.shape[1] // N_SPLITS
    nh = e // HGRN_HEAD
    tb = _tile(s, HGRN_BLOCK)
    nc = tb // HGRN_CHUNK
    nb = s // tb
    C = HGRN_CHUNK
    H = HGRN_HEAD

    def body(q_ref, f_ref, i_ref, g_ref, dy_ref, st_ref, lb_ref, gn_ref, dz_ref, dlb_ref, dgn_ref,
             gstate, o_scr, dq_scr, dk_scr, dv_scr, e_scr):
        first = (pl.program_id(0) == 0) & (pl.program_id(1) == 0)

        @pl.when(first)
        def _():
            dgn_ref[...] = jnp.zeros_like(dgn_ref)

        @pl.when(pl.program_id(1) == 0)
        def _():
            gstate[...] = jnp.zeros_like(gstate)
            dlb_ref[...] = jnp.zeros_like(dlb_ref)

        qp = q_ref[...]
        lb, sq, q, sg, f, k = _hgrn_gates(qp, f_ref[...], lb_ref[...])
        lower, upper = _chunk_masks()
        b = _tri_dot(lower, jnp.log(f))
        b3 = b.reshape(nc, C, H)
        bc = b3[:, C - 1:C, :]
        eb = jnp.exp(b)
        enb = jnp.exp(-b)
        eend = jnp.exp(bc - b3).reshape(tb, H)
        qt = (q * eb).astype(BF16)
        kt = (k * enb).astype(BF16)
        ke = (k * eend).astype(BF16)
        v = i_ref[...].astype(BF16)
        tri = lax.broadcasted_iota(jnp.int32, (C, C), 1) <= lax.broadcasted_iota(jnp.int32, (C, C), 0)
        sls = [slice(c * C, (c + 1) * C) for c in range(nc)]
        a = [jnp.where(tri, _dot_nt(qt[sl], kt[sl]), 0.0).astype(BF16) for sl in sls]
        for c, sl in enumerate(sls):
            o_scr[sl, :] = _dot(a[c], v[sl]) + _dot_nt(qt[sl], st_ref[c].astype(BF16))
        o = o_scr[...]
        rms = lax.rsqrt(jnp.mean(o * o, axis=-1, keepdims=True) + NORM_EPS)
        on = o * rms
        gn = gn_ref[...]
        gp = g_ref[...]
        sgg = _sigmoid(gp)
        dyv = dy_ref[...]
        d_on = dyv * (gp * sgg)
        dz_ref[3] = (dyv * on * gn * _dsilu(gp, sgg)).astype(BF16)
        dgn_ref[...] += jnp.sum(d_on * on, axis=0, keepdims=True)
        u = d_on * gn
        do = (rms * (u - on * jnp.mean(u * on, axis=-1, keepdims=True))).astype(BF16)
        gup = [_dot_tn(do[sl], qt[sl]) for sl in sls]
        da = [jnp.where(tri, _dot_nt(do[sl], v[sl]), 0.0).astype(BF16) for sl in sls]
        gt = gstate[...]
        gts = [None] * nc
        for c in reversed(range(nc)):
            gts[c] = gt
            gt = gt * jnp.exp(bc[c]) + gup[c]
        gstate[...] = gt
        for c, sl in enumerate(sls):
            stp = st_ref[c]
            gtb = gts[c].astype(BF16)
            dqt = _dot(da[c], kt[sl]) + _dot(do[sl], stp.astype(BF16))
            dkt = _dot_tn(da[c], qt[sl])
            dks = _dot(v[sl], gtb) * eend[sl]
            dv_scr[sl, :] = _dot_tn(a[c], do[sl]) + _dot_nt(ke[sl], gtb)
            dq_scr[sl, :] = dqt * eb[sl]
            dk_scr[sl, :] = dkt * enb[sl] + dks
            ech = (jnp.sum(k[sl] * dks, axis=0, keepdims=True)
                   + jnp.sum(gts[c] * jnp.exp(bc[c]) * stp, axis=0, keepdims=True))
            e_scr[sl, :] = jnp.broadcast_to(ech, (C, H))
        dq = dq_scr[...]
        dk = dk_scr[...]
        dlf = _tri_dot(upper, q * dq - k * dk) + e_scr[...]
        dft = dlf / f - dk
        dz_ref[0] = (dq * _dsilu(qp, sq)).astype(BF16)
        dz_ref[1] = (dft * (1.0 - lb) * sg * (1.0 - sg)).astype(BF16)
        dz_ref[2] = dv_scr[...].astype(BF16)
        dlb_ref[...] += jnp.sum(dft * (1.0 - sg), axis=0, keepdims=True)

    col = lambda kk: (lambda h, n: (nb - 1 - n, kk * nh + h))
    return pl.pallas_call(
        body, name="hgrn_bwd", grid=(nh, nb),
        out_shape=(jax.ShapeDtypeStruct((4, s, e), BF16), jax.ShapeDtypeStruct((1, e), F32),
                   jax.ShapeDtypeStruct((1, H), F32)),
        in_specs=[pl.BlockSpec((tb, H), col(0)), pl.BlockSpec((tb, H), col(1)),
                  pl.BlockSpec((tb, H), col(2)), pl.BlockSpec((tb, H), col(3)),
                  pl.BlockSpec((tb, H), lambda h, n: (nb - 1 - n, h)),
                  pl.BlockSpec((None, nc, H, H), lambda h, n: (h, nb - 1 - n, 0, 0)),
                  pl.BlockSpec((2, H), lambda h, n: (0, h)), pl.BlockSpec((1, H), lambda h, n: (0, 0))],
        out_specs=(pl.BlockSpec((4, tb, H), lambda h, n: (0, nb - 1 - n, h)),
                   pl.BlockSpec((1, H), lambda h, n: (0, h)), pl.BlockSpec((1, H), lambda h, n: (0, 0))),
        scratch_shapes=[pltpu.VMEM((H, H), F32)] + [pltpu.VMEM((tb, H), F32)] * 5,
        compiler_params=_params("arbitrary", "arbitrary"))(z, z, z, z, dy, states, lb_logits, gnorm)


ATTN_T = 16 * ATTN_BAND
SCALE = ATTN_HEAD ** -0.5
TILE_UNROLL = 2


def _slope(hh, nheads):
    head = (2 * pl.program_id(0) + hh + 1).astype(F32)
    return jnp.exp(jnp.full((1, 1), -8.0 / nheads * math.log(2.0), F32) * head)


def _fill_bias(bias, nheads, delta, edge_ok):
    band = (delta >= 0) & (delta <= ATTN_BAND)
    dist = delta.astype(F32)
    for pi, dil in enumerate(DILATIONS):
        for hh in range(2):
            full = jnp.where(band, -(_slope(hh, nheads) * float(dil)) * dist, NEG_BIG)
            bias[(pi * 2 + hh) * 2] = full
            bias[(pi * 2 + hh) * 2 + 1] = jnp.where(edge_ok, full, NEG_BIG)


def _rows(start, size, stride):
    if stride == 1:
        return pl.ds(pl.multiple_of(start, ATTN_BAND), size)
    return pl.ds(start, size, stride=stride)


def _head_lanes(rows, hh):
    return (lax.broadcasted_iota(jnp.int32, (rows, LANES), 1) // ATTN_HEAD) == hh


def _attn_fwd(z):
    s = z.shape[0]
    e = z.shape[1] // N_SPLITS
    npair = e // LANES
    T = ATTN_T
    assert s % T == 0
    nsb = s // T
    W = ATTN_BAND
    nt = T // W
    HD = ATTN_HEAD
    chunk = 256

    def body(q_ref, kp_ref, kc_ref, vp_ref, vc_ref, g_ref, o_ref, l_ref, y_ref, qa, kbuf, va, bias, accs, ms, lsw):
        sb = pl.program_id(1)
        def stage(i, carry):
            rows = pl.ds(pl.multiple_of(i * chunk, chunk), chunk)
            upper = pl.ds(pl.multiple_of(T + i * chunk, chunk), chunk)
            kbuf[upper, :] = kc_ref[rows, :]
            for hh in range(2):
                mine = _head_lanes(chunk, hh)
                qa[hh, rows, :] = jnp.where(mine, q_ref[rows, :] * SCALE, 0.0)
                va[hh, upper, :] = jnp.where(mine, vc_ref[rows, :], 1.0)
            return carry

        lax.fori_loop(0, T // chunk, stage, 0)

        @pl.when(sb == 0)
        def _():
            def stage_prev(i, carry):
                rows = pl.ds(pl.multiple_of(i * chunk, chunk), chunk)
                kbuf[rows, :] = kp_ref[rows, :]
                for hh in range(2):
                    va[hh, rows, :] = jnp.where(_head_lanes(chunk, hh), vp_ref[rows, :], 1.0)
                return carry

            lax.fori_loop(0, T // chunk, stage_prev, 0)
        qi = lax.broadcasted_iota(jnp.int32, (W, 2 * W), 0)
        kj = lax.broadcasted_iota(jnp.int32, (W, 2 * W), 1)
        _fill_bias(bias, 2 * npair, W + qi - kj, kj >= W)

        def tile(tau, carry):
            first = _head_lanes(W, 0)
            rows, scores = [], []
            for pi, dil in enumerate(DILATIONS):
                r = tau % dil
                ub = tau // dil
                qrows = _rows(r + dil * W * ub, W, dil)
                krows = _rows(T + dil * W * (ub - 1) + r, 2 * W, dil)
                var = jnp.where((sb == 0) & (ub == 0), 1, 0)
                kt = kbuf[krows, :].astype(BF16)
                rows.append((qrows, krows))
                scores.append([_dot_nt(qa[hh, qrows, :].astype(BF16), kt) + bias[(pi * 2 + hh) * 2 + var]
                               for hh in range(2)])
            maxes = [[jnp.max(sc, axis=-1, keepdims=True) for sc in pair] for pair in scores]
            probs = [[jnp.exp(sc - m).astype(BF16) for sc, m in zip(ps, pm)] for ps, pm in zip(scores, maxes)]
            for pi, (qrows, krows) in enumerate(rows):
                outs = [_dot(probs[pi][hh], va[hh, krows, :].astype(BF16)) for hh in range(2)]
                accs[pi, qrows, :] = jnp.where(first, outs[0], outs[1])
                lsw[pi, qrows, :] = jnp.where(first, outs[1], outs[0])
                ms[pi, qrows, :] = jnp.where(first, maxes[pi][0], maxes[pi][1])
            return carry

        lax.fori_loop(0, nt, tile, 0, unroll=TILE_UNROLL)

        def merge(i, carry):
            rows = pl.ds(pl.multiple_of(i * chunk, chunk), chunk)
            m1, m2, m3 = ms[0, rows, :], ms[1, rows, :], ms[2, rows, :]
            mx = jnp.maximum(jnp.maximum(m1, m2), m3)
            w1, w2, w3 = jnp.exp(m1 - mx), jnp.exp(m2 - mx), jnp.exp(m3 - mx)
            unswap = lambda a: pltpu.roll(a, ATTN_HEAD, 1)
            den = w1 * unswap(lsw[0, rows, :]) + w2 * unswap(lsw[1, rows, :]) + w3 * unswap(lsw[2, rows, :])
            o = (w1 * accs[0, rows, :] + w2 * accs[1, rows, :] + w3 * accs[2, rows, :]) / den
            o_ref[rows, :] = o
            l_ref[rows, :] = mx + jnp.log(den)
            gp = g_ref[rows, :]
            y_ref[rows, :] = (o * (gp * _sigmoid(gp))).astype(BF16)
            upper = pl.ds(pl.multiple_of(T + i * chunk, chunk), chunk)
            kbuf[rows, :] = kbuf[upper, :]
            for hh in range(2):
                va[hh, rows, :] = va[hh, upper, :]
            return carry

        lax.fori_loop(0, T // chunk, merge, 0)

    cur = lambda split: (lambda hp, sb: (sb, split * npair + hp))
    prev = lambda split: (lambda hp, sb: (0, split * npair + hp))
    blk = lambda index: pl.BlockSpec((T, LANES), index)
    out = blk(lambda hp, sb: (sb, hp))
    buf = lambda rows: pltpu.VMEM((rows, LANES), F32)
    return pl.pallas_call(
        body, name="attn_fwd", grid=(npair, nsb),
        out_shape=(jax.ShapeDtypeStruct((s, e), F32), jax.ShapeDtypeStruct((s, e), F32), jax.ShapeDtypeStruct((s, e), BF16)),
        in_specs=[blk(cur(4)), blk(prev(5)), blk(cur(5)), blk(prev(6)), blk(cur(6)), blk(cur(7))],
        out_specs=(out, out, out),
        scratch_shapes=[pltpu.VMEM((2, T, LANES), F32), buf(2 * T), pltpu.VMEM((2, 2 * T, LANES), F32),
                        pltpu.VMEM((12, W, 2 * W), F32)] + [pltpu.VMEM((3, T, LANES), F32)] * 3,
        compiler_params=_params("parallel", "arbitrary"))(z, z, z, z, z, z)


def _outproj_loss(x, y_h, y_a, w_out_full, final_gain, target):
    s, d = x.shape
    e = y_h.shape[1]
    tm = _tile(s, 256)

    def body(x_ref, yh_ref, ya_ref, w_ref, g_ref, t_ref, dx_ref, dxb_ref, dy_ref, loss_ref, dg_ref):
        @pl.when(pl.program_id(0) == 0)
        def _():
            loss_ref[...] = jnp.zeros_like(loss_ref)
            dg_ref[...] = jnp.zeros_like(dg_ref)

        w = w_ref[...]
        x2 = x_ref[...] + _dot(yh_ref[...], w[0:e]) + _dot(ya_ref[...], w[e:2 * e])
        r = lax.rsqrt(jnp.mean(x2 * x2, axis=-1, keepdims=True) + NORM_EPS)
        xn = x2 * r
        g = g_ref[...]
        err = xn * g - t_ref[...]
        loss_ref[...] += jnp.sum(err * err, axis=0, keepdims=True) * (0.5 / d)
        dyo = err * (1.0 / d)
        dg_ref[...] += jnp.sum(dyo * xn, axis=0, keepdims=True)
        u = dyo * g
        dx2 = r * (u - xn * jnp.mean(u * xn, axis=-1, keepdims=True))
        dx_ref[...] = dx2
        dxb = dx2.astype(BF16)
        dxb_ref[...] = dxb
        dy_ref[...] = _dot_nt(dxb, w)

    row = pl.BlockSpec((tm, d), lambda i: (i, 0))
    half = pl.BlockSpec((tm, e), lambda i: (i, 0))
    vec = pl.BlockSpec((1, d), lambda i: (0, 0))
    return pl.pallas_call(
        body, name="outproj_loss", grid=(s // tm,),
        out_shape=(jax.ShapeDtypeStruct((s, d), F32), jax.ShapeDtypeStruct((s, d), BF16),
                   jax.ShapeDtypeStruct((s, 2 * e), F32), jax.ShapeDtypeStruct((1, d), F32),
                   jax.ShapeDtypeStruct((1, d), F32)),
        in_specs=[row, half, half, pl.BlockSpec((2 * e, d), lambda i: (0, 0)), vec, row],
        out_specs=(row, row, pl.BlockSpec((tm, 2 * e), lambda i: (i, 0)), vec, vec),
        compiler_params=_params("arbitrary"))(x, y_h, y_a, w_out_full, final_gain, target)


def _attn_bwd(z, dy, o, lse):
    s, e = o.shape
    npair = e // LANES
    T = ATTN_T
    assert s % T == 0
    nsb = s // T
    W = ATTN_BAND
    nt = T // W
    HD = ATTN_HEAD
    chunk = 256

    def body(k_ref, v_ref, qc_ref, qn_ref, dyc_ref, dyn_ref, gc_ref, gn_ref, oc_ref, on_ref, lc_ref, ln_ref,
             dz_ref, qa, doa, ka, va, dqacc, dkacc, dvacc, bias):
        sb = pl.program_id(1)
        def stage_queries(half, q_r, dy_r, g_r, o_r, l_r):
            def stage(i, carry):
                rows = pl.ds(pl.multiple_of(i * chunk, chunk), chunk)
                dst = pl.ds(pl.multiple_of(half * T + i * chunk, chunk), chunk)
                lane = lax.broadcasted_iota(jnp.int32, (chunk, LANES), 1)
                gp = g_r[rows, :]
                dov = dy_r[rows, :] * (gp * _sigmoid(gp))
                qv = q_r[rows, :] * SCALE
                same_head = (lax.broadcasted_iota(jnp.int32, (LANES, LANES), 0) // HD
                             == lax.broadcasted_iota(jnp.int32, (LANES, LANES), 1) // HD)
                ones = jnp.where(same_head, 1.0, 0.0).astype(BF16)
                hi, mid, lo = (p.astype(BF16) for p in _split3(dov * o_r[rows, :]))
                delta = _dot(hi, ones) + _dot(mid, ones) + _dot(lo, ones)
                swap = lambda a: pltpu.roll(a, HD, 1)
                lse_parts = [swap(p) for p in _split3(l_r[rows, :])]
                dl_parts = [swap(p) for p in _split3(delta)]
                for hh in range(2):
                    mine = _head_lanes(chunk, hh)
                    spare = (1 - hh) * HD
                    qh = jnp.where(mine, qv, 0.0)
                    dh = jnp.where(mine, dov, 0.0)
                    for j in range(3):
                        qh = jnp.where(lane == spare + j, lse_parts[j], qh)
                        dh = jnp.where(lane == spare + j, dl_parts[j], dh)
                    qa[hh, dst, :] = qh
                    doa[hh, dst, :] = dh
                return carry

            lax.fori_loop(0, T // chunk, stage, 0)

        @pl.when(sb == 0)
        def _():
            stage_queries(0, qc_ref, dyc_ref, gc_ref, oc_ref, lc_ref)

        stage_queries(1, qn_ref, dyn_ref, gn_ref, on_ref, ln_ref)

        def stage_keys(i, carry):
            rows = pl.ds(pl.multiple_of(i * chunk, chunk), chunk)
            lane = lax.broadcasted_iota(jnp.int32, (chunk, LANES), 1)
            for hh in range(2):
                spare = (1 - hh) * HD
                minus = (lane >= spare) & (lane < spare + 3)
                ka[hh, rows, :] = jnp.where(minus, -1.0, k_ref[rows, :])
                va[hh, rows, :] = jnp.where(minus, -1.0, v_ref[rows, :])
            gp = gc_ref[rows, :]
            dz_ref[3, rows, :] = (dyc_ref[rows, :] * oc_ref[rows, :] * _dsilu(gp, _sigmoid(gp))).astype(BF16)
            return carry

        lax.fori_loop(0, T // chunk, stage_keys, 0)

        @pl.when(sb == 0)
        def _():
            dqacc[0:T, :] = jnp.zeros((T, LANES), F32)

        dqacc[T:, :] = jnp.zeros((T, LANES), F32)
        dkacc[...] = jnp.zeros_like(dkacc)
        dvacc[...] = jnp.zeros_like(dvacc)
        qi = lax.broadcasted_iota(jnp.int32, (2 * W, W), 0)
        kj = lax.broadcasted_iota(jnp.int32, (2 * W, W), 1)
        _fill_bias(bias, 2 * npair, qi - kj, qi < W)

        def tile(tau, carry):
            def scores(step, pi):
                dil = DILATIONS[pi]
                r = step % dil
                ub = step // dil
                start = r + dil * W * ub
                krows = _rows(start, W, dil)
                qrows = _rows(start, 2 * W, dil)
                var = jnp.where((sb == nsb - 1) & (ub == nt // dil - 1), 1, 0)
                unit = dict(krows=krows, qrows=qrows, ops=[], sc=[], dpd=[])
                for hh in range(2):
                    kt = ka[hh, krows, :].astype(BF16)
                    vt = va[hh, krows, :].astype(BF16)
                    qt = qa[hh, qrows, :].astype(BF16)
                    dt = doa[hh, qrows, :].astype(BF16)
                    unit["ops"].append((kt, qt, dt))
                    unit["sc"].append(_dot_nt(qt, kt) + bias[(pi * 2 + hh) * 2 + var])
                    unit["dpd"].append(_dot_nt(dt, vt))
                return unit

            def elementwise(unit):
                ps = [jnp.exp(s_) for s_ in unit["sc"]]
                unit["ds"] = [(p * d).astype(BF16) for p, d in zip(ps, unit["dpd"])]
                unit["pb"] = [p.astype(BF16) for p in ps]

            def products(unit):
                dvs = [_dot_tn(pb, dt) for pb, (kt, qt, dt) in zip(unit["pb"], unit["ops"])]
                dks = [_dot_tn(ds, qt) for ds, (kt, qt, dt) in zip(unit["ds"], unit["ops"])]
                dqs = [_dot(ds, kt) for ds, (kt, qt, dt) in zip(unit["ds"], unit["ops"])]
                dkacc[unit["krows"], :] += jnp.where(_head_lanes(W, 0), dks[0], dks[1])
                dvacc[unit["krows"], :] += jnp.where(_head_lanes(W, 0), dvs[0], dvs[1])
                dqacc[unit["qrows"], :] += jnp.where(_head_lanes(2 * W, 0), dqs[0], dqs[1]) * SCALE

            order = [(2 * tau + half, pi) for half in range(2) for pi in range(len(DILATIONS))]
            units = [None] * len(order)
            for n in range(len(order) + 2):
                if n < len(order):
                    units[n] = scores(*order[n])
                if 1 <= n <= len(order):
                    elementwise(units[n - 1])
                if n >= 2:
                    products(units[n - 2])
            return carry

        lax.fori_loop(0, nt // 2, tile, 0)

        def flush(i, carry):
            rows = pl.ds(pl.multiple_of(i * chunk, chunk), chunk)
            nxt = pl.ds(pl.multiple_of(T + i * chunk, chunk), chunk)
            dz_ref[0, rows, :] = dqacc[rows, :].astype(BF16)
            dz_ref[1, rows, :] = dkacc[rows, :].astype(BF16)
            dz_ref[2, rows, :] = dvacc[rows, :].astype(BF16)
            dqacc[rows, :] = dqacc[nxt, :]
            for hh in range(2):
                qa[hh, rows, :] = qa[hh, nxt, :]
                doa[hh, rows, :] = doa[hh, nxt, :]
            return carry

        lax.fori_loop(0, T // chunk, flush, 0)

    zc = lambda split: (lambda hp, sb: (sb, split * npair + hp))
    zn = lambda split: (lambda hp, sb: (jnp.minimum(sb + 1, nsb - 1), split * npair + hp))
    ec = lambda off: (lambda hp, sb: (sb, off + hp))
    en = lambda off: (lambda hp, sb: (jnp.minimum(sb + 1, nsb - 1), off + hp))
    z0 = lambda split: (lambda hp, sb: (0, split * npair + hp))
    e0 = lambda off: (lambda hp, sb: (0, off + hp))
    blk = lambda index: pl.BlockSpec((T, LANES), index)
    buf = lambda rows: pltpu.VMEM((rows, LANES), F32)
    return pl.pallas_call(
        body, name="attn_bwd", grid=(npair, nsb), out_shape=jax.ShapeDtypeStruct((4, s, e), BF16),
        in_specs=[blk(zc(5)), blk(zc(6)), blk(z0(4)), blk(zn(4)), blk(ec(npair)), blk(en(npair)),
                  blk(zc(7)), blk(zn(7)), blk(ec(0)), blk(en(0)), blk(e0(0)), blk(en(0))],
        out_specs=pl.BlockSpec((4, T, LANES), lambda hp, sb: (0, sb, hp)),
        scratch_shapes=[pltpu.VMEM((2, 2 * T, LANES), F32), pltpu.VMEM((2, 2 * T, LANES), F32),
                        pltpu.VMEM((2, T, LANES), F32), pltpu.VMEM((2, T, LANES), F32),
                        buf(2 * T), buf(T), buf(T), pltpu.VMEM((12, 2 * W, W), F32)],
        compiler_params=_params("parallel", "arbitrary"))(z, z, z, z, dy, dy, z, z, o, o, lse, lse)


def _dz_specs(tm, e):
    def mk(lo, hi):
        return pl.BlockSpec((None, tm, e), lambda i, k: (jnp.clip(k - lo, 0, hi - lo - 1), i, 0))
    return [mk(0, 4), mk(4, 8)]


def _dz_pick(grp, dzh_ref, dza_ref, fn):
    @pl.when(grp < 4)
    def _():
        fn(dzh_ref[...])

    @pl.when(grp >= 4)
    def _():
        fn(dza_ref[...])


def _dh_dx(dzh, dza, w_full, x, gain, dx2):
    s, d = x.shape
    e = dzh.shape[2]
    tm = _tile(s, 1024)
    ni = s // tm
    chunk = _tile(tm, 256)
    fetch_at = 2

    def body(dzh_ref, dza_ref, w_ref, x_hbm, g_ref, dx2_hbm, gx_hbm, dg_ref, acc, xbuf, dbuf, sems):
        i, k = pl.program_id(0), pl.program_id(1)
        tile_rows = pl.ds(pl.multiple_of(i * tm, tm), tm)
        fetch_x = pltpu.make_async_copy(x_hbm.at[tile_rows, :], xbuf, sems.at[0])
        fetch_d = pltpu.make_async_copy(dx2_hbm.at[tile_rows, :], dbuf, sems.at[1])
        store = pltpu.make_async_copy(xbuf, gx_hbm.at[tile_rows, :], sems.at[2])

        @pl.when((i == 0) & (k == 0))
        def _():
            dg_ref[...] = jnp.zeros_like(dg_ref)

        @pl.when(k == 0)
        def _():
            acc[...] = jnp.zeros_like(acc)

        @pl.when((k == fetch_at) & (i > 0))
        def _():
            store.wait()

        @pl.when(k == fetch_at)
        def _():
            fetch_x.start()
            fetch_d.start()

        def add(dz):
            acc[...] += _dot_nt(dz, w_ref[...])

        _dz_pick(k, dzh_ref, dza_ref, add)

        @pl.when(k == N_SPLITS - 1)
        def _():
            fetch_x.wait()
            fetch_d.wait()
            gain_row = g_ref[...]

            def finish(c, dg):
                rows = pl.ds(pl.multiple_of(c * chunk, chunk), chunk)
                dh = acc[rows, :]
                xv = xbuf[rows, :]
                r = lax.rsqrt(jnp.mean(xv * xv, axis=-1, keepdims=True) + NORM_EPS)
                xn = xv * r
                u = dh * gain_row
                xbuf[rows, :] = dbuf[rows, :] + r * (u - xn * jnp.mean(u * xn, axis=-1, keepdims=True))
                return dg + jnp.sum(dh * xn, axis=0, keepdims=True)

            dg_ref[...] += lax.fori_loop(0, tm // chunk, finish, jnp.zeros((1, d), F32))
            store.start()

        @pl.when((k == N_SPLITS - 1) & (i == ni - 1))
        def _():
            store.wait()

    vec = pl.BlockSpec((1, d), lambda i, k: (0, 0))
    return pl.pallas_call(
        body, name="dh_dx", grid=(ni, N_SPLITS),
        out_shape=(jax.ShapeDtypeStruct((s, d), F32), jax.ShapeDtypeStruct((1, d), F32)),
        in_specs=_dz_specs(tm, e) + [pl.BlockSpec((None, d, e), lambda i, k: (k, 0, 0)), ANY, vec, ANY],
        out_specs=(ANY, vec),
        scratch_shapes=[pltpu.VMEM((tm, d), F32), pltpu.VMEM((tm, d), F32), pltpu.VMEM((tm, d), F32),
                        pltpu.SemaphoreType.DMA((3,))],
        compiler_params=_params("arbitrary", "arbitrary"))(dzh, dza, w_full, x, gain, dx2)


def _position():
    x, y, c = lax.axis_index("x"), lax.axis_index("y"), lax.axis_index("c")
    return x, y, c


def _xor_peer(x, y, c, mask):
    return (x ^ ((mask >> 2) & 1), y ^ ((mask >> 1) & 1), c ^ (mask & 1))


def _block_order(masks):
    me = 4 * lax.axis_index("x") + 2 * lax.axis_index("y") + lax.axis_index("c")
    return jnp.stack([me ^ m for m in masks]).astype(jnp.int32)


GATHER_MASKS = (0, 1, 4, 5, 2, 3, 6, 7)


def _inproj_gather(h, w_loc, wo_loc):
    s, d = h.shape
    e = w_loc.shape[1]
    tm = _tile(s, 1024)
    ni = s // tm
    pre = max(ni - 2, 0)

    def body(order_ref, h_ref, w_ref, wo_ref, z_ref, wf_ref, wof_ref, wbuf, send_sems, recv_sems, osend, orecv,
             local_sems, wsems):
        j, i = pl.program_id(0), pl.program_id(1)
        x, y, c = _position()
        me, sibling = (x, y, c), (x, y, 1 - c)
        chips = [(1 - x, y), (x, 1 - y), (1 - x, 1 - y)]
        blk = lambda p: 4 * p[0] + 2 * p[1] + p[2]

        def copy(k, block, to, src=None):
            dst = wf_ref.at[blk(block)]
            return pltpu.make_async_remote_copy(
                src_ref=dst if src is None else src, dst_ref=dst, send_sem=send_sems.at[k], recv_sem=recv_sems.at[k],
                device_id=to, device_id_type=MESH)

        first = [copy(0, me, sibling, src=w_ref)] + [copy(1 + q, me, (*chip, c), src=w_ref) for q, chip in enumerate(chips)]
        passed = [copy(4 + q, (*chip, c), sibling) for q, chip in enumerate(chips)]
        mine = pltpu.make_async_copy(w_ref, wf_ref.at[blk(me)], local_sems.at[0])
        ocopies = [pltpu.make_async_remote_copy(
            src_ref=wo_ref, dst_ref=wof_ref.at[blk(me)], send_sem=osend.at[k], recv_sem=orecv.at[k],
            device_id=_xor_peer(x, y, c, k + 1), device_id_type=MESH) for k in range(N_DEV - 1)]
        omine = pltpu.make_async_copy(wo_ref, wof_ref.at[blk(me)], local_sems.at[1])
        blocks = [me, sibling] + [(*chip, c) for chip in chips] + [(*chip, 1 - c) for chip in chips]
        arrive = [None, copy(0, sibling, me)] + [copy(1 + q, (*chip, c), me) for q, chip in enumerate(chips)] \
            + [copy(4 + q, (*chip, 1 - c), me) for q, chip in enumerate(chips)]
        forward = [None, None] + passed + [None, None, None]
        use_order = (0, 1, 2, 5, 3, 6, 4, 7)
        blocks, arrive, forward = ([lst[n] for n in use_order] for lst in (blocks, arrive, forward))

        def load(slot, src):
            return pltpu.make_async_copy(src, wbuf.at[slot], wsems.at[slot])

        @pl.when((j == 0) & (i == 0))
        def _():
            for cp in [mine, omine] + first + ocopies:
                cp.start()
            load(0, w_ref).start()

        for jj in range(N_DEV):
            @pl.when((j == jj) & (i == 0))
            def _():
                load(jj % 2, w_ref).wait()

            if jj + 1 < N_DEV:
                @pl.when((j == jj) & (i == pre))
                def _():
                    arrive[jj + 1].wait_recv()
                    if forward[jj + 1] is not None:
                        forward[jj + 1].start()
                    load((jj + 1) % 2, wf_ref.at[blk(blocks[jj + 1])]).start()

        z_ref[...] = _dot(h_ref[...], wbuf[j % 2])

        @pl.when((j == N_DEV - 1) & (i == ni - 1))
        def _():
            for cp in first + passed:
                cp.wait_send()
            for cp in ocopies:
                cp.wait_send()
                cp.wait_recv()
            mine.wait()
            omine.wait()

    grid_spec = pltpu.PrefetchScalarGridSpec(
        num_scalar_prefetch=1, grid=(N_DEV, ni),
        in_specs=[pl.BlockSpec((tm, d), lambda j, i, o: (i, 0)), ANY, ANY],
        out_specs=(pl.BlockSpec((tm, e), lambda j, i, o: (i, o[j])), ANY, ANY),
        scratch_shapes=[pltpu.VMEM((2, d, e), BF16), pltpu.SemaphoreType.DMA((7,)), pltpu.SemaphoreType.DMA((7,)),
                        pltpu.SemaphoreType.DMA((7,)), pltpu.SemaphoreType.DMA((7,)), pltpu.SemaphoreType.DMA((2,)),
                        pltpu.SemaphoreType.DMA((2,))])
    return pl.pallas_call(
        body, name="inproj_gather", grid_spec=grid_spec,
        out_shape=(jax.ShapeDtypeStruct((s, N_SPLITS * e), F32), jax.ShapeDtypeStruct((N_DEV, d, e), BF16),
                   jax.ShapeDtypeStruct((N_DEV,) + wo_loc.shape, BF16)),
        compiler_params=_params("arbitrary", "arbitrary"))(_block_order(GATHER_MASKS), h, w_loc, wo_loc)


SCATTER_MASKS = (7, 6, 5, 4, 3, 2, 1, 0)
N_CHIPS = 4


def _scatter_block(k, acc, stage, tmp, own_ref, ra_ref, rb_ref, sa_send, sa_recv, sb_send, sb_recv, loc_sem, last):
    x, y, c = _position()
    chip_of = lambda t: _xor_peer(x, y, c, SCATTER_MASKS[2 * t + 1])

    def ship(t):
        return pltpu.make_async_remote_copy(
            src_ref=stage.at[0], dst_ref=ra_ref.at[t], send_sem=sa_send.at[t], recv_sem=sa_recv.at[t],
            device_id=(x, y, 1 - c), device_id_type=MESH)

    def send(t):
        return pltpu.make_async_remote_copy(
            src_ref=stage.at[1], dst_ref=rb_ref.at[t], send_sem=sb_send.at[t], recv_sem=sb_recv.at[t],
            device_id=chip_of(t), device_id_type=MESH)

    for kk in range(N_DEV):
        t = kk // 2

        @pl.when(last & (k == kk))
        def _():
            if kk % 2 == 0:
                if t >= 1:
                    ship(t - 1).wait_send()
                stage[0] = acc[...].astype(BF16)
                ship(t).start()
            else:
                ship(t).wait_recv()
                fetch = pltpu.make_async_copy(ra_ref.at[t], tmp, loc_sem)
                fetch.start()
                fetch.wait()
                acc[...] += tmp[...].astype(F32)
                if t < N_CHIPS - 1:
                    if t >= 1:
                        send(t - 1).wait_send()
                    stage[1] = acc[...].astype(BF16)
                    send(t).start()
                else:
                    keep = pltpu.make_async_copy(acc, own_ref, loc_sem)
                    keep.start()
                    keep.wait()
                    ship(t).wait_send()
                    send(t - 1).wait_send()
                    for q in range(N_CHIPS - 1):
                        send(q).wait_recv()


def _scatter_scratch(rows, cols):
    return [pltpu.VMEM((rows, cols), F32), pltpu.VMEM((2, rows, cols), BF16), pltpu.VMEM((rows, cols), BF16),
            pltpu.SemaphoreType.DMA((N_CHIPS,)), pltpu.SemaphoreType.DMA((N_CHIPS,)),
            pltpu.SemaphoreType.DMA((N_CHIPS - 1,)), pltpu.SemaphoreType.DMA((N_CHIPS - 1,)), pltpu.SemaphoreType.DMA(())]


def _scatter_out(rows, cols):
    return (jax.ShapeDtypeStruct((rows, cols), F32), jax.ShapeDtypeStruct((N_CHIPS, rows, cols), BF16),
            jax.ShapeDtypeStruct((N_CHIPS - 1, rows, cols), BF16))


def _dwin_scatter(h, dzh, dza):
    s, d = h.shape
    e = dzh.shape[2]
    ts = _tile(s, 1024)
    ns = s // ts

    def body(order_ref, dzh_ref, dza_ref, h_ref, own_ref, ra_ref, rb_ref, acc, stage, tmp, *sems):
        k, step = pl.program_id(0), pl.program_id(1)

        @pl.when(step == 0)
        def _():
            acc[...] = jnp.zeros_like(acc)

        def add(dz):
            acc[...] += _dot_tn(h_ref[...], dz)

        _dz_pick(order_ref[k], dzh_ref, dza_ref, add)
        _scatter_block(k, acc, stage, tmp, own_ref, ra_ref, rb_ref, *sems, step == ns - 1)

    def dz_spec(lo):
        return pl.BlockSpec((None, ts, e), lambda k, st, o: (jnp.clip(o[k] - lo, 0, 3), st, 0))

    grid_spec = pltpu.PrefetchScalarGridSpec(
        num_scalar_prefetch=1, grid=(N_DEV, ns),
        in_specs=[dz_spec(0), dz_spec(4), pl.BlockSpec((ts, d), lambda k, st, o: (st, 0))],
        out_specs=(ANY, ANY, ANY), scratch_shapes=_scatter_scratch(d, e))
    own, _, rb = pl.pallas_call(
        body, name="dwin_scatter", grid_spec=grid_spec, out_shape=_scatter_out(d, e),
        compiler_params=_params("arbitrary", "arbitrary"))(_block_order(SCATTER_MASKS), dzh, dza, h)
    return own, rb


def _dwout_scatter(y_h, y_a, dxb):
    s, e = y_h.shape
    d = dxb.shape[1]
    r = 2 * e // N_DEV
    pairs = e // (2 * r)
    ts = _tile(s, 1024)
    ns = s // ts
    chip_masks = SCATTER_MASKS[1::2]
    passes = ((0, 1), (2,), (3,))
    slots = max(len(chips) for chips in passes)
    slot_chip = [chips[min(u, len(chips) - 1)] for chips in passes for u in range(slots)]

    def body(pair_ref, yh0_ref, ya0_ref, yh1_ref, ya1_ref, dx_ref, own_ref, ra_ref, rb_ref, acc, keep_buf, ship_buf,
             send_buf, tmp, sa_send, sa_recv, sb_send, sb_recv, loc_sem):
        p, step = pl.program_id(0), pl.program_id(1)
        x, y, c = _position()

        @pl.when(step == 0)
        def _():
            acc[...] = jnp.zeros_like(acc)

        for u, (yh_ref, ya_ref) in enumerate(((yh0_ref, ya0_ref), (yh1_ref, ya1_ref))):
            rows = slice(u * 2 * r, (u + 1) * 2 * r)
            used = functools.reduce(jnp.logical_or, [p == pp for pp, chips in enumerate(passes) if u < len(chips)])

            @pl.when(used & (pair_ref[slots * p + u] < pairs))
            def _():
                acc[rows, :] += _dot_tn(yh_ref[...], dx_ref[...])

            @pl.when(used & (pair_ref[slots * p + u] >= pairs))
            def _():
                acc[rows, :] += _dot_tn(ya_ref[...], dx_ref[...])

        def block_rows(u, core):
            return pl.ds(pl.multiple_of(u * 2 * r + core * r, r), r)

        slot_of = {q: u for chips in passes for u, q in enumerate(chips)}

        def ship(q):
            return pltpu.make_async_remote_copy(
                src_ref=ship_buf.at[slot_of[q]], dst_ref=ra_ref.at[q], send_sem=sa_send.at[q], recv_sem=sa_recv.at[q],
                device_id=(x, y, 1 - c), device_id_type=MESH)

        def send(q):
            return pltpu.make_async_remote_copy(
                src_ref=send_buf.at[slot_of[q]], dst_ref=rb_ref.at[q], send_sem=sb_send.at[q], recv_sem=sb_recv.at[q],
                device_id=_xor_peer(x, y, c, chip_masks[q]), device_id_type=MESH)

        def sibling_share(q):
            ship(q).wait_recv()
            fetch = pltpu.make_async_copy(ra_ref.at[q], tmp, loc_sem)
            fetch.start()
            fetch.wait()
            return tmp[...].astype(F32)

        shipped, sent = {}, {}
        for pp, chips in enumerate(passes):
            @pl.when((step == ns - 1) & (p == pp))
            def _():
                for u, q in enumerate(chips):
                    if u in shipped:
                        ship(shipped.pop(u)).wait_send()
                    ship_buf[u] = acc[block_rows(u, 1 - c), :].astype(BF16)
                    ship(q).start()
                    shipped[u] = q
                for u, q in enumerate(chips):
                    total = acc[block_rows(u, c), :] + sibling_share(q)
                    if q < N_CHIPS - 1:
                        if u in sent:
                            send(sent.pop(u)).wait_send()
                        send_buf[u] = total.astype(BF16)
                        send(q).start()
                        sent[u] = q
                    else:
                        keep_buf[...] = total
                        keep = pltpu.make_async_copy(keep_buf, own_ref, loc_sem)
                        keep.start()
                        keep.wait()
                if pp == len(passes) - 1:
                    for q in shipped.values():
                        ship(q).wait_send()
                    for q in sent.values():
                        send(q).wait_send()
                    for q in range(N_CHIPS - 1):
                        send(q).wait_recv()

    def y_spec(u, lo):
        return pl.BlockSpec((ts, 2 * r), lambda p, st, o: (st, jnp.clip(o[slots * p + u] - lo, 0, pairs - 1)))

    pair_of_chip = _block_order(chip_masks) // 2
    grid_spec = pltpu.PrefetchScalarGridSpec(
        num_scalar_prefetch=1, grid=(len(passes), ns),
        in_specs=[y_spec(0, 0), y_spec(0, pairs), y_spec(1, 0), y_spec(1, pairs),
                  pl.BlockSpec((ts, d), lambda p, st, o: (st, 0))],
        out_specs=(ANY, ANY, ANY),
        scratch_shapes=[pltpu.VMEM((slots * 2 * r, d), F32), pltpu.VMEM((r, d), F32),
                        pltpu.VMEM((slots, r, d), BF16)] + _scatter_scratch(r, d)[1:])
    own, _, rb = pl.pallas_call(
        body, name="dwout_scatter", grid_spec=grid_spec, out_shape=_scatter_out(r, d),
        compiler_params=_params("arbitrary", "arbitrary"))(
            jnp.stack([pair_of_chip[q] for q in slot_chip]), y_h, y_a, y_h, y_a, dxb)
    return own, rb


def _sum_chips_adamw(own, recv, w, m, v):
    r, c = w.shape
    tr = _tile(r, 128)

    def body(own_ref, rc_ref, w_ref, m_ref, v_ref, g_ref, d_ref, mo_ref, vo_ref):
        g = own_ref[...]
        for q in range(N_CHIPS - 1):
            g = g + rc_ref[q].astype(F32)
        g_ref[...] = g
        d_ref[...], mo_ref[...], vo_ref[...] = _adamw(w_ref[...], g, m_ref[...], v_ref[...])

    blk = pl.BlockSpec((tr, c), lambda i: (i, 0))
    shp = jax.ShapeDtypeStruct((r, c), F32)
    return pl.pallas_call(
        body, name="sum_chips_adamw", grid=(r // tr,), out_shape=(shp, shp, shp, shp),
        in_specs=[blk, pl.BlockSpec((N_CHIPS - 1, tr, c), lambda i: (0, i, 0)), blk, blk, blk],
        out_specs=(blk, blk, blk, blk), compiler_params=_params("parallel"))(own, recv, w, m, v)


SMALL_ROWS = 8
ROW_LB = 4
ROW_GN = 6
ROW_LOSS = 7


def _small_allreduce_adamw(part, w, m, v, lb_logits):
    width = part.shape[1]

    def body(p_ref, w_ref, m_ref, v_ref, lb_ref, g_ref, d_ref, mo_ref, vo_ref, buf, send_sems, recv_sems):
        x, y, c = _position()
        me = 4 * x + 2 * y + c
        buf[me] = p_ref[...]
        copies = []
        for k in range(N_DEV - 1):
            bx, by, bc = ((k + 1) >> 2) & 1, ((k + 1) >> 1) & 1, (k + 1) & 1
            peer = (x ^ bx, y ^ by, c ^ bc)
            copies.append(pltpu.make_async_remote_copy(
                src_ref=p_ref, dst_ref=buf.at[me], send_sem=send_sems.at[k], recv_sem=recv_sems.at[k],
                device_id=peer, device_id_type=MESH))
        for cp in copies:
            cp.start()
        for cp in copies:
            cp.wait_recv()
        for cp in copies:
            cp.wait_send()
        tot = buf[0]
        for dev in range(1, N_DEV):
            tot = tot + buf[dev]
        lbv = lb_ref[...]
        lb = _sigmoid(lbv[0:1] - lbv[1:2])
        glb = tot[ROW_LB:ROW_LB + 1] * lb * (1.0 - lb)
        loss = jnp.sum(tot[ROW_LOSS:ROW_LOSS + 1], axis=-1, keepdims=True)
        row = lax.broadcasted_iota(jnp.int32, (SMALL_ROWS, width), 0)
        g = jnp.where(row == ROW_LB, glb, jnp.where(row == ROW_LB + 1, -glb, tot))
        g = jnp.where(row == ROW_LOSS, loss, g)
        g_ref[...] = g
        d_ref[...], mo_ref[...], vo_ref[...] = _adamw(w_ref[...], g, m_ref[...], v_ref[...])

    vm = pl.BlockSpec(memory_space=pltpu.VMEM)
    shp = jax.ShapeDtypeStruct((SMALL_ROWS, width), F32)
    return pl.pallas_call(
        body, name="small_allreduce_adamw", out_shape=(shp, shp, shp, shp),
        in_specs=[vm] * 5, out_specs=(vm, vm, vm, vm),
        scratch_shapes=[pltpu.VMEM((N_DEV, SMALL_ROWS, width), F32), pltpu.SemaphoreType.DMA((N_DEV - 1,)),
                        pltpu.SemaphoreType.DMA((N_DEV - 1,))],
    )(part, w, m, v, lb_logits)


def _pack_small(norm_gain, final_gain, lb2, gnorm, last_row, width):
    pad = lambda a: jnp.pad(a.reshape(1, -1), ((0, 0), (0, width - a.size)))
    return jnp.concatenate([norm_gain.reshape(2, width), final_gain.reshape(2, width), lb2.reshape(2, width),
                            pad(gnorm), last_row.reshape(1, width)], axis=0)


def _unpack_small(p, d, e, hd):
    return (p[0:2].reshape(1, d), p[2:4].reshape(d), p[4:6].reshape(2, e), p[6:7, :hd].reshape(1, hd))


def kernel(x, norm_gain, w_in, lb_logits, hgrn_gnorm, w_out, final_gain, loss_target, m_norm_gain, m_w_in, m_lb_logits, m_hgrn_gnorm, m_w_out, m_final_gain, v_norm_gain, v_w_in, v_lb_logits, v_hgrn_gnorm, v_w_out, v_final_gain):
    s, d = x.shape[1], x.shape[2]
    e = w_in.shape[2]
    assert d == 2 * e and lb_logits.shape == (2, e) and w_out.shape[1] * N_DEV == 2 * e
    x2d = x.reshape(s, d)
    tgt = loss_target.reshape(s, d)

    h = _rmsnorm_fwd(x2d, norm_gain)
    z, w_in_full, w_out_full = _inproj_gather(h, _cast_bf16(w_in[0]), _cast_bf16(w_out[0]))
    w_out_full = w_out_full.reshape(2 * e, d)
    y_h, states = _hgrn_fwd(z, lb_logits, hgrn_gnorm)
    o_attn, lse, y_a = _attn_fwd(z)
    dx2, dx2b, dy, loss_vec, dfg = _outproj_loss(x2d, y_h, y_a, w_out_full, final_gain.reshape(1, d), tgt)

    own_o, recv_o = _dwout_scatter(y_h, y_a, dx2b)
    dza = _attn_bwd(z, dy, o_attn, lse)
    dzh, dlb, dgn = _hgrn_bwd(z, dy, states, lb_logits, hgrn_gnorm)
    grad_x, dng = _dh_dx(dzh, dza, w_in_full, x2d, norm_gain, dx2)
    g_wo, d_wo, nm_wo, nv_wo = _sum_chips_adamw(own_o, recv_o, w_out[0], m_w_out[0], v_w_out[0])

    width = d // 2
    zero_row = jnp.zeros((1, width), F32)
    loss_row = loss_vec[:, :width] + loss_vec[:, width:]
    part = _pack_small(dng, dfg, jnp.concatenate([dlb, zero_row], axis=0), dgn, loss_row, width)
    pw = _pack_small(norm_gain, final_gain, lb_logits, hgrn_gnorm, zero_row, width)
    pm = _pack_small(m_norm_gain, m_final_gain, m_lb_logits, m_hgrn_gnorm, zero_row, width)
    pv = _pack_small(v_norm_gain, v_final_gain, v_lb_logits, v_hgrn_gnorm, zero_row, width)
    sg, sd, sm, sv = _small_allreduce_adamw(part, pw, pm, pv, lb_logits)
    own_i, recv_i = _dwin_scatter(h, dzh, dza)
    g_wi, d_wi, nm_wi, nv_wi = _sum_chips_adamw(own_i, recv_i, w_in[0], m_w_in[0], v_w_in[0])
    hd = hgrn_gnorm.shape[1]
    g_ng, g_fg, g_lb, g_gn = _unpack_small(sg, d, e, hd)
    d_ng, d_fg, d_lb, d_gn = _unpack_small(sd, d, e, hd)
    m_ng, m_fg, m_lb, m_gn = _unpack_small(sm, d, e, hd)
    v_ng, v_fg, v_lb, v_gn = _unpack_small(sv, d, e, hd)
    loss = sg[ROW_LOSS, 0]

    one = lambda a: a[None]
    return (loss, grad_x.reshape(1, s, d), g_ng, one(g_wi), g_lb, g_gn, one(g_wo), g_fg,
            d_ng, one(d_wi), d_lb, d_gn, one(d_wo), d_fg,
            m_ng, one(nm_wi), m_lb, m_gn, one(nm_wo), m_fg,
            v_ng, one(nv_wi), v_lb, v_gn, one(nv_wo), v_fg)
```

```python
import functools
import math

import jax
import jax.numpy as jnp
from jax import lax
from jax.experimental import pallas as pl
from jax.experimental.pallas import tpu as pltpu

NORM_EPS = 1e-6
HGRN_HEAD = 128
HGRN_CHUNK = 64
ATTN_HEAD = 64
ATTN_BAND = 128
DILATIONS = (1, 4, 16)
N_SPLITS = 8
N_DEV = 8
ADAM_LR = 0.001
ADAM_B1 = 0.9
ADAM_B2 = 0.999
ADAM_EPS = 1e-08
ADAM_WD = 0.01
ADAM_STEP = 10
LANES = 128
MESH = pl.DeviceIdType.MESH
F32 = jnp.float32
BF16 = jnp.bfloat16
NEG_BIG = -1e30
VMEM_LIMIT = 56 * 1024 * 1024

ANY = pl.BlockSpec(memory_space=pl.ANY)


def _params(*sem):
    return pltpu.CompilerParams(dimension_semantics=sem, vmem_limit_bytes=VMEM_LIMIT)


def _tile(n, pref):
    t = min(n, pref)
    assert n % t == 0, (n, pref)
    return t


def _dot(a, b, precision=None):
    return jnp.dot(a, b, preferred_element_type=F32, precision=precision)


def _dot_nt(a, b):
    return lax.dot_general(a, b, (((1,), (1,)), ((), ())), preferred_element_type=F32)


def _dot_tn(a, b):
    return lax.dot_general(a, b, (((0,), (0,)), ((), ())), preferred_element_type=F32)


def _sigmoid(x):
    return 0.5 * jnp.tanh(0.5 * x) + 0.5


def _dsilu(x, s):
    return s * (1.0 + x * (1.0 - s))


def _adamw(w, g, m, v):
    m = ADAM_B1 * m + (1.0 - ADAM_B1) * g
    v = ADAM_B2 * v + (1.0 - ADAM_B2) * (g * g)
    m_hat = m / (1.0 - ADAM_B1 ** ADAM_STEP)
    v_hat = v / (1.0 - ADAM_B2 ** ADAM_STEP)
    delta = -ADAM_LR * (m_hat / (jnp.sqrt(v_hat) + ADAM_EPS) + ADAM_WD * w)
    return delta, m, v


def _cast_bf16(a):
    r, c = a.shape
    tr = _tile(r, 256)

    def body(a_ref, o_ref):
        o_ref[...] = a_ref[...].astype(BF16)

    return pl.pallas_call(
        body, name="cast_bf16", grid=(r // tr,), out_shape=jax.ShapeDtypeStruct((r, c), BF16),
        in_specs=[pl.BlockSpec((tr, c), lambda i: (i, 0))], out_specs=pl.BlockSpec((tr, c), lambda i: (i, 0)),
        compiler_params=_params("parallel"))(a)


def _rmsnorm_fwd(x, gain):
    s, d = x.shape
    tm = _tile(s, 512)

    def body(x_ref, g_ref, h_ref):
        xv = x_ref[...]
        r = lax.rsqrt(jnp.mean(xv * xv, axis=-1, keepdims=True) + NORM_EPS)
        h_ref[...] = (xv * r * g_ref[...]).astype(BF16)

    return pl.pallas_call(
        body, name="rmsnorm_fwd", grid=(s // tm,), out_shape=jax.ShapeDtypeStruct((s, d), BF16),
        in_specs=[pl.BlockSpec((tm, d), lambda i: (i, 0)), pl.BlockSpec((1, d), lambda i: (0, 0))],
        out_specs=pl.BlockSpec((tm, d), lambda i: (i, 0)), compiler_params=_params("parallel"))(x, gain)


HGRN_BLOCK = 2048
TRI_ROWS = 256


def _chunk_masks():
    tb = TRI_ROWS
    row = lax.broadcasted_iota(jnp.int32, (tb, tb), 0)
    col = lax.broadcasted_iota(jnp.int32, (tb, tb), 1)
    same = (row // HGRN_CHUNK) == (col // HGRN_CHUNK)
    lower = jnp.where(same & (col <= row), 1.0, 0.0).astype(BF16)
    upper = jnp.where(same & (col >= row), 1.0, 0.0).astype(BF16)
    return lower, upper


def _split3(a):
    hi = a.astype(BF16).astype(F32)
    mid = (a - hi).astype(BF16).astype(F32)
    lo = (a - hi - mid).astype(BF16).astype(F32)
    return hi, mid, lo


def _tri_dot(tri, x):
    hi, mid, lo = (p.astype(BF16) for p in _split3(x))
    outs = []
    for r in range(0, x.shape[0], TRI_ROWS):
        sl = slice(r, r + TRI_ROWS)
        outs.append(_dot(tri, hi[sl]) + _dot(tri, mid[sl]) + _dot(tri, lo[sl]))
    return outs[0] if len(outs) == 1 else jnp.concatenate(outs, axis=0)


def _hgrn_gates(qp, fp, lbv):
    lb = _sigmoid(lbv[0:1] - lbv[1:2])
    sq = _sigmoid(qp)
    q = qp * sq
    sg = _sigmoid(fp)
    f = lb + (1.0 - lb) * sg
    k = 1.0 - f
    return lb, sq, q, sg, f, k


def _hgrn_fwd(z, lb_logits, gnorm):
    s = z.shape[0]
    e = z.shape[1] // N_SPLITS
    nh = e // HGRN_HEAD
    tb = _tile(s, HGRN_BLOCK)
    nc = tb // HGRN_CHUNK
    nb = s // tb
    C = HGRN_CHUNK

    def body(q_ref, f_ref, i_ref, g_ref, lb_ref, gn_ref, y_ref, st_ref, state, o_scr):
        @pl.when(pl.program_id(1) == 0)
        def _():
            state[...] = jnp.zeros_like(state)

        lb, sq, q, sg, f, k = _hgrn_gates(q_ref[...], f_ref[...], lb_ref[...])
        lower, _ = _chunk_masks()
        b = _tri_dot(lower, jnp.log(f))
        b3 = b.reshape(nc, C, HGRN_HEAD)
        bc = b3[:, C - 1:C, :]
        qt = (q * jnp.exp(b)).astype(BF16)
        kt = (k * jnp.exp(-b)).astype(BF16)
        ke = (k.reshape(nc, C, HGRN_HEAD) * jnp.exp(bc - b3)).reshape(tb, HGRN_HEAD).astype(BF16)
        v = i_ref[...].astype(BF16)
        tri = lax.broadcasted_iota(jnp.int32, (C, C), 1) <= lax.broadcasted_iota(jnp.int32, (C, C), 0)
        sls = [slice(c * C, (c + 1) * C) for c in range(nc)]
        kv = [_dot_tn(v[sl], ke[sl]) for sl in sls]
        a = [jnp.where(tri, _dot_nt(qt[sl], kt[sl]), 0.0).astype(BF16) for sl in sls]
        st = state[...]
        sts = []
        for c in range(nc):
            sts.append(st)
            st_ref[c] = st
            st = st * jnp.exp(bc[c]) + kv[c]
        state[...] = st
        for c, sl in enumerate(sls):
            o_scr[sl, :] = _dot(a[c], v[sl]) + _dot_nt(qt[sl], sts[c].astype(BF16))
        o = o_scr[...]
        rms = lax.rsqrt(jnp.mean(o * o, axis=-1, keepdims=True) + NORM_EPS)
        gp = g_ref[...]
        y_ref[...] = (o * rms * gn_ref[...] * (gp * _sigmoid(gp))).astype(BF16)

    col = lambda kk: (lambda h, n: (n, kk * nh + h))
    return pl.pallas_call(
        body, name="hgrn_fwd", grid=(nh, nb),
        out_shape=(jax.ShapeDtypeStruct((s, e), BF16),
                   jax.ShapeDtypeStruct((nh, s // C, HGRN_HEAD, HGRN_HEAD), F32)),
        in_specs=[pl.BlockSpec((tb, HGRN_HEAD), col(0)), pl.BlockSpec((tb, HGRN_HEAD), col(1)),
                  pl.BlockSpec((tb, HGRN_HEAD), col(2)), pl.BlockSpec((tb, HGRN_HEAD), col(3)),
                  pl.BlockSpec((2, HGRN_HEAD), lambda h, n: (0, h)), pl.BlockSpec((1, HGRN_HEAD), lambda h, n: (0, 0))],
        out_specs=(pl.BlockSpec((tb, HGRN_HEAD), lambda h, n: (n, h)),
                   pl.BlockSpec((None, nc, HGRN_HEAD, HGRN_HEAD), lambda h, n: (h, n, 0, 0))),
        scratch_shapes=[pltpu.VMEM((HGRN_HEAD, HGRN_HEAD), F32), pltpu.VMEM((tb, HGRN_HEAD), F32)],
        compiler_params=_params("parallel", "arbitrary"))(z, z, z, z, lb_logits, gnorm)


def _hgrn_bwd(z, dy, states, lb_logits, gnorm):
    s = z.shape[0]
    e = z.shape[1] // N_SPLITS
    nh = e // HGRN_HEAD
    tb = _tile(s, HGRN_BLOCK)
    nc = tb // HGRN_CHUNK
    nb = s // tb
    C = HGRN_CHUNK
    H = HGRN_HEAD

    def body(q_ref, f_ref, i_ref, g_ref, dy_ref, st_ref, lb_ref, gn_ref, dz_ref, dlb_ref, dgn_ref,
             gstate, o_scr, dq_scr, dk_scr, dv_scr, e_scr):
        first = (pl.program_id(0) == 0) & (pl.program_id(1) == 0)

        @pl.when(first)
        def _():
            dgn_ref[...] = jnp.zeros_like(dgn_ref)

        @pl.when(pl.program_id(1) == 0)
        def _():
            gstate[...] = jnp.zeros_like(gstate)
            dlb_ref[...] = jnp.zeros_like(dlb_ref)

        qp = q_ref[...]
        lb, sq, q, sg, f, k = _hgrn_gates(qp, f_ref[...], lb_ref[...])
        lower, upper = _chunk_masks()
        b = _tri_dot(lower, jnp.log(f))
        b3 = b.reshape(nc, C, H)
        bc = b3[:, C - 1:C, :]
        eb = jnp.exp(b)
        enb = jnp.exp(-b)
        eend = jnp.exp(bc - b3).reshape(tb, H)
        qt = (q * eb).astype(BF16)
        kt = (k * enb).astype(BF16)
        ke = (k * eend).astype(BF16)
        v = i_ref[...].astype(BF16)
        tri = lax.broadcasted_iota(jnp.int32, (C, C), 1) <= lax.broadcasted_iota(jnp.int32, (C, C), 0)
        sls = [slice(c * C, (c + 1) * C) for c in range(nc)]
        a = [jnp.where(tri, _dot_nt(qt[sl], kt[sl]), 0.0).astype(BF16) for sl in sls]
        for c, sl in enumerate(sls):
            o_scr[sl, :] = _dot(a[c], v[sl]) + _dot_nt(qt[sl], st_ref[c].astype(BF16))
        o = o_scr[...]
        rms = lax.rsqrt(jnp.mean(o * o, axis=-1, keepdims=True) + NORM_EPS)
        on = o * rms
        gn = gn_ref[...]
        gp = g_ref[...]
        sgg = _sigmoid(gp)
        dyv = dy_ref[...]
        d_on = dyv * (gp * sgg)
        dz_ref[3] = (dyv * on * gn * _dsilu(gp, sgg)).astype(BF16)
        dgn_ref[...] += jnp.sum(d_on * on, axis=0, keepdims=True)
        u = d_on * gn
        do = (rms * (u - on * jnp.mean(u * on, axis=-1, keepdims=True))).astype(BF16)
        gup = [_dot_tn(do[sl], qt[sl]) for sl in sls]
        da = [jnp.where(tri, _dot_nt(do[sl], v[sl]), 0.0).astype(BF16) for sl in sls]
        gt = gstate[...]
        gts = [None] * nc
        for c in reversed(range(nc)):
            gts[c] = gt
            gt = gt * jnp.exp(bc[c]) + gup[c]
        gstate[...] = gt
        for c, sl in enumerate(sls):
            stp = st_ref[c]
            gtb = gts[c].astype(BF16)
            dqt = _dot(da[c], kt[sl]) + _dot(do[sl], stp.astype(BF16))
            dkt = _dot_tn(da[c], qt[sl])
            dks = _dot(v[sl], gtb) * eend[sl]
            dv_scr[sl, :] = _dot_tn(a[c], do[sl]) + _dot_nt(ke[sl], gtb)
            dq_scr[sl, :] = dqt * eb[sl]
            dk_scr[sl, :] = dkt * enb[sl] + dks
            ech = (jnp.sum(k[sl] * dks, axis=0, keepdims=True)
                   + jnp.sum(gts[c] * jnp.exp(bc[c]) * stp, axis=0, keepdims=True))
            e_scr[sl, :] = jnp.broadcast_to(ech, (C, H))
        dq = dq_scr[...]
        dk = dk_scr[...]
        dlf = _tri_dot(upper, q * dq - k * dk) + e_scr[...]
        dft = dlf / f - dk
        dz_ref[0] = (dq * _dsilu(qp, sq)).astype(BF16)
        dz_ref[1] = (dft * (1.0 - lb) * sg * (1.0 - sg)).astype(BF16)
        dz_ref[2] = dv_scr[...].astype(BF16)
        dlb_ref[...] += jnp.sum(dft * (1.0 - sg), axis=0, keepdims=True)

    col = lambda kk: (lambda h, n: (nb - 1 - n, kk * nh + h))
    return pl.pallas_call(
        body, name="hgrn_bwd", grid=(nh, nb),
        out_shape=(jax.ShapeDtypeStruct((4, s, e), BF16), jax.ShapeDtypeStruct((1, e), F32),
                   jax.ShapeDtypeStruct((1, H), F32)),
        in_specs=[pl.BlockSpec((tb, H), col(0)), pl.BlockSpec((tb, H), col(1)),
                  pl.BlockSpec((tb, H), col(2)), pl.BlockSpec((tb, H), col(3)),
                  pl.BlockSpec((tb, H), lambda h, n: (nb - 1 - n, h)),
                  pl.BlockSpec((None, nc, H, H), lambda h, n: (h, nb - 1 - n, 0, 0)),
                  pl.BlockSpec((2, H), lambda h, n: (0, h)), pl.BlockSpec((1, H), lambda h, n: (0, 0))],
        out_specs=(pl.BlockSpec((4, tb, H), lambda h, n: (0, nb - 1 - n, h)),
                   pl.BlockSpec((1, H), lambda h, n: (0, h)), pl.BlockSpec((1, H), lambda h, n: (0, 0))),
        scratch_shapes=[pltpu.VMEM((H, H), F32)] + [pltpu.VMEM((tb, H), F32)] * 5,
        compiler_params=_params("arbitrary", "arbitrary"))(z, z, z, z, dy, states, lb_logits, gnorm)


ATTN_T = 16 * ATTN_BAND
SCALE = ATTN_HEAD ** -0.5
TILE_UNROLL = 2


def _slope(hh, nheads):
    head = (2 * pl.program_id(0) + hh + 1).astype(F32)
    return jnp.exp(jnp.full((1, 1), -8.0 / nheads * math.log(2.0), F32) * head)


def _fill_bias(bias, nheads, delta, edge_ok):
    band = (delta >= 0) & (delta <= ATTN_BAND)
    dist = delta.astype(F32)
    for pi, dil in enumerate(DILATIONS):
        for hh in range(2):
            full = jnp.where(band, -(_slope(hh, nheads) * float(dil)) * dist, NEG_BIG)
            bias[(pi * 2 + hh) * 2] = full
            bias[(pi * 2 + hh) * 2 + 1] = jnp.where(edge_ok, full, NEG_BIG)


def _rows(start, size, stride):
    if stride == 1:
        return pl.ds(pl.multiple_of(start, ATTN_BAND), size)
    return pl.ds(start, size, stride=stride)


def _head_lanes(rows, hh):
    return (lax.broadcasted_iota(jnp.int32, (rows, LANES), 1) // ATTN_HEAD) == hh


def _attn_fwd(z):
    s = z.shape[0]
    e = z.shape[1] // N_SPLITS
    npair = e // LANES
    T = ATTN_T
    assert s % T == 0
    nsb = s // T
    W = ATTN_BAND
    nt = T // W
    HD = ATTN_HEAD
    chunk = 256

    def body(q_ref, kp_ref, kc_ref, vp_ref, vc_ref, g_ref, o_ref, l_ref, y_ref, qa, kbuf, va, bias, accs, ms, lsw):
        sb = pl.program_id(1)
        def stage(i, carry):
            rows = pl.ds(pl.multiple_of(i * chunk, chunk), chunk)
            upper = pl.ds(pl.multiple_of(T + i * chunk, chunk), chunk)
            kbuf[upper, :] = kc_ref[rows, :]
            for hh in range(2):
                mine = _head_lanes(chunk, hh)
                qa[hh, rows, :] = jnp.where(mine, q_ref[rows, :] * SCALE, 0.0)
                va[hh, upper, :] = jnp.where(mine, vc_ref[rows, :], 1.0)
            return carry

        lax.fori_loop(0, T // chunk, stage, 0)

        @pl.when(sb == 0)
        def _():
            def stage_prev(i, carry):
                rows = pl.ds(pl.multiple_of(i * chunk, chunk), chunk)
                kbuf[rows, :] = kp_ref[rows, :]
                for hh in range(2):
                    va[hh, rows, :] = jnp.where(_head_lanes(chunk, hh), vp_ref[rows, :], 1.0)
                return carry

            lax.fori_loop(0, T // chunk, stage_prev, 0)
        qi = lax.broadcasted_iota(jnp.int32, (W, 2 * W), 0)
        kj = lax.broadcasted_iota(jnp.int32, (W, 2 * W), 1)
        _fill_bias(bias, 2 * npair, W + qi - kj, kj >= W)

        def tile(tau, carry):
            first = _head_lanes(W, 0)
            rows, scores = [], []
            for pi, dil in enumerate(DILATIONS):
                r = tau % dil
                ub = tau // dil
                qrows = _rows(r + dil * W * ub, W, dil)
                krows = _rows(T + dil * W * (ub - 1) + r, 2 * W, dil)
                var = jnp.where((sb == 0) & (ub == 0), 1, 0)
                kt = kbuf[krows, :].astype(BF16)
                rows.append((qrows, krows))
                scores.append([_dot_nt(qa[hh, qrows, :].astype(BF16), kt) + bias[(pi * 2 + hh) * 2 + var]
                               for hh in range(2)])
            maxes = [[jnp.max(sc, axis=-1, keepdims=True) for sc in pair] for pair in scores]
            probs = [[jnp.exp(sc - m).astype(BF16) for sc, m in zip(ps, pm)] for ps, pm in zip(scores, maxes)]
            for pi, (qrows, krows) in enumerate(rows):
                outs = [_dot(probs[pi][hh], va[hh, krows, :].astype(BF16)) for hh in range(2)]
                accs[pi, qrows, :] = jnp.where(first, outs[0], outs[1])
                lsw[pi, qrows, :] = jnp.where(first, outs[1], outs[0])
                ms[pi, qrows, :] = jnp.where(first, maxes[pi][0], maxes[pi][1])
            return carry

        lax.fori_loop(0, nt, tile, 0, unroll=TILE_UNROLL)

        def merge(i, carry):
            rows = pl.ds(pl.multiple_of(i * chunk, chunk), chunk)
            m1, m2, m3 = ms[0, rows, :], ms[1, rows, :], ms[2, rows, :]
            mx = jnp.maximum(jnp.maximum(m1, m2), m3)
            w1, w2, w3 = jnp.exp(m1 - mx), jnp.exp(m2 - mx), jnp.exp(m3 - mx)
            unswap = lambda a: pltpu.roll(a, ATTN_HEAD, 1)
            den = w1 * unswap(lsw[0, rows, :]) + w2 * unswap(lsw[1, rows, :]) + w3 * unswap(lsw[2, rows, :])
            o = (w1 * accs[0, rows, :] + w2 * accs[1, rows, :] + w3 * accs[2, rows, :]) / den
            o_ref[rows, :] = o
            l_ref[rows, :] = mx + jnp.log(den)
            gp = g_ref[rows, :]
            y_ref[rows, :] = (o * (gp * _sigmoid(gp))).astype(BF16)
            upper = pl.ds(pl.multiple_of(T + i * chunk, chunk), chunk)
            kbuf[rows, :] = kbuf[upper, :]
            for hh in range(2):
                va[hh, rows, :] = va[hh, upper, :]
            return carry

        lax.fori_loop(0, T // chunk, merge, 0)

    cur = lambda split: (lambda hp, sb: (sb, split * npair + hp))
    prev = lambda split: (lambda hp, sb: (0, split * npair + hp))
    blk = lambda index: pl.BlockSpec((T, LANES), index)
    out = blk(lambda hp, sb: (sb, hp))
    buf = lambda rows: pltpu.VMEM((rows, LANES), F32)
    return pl.pallas_call(
        body, name="attn_fwd", grid=(npair, nsb),
        out_shape=(jax.ShapeDtypeStruct((s, e), F32), jax.ShapeDtypeStruct((s, e), F32), jax.ShapeDtypeStruct((s, e), BF16)),
        in_specs=[blk(cur(4)), blk(prev(5)), blk(cur(5)), blk(prev(6)), blk(cur(6)), blk(cur(7))],
        out_specs=(out, out, out),
        scratch_shapes=[pltpu.VMEM((2, T, LANES), F32), buf(2 * T), pltpu.VMEM((2, 2 * T, LANES), F32),
                        pltpu.VMEM((12, W, 2 * W), F32)] + [pltpu.VMEM((3, T, LANES), F32)] * 3,
        compiler_params=_params("parallel", "arbitrary"))(z, z, z, z, z, z)


def _outproj_loss(x, y_h, y_a, w_out_full, final_gain, target):
    s, d = x.shape
    e = y_h.shape[1]
    tm = _tile(s, 256)

    def body(x_ref, yh_ref, ya_ref, w_ref, g_ref, t_ref, dx_ref, dxb_ref, dy_ref, loss_ref, dg_ref):
        @pl.when(pl.program_id(0) == 0)
        def _():
            loss_ref[...] = jnp.zeros_like(loss_ref)
            dg_ref[...] = jnp.zeros_like(dg_ref)

        w = w_ref[...]
        x2 = x_ref[...] + _dot(yh_ref[...], w[0:e]) + _dot(ya_ref[...], w[e:2 * e])
        r = lax.rsqrt(jnp.mean(x2 * x2, axis=-1, keepdims=True) + NORM_EPS)
        xn = x2 * r
        g = g_ref[...]
        err = xn * g - t_ref[...]
        loss_ref[...] += jnp.sum(err * err, axis=0, keepdims=True) * (0.5 / d)
        dyo = err * (1.0 / d)
        dg_ref[...] += jnp.sum(dyo * xn, axis=0, keepdims=True)
        u = dyo * g
        dx2 = r * (u - xn * jnp.mean(u * xn, axis=-1, keepdims=True))
        dx_ref[...] = dx2
        dxb = dx2.astype(BF16)
        dxb_ref[...] = dxb
        dy_ref[...] = _dot_nt(dxb, w)

    row = pl.BlockSpec((tm, d), lambda i: (i, 0))
    half = pl.BlockSpec((tm, e), lambda i: (i, 0))
    vec = pl.BlockSpec((1, d), lambda i: (0, 0))
    return pl.pallas_call(
        body, name="outproj_loss", grid=(s // tm,),
        out_shape=(jax.ShapeDtypeStruct((s, d), F32), jax.ShapeDtypeStruct((s, d), BF16),
                   jax.ShapeDtypeStruct((s, 2 * e), F32), jax.ShapeDtypeStruct((1, d), F32),
                   jax.ShapeDtypeStruct((1, d), F32)),
        in_specs=[row, half, half, pl.BlockSpec((2 * e, d), lambda i: (0, 0)), vec, row],
        out_specs=(row, row, pl.BlockSpec((tm, 2 * e), lambda i: (i, 0)), vec, vec),
        compiler_params=_params("arbitrary"))(x, y_h, y_a, w_out_full, final_gain, target)


def _attn_bwd(z, dy, o, lse):
    s, e = o.shape
    npair = e // LANES
    T = ATTN_T
    assert s % T == 0
    nsb = s // T
    W = ATTN_BAND
    nt = T // W
    HD = ATTN_HEAD
    chunk = 256

    def body(k_ref, v_ref, qc_ref, qn_ref, dyc_ref, dyn_ref, gc_ref, gn_ref, oc_ref, on_ref, lc_ref, ln_ref,
             dz_ref, qa, doa, ka, va, dqacc, dkacc, dvacc, bias):
        sb = pl.program_id(1)
        def stage_queries(half, q_r, dy_r, g_r, o_r, l_r):
            def stage(i, carry):
                rows = pl.ds(pl.multiple_of(i * chunk, chunk), chunk)
                dst = pl.ds(pl.multiple_of(half * T + i * chunk, chunk), chunk)
                lane = lax.broadcasted_iota(jnp.int32, (chunk, LANES), 1)
                gp = g_r[rows, :]
                dov = dy_r[rows, :] * (gp * _sigmoid(gp))
                qv = q_r[rows, :] * SCALE
                same_head = (lax.broadcasted_iota(jnp.int32, (LANES, LANES), 0) // HD
                             == lax.broadcasted_iota(jnp.int32, (LANES, LANES), 1) // HD)
                ones = jnp.where(same_head, 1.0, 0.0).astype(BF16)
                hi, mid, lo = (p.astype(BF16) for p in _split3(dov * o_r[rows, :]))
                delta = _dot(hi, ones) + _dot(mid, ones) + _dot(lo, ones)
                swap = lambda a: pltpu.roll(a, HD, 1)
                lse_parts = [swap(p) for p in _split3(l_r[rows, :])]
                dl_parts = [swap(p) for p in _split3(delta)]
                for hh in range(2):
                    mine = _head_lanes(chunk, hh)
                    spare = (1 - hh) * HD
                    qh = jnp.where(mine, qv, 0.0)
                    dh = jnp.where(mine, dov, 0.0)
                    for j in range(3):
                        qh = jnp.where(lane == spare + j, lse_parts[j], qh)
                        dh = jnp.where(lane == spare + j, dl_parts[j], dh)
                    qa[hh, dst, :] = qh
                    doa[hh, dst, :] = dh
                return carry

            lax.fori_loop(0, T // chunk, stage, 0)

        @pl.when(sb == 0)
        def _():
            stage_queries(0, qc_ref, dyc_ref, gc_ref, oc_ref, lc_ref)

        stage_queries(1, qn_ref, dyn_ref, gn_ref, on_ref, ln_ref)

        def stage_keys(i, carry):
            rows = pl.ds(pl.multiple_of(i * chunk, chunk), chunk)
            lane = lax.broadcasted_iota(jnp.int32, (chunk, LANES), 1)
            for hh in range(2):
                spare = (1 - hh) * HD
                minus = (lane >= spare) & (lane < spare + 3)
                ka[hh, rows, :] = jnp.where(minus, -1.0, k_ref[rows, :])
                va[hh, rows, :] = jnp.where(minus, -1.0, v_ref[rows, :])
            gp = gc_ref[rows, :]
            dz_ref[3, rows, :] = (dyc_ref[rows, :] * oc_ref[rows, :] * _dsilu(gp, _sigmoid(gp))).astype(BF16)
            return carry

        lax.fori_loop(0, T // chunk, stage_keys, 0)

        @pl.when(sb == 0)
        def _():
            dqacc[0:T, :] = jnp.zeros((T, LANES), F32)

        dqacc[T:, :] = jnp.zeros((T, LANES), F32)
        dkacc[...] = jnp.zeros_like(dkacc)
        dvacc[...] = jnp.zeros_like(dvacc)
        qi = lax.broadcasted_iota(jnp.int32, (2 * W, W), 0)
        kj = lax.broadcasted_iota(jnp.int32, (2 * W, W), 1)
        _fill_bias(bias, 2 * npair, qi - kj, qi < W)

        def tile(tau, carry):
            def scores(step, pi):
                dil = DILATIONS[pi]
                r = step % dil
                ub = step // dil
                start = r + dil * W * ub
                krows = _rows(start, W, dil)
                qrows = _rows(start, 2 * W, dil)
                var = jnp.where((sb == nsb - 1) & (ub == nt // dil - 1), 1, 0)
                unit = dict(krows=krows, qrows=qrows, ops=[], sc=[], dpd=[])
                for hh in range(2):
                    kt = ka[hh, krows, :].astype(BF16)
                    vt = va[hh, krows, :].astype(BF16)
                    qt = qa[hh, qrows, :].astype(BF16)
                    dt = doa[hh, qrows, :].astype(BF16)
                    unit["ops"].append((kt, qt, dt))
                    unit["sc"].append(_dot_nt(qt, kt) + bias[(pi * 2 + hh) * 2 + var])
                    unit["dpd"].append(_dot_nt(dt, vt))
                return unit

            def elementwise(unit):
                ps = [jnp.exp(s_) for s_ in unit["sc"]]
                unit["ds"] = [(p * d).astype(BF16) for p, d in zip(ps, unit["dpd"])]
                unit["pb"] = [p.astype(BF16) for p in ps]

            def products(unit):
                dvs = [_dot_tn(pb, dt) for pb, (kt, qt, dt) in zip(unit["pb"], unit["ops"])]
                dks = [_dot_tn(ds, qt) for ds, (kt, qt, dt) in zip(unit["ds"], unit["ops"])]
                dqs = [_dot(ds, kt) for ds, (kt, qt, dt) in zip(unit["ds"], unit["ops"])]
                dkacc[unit["krows"], :] += jnp.where(_head_lanes(W, 0), dks[0], dks[1])
                dvacc[unit["krows"], :] += jnp.where(_head_lanes(W, 0), dvs[0], dvs[1])
                dqacc[unit["qrows"], :] += jnp.where(_head_lanes(2 * W, 0), dqs[0], dqs[1]) * SCALE

            order = [(2 * tau + half, pi) for half in range(2) for pi in range(len(DILATIONS))]
            units = [None] * len(order)
            for n in range(len(order) + 2):
                if n < len(order):
                    units[n] = scores(*order[n])
                if 1 <= n <= len(order):
                    elementwise(units[n - 1])
                if n >= 2:
                    products(units[n - 2])
            return carry

        lax.fori_loop(0, nt // 2, tile, 0)

        def flush(i, carry):
            rows = pl.ds(pl.multiple_of(i * chunk, chunk), chunk)
            nxt = pl.ds(pl.multiple_of(T + i * chunk, chunk), chunk)
            dz_ref[0, rows, :] = dqacc[rows, :].astype(BF16)
            dz_ref[1, rows, :] = dkacc[rows, :].astype(BF16)
            dz_ref[2, rows, :] = dvacc[rows, :].astype(BF16)
            dqacc[rows, :] = dqacc[nxt, :]
            for hh in range(2):
                qa[hh, rows, :] = qa[hh, nxt, :]
                doa[hh, rows, :] = doa[hh, nxt, :]
            return carry

        lax.fori_loop(0, T // chunk, flush, 0)

    zc = lambda split: (lambda hp, sb: (sb, split * npair + hp))
    zn = lambda split: (lambda hp, sb: (jnp.minimum(sb + 1, nsb - 1), split * npair + hp))
    ec = lambda off: (lambda hp, sb: (sb, off + hp))
    en = lambda off: (lambda hp, sb: (jnp.minimum(sb + 1, nsb - 1), off + hp))
    z0 = lambda split: (lambda hp, sb: (0, split * npair + hp))
    e0 = lambda off: (lambda hp, sb: (0, off + hp))
    blk = lambda index: pl.BlockSpec((T, LANES), index)
    buf = lambda rows: pltpu.VMEM((rows, LANES), F32)
    return pl.pallas_call(
        body, name="attn_bwd", grid=(npair, nsb), out_shape=jax.ShapeDtypeStruct((4, s, e), BF16),
        in_specs=[blk(zc(5)), blk(zc(6)), blk(z0(4)), blk(zn(4)), blk(ec(npair)), blk(en(npair)),
                  blk(zc(7)), blk(zn(7)), blk(ec(0)), blk(en(0)), blk(e0(0)), blk(en(0))],
        out_specs=pl.BlockSpec((4, T, LANES), lambda hp, sb: (0, sb, hp)),
        scratch_shapes=[pltpu.VMEM((2, 2 * T, LANES), F32), pltpu.VMEM((2, 2 * T, LANES), F32),
                        pltpu.VMEM((2, T, LANES), F32), pltpu.VMEM((2, T, LANES), F32),
                        buf(2 * T), buf(T), buf(T), pltpu.VMEM((12, 2 * W, W), F32)],
        compiler_params=_params("parallel", "arbitrary"))(z, z, z, z, dy, dy, z, z, o, o, lse, lse)


def _dz_specs(tm, e):
    def mk(lo, hi):
        return pl.BlockSpec((None, tm, e), lambda i, k: (jnp.clip(k - lo, 0, hi - lo - 1), i, 0))
    return [mk(0, 4), mk(4, 8)]


def _dz_pick(grp, dzh_ref, dza_ref, fn):
    @pl.when(grp < 4)
    def _():
        fn(dzh_ref[...])

    @pl.when(grp >= 4)
    def _():
        fn(dza_ref[...])


def _dh_dx(dzh, dza, w_full, x, gain, dx2):
    s, d = x.shape
    e = dzh.shape[2]
    tm = _tile(s, 1024)
    ni = s // tm
    chunk = _tile(tm, 256)
    fetch_at = 2

    def body(dzh_ref, dza_ref, w_ref, x_hbm, g_ref, dx2_hbm, gx_hbm, dg_ref, acc, xbuf, dbuf, sems):
        i, k = pl.program_id(0), pl.program_id(1)
        tile_rows = pl.ds(pl.multiple_of(i * tm, tm), tm)
        fetch_x = pltpu.make_async_copy(x_hbm.at[tile_rows, :], xbuf, sems.at[0])
        fetch_d = pltpu.make_async_copy(dx2_hbm.at[tile_rows, :], dbuf, sems.at[1])
        store = pltpu.make_async_copy(xbuf, gx_hbm.at[tile_rows, :], sems.at[2])

        @pl.when((i == 0) & (k == 0))
        def _():
            dg_ref[...] = jnp.zeros_like(dg_ref)

        @pl.when(k == 0)
        def _():
            acc[...] = jnp.zeros_like(acc)

        @pl.when((k == fetch_at) & (i > 0))
        def _():
            store.wait()

        @pl.when(k == fetch_at)
        def _():
            fetch_x.start()
            fetch_d.start()

        def add(dz):
            acc[...] += _dot_nt(dz, w_ref[...])

        _dz_pick(k, dzh_ref, dza_ref, add)

        @pl.when(k == N_SPLITS - 1)
        def _():
            fetch_x.wait()
            fetch_d.wait()
            gain_row = g_ref[...]

            def finish(c, dg):
                rows = pl.ds(pl.multiple_of(c * chunk, chunk), chunk)
                dh = acc[rows, :]
                xv = xbuf[rows, :]
                r = lax.rsqrt(jnp.mean(xv * xv, axis=-1, keepdims=True) + NORM_EPS)
                xn = xv * r
                u = dh * gain_row
                xbuf[rows, :] = dbuf[rows, :] + r * (u - xn * jnp.mean(u * xn, axis=-1, keepdims=True))
                return dg + jnp.sum(dh * xn, axis=0, keepdims=True)

            dg_ref[...] += lax.fori_loop(0, tm // chunk, finish, jnp.zeros((1, d), F32))
            store.start()

        @pl.when((k == N_SPLITS - 1) & (i == ni - 1))
        def _():
            store.wait()

    vec = pl.BlockSpec((1, d), lambda i, k: (0, 0))
    return pl.pallas_call(
        body, name="dh_dx", grid=(ni, N_SPLITS),
        out_shape=(jax.ShapeDtypeStruct((s, d), F32), jax.ShapeDtypeStruct((1, d), F32)),
        in_specs=_dz_specs(tm, e) + [pl.BlockSpec((None, d, e), lambda i, k: (k, 0, 0)), ANY, vec, ANY],
        out_specs=(ANY, vec),
        scratch_shapes=[pltpu.VMEM((tm, d), F32), pltpu.VMEM((tm, d), F32), pltpu.VMEM((tm, d), F32),
                        pltpu.SemaphoreType.DMA((3,))],
        compiler_params=_params("arbitrary", "arbitrary"))(dzh, dza, w_full, x, gain, dx2)


def _position():
    x, y, c = lax.axis_index("x"), lax.axis_index("y"), lax.axis_index("c")
    return x, y, c


def _xor_peer(x, y, c, mask):
    return (x ^ ((mask >> 2) & 1), y ^ ((mask >> 1) & 1), c ^ (mask & 1))


def _block_order(masks):
    me = 4 * lax.axis_index("x") + 2 * lax.axis_index("y") + lax.axis_index("c")
    return jnp.stack([me ^ m for m in masks]).astype(jnp.int32)


GATHER_MASKS = (0, 1, 4, 5, 2, 3, 6, 7)


def _inproj_gather(h, w_loc, wo_loc):
    s, d = h.shape
    e = w_loc.shape[1]
    tm = _tile(s, 1024)
    ni = s // tm
    pre = max(ni - 2, 0)

    def body(order_ref, h_ref, w_ref, wo_ref, z_ref, wf_ref, wof_ref, wbuf, send_sems, recv_sems, osend, orecv,
             local_sems, wsems):
        j, i = pl.program_id(0), pl.program_id(1)
        x, y, c = _position()
        me, sibling = (x, y, c), (x, y, 1 - c)
        chips = [(1 - x, y), (x, 1 - y), (1 - x, 1 - y)]
        blk = lambda p: 4 * p[0] + 2 * p[1] + p[2]

        def copy(k, block, to, src=None):
            dst = wf_ref.at[blk(block)]
            return pltpu.make_async_remote_copy(
                src_ref=dst if src is None else src, dst_ref=dst, send_sem=send_sems.at[k], recv_sem=recv_sems.at[k],
                device_id=to, device_id_type=MESH)

        first = [copy(0, me, sibling, src=w_ref)] + [copy(1 + q, me, (*chip, c), src=w_ref) for q, chip in enumerate(chips)]
        passed = [copy(4 + q, (*chip, c), sibling) for q, chip in enumerate(chips)]
        mine = pltpu.make_async_copy(w_ref, wf_ref.at[blk(me)], local_sems.at[0])
        ocopies = [pltpu.make_async_remote_copy(
            src_ref=wo_ref, dst_ref=wof_ref.at[blk(me)], send_sem=osend.at[k], recv_sem=orecv.at[k],
            device_id=_xor_peer(x, y, c, k + 1), device_id_type=MESH) for k in range(N_DEV - 1)]
        omine = pltpu.make_async_copy(wo_ref, wof_ref.at[blk(me)], local_sems.at[1])
        blocks = [me, sibling] + [(*chip, c) for chip in chips] + [(*chip, 1 - c) for chip in chips]
        arrive = [None, copy(0, sibling, me)] + [copy(1 + q, (*chip, c), me) for q, chip in enumerate(chips)] \
            + [copy(4 + q, (*chip, 1 - c), me) for q, chip in enumerate(chips)]
        forward = [None, None] + passed + [None, None, None]
        use_order = (0, 1, 2, 5, 3, 6, 4, 7)
        blocks, arrive, forward = ([lst[n] for n in use_order] for lst in (blocks, arrive, forward))

        def load(slot, src):
            return pltpu.make_async_copy(src, wbuf.at[slot], wsems.at[slot])

        @pl.when((j == 0) & (i == 0))
        def _():
            for cp in [mine, omine] + first + ocopies:
                cp.start()
            load(0, w_ref).start()

        for jj in range(N_DEV):
            @pl.when((j == jj) & (i == 0))
            def _():
                load(jj % 2, w_ref).wait()

            if jj + 1 < N_DEV:
                @pl.when((j == jj) & (i == pre))
                def _():
                    arrive[jj + 1].wait_recv()
                    if forward[jj + 1] is not None:
                        forward[jj + 1].start()
                    load((jj + 1) % 2, wf_ref.at[blk(blocks[jj + 1])]).start()

        z_ref[...] = _dot(h_ref[...], wbuf[j % 2])

        @pl.when((j == N_DEV - 1) & (i == ni - 1))
        def _():
            for cp in first + passed:
                cp.wait_send()
            for cp in ocopies:
                cp.wait_send()
                cp.wait_recv()
            mine.wait()
            omine.wait()

    grid_spec = pltpu.PrefetchScalarGridSpec(
        num_scalar_prefetch=1, grid=(N_DEV, ni),
        in_specs=[pl.BlockSpec((tm, d), lambda j, i, o: (i, 0)), ANY, ANY],
        out_specs=(pl.BlockSpec((tm, e), lambda j, i, o: (i, o[j])), ANY, ANY),
        scratch_shapes=[pltpu.VMEM((2, d, e), BF16), pltpu.SemaphoreType.DMA((7,)), pltpu.SemaphoreType.DMA((7,)),
                        pltpu.SemaphoreType.DMA((7,)), pltpu.SemaphoreType.DMA((7,)), pltpu.SemaphoreType.DMA((2,)),
                        pltpu.SemaphoreType.DMA((2,))])
    return pl.pallas_call(
        body, name="inproj_gather", grid_spec=grid_spec,
        out_shape=(jax.ShapeDtypeStruct((s, N_SPLITS * e), F32), jax.ShapeDtypeStruct((N_DEV, d, e), BF16),
                   jax.ShapeDtypeStruct((N_DEV,) + wo_loc.shape, BF16)),
        compiler_params=_params("arbitrary", "arbitrary"))(_block_order(GATHER_MASKS), h, w_loc, wo_loc)


SCATTER_MASKS = (7, 6, 5, 4, 3, 2, 1, 0)
N_CHIPS = 4


def _scatter_block(k, acc, stage, tmp, own_ref, ra_ref, rb_ref, sa_send, sa_recv, sb_send, sb_recv, loc_sem, step, ns):
    x, y, c = _position()
    chip_of = lambda t: _xor_peer(x, y, c, SCATTER_MASKS[2 * t + 1])
    last = step == ns - 1
    fetch_at = min(1, ns - 1)

    def ship(t):
        return pltpu.make_async_remote_copy(
            src_ref=stage.at[0], dst_ref=ra_ref.at[t], send_sem=sa_send.at[t], recv_sem=sa_recv.at[t],
            device_id=(x, y, 1 - c), device_id_type=MESH)

    def send(t):
        return pltpu.make_async_remote_copy(
            src_ref=stage.at[1], dst_ref=rb_ref.at[t], send_sem=sb_send.at[t], recv_sem=sb_recv.at[t],
            device_id=chip_of(t), device_id_type=MESH)

    for kk in range(N_DEV):
        t = kk // 2
        fetch = pltpu.make_async_copy(ra_ref.at[t], tmp, loc_sem)

        if kk % 2 == 1:
            @pl.when((step == fetch_at) & (k == kk))
            def _():
                ship(t).wait_recv()
                fetch.start()

        @pl.when(last & (k == kk))
        def _():
            if kk % 2 == 0:
                if t >= 1:
                    ship(t - 1).wait_send()
                stage[0] = acc[...].astype(BF16)
                ship(t).start()
            else:
                fetch.wait()
                acc[...] += tmp[...].astype(F32)
                if t < N_CHIPS - 1:
                    if t >= 1:
                        send(t - 1).wait_send()
                    stage[1] = acc[...].astype(BF16)
                    send(t).start()
                else:
                    keep = pltpu.make_async_copy(acc, own_ref, loc_sem)
                    keep.start()
                    keep.wait()
                    ship(t).wait_send()
                    send(t - 1).wait_send()
                    for q in range(N_CHIPS - 1):
                        send(q).wait_recv()


def _scatter_scratch(rows, cols):
    return [pltpu.VMEM((rows, cols), F32), pltpu.VMEM((2, rows, cols), BF16), pltpu.VMEM((rows, cols), BF16),
            pltpu.SemaphoreType.DMA((N_CHIPS,)), pltpu.SemaphoreType.DMA((N_CHIPS,)),
            pltpu.SemaphoreType.DMA((N_CHIPS - 1,)), pltpu.SemaphoreType.DMA((N_CHIPS - 1,)), pltpu.SemaphoreType.DMA(())]


def _scatter_out(rows, cols):
    return (jax.ShapeDtypeStruct((rows, cols), F32), jax.ShapeDtypeStruct((N_CHIPS, rows, cols), BF16),
            jax.ShapeDtypeStruct((N_CHIPS - 1, rows, cols), BF16))


def _dwin_scatter(h, dzh, dza):
    s, d = h.shape
    e = dzh.shape[2]
    ts = _tile(s, 1024)
    ns = s // ts

    def body(order_ref, dzh_ref, dza_ref, h_ref, own_ref, ra_ref, rb_ref, acc, stage, tmp, *sems):
        k, step = pl.program_id(0), pl.program_id(1)

        @pl.when(step == 0)
        def _():
            acc[...] = jnp.zeros_like(acc)

        def add(dz):
            acc[...] += _dot_tn(h_ref[...], dz)

        _dz_pick(order_ref[k], dzh_ref, dza_ref, add)
        _scatter_block(k, acc, stage, tmp, own_ref, ra_ref, rb_ref, *sems, step, ns)

    def dz_spec(lo):
        return pl.BlockSpec((None, ts, e), lambda k, st, o: (jnp.clip(o[k] - lo, 0, 3), st, 0))

    grid_spec = pltpu.PrefetchScalarGridSpec(
        num_scalar_prefetch=1, grid=(N_DEV, ns),
        in_specs=[dz_spec(0), dz_spec(4), pl.BlockSpec((ts, d), lambda k, st, o: (st, 0))],
        out_specs=(ANY, ANY, ANY), scratch_shapes=_scatter_scratch(d, e))
    own, _, rb = pl.pallas_call(
        body, name="dwin_scatter", grid_spec=grid_spec, out_shape=_scatter_out(d, e),
        compiler_params=_params("arbitrary", "arbitrary"))(_block_order(SCATTER_MASKS), dzh, dza, h)
    return own, rb


def _dwout_scatter(y_h, y_a, dxb):
    s, e = y_h.shape
    d = dxb.shape[1]
    r = 2 * e // N_DEV
    pairs = e // (2 * r)
    ts = _tile(s, 1024)
    ns = s // ts
    chip_masks = SCATTER_MASKS[1::2]
    passes = ((0, 1), (2,), (3,))
    slots = max(len(chips) for chips in passes)
    slot_chip = [chips[min(u, len(chips) - 1)] for chips in passes for u in range(slots)]

    def body(pair_ref, yh0_ref, ya0_ref, yh1_ref, ya1_ref, dx_ref, own_ref, ra_ref, rb_ref, acc, keep_buf, ship_buf,
             send_buf, tmp, sa_send, sa_recv, sb_send, sb_recv, loc_sem):
        p, step = pl.program_id(0), pl.program_id(1)
        x, y, c = _position()

        @pl.when(step == 0)
        def _():
            acc[...] = jnp.zeros_like(acc)

        for u, (yh_ref, ya_ref) in enumerate(((yh0_ref, ya0_ref), (yh1_ref, ya1_ref))):
            rows = slice(u * 2 * r, (u + 1) * 2 * r)
            used = functools.reduce(jnp.logical_or, [p == pp for pp, chips in enumerate(passes) if u < len(chips)])

            @pl.when(used & (pair_ref[slots * p + u] < pairs))
            def _():
                acc[rows, :] += _dot_tn(yh_ref[...], dx_ref[...])

            @pl.when(used & (pair_ref[slots * p + u] >= pairs))
            def _():
                acc[rows, :] += _dot_tn(ya_ref[...], dx_ref[...])

        def block_rows(u, core):
            return pl.ds(pl.multiple_of(u * 2 * r + core * r, r), r)

        slot_of = {q: u for chips in passes for u, q in enumerate(chips)}

        def ship(q):
            return pltpu.make_async_remote_copy(
                src_ref=ship_buf.at[slot_of[q]], dst_ref=ra_ref.at[q], send_sem=sa_send.at[q], recv_sem=sa_recv.at[q],
                device_id=(x, y, 1 - c), device_id_type=MESH)

        def send(q):
            return pltpu.make_async_remote_copy(
                src_ref=send_buf.at[slot_of[q]], dst_ref=rb_ref.at[q], send_sem=sb_send.at[q], recv_sem=sb_recv.at[q],
                device_id=_xor_peer(x, y, c, chip_masks[q]), device_id_type=MESH)

        def sibling_share(q):
            ship(q).wait_recv()
            fetch = pltpu.make_async_copy(ra_ref.at[q], tmp, loc_sem)
            fetch.start()
            fetch.wait()
            return tmp[...].astype(F32)

        shipped, sent = {}, {}
        for pp, chips in enumerate(passes):
            @pl.when((step == ns - 1) & (p == pp))
            def _():
                for u, q in enumerate(chips):
                    if u in shipped:
                        ship(shipped.pop(u)).wait_send()
                    ship_buf[u] = acc[block_rows(u, 1 - c), :].astype(BF16)
                    ship(q).start()
                    shipped[u] = q
                for u, q in enumerate(chips):
                    total = acc[block_rows(u, c), :] + sibling_share(q)
                    if q < N_CHIPS - 1:
                        if u in sent:
                            send(sent.pop(u)).wait_send()
                        send_buf[u] = total.astype(BF16)
                        send(q).start()
                        sent[u] = q
                    else:
                        keep_buf[...] = total
                        keep = pltpu.make_async_copy(keep_buf, own_ref, loc_sem)
                        keep.start()
                        keep.wait()
                if pp == len(passes) - 1:
                    for q in shipped.values():
                        ship(q).wait_send()
                    for q in sent.values():
                        send(q).wait_send()
                    for q in range(N_CHIPS - 1):
                        send(q).wait_recv()

    def y_spec(u, lo):
        return pl.BlockSpec((ts, 2 * r), lambda p, st, o: (st, jnp.clip(o[slots * p + u] - lo, 0, pairs - 1)))

    pair_of_chip = _block_order(chip_masks) // 2
    grid_spec = pltpu.PrefetchScalarGridSpec(
        num_scalar_prefetch=1, grid=(len(passes), ns),
        in_specs=[y_spec(0, 0), y_spec(0, pairs), y_spec(1, 0), y_spec(1, pairs),
                  pl.BlockSpec((ts, d), lambda p, st, o: (st, 0))],
        out_specs=(ANY, ANY, ANY),
        scratch_shapes=[pltpu.VMEM((slots * 2 * r, d), F32), pltpu.VMEM((r, d), F32),
                        pltpu.VMEM((slots, r, d), BF16)] + _scatter_scratch(r, d)[1:])
    own, _, rb = pl.pallas_call(
        body, name="dwout_scatter", grid_spec=grid_spec, out_shape=_scatter_out(r, d),
        compiler_params=_params("arbitrary", "arbitrary"))(
            jnp.stack([pair_of_chip[q] for q in slot_chip]), y_h, y_a, y_h, y_a, dxb)
    return own, rb


def _sum_chips_adamw(own, recv, w, m, v):
    r, c = w.shape
    tr = _tile(r, 128)

    def body(own_ref, rc_ref, w_ref, m_ref, v_ref, g_ref, d_ref, mo_ref, vo_ref):
        g = own_ref[...]
        for q in range(N_CHIPS - 1):
            g = g + rc_ref[q].astype(F32)
        g_ref[...] = g
        d_ref[...], mo_ref[...], vo_ref[...] = _adamw(w_ref[...], g, m_ref[...], v_ref[...])

    blk = pl.BlockSpec((tr, c), lambda i: (i, 0))
    shp = jax.ShapeDtypeStruct((r, c), F32)
    return pl.pallas_call(
        body, name="sum_chips_adamw", grid=(r // tr,), out_shape=(shp, shp, shp, shp),
        in_specs=[blk, pl.BlockSpec((N_CHIPS - 1, tr, c), lambda i: (0, i, 0)), blk, blk, blk],
        out_specs=(blk, blk, blk, blk), compiler_params=_params("parallel"))(own, recv, w, m, v)


SMALL_ROWS = 8
ROW_LB = 4
ROW_GN = 6
ROW_LOSS = 7


def _small_allreduce_adamw(part, w, m, v, lb_logits):
    width = part.shape[1]

    def body(p_ref, w_ref, m_ref, v_ref, lb_ref, g_ref, d_ref, mo_ref, vo_ref, buf, send_sems, recv_sems):
        x, y, c = _position()
        me = 4 * x + 2 * y + c
        buf[me] = p_ref[...]
        copies = []
        for k in range(N_DEV - 1):
            bx, by, bc = ((k + 1) >> 2) & 1, ((k + 1) >> 1) & 1, (k + 1) & 1
            peer = (x ^ bx, y ^ by, c ^ bc)
            copies.append(pltpu.make_async_remote_copy(
                src_ref=p_ref, dst_ref=buf.at[me], send_sem=send_sems.at[k], recv_sem=recv_sems.at[k],
                device_id=peer, device_id_type=MESH))
        for cp in copies:
            cp.start()
        for cp in copies:
            cp.wait_recv()
        for cp in copies:
            cp.wait_send()
        tot = buf[0]
        for dev in range(1, N_DEV):
            tot = tot + buf[dev]
        lbv = lb_ref[...]
        lb = _sigmoid(lbv[0:1] - lbv[1:2])
        glb = tot[ROW_LB:ROW_LB + 1] * lb * (1.0 - lb)
        loss = jnp.sum(tot[ROW_LOSS:ROW_LOSS + 1], axis=-1, keepdims=True)
        row = lax.broadcasted_iota(jnp.int32, (SMALL_ROWS, width), 0)
        g = jnp.where(row == ROW_LB, glb, jnp.where(row == ROW_LB + 1, -glb, tot))
        g = jnp.where(row == ROW_LOSS, loss, g)
        g_ref[...] = g
        d_ref[...], mo_ref[...], vo_ref[...] = _adamw(w_ref[...], g, m_ref[...], v_ref[...])

    vm = pl.BlockSpec(memory_space=pltpu.VMEM)
    shp = jax.ShapeDtypeStruct((SMALL_ROWS, width), F32)
    return pl.pallas_call(
        body, name="small_allreduce_adamw", out_shape=(shp, shp, shp, shp),
        in_specs=[vm] * 5, out_specs=(vm, vm, vm, vm),
        scratch_shapes=[pltpu.VMEM((N_DEV, SMALL_ROWS, width), F32), pltpu.SemaphoreType.DMA((N_DEV - 1,)),
                        pltpu.SemaphoreType.DMA((N_DEV - 1,))],
    )(part, w, m, v, lb_logits)


def _pack_small(norm_gain, final_gain, lb2, gnorm, last_row, width):
    pad = lambda a: jnp.pad(a.reshape(1, -1), ((0, 0), (0, width - a.size)))
    return jnp.concatenate([norm_gain.reshape(2, width), final_gain.reshape(2, width), lb2.reshape(2, width),
                            pad(gnorm), last_row.reshape(1, width)], axis=0)


def _unpack_small(p, d, e, hd):
    return (p[0:2].reshape(1, d), p[2:4].reshape(d), p[4:6].reshape(2, e), p[6:7, :hd].reshape(1, hd))


def kernel(x, norm_gain, w_in, lb_logits, hgrn_gnorm, w_out, final_gain, loss_target, m_norm_gain, m_w_in, m_lb_logits, m_hgrn_gnorm, m_w_out, m_final_gain, v_norm_gain, v_w_in, v_lb_logits, v_hgrn_gnorm, v_w_out, v_final_gain):
    s, d = x.shape[1], x.shape[2]
    e = w_in.shape[2]
    assert d == 2 * e and lb_logits.shape == (2, e) and w_out.shape[1] * N_DEV == 2 * e
    x2d = x.reshape(s, d)
    tgt = loss_target.reshape(s, d)

    h = _rmsnorm_fwd(x2d, norm_gain)
    z, w_in_full, w_out_full = _inproj_gather(h, _cast_bf16(w_in[0]), _cast_bf16(w_out[0]))
    w_out_full = w_out_full.reshape(2 * e, d)
    y_h, states = _hgrn_fwd(z, lb_logits, hgrn_gnorm)
    o_attn, lse, y_a = _attn_fwd(z)
    dx2, dx2b, dy, loss_vec, dfg = _outproj_loss(x2d, y_h, y_a, w_out_full, final_gain.reshape(1, d), tgt)

    own_o, recv_o = _dwout_scatter(y_h, y_a, dx2b)
    dza = _attn_bwd(z, dy, o_attn, lse)
    dzh, dlb, dgn = _hgrn_bwd(z, dy, states, lb_logits, hgrn_gnorm)
    grad_x, dng = _dh_dx(dzh, dza, w_in_full, x2d, norm_gain, dx2)
    g_wo, d_wo, nm_wo, nv_wo = _sum_chips_adamw(own_o, recv_o, w_out[0], m_w_out[0], v_w_out[0])

    width = d // 2
    zero_row = jnp.zeros((1, width), F32)
    loss_row = loss_vec[:, :width] + loss_vec[:, width:]
    part = _pack_small(dng, dfg, jnp.concatenate([dlb, zero_row], axis=0), dgn, loss_row, width)
    pw = _pack_small(norm_gain, final_gain, lb_logits, hgrn_gnorm, zero_row, width)
    pm = _pack_small(m_norm_gain, m_final_gain, m_lb_logits, m_hgrn_gnorm, zero_row, width)
    pv = _pack_small(v_norm_gain, v_final_gain, v_lb_logits, v_hgrn_gnorm, zero_row, width)
    sg, sd, sm, sv = _small_allreduce_adamw(part, pw, pm, pv, lb_logits)
    own_i, recv_i = _dwin_scatter(h, dzh, dza)
    g_wi, d_wi, nm_wi, nv_wi = _sum_chips_adamw(own_i, recv_i, w_in[0], m_w_in[0], v_w_in[0])
    hd = hgrn_gnorm.shape[1]
    g_ng, g_fg, g_lb, g_gn = _unpack_small(sg, d, e, hd)
    d_ng, d_fg, d_lb, d_gn = _unpack_small(sd, d, e, hd)
    m_ng, m_fg, m_lb, m_gn = _unpack_small(sm, d, e, hd)
    v_ng, v_fg, v_lb, v_gn = _unpack_small(sv, d, e, hd)
    loss = sg[ROW_LOSS, 0]

    one = lambda a: a[None]
    return (loss, grad_x.reshape(1, s, d), g_ng, one(g_wi), g_lb, g_gn, one(g_wo), g_fg,
            d_ng, one(d_wi), d_lb, d_gn, one(d_wo), d_fg,
            m_ng, one(nm_wi), m_lb, m_gn, one(nm_wo), m_fg,
            v_ng, one(nv_wi), v_lb, v_gn, one(nv_wo), v_fg)
```

```python
import functools
import math

import jax
import jax.numpy as jnp
from jax import lax
from jax.experimental import pallas as pl
from jax.experimental.pallas import tpu as pltpu

NORM_EPS = 1e-6
HGRN_HEAD = 128
HGRN_CHUNK = 64
ATTN_HEAD = 64
ATTN_BAND = 128
DILATIONS = (1, 4, 16)
N_SPLITS = 8
N_DEV = 8
ADAM_LR = 0.001
ADAM_B1 = 0.9
ADAM_B2 = 0.999
ADAM_EPS = 1e-08
ADAM_WD = 0.01
ADAM_STEP = 10
LANES = 128
MESH = pl.DeviceIdType.MESH
F32 = jnp.float32
BF16 = jnp.bfloat16
NEG_BIG = -1e30
VMEM_LIMIT = 56 * 1024 * 1024

ANY = pl.BlockSpec(memory_space=pl.ANY)


def _params(*sem):
    return pltpu.CompilerParams(dimension_semantics=sem, vmem_limit_bytes=VMEM_LIMIT)


def _tile(n, pref):
    t = min(n, pref)
    assert n % t == 0, (n, pref)
    return t


def _dot(a, b, precision=None):
    return jnp.dot(a, b, preferred_element_type=F32, precision=precision)


def _dot_nt(a, b):
    return lax.dot_general(a, b, (((1,), (1,)), ((), ())), preferred_element_type=F32)


def _dot_tn(a, b):
    return lax.dot_general(a, b, (((0,), (0,)), ((), ())), preferred_element_type=F32)


def _sigmoid(x):
    return 0.5 * jnp.tanh(0.5 * x) + 0.5


def _dsilu(x, s):
    return s * (1.0 + x * (1.0 - s))


def _adamw(w, g, m, v):
    m = ADAM_B1 * m + (1.0 - ADAM_B1) * g
    v = ADAM_B2 * v + (1.0 - ADAM_B2) * (g * g)
    m_hat = m / (1.0 - ADAM_B1 ** ADAM_STEP)
    v_hat = v / (1.0 - ADAM_B2 ** ADAM_STEP)
    delta = -ADAM_LR * (m_hat / (jnp.sqrt(v_hat) + ADAM_EPS) + ADAM_WD * w)
    return delta, m, v


def _cast_bf16(a):
    r, c = a.shape
    tr = _tile(r, 256)

    def body(a_ref, o_ref):
        o_ref[...] = a_ref[...].astype(BF16)

    return pl.pallas_call(
        body, name="cast_bf16", grid=(r // tr,), out_shape=jax.ShapeDtypeStruct((r, c), BF16),
        in_specs=[pl.BlockSpec((tr, c), lambda i: (i, 0))], out_specs=pl.BlockSpec((tr, c), lambda i: (i, 0)),
        compiler_params=_params("parallel"))(a)


def _rmsnorm_fwd(x, gain):
    s, d = x.shape
    tm = _tile(s, 512)

    def body(x_ref, g_ref, h_ref):
        xv = x_ref[...]
        r = lax.rsqrt(jnp.mean(xv * xv, axis=-1, keepdims=True) + NORM_EPS)
        h_ref[...] = (xv * r * g_ref[...]).astype(BF16)

    return pl.pallas_call(
        body, name="rmsnorm_fwd", grid=(s // tm,), out_shape=jax.ShapeDtypeStruct((s, d), BF16),
        in_specs=[pl.BlockSpec((tm, d), lambda i: (i, 0)), pl.BlockSpec((1, d), lambda i: (0, 0))],
        out_specs=pl.BlockSpec((tm, d), lambda i: (i, 0)), compiler_params=_params("parallel"))(x, gain)


HGRN_BLOCK = 2048
TRI_ROWS = 256


def _chunk_masks():
    tb = TRI_ROWS
    row = lax.broadcasted_iota(jnp.int32, (tb, tb), 0)
    col = lax.broadcasted_iota(jnp.int32, (tb, tb), 1)
    same = (row // HGRN_CHUNK) == (col // HGRN_CHUNK)
    lower = jnp.where(same & (col <= row), 1.0, 0.0).astype(BF16)
    upper = jnp.where(same & (col >= row), 1.0, 0.0).astype(BF16)
    return lower, upper


def _split3(a):
    hi = a.astype(BF16).astype(F32)
    mid = (a - hi).astype(BF16).astype(F32)
    lo = (a - hi - mid).astype(BF16).astype(F32)
    return hi, mid, lo


def _tri_dot(tri, x):
    hi, mid, lo = (p.astype(BF16) for p in _split3(x))
    outs = []
    for r in range(0, x.shape[0], TRI_ROWS):
        sl = slice(r, r + TRI_ROWS)
        outs.append(_dot(tri, hi[sl]) + _dot(tri, mid[sl]) + _dot(tri, lo[sl]))
    return outs[0] if len(outs) == 1 else jnp.concatenate(outs, axis=0)


def _hgrn_gates(qp, fp, lbv):
    lb = _sigmoid(lbv[0:1] - lbv[1:2])
    sq = _sigmoid(qp)
    q = qp * sq
    sg = _sigmoid(fp)
    f = lb + (1.0 - lb) * sg
    k = 1.0 - f
    return lb, sq, q, sg, f, k


def _hgrn_fwd(z, lb_logits, gnorm):
    s = z.shape[0]
    e = z.shape[1] // N_SPLITS
    nh = e // HGRN_HEAD
    tb = _tile(s, HGRN_BLOCK)
    nc = tb // HGRN_CHUNK
    nb = s // tb
    C = HGRN_CHUNK

    def body(q_ref, f_ref, i_ref, g_ref, lb_ref, gn_ref, y_ref, st_ref, state, o_scr):
        @pl.when(pl.program_id(1) == 0)
        def _():
            state[...] = jnp.zeros_like(state)

        lb, sq, q, sg, f, k = _hgrn_gates(q_ref[...], f_ref[...], lb_ref[...])
        lower, _ = _chunk_masks()
        b = _tri_dot(lower, jnp.log(f))
        b3 = b.reshape(nc, C, HGRN_HEAD)
        bc = b3[:, C - 1:C, :]
        qt = (q * jnp.exp(b)).astype(BF16)
        kt = (k * jnp.exp(-b)).astype(BF16)
        ke = (k.reshape(nc, C, HGRN_HEAD) * jnp.exp(bc - b3)).reshape(tb, HGRN_HEAD).astype(BF16)
        v = i_ref[...].astype(BF16)
        tri = lax.broadcasted_iota(jnp.int32, (C, C), 1) <= lax.broadcasted_iota(jnp.int32, (C, C), 0)
        sls = [slice(c * C, (c + 1) * C) for c in range(nc)]
        kv = [_dot_tn(v[sl], ke[sl]) for sl in sls]
        a = [jnp.where(tri, _dot_nt(qt[sl], kt[sl]), 0.0).astype(BF16) for sl in sls]
        st = state[...]
        sts = []
        for c in range(nc):
            sts.append(st)
            st_ref[c] = st
            st = st * jnp.exp(bc[c]) + kv[c]
        state[...] = st
        for c, sl in enumerate(sls):
            o_scr[sl, :] = _dot(a[c], v[sl]) + _dot_nt(qt[sl], sts[c].astype(BF16))
        o = o_scr[...]
        rms = lax.rsqrt(jnp.mean(o * o, axis=-1, keepdims=True) + NORM_EPS)
        gp = g_ref[...]
        y_ref[...] = (o * rms * gn_ref[...] * (gp * _sigmoid(gp))).astype(BF16)

    col = lambda kk: (lambda h, n: (n, kk * nh + h))
    return pl.pallas_call(
        body, name="hgrn_fwd", grid=(nh, nb),
        out_shape=(jax.ShapeDtypeStruct((s, e), BF16),
                   jax.ShapeDtypeStruct((nh, s // C, HGRN_HEAD, HGRN_HEAD), F32)),
        in_specs=[pl.BlockSpec((tb, HGRN_HEAD), col(0)), pl.BlockSpec((tb, HGRN_HEAD), col(1)),
                  pl.BlockSpec((tb, HGRN_HEAD), col(2)), pl.BlockSpec((tb, HGRN_HEAD), col(3)),
                  pl.BlockSpec((2, HGRN_HEAD), lambda h, n: (0, h)), pl.BlockSpec((1, HGRN_HEAD), lambda h, n: (0, 0))],
        out_specs=(pl.BlockSpec((tb, HGRN_HEAD), lambda h, n: (n, h)),
                   pl.BlockSpec((None, nc, HGRN_HEAD, HGRN_HEAD), lambda h, n: (h, n, 0, 0))),
        scratch_shapes=[pltpu.VMEM((HGRN_HEAD, HGRN_HEAD), F32), pltpu.VMEM((tb, HGRN_HEAD), F32)],
        compiler_params=_params("parallel", "arbitrary"))(z, z, z, z, lb_logits, gnorm)


def _hgrn_bwd(z, dy, states, lb_logits, gnorm):
    s = z.shape[0]
    e = z.shape[1] // N_SPLITS
    nh = e // HGRN_HEAD
    tb = _tile(s, HGRN_BLOCK)
    nc = tb // HGRN_CHUNK
    nb = s // tb
    C = HGRN_CHUNK
    H = HGRN_HEAD

    def body(q_ref, f_ref, i_ref, g_ref, dy_ref, st_ref, lb_ref, gn_ref, dz_ref, dlb_ref, dgn_ref,
             gstate, o_scr, dq_scr, dk_scr, dv_scr, e_scr):
        first = (pl.program_id(0) == 0) & (pl.program_id(1) == 0)

        @pl.when(first)
        def _():
            dgn_ref[...] = jnp.zeros_like(dgn_ref)

        @pl.when(pl.program_id(1) == 0)
        def _():
            gstate[...] = jnp.zeros_like(gstate)
            dlb_ref[...] = jnp.zeros_like(dlb_ref)

        qp = q_ref[...]
        lb, sq, q, sg, f, k = _hgrn_gates(qp, f_ref[...], lb_ref[...])
        lower, upper = _chunk_masks()
        b = _tri_dot(lower, jnp.log(f))
        b3 = b.reshape(nc, C, H)
        bc = b3[:, C - 1:C, :]
        eb = jnp.exp(b)
        enb = jnp.exp(-b)
        eend = jnp.exp(bc - b3).reshape(tb, H)
        qt = (q * eb).astype(BF16)
        kt = (k * enb).astype(BF16)
        ke = (k * eend).astype(BF16)
        v = i_ref[...].astype(BF16)
        tri = lax.broadcasted_iota(jnp.int32, (C, C), 1) <= lax.broadcasted_iota(jnp.int32, (C, C), 0)
        sls = [slice(c * C, (c + 1) * C) for c in range(nc)]
        a = [jnp.where(tri, _dot_nt(qt[sl], kt[sl]), 0.0).astype(BF16) for sl in sls]
        for c, sl in enumerate(sls):
            o_scr[sl, :] = _dot(a[c], v[sl]) + _dot_nt(qt[sl], st_ref[c].astype(BF16))
        o = o_scr[...]
        rms = lax.rsqrt(jnp.mean(o * o, axis=-1, keepdims=True) + NORM_EPS)
        on = o * rms
        gn = gn_ref[...]
        gp = g_ref[...]
        sgg = _sigmoid(gp)
        dyv = dy_ref[...]
        d_on = dyv * (gp * sgg)
        dz_ref[3] = (dyv * on * gn * _dsilu(gp, sgg)).astype(BF16)
        dgn_ref[...] += jnp.sum(d_on * on, axis=0, keepdims=True)
        u = d_on * gn
        do = (rms * (u - on * jnp.mean(u * on, axis=-1, keepdims=True))).astype(BF16)
        gup = [_dot_tn(do[sl], qt[sl]) for sl in sls]
        da = [jnp.where(tri, _dot_nt(do[sl], v[sl]), 0.0).astype(BF16) for sl in sls]
        gt = gstate[...]
        gts = [None] * nc
        for c in reversed(range(nc)):
            gts[c] = gt
            gt = gt * jnp.exp(bc[c]) + gup[c]
        gstate[...] = gt
        for c, sl in enumerate(sls):
            stp = st_ref[c]
            gtb = gts[c].astype(BF16)
            dqt = _dot(da[c], kt[sl]) + _dot(do[sl], stp.astype(BF16))
            dkt = _dot_tn(da[c], qt[sl])
            dks = _dot(v[sl], gtb) * eend[sl]
            dv_scr[sl, :] = _dot_tn(a[c], do[sl]) + _dot_nt(ke[sl], gtb)
            dq_scr[sl, :] = dqt * eb[sl]
            dk_scr[sl, :] = dkt * enb[sl] + dks
            ech = (jnp.sum(k[sl] * dks, axis=0, keepdims=True)
                   + jnp.sum(gts[c] * jnp.exp(bc[c]) * stp, axis=0, keepdims=True))
            e_scr[sl, :] = jnp.broadcast_to(ech, (C, H))
        dq = dq_scr[...]
        dk = dk_scr[...]
        dlf = _tri_dot(upper, q * dq - k * dk) + e_scr[...]
        dft = dlf / f - dk
        dz_ref[0] = (dq * _dsilu(qp, sq)).astype(BF16)
        dz_ref[1] = (dft * (1.0 - lb) * sg * (1.0 - sg)).astype(BF16)
        dz_ref[2] = dv_scr[...].astype(BF16)
        dlb_ref[...] += jnp.sum(dft * (1.0 - sg), axis=0, keepdims=True)

    col = lambda kk: (lambda h, n: (nb - 1 - n, kk * nh + h))
    return pl.pallas_call(
        body, name="hgrn_bwd", grid=(nh, nb),
        out_shape=(jax.ShapeDtypeStruct((4, s, e), BF16), jax.ShapeDtypeStruct((1, e), F32),
                   jax.ShapeDtypeStruct((1, H), F32)),
        in_specs=[pl.BlockSpec((tb, H), col(0)), pl.BlockSpec((tb, H), col(1)),
                  pl.BlockSpec((tb, H), col(2)), pl.BlockSpec((tb, H), col(3)),
                  pl.BlockSpec((tb, H), lambda h, n: (nb - 1 - n, h)),
                  pl.BlockSpec((None, nc, H, H), lambda h, n: (h, nb - 1 - n, 0, 0)),
                  pl.BlockSpec((2, H), lambda h, n: (0, h)), pl.BlockSpec((1, H), lambda h, n: (0, 0))],
        out_specs=(pl.BlockSpec((4, tb, H), lambda h, n: (0, nb - 1 - n, h)),
                   pl.BlockSpec((1, H), lambda h, n: (0, h)), pl.BlockSpec((1, H), lambda h, n: (0, 0))),
        scratch_shapes=[pltpu.VMEM((H, H), F32)] + [pltpu.VMEM((tb, H), F32)] * 5,
        compiler_params=_params("arbitrary", "arbitrary"))(z, z, z, z, dy, states, lb_logits, gnorm)


ATTN_T = 16 * ATTN_BAND
SCALE = ATTN_HEAD ** -0.5
TILE_UNROLL = 2


def _slope(hh, nheads):
    head = (2 * pl.program_id(0) + hh + 1).astype(F32)
    return jnp.exp(jnp.full((1, 1), -8.0 / nheads * math.log(2.0), F32) * head)


def _fill_bias(bias, nheads, delta, edge_ok):
    band = (delta >= 0) & (delta <= ATTN_BAND)
    dist = delta.astype(F32)
    for pi, dil in enumerate(DILATIONS):
        for hh in range(2):
            full = jnp.where(band, -(_slope(hh, nheads) * float(dil)) * dist, NEG_BIG)
            bias[(pi * 2 + hh) * 2] = full
            bias[(pi * 2 + hh) * 2 + 1] = jnp.where(edge_ok, full, NEG_BIG)


def _rows(start, size, stride):
    if stride == 1:
        return pl.ds(pl.multiple_of(start, ATTN_BAND), size)
    return pl.ds(start, size, stride=stride)


def _head_lanes(rows, hh):
    return (lax.broadcasted_iota(jnp.int32, (rows, LANES), 1) // ATTN_HEAD) == hh


def _attn_fwd(z):
    s = z.shape[0]
    e = z.shape[1] // N_SPLITS
    npair = e // LANES
    T = ATTN_T
    assert s % T == 0
    nsb = s // T
    W = ATTN_BAND
    nt = T // W
    HD = ATTN_HEAD
    chunk = 256

    def body(q_ref, kp_ref, kc_ref, vp_ref, vc_ref, g_ref, o_ref, l_ref, y_ref, qa, kbuf, va, bias, accs, ms, lsw):
        sb = pl.program_id(1)
        def stage(i, carry):
            rows = pl.ds(pl.multiple_of(i * chunk, chunk), chunk)
            upper = pl.ds(pl.multiple_of(T + i * chunk, chunk), chunk)
            kbuf[upper, :] = kc_ref[rows, :]
            for hh in range(2):
                mine = _head_lanes(chunk, hh)
                qa[hh, rows, :] = jnp.where(mine, q_ref[rows, :] * SCALE, 0.0)
                va[hh, upper, :] = jnp.where(mine, vc_ref[rows, :], 1.0)
            return carry

        lax.fori_loop(0, T // chunk, stage, 0)

        @pl.when(sb == 0)
        def _():
            def stage_prev(i, carry):
                rows = pl.ds(pl.multiple_of(i * chunk, chunk), chunk)
                kbuf[rows, :] = kp_ref[rows, :]
                for hh in range(2):
                    va[hh, rows, :] = jnp.where(_head_lanes(chunk, hh), vp_ref[rows, :], 1.0)
                return carry

            lax.fori_loop(0, T // chunk, stage_prev, 0)
        qi = lax.broadcasted_iota(jnp.int32, (W, 2 * W), 0)
        kj = lax.broadcasted_iota(jnp.int32, (W, 2 * W), 1)
        _fill_bias(bias, 2 * npair, W + qi - kj, kj >= W)

        def tile(tau, carry):
            first = _head_lanes(W, 0)
            rows, scores = [], []
            for pi, dil in enumerate(DILATIONS):
                r = tau % dil
                ub = tau // dil
                qrows = _rows(r + dil * W * ub, W, dil)
                krows = _rows(T + dil * W * (ub - 1) + r, 2 * W, dil)
                var = jnp.where((sb == 0) & (ub == 0), 1, 0)
                kt = kbuf[krows, :].astype(BF16)
                rows.append((qrows, krows))
                scores.append([_dot_nt(qa[hh, qrows, :].astype(BF16), kt) + bias[(pi * 2 + hh) * 2 + var]
                               for hh in range(2)])
            maxes = [[jnp.max(sc, axis=-1, keepdims=True) for sc in pair] for pair in scores]
            probs = [[jnp.exp(sc - m).astype(BF16) for sc, m in zip(ps, pm)] for ps, pm in zip(scores, maxes)]
            for pi, (qrows, krows) in enumerate(rows):
                outs = [_dot(probs[pi][hh], va[hh, krows, :].astype(BF16)) for hh in range(2)]
                accs[pi, qrows, :] = jnp.where(first, outs[0], outs[1])
                lsw[pi, qrows, :] = jnp.where(first, outs[1], outs[0])
                ms[pi, qrows, :] = jnp.where(first, maxes[pi][0], maxes[pi][1])
            return carry

        lax.fori_loop(0, nt, tile, 0, unroll=TILE_UNROLL)

        def merge(i, carry):
            rows = pl.ds(pl.multiple_of(i * chunk, chunk), chunk)
            m1, m2, m3 = ms[0, rows, :], ms[1, rows, :], ms[2, rows, :]
            mx = jnp.maximum(jnp.maximum(m1, m2), m3)
            w1, w2, w3 = jnp.exp(m1 - mx), jnp.exp(m2 - mx), jnp.exp(m3 - mx)
            unswap = lambda a: pltpu.roll(a, ATTN_HEAD, 1)
            den = w1 * unswap(lsw[0, rows, :]) + w2 * unswap(lsw[1, rows, :]) + w3 * unswap(lsw[2, rows, :])
            o = (w1 * accs[0, rows, :] + w2 * accs[1, rows, :] + w3 * accs[2, rows, :]) / den
            o_ref[rows, :] = o
            l_ref[rows, :] = mx + jnp.log(den)
            gp = g_ref[rows, :]
            y_ref[rows, :] = (o * (gp * _sigmoid(gp))).astype(BF16)
            upper = pl.ds(pl.multiple_of(T + i * chunk, chunk), chunk)
            kbuf[rows, :] = kbuf[upper, :]
            for hh in range(2):
                va[hh, rows, :] = va[hh, upper, :]
            return carry

        lax.fori_loop(0, T // chunk, merge, 0)

    cur = lambda split: (lambda hp, sb: (sb, split * npair + hp))
    prev = lambda split: (lambda hp, sb: (0, split * npair + hp))
    blk = lambda index: pl.BlockSpec((T, LANES), index)
    out = blk(lambda hp, sb: (sb, hp))
    buf = lambda rows: pltpu.VMEM((rows, LANES), F32)
    return pl.pallas_call(
        body, name="attn_fwd", grid=(npair, nsb),
        out_shape=(jax.ShapeDtypeStruct((s, e), F32), jax.ShapeDtypeStruct((s, e), F32), jax.ShapeDtypeStruct((s, e), BF16)),
        in_specs=[blk(cur(4)), blk(prev(5)), blk(cur(5)), blk(prev(6)), blk(cur(6)), blk(cur(7))],
        out_specs=(out, out, out),
        scratch_shapes=[pltpu.VMEM((2, T, LANES), F32), buf(2 * T), pltpu.VMEM((2, 2 * T, LANES), F32),
                        pltpu.VMEM((12, W, 2 * W), F32)] + [pltpu.VMEM((3, T, LANES), F32)] * 3,
        compiler_params=_params("parallel", "arbitrary"))(z, z, z, z, z, z)


def _outproj_loss(x, y_h, y_a, w_out_full, final_gain, target):
    s, d = x.shape
    e = y_h.shape[1]
    tm = _tile(s, 256)

    def body(x_ref, yh_ref, ya_ref, w_ref, g_ref, t_ref, dx_ref, dxb_ref, dy_ref, loss_ref, dg_ref):
        @pl.when(pl.program_id(0) == 0)
        def _():
            loss_ref[...] = jnp.zeros_like(loss_ref)
            dg_ref[...] = jnp.zeros_like(dg_ref)

        w = w_ref[...]
        x2 = x_ref[...] + _dot(yh_ref[...], w[0:e]) + _dot(ya_ref[...], w[e:2 * e])
        r = lax.rsqrt(jnp.mean(x2 * x2, axis=-1, keepdims=True) + NORM_EPS)
        xn = x2 * r
        g = g_ref[...]
        err = xn * g - t_ref[...]
        loss_ref[...] += jnp.sum(err * err, axis=0, keepdims=True) * (0.5 / d)
        dyo = err * (1.0 / d)
        dg_ref[...] += jnp.sum(dyo * xn, axis=0, keepdims=True)
        u = dyo * g
        dx2 = r * (u - xn * jnp.mean(u * xn, axis=-1, keepdims=True))
        dx_ref[...] = dx2
        dxb = dx2.astype(BF16)
        dxb_ref[...] = dxb
        dy_ref[...] = _dot_nt(dxb, w)

    row = pl.BlockSpec((tm, d), lambda i: (i, 0))
    half = pl.BlockSpec((tm, e), lambda i: (i, 0))
    vec = pl.BlockSpec((1, d), lambda i: (0, 0))
    return pl.pallas_call(
        body, name="outproj_loss", grid=(s // tm,),
        out_shape=(jax.ShapeDtypeStruct((s, d), F32), jax.ShapeDtypeStruct((s, d), BF16),
                   jax.ShapeDtypeStruct((s, 2 * e), F32), jax.ShapeDtypeStruct((1, d), F32),
                   jax.ShapeDtypeStruct((1, d), F32)),
        in_specs=[row, half, half, pl.BlockSpec((2 * e, d), lambda i: (0, 0)), vec, row],
        out_specs=(row, row, pl.BlockSpec((tm, 2 * e), lambda i: (i, 0)), vec, vec),
        compiler_params=_params("arbitrary"))(x, y_h, y_a, w_out_full, final_gain, target)


def _attn_bwd(z, dy, o, lse):
    s, e = o.shape
    npair = e // LANES
    T = ATTN_T
    assert s % T == 0
    nsb = s // T
    W = ATTN_BAND
    nt = T // W
    HD = ATTN_HEAD
    chunk = 256

    def body(k_ref, v_ref, qc_ref, qn_ref, dyc_ref, dyn_ref, gc_ref, gn_ref, oc_ref, on_ref, lc_ref, ln_ref,
             dz_ref, qa, doa, ka, va, dqacc, dkacc, dvacc, bias):
        sb = pl.program_id(1)
        def stage_queries(half, q_r, dy_r, g_r, o_r, l_r):
            def stage(i, carry):
                rows = pl.ds(pl.multiple_of(i * chunk, chunk), chunk)
                dst = pl.ds(pl.multiple_of(half * T + i * chunk, chunk), chunk)
                lane = lax.broadcasted_iota(jnp.int32, (chunk, LANES), 1)
                gp = g_r[rows, :]
                dov = dy_r[rows, :] * (gp * _sigmoid(gp))
                qv = q_r[rows, :] * SCALE
                same_head = (lax.broadcasted_iota(jnp.int32, (LANES, LANES), 0) // HD
                             == lax.broadcasted_iota(jnp.int32, (LANES, LANES), 1) // HD)
                ones = jnp.where(same_head, 1.0, 0.0).astype(BF16)
                hi, mid, lo = (p.astype(BF16) for p in _split3(dov * o_r[rows, :]))
                delta = _dot(hi, ones) + _dot(mid, ones) + _dot(lo, ones)
                swap = lambda a: pltpu.roll(a, HD, 1)
                lse_parts = [swap(p) for p in _split3(l_r[rows, :])]
                dl_parts = [swap(p) for p in _split3(delta)]
                for hh in range(2):
                    mine = _head_lanes(chunk, hh)
                    spare = (1 - hh) * HD
                    qh = jnp.where(mine, qv, 0.0)
                    dh = jnp.where(mine, dov, 0.0)
                    for j in range(3):
                        qh = jnp.where(lane == spare + j, lse_parts[j], qh)
                        dh = jnp.where(lane == spare + j, dl_parts[j], dh)
                    qa[hh, dst, :] = qh
                    doa[hh, dst, :] = dh
                return carry

            lax.fori_loop(0, T // chunk, stage, 0)

        @pl.when(sb == 0)
        def _():
            stage_queries(0, qc_ref, dyc_ref, gc_ref, oc_ref, lc_ref)

        stage_queries(1, qn_ref, dyn_ref, gn_ref, on_ref, ln_ref)

        def stage_keys(i, carry):
            rows = pl.ds(pl.multiple_of(i * chunk, chunk), chunk)
            lane = lax.broadcasted_iota(jnp.int32, (chunk, LANES), 1)
            for hh in range(2):
                spare = (1 - hh) * HD
                minus = (lane >= spare) & (lane < spare + 3)
                ka[hh, rows, :] = jnp.where(minus, -1.0, k_ref[rows, :])
                va[hh, rows, :] = jnp.where(minus, -1.0, v_ref[rows, :])
            gp = gc_ref[rows, :]
            dz_ref[3, rows, :] = (dyc_ref[rows, :] * oc_ref[rows, :] * _dsilu(gp, _sigmoid(gp))).astype(BF16)
            return carry

        lax.fori_loop(0, T // chunk, stage_keys, 0)

        @pl.when(sb == 0)
        def _():
            dqacc[0:T, :] = jnp.zeros((T, LANES), F32)

        dqacc[T:, :] = jnp.zeros((T, LANES), F32)
        dkacc[...] = jnp.zeros_like(dkacc)
        dvacc[...] = jnp.zeros_like(dvacc)
        qi = lax.broadcasted_iota(jnp.int32, (2 * W, W), 0)
        kj = lax.broadcasted_iota(jnp.int32, (2 * W, W), 1)
        _fill_bias(bias, 2 * npair, qi - kj, qi < W)

        def tile(tau, carry):
            def scores(step, pi):
                dil = DILATIONS[pi]
                r = step % dil
                ub = step // dil
                start = r + dil * W * ub
                krows = _rows(start, W, dil)
                qrows = _rows(start, 2 * W, dil)
                var = jnp.where((sb == nsb - 1) & (ub == nt // dil - 1), 1, 0)
                unit = dict(krows=krows, qrows=qrows, ops=[], sc=[], dpd=[])
                for hh in range(2):
                    kt = ka[hh, krows, :].astype(BF16)
                    vt = va[hh, krows, :].astype(BF16)
                    qt = qa[hh, qrows, :].astype(BF16)
                    dt = doa[hh, qrows, :].astype(BF16)
                    unit["ops"].append((kt, qt, dt))
                    unit["sc"].append(_dot_nt(qt, kt) + bias[(pi * 2 + hh) * 2 + var])
                    unit["dpd"].append(_dot_nt(dt, vt))
                return unit

            def elementwise(unit):
                ps = [jnp.exp(s_) for s_ in unit["sc"]]
                unit["ds"] = [(p * d).astype(BF16) for p, d in zip(ps, unit["dpd"])]
                unit["pb"] = [p.astype(BF16) for p in ps]

            def products(unit):
                dvs = [_dot_tn(pb, dt) for pb, (kt, qt, dt) in zip(unit["pb"], unit["ops"])]
                dks = [_dot_tn(ds, qt) for ds, (kt, qt, dt) in zip(unit["ds"], unit["ops"])]
                dqs = [_dot(ds, kt) for ds, (kt, qt, dt) in zip(unit["ds"], unit["ops"])]
                dkacc[unit["krows"], :] += jnp.where(_head_lanes(W, 0), dks[0], dks[1])
                dvacc[unit["krows"], :] += jnp.where(_head_lanes(W, 0), dvs[0], dvs[1])
                dqacc[unit["qrows"], :] += jnp.where(_head_lanes(2 * W, 0), dqs[0], dqs[1]) * SCALE

            order = [(2 * tau + half, pi) for half in range(2) for pi in range(len(DILATIONS))]
            units = [None] * len(order)
            for n in range(len(order) + 2):
                if n < len(order):
                    units[n] = scores(*order[n])
                if 1 <= n <= len(order):
                    elementwise(units[n - 1])
                if n >= 2:
                    products(units[n - 2])
            return carry

        lax.fori_loop(0, nt // 2, tile, 0)

        def flush(i, carry):
            rows = pl.ds(pl.multiple_of(i * chunk, chunk), chunk)
            nxt = pl.ds(pl.multiple_of(T + i * chunk, chunk), chunk)
            dz_ref[0, rows, :] = dqacc[rows, :].astype(BF16)
            dz_ref[1, rows, :] = dkacc[rows, :].astype(BF16)
            dz_ref[2, rows, :] = dvacc[rows, :].astype(BF16)
            dqacc[rows, :] = dqacc[nxt, :]
            for hh in range(2):
                qa[hh, rows, :] = qa[hh, nxt, :]
                doa[hh, rows, :] = doa[hh, nxt, :]
            return carry

        lax.fori_loop(0, T // chunk, flush, 0)

    zc = lambda split: (lambda hp, sb: (sb, split * npair + hp))
    zn = lambda split: (lambda hp, sb: (jnp.minimum(sb + 1, nsb - 1), split * npair + hp))
    ec = lambda off: (lambda hp, sb: (sb, off + hp))
    en = lambda off: (lambda hp, sb: (jnp.minimum(sb + 1, nsb - 1), off + hp))
    z0 = lambda split: (lambda hp, sb: (0, split * npair + hp))
    e0 = lambda off: (lambda hp, sb: (0, off + hp))
    blk = lambda index: pl.BlockSpec((T, LANES), index)
    buf = lambda rows: pltpu.VMEM((rows, LANES), F32)
    return pl.pallas_call(
        body, name="attn_bwd", grid=(npair, nsb), out_shape=jax.ShapeDtypeStruct((4, s, e), BF16),
        in_specs=[blk(zc(5)), blk(zc(6)), blk(z0(4)), blk(zn(4)), blk(ec(npair)), blk(en(npair)),
                  blk(zc(7)), blk(zn(7)), blk(ec(0)), blk(en(0)), blk(e0(0)), blk(en(0))],
        out_specs=pl.BlockSpec((4, T, LANES), lambda hp, sb: (0, sb, hp)),
        scratch_shapes=[pltpu.VMEM((2, 2 * T, LANES), F32), pltpu.VMEM((2, 2 * T, LANES), F32),
                        pltpu.VMEM((2, T, LANES), F32), pltpu.VMEM((2, T, LANES), F32),
                        buf(2 * T), buf(T), buf(T), pltpu.VMEM((12, 2 * W, W), F32)],
        compiler_params=_params("parallel", "arbitrary"))(z, z, z, z, dy, dy, z, z, o, o, lse, lse)


def _dz_specs(tm, e):
    def mk(lo, hi):
        return pl.BlockSpec((None, tm, e), lambda i, k: (jnp.clip(k - lo, 0, hi - lo - 1), i, 0))
    return [mk(0, 4), mk(4, 8)]


def _dz_pick(grp, dzh_ref, dza_ref, fn):
    @pl.when(grp < 4)
    def _():
        fn(dzh_ref[...])

    @pl.when(grp >= 4)
    def _():
        fn(dza_ref[...])


def _dh_dx(dzh, dza, w_full, x, gain, dx2):
    s, d = x.shape
    e = dzh.shape[2]
    tm = _tile(s, 1024)
    ni = s // tm
    chunk = _tile(tm, 256)
    fetch_at = 2

    def body(dzh_ref, dza_ref, w_ref, x_hbm, g_ref, dx2_hbm, gx_hbm, dg_ref, acc, xbuf, dbuf, sems):
        i, k = pl.program_id(0), pl.program_id(1)
        tile_rows = pl.ds(pl.multiple_of(i * tm, tm), tm)
        fetch_x = pltpu.make_async_copy(x_hbm.at[tile_rows, :], xbuf, sems.at[0])
        fetch_d = pltpu.make_async_copy(dx2_hbm.at[tile_rows, :], dbuf, sems.at[1])
        store = pltpu.make_async_copy(xbuf, gx_hbm.at[tile_rows, :], sems.at[2])

        @pl.when((i == 0) & (k == 0))
        def _():
            dg_ref[...] = jnp.zeros_like(dg_ref)

        @pl.when(k == 0)
        def _():
            acc[...] = jnp.zeros_like(acc)

        @pl.when((k == fetch_at) & (i > 0))
        def _():
            store.wait()

        @pl.when(k == fetch_at)
        def _():
            fetch_x.start()
            fetch_d.start()

        def add(dz):
            acc[...] += _dot_nt(dz, w_ref[...])

        _dz_pick(k, dzh_ref, dza_ref, add)

        @pl.when(k == N_SPLITS - 1)
        def _():
            fetch_x.wait()
            fetch_d.wait()
            gain_row = g_ref[...]

            def finish(c, dg):
                rows = pl.ds(pl.multiple_of(c * chunk, chunk), chunk)
                dh = acc[rows, :]
                xv = xbuf[rows, :]
                r = lax.rsqrt(jnp.mean(xv * xv, axis=-1, keepdims=True) + NORM_EPS)
                xn = xv * r
                u = dh * gain_row
                xbuf[rows, :] = dbuf[rows, :] + r * (u - xn * jnp.mean(u * xn, axis=-1, keepdims=True))
                return dg + jnp.sum(dh * xn, axis=0, keepdims=True)

            dg_ref[...] += lax.fori_loop(0, tm // chunk, finish, jnp.zeros((1, d), F32))
            store.start()

        @pl.when((k == N_SPLITS - 1) & (i == ni - 1))
        def _():
            store.wait()

    vec = pl.BlockSpec((1, d), lambda i, k: (0, 0))
    return pl.pallas_call(
        body, name="dh_dx", grid=(ni, N_SPLITS),
        out_shape=(jax.ShapeDtypeStruct((s, d), F32), jax.ShapeDtypeStruct((1, d), F32)),
        in_specs=_dz_specs(tm, e) + [pl.BlockSpec((None, d, e), lambda i, k: (k, 0, 0)), ANY, vec, ANY],
        out_specs=(ANY, vec),
        scratch_shapes=[pltpu.VMEM((tm, d), F32), pltpu.VMEM((tm, d), F32), pltpu.VMEM((tm, d), F32),
                        pltpu.SemaphoreType.DMA((3,))],
        compiler_params=_params("arbitrary", "arbitrary"))(dzh, dza, w_full, x, gain, dx2)


def _position():
    x, y, c = lax.axis_index("x"), lax.axis_index("y"), lax.axis_index("c")
    return x, y, c


def _xor_peer(x, y, c, mask):
    return (x ^ ((mask >> 2) & 1), y ^ ((mask >> 1) & 1), c ^ (mask & 1))


def _block_order(masks):
    me = 4 * lax.axis_index("x") + 2 * lax.axis_index("y") + lax.axis_index("c")
    return jnp.stack([me ^ m for m in masks]).astype(jnp.int32)


GATHER_MASKS = (0, 1, 4, 5, 2, 3, 6, 7)


def _inproj_gather(h, w_loc, wo_loc):
    s, d = h.shape
    e = w_loc.shape[1]
    tm = _tile(s, 2048)
    ni = s // tm
    pre = max(ni - 2, 0)

    def body(order_ref, h_ref, w_ref, wo_ref, z_ref, wf_ref, wof_ref, wbuf, send_sems, recv_sems, osend, orecv,
             local_sems, wsems):
        j, i = pl.program_id(0), pl.program_id(1)
        x, y, c = _position()
        me, sibling = (x, y, c), (x, y, 1 - c)
        chips = [(1 - x, y), (x, 1 - y), (1 - x, 1 - y)]
        blk = lambda p: 4 * p[0] + 2 * p[1] + p[2]

        def copy(k, block, to, src=None):
            dst = wf_ref.at[blk(block)]
            return pltpu.make_async_remote_copy(
                src_ref=dst if src is None else src, dst_ref=dst, send_sem=send_sems.at[k], recv_sem=recv_sems.at[k],
                device_id=to, device_id_type=MESH)

        first = [copy(0, me, sibling, src=w_ref)] + [copy(1 + q, me, (*chip, c), src=w_ref) for q, chip in enumerate(chips)]
        passed = [copy(4 + q, (*chip, c), sibling) for q, chip in enumerate(chips)]
        mine = pltpu.make_async_copy(w_ref, wf_ref.at[blk(me)], local_sems.at[0])
        ocopies = [pltpu.make_async_remote_copy(
            src_ref=wo_ref, dst_ref=wof_ref.at[blk(me)], send_sem=osend.at[k], recv_sem=orecv.at[k],
            device_id=_xor_peer(x, y, c, k + 1), device_id_type=MESH) for k in range(N_DEV - 1)]
        omine = pltpu.make_async_copy(wo_ref, wof_ref.at[blk(me)], local_sems.at[1])
        blocks = [me, sibling] + [(*chip, c) for chip in chips] + [(*chip, 1 - c) for chip in chips]
        arrive = [None, copy(0, sibling, me)] + [copy(1 + q, (*chip, c), me) for q, chip in enumerate(chips)] \
            + [copy(4 + q, (*chip, 1 - c), me) for q, chip in enumerate(chips)]
        forward = [None, None] + passed + [None, None, None]
        use_order = (0, 1, 2, 5, 3, 6, 4, 7)
        blocks, arrive, forward = ([lst[n] for n in use_order] for lst in (blocks, arrive, forward))

        def load(slot, src):
            return pltpu.make_async_copy(src, wbuf.at[slot], wsems.at[slot])

        @pl.when((j == 0) & (i == 0))
        def _():
            for cp in [mine, omine] + first + ocopies:
                cp.start()
            load(0, w_ref).start()

        for jj in range(N_DEV):
            @pl.when((j == jj) & (i == 0))
            def _():
                load(jj % 2, w_ref).wait()

            if jj + 1 < N_DEV:
                @pl.when((j == jj) & (i == pre))
                def _():
                    arrive[jj + 1].wait_recv()
                    if forward[jj + 1] is not None:
                        forward[jj + 1].start()
                    load((jj + 1) % 2, wf_ref.at[blk(blocks[jj + 1])]).start()

        z_ref[...] = _dot(h_ref[...], wbuf[j % 2])

        @pl.when((j == N_DEV - 1) & (i == ni - 1))
        def _():
            for cp in first + passed:
                cp.wait_send()
            for cp in ocopies:
                cp.wait_send()
                cp.wait_recv()
            mine.wait()
            omine.wait()

    grid_spec = pltpu.PrefetchScalarGridSpec(
        num_scalar_prefetch=1, grid=(N_DEV, ni),
        in_specs=[pl.BlockSpec((tm, d), lambda j, i, o: (i, 0)), ANY, ANY],
        out_specs=(pl.BlockSpec((tm, e), lambda j, i, o: (i, o[j])), ANY, ANY),
        scratch_shapes=[pltpu.VMEM((2, d, e), BF16), pltpu.SemaphoreType.DMA((7,)), pltpu.SemaphoreType.DMA((7,)),
                        pltpu.SemaphoreType.DMA((7,)), pltpu.SemaphoreType.DMA((7,)), pltpu.SemaphoreType.DMA((2,)),
                        pltpu.SemaphoreType.DMA((2,))])
    return pl.pallas_call(
        body, name="inproj_gather", grid_spec=grid_spec,
        out_shape=(jax.ShapeDtypeStruct((s, N_SPLITS * e), F32), jax.ShapeDtypeStruct((N_DEV, d, e), BF16),
                   jax.ShapeDtypeStruct((N_DEV,) + wo_loc.shape, BF16)),
        compiler_params=_params("arbitrary", "arbitrary"))(_block_order(GATHER_MASKS), h, w_loc, wo_loc)


SCATTER_MASKS = (7, 6, 5, 4, 3, 2, 1, 0)
N_CHIPS = 4


def _scatter_block(k, acc, stage, tmp, own_ref, ra_ref, rb_ref, sa_send, sa_recv, sb_send, sb_recv, loc_sem, step, ns):
    x, y, c = _position()
    chip_of = lambda t: _xor_peer(x, y, c, SCATTER_MASKS[2 * t + 1])
    last = step == ns - 1
    fetch_at = min(1, ns - 1)

    def ship(t):
        return pltpu.make_async_remote_copy(
            src_ref=stage.at[0], dst_ref=ra_ref.at[t], send_sem=sa_send.at[t], recv_sem=sa_recv.at[t],
            device_id=(x, y, 1 - c), device_id_type=MESH)

    def send(t):
        return pltpu.make_async_remote_copy(
            src_ref=stage.at[1], dst_ref=rb_ref.at[t], send_sem=sb_send.at[t], recv_sem=sb_recv.at[t],
            device_id=chip_of(t), device_id_type=MESH)

    for kk in range(N_DEV):
        t = kk // 2
        fetch = pltpu.make_async_copy(ra_ref.at[t], tmp, loc_sem)

        if kk % 2 == 1:
            @pl.when((step == fetch_at) & (k == kk))
            def _():
                ship(t).wait_recv()
                fetch.start()

        @pl.when(last & (k == kk))
        def _():
            if kk % 2 == 0:
                if t >= 1:
                    ship(t - 1).wait_send()
                stage[0] = acc[...].astype(BF16)
                ship(t).start()
            else:
                fetch.wait()
                acc[...] += tmp[...].astype(F32)
                if t < N_CHIPS - 1:
                    if t >= 1:
                        send(t - 1).wait_send()
                    stage[1] = acc[...].astype(BF16)
                    send(t).start()
                else:
                    keep = pltpu.make_async_copy(acc, own_ref, loc_sem)
                    keep.start()
                    keep.wait()
                    ship(t).wait_send()
                    send(t - 1).wait_send()
                    for q in range(N_CHIPS - 1):
                        send(q).wait_recv()


def _scatter_scratch(rows, cols):
    return [pltpu.VMEM((rows, cols), F32), pltpu.VMEM((2, rows, cols), BF16), pltpu.VMEM((rows, cols), BF16),
            pltpu.SemaphoreType.DMA((N_CHIPS,)), pltpu.SemaphoreType.DMA((N_CHIPS,)),
            pltpu.SemaphoreType.DMA((N_CHIPS - 1,)), pltpu.SemaphoreType.DMA((N_CHIPS - 1,)), pltpu.SemaphoreType.DMA(())]


def _scatter_out(rows, cols):
    return (jax.ShapeDtypeStruct((rows, cols), F32), jax.ShapeDtypeStruct((N_CHIPS, rows, cols), BF16),
            jax.ShapeDtypeStruct((N_CHIPS - 1, rows, cols), BF16))


def _dwin_scatter(h, dzh, dza):
    s, d = h.shape
    e = dzh.shape[2]
    ts = _tile(s, 1024)
    ns = s // ts

    def body(order_ref, dzh_ref, dza_ref, h_ref, own_ref, ra_ref, rb_ref, acc, stage, tmp, *sems):
        k, step = pl.program_id(0), pl.program_id(1)

        @pl.when(step == 0)
        def _():
            acc[...] = jnp.zeros_like(acc)

        def add(dz):
            acc[...] += _dot_tn(h_ref[...], dz)

        _dz_pick(order_ref[k], dzh_ref, dza_ref, add)
        _scatter_block(k, acc, stage, tmp, own_ref, ra_ref, rb_ref, *sems, step, ns)

    def dz_spec(lo):
        return pl.BlockSpec((None, ts, e), lambda k, st, o: (jnp.clip(o[k] - lo, 0, 3), st, 0))

    grid_spec = pltpu.PrefetchScalarGridSpec(
        num_scalar_prefetch=1, grid=(N_DEV, ns),
        in_specs=[dz_spec(0), dz_spec(4), pl.BlockSpec((ts, d), lambda k, st, o: (st, 0))],
        out_specs=(ANY, ANY, ANY), scratch_shapes=_scatter_scratch(d, e))
    own, _, rb = pl.pallas_call(
        body, name="dwin_scatter", grid_spec=grid_spec, out_shape=_scatter_out(d, e),
        compiler_params=_params("arbitrary", "arbitrary"))(_block_order(SCATTER_MASKS), dzh, dza, h)
    return own, rb


def _dwout_scatter(y_h, y_a, dxb):
    s, e = y_h.shape
    d = dxb.shape[1]
    r = 2 * e // N_DEV
    pairs = e // (2 * r)
    ts = _tile(s, 1024)
    ns = s // ts
    chip_masks = SCATTER_MASKS[1::2]
    passes = ((0, 1), (2,), (3,))
    slots = max(len(chips) for chips in passes)
    slot_chip = [chips[min(u, len(chips) - 1)] for chips in passes for u in range(slots)]

    def body(pair_ref, yh0_ref, ya0_ref, yh1_ref, ya1_ref, dx_ref, own_ref, ra_ref, rb_ref, acc, keep_buf, ship_buf,
             send_buf, tmp, sa_send, sa_recv, sb_send, sb_recv, loc_sem):
        p, step = pl.program_id(0), pl.program_id(1)
        x, y, c = _position()

        @pl.when(step == 0)
        def _():
            acc[...] = jnp.zeros_like(acc)

        for u, (yh_ref, ya_ref) in enumerate(((yh0_ref, ya0_ref), (yh1_ref, ya1_ref))):
            rows = slice(u * 2 * r, (u + 1) * 2 * r)
            used = functools.reduce(jnp.logical_or, [p == pp for pp, chips in enumerate(passes) if u < len(chips)])

            @pl.when(used & (pair_ref[slots * p + u] < pairs))
            def _():
                acc[rows, :] += _dot_tn(yh_ref[...], dx_ref[...])

            @pl.when(used & (pair_ref[slots * p + u] >= pairs))
            def _():
                acc[rows, :] += _dot_tn(ya_ref[...], dx_ref[...])

        def block_rows(u, core):
            return pl.ds(pl.multiple_of(u * 2 * r + core * r, r), r)

        slot_of = {q: u for chips in passes for u, q in enumerate(chips)}

        def ship(q):
            return pltpu.make_async_remote_copy(
                src_ref=ship_buf.at[slot_of[q]], dst_ref=ra_ref.at[q], send_sem=sa_send.at[q], recv_sem=sa_recv.at[q],
                device_id=(x, y, 1 - c), device_id_type=MESH)

        def send(q):
            return pltpu.make_async_remote_copy(
                src_ref=send_buf.at[slot_of[q]], dst_ref=rb_ref.at[q], send_sem=sb_send.at[q], recv_sem=sb_recv.at[q],
                device_id=_xor_peer(x, y, c, chip_masks[q]), device_id_type=MESH)

        def sibling_share(q):
            ship(q).wait_recv()
            fetch = pltpu.make_async_copy(ra_ref.at[q], tmp, loc_sem)
            fetch.start()
            fetch.wait()
            return tmp[...].astype(F32)

        shipped, sent = {}, {}
        for pp, chips in enumerate(passes):
            @pl.when((step == ns - 1) & (p == pp))
            def _():
                for u, q in enumerate(chips):
                    if u in shipped:
                        ship(shipped.pop(u)).wait_send()
                    ship_buf[u] = acc[block_rows(u, 1 - c), :].astype(BF16)
                    ship(q).start()
                    shipped[u] = q
                for u, q in enumerate(chips):
                    total = acc[block_rows(u, c), :] + sibling_share(q)
                    if q < N_CHIPS - 1:
                        if u in sent:
                            send(sent.pop(u)).wait_send()
                        send_buf[u] = total.astype(BF16)
                        send(q).start()
                        sent[u] = q
                    else:
                        keep_buf[...] = total
                        keep = pltpu.make_async_copy(keep_buf, own_ref, loc_sem)
                        keep.start()
                        keep.wait()
                if pp == len(passes) - 1:
                    for q in shipped.values():
                        ship(q).wait_send()
                    for q in sent.values():
                        send(q).wait_send()
                    for q in range(N_CHIPS - 1):
                        send(q).wait_recv()

    def y_spec(u, lo):
        return pl.BlockSpec((ts, 2 * r), lambda p, st, o: (st, jnp.clip(o[slots * p + u] - lo, 0, pairs - 1)))

    pair_of_chip = _block_order(chip_masks) // 2
    grid_spec = pltpu.PrefetchScalarGridSpec(
        num_scalar_prefetch=1, grid=(len(passes), ns),
        in_specs=[y_spec(0, 0), y_spec(0, pairs), y_spec(1, 0), y_spec(1, pairs),
                  pl.BlockSpec((ts, d), lambda p, st, o: (st, 0))],
        out_specs=(ANY, ANY, ANY),
        scratch_shapes=[pltpu.VMEM((slots * 2 * r, d), F32), pltpu.VMEM((r, d), F32),
                        pltpu.VMEM((slots, r, d), BF16)] + _scatter_scratch(r, d)[1:])
    own, _, rb = pl.pallas_call(
        body, name="dwout_scatter", grid_spec=grid_spec, out_shape=_scatter_out(r, d),
        compiler_params=_params("arbitrary", "arbitrary"))(
            jnp.stack([pair_of_chip[q] for q in slot_chip]), y_h, y_a, y_h, y_a, dxb)
    return own, rb


def _sum_chips_adamw(own, recv, w, m, v):
    r, c = w.shape
    tr = _tile(r, 128)

    def body(own_ref, rc_ref, w_ref, m_ref, v_ref, g_ref, d_ref, mo_ref, vo_ref):
        g = own_ref[...]
        for q in range(N_CHIPS - 1):
            g = g + rc_ref[q].astype(F32)
        g_ref[...] = g
        d_ref[...], mo_ref[...], vo_ref[...] = _adamw(w_ref[...], g, m_ref[...], v_ref[...])

    blk = pl.BlockSpec((tr, c), lambda i: (i, 0))
    shp = jax.ShapeDtypeStruct((r, c), F32)
    return pl.pallas_call(
        body, name="sum_chips_adamw", grid=(r // tr,), out_shape=(shp, shp, shp, shp),
        in_specs=[blk, pl.BlockSpec((N_CHIPS - 1, tr, c), lambda i: (0, i, 0)), blk, blk, blk],
        out_specs=(blk, blk, blk, blk), compiler_params=_params("parallel"))(own, recv, w, m, v)


SMALL_ROWS = 8
ROW_LB = 4
ROW_GN = 6
ROW_LOSS = 7


def _small_allreduce_adamw(part, w, m, v, lb_logits):
    width = part.shape[1]

    def body(p_ref, w_ref, m_ref, v_ref, lb_ref, g_ref, d_ref, mo_ref, vo_ref, buf, send_sems, recv_sems):
        x, y, c = _position()
        me = 4 * x + 2 * y + c
        buf[me] = p_ref[...]
        copies = []
        for k in range(N_DEV - 1):
            bx, by, bc = ((k + 1) >> 2) & 1, ((k + 1) >> 1) & 1, (k + 1) & 1
            peer = (x ^ bx, y ^ by, c ^ bc)
            copies.append(pltpu.make_async_remote_copy(
                src_ref=p_ref, dst_ref=buf.at[me], send_sem=send_sems.at[k], recv_sem=recv_sems.at[k],
                device_id=peer, device_id_type=MESH))
        for cp in copies:
            cp.start()
        for cp in copies:
            cp.wait_recv()
        for cp in copies:
            cp.wait_send()
        tot = buf[0]
        for dev in range(1, N_DEV):
            tot = tot + buf[dev]
        lbv = lb_ref[...]
        lb = _sigmoid(lbv[0:1] - lbv[1:2])
        glb = tot[ROW_LB:ROW_LB + 1] * lb * (1.0 - lb)
        loss = jnp.sum(tot[ROW_LOSS:ROW_LOSS + 1], axis=-1, keepdims=True)
        row = lax.broadcasted_iota(jnp.int32, (SMALL_ROWS, width), 0)
        g = jnp.where(row == ROW_LB, glb, jnp.where(row == ROW_LB + 1, -glb, tot))
        g = jnp.where(row == ROW_LOSS, loss, g)
        g_ref[...] = g
        d_ref[...], mo_ref[...], vo_ref[...] = _adamw(w_ref[...], g, m_ref[...], v_ref[...])

    vm = pl.BlockSpec(memory_space=pltpu.VMEM)
    shp = jax.ShapeDtypeStruct((SMALL_ROWS, width), F32)
    return pl.pallas_call(
        body, name="small_allreduce_adamw", out_shape=(shp, shp, shp, shp),
        in_specs=[vm] * 5, out_specs=(vm, vm, vm, vm),
        scratch_shapes=[pltpu.VMEM((N_DEV, SMALL_ROWS, width), F32), pltpu.SemaphoreType.DMA((N_DEV - 1,)),
                        pltpu.SemaphoreType.DMA((N_DEV - 1,))],
    )(part, w, m, v, lb_logits)


def _pack_small(norm_gain, final_gain, lb2, gnorm, last_row, width):
    pad = lambda a: jnp.pad(a.reshape(1, -1), ((0, 0), (0, width - a.size)))
    return jnp.concatenate([norm_gain.reshape(2, width), final_gain.reshape(2, width), lb2.reshape(2, width),
                            pad(gnorm), last_row.reshape(1, width)], axis=0)


def _unpack_small(p, d, e, hd):
    return (p[0:2].reshape(1, d), p[2:4].reshape(d), p[4:6].reshape(2, e), p[6:7, :hd].reshape(1, hd))


def kernel(x, norm_gain, w_in, lb_logits, hgrn_gnorm, w_out, final_gain, loss_target, m_norm_gain, m_w_in, m_lb_logits, m_hgrn_gnorm, m_w_out, m_final_gain, v_norm_gain, v_w_in, v_lb_logits, v_hgrn_gnorm, v_w_out, v_final_gain):
    s, d = x.shape[1], x.shape[2]
    e = w_in.shape[2]
    assert d == 2 * e and lb_logits.shape == (2, e) and w_out.shape[1] * N_DEV == 2 * e
    x2d = x.reshape(s, d)
    tgt = loss_target.reshape(s, d)

    h = _rmsnorm_fwd(x2d, norm_gain)
    z, w_in_full, w_out_full = _inproj_gather(h, _cast_bf16(w_in[0]), _cast_bf16(w_out[0]))
    w_out_full = w_out_full.reshape(2 * e, d)
    y_h, states = _hgrn_fwd(z, lb_logits, hgrn_gnorm)
    o_attn, lse, y_a = _attn_fwd(z)
    dx2, dx2b, dy, loss_vec, dfg = _outproj_loss(x2d, y_h, y_a, w_out_full, final_gain.reshape(1, d), tgt)

    own_o, recv_o = _dwout_scatter(y_h, y_a, dx2b)
    dza = _attn_bwd(z, dy, o_attn, lse)
    dzh, dlb, dgn = _hgrn_bwd(z, dy, states, lb_logits, hgrn_gnorm)
    grad_x, dng = _dh_dx(dzh, dza, w_in_full, x2d, norm_gain, dx2)
    g_wo, d_wo, nm_wo, nv_wo = _sum_chips_adamw(own_o, recv_o, w_out[0], m_w_out[0], v_w_out[0])

    width = d // 2
    zero_row = jnp.zeros((1, width), F32)
    loss_row = loss_vec[:, :width] + loss_vec[:, width:]
    part = _pack_small(dng, dfg, jnp.concatenate([dlb, zero_row], axis=0), dgn, loss_row, width)
    pw = _pack_small(norm_gain, final_gain, lb_logits, hgrn_gnorm, zero_row, width)
    pm = _pack_small(m_norm_gain, m_final_gain, m_lb_logits, m_hgrn_gnorm, zero_row, width)
    pv = _pack_small(v_norm_gain, v_final_gain, v_lb_logits, v_hgrn_gnorm, zero_row, width)
    sg, sd, sm, sv = _small_allreduce_adamw(part, pw, pm, pv, lb_logits)
    own_i, recv_i = _dwin_scatter(h, dzh, dza)
    g_wi, d_wi, nm_wi, nv_wi = _sum_chips_adamw(own_i, recv_i, w_in[0], m_w_in[0], v_w_in[0])
    hd = hgrn_gnorm.shape[1]
    g_ng, g_fg, g_lb, g_gn = _unpack_small(sg, d, e, hd)
    d_ng, d_fg, d_lb, d_gn = _unpack_small(sd, d, e, hd)
    m_ng, m_fg, m_lb, m_gn = _unpack_small(sm, d, e, hd)
    v_ng, v_fg, v_lb, v_gn = _unpack_small(sv, d, e, hd)
    loss = sg[ROW_LOSS, 0]

    one = lambda a: a[None]
    return (loss, grad_x.reshape(1, s, d), g_ng, one(g_wi), g_lb, g_gn, one(g_wo), g_fg,
            d_ng, one(d_wi), d_lb, d_gn, one(d_wo), d_fg,
            m_ng, one(nm_wi), m_lb, m_gn, one(nm_wo), m_fg,
            v_ng, one(nv_wi), v_lb, v_gn, one(nv_wo), v_fg)
```

```python
import functools
import math

import jax
import jax.numpy as jnp
from jax import lax
from jax.experimental import pallas as pl
from jax.experimental.pallas import tpu as pltpu

NORM_EPS = 1e-6
HGRN_HEAD = 128
HGRN_CHUNK = 64
ATTN_HEAD = 64
ATTN_BAND = 128
DILATIONS = (1, 4, 16)
N_SPLITS = 8
N_DEV = 8
ADAM_LR = 0.001
ADAM_B1 = 0.9
ADAM_B2 = 0.999
ADAM_EPS = 1e-08
ADAM_WD = 0.01
ADAM_STEP = 10
LANES = 128
MESH = pl.DeviceIdType.MESH
F32 = jnp.float32
BF16 = jnp.bfloat16
NEG_BIG = -1e30
VMEM_LIMIT = 56 * 1024 * 1024
OUTPROJ_VMEM_LIMIT = 63 * 1024 * 1024

ANY = pl.BlockSpec(memory_space=pl.ANY)


def _params(*sem):
    return pltpu.CompilerParams(dimension_semantics=sem, vmem_limit_bytes=VMEM_LIMIT)


def _tile(n, pref):
    t = min(n, pref)
    assert n % t == 0, (n, pref)
    return t


def _dot(a, b, precision=None):
    return jnp.dot(a, b, preferred_element_type=F32, precision=precision)


def _dot_nt(a, b):
    return lax.dot_general(a, b, (((1,), (1,)), ((), ())), preferred_element_type=F32)


def _dot_tn(a, b):
    return lax.dot_general(a, b, (((0,), (0,)), ((), ())), preferred_element_type=F32)


def _sigmoid(x):
    return 0.5 * jnp.tanh(0.5 * x) + 0.5


def _dsilu(x, s):
    return s * (1.0 + x * (1.0 - s))


def _adamw(w, g, m, v):
    m = ADAM_B1 * m + (1.0 - ADAM_B1) * g
    v = ADAM_B2 * v + (1.0 - ADAM_B2) * (g * g)
    m_hat = m / (1.0 - ADAM_B1 ** ADAM_STEP)
    v_hat = v / (1.0 - ADAM_B2 ** ADAM_STEP)
    delta = -ADAM_LR * (m_hat / (jnp.sqrt(v_hat) + ADAM_EPS) + ADAM_WD * w)
    return delta, m, v


def _cast_bf16(a):
    r, c = a.shape
    tr = _tile(r, 256)

    def body(a_ref, o_ref):
        o_ref[...] = a_ref[...].astype(BF16)

    return pl.pallas_call(
        body, name="cast_bf16", grid=(r // tr,), out_shape=jax.ShapeDtypeStruct((r, c), BF16),
        in_specs=[pl.BlockSpec((tr, c), lambda i: (i, 0))], out_specs=pl.BlockSpec((tr, c), lambda i: (i, 0)),
        compiler_params=_params("parallel"))(a)


def _rmsnorm_fwd(x, gain):
    s, d = x.shape
    tm = _tile(s, 512)

    def body(x_ref, g_ref, h_ref):
        xv = x_ref[...]
        r = lax.rsqrt(jnp.mean(xv * xv, axis=-1, keepdims=True) + NORM_EPS)
        h_ref[...] = (xv * r * g_ref[...]).astype(BF16)

    return pl.pallas_call(
        body, name="rmsnorm_fwd", grid=(s // tm,), out_shape=jax.ShapeDtypeStruct((s, d), BF16),
        in_specs=[pl.BlockSpec((tm, d), lambda i: (i, 0)), pl.BlockSpec((1, d), lambda i: (0, 0))],
        out_specs=pl.BlockSpec((tm, d), lambda i: (i, 0)), compiler_params=_params("parallel"))(x, gain)


HGRN_BLOCK = 2048
TRI_ROWS = 256


def _chunk_masks():
    tb = TRI_ROWS
    row = lax.broadcasted_iota(jnp.int32, (tb, tb), 0)
    col = lax.broadcasted_iota(jnp.int32, (tb, tb), 1)
    same = (row // HGRN_CHUNK) == (col // HGRN_CHUNK)
    lower = jnp.where(same & (col <= row), 1.0, 0.0).astype(BF16)
    upper = jnp.where(same & (col >= row), 1.0, 0.0).astype(BF16)
    return lower, upper


def _split3(a):
    hi = a.astype(BF16).astype(F32)
    mid = (a - hi).astype(BF16).astype(F32)
    lo = (a - hi - mid).astype(BF16).astype(F32)
    return hi, mid, lo


def _tri_dot(tri, x):
    hi, mid, lo = (p.astype(BF16) for p in _split3(x))
    outs = []
    for r in range(0, x.shape[0], TRI_ROWS):
        sl = slice(r, r + TRI_ROWS)
        outs.append(_dot(tri, hi[sl]) + _dot(tri, mid[sl]) + _dot(tri, lo[sl]))
    return outs[0] if len(outs) == 1 else jnp.concatenate(outs, axis=0)


def _hgrn_gates(qp, fp, lbv):
    lb = _sigmoid(lbv[0:1] - lbv[1:2])
    sq = _sigmoid(qp)
    q = qp * sq
    sg = _sigmoid(fp)
    f = lb + (1.0 - lb) * sg
    k = 1.0 - f
    return lb, sq, q, sg, f, k


def _hgrn_fwd(z, lb_logits, gnorm):
    s = z.shape[0]
    e = z.shape[1] // N_SPLITS
    nh = e // HGRN_HEAD
    tb = _tile(s, HGRN_BLOCK)
    nc = tb // HGRN_CHUNK
    nb = s // tb
    C = HGRN_CHUNK

    def body(q_ref, f_ref, i_ref, g_ref, lb_ref, gn_ref, y_ref, st_ref, state, o_scr):
        @pl.when(pl.program_id(1) == 0)
        def _():
            state[...] = jnp.zeros_like(state)

        lb, sq, q, sg, f, k = _hgrn_gates(q_ref[...], f_ref[...], lb_ref[...])
        lower, _ = _chunk_masks()
        b = _tri_dot(lower, jnp.log(f))
        b3 = b.reshape(nc, C, HGRN_HEAD)
        bc = b3[:, C - 1:C, :]
        qt = (q * jnp.exp(b)).astype(BF16)
        kt = (k * jnp.exp(-b)).astype(BF16)
        ke = (k.reshape(nc, C, HGRN_HEAD) * jnp.exp(bc - b3)).reshape(tb, HGRN_HEAD).astype(BF16)
        v = i_ref[...].astype(BF16)
        tri = lax.broadcasted_iota(jnp.int32, (C, C), 1) <= lax.broadcasted_iota(jnp.int32, (C, C), 0)
        sls = [slice(c * C, (c + 1) * C) for c in range(nc)]
        kv = [_dot_tn(v[sl], ke[sl]) for sl in sls]
        a = [jnp.where(tri, _dot_nt(qt[sl], kt[sl]), 0.0).astype(BF16) for sl in sls]
        st = state[...]
        sts = []
        for c in range(nc):
            sts.append(st)
            st_ref[c] = st
            st = st * jnp.exp(bc[c]) + kv[c]
        state[...] = st
        for c, sl in enumerate(sls):
            o_scr[sl, :] = _dot(a[c], v[sl]) + _dot_nt(qt[sl], sts[c].astype(BF16))
        o = o_scr[...]
        rms = lax.rsqrt(jnp.mean(o * o, axis=-1, keepdims=True) + NORM_EPS)
        gp = g_ref[...]
        y_ref[...] = (o * rms * gn_ref[...] * (gp * _sigmoid(gp))).astype(BF16)

    col = lambda kk: (lambda h, n: (n, kk * nh + h))
    return pl.pallas_call(
        body, name="hgrn_fwd", grid=(nh, nb),
        out_shape=(jax.ShapeDtypeStruct((s, e), BF16),
                   jax.ShapeDtypeStruct((nh, s // C, HGRN_HEAD, HGRN_HEAD), F32)),
        in_specs=[pl.BlockSpec((tb, HGRN_HEAD), col(0)), pl.BlockSpec((tb, HGRN_HEAD), col(1)),
                  pl.BlockSpec((tb, HGRN_HEAD), col(2)), pl.BlockSpec((tb, HGRN_HEAD), col(3)),
                  pl.BlockSpec((2, HGRN_HEAD), lambda h, n: (0, h)), pl.BlockSpec((1, HGRN_HEAD), lambda h, n: (0, 0))],
        out_specs=(pl.BlockSpec((tb, HGRN_HEAD), lambda h, n: (n, h)),
                   pl.BlockSpec((None, nc, HGRN_HEAD, HGRN_HEAD), lambda h, n: (h, n, 0, 0))),
        scratch_shapes=[pltpu.VMEM((HGRN_HEAD, HGRN_HEAD), F32), pltpu.VMEM((tb, HGRN_HEAD), F32)],
        compiler_params=_params("parallel", "arbitrary"))(z, z, z, z, lb_logits, gnorm)


def _hgrn_bwd(z, dy, states, lb_logits, gnorm):
    s = z.shape[0]
    e = z.shape[1] // N_SPLITS
    nh = e // HGRN_HEAD
    tb = _tile(s, HGRN_BLOCK)
    nc = tb // HGRN_CHUNK
    nb = s // tb
    C = HGRN_CHUNK
    H = HGRN_HEAD

    def body(q_ref, f_ref, i_ref, g_ref, dy_ref, st_ref, lb_ref, gn_ref, dz_ref, dlb_ref, dgn_ref,
             gstate, o_scr, dq_scr, dk_scr, dv_scr, e_scr):
        first = (pl.program_id(0) == 0) & (pl.program_id(1) == 0)

        @pl.when(first)
        def _():
            dgn_ref[...] = jnp.zeros_like(dgn_ref)

        @pl.when(pl.program_id(1) == 0)
        def _():
            gstate[...] = jnp.zeros_like(gstate)
            dlb_ref[...] = jnp.zeros_like(dlb_ref)

        qp = q_ref[...]
        lb, sq, q, sg, f, k = _hgrn_gates(qp, f_ref[...], lb_ref[...])
        lower, upper = _chunk_masks()
        b = _tri_dot(lower, jnp.log(f))
        b3 = b.reshape(nc, C, H)
        bc = b3[:, C - 1:C, :]
        eb = jnp.exp(b)
        enb = jnp.exp(-b)
        eend = jnp.exp(bc - b3).reshape(tb, H)
        qt = (q * eb).astype(BF16)
        kt = (k * enb).astype(BF16)
        ke = (k * eend).astype(BF16)
        v = i_ref[...].astype(BF16)
        tri = lax.broadcasted_iota(jnp.int32, (C, C), 1) <= lax.broadcasted_iota(jnp.int32, (C, C), 0)
        sls = [slice(c * C, (c + 1) * C) for c in range(nc)]
        a = [jnp.where(tri, _dot_nt(qt[sl], kt[sl]), 0.0).astype(BF16) for sl in sls]
        for c, sl in enumerate(sls):
            o_scr[sl, :] = _dot(a[c], v[sl]) + _dot_nt(qt[sl], st_ref[c].astype(BF16))
        o = o_scr[...]
        rms = lax.rsqrt(jnp.mean(o * o, axis=-1, keepdims=True) + NORM_EPS)
        on = o * rms
        gn = gn_ref[...]
        gp = g_ref[...]
        sgg = _sigmoid(gp)
        dyv = dy_ref[...]
        d_on = dyv * (gp * sgg)
        dz_ref[3] = (dyv * on * gn * _dsilu(gp, sgg)).astype(BF16)
        dgn_ref[...] += jnp.sum(d_on * on, axis=0, keepdims=True)
        u = d_on * gn
        do = (rms * (u - on * jnp.mean(u * on, axis=-1, keepdims=True))).astype(BF16)
        gup = [_dot_tn(do[sl], qt[sl]) for sl in sls]
        da = [jnp.where(tri, _dot_nt(do[sl], v[sl]), 0.0).astype(BF16) for sl in sls]
        gt = gstate[...]
        gts = [None] * nc
        for c in reversed(range(nc)):
            gts[c] = gt
            gt = gt * jnp.exp(bc[c]) + gup[c]
        gstate[...] = gt
        for c, sl in enumerate(sls):
            stp = st_ref[c]
            gtb = gts[c].astype(BF16)
            dqt = _dot(da[c], kt[sl]) + _dot(do[sl], stp.astype(BF16))
            dkt = _dot_tn(da[c], qt[sl])
            dks = _dot(v[sl], gtb) * eend[sl]
            dv_scr[sl, :] = _dot_tn(a[c], do[sl]) + _dot_nt(ke[sl], gtb)
            dq_scr[sl, :] = dqt * eb[sl]
            dk_scr[sl, :] = dkt * enb[sl] + dks
            ech = (jnp.sum(k[sl] * dks, axis=0, keepdims=True)
                   + jnp.sum(gts[c] * jnp.exp(bc[c]) * stp, axis=0, keepdims=True))
            e_scr[sl, :] = jnp.broadcast_to(ech, (C, H))
        dq = dq_scr[...]
        dk = dk_scr[...]
        dlf = _tri_dot(upper, q * dq - k * dk) + e_scr[...]
        dft = dlf / f - dk
        dz_ref[0] = (dq * _dsilu(qp, sq)).astype(BF16)
        dz_ref[1] = (dft * (1.0 - lb) * sg * (1.0 - sg)).astype(BF16)
        dz_ref[2] = dv_scr[...].astype(BF16)
        dlb_ref[...] += jnp.sum(dft * (1.0 - sg), axis=0, keepdims=True)

    col = lambda kk: (lambda h, n: (nb - 1 - n, kk * nh + h))
    return pl.pallas_call(
        body, name="hgrn_bwd", grid=(nh, nb),
        out_shape=(jax.ShapeDtypeStruct((4, s, e), BF16), jax.ShapeDtypeStruct((1, e), F32),
                   jax.ShapeDtypeStruct((1, H), F32)),
        in_specs=[pl.BlockSpec((tb, H), col(0)), pl.BlockSpec((tb, H), col(1)),
                  pl.BlockSpec((tb, H), col(2)), pl.BlockSpec((tb, H), col(3)),
                  pl.BlockSpec((tb, H), lambda h, n: (nb - 1 - n, h)),
                  pl.BlockSpec((None, nc, H, H), lambda h, n: (h, nb - 1 - n, 0, 0)),
                  pl.BlockSpec((2, H), lambda h, n: (0, h)), pl.BlockSpec((1, H), lambda h, n: (0, 0))],
        out_specs=(pl.BlockSpec((4, tb, H), lambda h, n: (0, nb - 1 - n, h)),
                   pl.BlockSpec((1, H), lambda h, n: (0, h)), pl.BlockSpec((1, H), lambda h, n: (0, 0))),
        scratch_shapes=[pltpu.VMEM((H, H), F32)] + [pltpu.VMEM((tb, H), F32)] * 5,
        compiler_params=_params("arbitrary", "arbitrary"))(z, z, z, z, dy, states, lb_logits, gnorm)


ATTN_T = 16 * ATTN_BAND
SCALE = ATTN_HEAD ** -0.5
TILE_UNROLL = 2


def _slope(hh, nheads):
    head = (2 * pl.program_id(0) + hh + 1).astype(F32)
    return jnp.exp(jnp.full((1, 1), -8.0 / nheads * math.log(2.0), F32) * head)


def _fill_bias(bias, nheads, delta, edge_ok):
    band = (delta >= 0) & (delta <= ATTN_BAND)
    dist = delta.astype(F32)
    for pi, dil in enumerate(DILATIONS):
        for hh in range(2):
            full = jnp.where(band, -(_slope(hh, nheads) * float(dil)) * dist, NEG_BIG)
            bias[(pi * 2 + hh) * 2] = full
            bias[(pi * 2 + hh) * 2 + 1] = jnp.where(edge_ok, full, NEG_BIG)


def _rows(start, size, stride):
    if stride == 1:
        return pl.ds(pl.multiple_of(start, ATTN_BAND), size)
    return pl.ds(start, size, stride=stride)


def _head_lanes(rows, hh):
    return (lax.broadcasted_iota(jnp.int32, (rows, LANES), 1) // ATTN_HEAD) == hh


def _attn_fwd(z):
    s = z.shape[0]
    e = z.shape[1] // N_SPLITS
    npair = e // LANES
    T = ATTN_T
    assert s % T == 0
    nsb = s // T
    W = ATTN_BAND
    nt = T // W
    HD = ATTN_HEAD
    chunk = 256

    def body(q_ref, kp_ref, kc_ref, vp_ref, vc_ref, g_ref, o_ref, l_ref, y_ref, qa, kbuf, va, bias, accs, ms, lsw):
        sb = pl.program_id(1)
        def stage(i, carry):
            rows = pl.ds(pl.multiple_of(i * chunk, chunk), chunk)
            upper = pl.ds(pl.multiple_of(T + i * chunk, chunk), chunk)
            kbuf[upper, :] = kc_ref[rows, :]
            for hh in range(2):
                mine = _head_lanes(chunk, hh)
                qa[hh, rows, :] = jnp.where(mine, q_ref[rows, :] * SCALE, 0.0)
                va[hh, upper, :] = jnp.where(mine, vc_ref[rows, :], 1.0)
            return carry

        lax.fori_loop(0, T // chunk, stage, 0)

        @pl.when(sb == 0)
        def _():
            def stage_prev(i, carry):
                rows = pl.ds(pl.multiple_of(i * chunk, chunk), chunk)
                kbuf[rows, :] = kp_ref[rows, :]
                for hh in range(2):
                    va[hh, rows, :] = jnp.where(_head_lanes(chunk, hh), vp_ref[rows, :], 1.0)
                return carry

            lax.fori_loop(0, T // chunk, stage_prev, 0)
        qi = lax.broadcasted_iota(jnp.int32, (W, 2 * W), 0)
        kj = lax.broadcasted_iota(jnp.int32, (W, 2 * W), 1)
        _fill_bias(bias, 2 * npair, W + qi - kj, kj >= W)

        def tile(tau, carry):
            first = _head_lanes(W, 0)
            rows, scores = [], []
            for pi, dil in enumerate(DILATIONS):
                r = tau % dil
                ub = tau // dil
                qrows = _rows(r + dil * W * ub, W, dil)
                krows = _rows(T + dil * W * (ub - 1) + r, 2 * W, dil)
                var = jnp.where((sb == 0) & (ub == 0), 1, 0)
                kt = kbuf[krows, :].astype(BF16)
                rows.append((qrows, krows))
                scores.append([_dot_nt(qa[hh, qrows, :].astype(BF16), kt) + bias[(pi * 2 + hh) * 2 + var]
                               for hh in range(2)])
            maxes = [[jnp.max(sc, axis=-1, keepdims=True) for sc in pair] for pair in scores]
            probs = [[jnp.exp(sc - m).astype(BF16) for sc, m in zip(ps, pm)] for ps, pm in zip(scores, maxes)]
            for pi, (qrows, krows) in enumerate(rows):
                outs = [_dot(probs[pi][hh], va[hh, krows, :].astype(BF16)) for hh in range(2)]
                accs[pi, qrows, :] = jnp.where(first, outs[0], outs[1])
                lsw[pi, qrows, :] = jnp.where(first, outs[1], outs[0])
                ms[pi, qrows, :] = jnp.where(first, maxes[pi][0], maxes[pi][1])
            return carry

        lax.fori_loop(0, nt, tile, 0, unroll=TILE_UNROLL)

        def merge(i, carry):
            rows = pl.ds(pl.multiple_of(i * chunk, chunk), chunk)
            m1, m2, m3 = ms[0, rows, :], ms[1, rows, :], ms[2, rows, :]
            mx = jnp.maximum(jnp.maximum(m1, m2), m3)
            w1, w2, w3 = jnp.exp(m1 - mx), jnp.exp(m2 - mx), jnp.exp(m3 - mx)
            unswap = lambda a: pltpu.roll(a, ATTN_HEAD, 1)
            den = w1 * unswap(lsw[0, rows, :]) + w2 * unswap(lsw[1, rows, :]) + w3 * unswap(lsw[2, rows, :])
            o = (w1 * accs[0, rows, :] + w2 * accs[1, rows, :] + w3 * accs[2, rows, :]) / den
            o_ref[rows, :] = o
            l_ref[rows, :] = mx + jnp.log(den)
            gp = g_ref[rows, :]
            y_ref[rows, :] = (o * (gp * _sigmoid(gp))).astype(BF16)
            upper = pl.ds(pl.multiple_of(T + i * chunk, chunk), chunk)
            kbuf[rows, :] = kbuf[upper, :]
            for hh in range(2):
                va[hh, rows, :] = va[hh, upper, :]
            return carry

        lax.fori_loop(0, T // chunk, merge, 0)

    cur = lambda split: (lambda hp, sb: (sb, split * npair + hp))
    prev = lambda split: (lambda hp, sb: (0, split * npair + hp))
    blk = lambda index: pl.BlockSpec((T, LANES), index)
    out = blk(lambda hp, sb: (sb, hp))
    buf = lambda rows: pltpu.VMEM((rows, LANES), F32)
    return pl.pallas_call(
        body, name="attn_fwd", grid=(npair, nsb),
        out_shape=(jax.ShapeDtypeStruct((s, e), F32), jax.ShapeDtypeStruct((s, e), F32), jax.ShapeDtypeStruct((s, e), BF16)),
        in_specs=[blk(cur(4)), blk(prev(5)), blk(cur(5)), blk(prev(6)), blk(cur(6)), blk(cur(7))],
        out_specs=(out, out, out),
        scratch_shapes=[pltpu.VMEM((2, T, LANES), F32), buf(2 * T), pltpu.VMEM((2, 2 * T, LANES), F32),
                        pltpu.VMEM((12, W, 2 * W), F32)] + [pltpu.VMEM((3, T, LANES), F32)] * 3,
        compiler_params=_params("parallel", "arbitrary"))(z, z, z, z, z, z)


def _outproj_loss(x, y_h, y_a, w_out_full, final_gain, target):
    s, d = x.shape
    e = y_h.shape[1]
    tm = _tile(s, 512)
    sub = _tile(tm, 256)

    def body(x_ref, yh_ref, ya_ref, w_ref, g_ref, t_ref, dx_ref, dxb_ref, dy_ref, loss_ref, dg_ref):
        @pl.when(pl.program_id(0) == 0)
        def _():
            loss_ref[...] = jnp.zeros_like(loss_ref)
            dg_ref[...] = jnp.zeros_like(dg_ref)

        w = w_ref[...]
        g = g_ref[...]
        parts = [slice(r0, r0 + sub) for r0 in range(0, tm, sub)]
        x2s = [x_ref[rows, :] + _dot(yh_ref[rows, :], w[0:e]) + _dot(ya_ref[rows, :], w[e:2 * e]) for rows in parts]
        for rows, x2 in zip(parts, x2s):
            r = lax.rsqrt(jnp.mean(x2 * x2, axis=-1, keepdims=True) + NORM_EPS)
            xn = x2 * r
            err = xn * g - t_ref[rows, :]
            loss_ref[...] += jnp.sum(err * err, axis=0, keepdims=True) * (0.5 / d)
            dyo = err * (1.0 / d)
            dg_ref[...] += jnp.sum(dyo * xn, axis=0, keepdims=True)
            u = dyo * g
            dx2 = r * (u - xn * jnp.mean(u * xn, axis=-1, keepdims=True))
            dx_ref[rows, :] = dx2
            dxb = dx2.astype(BF16)
            dxb_ref[rows, :] = dxb
            dy_ref[rows, :] = _dot_nt(dxb, w)

    row = pl.BlockSpec((tm, d), lambda i: (i, 0))
    half = pl.BlockSpec((tm, e), lambda i: (i, 0))
    vec = pl.BlockSpec((1, d), lambda i: (0, 0))
    whole = pl.BlockSpec((2 * e, d), lambda i: (0, 0), pipeline_mode=pl.Buffered(1))
    return pl.pallas_call(
        body, name="outproj_loss", grid=(s // tm,),
        out_shape=(jax.ShapeDtypeStruct((s, d), F32), jax.ShapeDtypeStruct((s, d), BF16),
                   jax.ShapeDtypeStruct((s, 2 * e), F32), jax.ShapeDtypeStruct((1, d), F32),
                   jax.ShapeDtypeStruct((1, d), F32)),
        in_specs=[row, half, half, whole, vec, row],
        out_specs=(row, row, pl.BlockSpec((tm, 2 * e), lambda i: (i, 0)), vec, vec),
        compiler_params=pltpu.CompilerParams(dimension_semantics=("arbitrary",), vmem_limit_bytes=OUTPROJ_VMEM_LIMIT),
    )(x, y_h, y_a, w_out_full, final_gain, target)


def _attn_bwd(z, dy, o, lse):
    s, e = o.shape
    npair = e // LANES
    T = ATTN_T
    assert s % T == 0
    nsb = s // T
    W = ATTN_BAND
    nt = T // W
    HD = ATTN_HEAD
    chunk = 256

    def body(k_ref, v_ref, qc_ref, qn_ref, dyc_ref, dyn_ref, gc_ref, gn_ref, oc_ref, on_ref, lc_ref, ln_ref,
             dz_ref, qa, doa, ka, va, dqacc, dkacc, dvacc, bias):
        sb = pl.program_id(1)
        def stage_queries(half, q_r, dy_r, g_r, o_r, l_r):
            def stage(i, carry):
                rows = pl.ds(pl.multiple_of(i * chunk, chunk), chunk)
                dst = pl.ds(pl.multiple_of(half * T + i * chunk, chunk), chunk)
                lane = lax.broadcasted_iota(jnp.int32, (chunk, LANES), 1)
                gp = g_r[rows, :]
                dov = dy_r[rows, :] * (gp * _sigmoid(gp))
                qv = q_r[rows, :] * SCALE
                same_head = (lax.broadcasted_iota(jnp.int32, (LANES, LANES), 0) // HD
                             == lax.broadcasted_iota(jnp.int32, (LANES, LANES), 1) // HD)
                ones = jnp.where(same_head, 1.0, 0.0).astype(BF16)
                hi, mid, lo = (p.astype(BF16) for p in _split3(dov * o_r[rows, :]))
                delta = _dot(hi, ones) + _dot(mid, ones) + _dot(lo, ones)
                swap = lambda a: pltpu.roll(a, HD, 1)
                lse_parts = [swap(p) for p in _split3(l_r[rows, :])]
                dl_parts = [swap(p) for p in _split3(delta)]
                for hh in range(2):
                    mine = _head_lanes(chunk, hh)
                    spare = (1 - hh) * HD
                    qh = jnp.where(mine, qv, 0.0)
                    dh = jnp.where(mine, dov, 0.0)
                    for j in range(3):
                        qh = jnp.where(lane == spare + j, lse_parts[j], qh)
                        dh = jnp.where(lane == spare + j, dl_parts[j], dh)
                    qa[hh, dst, :] = qh
                    doa[hh, dst, :] = dh
                return carry

            lax.fori_loop(0, T // chunk, stage, 0)

        @pl.when(sb == 0)
        def _():
            stage_queries(0, qc_ref, dyc_ref, gc_ref, oc_ref, lc_ref)

        stage_queries(1, qn_ref, dyn_ref, gn_ref, on_ref, ln_ref)

        def stage_keys(i, carry):
            rows = pl.ds(pl.multiple_of(i * chunk, chunk), chunk)
            lane = lax.broadcasted_iota(jnp.int32, (chunk, LANES), 1)
            for hh in range(2):
                spare = (1 - hh) * HD
                minus = (lane >= spare) & (lane < spare + 3)
                ka[hh, rows, :] = jnp.where(minus, -1.0, k_ref[rows, :])
                va[hh, rows, :] = jnp.where(minus, -1.0, v_ref[rows, :])
            gp = gc_ref[rows, :]
            dz_ref[3, rows, :] = (dyc_ref[rows, :] * oc_ref[rows, :] * _dsilu(gp, _sigmoid(gp))).astype(BF16)
            return carry

        lax.fori_loop(0, T // chunk, stage_keys, 0)

        @pl.when(sb == 0)
        def _():
            dqacc[0:T, :] = jnp.zeros((T, LANES), F32)

        dqacc[T:, :] = jnp.zeros((T, LANES), F32)
        dkacc[...] = jnp.zeros_like(dkacc)
        dvacc[...] = jnp.zeros_like(dvacc)
        qi = lax.broadcasted_iota(jnp.int32, (2 * W, W), 0)
        kj = lax.broadcasted_iota(jnp.int32, (2 * W, W), 1)
        _fill_bias(bias, 2 * npair, qi - kj, qi < W)

        def tile(tau, carry):
            def scores(step, pi):
                dil = DILATIONS[pi]
                r = step % dil
                ub = step // dil
                start = r + dil * W * ub
                krows = _rows(start, W, dil)
                qrows = _rows(start, 2 * W, dil)
                var = jnp.where((sb == nsb - 1) & (ub == nt // dil - 1), 1, 0)
                unit = dict(krows=krows, qrows=qrows, ops=[], sc=[], dpd=[])
                for hh in range(2):
                    kt = ka[hh, krows, :].astype(BF16)
                    vt = va[hh, krows, :].astype(BF16)
                    qt = qa[hh, qrows, :].astype(BF16)
                    dt = doa[hh, qrows, :].astype(BF16)
                    unit["ops"].append((kt, qt, dt))
                    unit["sc"].append(_dot_nt(qt, kt) + bias[(pi * 2 + hh) * 2 + var])
                    unit["dpd"].append(_dot_nt(dt, vt))
                return unit

            def elementwise(unit):
                ps = [jnp.exp(s_) for s_ in unit["sc"]]
                unit["ds"] = [(p * d).astype(BF16) for p, d in zip(ps, unit["dpd"])]
                unit["pb"] = [p.astype(BF16) for p in ps]

            def products(unit):
                dvs = [_dot_tn(pb, dt) for pb, (kt, qt, dt) in zip(unit["pb"], unit["ops"])]
                dks = [_dot_tn(ds, qt) for ds, (kt, qt, dt) in zip(unit["ds"], unit["ops"])]
                dqs = [_dot(ds, kt) for ds, (kt, qt, dt) in zip(unit["ds"], unit["ops"])]
                dkacc[unit["krows"], :] += jnp.where(_head_lanes(W, 0), dks[0], dks[1])
                dvacc[unit["krows"], :] += jnp.where(_head_lanes(W, 0), dvs[0], dvs[1])
                dqacc[unit["qrows"], :] += jnp.where(_head_lanes(2 * W, 0), dqs[0], dqs[1]) * SCALE

            order = [(2 * tau + half, pi) for half in range(2) for pi in range(len(DILATIONS))]
            units = [None] * len(order)
            for n in range(len(order) + 2):
                if n < len(order):
                    units[n] = scores(*order[n])
                if 1 <= n <= len(order):
                    elementwise(units[n - 1])
                if n >= 2:
                    products(units[n - 2])
            return carry

        lax.fori_loop(0, nt // 2, tile, 0)

        def flush(i, carry):
            rows = pl.ds(pl.multiple_of(i * chunk, chunk), chunk)
            nxt = pl.ds(pl.multiple_of(T + i * chunk, chunk), chunk)
            dz_ref[0, rows, :] = dqacc[rows, :].astype(BF16)
            dz_ref[1, rows, :] = dkacc[rows, :].astype(BF16)
            dz_ref[2, rows, :] = dvacc[rows, :].astype(BF16)
            dqacc[rows, :] = dqacc[nxt, :]
            for hh in range(2):
                qa[hh, rows, :] = qa[hh, nxt, :]
                doa[hh, rows, :] = doa[hh, nxt, :]
            return carry

        lax.fori_loop(0, T // chunk, flush, 0)

    zc = lambda split: (lambda hp, sb: (sb, split * npair + hp))
    zn = lambda split: (lambda hp, sb: (jnp.minimum(sb + 1, nsb - 1), split * npair + hp))
    ec = lambda off: (lambda hp, sb: (sb, off + hp))
    en = lambda off: (lambda hp, sb: (jnp.minimum(sb + 1, nsb - 1), off + hp))
    z0 = lambda split: (lambda hp, sb: (0, split * npair + hp))
    e0 = lambda off: (lambda hp, sb: (0, off + hp))
    blk = lambda index: pl.BlockSpec((T, LANES), index)
    buf = lambda rows: pltpu.VMEM((rows, LANES), F32)
    return pl.pallas_call(
        body, name="attn_bwd", grid=(npair, nsb), out_shape=jax.ShapeDtypeStruct((4, s, e), BF16),
        in_specs=[blk(zc(5)), blk(zc(6)), blk(z0(4)), blk(zn(4)), blk(ec(npair)), blk(en(npair)),
                  blk(zc(7)), blk(zn(7)), blk(ec(0)), blk(en(0)), blk(e0(0)), blk(en(0))],
        out_specs=pl.BlockSpec((4, T, LANES), lambda hp, sb: (0, sb, hp)),
        scratch_shapes=[pltpu.VMEM((2, 2 * T, LANES), F32), pltpu.VMEM((2, 2 * T, LANES), F32),
                        pltpu.VMEM((2, T, LANES), F32), pltpu.VMEM((2, T, LANES), F32),
                        buf(2 * T), buf(T), buf(T), pltpu.VMEM((12, 2 * W, W), F32)],
        compiler_params=_params("parallel", "arbitrary"))(z, z, z, z, dy, dy, z, z, o, o, lse, lse)


def _dz_specs(tm, e):
    def mk(lo, hi):
        return pl.BlockSpec((None, tm, e), lambda i, k: (jnp.clip(k - lo, 0, hi - lo - 1), i, 0))
    return [mk(0, 4), mk(4, 8)]


def _dz_pick(grp, dzh_ref, dza_ref, fn):
    @pl.when(grp < 4)
    def _():
        fn(dzh_ref[...])

    @pl.when(grp >= 4)
    def _():
        fn(dza_ref[...])


def _dh_dx(dzh, dza, w_full, x, gain, dx2):
    s, d = x.shape
    e = dzh.shape[2]
    tm = _tile(s, 1024)
    ni = s // tm
    chunk = _tile(tm, 256)
    fetch_at = 2

    def body(dzh_ref, dza_ref, w_ref, x_hbm, g_ref, dx2_hbm, gx_hbm, dg_ref, acc, xbuf, dbuf, sems):
        i, k = pl.program_id(0), pl.program_id(1)
        tile_rows = pl.ds(pl.multiple_of(i * tm, tm), tm)
        fetch_x = pltpu.make_async_copy(x_hbm.at[tile_rows, :], xbuf, sems.at[0])
        fetch_d = pltpu.make_async_copy(dx2_hbm.at[tile_rows, :], dbuf, sems.at[1])
        store = pltpu.make_async_copy(xbuf, gx_hbm.at[tile_rows, :], sems.at[2])

        @pl.when((i == 0) & (k == 0))
        def _():
            dg_ref[...] = jnp.zeros_like(dg_ref)

        @pl.when(k == 0)
        def _():
            acc[...] = jnp.zeros_like(acc)

        @pl.when((k == fetch_at) & (i > 0))
        def _():
            store.wait()

        @pl.when(k == fetch_at)
        def _():
            fetch_x.start()
            fetch_d.start()

        def add(dz):
            acc[...] += _dot_nt(dz, w_ref[...])

        _dz_pick(k, dzh_ref, dza_ref, add)

        @pl.when(k == N_SPLITS - 1)
        def _():
            fetch_x.wait()
            fetch_d.wait()
            gain_row = g_ref[...]

            def finish(c, dg):
                rows = pl.ds(pl.multiple_of(c * chunk, chunk), chunk)
                dh = acc[rows, :]
                xv = xbuf[rows, :]
                r = lax.rsqrt(jnp.mean(xv * xv, axis=-1, keepdims=True) + NORM_EPS)
                xn = xv * r
                u = dh * gain_row
                xbuf[rows, :] = dbuf[rows, :] + r * (u - xn * jnp.mean(u * xn, axis=-1, keepdims=True))
                return dg + jnp.sum(dh * xn, axis=0, keepdims=True)

            dg_ref[...] += lax.fori_loop(0, tm // chunk, finish, jnp.zeros((1, d), F32))
            store.start()

        @pl.when((k == N_SPLITS - 1) & (i == ni - 1))
        def _():
            store.wait()

    vec = pl.BlockSpec((1, d), lambda i, k: (0, 0))
    return pl.pallas_call(
        body, name="dh_dx", grid=(ni, N_SPLITS),
        out_shape=(jax.ShapeDtypeStruct((s, d), F32), jax.ShapeDtypeStruct((1, d), F32)),
        in_specs=_dz_specs(tm, e) + [pl.BlockSpec((None, d, e), lambda i, k: (k, 0, 0)), ANY, vec, ANY],
        out_specs=(ANY, vec),
        scratch_shapes=[pltpu.VMEM((tm, d), F32), pltpu.VMEM((tm, d), F32), pltpu.VMEM((tm, d), F32),
                        pltpu.SemaphoreType.DMA((3,))],
        compiler_params=_params("arbitrary", "arbitrary"))(dzh, dza, w_full, x, gain, dx2)


def _position():
    x, y, c = lax.axis_index("x"), lax.axis_index("y"), lax.axis_index("c")
    return x, y, c


def _xor_peer(x, y, c, mask):
    return (x ^ ((mask >> 2) & 1), y ^ ((mask >> 1) & 1), c ^ (mask & 1))


def _block_order(masks):
    me = 4 * lax.axis_index("x") + 2 * lax.axis_index("y") + lax.axis_index("c")
    return jnp.stack([me ^ m for m in masks]).astype(jnp.int32)


GATHER_MASKS = (0, 1, 4, 5, 2, 3, 6, 7)


def _inproj_gather(h, w_loc, wo_loc):
    s, d = h.shape
    e = w_loc.shape[1]
    tm = _tile(s, 1024)
    ni = s // tm
    pre = max(ni - 2, 0)

    def body(order_ref, h_ref, w_ref, wo_ref, z_ref, wf_ref, wof_ref, wbuf, send_sems, recv_sems, osend, orecv,
             local_sems, wsems):
        j, i = pl.program_id(0), pl.program_id(1)
        x, y, c = _position()
        me, sibling = (x, y, c), (x, y, 1 - c)
        chips = [(1 - x, y), (x, 1 - y), (1 - x, 1 - y)]
        blk = lambda p: 4 * p[0] + 2 * p[1] + p[2]

        def copy(k, block, to, src=None):
            dst = wf_ref.at[blk(block)]
            return pltpu.make_async_remote_copy(
                src_ref=dst if src is None else src, dst_ref=dst, send_sem=send_sems.at[k], recv_sem=recv_sems.at[k],
                device_id=to, device_id_type=MESH)

        first = [copy(0, me, sibling, src=w_ref)] + [copy(1 + q, me, (*chip, c), src=w_ref) for q, chip in enumerate(chips)]
        passed = [copy(4 + q, (*chip, c), sibling) for q, chip in enumerate(chips)]
        mine = pltpu.make_async_copy(w_ref, wf_ref.at[blk(me)], local_sems.at[0])
        ocopies = [pltpu.make_async_remote_copy(
            src_ref=wo_ref, dst_ref=wof_ref.at[blk(me)], send_sem=osend.at[k], recv_sem=orecv.at[k],
            device_id=_xor_peer(x, y, c, k + 1), device_id_type=MESH) for k in range(N_DEV - 1)]
        omine = pltpu.make_async_copy(wo_ref, wof_ref.at[blk(me)], local_sems.at[1])
        blocks = [me, sibling] + [(*chip, c) for chip in chips] + [(*chip, 1 - c) for chip in chips]
        arrive = [None, copy(0, sibling, me)] + [copy(1 + q, (*chip, c), me) for q, chip in enumerate(chips)] \
            + [copy(4 + q, (*chip, 1 - c), me) for q, chip in enumerate(chips)]
        forward = [None, None] + passed + [None, None, None]
        use_order = (0, 1, 2, 5, 3, 6, 4, 7)
        blocks, arrive, forward = ([lst[n] for n in use_order] for lst in (blocks, arrive, forward))

        def load(slot, src):
            return pltpu.make_async_copy(src, wbuf.at[slot], wsems.at[slot])

        @pl.when((j == 0) & (i == 0))
        def _():
            for cp in [mine, omine] + first + ocopies:
                cp.start()
            load(0, w_ref).start()

        for jj in range(N_DEV):
            @pl.when((j == jj) & (i == 0))
            def _():
                load(jj % 2, w_ref).wait()

            if jj + 1 < N_DEV:
                @pl.when((j == jj) & (i == pre))
                def _():
                    arrive[jj + 1].wait_recv()
                    if forward[jj + 1] is not None:
                        forward[jj + 1].start()
                    load((jj + 1) % 2, wf_ref.at[blk(blocks[jj + 1])]).start()

        z_ref[...] = _dot(h_ref[...], wbuf[j % 2])

        @pl.when((j == N_DEV - 1) & (i == ni - 1))
        def _():
            for cp in first + passed:
                cp.wait_send()
            for cp in ocopies:
                cp.wait_send()
                cp.wait_recv()
            mine.wait()
            omine.wait()

    grid_spec = pltpu.PrefetchScalarGridSpec(
        num_scalar_prefetch=1, grid=(N_DEV, ni),
        in_specs=[pl.BlockSpec((tm, d), lambda j, i, o: (i, 0)), ANY, ANY],
        out_specs=(pl.BlockSpec((tm, e), lambda j, i, o: (i, o[j])), ANY, ANY),
        scratch_shapes=[pltpu.VMEM((2, d, e), BF16), pltpu.SemaphoreType.DMA((7,)), pltpu.SemaphoreType.DMA((7,)),
                        pltpu.SemaphoreType.DMA((7,)), pltpu.SemaphoreType.DMA((7,)), pltpu.SemaphoreType.DMA((2,)),
                        pltpu.SemaphoreType.DMA((2,))])
    return pl.pallas_call(
        body, name="inproj_gather", grid_spec=grid_spec,
        out_shape=(jax.ShapeDtypeStruct((s, N_SPLITS * e), F32), jax.ShapeDtypeStruct((N_DEV, d, e), BF16),
                   jax.ShapeDtypeStruct((N_DEV,) + wo_loc.shape, BF16)),
        compiler_params=_params("arbitrary", "arbitrary"))(_block_order(GATHER_MASKS), h, w_loc, wo_loc)


SCATTER_MASKS = (7, 6, 5, 4, 3, 2, 1, 0)
N_CHIPS = 4


def _scatter_block(k, acc, stage, tmp, own_ref, ra_ref, rb_ref, sa_send, sa_recv, sb_send, sb_recv, loc_sem, step, ns):
    x, y, c = _position()
    chip_of = lambda t: _xor_peer(x, y, c, SCATTER_MASKS[2 * t + 1])
    last = step == ns - 1
    fetch_at = min(1, ns - 1)

    def ship(t):
        return pltpu.make_async_remote_copy(
            src_ref=stage.at[0], dst_ref=ra_ref.at[t], send_sem=sa_send.at[t], recv_sem=sa_recv.at[t],
            device_id=(x, y, 1 - c), device_id_type=MESH)

    def send(t):
        return pltpu.make_async_remote_copy(
            src_ref=stage.at[1], dst_ref=rb_ref.at[t], send_sem=sb_send.at[t], recv_sem=sb_recv.at[t],
            device_id=chip_of(t), device_id_type=MESH)

    for kk in range(N_DEV):
        t = kk // 2
        fetch = pltpu.make_async_copy(ra_ref.at[t], tmp, loc_sem)

        if kk % 2 == 1:
            @pl.when((step == fetch_at) & (k == kk))
            def _():
                ship(t).wait_recv()
                fetch.start()

        @pl.when(last & (k == kk))
        def _():
            if kk % 2 == 0:
                if t >= 1:
                    ship(t - 1).wait_send()
                stage[0] = acc[...].astype(BF16)
                ship(t).start()
            else:
                fetch.wait()
                acc[...] += tmp[...].astype(F32)
                if t < N_CHIPS - 1:
                    if t >= 1:
                        send(t - 1).wait_send()
                    stage[1] = acc[...].astype(BF16)
                    send(t).start()
                else:
                    keep = pltpu.make_async_copy(acc, own_ref, loc_sem)
                    keep.start()
                    keep.wait()
                    ship(t).wait_send()
                    send(t - 1).wait_send()
                    for q in range(N_CHIPS - 1):
                        send(q).wait_recv()


def _scatter_scratch(rows, cols):
    return [pltpu.VMEM((rows, cols), F32), pltpu.VMEM((2, rows, cols), BF16), pltpu.VMEM((rows, cols), BF16),
            pltpu.SemaphoreType.DMA((N_CHIPS,)), pltpu.SemaphoreType.DMA((N_CHIPS,)),
            pltpu.SemaphoreType.DMA((N_CHIPS - 1,)), pltpu.SemaphoreType.DMA((N_CHIPS - 1,)), pltpu.SemaphoreType.DMA(())]


def _scatter_out(rows, cols):
    return (jax.ShapeDtypeStruct((rows, cols), F32), jax.ShapeDtypeStruct((N_CHIPS, rows, cols), BF16),
            jax.ShapeDtypeStruct((N_CHIPS - 1, rows, cols), BF16))


def _dwin_scatter(h, dzh, dza):
    s, d = h.shape
    e = dzh.shape[2]
    ts = _tile(s, 1024)
    ns = s // ts

    def body(order_ref, dzh_ref, dza_ref, h_ref, own_ref, ra_ref, rb_ref, acc, stage, tmp, *sems):
        k, step = pl.program_id(0), pl.program_id(1)

        @pl.when(step == 0)
        def _():
            acc[...] = jnp.zeros_like(acc)

        def add(dz):
            acc[...] += _dot_tn(h_ref[...], dz)

        _dz_pick(order_ref[k], dzh_ref, dza_ref, add)
        _scatter_block(k, acc, stage, tmp, own_ref, ra_ref, rb_ref, *sems, step, ns)

    def dz_spec(lo):
        return pl.BlockSpec((None, ts, e), lambda k, st, o: (jnp.clip(o[k] - lo, 0, 3), st, 0))

    grid_spec = pltpu.PrefetchScalarGridSpec(
        num_scalar_prefetch=1, grid=(N_DEV, ns),
        in_specs=[dz_spec(0), dz_spec(4), pl.BlockSpec((ts, d), lambda k, st, o: (st, 0))],
        out_specs=(ANY, ANY, ANY), scratch_shapes=_scatter_scratch(d, e))
    own, _, rb = pl.pallas_call(
        body, name="dwin_scatter", grid_spec=grid_spec, out_shape=_scatter_out(d, e),
        compiler_params=_params("arbitrary", "arbitrary"))(_block_order(SCATTER_MASKS), dzh, dza, h)
    return own, rb


def _dwout_scatter(y_h, y_a, dxb):
    s, e = y_h.shape
    d = dxb.shape[1]
    r = 2 * e // N_DEV
    pairs = e // (2 * r)
    ts = _tile(s, 1024)
    ns = s // ts
    chip_masks = SCATTER_MASKS[1::2]
    passes = ((0, 1), (2,), (3,))
    slots = max(len(chips) for chips in passes)
    slot_chip = [chips[min(u, len(chips) - 1)] for chips in passes for u in range(slots)]

    def body(pair_ref, yh0_ref, ya0_ref, yh1_ref, ya1_ref, dx_ref, own_ref, ra_ref, rb_ref, acc, keep_buf, ship_buf,
             send_buf, tmp, sa_send, sa_recv, sb_send, sb_recv, loc_sem):
        p, step = pl.program_id(0), pl.program_id(1)
        x, y, c = _position()

        @pl.when(step == 0)
        def _():
            acc[...] = jnp.zeros_like(acc)

        for u, (yh_ref, ya_ref) in enumerate(((yh0_ref, ya0_ref), (yh1_ref, ya1_ref))):
            rows = slice(u * 2 * r, (u + 1) * 2 * r)
            used = functools.reduce(jnp.logical_or, [p == pp for pp, chips in enumerate(passes) if u < len(chips)])

            @pl.when(used & (pair_ref[slots * p + u] < pairs))
            def _():
                acc[rows, :] += _dot_tn(yh_ref[...], dx_ref[...])

            @pl.when(used & (pair_ref[slots * p + u] >= pairs))
            def _():
                acc[rows, :] += _dot_tn(ya_ref[...], dx_ref[...])

        def block_rows(u, core):
            return pl.ds(pl.multiple_of(u * 2 * r + core * r, r), r)

        slot_of = {q: u for chips in passes for u, q in enumerate(chips)}

        def ship(q):
            return pltpu.make_async_remote_copy(
                src_ref=ship_buf.at[slot_of[q]], dst_ref=ra_ref.at[q], send_sem=sa_send.at[q], recv_sem=sa_recv.at[q],
                device_id=(x, y, 1 - c), device_id_type=MESH)

        def send(q):
            return pltpu.make_async_remote_copy(
                src_ref=send_buf.at[slot_of[q]], dst_ref=rb_ref.at[q], send_sem=sb_send.at[q], recv_sem=sb_recv.at[q],
                device_id=_xor_peer(x, y, c, chip_masks[q]), device_id_type=MESH)

        def sibling_share(q):
            ship(q).wait_recv()
            fetch = pltpu.make_async_copy(ra_ref.at[q], tmp, loc_sem)
            fetch.start()
            fetch.wait()
            return tmp[...].astype(F32)

        shipped, sent = {}, {}
        for pp, chips in enumerate(passes):
            @pl.when((step == ns - 1) & (p == pp))
            def _():
                for u, q in enumerate(chips):
                    if u in shipped:
                        ship(shipped.pop(u)).wait_send()
                    ship_buf[u] = acc[block_rows(u, 1 - c), :].astype(BF16)
                    ship(q).start()
                    shipped[u] = q
                for u, q in enumerate(chips):
                    total = acc[block_rows(u, c), :] + sibling_share(q)
                    if q < N_CHIPS - 1:
                        if u in sent:
                            send(sent.pop(u)).wait_send()
                        send_buf[u] = total.astype(BF16)
                        send(q).start()
                        sent[u] = q
                    else:
                        keep_buf[...] = total
                        keep = pltpu.make_async_copy(keep_buf, own_ref, loc_sem)
                        keep.start()
                        keep.wait()
                if pp == len(passes) - 1:
                    for q in shipped.values():
                        ship(q).wait_send()
                    for q in sent.values():
                        send(q).wait_send()
                    for q in range(N_CHIPS - 1):
                        send(q).wait_recv()

    def y_spec(u, lo):
        return pl.BlockSpec((ts, 2 * r), lambda p, st, o: (st, jnp.clip(o[slots * p + u] - lo, 0, pairs - 1)))

    pair_of_chip = _block_order(chip_masks) // 2
    grid_spec = pltpu.PrefetchScalarGridSpec(
        num_scalar_prefetch=1, grid=(len(passes), ns),
        in_specs=[y_spec(0, 0), y_spec(0, pairs), y_spec(1, 0), y_spec(1, pairs),
                  pl.BlockSpec((ts, d), lambda p, st, o: (st, 0))],
        out_specs=(ANY, ANY, ANY),
        scratch_shapes=[pltpu.VMEM((slots * 2 * r, d), F32), pltpu.VMEM((r, d), F32),
                        pltpu.VMEM((slots, r, d), BF16)] + _scatter_scratch(r, d)[1:])
    own, _, rb = pl.pallas_call(
        body, name="dwout_scatter", grid_spec=grid_spec, out_shape=_scatter_out(r, d),
        compiler_params=_params("arbitrary", "arbitrary"))(
            jnp.stack([pair_of_chip[q] for q in slot_chip]), y_h, y_a, y_h, y_a, dxb)
    return own, rb


def _sum_chips_adamw(own, recv, w, m, v):
    r, c = w.shape
    tr = _tile(r, 128)

    def body(own_ref, rc_ref, w_ref, m_ref, v_ref, g_ref, d_ref, mo_ref, vo_ref):
        g = own_ref[...]
        for q in range(N_CHIPS - 1):
            g = g + rc_ref[q].astype(F32)
        g_ref[...] = g
        d_ref[...], mo_ref[...], vo_ref[...] = _adamw(w_ref[...], g, m_ref[...], v_ref[...])

    blk = pl.BlockSpec((tr, c), lambda i: (i, 0))
    shp = jax.ShapeDtypeStruct((r, c), F32)
    return pl.pallas_call(
        body, name="sum_chips_adamw", grid=(r // tr,), out_shape=(shp, shp, shp, shp),
        in_specs=[blk, pl.BlockSpec((N_CHIPS - 1, tr, c), lambda i: (0, i, 0)), blk, blk, blk],
        out_specs=(blk, blk, blk, blk), compiler_params=_params("parallel"))(own, recv, w, m, v)


SMALL_ROWS = 8
ROW_LB = 4
ROW_GN = 6
ROW_LOSS = 7


def _small_allreduce_adamw(part, w, m, v, lb_logits):
    width = part.shape[1]

    def body(p_ref, w_ref, m_ref, v_ref, lb_ref, g_ref, d_ref, mo_ref, vo_ref, buf, send_sems, recv_sems):
        x, y, c = _position()
        me = 4 * x + 2 * y + c
        buf[me] = p_ref[...]
        copies = []
        for k in range(N_DEV - 1):
            bx, by, bc = ((k + 1) >> 2) & 1, ((k + 1) >> 1) & 1, (k + 1) & 1
            peer = (x ^ bx, y ^ by, c ^ bc)
            copies.append(pltpu.make_async_remote_copy(
                src_ref=p_ref, dst_ref=buf.at[me], send_sem=send_sems.at[k], recv_sem=recv_sems.at[k],
                device_id=peer, device_id_type=MESH))
        for cp in copies:
            cp.start()
        for cp in copies:
            cp.wait_recv()
        for cp in copies:
            cp.wait_send()
        tot = buf[0]
        for dev in range(1, N_DEV):
            tot = tot + buf[dev]
        lbv = lb_ref[...]
        lb = _sigmoid(lbv[0:1] - lbv[1:2])
        glb = tot[ROW_LB:ROW_LB + 1] * lb * (1.0 - lb)
        loss = jnp.sum(tot[ROW_LOSS:ROW_LOSS + 1], axis=-1, keepdims=True)
        row = lax.broadcasted_iota(jnp.int32, (SMALL_ROWS, width), 0)
        g = jnp.where(row == ROW_LB, glb, jnp.where(row == ROW_LB + 1, -glb, tot))
        g = jnp.where(row == ROW_LOSS, loss, g)
        g_ref[...] = g
        d_ref[...], mo_ref[...], vo_ref[...] = _adamw(w_ref[...], g, m_ref[...], v_ref[...])

    vm = pl.BlockSpec(memory_space=pltpu.VMEM)
    shp = jax.ShapeDtypeStruct((SMALL_ROWS, width), F32)
    return pl.pallas_call(
        body, name="small_allreduce_adamw", out_shape=(shp, shp, shp, shp),
        in_specs=[vm] * 5, out_specs=(vm, vm, vm, vm),
        scratch_shapes=[pltpu.VMEM((N_DEV, SMALL_ROWS, width), F32), pltpu.SemaphoreType.DMA((N_DEV - 1,)),
                        pltpu.SemaphoreType.DMA((N_DEV - 1,))],
    )(part, w, m, v, lb_logits)


def _pack_small(norm_gain, final_gain, lb2, gnorm, last_row, width):
    pad = lambda a: jnp.pad(a.reshape(1, -1), ((0, 0), (0, width - a.size)))
    return jnp.concatenate([norm_gain.reshape(2, width), final_gain.reshape(2, width), lb2.reshape(2, width),
                            pad(gnorm), last_row.reshape(1, width)], axis=0)


def _unpack_small(p, d, e, hd):
    return (p[0:2].reshape(1, d), p[2:4].reshape(d), p[4:6].reshape(2, e), p[6:7, :hd].reshape(1, hd))


def kernel(x, norm_gain, w_in, lb_logits, hgrn_gnorm, w_out, final_gain, loss_target, m_norm_gain, m_w_in, m_lb_logits, m_hgrn_gnorm, m_w_out, m_final_gain, v_norm_gain, v_w_in, v_lb_logits, v_hgrn_gnorm, v_w_out, v_final_gain):
    s, d = x.shape[1], x.shape[2]
    e = w_in.shape[2]
    assert d == 2 * e and lb_logits.shape == (2, e) and w_out.shape[1] * N_DEV == 2 * e
    x2d = x.reshape(s, d)
    tgt = loss_target.reshape(s, d)

    h = _rmsnorm_fwd(x2d, norm_gain)
    z, w_in_full, w_out_full = _inproj_gather(h, _cast_bf16(w_in[0]), _cast_bf16(w_out[0]))
    w_out_full = w_out_full.reshape(2 * e, d)
    y_h, states = _hgrn_fwd(z, lb_logits, hgrn_gnorm)
    o_attn, lse, y_a = _attn_fwd(z)
    dx2, dx2b, dy, loss_vec, dfg = _outproj_loss(x2d, y_h, y_a, w_out_full, final_gain.reshape(1, d), tgt)

    own_o, recv_o = _dwout_scatter(y_h, y_a, dx2b)
    dza = _attn_bwd(z, dy, o_attn, lse)
    dzh, dlb, dgn = _hgrn_bwd(z, dy, states, lb_logits, hgrn_gnorm)
    grad_x, dng = _dh_dx(dzh, dza, w_in_full, x2d, norm_gain, dx2)
    g_wo, d_wo, nm_wo, nv_wo = _sum_chips_adamw(own_o, recv_o, w_out[0], m_w_out[0], v_w_out[0])

    width = d // 2
    zero_row = jnp.zeros((1, width), F32)
    loss_row = loss_vec[:, :width] + loss_vec[:, width:]
    part = _pack_small(dng, dfg, jnp.concatenate([dlb, zero_row], axis=0), dgn, loss_row, width)
    pw = _pack_small(norm_gain, final_gain, lb_logits, hgrn_gnorm, zero_row, width)
    pm = _pack_small(m_norm_gain, m_final_gain, m_lb_logits, m_hgrn_gnorm, zero_row, width)
    pv = _pack_small(v_norm_gain, v_final_gain, v_lb_logits, v_hgrn_gnorm, zero_row, width)
    sg, sd, sm, sv = _small_allreduce_adamw(part, pw, pm, pv, lb_logits)
    own_i, recv_i = _dwin_scatter(h, dzh, dza)
    g_wi, d_wi, nm_wi, nv_wi = _sum_chips_adamw(own_i, recv_i, w_in[0], m_w_in[0], v_w_in[0])
    hd = hgrn_gnorm.shape[1]
    g_ng, g_fg, g_lb, g_gn = _unpack_small(sg, d, e, hd)
    d_ng, d_fg, d_lb, d_gn = _unpack_small(sd, d, e, hd)
    m_ng, m_fg, m_lb, m_gn = _unpack_small(sm, d, e, hd)
    v_ng, v_fg, v_lb, v_gn = _unpack_small(sv, d, e, hd)
    loss = sg[ROW_LOSS, 0]

    one = lambda a: a[None]
    return (loss, grad_x.reshape(1, s, d), g_ng, one(g_wi), g_lb, g_gn, one(g_wo), g_fg,
            d_ng, one(d_wi), d_lb, d_gn, one(d_wo), d_fg,
            m_ng, one(nm_wi), m_lb, m_gn, one(nm_wo), m_fg,
            v_ng, one(nv_wi), v_lb, v_gn, one(nv_wo), v_fg)
```

```python
import functools
import math

import jax
import jax.numpy as jnp
from jax import lax
from jax.experimental import pallas as pl
from jax.experimental.pallas import tpu as pltpu

NORM_EPS = 1e-6
HGRN_HEAD = 128
HGRN_CHUNK = 64
ATTN_HEAD = 64
ATTN_BAND = 128
DILATIONS = (1, 4, 16)
N_SPLITS = 8
N_DEV = 8
ADAM_LR = 0.001
ADAM_B1 = 0.9
ADAM_B2 = 0.999
ADAM_EPS = 1e-08
ADAM_WD = 0.01
ADAM_STEP = 10
LANES = 128
MESH = pl.DeviceIdType.MESH
F32 = jnp.float32
BF16 = jnp.bfloat16
NEG_BIG = -1e30
VMEM_LIMIT = 56 * 1024 * 1024
OUTPROJ_VMEM_LIMIT = 63 * 1024 * 1024
DHDX_VMEM_LIMIT = 60 * 1024 * 1024

ANY = pl.BlockSpec(memory_space=pl.ANY)


def _params(*sem):
    return pltpu.CompilerParams(dimension_semantics=sem, vmem_limit_bytes=VMEM_LIMIT)


def _tile(n, pref):
    t = min(n, pref)
    assert n % t == 0, (n, pref)
    return t


def _dot(a, b, precision=None):
    return jnp.dot(a, b, preferred_element_type=F32, precision=precision)


def _dot_nt(a, b):
    return lax.dot_general(a, b, (((1,), (1,)), ((), ())), preferred_element_type=F32)


def _dot_tn(a, b):
    return lax.dot_general(a, b, (((0,), (0,)), ((), ())), preferred_element_type=F32)


def _sigmoid(x):
    return 0.5 * jnp.tanh(0.5 * x) + 0.5


def _dsilu(x, s):
    return s * (1.0 + x * (1.0 - s))


def _adamw(w, g, m, v):
    m = ADAM_B1 * m + (1.0 - ADAM_B1) * g
    v = ADAM_B2 * v + (1.0 - ADAM_B2) * (g * g)
    m_hat = m / (1.0 - ADAM_B1 ** ADAM_STEP)
    v_hat = v / (1.0 - ADAM_B2 ** ADAM_STEP)
    delta = -ADAM_LR * (m_hat / (jnp.sqrt(v_hat) + ADAM_EPS) + ADAM_WD * w)
    return delta, m, v


def _cast_bf16(a):
    r, c = a.shape
    tr = _tile(r, 256)

    def body(a_ref, o_ref):
        o_ref[...] = a_ref[...].astype(BF16)

    return pl.pallas_call(
        body, name="cast_bf16", grid=(r // tr,), out_shape=jax.ShapeDtypeStruct((r, c), BF16),
        in_specs=[pl.BlockSpec((tr, c), lambda i: (i, 0))], out_specs=pl.BlockSpec((tr, c), lambda i: (i, 0)),
        compiler_params=_params("parallel"))(a)


def _rmsnorm_fwd(x, gain):
    s, d = x.shape
    tm = _tile(s, 512)

    def body(x_ref, g_ref, h_ref):
        xv = x_ref[...]
        r = lax.rsqrt(jnp.mean(xv * xv, axis=-1, keepdims=True) + NORM_EPS)
        h_ref[...] = (xv * r * g_ref[...]).astype(BF16)

    return pl.pallas_call(
        body, name="rmsnorm_fwd", grid=(s // tm,), out_shape=jax.ShapeDtypeStruct((s, d), BF16),
        in_specs=[pl.BlockSpec((tm, d), lambda i: (i, 0)), pl.BlockSpec((1, d), lambda i: (0, 0))],
        out_specs=pl.BlockSpec((tm, d), lambda i: (i, 0)), compiler_params=_params("parallel"))(x, gain)


HGRN_BLOCK = 2048
TRI_ROWS = 256


def _chunk_masks():
    tb = TRI_ROWS
    row = lax.broadcasted_iota(jnp.int32, (tb, tb), 0)
    col = lax.broadcasted_iota(jnp.int32, (tb, tb), 1)
    same = (row // HGRN_CHUNK) == (col // HGRN_CHUNK)
    lower = jnp.where(same & (col <= row), 1.0, 0.0).astype(BF16)
    upper = jnp.where(same & (col >= row), 1.0, 0.0).astype(BF16)
    return lower, upper


def _split3(a):
    hi = a.astype(BF16).astype(F32)
    mid = (a - hi).astype(BF16).astype(F32)
    lo = (a - hi - mid).astype(BF16).astype(F32)
    return hi, mid, lo


def _tri_dot(tri, x):
    hi, mid, lo = (p.astype(BF16) for p in _split3(x))
    outs = []
    for r in range(0, x.shape[0], TRI_ROWS):
        sl = slice(r, r + TRI_ROWS)
        outs.append(_dot(tri, hi[sl]) + _dot(tri, mid[sl]) + _dot(tri, lo[sl]))
    return outs[0] if len(outs) == 1 else jnp.concatenate(outs, axis=0)


def _hgrn_gates(qp, fp, lbv):
    lb = _sigmoid(lbv[0:1] - lbv[1:2])
    sq = _sigmoid(qp)
    q = qp * sq
    sg = _sigmoid(fp)
    f = lb + (1.0 - lb) * sg
    k = 1.0 - f
    return lb, sq, q, sg, f, k


def _hgrn_fwd(z, lb_logits, gnorm):
    s = z.shape[0]
    e = z.shape[1] // N_SPLITS
    nh = e // HGRN_HEAD
    tb = _tile(s, HGRN_BLOCK)
    nc = tb // HGRN_CHUNK
    nb = s // tb
    C = HGRN_CHUNK

    def body(q_ref, f_ref, i_ref, g_ref, lb_ref, gn_ref, y_ref, st_ref, state, o_scr):
        @pl.when(pl.program_id(1) == 0)
        def _():
            state[...] = jnp.zeros_like(state)

        lb, sq, q, sg, f, k = _hgrn_gates(q_ref[...], f_ref[...], lb_ref[...])
        lower, _ = _chunk_masks()
        b = _tri_dot(lower, jnp.log(f))
        b3 = b.reshape(nc, C, HGRN_HEAD)
        bc = b3[:, C - 1:C, :]
        qt = (q * jnp.exp(b)).astype(BF16)
        kt = (k * jnp.exp(-b)).astype(BF16)
        ke = (k.reshape(nc, C, HGRN_HEAD) * jnp.exp(bc - b3)).reshape(tb, HGRN_HEAD).astype(BF16)
        v = i_ref[...].astype(BF16)
        tri = lax.broadcasted_iota(jnp.int32, (C, C), 1) <= lax.broadcasted_iota(jnp.int32, (C, C), 0)
        sls = [slice(c * C, (c + 1) * C) for c in range(nc)]
        kv = [_dot_tn(v[sl], ke[sl]) for sl in sls]
        a = [jnp.where(tri, _dot_nt(qt[sl], kt[sl]), 0.0).astype(BF16) for sl in sls]
        st = state[...]
        sts = []
        for c in range(nc):
            sts.append(st)
            st_ref[c] = st
            st = st * jnp.exp(bc[c]) + kv[c]
        state[...] = st
        for c, sl in enumerate(sls):
            o_scr[sl, :] = _dot(a[c], v[sl]) + _dot_nt(qt[sl], sts[c].astype(BF16))
        o = o_scr[...]
        rms = lax.rsqrt(jnp.mean(o * o, axis=-1, keepdims=True) + NORM_EPS)
        gp = g_ref[...]
        y_ref[...] = (o * rms * gn_ref[...] * (gp * _sigmoid(gp))).astype(BF16)

    col = lambda kk: (lambda h, n: (n, kk * nh + h))
    return pl.pallas_call(
        body, name="hgrn_fwd", grid=(nh, nb),
        out_shape=(jax.ShapeDtypeStruct((s, e), BF16),
                   jax.ShapeDtypeStruct((nh, s // C, HGRN_HEAD, HGRN_HEAD), F32)),
        in_specs=[pl.BlockSpec((tb, HGRN_HEAD), col(0)), pl.BlockSpec((tb, HGRN_HEAD), col(1)),
                  pl.BlockSpec((tb, HGRN_HEAD), col(2)), pl.BlockSpec((tb, HGRN_HEAD), col(3)),
                  pl.BlockSpec((2, HGRN_HEAD), lambda h, n: (0, h)), pl.BlockSpec((1, HGRN_HEAD), lambda h, n: (0, 0))],
        out_specs=(pl.BlockSpec((tb, HGRN_HEAD), lambda h, n: (n, h)),
                   pl.BlockSpec((None, nc, HGRN_HEAD, HGRN_HEAD), lambda h, n: (h, n, 0, 0))),
        scratch_shapes=[pltpu.VMEM((HGRN_HEAD, HGRN_HEAD), F32), pltpu.VMEM((tb, HGRN_HEAD), F32)],
        compiler_params=_params("parallel", "arbitrary"))(z, z, z, z, lb_logits, gnorm)


def _hgrn_bwd(z, dy, states, lb_logits, gnorm):
    s = z.shape[0]
    e = z.shape[1] // N_SPLITS
    nh = e // HGRN_HEAD
    tb = _tile(s, HGRN_BLOCK)
    nc = tb // HGRN_CHUNK
    nb = s // tb
    C = HGRN_CHUNK
    H = HGRN_HEAD

    def body(q_ref, f_ref, i_ref, g_ref, dy_ref, st_ref, lb_ref, gn_ref, dz_ref, dlb_ref, dgn_ref,
             gstate, o_scr, dq_scr, dk_scr, dv_scr, e_scr):
        first = (pl.program_id(0) == 0) & (pl.program_id(1) == 0)

        @pl.when(first)
        def _():
            dgn_ref[...] = jnp.zeros_like(dgn_ref)

        @pl.when(pl.program_id(1) == 0)
        def _():
            gstate[...] = jnp.zeros_like(gstate)
            dlb_ref[...] = jnp.zeros_like(dlb_ref)

        qp = q_ref[...]
        lb, sq, q, sg, f, k = _hgrn_gates(qp, f_ref[...], lb_ref[...])
        lower, upper = _chunk_masks()
        b = _tri_dot(lower, jnp.log(f))
        b3 = b.reshape(nc, C, H)
        bc = b3[:, C - 1:C, :]
        eb = jnp.exp(b)
        enb = jnp.exp(-b)
        eend = jnp.exp(bc - b3).reshape(tb, H)
        qt = (q * eb).astype(BF16)
        kt = (k * enb).astype(BF16)
        ke = (k * eend).astype(BF16)
        v = i_ref[...].astype(BF16)
        tri = lax.broadcasted_iota(jnp.int32, (C, C), 1) <= lax.broadcasted_iota(jnp.int32, (C, C), 0)
        sls = [slice(c * C, (c + 1) * C) for c in range(nc)]
        a = [jnp.where(tri, _dot_nt(qt[sl], kt[sl]), 0.0).astype(BF16) for sl in sls]
        for c, sl in enumerate(sls):
            o_scr[sl, :] = _dot(a[c], v[sl]) + _dot_nt(qt[sl], st_ref[c].astype(BF16))
        o = o_scr[...]
        rms = lax.rsqrt(jnp.mean(o * o, axis=-1, keepdims=True) + NORM_EPS)
        on = o * rms
        gn = gn_ref[...]
        gp = g_ref[...]
        sgg = _sigmoid(gp)
        dyv = dy_ref[...]
        d_on = dyv * (gp * sgg)
        dz_ref[3] = (dyv * on * gn * _dsilu(gp, sgg)).astype(BF16)
        dgn_ref[...] += jnp.sum(d_on * on, axis=0, keepdims=True)
        u = d_on * gn
        do = (rms * (u - on * jnp.mean(u * on, axis=-1, keepdims=True))).astype(BF16)
        gup = [_dot_tn(do[sl], qt[sl]) for sl in sls]
        da = [jnp.where(tri, _dot_nt(do[sl], v[sl]), 0.0).astype(BF16) for sl in sls]
        gt = gstate[...]
        gts = [None] * nc
        for c in reversed(range(nc)):
            gts[c] = gt
            gt = gt * jnp.exp(bc[c]) + gup[c]
        gstate[...] = gt
        for c, sl in enumerate(sls):
            stp = st_ref[c]
            gtb = gts[c].astype(BF16)
            dqt = _dot(da[c], kt[sl]) + _dot(do[sl], stp.astype(BF16))
            dkt = _dot_tn(da[c], qt[sl])
            dks = _dot(v[sl], gtb) * eend[sl]
            dv_scr[sl, :] = _dot_tn(a[c], do[sl]) + _dot_nt(ke[sl], gtb)
            dq_scr[sl, :] = dqt * eb[sl]
            dk_scr[sl, :] = dkt * enb[sl] + dks
            ech = (jnp.sum(k[sl] * dks, axis=0, keepdims=True)
                   + jnp.sum(gts[c] * jnp.exp(bc[c]) * stp, axis=0, keepdims=True))
            e_scr[sl, :] = jnp.broadcast_to(ech, (C, H))
        dq = dq_scr[...]
        dk = dk_scr[...]
        dlf = _tri_dot(upper, q * dq - k * dk) + e_scr[...]
        dft = dlf / f - dk
        dz_ref[0] = (dq * _dsilu(qp, sq)).astype(BF16)
        dz_ref[1] = (dft * (1.0 - lb) * sg * (1.0 - sg)).astype(BF16)
        dz_ref[2] = dv_scr[...].astype(BF16)
        dlb_ref[...] += jnp.sum(dft * (1.0 - sg), axis=0, keepdims=True)

    col = lambda kk: (lambda h, n: (nb - 1 - n, kk * nh + h))
    return pl.pallas_call(
        body, name="hgrn_bwd", grid=(nh, nb),
        out_shape=(jax.ShapeDtypeStruct((4, s, e), BF16), jax.ShapeDtypeStruct((1, e), F32),
                   jax.ShapeDtypeStruct((1, H), F32)),
        in_specs=[pl.BlockSpec((tb, H), col(0)), pl.BlockSpec((tb, H), col(1)),
                  pl.BlockSpec((tb, H), col(2)), pl.BlockSpec((tb, H), col(3)),
                  pl.BlockSpec((tb, H), lambda h, n: (nb - 1 - n, h)),
                  pl.BlockSpec((None, nc, H, H), lambda h, n: (h, nb - 1 - n, 0, 0)),
                  pl.BlockSpec((2, H), lambda h, n: (0, h)), pl.BlockSpec((1, H), lambda h, n: (0, 0))],
        out_specs=(pl.BlockSpec((4, tb, H), lambda h, n: (0, nb - 1 - n, h)),
                   pl.BlockSpec((1, H), lambda h, n: (0, h)), pl.BlockSpec((1, H), lambda h, n: (0, 0))),
        scratch_shapes=[pltpu.VMEM((H, H), F32)] + [pltpu.VMEM((tb, H), F32)] * 5,
        compiler_params=_params("arbitrary", "arbitrary"))(z, z, z, z, dy, states, lb_logits, gnorm)


ATTN_T = 16 * ATTN_BAND
SCALE = ATTN_HEAD ** -0.5
TILE_UNROLL = 2


def _slope(hh, nheads):
    head = (2 * pl.program_id(0) + hh + 1).astype(F32)
    return jnp.exp(jnp.full((1, 1), -8.0 / nheads * math.log(2.0), F32) * head)


def _fill_bias(bias, nheads, delta, edge_ok):
    band = (delta >= 0) & (delta <= ATTN_BAND)
    dist = delta.astype(F32)
    for pi, dil in enumerate(DILATIONS):
        for hh in range(2):
            full = jnp.where(band, -(_slope(hh, nheads) * float(dil)) * dist, NEG_BIG)
            bias[(pi * 2 + hh) * 2] = full
            bias[(pi * 2 + hh) * 2 + 1] = jnp.where(edge_ok, full, NEG_BIG)


def _rows(start, size, stride):
    if stride == 1:
        return pl.ds(pl.multiple_of(start, ATTN_BAND), size)
    return pl.ds(start, size, stride=stride)


def _head_lanes(rows, hh):
    return (lax.broadcasted_iota(jnp.int32, (rows, LANES), 1) // ATTN_HEAD) == hh


def _attn_fwd(z):
    s = z.shape[0]
    e = z.shape[1] // N_SPLITS
    npair = e // LANES
    T = ATTN_T
    assert s % T == 0
    nsb = s // T
    W = ATTN_BAND
    nt = T // W
    HD = ATTN_HEAD
    chunk = 256

    def body(q_ref, kp_ref, kc_ref, vp_ref, vc_ref, g_ref, o_ref, l_ref, y_ref, qa, kbuf, va, bias, accs, ms, lsw):
        sb = pl.program_id(1)
        def stage(i, carry):
            rows = pl.ds(pl.multiple_of(i * chunk, chunk), chunk)
            upper = pl.ds(pl.multiple_of(T + i * chunk, chunk), chunk)
            kbuf[upper, :] = kc_ref[rows, :]
            for hh in range(2):
                mine = _head_lanes(chunk, hh)
                qa[hh, rows, :] = jnp.where(mine, q_ref[rows, :] * SCALE, 0.0)
                va[hh, upper, :] = jnp.where(mine, vc_ref[rows, :], 1.0)
            return carry

        lax.fori_loop(0, T // chunk, stage, 0)

        @pl.when(sb == 0)
        def _():
            def stage_prev(i, carry):
                rows = pl.ds(pl.multiple_of(i * chunk, chunk), chunk)
                kbuf[rows, :] = kp_ref[rows, :]
                for hh in range(2):
                    va[hh, rows, :] = jnp.where(_head_lanes(chunk, hh), vp_ref[rows, :], 1.0)
                return carry

            lax.fori_loop(0, T // chunk, stage_prev, 0)
        qi = lax.broadcasted_iota(jnp.int32, (W, 2 * W), 0)
        kj = lax.broadcasted_iota(jnp.int32, (W, 2 * W), 1)
        _fill_bias(bias, 2 * npair, W + qi - kj, kj >= W)

        def tile(tau, carry):
            first = _head_lanes(W, 0)
            rows, scores = [], []
            for pi, dil in enumerate(DILATIONS):
                r = tau % dil
                ub = tau // dil
                qrows = _rows(r + dil * W * ub, W, dil)
                krows = _rows(T + dil * W * (ub - 1) + r, 2 * W, dil)
                var = jnp.where((sb == 0) & (ub == 0), 1, 0)
                kt = kbuf[krows, :].astype(BF16)
                rows.append((qrows, krows))
                scores.append([_dot_nt(qa[hh, qrows, :].astype(BF16), kt) + bias[(pi * 2 + hh) * 2 + var]
                               for hh in range(2)])
            maxes = [[jnp.max(sc, axis=-1, keepdims=True) for sc in pair] for pair in scores]
            probs = [[jnp.exp(sc - m).astype(BF16) for sc, m in zip(ps, pm)] for ps, pm in zip(scores, maxes)]
            for pi, (qrows, krows) in enumerate(rows):
                outs = [_dot(probs[pi][hh], va[hh, krows, :].astype(BF16)) for hh in range(2)]
                accs[pi, qrows, :] = jnp.where(first, outs[0], outs[1])
                lsw[pi, qrows, :] = jnp.where(first, outs[1], outs[0])
                ms[pi, qrows, :] = jnp.where(first, maxes[pi][0], maxes[pi][1])
            return carry

        lax.fori_loop(0, nt, tile, 0, unroll=TILE_UNROLL)

        def merge(i, carry):
            rows = pl.ds(pl.multiple_of(i * chunk, chunk), chunk)
            m1, m2, m3 = ms[0, rows, :], ms[1, rows, :], ms[2, rows, :]
            mx = jnp.maximum(jnp.maximum(m1, m2), m3)
            w1, w2, w3 = jnp.exp(m1 - mx), jnp.exp(m2 - mx), jnp.exp(m3 - mx)
            unswap = lambda a: pltpu.roll(a, ATTN_HEAD, 1)
            den = w1 * unswap(lsw[0, rows, :]) + w2 * unswap(lsw[1, rows, :]) + w3 * unswap(lsw[2, rows, :])
            o = (w1 * accs[0, rows, :] + w2 * accs[1, rows, :] + w3 * accs[2, rows, :]) / den
            o_ref[rows, :] = o
            l_ref[rows, :] = mx + jnp.log(den)
            gp = g_ref[rows, :]
            y_ref[rows, :] = (o * (gp * _sigmoid(gp))).astype(BF16)
            upper = pl.ds(pl.multiple_of(T + i * chunk, chunk), chunk)
            kbuf[rows, :] = kbuf[upper, :]
            for hh in range(2):
                va[hh, rows, :] = va[hh, upper, :]
            return carry

        lax.fori_loop(0, T // chunk, merge, 0)

    cur = lambda split: (lambda hp, sb: (sb, split * npair + hp))
    prev = lambda split: (lambda hp, sb: (0, split * npair + hp))
    blk = lambda index: pl.BlockSpec((T, LANES), index)
    out = blk(lambda hp, sb: (sb, hp))
    buf = lambda rows: pltpu.VMEM((rows, LANES), F32)
    return pl.pallas_call(
        body, name="attn_fwd", grid=(npair, nsb),
        out_shape=(jax.ShapeDtypeStruct((s, e), F32), jax.ShapeDtypeStruct((s, e), F32), jax.ShapeDtypeStruct((s, e), BF16)),
        in_specs=[blk(cur(4)), blk(prev(5)), blk(cur(5)), blk(prev(6)), blk(cur(6)), blk(cur(7))],
        out_specs=(out, out, out),
        scratch_shapes=[pltpu.VMEM((2, T, LANES), F32), buf(2 * T), pltpu.VMEM((2, 2 * T, LANES), F32),
                        pltpu.VMEM((12, W, 2 * W), F32)] + [pltpu.VMEM((3, T, LANES), F32)] * 3,
        compiler_params=_params("parallel", "arbitrary"))(z, z, z, z, z, z)


def _outproj_loss(x, y_h, y_a, w_out_full, final_gain, target):
    s, d = x.shape
    e = y_h.shape[1]
    tm = _tile(s, 512)
    sub = _tile(tm, 256)

    def body(x_ref, yh_ref, ya_ref, w_ref, g_ref, t_ref, dx_ref, dxb_ref, dy_ref, loss_ref, dg_ref):
        @pl.when(pl.program_id(0) == 0)
        def _():
            loss_ref[...] = jnp.zeros_like(loss_ref)
            dg_ref[...] = jnp.zeros_like(dg_ref)

        w = w_ref[...]
        g = g_ref[...]
        parts = [slice(r0, r0 + sub) for r0 in range(0, tm, sub)]
        x2s = [x_ref[rows, :] + _dot(yh_ref[rows, :], w[0:e]) + _dot(ya_ref[rows, :], w[e:2 * e]) for rows in parts]
        for rows, x2 in zip(parts, x2s):
            r = lax.rsqrt(jnp.mean(x2 * x2, axis=-1, keepdims=True) + NORM_EPS)
            xn = x2 * r
            err = xn * g - t_ref[rows, :]
            loss_ref[...] += jnp.sum(err * err, axis=0, keepdims=True) * (0.5 / d)
            dyo = err * (1.0 / d)
            dg_ref[...] += jnp.sum(dyo * xn, axis=0, keepdims=True)
            u = dyo * g
            dx2 = r * (u - xn * jnp.mean(u * xn, axis=-1, keepdims=True))
            dx_ref[rows, :] = dx2
            dxb = dx2.astype(BF16)
            dxb_ref[rows, :] = dxb
            dy_ref[rows, :] = _dot_nt(dxb, w)

    row = pl.BlockSpec((tm, d), lambda i: (i, 0))
    half = pl.BlockSpec((tm, e), lambda i: (i, 0))
    vec = pl.BlockSpec((1, d), lambda i: (0, 0))
    whole = pl.BlockSpec((2 * e, d), lambda i: (0, 0), pipeline_mode=pl.Buffered(1))
    return pl.pallas_call(
        body, name="outproj_loss", grid=(s // tm,),
        out_shape=(jax.ShapeDtypeStruct((s, d), F32), jax.ShapeDtypeStruct((s, d), BF16),
                   jax.ShapeDtypeStruct((s, 2 * e), F32), jax.ShapeDtypeStruct((1, d), F32),
                   jax.ShapeDtypeStruct((1, d), F32)),
        in_specs=[row, half, half, whole, vec, row],
        out_specs=(row, row, pl.BlockSpec((tm, 2 * e), lambda i: (i, 0)), vec, vec),
        compiler_params=pltpu.CompilerParams(dimension_semantics=("arbitrary",), vmem_limit_bytes=OUTPROJ_VMEM_LIMIT),
    )(x, y_h, y_a, w_out_full, final_gain, target)


def _attn_bwd(z, dy, o, lse):
    s, e = o.shape
    npair = e // LANES
    T = ATTN_T
    assert s % T == 0
    nsb = s // T
    W = ATTN_BAND
    nt = T // W
    HD = ATTN_HEAD
    chunk = 256

    def body(k_ref, v_ref, qc_ref, qn_ref, dyc_ref, dyn_ref, gc_ref, gn_ref, oc_ref, on_ref, lc_ref, ln_ref,
             dz_ref, qa, doa, ka, va, dqacc, dkacc, dvacc, bias):
        sb = pl.program_id(1)
        def stage_queries(half, q_r, dy_r, g_r, o_r, l_r):
            def stage(i, carry):
                rows = pl.ds(pl.multiple_of(i * chunk, chunk), chunk)
                dst = pl.ds(pl.multiple_of(half * T + i * chunk, chunk), chunk)
                lane = lax.broadcasted_iota(jnp.int32, (chunk, LANES), 1)
                gp = g_r[rows, :]
                dov = dy_r[rows, :] * (gp * _sigmoid(gp))
                qv = q_r[rows, :] * SCALE
                same_head = (lax.broadcasted_iota(jnp.int32, (LANES, LANES), 0) // HD
                             == lax.broadcasted_iota(jnp.int32, (LANES, LANES), 1) // HD)
                ones = jnp.where(same_head, 1.0, 0.0).astype(BF16)
                hi, mid, lo = (p.astype(BF16) for p in _split3(dov * o_r[rows, :]))
                delta = _dot(hi, ones) + _dot(mid, ones) + _dot(lo, ones)
                swap = lambda a: pltpu.roll(a, HD, 1)
                lse_parts = [swap(p) for p in _split3(l_r[rows, :])]
                dl_parts = [swap(p) for p in _split3(delta)]
                for hh in range(2):
                    mine = _head_lanes(chunk, hh)
                    spare = (1 - hh) * HD
                    qh = jnp.where(mine, qv, 0.0)
                    dh = jnp.where(mine, dov, 0.0)
                    for j in range(3):
                        qh = jnp.where(lane == spare + j, lse_parts[j], qh)
                        dh = jnp.where(lane == spare + j, dl_parts[j], dh)
                    qa[hh, dst, :] = qh
                    doa[hh, dst, :] = dh
                return carry

            lax.fori_loop(0, T // chunk, stage, 0)

        @pl.when(sb == 0)
        def _():
            stage_queries(0, qc_ref, dyc_ref, gc_ref, oc_ref, lc_ref)

        stage_queries(1, qn_ref, dyn_ref, gn_ref, on_ref, ln_ref)

        def stage_keys(i, carry):
            rows = pl.ds(pl.multiple_of(i * chunk, chunk), chunk)
            lane = lax.broadcasted_iota(jnp.int32, (chunk, LANES), 1)
            for hh in range(2):
                spare = (1 - hh) * HD
                minus = (lane >= spare) & (lane < spare + 3)
                ka[hh, rows, :] = jnp.where(minus, -1.0, k_ref[rows, :])
                va[hh, rows, :] = jnp.where(minus, -1.0, v_ref[rows, :])
            gp = gc_ref[rows, :]
            dz_ref[3, rows, :] = (dyc_ref[rows, :] * oc_ref[rows, :] * _dsilu(gp, _sigmoid(gp))).astype(BF16)
            return carry

        lax.fori_loop(0, T // chunk, stage_keys, 0)

        @pl.when(sb == 0)
        def _():
            dqacc[0:T, :] = jnp.zeros((T, LANES), F32)

        dqacc[T:, :] = jnp.zeros((T, LANES), F32)
        dkacc[...] = jnp.zeros_like(dkacc)
        dvacc[...] = jnp.zeros_like(dvacc)
        qi = lax.broadcasted_iota(jnp.int32, (2 * W, W), 0)
        kj = lax.broadcasted_iota(jnp.int32, (2 * W, W), 1)
        _fill_bias(bias, 2 * npair, qi - kj, qi < W)

        def tile(tau, carry):
            def scores(step, pi):
                dil = DILATIONS[pi]
                r = step % dil
                ub = step // dil
                start = r + dil * W * ub
                krows = _rows(start, W, dil)
                qrows = _rows(start, 2 * W, dil)
                var = jnp.where((sb == nsb - 1) & (ub == nt // dil - 1), 1, 0)
                unit = dict(krows=krows, qrows=qrows, ops=[], sc=[], dpd=[])
                for hh in range(2):
                    kt = ka[hh, krows, :].astype(BF16)
                    vt = va[hh, krows, :].astype(BF16)
                    qt = qa[hh, qrows, :].astype(BF16)
                    dt = doa[hh, qrows, :].astype(BF16)
                    unit["ops"].append((kt, qt, dt))
                    unit["sc"].append(_dot_nt(qt, kt) + bias[(pi * 2 + hh) * 2 + var])
                    unit["dpd"].append(_dot_nt(dt, vt))
                return unit

            def elementwise(unit):
                ps = [jnp.exp(s_) for s_ in unit["sc"]]
                unit["ds"] = [(p * d).astype(BF16) for p, d in zip(ps, unit["dpd"])]
                unit["pb"] = [p.astype(BF16) for p in ps]

            def products(unit):
                dvs = [_dot_tn(pb, dt) for pb, (kt, qt, dt) in zip(unit["pb"], unit["ops"])]
                dks = [_dot_tn(ds, qt) for ds, (kt, qt, dt) in zip(unit["ds"], unit["ops"])]
                dqs = [_dot(ds, kt) for ds, (kt, qt, dt) in zip(unit["ds"], unit["ops"])]
                dkacc[unit["krows"], :] += jnp.where(_head_lanes(W, 0), dks[0], dks[1])
                dvacc[unit["krows"], :] += jnp.where(_head_lanes(W, 0), dvs[0], dvs[1])
                dqacc[unit["qrows"], :] += jnp.where(_head_lanes(2 * W, 0), dqs[0], dqs[1]) * SCALE

            order = [(2 * tau + half, pi) for half in range(2) for pi in range(len(DILATIONS))]
            units = [None] * len(order)
            for n in range(len(order) + 2):
                if n < len(order):
                    units[n] = scores(*order[n])
                if 1 <= n <= len(order):
                    elementwise(units[n - 1])
                if n >= 2:
                    products(units[n - 2])
            return carry

        lax.fori_loop(0, nt // 2, tile, 0)

        def flush(i, carry):
            rows = pl.ds(pl.multiple_of(i * chunk, chunk), chunk)
            nxt = pl.ds(pl.multiple_of(T + i * chunk, chunk), chunk)
            dz_ref[0, rows, :] = dqacc[rows, :].astype(BF16)
            dz_ref[1, rows, :] = dkacc[rows, :].astype(BF16)
            dz_ref[2, rows, :] = dvacc[rows, :].astype(BF16)
            dqacc[rows, :] = dqacc[nxt, :]
            for hh in range(2):
                qa[hh, rows, :] = qa[hh, nxt, :]
                doa[hh, rows, :] = doa[hh, nxt, :]
            return carry

        lax.fori_loop(0, T // chunk, flush, 0)

    zc = lambda split: (lambda hp, sb: (sb, split * npair + hp))
    zn = lambda split: (lambda hp, sb: (jnp.minimum(sb + 1, nsb - 1), split * npair + hp))
    ec = lambda off: (lambda hp, sb: (sb, off + hp))
    en = lambda off: (lambda hp, sb: (jnp.minimum(sb + 1, nsb - 1), off + hp))
    z0 = lambda split: (lambda hp, sb: (0, split * npair + hp))
    e0 = lambda off: (lambda hp, sb: (0, off + hp))
    blk = lambda index: pl.BlockSpec((T, LANES), index)
    buf = lambda rows: pltpu.VMEM((rows, LANES), F32)
    return pl.pallas_call(
        body, name="attn_bwd", grid=(npair, nsb), out_shape=jax.ShapeDtypeStruct((4, s, e), BF16),
        in_specs=[blk(zc(5)), blk(zc(6)), blk(z0(4)), blk(zn(4)), blk(ec(npair)), blk(en(npair)),
                  blk(zc(7)), blk(zn(7)), blk(ec(0)), blk(en(0)), blk(e0(0)), blk(en(0))],
        out_specs=pl.BlockSpec((4, T, LANES), lambda hp, sb: (0, sb, hp)),
        scratch_shapes=[pltpu.VMEM((2, 2 * T, LANES), F32), pltpu.VMEM((2, 2 * T, LANES), F32),
                        pltpu.VMEM((2, T, LANES), F32), pltpu.VMEM((2, T, LANES), F32),
                        buf(2 * T), buf(T), buf(T), pltpu.VMEM((12, 2 * W, W), F32)],
        compiler_params=_params("parallel", "arbitrary"))(z, z, z, z, dy, dy, z, z, o, o, lse, lse)


def _dz_specs(tm, e):
    def mk(lo, hi):
        return pl.BlockSpec((None, tm, e), lambda i, k: (jnp.clip(k - lo, 0, hi - lo - 1), i, 0))
    return [mk(0, 4), mk(4, 8)]


def _dz_pick(grp, dzh_ref, dza_ref, fn):
    @pl.when(grp < 4)
    def _():
        fn(dzh_ref[...])

    @pl.when(grp >= 4)
    def _():
        fn(dza_ref[...])


def _dh_dx(dzh, dza, w_full, x, gain, dx2):
    s, d = x.shape
    e = dzh.shape[2]
    tm = _tile(s, 1024)
    ni = s // tm
    chunk = _tile(tm, 128)
    fetch_at = 2

    def body(dzh_ref, dza_ref, w_ref, x_hbm, g_ref, dx2_hbm, gx_hbm, dg_ref, acc, xbuf, dbuf, sems):
        i, k = pl.program_id(0), pl.program_id(1)
        rows_of = lambda tile: pl.ds(pl.multiple_of(tile * tm, tm), tm)
        fetch_x = pltpu.make_async_copy(x_hbm.at[rows_of(i), :], xbuf, sems.at[0])
        fetch_d = pltpu.make_async_copy(dx2_hbm.at[rows_of(i), :], dbuf, sems.at[1])

        def store(tile):
            return pltpu.make_async_copy(xbuf, gx_hbm.at[rows_of(tile), :], sems.at[2])

        @pl.when((i == 0) & (k == 0))
        def _():
            dg_ref[...] = jnp.zeros_like(dg_ref)

        @pl.when((k == fetch_at) & (i > 0))
        def _():
            store(i).wait()

        @pl.when(k == fetch_at)
        def _():
            fetch_x.start()
            fetch_d.start()

        def finish(tile):
            fetch_x.wait()
            fetch_d.wait()
            gain_row = g_ref[...]
            for r0 in range(0, tm, chunk):
                rows = slice(r0, r0 + chunk)
                dh = acc[rows, :]
                xv = xbuf[rows, :]
                r = lax.rsqrt(jnp.mean(xv * xv, axis=-1, keepdims=True) + NORM_EPS)
                xn = xv * r
                u = dh * gain_row
                xbuf[rows, :] = dbuf[rows, :] + r * (u - xn * jnp.mean(u * xn, axis=-1, keepdims=True))
                dg_ref[...] += jnp.sum(dh * xn, axis=0, keepdims=True)
            store(tile).start()

        @pl.when((k == 0) & (i == 0))
        def _():
            acc[...] = _dot_nt(dzh_ref[...], w_ref[...])

        @pl.when((k == 0) & (i > 0))
        def _():
            finish(i - 1)
            acc[...] = _dot_nt(dzh_ref[...], w_ref[...])

        @pl.when(k > 0)
        def _():
            def add(dz):
                acc[...] += _dot_nt(dz, w_ref[...])

            _dz_pick(k, dzh_ref, dza_ref, add)

        @pl.when((k == N_SPLITS - 1) & (i == ni - 1))
        def _():
            finish(i)
            store(i).wait()

    vec = pl.BlockSpec((1, d), lambda i, k: (0, 0))
    return pl.pallas_call(
        body, name="dh_dx", grid=(ni, N_SPLITS),
        out_shape=(jax.ShapeDtypeStruct((s, d), F32), jax.ShapeDtypeStruct((1, d), F32)),
        in_specs=_dz_specs(tm, e) + [pl.BlockSpec((None, d, e), lambda i, k: (k, 0, 0)), ANY, vec, ANY],
        out_specs=(ANY, vec),
        scratch_shapes=[pltpu.VMEM((tm, d), F32)] * 3 + [pltpu.SemaphoreType.DMA((3,))],
        compiler_params=pltpu.CompilerParams(dimension_semantics=("arbitrary", "arbitrary"),
                                             vmem_limit_bytes=DHDX_VMEM_LIMIT))(dzh, dza, w_full, x, gain, dx2)


def _position():
    x, y, c = lax.axis_index("x"), lax.axis_index("y"), lax.axis_index("c")
    return x, y, c


def _xor_peer(x, y, c, mask):
    return (x ^ ((mask >> 2) & 1), y ^ ((mask >> 1) & 1), c ^ (mask & 1))


def _block_order(masks):
    me = 4 * lax.axis_index("x") + 2 * lax.axis_index("y") + lax.axis_index("c")
    return jnp.stack([me ^ m for m in masks]).astype(jnp.int32)


GATHER_MASKS = (0, 1, 4, 5, 2, 3, 6, 7)


def _inproj_gather(h, w_loc, wo_loc):
    s, d = h.shape
    e = w_loc.shape[1]
    tm = _tile(s, 1024)
    ni = s // tm
    pre = max(ni - 2, 0)

    def body(order_ref, h_ref, w_ref, wo_ref, z_ref, wf_ref, wof_ref, wbuf, send_sems, recv_sems, osend, orecv,
             local_sems, wsems):
        j, i = pl.program_id(0), pl.program_id(1)
        x, y, c = _position()
        me, sibling = (x, y, c), (x, y, 1 - c)
        chips = [(1 - x, y), (x, 1 - y), (1 - x, 1 - y)]
        blk = lambda p: 4 * p[0] + 2 * p[1] + p[2]

        def copy(k, block, to, src=None):
            dst = wf_ref.at[blk(block)]
            return pltpu.make_async_remote_copy(
                src_ref=dst if src is None else src, dst_ref=dst, send_sem=send_sems.at[k], recv_sem=recv_sems.at[k],
                device_id=to, device_id_type=MESH)

        first = [copy(0, me, sibling, src=w_ref)] + [copy(1 + q, me, (*chip, c), src=w_ref) for q, chip in enumerate(chips)]
        passed = [copy(4 + q, (*chip, c), sibling) for q, chip in enumerate(chips)]
        mine = pltpu.make_async_copy(w_ref, wf_ref.at[blk(me)], local_sems.at[0])
        ocopies = [pltpu.make_async_remote_copy(
            src_ref=wo_ref, dst_ref=wof_ref.at[blk(me)], send_sem=osend.at[k], recv_sem=orecv.at[k],
            device_id=_xor_peer(x, y, c, k + 1), device_id_type=MESH) for k in range(N_DEV - 1)]
        omine = pltpu.make_async_copy(wo_ref, wof_ref.at[blk(me)], local_sems.at[1])
        blocks = [me, sibling] + [(*chip, c) for chip in chips] + [(*chip, 1 - c) for chip in chips]
        arrive = [None, copy(0, sibling, me)] + [copy(1 + q, (*chip, c), me) for q, chip in enumerate(chips)] \
            + [copy(4 + q, (*chip, 1 - c), me) for q, chip in enumerate(chips)]
        forward = [None, None] + passed + [None, None, None]
        use_order = (0, 1, 2, 5, 3, 6, 4, 7)
        blocks, arrive, forward = ([lst[n] for n in use_order] for lst in (blocks, arrive, forward))

        def load(slot, src):
            return pltpu.make_async_copy(src, wbuf.at[slot], wsems.at[slot])

        @pl.when((j == 0) & (i == 0))
        def _():
            for cp in [mine, omine] + first + ocopies:
                cp.start()
            load(0, w_ref).start()

        for jj in range(N_DEV):
            @pl.when((j == jj) & (i == 0))
            def _():
                load(jj % 2, w_ref).wait()

            if jj + 1 < N_DEV:
                @pl.when((j == jj) & (i == pre))
                def _():
                    arrive[jj + 1].wait_recv()
                    if forward[jj + 1] is not None:
                        forward[jj + 1].start()
                    load((jj + 1) % 2, wf_ref.at[blk(blocks[jj + 1])]).start()

        z_ref[...] = _dot(h_ref[...], wbuf[j % 2])

        @pl.when((j == N_DEV - 1) & (i == ni - 1))
        def _():
            for cp in first + passed:
                cp.wait_send()
            for cp in ocopies:
                cp.wait_send()
                cp.wait_recv()
            mine.wait()
            omine.wait()

    grid_spec = pltpu.PrefetchScalarGridSpec(
        num_scalar_prefetch=1, grid=(N_DEV, ni),
        in_specs=[pl.BlockSpec((tm, d), lambda j, i, o: (i, 0)), ANY, ANY],
        out_specs=(pl.BlockSpec((tm, e), lambda j, i, o: (i, o[j])), ANY, ANY),
        scratch_shapes=[pltpu.VMEM((2, d, e), BF16), pltpu.SemaphoreType.DMA((7,)), pltpu.SemaphoreType.DMA((7,)),
                        pltpu.SemaphoreType.DMA((7,)), pltpu.SemaphoreType.DMA((7,)), pltpu.SemaphoreType.DMA((2,)),
                        pltpu.SemaphoreType.DMA((2,))])
    return pl.pallas_call(
        body, name="inproj_gather", grid_spec=grid_spec,
        out_shape=(jax.ShapeDtypeStruct((s, N_SPLITS * e), F32), jax.ShapeDtypeStruct((N_DEV, d, e), BF16),
                   jax.ShapeDtypeStruct((N_DEV,) + wo_loc.shape, BF16)),
        compiler_params=_params("arbitrary", "arbitrary"))(_block_order(GATHER_MASKS), h, w_loc, wo_loc)


SCATTER_MASKS = (7, 6, 5, 4, 3, 2, 1, 0)
N_CHIPS = 4


def _scatter_block(k, acc, stage, tmp, own_ref, ra_ref, rb_ref, sa_send, sa_recv, sb_send, sb_recv, loc_sem, step, ns):
    x, y, c = _position()
    chip_of = lambda t: _xor_peer(x, y, c, SCATTER_MASKS[2 * t + 1])
    last = step == ns - 1
    fetch_at = min(1, ns - 1)

    def ship(t):
        return pltpu.make_async_remote_copy(
            src_ref=stage.at[0], dst_ref=ra_ref.at[t], send_sem=sa_send.at[t], recv_sem=sa_recv.at[t],
            device_id=(x, y, 1 - c), device_id_type=MESH)

    def send(t):
        return pltpu.make_async_remote_copy(
            src_ref=stage.at[1], dst_ref=rb_ref.at[t], send_sem=sb_send.at[t], recv_sem=sb_recv.at[t],
            device_id=chip_of(t), device_id_type=MESH)

    for kk in range(N_DEV):
        t = kk // 2
        fetch = pltpu.make_async_copy(ra_ref.at[t], tmp, loc_sem)

        if kk % 2 == 1:
            @pl.when((step == fetch_at) & (k == kk))
            def _():
                ship(t).wait_recv()
                fetch.start()

        @pl.when(last & (k == kk))
        def _():
            if kk % 2 == 0:
                if t >= 1:
                    ship(t - 1).wait_send()
                stage[0] = acc[...].astype(BF16)
                ship(t).start()
            else:
                fetch.wait()
                acc[...] += tmp[...].astype(F32)
                if t < N_CHIPS - 1:
                    if t >= 1:
                        send(t - 1).wait_send()
                    stage[1] = acc[...].astype(BF16)
                    send(t).start()
                else:
                    keep = pltpu.make_async_copy(acc, own_ref, loc_sem)
                    keep.start()
                    keep.wait()
                    ship(t).wait_send()
                    send(t - 1).wait_send()
                    for q in range(N_CHIPS - 1):
                        send(q).wait_recv()


def _scatter_scratch(rows, cols):
    return [pltpu.VMEM((rows, cols), F32), pltpu.VMEM((2, rows, cols), BF16), pltpu.VMEM((rows, cols), BF16),
            pltpu.SemaphoreType.DMA((N_CHIPS,)), pltpu.SemaphoreType.DMA((N_CHIPS,)),
            pltpu.SemaphoreType.DMA((N_CHIPS - 1,)), pltpu.SemaphoreType.DMA((N_CHIPS - 1,)), pltpu.SemaphoreType.DMA(())]


def _scatter_out(rows, cols):
    return (jax.ShapeDtypeStruct((rows, cols), F32), jax.ShapeDtypeStruct((N_CHIPS, rows, cols), BF16),
            jax.ShapeDtypeStruct((N_CHIPS - 1, rows, cols), BF16))


def _dwin_scatter(h, dzh, dza):
    s, d = h.shape
    e = dzh.shape[2]
    ts = _tile(s, 1024)
    ns = s // ts

    def body(order_ref, dzh_ref, dza_ref, h_ref, own_ref, ra_ref, rb_ref, acc, stage, tmp, *sems):
        k, step = pl.program_id(0), pl.program_id(1)

        @pl.when(step == 0)
        def _():
            acc[...] = jnp.zeros_like(acc)

        def add(dz):
            acc[...] += _dot_tn(h_ref[...], dz)

        _dz_pick(order_ref[k], dzh_ref, dza_ref, add)
        _scatter_block(k, acc, stage, tmp, own_ref, ra_ref, rb_ref, *sems, step, ns)

    def dz_spec(lo):
        return pl.BlockSpec((None, ts, e), lambda k, st, o: (jnp.clip(o[k] - lo, 0, 3), st, 0))

    grid_spec = pltpu.PrefetchScalarGridSpec(
        num_scalar_prefetch=1, grid=(N_DEV, ns),
        in_specs=[dz_spec(0), dz_spec(4), pl.BlockSpec((ts, d), lambda k, st, o: (st, 0))],
        out_specs=(ANY, ANY, ANY), scratch_shapes=_scatter_scratch(d, e))
    own, _, rb = pl.pallas_call(
        body, name="dwin_scatter", grid_spec=grid_spec, out_shape=_scatter_out(d, e),
        compiler_params=_params("arbitrary", "arbitrary"))(_block_order(SCATTER_MASKS), dzh, dza, h)
    return own, rb


def _dwout_scatter(y_h, y_a, dxb):
    s, e = y_h.shape
    d = dxb.shape[1]
    r = 2 * e // N_DEV
    pairs = e // (2 * r)
    ts = _tile(s, 1024)
    ns = s // ts
    chip_masks = SCATTER_MASKS[1::2]
    passes = ((0, 1), (2,), (3,))
    slots = max(len(chips) for chips in passes)
    slot_chip = [chips[min(u, len(chips) - 1)] for chips in passes for u in range(slots)]

    def body(pair_ref, yh0_ref, ya0_ref, yh1_ref, ya1_ref, dx_ref, own_ref, ra_ref, rb_ref, acc, keep_buf, ship_buf,
             send_buf, tmp, sa_send, sa_recv, sb_send, sb_recv, loc_sem):
        p, step = pl.program_id(0), pl.program_id(1)
        x, y, c = _position()

        @pl.when(step == 0)
        def _():
            acc[...] = jnp.zeros_like(acc)

        for u, (yh_ref, ya_ref) in enumerate(((yh0_ref, ya0_ref), (yh1_ref, ya1_ref))):
            rows = slice(u * 2 * r, (u + 1) * 2 * r)
            used = functools.reduce(jnp.logical_or, [p == pp for pp, chips in enumerate(passes) if u < len(chips)])

            @pl.when(used & (pair_ref[slots * p + u] < pairs))
            def _():
                acc[rows, :] += _dot_tn(yh_ref[...], dx_ref[...])

            @pl.when(used & (pair_ref[slots * p + u] >= pairs))
            def _():
                acc[rows, :] += _dot_tn(ya_ref[...], dx_ref[...])

        def block_rows(u, core):
            return pl.ds(pl.multiple_of(u * 2 * r + core * r, r), r)

        slot_of = {q: u for chips in passes for u, q in enumerate(chips)}

        def ship(q):
            return pltpu.make_async_remote_copy(
                src_ref=ship_buf.at[slot_of[q]], dst_ref=ra_ref.at[q], send_sem=sa_send.at[q], recv_sem=sa_recv.at[q],
                device_id=(x, y, 1 - c), device_id_type=MESH)

        def send(q):
            return pltpu.make_async_remote_copy(
                src_ref=send_buf.at[slot_of[q]], dst_ref=rb_ref.at[q], send_sem=sb_send.at[q], recv_sem=sb_recv.at[q],
                device_id=_xor_peer(x, y, c, chip_masks[q]), device_id_type=MESH)

        def sibling_share(q):
            ship(q).wait_recv()
            fetch = pltpu.make_async_copy(ra_ref.at[q], tmp, loc_sem)
            fetch.start()
            fetch.wait()
            return tmp[...].astype(F32)

        shipped, sent = {}, {}
        for pp, chips in enumerate(passes):
            @pl.when((step == ns - 1) & (p == pp))
            def _():
                for u, q in enumerate(chips):
                    if u in shipped:
                        ship(shipped.pop(u)).wait_send()
                    ship_buf[u] = acc[block_rows(u, 1 - c), :].astype(BF16)
                    ship(q).start()
                    shipped[u] = q
                for u, q in enumerate(chips):
                    total = acc[block_rows(u, c), :] + sibling_share(q)
                    if q < N_CHIPS - 1:
                        if u in sent:
                            send(sent.pop(u)).wait_send()
                        send_buf[u] = total.astype(BF16)
                        send(q).start()
                        sent[u] = q
                    else:
                        keep_buf[...] = total
                        keep = pltpu.make_async_copy(keep_buf, own_ref, loc_sem)
                        keep.start()
                        keep.wait()
                if pp == len(passes) - 1:
                    for q in shipped.values():
                        ship(q).wait_send()
                    for q in sent.values():
                        send(q).wait_send()
                    for q in range(N_CHIPS - 1):
                        send(q).wait_recv()

    def y_spec(u, lo):
        return pl.BlockSpec((ts, 2 * r), lambda p, st, o: (st, jnp.clip(o[slots * p + u] - lo, 0, pairs - 1)))

    pair_of_chip = _block_order(chip_masks) // 2
    grid_spec = pltpu.PrefetchScalarGridSpec(
        num_scalar_prefetch=1, grid=(len(passes), ns),
        in_specs=[y_spec(0, 0), y_spec(0, pairs), y_spec(1, 0), y_spec(1, pairs),
                  pl.BlockSpec((ts, d), lambda p, st, o: (st, 0))],
        out_specs=(ANY, ANY, ANY),
        scratch_shapes=[pltpu.VMEM((slots * 2 * r, d), F32), pltpu.VMEM((r, d), F32),
                        pltpu.VMEM((slots, r, d), BF16)] + _scatter_scratch(r, d)[1:])
    own, _, rb = pl.pallas_call(
        body, name="dwout_scatter", grid_spec=grid_spec, out_shape=_scatter_out(r, d),
        compiler_params=_params("arbitrary", "arbitrary"))(
            jnp.stack([pair_of_chip[q] for q in slot_chip]), y_h, y_a, y_h, y_a, dxb)
    return own, rb


def _sum_chips_adamw(own, recv, w, m, v):
    r, c = w.shape
    tr = _tile(r, 128)

    def body(own_ref, rc_ref, w_ref, m_ref, v_ref, g_ref, d_ref, mo_ref, vo_ref):
        g = own_ref[...]
        for q in range(N_CHIPS - 1):
            g = g + rc_ref[q].astype(F32)
        g_ref[...] = g
        d_ref[...], mo_ref[...], vo_ref[...] = _adamw(w_ref[...], g, m_ref[...], v_ref[...])

    blk = pl.BlockSpec((tr, c), lambda i: (i, 0))
    shp = jax.ShapeDtypeStruct((r, c), F32)
    return pl.pallas_call(
        body, name="sum_chips_adamw", grid=(r // tr,), out_shape=(shp, shp, shp, shp),
        in_specs=[blk, pl.BlockSpec((N_CHIPS - 1, tr, c), lambda i: (0, i, 0)), blk, blk, blk],
        out_specs=(blk, blk, blk, blk), compiler_params=_params("parallel"))(own, recv, w, m, v)


SMALL_ROWS = 8
ROW_LB = 4
ROW_GN = 6
ROW_LOSS = 7


def _small_allreduce_adamw(part, w, m, v, lb_logits):
    width = part.shape[1]

    def body(p_ref, w_ref, m_ref, v_ref, lb_ref, g_ref, d_ref, mo_ref, vo_ref, buf, send_sems, recv_sems):
        x, y, c = _position()
        me = 4 * x + 2 * y + c
        buf[me] = p_ref[...]
        copies = []
        for k in range(N_DEV - 1):
            bx, by, bc = ((k + 1) >> 2) & 1, ((k + 1) >> 1) & 1, (k + 1) & 1
            peer = (x ^ bx, y ^ by, c ^ bc)
            copies.append(pltpu.make_async_remote_copy(
                src_ref=p_ref, dst_ref=buf.at[me], send_sem=send_sems.at[k], recv_sem=recv_sems.at[k],
                device_id=peer, device_id_type=MESH))
        for cp in copies:
            cp.start()
        for cp in copies:
            cp.wait_recv()
        for cp in copies:
            cp.wait_send()
        tot = buf[0]
        for dev in range(1, N_DEV):
            tot = tot + buf[dev]
        lbv = lb_ref[...]
        lb = _sigmoid(lbv[0:1] - lbv[1:2])
        glb = tot[ROW_LB:ROW_LB + 1] * lb * (1.0 - lb)
        loss = jnp.sum(tot[ROW_LOSS:ROW_LOSS + 1], axis=-1, keepdims=True)
        row = lax.broadcasted_iota(jnp.int32, (SMALL_ROWS, width), 0)
        g = jnp.where(row == ROW_LB, glb, jnp.where(row == ROW_LB + 1, -glb, tot))
        g = jnp.where(row == ROW_LOSS, loss, g)
        g_ref[...] = g
        d_ref[...], mo_ref[...], vo_ref[...] = _adamw(w_ref[...], g, m_ref[...], v_ref[...])

    vm = pl.BlockSpec(memory_space=pltpu.VMEM)
    shp = jax.ShapeDtypeStruct((SMALL_ROWS, width), F32)
    return pl.pallas_call(
        body, name="small_allreduce_adamw", out_shape=(shp, shp, shp, shp),
        in_specs=[vm] * 5, out_specs=(vm, vm, vm, vm),
        scratch_shapes=[pltpu.VMEM((N_DEV, SMALL_ROWS, width), F32), pltpu.SemaphoreType.DMA((N_DEV - 1,)),
                        pltpu.SemaphoreType.DMA((N_DEV - 1,))],
    )(part, w, m, v, lb_logits)


def _pack_small(norm_gain, final_gain, lb2, gnorm, last_row, width):
    pad = lambda a: jnp.pad(a.reshape(1, -1), ((0, 0), (0, width - a.size)))
    return jnp.concatenate([norm_gain.reshape(2, width), final_gain.reshape(2, width), lb2.reshape(2, width),
                            pad(gnorm), last_row.reshape(1, width)], axis=0)


def _unpack_small(p, d, e, hd):
    return (p[0:2].reshape(1, d), p[2:4].reshape(d), p[4:6].reshape(2, e), p[6:7, :hd].reshape(1, hd))


def kernel(x, norm_gain, w_in, lb_logits, hgrn_gnorm, w_out, final_gain, loss_target, m_norm_gain, m_w_in, m_lb_logits, m_hgrn_gnorm, m_w_out, m_final_gain, v_norm_gain, v_w_in, v_lb_logits, v_hgrn_gnorm, v_w_out, v_final_gain):
    s, d = x.shape[1], x.shape[2]
    e = w_in.shape[2]
    assert d == 2 * e and lb_logits.shape == (2, e) and w_out.shape[1] * N_DEV == 2 * e
    x2d = x.reshape(s, d)
    tgt = loss_target.reshape(s, d)

    h = _rmsnorm_fwd(x2d, norm_gain)
    z, w_in_full, w_out_full = _inproj_gather(h, _cast_bf16(w_in[0]), _cast_bf16(w_out[0]))
    w_out_full = w_out_full.reshape(2 * e, d)
    y_h, states = _hgrn_fwd(z, lb_logits, hgrn_gnorm)
    o_attn, lse, y_a = _attn_fwd(z)
    dx2, dx2b, dy, loss_vec, dfg = _outproj_loss(x2d, y_h, y_a, w_out_full, final_gain.reshape(1, d), tgt)

    own_o, recv_o = _dwout_scatter(y_h, y_a, dx2b)
    dza = _attn_bwd(z, dy, o_attn, lse)
    dzh, dlb, dgn = _hgrn_bwd(z, dy, states, lb_logits, hgrn_gnorm)
    grad_x, dng = _dh_dx(dzh, dza, w_in_full, x2d, norm_gain, dx2)
    g_wo, d_wo, nm_wo, nv_wo = _sum_chips_adamw(own_o, recv_o, w_out[0], m_w_out[0], v_w_out[0])

    width = d // 2
    zero_row = jnp.zeros((1, width), F32)
    loss_row = loss_vec[:, :width] + loss_vec[:, width:]
    part = _pack_small(dng, dfg, jnp.concatenate([dlb, zero_row], axis=0), dgn, loss_row, width)
    pw = _pack_small(norm_gain, final_gain, lb_logits, hgrn_gnorm, zero_row, width)
    pm = _pack_small(m_norm_gain, m_final_gain, m_lb_logits, m_hgrn_gnorm, zero_row, width)
    pv = _pack_small(v_norm_gain, v_final_gain, v_lb_logits, v_hgrn_gnorm, zero_row, width)
    sg, sd, sm, sv = _small_allreduce_adamw(part, pw, pm, pv, lb_logits)
    own_i, recv_i = _dwin_scatter(h, dzh, dza)
    g_wi, d_wi, nm_wi, nv_wi = _sum_chips_adamw(own_i, recv_i, w_in[0], m_w_in[0], v_w_in[0])
    hd = hgrn_gnorm.shape[1]
    g_ng, g_fg, g_lb, g_gn = _unpack_small(sg, d, e, hd)
    d_ng, d_fg, d_lb, d_gn = _unpack_small(sd, d, e, hd)
    m_ng, m_fg, m_lb, m_gn = _unpack_small(sm, d, e, hd)
    v_ng, v_fg, v_lb, v_gn = _unpack_small(sv, d, e, hd)
    loss = sg[ROW_LOSS, 0]

    one = lambda a: a[None]
    return (loss, grad_x.reshape(1, s, d), g_ng, one(g_wi), g_lb, g_gn, one(g_wo), g_fg,
            d_ng, one(d_wi), d_lb, d_gn, one(d_wo), d_fg,
            m_ng, one(nm_wi), m_lb, m_gn, one(nm_wo), m_fg,
            v_ng, one(nv_wi), v_lb, v_gn, one(nv_wo), v_fg)
```

```python
import functools
import math

import jax
import jax.numpy as jnp
from jax import lax
from jax.experimental import pallas as pl
from jax.experimental.pallas import tpu as pltpu

NORM_EPS = 1e-6
HGRN_HEAD = 128
HGRN_CHUNK = 64
ATTN_HEAD = 64
ATTN_BAND = 128
DILATIONS = (1, 4, 16)
N_SPLITS = 8
N_DEV = 8
ADAM_LR = 0.001
ADAM_B1 = 0.9
ADAM_B2 = 0.999
ADAM_EPS = 1e-08
ADAM_WD = 0.01
ADAM_STEP = 10
LANES = 128
MESH = pl.DeviceIdType.MESH
F32 = jnp.float32
BF16 = jnp.bfloat16
NEG_BIG = -1e30
VMEM_LIMIT = 56 * 1024 * 1024
OUTPROJ_VMEM_LIMIT = 63 * 1024 * 1024
DHDX_VMEM_LIMIT = 60 * 1024 * 1024

ANY = pl.BlockSpec(memory_space=pl.ANY)


def _params(*sem):
    return pltpu.CompilerParams(dimension_semantics=sem, vmem_limit_bytes=VMEM_LIMIT)


def _tile(n, pref):
    t = min(n, pref)
    assert n % t == 0, (n, pref)
    return t


def _dot(a, b, precision=None):
    return jnp.dot(a, b, preferred_element_type=F32, precision=precision)


def _dot_nt(a, b):
    return lax.dot_general(a, b, (((1,), (1,)), ((), ())), preferred_element_type=F32)


def _dot_tn(a, b):
    return lax.dot_general(a, b, (((0,), (0,)), ((), ())), preferred_element_type=F32)


def _sigmoid(x):
    return 0.5 * jnp.tanh(0.5 * x) + 0.5


def _dsilu(x, s):
    return s * (1.0 + x * (1.0 - s))


def _adamw(w, g, m, v):
    m = ADAM_B1 * m + (1.0 - ADAM_B1) * g
    v = ADAM_B2 * v + (1.0 - ADAM_B2) * (g * g)
    m_hat = m / (1.0 - ADAM_B1 ** ADAM_STEP)
    v_hat = v / (1.0 - ADAM_B2 ** ADAM_STEP)
    delta = -ADAM_LR * (m_hat / (jnp.sqrt(v_hat) + ADAM_EPS) + ADAM_WD * w)
    return delta, m, v


def _cast_bf16(a):
    r, c = a.shape
    tr = _tile(r, 256)

    def body(a_ref, o_ref):
        o_ref[...] = a_ref[...].astype(BF16)

    return pl.pallas_call(
        body, name="cast_bf16", grid=(r // tr,), out_shape=jax.ShapeDtypeStruct((r, c), BF16),
        in_specs=[pl.BlockSpec((tr, c), lambda i: (i, 0))], out_specs=pl.BlockSpec((tr, c), lambda i: (i, 0)),
        compiler_params=_params("parallel"))(a)


def _rmsnorm_fwd(x, gain):
    s, d = x.shape
    tm = _tile(s, 512)

    def body(x_ref, g_ref, h_ref):
        xv = x_ref[...]
        r = lax.rsqrt(jnp.mean(xv * xv, axis=-1, keepdims=True) + NORM_EPS)
        h_ref[...] = (xv * r * g_ref[...]).astype(BF16)

    return pl.pallas_call(
        body, name="rmsnorm_fwd", grid=(s // tm,), out_shape=jax.ShapeDtypeStruct((s, d), BF16),
        in_specs=[pl.BlockSpec((tm, d), lambda i: (i, 0)), pl.BlockSpec((1, d), lambda i: (0, 0))],
        out_specs=pl.BlockSpec((tm, d), lambda i: (i, 0)), compiler_params=_params("parallel"))(x, gain)


HGRN_BLOCK = 2048
TRI_ROWS = 256


def _chunk_masks():
    tb = TRI_ROWS
    row = lax.broadcasted_iota(jnp.int32, (tb, tb), 0)
    col = lax.broadcasted_iota(jnp.int32, (tb, tb), 1)
    same = (row // HGRN_CHUNK) == (col // HGRN_CHUNK)
    lower = jnp.where(same & (col <= row), 1.0, 0.0).astype(BF16)
    upper = jnp.where(same & (col >= row), 1.0, 0.0).astype(BF16)
    return lower, upper


def _split3(a):
    hi = a.astype(BF16).astype(F32)
    mid = (a - hi).astype(BF16).astype(F32)
    lo = (a - hi - mid).astype(BF16).astype(F32)
    return hi, mid, lo


def _tri_dot(tri, x):
    hi, mid, lo = (p.astype(BF16) for p in _split3(x))
    outs = []
    for r in range(0, x.shape[0], TRI_ROWS):
        sl = slice(r, r + TRI_ROWS)
        outs.append(_dot(tri, hi[sl]) + _dot(tri, mid[sl]) + _dot(tri, lo[sl]))
    return outs[0] if len(outs) == 1 else jnp.concatenate(outs, axis=0)


def _hgrn_gates(qp, fp, lbv):
    lb = _sigmoid(lbv[0:1] - lbv[1:2])
    sq = _sigmoid(qp)
    q = qp * sq
    sg = _sigmoid(fp)
    f = lb + (1.0 - lb) * sg
    k = 1.0 - f
    return lb, sq, q, sg, f, k


def _hgrn_fwd(z, lb_logits, gnorm):
    s = z.shape[0]
    e = z.shape[1] // N_SPLITS
    nh = e // HGRN_HEAD
    tb = _tile(s, HGRN_BLOCK)
    nc = tb // HGRN_CHUNK
    nb = s // tb
    C = HGRN_CHUNK

    def body(q_ref, f_ref, i_ref, g_ref, lb_ref, gn_ref, y_ref, st_ref, state, o_scr):
        @pl.when(pl.program_id(1) == 0)
        def _():
            state[...] = jnp.zeros_like(state)

        lb, sq, q, sg, f, k = _hgrn_gates(q_ref[...], f_ref[...], lb_ref[...])
        lower, _ = _chunk_masks()
        b = _tri_dot(lower, jnp.log(f))
        b3 = b.reshape(nc, C, HGRN_HEAD)
        bc = b3[:, C - 1:C, :]
        qt = (q * jnp.exp(b)).astype(BF16)
        kt = (k * jnp.exp(-b)).astype(BF16)
        ke = (k.reshape(nc, C, HGRN_HEAD) * jnp.exp(bc - b3)).reshape(tb, HGRN_HEAD).astype(BF16)
        v = i_ref[...].astype(BF16)
        tri = lax.broadcasted_iota(jnp.int32, (C, C), 1) <= lax.broadcasted_iota(jnp.int32, (C, C), 0)
        sls = [slice(c * C, (c + 1) * C) for c in range(nc)]
        kv = [_dot_tn(v[sl], ke[sl]) for sl in sls]
        a = [jnp.where(tri, _dot_nt(qt[sl], kt[sl]), 0.0).astype(BF16) for sl in sls]
        st = state[...]
        sts = []
        for c in range(nc):
            sts.append(st)
            st_ref[c] = st
            st = st * jnp.exp(bc[c]) + kv[c]
        state[...] = st
        for c, sl in enumerate(sls):
            o_scr[sl, :] = _dot(a[c], v[sl]) + _dot_nt(qt[sl], sts[c].astype(BF16))
        o = o_scr[...]
        rms = lax.rsqrt(jnp.mean(o * o, axis=-1, keepdims=True) + NORM_EPS)
        gp = g_ref[...]
        y_ref[...] = (o * rms * gn_ref[...] * (gp * _sigmoid(gp))).astype(BF16)

    col = lambda kk: (lambda h, n: (n, kk * nh + h))
    return pl.pallas_call(
        body, name="hgrn_fwd", grid=(nh, nb),
        out_shape=(jax.ShapeDtypeStruct((s, e), BF16),
                   jax.ShapeDtypeStruct((nh, s // C, HGRN_HEAD, HGRN_HEAD), F32)),
        in_specs=[pl.BlockSpec((tb, HGRN_HEAD), col(0)), pl.BlockSpec((tb, HGRN_HEAD), col(1)),
                  pl.BlockSpec((tb, HGRN_HEAD), col(2)), pl.BlockSpec((tb, HGRN_HEAD), col(3)),
                  pl.BlockSpec((2, HGRN_HEAD), lambda h, n: (0, h)), pl.BlockSpec((1, HGRN_HEAD), lambda h, n: (0, 0))],
        out_specs=(pl.BlockSpec((tb, HGRN_HEAD), lambda h, n: (n, h)),
                   pl.BlockSpec((None, nc, HGRN_HEAD, HGRN_HEAD), lambda h, n: (h, n, 0, 0))),
        scratch_shapes=[pltpu.VMEM((HGRN_HEAD, HGRN_HEAD), F32), pltpu.VMEM((tb, HGRN_HEAD), F32)],
        compiler_params=_params("parallel", "arbitrary"))(z, z, z, z, lb_logits, gnorm)


def _hgrn_bwd(z, dy, states, lb_logits, gnorm):
    s = z.shape[0]
    e = z.shape[1] // N_SPLITS
    nh = e // HGRN_HEAD
    tb = _tile(s, HGRN_BLOCK)
    nc = tb // HGRN_CHUNK
    nb = s // tb
    C = HGRN_CHUNK
    H = HGRN_HEAD

    def body(q_ref, f_ref, i_ref, g_ref, dy_ref, st_ref, lb_ref, gn_ref, dz_ref, dlb_ref, dgn_ref,
             gstate, o_scr, dq_scr, dk_scr, dv_scr, e_scr):
        first = (pl.program_id(0) == 0) & (pl.program_id(1) == 0)

        @pl.when(first)
        def _():
            dgn_ref[...] = jnp.zeros_like(dgn_ref)

        @pl.when(pl.program_id(1) == 0)
        def _():
            gstate[...] = jnp.zeros_like(gstate)
            dlb_ref[...] = jnp.zeros_like(dlb_ref)

        qp = q_ref[...]
        lb, sq, q, sg, f, k = _hgrn_gates(qp, f_ref[...], lb_ref[...])
        lower, upper = _chunk_masks()
        b = _tri_dot(lower, jnp.log(f))
        b3 = b.reshape(nc, C, H)
        bc = b3[:, C - 1:C, :]
        eb = jnp.exp(b)
        enb = jnp.exp(-b)
        eend = jnp.exp(bc - b3).reshape(tb, H)
        qt = (q * eb).astype(BF16)
        kt = (k * enb).astype(BF16)
        ke = (k * eend).astype(BF16)
        v = i_ref[...].astype(BF16)
        tri = lax.broadcasted_iota(jnp.int32, (C, C), 1) <= lax.broadcasted_iota(jnp.int32, (C, C), 0)
        sls = [slice(c * C, (c + 1) * C) for c in range(nc)]
        a = [jnp.where(tri, _dot_nt(qt[sl], kt[sl]), 0.0).astype(BF16) for sl in sls]
        for c, sl in enumerate(sls):
            o_scr[sl, :] = _dot(a[c], v[sl]) + _dot_nt(qt[sl], st_ref[c].astype(BF16))
        o = o_scr[...]
        rms = lax.rsqrt(jnp.mean(o * o, axis=-1, keepdims=True) + NORM_EPS)
        on = o * rms
        gn = gn_ref[...]
        gp = g_ref[...]
        sgg = _sigmoid(gp)
        dyv = dy_ref[...]
        d_on = dyv * (gp * sgg)
        dz_ref[3] = (dyv * on * gn * _dsilu(gp, sgg)).astype(BF16)
        dgn_ref[...] += jnp.sum(d_on * on, axis=0, keepdims=True)
        u = d_on * gn
        do = (rms * (u - on * jnp.mean(u * on, axis=-1, keepdims=True))).astype(BF16)
        gup = [_dot_tn(do[sl], qt[sl]) for sl in sls]
        da = [jnp.where(tri, _dot_nt(do[sl], v[sl]), 0.0).astype(BF16) for sl in sls]
        gt = gstate[...]
        gts = [None] * nc
        for c in reversed(range(nc)):
            gts[c] = gt
            gt = gt * jnp.exp(bc[c]) + gup[c]
        gstate[...] = gt
        for c, sl in enumerate(sls):
            stp = st_ref[c]
            gtb = gts[c].astype(BF16)
            dqt = _dot(da[c], kt[sl]) + _dot(do[sl], stp.astype(BF16))
            dkt = _dot_tn(da[c], qt[sl])
            dks = _dot(v[sl], gtb) * eend[sl]
            dv_scr[sl, :] = _dot_tn(a[c], do[sl]) + _dot_nt(ke[sl], gtb)
            dq_scr[sl, :] = dqt * eb[sl]
            dk_scr[sl, :] = dkt * enb[sl] + dks
            ech = (jnp.sum(k[sl] * dks, axis=0, keepdims=True)
                   + jnp.sum(gts[c] * jnp.exp(bc[c]) * stp, axis=0, keepdims=True))
            e_scr[sl, :] = jnp.broadcast_to(ech, (C, H))
        dq = dq_scr[...]
        dk = dk_scr[...]
        dlf = _tri_dot(upper, q * dq - k * dk) + e_scr[...]
        dft = dlf / f - dk
        dz_ref[0] = (dq * _dsilu(qp, sq)).astype(BF16)
        dz_ref[1] = (dft * (1.0 - lb) * sg * (1.0 - sg)).astype(BF16)
        dz_ref[2] = dv_scr[...].astype(BF16)
        dlb_ref[...] += jnp.sum(dft * (1.0 - sg), axis=0, keepdims=True)

    col = lambda kk: (lambda h, n: (nb - 1 - n, kk * nh + h))
    return pl.pallas_call(
        body, name="hgrn_bwd", grid=(nh, nb),
        out_shape=(jax.ShapeDtypeStruct((4, s, e), BF16), jax.ShapeDtypeStruct((1, e), F32),
                   jax.ShapeDtypeStruct((1, H), F32)),
        in_specs=[pl.BlockSpec((tb, H), col(0)), pl.BlockSpec((tb, H), col(1)),
                  pl.BlockSpec((tb, H), col(2)), pl.BlockSpec((tb, H), col(3)),
                  pl.BlockSpec((tb, H), lambda h, n: (nb - 1 - n, h)),
                  pl.BlockSpec((None, nc, H, H), lambda h, n: (h, nb - 1 - n, 0, 0)),
                  pl.BlockSpec((2, H), lambda h, n: (0, h)), pl.BlockSpec((1, H), lambda h, n: (0, 0))],
        out_specs=(pl.BlockSpec((4, tb, H), lambda h, n: (0, nb - 1 - n, h)),
                   pl.BlockSpec((1, H), lambda h, n: (0, h)), pl.BlockSpec((1, H), lambda h, n: (0, 0))),
        scratch_shapes=[pltpu.VMEM((H, H), F32)] + [pltpu.VMEM((tb, H), F32)] * 5,
        compiler_params=_params("arbitrary", "arbitrary"))(z, z, z, z, dy, states, lb_logits, gnorm)


ATTN_T = 16 * ATTN_BAND
SCALE = ATTN_HEAD ** -0.5
TILE_UNROLL = 2


def _slope(pair, hh, nheads):
    head = (2 * pair + hh + 1).astype(F32)
    return jnp.exp(jnp.full((1, 1), -8.0 / nheads * math.log(2.0), F32) * head)


def _fill_bias(bias, pair, nheads, delta, edge_ok):
    band = (delta >= 0) & (delta <= ATTN_BAND)
    dist = delta.astype(F32)
    for pi, dil in enumerate(DILATIONS):
        for hh in range(2):
            full = jnp.where(band, -(_slope(pair, hh, nheads) * float(dil)) * dist, NEG_BIG)
            bias[(pi * 2 + hh) * 2] = full
            bias[(pi * 2 + hh) * 2 + 1] = jnp.where(edge_ok, full, NEG_BIG)


def _rows(start, size, stride):
    if stride == 1:
        return pl.ds(pl.multiple_of(start, ATTN_BAND), size)
    return pl.ds(start, size, stride=stride)


def _head_lanes(rows, hh):
    return (lax.broadcasted_iota(jnp.int32, (rows, LANES), 1) // ATTN_HEAD) == hh


def _attn_fwd(z):
    s = z.shape[0]
    e = z.shape[1] // N_SPLITS
    npair = e // LANES
    T = ATTN_T
    assert s % T == 0
    nsb = s // T
    W = ATTN_BAND
    nt = T // W
    HD = ATTN_HEAD
    chunk = 256

    nsteps = npair * nsb
    assert nt % 2 == 0 and T // chunk == nt // 2

    def body(q_ref, kp_ref, kc_ref, vp_ref, vc_ref, g_ref, o_ref, l_ref, y_ref, qa, kbuf, va, bias, *sets):
        t = pl.program_id(0)
        sb = jnp.minimum(t, nsteps - 1) % nsb
        pair = jnp.minimum(t, nsteps - 1) // nsb
        sets = (sets[0:3], sets[3:6])

        def merge_chunk(i, done):
            accs, ms, lsw = done
            rows = pl.ds(pl.multiple_of(i * chunk, chunk), chunk)
            m1, m2, m3 = ms[0, rows, :], ms[1, rows, :], ms[2, rows, :]
            mx = jnp.maximum(jnp.maximum(m1, m2), m3)
            w1, w2, w3 = jnp.exp(m1 - mx), jnp.exp(m2 - mx), jnp.exp(m3 - mx)
            unswap = lambda a: pltpu.roll(a, ATTN_HEAD, 1)
            den = w1 * unswap(lsw[0, rows, :]) + w2 * unswap(lsw[1, rows, :]) + w3 * unswap(lsw[2, rows, :])
            o = (w1 * accs[0, rows, :] + w2 * accs[1, rows, :] + w3 * accs[2, rows, :]) / den
            o_ref[rows, :] = o
            l_ref[rows, :] = mx + jnp.log(den)
            gp = g_ref[rows, :]
            y_ref[rows, :] = (o * (gp * _sigmoid(gp))).astype(BF16)

        @pl.when(t == 0)
        def _():
            accs, ms, lsw = sets[1]
            accs[...] = jnp.zeros_like(accs)
            ms[...] = jnp.zeros_like(ms)
            lsw[...] = jnp.ones_like(lsw)

        @pl.when(t == nsteps)
        def _():
            def drain(i, carry):
                merge_chunk(i, sets[(nsteps - 1) % 2])
                return carry

            lax.fori_loop(0, T // chunk, drain, 0)

        def compute(cur, done):
            accs, ms, lsw = cur
            def stage(i, carry):
                rows = pl.ds(pl.multiple_of(i * chunk, chunk), chunk)
                upper = pl.ds(pl.multiple_of(T + i * chunk, chunk), chunk)
                kbuf[upper, :] = kc_ref[rows, :]
                for hh in range(2):
                    mine = _head_lanes(chunk, hh)
                    qa[hh, rows, :] = jnp.where(mine, q_ref[rows, :] * SCALE, 0.0)
                    va[hh, upper, :] = jnp.where(mine, vc_ref[rows, :], 1.0)
                return carry

            lax.fori_loop(0, T // chunk, stage, 0)

            @pl.when(sb == 0)
            def _():
                def stage_prev(i, carry):
                    rows = pl.ds(pl.multiple_of(i * chunk, chunk), chunk)
                    kbuf[rows, :] = kp_ref[rows, :]
                    for hh in range(2):
                        va[hh, rows, :] = jnp.where(_head_lanes(chunk, hh), vp_ref[rows, :], 1.0)
                    return carry

                lax.fori_loop(0, T // chunk, stage_prev, 0)

            qi = lax.broadcasted_iota(jnp.int32, (W, 2 * W), 0)
            kj = lax.broadcasted_iota(jnp.int32, (W, 2 * W), 1)
            _fill_bias(bias, pair, 2 * npair, W + qi - kj, kj >= W)

            def tile(tau):
                first = _head_lanes(W, 0)
                rows, scores = [], []
                for pi, dil in enumerate(DILATIONS):
                    r = tau % dil
                    ub = tau // dil
                    qrows = _rows(r + dil * W * ub, W, dil)
                    krows = _rows(T + dil * W * (ub - 1) + r, 2 * W, dil)
                    var = jnp.where((sb == 0) & (ub == 0), 1, 0)
                    kt = kbuf[krows, :].astype(BF16)
                    rows.append((qrows, krows))
                    scores.append([_dot_nt(qa[hh, qrows, :].astype(BF16), kt) + bias[(pi * 2 + hh) * 2 + var]
                                   for hh in range(2)])
                maxes = [[jnp.max(sc, axis=-1, keepdims=True) for sc in pair_] for pair_ in scores]
                probs = [[jnp.exp(sc - m).astype(BF16) for sc, m in zip(ps, pm)] for ps, pm in zip(scores, maxes)]
                for pi, (qrows, krows) in enumerate(rows):
                    outs = [_dot(probs[pi][hh], va[hh, krows, :].astype(BF16)) for hh in range(2)]
                    accs[pi, qrows, :] = jnp.where(first, outs[0], outs[1])
                    lsw[pi, qrows, :] = jnp.where(first, outs[1], outs[0])
                    ms[pi, qrows, :] = jnp.where(first, maxes[pi][0], maxes[pi][1])

            def two_tiles(i, carry):
                tile(2 * i)
                tile(2 * i + 1)
                merge_chunk(i, done)
                return carry

            lax.fori_loop(0, nt // 2, two_tiles, 0)

            def move_down(i, carry):
                rows = pl.ds(pl.multiple_of(i * chunk, chunk), chunk)
                upper = pl.ds(pl.multiple_of(T + i * chunk, chunk), chunk)
                kbuf[rows, :] = kbuf[upper, :]
                for hh in range(2):
                    va[hh, rows, :] = va[hh, upper, :]
                return carry

            lax.fori_loop(0, T // chunk, move_down, 0)

        for parity in range(2):
            @pl.when((t < nsteps) & (t % 2 == parity))
            def _():
                compute(sets[parity], sets[1 - parity])

    step_of = lambda t: jnp.minimum(t, nsteps - 1)
    lag_of = lambda t: jnp.maximum(t - 1, 0)
    cur = lambda split: (lambda t: (step_of(t) % nsb, split * npair + step_of(t) // nsb))
    prev = lambda split: (lambda t: (0, split * npair + step_of(t) // nsb))
    blk = lambda index: pl.BlockSpec((T, LANES), index)
    out = blk(lambda t: (lag_of(t) % nsb, lag_of(t) // nsb))
    gate = blk(lambda t: (lag_of(t) % nsb, 7 * npair + lag_of(t) // nsb))
    buf = lambda rows: pltpu.VMEM((rows, LANES), F32)
    return pl.pallas_call(
        body, name="attn_fwd", grid=(nsteps + 1,),
        out_shape=(jax.ShapeDtypeStruct((s, e), F32), jax.ShapeDtypeStruct((s, e), F32), jax.ShapeDtypeStruct((s, e), BF16)),
        in_specs=[blk(cur(4)), blk(prev(5)), blk(cur(5)), blk(prev(6)), blk(cur(6)), gate],
        out_specs=(out, out, out),
        scratch_shapes=[pltpu.VMEM((2, T, LANES), F32), buf(2 * T), pltpu.VMEM((2, 2 * T, LANES), F32),
                        pltpu.VMEM((12, W, 2 * W), F32)] + [pltpu.VMEM((3, T, LANES), F32)] * 6,
        compiler_params=_params("arbitrary"))(z, z, z, z, z, z)


def _outproj_loss(x, y_h, y_a, w_out_full, final_gain, target):
    s, d = x.shape
    e = y_h.shape[1]
    tm = _tile(s, 512)
    sub = _tile(tm, 256)

    def body(x_ref, yh_ref, ya_ref, w_ref, g_ref, t_ref, dx_ref, dxb_ref, dy_ref, loss_ref, dg_ref):
        @pl.when(pl.program_id(0) == 0)
        def _():
            loss_ref[...] = jnp.zeros_like(loss_ref)
            dg_ref[...] = jnp.zeros_like(dg_ref)

        w = w_ref[...]
        g = g_ref[...]
        parts = [slice(r0, r0 + sub) for r0 in range(0, tm, sub)]
        x2s = [x_ref[rows, :] + _dot(yh_ref[rows, :], w[0:e]) + _dot(ya_ref[rows, :], w[e:2 * e]) for rows in parts]
        for rows, x2 in zip(parts, x2s):
            r = lax.rsqrt(jnp.mean(x2 * x2, axis=-1, keepdims=True) + NORM_EPS)
            xn = x2 * r
            err = xn * g - t_ref[rows, :]
            loss_ref[...] += jnp.sum(err * err, axis=0, keepdims=True) * (0.5 / d)
            dyo = err * (1.0 / d)
            dg_ref[...] += jnp.sum(dyo * xn, axis=0, keepdims=True)
            u = dyo * g
            dx2 = r * (u - xn * jnp.mean(u * xn, axis=-1, keepdims=True))
            dx_ref[rows, :] = dx2
            dxb = dx2.astype(BF16)
            dxb_ref[rows, :] = dxb
            dy_ref[rows, :] = _dot_nt(dxb, w)

    row = pl.BlockSpec((tm, d), lambda i: (i, 0))
    half = pl.BlockSpec((tm, e), lambda i: (i, 0))
    vec = pl.BlockSpec((1, d), lambda i: (0, 0))
    whole = pl.BlockSpec((2 * e, d), lambda i: (0, 0), pipeline_mode=pl.Buffered(1))
    return pl.pallas_call(
        body, name="outproj_loss", grid=(s // tm,),
        out_shape=(jax.ShapeDtypeStruct((s, d), F32), jax.ShapeDtypeStruct((s, d), BF16),
                   jax.ShapeDtypeStruct((s, 2 * e), F32), jax.ShapeDtypeStruct((1, d), F32),
                   jax.ShapeDtypeStruct((1, d), F32)),
        in_specs=[row, half, half, whole, vec, row],
        out_specs=(row, row, pl.BlockSpec((tm, 2 * e), lambda i: (i, 0)), vec, vec),
        compiler_params=pltpu.CompilerParams(dimension_semantics=("arbitrary",), vmem_limit_bytes=OUTPROJ_VMEM_LIMIT),
    )(x, y_h, y_a, w_out_full, final_gain, target)


def _attn_bwd(z, dy, o, lse):
    s, e = o.shape
    npair = e // LANES
    T = ATTN_T
    assert s % T == 0
    nsb = s // T
    W = ATTN_BAND
    nt = T // W
    HD = ATTN_HEAD
    chunk = 256

    def body(k_ref, v_ref, qc_ref, qn_ref, dyc_ref, dyn_ref, gc_ref, gn_ref, oc_ref, on_ref, lc_ref, ln_ref,
             dz_ref, qa, doa, ka, va, dqacc, dkacc, dvacc, bias):
        sb = pl.program_id(1)
        def stage_queries(half, q_r, dy_r, g_r, o_r, l_r):
            def stage(i, carry):
                rows = pl.ds(pl.multiple_of(i * chunk, chunk), chunk)
                dst = pl.ds(pl.multiple_of(half * T + i * chunk, chunk), chunk)
                lane = lax.broadcasted_iota(jnp.int32, (chunk, LANES), 1)
                gp = g_r[rows, :]
                dov = dy_r[rows, :] * (gp * _sigmoid(gp))
                qv = q_r[rows, :] * SCALE
                same_head = (lax.broadcasted_iota(jnp.int32, (LANES, LANES), 0) // HD
                             == lax.broadcasted_iota(jnp.int32, (LANES, LANES), 1) // HD)
                ones = jnp.where(same_head, 1.0, 0.0).astype(BF16)
                hi, mid, lo = (p.astype(BF16) for p in _split3(dov * o_r[rows, :]))
                delta = _dot(hi, ones) + _dot(mid, ones) + _dot(lo, ones)
                swap = lambda a: pltpu.roll(a, HD, 1)
                lse_parts = [swap(p) for p in _split3(l_r[rows, :])]
                dl_parts = [swap(p) for p in _split3(delta)]
                for hh in range(2):
                    mine = _head_lanes(chunk, hh)
                    spare = (1 - hh) * HD
                    qh = jnp.where(mine, qv, 0.0)
                    dh = jnp.where(mine, dov, 0.0)
                    for j in range(3):
                        qh = jnp.where(lane == spare + j, lse_parts[j], qh)
                        dh = jnp.where(lane == spare + j, dl_parts[j], dh)
                    qa[hh, dst, :] = qh
                    doa[hh, dst, :] = dh
                return carry

            lax.fori_loop(0, T // chunk, stage, 0)

        @pl.when(sb == 0)
        def _():
            stage_queries(0, qc_ref, dyc_ref, gc_ref, oc_ref, lc_ref)

        stage_queries(1, qn_ref, dyn_ref, gn_ref, on_ref, ln_ref)

        def stage_keys(i, carry):
            rows = pl.ds(pl.multiple_of(i * chunk, chunk), chunk)
            lane = lax.broadcasted_iota(jnp.int32, (chunk, LANES), 1)
            for hh in range(2):
                spare = (1 - hh) * HD
                minus = (lane >= spare) & (lane < spare + 3)
                ka[hh, rows, :] = jnp.where(minus, -1.0, k_ref[rows, :])
                va[hh, rows, :] = jnp.where(minus, -1.0, v_ref[rows, :])
            gp = gc_ref[rows, :]
            dz_ref[3, rows, :] = (dyc_ref[rows, :] * oc_ref[rows, :] * _dsilu(gp, _sigmoid(gp))).astype(BF16)
            return carry

        lax.fori_loop(0, T // chunk, stage_keys, 0)

        @pl.when(sb == 0)
        def _():
            dqacc[0:T, :] = jnp.zeros((T, LANES), F32)

        dqacc[T:, :] = jnp.zeros((T, LANES), F32)
        dkacc[...] = jnp.zeros_like(dkacc)
        dvacc[...] = jnp.zeros_like(dvacc)
        qi = lax.broadcasted_iota(jnp.int32, (2 * W, W), 0)
        kj = lax.broadcasted_iota(jnp.int32, (2 * W, W), 1)
        _fill_bias(bias, pl.program_id(0), 2 * npair, qi - kj, qi < W)

        def tile(tau, carry):
            def scores(step, pi):
                dil = DILATIONS[pi]
                r = step % dil
                ub = step // dil
                start = r + dil * W * ub
                krows = _rows(start, W, dil)
                qrows = _rows(start, 2 * W, dil)
                var = jnp.where((sb == nsb - 1) & (ub == nt // dil - 1), 1, 0)
                unit = dict(krows=krows, qrows=qrows, ops=[], sc=[], dpd=[])
                for hh in range(2):
                    kt = ka[hh, krows, :].astype(BF16)
                    vt = va[hh, krows, :].astype(BF16)
                    qt = qa[hh, qrows, :].astype(BF16)
                    dt = doa[hh, qrows, :].astype(BF16)
                    unit["ops"].append((kt, qt, dt))
                    unit["sc"].append(_dot_nt(qt, kt) + bias[(pi * 2 + hh) * 2 + var])
                    unit["dpd"].append(_dot_nt(dt, vt))
                return unit

            def elementwise(unit):
                ps = [jnp.exp(s_) for s_ in unit["sc"]]
                unit["ds"] = [(p * d).astype(BF16) for p, d in zip(ps, unit["dpd"])]
                unit["pb"] = [p.astype(BF16) for p in ps]

            def products(unit):
                dvs = [_dot_tn(pb, dt) for pb, (kt, qt, dt) in zip(unit["pb"], unit["ops"])]
                dks = [_dot_tn(ds, qt) for ds, (kt, qt, dt) in zip(unit["ds"], unit["ops"])]
                dqs = [_dot(ds, kt) for ds, (kt, qt, dt) in zip(unit["ds"], unit["ops"])]
                dkacc[unit["krows"], :] += jnp.where(_head_lanes(W, 0), dks[0], dks[1])
                dvacc[unit["krows"], :] += jnp.where(_head_lanes(W, 0), dvs[0], dvs[1])
                dqacc[unit["qrows"], :] += jnp.where(_head_lanes(2 * W, 0), dqs[0], dqs[1]) * SCALE

            order = [(2 * tau + half, pi) for half in range(2) for pi in range(len(DILATIONS))]
            units = [None] * len(order)
            for n in range(len(order) + 2):
                if n < len(order):
                    units[n] = scores(*order[n])
                if 1 <= n <= len(order):
                    elementwise(units[n - 1])
                if n >= 2:
                    products(units[n - 2])
            return carry

        lax.fori_loop(0, nt // 2, tile, 0)

        def flush(i, carry):
            rows = pl.ds(pl.multiple_of(i * chunk, chunk), chunk)
            nxt = pl.ds(pl.multiple_of(T + i * chunk, chunk), chunk)
            dz_ref[0, rows, :] = dqacc[rows, :].astype(BF16)
            dz_ref[1, rows, :] = dkacc[rows, :].astype(BF16)
            dz_ref[2, rows, :] = dvacc[rows, :].astype(BF16)
            dqacc[rows, :] = dqacc[nxt, :]
            for hh in range(2):
                qa[hh, rows, :] = qa[hh, nxt, :]
                doa[hh, rows, :] = doa[hh, nxt, :]
            return carry

        lax.fori_loop(0, T // chunk, flush, 0)

    zc = lambda split: (lambda hp, sb: (sb, split * npair + hp))
    zn = lambda split: (lambda hp, sb: (jnp.minimum(sb + 1, nsb - 1), split * npair + hp))
    ec = lambda off: (lambda hp, sb: (sb, off + hp))
    en = lambda off: (lambda hp, sb: (jnp.minimum(sb + 1, nsb - 1), off + hp))
    z0 = lambda split: (lambda hp, sb: (0, split * npair + hp))
    e0 = lambda off: (lambda hp, sb: (0, off + hp))
    blk = lambda index: pl.BlockSpec((T, LANES), index)
    buf = lambda rows: pltpu.VMEM((rows, LANES), F32)
    return pl.pallas_call(
        body, name="attn_bwd", grid=(npair, nsb), out_shape=jax.ShapeDtypeStruct((4, s, e), BF16),
        in_specs=[blk(zc(5)), blk(zc(6)), blk(z0(4)), blk(zn(4)), blk(ec(npair)), blk(en(npair)),
                  blk(zc(7)), blk(zn(7)), blk(ec(0)), blk(en(0)), blk(e0(0)), blk(en(0))],
        out_specs=pl.BlockSpec((4, T, LANES), lambda hp, sb: (0, sb, hp)),
        scratch_shapes=[pltpu.VMEM((2, 2 * T, LANES), F32), pltpu.VMEM((2, 2 * T, LANES), F32),
                        pltpu.VMEM((2, T, LANES), F32), pltpu.VMEM((2, T, LANES), F32),
                        buf(2 * T), buf(T), buf(T), pltpu.VMEM((12, 2 * W, W), F32)],
        compiler_params=_params("parallel", "arbitrary"))(z, z, z, z, dy, dy, z, z, o, o, lse, lse)


def _dz_specs(tm, e):
    def mk(lo, hi):
        return pl.BlockSpec((None, tm, e), lambda i, k: (jnp.clip(k - lo, 0, hi - lo - 1), i, 0))
    return [mk(0, 4), mk(4, 8)]


def _dz_pick(grp, dzh_ref, dza_ref, fn):
    @pl.when(grp < 4)
    def _():
        fn(dzh_ref[...])

    @pl.when(grp >= 4)
    def _():
        fn(dza_ref[...])


def _dh_dx(dzh, dza, w_full, x, gain, dx2):
    s, d = x.shape
    e = dzh.shape[2]
    tm = _tile(s, 1024)
    ni = s // tm
    chunk = _tile(tm, 128)
    fetch_at = 2

    def body(dzh_ref, dza_ref, w_ref, x_hbm, g_ref, dx2_hbm, gx_hbm, dg_ref, acc, xbuf, dbuf, sems):
        i, k = pl.program_id(0), pl.program_id(1)
        rows_of = lambda tile: pl.ds(pl.multiple_of(tile * tm, tm), tm)
        fetch_x = pltpu.make_async_copy(x_hbm.at[rows_of(i), :], xbuf, sems.at[0])
        fetch_d = pltpu.make_async_copy(dx2_hbm.at[rows_of(i), :], dbuf, sems.at[1])

        def store(tile):
            return pltpu.make_async_copy(xbuf, gx_hbm.at[rows_of(tile), :], sems.at[2])

        @pl.when((i == 0) & (k == 0))
        def _():
            dg_ref[...] = jnp.zeros_like(dg_ref)

        @pl.when((k == fetch_at) & (i > 0))
        def _():
            store(i).wait()

        @pl.when(k == fetch_at)
        def _():
            fetch_x.start()
            fetch_d.start()

        def finish(tile):
            fetch_x.wait()
            fetch_d.wait()
            gain_row = g_ref[...]
            for r0 in range(0, tm, chunk):
                rows = slice(r0, r0 + chunk)
                dh = acc[rows, :]
                xv = xbuf[rows, :]
                r = lax.rsqrt(jnp.mean(xv * xv, axis=-1, keepdims=True) + NORM_EPS)
                xn = xv * r
                u = dh * gain_row
                xbuf[rows, :] = dbuf[rows, :] + r * (u - xn * jnp.mean(u * xn, axis=-1, keepdims=True))
                dg_ref[...] += jnp.sum(dh * xn, axis=0, keepdims=True)
            store(tile).start()

        @pl.when((k == 0) & (i == 0))
        def _():
            acc[...] = _dot_nt(dzh_ref[...], w_ref[...])

        @pl.when((k == 0) & (i > 0))
        def _():
            finish(i - 1)
            acc[...] = _dot_nt(dzh_ref[...], w_ref[...])

        @pl.when(k > 0)
        def _():
            def add(dz):
                acc[...] += _dot_nt(dz, w_ref[...])

            _dz_pick(k, dzh_ref, dza_ref, add)

        @pl.when((k == N_SPLITS - 1) & (i == ni - 1))
        def _():
            finish(i)
            store(i).wait()

    vec = pl.BlockSpec((1, d), lambda i, k: (0, 0))
    return pl.pallas_call(
        body, name="dh_dx", grid=(ni, N_SPLITS),
        out_shape=(jax.ShapeDtypeStruct((s, d), F32), jax.ShapeDtypeStruct((1, d), F32)),
        in_specs=_dz_specs(tm, e) + [pl.BlockSpec((None, d, e), lambda i, k: (k, 0, 0)), ANY, vec, ANY],
        out_specs=(ANY, vec),
        scratch_shapes=[pltpu.VMEM((tm, d), F32)] * 3 + [pltpu.SemaphoreType.DMA((3,))],
        compiler_params=pltpu.CompilerParams(dimension_semantics=("arbitrary", "arbitrary"),
                                             vmem_limit_bytes=DHDX_VMEM_LIMIT))(dzh, dza, w_full, x, gain, dx2)


def _position():
    x, y, c = lax.axis_index("x"), lax.axis_index("y"), lax.axis_index("c")
    return x, y, c


def _xor_peer(x, y, c, mask):
    return (x ^ ((mask >> 2) & 1), y ^ ((mask >> 1) & 1), c ^ (mask & 1))


def _block_order(masks):
    me = 4 * lax.axis_index("x") + 2 * lax.axis_index("y") + lax.axis_index("c")
    return jnp.stack([me ^ m for m in masks]).astype(jnp.int32)


GATHER_MASKS = (0, 1, 4, 5, 2, 3, 6, 7)


def _inproj_gather(h, w_loc, wo_loc):
    s, d = h.shape
    e = w_loc.shape[1]
    tm = _tile(s, 1024)
    ni = s // tm
    pre = max(ni - 2, 0)

    def body(order_ref, h_ref, w_ref, wo_ref, z_ref, wf_ref, wof_ref, wbuf, send_sems, recv_sems, osend, orecv,
             local_sems, wsems):
        j, i = pl.program_id(0), pl.program_id(1)
        x, y, c = _position()
        me, sibling = (x, y, c), (x, y, 1 - c)
        chips = [(1 - x, y), (x, 1 - y), (1 - x, 1 - y)]
        blk = lambda p: 4 * p[0] + 2 * p[1] + p[2]

        def copy(k, block, to, src=None):
            dst = wf_ref.at[blk(block)]
            return pltpu.make_async_remote_copy(
                src_ref=dst if src is None else src, dst_ref=dst, send_sem=send_sems.at[k], recv_sem=recv_sems.at[k],
                device_id=to, device_id_type=MESH)

        first = [copy(0, me, sibling, src=w_ref)] + [copy(1 + q, me, (*chip, c), src=w_ref) for q, chip in enumerate(chips)]
        passed = [copy(4 + q, (*chip, c), sibling) for q, chip in enumerate(chips)]
        mine = pltpu.make_async_copy(w_ref, wf_ref.at[blk(me)], local_sems.at[0])
        ocopies = [pltpu.make_async_remote_copy(
            src_ref=wo_ref, dst_ref=wof_ref.at[blk(me)], send_sem=osend.at[k], recv_sem=orecv.at[k],
            device_id=_xor_peer(x, y, c, k + 1), device_id_type=MESH) for k in range(N_DEV - 1)]
        omine = pltpu.make_async_copy(wo_ref, wof_ref.at[blk(me)], local_sems.at[1])
        blocks = [me, sibling] + [(*chip, c) for chip in chips] + [(*chip, 1 - c) for chip in chips]
        arrive = [None, copy(0, sibling, me)] + [copy(1 + q, (*chip, c), me) for q, chip in enumerate(chips)] \
            + [copy(4 + q, (*chip, 1 - c), me) for q, chip in enumerate(chips)]
        forward = [None, None] + passed + [None, None, None]
        use_order = (0, 1, 2, 5, 3, 6, 4, 7)
        blocks, arrive, forward = ([lst[n] for n in use_order] for lst in (blocks, arrive, forward))

        def load(slot, src):
            return pltpu.make_async_copy(src, wbuf.at[slot], wsems.at[slot])

        @pl.when((j == 0) & (i == 0))
        def _():
            for cp in [mine, omine] + first + ocopies:
                cp.start()
            load(0, w_ref).start()

        for jj in range(N_DEV):
            @pl.when((j == jj) & (i == 0))
            def _():
                load(jj % 2, w_ref).wait()

            if jj + 1 < N_DEV:
                @pl.when((j == jj) & (i == pre))
                def _():
                    arrive[jj + 1].wait_recv()
                    if forward[jj + 1] is not None:
                        forward[jj + 1].start()
                    load((jj + 1) % 2, wf_ref.at[blk(blocks[jj + 1])]).start()

        z_ref[...] = _dot(h_ref[...], wbuf[j % 2])

        @pl.when((j == N_DEV - 1) & (i == ni - 1))
        def _():
            for cp in first + passed:
                cp.wait_send()
            for cp in ocopies:
                cp.wait_send()
                cp.wait_recv()
            mine.wait()
            omine.wait()

    grid_spec = pltpu.PrefetchScalarGridSpec(
        num_scalar_prefetch=1, grid=(N_DEV, ni),
        in_specs=[pl.BlockSpec((tm, d), lambda j, i, o: (i, 0)), ANY, ANY],
        out_specs=(pl.BlockSpec((tm, e), lambda j, i, o: (i, o[j])), ANY, ANY),
        scratch_shapes=[pltpu.VMEM((2, d, e), BF16), pltpu.SemaphoreType.DMA((7,)), pltpu.SemaphoreType.DMA((7,)),
                        pltpu.SemaphoreType.DMA((7,)), pltpu.SemaphoreType.DMA((7,)), pltpu.SemaphoreType.DMA((2,)),
                        pltpu.SemaphoreType.DMA((2,))])
    return pl.pallas_call(
        body, name="inproj_gather", grid_spec=grid_spec,
        out_shape=(jax.ShapeDtypeStruct((s, N_SPLITS * e), F32), jax.ShapeDtypeStruct((N_DEV, d, e), BF16),
                   jax.ShapeDtypeStruct((N_DEV,) + wo_loc.shape, BF16)),
        compiler_params=_params("arbitrary", "arbitrary"))(_block_order(GATHER_MASKS), h, w_loc, wo_loc)


SCATTER_MASKS = (7, 6, 5, 4, 3, 2, 1, 0)
N_CHIPS = 4


def _scatter_block(k, acc, stage, tmp, own_ref, ra_ref, rb_ref, sa_send, sa_recv, sb_send, sb_recv, loc_sem, step, ns):
    x, y, c = _position()
    chip_of = lambda t: _xor_peer(x, y, c, SCATTER_MASKS[2 * t + 1])
    last = step == ns - 1
    fetch_at = min(1, ns - 1)

    def ship(t):
        return pltpu.make_async_remote_copy(
            src_ref=stage.at[0], dst_ref=ra_ref.at[t], send_sem=sa_send.at[t], recv_sem=sa_recv.at[t],
            device_id=(x, y, 1 - c), device_id_type=MESH)

    def send(t):
        return pltpu.make_async_remote_copy(
            src_ref=stage.at[1], dst_ref=rb_ref.at[t], send_sem=sb_send.at[t], recv_sem=sb_recv.at[t],
            device_id=chip_of(t), device_id_type=MESH)

    for kk in range(N_DEV):
        t = kk // 2
        fetch = pltpu.make_async_copy(ra_ref.at[t], tmp, loc_sem)

        if kk % 2 == 1:
            @pl.when((step == fetch_at) & (k == kk))
            def _():
                ship(t).wait_recv()
                fetch.start()

        @pl.when(last & (k == kk))
        def _():
            if kk % 2 == 0:
                if t >= 1:
                    ship(t - 1).wait_send()
                stage[0] = acc[...].astype(BF16)
                ship(t).start()
            else:
                fetch.wait()
                acc[...] += tmp[...].astype(F32)
                if t < N_CHIPS - 1:
                    if t >= 1:
                        send(t - 1).wait_send()
                    stage[1] = acc[...].astype(BF16)
                    send(t).start()
                else:
                    keep = pltpu.make_async_copy(acc, own_ref, loc_sem)
                    keep.start()
                    keep.wait()
                    ship(t).wait_send()
                    send(t - 1).wait_send()
                    for q in range(N_CHIPS - 1):
                        send(q).wait_recv()


def _scatter_scratch(rows, cols):
    return [pltpu.VMEM((rows, cols), F32), pltpu.VMEM((2, rows, cols), BF16), pltpu.VMEM((rows, cols), BF16),
            pltpu.SemaphoreType.DMA((N_CHIPS,)), pltpu.SemaphoreType.DMA((N_CHIPS,)),
            pltpu.SemaphoreType.DMA((N_CHIPS - 1,)), pltpu.SemaphoreType.DMA((N_CHIPS - 1,)), pltpu.SemaphoreType.DMA(())]


def _scatter_out(rows, cols):
    return (jax.ShapeDtypeStruct((rows, cols), F32), jax.ShapeDtypeStruct((N_CHIPS, rows, cols), BF16),
            jax.ShapeDtypeStruct((N_CHIPS - 1, rows, cols), BF16))


def _dwin_scatter(h, dzh, dza):
    s, d = h.shape
    e = dzh.shape[2]
    ts = _tile(s, 1024)
    ns = s // ts

    def body(order_ref, dzh_ref, dza_ref, h_ref, own_ref, ra_ref, rb_ref, acc, stage, tmp, *sems):
        k, step = pl.program_id(0), pl.program_id(1)

        @pl.when(step == 0)
        def _():
            acc[...] = jnp.zeros_like(acc)

        def add(dz):
            acc[...] += _dot_tn(h_ref[...], dz)

        _dz_pick(order_ref[k], dzh_ref, dza_ref, add)
        _scatter_block(k, acc, stage, tmp, own_ref, ra_ref, rb_ref, *sems, step, ns)

    def dz_spec(lo):
        return pl.BlockSpec((None, ts, e), lambda k, st, o: (jnp.clip(o[k] - lo, 0, 3), st, 0))

    grid_spec = pltpu.PrefetchScalarGridSpec(
        num_scalar_prefetch=1, grid=(N_DEV, ns),
        in_specs=[dz_spec(0), dz_spec(4), pl.BlockSpec((ts, d), lambda k, st, o: (st, 0))],
        out_specs=(ANY, ANY, ANY), scratch_shapes=_scatter_scratch(d, e))
    own, _, rb = pl.pallas_call(
        body, name="dwin_scatter", grid_spec=grid_spec, out_shape=_scatter_out(d, e),
        compiler_params=_params("arbitrary", "arbitrary"))(_block_order(SCATTER_MASKS), dzh, dza, h)
    return own, rb


def _dwout_scatter(y_h, y_a, dxb):
    s, e = y_h.shape
    d = dxb.shape[1]
    r = 2 * e // N_DEV
    pairs = e // (2 * r)
    ts = _tile(s, 1024)
    ns = s // ts
    chip_masks = SCATTER_MASKS[1::2]
    passes = ((0, 1), (2,), (3,))
    slots = max(len(chips) for chips in passes)
    slot_chip = [chips[min(u, len(chips) - 1)] for chips in passes for u in range(slots)]

    def body(pair_ref, yh0_ref, ya0_ref, yh1_ref, ya1_ref, dx_ref, own_ref, ra_ref, rb_ref, acc, keep_buf, ship_buf,
             send_buf, tmp, sa_send, sa_recv, sb_send, sb_recv, loc_sem):
        p, step = pl.program_id(0), pl.program_id(1)
        x, y, c = _position()

        @pl.when(step == 0)
        def _():
            acc[...] = jnp.zeros_like(acc)

        for u, (yh_ref, ya_ref) in enumerate(((yh0_ref, ya0_ref), (yh1_ref, ya1_ref))):
            rows = slice(u * 2 * r, (u + 1) * 2 * r)
            used = functools.reduce(jnp.logical_or, [p == pp for pp, chips in enumerate(passes) if u < len(chips)])

            @pl.when(used & (pair_ref[slots * p + u] < pairs))
            def _():
                acc[rows, :] += _dot_tn(yh_ref[...], dx_ref[...])

            @pl.when(used & (pair_ref[slots * p + u] >= pairs))
            def _():
                acc[rows, :] += _dot_tn(ya_ref[...], dx_ref[...])

        def block_rows(u, core):
            return pl.ds(pl.multiple_of(u * 2 * r + core * r, r), r)

        slot_of = {q: u for chips in passes for u, q in enumerate(chips)}

        def ship(q):
            return pltpu.make_async_remote_copy(
                src_ref=ship_buf.at[slot_of[q]], dst_ref=ra_ref.at[q], send_sem=sa_send.at[q], recv_sem=sa_recv.at[q],
                device_id=(x, y, 1 - c), device_id_type=MESH)

        def send(q):
            return pltpu.make_async_remote_copy(
                src_ref=send_buf.at[slot_of[q]], dst_ref=rb_ref.at[q], send_sem=sb_send.at[q], recv_sem=sb_recv.at[q],
                device_id=_xor_peer(x, y, c, chip_masks[q]), device_id_type=MESH)

        def sibling_share(q):
            ship(q).wait_recv()
            fetch = pltpu.make_async_copy(ra_ref.at[q], tmp, loc_sem)
            fetch.start()
            fetch.wait()
            return tmp[...].astype(F32)

        shipped, sent = {}, {}
        for pp, chips in enumerate(passes):
            @pl.when((step == ns - 1) & (p == pp))
            def _():
                for u, q in enumerate(chips):
                    if u in shipped:
                        ship(shipped.pop(u)).wait_send()
                    ship_buf[u] = acc[block_rows(u, 1 - c), :].astype(BF16)
                    ship(q).start()
                    shipped[u] = q
                for u, q in enumerate(chips):
                    total = acc[block_rows(u, c), :] + sibling_share(q)
                    if q < N_CHIPS - 1:
                        if u in sent:
                            send(sent.pop(u)).wait_send()
                        send_buf[u] = total.astype(BF16)
                        send(q).start()
                        sent[u] = q
                    else:
                        keep_buf[...] = total
                        keep = pltpu.make_async_copy(keep_buf, own_ref, loc_sem)
                        keep.start()
                        keep.wait()
                if pp == len(passes) - 1:
                    for q in shipped.values():
                        ship(q).wait_send()
                    for q in sent.values():
                        send(q).wait_send()
                    for q in range(N_CHIPS - 1):
                        send(q).wait_recv()

    def y_spec(u, lo):
        return pl.BlockSpec((ts, 2 * r), lambda p, st, o: (st, jnp.clip(o[slots * p + u] - lo, 0, pairs - 1)))

    pair_of_chip = _block_order(chip_masks) // 2
    grid_spec = pltpu.PrefetchScalarGridSpec(
        num_scalar_prefetch=1, grid=(len(passes), ns),
        in_specs=[y_spec(0, 0), y_spec(0, pairs), y_spec(1, 0), y_spec(1, pairs),
                  pl.BlockSpec((ts, d), lambda p, st, o: (st, 0))],
        out_specs=(ANY, ANY, ANY),
        scratch_shapes=[pltpu.VMEM((slots * 2 * r, d), F32), pltpu.VMEM((r, d), F32),
                        pltpu.VMEM((slots, r, d), BF16)] + _scatter_scratch(r, d)[1:])
    own, _, rb = pl.pallas_call(
        body, name="dwout_scatter", grid_spec=grid_spec, out_shape=_scatter_out(r, d),
        compiler_params=_params("arbitrary", "arbitrary"))(
            jnp.stack([pair_of_chip[q] for q in slot_chip]), y_h, y_a, y_h, y_a, dxb)
    return own, rb


def _sum_chips_adamw(own, recv, w, m, v):
    r, c = w.shape
    tr = _tile(r, 128)

    def body(own_ref, rc_ref, w_ref, m_ref, v_ref, g_ref, d_ref, mo_ref, vo_ref):
        g = own_ref[...]
        for q in range(N_CHIPS - 1):
            g = g + rc_ref[q].astype(F32)
        g_ref[...] = g
        d_ref[...], mo_ref[...], vo_ref[...] = _adamw(w_ref[...], g, m_ref[...], v_ref[...])

    blk = pl.BlockSpec((tr, c), lambda i: (i, 0))
    shp = jax.ShapeDtypeStruct((r, c), F32)
    return pl.pallas_call(
        body, name="sum_chips_adamw", grid=(r // tr,), out_shape=(shp, shp, shp, shp),
        in_specs=[blk, pl.BlockSpec((N_CHIPS - 1, tr, c), lambda i: (0, i, 0)), blk, blk, blk],
        out_specs=(blk, blk, blk, blk), compiler_params=_params("parallel"))(own, recv, w, m, v)


SMALL_ROWS = 8
ROW_LB = 4
ROW_GN = 6
ROW_LOSS = 7


def _small_allreduce_adamw(part, w, m, v, lb_logits):
    width = part.shape[1]

    def body(p_ref, w_ref, m_ref, v_ref, lb_ref, g_ref, d_ref, mo_ref, vo_ref, buf, send_sems, recv_sems):
        x, y, c = _position()
        me = 4 * x + 2 * y + c
        buf[me] = p_ref[...]
        copies = []
        for k in range(N_DEV - 1):
            bx, by, bc = ((k + 1) >> 2) & 1, ((k + 1) >> 1) & 1, (k + 1) & 1
            peer = (x ^ bx, y ^ by, c ^ bc)
            copies.append(pltpu.make_async_remote_copy(
                src_ref=p_ref, dst_ref=buf.at[me], send_sem=send_sems.at[k], recv_sem=recv_sems.at[k],
                device_id=peer, device_id_type=MESH))
        for cp in copies:
            cp.start()
        for cp in copies:
            cp.wait_recv()
        for cp in copies:
            cp.wait_send()
        tot = buf[0]
        for dev in range(1, N_DEV):
            tot = tot + buf[dev]
        lbv = lb_ref[...]
        lb = _sigmoid(lbv[0:1] - lbv[1:2])
        glb = tot[ROW_LB:ROW_LB + 1] * lb * (1.0 - lb)
        loss = jnp.sum(tot[ROW_LOSS:ROW_LOSS + 1], axis=-1, keepdims=True)
        row = lax.broadcasted_iota(jnp.int32, (SMALL_ROWS, width), 0)
        g = jnp.where(row == ROW_LB, glb, jnp.where(row == ROW_LB + 1, -glb, tot))
        g = jnp.where(row == ROW_LOSS, loss, g)
        g_ref[...] = g
        d_ref[...], mo_ref[...], vo_ref[...] = _adamw(w_ref[...], g, m_ref[...], v_ref[...])

    vm = pl.BlockSpec(memory_space=pltpu.VMEM)
    shp = jax.ShapeDtypeStruct((SMALL_ROWS, width), F32)
    return pl.pallas_call(
        body, name="small_allreduce_adamw", out_shape=(shp, shp, shp, shp),
        in_specs=[vm] * 5, out_specs=(vm, vm, vm, vm),
        scratch_shapes=[pltpu.VMEM((N_DEV, SMALL_ROWS, width), F32), pltpu.SemaphoreType.DMA((N_DEV - 1,)),
                        pltpu.SemaphoreType.DMA((N_DEV - 1,))],
    )(part, w, m, v, lb_logits)


def _pack_small(norm_gain, final_gain, lb2, gnorm, last_row, width):
    pad = lambda a: jnp.pad(a.reshape(1, -1), ((0, 0), (0, width - a.size)))
    return jnp.concatenate([norm_gain.reshape(2, width), final_gain.reshape(2, width), lb2.reshape(2, width),
                            pad(gnorm), last_row.reshape(1, width)], axis=0)


def _unpack_small(p, d, e, hd):
    return (p[0:2].reshape(1, d), p[2:4].reshape(d), p[4:6].reshape(2, e), p[6:7, :hd].reshape(1, hd))


def kernel(x, norm_gain, w_in, lb_logits, hgrn_gnorm, w_out, final_gain, loss_target, m_norm_gain, m_w_in, m_lb_logits, m_hgrn_gnorm, m_w_out, m_final_gain, v_norm_gain, v_w_in, v_lb_logits, v_hgrn_gnorm, v_w_out, v_final_gain):
    s, d = x.shape[1], x.shape[2]
    e = w_in.shape[2]
    assert d == 2 * e and lb_logits.shape == (2, e) and w_out.shape[1] * N_DEV == 2 * e
    x2d = x.reshape(s, d)
    tgt = loss_target.reshape(s, d)

    h = _rmsnorm_fwd(x2d, norm_gain)
    z, w_in_full, w_out_full = _inproj_gather(h, _cast_bf16(w_in[0]), _cast_bf16(w_out[0]))
    w_out_full = w_out_full.reshape(2 * e, d)
    y_h, states = _hgrn_fwd(z, lb_logits, hgrn_gnorm)
    o_attn, lse, y_a = _attn_fwd(z)
    dx2, dx2b, dy, loss_vec, dfg = _outproj_loss(x2d, y_h, y_a, w_out_full, final_gain.reshape(1, d), tgt)

    own_o, recv_o = _dwout_scatter(y_h, y_a, dx2b)
    dza = _attn_bwd(z, dy, o_attn, lse)
    dzh, dlb, dgn = _hgrn_bwd(z, dy, states, lb_logits, hgrn_gnorm)
    grad_x, dng = _dh_dx(dzh, dza, w_in_full, x2d, norm_gain, dx2)
    g_wo, d_wo, nm_wo, nv_wo = _sum_chips_adamw(own_o, recv_o, w_out[0], m_w_out[0], v_w_out[0])

    width = d // 2
    zero_row = jnp.zeros((1, width), F32)
    loss_row = loss_vec[:, :width] + loss_vec[:, width:]
    part = _pack_small(dng, dfg, jnp.concatenate([dlb, zero_row], axis=0), dgn, loss_row, width)
    pw = _pack_small(norm_gain, final_gain, lb_logits, hgrn_gnorm, zero_row, width)
    pm = _pack_small(m_norm_gain, m_final_gain, m_lb_logits, m_hgrn_gnorm, zero_row, width)
    pv = _pack_small(v_norm_gain, v_final_gain, v_lb_logits, v_hgrn_gnorm, zero_row, width)
    sg, sd, sm, sv = _small_allreduce_adamw(part, pw, pm, pv, lb_logits)
    own_i, recv_i = _dwin_scatter(h, dzh, dza)
    g_wi, d_wi, nm_wi, nv_wi = _sum_chips_adamw(own_i, recv_i, w_in[0], m_w_in[0], v_w_in[0])
    hd = hgrn_gnorm.shape[1]
    g_ng, g_fg, g_lb, g_gn = _unpack_small(sg, d, e, hd)
    d_ng, d_fg, d_lb, d_gn = _unpack_small(sd, d, e, hd)
    m_ng, m_fg, m_lb, m_gn = _unpack_small(sm, d, e, hd)
    v_ng, v_fg, v_lb, v_gn = _unpack_small(sv, d, e, hd)
    loss = sg[ROW_LOSS, 0]

    one = lambda a: a[None]
    return (loss, grad_x.reshape(1, s, d), g_ng, one(g_wi), g_lb, g_gn, one(g_wo), g_fg,
            d_ng, one(d_wi), d_lb, d_gn, one(d_wo), d_fg,
            m_ng, one(nm_wi), m_lb, m_gn, one(nm_wo), m_fg,
            v_ng, one(nv_wi), v_lb, v_gn, one(nv_wo), v_fg)
```

```python
import functools
import math

import jax
import jax.numpy as jnp
from jax import lax
from jax.experimental import pallas as pl
from jax.experimental.pallas import tpu as pltpu

NORM_EPS = 1e-6
HGRN_HEAD = 128
HGRN_CHUNK = 64
ATTN_HEAD = 64
ATTN_BAND = 128
DILATIONS = (1, 4, 16)
N_SPLITS = 8
N_DEV = 8
ADAM_LR = 0.001
ADAM_B1 = 0.9
ADAM_B2 = 0.999
ADAM_EPS = 1e-08
ADAM_WD = 0.01
ADAM_STEP = 10
LANES = 128
MESH = pl.DeviceIdType.MESH
F32 = jnp.float32
BF16 = jnp.bfloat16
NEG_BIG = -1e30
VMEM_LIMIT = 56 * 1024 * 1024
OUTPROJ_VMEM_LIMIT = 63 * 1024 * 1024
DHDX_VMEM_LIMIT = 60 * 1024 * 1024

ANY = pl.BlockSpec(memory_space=pl.ANY)


def _params(*sem):
    return pltpu.CompilerParams(dimension_semantics=sem, vmem_limit_bytes=VMEM_LIMIT)


def _tile(n, pref):
    t = min(n, pref)
    assert n % t == 0, (n, pref)
    return t


def _dot(a, b, precision=None):
    return jnp.dot(a, b, preferred_element_type=F32, precision=precision)


def _dot_nt(a, b):
    return lax.dot_general(a, b, (((1,), (1,)), ((), ())), preferred_element_type=F32)


def _dot_tn(a, b):
    return lax.dot_general(a, b, (((0,), (0,)), ((), ())), preferred_element_type=F32)


def _sigmoid(x):
    return 0.5 * jnp.tanh(0.5 * x) + 0.5


def _dsilu(x, s):
    return s * (1.0 + x * (1.0 - s))


def _adamw(w, g, m, v):
    m = ADAM_B1 * m + (1.0 - ADAM_B1) * g
    v = ADAM_B2 * v + (1.0 - ADAM_B2) * (g * g)
    m_hat = m / (1.0 - ADAM_B1 ** ADAM_STEP)
    v_hat = v / (1.0 - ADAM_B2 ** ADAM_STEP)
    delta = -ADAM_LR * (m_hat / (jnp.sqrt(v_hat) + ADAM_EPS) + ADAM_WD * w)
    return delta, m, v


def _cast_bf16(a):
    r, c = a.shape
    tr = _tile(r, 256)

    def body(a_ref, o_ref):
        o_ref[...] = a_ref[...].astype(BF16)

    return pl.pallas_call(
        body, name="cast_bf16", grid=(r // tr,), out_shape=jax.ShapeDtypeStruct((r, c), BF16),
        in_specs=[pl.BlockSpec((tr, c), lambda i: (i, 0))], out_specs=pl.BlockSpec((tr, c), lambda i: (i, 0)),
        compiler_params=_params("parallel"))(a)


def _rmsnorm_fwd(x, gain):
    s, d = x.shape
    tm = _tile(s, 512)

    def body(x_ref, g_ref, h_ref):
        xv = x_ref[...]
        r = lax.rsqrt(jnp.mean(xv * xv, axis=-1, keepdims=True) + NORM_EPS)
        h_ref[...] = (xv * r * g_ref[...]).astype(BF16)

    return pl.pallas_call(
        body, name="rmsnorm_fwd", grid=(s // tm,), out_shape=jax.ShapeDtypeStruct((s, d), BF16),
        in_specs=[pl.BlockSpec((tm, d), lambda i: (i, 0)), pl.BlockSpec((1, d), lambda i: (0, 0))],
        out_specs=pl.BlockSpec((tm, d), lambda i: (i, 0)), compiler_params=_params("parallel"))(x, gain)


HGRN_BLOCK = 2048
TRI_ROWS = 64


def _chunk_masks():
    tb = TRI_ROWS
    row = lax.broadcasted_iota(jnp.int32, (tb, tb), 0)
    col = lax.broadcasted_iota(jnp.int32, (tb, tb), 1)
    same = (row // HGRN_CHUNK) == (col // HGRN_CHUNK)
    lower = jnp.where(same & (col <= row), 1.0, 0.0).astype(BF16)
    upper = jnp.where(same & (col >= row), 1.0, 0.0).astype(BF16)
    return lower, upper


def _split3(a):
    hi = a.astype(BF16).astype(F32)
    mid = (a - hi).astype(BF16).astype(F32)
    lo = (a - hi - mid).astype(BF16).astype(F32)
    return hi, mid, lo


def _tri_dot(tri, x):
    hi, mid, lo = (p.astype(BF16) for p in _split3(x))
    outs = []
    for r in range(0, x.shape[0], TRI_ROWS):
        sl = slice(r, r + TRI_ROWS)
        outs.append(_dot(tri, hi[sl]) + _dot(tri, mid[sl]) + _dot(tri, lo[sl]))
    return outs[0] if len(outs) == 1 else jnp.concatenate(outs, axis=0)


def _hgrn_gates(qp, fp, lbv):
    lb = _sigmoid(lbv[0:1] - lbv[1:2])
    sq = _sigmoid(qp)
    q = qp * sq
    sg = _sigmoid(fp)
    f = lb + (1.0 - lb) * sg
    k = 1.0 - f
    return lb, sq, q, sg, f, k


def _hgrn_fwd(z, lb_logits, gnorm):
    s = z.shape[0]
    e = z.shape[1] // N_SPLITS
    nh = e // HGRN_HEAD
    tb = _tile(s, HGRN_BLOCK)
    nc = tb // HGRN_CHUNK
    nb = s // tb
    C = HGRN_CHUNK

    def body(q_ref, f_ref, i_ref, g_ref, lb_ref, gn_ref, y_ref, st_ref, state, o_scr):
        @pl.when(pl.program_id(1) == 0)
        def _():
            state[...] = jnp.zeros_like(state)

        lb, sq, q, sg, f, k = _hgrn_gates(q_ref[...], f_ref[...], lb_ref[...])
        lower, _ = _chunk_masks()
        b = _tri_dot(lower, jnp.log(f))
        b3 = b.reshape(nc, C, HGRN_HEAD)
        bc = b3[:, C - 1:C, :]
        qt = (q * jnp.exp(b)).astype(BF16)
        kt = (k * jnp.exp(-b)).astype(BF16)
        ke = (k.reshape(nc, C, HGRN_HEAD) * jnp.exp(bc - b3)).reshape(tb, HGRN_HEAD).astype(BF16)
        v = i_ref[...].astype(BF16)
        tri = lax.broadcasted_iota(jnp.int32, (C, C), 1) <= lax.broadcasted_iota(jnp.int32, (C, C), 0)
        sls = [slice(c * C, (c + 1) * C) for c in range(nc)]
        kv = [_dot_tn(v[sl], ke[sl]) for sl in sls]
        a = [jnp.where(tri, _dot_nt(qt[sl], kt[sl]), 0.0).astype(BF16) for sl in sls]
        st = state[...]
        sts = []
        for c in range(nc):
            sts.append(st)
            st_ref[c] = st
            st = st * jnp.exp(bc[c]) + kv[c]
        state[...] = st
        for c, sl in enumerate(sls):
            o_scr[sl, :] = _dot(a[c], v[sl]) + _dot_nt(qt[sl], sts[c].astype(BF16))
        o = o_scr[...]
        rms = lax.rsqrt(jnp.mean(o * o, axis=-1, keepdims=True) + NORM_EPS)
        gp = g_ref[...]
        y_ref[...] = (o * rms * gn_ref[...] * (gp * _sigmoid(gp))).astype(BF16)

    col = lambda kk: (lambda h, n: (n, kk * nh + h))
    return pl.pallas_call(
        body, name="hgrn_fwd", grid=(nh, nb),
        out_shape=(jax.ShapeDtypeStruct((s, e), BF16),
                   jax.ShapeDtypeStruct((nh, s // C, HGRN_HEAD, HGRN_HEAD), F32)),
        in_specs=[pl.BlockSpec((tb, HGRN_HEAD), col(0)), pl.BlockSpec((tb, HGRN_HEAD), col(1)),
                  pl.BlockSpec((tb, HGRN_HEAD), col(2)), pl.BlockSpec((tb, HGRN_HEAD), col(3)),
                  pl.BlockSpec((2, HGRN_HEAD), lambda h, n: (0, h)), pl.BlockSpec((1, HGRN_HEAD), lambda h, n: (0, 0))],
        out_specs=(pl.BlockSpec((tb, HGRN_HEAD), lambda h, n: (n, h)),
                   pl.BlockSpec((None, nc, HGRN_HEAD, HGRN_HEAD), lambda h, n: (h, n, 0, 0))),
        scratch_shapes=[pltpu.VMEM((HGRN_HEAD, HGRN_HEAD), F32), pltpu.VMEM((tb, HGRN_HEAD), F32)],
        compiler_params=_params("parallel", "arbitrary"))(z, z, z, z, lb_logits, gnorm)


def _hgrn_bwd(z, dy, states, lb_logits, gnorm):
    s = z.shape[0]
    e = z.shape[1] // N_SPLITS
    nh = e // HGRN_HEAD
    tb = _tile(s, HGRN_BLOCK)
    nc = tb // HGRN_CHUNK
    nb = s // tb
    C = HGRN_CHUNK
    H = HGRN_HEAD

    def body(q_ref, f_ref, i_ref, g_ref, dy_ref, st_ref, lb_ref, gn_ref, dz_ref, dlb_ref, dgn_ref,
             gstate, o_scr, dq_scr, dk_scr, dv_scr, e_scr):
        first = (pl.program_id(0) == 0) & (pl.program_id(1) == 0)

        @pl.when(first)
        def _():
            dgn_ref[...] = jnp.zeros_like(dgn_ref)

        @pl.when(pl.program_id(1) == 0)
        def _():
            gstate[...] = jnp.zeros_like(gstate)
            dlb_ref[...] = jnp.zeros_like(dlb_ref)

        qp = q_ref[...]
        lb, sq, q, sg, f, k = _hgrn_gates(qp, f_ref[...], lb_ref[...])
        lower, upper = _chunk_masks()
        b = _tri_dot(lower, jnp.log(f))
        b3 = b.reshape(nc, C, H)
        bc = b3[:, C - 1:C, :]
        eb = jnp.exp(b)
        enb = jnp.exp(-b)
        eend = jnp.exp(bc - b3).reshape(tb, H)
        qt = (q * eb).astype(BF16)
        kt = (k * enb).astype(BF16)
        ke = (k * eend).astype(BF16)
        v = i_ref[...].astype(BF16)
        tri = lax.broadcasted_iota(jnp.int32, (C, C), 1) <= lax.broadcasted_iota(jnp.int32, (C, C), 0)
        sls = [slice(c * C, (c + 1) * C) for c in range(nc)]
        a = [jnp.where(tri, _dot_nt(qt[sl], kt[sl]), 0.0).astype(BF16) for sl in sls]
        for c, sl in enumerate(sls):
            o_scr[sl, :] = _dot(a[c], v[sl]) + _dot_nt(qt[sl], st_ref[c].astype(BF16))
        o = o_scr[...]
        rms = lax.rsqrt(jnp.mean(o * o, axis=-1, keepdims=True) + NORM_EPS)
        on = o * rms
        gn = gn_ref[...]
        gp = g_ref[...]
        sgg = _sigmoid(gp)
        dyv = dy_ref[...]
        d_on = dyv * (gp * sgg)
        dz_ref[3] = (dyv * on * gn * _dsilu(gp, sgg)).astype(BF16)
        dgn_ref[...] += jnp.sum(d_on * on, axis=0, keepdims=True)
        u = d_on * gn
        do = (rms * (u - on * jnp.mean(u * on, axis=-1, keepdims=True))).astype(BF16)
        gup = [_dot_tn(do[sl], qt[sl]) for sl in sls]
        da = [jnp.where(tri, _dot_nt(do[sl], v[sl]), 0.0).astype(BF16) for sl in sls]
        gt = gstate[...]
        gts = [None] * nc
        for c in reversed(range(nc)):
            gts[c] = gt
            gt = gt * jnp.exp(bc[c]) + gup[c]
        gstate[...] = gt
        for c, sl in enumerate(sls):
            stp = st_ref[c]
            gtb = gts[c].astype(BF16)
            dqt = _dot(da[c], kt[sl]) + _dot(do[sl], stp.astype(BF16))
            dkt = _dot_tn(da[c], qt[sl])
            dks = _dot(v[sl], gtb) * eend[sl]
            dv_scr[sl, :] = _dot_tn(a[c], do[sl]) + _dot_nt(ke[sl], gtb)
            dq_scr[sl, :] = dqt * eb[sl]
            dk_scr[sl, :] = dkt * enb[sl] + dks
            ech = (jnp.sum(k[sl] * dks, axis=0, keepdims=True)
                   + jnp.sum(gts[c] * jnp.exp(bc[c]) * stp, axis=0, keepdims=True))
            e_scr[sl, :] = jnp.broadcast_to(ech, (C, H))
        dq = dq_scr[...]
        dk = dk_scr[...]
        dlf = _tri_dot(upper, q * dq - k * dk) + e_scr[...]
        dft = dlf / f - dk
        dz_ref[0] = (dq * _dsilu(qp, sq)).astype(BF16)
        dz_ref[1] = (dft * (1.0 - lb) * sg * (1.0 - sg)).astype(BF16)
        dz_ref[2] = dv_scr[...].astype(BF16)
        dlb_ref[...] += jnp.sum(dft * (1.0 - sg), axis=0, keepdims=True)

    col = lambda kk: (lambda h, n: (nb - 1 - n, kk * nh + h))
    return pl.pallas_call(
        body, name="hgrn_bwd", grid=(nh, nb),
        out_shape=(jax.ShapeDtypeStruct((4, s, e), BF16), jax.ShapeDtypeStruct((1, e), F32),
                   jax.ShapeDtypeStruct((1, H), F32)),
        in_specs=[pl.BlockSpec((tb, H), col(0)), pl.BlockSpec((tb, H), col(1)),
                  pl.BlockSpec((tb, H), col(2)), pl.BlockSpec((tb, H), col(3)),
                  pl.BlockSpec((tb, H), lambda h, n: (nb - 1 - n, h)),
                  pl.BlockSpec((None, nc, H, H), lambda h, n: (h, nb - 1 - n, 0, 0)),
                  pl.BlockSpec((2, H), lambda h, n: (0, h)), pl.BlockSpec((1, H), lambda h, n: (0, 0))],
        out_specs=(pl.BlockSpec((4, tb, H), lambda h, n: (0, nb - 1 - n, h)),
                   pl.BlockSpec((1, H), lambda h, n: (0, h)), pl.BlockSpec((1, H), lambda h, n: (0, 0))),
        scratch_shapes=[pltpu.VMEM((H, H), F32)] + [pltpu.VMEM((tb, H), F32)] * 5,
        compiler_params=_params("arbitrary", "arbitrary"))(z, z, z, z, dy, states, lb_logits, gnorm)


ATTN_T = 16 * ATTN_BAND
SCALE = ATTN_HEAD ** -0.5


def _slope(pair, hh, nheads):
    head = (2 * pair + hh + 1).astype(F32)
    return jnp.exp(jnp.full((1, 1), -8.0 / nheads * math.log(2.0), F32) * head)


def _fill_bias(bias, pair, nheads, delta, edge_ok):
    band = (delta >= 0) & (delta <= ATTN_BAND)
    dist = delta.astype(F32)
    for pi, dil in enumerate(DILATIONS):
        for hh in range(2):
            full = jnp.where(band, -(_slope(pair, hh, nheads) * float(dil)) * dist, NEG_BIG)
            bias[(pi * 2 + hh) * 2] = full
            bias[(pi * 2 + hh) * 2 + 1] = jnp.where(edge_ok, full, NEG_BIG)


def _rows(start, size, stride):
    if stride == 1:
        return pl.ds(pl.multiple_of(start, ATTN_BAND), size)
    return pl.ds(start, size, stride=stride)


def _head_lanes(rows, hh):
    return (lax.broadcasted_iota(jnp.int32, (rows, LANES), 1) // ATTN_HEAD) == hh


def _attn_fwd(z):
    s = z.shape[0]
    e = z.shape[1] // N_SPLITS
    npair = e // LANES
    T = ATTN_T
    assert s % T == 0
    nsb = s // T
    W = ATTN_BAND
    nt = T // W
    HD = ATTN_HEAD
    chunk = 256

    nsteps = npair * nsb
    assert nt % 2 == 0 and T // chunk == nt // 2

    def body(q_ref, kp_ref, kc_ref, vp_ref, vc_ref, g_ref, o_ref, l_ref, y_ref, qa, kbuf, va, bias, *sets):
        t = pl.program_id(0)
        sb = jnp.minimum(t, nsteps - 1) % nsb
        pair = jnp.minimum(t, nsteps - 1) // nsb
        sets = (sets[0:3], sets[3:6])

        def merge_chunk(i, done):
            accs, ms, lsw = done
            rows = pl.ds(pl.multiple_of(i * chunk, chunk), chunk)
            m1, m2, m3 = ms[0, rows, :], ms[1, rows, :], ms[2, rows, :]
            mx = jnp.maximum(jnp.maximum(m1, m2), m3)
            w1, w2, w3 = jnp.exp(m1 - mx), jnp.exp(m2 - mx), jnp.exp(m3 - mx)
            unswap = lambda a: pltpu.roll(a, ATTN_HEAD, 1)
            den = w1 * unswap(lsw[0, rows, :]) + w2 * unswap(lsw[1, rows, :]) + w3 * unswap(lsw[2, rows, :])
            o = (w1 * accs[0, rows, :] + w2 * accs[1, rows, :] + w3 * accs[2, rows, :]) / den
            o_ref[rows, :] = o
            l_ref[rows, :] = mx + jnp.log(den)
            gp = g_ref[rows, :]
            y_ref[rows, :] = (o * (gp * _sigmoid(gp))).astype(BF16)

        @pl.when(t == 0)
        def _():
            accs, ms, lsw = sets[1]
            accs[...] = jnp.zeros_like(accs)
            ms[...] = jnp.zeros_like(ms)
            lsw[...] = jnp.ones_like(lsw)

        @pl.when(t == nsteps)
        def _():
            def drain(i, carry):
                merge_chunk(i, sets[(nsteps - 1) % 2])
                return carry

            lax.fori_loop(0, T // chunk, drain, 0)

        def compute(cur, done):
            accs, ms, lsw = cur
            def stage(i, carry):
                rows = pl.ds(pl.multiple_of(i * chunk, chunk), chunk)
                upper = pl.ds(pl.multiple_of(T + i * chunk, chunk), chunk)
                kbuf[upper, :] = kc_ref[rows, :]
                for hh in range(2):
                    mine = _head_lanes(chunk, hh)
                    qa[hh, rows, :] = jnp.where(mine, q_ref[rows, :] * SCALE, 0.0)
                    va[hh, upper, :] = jnp.where(mine, vc_ref[rows, :], 1.0)
                return carry

            lax.fori_loop(0, T // chunk, stage, 0)

            @pl.when(sb == 0)
            def _():
                def stage_prev(i, carry):
                    rows = pl.ds(pl.multiple_of(i * chunk, chunk), chunk)
                    kbuf[rows, :] = kp_ref[rows, :]
                    for hh in range(2):
                        va[hh, rows, :] = jnp.where(_head_lanes(chunk, hh), vp_ref[rows, :], 1.0)
                    return carry

                lax.fori_loop(0, T // chunk, stage_prev, 0)

            qi = lax.broadcasted_iota(jnp.int32, (W, 2 * W), 0)
            kj = lax.broadcasted_iota(jnp.int32, (W, 2 * W), 1)
            _fill_bias(bias, pair, 2 * npair, W + qi - kj, kj >= W)

            def tile(tau):
                first = _head_lanes(W, 0)
                rows, scores = [], []
                for pi, dil in enumerate(DILATIONS):
                    r = tau % dil
                    ub = tau // dil
                    qrows = _rows(r + dil * W * ub, W, dil)
                    krows = _rows(T + dil * W * (ub - 1) + r, 2 * W, dil)
                    var = jnp.where((sb == 0) & (ub == 0), 1, 0)
                    kt = kbuf[krows, :].astype(BF16)
                    rows.append((qrows, krows))
                    scores.append([_dot_nt(qa[hh, qrows, :].astype(BF16), kt) + bias[(pi * 2 + hh) * 2 + var]
                                   for hh in range(2)])
                maxes = [[jnp.max(sc, axis=-1, keepdims=True) for sc in pair_] for pair_ in scores]
                probs = [[jnp.exp(sc - m).astype(BF16) for sc, m in zip(ps, pm)] for ps, pm in zip(scores, maxes)]
                for pi, (qrows, krows) in enumerate(rows):
                    outs = [_dot(probs[pi][hh], va[hh, krows, :].astype(BF16)) for hh in range(2)]
                    accs[pi, qrows, :] = jnp.where(first, outs[0], outs[1])
                    lsw[pi, qrows, :] = jnp.where(first, outs[1], outs[0])
                    ms[pi, qrows, :] = jnp.where(first, maxes[pi][0], maxes[pi][1])

            def two_tiles(i, carry):
                tile(2 * i)
                tile(2 * i + 1)
                merge_chunk(i, done)
                return carry

            lax.fori_loop(0, nt // 2, two_tiles, 0)

            def move_down(i, carry):
                rows = pl.ds(pl.multiple_of(i * chunk, chunk), chunk)
                upper = pl.ds(pl.multiple_of(T + i * chunk, chunk), chunk)
                kbuf[rows, :] = kbuf[upper, :]
                for hh in range(2):
                    va[hh, rows, :] = va[hh, upper, :]
                return carry

            lax.fori_loop(0, T // chunk, move_down, 0)

        for parity in range(2):
            @pl.when((t < nsteps) & (t % 2 == parity))
            def _():
                compute(sets[parity], sets[1 - parity])

    step_of = lambda t: jnp.minimum(t, nsteps - 1)
    lag_of = lambda t: jnp.maximum(t - 1, 0)
    cur = lambda split: (lambda t: (step_of(t) % nsb, split * npair + step_of(t) // nsb))
    prev = lambda split: (lambda t: (0, split * npair + step_of(t) // nsb))
    blk = lambda index: pl.BlockSpec((T, LANES), index)
    out = blk(lambda t: (lag_of(t) % nsb, lag_of(t) // nsb))
    gate = blk(lambda t: (lag_of(t) % nsb, 7 * npair + lag_of(t) // nsb))
    buf = lambda rows: pltpu.VMEM((rows, LANES), F32)
    return pl.pallas_call(
        body, name="attn_fwd", grid=(nsteps + 1,),
        out_shape=(jax.ShapeDtypeStruct((s, e), F32), jax.ShapeDtypeStruct((s, e), F32), jax.ShapeDtypeStruct((s, e), BF16)),
        in_specs=[blk(cur(4)), blk(prev(5)), blk(cur(5)), blk(prev(6)), blk(cur(6)), gate],
        out_specs=(out, out, out),
        scratch_shapes=[pltpu.VMEM((2, T, LANES), F32), buf(2 * T), pltpu.VMEM((2, 2 * T, LANES), F32),
                        pltpu.VMEM((12, W, 2 * W), F32)] + [pltpu.VMEM((3, T, LANES), F32)] * 6,
        compiler_params=_params("arbitrary"))(z, z, z, z, z, z)


def _outproj_loss(x, y_h, y_a, w_out_full, final_gain, target):
    s, d = x.shape
    e = y_h.shape[1]
    tm = _tile(s, 512)
    sub = _tile(tm, 256)

    def body(x_ref, yh_ref, ya_ref, w_ref, g_ref, t_ref, dx_ref, dxb_ref, dy_ref, loss_ref, dg_ref):
        @pl.when(pl.program_id(0) == 0)
        def _():
            loss_ref[...] = jnp.zeros_like(loss_ref)
            dg_ref[...] = jnp.zeros_like(dg_ref)

        w = w_ref[...]
        g = g_ref[...]
        parts = [slice(r0, r0 + sub) for r0 in range(0, tm, sub)]
        x2s = [x_ref[rows, :] + _dot(yh_ref[rows, :], w[0:e]) + _dot(ya_ref[rows, :], w[e:2 * e]) for rows in parts]
        for rows, x2 in zip(parts, x2s):
            r = lax.rsqrt(jnp.mean(x2 * x2, axis=-1, keepdims=True) + NORM_EPS)
            xn = x2 * r
            err = xn * g - t_ref[rows, :]
            loss_ref[...] += jnp.sum(err * err, axis=0, keepdims=True) * (0.5 / d)
            dyo = err * (1.0 / d)
            dg_ref[...] += jnp.sum(dyo * xn, axis=0, keepdims=True)
            u = dyo * g
            dx2 = r * (u - xn * jnp.mean(u * xn, axis=-1, keepdims=True))
            dx_ref[rows, :] = dx2
            dxb = dx2.astype(BF16)
            dxb_ref[rows, :] = dxb
            dy_ref[rows, :] = _dot_nt(dxb, w)

    row = pl.BlockSpec((tm, d), lambda i: (i, 0))
    half = pl.BlockSpec((tm, e), lambda i: (i, 0))
    vec = pl.BlockSpec((1, d), lambda i: (0, 0))
    whole = pl.BlockSpec((2 * e, d), lambda i: (0, 0), pipeline_mode=pl.Buffered(1))
    return pl.pallas_call(
        body, name="outproj_loss", grid=(s // tm,),
        out_shape=(jax.ShapeDtypeStruct((s, d), F32), jax.ShapeDtypeStruct((s, d), BF16),
                   jax.ShapeDtypeStruct((s, 2 * e), F32), jax.ShapeDtypeStruct((1, d), F32),
                   jax.ShapeDtypeStruct((1, d), F32)),
        in_specs=[row, half, half, whole, vec, row],
        out_specs=(row, row, pl.BlockSpec((tm, 2 * e), lambda i: (i, 0)), vec, vec),
        compiler_params=pltpu.CompilerParams(dimension_semantics=("arbitrary",), vmem_limit_bytes=OUTPROJ_VMEM_LIMIT),
    )(x, y_h, y_a, w_out_full, final_gain, target)


def _attn_bwd(z, dy, o, lse):
    s, e = o.shape
    npair = e // LANES
    T = ATTN_T
    assert s % T == 0
    nsb = s // T
    W = ATTN_BAND
    nt = T // W
    HD = ATTN_HEAD
    chunk = 256

    def body(k_ref, v_ref, qc_ref, qn_ref, dyc_ref, dyn_ref, gc_ref, gn_ref, oc_ref, on_ref, lc_ref, ln_ref,
             dz_ref, qa, doa, ka, va, dqacc, dkacc, dvacc, bias):
        sb = pl.program_id(1)
        def stage_queries(half, q_r, dy_r, g_r, o_r, l_r):
            def stage(i, carry):
                rows = pl.ds(pl.multiple_of(i * chunk, chunk), chunk)
                dst = pl.ds(pl.multiple_of(half * T + i * chunk, chunk), chunk)
                lane = lax.broadcasted_iota(jnp.int32, (chunk, LANES), 1)
                gp = g_r[rows, :]
                dov = dy_r[rows, :] * (gp * _sigmoid(gp))
                qv = q_r[rows, :] * SCALE
                same_head = (lax.broadcasted_iota(jnp.int32, (LANES, LANES), 0) // HD
                             == lax.broadcasted_iota(jnp.int32, (LANES, LANES), 1) // HD)
                ones = jnp.where(same_head, 1.0, 0.0).astype(BF16)
                hi, mid, lo = (p.astype(BF16) for p in _split3(dov * o_r[rows, :]))
                delta = _dot(hi, ones) + _dot(mid, ones) + _dot(lo, ones)
                swap = lambda a: pltpu.roll(a, HD, 1)
                lse_parts = [swap(p) for p in _split3(l_r[rows, :])]
                dl_parts = [swap(p) for p in _split3(delta)]
                for hh in range(2):
                    mine = _head_lanes(chunk, hh)
                    spare = (1 - hh) * HD
                    qh = jnp.where(mine, qv, 0.0)
                    dh = jnp.where(mine, dov, 0.0)
                    for j in range(3):
                        qh = jnp.where(lane == spare + j, lse_parts[j], qh)
                        dh = jnp.where(lane == spare + j, dl_parts[j], dh)
                    qa[hh, dst, :] = qh
                    doa[hh, dst, :] = dh
                return carry

            lax.fori_loop(0, T // chunk, stage, 0)

        @pl.when(sb == 0)
        def _():
            stage_queries(0, qc_ref, dyc_ref, gc_ref, oc_ref, lc_ref)

        stage_queries(1, qn_ref, dyn_ref, gn_ref, on_ref, ln_ref)

        def stage_keys(i, carry):
            rows = pl.ds(pl.multiple_of(i * chunk, chunk), chunk)
            lane = lax.broadcasted_iota(jnp.int32, (chunk, LANES), 1)
            for hh in range(2):
                spare = (1 - hh) * HD
                minus = (lane >= spare) & (lane < spare + 3)
                ka[hh, rows, :] = jnp.where(minus, -1.0, k_ref[rows, :])
                va[hh, rows, :] = jnp.where(minus, -1.0, v_ref[rows, :])
            gp = gc_ref[rows, :]
            dz_ref[3, rows, :] = (dyc_ref[rows, :] * oc_ref[rows, :] * _dsilu(gp, _sigmoid(gp))).astype(BF16)
            return carry

        lax.fori_loop(0, T // chunk, stage_keys, 0)

        @pl.when(sb == 0)
        def _():
            dqacc[0:T, :] = jnp.zeros((T, LANES), F32)

        dqacc[T:, :] = jnp.zeros((T, LANES), F32)
        dkacc[...] = jnp.zeros_like(dkacc)
        dvacc[...] = jnp.zeros_like(dvacc)
        qi = lax.broadcasted_iota(jnp.int32, (2 * W, W), 0)
        kj = lax.broadcasted_iota(jnp.int32, (2 * W, W), 1)
        _fill_bias(bias, pl.program_id(0), 2 * npair, qi - kj, qi < W)

        def tile(tau, carry):
            def scores(step, pi):
                dil = DILATIONS[pi]
                r = step % dil
                ub = step // dil
                start = r + dil * W * ub
                krows = _rows(start, W, dil)
                qrows = _rows(start, 2 * W, dil)
                var = jnp.where((sb == nsb - 1) & (ub == nt // dil - 1), 1, 0)
                unit = dict(krows=krows, qrows=qrows, ops=[], sc=[], dpd=[])
                for hh in range(2):
                    kt = ka[hh, krows, :].astype(BF16)
                    vt = va[hh, krows, :].astype(BF16)
                    qt = qa[hh, qrows, :].astype(BF16)
                    dt = doa[hh, qrows, :].astype(BF16)
                    unit["ops"].append((kt, qt, dt))
                    unit["sc"].append(_dot_nt(qt, kt) + bias[(pi * 2 + hh) * 2 + var])
                    unit["dpd"].append(_dot_nt(dt, vt))
                return unit

            def elementwise(unit):
                ps = [jnp.exp(s_) for s_ in unit["sc"]]
                unit["ds"] = [(p * d).astype(BF16) for p, d in zip(ps, unit["dpd"])]
                unit["pb"] = [p.astype(BF16) for p in ps]

            def products(unit):
                dvs = [_dot_tn(pb, dt) for pb, (kt, qt, dt) in zip(unit["pb"], unit["ops"])]
                dks = [_dot_tn(ds, qt) for ds, (kt, qt, dt) in zip(unit["ds"], unit["ops"])]
                dqs = [_dot(ds, kt) for ds, (kt, qt, dt) in zip(unit["ds"], unit["ops"])]
                dkacc[unit["krows"], :] += jnp.where(_head_lanes(W, 0), dks[0], dks[1])
                dvacc[unit["krows"], :] += jnp.where(_head_lanes(W, 0), dvs[0], dvs[1])
                dqacc[unit["qrows"], :] += jnp.where(_head_lanes(2 * W, 0), dqs[0], dqs[1]) * SCALE

            order = [(2 * tau + half, pi) for half in range(2) for pi in range(len(DILATIONS))]
            units = [None] * len(order)
            for n in range(len(order) + 2):
                if n < len(order):
                    units[n] = scores(*order[n])
                if 1 <= n <= len(order):
                    elementwise(units[n - 1])
                if n >= 2:
                    products(units[n - 2])
            return carry

        lax.fori_loop(0, nt // 2, tile, 0)

        def flush(i, carry):
            rows = pl.ds(pl.multiple_of(i * chunk, chunk), chunk)
            nxt = pl.ds(pl.multiple_of(T + i * chunk, chunk), chunk)
            dz_ref[0, rows, :] = dqacc[rows, :].astype(BF16)
            dz_ref[1, rows, :] = dkacc[rows, :].astype(BF16)
            dz_ref[2, rows, :] = dvacc[rows, :].astype(BF16)
            dqacc[rows, :] = dqacc[nxt, :]
            for hh in range(2):
                qa[hh, rows, :] = qa[hh, nxt, :]
                doa[hh, rows, :] = doa[hh, nxt, :]
            return carry

        lax.fori_loop(0, T // chunk, flush, 0)

    zc = lambda split: (lambda hp, sb: (sb, split * npair + hp))
    zn = lambda split: (lambda hp, sb: (jnp.minimum(sb + 1, nsb - 1), split * npair + hp))
    ec = lambda off: (lambda hp, sb: (sb, off + hp))
    en = lambda off: (lambda hp, sb: (jnp.minimum(sb + 1, nsb - 1), off + hp))
    z0 = lambda split: (lambda hp, sb: (0, split * npair + hp))
    e0 = lambda off: (lambda hp, sb: (0, off + hp))
    blk = lambda index: pl.BlockSpec((T, LANES), index)
    buf = lambda rows: pltpu.VMEM((rows, LANES), F32)
    return pl.pallas_call(
        body, name="attn_bwd", grid=(npair, nsb), out_shape=jax.ShapeDtypeStruct((4, s, e), BF16),
        in_specs=[blk(zc(5)), blk(zc(6)), blk(z0(4)), blk(zn(4)), blk(ec(npair)), blk(en(npair)),
                  blk(zc(7)), blk(zn(7)), blk(ec(0)), blk(en(0)), blk(e0(0)), blk(en(0))],
        out_specs=pl.BlockSpec((4, T, LANES), lambda hp, sb: (0, sb, hp)),
        scratch_shapes=[pltpu.VMEM((2, 2 * T, LANES), F32), pltpu.VMEM((2, 2 * T, LANES), F32),
                        pltpu.VMEM((2, T, LANES), F32), pltpu.VMEM((2, T, LANES), F32),
                        buf(2 * T), buf(T), buf(T), pltpu.VMEM((12, 2 * W, W), F32)],
        compiler_params=_params("parallel", "arbitrary"))(z, z, z, z, dy, dy, z, z, o, o, lse, lse)


def _dz_specs(tm, e):
    def mk(lo, hi):
        return pl.BlockSpec((None, tm, e), lambda i, k: (jnp.clip(k - lo, 0, hi - lo - 1), i, 0))
    return [mk(0, 4), mk(4, 8)]


def _dz_pick(grp, dzh_ref, dza_ref, fn):
    @pl.when(grp < 4)
    def _():
        fn(dzh_ref[...])

    @pl.when(grp >= 4)
    def _():
        fn(dza_ref[...])


def _dh_dx(dzh, dza, w_full, x, gain, dx2):
    s, d = x.shape
    e = dzh.shape[2]
    tm = _tile(s, 1024)
    ni = s // tm
    chunk = _tile(tm, 128)
    fetch_at = 2

    def body(dzh_ref, dza_ref, w_ref, x_hbm, g_ref, dx2_hbm, gx_hbm, dg_ref, acc, xbuf, dbuf, sems):
        i, k = pl.program_id(0), pl.program_id(1)
        rows_of = lambda tile: pl.ds(pl.multiple_of(tile * tm, tm), tm)
        fetch_x = pltpu.make_async_copy(x_hbm.at[rows_of(i), :], xbuf, sems.at[0])
        fetch_d = pltpu.make_async_copy(dx2_hbm.at[rows_of(i), :], dbuf, sems.at[1])

        def store(tile):
            return pltpu.make_async_copy(xbuf, gx_hbm.at[rows_of(tile), :], sems.at[2])

        @pl.when((i == 0) & (k == 0))
        def _():
            dg_ref[...] = jnp.zeros_like(dg_ref)

        @pl.when((k == fetch_at) & (i > 0))
        def _():
            store(i).wait()

        @pl.when(k == fetch_at)
        def _():
            fetch_x.start()
            fetch_d.start()

        def finish(tile):
            fetch_x.wait()
            fetch_d.wait()
            gain_row = g_ref[...]
            for r0 in range(0, tm, chunk):
                rows = slice(r0, r0 + chunk)
                dh = acc[rows, :]
                xv = xbuf[rows, :]
                r = lax.rsqrt(jnp.mean(xv * xv, axis=-1, keepdims=True) + NORM_EPS)
                xn = xv * r
                u = dh * gain_row
                xbuf[rows, :] = dbuf[rows, :] + r * (u - xn * jnp.mean(u * xn, axis=-1, keepdims=True))
                dg_ref[...] += jnp.sum(dh * xn, axis=0, keepdims=True)
            store(tile).start()

        @pl.when((k == 0) & (i == 0))
        def _():
            acc[...] = _dot_nt(dzh_ref[...], w_ref[...])

        @pl.when((k == 0) & (i > 0))
        def _():
            finish(i - 1)
            acc[...] = _dot_nt(dzh_ref[...], w_ref[...])

        @pl.when(k > 0)
        def _():
            def add(dz):
                acc[...] += _dot_nt(dz, w_ref[...])

            _dz_pick(k, dzh_ref, dza_ref, add)

        @pl.when((k == N_SPLITS - 1) & (i == ni - 1))
        def _():
            finish(i)
            store(i).wait()

    vec = pl.BlockSpec((1, d), lambda i, k: (0, 0))
    return pl.pallas_call(
        body, name="dh_dx", grid=(ni, N_SPLITS),
        out_shape=(jax.ShapeDtypeStruct((s, d), F32), jax.ShapeDtypeStruct((1, d), F32)),
        in_specs=_dz_specs(tm, e) + [pl.BlockSpec((None, d, e), lambda i, k: (k, 0, 0)), ANY, vec, ANY],
        out_specs=(ANY, vec),
        scratch_shapes=[pltpu.VMEM((tm, d), F32)] * 3 + [pltpu.SemaphoreType.DMA((3,))],
        compiler_params=pltpu.CompilerParams(dimension_semantics=("arbitrary", "arbitrary"),
                                             vmem_limit_bytes=DHDX_VMEM_LIMIT))(dzh, dza, w_full, x, gain, dx2)


def _position():
    x, y, c = lax.axis_index("x"), lax.axis_index("y"), lax.axis_index("c")
    return x, y, c


def _xor_peer(x, y, c, mask):
    return (x ^ ((mask >> 2) & 1), y ^ ((mask >> 1) & 1), c ^ (mask & 1))


def _block_order(masks):
    me = 4 * lax.axis_index("x") + 2 * lax.axis_index("y") + lax.axis_index("c")
    return jnp.stack([me ^ m for m in masks]).astype(jnp.int32)


GATHER_MASKS = (0, 1, 4, 5, 2, 3, 6, 7)


def _inproj_gather(h, w_loc, wo_loc):
    s, d = h.shape
    e = w_loc.shape[1]
    tm = _tile(s, 1024)
    ni = s // tm
    pre = max(ni - 2, 0)

    def body(order_ref, h_ref, w_ref, wo_ref, z_ref, wf_ref, wof_ref, wbuf, send_sems, recv_sems, osend, orecv,
             local_sems, wsems):
        j, i = pl.program_id(0), pl.program_id(1)
        x, y, c = _position()
        me, sibling = (x, y, c), (x, y, 1 - c)
        chips = [(1 - x, y), (x, 1 - y), (1 - x, 1 - y)]
        blk = lambda p: 4 * p[0] + 2 * p[1] + p[2]

        def copy(k, block, to, src=None):
            dst = wf_ref.at[blk(block)]
            return pltpu.make_async_remote_copy(
                src_ref=dst if src is None else src, dst_ref=dst, send_sem=send_sems.at[k], recv_sem=recv_sems.at[k],
                device_id=to, device_id_type=MESH)

        first = [copy(0, me, sibling, src=w_ref)] + [copy(1 + q, me, (*chip, c), src=w_ref) for q, chip in enumerate(chips)]
        passed = [copy(4 + q, (*chip, c), sibling) for q, chip in enumerate(chips)]
        mine = pltpu.make_async_copy(w_ref, wf_ref.at[blk(me)], local_sems.at[0])
        ocopies = [pltpu.make_async_remote_copy(
            src_ref=wo_ref, dst_ref=wof_ref.at[blk(me)], send_sem=osend.at[k], recv_sem=orecv.at[k],
            device_id=_xor_peer(x, y, c, k + 1), device_id_type=MESH) for k in range(N_DEV - 1)]
        omine = pltpu.make_async_copy(wo_ref, wof_ref.at[blk(me)], local_sems.at[1])
        blocks = [me, sibling] + [(*chip, c) for chip in chips] + [(*chip, 1 - c) for chip in chips]
        arrive = [None, copy(0, sibling, me)] + [copy(1 + q, (*chip, c), me) for q, chip in enumerate(chips)] \
            + [copy(4 + q, (*chip, 1 - c), me) for q, chip in enumerate(chips)]
        forward = [None, None] + passed + [None, None, None]
        use_order = (0, 1, 2, 5, 3, 6, 4, 7)
        blocks, arrive, forward = ([lst[n] for n in use_order] for lst in (blocks, arrive, forward))

        def load(slot, src):
            return pltpu.make_async_copy(src, wbuf.at[slot], wsems.at[slot])

        @pl.when((j == 0) & (i == 0))
        def _():
            for cp in [mine, omine] + first + ocopies:
                cp.start()
            load(0, w_ref).start()

        for jj in range(N_DEV):
            @pl.when((j == jj) & (i == 0))
            def _():
                load(jj % 2, w_ref).wait()

            if jj + 1 < N_DEV:
                @pl.when((j == jj) & (i == pre))
                def _():
                    arrive[jj + 1].wait_recv()
                    if forward[jj + 1] is not None:
                        forward[jj + 1].start()
                    load((jj + 1) % 2, wf_ref.at[blk(blocks[jj + 1])]).start()

        z_ref[...] = _dot(h_ref[...], wbuf[j % 2])

        @pl.when((j == N_DEV - 1) & (i == ni - 1))
        def _():
            for cp in first + passed:
                cp.wait_send()
            for cp in ocopies:
                cp.wait_send()
                cp.wait_recv()
            mine.wait()
            omine.wait()

    grid_spec = pltpu.PrefetchScalarGridSpec(
        num_scalar_prefetch=1, grid=(N_DEV, ni),
        in_specs=[pl.BlockSpec((tm, d), lambda j, i, o: (i, 0)), ANY, ANY],
        out_specs=(pl.BlockSpec((tm, e), lambda j, i, o: (i, o[j])), ANY, ANY),
        scratch_shapes=[pltpu.VMEM((2, d, e), BF16), pltpu.SemaphoreType.DMA((7,)), pltpu.SemaphoreType.DMA((7,)),
                        pltpu.SemaphoreType.DMA((7,)), pltpu.SemaphoreType.DMA((7,)), pltpu.SemaphoreType.DMA((2,)),
                        pltpu.SemaphoreType.DMA((2,))])
    return pl.pallas_call(
        body, name="inproj_gather", grid_spec=grid_spec,
        out_shape=(jax.ShapeDtypeStruct((s, N_SPLITS * e), F32), jax.ShapeDtypeStruct((N_DEV, d, e), BF16),
                   jax.ShapeDtypeStruct((N_DEV,) + wo_loc.shape, BF16)),
        compiler_params=_params("arbitrary", "arbitrary"))(_block_order(GATHER_MASKS), h, w_loc, wo_loc)


SCATTER_MASKS = (7, 6, 5, 4, 3, 2, 1, 0)
N_CHIPS = 4


def _scatter_block(k, acc, stage, tmp, own_ref, ra_ref, rb_ref, sa_send, sa_recv, sb_send, sb_recv, loc_sem, step, ns):
    x, y, c = _position()
    chip_of = lambda t: _xor_peer(x, y, c, SCATTER_MASKS[2 * t + 1])
    last = step == ns - 1
    fetch_at = min(1, ns - 1)

    def ship(t):
        return pltpu.make_async_remote_copy(
            src_ref=stage.at[0], dst_ref=ra_ref.at[t], send_sem=sa_send.at[t], recv_sem=sa_recv.at[t],
            device_id=(x, y, 1 - c), device_id_type=MESH)

    def send(t):
        return pltpu.make_async_remote_copy(
            src_ref=stage.at[1], dst_ref=rb_ref.at[t], send_sem=sb_send.at[t], recv_sem=sb_recv.at[t],
            device_id=chip_of(t), device_id_type=MESH)

    for kk in range(N_DEV):
        t = kk // 2
        fetch = pltpu.make_async_copy(ra_ref.at[t], tmp, loc_sem)

        if kk % 2 == 1:
            @pl.when((step == fetch_at) & (k == kk))
            def _():
                ship(t).wait_recv()
                fetch.start()

        @pl.when(last & (k == kk))
        def _():
            if kk % 2 == 0:
                if t >= 1:
                    ship(t - 1).wait_send()
                stage[0] = acc[...].astype(BF16)
                ship(t).start()
            else:
                fetch.wait()
                acc[...] += tmp[...].astype(F32)
                if t < N_CHIPS - 1:
                    if t >= 1:
                        send(t - 1).wait_send()
                    stage[1] = acc[...].astype(BF16)
                    send(t).start()
                else:
                    keep = pltpu.make_async_copy(acc, own_ref, loc_sem)
                    keep.start()
                    keep.wait()
                    ship(t).wait_send()
                    send(t - 1).wait_send()
                    for q in range(N_CHIPS - 1):
                        send(q).wait_recv()


def _scatter_scratch(rows, cols):
    return [pltpu.VMEM((rows, cols), F32), pltpu.VMEM((2, rows, cols), BF16), pltpu.VMEM((rows, cols), BF16),
            pltpu.SemaphoreType.DMA((N_CHIPS,)), pltpu.SemaphoreType.DMA((N_CHIPS,)),
            pltpu.SemaphoreType.DMA((N_CHIPS - 1,)), pltpu.SemaphoreType.DMA((N_CHIPS - 1,)), pltpu.SemaphoreType.DMA(())]


def _scatter_out(rows, cols):
    return (jax.ShapeDtypeStruct((rows, cols), F32), jax.ShapeDtypeStruct((N_CHIPS, rows, cols), BF16),
            jax.ShapeDtypeStruct((N_CHIPS - 1, rows, cols), BF16))


def _dwin_scatter(h, dzh, dza):
    s, d = h.shape
    e = dzh.shape[2]
    ts = _tile(s, 1024)
    ns = s // ts

    def body(order_ref, dzh_ref, dza_ref, h_ref, own_ref, ra_ref, rb_ref, acc, stage, tmp, *sems):
        k, step = pl.program_id(0), pl.program_id(1)

        @pl.when(step == 0)
        def _():
            acc[...] = jnp.zeros_like(acc)

        def add(dz):
            acc[...] += _dot_tn(h_ref[...], dz)

        _dz_pick(order_ref[k], dzh_ref, dza_ref, add)
        _scatter_block(k, acc, stage, tmp, own_ref, ra_ref, rb_ref, *sems, step, ns)

    def dz_spec(lo):
        return pl.BlockSpec((None, ts, e), lambda k, st, o: (jnp.clip(o[k] - lo, 0, 3), st, 0))

    grid_spec = pltpu.PrefetchScalarGridSpec(
        num_scalar_prefetch=1, grid=(N_DEV, ns),
        in_specs=[dz_spec(0), dz_spec(4), pl.BlockSpec((ts, d), lambda k, st, o: (st, 0))],
        out_specs=(ANY, ANY, ANY), scratch_shapes=_scatter_scratch(d, e))
    own, _, rb = pl.pallas_call(
        body, name="dwin_scatter", grid_spec=grid_spec, out_shape=_scatter_out(d, e),
        compiler_params=_params("arbitrary", "arbitrary"))(_block_order(SCATTER_MASKS), dzh, dza, h)
    return own, rb


def _dwout_scatter(y_h, y_a, dxb):
    s, e = y_h.shape
    d = dxb.shape[1]
    r = 2 * e // N_DEV
    pairs = e // (2 * r)
    ts = _tile(s, 1024)
    ns = s // ts
    chip_masks = SCATTER_MASKS[1::2]
    passes = ((0, 1), (2,), (3,))
    slots = max(len(chips) for chips in passes)
    slot_chip = [chips[min(u, len(chips) - 1)] for chips in passes for u in range(slots)]

    def body(pair_ref, yh0_ref, ya0_ref, yh1_ref, ya1_ref, dx_ref, own_ref, ra_ref, rb_ref, acc, keep_buf, ship_buf,
             send_buf, tmp, sa_send, sa_recv, sb_send, sb_recv, loc_sem):
        p, step = pl.program_id(0), pl.program_id(1)
        x, y, c = _position()

        @pl.when(step == 0)
        def _():
            acc[...] = jnp.zeros_like(acc)

        for u, (yh_ref, ya_ref) in enumerate(((yh0_ref, ya0_ref), (yh1_ref, ya1_ref))):
            rows = slice(u * 2 * r, (u + 1) * 2 * r)
            used = functools.reduce(jnp.logical_or, [p == pp for pp, chips in enumerate(passes) if u < len(chips)])

            @pl.when(used & (pair_ref[slots * p + u] < pairs))
            def _():
                acc[rows, :] += _dot_tn(yh_ref[...], dx_ref[...])

            @pl.when(used & (pair_ref[slots * p + u] >= pairs))
            def _():
                acc[rows, :] += _dot_tn(ya_ref[...], dx_ref[...])

        def block_rows(u, core):
            return pl.ds(pl.multiple_of(u * 2 * r + core * r, r), r)

        slot_of = {q: u for chips in passes for u, q in enumerate(chips)}

        def ship(q):
            return pltpu.make_async_remote_copy(
                src_ref=ship_buf.at[slot_of[q]], dst_ref=ra_ref.at[q], send_sem=sa_send.at[q], recv_sem=sa_recv.at[q],
                device_id=(x, y, 1 - c), device_id_type=MESH)

        def send(q):
            return pltpu.make_async_remote_copy(
                src_ref=send_buf.at[slot_of[q]], dst_ref=rb_ref.at[q], send_sem=sb_send.at[q], recv_sem=sb_recv.at[q],
                device_id=_xor_peer(x, y, c, chip_masks[q]), device_id_type=MESH)

        def sibling_share(q):
            ship(q).wait_recv()
            fetch = pltpu.make_async_copy(ra_ref.at[q], tmp, loc_sem)
            fetch.start()
            fetch.wait()
            return tmp[...].astype(F32)

        shipped, sent = {}, {}
        for pp, chips in enumerate(passes):
            @pl.when((step == ns - 1) & (p == pp))
            def _():
                for u, q in enumerate(chips):
                    if u in shipped:
                        ship(shipped.pop(u)).wait_send()
                    ship_buf[u] = acc[block_rows(u, 1 - c), :].astype(BF16)
                    ship(q).start()
                    shipped[u] = q
                for u, q in enumerate(chips):
                    total = acc[block_rows(u, c), :] + sibling_share(q)
                    if q < N_CHIPS - 1:
                        if u in sent:
                            send(sent.pop(u)).wait_send()
                        send_buf[u] = total.astype(BF16)
                        send(q).start()
                        sent[u] = q
                    else:
                        keep_buf[...] = total
                        keep = pltpu.make_async_copy(keep_buf, own_ref, loc_sem)
                        keep.start()
                        keep.wait()
                if pp == len(passes) - 1:
                    for q in shipped.values():
                        ship(q).wait_send()
                    for q in sent.values():
                        send(q).wait_send()
                    for q in range(N_CHIPS - 1):
                        send(q).wait_recv()

    def y_spec(u, lo):
        return pl.BlockSpec((ts, 2 * r), lambda p, st, o: (st, jnp.clip(o[slots * p + u] - lo, 0, pairs - 1)))

    pair_of_chip = _block_order(chip_masks) // 2
    grid_spec = pltpu.PrefetchScalarGridSpec(
        num_scalar_prefetch=1, grid=(len(passes), ns),
        in_specs=[y_spec(0, 0), y_spec(0, pairs), y_spec(1, 0), y_spec(1, pairs),
                  pl.BlockSpec((ts, d), lambda p, st, o: (st, 0))],
        out_specs=(ANY, ANY, ANY),
        scratch_shapes=[pltpu.VMEM((slots * 2 * r, d), F32), pltpu.VMEM((r, d), F32),
                        pltpu.VMEM((slots, r, d), BF16)] + _scatter_scratch(r, d)[1:])
    own, _, rb = pl.pallas_call(
        body, name="dwout_scatter", grid_spec=grid_spec, out_shape=_scatter_out(r, d),
        compiler_params=_params("arbitrary", "arbitrary"))(
            jnp.stack([pair_of_chip[q] for q in slot_chip]), y_h, y_a, y_h, y_a, dxb)
    return own, rb


def _sum_chips_adamw(own, recv, w, m, v):
    r, c = w.shape
    tr = _tile(r, 128)

    def body(own_ref, rc_ref, w_ref, m_ref, v_ref, g_ref, d_ref, mo_ref, vo_ref):
        g = own_ref[...]
        for q in range(N_CHIPS - 1):
            g = g + rc_ref[q].astype(F32)
        g_ref[...] = g
        d_ref[...], mo_ref[...], vo_ref[...] = _adamw(w_ref[...], g, m_ref[...], v_ref[...])

    blk = pl.BlockSpec((tr, c), lambda i: (i, 0))
    shp = jax.ShapeDtypeStruct((r, c), F32)
    return pl.pallas_call(
        body, name="sum_chips_adamw", grid=(r // tr,), out_shape=(shp, shp, shp, shp),
        in_specs=[blk, pl.BlockSpec((N_CHIPS - 1, tr, c), lambda i: (0, i, 0)), blk, blk, blk],
        out_specs=(blk, blk, blk, blk), compiler_params=_params("parallel"))(own, recv, w, m, v)


SMALL_ROWS = 8
ROW_LB = 4
ROW_GN = 6
ROW_LOSS = 7


def _small_allreduce_adamw(part, w, m, v, lb_logits):
    width = part.shape[1]

    def body(p_ref, w_ref, m_ref, v_ref, lb_ref, g_ref, d_ref, mo_ref, vo_ref, buf, send_sems, recv_sems):
        x, y, c = _position()
        me = 4 * x + 2 * y + c
        buf[me] = p_ref[...]
        copies = []
        for k in range(N_DEV - 1):
            bx, by, bc = ((k + 1) >> 2) & 1, ((k + 1) >> 1) & 1, (k + 1) & 1
            peer = (x ^ bx, y ^ by, c ^ bc)
            copies.append(pltpu.make_async_remote_copy(
                src_ref=p_ref, dst_ref=buf.at[me], send_sem=send_sems.at[k], recv_sem=recv_sems.at[k],
                device_id=peer, device_id_type=MESH))
        for cp in copies:
            cp.start()
        for cp in copies:
            cp.wait_recv()
        for cp in copies:
            cp.wait_send()
        tot = buf[0]
        for dev in range(1, N_DEV):
            tot = tot + buf[dev]
        lbv = lb_ref[...]
        lb = _sigmoid(lbv[0:1] - lbv[1:2])
        glb = tot[ROW_LB:ROW_LB + 1] * lb * (1.0 - lb)
        loss = jnp.sum(tot[ROW_LOSS:ROW_LOSS + 1], axis=-1, keepdims=True)
        row = lax.broadcasted_iota(jnp.int32, (SMALL_ROWS, width), 0)
        g = jnp.where(row == ROW_LB, glb, jnp.where(row == ROW_LB + 1, -glb, tot))
        g = jnp.where(row == ROW_LOSS, loss, g)
        g_ref[...] = g
        d_ref[...], mo_ref[...], vo_ref[...] = _adamw(w_ref[...], g, m_ref[...], v_ref[...])

    vm = pl.BlockSpec(memory_space=pltpu.VMEM)
    shp = jax.ShapeDtypeStruct((SMALL_ROWS, width), F32)
    return pl.pallas_call(
        body, name="small_allreduce_adamw", out_shape=(shp, shp, shp, shp),
        in_specs=[vm] * 5, out_specs=(vm, vm, vm, vm),
        scratch_shapes=[pltpu.VMEM((N_DEV, SMALL_ROWS, width), F32), pltpu.SemaphoreType.DMA((N_DEV - 1,)),
                        pltpu.SemaphoreType.DMA((N_DEV - 1,))],
    )(part, w, m, v, lb_logits)


def _pack_small(norm_gain, final_gain, lb2, gnorm, last_row, width):
    pad = lambda a: jnp.pad(a.reshape(1, -1), ((0, 0), (0, width - a.size)))
    return jnp.concatenate([norm_gain.reshape(2, width), final_gain.reshape(2, width), lb2.reshape(2, width),
                            pad(gnorm), last_row.reshape(1, width)], axis=0)


def _unpack_small(p, d, e, hd):
    return (p[0:2].reshape(1, d), p[2:4].reshape(d), p[4:6].reshape(2, e), p[6:7, :hd].reshape(1, hd))


def kernel(x, norm_gain, w_in, lb_logits, hgrn_gnorm, w_out, final_gain, loss_target, m_norm_gain, m_w_in, m_lb_logits, m_hgrn_gnorm, m_w_out, m_final_gain, v_norm_gain, v_w_in, v_lb_logits, v_hgrn_gnorm, v_w_out, v_final_gain):
    s, d = x.shape[1], x.shape[2]
    e = w_in.shape[2]
    assert d == 2 * e and lb_logits.shape == (2, e) and w_out.shape[1] * N_DEV == 2 * e
    x2d = x.reshape(s, d)
    tgt = loss_target.reshape(s, d)

    h = _rmsnorm_fwd(x2d, norm_gain)
    z, w_in_full, w_out_full = _inproj_gather(h, _cast_bf16(w_in[0]), _cast_bf16(w_out[0]))
    w_out_full = w_out_full.reshape(2 * e, d)
    y_h, states = _hgrn_fwd(z, lb_logits, hgrn_gnorm)
    o_attn, lse, y_a = _attn_fwd(z)
    dx2, dx2b, dy, loss_vec, dfg = _outproj_loss(x2d, y_h, y_a, w_out_full, final_gain.reshape(1, d), tgt)

    own_o, recv_o = _dwout_scatter(y_h, y_a, dx2b)
    dza = _attn_bwd(z, dy, o_attn, lse)
    dzh, dlb, dgn = _hgrn_bwd(z, dy, states, lb_logits, hgrn_gnorm)
    grad_x, dng = _dh_dx(dzh, dza, w_in_full, x2d, norm_gain, dx2)
    g_wo, d_wo, nm_wo, nv_wo = _sum_chips_adamw(own_o, recv_o, w_out[0], m_w_out[0], v_w_out[0])

    width = d // 2
    zero_row = jnp.zeros((1, width), F32)
    loss_row = loss_vec[:, :width] + loss_vec[:, width:]
    part = _pack_small(dng, dfg, jnp.concatenate([dlb, zero_row], axis=0), dgn, loss_row, width)
    pw = _pack_small(norm_gain, final_gain, lb_logits, hgrn_gnorm, zero_row, width)
    pm = _pack_small(m_norm_gain, m_final_gain, m_lb_logits, m_hgrn_gnorm, zero_row, width)
    pv = _pack_small(v_norm_gain, v_final_gain, v_lb_logits, v_hgrn_gnorm, zero_row, width)
    sg, sd, sm, sv = _small_allreduce_adamw(part, pw, pm, pv, lb_logits)
    own_i, recv_i = _dwin_scatter(h, dzh, dza)
    g_wi, d_wi, nm_wi, nv_wi = _sum_chips_adamw(own_i, recv_i, w_in[0], m_w_in[0], v_w_in[0])
    hd = hgrn_gnorm.shape[1]
    g_ng, g_fg, g_lb, g_gn = _unpack_small(sg, d, e, hd)
    d_ng, d_fg, d_lb, d_gn = _unpack_small(sd, d, e, hd)
    m_ng, m_fg, m_lb, m_gn = _unpack_small(sm, d, e, hd)
    v_ng, v_fg, v_lb, v_gn = _unpack_small(sv, d, e, hd)
    loss = sg[ROW_LOSS, 0]

    one = lambda a: a[None]
    return (loss, grad_x.reshape(1, s, d), g_ng, one(g_wi), g_lb, g_gn, one(g_wo), g_fg,
            d_ng, one(d_wi), d_lb, d_gn, one(d_wo), d_fg,
            m_ng, one(nm_wi), m_lb, m_gn, one(nm_wo), m_fg,
            v_ng, one(nv_wi), v_lb, v_gn, one(nv_wo), v_fg)
```

```python
import functools
import math

import jax
import jax.numpy as jnp
from jax import lax
from jax.experimental import pallas as pl
from jax.experimental.pallas import tpu as pltpu

NORM_EPS = 1e-6
HGRN_HEAD = 128
HGRN_CHUNK = 64
ATTN_HEAD = 64
ATTN_BAND = 128
DILATIONS = (1, 4, 16)
N_SPLITS = 8
N_DEV = 8
ADAM_LR = 0.001
ADAM_B1 = 0.9
ADAM_B2 = 0.999
ADAM_EPS = 1e-08
ADAM_WD = 0.01
ADAM_STEP = 10
LANES = 128
MESH = pl.DeviceIdType.MESH
F32 = jnp.float32
BF16 = jnp.bfloat16
NEG_BIG = -1e30
VMEM_LIMIT = 56 * 1024 * 1024
OUTPROJ_VMEM_LIMIT = 63 * 1024 * 1024
DHDX_VMEM_LIMIT = 60 * 1024 * 1024

ANY = pl.BlockSpec(memory_space=pl.ANY)


def _params(*sem):
    return pltpu.CompilerParams(dimension_semantics=sem, vmem_limit_bytes=VMEM_LIMIT)


def _tile(n, pref):
    t = min(n, pref)
    assert n % t == 0, (n, pref)
    return t


def _dot(a, b, precision=None):
    return jnp.dot(a, b, preferred_element_type=F32, precision=precision)


def _dot_nt(a, b):
    return lax.dot_general(a, b, (((1,), (1,)), ((), ())), preferred_element_type=F32)


def _dot_tn(a, b):
    return lax.dot_general(a, b, (((0,), (0,)), ((), ())), preferred_element_type=F32)


def _sigmoid(x):
    return 0.5 * jnp.tanh(0.5 * x) + 0.5


def _dsilu(x, s):
    return s * (1.0 + x * (1.0 - s))


def _adamw(w, g, m, v):
    m = ADAM_B1 * m + (1.0 - ADAM_B1) * g
    v = ADAM_B2 * v + (1.0 - ADAM_B2) * (g * g)
    m_hat = m / (1.0 - ADAM_B1 ** ADAM_STEP)
    v_hat = v / (1.0 - ADAM_B2 ** ADAM_STEP)
    delta = -ADAM_LR * (m_hat / (jnp.sqrt(v_hat) + ADAM_EPS) + ADAM_WD * w)
    return delta, m, v


def _cast_bf16(a):
    r, c = a.shape
    tr = _tile(r, 256)

    def body(a_ref, o_ref):
        o_ref[...] = a_ref[...].astype(BF16)

    return pl.pallas_call(
        body, name="cast_bf16", grid=(r // tr,), out_shape=jax.ShapeDtypeStruct((r, c), BF16),
        in_specs=[pl.BlockSpec((tr, c), lambda i: (i, 0))], out_specs=pl.BlockSpec((tr, c), lambda i: (i, 0)),
        compiler_params=_params("parallel"))(a)


def _rmsnorm_fwd(x, gain):
    s, d = x.shape
    tm = _tile(s, 512)

    def body(x_ref, g_ref, h_ref):
        xv = x_ref[...]
        r = lax.rsqrt(jnp.mean(xv * xv, axis=-1, keepdims=True) + NORM_EPS)
        h_ref[...] = (xv * r * g_ref[...]).astype(BF16)

    return pl.pallas_call(
        body, name="rmsnorm_fwd", grid=(s // tm,), out_shape=jax.ShapeDtypeStruct((s, d), BF16),
        in_specs=[pl.BlockSpec((tm, d), lambda i: (i, 0)), pl.BlockSpec((1, d), lambda i: (0, 0))],
        out_specs=pl.BlockSpec((tm, d), lambda i: (i, 0)), compiler_params=_params("parallel"))(x, gain)


HGRN_BLOCK = 2048
TRI_ROWS = 64


def _chunk_masks():
    tb = TRI_ROWS
    row = lax.broadcasted_iota(jnp.int32, (tb, tb), 0)
    col = lax.broadcasted_iota(jnp.int32, (tb, tb), 1)
    same = (row // HGRN_CHUNK) == (col // HGRN_CHUNK)
    lower = jnp.where(same & (col <= row), 1.0, 0.0).astype(BF16)
    upper = jnp.where(same & (col >= row), 1.0, 0.0).astype(BF16)
    return lower, upper


def _split3(a):
    hi = a.astype(BF16).astype(F32)
    mid = (a - hi).astype(BF16).astype(F32)
    lo = (a - hi - mid).astype(BF16).astype(F32)
    return hi, mid, lo


def _tri_dot(tri, x):
    hi, mid, lo = (p.astype(BF16) for p in _split3(x))
    outs = []
    for r in range(0, x.shape[0], TRI_ROWS):
        sl = slice(r, r + TRI_ROWS)
        outs.append(_dot(tri, hi[sl]) + _dot(tri, mid[sl]) + _dot(tri, lo[sl]))
    return outs[0] if len(outs) == 1 else jnp.concatenate(outs, axis=0)


def _hgrn_gates(qp, fp, lbv):
    lb = _sigmoid(lbv[0:1] - lbv[1:2])
    sq = _sigmoid(qp)
    q = qp * sq
    sg = _sigmoid(fp)
    f = lb + (1.0 - lb) * sg
    k = 1.0 - f
    return lb, sq, q, sg, f, k


def _hgrn_fwd(z, lb_logits, gnorm):
    s = z.shape[0]
    e = z.shape[1] // N_SPLITS
    nh = e // HGRN_HEAD
    tb = _tile(s, HGRN_BLOCK)
    nc = tb // HGRN_CHUNK
    nb = s // tb
    C = HGRN_CHUNK

    def body(q_ref, f_ref, i_ref, g_ref, lb_ref, gn_ref, y_ref, st_ref, state, o_scr):
        @pl.when(pl.program_id(1) == 0)
        def _():
            state[...] = jnp.zeros_like(state)

        lb, sq, q, sg, f, k = _hgrn_gates(q_ref[...], f_ref[...], lb_ref[...])
        lower, _ = _chunk_masks()
        b = _tri_dot(lower, jnp.log(f))
        b3 = b.reshape(nc, C, HGRN_HEAD)
        bc = b3[:, C - 1:C, :]
        qt = (q * jnp.exp(b)).astype(BF16)
        kt = (k * jnp.exp(-b)).astype(BF16)
        ke = (k.reshape(nc, C, HGRN_HEAD) * jnp.exp(bc - b3)).reshape(tb, HGRN_HEAD).astype(BF16)
        v = i_ref[...].astype(BF16)
        tri = lax.broadcasted_iota(jnp.int32, (C, C), 1) <= lax.broadcasted_iota(jnp.int32, (C, C), 0)
        sls = [slice(c * C, (c + 1) * C) for c in range(nc)]
        kv = [_dot_tn(v[sl], ke[sl]) for sl in sls]
        a = [jnp.where(tri, _dot_nt(qt[sl], kt[sl]), 0.0).astype(BF16) for sl in sls]
        st = state[...]
        sts = []
        for c in range(nc):
            sts.append(st)
            st_ref[c] = st
            st = st * jnp.exp(bc[c]) + kv[c]
        state[...] = st
        for c, sl in enumerate(sls):
            o_scr[sl, :] = _dot(a[c], v[sl]) + _dot_nt(qt[sl], sts[c].astype(BF16))
        o = o_scr[...]
        rms = lax.rsqrt(jnp.mean(o * o, axis=-1, keepdims=True) + NORM_EPS)
        gp = g_ref[...]
        y_ref[...] = (o * rms * gn_ref[...] * (gp * _sigmoid(gp))).astype(BF16)

    col = lambda kk: (lambda h, n: (n, kk * nh + h))
    return pl.pallas_call(
        body, name="hgrn_fwd", grid=(nh, nb),
        out_shape=(jax.ShapeDtypeStruct((s, e), BF16),
                   jax.ShapeDtypeStruct((nh, s // C, HGRN_HEAD, HGRN_HEAD), F32)),
        in_specs=[pl.BlockSpec((tb, HGRN_HEAD), col(0)), pl.BlockSpec((tb, HGRN_HEAD), col(1)),
                  pl.BlockSpec((tb, HGRN_HEAD), col(2)), pl.BlockSpec((tb, HGRN_HEAD), col(3)),
                  pl.BlockSpec((2, HGRN_HEAD), lambda h, n: (0, h)), pl.BlockSpec((1, HGRN_HEAD), lambda h, n: (0, 0))],
        out_specs=(pl.BlockSpec((tb, HGRN_HEAD), lambda h, n: (n, h)),
                   pl.BlockSpec((None, nc, HGRN_HEAD, HGRN_HEAD), lambda h, n: (h, n, 0, 0))),
        scratch_shapes=[pltpu.VMEM((HGRN_HEAD, HGRN_HEAD), F32), pltpu.VMEM((tb, HGRN_HEAD), F32)],
        compiler_params=_params("parallel", "arbitrary"))(z, z, z, z, lb_logits, gnorm)


def _hgrn_bwd(z, dy, states, lb_logits, gnorm):
    s = z.shape[0]
    e = z.shape[1] // N_SPLITS
    nh = e // HGRN_HEAD
    tb = _tile(s, HGRN_BLOCK)
    nc = tb // HGRN_CHUNK
    nb = s // tb
    C = HGRN_CHUNK
    H = HGRN_HEAD

    def body(q_ref, f_ref, i_ref, g_ref, dy_ref, st_ref, lb_ref, gn_ref, dz_ref, dlb_ref, dgn_ref,
             gstate, o_scr, dq_scr, dk_scr, dv_scr, e_scr):
        first = (pl.program_id(0) == 0) & (pl.program_id(1) == 0)

        @pl.when(first)
        def _():
            dgn_ref[...] = jnp.zeros_like(dgn_ref)

        @pl.when(pl.program_id(1) == 0)
        def _():
            gstate[...] = jnp.zeros_like(gstate)
            dlb_ref[...] = jnp.zeros_like(dlb_ref)

        qp = q_ref[...]
        lb, sq, q, sg, f, k = _hgrn_gates(qp, f_ref[...], lb_ref[...])
        lower, upper = _chunk_masks()
        b = _tri_dot(lower, jnp.log(f))
        b3 = b.reshape(nc, C, H)
        bc = b3[:, C - 1:C, :]
        eb = jnp.exp(b)
        enb = jnp.exp(-b)
        eend = jnp.exp(bc - b3).reshape(tb, H)
        qt = (q * eb).astype(BF16)
        kt = (k * enb).astype(BF16)
        ke = (k * eend).astype(BF16)
        v = i_ref[...].astype(BF16)
        tri = lax.broadcasted_iota(jnp.int32, (C, C), 1) <= lax.broadcasted_iota(jnp.int32, (C, C), 0)
        sls = [slice(c * C, (c + 1) * C) for c in range(nc)]
        a = [jnp.where(tri, _dot_nt(qt[sl], kt[sl]), 0.0).astype(BF16) for sl in sls]
        for c, sl in enumerate(sls):
            o_scr[sl, :] = _dot(a[c], v[sl]) + _dot_nt(qt[sl], st_ref[c].astype(BF16))
        o = o_scr[...]
        rms = lax.rsqrt(jnp.mean(o * o, axis=-1, keepdims=True) + NORM_EPS)
        on = o * rms
        gn = gn_ref[...]
        gp = g_ref[...]
        sgg = _sigmoid(gp)
        dyv = dy_ref[...]
        d_on = dyv * (gp * sgg)
        dz_ref[3] = (dyv * on * gn * _dsilu(gp, sgg)).astype(BF16)
        dgn_ref[...] += jnp.sum(d_on * on, axis=0, keepdims=True)
        u = d_on * gn
        do = (rms * (u - on * jnp.mean(u * on, axis=-1, keepdims=True))).astype(BF16)
        gup = [_dot_tn(do[sl], qt[sl]) for sl in sls]
        da = [jnp.where(tri, _dot_nt(do[sl], v[sl]), 0.0).astype(BF16) for sl in sls]
        gt = gstate[...]
        gts = [None] * nc
        for c in reversed(range(nc)):
            gts[c] = gt
            gt = gt * jnp.exp(bc[c]) + gup[c]
        gstate[...] = gt
        for c, sl in enumerate(sls):
            stp = st_ref[c]
            gtb = gts[c].astype(BF16)
            dqt = _dot(da[c], kt[sl]) + _dot(do[sl], stp.astype(BF16))
            dkt = _dot_tn(da[c], qt[sl])
            dks = _dot(v[sl], gtb) * eend[sl]
            dv_scr[sl, :] = _dot_tn(a[c], do[sl]) + _dot_nt(ke[sl], gtb)
            dq_scr[sl, :] = dqt * eb[sl]
            dk_scr[sl, :] = dkt * enb[sl] + dks
            ech = (jnp.sum(k[sl] * dks, axis=0, keepdims=True)
                   + jnp.sum(gts[c] * jnp.exp(bc[c]) * stp, axis=0, keepdims=True))
            e_scr[sl, :] = jnp.broadcast_to(ech, (C, H))
        dq = dq_scr[...]
        dk = dk_scr[...]
        dlf = _tri_dot(upper, q * dq - k * dk) + e_scr[...]
        dft = dlf / f - dk
        dz_ref[0] = (dq * _dsilu(qp, sq)).astype(BF16)
        dz_ref[1] = (dft * (1.0 - lb) * sg * (1.0 - sg)).astype(BF16)
        dz_ref[2] = dv_scr[...].astype(BF16)
        dlb_ref[...] += jnp.sum(dft * (1.0 - sg), axis=0, keepdims=True)

    col = lambda kk: (lambda h, n: (nb - 1 - n, kk * nh + h))
    return pl.pallas_call(
        body, name="hgrn_bwd", grid=(nh, nb),
        out_shape=(jax.ShapeDtypeStruct((4, s, e), BF16), jax.ShapeDtypeStruct((1, e), F32),
                   jax.ShapeDtypeStruct((1, H), F32)),
        in_specs=[pl.BlockSpec((tb, H), col(0)), pl.BlockSpec((tb, H), col(1)),
                  pl.BlockSpec((tb, H), col(2)), pl.BlockSpec((tb, H), col(3)),
                  pl.BlockSpec((tb, H), lambda h, n: (nb - 1 - n, h)),
                  pl.BlockSpec((None, nc, H, H), lambda h, n: (h, nb - 1 - n, 0, 0)),
                  pl.BlockSpec((2, H), lambda h, n: (0, h)), pl.BlockSpec((1, H), lambda h, n: (0, 0))],
        out_specs=(pl.BlockSpec((4, tb, H), lambda h, n: (0, nb - 1 - n, h)),
                   pl.BlockSpec((1, H), lambda h, n: (0, h)), pl.BlockSpec((1, H), lambda h, n: (0, 0))),
        scratch_shapes=[pltpu.VMEM((H, H), F32)] + [pltpu.VMEM((tb, H), F32)] * 5,
        compiler_params=_params("arbitrary", "arbitrary"))(z, z, z, z, dy, states, lb_logits, gnorm)


ATTN_T = 16 * ATTN_BAND
SCALE = ATTN_HEAD ** -0.5


def _slope(pair, hh, nheads):
    head = (2 * pair + hh + 1).astype(F32)
    return jnp.exp(jnp.full((1, 1), -8.0 / nheads * math.log(2.0), F32) * head)


def _fill_bias(bias, pair, nheads, delta, edge_ok):
    band = (delta >= 0) & (delta <= ATTN_BAND)
    dist = delta.astype(F32)
    for pi, dil in enumerate(DILATIONS):
        for hh in range(2):
            full = jnp.where(band, -(_slope(pair, hh, nheads) * float(dil)) * dist, NEG_BIG)
            bias[(pi * 2 + hh) * 2] = full
            bias[(pi * 2 + hh) * 2 + 1] = jnp.where(edge_ok, full, NEG_BIG)


def _rows(start, size, stride):
    if stride == 1:
        return pl.ds(pl.multiple_of(start, ATTN_BAND), size)
    return pl.ds(start, size, stride=stride)


def _head_lanes(rows, hh):
    return (lax.broadcasted_iota(jnp.int32, (rows, LANES), 1) // ATTN_HEAD) == hh


def _attn_fwd(z):
    s = z.shape[0]
    e = z.shape[1] // N_SPLITS
    npair = e // LANES
    T = ATTN_T
    assert s % T == 0
    nsb = s // T
    W = ATTN_BAND
    nt = T // W
    HD = ATTN_HEAD
    chunk = 256

    nsteps = npair * nsb
    assert nt % 2 == 0 and T // chunk == nt // 2

    def body(q_ref, kp_ref, kc_ref, vp_ref, vc_ref, g_ref, o_ref, l_ref, y_ref, qa, kbuf, va, bias, *sets):
        t = pl.program_id(0)
        sb = jnp.minimum(t, nsteps - 1) % nsb
        pair = jnp.minimum(t, nsteps - 1) // nsb
        sets = (sets[0:3], sets[3:6])

        def merge_chunk(i, done):
            accs, ms, lsw = done
            rows = pl.ds(pl.multiple_of(i * chunk, chunk), chunk)
            m1, m2, m3 = ms[0, rows, :], ms[1, rows, :], ms[2, rows, :]
            mx = jnp.maximum(jnp.maximum(m1, m2), m3)
            w1, w2, w3 = jnp.exp(m1 - mx), jnp.exp(m2 - mx), jnp.exp(m3 - mx)
            unswap = lambda a: pltpu.roll(a, ATTN_HEAD, 1)
            den = w1 * unswap(lsw[0, rows, :]) + w2 * unswap(lsw[1, rows, :]) + w3 * unswap(lsw[2, rows, :])
            o = (w1 * accs[0, rows, :] + w2 * accs[1, rows, :] + w3 * accs[2, rows, :]) / den
            o_ref[rows, :] = o
            l_ref[rows, :] = mx + jnp.log(den)
            gp = g_ref[rows, :]
            y_ref[rows, :] = (o * (gp * _sigmoid(gp))).astype(BF16)

        @pl.when(t == 0)
        def _():
            accs, ms, lsw = sets[1]
            accs[...] = jnp.zeros_like(accs)
            ms[...] = jnp.zeros_like(ms)
            lsw[...] = jnp.ones_like(lsw)

        @pl.when(t == nsteps)
        def _():
            def drain(i, carry):
                merge_chunk(i, sets[(nsteps - 1) % 2])
                return carry

            lax.fori_loop(0, T // chunk, drain, 0)

        def compute(cur, done):
            accs, ms, lsw = cur
            def stage(i, carry):
                rows = pl.ds(pl.multiple_of(i * chunk, chunk), chunk)
                upper = pl.ds(pl.multiple_of(T + i * chunk, chunk), chunk)
                kbuf[upper, :] = kc_ref[rows, :]
                for hh in range(2):
                    mine = _head_lanes(chunk, hh)
                    qa[hh, rows, :] = jnp.where(mine, q_ref[rows, :] * SCALE, 0.0)
                    va[hh, upper, :] = jnp.where(mine, vc_ref[rows, :], 1.0)
                return carry

            lax.fori_loop(0, T // chunk, stage, 0)

            @pl.when(sb == 0)
            def _():
                def stage_prev(i, carry):
                    rows = pl.ds(pl.multiple_of(i * chunk, chunk), chunk)
                    kbuf[rows, :] = kp_ref[rows, :]
                    for hh in range(2):
                        va[hh, rows, :] = jnp.where(_head_lanes(chunk, hh), vp_ref[rows, :], 1.0)
                    return carry

                lax.fori_loop(0, T // chunk, stage_prev, 0)

            qi = lax.broadcasted_iota(jnp.int32, (W, 2 * W), 0)
            kj = lax.broadcasted_iota(jnp.int32, (W, 2 * W), 1)
            _fill_bias(bias, pair, 2 * npair, W + qi - kj, kj >= W)

            def tile(tau):
                first = _head_lanes(W, 0)
                rows, scores = [], []
                for pi, dil in enumerate(DILATIONS):
                    r = tau % dil
                    ub = tau // dil
                    qrows = _rows(r + dil * W * ub, W, dil)
                    krows = _rows(T + dil * W * (ub - 1) + r, 2 * W, dil)
                    var = jnp.where((sb == 0) & (ub == 0), 1, 0)
                    kt = kbuf[krows, :].astype(BF16)
                    rows.append((qrows, krows))
                    scores.append([_dot_nt(qa[hh, qrows, :].astype(BF16), kt) + bias[(pi * 2 + hh) * 2 + var]
                                   for hh in range(2)])
                maxes = [[jnp.max(sc, axis=-1, keepdims=True) for sc in pair_] for pair_ in scores]
                probs = [[jnp.exp(sc - m).astype(BF16) for sc, m in zip(ps, pm)] for ps, pm in zip(scores, maxes)]
                for pi, (qrows, krows) in enumerate(rows):
                    outs = [_dot(probs[pi][hh], va[hh, krows, :].astype(BF16)) for hh in range(2)]
                    accs[pi, qrows, :] = jnp.where(first, outs[0], outs[1])
                    lsw[pi, qrows, :] = jnp.where(first, outs[1], outs[0])
                    ms[pi, qrows, :] = jnp.where(first, maxes[pi][0], maxes[pi][1])

            def two_tiles(i, carry):
                tile(2 * i)
                tile(2 * i + 1)
                merge_chunk(i, done)
                return carry

            lax.fori_loop(0, nt // 2, two_tiles, 0)

            def move_down(i, carry):
                rows = pl.ds(pl.multiple_of(i * chunk, chunk), chunk)
                upper = pl.ds(pl.multiple_of(T + i * chunk, chunk), chunk)
                kbuf[rows, :] = kbuf[upper, :]
                for hh in range(2):
                    va[hh, rows, :] = va[hh, upper, :]
                return carry

            lax.fori_loop(0, T // chunk, move_down, 0)

        for parity in range(2):
            @pl.when((t < nsteps) & (t % 2 == parity))
            def _():
                compute(sets[parity], sets[1 - parity])

    step_of = lambda t: jnp.minimum(t, nsteps - 1)
    lag_of = lambda t: jnp.maximum(t - 1, 0)
    cur = lambda split: (lambda t: (step_of(t) % nsb, split * npair + step_of(t) // nsb))
    prev = lambda split: (lambda t: (0, split * npair + step_of(t) // nsb))
    blk = lambda index: pl.BlockSpec((T, LANES), index)
    out = blk(lambda t: (lag_of(t) % nsb, lag_of(t) // nsb))
    gate = blk(lambda t: (lag_of(t) % nsb, 7 * npair + lag_of(t) // nsb))
    buf = lambda rows: pltpu.VMEM((rows, LANES), F32)
    return pl.pallas_call(
        body, name="attn_fwd", grid=(nsteps + 1,),
        out_shape=(jax.ShapeDtypeStruct((s, e), F32), jax.ShapeDtypeStruct((s, e), F32), jax.ShapeDtypeStruct((s, e), BF16)),
        in_specs=[blk(cur(4)), blk(prev(5)), blk(cur(5)), blk(prev(6)), blk(cur(6)), gate],
        out_specs=(out, out, out),
        scratch_shapes=[pltpu.VMEM((2, T, LANES), F32), buf(2 * T), pltpu.VMEM((2, 2 * T, LANES), F32),
                        pltpu.VMEM((12, W, 2 * W), F32)] + [pltpu.VMEM((3, T, LANES), F32)] * 6,
        compiler_params=_params("arbitrary"))(z, z, z, z, z, z)


def _outproj_loss(x, y_h, y_a, w_out_full, final_gain, target):
    s, d = x.shape
    e = y_h.shape[1]
    tm = _tile(s, 512)
    sub = _tile(tm, 256)

    def body(x_ref, yh_ref, ya_ref, w_ref, g_ref, t_ref, dx_ref, dxb_ref, dy_ref, loss_ref, dg_ref):
        @pl.when(pl.program_id(0) == 0)
        def _():
            loss_ref[...] = jnp.zeros_like(loss_ref)
            dg_ref[...] = jnp.zeros_like(dg_ref)

        w = w_ref[...]
        g = g_ref[...]
        parts = [slice(r0, r0 + sub) for r0 in range(0, tm, sub)]
        x2s = [x_ref[rows, :] + _dot(yh_ref[rows, :], w[0:e]) + _dot(ya_ref[rows, :], w[e:2 * e]) for rows in parts]
        for rows, x2 in zip(parts, x2s):
            r = lax.rsqrt(jnp.mean(x2 * x2, axis=-1, keepdims=True) + NORM_EPS)
            xn = x2 * r
            err = xn * g - t_ref[rows, :]
            loss_ref[...] += jnp.sum(err * err, axis=0, keepdims=True) * (0.5 / d)
            dyo = err * (1.0 / d)
            dg_ref[...] += jnp.sum(dyo * xn, axis=0, keepdims=True)
            u = dyo * g
            dx2 = r * (u - xn * jnp.mean(u * xn, axis=-1, keepdims=True))
            dx_ref[rows, :] = dx2
            dxb = dx2.astype(BF16)
            dxb_ref[rows, :] = dxb
            dy_ref[rows, :] = _dot_nt(dxb, w)

    row = pl.BlockSpec((tm, d), lambda i: (i, 0))
    half = pl.BlockSpec((tm, e), lambda i: (i, 0))
    vec = pl.BlockSpec((1, d), lambda i: (0, 0))
    whole = pl.BlockSpec((2 * e, d), lambda i: (0, 0), pipeline_mode=pl.Buffered(1))
    return pl.pallas_call(
        body, name="outproj_loss", grid=(s // tm,),
        out_shape=(jax.ShapeDtypeStruct((s, d), F32), jax.ShapeDtypeStruct((s, d), BF16),
                   jax.ShapeDtypeStruct((s, 2 * e), F32), jax.ShapeDtypeStruct((1, d), F32),
                   jax.ShapeDtypeStruct((1, d), F32)),
        in_specs=[row, half, half, whole, vec, row],
        out_specs=(row, row, pl.BlockSpec((tm, 2 * e), lambda i: (i, 0)), vec, vec),
        compiler_params=pltpu.CompilerParams(dimension_semantics=("arbitrary",), vmem_limit_bytes=OUTPROJ_VMEM_LIMIT),
    )(x, y_h, y_a, w_out_full, final_gain, target)


def _attn_bwd(z, dy, o, lse):
    s, e = o.shape
    npair = e // LANES
    T = ATTN_T
    assert s % T == 0
    nsb = s // T
    W = ATTN_BAND
    nt = T // W
    HD = ATTN_HEAD
    chunk = 256

    def body(k_ref, v_ref, qc_ref, qn_ref, dyc_ref, dyn_ref, gc_ref, gn_ref, oc_ref, on_ref, lc_ref, ln_ref,
             dz_ref, qa, doa, ka, va, dqacc, dkacc, dvacc, bias):
        sb = pl.program_id(1)
        def stage_queries(half, q_r, dy_r, g_r, o_r, l_r):
            def stage(i, carry):
                rows = pl.ds(pl.multiple_of(i * chunk, chunk), chunk)
                dst = pl.ds(pl.multiple_of(half * T + i * chunk, chunk), chunk)
                lane = lax.broadcasted_iota(jnp.int32, (chunk, LANES), 1)
                gp = g_r[rows, :]
                dov = dy_r[rows, :] * (gp * _sigmoid(gp))
                qv = q_r[rows, :] * SCALE
                same_head = (lax.broadcasted_iota(jnp.int32, (LANES, LANES), 0) // HD
                             == lax.broadcasted_iota(jnp.int32, (LANES, LANES), 1) // HD)
                ones = jnp.where(same_head, 1.0, 0.0).astype(BF16)
                hi, mid, lo = (p.astype(BF16) for p in _split3(dov * o_r[rows, :]))
                delta = _dot(hi, ones) + _dot(mid, ones) + _dot(lo, ones)
                swap = lambda a: pltpu.roll(a, HD, 1)
                lse_parts = [swap(p) for p in _split3(l_r[rows, :])]
                dl_parts = [swap(p) for p in _split3(delta)]
                for hh in range(2):
                    mine = _head_lanes(chunk, hh)
                    spare = (1 - hh) * HD
                    qh = jnp.where(mine, qv, 0.0)
                    dh = jnp.where(mine, dov, 0.0)
                    for j in range(3):
                        qh = jnp.where(lane == spare + j, lse_parts[j], qh)
                        dh = jnp.where(lane == spare + j, dl_parts[j], dh)
                    qa[hh, dst, :] = qh
                    doa[hh, dst, :] = dh
                return carry

            lax.fori_loop(0, T // chunk, stage, 0)

        @pl.when(sb == 0)
        def _():
            stage_queries(0, qc_ref, dyc_ref, gc_ref, oc_ref, lc_ref)

        stage_queries(1, qn_ref, dyn_ref, gn_ref, on_ref, ln_ref)

        def stage_keys(i, carry):
            rows = pl.ds(pl.multiple_of(i * chunk, chunk), chunk)
            lane = lax.broadcasted_iota(jnp.int32, (chunk, LANES), 1)
            for hh in range(2):
                spare = (1 - hh) * HD
                minus = (lane >= spare) & (lane < spare + 3)
                ka[hh, rows, :] = jnp.where(minus, -1.0, k_ref[rows, :])
                va[hh, rows, :] = jnp.where(minus, -1.0, v_ref[rows, :])
            gp = gc_ref[rows, :]
            dz_ref[3, rows, :] = (dyc_ref[rows, :] * oc_ref[rows, :] * _dsilu(gp, _sigmoid(gp))).astype(BF16)
            return carry

        lax.fori_loop(0, T // chunk, stage_keys, 0)

        @pl.when(sb == 0)
        def _():
            dqacc[0:T, :] = jnp.zeros((T, LANES), F32)

        dqacc[T:, :] = jnp.zeros((T, LANES), F32)
        dkacc[...] = jnp.zeros_like(dkacc)
        dvacc[...] = jnp.zeros_like(dvacc)
        qi = lax.broadcasted_iota(jnp.int32, (2 * W, W), 0)
        kj = lax.broadcasted_iota(jnp.int32, (2 * W, W), 1)
        _fill_bias(bias, pl.program_id(0), 2 * npair, qi - kj, qi < W)

        def tile(tau, carry):
            def scores(step, pi):
                dil = DILATIONS[pi]
                r = step % dil
                ub = step // dil
                start = r + dil * W * ub
                krows = _rows(start, W, dil)
                qrows = _rows(start, 2 * W, dil)
                var = jnp.where((sb == nsb - 1) & (ub == nt // dil - 1), 1, 0)
                unit = dict(krows=krows, qrows=qrows, ops=[], sc=[], dpd=[])
                for hh in range(2):
                    kt = ka[hh, krows, :].astype(BF16)
                    vt = va[hh, krows, :].astype(BF16)
                    qt = qa[hh, qrows, :].astype(BF16)
                    dt = doa[hh, qrows, :].astype(BF16)
                    unit["ops"].append((kt, qt, dt))
                    unit["sc"].append(_dot_nt(qt, kt) + bias[(pi * 2 + hh) * 2 + var])
                    unit["dpd"].append(_dot_nt(dt, vt))
                return unit

            def elementwise(unit):
                ps = [jnp.exp(s_) for s_ in unit["sc"]]
                unit["ds"] = [(p * d).astype(BF16) for p, d in zip(ps, unit["dpd"])]
                unit["pb"] = [p.astype(BF16) for p in ps]

            def products(unit):
                dvs = [_dot_tn(pb, dt) for pb, (kt, qt, dt) in zip(unit["pb"], unit["ops"])]
                dks = [_dot_tn(ds, qt) for ds, (kt, qt, dt) in zip(unit["ds"], unit["ops"])]
                dqs = [_dot(ds, kt) for ds, (kt, qt, dt) in zip(unit["ds"], unit["ops"])]
                dkacc[unit["krows"], :] += jnp.where(_head_lanes(W, 0), dks[0], dks[1])
                dvacc[unit["krows"], :] += jnp.where(_head_lanes(W, 0), dvs[0], dvs[1])
                dqacc[unit["qrows"], :] += jnp.where(_head_lanes(2 * W, 0), dqs[0], dqs[1]) * SCALE

            order = [(2 * tau + half, pi) for half in range(2) for pi in range(len(DILATIONS))]
            units = [None] * len(order)
            for n in range(len(order) + 2):
                if n < len(order):
                    units[n] = scores(*order[n])
                if 1 <= n <= len(order):
                    elementwise(units[n - 1])
                if n >= 2:
                    products(units[n - 2])
            return carry

        lax.fori_loop(0, nt // 2, tile, 0)

        def flush(i, carry):
            rows = pl.ds(pl.multiple_of(i * chunk, chunk), chunk)
            nxt = pl.ds(pl.multiple_of(T + i * chunk, chunk), chunk)
            dz_ref[0, rows, :] = dqacc[rows, :].astype(BF16)
            dz_ref[1, rows, :] = dkacc[rows, :].astype(BF16)
            dz_ref[2, rows, :] = dvacc[rows, :].astype(BF16)
            dqacc[rows, :] = dqacc[nxt, :]
            for hh in range(2):
                qa[hh, rows, :] = qa[hh, nxt, :]
                doa[hh, rows, :] = doa[hh, nxt, :]
            return carry

        lax.fori_loop(0, T // chunk, flush, 0)

    zc = lambda split: (lambda hp, sb: (sb, split * npair + hp))
    zn = lambda split: (lambda hp, sb: (jnp.minimum(sb + 1, nsb - 1), split * npair + hp))
    ec = lambda off: (lambda hp, sb: (sb, off + hp))
    en = lambda off: (lambda hp, sb: (jnp.minimum(sb + 1, nsb - 1), off + hp))
    z0 = lambda split: (lambda hp, sb: (0, split * npair + hp))
    e0 = lambda off: (lambda hp, sb: (0, off + hp))
    blk = lambda index: pl.BlockSpec((T, LANES), index)
    buf = lambda rows: pltpu.VMEM((rows, LANES), F32)
    return pl.pallas_call(
        body, name="attn_bwd", grid=(npair, nsb), out_shape=jax.ShapeDtypeStruct((4, s, e), BF16),
        in_specs=[blk(zc(5)), blk(zc(6)), blk(z0(4)), blk(zn(4)), blk(ec(npair)), blk(en(npair)),
                  blk(zc(7)), blk(zn(7)), blk(ec(0)), blk(en(0)), blk(e0(0)), blk(en(0))],
        out_specs=pl.BlockSpec((4, T, LANES), lambda hp, sb: (0, sb, hp)),
        scratch_shapes=[pltpu.VMEM((2, 2 * T, LANES), F32), pltpu.VMEM((2, 2 * T, LANES), F32),
                        pltpu.VMEM((2, T, LANES), F32), pltpu.VMEM((2, T, LANES), F32),
                        buf(2 * T), buf(T), buf(T), pltpu.VMEM((12, 2 * W, W), F32)],
        compiler_params=_params("parallel", "arbitrary"))(z, z, z, z, dy, dy, z, z, o, o, lse, lse)


def _dz_specs(tm, e):
    def mk(lo, hi):
        return pl.BlockSpec((None, tm, e), lambda i, k: (jnp.clip(k - lo, 0, hi - lo - 1), i, 0))
    return [mk(0, 4), mk(4, 8)]


def _dz_pick(grp, dzh_ref, dza_ref, fn):
    @pl.when(grp < 4)
    def _():
        fn(dzh_ref[...])

    @pl.when(grp >= 4)
    def _():
        fn(dza_ref[...])


def _dh_dx(dzh, dza, w_full, x, gain, dx2):
    s, d = x.shape
    e = dzh.shape[2]
    tm = _tile(s, 1024)
    ni = s // tm
    chunk = _tile(tm, 128)
    fetch_at = 2

    def body(dzh_ref, dza_ref, w_ref, x_hbm, g_ref, dx2_hbm, gx_hbm, dg_ref, acc, xbuf, dbuf, sems):
        i, k = pl.program_id(0), pl.program_id(1)
        rows_of = lambda tile: pl.ds(pl.multiple_of(tile * tm, tm), tm)
        fetch_x = pltpu.make_async_copy(x_hbm.at[rows_of(i), :], xbuf, sems.at[0])
        fetch_d = pltpu.make_async_copy(dx2_hbm.at[rows_of(i), :], dbuf, sems.at[1])

        def store(tile):
            return pltpu.make_async_copy(xbuf, gx_hbm.at[rows_of(tile), :], sems.at[2])

        @pl.when((i == 0) & (k == 0))
        def _():
            dg_ref[...] = jnp.zeros_like(dg_ref)

        @pl.when((k == fetch_at) & (i > 0))
        def _():
            store(i).wait()

        @pl.when(k == fetch_at)
        def _():
            fetch_x.start()
            fetch_d.start()

        def finish(tile):
            fetch_x.wait()
            fetch_d.wait()
            gain_row = g_ref[...]
            for r0 in range(0, tm, chunk):
                rows = slice(r0, r0 + chunk)
                dh = acc[rows, :]
                xv = xbuf[rows, :]
                r = lax.rsqrt(jnp.mean(xv * xv, axis=-1, keepdims=True) + NORM_EPS)
                xn = xv * r
                u = dh * gain_row
                xbuf[rows, :] = dbuf[rows, :] + r * (u - xn * jnp.mean(u * xn, axis=-1, keepdims=True))
                dg_ref[...] += jnp.sum(dh * xn, axis=0, keepdims=True)
            store(tile).start()

        @pl.when((k == 0) & (i == 0))
        def _():
            acc[...] = _dot_nt(dzh_ref[...], w_ref[...])

        @pl.when((k == 0) & (i > 0))
        def _():
            finish(i - 1)
            acc[...] = _dot_nt(dzh_ref[...], w_ref[...])

        @pl.when(k > 0)
        def _():
            def add(dz):
                acc[...] += _dot_nt(dz, w_ref[...])

            _dz_pick(k, dzh_ref, dza_ref, add)

        @pl.when((k == N_SPLITS - 1) & (i == ni - 1))
        def _():
            finish(i)
            store(i).wait()

    vec = pl.BlockSpec((1, d), lambda i, k: (0, 0))
    return pl.pallas_call(
        body, name="dh_dx", grid=(ni, N_SPLITS),
        out_shape=(jax.ShapeDtypeStruct((s, d), F32), jax.ShapeDtypeStruct((1, d), F32)),
        in_specs=_dz_specs(tm, e) + [pl.BlockSpec((None, d, e), lambda i, k: (k, 0, 0)), ANY, vec, ANY],
        out_specs=(ANY, vec),
        scratch_shapes=[pltpu.VMEM((tm, d), F32)] * 3 + [pltpu.SemaphoreType.DMA((3,))],
        compiler_params=pltpu.CompilerParams(dimension_semantics=("arbitrary", "arbitrary"),
                                             vmem_limit_bytes=DHDX_VMEM_LIMIT))(dzh, dza, w_full, x, gain, dx2)


def _position():
    x, y, c = lax.axis_index("x"), lax.axis_index("y"), lax.axis_index("c")
    return x, y, c


def _xor_peer(x, y, c, mask):
    return (x ^ ((mask >> 2) & 1), y ^ ((mask >> 1) & 1), c ^ (mask & 1))


def _block_order(masks):
    me = 4 * lax.axis_index("x") + 2 * lax.axis_index("y") + lax.axis_index("c")
    return jnp.stack([me ^ m for m in masks]).astype(jnp.int32)


GATHER_MASKS = (0, 1, 4, 5, 2, 3, 6, 7)


def _inproj_gather(h, w_loc, wo_loc):
    s, d = h.shape
    e = w_loc.shape[1]
    tm = _tile(s, 1024)
    ni = s // tm
    pre = max(ni - 2, 0)

    def body(order_ref, h_ref, w_ref, wo_ref, z_ref, wf_ref, wof_ref, wbuf, send_sems, recv_sems, osend, orecv,
             local_sems, wsems):
        j, i = pl.program_id(0), pl.program_id(1)
        x, y, c = _position()
        me, sibling = (x, y, c), (x, y, 1 - c)
        chips = [(1 - x, y), (x, 1 - y), (1 - x, 1 - y)]
        blk = lambda p: 4 * p[0] + 2 * p[1] + p[2]

        def copy(k, block, to, src=None):
            dst = wf_ref.at[blk(block)]
            return pltpu.make_async_remote_copy(
                src_ref=dst if src is None else src, dst_ref=dst, send_sem=send_sems.at[k], recv_sem=recv_sems.at[k],
                device_id=to, device_id_type=MESH)

        first = [copy(0, me, sibling, src=w_ref)] + [copy(1 + q, me, (*chip, c), src=w_ref) for q, chip in enumerate(chips)]
        passed = [copy(4 + q, (*chip, c), sibling) for q, chip in enumerate(chips)]
        mine = pltpu.make_async_copy(w_ref, wf_ref.at[blk(me)], local_sems.at[0])
        ocopies = [pltpu.make_async_remote_copy(
            src_ref=wo_ref, dst_ref=wof_ref.at[blk(me)], send_sem=osend.at[k], recv_sem=orecv.at[k],
            device_id=_xor_peer(x, y, c, k + 1), device_id_type=MESH) for k in range(N_DEV - 1)]
        omine = pltpu.make_async_copy(wo_ref, wof_ref.at[blk(me)], local_sems.at[1])
        blocks = [me, sibling] + [(*chip, c) for chip in chips] + [(*chip, 1 - c) for chip in chips]
        arrive = [None, copy(0, sibling, me)] + [copy(1 + q, (*chip, c), me) for q, chip in enumerate(chips)] \
            + [copy(4 + q, (*chip, 1 - c), me) for q, chip in enumerate(chips)]
        forward = [None, None] + passed + [None, None, None]
        use_order = (0, 1, 2, 5, 3, 6, 4, 7)
        blocks, arrive, forward = ([lst[n] for n in use_order] for lst in (blocks, arrive, forward))

        def load(slot, src):
            return pltpu.make_async_copy(src, wbuf.at[slot], wsems.at[slot])

        @pl.when((j == 0) & (i == 0))
        def _():
            for cp in [mine, omine] + first + ocopies:
                cp.start()
            load(0, w_ref).start()

        for jj in range(N_DEV):
            @pl.when((j == jj) & (i == 0))
            def _():
                load(jj % 2, w_ref).wait()

            if jj + 1 < N_DEV:
                @pl.when((j == jj) & (i == pre))
                def _():
                    arrive[jj + 1].wait_recv()
                    if forward[jj + 1] is not None:
                        forward[jj + 1].start()
                    load((jj + 1) % 2, wf_ref.at[blk(blocks[jj + 1])]).start()

        z_ref[...] = _dot(h_ref[...], wbuf[j % 2])

        @pl.when((j == N_DEV - 1) & (i == ni - 1))
        def _():
            for cp in first + passed:
                cp.wait_send()
            for cp in ocopies:
                cp.wait_send()
                cp.wait_recv()
            mine.wait()
            omine.wait()

    grid_spec = pltpu.PrefetchScalarGridSpec(
        num_scalar_prefetch=1, grid=(N_DEV, ni),
        in_specs=[pl.BlockSpec((tm, d), lambda j, i, o: (i, 0)), ANY, ANY],
        out_specs=(pl.BlockSpec((tm, e), lambda j, i, o: (i, o[j])), ANY, ANY),
        scratch_shapes=[pltpu.VMEM((2, d, e), BF16), pltpu.SemaphoreType.DMA((7,)), pltpu.SemaphoreType.DMA((7,)),
                        pltpu.SemaphoreType.DMA((7,)), pltpu.SemaphoreType.DMA((7,)), pltpu.SemaphoreType.DMA((2,)),
                        pltpu.SemaphoreType.DMA((2,))])
    return pl.pallas_call(
        body, name="inproj_gather", grid_spec=grid_spec,
        out_shape=(jax.ShapeDtypeStruct((s, N_SPLITS * e), F32), jax.ShapeDtypeStruct((N_DEV, d, e), BF16),
                   jax.ShapeDtypeStruct((N_DEV,) + wo_loc.shape, BF16)),
        compiler_params=_params("arbitrary", "arbitrary"))(_block_order(GATHER_MASKS), h, w_loc, wo_loc)


SCATTER_MASKS = (7, 6, 5, 4, 3, 2, 1, 0)
N_CHIPS = 4


def _scatter_block(k, acc, stage, tmp, own_ref, ra_ref, rb_ref, sa_send, sa_recv, sb_send, sb_recv, loc_sem, step, ns):
    x, y, c = _position()
    chip_of = lambda t: _xor_peer(x, y, c, SCATTER_MASKS[2 * t + 1])
    last = step == ns - 1
    fetch_at = min(1, ns - 1)

    def ship(t):
        return pltpu.make_async_remote_copy(
            src_ref=stage.at[0], dst_ref=ra_ref.at[t], send_sem=sa_send.at[t], recv_sem=sa_recv.at[t],
            device_id=(x, y, 1 - c), device_id_type=MESH)

    def send(t):
        return pltpu.make_async_remote_copy(
            src_ref=stage.at[1], dst_ref=rb_ref.at[t], send_sem=sb_send.at[t], recv_sem=sb_recv.at[t],
            device_id=chip_of(t), device_id_type=MESH)

    for kk in range(N_DEV):
        t = kk // 2
        fetch = pltpu.make_async_copy(ra_ref.at[t], tmp, loc_sem)

        if kk % 2 == 1:
            @pl.when((step == fetch_at) & (k == kk))
            def _():
                ship(t).wait_recv()
                fetch.start()

        @pl.when(last & (k == kk))
        def _():
            if kk % 2 == 0:
                if t >= 1:
                    ship(t - 1).wait_send()
                stage[0] = acc[...].astype(BF16)
                ship(t).start()
            else:
                fetch.wait()
                acc[...] += tmp[...].astype(F32)
                if t < N_CHIPS - 1:
                    if t >= 1:
                        send(t - 1).wait_send()
                    stage[1] = acc[...].astype(BF16)
                    send(t).start()
                else:
                    keep = pltpu.make_async_copy(acc, own_ref, loc_sem)
                    keep.start()
                    keep.wait()
                    ship(t).wait_send()
                    send(t - 1).wait_send()
                    for q in range(N_CHIPS - 1):
                        send(q).wait_recv()


def _scatter_scratch(rows, cols):
    return [pltpu.VMEM((rows, cols), F32), pltpu.VMEM((2, rows, cols), BF16), pltpu.VMEM((rows, cols), BF16),
            pltpu.SemaphoreType.DMA((N_CHIPS,)), pltpu.SemaphoreType.DMA((N_CHIPS,)),
            pltpu.SemaphoreType.DMA((N_CHIPS - 1,)), pltpu.SemaphoreType.DMA((N_CHIPS - 1,)), pltpu.SemaphoreType.DMA(())]


def _scatter_out(rows, cols):
    return (jax.ShapeDtypeStruct((rows, cols), F32), jax.ShapeDtypeStruct((N_CHIPS, rows, cols), BF16),
            jax.ShapeDtypeStruct((N_CHIPS - 1, rows, cols), BF16))


def _dwin_scatter(h, dzh, dza):
    s, d = h.shape
    e = dzh.shape[2]
    ts = _tile(s, 1024)
    ns = s // ts

    def body(order_ref, dzh_ref, dza_ref, h_ref, own_ref, ra_ref, rb_ref, acc, stage, tmp, *sems):
        k, step = pl.program_id(0), pl.program_id(1)

        @pl.when(step == 0)
        def _():
            acc[...] = jnp.zeros_like(acc)

        def add(dz):
            acc[...] += _dot_tn(h_ref[...], dz)

        _dz_pick(order_ref[k], dzh_ref, dza_ref, add)
        _scatter_block(k, acc, stage, tmp, own_ref, ra_ref, rb_ref, *sems, step, ns)

    def dz_spec(lo):
        return pl.BlockSpec((None, ts, e), lambda k, st, o: (jnp.clip(o[k] - lo, 0, 3), st, 0))

    grid_spec = pltpu.PrefetchScalarGridSpec(
        num_scalar_prefetch=1, grid=(N_DEV, ns),
        in_specs=[dz_spec(0), dz_spec(4), pl.BlockSpec((ts, d), lambda k, st, o: (st, 0))],
        out_specs=(ANY, ANY, ANY), scratch_shapes=_scatter_scratch(d, e))
    own, _, rb = pl.pallas_call(
        body, name="dwin_scatter", grid_spec=grid_spec, out_shape=_scatter_out(d, e),
        compiler_params=_params("arbitrary", "arbitrary"))(_block_order(SCATTER_MASKS), dzh, dza, h)
    return own, rb


def _dwout_scatter(y_h, y_a, dxb):
    s, e = y_h.shape
    d = dxb.shape[1]
    r = 2 * e // N_DEV
    pairs = e // (2 * r)
    ts = _tile(s, 2048)
    ns = s // ts
    chip_masks = SCATTER_MASKS[1::2]
    passes = ((0, 1), (2,), (3,))
    slots = max(len(chips) for chips in passes)
    slot_chip = [chips[min(u, len(chips) - 1)] for chips in passes for u in range(slots)]

    def body(pair_ref, yh0_ref, ya0_ref, yh1_ref, ya1_ref, dx_ref, own_ref, ra_ref, rb_ref, acc, keep_buf, ship_buf,
             send_buf, tmp, sa_send, sa_recv, sb_send, sb_recv, loc_sem):
        p, step = pl.program_id(0), pl.program_id(1)
        x, y, c = _position()

        @pl.when(step == 0)
        def _():
            acc[...] = jnp.zeros_like(acc)

        for u, (yh_ref, ya_ref) in enumerate(((yh0_ref, ya0_ref), (yh1_ref, ya1_ref))):
            rows = slice(u * 2 * r, (u + 1) * 2 * r)
            used = functools.reduce(jnp.logical_or, [p == pp for pp, chips in enumerate(passes) if u < len(chips)])

            @pl.when(used & (pair_ref[slots * p + u] < pairs))
            def _():
                acc[rows, :] += _dot_tn(yh_ref[...], dx_ref[...])

            @pl.when(used & (pair_ref[slots * p + u] >= pairs))
            def _():
                acc[rows, :] += _dot_tn(ya_ref[...], dx_ref[...])

        def block_rows(u, core):
            return pl.ds(pl.multiple_of(u * 2 * r + core * r, r), r)

        slot_of = {q: u for chips in passes for u, q in enumerate(chips)}

        def ship(q):
            return pltpu.make_async_remote_copy(
                src_ref=ship_buf.at[slot_of[q]], dst_ref=ra_ref.at[q], send_sem=sa_send.at[q], recv_sem=sa_recv.at[q],
                device_id=(x, y, 1 - c), device_id_type=MESH)

        def send(q):
            return pltpu.make_async_remote_copy(
                src_ref=send_buf.at[slot_of[q]], dst_ref=rb_ref.at[q], send_sem=sb_send.at[q], recv_sem=sb_recv.at[q],
                device_id=_xor_peer(x, y, c, chip_masks[q]), device_id_type=MESH)

        def sibling_share(q):
            ship(q).wait_recv()
            fetch = pltpu.make_async_copy(ra_ref.at[q], tmp, loc_sem)
            fetch.start()
            fetch.wait()
            return tmp[...].astype(F32)

        shipped, sent = {}, {}
        for pp, chips in enumerate(passes):
            @pl.when((step == ns - 1) & (p == pp))
            def _():
                for u, q in enumerate(chips):
                    if u in shipped:
                        ship(shipped.pop(u)).wait_send()
                    ship_buf[u] = acc[block_rows(u, 1 - c), :].astype(BF16)
                    ship(q).start()
                    shipped[u] = q
                for u, q in enumerate(chips):
                    total = acc[block_rows(u, c), :] + sibling_share(q)
                    if q < N_CHIPS - 1:
                        if u in sent:
                            send(sent.pop(u)).wait_send()
                        send_buf[u] = total.astype(BF16)
                        send(q).start()
                        sent[u] = q
                    else:
                        keep_buf[...] = total
                        keep = pltpu.make_async_copy(keep_buf, own_ref, loc_sem)
                        keep.start()
                        keep.wait()
                if pp == len(passes) - 1:
                    for q in shipped.values():
                        ship(q).wait_send()
                    for q in sent.values():
                        send(q).wait_send()
                    for q in range(N_CHIPS - 1):
                        send(q).wait_recv()

    def y_spec(u, lo):
        return pl.BlockSpec((ts, 2 * r), lambda p, st, o: (st, jnp.clip(o[slots * p + u] - lo, 0, pairs - 1)))

    pair_of_chip = _block_order(chip_masks) // 2
    grid_spec = pltpu.PrefetchScalarGridSpec(
        num_scalar_prefetch=1, grid=(len(passes), ns),
        in_specs=[y_spec(0, 0), y_spec(0, pairs), y_spec(1, 0), y_spec(1, pairs),
                  pl.BlockSpec((ts, d), lambda p, st, o: (st, 0))],
        out_specs=(ANY, ANY, ANY),
        scratch_shapes=[pltpu.VMEM((slots * 2 * r, d), F32), pltpu.VMEM((r, d), F32),
                        pltpu.VMEM((slots, r, d), BF16)] + _scatter_scratch(r, d)[1:])
    own, _, rb = pl.pallas_call(
        body, name="dwout_scatter", grid_spec=grid_spec, out_shape=_scatter_out(r, d),
        compiler_params=_params("arbitrary", "arbitrary"))(
            jnp.stack([pair_of_chip[q] for q in slot_chip]), y_h, y_a, y_h, y_a, dxb)
    return own, rb


def _sum_chips_adamw(own, recv, w, m, v):
    r, c = w.shape
    tr = _tile(r, 256)

    def body(own_ref, rc_ref, w_ref, m_ref, v_ref, g_ref, d_ref, mo_ref, vo_ref):
        g = own_ref[...]
        for q in range(N_CHIPS - 1):
            g = g + rc_ref[q].astype(F32)
        g_ref[...] = g
        d_ref[...], mo_ref[...], vo_ref[...] = _adamw(w_ref[...], g, m_ref[...], v_ref[...])

    blk = pl.BlockSpec((tr, c), lambda i: (i, 0))
    shp = jax.ShapeDtypeStruct((r, c), F32)
    return pl.pallas_call(
        body, name="sum_chips_adamw", grid=(r // tr,), out_shape=(shp, shp, shp, shp),
        in_specs=[blk, pl.BlockSpec((N_CHIPS - 1, tr, c), lambda i: (0, i, 0)), blk, blk, blk],
        out_specs=(blk, blk, blk, blk), compiler_params=_params("parallel"))(own, recv, w, m, v)


SMALL_ROWS = 8
ROW_LB = 4
ROW_GN = 6
ROW_LOSS = 7


def _small_allreduce_adamw(part, w, m, v, lb_logits):
    width = part.shape[1]

    def body(p_ref, w_ref, m_ref, v_ref, lb_ref, g_ref, d_ref, mo_ref, vo_ref, buf, send_sems, recv_sems):
        x, y, c = _position()
        me = 4 * x + 2 * y + c
        buf[me] = p_ref[...]
        copies = []
        for k in range(N_DEV - 1):
            bx, by, bc = ((k + 1) >> 2) & 1, ((k + 1) >> 1) & 1, (k + 1) & 1
            peer = (x ^ bx, y ^ by, c ^ bc)
            copies.append(pltpu.make_async_remote_copy(
                src_ref=p_ref, dst_ref=buf.at[me], send_sem=send_sems.at[k], recv_sem=recv_sems.at[k],
                device_id=peer, device_id_type=MESH))
        for cp in copies:
            cp.start()
        for cp in copies:
            cp.wait_recv()
        for cp in copies:
            cp.wait_send()
        tot = buf[0]
        for dev in range(1, N_DEV):
            tot = tot + buf[dev]
        lbv = lb_ref[...]
        lb = _sigmoid(lbv[0:1] - lbv[1:2])
        glb = tot[ROW_LB:ROW_LB + 1] * lb * (1.0 - lb)
        loss = jnp.sum(tot[ROW_LOSS:ROW_LOSS + 1], axis=-1, keepdims=True)
        row = lax.broadcasted_iota(jnp.int32, (SMALL_ROWS, width), 0)
        g = jnp.where(row == ROW_LB, glb, jnp.where(row == ROW_LB + 1, -glb, tot))
        g = jnp.where(row == ROW_LOSS, loss, g)
        g_ref[...] = g
        d_ref[...], mo_ref[...], vo_ref[...] = _adamw(w_ref[...], g, m_ref[...], v_ref[...])

    vm = pl.BlockSpec(memory_space=pltpu.VMEM)
    shp = jax.ShapeDtypeStruct((SMALL_ROWS, width), F32)
    return pl.pallas_call(
        body, name="small_allreduce_adamw", out_shape=(shp, shp, shp, shp),
        in_specs=[vm] * 5, out_specs=(vm, vm, vm, vm),
        scratch_shapes=[pltpu.VMEM((N_DEV, SMALL_ROWS, width), F32), pltpu.SemaphoreType.DMA((N_DEV - 1,)),
                        pltpu.SemaphoreType.DMA((N_DEV - 1,))],
    )(part, w, m, v, lb_logits)


def _pack_small(norm_gain, final_gain, lb2, gnorm, last_row, width):
    pad = lambda a: jnp.pad(a.reshape(1, -1), ((0, 0), (0, width - a.size)))
    return jnp.concatenate([norm_gain.reshape(2, width), final_gain.reshape(2, width), lb2.reshape(2, width),
                            pad(gnorm), last_row.reshape(1, width)], axis=0)


def _unpack_small(p, d, e, hd):
    return (p[0:2].reshape(1, d), p[2:4].reshape(d), p[4:6].reshape(2, e), p[6:7, :hd].reshape(1, hd))


def kernel(x, norm_gain, w_in, lb_logits, hgrn_gnorm, w_out, final_gain, loss_target, m_norm_gain, m_w_in, m_lb_logits, m_hgrn_gnorm, m_w_out, m_final_gain, v_norm_gain, v_w_in, v_lb_logits, v_hgrn_gnorm, v_w_out, v_final_gain):
    s, d = x.shape[1], x.shape[2]
    e = w_in.shape[2]
    assert d == 2 * e and lb_logits.shape == (2, e) and w_out.shape[1] * N_DEV == 2 * e
    x2d = x.reshape(s, d)
    tgt = loss_target.reshape(s, d)

    h = _rmsnorm_fwd(x2d, norm_gain)
    z, w_in_full, w_out_full = _inproj_gather(h, _cast_bf16(w_in[0]), _cast_bf16(w_out[0]))
    w_out_full = w_out_full.reshape(2 * e, d)
    y_h, states = _hgrn_fwd(z, lb_logits, hgrn_gnorm)
    o_attn, lse, y_a = _attn_fwd(z)
    dx2, dx2b, dy, loss_vec, dfg = _outproj_loss(x2d, y_h, y_a, w_out_full, final_gain.reshape(1, d), tgt)

    own_o, recv_o = _dwout_scatter(y_h, y_a, dx2b)
    dza = _attn_bwd(z, dy, o_attn, lse)
    dzh, dlb, dgn = _hgrn_bwd(z, dy, states, lb_logits, hgrn_gnorm)
    grad_x, dng = _dh_dx(dzh, dza, w_in_full, x2d, norm_gain, dx2)
    g_wo, d_wo, nm_wo, nv_wo = _sum_chips_adamw(own_o, recv_o, w_out[0], m_w_out[0], v_w_out[0])

    width = d // 2
    zero_row = jnp.zeros((1, width), F32)
    loss_row = loss_vec[:, :width] + loss_vec[:, width:]
    part = _pack_small(dng, dfg, jnp.concatenate([dlb, zero_row], axis=0), dgn, loss_row, width)
    pw = _pack_small(norm_gain, final_gain, lb_logits, hgrn_gnorm, zero_row, width)
    pm = _pack_small(m_norm_gain, m_final_gain, m_lb_logits, m_hgrn_gnorm, zero_row, width)
    pv = _pack_small(v_norm_gain, v_final_gain, v_lb_logits, v_hgrn_gnorm, zero_row, width)
    sg, sd, sm, sv = _small_allreduce_adamw(part, pw, pm, pv, lb_logits)
    own_i, recv_i = _dwin_scatter(h, dzh, dza)
    g_wi, d_wi, nm_wi, nv_wi = _sum_chips_adamw(own_i, recv_i, w_in[0], m_w_in[0], v_w_in[0])
    hd = hgrn_gnorm.shape[1]
    g_ng, g_fg, g_lb, g_gn = _unpack_small(sg, d, e, hd)
    d_ng, d_fg, d_lb, d_gn = _unpack_small(sd, d, e, hd)
    m_ng, m_fg, m_lb, m_gn = _unpack_small(sm, d, e, hd)
    v_ng, v_fg, v_lb, v_gn = _unpack_small(sv, d, e, hd)
    loss = sg[ROW_LOSS, 0]

    one = lambda a: a[None]
    return (loss, grad_x.reshape(1, s, d), g_ng, one(g_wi), g_lb, g_gn, one(g_wo), g_fg,
            d_ng, one(d_wi), d_lb, d_gn, one(d_wo), d_fg,
            m_ng, one(nm_wi), m_lb, m_gn, one(nm_wo), m_fg,
            v_ng, one(nv_wi), v_lb, v_gn, one(nv_wo), v_fg)
```

```python
import functools
import math

import jax
import jax.numpy as jnp
from jax import lax
from jax.experimental import pallas as pl
from jax.experimental.pallas import tpu as pltpu

NORM_EPS = 1e-6
HGRN_HEAD = 128
HGRN_CHUNK = 64
ATTN_HEAD = 64
ATTN_BAND = 128
DILATIONS = (1, 4, 16)
N_SPLITS = 8
N_DEV = 8
ADAM_LR = 0.001
ADAM_B1 = 0.9
ADAM_B2 = 0.999
ADAM_EPS = 1e-08
ADAM_WD = 0.01
ADAM_STEP = 10
LANES = 128
MESH = pl.DeviceIdType.MESH
F32 = jnp.float32
BF16 = jnp.bfloat16
NEG_BIG = -1e30
VMEM_LIMIT = 56 * 1024 * 1024
OUTPROJ_VMEM_LIMIT = 63 * 1024 * 1024
DHDX_VMEM_LIMIT = 60 * 1024 * 1024

ANY = pl.BlockSpec(memory_space=pl.ANY)


def _params(*sem):
    return pltpu.CompilerParams(dimension_semantics=sem, vmem_limit_bytes=VMEM_LIMIT)


def _tile(n, pref):
    t = min(n, pref)
    assert n % t == 0, (n, pref)
    return t


def _dot(a, b, precision=None):
    return jnp.dot(a, b, preferred_element_type=F32, precision=precision)


def _dot_nt(a, b):
    return lax.dot_general(a, b, (((1,), (1,)), ((), ())), preferred_element_type=F32)


def _dot_tn(a, b):
    return lax.dot_general(a, b, (((0,), (0,)), ((), ())), preferred_element_type=F32)


def _sigmoid(x):
    return 0.5 * jnp.tanh(0.5 * x) + 0.5


def _dsilu(x, s):
    return s * (1.0 + x * (1.0 - s))


def _adamw(w, g, m, v):
    m = ADAM_B1 * m + (1.0 - ADAM_B1) * g
    v = ADAM_B2 * v + (1.0 - ADAM_B2) * (g * g)
    m_hat = m / (1.0 - ADAM_B1 ** ADAM_STEP)
    v_hat = v / (1.0 - ADAM_B2 ** ADAM_STEP)
    delta = -ADAM_LR * (m_hat / (jnp.sqrt(v_hat) + ADAM_EPS) + ADAM_WD * w)
    return delta, m, v


def _cast_bf16(a):
    r, c = a.shape
    tr = _tile(r, 256)

    def body(a_ref, o_ref):
        o_ref[...] = a_ref[...].astype(BF16)

    return pl.pallas_call(
        body, name="cast_bf16", grid=(r // tr,), out_shape=jax.ShapeDtypeStruct((r, c), BF16),
        in_specs=[pl.BlockSpec((tr, c), lambda i: (i, 0))], out_specs=pl.BlockSpec((tr, c), lambda i: (i, 0)),
        compiler_params=_params("parallel"))(a)


def _rmsnorm_fwd(x, gain):
    s, d = x.shape
    tm = _tile(s, 512)

    def body(x_ref, g_ref, h_ref):
        xv = x_ref[...]
        r = lax.rsqrt(jnp.mean(xv * xv, axis=-1, keepdims=True) + NORM_EPS)
        h_ref[...] = (xv * r * g_ref[...]).astype(BF16)

    return pl.pallas_call(
        body, name="rmsnorm_fwd", grid=(s // tm,), out_shape=jax.ShapeDtypeStruct((s, d), BF16),
        in_specs=[pl.BlockSpec((tm, d), lambda i: (i, 0)), pl.BlockSpec((1, d), lambda i: (0, 0))],
        out_specs=pl.BlockSpec((tm, d), lambda i: (i, 0)), compiler_params=_params("parallel"))(x, gain)


HGRN_BLOCK = 2048
TRI_ROWS = 64


def _chunk_masks():
    tb = TRI_ROWS
    row = lax.broadcasted_iota(jnp.int32, (tb, tb), 0)
    col = lax.broadcasted_iota(jnp.int32, (tb, tb), 1)
    same = (row // HGRN_CHUNK) == (col // HGRN_CHUNK)
    lower = jnp.where(same & (col <= row), 1.0, 0.0).astype(BF16)
    upper = jnp.where(same & (col >= row), 1.0, 0.0).astype(BF16)
    return lower, upper


def _split3(a):
    hi = a.astype(BF16).astype(F32)
    mid = (a - hi).astype(BF16).astype(F32)
    lo = (a - hi - mid).astype(BF16).astype(F32)
    return hi, mid, lo


def _tri_dot(tri, x):
    hi, mid, lo = (p.astype(BF16) for p in _split3(x))
    outs = []
    for r in range(0, x.shape[0], TRI_ROWS):
        sl = slice(r, r + TRI_ROWS)
        outs.append(_dot(tri, hi[sl]) + _dot(tri, mid[sl]) + _dot(tri, lo[sl]))
    return outs[0] if len(outs) == 1 else jnp.concatenate(outs, axis=0)


def _hgrn_gates(qp, fp, lbv):
    lb = _sigmoid(lbv[0:1] - lbv[1:2])
    sq = _sigmoid(qp)
    q = qp * sq
    sg = _sigmoid(fp)
    f = lb + (1.0 - lb) * sg
    k = 1.0 - f
    return lb, sq, q, sg, f, k


def _hgrn_fwd(z, lb_logits, gnorm):
    s = z.shape[0]
    e = z.shape[1] // N_SPLITS
    nh = e // HGRN_HEAD
    tb = _tile(s, HGRN_BLOCK)
    nc = tb // HGRN_CHUNK
    nb = s // tb
    C = HGRN_CHUNK

    def body(q_ref, f_ref, i_ref, g_ref, lb_ref, gn_ref, y_ref, st_ref, state, o_scr):
        @pl.when(pl.program_id(1) == 0)
        def _():
            state[...] = jnp.zeros_like(state)

        lb, sq, q, sg, f, k = _hgrn_gates(q_ref[...], f_ref[...], lb_ref[...])
        lower, _ = _chunk_masks()
        b = _tri_dot(lower, jnp.log(f))
        b3 = b.reshape(nc, C, HGRN_HEAD)
        bc = b3[:, C - 1:C, :]
        qt = (q * jnp.exp(b)).astype(BF16)
        kt = (k * jnp.exp(-b)).astype(BF16)
        ke = (k.reshape(nc, C, HGRN_HEAD) * jnp.exp(bc - b3)).reshape(tb, HGRN_HEAD).astype(BF16)
        v = i_ref[...].astype(BF16)
        tri = lax.broadcasted_iota(jnp.int32, (C, C), 1) <= lax.broadcasted_iota(jnp.int32, (C, C), 0)
        sls = [slice(c * C, (c + 1) * C) for c in range(nc)]
        kv = [_dot_tn(v[sl], ke[sl]) for sl in sls]
        a = [jnp.where(tri, _dot_nt(qt[sl], kt[sl]), 0.0).astype(BF16) for sl in sls]
        st = state[...]
        sts = []
        for c in range(nc):
            sts.append(st)
            st_ref[c] = st
            st = st * jnp.exp(bc[c]) + kv[c]
        state[...] = st
        for c, sl in enumerate(sls):
            o_scr[sl, :] = _dot(a[c], v[sl]) + _dot_nt(qt[sl], sts[c].astype(BF16))
        o = o_scr[...]
        rms = lax.rsqrt(jnp.mean(o * o, axis=-1, keepdims=True) + NORM_EPS)
        gp = g_ref[...]
        y_ref[...] = (o * rms * gn_ref[...] * (gp * _sigmoid(gp))).astype(BF16)

    col = lambda kk: (lambda h, n: (n, kk * nh + h))
    return pl.pallas_call(
        body, name="hgrn_fwd", grid=(nh, nb),
        out_shape=(jax.ShapeDtypeStruct((s, e), BF16),
                   jax.ShapeDtypeStruct((nh, s // C, HGRN_HEAD, HGRN_HEAD), F32)),
        in_specs=[pl.BlockSpec((tb, HGRN_HEAD), col(0)), pl.BlockSpec((tb, HGRN_HEAD), col(1)),
                  pl.BlockSpec((tb, HGRN_HEAD), col(2)), pl.BlockSpec((tb, HGRN_HEAD), col(3)),
                  pl.BlockSpec((2, HGRN_HEAD), lambda h, n: (0, h)), pl.BlockSpec((1, HGRN_HEAD), lambda h, n: (0, 0))],
        out_specs=(pl.BlockSpec((tb, HGRN_HEAD), lambda h, n: (n, h)),
                   pl.BlockSpec((None, nc, HGRN_HEAD, HGRN_HEAD), lambda h, n: (h, n, 0, 0))),
        scratch_shapes=[pltpu.VMEM((HGRN_HEAD, HGRN_HEAD), F32), pltpu.VMEM((tb, HGRN_HEAD), F32)],
        compiler_params=_params("parallel", "arbitrary"))(z, z, z, z, lb_logits, gnorm)


def _hgrn_bwd(z, dy, states, lb_logits, gnorm):
    s = z.shape[0]
    e = z.shape[1] // N_SPLITS
    nh = e // HGRN_HEAD
    tb = _tile(s, HGRN_BLOCK)
    nc = tb // HGRN_CHUNK
    nb = s // tb
    C = HGRN_CHUNK
    H = HGRN_HEAD

    def body(q_ref, f_ref, i_ref, g_ref, dy_ref, st_ref, lb_ref, gn_ref, dz_ref, dlb_ref, dgn_ref,
             gstate, o_scr, dq_scr, dk_scr, dv_scr, e_scr):
        first = (pl.program_id(0) == 0) & (pl.program_id(1) == 0)

        @pl.when(first)
        def _():
            dgn_ref[...] = jnp.zeros_like(dgn_ref)

        @pl.when(pl.program_id(1) == 0)
        def _():
            gstate[...] = jnp.zeros_like(gstate)
            dlb_ref[...] = jnp.zeros_like(dlb_ref)

        qp = q_ref[...]
        lb, sq, q, sg, f, k = _hgrn_gates(qp, f_ref[...], lb_ref[...])
        lower, upper = _chunk_masks()
        b = _tri_dot(lower, jnp.log(f))
        b3 = b.reshape(nc, C, H)
        bc = b3[:, C - 1:C, :]
        eb = jnp.exp(b)
        enb = jnp.exp(-b)
        eend = jnp.exp(bc - b3).reshape(tb, H)
        qt = (q * eb).astype(BF16)
        kt = (k * enb).astype(BF16)
        ke = (k * eend).astype(BF16)
        v = i_ref[...].astype(BF16)
        tri = lax.broadcasted_iota(jnp.int32, (C, C), 1) <= lax.broadcasted_iota(jnp.int32, (C, C), 0)
        sls = [slice(c * C, (c + 1) * C) for c in range(nc)]
        a = [jnp.where(tri, _dot_nt(qt[sl], kt[sl]), 0.0).astype(BF16) for sl in sls]
        for c, sl in enumerate(sls):
            o_scr[sl, :] = _dot(a[c], v[sl]) + _dot_nt(qt[sl], st_ref[c].astype(BF16))
        o = o_scr[...]
        rms = lax.rsqrt(jnp.mean(o * o, axis=-1, keepdims=True) + NORM_EPS)
        on = o * rms
        gn = gn_ref[...]
        gp = g_ref[...]
        sgg = _sigmoid(gp)
        dyv = dy_ref[...]
        d_on = dyv * (gp * sgg)
        dz_ref[3] = (dyv * on * gn * _dsilu(gp, sgg)).astype(BF16)
        dgn_ref[...] += jnp.sum(d_on * on, axis=0, keepdims=True)
        u = d_on * gn
        do = (rms * (u - on * jnp.mean(u * on, axis=-1, keepdims=True))).astype(BF16)
        gup = [_dot_tn(do[sl], qt[sl]) for sl in sls]
        da = [jnp.where(tri, _dot_nt(do[sl], v[sl]), 0.0).astype(BF16) for sl in sls]
        gt = gstate[...]
        gts = [None] * nc
        for c in reversed(range(nc)):
            gts[c] = gt
            gt = gt * jnp.exp(bc[c]) + gup[c]
        gstate[...] = gt
        for c, sl in enumerate(sls):
            stp = st_ref[c]
            gtb = gts[c].astype(BF16)
            dqt = _dot(da[c], kt[sl]) + _dot(do[sl], stp.astype(BF16))
            dkt = _dot_tn(da[c], qt[sl])
            dks = _dot(v[sl], gtb) * eend[sl]
            dv_scr[sl, :] = _dot_tn(a[c], do[sl]) + _dot_nt(ke[sl], gtb)
            dq_scr[sl, :] = dqt * eb[sl]
            dk_scr[sl, :] = dkt * enb[sl] + dks
            ech = (jnp.sum(k[sl] * dks, axis=0, keepdims=True)
                   + jnp.sum(gts[c] * jnp.exp(bc[c]) * stp, axis=0, keepdims=True))
            e_scr[sl, :] = jnp.broadcast_to(ech, (C, H))
        dq = dq_scr[...]
        dk = dk_scr[...]
        dlf = _tri_dot(upper, q * dq - k * dk) + e_scr[...]
        dft = dlf / f - dk
        dz_ref[0] = (dq * _dsilu(qp, sq)).astype(BF16)
        dz_ref[1] = (dft * (1.0 - lb) * sg * (1.0 - sg)).astype(BF16)
        dz_ref[2] = dv_scr[...].astype(BF16)
        dlb_ref[...] += jnp.sum(dft * (1.0 - sg), axis=0, keepdims=True)

    col = lambda kk: (lambda h, n: (nb - 1 - n, kk * nh + h))
    return pl.pallas_call(
        body, name="hgrn_bwd", grid=(nh, nb),
        out_shape=(jax.ShapeDtypeStruct((4, s, e), BF16), jax.ShapeDtypeStruct((1, e), F32),
                   jax.ShapeDtypeStruct((1, H), F32)),
        in_specs=[pl.BlockSpec((tb, H), col(0)), pl.BlockSpec((tb, H), col(1)),
                  pl.BlockSpec((tb, H), col(2)), pl.BlockSpec((tb, H), col(3)),
                  pl.BlockSpec((tb, H), lambda h, n: (nb - 1 - n, h)),
                  pl.BlockSpec((None, nc, H, H), lambda h, n: (h, nb - 1 - n, 0, 0)),
                  pl.BlockSpec((2, H), lambda h, n: (0, h)), pl.BlockSpec((1, H), lambda h, n: (0, 0))],
        out_specs=(pl.BlockSpec((4, tb, H), lambda h, n: (0, nb - 1 - n, h)),
                   pl.BlockSpec((1, H), lambda h, n: (0, h)), pl.BlockSpec((1, H), lambda h, n: (0, 0))),
        scratch_shapes=[pltpu.VMEM((H, H), F32)] + [pltpu.VMEM((tb, H), F32)] * 5,
        compiler_params=_params("arbitrary", "arbitrary"))(z, z, z, z, dy, states, lb_logits, gnorm)


ATTN_T = 16 * ATTN_BAND
SCALE = ATTN_HEAD ** -0.5


def _slope(pair, hh, nheads):
    head = (2 * pair + hh + 1).astype(F32)
    return jnp.exp(jnp.full((1, 1), -8.0 / nheads * math.log(2.0), F32) * head)


def _fill_bias(bias, pair, nheads, delta, edge_ok):
    band = (delta >= 0) & (delta <= ATTN_BAND)
    dist = delta.astype(F32)
    for pi, dil in enumerate(DILATIONS):
        for hh in range(2):
            full = jnp.where(band, -(_slope(pair, hh, nheads) * float(dil)) * dist, NEG_BIG)
            bias[(pi * 2 + hh) * 2] = full
            bias[(pi * 2 + hh) * 2 + 1] = jnp.where(edge_ok, full, NEG_BIG)


def _rows(start, size, stride):
    if stride == 1:
        return pl.ds(pl.multiple_of(start, ATTN_BAND), size)
    return pl.ds(start, size, stride=stride)


def _head_lanes(rows, hh):
    return (lax.broadcasted_iota(jnp.int32, (rows, LANES), 1) // ATTN_HEAD) == hh


def _attn_fwd(z):
    s = z.shape[0]
    e = z.shape[1] // N_SPLITS
    npair = e // LANES
    T = ATTN_T
    assert s % T == 0
    nsb = s // T
    W = ATTN_BAND
    nt = T // W
    HD = ATTN_HEAD
    chunk = 256

    nsteps = npair * nsb
    assert nt % 2 == 0 and T // chunk == nt // 2

    def body(q_ref, kp_ref, kc_ref, vp_ref, vc_ref, g_ref, o_ref, l_ref, y_ref, qa, kbuf, va, bias, *sets):
        t = pl.program_id(0)
        sb = jnp.minimum(t, nsteps - 1) % nsb
        pair = jnp.minimum(t, nsteps - 1) // nsb
        sets = (sets[0:3], sets[3:6])

        def merge_chunk(i, done):
            accs, ms, lsw = done
            rows = pl.ds(pl.multiple_of(i * chunk, chunk), chunk)
            m1, m2, m3 = ms[0, rows, :], ms[1, rows, :], ms[2, rows, :]
            mx = jnp.maximum(jnp.maximum(m1, m2), m3)
            w1, w2, w3 = jnp.exp(m1 - mx), jnp.exp(m2 - mx), jnp.exp(m3 - mx)
            unswap = lambda a: pltpu.roll(a, ATTN_HEAD, 1)
            den = w1 * unswap(lsw[0, rows, :]) + w2 * unswap(lsw[1, rows, :]) + w3 * unswap(lsw[2, rows, :])
            o = (w1 * accs[0, rows, :] + w2 * accs[1, rows, :] + w3 * accs[2, rows, :]) / den
            o_ref[rows, :] = o
            l_ref[rows, :] = mx + jnp.log(den)
            gp = g_ref[rows, :]
            y_ref[rows, :] = (o * (gp * _sigmoid(gp))).astype(BF16)

        @pl.when(t == 0)
        def _():
            accs, ms, lsw = sets[1]
            accs[...] = jnp.zeros_like(accs)
            ms[...] = jnp.zeros_like(ms)
            lsw[...] = jnp.ones_like(lsw)

        @pl.when(t == nsteps)
        def _():
            def drain(i, carry):
                merge_chunk(i, sets[(nsteps - 1) % 2])
                return carry

            lax.fori_loop(0, T // chunk, drain, 0)

        def compute(cur, done):
            accs, ms, lsw = cur
            def stage(i, carry):
                rows = pl.ds(pl.multiple_of(i * chunk, chunk), chunk)
                upper = pl.ds(pl.multiple_of(T + i * chunk, chunk), chunk)
                kbuf[upper, :] = kc_ref[rows, :]
                for hh in range(2):
                    mine = _head_lanes(chunk, hh)
                    qa[hh, rows, :] = jnp.where(mine, q_ref[rows, :] * SCALE, 0.0)
                    va[hh, upper, :] = jnp.where(mine, vc_ref[rows, :], 1.0)
                return carry

            lax.fori_loop(0, T // chunk, stage, 0)

            @pl.when(sb == 0)
            def _():
                def stage_prev(i, carry):
                    rows = pl.ds(pl.multiple_of(i * chunk, chunk), chunk)
                    kbuf[rows, :] = kp_ref[rows, :]
                    for hh in range(2):
                        va[hh, rows, :] = jnp.where(_head_lanes(chunk, hh), vp_ref[rows, :], 1.0)
                    return carry

                lax.fori_loop(0, T // chunk, stage_prev, 0)

            @pl.when(sb == 0)
            def _():
                qi = lax.broadcasted_iota(jnp.int32, (W, 2 * W), 0)
                kj = lax.broadcasted_iota(jnp.int32, (W, 2 * W), 1)
                _fill_bias(bias, pair, 2 * npair, W + qi - kj, kj >= W)

            def tile(tau):
                first = _head_lanes(W, 0)
                rows, scores = [], []
                for pi, dil in enumerate(DILATIONS):
                    r = tau % dil
                    ub = tau // dil
                    qrows = _rows(r + dil * W * ub, W, dil)
                    krows = _rows(T + dil * W * (ub - 1) + r, 2 * W, dil)
                    var = jnp.where((sb == 0) & (ub == 0), 1, 0)
                    kt = kbuf[krows, :].astype(BF16)
                    rows.append((qrows, krows))
                    scores.append([_dot_nt(qa[hh, qrows, :].astype(BF16), kt) + bias[(pi * 2 + hh) * 2 + var]
                                   for hh in range(2)])
                maxes = [[jnp.max(sc, axis=-1, keepdims=True) for sc in pair_] for pair_ in scores]
                probs = [[jnp.exp(sc - m).astype(BF16) for sc, m in zip(ps, pm)] for ps, pm in zip(scores, maxes)]
                for pi, (qrows, krows) in enumerate(rows):
                    outs = [_dot(probs[pi][hh], va[hh, krows, :].astype(BF16)) for hh in range(2)]
                    accs[pi, qrows, :] = jnp.where(first, outs[0], outs[1])
                    lsw[pi, qrows, :] = jnp.where(first, outs[1], outs[0])
                    ms[pi, qrows, :] = jnp.where(first, maxes[pi][0], maxes[pi][1])

            def two_tiles(i, carry):
                tile(2 * i)
                tile(2 * i + 1)
                merge_chunk(i, done)
                return carry

            lax.fori_loop(0, nt // 2, two_tiles, 0)

            def move_down(i, carry):
                rows = pl.ds(pl.multiple_of(i * chunk, chunk), chunk)
                upper = pl.ds(pl.multiple_of(T + i * chunk, chunk), chunk)
                kbuf[rows, :] = kbuf[upper, :]
                for hh in range(2):
                    va[hh, rows, :] = va[hh, upper, :]
                return carry

            lax.fori_loop(0, T // chunk, move_down, 0)

        for parity in range(2):
            @pl.when((t < nsteps) & (t % 2 == parity))
            def _():
                compute(sets[parity], sets[1 - parity])

    step_of = lambda t: jnp.minimum(t, nsteps - 1)
    lag_of = lambda t: jnp.maximum(t - 1, 0)
    cur = lambda split: (lambda t: (step_of(t) % nsb, split * npair + step_of(t) // nsb))
    prev = lambda split: (lambda t: (0, split * npair + step_of(t) // nsb))
    blk = lambda index: pl.BlockSpec((T, LANES), index)
    out = blk(lambda t: (lag_of(t) % nsb, lag_of(t) // nsb))
    gate = blk(lambda t: (lag_of(t) % nsb, 7 * npair + lag_of(t) // nsb))
    buf = lambda rows: pltpu.VMEM((rows, LANES), F32)
    return pl.pallas_call(
        body, name="attn_fwd", grid=(nsteps + 1,),
        out_shape=(jax.ShapeDtypeStruct((s, e), F32), jax.ShapeDtypeStruct((s, e), F32), jax.ShapeDtypeStruct((s, e), BF16)),
        in_specs=[blk(cur(4)), blk(prev(5)), blk(cur(5)), blk(prev(6)), blk(cur(6)), gate],
        out_specs=(out, out, out),
        scratch_shapes=[pltpu.VMEM((2, T, LANES), F32), buf(2 * T), pltpu.VMEM((2, 2 * T, LANES), F32),
                        pltpu.VMEM((12, W, 2 * W), F32)] + [pltpu.VMEM((3, T, LANES), F32)] * 6,
        compiler_params=_params("arbitrary"))(z, z, z, z, z, z)


def _outproj_loss(x, y_h, y_a, w_out_full, final_gain, target):
    s, d = x.shape
    e = y_h.shape[1]
    tm = _tile(s, 512)
    sub = _tile(tm, 256)

    def body(x_ref, yh_ref, ya_ref, w_ref, g_ref, t_ref, dx_ref, dxb_ref, dy_ref, loss_ref, dg_ref):
        @pl.when(pl.program_id(0) == 0)
        def _():
            loss_ref[...] = jnp.zeros_like(loss_ref)
            dg_ref[...] = jnp.zeros_like(dg_ref)

        w = w_ref[...]
        g = g_ref[...]
        parts = [slice(r0, r0 + sub) for r0 in range(0, tm, sub)]
        x2s = [x_ref[rows, :] + _dot(yh_ref[rows, :], w[0:e]) + _dot(ya_ref[rows, :], w[e:2 * e]) for rows in parts]
        for rows, x2 in zip(parts, x2s):
            r = lax.rsqrt(jnp.mean(x2 * x2, axis=-1, keepdims=True) + NORM_EPS)
            xn = x2 * r
            err = xn * g - t_ref[rows, :]
            loss_ref[...] += jnp.sum(err * err, axis=0, keepdims=True) * (0.5 / d)
            dyo = err * (1.0 / d)
            dg_ref[...] += jnp.sum(dyo * xn, axis=0, keepdims=True)
            u = dyo * g
            dx2 = r * (u - xn * jnp.mean(u * xn, axis=-1, keepdims=True))
            dx_ref[rows, :] = dx2
            dxb = dx2.astype(BF16)
            dxb_ref[rows, :] = dxb
            dy_ref[rows, :] = _dot_nt(dxb, w)

    row = pl.BlockSpec((tm, d), lambda i: (i, 0))
    half = pl.BlockSpec((tm, e), lambda i: (i, 0))
    vec = pl.BlockSpec((1, d), lambda i: (0, 0))
    whole = pl.BlockSpec((2 * e, d), lambda i: (0, 0), pipeline_mode=pl.Buffered(1))
    return pl.pallas_call(
        body, name="outproj_loss", grid=(s // tm,),
        out_shape=(jax.ShapeDtypeStruct((s, d), F32), jax.ShapeDtypeStruct((s, d), BF16),
                   jax.ShapeDtypeStruct((s, 2 * e), F32), jax.ShapeDtypeStruct((1, d), F32),
                   jax.ShapeDtypeStruct((1, d), F32)),
        in_specs=[row, half, half, whole, vec, row],
        out_specs=(row, row, pl.BlockSpec((tm, 2 * e), lambda i: (i, 0)), vec, vec),
        compiler_params=pltpu.CompilerParams(dimension_semantics=("arbitrary",), vmem_limit_bytes=OUTPROJ_VMEM_LIMIT),
    )(x, y_h, y_a, w_out_full, final_gain, target)


def _attn_bwd(z, dy, o, lse):
    s, e = o.shape
    npair = e // LANES
    T = ATTN_T
    assert s % T == 0
    nsb = s // T
    W = ATTN_BAND
    nt = T // W
    HD = ATTN_HEAD
    chunk = 256

    def body(k_ref, v_ref, qc_ref, qn_ref, dyc_ref, dyn_ref, gc_ref, gn_ref, oc_ref, on_ref, lc_ref, ln_ref,
             dz_ref, qa, doa, ka, va, dqacc, dkacc, dvacc, bias):
        sb = pl.program_id(1)
        def stage_queries(half, q_r, dy_r, g_r, o_r, l_r):
            def stage(i, carry):
                rows = pl.ds(pl.multiple_of(i * chunk, chunk), chunk)
                dst = pl.ds(pl.multiple_of(half * T + i * chunk, chunk), chunk)
                lane = lax.broadcasted_iota(jnp.int32, (chunk, LANES), 1)
                gp = g_r[rows, :]
                dov = dy_r[rows, :] * (gp * _sigmoid(gp))
                qv = q_r[rows, :] * SCALE
                same_head = (lax.broadcasted_iota(jnp.int32, (LANES, LANES), 0) // HD
                             == lax.broadcasted_iota(jnp.int32, (LANES, LANES), 1) // HD)
                ones = jnp.where(same_head, 1.0, 0.0).astype(BF16)
                hi, mid, lo = (p.astype(BF16) for p in _split3(dov * o_r[rows, :]))
                delta = _dot(hi, ones) + _dot(mid, ones) + _dot(lo, ones)
                swap = lambda a: pltpu.roll(a, HD, 1)
                lse_parts = [swap(p) for p in _split3(l_r[rows, :])]
                dl_parts = [swap(p) for p in _split3(delta)]
                for hh in range(2):
                    mine = _head_lanes(chunk, hh)
                    spare = (1 - hh) * HD
                    qh = jnp.where(mine, qv, 0.0)
                    dh = jnp.where(mine, dov, 0.0)
                    for j in range(3):
                        qh = jnp.where(lane == spare + j, lse_parts[j], qh)
                        dh = jnp.where(lane == spare + j, dl_parts[j], dh)
                    qa[hh, dst, :] = qh
                    doa[hh, dst, :] = dh
                return carry

            lax.fori_loop(0, T // chunk, stage, 0)

        @pl.when(sb == 0)
        def _():
            stage_queries(0, qc_ref, dyc_ref, gc_ref, oc_ref, lc_ref)

        stage_queries(1, qn_ref, dyn_ref, gn_ref, on_ref, ln_ref)

        def stage_keys(i, carry):
            rows = pl.ds(pl.multiple_of(i * chunk, chunk), chunk)
            lane = lax.broadcasted_iota(jnp.int32, (chunk, LANES), 1)
            for hh in range(2):
                spare = (1 - hh) * HD
                minus = (lane >= spare) & (lane < spare + 3)
                ka[hh, rows, :] = jnp.where(minus, -1.0, k_ref[rows, :])
                va[hh, rows, :] = jnp.where(minus, -1.0, v_ref[rows, :])
            gp = gc_ref[rows, :]
            dz_ref[3, rows, :] = (dyc_ref[rows, :] * oc_ref[rows, :] * _dsilu(gp, _sigmoid(gp))).astype(BF16)
            return carry

        lax.fori_loop(0, T // chunk, stage_keys, 0)

        @pl.when(sb == 0)
        def _():
            dqacc[0:T, :] = jnp.zeros((T, LANES), F32)

        dqacc[T:, :] = jnp.zeros((T, LANES), F32)
        dkacc[...] = jnp.zeros_like(dkacc)
        dvacc[...] = jnp.zeros_like(dvacc)
        @pl.when(sb == 0)
        def _():
            qi = lax.broadcasted_iota(jnp.int32, (2 * W, W), 0)
            kj = lax.broadcasted_iota(jnp.int32, (2 * W, W), 1)
            _fill_bias(bias, pl.program_id(0), 2 * npair, qi - kj, qi < W)

        def tile(tau, carry):
            def scores(step, pi):
                dil = DILATIONS[pi]
                r = step % dil
                ub = step // dil
                start = r + dil * W * ub
                krows = _rows(start, W, dil)
                qrows = _rows(start, 2 * W, dil)
                var = jnp.where((sb == nsb - 1) & (ub == nt // dil - 1), 1, 0)
                unit = dict(krows=krows, qrows=qrows, ops=[], sc=[], dpd=[])
                for hh in range(2):
                    kt = ka[hh, krows, :].astype(BF16)
                    vt = va[hh, krows, :].astype(BF16)
                    qt = qa[hh, qrows, :].astype(BF16)
                    dt = doa[hh, qrows, :].astype(BF16)
                    unit["ops"].append((kt, qt, dt))
                    unit["sc"].append(_dot_nt(qt, kt) + bias[(pi * 2 + hh) * 2 + var])
                    unit["dpd"].append(_dot_nt(dt, vt))
                return unit

            def elementwise(unit):
                ps = [jnp.exp(s_) for s_ in unit["sc"]]
                unit["ds"] = [(p * d).astype(BF16) for p, d in zip(ps, unit["dpd"])]
                unit["pb"] = [p.astype(BF16) for p in ps]

            def products(unit):
                dvs = [_dot_tn(pb, dt) for pb, (kt, qt, dt) in zip(unit["pb"], unit["ops"])]
                dks = [_dot_tn(ds, qt) for ds, (kt, qt, dt) in zip(unit["ds"], unit["ops"])]
                dqs = [_dot(ds, kt) for ds, (kt, qt, dt) in zip(unit["ds"], unit["ops"])]
                dkacc[unit["krows"], :] += jnp.where(_head_lanes(W, 0), dks[0], dks[1])
                dvacc[unit["krows"], :] += jnp.where(_head_lanes(W, 0), dvs[0], dvs[1])
                dqacc[unit["qrows"], :] += jnp.where(_head_lanes(2 * W, 0), dqs[0], dqs[1]) * SCALE

            order = [(2 * tau + half, pi) for half in range(2) for pi in range(len(DILATIONS))]
            units = [None] * len(order)
            for n in range(len(order) + 2):
                if n < len(order):
                    units[n] = scores(*order[n])
                if 1 <= n <= len(order):
                    elementwise(units[n - 1])
                if n >= 2:
                    products(units[n - 2])
            return carry

        lax.fori_loop(0, nt // 2, tile, 0)

        def flush(i, carry):
            rows = pl.ds(pl.multiple_of(i * chunk, chunk), chunk)
            nxt = pl.ds(pl.multiple_of(T + i * chunk, chunk), chunk)
            dz_ref[0, rows, :] = dqacc[rows, :].astype(BF16)
            dz_ref[1, rows, :] = dkacc[rows, :].astype(BF16)
            dz_ref[2, rows, :] = dvacc[rows, :].astype(BF16)
            dqacc[rows, :] = dqacc[nxt, :]
            for hh in range(2):
                qa[hh, rows, :] = qa[hh, nxt, :]
                doa[hh, rows, :] = doa[hh, nxt, :]
            return carry

        lax.fori_loop(0, T // chunk, flush, 0)

    zc = lambda split: (lambda hp, sb: (sb, split * npair + hp))
    zn = lambda split: (lambda hp, sb: (jnp.minimum(sb + 1, nsb - 1), split * npair + hp))
    ec = lambda off: (lambda hp, sb: (sb, off + hp))
    en = lambda off: (lambda hp, sb: (jnp.minimum(sb + 1, nsb - 1), off + hp))
    z0 = lambda split: (lambda hp, sb: (0, split * npair + hp))
    e0 = lambda off: (lambda hp, sb: (0, off + hp))
    blk = lambda index: pl.BlockSpec((T, LANES), index)
    buf = lambda rows: pltpu.VMEM((rows, LANES), F32)
    return pl.pallas_call(
        body, name="attn_bwd", grid=(npair, nsb), out_shape=jax.ShapeDtypeStruct((4, s, e), BF16),
        in_specs=[blk(zc(5)), blk(zc(6)), blk(z0(4)), blk(zn(4)), blk(ec(npair)), blk(en(npair)),
                  blk(zc(7)), blk(zn(7)), blk(ec(0)), blk(en(0)), blk(e0(0)), blk(en(0))],
        out_specs=pl.BlockSpec((4, T, LANES), lambda hp, sb: (0, sb, hp)),
        scratch_shapes=[pltpu.VMEM((2, 2 * T, LANES), F32), pltpu.VMEM((2, 2 * T, LANES), F32),
                        pltpu.VMEM((2, T, LANES), F32), pltpu.VMEM((2, T, LANES), F32),
                        buf(2 * T), buf(T), buf(T), pltpu.VMEM((12, 2 * W, W), F32)],
        compiler_params=_params("parallel", "arbitrary"))(z, z, z, z, dy, dy, z, z, o, o, lse, lse)


def _dz_specs(tm, e):
    def mk(lo, hi):
        return pl.BlockSpec((None, tm, e), lambda i, k: (jnp.clip(k - lo, 0, hi - lo - 1), i, 0))
    return [mk(0, 4), mk(4, 8)]


def _dz_pick(grp, dzh_ref, dza_ref, fn):
    @pl.when(grp < 4)
    def _():
        fn(dzh_ref[...])

    @pl.when(grp >= 4)
    def _():
        fn(dza_ref[...])


def _dh_dx(dzh, dza, w_full, x, gain, dx2):
    s, d = x.shape
    e = dzh.shape[2]
    tm = _tile(s, 1024)
    ni = s // tm
    chunk = _tile(tm, 128)
    fetch_at = 2

    def body(dzh_ref, dza_ref, w_ref, x_hbm, g_ref, dx2_hbm, gx_hbm, dg_ref, acc, xbuf, dbuf, sems):
        i, k = pl.program_id(0), pl.program_id(1)
        rows_of = lambda tile: pl.ds(pl.multiple_of(tile * tm, tm), tm)
        fetch_x = pltpu.make_async_copy(x_hbm.at[rows_of(i), :], xbuf, sems.at[0])
        fetch_d = pltpu.make_async_copy(dx2_hbm.at[rows_of(i), :], dbuf, sems.at[1])

        def store(tile):
            return pltpu.make_async_copy(xbuf, gx_hbm.at[rows_of(tile), :], sems.at[2])

        @pl.when((i == 0) & (k == 0))
        def _():
            dg_ref[...] = jnp.zeros_like(dg_ref)

        @pl.when((k == fetch_at) & (i > 0))
        def _():
            store(i).wait()

        @pl.when(k == fetch_at)
        def _():
            fetch_x.start()
            fetch_d.start()

        def finish(tile):
            fetch_x.wait()
            fetch_d.wait()
            gain_row = g_ref[...]
            for r0 in range(0, tm, chunk):
                rows = slice(r0, r0 + chunk)
                dh = acc[rows, :]
                xv = xbuf[rows, :]
                r = lax.rsqrt(jnp.mean(xv * xv, axis=-1, keepdims=True) + NORM_EPS)
                xn = xv * r
                u = dh * gain_row
                xbuf[rows, :] = dbuf[rows, :] + r * (u - xn * jnp.mean(u * xn, axis=-1, keepdims=True))
                dg_ref[...] += jnp.sum(dh * xn, axis=0, keepdims=True)
            store(tile).start()

        @pl.when((k == 0) & (i == 0))
        def _():
            acc[...] = _dot_nt(dzh_ref[...], w_ref[...])

        @pl.when((k == 0) & (i > 0))
        def _():
            finish(i - 1)
            acc[...] = _dot_nt(dzh_ref[...], w_ref[...])

        @pl.when(k > 0)
        def _():
            def add(dz):
                acc[...] += _dot_nt(dz, w_ref[...])

            _dz_pick(k, dzh_ref, dza_ref, add)

        @pl.when((k == N_SPLITS - 1) & (i == ni - 1))
        def _():
            finish(i)
            store(i).wait()

    vec = pl.BlockSpec((1, d), lambda i, k: (0, 0))
    return pl.pallas_call(
        body, name="dh_dx", grid=(ni, N_SPLITS),
        out_shape=(jax.ShapeDtypeStruct((s, d), F32), jax.ShapeDtypeStruct((1, d), F32)),
        in_specs=_dz_specs(tm, e) + [pl.BlockSpec((None, d, e), lambda i, k: (k, 0, 0)), ANY, vec, ANY],
        out_specs=(ANY, vec),
        scratch_shapes=[pltpu.VMEM((tm, d), F32)] * 3 + [pltpu.SemaphoreType.DMA((3,))],
        compiler_params=pltpu.CompilerParams(dimension_semantics=("arbitrary", "arbitrary"),
                                             vmem_limit_bytes=DHDX_VMEM_LIMIT))(dzh, dza, w_full, x, gain, dx2)


def _position():
    x, y, c = lax.axis_index("x"), lax.axis_index("y"), lax.axis_index("c")
    return x, y, c


def _xor_peer(x, y, c, mask):
    return (x ^ ((mask >> 2) & 1), y ^ ((mask >> 1) & 1), c ^ (mask & 1))


def _block_order(masks):
    me = 4 * lax.axis_index("x") + 2 * lax.axis_index("y") + lax.axis_index("c")
    return jnp.stack([me ^ m for m in masks]).astype(jnp.int32)


GATHER_MASKS = (0, 1, 4, 5, 2, 3, 6, 7)


def _inproj_gather(h, w_loc, wo_loc):
    s, d = h.shape
    e = w_loc.shape[1]
    tm = _tile(s, 1024)
    ni = s // tm
    pre = max(ni - 2, 0)

    def body(order_ref, h_ref, w_ref, wo_ref, z_ref, wf_ref, wof_ref, wbuf, send_sems, recv_sems, osend, orecv,
             local_sems, wsems):
        j, i = pl.program_id(0), pl.program_id(1)
        x, y, c = _position()
        me, sibling = (x, y, c), (x, y, 1 - c)
        chips = [(1 - x, y), (x, 1 - y), (1 - x, 1 - y)]
        blk = lambda p: 4 * p[0] + 2 * p[1] + p[2]

        def copy(k, block, to, src=None):
            dst = wf_ref.at[blk(block)]
            return pltpu.make_async_remote_copy(
                src_ref=dst if src is None else src, dst_ref=dst, send_sem=send_sems.at[k], recv_sem=recv_sems.at[k],
                device_id=to, device_id_type=MESH)

        first = [copy(0, me, sibling, src=w_ref)] + [copy(1 + q, me, (*chip, c), src=w_ref) for q, chip in enumerate(chips)]
        passed = [copy(4 + q, (*chip, c), sibling) for q, chip in enumerate(chips)]
        mine = pltpu.make_async_copy(w_ref, wf_ref.at[blk(me)], local_sems.at[0])
        ocopies = [pltpu.make_async_remote_copy(
            src_ref=wo_ref, dst_ref=wof_ref.at[blk(me)], send_sem=osend.at[k], recv_sem=orecv.at[k],
            device_id=_xor_peer(x, y, c, k + 1), device_id_type=MESH) for k in range(N_DEV - 1)]
        omine = pltpu.make_async_copy(wo_ref, wof_ref.at[blk(me)], local_sems.at[1])
        blocks = [me, sibling] + [(*chip, c) for chip in chips] + [(*chip, 1 - c) for chip in chips]
        arrive = [None, copy(0, sibling, me)] + [copy(1 + q, (*chip, c), me) for q, chip in enumerate(chips)] \
            + [copy(4 + q, (*chip, 1 - c), me) for q, chip in enumerate(chips)]
        forward = [None, None] + passed + [None, None, None]
        use_order = (0, 1, 2, 5, 3, 6, 4, 7)
        blocks, arrive, forward = ([lst[n] for n in use_order] for lst in (blocks, arrive, forward))

        def load(slot, src):
            return pltpu.make_async_copy(src, wbuf.at[slot], wsems.at[slot])

        @pl.when((j == 0) & (i == 0))
        def _():
            for cp in [mine, omine] + first + ocopies:
                cp.start()
            load(0, w_ref).start()

        for jj in range(N_DEV):
            @pl.when((j == jj) & (i == 0))
            def _():
                load(jj % 2, w_ref).wait()

            if jj + 1 < N_DEV:
                @pl.when((j == jj) & (i == pre))
                def _():
                    arrive[jj + 1].wait_recv()
                    if forward[jj + 1] is not None:
                        forward[jj + 1].start()
                    load((jj + 1) % 2, wf_ref.at[blk(blocks[jj + 1])]).start()

        z_ref[...] = _dot(h_ref[...], wbuf[j % 2])

        @pl.when((j == N_DEV - 1) & (i == ni - 1))
        def _():
            for cp in first + passed:
                cp.wait_send()
            for cp in ocopies:
                cp.wait_send()
                cp.wait_recv()
            mine.wait()
            omine.wait()

    grid_spec = pltpu.PrefetchScalarGridSpec(
        num_scalar_prefetch=1, grid=(N_DEV, ni),
        in_specs=[pl.BlockSpec((tm, d), lambda j, i, o: (i, 0)), ANY, ANY],
        out_specs=(pl.BlockSpec((tm, e), lambda j, i, o: (i, o[j])), ANY, ANY),
        scratch_shapes=[pltpu.VMEM((2, d, e), BF16), pltpu.SemaphoreType.DMA((7,)), pltpu.SemaphoreType.DMA((7,)),
                        pltpu.SemaphoreType.DMA((7,)), pltpu.SemaphoreType.DMA((7,)), pltpu.SemaphoreType.DMA((2,)),
                        pltpu.SemaphoreType.DMA((2,))])
    return pl.pallas_call(
        body, name="inproj_gather", grid_spec=grid_spec,
        out_shape=(jax.ShapeDtypeStruct((s, N_SPLITS * e), F32), jax.ShapeDtypeStruct((N_DEV, d, e), BF16),
                   jax.ShapeDtypeStruct((N_DEV,) + wo_loc.shape, BF16)),
        compiler_params=_params("arbitrary", "arbitrary"))(_block_order(GATHER_MASKS), h, w_loc, wo_loc)


SCATTER_MASKS = (7, 6, 5, 4, 3, 2, 1, 0)
N_CHIPS = 4


def _scatter_block(k, acc, stage, tmp, own_ref, ra_ref, rb_ref, sa_send, sa_recv, sb_send, sb_recv, loc_sem, step, ns):
    x, y, c = _position()
    chip_of = lambda t: _xor_peer(x, y, c, SCATTER_MASKS[2 * t + 1])
    last = step == ns - 1
    fetch_at = min(1, ns - 1)

    def ship(t):
        return pltpu.make_async_remote_copy(
            src_ref=stage.at[0], dst_ref=ra_ref.at[t], send_sem=sa_send.at[t], recv_sem=sa_recv.at[t],
            device_id=(x, y, 1 - c), device_id_type=MESH)

    def send(t):
        return pltpu.make_async_remote_copy(
            src_ref=stage.at[1], dst_ref=rb_ref.at[t], send_sem=sb_send.at[t], recv_sem=sb_recv.at[t],
            device_id=chip_of(t), device_id_type=MESH)

    for kk in range(N_DEV):
        t = kk // 2
        fetch = pltpu.make_async_copy(ra_ref.at[t], tmp, loc_sem)

        if kk % 2 == 1:
            @pl.when((step == fetch_at) & (k == kk))
            def _():
                ship(t).wait_recv()
                fetch.start()

        @pl.when(last & (k == kk))
        def _():
            if kk % 2 == 0:
                if t >= 1:
                    ship(t - 1).wait_send()
                stage[0] = acc[...].astype(BF16)
                ship(t).start()
            else:
                fetch.wait()
                acc[...] += tmp[...].astype(F32)
                if t < N_CHIPS - 1:
                    if t >= 1:
                        send(t - 1).wait_send()
                    stage[1] = acc[...].astype(BF16)
                    send(t).start()
                else:
                    keep = pltpu.make_async_copy(acc, own_ref, loc_sem)
                    keep.start()
                    keep.wait()
                    ship(t).wait_send()
                    send(t - 1).wait_send()
                    for q in range(N_CHIPS - 1):
                        send(q).wait_recv()


def _scatter_scratch(rows, cols):
    return [pltpu.VMEM((rows, cols), F32), pltpu.VMEM((2, rows, cols), BF16), pltpu.VMEM((rows, cols), BF16),
            pltpu.SemaphoreType.DMA((N_CHIPS,)), pltpu.SemaphoreType.DMA((N_CHIPS,)),
            pltpu.SemaphoreType.DMA((N_CHIPS - 1,)), pltpu.SemaphoreType.DMA((N_CHIPS - 1,)), pltpu.SemaphoreType.DMA(())]


def _scatter_out(rows, cols):
    return (jax.ShapeDtypeStruct((rows, cols), F32), jax.ShapeDtypeStruct((N_CHIPS, rows, cols), BF16),
            jax.ShapeDtypeStruct((N_CHIPS - 1, rows, cols), BF16))


def _dwin_scatter(h, dzh, dza):
    s, d = h.shape
    e = dzh.shape[2]
    ts = _tile(s, 1024)
    ns = s // ts

    def body(order_ref, dzh_ref, dza_ref, h_ref, own_ref, ra_ref, rb_ref, acc, stage, tmp, *sems):
        k, step = pl.program_id(0), pl.program_id(1)

        @pl.when(step == 0)
        def _():
            acc[...] = jnp.zeros_like(acc)

        def add(dz):
            acc[...] += _dot_tn(h_ref[...], dz)

        _dz_pick(order_ref[k], dzh_ref, dza_ref, add)
        _scatter_block(k, acc, stage, tmp, own_ref, ra_ref, rb_ref, *sems, step, ns)

    def dz_spec(lo):
        return pl.BlockSpec((None, ts, e), lambda k, st, o: (jnp.clip(o[k] - lo, 0, 3), st, 0))

    grid_spec = pltpu.PrefetchScalarGridSpec(
        num_scalar_prefetch=1, grid=(N_DEV, ns),
        in_specs=[dz_spec(0), dz_spec(4), pl.BlockSpec((ts, d), lambda k, st, o: (st, 0))],
        out_specs=(ANY, ANY, ANY), scratch_shapes=_scatter_scratch(d, e))
    own, _, rb = pl.pallas_call(
        body, name="dwin_scatter", grid_spec=grid_spec, out_shape=_scatter_out(d, e),
        compiler_params=_params("arbitrary", "arbitrary"))(_block_order(SCATTER_MASKS), dzh, dza, h)
    return own, rb


def _dwout_scatter(y_h, y_a, dxb):
    s, e = y_h.shape
    d = dxb.shape[1]
    r = 2 * e // N_DEV
    pairs = e // (2 * r)
    ts = _tile(s, 1024)
    ns = s // ts
    chip_masks = SCATTER_MASKS[1::2]
    passes = ((0, 1), (2,), (3,))
    slots = max(len(chips) for chips in passes)
    slot_chip = [chips[min(u, len(chips) - 1)] for chips in passes for u in range(slots)]

    def body(pair_ref, yh0_ref, ya0_ref, yh1_ref, ya1_ref, dx_ref, own_ref, ra_ref, rb_ref, acc, keep_buf, ship_buf,
             send_buf, tmp, sa_send, sa_recv, sb_send, sb_recv, loc_sem):
        p, step = pl.program_id(0), pl.program_id(1)
        x, y, c = _position()

        @pl.when(step == 0)
        def _():
            acc[...] = jnp.zeros_like(acc)

        for u, (yh_ref, ya_ref) in enumerate(((yh0_ref, ya0_ref), (yh1_ref, ya1_ref))):
            rows = slice(u * 2 * r, (u + 1) * 2 * r)
            used = functools.reduce(jnp.logical_or, [p == pp for pp, chips in enumerate(passes) if u < len(chips)])

            @pl.when(used & (pair_ref[slots * p + u] < pairs))
            def _():
                acc[rows, :] += _dot_tn(yh_ref[...], dx_ref[...])

            @pl.when(used & (pair_ref[slots * p + u] >= pairs))
            def _():
                acc[rows, :] += _dot_tn(ya_ref[...], dx_ref[...])

        def block_rows(u, core):
            return pl.ds(pl.multiple_of(u * 2 * r + core * r, r), r)

        slot_of = {q: u for chips in passes for u, q in enumerate(chips)}

        def ship(q):
            return pltpu.make_async_remote_copy(
                src_ref=ship_buf.at[slot_of[q]], dst_ref=ra_ref.at[q], send_sem=sa_send.at[q], recv_sem=sa_recv.at[q],
                device_id=(x, y, 1 - c), device_id_type=MESH)

        def send(q):
            return pltpu.make_async_remote_copy(
                src_ref=send_buf.at[slot_of[q]], dst_ref=rb_ref.at[q], send_sem=sb_send.at[q], recv_sem=sb_recv.at[q],
                device_id=_xor_peer(x, y, c, chip_masks[q]), device_id_type=MESH)

        def sibling_share(q):
            ship(q).wait_recv()
            fetch = pltpu.make_async_copy(ra_ref.at[q], tmp, loc_sem)
            fetch.start()
            fetch.wait()
            return tmp[...].astype(F32)

        shipped, sent = {}, {}
        for pp, chips in enumerate(passes):
            @pl.when((step == ns - 1) & (p == pp))
            def _():
                for u, q in enumerate(chips):
                    if u in shipped:
                        ship(shipped.pop(u)).wait_send()
                    ship_buf[u] = acc[block_rows(u, 1 - c), :].astype(BF16)
                    ship(q).start()
                    shipped[u] = q
                for u, q in enumerate(chips):
                    total = acc[block_rows(u, c), :] + sibling_share(q)
                    if q < N_CHIPS - 1:
                        if u in sent:
                            send(sent.pop(u)).wait_send()
                        send_buf[u] = total.astype(BF16)
                        send(q).start()
                        sent[u] = q
                    else:
                        keep_buf[...] = total
                        keep = pltpu.make_async_copy(keep_buf, own_ref, loc_sem)
                        keep.start()
                        keep.wait()
                if pp == len(passes) - 1:
                    for q in shipped.values():
                        ship(q).wait_send()
                    for q in sent.values():
                        send(q).wait_send()
                    for q in range(N_CHIPS - 1):
                        send(q).wait_recv()

    def y_spec(u, lo):
        return pl.BlockSpec((ts, 2 * r), lambda p, st, o: (st, jnp.clip(o[slots * p + u] - lo, 0, pairs - 1)))

    pair_of_chip = _block_order(chip_masks) // 2
    grid_spec = pltpu.PrefetchScalarGridSpec(
        num_scalar_prefetch=1, grid=(len(passes), ns),
        in_specs=[y_spec(0, 0), y_spec(0, pairs), y_spec(1, 0), y_spec(1, pairs),
                  pl.BlockSpec((ts, d), lambda p, st, o: (st, 0))],
        out_specs=(ANY, ANY, ANY),
        scratch_shapes=[pltpu.VMEM((slots * 2 * r, d), F32), pltpu.VMEM((r, d), F32),
                        pltpu.VMEM((slots, r, d), BF16)] + _scatter_scratch(r, d)[1:])
    own, _, rb = pl.pallas_call(
        body, name="dwout_scatter", grid_spec=grid_spec, out_shape=_scatter_out(r, d),
        compiler_params=_params("arbitrary", "arbitrary"))(
            jnp.stack([pair_of_chip[q] for q in slot_chip]), y_h, y_a, y_h, y_a, dxb)
    return own, rb


def _sum_chips_adamw(own, recv, w, m, v):
    r, c = w.shape
    tr = _tile(r, 128)

    def body(own_ref, rc_ref, w_ref, m_ref, v_ref, g_ref, d_ref, mo_ref, vo_ref):
        g = own_ref[...]
        for q in range(N_CHIPS - 1):
            g = g + rc_ref[q].astype(F32)
        g_ref[...] = g
        d_ref[...], mo_ref[...], vo_ref[...] = _adamw(w_ref[...], g, m_ref[...], v_ref[...])

    blk = pl.BlockSpec((tr, c), lambda i: (i, 0))
    shp = jax.ShapeDtypeStruct((r, c), F32)
    return pl.pallas_call(
        body, name="sum_chips_adamw", grid=(r // tr,), out_shape=(shp, shp, shp, shp),
        in_specs=[blk, pl.BlockSpec((N_CHIPS - 1, tr, c), lambda i: (0, i, 0)), blk, blk, blk],
        out_specs=(blk, blk, blk, blk), compiler_params=_params("parallel"))(own, recv, w, m, v)


SMALL_ROWS = 8
ROW_LB = 4
ROW_GN = 6
ROW_LOSS = 7


def _small_allreduce_adamw(part, w, m, v, lb_logits):
    width = part.shape[1]

    def body(p_ref, w_ref, m_ref, v_ref, lb_ref, g_ref, d_ref, mo_ref, vo_ref, buf, send_sems, recv_sems):
        x, y, c = _position()
        me = 4 * x + 2 * y + c
        buf[me] = p_ref[...]
        copies = []
        for k in range(N_DEV - 1):
            bx, by, bc = ((k + 1) >> 2) & 1, ((k + 1) >> 1) & 1, (k + 1) & 1
            peer = (x ^ bx, y ^ by, c ^ bc)
            copies.append(pltpu.make_async_remote_copy(
                src_ref=p_ref, dst_ref=buf.at[me], send_sem=send_sems.at[k], recv_sem=recv_sems.at[k],
                device_id=peer, device_id_type=MESH))
        for cp in copies:
            cp.start()
        for cp in copies:
            cp.wait_recv()
        for cp in copies:
            cp.wait_send()
        tot = buf[0]
        for dev in range(1, N_DEV):
            tot = tot + buf[dev]
        lbv = lb_ref[...]
        lb = _sigmoid(lbv[0:1] - lbv[1:2])
        glb = tot[ROW_LB:ROW_LB + 1] * lb * (1.0 - lb)
        loss = jnp.sum(tot[ROW_LOSS:ROW_LOSS + 1], axis=-1, keepdims=True)
        row = lax.broadcasted_iota(jnp.int32, (SMALL_ROWS, width), 0)
        g = jnp.where(row == ROW_LB, glb, jnp.where(row == ROW_LB + 1, -glb, tot))
        g = jnp.where(row == ROW_LOSS, loss, g)
        g_ref[...] = g
        d_ref[...], mo_ref[...], vo_ref[...] = _adamw(w_ref[...], g, m_ref[...], v_ref[...])

    vm = pl.BlockSpec(memory_space=pltpu.VMEM)
    shp = jax.ShapeDtypeStruct((SMALL_ROWS, width), F32)
    return pl.pallas_call(
        body, name="small_allreduce_adamw", out_shape=(shp, shp, shp, shp),
        in_specs=[vm] * 5, out_specs=(vm, vm, vm, vm),
        scratch_shapes=[pltpu.VMEM((N_DEV, SMALL_ROWS, width), F32), pltpu.SemaphoreType.DMA((N_DEV - 1,)),
                        pltpu.SemaphoreType.DMA((N_DEV - 1,))],
    )(part, w, m, v, lb_logits)


def _pack_small(norm_gain, final_gain, lb2, gnorm, last_row, width):
    pad = lambda a: jnp.pad(a.reshape(1, -1), ((0, 0), (0, width - a.size)))
    return jnp.concatenate([norm_gain.reshape(2, width), final_gain.reshape(2, width), lb2.reshape(2, width),
                            pad(gnorm), last_row.reshape(1, width)], axis=0)


def _unpack_small(p, d, e, hd):
    return (p[0:2].reshape(1, d), p[2:4].reshape(d), p[4:6].reshape(2, e), p[6:7, :hd].reshape(1, hd))


def kernel(x, norm_gain, w_in, lb_logits, hgrn_gnorm, w_out, final_gain, loss_target, m_norm_gain, m_w_in, m_lb_logits, m_hgrn_gnorm, m_w_out, m_final_gain, v_norm_gain, v_w_in, v_lb_logits, v_hgrn_gnorm, v_w_out, v_final_gain):
    s, d = x.shape[1], x.shape[2]
    e = w_in.shape[2]
    assert d == 2 * e and lb_logits.shape == (2, e) and w_out.shape[1] * N_DEV == 2 * e
    x2d = x.reshape(s, d)
    tgt = loss_target.reshape(s, d)

    h = _rmsnorm_fwd(x2d, norm_gain)
    z, w_in_full, w_out_full = _inproj_gather(h, _cast_bf16(w_in[0]), _cast_bf16(w_out[0]))
    w_out_full = w_out_full.reshape(2 * e, d)
    y_h, states = _hgrn_fwd(z, lb_logits, hgrn_gnorm)
    o_attn, lse, y_a = _attn_fwd(z)
    dx2, dx2b, dy, loss_vec, dfg = _outproj_loss(x2d, y_h, y_a, w_out_full, final_gain.reshape(1, d), tgt)

    own_o, recv_o = _dwout_scatter(y_h, y_a, dx2b)
    dza = _attn_bwd(z, dy, o_attn, lse)
    dzh, dlb, dgn = _hgrn_bwd(z, dy, states, lb_logits, hgrn_gnorm)
    grad_x, dng = _dh_dx(dzh, dza, w_in_full, x2d, norm_gain, dx2)
    g_wo, d_wo, nm_wo, nv_wo = _sum_chips_adamw(own_o, recv_o, w_out[0], m_w_out[0], v_w_out[0])

    width = d // 2
    zero_row = jnp.zeros((1, width), F32)
    loss_row = loss_vec[:, :width] + loss_vec[:, width:]
    part = _pack_small(dng, dfg, jnp.concatenate([dlb, zero_row], axis=0), dgn, loss_row, width)
    pw = _pack_small(norm_gain, final_gain, lb_logits, hgrn_gnorm, zero_row, width)
    pm = _pack_small(m_norm_gain, m_final_gain, m_lb_logits, m_hgrn_gnorm, zero_row, width)
    pv = _pack_small(v_norm_gain, v_final_gain, v_lb_logits, v_hgrn_gnorm, zero_row, width)
    sg, sd, sm, sv = _small_allreduce_adamw(part, pw, pm, pv, lb_logits)
    own_i, recv_i = _dwin_scatter(h, dzh, dza)
    g_wi, d_wi, nm_wi, nv_wi = _sum_chips_adamw(own_i, recv_i, w_in[0], m_w_in[0], v_w_in[0])
    hd = hgrn_gnorm.shape[1]
    g_ng, g_fg, g_lb, g_gn = _unpack_small(sg, d, e, hd)
    d_ng, d_fg, d_lb, d_gn = _unpack_small(sd, d, e, hd)
    m_ng, m_fg, m_lb, m_gn = _unpack_small(sm, d, e, hd)
    v_ng, v_fg, v_lb, v_gn = _unpack_small(sv, d, e, hd)
    loss = sg[ROW_LOSS, 0]

    one = lambda a: a[None]
    return (loss, grad_x.reshape(1, s, d), g_ng, one(g_wi), g_lb, g_gn, one(g_wo), g_fg,
            d_ng, one(d_wi), d_lb, d_gn, one(d_wo), d_fg,
            m_ng, one(nm_wi), m_lb, m_gn, one(nm_wo), m_fg,
            v_ng, one(nv_wi), v_lb, v_gn, one(nv_wo), v_fg)
```

```python
import functools
import math

import jax
import jax.numpy as jnp
from jax import lax
from jax.experimental import pallas as pl
from jax.experimental.pallas import tpu as pltpu

NORM_EPS = 1e-6
HGRN_HEAD = 128
HGRN_CHUNK = 64
ATTN_HEAD = 64
ATTN_BAND = 128
DILATIONS = (1, 4, 16)
N_SPLITS = 8
N_DEV = 8
ADAM_LR = 0.001
ADAM_B1 = 0.9
ADAM_B2 = 0.999
ADAM_EPS = 1e-08
ADAM_WD = 0.01
ADAM_STEP = 10
LANES = 128
MESH = pl.DeviceIdType.MESH
F32 = jnp.float32
BF16 = jnp.bfloat16
NEG_BIG = -1e30
VMEM_LIMIT = 56 * 1024 * 1024
OUTPROJ_VMEM_LIMIT = 63 * 1024 * 1024
DHDX_VMEM_LIMIT = 60 * 1024 * 1024

ANY = pl.BlockSpec(memory_space=pl.ANY)


def _params(*sem):
    return pltpu.CompilerParams(dimension_semantics=sem, vmem_limit_bytes=VMEM_LIMIT)


def _tile(n, pref):
    t = min(n, pref)
    assert n % t == 0, (n, pref)
    return t


def _dot(a, b, precision=None):
    return jnp.dot(a, b, preferred_element_type=F32, precision=precision)


def _dot_nt(a, b):
    return lax.dot_general(a, b, (((1,), (1,)), ((), ())), preferred_element_type=F32)


def _dot_tn(a, b):
    return lax.dot_general(a, b, (((0,), (0,)), ((), ())), preferred_element_type=F32)


def _sigmoid(x):
    return 0.5 * jnp.tanh(0.5 * x) + 0.5


def _dsilu(x, s):
    return s * (1.0 + x * (1.0 - s))


def _adamw(w, g, m, v):
    m = ADAM_B1 * m + (1.0 - ADAM_B1) * g
    v = ADAM_B2 * v + (1.0 - ADAM_B2) * (g * g)
    m_hat = m / (1.0 - ADAM_B1 ** ADAM_STEP)
    v_hat = v / (1.0 - ADAM_B2 ** ADAM_STEP)
    delta = -ADAM_LR * (m_hat / (jnp.sqrt(v_hat) + ADAM_EPS) + ADAM_WD * w)
    return delta, m, v


def _cast_bf16(a):
    r, c = a.shape
    tr = _tile(r, 256)

    def body(a_ref, o_ref):
        o_ref[...] = a_ref[...].astype(BF16)

    return pl.pallas_call(
        body, name="cast_bf16", grid=(r // tr,), out_shape=jax.ShapeDtypeStruct((r, c), BF16),
        in_specs=[pl.BlockSpec((tr, c), lambda i: (i, 0))], out_specs=pl.BlockSpec((tr, c), lambda i: (i, 0)),
        compiler_params=_params("parallel"))(a)


def _rmsnorm_fwd(x, gain):
    s, d = x.shape
    tm = _tile(s, 512)

    def body(x_ref, g_ref, h_ref):
        xv = x_ref[...]
        r = lax.rsqrt(jnp.mean(xv * xv, axis=-1, keepdims=True) + NORM_EPS)
        h_ref[...] = (xv * r * g_ref[...]).astype(BF16)

    return pl.pallas_call(
        body, name="rmsnorm_fwd", grid=(s // tm,), out_shape=jax.ShapeDtypeStruct((s, d), BF16),
        in_specs=[pl.BlockSpec((tm, d), lambda i: (i, 0)), pl.BlockSpec((1, d), lambda i: (0, 0))],
        out_specs=pl.BlockSpec((tm, d), lambda i: (i, 0)), compiler_params=_params("parallel"))(x, gain)


HGRN_BLOCK = 2048
TRI_ROWS = 64


def _chunk_masks():
    tb = TRI_ROWS
    row = lax.broadcasted_iota(jnp.int32, (tb, tb), 0)
    col = lax.broadcasted_iota(jnp.int32, (tb, tb), 1)
    same = (row // HGRN_CHUNK) == (col // HGRN_CHUNK)
    lower = jnp.where(same & (col <= row), 1.0, 0.0).astype(BF16)
    upper = jnp.where(same & (col >= row), 1.0, 0.0).astype(BF16)
    return lower, upper


def _split3(a):
    hi = a.astype(BF16).astype(F32)
    mid = (a - hi).astype(BF16).astype(F32)
    lo = (a - hi - mid).astype(BF16).astype(F32)
    return hi, mid, lo


def _tri_dot(tri, x):
    hi, mid, lo = (p.astype(BF16) for p in _split3(x))
    outs = []
    for r in range(0, x.shape[0], TRI_ROWS):
        sl = slice(r, r + TRI_ROWS)
        outs.append(_dot(tri, hi[sl]) + _dot(tri, mid[sl]) + _dot(tri, lo[sl]))
    return outs[0] if len(outs) == 1 else jnp.concatenate(outs, axis=0)


def _hgrn_gates(qp, fp, lbv):
    lb = _sigmoid(lbv[0:1] - lbv[1:2])
    sq = _sigmoid(qp)
    q = qp * sq
    sg = _sigmoid(fp)
    f = lb + (1.0 - lb) * sg
    k = 1.0 - f
    return lb, sq, q, sg, f, k


def _hgrn_fwd(z, lb_logits, gnorm):
    s = z.shape[0]
    e = z.shape[1] // N_SPLITS
    nh = e // HGRN_HEAD
    tb = _tile(s, HGRN_BLOCK)
    nc = tb // HGRN_CHUNK
    nb = s // tb
    C = HGRN_CHUNK

    def body(q_ref, f_ref, i_ref, g_ref, lb_ref, gn_ref, y_ref, st_ref, state, o_scr):
        @pl.when(pl.program_id(1) == 0)
        def _():
            state[...] = jnp.zeros_like(state)

        lb, sq, q, sg, f, k = _hgrn_gates(q_ref[...], f_ref[...], lb_ref[...])
        lower, _ = _chunk_masks()
        b = _tri_dot(lower, jnp.log(f))
        b3 = b.reshape(nc, C, HGRN_HEAD)
        bc = b3[:, C - 1:C, :]
        qt = (q * jnp.exp(b)).astype(BF16)
        kt = (k * jnp.exp(-b)).astype(BF16)
        ke = (k.reshape(nc, C, HGRN_HEAD) * jnp.exp(bc - b3)).reshape(tb, HGRN_HEAD).astype(BF16)
        v = i_ref[...].astype(BF16)
        tri = lax.broadcasted_iota(jnp.int32, (C, C), 1) <= lax.broadcasted_iota(jnp.int32, (C, C), 0)
        sls = [slice(c * C, (c + 1) * C) for c in range(nc)]
        kv = [_dot_tn(v[sl], ke[sl]) for sl in sls]
        a = [jnp.where(tri, _dot_nt(qt[sl], kt[sl]), 0.0).astype(BF16) for sl in sls]
        st = state[...]
        sts = []
        for c in range(nc):
            sts.append(st)
            st_ref[c] = st
            st = st * jnp.exp(bc[c]) + kv[c]
        state[...] = st
        for c, sl in enumerate(sls):
            o_scr[sl, :] = _dot(a[c], v[sl]) + _dot_nt(qt[sl], sts[c].astype(BF16))
        o = o_scr[...]
        rms = lax.rsqrt(jnp.mean(o * o, axis=-1, keepdims=True) + NORM_EPS)
        gp = g_ref[...]
        y_ref[...] = (o * rms * gn_ref[...] * (gp * _sigmoid(gp))).astype(BF16)

    col = lambda kk: (lambda h, n: (n, kk * nh + h))
    return pl.pallas_call(
        body, name="hgrn_fwd", grid=(nh, nb),
        out_shape=(jax.ShapeDtypeStruct((s, e), BF16),
                   jax.ShapeDtypeStruct((nh, s // C, HGRN_HEAD, HGRN_HEAD), F32)),
        in_specs=[pl.BlockSpec((tb, HGRN_HEAD), col(0)), pl.BlockSpec((tb, HGRN_HEAD), col(1)),
                  pl.BlockSpec((tb, HGRN_HEAD), col(2)), pl.BlockSpec((tb, HGRN_HEAD), col(3)),
                  pl.BlockSpec((2, HGRN_HEAD), lambda h, n: (0, h)), pl.BlockSpec((1, HGRN_HEAD), lambda h, n: (0, 0))],
        out_specs=(pl.BlockSpec((tb, HGRN_HEAD), lambda h, n: (n, h)),
                   pl.BlockSpec((None, nc, HGRN_HEAD, HGRN_HEAD), lambda h, n: (h, n, 0, 0))),
        scratch_shapes=[pltpu.VMEM((HGRN_HEAD, HGRN_HEAD), F32), pltpu.VMEM((tb, HGRN_HEAD), F32)],
        compiler_params=_params("parallel", "arbitrary"))(z, z, z, z, lb_logits, gnorm)


def _hgrn_bwd(z, dy, states, lb_logits, gnorm):
    s = z.shape[0]
    e = z.shape[1] // N_SPLITS
    nh = e // HGRN_HEAD
    tb = _tile(s, HGRN_BLOCK)
    nc = tb // HGRN_CHUNK
    nb = s // tb
    C = HGRN_CHUNK
    H = HGRN_HEAD

    def body(q_ref, f_ref, i_ref, g_ref, dy_ref, st_ref, lb_ref, gn_ref, dz_ref, dlb_ref, dgn_ref,
             gstate, o_scr, dq_scr, dk_scr, dv_scr, e_scr):
        first = (pl.program_id(0) == 0) & (pl.program_id(1) == 0)

        @pl.when(first)
        def _():
            dgn_ref[...] = jnp.zeros_like(dgn_ref)

        @pl.when(pl.program_id(1) == 0)
        def _():
            gstate[...] = jnp.zeros_like(gstate)
            dlb_ref[...] = jnp.zeros_like(dlb_ref)

        qp = q_ref[...]
        lb, sq, q, sg, f, k = _hgrn_gates(qp, f_ref[...], lb_ref[...])
        lower, upper = _chunk_masks()
        b = _tri_dot(lower, jnp.log(f))
        b3 = b.reshape(nc, C, H)
        bc = b3[:, C - 1:C, :]
        eb = jnp.exp(b)
        enb = jnp.exp(-b)
        eend = jnp.exp(bc - b3).reshape(tb, H)
        qt = (q * eb).astype(BF16)
        kt = (k * enb).astype(BF16)
        ke = (k * eend).astype(BF16)
        v = i_ref[...].astype(BF16)
        tri = lax.broadcasted_iota(jnp.int32, (C, C), 1) <= lax.broadcasted_iota(jnp.int32, (C, C), 0)
        sls = [slice(c * C, (c + 1) * C) for c in range(nc)]
        a = [jnp.where(tri, _dot_nt(qt[sl], kt[sl]), 0.0).astype(BF16) for sl in sls]
        for c, sl in enumerate(sls):
            o_scr[sl, :] = _dot(a[c], v[sl]) + _dot_nt(qt[sl], st_ref[c].astype(BF16))
        o = o_scr[...]
        rms = lax.rsqrt(jnp.mean(o * o, axis=-1, keepdims=True) + NORM_EPS)
        on = o * rms
        gn = gn_ref[...]
        gp = g_ref[...]
        sgg = _sigmoid(gp)
        dyv = dy_ref[...]
        d_on = dyv * (gp * sgg)
        dz_ref[3] = (dyv * on * gn * _dsilu(gp, sgg)).astype(BF16)
        dgn_ref[...] += jnp.sum(d_on * on, axis=0, keepdims=True)
        u = d_on * gn
        do = (rms * (u - on * jnp.mean(u * on, axis=-1, keepdims=True))).astype(BF16)
        gup = [_dot_tn(do[sl], qt[sl]) for sl in sls]
        da = [jnp.where(tri, _dot_nt(do[sl], v[sl]), 0.0).astype(BF16) for sl in sls]
        gt = gstate[...]
        gts = [None] * nc
        for c in reversed(range(nc)):
            gts[c] = gt
            gt = gt * jnp.exp(bc[c]) + gup[c]
        gstate[...] = gt
        for c, sl in enumerate(sls):
            stp = st_ref[c]
            gtb = gts[c].astype(BF16)
            dqt = _dot(da[c], kt[sl]) + _dot(do[sl], stp.astype(BF16))
            dkt = _dot_tn(da[c], qt[sl])
            dks = _dot(v[sl], gtb) * eend[sl]
            dv_scr[sl, :] = _dot_tn(a[c], do[sl]) + _dot_nt(ke[sl], gtb)
            dq_scr[sl, :] = dqt * eb[sl]
            dk_scr[sl, :] = dkt * enb[sl] + dks
            ech = (jnp.sum(k[sl] * dks, axis=0, keepdims=True)
                   + jnp.sum(gts[c] * jnp.exp(bc[c]) * stp, axis=0, keepdims=True))
            e_scr[sl, :] = jnp.broadcast_to(ech, (C, H))
        dq = dq_scr[...]
        dk = dk_scr[...]
        dlf = _tri_dot(upper, q * dq - k * dk) + e_scr[...]
        dft = dlf / f - dk
        dz_ref[0] = (dq * _dsilu(qp, sq)).astype(BF16)
        dz_ref[1] = (dft * (1.0 - lb) * sg * (1.0 - sg)).astype(BF16)
        dz_ref[2] = dv_scr[...].astype(BF16)
        dlb_ref[...] += jnp.sum(dft * (1.0 - sg), axis=0, keepdims=True)

    col = lambda kk: (lambda h, n: (nb - 1 - n, kk * nh + h))
    return pl.pallas_call(
        body, name="hgrn_bwd", grid=(nh, nb),
        out_shape=(jax.ShapeDtypeStruct((4, s, e), BF16), jax.ShapeDtypeStruct((1, e), F32),
                   jax.ShapeDtypeStruct((1, H), F32)),
        in_specs=[pl.BlockSpec((tb, H), col(0)), pl.BlockSpec((tb, H), col(1)),
                  pl.BlockSpec((tb, H), col(2)), pl.BlockSpec((tb, H), col(3)),
                  pl.BlockSpec((tb, H), lambda h, n: (nb - 1 - n, h)),
                  pl.BlockSpec((None, nc, H, H), lambda h, n: (h, nb - 1 - n, 0, 0)),
                  pl.BlockSpec((2, H), lambda h, n: (0, h)), pl.BlockSpec((1, H), lambda h, n: (0, 0))],
        out_specs=(pl.BlockSpec((4, tb, H), lambda h, n: (0, nb - 1 - n, h)),
                   pl.BlockSpec((1, H), lambda h, n: (0, h)), pl.BlockSpec((1, H), lambda h, n: (0, 0))),
        scratch_shapes=[pltpu.VMEM((H, H), F32)] + [pltpu.VMEM((tb, H), F32)] * 5,
        compiler_params=_params("arbitrary", "arbitrary"))(z, z, z, z, dy, states, lb_logits, gnorm)


ATTN_T = 16 * ATTN_BAND
SCALE = ATTN_HEAD ** -0.5


def _slope(pair, hh, nheads):
    head = (2 * pair + hh + 1).astype(F32)
    return jnp.exp(jnp.full((1, 1), -8.0 / nheads * math.log(2.0), F32) * head)


def _fill_bias(bias, pair, nheads, delta, edge_ok):
    band = (delta >= 0) & (delta <= ATTN_BAND)
    dist = delta.astype(F32)
    for pi, dil in enumerate(DILATIONS):
        for hh in range(2):
            full = jnp.where(band, -(_slope(pair, hh, nheads) * float(dil)) * dist, NEG_BIG)
            bias[(pi * 2 + hh) * 2] = full
            bias[(pi * 2 + hh) * 2 + 1] = jnp.where(edge_ok, full, NEG_BIG)


def _rows(start, size, stride):
    if stride == 1:
        return pl.ds(pl.multiple_of(start, ATTN_BAND), size)
    return pl.ds(start, size, stride=stride)


def _head_lanes(rows, hh):
    return (lax.broadcasted_iota(jnp.int32, (rows, LANES), 1) // ATTN_HEAD) == hh


def _attn_fwd(z):
    s = z.shape[0]
    e = z.shape[1] // N_SPLITS
    npair = e // LANES
    T = ATTN_T
    assert s % T == 0
    nsb = s // T
    W = ATTN_BAND
    nt = T // W
    HD = ATTN_HEAD
    chunk = 256

    nsteps = npair * nsb
    assert nt % 2 == 0 and T // chunk == nt // 2

    def body(q_ref, kp_ref, kc_ref, vp_ref, vc_ref, g_ref, o_ref, l_ref, y_ref, qa, kbuf, va, bias, *sets):
        t = pl.program_id(0)
        sb = jnp.minimum(t, nsteps - 1) % nsb
        pair = jnp.minimum(t, nsteps - 1) // nsb
        sets = (sets[0:3], sets[3:6])

        def merge_chunk(i, done):
            accs, ms, lsw = done
            rows = pl.ds(pl.multiple_of(i * chunk, chunk), chunk)
            m1, m2, m3 = ms[0, rows, :], ms[1, rows, :], ms[2, rows, :]
            mx = jnp.maximum(jnp.maximum(m1, m2), m3)
            w1, w2, w3 = jnp.exp(m1 - mx), jnp.exp(m2 - mx), jnp.exp(m3 - mx)
            unswap = lambda a: pltpu.roll(a, ATTN_HEAD, 1)
            den = w1 * unswap(lsw[0, rows, :]) + w2 * unswap(lsw[1, rows, :]) + w3 * unswap(lsw[2, rows, :])
            o = (w1 * accs[0, rows, :] + w2 * accs[1, rows, :] + w3 * accs[2, rows, :]) / den
            o_ref[rows, :] = o
            l_ref[rows, :] = mx + jnp.log(den)
            gp = g_ref[rows, :]
            y_ref[rows, :] = (o * (gp * _sigmoid(gp))).astype(BF16)

        @pl.when(t == 0)
        def _():
            accs, ms, lsw = sets[1]
            accs[...] = jnp.zeros_like(accs)
            ms[...] = jnp.zeros_like(ms)
            lsw[...] = jnp.ones_like(lsw)

        @pl.when(t == nsteps)
        def _():
            def drain(i, carry):
                merge_chunk(i, sets[(nsteps - 1) % 2])
                return carry

            lax.fori_loop(0, T // chunk, drain, 0)

        def compute(cur, done):
            accs, ms, lsw = cur
            def stage(i, carry):
                rows = pl.ds(pl.multiple_of(i * chunk, chunk), chunk)
                upper = pl.ds(pl.multiple_of(T + i * chunk, chunk), chunk)
                kbuf[upper, :] = kc_ref[rows, :]
                for hh in range(2):
                    mine = _head_lanes(chunk, hh)
                    qa[hh, rows, :] = jnp.where(mine, q_ref[rows, :] * SCALE, 0.0)
                    va[hh, upper, :] = jnp.where(mine, vc_ref[rows, :], 1.0)
                return carry

            lax.fori_loop(0, T // chunk, stage, 0)

            @pl.when(sb == 0)
            def _():
                def stage_prev(i, carry):
                    rows = pl.ds(pl.multiple_of(i * chunk, chunk), chunk)
                    kbuf[rows, :] = kp_ref[rows, :]
                    for hh in range(2):
                        va[hh, rows, :] = jnp.where(_head_lanes(chunk, hh), vp_ref[rows, :], 1.0)
                    return carry

                lax.fori_loop(0, T // chunk, stage_prev, 0)

            @pl.when(sb == 0)
            def _():
                qi = lax.broadcasted_iota(jnp.int32, (W, 2 * W), 0)
                kj = lax.broadcasted_iota(jnp.int32, (W, 2 * W), 1)
                _fill_bias(bias, pair, 2 * npair, W + qi - kj, kj >= W)

            def tile(tau):
                first = _head_lanes(W, 0)
                rows, scores = [], []
                for pi, dil in enumerate(DILATIONS):
                    r = tau % dil
                    ub = tau // dil
                    qrows = _rows(r + dil * W * ub, W, dil)
                    krows = _rows(T + dil * W * (ub - 1) + r, 2 * W, dil)
                    var = jnp.where((sb == 0) & (ub == 0), 1, 0)
                    kt = kbuf[krows, :].astype(BF16)
                    rows.append((qrows, krows))
                    scores.append([_dot_nt(qa[hh, qrows, :].astype(BF16), kt) + bias[(pi * 2 + hh) * 2 + var]
                                   for hh in range(2)])
                maxes = [[jnp.max(sc, axis=-1, keepdims=True) for sc in pair_] for pair_ in scores]
                probs = [[jnp.exp(sc - m).astype(BF16) for sc, m in zip(ps, pm)] for ps, pm in zip(scores, maxes)]
                for pi, (qrows, krows) in enumerate(rows):
                    outs = [_dot(probs[pi][hh], va[hh, krows, :].astype(BF16)) for hh in range(2)]
                    accs[pi, qrows, :] = jnp.where(first, outs[0], outs[1])
                    lsw[pi, qrows, :] = jnp.where(first, outs[1], outs[0])
                    ms[pi, qrows, :] = jnp.where(first, maxes[pi][0], maxes[pi][1])

            def two_tiles(i, carry):
                tile(2 * i)
                tile(2 * i + 1)
                merge_chunk(i, done)
                return carry

            lax.fori_loop(0, nt // 2, two_tiles, 0)

            def move_down(i, carry):
                rows = pl.ds(pl.multiple_of(i * chunk, chunk), chunk)
                upper = pl.ds(pl.multiple_of(T + i * chunk, chunk), chunk)
                kbuf[rows, :] = kbuf[upper, :]
                for hh in range(2):
                    va[hh, rows, :] = va[hh, upper, :]
                return carry

            lax.fori_loop(0, T // chunk, move_down, 0)

        for parity in range(2):
            @pl.when((t < nsteps) & (t % 2 == parity))
            def _():
                compute(sets[parity], sets[1 - parity])

    step_of = lambda t: jnp.minimum(t, nsteps - 1)
    lag_of = lambda t: jnp.maximum(t - 1, 0)
    cur = lambda split: (lambda t: (step_of(t) % nsb, split * npair + step_of(t) // nsb))
    prev = lambda split: (lambda t: (0, split * npair + step_of(t) // nsb))
    blk = lambda index: pl.BlockSpec((T, LANES), index)
    out = blk(lambda t: (lag_of(t) % nsb, lag_of(t) // nsb))
    gate = blk(lambda t: (lag_of(t) % nsb, 7 * npair + lag_of(t) // nsb))
    buf = lambda rows: pltpu.VMEM((rows, LANES), F32)
    return pl.pallas_call(
        body, name="attn_fwd", grid=(nsteps + 1,),
        out_shape=(jax.ShapeDtypeStruct((s, e), F32), jax.ShapeDtypeStruct((s, e), F32), jax.ShapeDtypeStruct((s, e), BF16)),
        in_specs=[blk(cur(4)), blk(prev(5)), blk(cur(5)), blk(prev(6)), blk(cur(6)), gate],
        out_specs=(out, out, out),
        scratch_shapes=[pltpu.VMEM((2, T, LANES), F32), buf(2 * T), pltpu.VMEM((2, 2 * T, LANES), F32),
                        pltpu.VMEM((12, W, 2 * W), F32)] + [pltpu.VMEM((3, T, LANES), F32)] * 6,
        compiler_params=_params("arbitrary"))(z, z, z, z, z, z)


def _outproj_loss(x, y_h, y_a, w_out_full, final_gain, target):
    s, d = x.shape
    e = y_h.shape[1]
    tm = _tile(s, 512)
    sub = _tile(tm, 256)

    def body(x_ref, yh_ref, ya_ref, w_ref, g_ref, t_ref, dx_ref, dxb_ref, dy_ref, loss_ref, dg_ref):
        @pl.when(pl.program_id(0) == 0)
        def _():
            loss_ref[...] = jnp.zeros_like(loss_ref)
            dg_ref[...] = jnp.zeros_like(dg_ref)

        w = w_ref[...]
        g = g_ref[...]
        parts = [slice(r0, r0 + sub) for r0 in range(0, tm, sub)]
        x2s = [x_ref[rows, :] + _dot(yh_ref[rows, :], w[0:e]) + _dot(ya_ref[rows, :], w[e:2 * e]) for rows in parts]
        for rows, x2 in zip(parts, x2s):
            r = lax.rsqrt(jnp.mean(x2 * x2, axis=-1, keepdims=True) + NORM_EPS)
            xn = x2 * r
            err = xn * g - t_ref[rows, :]
            loss_ref[...] += jnp.sum(err * err, axis=0, keepdims=True) * (0.5 / d)
            dyo = err * (1.0 / d)
            dg_ref[...] += jnp.sum(dyo * xn, axis=0, keepdims=True)
            u = dyo * g
            dx2 = r * (u - xn * jnp.mean(u * xn, axis=-1, keepdims=True))
            dx_ref[rows, :] = dx2
            dxb = dx2.astype(BF16)
            dxb_ref[rows, :] = dxb
            dy_ref[rows, :] = _dot_nt(dxb, w)

    row = pl.BlockSpec((tm, d), lambda i: (i, 0))
    half = pl.BlockSpec((tm, e), lambda i: (i, 0))
    vec = pl.BlockSpec((1, d), lambda i: (0, 0))
    whole = pl.BlockSpec((2 * e, d), lambda i: (0, 0), pipeline_mode=pl.Buffered(1))
    return pl.pallas_call(
        body, name="outproj_loss", grid=(s // tm,),
        out_shape=(jax.ShapeDtypeStruct((s, d), F32), jax.ShapeDtypeStruct((s, d), BF16),
                   jax.ShapeDtypeStruct((s, 2 * e), F32), jax.ShapeDtypeStruct((1, d), F32),
                   jax.ShapeDtypeStruct((1, d), F32)),
        in_specs=[row, half, half, whole, vec, row],
        out_specs=(row, row, pl.BlockSpec((tm, 2 * e), lambda i: (i, 0)), vec, vec),
        compiler_params=pltpu.CompilerParams(dimension_semantics=("arbitrary",), vmem_limit_bytes=OUTPROJ_VMEM_LIMIT),
    )(x, y_h, y_a, w_out_full, final_gain, target)


def _attn_bwd(z, dy, o, lse):
    s, e = o.shape
    npair = e // LANES
    T = ATTN_T
    assert s % T == 0
    nsb = s // T
    W = ATTN_BAND
    nt = T // W
    HD = ATTN_HEAD
    chunk = 256

    def body(k_ref, v_ref, qc_ref, qn_ref, dyc_ref, dyn_ref, gc_ref, gn_ref, oc_ref, on_ref, lc_ref, ln_ref,
             dz_ref, qa, doa, ka, va, dqacc, dkacc, dvacc, bias):
        sb = pl.program_id(1)
        def stage_queries(half, q_r, dy_r, g_r, o_r, l_r):
            def stage(i, carry):
                rows = pl.ds(pl.multiple_of(i * chunk, chunk), chunk)
                dst = pl.ds(pl.multiple_of(half * T + i * chunk, chunk), chunk)
                lane = lax.broadcasted_iota(jnp.int32, (chunk, LANES), 1)
                gp = g_r[rows, :]
                dov = dy_r[rows, :] * (gp * _sigmoid(gp))
                qv = q_r[rows, :] * SCALE
                same_head = (lax.broadcasted_iota(jnp.int32, (LANES, LANES), 0) // HD
                             == lax.broadcasted_iota(jnp.int32, (LANES, LANES), 1) // HD)
                ones = jnp.where(same_head, 1.0, 0.0).astype(BF16)
                hi, mid, lo = (p.astype(BF16) for p in _split3(dov * o_r[rows, :]))
                delta = _dot(hi, ones) + _dot(mid, ones) + _dot(lo, ones)
                swap = lambda a: pltpu.roll(a, HD, 1)
                lse_parts = [swap(p) for p in _split3(l_r[rows, :])]
                dl_parts = [swap(p) for p in _split3(delta)]
                for hh in range(2):
                    mine = _head_lanes(chunk, hh)
                    spare = (1 - hh) * HD
                    qh = jnp.where(mine, qv, 0.0)
                    dh = jnp.where(mine, dov, 0.0)
                    for j in range(3):
                        qh = jnp.where(lane == spare + j, lse_parts[j], qh)
                        dh = jnp.where(lane == spare + j, dl_parts[j], dh)
                    qa[hh, dst, :] = qh
                    doa[hh, dst, :] = dh
                return carry

            lax.fori_loop(0, T // chunk, stage, 0)

        @pl.when(sb == 0)
        def _():
            stage_queries(0, qc_ref, dyc_ref, gc_ref, oc_ref, lc_ref)

        stage_queries(1, qn_ref, dyn_ref, gn_ref, on_ref, ln_ref)

        def stage_keys(i, carry):
            rows = pl.ds(pl.multiple_of(i * chunk, chunk), chunk)
            lane = lax.broadcasted_iota(jnp.int32, (chunk, LANES), 1)
            for hh in range(2):
                spare = (1 - hh) * HD
                minus = (lane >= spare) & (lane < spare + 3)
                ka[hh, rows, :] = jnp.where(minus, -1.0, k_ref[rows, :])
                va[hh, rows, :] = jnp.where(minus, -1.0, v_ref[rows, :])
            gp = gc_ref[rows, :]
            dz_ref[3, rows, :] = (dyc_ref[rows, :] * oc_ref[rows, :] * _dsilu(gp, _sigmoid(gp))).astype(BF16)
            return carry

        lax.fori_loop(0, T // chunk, stage_keys, 0)

        @pl.when(sb == 0)
        def _():
            dqacc[0:T, :] = jnp.zeros((T, LANES), F32)

        dqacc[T:, :] = jnp.zeros((T, LANES), F32)
        dkacc[...] = jnp.zeros_like(dkacc)
        dvacc[...] = jnp.zeros_like(dvacc)
        @pl.when(sb == 0)
        def _():
            qi = lax.broadcasted_iota(jnp.int32, (2 * W, W), 0)
            kj = lax.broadcasted_iota(jnp.int32, (2 * W, W), 1)
            _fill_bias(bias, pl.program_id(0), 2 * npair, qi - kj, qi < W)

        def tile(tau, carry):
            def scores(step, pi):
                dil = DILATIONS[pi]
                r = step % dil
                ub = step // dil
                start = r + dil * W * ub
                krows = _rows(start, W, dil)
                qrows = _rows(start, 2 * W, dil)
                var = jnp.where((sb == nsb - 1) & (ub == nt // dil - 1), 1, 0)
                unit = dict(krows=krows, qrows=qrows, ops=[], sc=[], dpd=[])
                for hh in range(2):
                    kt = ka[hh, krows, :].astype(BF16)
                    vt = va[hh, krows, :].astype(BF16)
                    qt = qa[hh, qrows, :].astype(BF16)
                    dt = doa[hh, qrows, :].astype(BF16)
                    unit["ops"].append((kt, qt, dt))
                    unit["sc"].append(_dot_nt(qt, kt) + bias[(pi * 2 + hh) * 2 + var])
                    unit["dpd"].append(_dot_nt(dt, vt))
                return unit

            def elementwise(unit):
                ps = [jnp.exp(s_) for s_ in unit["sc"]]
                unit["ds"] = [(p * d).astype(BF16) for p, d in zip(ps, unit["dpd"])]
                unit["pb"] = [p.astype(BF16) for p in ps]

            def products(unit):
                dvs = [_dot_tn(pb, dt) for pb, (kt, qt, dt) in zip(unit["pb"], unit["ops"])]
                dks = [_dot_tn(ds, qt) for ds, (kt, qt, dt) in zip(unit["ds"], unit["ops"])]
                dqs = [_dot(ds, kt) for ds, (kt, qt, dt) in zip(unit["ds"], unit["ops"])]
                dkacc[unit["krows"], :] += jnp.where(_head_lanes(W, 0), dks[0], dks[1])
                dvacc[unit["krows"], :] += jnp.where(_head_lanes(W, 0), dvs[0], dvs[1])
                dqacc[unit["qrows"], :] += jnp.where(_head_lanes(2 * W, 0), dqs[0], dqs[1]) * SCALE

            order = [(2 * tau + half, pi) for half in range(2) for pi in range(len(DILATIONS))]
            units = [None] * len(order)
            for n in range(len(order) + 2):
                if n < len(order):
                    units[n] = scores(*order[n])
                if 1 <= n <= len(order):
                    elementwise(units[n - 1])
                if n >= 2:
                    products(units[n - 2])
            return carry

        lax.fori_loop(0, nt // 2, tile, 0)

        def flush(i, carry):
            rows = pl.ds(pl.multiple_of(i * chunk, chunk), chunk)
            nxt = pl.ds(pl.multiple_of(T + i * chunk, chunk), chunk)
            dz_ref[0, rows, :] = dqacc[rows, :].astype(BF16)
            dz_ref[1, rows, :] = dkacc[rows, :].astype(BF16)
            dz_ref[2, rows, :] = dvacc[rows, :].astype(BF16)
            dqacc[rows, :] = dqacc[nxt, :]
            for hh in range(2):
                qa[hh, rows, :] = qa[hh, nxt, :]
                doa[hh, rows, :] = doa[hh, nxt, :]
            return carry

        lax.fori_loop(0, T // chunk, flush, 0)

    zc = lambda split: (lambda hp, sb: (sb, split * npair + hp))
    zn = lambda split: (lambda hp, sb: (jnp.minimum(sb + 1, nsb - 1), split * npair + hp))
    ec = lambda off: (lambda hp, sb: (sb, off + hp))
    en = lambda off: (lambda hp, sb: (jnp.minimum(sb + 1, nsb - 1), off + hp))
    z0 = lambda split: (lambda hp, sb: (0, split * npair + hp))
    e0 = lambda off: (lambda hp, sb: (0, off + hp))
    blk = lambda index: pl.BlockSpec((T, LANES), index)
    buf = lambda rows: pltpu.VMEM((rows, LANES), F32)
    return pl.pallas_call(
        body, name="attn_bwd", grid=(npair, nsb), out_shape=jax.ShapeDtypeStruct((4, s, e), BF16),
        in_specs=[blk(zc(5)), blk(zc(6)), blk(z0(4)), blk(zn(4)), blk(ec(npair)), blk(en(npair)),
                  blk(zc(7)), blk(zn(7)), blk(ec(0)), blk(en(0)), blk(e0(0)), blk(en(0))],
        out_specs=pl.BlockSpec((4, T, LANES), lambda hp, sb: (0, sb, hp)),
        scratch_shapes=[pltpu.VMEM((2, 2 * T, LANES), F32), pltpu.VMEM((2, 2 * T, LANES), F32),
                        pltpu.VMEM((2, T, LANES), F32), pltpu.VMEM((2, T, LANES), F32),
                        buf(2 * T), buf(T), buf(T), pltpu.VMEM((12, 2 * W, W), F32)],
        compiler_params=_params("parallel", "arbitrary"))(z, z, z, z, dy, dy, z, z, o, o, lse, lse)


def _dz_specs(tm, e):
    def mk(lo, hi):
        return pl.BlockSpec((None, tm, e), lambda i, k: (jnp.clip(k - lo, 0, hi - lo - 1), i, 0))
    return [mk(0, 4), mk(4, 8)]


def _dz_pick(grp, dzh_ref, dza_ref, fn):
    @pl.when(grp < 4)
    def _():
        fn(dzh_ref[...])

    @pl.when(grp >= 4)
    def _():
        fn(dza_ref[...])


def _dh_dx(dzh, dza, w_full, x, gain, dx2):
    s, d = x.shape
    e = dzh.shape[2]
    tm = _tile(s, 1024)
    ni = s // tm
    chunk = _tile(tm, 128)
    fetch_at = 2

    def body(dzh_ref, dza_ref, w_ref, x_hbm, g_ref, dx2_hbm, gx_hbm, dg_ref, acc, xbuf, dbuf, sems):
        i, k = pl.program_id(0), pl.program_id(1)
        rows_of = lambda tile: pl.ds(pl.multiple_of(tile * tm, tm), tm)
        fetch_x = pltpu.make_async_copy(x_hbm.at[rows_of(i), :], xbuf, sems.at[0])
        fetch_d = pltpu.make_async_copy(dx2_hbm.at[rows_of(i), :], dbuf, sems.at[1])

        def store(tile):
            return pltpu.make_async_copy(xbuf, gx_hbm.at[rows_of(tile), :], sems.at[2])

        @pl.when((i == 0) & (k == 0))
        def _():
            dg_ref[...] = jnp.zeros_like(dg_ref)

        @pl.when((k == fetch_at) & (i > 0))
        def _():
            store(i).wait()

        @pl.when(k == fetch_at)
        def _():
            fetch_x.start()
            fetch_d.start()

        def finish(tile):
            fetch_x.wait()
            fetch_d.wait()
            gain_row = g_ref[...]
            for r0 in range(0, tm, chunk):
                rows = slice(r0, r0 + chunk)
                dh = acc[rows, :]
                xv = xbuf[rows, :]
                r = lax.rsqrt(jnp.mean(xv * xv, axis=-1, keepdims=True) + NORM_EPS)
                xn = xv * r
                u = dh * gain_row
                xbuf[rows, :] = dbuf[rows, :] + r * (u - xn * jnp.mean(u * xn, axis=-1, keepdims=True))
                dg_ref[...] += jnp.sum(dh * xn, axis=0, keepdims=True)
            store(tile).start()

        @pl.when((k == 0) & (i == 0))
        def _():
            acc[...] = _dot_nt(dzh_ref[...], w_ref[...])

        @pl.when((k == 0) & (i > 0))
        def _():
            finish(i - 1)
            acc[...] = _dot_nt(dzh_ref[...], w_ref[...])

        @pl.when(k > 0)
        def _():
            def add(dz):
                acc[...] += _dot_nt(dz, w_ref[...])

            _dz_pick(k, dzh_ref, dza_ref, add)

        @pl.when((k == N_SPLITS - 1) & (i == ni - 1))
        def _():
            finish(i)
            store(i).wait()

    vec = pl.BlockSpec((1, d), lambda i, k: (0, 0))
    return pl.pallas_call(
        body, name="dh_dx", grid=(ni, N_SPLITS),
        out_shape=(jax.ShapeDtypeStruct((s, d), F32), jax.ShapeDtypeStruct((1, d), F32)),
        in_specs=_dz_specs(tm, e) + [pl.BlockSpec((None, d, e), lambda i, k: (k, 0, 0)), ANY, vec, ANY],
        out_specs=(ANY, vec),
        scratch_shapes=[pltpu.VMEM((tm, d), F32)] * 3 + [pltpu.SemaphoreType.DMA((3,))],
        compiler_params=pltpu.CompilerParams(dimension_semantics=("arbitrary", "arbitrary"),
                                             vmem_limit_bytes=DHDX_VMEM_LIMIT))(dzh, dza, w_full, x, gain, dx2)


def _position():
    x, y, c = lax.axis_index("x"), lax.axis_index("y"), lax.axis_index("c")
    return x, y, c


def _xor_peer(x, y, c, mask):
    return (x ^ ((mask >> 2) & 1), y ^ ((mask >> 1) & 1), c ^ (mask & 1))


def _block_order(masks):
    me = 4 * lax.axis_index("x") + 2 * lax.axis_index("y") + lax.axis_index("c")
    return jnp.stack([me ^ m for m in masks]).astype(jnp.int32)


GATHER_MASKS = (0, 1, 4, 5, 2, 3, 6, 7)


def _inproj_gather(h, w_loc, wo_loc):
    s, d = h.shape
    e = w_loc.shape[1]
    tm = _tile(s, 1024)
    ni = s // tm
    pre = max(ni - 2, 0)

    def body(order_ref, h_ref, w_ref, wo_ref, z_ref, wf_ref, wof_ref, wbuf, send_sems, recv_sems, osend, orecv,
             local_sems, wsems):
        j, i = pl.program_id(0), pl.program_id(1)
        x, y, c = _position()
        me, sibling = (x, y, c), (x, y, 1 - c)
        chips = [(1 - x, y), (x, 1 - y), (1 - x, 1 - y)]
        blk = lambda p: 4 * p[0] + 2 * p[1] + p[2]

        def copy(k, block, to, src=None):
            dst = wf_ref.at[blk(block)]
            return pltpu.make_async_remote_copy(
                src_ref=dst if src is None else src, dst_ref=dst, send_sem=send_sems.at[k], recv_sem=recv_sems.at[k],
                device_id=to, device_id_type=MESH)

        first = [copy(0, me, sibling, src=w_ref)] + [copy(1 + q, me, (*chip, c), src=w_ref) for q, chip in enumerate(chips)]
        passed = [copy(4 + q, (*chip, c), sibling) for q, chip in enumerate(chips)]
        mine = pltpu.make_async_copy(w_ref, wf_ref.at[blk(me)], local_sems.at[0])
        ocopies = [pltpu.make_async_remote_copy(
            src_ref=wo_ref, dst_ref=wof_ref.at[blk(me)], send_sem=osend.at[k], recv_sem=orecv.at[k],
            device_id=_xor_peer(x, y, c, k + 1), device_id_type=MESH) for k in range(N_DEV - 1)]
        omine = pltpu.make_async_copy(wo_ref, wof_ref.at[blk(me)], local_sems.at[1])
        blocks = [me, sibling] + [(*chip, c) for chip in chips] + [(*chip, 1 - c) for chip in chips]
        arrive = [None, copy(0, sibling, me)] + [copy(1 + q, (*chip, c), me) for q, chip in enumerate(chips)] \
            + [copy(4 + q, (*chip, 1 - c), me) for q, chip in enumerate(chips)]
        forward = [None, None] + passed + [None, None, None]
        use_order = (0, 1, 2, 5, 3, 6, 4, 7)
        blocks, arrive, forward = ([lst[n] for n in use_order] for lst in (blocks, arrive, forward))

        def load(slot, src):
            return pltpu.make_async_copy(src, wbuf.at[slot], wsems.at[slot])

        @pl.when((j == 0) & (i == 0))
        def _():
            for cp in [mine, omine] + first + ocopies:
                cp.start()
            load(0, w_ref).start()

        for jj in range(N_DEV):
            @pl.when((j == jj) & (i == 0))
            def _():
                load(jj % 2, w_ref).wait()

            if jj + 1 < N_DEV:
                @pl.when((j == jj) & (i == pre))
                def _():
                    arrive[jj + 1].wait_recv()
                    if forward[jj + 1] is not None:
                        forward[jj + 1].start()
                    load((jj + 1) % 2, wf_ref.at[blk(blocks[jj + 1])]).start()

        z_ref[...] = _dot(h_ref[...], wbuf[j % 2])

        @pl.when((j == N_DEV - 1) & (i == ni - 1))
        def _():
            for cp in first + passed:
                cp.wait_send()
            for cp in ocopies:
                cp.wait_send()
                cp.wait_recv()
            mine.wait()
            omine.wait()

    grid_spec = pltpu.PrefetchScalarGridSpec(
        num_scalar_prefetch=1, grid=(N_DEV, ni),
        in_specs=[pl.BlockSpec((tm, d), lambda j, i, o: (i, 0)), ANY, ANY],
        out_specs=(pl.BlockSpec((tm, e), lambda j, i, o: (i, o[j])), ANY, ANY),
        scratch_shapes=[pltpu.VMEM((2, d, e), BF16), pltpu.SemaphoreType.DMA((7,)), pltpu.SemaphoreType.DMA((7,)),
                        pltpu.SemaphoreType.DMA((7,)), pltpu.SemaphoreType.DMA((7,)), pltpu.SemaphoreType.DMA((2,)),
                        pltpu.SemaphoreType.DMA((2,))])
    return pl.pallas_call(
        body, name="inproj_gather", grid_spec=grid_spec,
        out_shape=(jax.ShapeDtypeStruct((s, N_SPLITS * e), F32), jax.ShapeDtypeStruct((N_DEV, d, e), BF16),
                   jax.ShapeDtypeStruct((N_DEV,) + wo_loc.shape, BF16)),
        compiler_params=_params("arbitrary", "arbitrary"))(_block_order(GATHER_MASKS), h, w_loc, wo_loc)


SCATTER_MASKS = (7, 6, 5, 4, 3, 2, 1, 0)
N_CHIPS = 4


def _scatter_block(k, acc, stage, tmp, own_ref, ra_ref, rb_ref, sa_send, sa_recv, sb_send, sb_recv, loc_sem, step, ns):
    x, y, c = _position()
    chip_of = lambda t: _xor_peer(x, y, c, SCATTER_MASKS[2 * t + 1])
    last = step == ns - 1
    fetch_at = min(1, ns - 1)

    def ship(t):
        return pltpu.make_async_remote_copy(
            src_ref=stage.at[0], dst_ref=ra_ref.at[t], send_sem=sa_send.at[t], recv_sem=sa_recv.at[t],
            device_id=(x, y, 1 - c), device_id_type=MESH)

    def send(t):
        return pltpu.make_async_remote_copy(
            src_ref=stage.at[1], dst_ref=rb_ref.at[t], send_sem=sb_send.at[t], recv_sem=sb_recv.at[t],
            device_id=chip_of(t), device_id_type=MESH)

    for kk in range(N_DEV):
        t = kk // 2
        fetch = pltpu.make_async_copy(ra_ref.at[t], tmp, loc_sem)

        if kk % 2 == 1:
            @pl.when((step == fetch_at) & (k == kk))
            def _():
                ship(t).wait_recv()
                fetch.start()

        @pl.when(last & (k == kk))
        def _():
            if kk % 2 == 0:
                if t >= 1:
                    ship(t - 1).wait_send()
                stage[0] = acc[...].astype(BF16)
                ship(t).start()
            else:
                fetch.wait()
                acc[...] += tmp[...].astype(F32)
                if t < N_CHIPS - 1:
                    if t >= 1:
                        send(t - 1).wait_send()
                    stage[1] = acc[...].astype(BF16)
                    send(t).start()
                else:
                    keep = pltpu.make_async_copy(acc, own_ref, loc_sem)
                    keep.start()
                    keep.wait()
                    ship(t).wait_send()
                    send(t - 1).wait_send()
                    for q in range(N_CHIPS - 1):
                        send(q).wait_recv()


def _scatter_scratch(rows, cols):
    return [pltpu.VMEM((rows, cols), F32), pltpu.VMEM((2, rows, cols), BF16), pltpu.VMEM((rows, cols), BF16),
            pltpu.SemaphoreType.DMA((N_CHIPS,)), pltpu.SemaphoreType.DMA((N_CHIPS,)),
            pltpu.SemaphoreType.DMA((N_CHIPS - 1,)), pltpu.SemaphoreType.DMA((N_CHIPS - 1,)), pltpu.SemaphoreType.DMA(())]


def _scatter_out(rows, cols):
    return (jax.ShapeDtypeStruct((rows, cols), F32), jax.ShapeDtypeStruct((N_CHIPS, rows, cols), BF16),
            jax.ShapeDtypeStruct((N_CHIPS - 1, rows, cols), BF16))


def _dwin_scatter(h, dzh, dza):
    s, d = h.shape
    e = dzh.shape[2]
    ts = _tile(s, 1024)
    ns = s // ts

    def body(order_ref, dzh_ref, dza_ref, h_ref, own_ref, ra_ref, rb_ref, acc, stage, tmp, *sems):
        k, step = pl.program_id(0), pl.program_id(1)

        @pl.when(step == 0)
        def _():
            acc[...] = jnp.zeros_like(acc)

        def add(dz):
            acc[...] += _dot_tn(h_ref[...], dz)

        _dz_pick(order_ref[k], dzh_ref, dza_ref, add)
        _scatter_block(k, acc, stage, tmp, own_ref, ra_ref, rb_ref, *sems, step, ns)

    def dz_spec(lo):
        return pl.BlockSpec((None, ts, e), lambda k, st, o: (jnp.clip(o[k] - lo, 0, 3), st, 0))

    grid_spec = pltpu.PrefetchScalarGridSpec(
        num_scalar_prefetch=1, grid=(N_DEV, ns),
        in_specs=[dz_spec(0), dz_spec(4), pl.BlockSpec((ts, d), lambda k, st, o: (st, 0))],
        out_specs=(ANY, ANY, ANY), scratch_shapes=_scatter_scratch(d, e))
    own, _, rb = pl.pallas_call(
        body, name="dwin_scatter", grid_spec=grid_spec, out_shape=_scatter_out(d, e),
        compiler_params=_params("arbitrary", "arbitrary"))(_block_order(SCATTER_MASKS), dzh, dza, h)
    return own, rb


def _dwout_scatter(y_h, y_a, dxb):
    s, e = y_h.shape
    d = dxb.shape[1]
    r = 2 * e // N_DEV
    pairs = e // (2 * r)
    ts = _tile(s, 2048)
    ns = s // ts
    chip_masks = SCATTER_MASKS[1::2]
    passes = ((0, 1), (2,), (3,))
    slots = max(len(chips) for chips in passes)
    slot_chip = [chips[min(u, len(chips) - 1)] for chips in passes for u in range(slots)]

    def body(pair_ref, yh0_ref, ya0_ref, yh1_ref, ya1_ref, dx_ref, own_ref, ra_ref, rb_ref, acc, keep_buf, ship_buf,
             send_buf, tmp, sa_send, sa_recv, sb_send, sb_recv, loc_sem):
        p, step = pl.program_id(0), pl.program_id(1)
        x, y, c = _position()

        @pl.when(step == 0)
        def _():
            acc[...] = jnp.zeros_like(acc)

        for u, (yh_ref, ya_ref) in enumerate(((yh0_ref, ya0_ref), (yh1_ref, ya1_ref))):
            rows = slice(u * 2 * r, (u + 1) * 2 * r)
            used = functools.reduce(jnp.logical_or, [p == pp for pp, chips in enumerate(passes) if u < len(chips)])

            @pl.when(used & (pair_ref[slots * p + u] < pairs))
            def _():
                acc[rows, :] += _dot_tn(yh_ref[...], dx_ref[...])

            @pl.when(used & (pair_ref[slots * p + u] >= pairs))
            def _():
                acc[rows, :] += _dot_tn(ya_ref[...], dx_ref[...])

        def block_rows(u, core):
            return pl.ds(pl.multiple_of(u * 2 * r + core * r, r), r)

        slot_of = {q: u for chips in passes for u, q in enumerate(chips)}

        def ship(q):
            return pltpu.make_async_remote_copy(
                src_ref=ship_buf.at[slot_of[q]], dst_ref=ra_ref.at[q], send_sem=sa_send.at[q], recv_sem=sa_recv.at[q],
                device_id=(x, y, 1 - c), device_id_type=MESH)

        def send(q):
            return pltpu.make_async_remote_copy(
                src_ref=send_buf.at[slot_of[q]], dst_ref=rb_ref.at[q], send_sem=sb_send.at[q], recv_sem=sb_recv.at[q],
                device_id=_xor_peer(x, y, c, chip_masks[q]), device_id_type=MESH)

        def sibling_share(q):
            ship(q).wait_recv()
            fetch = pltpu.make_async_copy(ra_ref.at[q], tmp, loc_sem)
            fetch.start()
            fetch.wait()
            return tmp[...].astype(F32)

        shipped, sent = {}, {}
        for pp, chips in enumerate(passes):
            @pl.when((step == ns - 1) & (p == pp))
            def _():
                for u, q in enumerate(chips):
                    if u in shipped:
                        ship(shipped.pop(u)).wait_send()
                    ship_buf[u] = acc[block_rows(u, 1 - c), :].astype(BF16)
                    ship(q).start()
                    shipped[u] = q
                for u, q in enumerate(chips):
                    total = acc[block_rows(u, c), :] + sibling_share(q)
                    if q < N_CHIPS - 1:
                        if u in sent:
                            send(sent.pop(u)).wait_send()
                        send_buf[u] = total.astype(BF16)
                        send(q).start()
                        sent[u] = q
                    else:
                        keep_buf[...] = total
                        keep = pltpu.make_async_copy(keep_buf, own_ref, loc_sem)
                        keep.start()
                        keep.wait()
                if pp == len(passes) - 1:
                    for q in shipped.values():
                        ship(q).wait_send()
                    for q in sent.values():
                        send(q).wait_send()
                    for q in range(N_CHIPS - 1):
                        send(q).wait_recv()

    def y_spec(u, lo):
        return pl.BlockSpec((ts, 2 * r), lambda p, st, o: (st, jnp.clip(o[slots * p + u] - lo, 0, pairs - 1)))

    pair_of_chip = _block_order(chip_masks) // 2
    grid_spec = pltpu.PrefetchScalarGridSpec(
        num_scalar_prefetch=1, grid=(len(passes), ns),
        in_specs=[y_spec(0, 0), y_spec(0, pairs), y_spec(1, 0), y_spec(1, pairs),
                  pl.BlockSpec((ts, d), lambda p, st, o: (st, 0))],
        out_specs=(ANY, ANY, ANY),
        scratch_shapes=[pltpu.VMEM((slots * 2 * r, d), F32), pltpu.VMEM((r, d), F32),
                        pltpu.VMEM((slots, r, d), BF16)] + _scatter_scratch(r, d)[1:])
    own, _, rb = pl.pallas_call(
        body, name="dwout_scatter", grid_spec=grid_spec, out_shape=_scatter_out(r, d),
        compiler_params=_params("arbitrary", "arbitrary"))(
            jnp.stack([pair_of_chip[q] for q in slot_chip]), y_h, y_a, y_h, y_a, dxb)
    return own, rb


def _sum_chips_adamw(own, recv, w, m, v):
    r, c = w.shape
    tr = _tile(r, 256)

    def body(own_ref, rc_ref, w_ref, m_ref, v_ref, g_ref, d_ref, mo_ref, vo_ref):
        g = own_ref[...]
        for q in range(N_CHIPS - 1):
            g = g + rc_ref[q].astype(F32)
        g_ref[...] = g
        d_ref[...], mo_ref[...], vo_ref[...] = _adamw(w_ref[...], g, m_ref[...], v_ref[...])

    blk = pl.BlockSpec((tr, c), lambda i: (i, 0))
    shp = jax.ShapeDtypeStruct((r, c), F32)
    return pl.pallas_call(
        body, name="sum_chips_adamw", grid=(r // tr,), out_shape=(shp, shp, shp, shp),
        in_specs=[blk, pl.BlockSpec((N_CHIPS - 1, tr, c), lambda i: (0, i, 0)), blk, blk, blk],
        out_specs=(blk, blk, blk, blk), compiler_params=_params("parallel"))(own, recv, w, m, v)


SMALL_ROWS = 8
ROW_LB = 4
ROW_GN = 6
ROW_LOSS = 7


def _small_allreduce_adamw(part, w, m, v, lb_logits):
    width = part.shape[1]

    def body(p_ref, w_ref, m_ref, v_ref, lb_ref, g_ref, d_ref, mo_ref, vo_ref, buf, send_sems, recv_sems):
        x, y, c = _position()
        me = 4 * x + 2 * y + c
        buf[me] = p_ref[...]
        copies = []
        for k in range(N_DEV - 1):
            bx, by, bc = ((k + 1) >> 2) & 1, ((k + 1) >> 1) & 1, (k + 1) & 1
            peer = (x ^ bx, y ^ by, c ^ bc)
            copies.append(pltpu.make_async_remote_copy(
                src_ref=p_ref, dst_ref=buf.at[me], send_sem=send_sems.at[k], recv_sem=recv_sems.at[k],
                device_id=peer, device_id_type=MESH))
        for cp in copies:
            cp.start()
        for cp in copies:
            cp.wait_recv()
        for cp in copies:
            cp.wait_send()
        tot = buf[0]
        for dev in range(1, N_DEV):
            tot = tot + buf[dev]
        lbv = lb_ref[...]
        lb = _sigmoid(lbv[0:1] - lbv[1:2])
        glb = tot[ROW_LB:ROW_LB + 1] * lb * (1.0 - lb)
        loss = jnp.sum(tot[ROW_LOSS:ROW_LOSS + 1], axis=-1, keepdims=True)
        row = lax.broadcasted_iota(jnp.int32, (SMALL_ROWS, width), 0)
        g = jnp.where(row == ROW_LB, glb, jnp.where(row == ROW_LB + 1, -glb, tot))
        g = jnp.where(row == ROW_LOSS, loss, g)
        g_ref[...] = g
        d_ref[...], mo_ref[...], vo_ref[...] = _adamw(w_ref[...], g, m_ref[...], v_ref[...])

    vm = pl.BlockSpec(memory_space=pltpu.VMEM)
    shp = jax.ShapeDtypeStruct((SMALL_ROWS, width), F32)
    return pl.pallas_call(
        body, name="small_allreduce_adamw", out_shape=(shp, shp, shp, shp),
        in_specs=[vm] * 5, out_specs=(vm, vm, vm, vm),
        scratch_shapes=[pltpu.VMEM((N_DEV, SMALL_ROWS, width), F32), pltpu.SemaphoreType.DMA((N_DEV - 1,)),
                        pltpu.SemaphoreType.DMA((N_DEV - 1,))],
    )(part, w, m, v, lb_logits)


def _pack_small(norm_gain, final_gain, lb2, gnorm, last_row, width):
    pad = lambda a: jnp.pad(a.reshape(1, -1), ((0, 0), (0, width - a.size)))
    return jnp.concatenate([norm_gain.reshape(2, width), final_gain.reshape(2, width), lb2.reshape(2, width),
                            pad(gnorm), last_row.reshape(1, width)], axis=0)


def _unpack_small(p, d, e, hd):
    return (p[0:2].reshape(1, d), p[2:4].reshape(d), p[4:6].reshape(2, e), p[6:7, :hd].reshape(1, hd))


def kernel(x, norm_gain, w_in, lb_logits, hgrn_gnorm, w_out, final_gain, loss_target, m_norm_gain, m_w_in, m_lb_logits, m_hgrn_gnorm, m_w_out, m_final_gain, v_norm_gain, v_w_in, v_lb_logits, v_hgrn_gnorm, v_w_out, v_final_gain):
    s, d = x.shape[1], x.shape[2]
    e = w_in.shape[2]
    assert d == 2 * e and lb_logits.shape == (2, e) and w_out.shape[1] * N_DEV == 2 * e
    x2d = x.reshape(s, d)
    tgt = loss_target.reshape(s, d)

    h = _rmsnorm_fwd(x2d, norm_gain)
    z, w_in_full, w_out_full = _inproj_gather(h, _cast_bf16(w_in[0]), _cast_bf16(w_out[0]))
    w_out_full = w_out_full.reshape(2 * e, d)
    y_h, states = _hgrn_fwd(z, lb_logits, hgrn_gnorm)
    o_attn, lse, y_a = _attn_fwd(z)
    dx2, dx2b, dy, loss_vec, dfg = _outproj_loss(x2d, y_h, y_a, w_out_full, final_gain.reshape(1, d), tgt)

    own_o, recv_o = _dwout_scatter(y_h, y_a, dx2b)
    dza = _attn_bwd(z, dy, o_attn, lse)
    dzh, dlb, dgn = _hgrn_bwd(z, dy, states, lb_logits, hgrn_gnorm)
    grad_x, dng = _dh_dx(dzh, dza, w_in_full, x2d, norm_gain, dx2)
    g_wo, d_wo, nm_wo, nv_wo = _sum_chips_adamw(own_o, recv_o, w_out[0], m_w_out[0], v_w_out[0])

    width = d // 2
    zero_row = jnp.zeros((1, width), F32)
    loss_row = loss_vec[:, :width] + loss_vec[:, width:]
    part = _pack_small(dng, dfg, jnp.concatenate([dlb, zero_row], axis=0), dgn, loss_row, width)
    pw = _pack_small(norm_gain, final_gain, lb_logits, hgrn_gnorm, zero_row, width)
    pm = _pack_small(m_norm_gain, m_final_gain, m_lb_logits, m_hgrn_gnorm, zero_row, width)
    pv = _pack_small(v_norm_gain, v_final_gain, v_lb_logits, v_hgrn_gnorm, zero_row, width)
    sg, sd, sm, sv = _small_allreduce_adamw(part, pw, pm, pv, lb_logits)
    own_i, recv_i = _dwin_scatter(h, dzh, dza)
    g_wi, d_wi, nm_wi, nv_wi = _sum_chips_adamw(own_i, recv_i, w_in[0], m_w_in[0], v_w_in[0])
    hd = hgrn_gnorm.shape[1]
    g_ng, g_fg, g_lb, g_gn = _unpack_small(sg, d, e, hd)
    d_ng, d_fg, d_lb, d_gn = _unpack_small(sd, d, e, hd)
    m_ng, m_fg, m_lb, m_gn = _unpack_small(sm, d, e, hd)
    v_ng, v_fg, v_lb, v_gn = _unpack_small(sv, d, e, hd)
    loss = sg[ROW_LOSS, 0]

    one = lambda a: a[None]
    return (loss, grad_x.reshape(1, s, d), g_ng, one(g_wi), g_lb, g_gn, one(g_wo), g_fg,
            d_ng, one(d_wi), d_lb, d_gn, one(d_wo), d_fg,
            m_ng, one(nm_wi), m_lb, m_gn, one(nm_wo), m_fg,
            v_ng, one(nv_wi), v_lb, v_gn, one(nv_wo), v_fg)
```

```python
import functools
import math

import jax
import jax.numpy as jnp
from jax import lax
from jax.experimental import pallas as pl
from jax.experimental.pallas import tpu as pltpu

NORM_EPS = 1e-6
HGRN_HEAD = 128
HGRN_CHUNK = 64
ATTN_HEAD = 64
ATTN_BAND = 128
DILATIONS = (1, 4, 16)
N_SPLITS = 8
N_DEV = 8
ADAM_LR = 0.001
ADAM_B1 = 0.9
ADAM_B2 = 0.999
ADAM_EPS = 1e-08
ADAM_WD = 0.01
ADAM_STEP = 10
LANES = 128
MESH = pl.DeviceIdType.MESH
F32 = jnp.float32
BF16 = jnp.bfloat16
NEG_BIG = -1e30
VMEM_LIMIT = 56 * 1024 * 1024
OUTPROJ_VMEM_LIMIT = 63 * 1024 * 1024
DHDX_VMEM_LIMIT = 60 * 1024 * 1024

ANY = pl.BlockSpec(memory_space=pl.ANY)


def _params(*sem):
    return pltpu.CompilerParams(dimension_semantics=sem, vmem_limit_bytes=VMEM_LIMIT)


def _tile(n, pref):
    t = min(n, pref)
    assert n % t == 0, (n, pref)
    return t


def _dot(a, b, precision=None):
    return jnp.dot(a, b, preferred_element_type=F32, precision=precision)


def _dot_nt(a, b):
    return lax.dot_general(a, b, (((1,), (1,)), ((), ())), preferred_element_type=F32)


def _dot_tn(a, b):
    return lax.dot_general(a, b, (((0,), (0,)), ((), ())), preferred_element_type=F32)


def _sigmoid(x):
    return 0.5 * jnp.tanh(0.5 * x) + 0.5


def _dsilu(x, s):
    return s * (1.0 + x * (1.0 - s))


def _adamw(w, g, m, v):
    m = ADAM_B1 * m + (1.0 - ADAM_B1) * g
    v = ADAM_B2 * v + (1.0 - ADAM_B2) * (g * g)
    m_hat = m / (1.0 - ADAM_B1 ** ADAM_STEP)
    v_hat = v / (1.0 - ADAM_B2 ** ADAM_STEP)
    delta = -ADAM_LR * (m_hat / (jnp.sqrt(v_hat) + ADAM_EPS) + ADAM_WD * w)
    return delta, m, v


def _cast_bf16(a):
    r, c = a.shape
    tr = _tile(r, 256)

    def body(a_ref, o_ref):
        o_ref[...] = a_ref[...].astype(BF16)

    return pl.pallas_call(
        body, name="cast_bf16", grid=(r // tr,), out_shape=jax.ShapeDtypeStruct((r, c), BF16),
        in_specs=[pl.BlockSpec((tr, c), lambda i: (i, 0))], out_specs=pl.BlockSpec((tr, c), lambda i: (i, 0)),
        compiler_params=_params("parallel"))(a)


def _rmsnorm_fwd(x, gain):
    s, d = x.shape
    tm = _tile(s, 512)

    def body(x_ref, g_ref, h_ref):
        xv = x_ref[...]
        r = lax.rsqrt(jnp.mean(xv * xv, axis=-1, keepdims=True) + NORM_EPS)
        h_ref[...] = (xv * r * g_ref[...]).astype(BF16)

    return pl.pallas_call(
        body, name="rmsnorm_fwd", grid=(s // tm,), out_shape=jax.ShapeDtypeStruct((s, d), BF16),
        in_specs=[pl.BlockSpec((tm, d), lambda i: (i, 0)), pl.BlockSpec((1, d), lambda i: (0, 0))],
        out_specs=pl.BlockSpec((tm, d), lambda i: (i, 0)), compiler_params=_params("parallel"))(x, gain)


HGRN_BLOCK = 2048
TRI_ROWS = 64


def _chunk_masks():
    tb = TRI_ROWS
    row = lax.broadcasted_iota(jnp.int32, (tb, tb), 0)
    col = lax.broadcasted_iota(jnp.int32, (tb, tb), 1)
    same = (row // HGRN_CHUNK) == (col // HGRN_CHUNK)
    lower = jnp.where(same & (col <= row), 1.0, 0.0).astype(BF16)
    upper = jnp.where(same & (col >= row), 1.0, 0.0).astype(BF16)
    return lower, upper


def _split3(a):
    hi = a.astype(BF16).astype(F32)
    mid = (a - hi).astype(BF16).astype(F32)
    lo = (a - hi - mid).astype(BF16).astype(F32)
    return hi, mid, lo


def _tri_dot(tri, x):
    hi, mid, lo = (p.astype(BF16) for p in _split3(x))
    outs = []
    for r in range(0, x.shape[0], TRI_ROWS):
        sl = slice(r, r + TRI_ROWS)
        outs.append(_dot(tri, hi[sl]) + _dot(tri, mid[sl]) + _dot(tri, lo[sl]))
    return outs[0] if len(outs) == 1 else jnp.concatenate(outs, axis=0)


def _hgrn_gates(qp, fp, lbv):
    lb = _sigmoid(lbv[0:1] - lbv[1:2])
    sq = _sigmoid(qp)
    q = qp * sq
    sg = _sigmoid(fp)
    f = lb + (1.0 - lb) * sg
    k = 1.0 - f
    return lb, sq, q, sg, f, k


def _hgrn_fwd(z, lb_logits, gnorm):
    s = z.shape[0]
    e = z.shape[1] // N_SPLITS
    nh = e // HGRN_HEAD
    tb = _tile(s, HGRN_BLOCK)
    nc = tb // HGRN_CHUNK
    nb = s // tb
    C = HGRN_CHUNK

    def body(q_ref, f_ref, i_ref, g_ref, lb_ref, gn_ref, y_ref, st_ref, state, o_scr):
        @pl.when(pl.program_id(1) == 0)
        def _():
            state[...] = jnp.zeros_like(state)

        lb, sq, q, sg, f, k = _hgrn_gates(q_ref[...], f_ref[...], lb_ref[...])
        lower, _ = _chunk_masks()
        b = _tri_dot(lower, jnp.log(f))
        b3 = b.reshape(nc, C, HGRN_HEAD)
        bc = b3[:, C - 1:C, :]
        qt = (q * jnp.exp(b)).astype(BF16)
        kt = (k * jnp.exp(-b)).astype(BF16)
        ke = (k.reshape(nc, C, HGRN_HEAD) * jnp.exp(bc - b3)).reshape(tb, HGRN_HEAD).astype(BF16)
        v = i_ref[...].astype(BF16)
        tri = lax.broadcasted_iota(jnp.int32, (C, C), 1) <= lax.broadcasted_iota(jnp.int32, (C, C), 0)
        sls = [slice(c * C, (c + 1) * C) for c in range(nc)]
        kv = [_dot_tn(v[sl], ke[sl]) for sl in sls]
        a = [jnp.where(tri, _dot_nt(qt[sl], kt[sl]), 0.0).astype(BF16) for sl in sls]
        st = state[...]
        sts = []
        for c in range(nc):
            sts.append(st)
            st_ref[c] = st
            st = st * jnp.exp(bc[c]) + kv[c]
        state[...] = st
        for c, sl in enumerate(sls):
            o_scr[sl, :] = _dot(a[c], v[sl]) + _dot_nt(qt[sl], sts[c].astype(BF16))
        o = o_scr[...]
        rms = lax.rsqrt(jnp.mean(o * o, axis=-1, keepdims=True) + NORM_EPS)
        gp = g_ref[...]
        y_ref[...] = (o * rms * gn_ref[...] * (gp * _sigmoid(gp))).astype(BF16)

    col = lambda kk: (lambda h, n: (n, kk * nh + h))
    return pl.pallas_call(
        body, name="hgrn_fwd", grid=(nh, nb),
        out_shape=(jax.ShapeDtypeStruct((s, e), BF16),
                   jax.ShapeDtypeStruct((nh, s // C, HGRN_HEAD, HGRN_HEAD), F32)),
        in_specs=[pl.BlockSpec((tb, HGRN_HEAD), col(0)), pl.BlockSpec((tb, HGRN_HEAD), col(1)),
                  pl.BlockSpec((tb, HGRN_HEAD), col(2)), pl.BlockSpec((tb, HGRN_HEAD), col(3)),
                  pl.BlockSpec((2, HGRN_HEAD), lambda h, n: (0, h)), pl.BlockSpec((1, HGRN_HEAD), lambda h, n: (0, 0))],
        out_specs=(pl.BlockSpec((tb, HGRN_HEAD), lambda h, n: (n, h)),
                   pl.BlockSpec((None, nc, HGRN_HEAD, HGRN_HEAD), lambda h, n: (h, n, 0, 0))),
        scratch_shapes=[pltpu.VMEM((HGRN_HEAD, HGRN_HEAD), F32), pltpu.VMEM((tb, HGRN_HEAD), F32)],
        compiler_params=_params("parallel", "arbitrary"))(z, z, z, z, lb_logits, gnorm)


def _hgrn_bwd(z, dy, states, lb_logits, gnorm):
    s = z.shape[0]
    e = z.shape[1] // N_SPLITS
    nh = e // HGRN_HEAD
    tb = _tile(s, HGRN_BLOCK)
    nc = tb // HGRN_CHUNK
    nb = s // tb
    C = HGRN_CHUNK
    H = HGRN_HEAD

    def body(q_ref, f_ref, i_ref, g_ref, dy_ref, st_ref, lb_ref, gn_ref, dz_ref, dlb_ref, dgn_ref,
             gstate, o_scr, dq_scr, dk_scr, dv_scr, e_scr):
        first = (pl.program_id(0) == 0) & (pl.program_id(1) == 0)

        @pl.when(first)
        def _():
            dgn_ref[...] = jnp.zeros_like(dgn_ref)

        @pl.when(pl.program_id(1) == 0)
        def _():
            gstate[...] = jnp.zeros_like(gstate)
            dlb_ref[...] = jnp.zeros_like(dlb_ref)

        qp = q_ref[...]
        lb, sq, q, sg, f, k = _hgrn_gates(qp, f_ref[...], lb_ref[...])
        lower, upper = _chunk_masks()
        b = _tri_dot(lower, jnp.log(f))
        b3 = b.reshape(nc, C, H)
        bc = b3[:, C - 1:C, :]
        eb = jnp.exp(b)
        enb = jnp.exp(-b)
        eend = jnp.exp(bc - b3).reshape(tb, H)
        qt = (q * eb).astype(BF16)
        kt = (k * enb).astype(BF16)
        ke = (k * eend).astype(BF16)
        v = i_ref[...].astype(BF16)
        tri = lax.broadcasted_iota(jnp.int32, (C, C), 1) <= lax.broadcasted_iota(jnp.int32, (C, C), 0)
        sls = [slice(c * C, (c + 1) * C) for c in range(nc)]
        a = [jnp.where(tri, _dot_nt(qt[sl], kt[sl]), 0.0).astype(BF16) for sl in sls]
        for c, sl in enumerate(sls):
            o_scr[sl, :] = _dot(a[c], v[sl]) + _dot_nt(qt[sl], st_ref[c].astype(BF16))
        o = o_scr[...]
        rms = lax.rsqrt(jnp.mean(o * o, axis=-1, keepdims=True) + NORM_EPS)
        on = o * rms
        gn = gn_ref[...]
        gp = g_ref[...]
        sgg = _sigmoid(gp)
        dyv = dy_ref[...]
        d_on = dyv * (gp * sgg)
        dz_ref[3] = (dyv * on * gn * _dsilu(gp, sgg)).astype(BF16)
        dgn_ref[...] += jnp.sum(d_on * on, axis=0, keepdims=True)
        u = d_on * gn
        do = (rms * (u - on * jnp.mean(u * on, axis=-1, keepdims=True))).astype(BF16)
        gup = [_dot_tn(do[sl], qt[sl]) for sl in sls]
        da = [jnp.where(tri, _dot_nt(do[sl], v[sl]), 0.0).astype(BF16) for sl in sls]
        gt = gstate[...]
        gts = [None] * nc
        for c in reversed(range(nc)):
            gts[c] = gt
            gt = gt * jnp.exp(bc[c]) + gup[c]
        gstate[...] = gt
        for c, sl in enumerate(sls):
            stp = st_ref[c]
            gtb = gts[c].astype(BF16)
            dqt = _dot(da[c], kt[sl]) + _dot(do[sl], stp.astype(BF16))
            dkt = _dot_tn(da[c], qt[sl])
            dks = _dot(v[sl], gtb) * eend[sl]
            dv_scr[sl, :] = _dot_tn(a[c], do[sl]) + _dot_nt(ke[sl], gtb)
            dq_scr[sl, :] = dqt * eb[sl]
            dk_scr[sl, :] = dkt * enb[sl] + dks
            ech = (jnp.sum(k[sl] * dks, axis=0, keepdims=True)
                   + jnp.sum(gts[c] * jnp.exp(bc[c]) * stp, axis=0, keepdims=True))
            e_scr[sl, :] = jnp.broadcast_to(ech, (C, H))
        dq = dq_scr[...]
        dk = dk_scr[...]
        dlf = _tri_dot(upper, q * dq - k * dk) + e_scr[...]
        dft = dlf / f - dk
        dz_ref[0] = (dq * _dsilu(qp, sq)).astype(BF16)
        dz_ref[1] = (dft * (1.0 - lb) * sg * (1.0 - sg)).astype(BF16)
        dz_ref[2] = dv_scr[...].astype(BF16)
        dlb_ref[...] += jnp.sum(dft * (1.0 - sg), axis=0, keepdims=True)

    col = lambda kk: (lambda h, n: (nb - 1 - n, kk * nh + h))
    return pl.pallas_call(
        body, name="hgrn_bwd", grid=(nh, nb),
        out_shape=(jax.ShapeDtypeStruct((4, s, e), BF16), jax.ShapeDtypeStruct((1, e), F32),
                   jax.ShapeDtypeStruct((1, H), F32)),
        in_specs=[pl.BlockSpec((tb, H), col(0)), pl.BlockSpec((tb, H), col(1)),
                  pl.BlockSpec((tb, H), col(2)), pl.BlockSpec((tb, H), col(3)),
                  pl.BlockSpec((tb, H), lambda h, n: (nb - 1 - n, h)),
                  pl.BlockSpec((None, nc, H, H), lambda h, n: (h, nb - 1 - n, 0, 0)),
                  pl.BlockSpec((2, H), lambda h, n: (0, h)), pl.BlockSpec((1, H), lambda h, n: (0, 0))],
        out_specs=(pl.BlockSpec((4, tb, H), lambda h, n: (0, nb - 1 - n, h)),
                   pl.BlockSpec((1, H), lambda h, n: (0, h)), pl.BlockSpec((1, H), lambda h, n: (0, 0))),
        scratch_shapes=[pltpu.VMEM((H, H), F32)] + [pltpu.VMEM((tb, H), F32)] * 5,
        compiler_params=_params("arbitrary", "arbitrary"))(z, z, z, z, dy, states, lb_logits, gnorm)


ATTN_T = 16 * ATTN_BAND
SCALE = ATTN_HEAD ** -0.5


def _slope(pair, hh, nheads):
    head = (2 * pair + hh + 1).astype(F32)
    return jnp.exp(jnp.full((1, 1), -8.0 / nheads * math.log(2.0), F32) * head)


def _fill_bias(bias, pair, nheads, delta, edge_ok):
    band = (delta >= 0) & (delta <= ATTN_BAND)
    dist = delta.astype(F32)
    for pi, dil in enumerate(DILATIONS):
        for hh in range(2):
            full = jnp.where(band, -(_slope(pair, hh, nheads) * float(dil)) * dist, NEG_BIG)
            bias[(pi * 2 + hh) * 2] = full
            bias[(pi * 2 + hh) * 2 + 1] = jnp.where(edge_ok, full, NEG_BIG)


def _rows(start, size, stride):
    if stride == 1:
        return pl.ds(pl.multiple_of(start, ATTN_BAND), size)
    return pl.ds(start, size, stride=stride)


def _head_lanes(rows, hh):
    return (lax.broadcasted_iota(jnp.int32, (rows, LANES), 1) // ATTN_HEAD) == hh


def _attn_fwd(z):
    s = z.shape[0]
    e = z.shape[1] // N_SPLITS
    npair = e // LANES
    T = ATTN_T
    assert s % T == 0
    nsb = s // T
    W = ATTN_BAND
    nt = T // W
    HD = ATTN_HEAD
    chunk = 256

    nsteps = npair * nsb
    assert nt % 2 == 0 and T // chunk == nt // 2

    def body(q_ref, kp_ref, kc_ref, vp_ref, vc_ref, g_ref, o_ref, l_ref, y_ref, qa, kbuf, va, bias, *sets):
        t = pl.program_id(0)
        sb = jnp.minimum(t, nsteps - 1) % nsb
        pair = jnp.minimum(t, nsteps - 1) // nsb
        sets = (sets[0:3], sets[3:6])

        def merge_chunk(i, done):
            accs, ms, lsw = done
            rows = pl.ds(pl.multiple_of(i * chunk, chunk), chunk)
            m1, m2, m3 = ms[0, rows, :], ms[1, rows, :], ms[2, rows, :]
            mx = jnp.maximum(jnp.maximum(m1, m2), m3)
            w1, w2, w3 = jnp.exp(m1 - mx), jnp.exp(m2 - mx), jnp.exp(m3 - mx)
            unswap = lambda a: pltpu.roll(a, ATTN_HEAD, 1)
            den = w1 * unswap(lsw[0, rows, :]) + w2 * unswap(lsw[1, rows, :]) + w3 * unswap(lsw[2, rows, :])
            o = (w1 * accs[0, rows, :] + w2 * accs[1, rows, :] + w3 * accs[2, rows, :]) / den
            o_ref[rows, :] = o
            l_ref[rows, :] = mx + jnp.log(den)
            gp = g_ref[rows, :]
            y_ref[rows, :] = (o * (gp * _sigmoid(gp))).astype(BF16)

        @pl.when(t == 0)
        def _():
            accs, ms, lsw = sets[1]
            accs[...] = jnp.zeros_like(accs)
            ms[...] = jnp.zeros_like(ms)
            lsw[...] = jnp.ones_like(lsw)

        @pl.when(t == nsteps)
        def _():
            def drain(i, carry):
                merge_chunk(i, sets[(nsteps - 1) % 2])
                return carry

            lax.fori_loop(0, T // chunk, drain, 0)

        def compute(cur, done):
            accs, ms, lsw = cur
            def stage(i, carry):
                rows = pl.ds(pl.multiple_of(i * chunk, chunk), chunk)
                upper = pl.ds(pl.multiple_of(T + i * chunk, chunk), chunk)
                kbuf[upper, :] = kc_ref[rows, :]
                for hh in range(2):
                    mine = _head_lanes(chunk, hh)
                    qa[hh, rows, :] = jnp.where(mine, q_ref[rows, :] * SCALE, 0.0)
                    va[hh, upper, :] = jnp.where(mine, vc_ref[rows, :], 1.0)
                return carry

            lax.fori_loop(0, T // chunk, stage, 0)

            @pl.when(sb == 0)
            def _():
                def stage_prev(i, carry):
                    rows = pl.ds(pl.multiple_of(i * chunk, chunk), chunk)
                    kbuf[rows, :] = kp_ref[rows, :]
                    for hh in range(2):
                        va[hh, rows, :] = jnp.where(_head_lanes(chunk, hh), vp_ref[rows, :], 1.0)
                    return carry

                lax.fori_loop(0, T // chunk, stage_prev, 0)

            @pl.when(sb == 0)
            def _():
                qi = lax.broadcasted_iota(jnp.int32, (W, 2 * W), 0)
                kj = lax.broadcasted_iota(jnp.int32, (W, 2 * W), 1)
                _fill_bias(bias, pair, 2 * npair, W + qi - kj, kj >= W)

            def tile(tau):
                first = _head_lanes(W, 0)
                rows, scores = [], []
                for pi, dil in enumerate(DILATIONS):
                    r = tau % dil
                    ub = tau // dil
                    qrows = _rows(r + dil * W * ub, W, dil)
                    krows = _rows(T + dil * W * (ub - 1) + r, 2 * W, dil)
                    var = jnp.where((sb == 0) & (ub == 0), 1, 0)
                    kt = kbuf[krows, :].astype(BF16)
                    rows.append((qrows, krows))
                    scores.append([_dot_nt(qa[hh, qrows, :].astype(BF16), kt) + bias[(pi * 2 + hh) * 2 + var]
                                   for hh in range(2)])
                maxes = [[jnp.max(sc, axis=-1, keepdims=True) for sc in pair_] for pair_ in scores]
                probs = [[jnp.exp(sc - m).astype(BF16) for sc, m in zip(ps, pm)] for ps, pm in zip(scores, maxes)]
                for pi, (qrows, krows) in enumerate(rows):
                    outs = [_dot(probs[pi][hh], va[hh, krows, :].astype(BF16)) for hh in range(2)]
                    accs[pi, qrows, :] = jnp.where(first, outs[0], outs[1])
                    lsw[pi, qrows, :] = jnp.where(first, outs[1], outs[0])
                    ms[pi, qrows, :] = jnp.where(first, maxes[pi][0], maxes[pi][1])

            def two_tiles(i, carry):
                tile(2 * i)
                tile(2 * i + 1)
                merge_chunk(i, done)
                return carry

            lax.fori_loop(0, nt // 2, two_tiles, 0)

            def move_down(i, carry):
                rows = pl.ds(pl.multiple_of(i * chunk, chunk), chunk)
                upper = pl.ds(pl.multiple_of(T + i * chunk, chunk), chunk)
                kbuf[rows, :] = kbuf[upper, :]
                for hh in range(2):
                    va[hh, rows, :] = va[hh, upper, :]
                return carry

            lax.fori_loop(0, T // chunk, move_down, 0)

        for parity in range(2):
            @pl.when((t < nsteps) & (t % 2 == parity))
            def _():
                compute(sets[parity], sets[1 - parity])

    step_of = lambda t: jnp.minimum(t, nsteps - 1)
    lag_of = lambda t: jnp.maximum(t - 1, 0)
    cur = lambda split: (lambda t: (step_of(t) % nsb, split * npair + step_of(t) // nsb))
    prev = lambda split: (lambda t: (0, split * npair + step_of(t) // nsb))
    blk = lambda index: pl.BlockSpec((T, LANES), index)
    out = blk(lambda t: (lag_of(t) % nsb, lag_of(t) // nsb))
    gate = blk(lambda t: (lag_of(t) % nsb, 7 * npair + lag_of(t) // nsb))
    buf = lambda rows: pltpu.VMEM((rows, LANES), F32)
    return pl.pallas_call(
        body, name="attn_fwd", grid=(nsteps + 1,),
        out_shape=(jax.ShapeDtypeStruct((s, e), F32), jax.ShapeDtypeStruct((s, e), F32), jax.ShapeDtypeStruct((s, e), BF16)),
        in_specs=[blk(cur(4)), blk(prev(5)), blk(cur(5)), blk(prev(6)), blk(cur(6)), gate],
        out_specs=(out, out, out),
        scratch_shapes=[pltpu.VMEM((2, T, LANES), F32), buf(2 * T), pltpu.VMEM((2, 2 * T, LANES), F32),
                        pltpu.VMEM((12, W, 2 * W), F32)] + [pltpu.VMEM((3, T, LANES), F32)] * 6,
        compiler_params=_params("arbitrary"))(z, z, z, z, z, z)


def _outproj_loss(x, y_h, y_a, w_out_full, final_gain, target):
    s, d = x.shape
    e = y_h.shape[1]
    tm = _tile(s, 512)
    sub = _tile(tm, 256)

    def body(x_ref, yh_ref, ya_ref, w_ref, g_ref, t_ref, dx_ref, dxb_ref, dy_ref, loss_ref, dg_ref):
        @pl.when(pl.program_id(0) == 0)
        def _():
            loss_ref[...] = jnp.zeros_like(loss_ref)
            dg_ref[...] = jnp.zeros_like(dg_ref)

        w = w_ref[...]
        g = g_ref[...]
        parts = [slice(r0, r0 + sub) for r0 in range(0, tm, sub)]
        x2s = [x_ref[rows, :] + _dot(yh_ref[rows, :], w[0:e]) + _dot(ya_ref[rows, :], w[e:2 * e]) for rows in parts]
        for rows, x2 in zip(parts, x2s):
            r = lax.rsqrt(jnp.mean(x2 * x2, axis=-1, keepdims=True) + NORM_EPS)
            xn = x2 * r
            err = xn * g - t_ref[rows, :]
            loss_ref[...] += jnp.sum(err * err, axis=0, keepdims=True) * (0.5 / d)
            dyo = err * (1.0 / d)
            dg_ref[...] += jnp.sum(dyo * xn, axis=0, keepdims=True)
            u = dyo * g
            dx2 = r * (u - xn * jnp.mean(u * xn, axis=-1, keepdims=True))
            dx_ref[rows, :] = dx2
            dxb = dx2.astype(BF16)
            dxb_ref[rows, :] = dxb
            dy_ref[rows, :] = _dot_nt(dxb, w)

    row = pl.BlockSpec((tm, d), lambda i: (i, 0))
    half = pl.BlockSpec((tm, e), lambda i: (i, 0))
    vec = pl.BlockSpec((1, d), lambda i: (0, 0))
    whole = pl.BlockSpec((2 * e, d), lambda i: (0, 0), pipeline_mode=pl.Buffered(1))
    return pl.pallas_call(
        body, name="outproj_loss", grid=(s // tm,),
        out_shape=(jax.ShapeDtypeStruct((s, d), F32), jax.ShapeDtypeStruct((s, d), BF16),
                   jax.ShapeDtypeStruct((s, 2 * e), F32), jax.ShapeDtypeStruct((1, d), F32),
                   jax.ShapeDtypeStruct((1, d), F32)),
        in_specs=[row, half, half, whole, vec, row],
        out_specs=(row, row, pl.BlockSpec((tm, 2 * e), lambda i: (i, 0)), vec, vec),
        compiler_params=pltpu.CompilerParams(dimension_semantics=("arbitrary",), vmem_limit_bytes=OUTPROJ_VMEM_LIMIT),
    )(x, y_h, y_a, w_out_full, final_gain, target)


def _attn_bwd(z, dy, o, lse):
    s, e = o.shape
    npair = e // LANES
    T = ATTN_T
    assert s % T == 0
    nsb = s // T
    W = ATTN_BAND
    nt = T // W
    HD = ATTN_HEAD
    chunk = 256

    def body(k_ref, v_ref, qc_ref, qn_ref, dyc_ref, dyn_ref, gc_ref, gn_ref, oc_ref, on_ref, lc_ref, ln_ref,
             dz_ref, qa, doa, ka, va, dqacc, dkacc, dvacc, bias):
        sb = pl.program_id(1)
        def stage_queries(half, q_r, dy_r, g_r, o_r, l_r):
            def stage(i, carry):
                rows = pl.ds(pl.multiple_of(i * chunk, chunk), chunk)
                dst = pl.ds(pl.multiple_of(half * T + i * chunk, chunk), chunk)
                lane = lax.broadcasted_iota(jnp.int32, (chunk, LANES), 1)
                gp = g_r[rows, :]
                dov = dy_r[rows, :] * (gp * _sigmoid(gp))
                qv = q_r[rows, :] * SCALE
                same_head = (lax.broadcasted_iota(jnp.int32, (LANES, LANES), 0) // HD
                             == lax.broadcasted_iota(jnp.int32, (LANES, LANES), 1) // HD)
                ones = jnp.where(same_head, 1.0, 0.0).astype(BF16)
                hi, mid, lo = (p.astype(BF16) for p in _split3(dov * o_r[rows, :]))
                delta = _dot(hi, ones) + _dot(mid, ones) + _dot(lo, ones)
                swap = lambda a: pltpu.roll(a, HD, 1)
                lse_parts = [swap(p) for p in _split3(l_r[rows, :])]
                dl_parts = [swap(p) for p in _split3(delta)]
                for hh in range(2):
                    mine = _head_lanes(chunk, hh)
                    spare = (1 - hh) * HD
                    qh = jnp.where(mine, qv, 0.0)
                    dh = jnp.where(mine, dov, 0.0)
                    for j in range(3):
                        qh = jnp.where(lane == spare + j, lse_parts[j], qh)
                        dh = jnp.where(lane == spare + j, dl_parts[j], dh)
                    qa[hh, dst, :] = qh
                    doa[hh, dst, :] = dh
                return carry

            lax.fori_loop(0, T // chunk, stage, 0)

        @pl.when(sb == 0)
        def _():
            stage_queries(0, qc_ref, dyc_ref, gc_ref, oc_ref, lc_ref)

        stage_queries(1, qn_ref, dyn_ref, gn_ref, on_ref, ln_ref)

        def stage_keys(i, carry):
            rows = pl.ds(pl.multiple_of(i * chunk, chunk), chunk)
            lane = lax.broadcasted_iota(jnp.int32, (chunk, LANES), 1)
            for hh in range(2):
                spare = (1 - hh) * HD
                minus = (lane >= spare) & (lane < spare + 3)
                ka[hh, rows, :] = jnp.where(minus, -1.0, k_ref[rows, :])
                va[hh, rows, :] = jnp.where(minus, -1.0, v_ref[rows, :])
            gp = gc_ref[rows, :]
            dz_ref[3, rows, :] = (dyc_ref[rows, :] * oc_ref[rows, :] * _dsilu(gp, _sigmoid(gp))).astype(BF16)
            return carry

        lax.fori_loop(0, T // chunk, stage_keys, 0)

        @pl.when(sb == 0)
        def _():
            dqacc[0:T, :] = jnp.zeros((T, LANES), F32)

        dqacc[T:, :] = jnp.zeros((T, LANES), F32)
        dkacc[...] = jnp.zeros_like(dkacc)
        dvacc[...] = jnp.zeros_like(dvacc)
        @pl.when(sb == 0)
        def _():
            qi = lax.broadcasted_iota(jnp.int32, (2 * W, W), 0)
            kj = lax.broadcasted_iota(jnp.int32, (2 * W, W), 1)
            _fill_bias(bias, pl.program_id(0), 2 * npair, qi - kj, qi < W)

        def tile(tau, carry):
            def scores(step, pi):
                dil = DILATIONS[pi]
                r = step % dil
                ub = step // dil
                start = r + dil * W * ub
                krows = _rows(start, W, dil)
                qrows = _rows(start, 2 * W, dil)
                var = jnp.where((sb == nsb - 1) & (ub == nt // dil - 1), 1, 0)
                unit = dict(krows=krows, qrows=qrows, ops=[], sc=[], dpd=[])
                for hh in range(2):
                    kt = ka[hh, krows, :].astype(BF16)
                    vt = va[hh, krows, :].astype(BF16)
                    qt = qa[hh, qrows, :].astype(BF16)
                    dt = doa[hh, qrows, :].astype(BF16)
                    unit["ops"].append((kt, qt, dt))
                    unit["sc"].append(_dot_nt(qt, kt) + bias[(pi * 2 + hh) * 2 + var])
                    unit["dpd"].append(_dot_nt(dt, vt))
                return unit

            def elementwise(unit):
                ps = [jnp.exp(s_) for s_ in unit["sc"]]
                unit["ds"] = [(p * d).astype(BF16) for p, d in zip(ps, unit["dpd"])]
                unit["pb"] = [p.astype(BF16) for p in ps]

            def products(unit):
                dvs = [_dot_tn(pb, dt) for pb, (kt, qt, dt) in zip(unit["pb"], unit["ops"])]
                dks = [_dot_tn(ds, qt) for ds, (kt, qt, dt) in zip(unit["ds"], unit["ops"])]
                dqs = [_dot(ds, kt) for ds, (kt, qt, dt) in zip(unit["ds"], unit["ops"])]
                dkacc[unit["krows"], :] += jnp.where(_head_lanes(W, 0), dks[0], dks[1])
                dvacc[unit["krows"], :] += jnp.where(_head_lanes(W, 0), dvs[0], dvs[1])
                dqacc[unit["qrows"], :] += jnp.where(_head_lanes(2 * W, 0), dqs[0], dqs[1]) * SCALE

            order = [(2 * tau + half, pi) for half in range(2) for pi in range(len(DILATIONS))]
            units = [None] * len(order)
            for n in range(len(order) + 2):
                if n < len(order):
                    units[n] = scores(*order[n])
                if 1 <= n <= len(order):
                    elementwise(units[n - 1])
                if n >= 2:
                    products(units[n - 2])
            return carry

        lax.fori_loop(0, nt // 2, tile, 0)

        def flush(i, carry):
            rows = pl.ds(pl.multiple_of(i * chunk, chunk), chunk)
            nxt = pl.ds(pl.multiple_of(T + i * chunk, chunk), chunk)
            dz_ref[0, rows, :] = dqacc[rows, :].astype(BF16)
            dz_ref[1, rows, :] = dkacc[rows, :].astype(BF16)
            dz_ref[2, rows, :] = dvacc[rows, :].astype(BF16)
            dqacc[rows, :] = dqacc[nxt, :]
            for hh in range(2):
                qa[hh, rows, :] = qa[hh, nxt, :]
                doa[hh, rows, :] = doa[hh, nxt, :]
            return carry

        lax.fori_loop(0, T // chunk, flush, 0)

    zc = lambda split: (lambda hp, sb: (sb, split * npair + hp))
    zn = lambda split: (lambda hp, sb: (jnp.minimum(sb + 1, nsb - 1), split * npair + hp))
    ec = lambda off: (lambda hp, sb: (sb, off + hp))
    en = lambda off: (lambda hp, sb: (jnp.minimum(sb + 1, nsb - 1), off + hp))
    z0 = lambda split: (lambda hp, sb: (0, split * npair + hp))
    e0 = lambda off: (lambda hp, sb: (0, off + hp))
    blk = lambda index: pl.BlockSpec((T, LANES), index)
    buf = lambda rows: pltpu.VMEM((rows, LANES), F32)
    return pl.pallas_call(
        body, name="attn_bwd", grid=(npair, nsb), out_shape=jax.ShapeDtypeStruct((4, s, e), BF16),
        in_specs=[blk(zc(5)), blk(zc(6)), blk(z0(4)), blk(zn(4)), blk(ec(npair)), blk(en(npair)),
                  blk(zc(7)), blk(zn(7)), blk(ec(0)), blk(en(0)), blk(e0(0)), blk(en(0))],
        out_specs=pl.BlockSpec((4, T, LANES), lambda hp, sb: (0, sb, hp)),
        scratch_shapes=[pltpu.VMEM((2, 2 * T, LANES), F32), pltpu.VMEM((2, 2 * T, LANES), F32),
                        pltpu.VMEM((2, T, LANES), F32), pltpu.VMEM((2, T, LANES), F32),
                        buf(2 * T), buf(T), buf(T), pltpu.VMEM((12, 2 * W, W), F32)],
        compiler_params=_params("parallel", "arbitrary"))(z, z, z, z, dy, dy, z, z, o, o, lse, lse)


def _dz_specs(tm, e):
    def mk(lo, hi):
        return pl.BlockSpec((None, tm, e), lambda i, k: (jnp.clip(k - lo, 0, hi - lo - 1), i, 0))
    return [mk(0, 4), mk(4, 8)]


def _dz_pick(grp, dzh_ref, dza_ref, fn):
    @pl.when(grp < 4)
    def _():
        fn(dzh_ref[...])

    @pl.when(grp >= 4)
    def _():
        fn(dza_ref[...])


def _dh_dx(dzh, dza, w_full, x, gain, dx2):
    s, d = x.shape
    e = dzh.shape[2]
    tm = _tile(s, 1024)
    ni = s // tm
    chunk = _tile(tm, 128)
    fetch_at = 2

    def body(dzh_ref, dza_ref, w_ref, x_hbm, g_ref, dx2_hbm, gx_hbm, dg_ref, acc, xbuf, dbuf, sems):
        i, k = pl.program_id(0), pl.program_id(1)
        rows_of = lambda tile: pl.ds(pl.multiple_of(tile * tm, tm), tm)
        fetch_x = pltpu.make_async_copy(x_hbm.at[rows_of(i), :], xbuf, sems.at[0])
        fetch_d = pltpu.make_async_copy(dx2_hbm.at[rows_of(i), :], dbuf, sems.at[1])

        def store(tile):
            return pltpu.make_async_copy(xbuf, gx_hbm.at[rows_of(tile), :], sems.at[2])

        @pl.when((i == 0) & (k == 0))
        def _():
            dg_ref[...] = jnp.zeros_like(dg_ref)

        @pl.when((k == fetch_at) & (i > 0))
        def _():
            store(i).wait()

        @pl.when(k == fetch_at)
        def _():
            fetch_x.start()
            fetch_d.start()

        def finish(tile):
            fetch_x.wait()
            fetch_d.wait()
            gain_row = g_ref[...]
            for r0 in range(0, tm, chunk):
                rows = slice(r0, r0 + chunk)
                dh = acc[rows, :]
                xv = xbuf[rows, :]
                r = lax.rsqrt(jnp.mean(xv * xv, axis=-1, keepdims=True) + NORM_EPS)
                xn = xv * r
                u = dh * gain_row
                xbuf[rows, :] = dbuf[rows, :] + r * (u - xn * jnp.mean(u * xn, axis=-1, keepdims=True))
                dg_ref[...] += jnp.sum(dh * xn, axis=0, keepdims=True)
            store(tile).start()

        @pl.when((k == 0) & (i == 0))
        def _():
            acc[...] = _dot_nt(dzh_ref[...], w_ref[...])

        @pl.when((k == 0) & (i > 0))
        def _():
            finish(i - 1)
            acc[...] = _dot_nt(dzh_ref[...], w_ref[...])

        @pl.when(k > 0)
        def _():
            def add(dz):
                acc[...] += _dot_nt(dz, w_ref[...])

            _dz_pick(k, dzh_ref, dza_ref, add)

        @pl.when((k == N_SPLITS - 1) & (i == ni - 1))
        def _():
            finish(i)
            store(i).wait()

    vec = pl.BlockSpec((1, d), lambda i, k: (0, 0))
    return pl.pallas_call(
        body, name="dh_dx", grid=(ni, N_SPLITS),
        out_shape=(jax.ShapeDtypeStruct((s, d), F32), jax.ShapeDtypeStruct((1, d), F32)),
        in_specs=_dz_specs(tm, e) + [pl.BlockSpec((None, d, e), lambda i, k: (k, 0, 0)), ANY, vec, ANY],
        out_specs=(ANY, vec),
        scratch_shapes=[pltpu.VMEM((tm, d), F32)] * 3 + [pltpu.SemaphoreType.DMA((3,))],
        compiler_params=pltpu.CompilerParams(dimension_semantics=("arbitrary", "arbitrary"),
                                             vmem_limit_bytes=DHDX_VMEM_LIMIT))(dzh, dza, w_full, x, gain, dx2)


def _position():
    x, y, c = lax.axis_index("x"), lax.axis_index("y"), lax.axis_index("c")
    return x, y, c


def _xor_peer(x, y, c, mask):
    return (x ^ ((mask >> 2) & 1), y ^ ((mask >> 1) & 1), c ^ (mask & 1))


def _block_order(masks):
    me = 4 * lax.axis_index("x") + 2 * lax.axis_index("y") + lax.axis_index("c")
    return jnp.stack([me ^ m for m in masks]).astype(jnp.int32)


GATHER_MASKS = (0, 1, 4, 5, 2, 3, 6, 7)


def _inproj_gather(h, w_loc, wo_loc):
    s, d = h.shape
    e = w_loc.shape[1]
    tm = _tile(s, 1024)
    ni = s // tm
    pre = max(ni - 2, 0)

    def body(order_ref, h_ref, w_ref, wo_ref, z_ref, wf_ref, wof_ref, wbuf, send_sems, recv_sems, osend, orecv,
             local_sems, wsems):
        j, i = pl.program_id(0), pl.program_id(1)
        x, y, c = _position()
        me, sibling = (x, y, c), (x, y, 1 - c)
        chips = [(1 - x, y), (x, 1 - y), (1 - x, 1 - y)]
        blk = lambda p: 4 * p[0] + 2 * p[1] + p[2]

        def copy(k, block, to, src=None):
            dst = wf_ref.at[blk(block)]
            return pltpu.make_async_remote_copy(
                src_ref=dst if src is None else src, dst_ref=dst, send_sem=send_sems.at[k], recv_sem=recv_sems.at[k],
                device_id=to, device_id_type=MESH)

        first = [copy(0, me, sibling, src=w_ref)] + [copy(1 + q, me, (*chip, c), src=w_ref) for q, chip in enumerate(chips)]
        passed = [copy(4 + q, (*chip, c), sibling) for q, chip in enumerate(chips)]
        mine = pltpu.make_async_copy(w_ref, wf_ref.at[blk(me)], local_sems.at[0])
        ocopies = [pltpu.make_async_remote_copy(
            src_ref=wo_ref, dst_ref=wof_ref.at[blk(me)], send_sem=osend.at[k], recv_sem=orecv.at[k],
            device_id=_xor_peer(x, y, c, k + 1), device_id_type=MESH) for k in range(N_DEV - 1)]
        omine = pltpu.make_async_copy(wo_ref, wof_ref.at[blk(me)], local_sems.at[1])
        blocks = [me, sibling] + [(*chip, c) for chip in chips] + [(*chip, 1 - c) for chip in chips]
        arrive = [None, copy(0, sibling, me)] + [copy(1 + q, (*chip, c), me) for q, chip in enumerate(chips)] \
            + [copy(4 + q, (*chip, 1 - c), me) for q, chip in enumerate(chips)]
        forward = [None, None] + passed + [None, None, None]
        use_order = (0, 1, 2, 5, 3, 6, 4, 7)
        blocks, arrive, forward = ([lst[n] for n in use_order] for lst in (blocks, arrive, forward))

        def load(slot, src):
            return pltpu.make_async_copy(src, wbuf.at[slot], wsems.at[slot])

        @pl.when((j == 0) & (i == 0))
        def _():
            for cp in [mine, omine] + first + ocopies:
                cp.start()
            load(0, w_ref).start()

        for jj in range(N_DEV):
            @pl.when((j == jj) & (i == 0))
            def _():
                load(jj % 2, w_ref).wait()

            if jj + 1 < N_DEV:
                @pl.when((j == jj) & (i == pre))
                def _():
                    arrive[jj + 1].wait_recv()
                    if forward[jj + 1] is not None:
                        forward[jj + 1].start()
                    load((jj + 1) % 2, wf_ref.at[blk(blocks[jj + 1])]).start()

        z_ref[...] = _dot(h_ref[...], wbuf[j % 2])

        @pl.when((j == N_DEV - 1) & (i == ni - 1))
        def _():
            for cp in first + passed:
                cp.wait_send()
            for cp in ocopies:
                cp.wait_send()
                cp.wait_recv()
            mine.wait()
            omine.wait()

    grid_spec = pltpu.PrefetchScalarGridSpec(
        num_scalar_prefetch=1, grid=(N_DEV, ni),
        in_specs=[pl.BlockSpec((tm, d), lambda j, i, o: (i, 0)), ANY, ANY],
        out_specs=(pl.BlockSpec((tm, e), lambda j, i, o: (i, o[j])), ANY, ANY),
        scratch_shapes=[pltpu.VMEM((2, d, e), BF16), pltpu.SemaphoreType.DMA((7,)), pltpu.SemaphoreType.DMA((7,)),
                        pltpu.SemaphoreType.DMA((7,)), pltpu.SemaphoreType.DMA((7,)), pltpu.SemaphoreType.DMA((2,)),
                        pltpu.SemaphoreType.DMA((2,))])
    return pl.pallas_call(
        body, name="inproj_gather", grid_spec=grid_spec,
        out_shape=(jax.ShapeDtypeStruct((s, N_SPLITS * e), F32), jax.ShapeDtypeStruct((N_DEV, d, e), BF16),
                   jax.ShapeDtypeStruct((N_DEV,) + wo_loc.shape, BF16)),
        compiler_params=_params("arbitrary", "arbitrary"))(_block_order(GATHER_MASKS), h, w_loc, wo_loc)


SCATTER_MASKS = (7, 6, 5, 4, 3, 2, 1, 0)
N_CHIPS = 4


def _scatter_block(k, acc, stage, tmp, own_ref, ra_ref, rb_ref, sa_send, sa_recv, sb_send, sb_recv, loc_sem, step, ns):
    x, y, c = _position()
    chip_of = lambda t: _xor_peer(x, y, c, SCATTER_MASKS[2 * t + 1])
    last = step == ns - 1
    fetch_at = min(1, ns - 1)

    def ship(t):
        return pltpu.make_async_remote_copy(
            src_ref=stage.at[0], dst_ref=ra_ref.at[t], send_sem=sa_send.at[t], recv_sem=sa_recv.at[t],
            device_id=(x, y, 1 - c), device_id_type=MESH)

    def send(t):
        return pltpu.make_async_remote_copy(
            src_ref=stage.at[1], dst_ref=rb_ref.at[t], send_sem=sb_send.at[t], recv_sem=sb_recv.at[t],
            device_id=chip_of(t), device_id_type=MESH)

    for kk in range(N_DEV):
        t = kk // 2
        fetch = pltpu.make_async_copy(ra_ref.at[t], tmp, loc_sem)

        if kk % 2 == 1:
            @pl.when((step == fetch_at) & (k == kk))
            def _():
                ship(t).wait_recv()
                fetch.start()

        @pl.when(last & (k == kk))
        def _():
            if kk % 2 == 0:
                if t >= 1:
                    ship(t - 1).wait_send()
                stage[0] = acc[...].astype(BF16)
                ship(t).start()
            else:
                fetch.wait()
                acc[...] += tmp[...].astype(F32)
                if t < N_CHIPS - 1:
                    if t >= 1:
                        send(t - 1).wait_send()
                    stage[1] = acc[...].astype(BF16)
                    send(t).start()
                else:
                    keep = pltpu.make_async_copy(acc, own_ref, loc_sem)
                    keep.start()
                    keep.wait()
                    ship(t).wait_send()
                    send(t - 1).wait_send()
                    for q in range(N_CHIPS - 1):
                        send(q).wait_recv()


def _scatter_scratch(rows, cols):
    return [pltpu.VMEM((rows, cols), F32), pltpu.VMEM((2, rows, cols), BF16), pltpu.VMEM((rows, cols), BF16),
            pltpu.SemaphoreType.DMA((N_CHIPS,)), pltpu.SemaphoreType.DMA((N_CHIPS,)),
            pltpu.SemaphoreType.DMA((N_CHIPS - 1,)), pltpu.SemaphoreType.DMA((N_CHIPS - 1,)), pltpu.SemaphoreType.DMA(())]


def _scatter_out(rows, cols):
    return (jax.ShapeDtypeStruct((rows, cols), F32), jax.ShapeDtypeStruct((N_CHIPS, rows, cols), BF16),
            jax.ShapeDtypeStruct((N_CHIPS - 1, rows, cols), BF16))


def _dwin_scatter(h, dzh, dza):
    s, d = h.shape
    e = dzh.shape[2]
    ts = _tile(s, 1024)
    ns = s // ts

    def body(order_ref, dzh_ref, dza_ref, h_ref, own_ref, ra_ref, rb_ref, acc, stage, tmp, *sems):
        k, step = pl.program_id(0), pl.program_id(1)

        @pl.when(step == 0)
        def _():
            acc[...] = jnp.zeros_like(acc)

        def add(dz):
            acc[...] += _dot_tn(h_ref[...], dz)

        _dz_pick(order_ref[k], dzh_ref, dza_ref, add)
        _scatter_block(k, acc, stage, tmp, own_ref, ra_ref, rb_ref, *sems, step, ns)

    def dz_spec(lo):
        return pl.BlockSpec((None, ts, e), lambda k, st, o: (jnp.clip(o[k] - lo, 0, 3), st, 0))

    grid_spec = pltpu.PrefetchScalarGridSpec(
        num_scalar_prefetch=1, grid=(N_DEV, ns),
        in_specs=[dz_spec(0), dz_spec(4), pl.BlockSpec((ts, d), lambda k, st, o: (st, 0))],
        out_specs=(ANY, ANY, ANY), scratch_shapes=_scatter_scratch(d, e))
    own, _, rb = pl.pallas_call(
        body, name="dwin_scatter", grid_spec=grid_spec, out_shape=_scatter_out(d, e),
        compiler_params=_params("arbitrary", "arbitrary"))(_block_order(SCATTER_MASKS), dzh, dza, h)
    return own, rb


def _dwout_scatter(y_h, y_a, dxb):
    s, e = y_h.shape
    d = dxb.shape[1]
    r = 2 * e // N_DEV
    pairs = e // (2 * r)
    ts = _tile(s, 2048)
    ns = s // ts
    chip_masks = SCATTER_MASKS[1::2]
    passes = ((0, 1), (2,), (3,))
    slots = max(len(chips) for chips in passes)
    slot_chip = [chips[min(u, len(chips) - 1)] for chips in passes for u in range(slots)]

    def body(pair_ref, yh0_ref, ya0_ref, yh1_ref, ya1_ref, dx_ref, own_ref, ra_ref, rb_ref, acc, keep_buf, ship_buf,
             send_buf, tmp, sa_send, sa_recv, sb_send, sb_recv, loc_sem):
        p, step = pl.program_id(0), pl.program_id(1)
        x, y, c = _position()

        @pl.when(step == 0)
        def _():
            acc[...] = jnp.zeros_like(acc)

        for u, (yh_ref, ya_ref) in enumerate(((yh0_ref, ya0_ref), (yh1_ref, ya1_ref))):
            rows = slice(u * 2 * r, (u + 1) * 2 * r)
            used = functools.reduce(jnp.logical_or, [p == pp for pp, chips in enumerate(passes) if u < len(chips)])

            @pl.when(used & (pair_ref[slots * p + u] < pairs))
            def _():
                acc[rows, :] += _dot_tn(yh_ref[...], dx_ref[...])

            @pl.when(used & (pair_ref[slots * p + u] >= pairs))
            def _():
                acc[rows, :] += _dot_tn(ya_ref[...], dx_ref[...])

        def block_rows(u, core):
            return pl.ds(pl.multiple_of(u * 2 * r + core * r, r), r)

        slot_of = {q: u for chips in passes for u, q in enumerate(chips)}

        def ship(q):
            return pltpu.make_async_remote_copy(
                src_ref=ship_buf.at[slot_of[q]], dst_ref=ra_ref.at[q], send_sem=sa_send.at[q], recv_sem=sa_recv.at[q],
                device_id=(x, y, 1 - c), device_id_type=MESH)

        def send(q):
            return pltpu.make_async_remote_copy(
                src_ref=send_buf.at[slot_of[q]], dst_ref=rb_ref.at[q], send_sem=sb_send.at[q], recv_sem=sb_recv.at[q],
                device_id=_xor_peer(x, y, c, chip_masks[q]), device_id_type=MESH)

        def sibling_share(q):
            ship(q).wait_recv()
            fetch = pltpu.make_async_copy(ra_ref.at[q], tmp, loc_sem)
            fetch.start()
            fetch.wait()
            return tmp[...].astype(F32)

        shipped, sent = {}, {}
        for pp, chips in enumerate(passes):
            @pl.when((step == ns - 1) & (p == pp))
            def _():
                for u, q in enumerate(chips):
                    if u in shipped:
                        ship(shipped.pop(u)).wait_send()
                    ship_buf[u] = acc[block_rows(u, 1 - c), :].astype(BF16)
                    ship(q).start()
                    shipped[u] = q
                for u, q in enumerate(chips):
                    total = acc[block_rows(u, c), :] + sibling_share(q)
                    if q < N_CHIPS - 1:
                        if u in sent:
                            send(sent.pop(u)).wait_send()
                        send_buf[u] = total.astype(BF16)
                        send(q).start()
                        sent[u] = q
                    else:
                        keep_buf[...] = total
                        keep = pltpu.make_async_copy(keep_buf, own_ref, loc_sem)
                        keep.start()
                        keep.wait()
                if pp == len(passes) - 1:
                    for q in shipped.values():
                        ship(q).wait_send()
                    for q in sent.values():
                        send(q).wait_send()
                    for q in range(N_CHIPS - 1):
                        send(q).wait_recv()

    def y_spec(u, lo):
        return pl.BlockSpec((ts, 2 * r), lambda p, st, o: (st, jnp.clip(o[slots * p + u] - lo, 0, pairs - 1)))

    pair_of_chip = _block_order(chip_masks) // 2
    grid_spec = pltpu.PrefetchScalarGridSpec(
        num_scalar_prefetch=1, grid=(len(passes), ns),
        in_specs=[y_spec(0, 0), y_spec(0, pairs), y_spec(1, 0), y_spec(1, pairs),
                  pl.BlockSpec((ts, d), lambda p, st, o: (st, 0))],
        out_specs=(ANY, ANY, ANY),
        scratch_shapes=[pltpu.VMEM((slots * 2 * r, d), F32), pltpu.VMEM((r, d), F32),
                        pltpu.VMEM((slots, r, d), BF16)] + _scatter_scratch(r, d)[1:])
    own, _, rb = pl.pallas_call(
        body, name="dwout_scatter", grid_spec=grid_spec, out_shape=_scatter_out(r, d),
        compiler_params=_params("arbitrary", "arbitrary"))(
            jnp.stack([pair_of_chip[q] for q in slot_chip]), y_h, y_a, y_h, y_a, dxb)
    return own, rb


def _sum_chips_adamw(own, recv, w, m, v):
    r, c = w.shape
    tr = _tile(r, 128)

    def body(own_ref, rc_ref, w_ref, m_ref, v_ref, g_ref, d_ref, mo_ref, vo_ref):
        g = own_ref[...]
        for q in range(N_CHIPS - 1):
            g = g + rc_ref[q].astype(F32)
        g_ref[...] = g
        d_ref[...], mo_ref[...], vo_ref[...] = _adamw(w_ref[...], g, m_ref[...], v_ref[...])

    blk = pl.BlockSpec((tr, c), lambda i: (i, 0))
    shp = jax.ShapeDtypeStruct((r, c), F32)
    return pl.pallas_call(
        body, name="sum_chips_adamw", grid=(r // tr,), out_shape=(shp, shp, shp, shp),
        in_specs=[blk, pl.BlockSpec((N_CHIPS - 1, tr, c), lambda i: (0, i, 0)), blk, blk, blk],
        out_specs=(blk, blk, blk, blk), compiler_params=_params("parallel"))(own, recv, w, m, v)


SMALL_ROWS = 8
ROW_LB = 4
ROW_GN = 6
ROW_LOSS = 7


def _small_allreduce_adamw(part, w, m, v, lb_logits):
    width = part.shape[1]

    def body(p_ref, w_ref, m_ref, v_ref, lb_ref, g_ref, d_ref, mo_ref, vo_ref, buf, send_sems, recv_sems):
        x, y, c = _position()
        me = 4 * x + 2 * y + c
        buf[me] = p_ref[...]
        copies = []
        for k in range(N_DEV - 1):
            bx, by, bc = ((k + 1) >> 2) & 1, ((k + 1) >> 1) & 1, (k + 1) & 1
            peer = (x ^ bx, y ^ by, c ^ bc)
            copies.append(pltpu.make_async_remote_copy(
                src_ref=p_ref, dst_ref=buf.at[me], send_sem=send_sems.at[k], recv_sem=recv_sems.at[k],
                device_id=peer, device_id_type=MESH))
        for cp in copies:
            cp.start()
        for cp in copies:
            cp.wait_recv()
        for cp in copies:
            cp.wait_send()
        tot = buf[0]
        for dev in range(1, N_DEV):
            tot = tot + buf[dev]
        lbv = lb_ref[...]
        lb = _sigmoid(lbv[0:1] - lbv[1:2])
        glb = tot[ROW_LB:ROW_LB + 1] * lb * (1.0 - lb)
        loss = jnp.sum(tot[ROW_LOSS:ROW_LOSS + 1], axis=-1, keepdims=True)
        row = lax.broadcasted_iota(jnp.int32, (SMALL_ROWS, width), 0)
        g = jnp.where(row == ROW_LB, glb, jnp.where(row == ROW_LB + 1, -glb, tot))
        g = jnp.where(row == ROW_LOSS, loss, g)
        g_ref[...] = g
        d_ref[...], mo_ref[...], vo_ref[...] = _adamw(w_ref[...], g, m_ref[...], v_ref[...])

    vm = pl.BlockSpec(memory_space=pltpu.VMEM)
    shp = jax.ShapeDtypeStruct((SMALL_ROWS, width), F32)
    return pl.pallas_call(
        body, name="small_allreduce_adamw", out_shape=(shp, shp, shp, shp),
        in_specs=[vm] * 5, out_specs=(vm, vm, vm, vm),
        scratch_shapes=[pltpu.VMEM((N_DEV, SMALL_ROWS, width), F32), pltpu.SemaphoreType.DMA((N_DEV - 1,)),
                        pltpu.SemaphoreType.DMA((N_DEV - 1,))],
    )(part, w, m, v, lb_logits)


def _pack_small(norm_gain, final_gain, lb2, gnorm, last_row, width):
    pad = lambda a: jnp.pad(a.reshape(1, -1), ((0, 0), (0, width - a.size)))
    return jnp.concatenate([norm_gain.reshape(2, width), final_gain.reshape(2, width), lb2.reshape(2, width),
                            pad(gnorm), last_row.reshape(1, width)], axis=0)


def _unpack_small(p, d, e, hd):
    return (p[0:2].reshape(1, d), p[2:4].reshape(d), p[4:6].reshape(2, e), p[6:7, :hd].reshape(1, hd))


def kernel(x, norm_gain, w_in, lb_logits, hgrn_gnorm, w_out, final_gain, loss_target, m_norm_gain, m_w_in, m_lb_logits, m_hgrn_gnorm, m_w_out, m_final_gain, v_norm_gain, v_w_in, v_lb_logits, v_hgrn_gnorm, v_w_out, v_final_gain):
    s, d = x.shape[1], x.shape[2]
    e = w_in.shape[2]
    assert d == 2 * e and lb_logits.shape == (2, e) and w_out.shape[1] * N_DEV == 2 * e
    x2d = x.reshape(s, d)
    tgt = loss_target.reshape(s, d)

    h = _rmsnorm_fwd(x2d, norm_gain)
    z, w_in_full, w_out_full = _inproj_gather(h, _cast_bf16(w_in[0]), _cast_bf16(w_out[0]))
    w_out_full = w_out_full.reshape(2 * e, d)
    y_h, states = _hgrn_fwd(z, lb_logits, hgrn_gnorm)
    o_attn, lse, y_a = _attn_fwd(z)
    dx2, dx2b, dy, loss_vec, dfg = _outproj_loss(x2d, y_h, y_a, w_out_full, final_gain.reshape(1, d), tgt)

    own_o, recv_o = _dwout_scatter(y_h, y_a, dx2b)
    dza = _attn_bwd(z, dy, o_attn, lse)
    dzh, dlb, dgn = _hgrn_bwd(z, dy, states, lb_logits, hgrn_gnorm)
    grad_x, dng = _dh_dx(dzh, dza, w_in_full, x2d, norm_gain, dx2)
    g_wo, d_wo, nm_wo, nv_wo = _sum_chips_adamw(own_o, recv_o, w_out[0], m_w_out[0], v_w_out[0])

    width = d // 2
    zero_row = jnp.zeros((1, width), F32)
    loss_row = loss_vec[:, :width] + loss_vec[:, width:]
    part = _pack_small(dng, dfg, jnp.concatenate([dlb, zero_row], axis=0), dgn, loss_row, width)
    pw = _pack_small(norm_gain, final_gain, lb_logits, hgrn_gnorm, zero_row, width)
    pm = _pack_small(m_norm_gain, m_final_gain, m_lb_logits, m_hgrn_gnorm, zero_row, width)
    pv = _pack_small(v_norm_gain, v_final_gain, v_lb_logits, v_hgrn_gnorm, zero_row, width)
    sg, sd, sm, sv = _small_allreduce_adamw(part, pw, pm, pv, lb_logits)
    own_i, recv_i = _dwin_scatter(h, dzh, dza)
    g_wi, d_wi, nm_wi, nv_wi = _sum_chips_adamw(own_i, recv_i, w_in[0], m_w_in[0], v_w_in[0])
    hd = hgrn_gnorm.shape[1]
    g_ng, g_fg, g_lb, g_gn = _unpack_small(sg, d, e, hd)
    d_ng, d_fg, d_lb, d_gn = _unpack_small(sd, d, e, hd)
    m_ng, m_fg, m_lb, m_gn = _unpack_small(sm, d, e, hd)
    v_ng, v_fg, v_lb, v_gn = _unpack_small(sv, d, e, hd)
    loss = sg[ROW_LOSS, 0]

    one = lambda a: a[None]
    return (loss, grad_x.reshape(1, s, d), g_ng, one(g_wi), g_lb, g_gn, one(g_wo), g_fg,
            d_ng, one(d_wi), d_lb, d_gn, one(d_wo), d_fg,
            m_ng, one(nm_wi), m_lb, m_gn, one(nm_wo), m_fg,
            v_ng, one(nv_wi), v_lb, v_gn, one(nv_wo), v_fg)
```

```python
import functools
import math

import jax
import jax.numpy as jnp
from jax import lax
from jax.experimental import pallas as pl
from jax.experimental.pallas import tpu as pltpu

NORM_EPS = 1e-6
HGRN_HEAD = 128
HGRN_CHUNK = 64
ATTN_HEAD = 64
ATTN_BAND = 128
DILATIONS = (1, 4, 16)
N_SPLITS = 8
N_DEV = 8
ADAM_LR = 0.001
ADAM_B1 = 0.9
ADAM_B2 = 0.999
ADAM_EPS = 1e-08
ADAM_WD = 0.01
ADAM_STEP = 10
LANES = 128
MESH = pl.DeviceIdType.MESH
F32 = jnp.float32
BF16 = jnp.bfloat16
NEG_BIG = -1e30
VMEM_LIMIT = 56 * 1024 * 1024
OUTPROJ_VMEM_LIMIT = 63 * 1024 * 1024
DHDX_VMEM_LIMIT = 60 * 1024 * 1024

ANY = pl.BlockSpec(memory_space=pl.ANY)


def _params(*sem):
    return pltpu.CompilerParams(dimension_semantics=sem, vmem_limit_bytes=VMEM_LIMIT)


def _tile(n, pref):
    t = min(n, pref)
    assert n % t == 0, (n, pref)
    return t


def _dot(a, b, precision=None):
    return jnp.dot(a, b, preferred_element_type=F32, precision=precision)


def _dot_nt(a, b):
    return lax.dot_general(a, b, (((1,), (1,)), ((), ())), preferred_element_type=F32)


def _dot_tn(a, b):
    return lax.dot_general(a, b, (((0,), (0,)), ((), ())), preferred_element_type=F32)


def _sigmoid(x):
    return 0.5 * jnp.tanh(0.5 * x) + 0.5


def _dsilu(x, s):
    return s * (1.0 + x * (1.0 - s))


def _adamw(w, g, m, v):
    m = ADAM_B1 * m + (1.0 - ADAM_B1) * g
    v = ADAM_B2 * v + (1.0 - ADAM_B2) * (g * g)
    m_hat = m / (1.0 - ADAM_B1 ** ADAM_STEP)
    v_hat = v / (1.0 - ADAM_B2 ** ADAM_STEP)
    delta = -ADAM_LR * (m_hat / (jnp.sqrt(v_hat) + ADAM_EPS) + ADAM_WD * w)
    return delta, m, v


def _cast_bf16(a):
    r, c = a.shape
    tr = _tile(r, 256)

    def body(a_ref, o_ref):
        o_ref[...] = a_ref[...].astype(BF16)

    return pl.pallas_call(
        body, name="cast_bf16", grid=(r // tr,), out_shape=jax.ShapeDtypeStruct((r, c), BF16),
        in_specs=[pl.BlockSpec((tr, c), lambda i: (i, 0))], out_specs=pl.BlockSpec((tr, c), lambda i: (i, 0)),
        compiler_params=_params("parallel"))(a)


def _rmsnorm_fwd(x, gain):
    s, d = x.shape
    tm = _tile(s, 512)

    def body(x_ref, g_ref, h_ref):
        xv = x_ref[...]
        r = lax.rsqrt(jnp.mean(xv * xv, axis=-1, keepdims=True) + NORM_EPS)
        h_ref[...] = (xv * r * g_ref[...]).astype(BF16)

    return pl.pallas_call(
        body, name="rmsnorm_fwd", grid=(s // tm,), out_shape=jax.ShapeDtypeStruct((s, d), BF16),
        in_specs=[pl.BlockSpec((tm, d), lambda i: (i, 0)), pl.BlockSpec((1, d), lambda i: (0, 0))],
        out_specs=pl.BlockSpec((tm, d), lambda i: (i, 0)), compiler_params=_params("parallel"))(x, gain)


HGRN_BLOCK = 2048
TRI_ROWS = 64


def _chunk_masks():
    tb = TRI_ROWS
    row = lax.broadcasted_iota(jnp.int32, (tb, tb), 0)
    col = lax.broadcasted_iota(jnp.int32, (tb, tb), 1)
    same = (row // HGRN_CHUNK) == (col // HGRN_CHUNK)
    lower = jnp.where(same & (col <= row), 1.0, 0.0).astype(BF16)
    upper = jnp.where(same & (col >= row), 1.0, 0.0).astype(BF16)
    return lower, upper


def _split3(a):
    hi = a.astype(BF16).astype(F32)
    mid = (a - hi).astype(BF16).astype(F32)
    lo = (a - hi - mid).astype(BF16).astype(F32)
    return hi, mid, lo


def _tri_dot(tri, x):
    hi, mid, lo = (p.astype(BF16) for p in _split3(x))
    outs = []
    for r in range(0, x.shape[0], TRI_ROWS):
        sl = slice(r, r + TRI_ROWS)
        outs.append(_dot(tri, hi[sl]) + _dot(tri, mid[sl]) + _dot(tri, lo[sl]))
    return outs[0] if len(outs) == 1 else jnp.concatenate(outs, axis=0)


def _hgrn_gates(qp, fp, lbv):
    lb = _sigmoid(lbv[0:1] - lbv[1:2])
    sq = _sigmoid(qp)
    q = qp * sq
    sg = _sigmoid(fp)
    f = lb + (1.0 - lb) * sg
    k = 1.0 - f
    return lb, sq, q, sg, f, k


def _hgrn_fwd(z, lb_logits, gnorm):
    s = z.shape[0]
    e = z.shape[1] // N_SPLITS
    nh = e // HGRN_HEAD
    tb = _tile(s, HGRN_BLOCK)
    nc = tb // HGRN_CHUNK
    nb = s // tb
    C = HGRN_CHUNK

    def body(q_ref, f_ref, i_ref, g_ref, lb_ref, gn_ref, y_ref, st_ref, state, o_scr):
        @pl.when(pl.program_id(1) == 0)
        def _():
            state[...] = jnp.zeros_like(state)

        lb, sq, q, sg, f, k = _hgrn_gates(q_ref[...], f_ref[...], lb_ref[...])
        lower, _ = _chunk_masks()
        b = _tri_dot(lower, jnp.log(f))
        b3 = b.reshape(nc, C, HGRN_HEAD)
        bc = b3[:, C - 1:C, :]
        qt = (q * jnp.exp(b)).astype(BF16)
        kt = (k * jnp.exp(-b)).astype(BF16)
        ke = (k.reshape(nc, C, HGRN_HEAD) * jnp.exp(bc - b3)).reshape(tb, HGRN_HEAD).astype(BF16)
        v = i_ref[...].astype(BF16)
        tri = lax.broadcasted_iota(jnp.int32, (C, C), 1) <= lax.broadcasted_iota(jnp.int32, (C, C), 0)
        sls = [slice(c * C, (c + 1) * C) for c in range(nc)]
        kv = [_dot_tn(v[sl], ke[sl]) for sl in sls]
        a = [jnp.where(tri, _dot_nt(qt[sl], kt[sl]), 0.0).astype(BF16) for sl in sls]
        st = state[...]
        sts = []
        for c in range(nc):
            sts.append(st)
            st_ref[c] = st
            st = st * jnp.exp(bc[c]) + kv[c]
        state[...] = st
        for c, sl in enumerate(sls):
            o_scr[sl, :] = _dot(a[c], v[sl]) + _dot_nt(qt[sl], sts[c].astype(BF16))
        o = o_scr[...]
        rms = lax.rsqrt(jnp.mean(o * o, axis=-1, keepdims=True) + NORM_EPS)
        gp = g_ref[...]
        y_ref[...] = (o * rms * gn_ref[...] * (gp * _sigmoid(gp))).astype(BF16)

    col = lambda kk: (lambda h, n: (n, kk * nh + h))
    return pl.pallas_call(
        body, name="hgrn_fwd", grid=(nh, nb),
        out_shape=(jax.ShapeDtypeStruct((s, e), BF16),
                   jax.ShapeDtypeStruct((nh, s // C, HGRN_HEAD, HGRN_HEAD), F32)),
        in_specs=[pl.BlockSpec((tb, HGRN_HEAD), col(0)), pl.BlockSpec((tb, HGRN_HEAD), col(1)),
                  pl.BlockSpec((tb, HGRN_HEAD), col(2)), pl.BlockSpec((tb, HGRN_HEAD), col(3)),
                  pl.BlockSpec((2, HGRN_HEAD), lambda h, n: (0, h)), pl.BlockSpec((1, HGRN_HEAD), lambda h, n: (0, 0))],
        out_specs=(pl.BlockSpec((tb, HGRN_HEAD), lambda h, n: (n, h)),
                   pl.BlockSpec((None, nc, HGRN_HEAD, HGRN_HEAD), lambda h, n: (h, n, 0, 0))),
        scratch_shapes=[pltpu.VMEM((HGRN_HEAD, HGRN_HEAD), F32), pltpu.VMEM((tb, HGRN_HEAD), F32)],
        compiler_params=_params("parallel", "arbitrary"))(z, z, z, z, lb_logits, gnorm)


def _hgrn_bwd(z, dy, states, lb_logits, gnorm):
    s = z.shape[0]
    e = z.shape[1] // N_SPLITS
    nh = e // HGRN_HEAD
    tb = _tile(s, HGRN_BLOCK)
    nc = tb // HGRN_CHUNK
    nb = s // tb
    C = HGRN_CHUNK
    H = HGRN_HEAD

    def body(q_ref, f_ref, i_ref, g_ref, dy_ref, st_ref, lb_ref, gn_ref, dz_ref, dlb_ref, dgn_ref,
             gstate, o_scr, dq_scr, dk_scr, dv_scr, e_scr):
        first = (pl.program_id(0) == 0) & (pl.program_id(1) == 0)

        @pl.when(first)
        def _():
            dgn_ref[...] = jnp.zeros_like(dgn_ref)

        @pl.when(pl.program_id(1) == 0)
        def _():
            gstate[...] = jnp.zeros_like(gstate)
            dlb_ref[...] = jnp.zeros_like(dlb_ref)

        qp = q_ref[...]
        lb, sq, q, sg, f, k = _hgrn_gates(qp, f_ref[...], lb_ref[...])
        lower, upper = _chunk_masks()
        b = _tri_dot(lower, jnp.log(f))
        b3 = b.reshape(nc, C, H)
        bc = b3[:, C - 1:C, :]
        eb = jnp.exp(b)
        enb = jnp.exp(-b)
        eend = jnp.exp(bc - b3).reshape(tb, H)
        qt = (q * eb).astype(BF16)
        kt = (k * enb).astype(BF16)
        ke = (k * eend).astype(BF16)
        v = i_ref[...].astype(BF16)
        tri = lax.broadcasted_iota(jnp.int32, (C, C), 1) <= lax.broadcasted_iota(jnp.int32, (C, C), 0)
        sls = [slice(c * C, (c + 1) * C) for c in range(nc)]
        a = [jnp.where(tri, _dot_nt(qt[sl], kt[sl]), 0.0).astype(BF16) for sl in sls]
        for c, sl in enumerate(sls):
            o_scr[sl, :] = _dot(a[c], v[sl]) + _dot_nt(qt[sl], st_ref[c].astype(BF16))
        o = o_scr[...]
        rms = lax.rsqrt(jnp.mean(o * o, axis=-1, keepdims=True) + NORM_EPS)
        on = o * rms
        gn = gn_ref[...]
        gp = g_ref[...]
        sgg = _sigmoid(gp)
        dyv = dy_ref[...]
        d_on = dyv * (gp * sgg)
        dz_ref[3] = (dyv * on * gn * _dsilu(gp, sgg)).astype(BF16)
        dgn_ref[...] += jnp.sum(d_on * on, axis=0, keepdims=True)
        u = d_on * gn
        do = (rms * (u - on * jnp.mean(u * on, axis=-1, keepdims=True))).astype(BF16)
        gup = [_dot_tn(do[sl], qt[sl]) for sl in sls]
        da = [jnp.where(tri, _dot_nt(do[sl], v[sl]), 0.0).astype(BF16) for sl in sls]
        gt = gstate[...]
        gts = [None] * nc
        for c in reversed(range(nc)):
            gts[c] = gt
            gt = gt * jnp.exp(bc[c]) + gup[c]
        gstate[...] = gt
        for c, sl in enumerate(sls):
            stp = st_ref[c]
            gtb = gts[c].astype(BF16)
            dqt = _dot(da[c], kt[sl]) + _dot(do[sl], stp.astype(BF16))
            dkt = _dot_tn(da[c], qt[sl])
            dks = _dot(v[sl], gtb) * eend[sl]
            dv_scr[sl, :] = _dot_tn(a[c], do[sl]) + _dot_nt(ke[sl], gtb)
            dq_scr[sl, :] = dqt * eb[sl]
            dk_scr[sl, :] = dkt * enb[sl] + dks
            ech = (jnp.sum(k[sl] * dks, axis=0, keepdims=True)
                   + jnp.sum(gts[c] * jnp.exp(bc[c]) * stp, axis=0, keepdims=True))
            e_scr[sl, :] = jnp.broadcast_to(ech, (C, H))
        dq = dq_scr[...]
        dk = dk_scr[...]
        dlf = _tri_dot(upper, q * dq - k * dk) + e_scr[...]
        dft = dlf / f - dk
        dz_ref[0] = (dq * _dsilu(qp, sq)).astype(BF16)
        dz_ref[1] = (dft * (1.0 - lb) * sg * (1.0 - sg)).astype(BF16)
        dz_ref[2] = dv_scr[...].astype(BF16)
        dlb_ref[...] += jnp.sum(dft * (1.0 - sg), axis=0, keepdims=True)

    col = lambda kk: (lambda h, n: (nb - 1 - n, kk * nh + h))
    return pl.pallas_call(
        body, name="hgrn_bwd", grid=(nh, nb),
        out_shape=(jax.ShapeDtypeStruct((4, s, e), BF16), jax.ShapeDtypeStruct((1, e), F32),
                   jax.ShapeDtypeStruct((1, H), F32)),
        in_specs=[pl.BlockSpec((tb, H), col(0)), pl.BlockSpec((tb, H), col(1)),
                  pl.BlockSpec((tb, H), col(2)), pl.BlockSpec((tb, H), col(3)),
                  pl.BlockSpec((tb, H), lambda h, n: (nb - 1 - n, h)),
                  pl.BlockSpec((None, nc, H, H), lambda h, n: (h, nb - 1 - n, 0, 0)),
                  pl.BlockSpec((2, H), lambda h, n: (0, h)), pl.BlockSpec((1, H), lambda h, n: (0, 0))],
        out_specs=(pl.BlockSpec((4, tb, H), lambda h, n: (0, nb - 1 - n, h)),
                   pl.BlockSpec((1, H), lambda h, n: (0, h)), pl.BlockSpec((1, H), lambda h, n: (0, 0))),
        scratch_shapes=[pltpu.VMEM((H, H), F32)] + [pltpu.VMEM((tb, H), F32)] * 5,
        compiler_params=_params("arbitrary", "arbitrary"))(z, z, z, z, dy, states, lb_logits, gnorm)


ATTN_T = 16 * ATTN_BAND
SCALE = ATTN_HEAD ** -0.5


def _slope(pair, hh, nheads):
    head = (2 * pair + hh + 1).astype(F32)
    return jnp.exp(jnp.full((1, 1), -8.0 / nheads * math.log(2.0), F32) * head)


def _fill_bias(bias, pair, nheads, delta, edge_ok):
    band = (delta >= 0) & (delta <= ATTN_BAND)
    dist = delta.astype(F32)
    for pi, dil in enumerate(DILATIONS):
        for hh in range(2):
            full = jnp.where(band, -(_slope(pair, hh, nheads) * float(dil)) * dist, NEG_BIG)
            bias[(pi * 2 + hh) * 2] = full
            bias[(pi * 2 + hh) * 2 + 1] = jnp.where(edge_ok, full, NEG_BIG)


def _rows(start, size, stride):
    if stride == 1:
        return pl.ds(pl.multiple_of(start, ATTN_BAND), size)
    return pl.ds(start, size, stride=stride)


def _head_lanes(rows, hh):
    return (lax.broadcasted_iota(jnp.int32, (rows, LANES), 1) // ATTN_HEAD) == hh


def _attn_fwd(z):
    s = z.shape[0]
    e = z.shape[1] // N_SPLITS
    npair = e // LANES
    T = ATTN_T
    assert s % T == 0
    nsb = s // T
    W = ATTN_BAND
    nt = T // W
    HD = ATTN_HEAD
    chunk = 256

    nsteps = npair * nsb
    assert nt % 2 == 0 and T // chunk == nt // 2

    def body(q_ref, kp_ref, kc_ref, vp_ref, vc_ref, g_ref, o_ref, l_ref, y_ref, qa, kbuf, va, bias, *sets):
        t = pl.program_id(0)
        sb = jnp.minimum(t, nsteps - 1) % nsb
        pair = jnp.minimum(t, nsteps - 1) // nsb
        sets = (sets[0:3], sets[3:6])

        def merge_chunk(i, done):
            accs, ms, lsw = done
            rows = pl.ds(pl.multiple_of(i * chunk, chunk), chunk)
            m1, m2, m3 = ms[0, rows, :], ms[1, rows, :], ms[2, rows, :]
            mx = jnp.maximum(jnp.maximum(m1, m2), m3)
            w1, w2, w3 = jnp.exp(m1 - mx), jnp.exp(m2 - mx), jnp.exp(m3 - mx)
            unswap = lambda a: pltpu.roll(a, ATTN_HEAD, 1)
            den = w1 * unswap(lsw[0, rows, :]) + w2 * unswap(lsw[1, rows, :]) + w3 * unswap(lsw[2, rows, :])
            o = (w1 * accs[0, rows, :] + w2 * accs[1, rows, :] + w3 * accs[2, rows, :]) / den
            o_ref[rows, :] = o
            l_ref[rows, :] = mx + jnp.log(den)
            gp = g_ref[rows, :]
            y_ref[rows, :] = (o * (gp * _sigmoid(gp))).astype(BF16)

        @pl.when(t == 0)
        def _():
            accs, ms, lsw = sets[1]
            accs[...] = jnp.zeros_like(accs)
            ms[...] = jnp.zeros_like(ms)
            lsw[...] = jnp.ones_like(lsw)

        @pl.when(t == nsteps)
        def _():
            def drain(i, carry):
                merge_chunk(i, sets[(nsteps - 1) % 2])
                return carry

            lax.fori_loop(0, T // chunk, drain, 0)

        def compute(cur, done):
            accs, ms, lsw = cur
            def stage(i, carry):
                rows = pl.ds(pl.multiple_of(i * chunk, chunk), chunk)
                upper = pl.ds(pl.multiple_of(T + i * chunk, chunk), chunk)
                kbuf[upper, :] = kc_ref[rows, :]
                for hh in range(2):
                    mine = _head_lanes(chunk, hh)
                    qa[hh, rows, :] = jnp.where(mine, q_ref[rows, :] * SCALE, 0.0)
                    va[hh, upper, :] = jnp.where(mine, vc_ref[rows, :], 1.0)
                return carry

            lax.fori_loop(0, T // chunk, stage, 0)

            @pl.when(sb == 0)
            def _():
                def stage_prev(i, carry):
                    rows = pl.ds(pl.multiple_of(i * chunk, chunk), chunk)
                    kbuf[rows, :] = kp_ref[rows, :]
                    for hh in range(2):
                        va[hh, rows, :] = jnp.where(_head_lanes(chunk, hh), vp_ref[rows, :], 1.0)
                    return carry

                lax.fori_loop(0, T // chunk, stage_prev, 0)

            @pl.when(sb == 0)
            def _():
                qi = lax.broadcasted_iota(jnp.int32, (W, 2 * W), 0)
                kj = lax.broadcasted_iota(jnp.int32, (W, 2 * W), 1)
                _fill_bias(bias, pair, 2 * npair, W + qi - kj, kj >= W)

            def tile(tau):
                first = _head_lanes(W, 0)
                rows, scores = [], []
                for pi, dil in enumerate(DILATIONS):
                    r = tau % dil
                    ub = tau // dil
                    qrows = _rows(r + dil * W * ub, W, dil)
                    krows = _rows(T + dil * W * (ub - 1) + r, 2 * W, dil)
                    var = jnp.where((sb == 0) & (ub == 0), 1, 0)
                    kt = kbuf[krows, :].astype(BF16)
                    rows.append((qrows, krows))
                    scores.append([_dot_nt(qa[hh, qrows, :].astype(BF16), kt) + bias[(pi * 2 + hh) * 2 + var]
                                   for hh in range(2)])
                maxes = [[jnp.max(sc, axis=-1, keepdims=True) for sc in pair_] for pair_ in scores]
                probs = [[jnp.exp(sc - m).astype(BF16) for sc, m in zip(ps, pm)] for ps, pm in zip(scores, maxes)]
                for pi, (qrows, krows) in enumerate(rows):
                    outs = [_dot(probs[pi][hh], va[hh, krows, :].astype(BF16)) for hh in range(2)]
                    accs[pi, qrows, :] = jnp.where(first, outs[0], outs[1])
                    lsw[pi, qrows, :] = jnp.where(first, outs[1], outs[0])
                    ms[pi, qrows, :] = jnp.where(first, maxes[pi][0], maxes[pi][1])

            def two_tiles(i, carry):
                tile(2 * i)
                tile(2 * i + 1)
                merge_chunk(i, done)
                return carry

            lax.fori_loop(0, nt // 2, two_tiles, 0)

            def move_down(i, carry):
                rows = pl.ds(pl.multiple_of(i * chunk, chunk), chunk)
                upper = pl.ds(pl.multiple_of(T + i * chunk, chunk), chunk)
                kbuf[rows, :] = kbuf[upper, :]
                for hh in range(2):
                    va[hh, rows, :] = va[hh, upper, :]
                return carry

            lax.fori_loop(0, T // chunk, move_down, 0)

        for parity in range(2):
            @pl.when((t < nsteps) & (t % 2 == parity))
            def _():
                compute(sets[parity], sets[1 - parity])

    step_of = lambda t: jnp.minimum(t, nsteps - 1)
    lag_of = lambda t: jnp.maximum(t - 1, 0)
    cur = lambda split: (lambda t: (step_of(t) % nsb, split * npair + step_of(t) // nsb))
    prev = lambda split: (lambda t: (0, split * npair + step_of(t) // nsb))
    blk = lambda index: pl.BlockSpec((T, LANES), index)
    out = blk(lambda t: (lag_of(t) % nsb, lag_of(t) // nsb))
    gate = blk(lambda t: (lag_of(t) % nsb, 7 * npair + lag_of(t) // nsb))
    buf = lambda rows: pltpu.VMEM((rows, LANES), F32)
    return pl.pallas_call(
        body, name="attn_fwd", grid=(nsteps + 1,),
        out_shape=(jax.ShapeDtypeStruct((s, e), F32), jax.ShapeDtypeStruct((s, e), F32), jax.ShapeDtypeStruct((s, e), BF16)),
        in_specs=[blk(cur(4)), blk(prev(5)), blk(cur(5)), blk(prev(6)), blk(cur(6)), gate],
        out_specs=(out, out, out),
        scratch_shapes=[pltpu.VMEM((2, T, LANES), F32), buf(2 * T), pltpu.VMEM((2, 2 * T, LANES), F32),
                        pltpu.VMEM((12, W, 2 * W), F32)] + [pltpu.VMEM((3, T, LANES), F32)] * 6,
        compiler_params=_params("arbitrary"))(z, z, z, z, z, z)


def _outproj_loss(x, y_h, y_a, w_out_full, final_gain, target):
    s, d = x.shape
    e = y_h.shape[1]
    tm = _tile(s, 512)
    sub = _tile(tm, 256)

    def body(x_ref, yh_ref, ya_ref, w_ref, g_ref, t_ref, dx_ref, dxb_ref, dy_ref, loss_ref, dg_ref):
        @pl.when(pl.program_id(0) == 0)
        def _():
            loss_ref[...] = jnp.zeros_like(loss_ref)
            dg_ref[...] = jnp.zeros_like(dg_ref)

        w = w_ref[...]
        g = g_ref[...]
        parts = [slice(r0, r0 + sub) for r0 in range(0, tm, sub)]
        x2s = [x_ref[rows, :] + _dot(yh_ref[rows, :], w[0:e]) + _dot(ya_ref[rows, :], w[e:2 * e]) for rows in parts]
        for rows, x2 in zip(parts, x2s):
            r = lax.rsqrt(jnp.mean(x2 * x2, axis=-1, keepdims=True) + NORM_EPS)
            xn = x2 * r
            err = xn * g - t_ref[rows, :]
            loss_ref[...] += jnp.sum(err * err, axis=0, keepdims=True) * (0.5 / d)
            dyo = err * (1.0 / d)
            dg_ref[...] += jnp.sum(dyo * xn, axis=0, keepdims=True)
            u = dyo * g
            dx2 = r * (u - xn * jnp.mean(u * xn, axis=-1, keepdims=True))
            dx_ref[rows, :] = dx2
            dxb = dx2.astype(BF16)
            dxb_ref[rows, :] = dxb
            dy_ref[rows, :] = _dot_nt(dxb, w)

    row = pl.BlockSpec((tm, d), lambda i: (i, 0))
    half = pl.BlockSpec((tm, e), lambda i: (i, 0))
    vec = pl.BlockSpec((1, d), lambda i: (0, 0))
    whole = pl.BlockSpec((2 * e, d), lambda i: (0, 0), pipeline_mode=pl.Buffered(1))
    return pl.pallas_call(
        body, name="outproj_loss", grid=(s // tm,),
        out_shape=(jax.ShapeDtypeStruct((s, d), F32), jax.ShapeDtypeStruct((s, d), BF16),
                   jax.ShapeDtypeStruct((s, 2 * e), F32), jax.ShapeDtypeStruct((1, d), F32),
                   jax.ShapeDtypeStruct((1, d), F32)),
        in_specs=[row, half, half, whole, vec, row],
        out_specs=(row, row, pl.BlockSpec((tm, 2 * e), lambda i: (i, 0)), vec, vec),
        compiler_params=pltpu.CompilerParams(dimension_semantics=("arbitrary",), vmem_limit_bytes=OUTPROJ_VMEM_LIMIT),
    )(x, y_h, y_a, w_out_full, final_gain, target)


def _attn_bwd(z, dy, o, lse):
    s, e = o.shape
    npair = e // LANES
    T = ATTN_T
    assert s % T == 0
    nsb = s // T
    W = ATTN_BAND
    nt = T // W
    HD = ATTN_HEAD
    chunk = 256

    def body(k_ref, v_ref, qc_ref, qn_ref, dyc_ref, dyn_ref, gc_ref, gn_ref, oc_ref, on_ref, lc_ref, ln_ref,
             dz_ref, qa, doa, ka, va, dqacc, dkacc, dvacc, bias):
        sb = pl.program_id(1)
        def stage_queries(half, q_r, dy_r, g_r, o_r, l_r):
            def stage(i, carry):
                rows = pl.ds(pl.multiple_of(i * chunk, chunk), chunk)
                dst = pl.ds(pl.multiple_of(half * T + i * chunk, chunk), chunk)
                lane = lax.broadcasted_iota(jnp.int32, (chunk, LANES), 1)
                gp = g_r[rows, :]
                dov = dy_r[rows, :] * (gp * _sigmoid(gp))
                qv = q_r[rows, :] * SCALE
                same_head = (lax.broadcasted_iota(jnp.int32, (LANES, LANES), 0) // HD
                             == lax.broadcasted_iota(jnp.int32, (LANES, LANES), 1) // HD)
                ones = jnp.where(same_head, 1.0, 0.0).astype(BF16)
                hi, mid, lo = (p.astype(BF16) for p in _split3(dov * o_r[rows, :]))
                delta = _dot(hi, ones) + _dot(mid, ones) + _dot(lo, ones)
                swap = lambda a: pltpu.roll(a, HD, 1)
                lse_parts = [swap(p) for p in _split3(l_r[rows, :])]
                dl_parts = [swap(p) for p in _split3(delta)]
                for hh in range(2):
                    mine = _head_lanes(chunk, hh)
                    spare = (1 - hh) * HD
                    qh = jnp.where(mine, qv, 0.0)
                    dh = jnp.where(mine, dov, 0.0)
                    for j in range(3):
                        qh = jnp.where(lane == spare + j, lse_parts[j], qh)
                        dh = jnp.where(lane == spare + j, dl_parts[j], dh)
                    qa[hh, dst, :] = qh
                    doa[hh, dst, :] = dh
                return carry

            lax.fori_loop(0, T // chunk, stage, 0)

        @pl.when(sb == 0)
        def _():
            stage_queries(0, qc_ref, dyc_ref, gc_ref, oc_ref, lc_ref)

        stage_queries(1, qn_ref, dyn_ref, gn_ref, on_ref, ln_ref)

        def stage_keys(i, carry):
            rows = pl.ds(pl.multiple_of(i * chunk, chunk), chunk)
            lane = lax.broadcasted_iota(jnp.int32, (chunk, LANES), 1)
            for hh in range(2):
                spare = (1 - hh) * HD
                minus = (lane >= spare) & (lane < spare + 3)
                ka[hh, rows, :] = jnp.where(minus, -1.0, k_ref[rows, :])
                va[hh, rows, :] = jnp.where(minus, -1.0, v_ref[rows, :])
            gp = gc_ref[rows, :]
            dz_ref[3, rows, :] = (dyc_ref[rows, :] * oc_ref[rows, :] * _dsilu(gp, _sigmoid(gp))).astype(BF16)
            return carry

        lax.fori_loop(0, T // chunk, stage_keys, 0)

        @pl.when(sb == 0)
        def _():
            dqacc[0:T, :] = jnp.zeros((T, LANES), F32)

        dqacc[T:, :] = jnp.zeros((T, LANES), F32)
        dkacc[...] = jnp.zeros_like(dkacc)
        dvacc[...] = jnp.zeros_like(dvacc)
        @pl.when(sb == 0)
        def _():
            qi = lax.broadcasted_iota(jnp.int32, (2 * W, W), 0)
            kj = lax.broadcasted_iota(jnp.int32, (2 * W, W), 1)
            _fill_bias(bias, pl.program_id(0), 2 * npair, qi - kj, qi < W)

        def tile(tau, carry):
            def scores(step, pi):
                dil = DILATIONS[pi]
                r = step % dil
                ub = step // dil
                start = r + dil * W * ub
                krows = _rows(start, W, dil)
                qrows = _rows(start, 2 * W, dil)
                var = jnp.where((sb == nsb - 1) & (ub == nt // dil - 1), 1, 0)
                unit = dict(krows=krows, qrows=qrows, ops=[], sc=[], dpd=[])
                for hh in range(2):
                    kt = ka[hh, krows, :].astype(BF16)
                    vt = va[hh, krows, :].astype(BF16)
                    qt = qa[hh, qrows, :].astype(BF16)
                    dt = doa[hh, qrows, :].astype(BF16)
                    unit["ops"].append((kt, qt, dt))
                    unit["sc"].append(_dot_nt(qt, kt) + bias[(pi * 2 + hh) * 2 + var])
                    unit["dpd"].append(_dot_nt(dt, vt))
                return unit

            def elementwise(unit):
                ps = [jnp.exp(s_) for s_ in unit["sc"]]
                unit["ds"] = [(p * d).astype(BF16) for p, d in zip(ps, unit["dpd"])]
                unit["pb"] = [p.astype(BF16) for p in ps]

            def products(unit):
                dvs = [_dot_tn(pb, dt) for pb, (kt, qt, dt) in zip(unit["pb"], unit["ops"])]
                dks = [_dot_tn(ds, qt) for ds, (kt, qt, dt) in zip(unit["ds"], unit["ops"])]
                dqs = [_dot(ds, kt) for ds, (kt, qt, dt) in zip(unit["ds"], unit["ops"])]
                dkacc[unit["krows"], :] += jnp.where(_head_lanes(W, 0), dks[0], dks[1])
                dvacc[unit["krows"], :] += jnp.where(_head_lanes(W, 0), dvs[0], dvs[1])
                dqacc[unit["qrows"], :] += jnp.where(_head_lanes(2 * W, 0), dqs[0], dqs[1]) * SCALE

            order = [(2 * tau + half, pi) for half in range(2) for pi in range(len(DILATIONS))]
            units = [None] * len(order)
            for n in range(len(order) + 2):
                if n < len(order):
                    units[n] = scores(*order[n])
                if 1 <= n <= len(order):
                    elementwise(units[n - 1])
                if n >= 2:
                    products(units[n - 2])
            return carry

        lax.fori_loop(0, nt // 2, tile, 0)

        def flush(i, carry):
            rows = pl.ds(pl.multiple_of(i * chunk, chunk), chunk)
            nxt = pl.ds(pl.multiple_of(T + i * chunk, chunk), chunk)
            dz_ref[0, rows, :] = dqacc[rows, :].astype(BF16)
            dz_ref[1, rows, :] = dkacc[rows, :].astype(BF16)
            dz_ref[2, rows, :] = dvacc[rows, :].astype(BF16)
            dqacc[rows, :] = dqacc[nxt, :]
            for hh in range(2):
                qa[hh, rows, :] = qa[hh, nxt, :]
                doa[hh, rows, :] = doa[hh, nxt, :]
            return carry

        lax.fori_loop(0, T // chunk, flush, 0)

    zc = lambda split: (lambda hp, sb: (sb, split * npair + hp))
    zn = lambda split: (lambda hp, sb: (jnp.minimum(sb + 1, nsb - 1), split * npair + hp))
    ec = lambda off: (lambda hp, sb: (sb, off + hp))
    en = lambda off: (lambda hp, sb: (jnp.minimum(sb + 1, nsb - 1), off + hp))
    z0 = lambda split: (lambda hp, sb: (0, split * npair + hp))
    e0 = lambda off: (lambda hp, sb: (0, off + hp))
    blk = lambda index: pl.BlockSpec((T, LANES), index)
    buf = lambda rows: pltpu.VMEM((rows, LANES), F32)
    return pl.pallas_call(
        body, name="attn_bwd", grid=(npair, nsb), out_shape=jax.ShapeDtypeStruct((4, s, e), BF16),
        in_specs=[blk(zc(5)), blk(zc(6)), blk(z0(4)), blk(zn(4)), blk(ec(npair)), blk(en(npair)),
                  blk(zc(7)), blk(zn(7)), blk(ec(0)), blk(en(0)), blk(e0(0)), blk(en(0))],
        out_specs=pl.BlockSpec((4, T, LANES), lambda hp, sb: (0, sb, hp)),
        scratch_shapes=[pltpu.VMEM((2, 2 * T, LANES), F32), pltpu.VMEM((2, 2 * T, LANES), F32),
                        pltpu.VMEM((2, T, LANES), F32), pltpu.VMEM((2, T, LANES), F32),
                        buf(2 * T), buf(T), buf(T), pltpu.VMEM((12, 2 * W, W), F32)],
        compiler_params=_params("parallel", "arbitrary"))(z, z, z, z, dy, dy, z, z, o, o, lse, lse)


def _dz_specs(tm, e):
    def mk(lo, hi):
        return pl.BlockSpec((None, tm, e), lambda i, k: (jnp.clip(k - lo, 0, hi - lo - 1), i, 0))
    return [mk(0, 4), mk(4, 8)]


def _dz_pick(grp, dzh_ref, dza_ref, fn):
    @pl.when(grp < 4)
    def _():
        fn(dzh_ref[...])

    @pl.when(grp >= 4)
    def _():
        fn(dza_ref[...])


def _dh_dx(dzh, dza, w_full, x, gain, dx2):
    s, d = x.shape
    e = dzh.shape[2]
    tm = _tile(s, 1024)
    ni = s // tm
    chunk = _tile(tm, 128)
    fetch_at = 2

    def body(dzh_ref, dza_ref, w_ref, x_hbm, g_ref, dx2_hbm, gx_hbm, dg_ref, acc, xbuf, dbuf, sems):
        i, k = pl.program_id(0), pl.program_id(1)
        rows_of = lambda tile: pl.ds(pl.multiple_of(tile * tm, tm), tm)
        fetch_x = pltpu.make_async_copy(x_hbm.at[rows_of(i), :], xbuf, sems.at[0])
        fetch_d = pltpu.make_async_copy(dx2_hbm.at[rows_of(i), :], dbuf, sems.at[1])

        def store(tile):
            return pltpu.make_async_copy(xbuf, gx_hbm.at[rows_of(tile), :], sems.at[2])

        @pl.when((i == 0) & (k == 0))
        def _():
            dg_ref[...] = jnp.zeros_like(dg_ref)

        @pl.when((k == fetch_at) & (i > 0))
        def _():
            store(i).wait()

        @pl.when(k == fetch_at)
        def _():
            fetch_x.start()
            fetch_d.start()

        def finish(tile):
            fetch_x.wait()
            fetch_d.wait()
            gain_row = g_ref[...]
            for r0 in range(0, tm, chunk):
                rows = slice(r0, r0 + chunk)
                dh = acc[rows, :]
                xv = xbuf[rows, :]
                r = lax.rsqrt(jnp.mean(xv * xv, axis=-1, keepdims=True) + NORM_EPS)
                xn = xv * r
                u = dh * gain_row
                xbuf[rows, :] = dbuf[rows, :] + r * (u - xn * jnp.mean(u * xn, axis=-1, keepdims=True))
                dg_ref[...] += jnp.sum(dh * xn, axis=0, keepdims=True)
            store(tile).start()

        @pl.when((k == 0) & (i == 0))
        def _():
            acc[...] = _dot_nt(dzh_ref[...], w_ref[...])

        @pl.when((k == 0) & (i > 0))
        def _():
            finish(i - 1)
            acc[...] = _dot_nt(dzh_ref[...], w_ref[...])

        @pl.when(k > 0)
        def _():
            def add(dz):
                acc[...] += _dot_nt(dz, w_ref[...])

            _dz_pick(k, dzh_ref, dza_ref, add)

        @pl.when((k == N_SPLITS - 1) & (i == ni - 1))
        def _():
            finish(i)
            store(i).wait()

    vec = pl.BlockSpec((1, d), lambda i, k: (0, 0))
    return pl.pallas_call(
        body, name="dh_dx", grid=(ni, N_SPLITS),
        out_shape=(jax.ShapeDtypeStruct((s, d), F32), jax.ShapeDtypeStruct((1, d), F32)),
        in_specs=_dz_specs(tm, e) + [pl.BlockSpec((None, d, e), lambda i, k: (k, 0, 0)), ANY, vec, ANY],
        out_specs=(ANY, vec),
        scratch_shapes=[pltpu.VMEM((tm, d), F32)] * 3 + [pltpu.SemaphoreType.DMA((3,))],
        compiler_params=pltpu.CompilerParams(dimension_semantics=("arbitrary", "arbitrary"),
                                             vmem_limit_bytes=DHDX_VMEM_LIMIT))(dzh, dza, w_full, x, gain, dx2)


def _position():
    x, y, c = lax.axis_index("x"), lax.axis_index("y"), lax.axis_index("c")
    return x, y, c


def _xor_peer(x, y, c, mask):
    return (x ^ ((mask >> 2) & 1), y ^ ((mask >> 1) & 1), c ^ (mask & 1))


def _block_order(masks):
    me = 4 * lax.axis_index("x") + 2 * lax.axis_index("y") + lax.axis_index("c")
    return jnp.stack([me ^ m for m in masks]).astype(jnp.int32)


GATHER_MASKS = (0, 1, 4, 5, 2, 3, 6, 7)


def _inproj_gather(h, w_loc, wo_loc):
    s, d = h.shape
    e = w_loc.shape[1]
    tm = _tile(s, 1024)
    ni = s // tm
    pre = max(ni - 2, 0)

    def body(order_ref, h_ref, w_ref, wo_ref, z_ref, wf_ref, wof_ref, wbuf, send_sems, recv_sems, osend, orecv,
             local_sems, wsems):
        j, i = pl.program_id(0), pl.program_id(1)
        x, y, c = _position()
        me, sibling = (x, y, c), (x, y, 1 - c)
        chips = [(1 - x, y), (x, 1 - y), (1 - x, 1 - y)]
        blk = lambda p: 4 * p[0] + 2 * p[1] + p[2]

        def copy(k, block, to, src=None):
            dst = wf_ref.at[blk(block)]
            return pltpu.make_async_remote_copy(
                src_ref=dst if src is None else src, dst_ref=dst, send_sem=send_sems.at[k], recv_sem=recv_sems.at[k],
                device_id=to, device_id_type=MESH)

        first = [copy(0, me, sibling, src=w_ref)] + [copy(1 + q, me, (*chip, c), src=w_ref) for q, chip in enumerate(chips)]
        passed = [copy(4 + q, (*chip, c), sibling) for q, chip in enumerate(chips)]
        mine = pltpu.make_async_copy(w_ref, wf_ref.at[blk(me)], local_sems.at[0])
        ocopies = [pltpu.make_async_remote_copy(
            src_ref=wo_ref, dst_ref=wof_ref.at[blk(me)], send_sem=osend.at[k], recv_sem=orecv.at[k],
            device_id=_xor_peer(x, y, c, k + 1), device_id_type=MESH) for k in range(N_DEV - 1)]
        omine = pltpu.make_async_copy(wo_ref, wof_ref.at[blk(me)], local_sems.at[1])
        blocks = [me, sibling] + [(*chip, c) for chip in chips] + [(*chip, 1 - c) for chip in chips]
        arrive = [None, copy(0, sibling, me)] + [copy(1 + q, (*chip, c), me) for q, chip in enumerate(chips)] \
            + [copy(4 + q, (*chip, 1 - c), me) for q, chip in enumerate(chips)]
        forward = [None, None] + passed + [None, None, None]
        use_order = (0, 1, 2, 5, 3, 6, 4, 7)
        blocks, arrive, forward = ([lst[n] for n in use_order] for lst in (blocks, arrive, forward))

        def load(slot, src):
            return pltpu.make_async_copy(src, wbuf.at[slot], wsems.at[slot])

        @pl.when((j == 0) & (i == 0))
        def _():
            for cp in [mine, omine] + first + ocopies:
                cp.start()
            load(0, w_ref).start()

        for jj in range(N_DEV):
            @pl.when((j == jj) & (i == 0))
            def _():
                load(jj % 2, w_ref).wait()

            if jj + 1 < N_DEV:
                @pl.when((j == jj) & (i == pre))
                def _():
                    arrive[jj + 1].wait_recv()
                    if forward[jj + 1] is not None:
                        forward[jj + 1].start()
                    load((jj + 1) % 2, wf_ref.at[blk(blocks[jj + 1])]).start()

        z_ref[...] = _dot(h_ref[...], wbuf[j % 2])

        @pl.when((j == N_DEV - 1) & (i == ni - 1))
        def _():
            for cp in first + passed:
                cp.wait_send()
            for cp in ocopies:
                cp.wait_send()
                cp.wait_recv()
            mine.wait()
            omine.wait()

    grid_spec = pltpu.PrefetchScalarGridSpec(
        num_scalar_prefetch=1, grid=(N_DEV, ni),
        in_specs=[pl.BlockSpec((tm, d), lambda j, i, o: (i, 0)), ANY, ANY],
        out_specs=(pl.BlockSpec((tm, e), lambda j, i, o: (i, o[j])), ANY, ANY),
        scratch_shapes=[pltpu.VMEM((2, d, e), BF16), pltpu.SemaphoreType.DMA((7,)), pltpu.SemaphoreType.DMA((7,)),
                        pltpu.SemaphoreType.DMA((7,)), pltpu.SemaphoreType.DMA((7,)), pltpu.SemaphoreType.DMA((2,)),
                        pltpu.SemaphoreType.DMA((2,))])
    return pl.pallas_call(
        body, name="inproj_gather", grid_spec=grid_spec,
        out_shape=(jax.ShapeDtypeStruct((s, N_SPLITS * e), F32), jax.ShapeDtypeStruct((N_DEV, d, e), BF16),
                   jax.ShapeDtypeStruct((N_DEV,) + wo_loc.shape, BF16)),
        compiler_params=_params("arbitrary", "arbitrary"))(_block_order(GATHER_MASKS), h, w_loc, wo_loc)


SCATTER_MASKS = (7, 6, 5, 4, 3, 2, 1, 0)
N_CHIPS = 4


def _scatter_block(k, acc, stage, tmp, own_ref, ra_ref, rb_ref, sa_send, sa_recv, sb_send, sb_recv, loc_sem, step, ns):
    x, y, c = _position()
    chip_of = lambda t: _xor_peer(x, y, c, SCATTER_MASKS[2 * t + 1])
    last = step == ns - 1
    fetch_at = min(1, ns - 1)

    def ship(t):
        return pltpu.make_async_remote_copy(
            src_ref=stage.at[0], dst_ref=ra_ref.at[t], send_sem=sa_send.at[t], recv_sem=sa_recv.at[t],
            device_id=(x, y, 1 - c), device_id_type=MESH)

    def send(t):
        return pltpu.make_async_remote_copy(
            src_ref=stage.at[1], dst_ref=rb_ref.at[t], send_sem=sb_send.at[t], recv_sem=sb_recv.at[t],
            device_id=chip_of(t), device_id_type=MESH)

    for kk in range(N_DEV):
        t = kk // 2
        fetch = pltpu.make_async_copy(ra_ref.at[t], tmp, loc_sem)

        if kk % 2 == 1:
            @pl.when((step == fetch_at) & (k == kk))
            def _():
                ship(t).wait_recv()
                fetch.start()

        @pl.when(last & (k == kk))
        def _():
            if kk % 2 == 0:
                if t >= 1:
                    ship(t - 1).wait_send()
                stage[0] = acc[...].astype(BF16)
                ship(t).start()
            else:
                fetch.wait()
                acc[...] += tmp[...].astype(F32)
                if t < N_CHIPS - 1:
                    if t >= 1:
                        send(t - 1).wait_send()
                    stage[1] = acc[...].astype(BF16)
                    send(t).start()
                else:
                    keep = pltpu.make_async_copy(acc, own_ref, loc_sem)
                    keep.start()
                    keep.wait()
                    ship(t).wait_send()
                    send(t - 1).wait_send()
                    for q in range(N_CHIPS - 1):
                        send(q).wait_recv()


def _scatter_scratch(rows, cols):
    return [pltpu.VMEM((rows, cols), F32), pltpu.VMEM((2, rows, cols), BF16), pltpu.VMEM((rows, cols), BF16),
            pltpu.SemaphoreType.DMA((N_CHIPS,)), pltpu.SemaphoreType.DMA((N_CHIPS,)),
            pltpu.SemaphoreType.DMA((N_CHIPS - 1,)), pltpu.SemaphoreType.DMA((N_CHIPS - 1,)), pltpu.SemaphoreType.DMA(())]


def _scatter_out(rows, cols):
    return (jax.ShapeDtypeStruct((rows, cols), F32), jax.ShapeDtypeStruct((N_CHIPS, rows, cols), BF16),
            jax.ShapeDtypeStruct((N_CHIPS - 1, rows, cols), BF16))


def _dwin_scatter(h, dzh, dza):
    s, d = h.shape
    e = dzh.shape[2]
    ts = _tile(s, 1024)
    ns = s // ts

    def body(order_ref, dzh_ref, dza_ref, h_ref, own_ref, ra_ref, rb_ref, acc, stage, tmp, *sems):
        k, step = pl.program_id(0), pl.program_id(1)

        @pl.when(step == 0)
        def _():
            acc[...] = jnp.zeros_like(acc)

        def add(dz):
            acc[...] += _dot_tn(h_ref[...], dz)

        _dz_pick(order_ref[k], dzh_ref, dza_ref, add)
        _scatter_block(k, acc, stage, tmp, own_ref, ra_ref, rb_ref, *sems, step, ns)

    def dz_spec(lo):
        return pl.BlockSpec((None, ts, e), lambda k, st, o: (jnp.clip(o[k] - lo, 0, 3), st, 0))

    grid_spec = pltpu.PrefetchScalarGridSpec(
        num_scalar_prefetch=1, grid=(N_DEV, ns),
        in_specs=[dz_spec(0), dz_spec(4), pl.BlockSpec((ts, d), lambda k, st, o: (st, 0))],
        out_specs=(ANY, ANY, ANY), scratch_shapes=_scatter_scratch(d, e))
    own, _, rb = pl.pallas_call(
        body, name="dwin_scatter", grid_spec=grid_spec, out_shape=_scatter_out(d, e),
        compiler_params=_params("arbitrary", "arbitrary"))(_block_order(SCATTER_MASKS), dzh, dza, h)
    return own, rb


def _dwout_scatter(y_h, y_a, dxb):
    s, e = y_h.shape
    d = dxb.shape[1]
    r = 2 * e // N_DEV
    pairs = e // (2 * r)
    ts = _tile(s, 1024)
    ns = s // ts
    chip_masks = SCATTER_MASKS[1::2]
    passes = ((0, 1), (2,), (3,))
    slots = max(len(chips) for chips in passes)
    slot_chip = [chips[min(u, len(chips) - 1)] for chips in passes for u in range(slots)]

    def body(pair_ref, yh0_ref, ya0_ref, yh1_ref, ya1_ref, dx_ref, own_ref, ra_ref, rb_ref, acc, keep_buf, ship_buf,
             send_buf, tmp, sa_send, sa_recv, sb_send, sb_recv, loc_sem):
        p, step = pl.program_id(0), pl.program_id(1)
        x, y, c = _position()

        @pl.when(step == 0)
        def _():
            acc[...] = jnp.zeros_like(acc)

        for u, (yh_ref, ya_ref) in enumerate(((yh0_ref, ya0_ref), (yh1_ref, ya1_ref))):
            rows = slice(u * 2 * r, (u + 1) * 2 * r)
            used = functools.reduce(jnp.logical_or, [p == pp for pp, chips in enumerate(passes) if u < len(chips)])

            @pl.when(used & (pair_ref[slots * p + u] < pairs))
            def _():
                acc[rows, :] += _dot_tn(yh_ref[...], dx_ref[...])

            @pl.when(used & (pair_ref[slots * p + u] >= pairs))
            def _():
                acc[rows, :] += _dot_tn(ya_ref[...], dx_ref[...])

        def block_rows(u, core):
            return pl.ds(pl.multiple_of(u * 2 * r + core * r, r), r)

        slot_of = {q: u for chips in passes for u, q in enumerate(chips)}

        def ship(q):
            return pltpu.make_async_remote_copy(
                src_ref=ship_buf.at[slot_of[q]], dst_ref=ra_ref.at[q], send_sem=sa_send.at[q], recv_sem=sa_recv.at[q],
                device_id=(x, y, 1 - c), device_id_type=MESH)

        def send(q):
            return pltpu.make_async_remote_copy(
                src_ref=send_buf.at[slot_of[q]], dst_ref=rb_ref.at[q], send_sem=sb_send.at[q], recv_sem=sb_recv.at[q],
                device_id=_xor_peer(x, y, c, chip_masks[q]), device_id_type=MESH)

        def sibling_share(q):
            ship(q).wait_recv()
            fetch = pltpu.make_async_copy(ra_ref.at[q], tmp, loc_sem)
            fetch.start()
            fetch.wait()
            return tmp[...].astype(F32)

        shipped, sent = {}, {}
        for pp, chips in enumerate(passes):
            @pl.when((step == ns - 1) & (p == pp))
            def _():
                for u, q in enumerate(chips):
                    if u in shipped:
                        ship(shipped.pop(u)).wait_send()
                    ship_buf[u] = acc[block_rows(u, 1 - c), :].astype(BF16)
                    ship(q).start()
                    shipped[u] = q
                for u, q in enumerate(chips):
                    total = acc[block_rows(u, c), :] + sibling_share(q)
                    if q < N_CHIPS - 1:
                        if u in sent:
                            send(sent.pop(u)).wait_send()
                        send_buf[u] = total.astype(BF16)
                        send(q).start()
                        sent[u] = q
                    else:
                        keep_buf[...] = total
                        keep = pltpu.make_async_copy(keep_buf, own_ref, loc_sem)
                        keep.start()
                        keep.wait()
                if pp == len(passes) - 1:
                    for q in shipped.values():
                        ship(q).wait_send()
                    for q in sent.values():
                        send(q).wait_send()
                    for q in range(N_CHIPS - 1):
                        send(q).wait_recv()

    def y_spec(u, lo):
        return pl.BlockSpec((ts, 2 * r), lambda p, st, o: (st, jnp.clip(o[slots * p + u] - lo, 0, pairs - 1)))

    pair_of_chip = _block_order(chip_masks) // 2
    grid_spec = pltpu.PrefetchScalarGridSpec(
        num_scalar_prefetch=1, grid=(len(passes), ns),
        in_specs=[y_spec(0, 0), y_spec(0, pairs), y_spec(1, 0), y_spec(1, pairs),
                  pl.BlockSpec((ts, d), lambda p, st, o: (st, 0))],
        out_specs=(ANY, ANY, ANY),
        scratch_shapes=[pltpu.VMEM((slots * 2 * r, d), F32), pltpu.VMEM((r, d), F32),
                        pltpu.VMEM((slots, r, d), BF16)] + _scatter_scratch(r, d)[1:])
    own, _, rb = pl.pallas_call(
        body, name="dwout_scatter", grid_spec=grid_spec, out_shape=_scatter_out(r, d),
        compiler_params=_params("arbitrary", "arbitrary"))(
            jnp.stack([pair_of_chip[q] for q in slot_chip]), y_h, y_a, y_h, y_a, dxb)
    return own, rb


def _sum_chips_adamw(own, recv, w, m, v):
    r, c = w.shape
    tr = _tile(r, 256)

    def body(own_ref, rc_ref, w_ref, m_ref, v_ref, g_ref, d_ref, mo_ref, vo_ref):
        g = own_ref[...]
        for q in range(N_CHIPS - 1):
            g = g + rc_ref[q].astype(F32)
        g_ref[...] = g
        d_ref[...], mo_ref[...], vo_ref[...] = _adamw(w_ref[...], g, m_ref[...], v_ref[...])

    blk = pl.BlockSpec((tr, c), lambda i: (i, 0))
    shp = jax.ShapeDtypeStruct((r, c), F32)
    return pl.pallas_call(
        body, name="sum_chips_adamw", grid=(r // tr,), out_shape=(shp, shp, shp, shp),
        in_specs=[blk, pl.BlockSpec((N_CHIPS - 1, tr, c), lambda i: (0, i, 0)), blk, blk, blk],
        out_specs=(blk, blk, blk, blk), compiler_params=_params("parallel"))(own, recv, w, m, v)


SMALL_ROWS = 8
ROW_LB = 4
ROW_GN = 6
ROW_LOSS = 7


def _small_allreduce_adamw(part, w, m, v, lb_logits):
    width = part.shape[1]

    def body(p_ref, w_ref, m_ref, v_ref, lb_ref, g_ref, d_ref, mo_ref, vo_ref, buf, send_sems, recv_sems):
        x, y, c = _position()
        me = 4 * x + 2 * y + c
        buf[me] = p_ref[...]
        copies = []
        for k in range(N_DEV - 1):
            bx, by, bc = ((k + 1) >> 2) & 1, ((k + 1) >> 1) & 1, (k + 1) & 1
            peer = (x ^ bx, y ^ by, c ^ bc)
            copies.append(pltpu.make_async_remote_copy(
                src_ref=p_ref, dst_ref=buf.at[me], send_sem=send_sems.at[k], recv_sem=recv_sems.at[k],
                device_id=peer, device_id_type=MESH))
        for cp in copies:
            cp.start()
        for cp in copies:
            cp.wait_recv()
        for cp in copies:
            cp.wait_send()
        tot = buf[0]
        for dev in range(1, N_DEV):
            tot = tot + buf[dev]
        lbv = lb_ref[...]
        lb = _sigmoid(lbv[0:1] - lbv[1:2])
        glb = tot[ROW_LB:ROW_LB + 1] * lb * (1.0 - lb)
        loss = jnp.sum(tot[ROW_LOSS:ROW_LOSS + 1], axis=-1, keepdims=True)
        row = lax.broadcasted_iota(jnp.int32, (SMALL_ROWS, width), 0)
        g = jnp.where(row == ROW_LB, glb, jnp.where(row == ROW_LB + 1, -glb, tot))
        g = jnp.where(row == ROW_LOSS, loss, g)
        g_ref[...] = g
        d_ref[...], mo_ref[...], vo_ref[...] = _adamw(w_ref[...], g, m_ref[...], v_ref[...])

    vm = pl.BlockSpec(memory_space=pltpu.VMEM)
    shp = jax.ShapeDtypeStruct((SMALL_ROWS, width), F32)
    return pl.pallas_call(
        body, name="small_allreduce_adamw", out_shape=(shp, shp, shp, shp),
        in_specs=[vm] * 5, out_specs=(vm, vm, vm, vm),
        scratch_shapes=[pltpu.VMEM((N_DEV, SMALL_ROWS, width), F32), pltpu.SemaphoreType.DMA((N_DEV - 1,)),
                        pltpu.SemaphoreType.DMA((N_DEV - 1,))],
    )(part, w, m, v, lb_logits)


def _pack_small(norm_gain, final_gain, lb2, gnorm, last_row, width):
    pad = lambda a: jnp.pad(a.reshape(1, -1), ((0, 0), (0, width - a.size)))
    return jnp.concatenate([norm_gain.reshape(2, width), final_gain.reshape(2, width), lb2.reshape(2, width),
                            pad(gnorm), last_row.reshape(1, width)], axis=0)


def _unpack_small(p, d, e, hd):
    return (p[0:2].reshape(1, d), p[2:4].reshape(d), p[4:6].reshape(2, e), p[6:7, :hd].reshape(1, hd))


def kernel(x, norm_gain, w_in, lb_logits, hgrn_gnorm, w_out, final_gain, loss_target, m_norm_gain, m_w_in, m_lb_logits, m_hgrn_gnorm, m_w_out, m_final_gain, v_norm_gain, v_w_in, v_lb_logits, v_hgrn_gnorm, v_w_out, v_final_gain):
    s, d = x.shape[1], x.shape[2]
    e = w_in.shape[2]
    assert d == 2 * e and lb_logits.shape == (2, e) and w_out.shape[1] * N_DEV == 2 * e
    x2d = x.reshape(s, d)
    tgt = loss_target.reshape(s, d)

    h = _rmsnorm_fwd(x2d, norm_gain)
    z, w_in_full, w_out_full = _inproj_gather(h, _cast_bf16(w_in[0]), _cast_bf16(w_out[0]))
    w_out_full = w_out_full.reshape(2 * e, d)
    y_h, states = _hgrn_fwd(z, lb_logits, hgrn_gnorm)
    o_attn, lse, y_a = _attn_fwd(z)
    dx2, dx2b, dy, loss_vec, dfg = _outproj_loss(x2d, y_h, y_a, w_out_full, final_gain.reshape(1, d), tgt)

    own_o, recv_o = _dwout_scatter(y_h, y_a, dx2b)
    dza = _attn_bwd(z, dy, o_attn, lse)
    dzh, dlb, dgn = _hgrn_bwd(z, dy, states, lb_logits, hgrn_gnorm)
    grad_x, dng = _dh_dx(dzh, dza, w_in_full, x2d, norm_gain, dx2)
    g_wo, d_wo, nm_wo, nv_wo = _sum_chips_adamw(own_o, recv_o, w_out[0], m_w_out[0], v_w_out[0])

    width = d // 2
    zero_row = jnp.zeros((1, width), F32)
    loss_row = loss_vec[:, :width] + loss_vec[:, width:]
    part = _pack_small(dng, dfg, jnp.concatenate([dlb, zero_row], axis=0), dgn, loss_row, width)
    pw = _pack_small(norm_gain, final_gain, lb_logits, hgrn_gnorm, zero_row, width)
    pm = _pack_small(m_norm_gain, m_final_gain, m_lb_logits, m_hgrn_gnorm, zero_row, width)
    pv = _pack_small(v_norm_gain, v_final_gain, v_lb_logits, v_hgrn_gnorm, zero_row, width)
    sg, sd, sm, sv = _small_allreduce_adamw(part, pw, pm, pv, lb_logits)
    own_i, recv_i = _dwin_scatter(h, dzh, dza)
    g_wi, d_wi, nm_wi, nv_wi = _sum_chips_adamw(own_i, recv_i, w_in[0], m_w_in[0], v_w_in[0])
    hd = hgrn_gnorm.shape[1]
    g_ng, g_fg, g_lb, g_gn = _unpack_small(sg, d, e, hd)
    d_ng, d_fg, d_lb, d_gn = _unpack_small(sd, d, e, hd)
    m_ng, m_fg, m_lb, m_gn = _unpack_small(sm, d, e, hd)
    v_ng, v_fg, v_lb, v_gn = _unpack_small(sv, d, e, hd)
    loss = sg[ROW_LOSS, 0]

    one = lambda a: a[None]
    return (loss, grad_x.reshape(1, s, d), g_ng, one(g_wi), g_lb, g_gn, one(g_wo), g_fg,
            d_ng, one(d_wi), d_lb, d_gn, one(d_wo), d_fg,
            m_ng, one(nm_wi), m_lb, m_gn, one(nm_wo), m_fg,
            v_ng, one(nv_wi), v_lb, v_gn, one(nv_wo), v_fg)
```
